```python
import math
import jax
import jax.numpy as jnp
from jax import lax
import numpy as np

D_MODEL = 1024
BATCH = 8
SEQ = 4096
DEPTH = 2

N_MIXERS = 2
EPS = 1e-6
NEG_INF = -1e30

MIX_WIDTH = D_MODEL
TOK_WIDTH = 3 * MIX_WIDTH // 4
MEM_WIDTH = MIX_WIDTH // 4

DILATED_GROUPS = ((128, 1), (512, 4), (2048, 16))
ATT_HEAD_DIM = 64
ATT_HEADS = TOK_WIDTH // ATT_HEAD_DIM
HEADS_PER_GROUP = ATT_HEADS // len(DILATED_GROUPS)
BAND_BLOCK = 64
REL_BUCKETS = 32
REL_MAX_DIST = 1024

DN_HEAD_DIM = 128
DN_HEADS = TOK_WIDTH // DN_HEAD_DIM
DN_CONV = 5
DN_CHUNK = 64

MEM_LEN = 256
MEM_HEADS = 4
MEM_HEAD_DIM = MEM_WIDTH // MEM_HEADS

_FF_RAW = -(-8 * D_MODEL // 3)
D_FF = -(-_FF_RAW // 256) * 256

ATT_IN = 3 * TOK_WIDTH + MEM_WIDTH
DN_IN = 4 * TOK_WIDTH + 4 * DN_HEADS + MEM_WIDTH

kernel_name = "hybrid_dilated_attn_gated_deltanet_encoder"


def _rms_norm(x, gain):
    xf = x.astype(jnp.float32)
    y = xf * lax.rsqrt(jnp.mean(xf * xf, axis=-1, keepdims=True) + EPS) * gain.astype(jnp.float32)
    return y.astype(x.dtype)


def _l2norm(t):
    return t * lax.rsqrt(jnp.sum(t * t, axis=-1, keepdims=True) + EPS)


def _t5_bucket(rel):
    half = REL_BUCKETS // 2
    max_exact = half // 2
    n = np.abs(rel)
    large = max_exact + (np.log(np.maximum(n, 1) / max_exact) / math.log(REL_MAX_DIST / max_exact)
                         * (half - max_exact)).astype(np.int64)
    large = np.minimum(large, half - 1)
    return ((rel > 0) * half + np.where(n < max_exact, n, large)).astype(np.int32)


def _dilated_band_attention(q, k, v, bias, dil, half):
    b, s, h, dh = q.shape
    L = s // dil
    nb = -(-L // BAND_BLOCK)
    lp = nb * BAND_BLOCK
    n = b * dil

    def to_sub(t):
        return t.reshape(b, L, dil, h, dh).transpose(0, 2, 1, 3, 4).reshape(n, L, h, dh)

    def key_windows(t):
        t = jnp.pad(to_sub(t), ((0, 0), (BAND_BLOCK, lp - L + BAND_BLOCK), (0, 0), (0, 0)))
        t = t.reshape(n, nb + 2, BAND_BLOCK, h, dh)
        return jnp.concatenate([t[:, :-2], t[:, 1:-1], t[:, 2:]], axis=2)

    qs = jnp.pad(to_sub(q), ((0, 0), (0, lp - L), (0, 0), (0, 0))).reshape(n, nb, BAND_BLOCK, h, dh)
    kw = key_windows(k)
    vw = key_windows(v)

    rel = np.arange(3 * BAND_BLOCK)[None, :] - BAND_BLOCK - np.arange(BAND_BLOCK)[:, None]
    in_band = np.abs(rel) <= half
    kpos = (np.arange(nb)[:, None] - 1) * BAND_BLOCK + np.arange(3 * BAND_BLOCK)[None, :]
    valid = in_band[None] & ((kpos >= 0) & (kpos < L))[:, None, :]
    bias_band = jnp.transpose(bias[np.clip(rel + half, 0, 2 * half)], (2, 0, 1)).astype(jnp.float32)

    logits = jnp.einsum('nbqhd,nbkhd->nbhqk', qs, kw, preferred_element_type=jnp.float32)
    logits = jnp.where(valid[None, :, None], logits + bias_band[None, None], NEG_INF)
    m = jnp.max(logits, axis=-1)
    p = jnp.exp(logits - m[..., None])
    den = jnp.sum(p, axis=-1)
    o = jnp.einsum('nbhqk,nbkhd->nbqhd', p.astype(v.dtype), vw, preferred_element_type=jnp.float32)
    o = o / jnp.swapaxes(den, -1, -2)[..., None]

    def from_sub(t):
        rest = t.shape[3:]
        t = t.reshape((b, dil, lp) + rest)[:, :, :L]
        return jnp.swapaxes(t, 1, 2).reshape((b, s) + rest)

    return from_sub(o), from_sub(jnp.swapaxes(m, -1, -2)), from_sub(jnp.swapaxes(den, -1, -2))


def _dilated_mixture(q, k, v, rel_bias):
    b, s = q.shape[:2]
    outs, lses = [], []
    for gi, (window, dil) in enumerate(DILATED_GROUPS):
        half = window // (2 * dil)
        heads = slice(gi * HEADS_PER_GROUP, (gi + 1) * HEADS_PER_GROUP)
        bias = rel_bias[_t5_bucket(np.arange(-half, half + 1) * dil)][:, heads]
        o, m, den = _dilated_band_attention(q[:, :, heads], k[:, :, heads], v[:, :, heads], bias, dil, half)
        outs.append(o)
        lses.append(m + jnp.log(den))
    wts = jax.nn.softmax(jnp.stack(lses), axis=0)
    mixed = jnp.concatenate([o * w[..., None] for o, w in zip(outs, wts)], axis=2)
    return mixed.reshape(b, s, TOK_WIDTH)


def _memory_attention(q_mem, mem_n, w_kv):
    b, s, _ = q_mem.shape
    q = q_mem.reshape(b, s, MEM_HEADS, MEM_HEAD_DIM) * (MEM_HEAD_DIM ** -0.5)
    k, v = jnp.split(mem_n @ w_kv, 2, axis=-1)
    k = k.reshape(b, -1, MEM_HEADS, MEM_HEAD_DIM)
    v = v.reshape(b, -1, MEM_HEADS, MEM_HEAD_DIM)
    logits = jnp.einsum('bshd,bmhd->bhsm', q, k, preferred_element_type=jnp.float32)
    p = jax.nn.softmax(logits, axis=-1)
    o = jnp.einsum('bhsm,bmhd->bshd', p.astype(v.dtype), v)
    return o.reshape(b, s, MEM_WIDTH)


def _gated_delta_chunked(q, k, v, g, beta):
    b, s, h, dk = q.shape
    dv = v.shape[-1]
    nc = s // DN_CHUNK

    def chunks(t):
        t = t.reshape((b, nc, DN_CHUNK, h) + t.shape[3:])
        return jnp.moveaxis(t, 3, 1)

    q, k, v, g, beta = (chunks(t) for t in (q, k, v, g, beta))
    g = jnp.cumsum(g, axis=-1)
    tril = np.tril(np.ones((DN_CHUNK, DN_CHUNK), dtype=bool))
    strict = np.tril(np.ones((DN_CHUNK, DN_CHUNK), dtype=bool), -1)
    decay = jnp.exp(jnp.where(tril, g[..., :, None] - g[..., None, :], NEG_INF))
    k_beta = k * beta[..., None]
    lmat = jnp.where(strict, jnp.einsum('bhnik,bhnjk->bhnij', k_beta, k) * decay, 0.0)
    a = lmat + jnp.eye(DN_CHUNK, dtype=jnp.float32)
    rhs = jnp.concatenate([v * beta[..., None], k_beta * jnp.exp(g)[..., None]], axis=-1)
    sol = lax.linalg.triangular_solve(a, rhs, left_side=True, lower=True, unit_diagonal=True)
    u, w = sol[..., :dv], sol[..., dv:]
    intra = jnp.where(tril, jnp.einsum('bhnik,bhnjk->bhnij', q, k) * decay, 0.0)

    def step(state, inp):
        qc, kc, uc, wc, gc, ac = inp
        v_new = uc - jnp.einsum('bhck,bhkv->bhcv', wc, state)
        out = (jnp.einsum('bhck,bhkv->bhcv', qc * jnp.exp(gc)[..., None], state)
               + jnp.einsum('bhij,bhjv->bhiv', ac, v_new))
        g_last = gc[..., -1]
        state = (state * jnp.exp(g_last)[..., None, None]
                 + jnp.einsum('bhck,bhcv->bhkv', kc * jnp.exp(g_last[..., None] - gc)[..., None], v_new))
        return state, out

    xs = tuple(jnp.moveaxis(t, 2, 0) for t in (q, k, u, w, g, intra))
    state0 = jnp.zeros((b, h, dk, dv), jnp.float32)
    _, out = lax.scan(step, state0, xs)
    out = jnp.moveaxis(out, 0, 2)
    return jnp.moveaxis(out, 1, 3).reshape(b, s, h, dv)


def _attention_sublayer(h, mem_n, w_in, w_out, rel_bias, w_mem_kv):
    b, s, _ = h.shape
    q, k, v, q_mem = jnp.split(h @ w_in, [TOK_WIDTH, 2 * TOK_WIDTH, 3 * TOK_WIDTH], axis=-1)
    q = q.reshape(b, s, ATT_HEADS, ATT_HEAD_DIM) * (ATT_HEAD_DIM ** -0.5)
    k = k.reshape(b, s, ATT_HEADS, ATT_HEAD_DIM)
    v = v.reshape(b, s, ATT_HEADS, ATT_HEAD_DIM)
    mixed = _dilated_mixture(q, k, v, rel_bias).astype(h.dtype)
    mem_out = _memory_attention(q_mem, mem_n, w_mem_kv)
    return jnp.concatenate([mixed, mem_out], axis=-1) @ w_out


def _deltanet_sublayer(h, mem_n, w_in, conv_w, a_log, dt_bias, out_norm, w_out, w_mem_kv):
    b, s, _ = h.shape
    qkv, z, gate_in, q_mem = jnp.split(
        h @ w_in, [3 * TOK_WIDTH, 4 * TOK_WIDTH, 4 * TOK_WIDTH + 4 * DN_HEADS], axis=-1)
    qkv = lax.conv_general_dilated(
        qkv, conv_w[:, None, :], window_strides=(1,), padding=[(DN_CONV // 2, DN_CONV // 2)],
        dimension_numbers=('NWC', 'WIO', 'NWC'), feature_group_count=3 * TOK_WIDTH)
    q, k, v = jnp.split(jax.nn.silu(qkv).astype(jnp.float32), 3, axis=-1)
    q = _l2norm(q.reshape(b, s, DN_HEADS, DN_HEAD_DIM)) * (DN_HEAD_DIM ** -0.5)
    k = _l2norm(k.reshape(b, s, DN_HEADS, DN_HEAD_DIM))
    v = v.reshape(b, s, DN_HEADS, DN_HEAD_DIM)
    gate_in = gate_in.astype(jnp.float32).reshape(b, s, 2, 2, DN_HEADS)
    g = -jnp.exp(a_log.astype(jnp.float32)) * jax.nn.softplus(gate_in[:, :, :, 0] + dt_bias.astype(jnp.float32))
    beta = jax.nn.sigmoid(gate_in[:, :, :, 1])
    o_fwd = _gated_delta_chunked(q, k, v, g[:, :, 0], beta[:, :, 0])
    rev = lambda t: jnp.flip(t, axis=1)
    o_bwd = rev(_gated_delta_chunked(rev(q), rev(k), rev(v), rev(g[:, :, 1]), rev(beta[:, :, 1])))
    o = o_fwd + o_bwd
    zf = z.astype(jnp.float32).reshape(b, s, DN_HEADS, DN_HEAD_DIM)
    o = (o * lax.rsqrt(jnp.mean(o * o, axis=-1, keepdims=True) + EPS)
         * out_norm.astype(jnp.float32) * jax.nn.silu(zf))
    o = o.reshape(b, s, TOK_WIDTH).astype(h.dtype)
    mem_out = _memory_attention(q_mem, mem_n, w_mem_kv)
    return jnp.concatenate([o, mem_out], axis=-1) @ w_out


def _swiglu(h, w_gate_up, w_down):
    gate, up = jnp.split(h @ w_gate_up, 2, axis=-1)
    return (jax.nn.silu(gate) * up) @ w_down


def _fwd_setup_inputs(seed: int = 0) -> dict:
    key = jax.random.key(seed)
    ks = jax.random.split(key, 24)
    n_att = (DEPTH + N_MIXERS - 1) // N_MIXERS
    n_dn = DEPTH // N_MIXERS

    def nrm(k, shape, scale):
        return jax.random.normal(k, shape, jnp.float32) * scale

    def gain(k, shape):
        return 1.0 + nrm(k, shape, 0.05)

    dt = jnp.exp(jax.random.uniform(ks[8], (n_dn, 2, DN_HEADS), jnp.float32,
                                    minval=math.log(1e-3), maxval=math.log(1e-1)))
    return {
        "x": nrm(ks[0], (BATCH, SEQ, D_MODEL), 1.0),
        "mem": nrm(ks[1], (BATCH, MEM_LEN, D_MODEL), 1.0),
        "rel_bias": nrm(ks[2], (REL_BUCKETS, ATT_HEADS), 0.5),
        "att_w_in": nrm(ks[3], (n_att, D_MODEL, ATT_IN), D_MODEL ** -0.5),
        "att_w_out": nrm(ks[4], (n_att, MIX_WIDTH, D_MODEL), MIX_WIDTH ** -0.5),
        "dn_w_in": nrm(ks[5], (n_dn, D_MODEL, DN_IN), D_MODEL ** -0.5),
        "dn_conv": nrm(ks[6], (n_dn, DN_CONV, 3 * TOK_WIDTH), DN_CONV ** -0.5),
        "dn_a_log": jnp.log(jax.random.uniform(ks[7], (n_dn, 2, DN_HEADS), jnp.float32, minval=1.0, maxval=16.0)),
        "dn_dt_bias": dt + jnp.log(-jnp.expm1(-dt)),
        "dn_out_norm": gain(ks[9], (n_dn, DN_HEAD_DIM)),
        "dn_w_out": nrm(ks[10], (n_dn, MIX_WIDTH, D_MODEL), MIX_WIDTH ** -0.5),
        "mem_norm": gain(ks[11], (DEPTH, D_MODEL)),
        "mem_w_kv": nrm(ks[12], (DEPTH, D_MODEL, 2 * MEM_WIDTH), D_MODEL ** -0.5),
        "norm_mix_pre": gain(ks[13], (DEPTH, D_MODEL)),
        "norm_mix_post": gain(ks[14], (DEPTH, D_MODEL)),
        "norm_ffn_pre": gain(ks[15], (DEPTH, D_MODEL)),
        "norm_ffn_post": gain(ks[16], (DEPTH, D_MODEL)),
        "ffn_w_gate_up": nrm(ks[17], (DEPTH, D_MODEL, 2 * D_FF), D_MODEL ** -0.5),
        "ffn_w_down": nrm(ks[18], (DEPTH, D_FF, D_MODEL), D_FF ** -0.5),
    }


def _fwd_reference(x, mem, rel_bias, att_w_in, att_w_out, dn_w_in, dn_conv, dn_a_log, dn_dt_bias,
              dn_out_norm, dn_w_out, mem_norm, mem_w_kv, norm_mix_pre, norm_mix_post,
              norm_ffn_pre, norm_ffn_post, ffn_w_gate_up, ffn_w_down):
    for i in range(DEPTH):
        j = i // N_MIXERS
        h = _rms_norm(x, norm_mix_pre[i])
        mem_n = _rms_norm(mem, mem_norm[i])
        if i % N_MIXERS == 0:
            mixed = _attention_sublayer(h, mem_n, att_w_in[j], att_w_out[j], rel_bias, mem_w_kv[i])
        else:
            mixed = _deltanet_sublayer(h, mem_n, dn_w_in[j], dn_conv[j], dn_a_log[j], dn_dt_bias[j],
                                       dn_out_norm[j], dn_w_out[j], mem_w_kv[i])
        x = x + _rms_norm(mixed, norm_mix_post[i])
        h = _rms_norm(x, norm_ffn_pre[i])
        x = x + _rms_norm(_swiglu(h, ffn_w_gate_up[i], ffn_w_down[i]), norm_ffn_post[i])
    return x


import jax as _jax
import jax.numpy as _jnp

TWIN_FORMAT = 'train_step'
FWD_PARAMS = ['x', 'mem', 'rel_bias', 'att_w_in', 'att_w_out', 'dn_w_in', 'dn_conv', 'dn_a_log', 'dn_dt_bias', 'dn_out_norm', 'dn_w_out', 'mem_norm', 'mem_w_kv', 'norm_mix_pre', 'norm_mix_post', 'norm_ffn_pre', 'norm_ffn_post', 'ffn_w_gate_up', 'ffn_w_down']
TWIN_WEIGHTS = ['rel_bias', 'att_w_in', 'att_w_out', 'dn_w_in', 'dn_conv', 'dn_a_log', 'dn_dt_bias', 'dn_out_norm', 'dn_w_out', 'mem_norm', 'mem_w_kv', 'norm_mix_pre', 'norm_mix_post', 'norm_ffn_pre', 'norm_ffn_post', 'ffn_w_gate_up', 'ffn_w_down']
TWIN_DIFF_INPUT = 'x'
TWIN_INPUTS = ['x', 'mem', 'rel_bias', 'att_w_in', 'att_w_out', 'dn_w_in', 'dn_conv', 'dn_a_log', 'dn_dt_bias', 'dn_out_norm', 'dn_w_out', 'mem_norm', 'mem_w_kv', 'norm_mix_pre', 'norm_mix_post', 'norm_ffn_pre', 'norm_ffn_post', 'ffn_w_gate_up', 'ffn_w_down', 'loss_target', 'm_rel_bias', 'm_att_w_in', 'm_att_w_out', 'm_dn_w_in', 'm_dn_conv', 'm_dn_a_log', 'm_dn_dt_bias', 'm_dn_out_norm', 'm_dn_w_out', 'm_mem_norm', 'm_mem_w_kv', 'm_norm_mix_pre', 'm_norm_mix_post', 'm_norm_ffn_pre', 'm_norm_ffn_post', 'm_ffn_w_gate_up', 'm_ffn_w_down', 'v_rel_bias', 'v_att_w_in', 'v_att_w_out', 'v_dn_w_in', 'v_dn_conv', 'v_dn_a_log', 'v_dn_dt_bias', 'v_dn_out_norm', 'v_dn_w_out', 'v_mem_norm', 'v_mem_w_kv', 'v_norm_mix_pre', 'v_norm_mix_post', 'v_norm_ffn_pre', 'v_norm_ffn_post', 'v_ffn_w_gate_up', 'v_ffn_w_down']
TWIN_OUTPUTS = ['loss', 'grad_x', 'grad_rel_bias', 'grad_att_w_in', 'grad_att_w_out', 'grad_dn_w_in', 'grad_dn_conv', 'grad_dn_a_log', 'grad_dn_dt_bias', 'grad_dn_out_norm', 'grad_dn_w_out', 'grad_mem_norm', 'grad_mem_w_kv', 'grad_norm_mix_pre', 'grad_norm_mix_post', 'grad_norm_ffn_pre', 'grad_norm_ffn_post', 'grad_ffn_w_gate_up', 'grad_ffn_w_down', 'delta_rel_bias', 'delta_att_w_in', 'delta_att_w_out', 'delta_dn_w_in', 'delta_dn_conv', 'delta_dn_a_log', 'delta_dn_dt_bias', 'delta_dn_out_norm', 'delta_dn_w_out', 'delta_mem_norm', 'delta_mem_w_kv', 'delta_norm_mix_pre', 'delta_norm_mix_post', 'delta_norm_ffn_pre', 'delta_norm_ffn_post', 'delta_ffn_w_gate_up', 'delta_ffn_w_down', 'new_m_rel_bias', 'new_m_att_w_in', 'new_m_att_w_out', 'new_m_dn_w_in', 'new_m_dn_conv', 'new_m_dn_a_log', 'new_m_dn_dt_bias', 'new_m_dn_out_norm', 'new_m_dn_w_out', 'new_m_mem_norm', 'new_m_mem_w_kv', 'new_m_norm_mix_pre', 'new_m_norm_mix_post', 'new_m_norm_ffn_pre', 'new_m_norm_ffn_post', 'new_m_ffn_w_gate_up', 'new_m_ffn_w_down', 'new_v_rel_bias', 'new_v_att_w_in', 'new_v_att_w_out', 'new_v_dn_w_in', 'new_v_dn_conv', 'new_v_dn_a_log', 'new_v_dn_dt_bias', 'new_v_dn_out_norm', 'new_v_dn_w_out', 'new_v_mem_norm', 'new_v_mem_w_kv', 'new_v_norm_mix_pre', 'new_v_norm_mix_post', 'new_v_norm_ffn_pre', 'new_v_norm_ffn_post', 'new_v_ffn_w_gate_up', 'new_v_ffn_w_down']
TWIN_LEAF_KINDS = {'loss': 'loss', 'grad_x': 'grad_x', 'grad_rel_bias': 'grad_w', 'grad_att_w_in': 'grad_w', 'grad_att_w_out': 'grad_w', 'grad_dn_w_in': 'grad_w', 'grad_dn_conv': 'grad_w', 'grad_dn_a_log': 'grad_w', 'grad_dn_dt_bias': 'grad_w', 'grad_dn_out_norm': 'grad_w', 'grad_dn_w_out': 'grad_w', 'grad_mem_norm': 'grad_w', 'grad_mem_w_kv': 'grad_w', 'grad_norm_mix_pre': 'grad_w', 'grad_norm_mix_post': 'grad_w', 'grad_norm_ffn_pre': 'grad_w', 'grad_norm_ffn_post': 'grad_w', 'grad_ffn_w_gate_up': 'grad_w', 'grad_ffn_w_down': 'grad_w', 'delta_rel_bias': 'delta_w', 'delta_att_w_in': 'delta_w', 'delta_att_w_out': 'delta_w', 'delta_dn_w_in': 'delta_w', 'delta_dn_conv': 'delta_w', 'delta_dn_a_log': 'delta_w', 'delta_dn_dt_bias': 'delta_w', 'delta_dn_out_norm': 'delta_w', 'delta_dn_w_out': 'delta_w', 'delta_mem_norm': 'delta_w', 'delta_mem_w_kv': 'delta_w', 'delta_norm_mix_pre': 'delta_w', 'delta_norm_mix_post': 'delta_w', 'delta_norm_ffn_pre': 'delta_w', 'delta_norm_ffn_post': 'delta_w', 'delta_ffn_w_gate_up': 'delta_w', 'delta_ffn_w_down': 'delta_w', 'new_m_rel_bias': 'new_m', 'new_m_att_w_in': 'new_m', 'new_m_att_w_out': 'new_m', 'new_m_dn_w_in': 'new_m', 'new_m_dn_conv': 'new_m', 'new_m_dn_a_log': 'new_m', 'new_m_dn_dt_bias': 'new_m', 'new_m_dn_out_norm': 'new_m', 'new_m_dn_w_out': 'new_m', 'new_m_mem_norm': 'new_m', 'new_m_mem_w_kv': 'new_m', 'new_m_norm_mix_pre': 'new_m', 'new_m_norm_mix_post': 'new_m', 'new_m_norm_ffn_pre': 'new_m', 'new_m_norm_ffn_post': 'new_m', 'new_m_ffn_w_gate_up': 'new_m', 'new_m_ffn_w_down': 'new_m', 'new_v_rel_bias': 'new_v', 'new_v_att_w_in': 'new_v', 'new_v_att_w_out': 'new_v', 'new_v_dn_w_in': 'new_v', 'new_v_dn_conv': 'new_v', 'new_v_dn_a_log': 'new_v', 'new_v_dn_dt_bias': 'new_v', 'new_v_dn_out_norm': 'new_v', 'new_v_dn_w_out': 'new_v', 'new_v_mem_norm': 'new_v', 'new_v_mem_w_kv': 'new_v', 'new_v_norm_mix_pre': 'new_v', 'new_v_norm_mix_post': 'new_v', 'new_v_norm_ffn_pre': 'new_v', 'new_v_norm_ffn_post': 'new_v', 'new_v_ffn_w_gate_up': 'new_v', 'new_v_ffn_w_down': 'new_v'}


def _forward(args):
    return _fwd_reference(*[args[k] for k in FWD_PARAMS])


def _output_shape():
    out = _jax.eval_shape(lambda: _forward(_fwd_setup_inputs(0)))
    return out.shape, out.dtype

N_MICROBATCH = 1
ADAM_LR = 0.001
ADAM_B1 = 0.9
ADAM_B2 = 0.999
ADAM_EPS = 1e-08
ADAM_WD = 0.01
ADAM_STEP = 10
PER_EXAMPLE_BATCH_AXIS = {'x': 0, 'mem': 0, 'loss_target': 0}
SHARED_INPUTS = []
_WEIGHT_DTYPES = {'rel_bias': _jnp.float32, 'att_w_in': _jnp.float32, 'att_w_out': _jnp.float32, 'dn_w_in': _jnp.float32, 'dn_conv': _jnp.float32, 'dn_a_log': _jnp.float32, 'dn_dt_bias': _jnp.float32, 'dn_out_norm': _jnp.float32, 'dn_w_out': _jnp.float32, 'mem_norm': _jnp.float32, 'mem_w_kv': _jnp.float32, 'norm_mix_pre': _jnp.float32, 'norm_mix_post': _jnp.float32, 'norm_ffn_pre': _jnp.float32, 'norm_ffn_post': _jnp.float32, 'ffn_w_gate_up': _jnp.float32, 'ffn_w_down': _jnp.float32}
MOMENT_SCALE = {'rel_bias': 1.048849e+00, 'att_w_in': 9.226017e-01, 'att_w_out': 2.717748e+00, 'dn_w_in': 8.515512e-01, 'dn_conv': 1.016522e+00, 'dn_a_log': 1.098077e+00, 'dn_dt_bias': 1.089203e+00, 'dn_out_norm': 9.165627e+00, 'dn_w_out': 2.174744e+00, 'mem_norm': 2.277842e+00, 'mem_w_kv': 3.099130e+00, 'norm_mix_pre': 1.330409e+00, 'norm_mix_post': 3.228109e+01, 'norm_ffn_pre': 1.263444e+00, 'norm_ffn_post': 3.207971e+01, 'ffn_w_gate_up': 5.632168e-01, 'ffn_w_down': 1.103474e+00}


def _to_microbatches(a, axis):
    t = _jnp.moveaxis(a, axis, 0)
    t = t.reshape((N_MICROBATCH, t.shape[0] // N_MICROBATCH) + t.shape[1:])
    return _jnp.moveaxis(t, 1, axis + 1)


def setup_inputs(seed: int = 0) -> dict:
    inp = _fwd_setup_inputs(seed)
    key = _jax.random.fold_in(_jax.random.key(seed), 7919)
    shape, _ = _output_shape()
    out = dict(inp)
    out["loss_target"] = _jax.random.normal(_jax.random.fold_in(key, 0), shape, _jnp.float32)
    for i, name in enumerate(TWIN_WEIGHTS):
        w = inp[name].astype(_jnp.float32)
        if MOMENT_SCALE is None:
            s = _jnp.sqrt(_jnp.mean(_jnp.square(w)) + 1e-30)
        else:
            s = MOMENT_SCALE[name]
        km, kv = _jax.random.split(_jax.random.fold_in(key, i + 1))
        out[name] = w
        out["m_" + name] = s * _jax.random.normal(km, w.shape, _jnp.float32)
        out["v_" + name] = (s * s) * _jax.random.uniform(kv, w.shape, _jnp.float32, 0.5, 1.5)
    if N_MICROBATCH > 1:
        for name, axis in PER_EXAMPLE_BATCH_AXIS.items():
            out[name] = _to_microbatches(out[name], axis)
    return {'x': out['x'], 'mem': out['mem'], 'rel_bias': out['rel_bias'], 'att_w_in': out['att_w_in'], 'att_w_out': out['att_w_out'], 'dn_w_in': out['dn_w_in'], 'dn_conv': out['dn_conv'], 'dn_a_log': out['dn_a_log'], 'dn_dt_bias': out['dn_dt_bias'], 'dn_out_norm': out['dn_out_norm'], 'dn_w_out': out['dn_w_out'], 'mem_norm': out['mem_norm'], 'mem_w_kv': out['mem_w_kv'], 'norm_mix_pre': out['norm_mix_pre'], 'norm_mix_post': out['norm_mix_post'], 'norm_ffn_pre': out['norm_ffn_pre'], 'norm_ffn_post': out['norm_ffn_post'], 'ffn_w_gate_up': out['ffn_w_gate_up'], 'ffn_w_down': out['ffn_w_down'], 'loss_target': out['loss_target'], 'm_rel_bias': out['m_rel_bias'], 'm_att_w_in': out['m_att_w_in'], 'm_att_w_out': out['m_att_w_out'], 'm_dn_w_in': out['m_dn_w_in'], 'm_dn_conv': out['m_dn_conv'], 'm_dn_a_log': out['m_dn_a_log'], 'm_dn_dt_bias': out['m_dn_dt_bias'], 'm_dn_out_norm': out['m_dn_out_norm'], 'm_dn_w_out': out['m_dn_w_out'], 'm_mem_norm': out['m_mem_norm'], 'm_mem_w_kv': out['m_mem_w_kv'], 'm_norm_mix_pre': out['m_norm_mix_pre'], 'm_norm_mix_post': out['m_norm_mix_post'], 'm_norm_ffn_pre': out['m_norm_ffn_pre'], 'm_norm_ffn_post': out['m_norm_ffn_post'], 'm_ffn_w_gate_up': out['m_ffn_w_gate_up'], 'm_ffn_w_down': out['m_ffn_w_down'], 'v_rel_bias': out['v_rel_bias'], 'v_att_w_in': out['v_att_w_in'], 'v_att_w_out': out['v_att_w_out'], 'v_dn_w_in': out['v_dn_w_in'], 'v_dn_conv': out['v_dn_conv'], 'v_dn_a_log': out['v_dn_a_log'], 'v_dn_dt_bias': out['v_dn_dt_bias'], 'v_dn_out_norm': out['v_dn_out_norm'], 'v_dn_w_out': out['v_dn_w_out'], 'v_mem_norm': out['v_mem_norm'], 'v_mem_w_kv': out['v_mem_w_kv'], 'v_norm_mix_pre': out['v_norm_mix_pre'], 'v_norm_mix_post': out['v_norm_mix_post'], 'v_norm_ffn_pre': out['v_norm_ffn_pre'], 'v_norm_ffn_post': out['v_norm_ffn_post'], 'v_ffn_w_gate_up': out['v_ffn_w_gate_up'], 'v_ffn_w_down': out['v_ffn_w_down']}


def _loss(weights, diff, rest, loss_target):
    with _jax.named_scope("forward"):
        args = {**rest, TWIN_DIFF_INPUT: diff, **{k: w.astype(_WEIGHT_DTYPES[k]) for k, w in weights.items()}}
        y = _forward(args)
    with _jax.named_scope("loss_head"):
        err = _jnp.square(y.astype(_jnp.float32) - loss_target)
        return 0.5 * _jnp.sum(_jnp.mean(err, axis=-1)) if err.ndim else 0.5 * err


def _adamw(w, g, m, v):
    m = ADAM_B1 * m + (1.0 - ADAM_B1) * g
    v = ADAM_B2 * v + (1.0 - ADAM_B2) * _jnp.square(g)
    m_hat = m / (1.0 - ADAM_B1 ** ADAM_STEP)
    v_hat = v / (1.0 - ADAM_B2 ** ADAM_STEP)
    delta = -ADAM_LR * (m_hat / (_jnp.sqrt(v_hat) + ADAM_EPS) + ADAM_WD * w)
    return delta, m, v


def reference(x, mem, rel_bias, att_w_in, att_w_out, dn_w_in, dn_conv, dn_a_log, dn_dt_bias, dn_out_norm, dn_w_out, mem_norm, mem_w_kv, norm_mix_pre, norm_mix_post, norm_ffn_pre, norm_ffn_post, ffn_w_gate_up, ffn_w_down, loss_target, m_rel_bias, m_att_w_in, m_att_w_out, m_dn_w_in, m_dn_conv, m_dn_a_log, m_dn_dt_bias, m_dn_out_norm, m_dn_w_out, m_mem_norm, m_mem_w_kv, m_norm_mix_pre, m_norm_mix_post, m_norm_ffn_pre, m_norm_ffn_post, m_ffn_w_gate_up, m_ffn_w_down, v_rel_bias, v_att_w_in, v_att_w_out, v_dn_w_in, v_dn_conv, v_dn_a_log, v_dn_dt_bias, v_dn_out_norm, v_dn_w_out, v_mem_norm, v_mem_w_kv, v_norm_mix_pre, v_norm_mix_post, v_norm_ffn_pre, v_norm_ffn_post, v_ffn_w_gate_up, v_ffn_w_down):
    given = dict(x=x, mem=mem, rel_bias=rel_bias, att_w_in=att_w_in, att_w_out=att_w_out, dn_w_in=dn_w_in, dn_conv=dn_conv, dn_a_log=dn_a_log, dn_dt_bias=dn_dt_bias, dn_out_norm=dn_out_norm, dn_w_out=dn_w_out, mem_norm=mem_norm, mem_w_kv=mem_w_kv, norm_mix_pre=norm_mix_pre, norm_mix_post=norm_mix_post, norm_ffn_pre=norm_ffn_pre, norm_ffn_post=norm_ffn_post, ffn_w_gate_up=ffn_w_gate_up, ffn_w_down=ffn_w_down, loss_target=loss_target, m_rel_bias=m_rel_bias, m_att_w_in=m_att_w_in, m_att_w_out=m_att_w_out, m_dn_w_in=m_dn_w_in, m_dn_conv=m_dn_conv, m_dn_a_log=m_dn_a_log, m_dn_dt_bias=m_dn_dt_bias, m_dn_out_norm=m_dn_out_norm, m_dn_w_out=m_dn_w_out, m_mem_norm=m_mem_norm, m_mem_w_kv=m_mem_w_kv, m_norm_mix_pre=m_norm_mix_pre, m_norm_mix_post=m_norm_mix_post, m_norm_ffn_pre=m_norm_ffn_pre, m_norm_ffn_post=m_norm_ffn_post, m_ffn_w_gate_up=m_ffn_w_gate_up, m_ffn_w_down=m_ffn_w_down, v_rel_bias=v_rel_bias, v_att_w_in=v_att_w_in, v_att_w_out=v_att_w_out, v_dn_w_in=v_dn_w_in, v_dn_conv=v_dn_conv, v_dn_a_log=v_dn_a_log, v_dn_dt_bias=v_dn_dt_bias, v_dn_out_norm=v_dn_out_norm, v_dn_w_out=v_dn_w_out, v_mem_norm=v_mem_norm, v_mem_w_kv=v_mem_w_kv, v_norm_mix_pre=v_norm_mix_pre, v_norm_mix_post=v_norm_mix_post, v_norm_ffn_pre=v_norm_ffn_pre, v_norm_ffn_post=v_norm_ffn_post, v_ffn_w_gate_up=v_ffn_w_gate_up, v_ffn_w_down=v_ffn_w_down)
    weights = {n: given[n] for n in TWIN_WEIGHTS}
    shared = {n: given[n] for n in SHARED_INPUTS}
    per_example = {n: given[n] for n in ['x', 'mem']}
    grad_fn = _jax.value_and_grad(_loss, argnums=(0, 1))

    def one_microbatch(ex, loss_target):
        ex = dict(ex)
        diff = ex.pop(TWIN_DIFF_INPUT)
        return grad_fn(weights, diff, {**shared, **ex}, loss_target)

    if N_MICROBATCH == 1:
        loss, (grad_w, grad_x) = one_microbatch(per_example, given["loss_target"])
    else:
        def body(carry, xs):
            loss_sum, grad_sum = carry
            l_k, (gw_k, gx_k) = one_microbatch(xs[0], xs[1])
            with _jax.named_scope("update"):
                return (loss_sum + l_k, _jax.tree.map(_jnp.add, grad_sum, gw_k)), gx_k

        init = (_jnp.zeros((), _jnp.float32), _jax.tree.map(_jnp.zeros_like, weights))
        (loss, grad_w), grad_x = _jax.lax.scan(body, init, (per_example, given["loss_target"]))
    with _jax.named_scope("update"):
        delta_w, new_m, new_v = {}, {}, {}
        for n in TWIN_WEIGHTS:
            delta_w[n], new_m[n], new_v[n] = _adamw(weights[n], grad_w[n], given["m_" + n], given["v_" + n])
    return (loss, grad_x, *[grad_w[n] for n in TWIN_WEIGHTS], *[delta_w[n] for n in TWIN_WEIGHTS],
            *[new_m[n] for n in TWIN_WEIGHTS], *[new_v[n] for n in TWIN_WEIGHTS])
```

```python
import functools
import math

import numpy as np
import jax
import jax.numpy as jnp
from jax import lax
from jax.experimental import pallas as pl
from jax.experimental.pallas import tpu as pltpu

F32 = jnp.float32
BF16 = jnp.bfloat16
HI = lax.Precision.HIGHEST
MESH = pl.DeviceIdType.MESH

N_DEV = 8
D = 1024
EPS = 1e-6
NEG = -1e30
TOK_W = 768
MEM_W = 256
ATT_HD = 64
DIL_GROUPS = ((128, 1), (512, 4), (2048, 16))
BAND_HALF = 64
REL_BUCKETS = 32
REL_MAX_DIST = 1024
DN_HD = 128
DN_HEADS = 6
DN_CONV = 5
DN_CHUNK = 64
MEM_HEADS = 4
D_FF = 2816
ATT_IN = 2560
DN_IN = 3352
DN_IN_PAD = 3456

ADAM_LR, ADAM_B1, ADAM_B2, ADAM_EPS, ADAM_WD, ADAM_STEP = 0.001, 0.9, 0.999, 1e-08, 0.01, 10

PACK_C = 512
BIG_ROWS = 6480
SMALL_ROWS = 48
VMEM_LIMIT = 48 * 1024 * 1024


def _cparams(sem=None):
    kw = dict(vmem_limit_bytes=VMEM_LIMIT)
    if sem is not None:
        kw["dimension_semantics"] = sem
    return pltpu.CompilerParams(**kw)


def _tile(n, cap):
    if n <= cap:
        return n
    best = None
    for t in range(128, cap + 1, 128):
        if n % t == 0:
            best = t
    assert best is not None, (n, cap)
    return best


def _matmul(a, b, mode, out_dtype, name, tm=512, tn=512, tk=1024):
    if mode == "nn":
        (m, kc), (_, n) = a.shape, b.shape
        dims = (((1,), (0,)), ((), ()))
    elif mode == "nt":
        (m, kc), (n, _) = a.shape, b.shape
        dims = (((1,), (1,)), ((), ()))
    else:
        (kc, m), (_, n) = a.shape, b.shape
        dims = (((0,), (0,)), ((), ()))
    tm = m if m <= tm else _tile(m, tm)
    tn = _tile(n, tn)
    tk = _tile(kc, tk)
    nk = kc // tk

    def body(a_ref, b_ref, o_ref, acc_ref):
        k = pl.program_id(2)
        part = lax.dot_general(a_ref[...], b_ref[...], dims, preferred_element_type=F32)

        @pl.when(k == 0)
        def _():
            acc_ref[...] = part

        @pl.when(k > 0)
        def _():
            acc_ref[...] += part

        @pl.when(k == nk - 1)
        def _():
            o_ref[...] = acc_ref[...].astype(o_ref.dtype)

    if mode == "nn":
        a_spec = pl.BlockSpec((tm, tk), lambda i, j, k: (i, k))
        b_spec = pl.BlockSpec((tk, tn), lambda i, j, k: (k, j))
    elif mode == "nt":
        a_spec = pl.BlockSpec((tm, tk), lambda i, j, k: (i, k))
        b_spec = pl.BlockSpec((tn, tk), lambda i, j, k: (j, k))
    else:
        a_spec = pl.BlockSpec((tk, tm), lambda i, j, k: (k, i))
        b_spec = pl.BlockSpec((tk, tn), lambda i, j, k: (k, j))
    return pl.pallas_call(
        body, name=name, grid=(m // tm, n // tn, nk),
        in_specs=[a_spec, b_spec],
        out_specs=pl.BlockSpec((tm, tn), lambda i, j, k: (i, j)),
        out_shape=jax.ShapeDtypeStruct((m, n), out_dtype),
        scratch_shapes=[pltpu.VMEM((tm, tn), F32)],
        compiler_params=_cparams(("parallel", "parallel", "arbitrary")),
    )(a, b)


def _rowwise(fn, rows, params, outs, tb, name):
    t = rows[0].shape[0]
    nr, npar = len(rows), len(params)

    def body(*refs):
        ins = [r[...].astype(F32) for r in refs[:nr + npar]]
        res = fn(*ins)
        for o_ref, r in zip(refs[nr + npar:], res):
            o_ref[...] = r.astype(o_ref.dtype)

    return pl.pallas_call(
        body, name=name, grid=(t // tb,),
        in_specs=[pl.BlockSpec((tb, r.shape[1]), lambda i: (i, 0)) for r in rows]
        + [pl.BlockSpec(p.shape, lambda i: (0, 0)) for p in params],
        out_specs=[pl.BlockSpec((tb, c), lambda i: (i, 0)) for c, _ in outs],
        out_shape=[jax.ShapeDtypeStruct((t, c), dt) for c, dt in outs],
        compiler_params=_cparams(("parallel",)),
    )(*rows, *params)


def _rowwise_bwd(fn, rows, params, cots, row_grad, tb, name):
    t = rows[0].shape[0]
    nr, npar, nc = len(rows), len(params), len(cots)
    want = [i for i, g in enumerate(row_grad) if g is not None]

    def body(*refs):
        ins = [r[...].astype(F32) for r in refs[:nr + npar]]
        cts = tuple(r[...].astype(F32) for r in refs[nr + npar:nr + npar + nc])
        outs = refs[nr + npar + nc:]
        _, vjp = jax.vjp(fn, *ins)
        grads = vjp(cts)
        for o_ref, i in zip(outs[:len(want)], want):
            o_ref[...] = grads[i].astype(o_ref.dtype)
        first = pl.program_id(0) == 0
        for o_ref, g in zip(outs[len(want):], grads[nr:]):
            @pl.when(first)
            def _(o_ref=o_ref, g=g):
                o_ref[...] = g

            @pl.when(jnp.logical_not(first))
            def _(o_ref=o_ref, g=g):
                o_ref[...] += g

    res = pl.pallas_call(
        body, name=name, grid=(t // tb,),
        in_specs=[pl.BlockSpec((tb, r.shape[1]), lambda i: (i, 0)) for r in rows]
        + [pl.BlockSpec(p.shape, lambda i: (0, 0)) for p in params]
        + [pl.BlockSpec((tb, c.shape[1]), lambda i: (i, 0)) for c in cots],
        out_specs=[pl.BlockSpec((tb, rows[i].shape[1]), lambda i_: (i_, 0)) for i in want]
        + [pl.BlockSpec(p.shape, lambda i: (0, 0)) for p in params],
        out_shape=[jax.ShapeDtypeStruct(rows[i].shape, row_grad[i]) for i in want]
        + [jax.ShapeDtypeStruct(p.shape, F32) for p in params],
        compiler_params=_cparams(("arbitrary",)),
    )(*rows, *params, *cots)
    return list(res[:len(want)]), list(res[len(want):])


def _rms(x, g):
    return x * lax.rsqrt(jnp.mean(x * x, axis=-1, keepdims=True) + EPS) * g


def _fn_pre(x, g):
    return (_rms(x, g),)


def _fn_res_pre(x, y, g_post, g_pre):
    x1 = x + _rms(y, g_post)
    return x1, _rms(x1, g_pre)


def _fn_res(x, y, g_post):
    return (x + _rms(y, g_post),)


def _sigmoid(x):
    return 1.0 / (1.0 + jnp.exp(-x))


def _silu(x):
    return x * _sigmoid(x)


def _fn_swiglu(gu):
    return (_silu(gu[:, :D_FF]) * gu[:, D_FF:],)


def _fn_combine(o, lse):
    ls = [lse[:, 256 * g:256 * (g + 1)] for g in range(3)]
    mx = lax.stop_gradient(jnp.maximum(jnp.maximum(ls[0], ls[1]), ls[2]))
    es = [jnp.exp(l - mx) for l in ls]
    inv = 1.0 / (es[0] + es[1] + es[2])
    return (jnp.concatenate([o[:, 256 * g:256 * (g + 1)] * (es[g] * inv) for g in range(3)], axis=1),)


def _fn_outnorm(o_f, o_r, z, gain):
    res = []
    for h in range(DN_HEADS):
        sl = slice(DN_HD * h, DN_HD * (h + 1))
        o = o_f[:, sl] + o_r[:, sl]
        res.append(o * lax.rsqrt(jnp.mean(o * o, axis=-1, keepdims=True) + EPS) * gain * _silu(z[:, sl]))
    return (jnp.concatenate(res, axis=1),)


def _loss_kernel(x, tgt, tb, name):
    t, d = x.shape

    def body(x_ref, t_ref, dx_ref, l_ref, acc_ref):
        i = pl.program_id(0)
        e = x_ref[...] - t_ref[...]
        dx_ref[...] = e * (1.0 / d)
        part = jnp.sum(e * e, axis=0, keepdims=True)

        @pl.when(i == 0)
        def _():
            acc_ref[...] = part

        @pl.when(i > 0)
        def _():
            acc_ref[...] += part

        @pl.when(i == t // tb - 1)
        def _():
            l_ref[...] = jnp.broadcast_to(jnp.sum(acc_ref[...], axis=-1, keepdims=True), (1, 128))

    return pl.pallas_call(
        body, name=name, grid=(t // tb,),
        in_specs=[pl.BlockSpec((tb, d), lambda i: (i, 0))] * 2,
        out_specs=[pl.BlockSpec((tb, d), lambda i: (i, 0)), pl.BlockSpec((1, 128), lambda i: (0, 0))],
        out_shape=[jax.ShapeDtypeStruct((t, d), F32), jax.ShapeDtypeStruct((1, 128), F32)],
        scratch_shapes=[pltpu.VMEM((1, d), F32)],
        compiler_params=_cparams(("arbitrary",)),
    )(x, tgt)


def _band_fn(l_sub, bq, i, q, kw, vw, bm):
    w = bq + 2 * BAND_HALF
    s = lax.dot_general((q * (ATT_HD ** -0.5)).astype(BF16), kw.astype(BF16), (((1,), (1,)), ((), ())),
                        preferred_element_type=F32) + bm
    kpos = i * bq - BAND_HALF + lax.broadcasted_iota(jnp.int32, (bq, w), 1)
    s = jnp.where((kpos >= 0) & (kpos < l_sub), s, NEG)
    m = lax.stop_gradient(jnp.max(s, axis=-1, keepdims=True))
    p = jnp.exp(s - m)
    den = jnp.sum(p, axis=-1, keepdims=True)
    o = jnp.dot(p.astype(BF16), vw.astype(BF16), preferred_element_type=F32) / den
    return o, jnp.broadcast_to(m + jnp.log(den), o.shape)


def _band_specs(dil, l_sub, bq):
    w = bq + 2 * BAND_HALF
    qs = pl.BlockSpec((None, bq, ATT_HD), lambda h, r, i: (h * dil + r, i, 0))
    ks = pl.BlockSpec((None, l_sub + 2 * BAND_HALF, ATT_HD), lambda h, r, i: (h * dil + r, 0, 0))
    bs = pl.BlockSpec((None, bq, w), lambda h, r, i: (h, 0, 0))
    return qs, ks, bs


def _band_fwd(q, k, v, bm, dil, l_sub, bq, name):
    w = bq + 2 * BAND_HALF
    qs, ks, bs = _band_specs(dil, l_sub, bq)

    def body(q_ref, k_ref, v_ref, bm_ref, o_ref, l_ref):
        i = pl.program_id(2)
        st = pl.multiple_of(i * bq, bq)
        o, lse = _band_fn(l_sub, bq, i, q_ref[...], k_ref[pl.ds(st, w), :], v_ref[pl.ds(st, w), :], bm_ref[...])
        o_ref[...] = o
        l_ref[...] = lse

    return pl.pallas_call(
        body, name=name, grid=(4, dil, l_sub // bq),
        in_specs=[qs, ks, ks, bs], out_specs=[qs, qs],
        out_shape=[jax.ShapeDtypeStruct(q.shape, F32)] * 2,
        compiler_params=_cparams(("parallel", "parallel", "arbitrary")),
    )(q, k, v, bm)


def _band_bwd(q, k, v, bm, do, dlse, dil, l_sub, bq, name):
    w = bq + 2 * BAND_HALF
    qs, ks, bs = _band_specs(dil, l_sub, bq)

    def body(q_ref, k_ref, v_ref, bm_ref, do_ref, dl_ref, dq_ref, dk_ref, dv_ref, dbm_ref):
        r, i = pl.program_id(1), pl.program_id(2)
        st = pl.multiple_of(i * bq, bq)
        _, vjp = jax.vjp(functools.partial(_band_fn, l_sub, bq, i),
                         q_ref[...], k_ref[pl.ds(st, w), :], v_ref[pl.ds(st, w), :], bm_ref[...])
        dq, dkw, dvw, dbm = vjp((do_ref[...], dl_ref[...]))
        dq_ref[...] = dq

        @pl.when(i == 0)
        def _():
            dk_ref[...] = jnp.zeros_like(dk_ref)
            dv_ref[...] = jnp.zeros_like(dv_ref)

        dk_ref[pl.ds(st, w), :] += dkw
        dv_ref[pl.ds(st, w), :] += dvw

        @pl.when((i == 0) & (r == 0))
        def _():
            dbm_ref[...] = dbm

        @pl.when((i > 0) | (r > 0))
        def _():
            dbm_ref[...] += dbm

    return pl.pallas_call(
        body, name=name, grid=(4, dil, l_sub // bq),
        in_specs=[qs, ks, ks, bs, qs, qs], out_specs=[qs, ks, ks, bs],
        out_shape=[jax.ShapeDtypeStruct(q.shape, F32), jax.ShapeDtypeStruct(k.shape, F32),
                   jax.ShapeDtypeStruct(k.shape, F32), jax.ShapeDtypeStruct(bm.shape, F32)],
        compiler_params=_cparams(("parallel", "arbitrary", "arbitrary")),
    )(q, k, v, bm, do, dlse)


def _t5_bucket(rel):
    half = REL_BUCKETS // 2
    max_exact = half // 2
    n = np.abs(rel)
    large = max_exact + (np.log(np.maximum(n, 1) / max_exact) / math.log(REL_MAX_DIST / max_exact)
                         * (half - max_exact)).astype(np.int64)
    large = np.minimum(large, half - 1)
    return ((rel > 0) * half + np.where(n < max_exact, n, large)).astype(np.int32)


def _bucket_onehot(dil):
    idx = _t5_bucket(np.arange(-BAND_HALF, BAND_HALF + 1) * dil)
    oh = np.zeros((2 * BAND_HALF + 1, REL_BUCKETS), np.float32)
    oh[np.arange(2 * BAND_HALF + 1), idx] = 1.0
    return oh


def _band_bias(rel_bias, gi, dil, bq):
    w = bq + 2 * BAND_HALF
    nb = 2 * BAND_HALF + 1
    bias = jnp.dot(jnp.asarray(_bucket_onehot(dil)), rel_bias[:, 4 * gi:4 * gi + 4], precision=HI)
    row = jnp.concatenate([bias.T, jnp.full((4, w + 1 - nb), NEG, F32)], axis=1)
    flat = jnp.tile(row, (1, bq))[:, :bq * w]
    return flat.reshape(4, bq, w)


def _relbias_grad(dbms, name):
    nb = 2 * BAND_HALF + 1
    bq = max(d.shape[1] for d in dbms)
    skew = []
    for dbm in dbms:
        bqg, w = dbm.shape[1], dbm.shape[2]
        flat = jnp.pad(dbm.reshape(4, bqg * w), ((0, 0), (0, bqg)))
        skew.append(jnp.pad(flat.reshape(4, bqg, w + 1)[:, :, :nb], ((0, 0), (0, bq - bqg), (0, 256 - nb))))
    sk = jnp.concatenate(skew, axis=0)
    oh = np.zeros((3, 256, 128), np.float32)
    for gi, (_, dil) in enumerate(DIL_GROUPS):
        oh[gi, :2 * BAND_HALF + 1, :REL_BUCKETS] = _bucket_onehot(dil)

    def body(s_ref, oh_ref, o_ref):
        col = jnp.sum(s_ref[...], axis=0, keepdims=True)
        o_ref[...] = jnp.dot(jnp.broadcast_to(col, (8, 256)), oh_ref[...], precision=HI, preferred_element_type=F32)

    out = pl.pallas_call(
        body, name=name, grid=(12,),
        in_specs=[pl.BlockSpec((None, bq, 256), lambda n: (n, 0, 0)),
                  pl.BlockSpec((None, 256, 128), lambda n: (n // 4, 0, 0))],
        out_specs=pl.BlockSpec((None, 8, 128), lambda n: (n, 0, 0)),
        out_shape=jax.ShapeDtypeStruct((12, 8, 128), F32),
        compiler_params=_cparams(("parallel",)),
    )(sk, jnp.asarray(oh))
    return out[:, 0, :REL_BUCKETS].T


def _mem_fn(q, k, v):
    s = lax.dot_general((q * (ATT_HD ** -0.5)).astype(BF16), k.astype(BF16), (((1,), (1,)), ((), ())),
                        preferred_element_type=F32)
    m = lax.stop_gradient(jnp.max(s, axis=-1, keepdims=True))
    p = jnp.exp(s - m)
    p = p / jnp.sum(p, axis=-1, keepdims=True)
    return jnp.dot(p.astype(BF16), v.astype(BF16), preferred_element_type=F32)


def _mem_specs(tb, ml):
    qs = pl.BlockSpec((None, tb, ATT_HD), lambda h, i: (h, i, 0))
    ks = pl.BlockSpec((None, ml, ATT_HD), lambda h, i: (h, 0, 0))
    return qs, ks


def _mem_fwd(q, k, v, tb, name):
    qs, ks = _mem_specs(tb, k.shape[1])

    def body(q_ref, k_ref, v_ref, o_ref):
        o_ref[...] = _mem_fn(q_ref[...], k_ref[...], v_ref[...])

    return pl.pallas_call(
        body, name=name, grid=(MEM_HEADS, q.shape[1] // tb),
        in_specs=[qs, ks, ks], out_specs=qs, out_shape=jax.ShapeDtypeStruct(q.shape, F32),
        compiler_params=_cparams(("parallel", "parallel")),
    )(q, k, v)


def _mem_bwd(q, k, v, do, tb, name):
    qs, ks = _mem_specs(tb, k.shape[1])

    def body(q_ref, k_ref, v_ref, do_ref, dq_ref, dk_ref, dv_ref):
        i = pl.program_id(1)
        _, vjp = jax.vjp(_mem_fn, q_ref[...], k_ref[...], v_ref[...])
        dq, dk, dv = vjp(do_ref[...])
        dq_ref[...] = dq

        @pl.when(i == 0)
        def _():
            dk_ref[...] = dk
            dv_ref[...] = dv

        @pl.when(i > 0)
        def _():
            dk_ref[...] += dk
            dv_ref[...] += dv

    return pl.pallas_call(
        body, name=name, grid=(MEM_HEADS, q.shape[1] // tb),
        in_specs=[qs, ks, ks, qs], out_specs=[qs, ks, ks],
        out_shape=[jax.ShapeDtypeStruct(q.shape, F32), jax.ShapeDtypeStruct(k.shape, F32),
                   jax.ShapeDtypeStruct(k.shape, F32)],
        compiler_params=_cparams(("parallel", "arbitrary")),
    )(q, k, v, do)


CONV_PAD = 8


def _conv_post(kind, acc):
    s = _silu(acc)
    if kind == 2:
        return s
    scale = DN_HD ** -0.5 if kind == 0 else 1.0
    return s * lax.rsqrt(jnp.sum(s * s, axis=-1, keepdims=True) + EPS) * scale


def _conv_acc(xp_ref, w, r0, rt):
    acc = None
    for i in range(DN_CONV):
        term = w[i:i + 1, :] * xp_ref[pl.ds(CONV_PAD + r0 + i - DN_CONV // 2, rt), :]
        acc = term if acc is None else acc + term
    return acc


def _conv_fwd(xp, w8, kind, rt, name):
    t = xp.shape[0] - 2 * CONV_PAD

    def body(xp_ref, w_ref, o_ref):
        w = w_ref[...]
        for r in range(t // rt):
            o_ref[pl.ds(r * rt, rt), :] = _conv_post(kind, _conv_acc(xp_ref, w, r * rt, rt))

    return pl.pallas_call(
        body, name=name, grid=(DN_HEADS,),
        in_specs=[pl.BlockSpec((t + 2 * CONV_PAD, DN_HD), lambda j: (0, 6 * kind + j)),
                  pl.BlockSpec((8, DN_HD), lambda j: (0, 6 * kind + j))],
        out_specs=pl.BlockSpec((t, DN_HD), lambda j: (0, j)),
        out_shape=jax.ShapeDtypeStruct((t, TOK_W), F32),
        compiler_params=_cparams(("parallel",)),
    )(xp, w8)


def _conv_bwd(xp, w8, d_f, d_r, kind, rt, name):
    t = xp.shape[0] - 2 * CONV_PAD

    def body(xp_ref, w_ref, df_ref, dr_ref, dx_ref, dw_ref, dpad_ref):
        w = w_ref[...]
        zero = jnp.zeros((CONV_PAD, DN_HD), F32)
        dpad_ref[pl.ds(0, CONV_PAD), :] = zero
        dpad_ref[pl.ds(CONV_PAD + t, CONV_PAD), :] = zero
        dw = [jnp.zeros((1, DN_HD), F32) for _ in range(DN_CONV)]
        for r in range(t // rt):
            rows = pl.ds(r * rt, rt)
            acc = _conv_acc(xp_ref, w, r * rt, rt)
            _, vjp = jax.vjp(functools.partial(_conv_post, kind), acc)
            (dacc,) = vjp(df_ref[rows, :] + dr_ref[rows, :])
            dpad_ref[pl.ds(CONV_PAD + r * rt, rt), :] = dacc
            for i in range(DN_CONV):
                xs = xp_ref[pl.ds(CONV_PAD + r * rt + i - DN_CONV // 2, rt), :]
                dw[i] = dw[i] + jnp.sum(dacc * xs, axis=0, keepdims=True)
        dw_ref[...] = jnp.concatenate(dw + [jnp.zeros((8 - DN_CONV, DN_HD), F32)], axis=0)
        for r in range(t // rt):
            acc = None
            for i in range(DN_CONV):
                term = w[i:i + 1, :] * dpad_ref[pl.ds(CONV_PAD + r * rt - i + DN_CONV // 2, rt), :]
                acc = term if acc is None else acc + term
            dx_ref[pl.ds(r * rt, rt), :] = acc

    return pl.pallas_call(
        body, name=name, grid=(DN_HEADS,),
        in_specs=[pl.BlockSpec((t + 2 * CONV_PAD, DN_HD), lambda j: (0, 6 * kind + j)),
                  pl.BlockSpec((8, DN_HD), lambda j: (0, 6 * kind + j)),
                  pl.BlockSpec((t, DN_HD), lambda j: (0, j)),
                  pl.BlockSpec((t, DN_HD), lambda j: (0, j))],
        out_specs=[pl.BlockSpec((t, DN_HD), lambda j: (0, j)), pl.BlockSpec((8, DN_HD), lambda j: (0, j))],
        out_shape=[jax.ShapeDtypeStruct((t, TOK_W), F32), jax.ShapeDtypeStruct((8, TOK_W), F32)],
        scratch_shapes=[pltpu.VMEM((t + 2 * CONV_PAD, DN_HD), F32)],
        compiler_params=_cparams(("parallel",)),
    )(xp, w8, d_f, d_r)


def _softplus(x):
    e = jnp.exp(-jnp.abs(x))
    return jnp.maximum(x, 0.0) + jnp.where(e < 1e-4, e - 0.5 * e * e, jnp.log(1.0 + e))


def _dot(a, b, dims=(((1,), (0,)), ((), ())), hi=False):
    if hi:
        return lax.dot_general(a, b, dims, precision=HI, preferred_element_type=F32)
    return lax.dot_general(a.astype(BF16), b.astype(BF16), dims, preferred_element_type=F32)


def _dn_chunk(rev, q, k, v, al, be, alc, bec, a_row, dt_row, a_rowc, dt_rowc, s):
    c = DN_CHUNK
    row = lax.broadcasted_iota(jnp.int32, (c, c), 0)
    col = lax.broadcasted_iota(jnp.int32, (c, c), 1)
    incl = (row <= col) if rev else (row >= col)
    strict = (row < col) if rev else (row > col)
    incl_f = incl.astype(F32)
    ones = jnp.ones((c, c), F32)
    nt = (((1,), (1,)), ((), ()))
    tn = (((0,), (0,)), ((), ()))

    g = -jnp.exp(a_row) * _softplus(al + dt_row)
    beta = _sigmoid(be)
    g_c = -jnp.exp(a_rowc) * _softplus(alc + dt_rowc)
    gc = _dot(incl_f, g, hi=True)
    gcc = _dot(incl_f, g_c, hi=True)
    gcr = _dot(ones, g_c * _incl_t(rev), hi=True)
    decay = jnp.exp(jnp.where(incl, gcc - gcr, NEG))
    kb = k * beta
    lmat = jnp.where(strict, _dot(kb, k, nt) * decay, 0.0)
    rhs = jnp.concatenate([v * beta, kb * jnp.exp(gc)], axis=1)
    xp = -lmat
    sol = rhs + _dot(xp, rhs, hi=True)
    for _ in range(5):
        xp = _dot(xp, xp, hi=True)
        sol = sol + _dot(xp, sol, hi=True)
    u, w = sol[:, :DN_HD], sol[:, DN_HD:]
    intra = jnp.where(incl, _dot(q, k, nt) * decay, 0.0)
    v_new = u - _dot(w, s)
    out = _dot(q * jnp.exp(gc), s) + _dot(intra, v_new)
    g_last = jnp.sum(g, axis=0, keepdims=True)
    s_new = s * jnp.exp(g_last) + _dot(k * jnp.exp(g_last - gc), v_new, tn)
    return out, s_new


def _incl_t(rev):
    c = DN_CHUNK
    row = lax.broadcasted_iota(jnp.int32, (c, c), 0)
    col = lax.broadcasted_iota(jnp.int32, (c, c), 1)
    return ((row >= col) if rev else (row <= col)).astype(F32)


DN_HG = 2


def _dn_views(nc, bwd):
    c, hg = DN_CHUNK, DN_HG
    if bwd:
        f_blk = lambda s: nc - 1 - s
        r_blk = lambda s: s
        st_blk = lambda s: nc - 1 - s
    else:
        f_blk = lambda s: s
        r_blk = lambda s: nc - 1 - s
        st_blk = lambda s: s
    tok_f = pl.BlockSpec((c, hg * DN_HD), lambda g, s: (f_blk(s), g))
    tok_r = pl.BlockSpec((c, hg * DN_HD), lambda g, s: (r_blk(s), g))
    gate_f = pl.BlockSpec((None, hg, c, DN_HD), lambda g, s: (0, g, f_blk(s), 0))
    gate_r = pl.BlockSpec((None, hg, c, DN_HD), lambda g, s: (1, g, r_blk(s), 0))
    par = pl.BlockSpec((2, hg, 1, DN_HD), lambda g, s: (0, g, 0, 0))
    state = pl.BlockSpec((2, hg, None, DN_HD, DN_HD), lambda g, s: (0, g, st_blk(s), 0, 0))
    return tok_f, tok_r, gate_f, gate_r, par, state


def _dn_fwd(q, k, v, al, be, a_rows, dt_rows, name):
    t = q.shape[0]
    c, hg = DN_CHUNK, DN_HG
    nc = t // c
    tok_f, tok_r, gate_f, gate_r, par, state = _dn_views(nc, False)

    def body(qf, kf, vf, qr, kr, vr, alf, bef, alr, ber, a_ref, dt_ref, of_ref, or_ref, st_ref, s_ref):
        @pl.when(pl.program_id(1) == 0)
        def _():
            s_ref[...] = jnp.zeros_like(s_ref)

        for d in range(2):
            qx, kx, vx, alx, bex, ox = (qf, kf, vf, alf, bef, of_ref) if d == 0 else (qr, kr, vr, alr, ber, or_ref)
            for h in range(hg):
                sl = slice(DN_HD * h, DN_HD * (h + 1))
                s = s_ref[d, h]
                st_ref[d, h] = s
                out, s_new = _dn_chunk(d == 1, qx[:, sl], kx[:, sl], vx[:, sl], alx[h], bex[h],
                                       alx[h, :, 0:c], bex[h, :, 0:c], a_ref[d, h], dt_ref[d, h],
                                       a_ref[d, h, :, 0:c], dt_ref[d, h, :, 0:c], s)
                ox[:, sl] = out
                s_ref[d, h] = s_new

    return pl.pallas_call(
        body, name=name, grid=(DN_HEADS // hg, nc),
        in_specs=[tok_f] * 3 + [tok_r] * 3 + [gate_f, gate_f, gate_r, gate_r, par, par],
        out_specs=[tok_f, tok_r, state],
        out_shape=[jax.ShapeDtypeStruct((t, TOK_W), F32)] * 2
        + [jax.ShapeDtypeStruct((2, DN_HEADS, nc, DN_HD, DN_HD), F32)],
        scratch_shapes=[pltpu.VMEM((2, hg, DN_HD, DN_HD), F32)],
        compiler_params=_cparams(("parallel", "arbitrary")),
    )(q, k, v, q, k, v, al, be, al, be, a_rows, dt_rows)


def _dn_bwd(q, k, v, al, be, a_rows, dt_rows, states, do, name):
    t = q.shape[0]
    c, hg = DN_CHUNK, DN_HG
    nc = t // c
    tok_f, tok_r, gate_f, gate_r, par, state = _dn_views(nc, True)
    gout_f = pl.BlockSpec((hg, c, DN_HD), lambda g, s: (g, nc - 1 - s, 0))
    gout_r = pl.BlockSpec((hg, c, DN_HD), lambda g, s: (g, s, 0))

    def body(qf, kf, vf, qr, kr, vr, alf, bef, alr, ber, a_ref, dt_ref, st_ref, dof, dor,
             dqf, dkf, dvf, dqr, dkr, dvr, dalf, dbef, dalr, dber, da_ref, ddt_ref, ds_ref):
        first = pl.program_id(1) == 0

        @pl.when(first)
        def _():
            ds_ref[...] = jnp.zeros_like(ds_ref)
            da_ref[...] = jnp.zeros_like(da_ref)
            ddt_ref[...] = jnp.zeros_like(ddt_ref)

        def lanes(x):
            return jnp.sum(x, axis=-1, keepdims=True)

        for d in range(2):
            qx, kx, vx, alx, bex, dox = (qf, kf, vf, alf, bef, dof) if d == 0 else (qr, kr, vr, alr, ber, dor)
            dqx, dkx, dvx, dalx, dbex = (dqf, dkf, dvf, dalf, dbef) if d == 0 else (dqr, dkr, dvr, dalr, dber)
            for h in range(hg):
                sl = slice(DN_HD * h, DN_HD * (h + 1))
                _, vjp = jax.vjp(functools.partial(_dn_chunk, d == 1), qx[:, sl], kx[:, sl], vx[:, sl],
                                 alx[h], bex[h], alx[h, :, 0:c], bex[h, :, 0:c], a_ref[d, h], dt_ref[d, h],
                                 a_ref[d, h, :, 0:c], dt_ref[d, h, :, 0:c], st_ref[d, h])
                dq, dk, dv, dal, dbe, dalc, dbec, da, ddt, dac, ddtc, ds = vjp((dox[:, sl], ds_ref[d, h]))
                dqx[:, sl] = dq
                dkx[:, sl] = dk
                dvx[:, sl] = dv
                dalx[h] = jnp.broadcast_to(lanes(dal) + lanes(dalc), (c, DN_HD))
                dbex[h] = jnp.broadcast_to(lanes(dbe) + lanes(dbec), (c, DN_HD))
                da_ref[d, h] += jnp.broadcast_to(lanes(da) + lanes(dac), (1, DN_HD))
                ddt_ref[d, h] += jnp.broadcast_to(lanes(ddt) + lanes(ddtc), (1, DN_HD))
                ds_ref[d, h] = ds

    tok = jax.ShapeDtypeStruct((t, TOK_W), F32)
    gate = jax.ShapeDtypeStruct((DN_HEADS, t, DN_HD), F32)
    parsh = jax.ShapeDtypeStruct((2, DN_HEADS, 1, DN_HD), F32)
    res = pl.pallas_call(
        body, name=name, grid=(DN_HEADS // hg, nc),
        in_specs=[tok_f] * 3 + [tok_r] * 3 + [gate_f, gate_f, gate_r, gate_r, par, par, state, tok_f, tok_r],
        out_specs=[tok_f] * 3 + [tok_r] * 3 + [gout_f, gout_f, gout_r, gout_r, par, par],
        out_shape=[tok] * 6 + [gate] * 4 + [parsh] * 2,
        scratch_shapes=[pltpu.VMEM((2, hg, DN_HD, DN_HD), F32)],
        compiler_params=_cparams(("parallel", "arbitrary")),
    )(q, k, v, q, k, v, al, be, al, be, a_rows, dt_rows, states, do, do)
    dqf, dkf, dvf, dqr, dkr, dvr, dalf, dbef, dalr, dber, da, ddt = res
    return (dqf, dkf, dvf), (dqr, dkr, dvr), jnp.stack([dalf, dalr]), jnp.stack([dbef, dber]), da, ddt


BAND_BQ = 256
ROW_TB = 256
MEM_TB = 512
CONV_RT = 512


def _to_sub(x, dil):
    l = x.shape[0] // dil
    return x.reshape(l, dil, 4, ATT_HD).transpose(2, 1, 0, 3).reshape(4 * dil, l, ATT_HD)


def _from_sub(x, dil):
    l = x.shape[1]
    return x.reshape(4, dil, l, ATT_HD).transpose(2, 1, 0, 3).reshape(l * dil, 4 * ATT_HD)


def _heads_major(x):
    return x.reshape(x.shape[0], MEM_HEADS, ATT_HD).transpose(1, 0, 2)


def _heads_minor(x):
    return x.transpose(1, 0, 2).reshape(x.shape[1], MEM_HEADS * ATT_HD)


def _mem_kv_fwd(mem, gain, w_kv, li):
    (memn,) = _rowwise(_fn_pre, [mem], [gain], [(D, BF16)], mem.shape[0], f"memnorm_fwd{li}")
    kv = _matmul(memn, w_kv, "nn", F32, f"memkv_fwd{li}")
    return _heads_major(kv[:, :MEM_W]), _heads_major(kv[:, MEM_W:]), memn


def _mem_kv_bwd(mem, gain, w_kv, memn, dkm, dvm, li):
    dkv = jnp.concatenate([_heads_minor(dkm), _heads_minor(dvm)], axis=1).astype(BF16)
    dw = _matmul(memn, dkv, "tn", F32, f"memkv_dw{li}")
    dmemn = _matmul(dkv, w_kv, "nt", F32, f"memkv_dx{li}")
    _, (dgain,) = _rowwise_bwd(_fn_pre, [mem], [gain], [dmemn], [None], mem.shape[0], f"memnorm_bwd{li}")
    return dw, dgain


def _attn_mixer_fwd(p, rel_bias, km, vm):
    t = p.shape[0]
    saved, outs, lses = [], [], []
    for gi, (_, dil) in enumerate(DIL_GROUPS):
        l_sub = t // dil
        bq = min(BAND_BQ, l_sub)
        q = _to_sub(p[:, 256 * gi:256 * (gi + 1)], dil)
        pad = ((0, 0), (BAND_HALF, BAND_HALF), (0, 0))
        k = jnp.pad(_to_sub(p[:, TOK_W + 256 * gi:TOK_W + 256 * (gi + 1)], dil), pad)
        v = jnp.pad(_to_sub(p[:, 2 * TOK_W + 256 * gi:2 * TOK_W + 256 * (gi + 1)], dil), pad)
        bm = _band_bias(rel_bias, gi, dil, bq)
        o, lse = _band_fwd(q, k, v, bm, dil, l_sub, bq, f"band_fwd{gi}")
        outs.append(_from_sub(o, dil))
        lses.append(_from_sub(lse, dil))
        saved.append((q, k, v, bm))
    o_all = jnp.concatenate(outs, axis=1)
    lse_all = jnp.concatenate(lses, axis=1)
    (mixed,) = _rowwise(_fn_combine, [o_all, lse_all], [], [(TOK_W, BF16)], ROW_TB, "combine_fwd")
    qm = _heads_major(p[:, 3 * TOK_W:])
    memo = _mem_fwd(qm, km, vm, min(MEM_TB, t), "mem_fwd0")
    cat = jnp.concatenate([mixed, _heads_minor(memo).astype(BF16)], axis=1)
    return cat, (saved, o_all, lse_all, qm)


def _attn_mixer_bwd(dcat, res, km, vm):
    saved, o_all, lse_all, qm = res
    t = dcat.shape[0]
    (do_all, dlse_all), _ = _rowwise_bwd(_fn_combine, [o_all, lse_all], [], [dcat[:, :TOK_W]], [F32, F32],
                                         ROW_TB, "combine_bwd")
    dqs, dks, dvs, dbms = [], [], [], []
    for gi, (_, dil) in enumerate(DIL_GROUPS):
        l_sub = t // dil
        bq = min(BAND_BQ, l_sub)
        q, k, v, bm = saved[gi]
        do = _to_sub(do_all[:, 256 * gi:256 * (gi + 1)], dil)
        dl = _to_sub(dlse_all[:, 256 * gi:256 * (gi + 1)], dil)
        dq, dk, dv, dbm = _band_bwd(q, k, v, bm, do, dl, dil, l_sub, bq, f"band_bwd{gi}")
        dqs.append(_from_sub(dq, dil))
        dks.append(_from_sub(dk[:, BAND_HALF:-BAND_HALF], dil))
        dvs.append(_from_sub(dv[:, BAND_HALF:-BAND_HALF], dil))
        dbms.append(dbm)
    dqm, dkm, dvm = _mem_bwd(qm, km, vm, _heads_major(dcat[:, TOK_W:]), min(MEM_TB, t), "mem_bwd0")
    dp = jnp.concatenate(dqs + dks + dvs + [_heads_minor(dqm)], axis=1).astype(BF16)
    return dp, _relbias_grad(dbms, "relbias_grad"), dkm, dvm


def _dn_mixer_fwd(p, conv_w, a_log, dt_bias, out_norm, km, vm):
    t = p.shape[0]
    rt = min(CONV_RT, t)
    xp = jnp.pad(p[:, :3 * TOK_W], ((CONV_PAD, CONV_PAD), (0, 0)))
    w8 = jnp.pad(conv_w, ((0, 8 - DN_CONV), (0, 0)))
    q = _conv_fwd(xp, w8, 0, rt, "conv_fwd_q")
    k = _conv_fwd(xp, w8, 1, rt, "conv_fwd_k")
    v = _conv_fwd(xp, w8, 2, rt, "conv_fwd_v")
    gate = p[:, 4 * TOK_W:4 * TOK_W + 4 * DN_HEADS].reshape(t, 2, 2, DN_HEADS)
    bshape = (2, DN_HEADS, t, DN_HD)
    al = jnp.broadcast_to(gate[:, :, 0, :].transpose(1, 2, 0)[..., None], bshape)
    be = jnp.broadcast_to(gate[:, :, 1, :].transpose(1, 2, 0)[..., None], bshape)
    a_rows = jnp.broadcast_to(a_log[:, :, None, None], (2, DN_HEADS, 1, DN_HD))
    dt_rows = jnp.broadcast_to(dt_bias[:, :, None, None], (2, DN_HEADS, 1, DN_HD))
    o_f, o_r, states = _dn_fwd(q, k, v, al, be, a_rows, dt_rows, "dn_fwd")
    z = p[:, 3 * TOK_W:4 * TOK_W]
    gain = out_norm.reshape(1, DN_HD)
    (og,) = _rowwise(_fn_outnorm, [o_f, o_r, z], [gain], [(TOK_W, BF16)], ROW_TB, "outnorm_fwd")
    qm = _heads_major(p[:, 4 * TOK_W + 4 * DN_HEADS:DN_IN])
    memo = _mem_fwd(qm, km, vm, min(MEM_TB, t), "mem_fwd1")
    cat = jnp.concatenate([og, _heads_minor(memo).astype(BF16)], axis=1)
    return cat, (xp, w8, q, k, v, al, be, a_rows, dt_rows, o_f, o_r, states, z, gain, qm)


def _dn_mixer_bwd(dcat, res, km, vm):
    xp, w8, q, k, v, al, be, a_rows, dt_rows, o_f, o_r, states, z, gain, qm = res
    t = dcat.shape[0]
    rt = min(CONV_RT, t)
    (do, dz), (dgain,) = _rowwise_bwd(_fn_outnorm, [o_f, o_r, z], [gain], [dcat[:, :TOK_W]], [F32, None, F32],
                                      ROW_TB, "outnorm_bwd")
    d_f, d_r, dal, dbe, da, ddt = _dn_bwd(q, k, v, al, be, a_rows, dt_rows, states, do, "dn_bwd")
    dxs, dws = [], []
    for kind, nm in enumerate("qkv"):
        dx, dw = _conv_bwd(xp, w8, d_f[kind], d_r[kind], kind, rt, f"conv_bwd_{nm}")
        dxs.append(dx)
        dws.append(dw)
    dconv = jnp.concatenate(dws, axis=1)[:DN_CONV]
    dgate = jnp.stack([dal[..., 0], dbe[..., 0]], axis=1).transpose(3, 0, 1, 2).reshape(t, 4 * DN_HEADS)
    dqm, dkm, dvm = _mem_bwd(qm, km, vm, _heads_major(dcat[:, TOK_W:]), min(MEM_TB, t), "mem_bwd1")
    dp = jnp.concatenate(dxs + [dz, dgate, _heads_minor(dqm), jnp.zeros((t, DN_IN_PAD - DN_IN), F32)],
                         axis=1).astype(BF16)
    return dp, dconv, da[:, :, 0, 0], ddt[:, :, 0, 0], dgain.reshape(DN_HD), dkm, dvm


SWI_TB = 128


def _ffn_fwd(h, w_gu, w_d, li):
    gu = _matmul(h, w_gu, "nn", F32, f"ffn_gu{li}")
    (a,) = _rowwise(_fn_swiglu, [gu], [], [(D_FF, BF16)], SWI_TB, f"swiglu_fwd{li}")
    return _matmul(a, w_d, "nn", F32, f"ffn_down{li}"), gu, a


def _ffn_bwd(df, h, w_gu, w_d, gu, a, li):
    da = _matmul(df, w_d, "nt", F32, f"ffn_down_dx{li}")
    dwd = _matmul(a, df, "tn", F32, f"ffn_down_dw{li}")
    (dgu,), _ = _rowwise_bwd(_fn_swiglu, [gu], [], [da], [BF16], SWI_TB, f"swiglu_bwd{li}")
    dh = _matmul(dgu, w_gu, "nt", F32, f"ffn_gu_dx{li}")
    dwgu = _matmul(h, dgu, "tn", F32, f"ffn_gu_dw{li}")
    return dh, dwgu, dwd


def _fn_first(x, g):
    return x, _rms(x, g)


def _me_xyc():
    return lax.axis_index("x"), lax.axis_index("y"), lax.axis_index("c")


def _flip(coords, k):
    x, y, c = coords
    return (1 - x if k & 4 else x, 1 - y if k & 2 else y, 1 - c if k & 1 else c)


def _index(coords):
    x, y, c = coords
    return 4 * x + 2 * y + c


def _all_gather(slab, name):
    def body(x_ref, o_ref, send_sems, recv_sems, loc_sem):
        me = _me_xyc()
        mine = pltpu.make_async_copy(x_ref, o_ref.at[_index(me)], loc_sem)
        mine.start()

        def copy(k, block_owner, to):
            return pltpu.make_async_remote_copy(
                src_ref=x_ref, dst_ref=o_ref.at[_index(block_owner)],
                send_sem=send_sems.at[k - 1], recv_sem=recv_sems.at[k - 1], device_id=to, device_id_type=MESH)

        sends = [copy(k, me, _flip(me, k)) for k in range(1, N_DEV)]
        for cp in sends:
            cp.start()
        for k in range(1, N_DEV):
            copy(k, _flip(me, k), me).wait_recv()
        for cp in sends:
            cp.wait_send()
        mine.wait()

    return pl.pallas_call(
        body, name=name,
        in_specs=[pl.BlockSpec(memory_space=pl.ANY)], out_specs=pl.BlockSpec(memory_space=pl.ANY),
        out_shape=jax.ShapeDtypeStruct((N_DEV,) + slab.shape, slab.dtype),
        scratch_shapes=[pltpu.SemaphoreType.DMA((N_DEV - 1,)), pltpu.SemaphoreType.DMA((N_DEV - 1,)),
                        pltpu.SemaphoreType.DMA],
    )(slab)


def _exchange(send, name):
    def body(s_ref, r_ref, send_sems, recv_sems, loc_sem):
        me = _me_xyc()
        mine = pltpu.make_async_copy(s_ref.at[_index(me)], r_ref.at[_index(me)], loc_sem)
        mine.start()

        def copy(k, sender, to):
            return pltpu.make_async_remote_copy(
                src_ref=s_ref.at[_index(to)], dst_ref=r_ref.at[_index(sender)],
                send_sem=send_sems.at[k - 1], recv_sem=recv_sems.at[k - 1], device_id=to, device_id_type=MESH)

        sends = [copy(k, me, _flip(me, k)) for k in range(1, N_DEV)]
        for cp in sends:
            cp.start()
        for k in range(1, N_DEV):
            copy(k, _flip(me, k), me).wait_recv()
        for cp in sends:
            cp.wait_send()
        mine.wait()

    return pl.pallas_call(
        body, name=name,
        in_specs=[pl.BlockSpec(memory_space=pl.ANY)], out_specs=pl.BlockSpec(memory_space=pl.ANY),
        out_shape=jax.ShapeDtypeStruct(send.shape, send.dtype),
        scratch_shapes=[pltpu.SemaphoreType.DMA((N_DEV - 1,)), pltpu.SemaphoreType.DMA((N_DEV - 1,)),
                        pltpu.SemaphoreType.DMA],
    )(send)


def _adam_math(g, w, m, v):
    m = ADAM_B1 * m + (1.0 - ADAM_B1) * g
    v = ADAM_B2 * v + (1.0 - ADAM_B2) * (g * g)
    m_hat = m / (1.0 - ADAM_B1 ** ADAM_STEP)
    v_hat = v / (1.0 - ADAM_B2 ** ADAM_STEP)
    delta = -ADAM_LR * (m_hat / (jnp.sqrt(v_hat) + ADAM_EPS) + ADAM_WD * w)
    return delta, m, v


def _sum_slabs(r_ref):
    g = r_ref[0].astype(F32)
    for s in range(1, N_DEV):
        g = g + r_ref[s].astype(F32)
    return g


def _adamw_reduce(recv, w, m, v, tb, name):
    r, c = w.shape

    def body(r_ref, w_ref, m_ref, v_ref, g_ref, d_ref, nm_ref, nv_ref):
        g = _sum_slabs(r_ref)
        g_ref[...] = g
        d_ref[...], nm_ref[...], nv_ref[...] = _adam_math(g, w_ref[...], m_ref[...], v_ref[...])

    blk = pl.BlockSpec((tb, c), lambda i: (i, 0))
    return pl.pallas_call(
        body, name=name, grid=(r // tb,),
        in_specs=[pl.BlockSpec((N_DEV, tb, c), lambda i: (0, i, 0)), blk, blk, blk],
        out_specs=[blk] * 4, out_shape=[jax.ShapeDtypeStruct((r, c), F32)] * 4,
        compiler_params=_cparams(("parallel",)),
    )(recv, w, m, v)


def _reduce8(recv, name):
    def body(r_ref, g_ref):
        g_ref[...] = _sum_slabs(r_ref)

    return pl.pallas_call(body, name=name, out_shape=jax.ShapeDtypeStruct(recv.shape[1:], F32))(recv)


def _adamw(g, w, m, v, name):
    def body(g_ref, w_ref, m_ref, v_ref, d_ref, nm_ref, nv_ref):
        d_ref[...], nm_ref[...], nv_ref[...] = _adam_math(g_ref[...], w_ref[...], m_ref[...], v_ref[...])

    return pl.pallas_call(body, name=name, out_shape=[jax.ShapeDtypeStruct(w.shape, F32)] * 3)(g, w, m, v)


_BIG = (((1, D, 320), "col"), ((1, 128, D), "row"), ((1, D, 419), "col"), ((1, 128, D), "row"),
        ((2, 128, 512), "row"), ((2, D, 704), "col"), ((2, 352, D), "row"))
_BIG_N = sum(int(np.prod(s)) for s, _ in _BIG)
assert _BIG_N <= BIG_ROWS * PACK_C
CONV_SHARD = (1, DN_CONV, 288)
CONV_BITS_ROWS = 16
SMALL_BITS_ROWS = 2 * SMALL_ROWS


def _pack_big(arrs, dtype):
    flat = jnp.concatenate([a.astype(dtype).reshape(-1) for a in arrs])
    return jnp.pad(flat, (0, BIG_ROWS * PACK_C - _BIG_N)).reshape(BIG_ROWS, PACK_C)


def _unpack_big(packed):
    flat, out, off = packed.reshape(-1), [], 0
    for shp, _ in _BIG:
        n = int(np.prod(shp))
        out.append(flat[off:off + n].reshape(shp))
        off += n
    return out


def _full_from_gathered(g):
    flat, out, off = g.reshape(N_DEV, -1), [], 0
    for shp, kind in _BIG:
        n = int(np.prod(shp))
        part = flat[:, off:off + n].reshape((N_DEV,) + shp)
        off += n
        if kind == "col":
            out.append(part.transpose(1, 2, 0, 3).reshape(shp[0], shp[1], N_DEV * shp[2]))
        else:
            out.append(part.transpose(1, 0, 2, 3).reshape(shp[0], N_DEV * shp[1], shp[2]))
    return out


def _split_full(fulls, dtype):
    parts = []
    for full, (shp, kind) in zip(fulls, _BIG):
        if kind == "col":
            p = full.reshape(shp[0], shp[1], N_DEV, shp[2]).transpose(2, 0, 1, 3)
        else:
            p = full.reshape(shp[0], N_DEV, shp[1], shp[2]).transpose(1, 0, 2, 3)
        parts.append(p.astype(dtype).reshape(N_DEV, -1))
    flat = jnp.pad(jnp.concatenate(parts, axis=1), ((0, 0), (0, BIG_ROWS * PACK_C - _BIG_N)))
    return flat.reshape(N_DEV, BIG_ROWS, PACK_C)


def _f32_bits(x, rows):
    bits = lax.bitcast_convert_type(x.reshape(-1), BF16).reshape(-1)
    return jnp.pad(bits, (0, rows * PACK_C - bits.size)).reshape(rows, PACK_C)


def _bits_f32(bits, n):
    lead = bits.shape[:-2]
    flat = bits.reshape(lead + (-1,))[..., :2 * n].reshape(lead + (n, 2))
    return lax.bitcast_convert_type(flat, F32)


def _pack_small(arrs, rows):
    flat = jnp.concatenate([a.astype(F32).reshape(-1) for a in arrs])
    return jnp.pad(flat, (0, rows * PACK_C - flat.size)).reshape(rows, PACK_C)


def _unpack_small(packed, shapes):
    flat, out, off = packed.reshape(-1), [], 0
    for shp in shapes:
        n = int(np.prod(shp))
        out.append(flat[off:off + n].reshape(shp))
        off += n
    return out


def kernel(x, mem, rel_bias, att_w_in, att_w_out, dn_w_in, dn_conv, dn_a_log, dn_dt_bias, dn_out_norm, dn_w_out, mem_norm, mem_w_kv, norm_mix_pre, norm_mix_post, norm_ffn_pre, norm_ffn_post, ffn_w_gate_up, ffn_w_down, loss_target, m_rel_bias, m_att_w_in, m_att_w_out, m_dn_w_in, m_dn_conv, m_dn_a_log, m_dn_dt_bias, m_dn_out_norm, m_dn_w_out, m_mem_norm, m_mem_w_kv, m_norm_mix_pre, m_norm_mix_post, m_norm_ffn_pre, m_norm_ffn_post, m_ffn_w_gate_up, m_ffn_w_down, v_rel_bias, v_att_w_in, v_att_w_out, v_dn_w_in, v_dn_conv, v_dn_a_log, v_dn_dt_bias, v_dn_out_norm, v_dn_w_out, v_mem_norm, v_mem_w_kv, v_norm_mix_pre, v_norm_mix_post, v_norm_ffn_pre, v_norm_ffn_post, v_ffn_w_gate_up, v_ffn_w_down):
    x0, mem0, tgt = x[0], mem[0], loss_target[0]
    t = x0.shape[0]
    axes = ("x", "y", "c")

    big_w = [att_w_in, att_w_out, dn_w_in, dn_w_out, mem_w_kv, ffn_w_gate_up, ffn_w_down]
    slab = jnp.concatenate([_pack_big(big_w, BF16), _f32_bits(dn_conv, CONV_BITS_ROWS)], axis=0)
    gath = _all_gather(slab, "allgather_weights")
    w_att_in, w_att_out, w_dn_in, w_dn_out, w_kv, w_gu, w_down = _full_from_gathered(gath[:, :BIG_ROWS])
    w_att_in, w_att_out, w_dn_out = w_att_in[0], w_att_out[0], w_dn_out[0]
    w_dn_in = jnp.pad(w_dn_in[0], ((0, 0), (0, DN_IN_PAD - DN_IN)))
    conv_full = _bits_f32(gath[:, BIG_ROWS:], DN_CONV * 288).reshape(N_DEV, DN_CONV, 288)
    conv_full = conv_full.transpose(1, 0, 2).reshape(DN_CONV, 3 * TOK_W)

    def gain(a, i):
        return a[i].reshape(1, D)

    (h0,) = _rowwise(_fn_pre, [x0], [gain(norm_mix_pre, 0)], [(D, BF16)], ROW_TB, "pre0")
    km0, vm0, memn0 = _mem_kv_fwd(mem0, gain(mem_norm, 0), w_kv[0], 0)
    p0 = _matmul(h0, w_att_in, "nn", F32, "att_in")
    cat0, res0 = _attn_mixer_fwd(p0, rel_bias, km0, vm0)
    y0 = _matmul(cat0, w_att_out, "nn", F32, "att_out")
    g_a = [gain(norm_mix_post, 0), gain(norm_ffn_pre, 0)]
    x1, h1 = _rowwise(_fn_res_pre, [x0, y0], g_a, [(D, F32), (D, BF16)], ROW_TB, "res_pre0")
    f0, gu0, a0 = _ffn_fwd(h1, w_gu[0], w_down[0], 0)
    g_b = [gain(norm_ffn_post, 0), gain(norm_mix_pre, 1)]
    x2, h2 = _rowwise(_fn_res_pre, [x1, f0], g_b, [(D, F32), (D, BF16)], ROW_TB, "res_pre1")
    km1, vm1, memn1 = _mem_kv_fwd(mem0, gain(mem_norm, 1), w_kv[1], 1)
    p1 = _matmul(h2, w_dn_in, "nn", F32, "dn_in")
    cat1, res1 = _dn_mixer_fwd(p1, conv_full, dn_a_log[0], dn_dt_bias[0], dn_out_norm[0], km1, vm1)
    y1 = _matmul(cat1, w_dn_out, "nn", F32, "dn_out")
    g_c = [gain(norm_mix_post, 1), gain(norm_ffn_pre, 1)]
    x3, h3 = _rowwise(_fn_res_pre, [x2, y1], g_c, [(D, F32), (D, BF16)], ROW_TB, "res_pre2")
    f1, gu1, a1 = _ffn_fwd(h3, w_gu[1], w_down[1], 1)
    g_d = [gain(norm_ffn_post, 1)]
    (x4,) = _rowwise(_fn_res, [x3, f1], g_d, [(D, F32)], ROW_TB, "res3")
    dx4, lrow = _loss_kernel(x4, tgt, ROW_TB, "loss")
    loss = lax.psum(lrow[0, 0] * (0.5 / D), axes)

    (df1,), (dg_fpost1,) = _rowwise_bwd(_fn_res, [x3, f1], g_d, [dx4], [None, BF16], ROW_TB, "res3_bwd")
    dh3, dwgu1, dwd1 = _ffn_bwd(df1, h3, w_gu[1], w_down[1], gu1, a1, 1)
    (dx2, dy1), (dg_mpost1, dg_fpre1) = _rowwise_bwd(_fn_res_pre, [x2, y1], g_c, [dx4, dh3], [F32, BF16],
                                                     ROW_TB, "res_pre2_bwd")
    dcat1 = _matmul(dy1, w_dn_out, "nt", F32, "dn_out_dx")
    dw_dn_out = _matmul(cat1, dy1, "tn", F32, "dn_out_dw")
    dp1, dconv, da_log, ddt_bias, dout_norm, dkm1, dvm1 = _dn_mixer_bwd(dcat1, res1, km1, vm1)
    dwkv1, dg_mem1 = _mem_kv_bwd(mem0, gain(mem_norm, 1), w_kv[1], memn1, dkm1, dvm1, 1)
    dh2 = _matmul(dp1, w_dn_in, "nt", F32, "dn_in_dx")
    dw_dn_in = _matmul(h2, dp1, "tn", F32, "dn_in_dw")[:, :DN_IN]
    (dx1, df0), (dg_fpost0, dg_mpre1) = _rowwise_bwd(_fn_res_pre, [x1, f0], g_b, [dx2, dh2], [F32, BF16],
                                                     ROW_TB, "res_pre1_bwd")
    dh1, dwgu0, dwd0 = _ffn_bwd(df0, h1, w_gu[0], w_down[0], gu0, a0, 0)
    (dx0, dy0), (dg_mpost0, dg_fpre0) = _rowwise_bwd(_fn_res_pre, [x0, y0], g_a, [dx1, dh1], [F32, BF16],
                                                     ROW_TB, "res_pre0_bwd")
    dcat0 = _matmul(dy0, w_att_out, "nt", F32, "att_out_dx")
    dw_att_out = _matmul(cat0, dy0, "tn", F32, "att_out_dw")
    dp0, drel, dkm0, dvm0 = _attn_mixer_bwd(dcat0, res0, km0, vm0)
    dwkv0, dg_mem0 = _mem_kv_bwd(mem0, gain(mem_norm, 0), w_kv[0], memn0, dkm0, dvm0, 0)
    dh0 = _matmul(dp0, w_att_in, "nt", F32, "att_in_dx")
    dw_att_in = _matmul(h0, dp0, "tn", F32, "att_in_dw")
    (grad_x,), (dg_mpre0,) = _rowwise_bwd(_fn_first, [x0], [gain(norm_mix_pre, 0)], [dx0, dh0], [F32],
                                          ROW_TB, "pre0_bwd")

    full_grads = [dw_att_in[None], dw_att_out[None], dw_dn_in[None], dw_dn_out[None],
                  jnp.stack([dwkv0, dwkv1]), jnp.stack([dwgu0, dwgu1]), jnp.stack([dwd0, dwd1])]
    small_grads = [drel, da_log, ddt_bias, dout_norm, jnp.concatenate([dg_mem0, dg_mem1]),
                   jnp.concatenate([dg_mpre0, dg_mpre1]), jnp.concatenate([dg_mpost0, dg_mpost1]),
                   jnp.concatenate([dg_fpre0, dg_fpre1]), jnp.concatenate([dg_fpost0, dg_fpost1]), dconv]
    small_bits = _f32_bits(_pack_small(small_grads, SMALL_ROWS), SMALL_BITS_ROWS)
    send = jnp.concatenate([_split_full(full_grads, BF16),
                            jnp.broadcast_to(small_bits, (N_DEV,) + small_bits.shape)], axis=1)
    recv = _exchange(send, "exchange_grads")

    big_m = [m_att_w_in, m_att_w_out, m_dn_w_in, m_dn_w_out, m_mem_w_kv, m_ffn_w_gate_up, m_ffn_w_down]
    big_v = [v_att_w_in, v_att_w_out, v_dn_w_in, v_dn_w_out, v_mem_w_kv, v_ffn_w_gate_up, v_ffn_w_down]
    outs_big = _adamw_reduce(recv, _pack_big(big_w, F32), _pack_big(big_m, F32), _pack_big(big_v, F32),
                             240, "adamw_big")
    g_big, d_big, nm_big, nv_big = [_unpack_big(o) for o in outs_big]

    small_recv = _bits_f32(recv[:, BIG_ROWS:], SMALL_ROWS * PACK_C).reshape(N_DEV, SMALL_ROWS, PACK_C)
    g_small = _reduce8(small_recv, "reduce_small")
    rep_shapes = [(32, 12), (1, 2, 6), (1, 2, 6), (1, 128), (2, D), (2, D), (2, D), (2, D), (2, D)]
    *g_rep, g_conv_full = _unpack_small(g_small, rep_shapes + [(DN_CONV, 3 * TOK_W)])
    me = _index(_me_xyc())
    g_conv = lax.dynamic_slice(g_conv_full, (0, me * 288), (DN_CONV, 288)).reshape(CONV_SHARD)
    small_shapes = rep_shapes + [CONV_SHARD]
    small_w = [rel_bias, dn_a_log, dn_dt_bias, dn_out_norm, mem_norm, norm_mix_pre, norm_mix_post,
               norm_ffn_pre, norm_ffn_post, dn_conv]
    small_m = [m_rel_bias, m_dn_a_log, m_dn_dt_bias, m_dn_out_norm, m_mem_norm, m_norm_mix_pre, m_norm_mix_post,
               m_norm_ffn_pre, m_norm_ffn_post, m_dn_conv]
    small_v = [v_rel_bias, v_dn_a_log, v_dn_dt_bias, v_dn_out_norm, v_mem_norm, v_norm_mix_pre, v_norm_mix_post,
               v_norm_ffn_pre, v_norm_ffn_post, v_dn_conv]
    g_small_list = g_rep + [g_conv]
    outs_small = _adamw(_pack_small(g_small_list, 24), _pack_small(small_w, 24), _pack_small(small_m, 24),
                        _pack_small(small_v, 24), "adamw_small")
    d_small, nm_small, nv_small = [_unpack_small(o, small_shapes) for o in outs_small]

    def ordered(small, big):
        return [small[0], big[0], big[1], big[2], small[9], small[1], small[2], small[3], big[3], small[4],
                big[4], small[5], small[6], small[7], small[8], big[5], big[6]]

    g_small_out = [g.reshape(s) for g, s in zip(g_small_list, small_shapes)]
    return (loss, grad_x[None], *ordered(g_small_out, g_big), *ordered(d_small, d_big),
            *ordered(nm_small, nm_big), *ordered(nv_small, nv_big))
```

```python
import functools
import math

import numpy as np
import jax
import jax.numpy as jnp
from jax import lax
from jax.experimental import pallas as pl
from jax.experimental.pallas import tpu as pltpu

F32 = jnp.float32
BF16 = jnp.bfloat16
HI = lax.Precision.HIGHEST
MESH = pl.DeviceIdType.MESH

N_DEV = 8
D = 1024
EPS = 1e-6
NEG = -1e30
TOK_W = 768
MEM_W = 256
ATT_HD = 64
DIL_GROUPS = ((128, 1), (512, 4), (2048, 16))
BAND_HALF = 64
REL_BUCKETS = 32
REL_MAX_DIST = 1024
DN_HD = 128
DN_HEADS = 6
DN_CONV = 5
DN_CHUNK = 64
MEM_HEADS = 4
D_FF = 2816
ATT_IN = 2560
DN_IN = 3352
DN_IN_PAD = 3456

ADAM_LR, ADAM_B1, ADAM_B2, ADAM_EPS, ADAM_WD, ADAM_STEP = 0.001, 0.9, 0.999, 1e-08, 0.01, 10

PACK_C = 512
BIG_ROWS = 6480
SMALL_ROWS = 48
VMEM_LIMIT = 48 * 1024 * 1024


def _cparams(sem=None):
    kw = dict(vmem_limit_bytes=VMEM_LIMIT)
    if sem is not None:
        kw["dimension_semantics"] = sem
    return pltpu.CompilerParams(**kw)


def _tile(n, cap):
    if n <= cap:
        return n
    best = None
    for t in range(128, cap + 1, 128):
        if n % t == 0:
            best = t
    assert best is not None, (n, cap)
    return best


def _matmul(a, b, mode, out_dtype, name, tm=512, tn=512, tk=1024):
    if mode == "nn":
        (m, kc), (_, n) = a.shape, b.shape
        dims = (((1,), (0,)), ((), ()))
    elif mode == "nt":
        (m, kc), (n, _) = a.shape, b.shape
        dims = (((1,), (1,)), ((), ()))
    else:
        (kc, m), (_, n) = a.shape, b.shape
        dims = (((0,), (0,)), ((), ()))
    tm = m if m <= tm else _tile(m, tm)
    tn = _tile(n, tn)
    tk = _tile(kc, tk)
    nk = kc // tk

    def body(a_ref, b_ref, o_ref, acc_ref):
        k = pl.program_id(2)
        part = lax.dot_general(a_ref[...], b_ref[...], dims, preferred_element_type=F32)

        @pl.when(k == 0)
        def _():
            acc_ref[...] = part

        @pl.when(k > 0)
        def _():
            acc_ref[...] += part

        @pl.when(k == nk - 1)
        def _():
            o_ref[...] = acc_ref[...].astype(o_ref.dtype)

    if mode == "nn":
        a_spec = pl.BlockSpec((tm, tk), lambda i, j, k: (i, k))
        b_spec = pl.BlockSpec((tk, tn), lambda i, j, k: (k, j))
    elif mode == "nt":
        a_spec = pl.BlockSpec((tm, tk), lambda i, j, k: (i, k))
        b_spec = pl.BlockSpec((tn, tk), lambda i, j, k: (j, k))
    else:
        a_spec = pl.BlockSpec((tk, tm), lambda i, j, k: (k, i))
        b_spec = pl.BlockSpec((tk, tn), lambda i, j, k: (k, j))
    return pl.pallas_call(
        body, name=name, grid=(m // tm, n // tn, nk),
        in_specs=[a_spec, b_spec],
        out_specs=pl.BlockSpec((tm, tn), lambda i, j, k: (i, j)),
        out_shape=jax.ShapeDtypeStruct((m, n), out_dtype),
        scratch_shapes=[pltpu.VMEM((tm, tn), F32)],
        compiler_params=_cparams(("parallel", "parallel", "arbitrary")),
    )(a, b)


def _rowwise(fn, rows, params, outs, tb, name):
    t = rows[0].shape[0]
    nr, npar = len(rows), len(params)

    def body(*refs):
        ins = [r[...].astype(F32) for r in refs[:nr + npar]]
        res = fn(*ins)
        for o_ref, r in zip(refs[nr + npar:], res):
            o_ref[...] = r.astype(o_ref.dtype)

    return pl.pallas_call(
        body, name=name, grid=(t // tb,),
        in_specs=[pl.BlockSpec((tb, r.shape[1]), lambda i: (i, 0)) for r in rows]
        + [pl.BlockSpec(p.shape, lambda i: (0, 0)) for p in params],
        out_specs=[pl.BlockSpec((tb, c), lambda i: (i, 0)) for c, _ in outs],
        out_shape=[jax.ShapeDtypeStruct((t, c), dt) for c, dt in outs],
        compiler_params=_cparams(("parallel",)),
    )(*rows, *params)


def _rowwise_bwd(fn, rows, params, cots, row_grad, tb, name):
    t = rows[0].shape[0]
    nr, npar, nc = len(rows), len(params), len(cots)
    want = [i for i, g in enumerate(row_grad) if g is not None]

    def body(*refs):
        ins = [r[...].astype(F32) for r in refs[:nr + npar]]
        cts = tuple(r[...].astype(F32) for r in refs[nr + npar:nr + npar + nc])
        outs = refs[nr + npar + nc:]
        _, vjp = jax.vjp(fn, *ins)
        grads = vjp(cts)
        for o_ref, i in zip(outs[:len(want)], want):
            o_ref[...] = grads[i].astype(o_ref.dtype)
        first = pl.program_id(0) == 0
        for o_ref, g in zip(outs[len(want):], grads[nr:]):
            @pl.when(first)
            def _(o_ref=o_ref, g=g):
                o_ref[...] = g

            @pl.when(jnp.logical_not(first))
            def _(o_ref=o_ref, g=g):
                o_ref[...] += g

    res = pl.pallas_call(
        body, name=name, grid=(t // tb,),
        in_specs=[pl.BlockSpec((tb, r.shape[1]), lambda i: (i, 0)) for r in rows]
        + [pl.BlockSpec(p.shape, lambda i: (0, 0)) for p in params]
        + [pl.BlockSpec((tb, c.shape[1]), lambda i: (i, 0)) for c in cots],
        out_specs=[pl.BlockSpec((tb, rows[i].shape[1]), lambda i_: (i_, 0)) for i in want]
        + [pl.BlockSpec(p.shape, lambda i: (0, 0)) for p in params],
        out_shape=[jax.ShapeDtypeStruct(rows[i].shape, row_grad[i]) for i in want]
        + [jax.ShapeDtypeStruct(p.shape, F32) for p in params],
        compiler_params=_cparams(("arbitrary",)),
    )(*rows, *params, *cots)
    return list(res[:len(want)]), list(res[len(want):])


def _rms(x, g):
    return x * lax.rsqrt(jnp.mean(x * x, axis=-1, keepdims=True) + EPS) * g


def _fn_pre(x, g):
    return (_rms(x, g),)


def _fn_res_pre(x, y, g_post, g_pre):
    x1 = x + _rms(y, g_post)
    return x1, _rms(x1, g_pre)


def _fn_res(x, y, g_post):
    return (x + _rms(y, g_post),)


def _sigmoid(x):
    return 1.0 / (1.0 + jnp.exp(-x))


def _silu(x):
    return x * _sigmoid(x)


def _fn_swiglu(gu):
    return (_silu(gu[:, :D_FF]) * gu[:, D_FF:],)


def _fn_combine(o, lse):
    ls = [lse[:, 256 * g:256 * (g + 1)] for g in range(3)]
    mx = lax.stop_gradient(jnp.maximum(jnp.maximum(ls[0], ls[1]), ls[2]))
    es = [jnp.exp(l - mx) for l in ls]
    inv = 1.0 / (es[0] + es[1] + es[2])
    return (jnp.concatenate([o[:, 256 * g:256 * (g + 1)] * (es[g] * inv) for g in range(3)], axis=1),)


def _fn_outnorm(o_f, o_r, z, gain):
    res = []
    for h in range(DN_HEADS):
        sl = slice(DN_HD * h, DN_HD * (h + 1))
        o = o_f[:, sl] + o_r[:, sl]
        res.append(o * lax.rsqrt(jnp.mean(o * o, axis=-1, keepdims=True) + EPS) * gain * _silu(z[:, sl]))
    return (jnp.concatenate(res, axis=1),)


def _loss_kernel(x, tgt, tb, name):
    t, d = x.shape

    def body(x_ref, t_ref, dx_ref, l_ref, acc_ref):
        i = pl.program_id(0)
        e = x_ref[...] - t_ref[...]
        dx_ref[...] = e * (1.0 / d)
        part = jnp.sum(e * e, axis=0, keepdims=True)

        @pl.when(i == 0)
        def _():
            acc_ref[...] = part

        @pl.when(i > 0)
        def _():
            acc_ref[...] += part

        @pl.when(i == t // tb - 1)
        def _():
            l_ref[...] = jnp.broadcast_to(jnp.sum(acc_ref[...], axis=-1, keepdims=True), (1, 128))

    return pl.pallas_call(
        body, name=name, grid=(t // tb,),
        in_specs=[pl.BlockSpec((tb, d), lambda i: (i, 0))] * 2,
        out_specs=[pl.BlockSpec((tb, d), lambda i: (i, 0)), pl.BlockSpec((1, 128), lambda i: (0, 0))],
        out_shape=[jax.ShapeDtypeStruct((t, d), F32), jax.ShapeDtypeStruct((1, 128), F32)],
        scratch_shapes=[pltpu.VMEM((1, d), F32)],
        compiler_params=_cparams(("arbitrary",)),
    )(x, tgt)


def _band_fn(l_sub, bq, i, q, kw, vw, bm):
    w = bq + 2 * BAND_HALF
    s = lax.dot_general((q * (ATT_HD ** -0.5)).astype(BF16), kw.astype(BF16), (((1,), (1,)), ((), ())),
                        preferred_element_type=F32) + bm
    kpos = i * bq - BAND_HALF + lax.broadcasted_iota(jnp.int32, (bq, w), 1)
    s = jnp.where((kpos >= 0) & (kpos < l_sub), s, NEG)
    m = lax.stop_gradient(jnp.max(s, axis=-1, keepdims=True))
    p = jnp.exp(s - m)
    den = jnp.sum(p, axis=-1, keepdims=True)
    o = jnp.dot(p.astype(BF16), vw.astype(BF16), preferred_element_type=F32) / den
    return o, jnp.broadcast_to(m + jnp.log(den), o.shape)


def _band_specs(dil, l_sub, bq):
    w = bq + 2 * BAND_HALF
    qs = pl.BlockSpec((None, bq, ATT_HD), lambda h, r, i: (h * dil + r, i, 0))
    ks = pl.BlockSpec((None, l_sub + 2 * BAND_HALF, ATT_HD), lambda h, r, i: (h * dil + r, 0, 0))
    bs = pl.BlockSpec((None, bq, w), lambda h, r, i: (h, 0, 0))
    return qs, ks, bs


def _band_fwd(q, k, v, bm, dil, l_sub, bq, name):
    w = bq + 2 * BAND_HALF
    qs, ks, bs = _band_specs(dil, l_sub, bq)

    def body(q_ref, k_ref, v_ref, bm_ref, o_ref, l_ref):
        i = pl.program_id(2)
        st = pl.multiple_of(i * bq, bq)
        o, lse = _band_fn(l_sub, bq, i, q_ref[...], k_ref[pl.ds(st, w), :], v_ref[pl.ds(st, w), :], bm_ref[...])
        o_ref[...] = o
        l_ref[...] = lse

    return pl.pallas_call(
        body, name=name, grid=(4, dil, l_sub // bq),
        in_specs=[qs, ks, ks, bs], out_specs=[qs, qs],
        out_shape=[jax.ShapeDtypeStruct(q.shape, F32)] * 2,
        compiler_params=_cparams(("parallel", "parallel", "arbitrary")),
    )(q, k, v, bm)


def _band_bwd(q, k, v, bm, do, dlse, dil, l_sub, bq, name):
    w = bq + 2 * BAND_HALF
    qs, ks, bs = _band_specs(dil, l_sub, bq)

    def body(q_ref, k_ref, v_ref, bm_ref, do_ref, dl_ref, dq_ref, dk_ref, dv_ref, dbm_ref):
        r, i = pl.program_id(1), pl.program_id(2)
        st = pl.multiple_of(i * bq, bq)
        _, vjp = jax.vjp(functools.partial(_band_fn, l_sub, bq, i),
                         q_ref[...], k_ref[pl.ds(st, w), :], v_ref[pl.ds(st, w), :], bm_ref[...])
        dq, dkw, dvw, dbm = vjp((do_ref[...], dl_ref[...]))
        dq_ref[...] = dq

        @pl.when(i == 0)
        def _():
            dk_ref[...] = jnp.zeros_like(dk_ref)
            dv_ref[...] = jnp.zeros_like(dv_ref)

        dk_ref[pl.ds(st, w), :] += dkw
        dv_ref[pl.ds(st, w), :] += dvw

        @pl.when((i == 0) & (r == 0))
        def _():
            dbm_ref[...] = dbm

        @pl.when((i > 0) | (r > 0))
        def _():
            dbm_ref[...] += dbm

    return pl.pallas_call(
        body, name=name, grid=(4, dil, l_sub // bq),
        in_specs=[qs, ks, ks, bs, qs, qs], out_specs=[qs, ks, ks, bs],
        out_shape=[jax.ShapeDtypeStruct(q.shape, F32), jax.ShapeDtypeStruct(k.shape, F32),
                   jax.ShapeDtypeStruct(k.shape, F32), jax.ShapeDtypeStruct(bm.shape, F32)],
        compiler_params=_cparams(("parallel", "arbitrary", "arbitrary")),
    )(q, k, v, bm, do, dlse)


def _t5_bucket(rel):
    half = REL_BUCKETS // 2
    max_exact = half // 2
    n = np.abs(rel)
    large = max_exact + (np.log(np.maximum(n, 1) / max_exact) / math.log(REL_MAX_DIST / max_exact)
                         * (half - max_exact)).astype(np.int64)
    large = np.minimum(large, half - 1)
    return ((rel > 0) * half + np.where(n < max_exact, n, large)).astype(np.int32)


def _bucket_onehot(dil):
    idx = _t5_bucket(np.arange(-BAND_HALF, BAND_HALF + 1) * dil)
    oh = np.zeros((2 * BAND_HALF + 1, REL_BUCKETS), np.float32)
    oh[np.arange(2 * BAND_HALF + 1), idx] = 1.0
    return oh


def _band_bias(rel_bias, gi, dil, bq):
    w = bq + 2 * BAND_HALF
    nb = 2 * BAND_HALF + 1
    bias = jnp.dot(jnp.asarray(_bucket_onehot(dil)), rel_bias[:, 4 * gi:4 * gi + 4], precision=HI)
    row = jnp.concatenate([bias.T, jnp.full((4, w + 1 - nb), NEG, F32)], axis=1)
    flat = jnp.tile(row, (1, bq))[:, :bq * w]
    return flat.reshape(4, bq, w)


def _relbias_grad(dbms, name):
    nb = 2 * BAND_HALF + 1
    bq = max(d.shape[1] for d in dbms)
    skew = []
    for dbm in dbms:
        bqg, w = dbm.shape[1], dbm.shape[2]
        flat = jnp.pad(dbm.reshape(4, bqg * w), ((0, 0), (0, bqg)))
        skew.append(jnp.pad(flat.reshape(4, bqg, w + 1)[:, :, :nb], ((0, 0), (0, bq - bqg), (0, 256 - nb))))
    sk = jnp.concatenate(skew, axis=0)
    oh = np.zeros((3, 256, 128), np.float32)
    for gi, (_, dil) in enumerate(DIL_GROUPS):
        oh[gi, :2 * BAND_HALF + 1, :REL_BUCKETS] = _bucket_onehot(dil)

    def body(s_ref, oh_ref, o_ref):
        col = jnp.sum(s_ref[...], axis=0, keepdims=True)
        o_ref[...] = jnp.dot(jnp.broadcast_to(col, (8, 256)), oh_ref[...], precision=HI, preferred_element_type=F32)

    out = pl.pallas_call(
        body, name=name, grid=(12,),
        in_specs=[pl.BlockSpec((None, bq, 256), lambda n: (n, 0, 0)),
                  pl.BlockSpec((None, 256, 128), lambda n: (n // 4, 0, 0))],
        out_specs=pl.BlockSpec((None, 8, 128), lambda n: (n, 0, 0)),
        out_shape=jax.ShapeDtypeStruct((12, 8, 128), F32),
        compiler_params=_cparams(("parallel",)),
    )(sk, jnp.asarray(oh))
    return out[:, 0, :REL_BUCKETS].T


def _mem_fn(q, k, v):
    s = lax.dot_general((q * (ATT_HD ** -0.5)).astype(BF16), k.astype(BF16), (((1,), (1,)), ((), ())),
                        preferred_element_type=F32)
    m = lax.stop_gradient(jnp.max(s, axis=-1, keepdims=True))
    p = jnp.exp(s - m)
    p = p / jnp.sum(p, axis=-1, keepdims=True)
    return jnp.dot(p.astype(BF16), v.astype(BF16), preferred_element_type=F32)


def _mem_specs(tb, ml):
    qs = pl.BlockSpec((None, tb, ATT_HD), lambda h, i: (h, i, 0))
    ks = pl.BlockSpec((None, ml, ATT_HD), lambda h, i: (h, 0, 0))
    return qs, ks


def _mem_fwd(q, k, v, tb, name):
    qs, ks = _mem_specs(tb, k.shape[1])

    def body(q_ref, k_ref, v_ref, o_ref):
        o_ref[...] = _mem_fn(q_ref[...], k_ref[...], v_ref[...])

    return pl.pallas_call(
        body, name=name, grid=(MEM_HEADS, q.shape[1] // tb),
        in_specs=[qs, ks, ks], out_specs=qs, out_shape=jax.ShapeDtypeStruct(q.shape, F32),
        compiler_params=_cparams(("parallel", "parallel")),
    )(q, k, v)


def _mem_bwd(q, k, v, do, tb, name):
    qs, ks = _mem_specs(tb, k.shape[1])

    def body(q_ref, k_ref, v_ref, do_ref, dq_ref, dk_ref, dv_ref):
        i = pl.program_id(1)
        _, vjp = jax.vjp(_mem_fn, q_ref[...], k_ref[...], v_ref[...])
        dq, dk, dv = vjp(do_ref[...])
        dq_ref[...] = dq

        @pl.when(i == 0)
        def _():
            dk_ref[...] = dk
            dv_ref[...] = dv

        @pl.when(i > 0)
        def _():
            dk_ref[...] += dk
            dv_ref[...] += dv

    return pl.pallas_call(
        body, name=name, grid=(MEM_HEADS, q.shape[1] // tb),
        in_specs=[qs, ks, ks, qs], out_specs=[qs, ks, ks],
        out_shape=[jax.ShapeDtypeStruct(q.shape, F32), jax.ShapeDtypeStruct(k.shape, F32),
                   jax.ShapeDtypeStruct(k.shape, F32)],
        compiler_params=_cparams(("parallel", "arbitrary")),
    )(q, k, v, do)


CONV_PAD = 8


def _conv_post(kind, acc):
    s = _silu(acc)
    if kind == 2:
        return s
    scale = DN_HD ** -0.5 if kind == 0 else 1.0
    return s * lax.rsqrt(jnp.sum(s * s, axis=-1, keepdims=True) + EPS) * scale


def _conv_acc(xp_ref, w, r0, rt):
    acc = None
    for i in range(DN_CONV):
        term = w[i:i + 1, :] * xp_ref[pl.ds(CONV_PAD + r0 + i - DN_CONV // 2, rt), :]
        acc = term if acc is None else acc + term
    return acc


def _conv_fwd(xp, w8, kind, rt, name):
    t = xp.shape[0] - 2 * CONV_PAD

    def body(xp_ref, w_ref, o_ref):
        w = w_ref[...]
        for r in range(t // rt):
            o_ref[pl.ds(r * rt, rt), :] = _conv_post(kind, _conv_acc(xp_ref, w, r * rt, rt))

    return pl.pallas_call(
        body, name=name, grid=(DN_HEADS,),
        in_specs=[pl.BlockSpec((t + 2 * CONV_PAD, DN_HD), lambda j: (0, 6 * kind + j)),
                  pl.BlockSpec((8, DN_HD), lambda j: (0, 6 * kind + j))],
        out_specs=pl.BlockSpec((t, DN_HD), lambda j: (0, j)),
        out_shape=jax.ShapeDtypeStruct((t, TOK_W), F32),
        compiler_params=_cparams(("parallel",)),
    )(xp, w8)


def _conv_bwd(xp, w8, d_f, d_r, kind, rt, name):
    t = xp.shape[0] - 2 * CONV_PAD

    def body(xp_ref, w_ref, df_ref, dr_ref, dx_ref, dw_ref, dpad_ref):
        w = w_ref[...]
        zero = jnp.zeros((CONV_PAD, DN_HD), F32)
        dpad_ref[pl.ds(0, CONV_PAD), :] = zero
        dpad_ref[pl.ds(CONV_PAD + t, CONV_PAD), :] = zero
        dw = [jnp.zeros((1, DN_HD), F32) for _ in range(DN_CONV)]
        for r in range(t // rt):
            rows = pl.ds(r * rt, rt)
            acc = _conv_acc(xp_ref, w, r * rt, rt)
            _, vjp = jax.vjp(functools.partial(_conv_post, kind), acc)
            (dacc,) = vjp(df_ref[rows, :] + dr_ref[rows, :])
            dpad_ref[pl.ds(CONV_PAD + r * rt, rt), :] = dacc
            for i in range(DN_CONV):
                xs = xp_ref[pl.ds(CONV_PAD + r * rt + i - DN_CONV // 2, rt), :]
                dw[i] = dw[i] + jnp.sum(dacc * xs, axis=0, keepdims=True)
        dw_ref[...] = jnp.concatenate(dw + [jnp.zeros((8 - DN_CONV, DN_HD), F32)], axis=0)
        for r in range(t // rt):
            acc = None
            for i in range(DN_CONV):
                term = w[i:i + 1, :] * dpad_ref[pl.ds(CONV_PAD + r * rt - i + DN_CONV // 2, rt), :]
                acc = term if acc is None else acc + term
            dx_ref[pl.ds(r * rt, rt), :] = acc

    return pl.pallas_call(
        body, name=name, grid=(DN_HEADS,),
        in_specs=[pl.BlockSpec((t + 2 * CONV_PAD, DN_HD), lambda j: (0, 6 * kind + j)),
                  pl.BlockSpec((8, DN_HD), lambda j: (0, 6 * kind + j)),
                  pl.BlockSpec((t, DN_HD), lambda j: (0, j)),
                  pl.BlockSpec((t, DN_HD), lambda j: (0, j))],
        out_specs=[pl.BlockSpec((t, DN_HD), lambda j: (0, j)), pl.BlockSpec((8, DN_HD), lambda j: (0, j))],
        out_shape=[jax.ShapeDtypeStruct((t, TOK_W), F32), jax.ShapeDtypeStruct((8, TOK_W), F32)],
        scratch_shapes=[pltpu.VMEM((t + 2 * CONV_PAD, DN_HD), F32)],
        compiler_params=_cparams(("parallel",)),
    )(xp, w8, d_f, d_r)


def _softplus(x):
    e = jnp.exp(-jnp.abs(x))
    return jnp.maximum(x, 0.0) + jnp.where(e < 1e-4, e - 0.5 * e * e, jnp.log(1.0 + e))


_NN = (((1,), (0,)), ((), ()))


def _dot(a, b, dims=_NN):
    return lax.dot_general(a.astype(BF16), b.astype(BF16), dims, preferred_element_type=F32)


def _hi_lo(x):
    hi = x.astype(BF16)
    return hi, (x - hi.astype(F32)).astype(BF16)


_NT = (((1,), (1,)), ((), ()))
_TN = (((0,), (0,)), ((), ()))


def _mask_dot(mask_bf16, x, dims):
    x1 = x.astype(BF16)
    r = x - x1.astype(F32)
    x2, x3 = _hi_lo(r)
    d = functools.partial(lax.dot_general, dimension_numbers=dims, preferred_element_type=F32)
    return d(mask_bf16, x1) + d(mask_bf16, x2) + d(mask_bf16, x3)


@jax.custom_vjp
def _dot_mask(mask_bf16, x):
    return _mask_dot(mask_bf16, x, _NN)


def _dot_mask_fwd(mask_bf16, x):
    return _mask_dot(mask_bf16, x, _NN), mask_bf16


def _dot_mask_bwd(mask_bf16, ct):
    return jnp.zeros_like(mask_bf16), _mask_dot(mask_bf16, ct, _TN)


_dot_mask.defvjp(_dot_mask_fwd, _dot_mask_bwd)


def _dot3_raw(a, b, dims):
    a1, a2 = _hi_lo(a)
    b1, b2 = _hi_lo(b)
    d = functools.partial(lax.dot_general, dimension_numbers=dims, preferred_element_type=F32)
    return d(a1, b1) + d(a1, b2) + d(a2, b1)


@jax.custom_vjp
def _dot3(a, b):
    return _dot3_raw(a, b, _NN)


def _dot3_fwd(a, b):
    return _dot3_raw(a, b, _NN), (a, b)


def _dot3_bwd(res, ct):
    a, b = res
    return _dot3_raw(ct, b, _NT), _dot3_raw(a, ct, _TN)


_dot3.defvjp(_dot3_fwd, _dot3_bwd)


def _dn_chunk(rev, q, k, v, al, be, alc, a_row, dt_row, a_rowc, dt_rowc, s):
    c = DN_CHUNK
    row = lax.broadcasted_iota(jnp.int32, (c, c), 0)
    col = lax.broadcasted_iota(jnp.int32, (c, c), 1)
    incl = (row <= col) if rev else (row >= col)
    strict = (row < col) if rev else (row > col)
    incl_b = incl.astype(BF16)
    nt = (((1,), (1,)), ((), ()))
    tn = (((0,), (0,)), ((), ()))

    g = -jnp.exp(a_row) * _softplus(al + dt_row)
    beta = _sigmoid(be)
    g_c = -jnp.exp(a_rowc) * _softplus(alc + dt_rowc)
    gc = _dot_mask(incl_b, g)
    gcc = _dot_mask(incl_b, g_c)
    decay = jnp.exp(jnp.where(incl, gcc - gcc.T, NEG))
    kb = k * beta
    lmat = jnp.where(strict, _dot(kb, k, nt) * decay, 0.0)
    rhs = jnp.concatenate([v * beta, kb * jnp.exp(gc)], axis=1)
    xp = -lmat
    sol = rhs + _dot3(xp, rhs)
    for _ in range(5):
        xp = _dot3(xp, xp)
        sol = sol + _dot3(xp, sol)
    u, w = sol[:, :DN_HD], sol[:, DN_HD:]
    intra = jnp.where(incl, _dot(q, k, nt) * decay, 0.0)
    v_new = u - _dot(w, s)
    out = _dot(q * jnp.exp(gc), s) + _dot(intra, v_new)
    g_last = jnp.sum(g, axis=0, keepdims=True)
    s_new = s * jnp.exp(g_last) + _dot(k * jnp.exp(g_last - gc), v_new, tn)
    return out, s_new


DN_HG = 3


def _dn_views(nc, bwd):
    c, hg = DN_CHUNK, DN_HG
    if bwd:
        f_blk = lambda s: nc - 1 - s
        r_blk = lambda s: s
        st_blk = lambda s: nc - 1 - s
    else:
        f_blk = lambda s: s
        r_blk = lambda s: nc - 1 - s
        st_blk = lambda s: s
    tok_f = pl.BlockSpec((c, hg * DN_HD), lambda g, s: (f_blk(s), g))
    tok_r = pl.BlockSpec((c, hg * DN_HD), lambda g, s: (r_blk(s), g))
    gate_f = pl.BlockSpec((None, hg, c, DN_HD), lambda g, s: (0, g, f_blk(s), 0))
    gate_r = pl.BlockSpec((None, hg, c, DN_HD), lambda g, s: (1, g, r_blk(s), 0))
    par = pl.BlockSpec((2, hg, 1, DN_HD), lambda g, s: (0, g, 0, 0))
    state = pl.BlockSpec((2, hg, None, DN_HD, DN_HD), lambda g, s: (0, g, st_blk(s), 0, 0))
    return tok_f, tok_r, gate_f, gate_r, par, state


def _dn_fwd(q, k, v, al, be, a_rows, dt_rows, name):
    t = q.shape[0]
    c, hg = DN_CHUNK, DN_HG
    nc = t // c
    tok_f, tok_r, gate_f, gate_r, par, state = _dn_views(nc, False)

    def body(qf, kf, vf, qr, kr, vr, alf, bef, alr, ber, a_ref, dt_ref, of_ref, or_ref, st_ref, s_ref):
        @pl.when(pl.program_id(1) == 0)
        def _():
            s_ref[...] = jnp.zeros_like(s_ref)

        for d in range(2):
            qx, kx, vx, alx, bex, ox = (qf, kf, vf, alf, bef, of_ref) if d == 0 else (qr, kr, vr, alr, ber, or_ref)
            for h in range(hg):
                sl = slice(DN_HD * h, DN_HD * (h + 1))
                s = s_ref[d, h]
                st_ref[d, h] = s
                out, s_new = _dn_chunk(d == 1, qx[:, sl], kx[:, sl], vx[:, sl], alx[h], bex[h],
                                       alx[h, :, 0:c], a_ref[d, h], dt_ref[d, h],
                                       a_ref[d, h, :, 0:c], dt_ref[d, h, :, 0:c], s)
                ox[:, sl] = out
                s_ref[d, h] = s_new

    return pl.pallas_call(
        body, name=name, grid=(DN_HEADS // hg, nc),
        in_specs=[tok_f] * 3 + [tok_r] * 3 + [gate_f, gate_f, gate_r, gate_r, par, par],
        out_specs=[tok_f, tok_r, state],
        out_shape=[jax.ShapeDtypeStruct((t, TOK_W), F32)] * 2
        + [jax.ShapeDtypeStruct((2, DN_HEADS, nc, DN_HD, DN_HD), F32)],
        scratch_shapes=[pltpu.VMEM((2, hg, DN_HD, DN_HD), F32)],
        compiler_params=_cparams(("parallel", "arbitrary")),
    )(q, k, v, q, k, v, al, be, al, be, a_rows, dt_rows)


def _dn_bwd(q, k, v, al, be, a_rows, dt_rows, states, do, name):
    t = q.shape[0]
    c, hg = DN_CHUNK, DN_HG
    nc = t // c
    tok_f, tok_r, gate_f, gate_r, par, state = _dn_views(nc, True)
    gout_f = pl.BlockSpec((hg, c, DN_HD), lambda g, s: (g, nc - 1 - s, 0))
    gout_r = pl.BlockSpec((hg, c, DN_HD), lambda g, s: (g, s, 0))

    def body(qf, kf, vf, qr, kr, vr, alf, bef, alr, ber, a_ref, dt_ref, st_ref, dof, dor,
             dqf, dkf, dvf, dqr, dkr, dvr, dalf, dbef, dalr, dber, da_ref, ddt_ref, ds_ref):
        first = pl.program_id(1) == 0

        @pl.when(first)
        def _():
            ds_ref[...] = jnp.zeros_like(ds_ref)
            da_ref[...] = jnp.zeros_like(da_ref)
            ddt_ref[...] = jnp.zeros_like(ddt_ref)

        def lanes(x):
            return jnp.sum(x, axis=-1, keepdims=True)

        for d in range(2):
            qx, kx, vx, alx, bex, dox = (qf, kf, vf, alf, bef, dof) if d == 0 else (qr, kr, vr, alr, ber, dor)
            dqx, dkx, dvx, dalx, dbex = (dqf, dkf, dvf, dalf, dbef) if d == 0 else (dqr, dkr, dvr, dalr, dber)
            for h in range(hg):
                sl = slice(DN_HD * h, DN_HD * (h + 1))
                _, vjp = jax.vjp(functools.partial(_dn_chunk, d == 1), qx[:, sl], kx[:, sl], vx[:, sl],
                                 alx[h], bex[h], alx[h, :, 0:c], a_ref[d, h], dt_ref[d, h],
                                 a_ref[d, h, :, 0:c], dt_ref[d, h, :, 0:c], st_ref[d, h])
                dq, dk, dv, dal, dbe, dalc, da, ddt, dac, ddtc, ds = vjp((dox[:, sl], ds_ref[d, h]))
                dqx[:, sl] = dq
                dkx[:, sl] = dk
                dvx[:, sl] = dv
                dalx[h] = jnp.broadcast_to(lanes(dal) + lanes(dalc), (c, DN_HD))
                dbex[h] = jnp.broadcast_to(lanes(dbe), (c, DN_HD))
                da_ref[d, h] += jnp.broadcast_to(lanes(da) + lanes(dac), (1, DN_HD))
                ddt_ref[d, h] += jnp.broadcast_to(lanes(ddt) + lanes(ddtc), (1, DN_HD))
                ds_ref[d, h] = ds

    tok = jax.ShapeDtypeStruct((t, TOK_W), F32)
    gate = jax.ShapeDtypeStruct((DN_HEADS, t, DN_HD), F32)
    parsh = jax.ShapeDtypeStruct((2, DN_HEADS, 1, DN_HD), F32)
    res = pl.pallas_call(
        body, name=name, grid=(DN_HEADS // hg, nc),
        in_specs=[tok_f] * 3 + [tok_r] * 3 + [gate_f, gate_f, gate_r, gate_r, par, par, state, tok_f, tok_r],
        out_specs=[tok_f] * 3 + [tok_r] * 3 + [gout_f, gout_f, gout_r, gout_r, par, par],
        out_shape=[tok] * 6 + [gate] * 4 + [parsh] * 2,
        scratch_shapes=[pltpu.VMEM((2, hg, DN_HD, DN_HD), F32)],
        compiler_params=_cparams(("parallel", "arbitrary")),
    )(q, k, v, q, k, v, al, be, al, be, a_rows, dt_rows, states, do, do)
    dqf, dkf, dvf, dqr, dkr, dvr, dalf, dbef, dalr, dber, da, ddt = res
    return (dqf, dkf, dvf), (dqr, dkr, dvr), jnp.stack([dalf, dalr]), jnp.stack([dbef, dber]), da, ddt


BAND_BQ = 256
ROW_TB = 256
MEM_TB = 512
CONV_RT = 512


def _to_sub(x, dil):
    l = x.shape[0] // dil
    return x.reshape(l, dil, 4, ATT_HD).transpose(2, 1, 0, 3).reshape(4 * dil, l, ATT_HD)


def _from_sub(x, dil):
    l = x.shape[1]
    return x.reshape(4, dil, l, ATT_HD).transpose(2, 1, 0, 3).reshape(l * dil, 4 * ATT_HD)


def _heads_major(x):
    return x.reshape(x.shape[0], MEM_HEADS, ATT_HD).transpose(1, 0, 2)


def _heads_minor(x):
    return x.transpose(1, 0, 2).reshape(x.shape[1], MEM_HEADS * ATT_HD)


def _mem_kv_fwd(mem, gain, w_kv, li):
    (memn,) = _rowwise(_fn_pre, [mem], [gain], [(D, BF16)], mem.shape[0], f"memnorm_fwd{li}")
    kv = _matmul(memn, w_kv, "nn", F32, f"memkv_fwd{li}")
    return _heads_major(kv[:, :MEM_W]), _heads_major(kv[:, MEM_W:]), memn


def _mem_kv_bwd(mem, gain, w_kv, memn, dkm, dvm, li):
    dkv = jnp.concatenate([_heads_minor(dkm), _heads_minor(dvm)], axis=1).astype(BF16)
    dw = _matmul(memn, dkv, "tn", BF16, f"memkv_dw{li}")
    dmemn = _matmul(dkv, w_kv, "nt", F32, f"memkv_dx{li}")
    _, (dgain,) = _rowwise_bwd(_fn_pre, [mem], [gain], [dmemn], [None], mem.shape[0], f"memnorm_bwd{li}")
    return dw, dgain


def _attn_mixer_fwd(p, rel_bias, km, vm):
    t = p.shape[0]
    saved, outs, lses = [], [], []
    for gi, (_, dil) in enumerate(DIL_GROUPS):
        l_sub = t // dil
        bq = min(BAND_BQ, l_sub)
        q = _to_sub(p[:, 256 * gi:256 * (gi + 1)], dil)
        pad = ((0, 0), (BAND_HALF, BAND_HALF), (0, 0))
        k = jnp.pad(_to_sub(p[:, TOK_W + 256 * gi:TOK_W + 256 * (gi + 1)], dil), pad)
        v = jnp.pad(_to_sub(p[:, 2 * TOK_W + 256 * gi:2 * TOK_W + 256 * (gi + 1)], dil), pad)
        bm = _band_bias(rel_bias, gi, dil, bq)
        o, lse = _band_fwd(q, k, v, bm, dil, l_sub, bq, f"band_fwd{gi}")
        outs.append(_from_sub(o, dil))
        lses.append(_from_sub(lse, dil))
        saved.append((q, k, v, bm))
    o_all = jnp.concatenate(outs, axis=1)
    lse_all = jnp.concatenate(lses, axis=1)
    (mixed,) = _rowwise(_fn_combine, [o_all, lse_all], [], [(TOK_W, BF16)], ROW_TB, "combine_fwd")
    qm = _heads_major(p[:, 3 * TOK_W:])
    memo = _mem_fwd(qm, km, vm, min(MEM_TB, t), "mem_fwd0")
    cat = jnp.concatenate([mixed, _heads_minor(memo).astype(BF16)], axis=1)
    return cat, (saved, o_all, lse_all, qm)


def _attn_mixer_bwd(dcat, res, km, vm):
    saved, o_all, lse_all, qm = res
    t = dcat.shape[0]
    (do_all, dlse_all), _ = _rowwise_bwd(_fn_combine, [o_all, lse_all], [], [dcat[:, :TOK_W]], [F32, F32],
                                         ROW_TB, "combine_bwd")
    dqs, dks, dvs, dbms = [], [], [], []
    for gi, (_, dil) in enumerate(DIL_GROUPS):
        l_sub = t // dil
        bq = min(BAND_BQ, l_sub)
        q, k, v, bm = saved[gi]
        do = _to_sub(do_all[:, 256 * gi:256 * (gi + 1)], dil)
        dl = _to_sub(dlse_all[:, 256 * gi:256 * (gi + 1)], dil)
        dq, dk, dv, dbm = _band_bwd(q, k, v, bm, do, dl, dil, l_sub, bq, f"band_bwd{gi}")
        dqs.append(_from_sub(dq, dil))
        dks.append(_from_sub(dk[:, BAND_HALF:-BAND_HALF], dil))
        dvs.append(_from_sub(dv[:, BAND_HALF:-BAND_HALF], dil))
        dbms.append(dbm)
    dqm, dkm, dvm = _mem_bwd(qm, km, vm, _heads_major(dcat[:, TOK_W:]), min(MEM_TB, t), "mem_bwd0")
    dp = jnp.concatenate(dqs + dks + dvs + [_heads_minor(dqm)], axis=1).astype(BF16)
    return dp, _relbias_grad(dbms, "relbias_grad"), dkm, dvm


def _dn_mixer_fwd(p, conv_w, a_log, dt_bias, out_norm, km, vm):
    t = p.shape[0]
    rt = min(CONV_RT, t)
    xp = jnp.pad(p[:, :3 * TOK_W], ((CONV_PAD, CONV_PAD), (0, 0)))
    w8 = jnp.pad(conv_w, ((0, 8 - DN_CONV), (0, 0)))
    q = _conv_fwd(xp, w8, 0, rt, "conv_fwd_q")
    k = _conv_fwd(xp, w8, 1, rt, "conv_fwd_k")
    v = _conv_fwd(xp, w8, 2, rt, "conv_fwd_v")
    gate = p[:, 4 * TOK_W:4 * TOK_W + 4 * DN_HEADS].reshape(t, 2, 2, DN_HEADS)
    bshape = (2, DN_HEADS, t, DN_HD)
    al = jnp.broadcast_to(gate[:, :, 0, :].transpose(1, 2, 0)[..., None], bshape)
    be = jnp.broadcast_to(gate[:, :, 1, :].transpose(1, 2, 0)[..., None], bshape)
    a_rows = jnp.broadcast_to(a_log[:, :, None, None], (2, DN_HEADS, 1, DN_HD))
    dt_rows = jnp.broadcast_to(dt_bias[:, :, None, None], (2, DN_HEADS, 1, DN_HD))
    o_f, o_r, states = _dn_fwd(q, k, v, al, be, a_rows, dt_rows, "dn_fwd")
    z = p[:, 3 * TOK_W:4 * TOK_W]
    gain = out_norm.reshape(1, DN_HD)
    (og,) = _rowwise(_fn_outnorm, [o_f, o_r, z], [gain], [(TOK_W, BF16)], ROW_TB, "outnorm_fwd")
    qm = _heads_major(p[:, 4 * TOK_W + 4 * DN_HEADS:DN_IN])
    memo = _mem_fwd(qm, km, vm, min(MEM_TB, t), "mem_fwd1")
    cat = jnp.concatenate([og, _heads_minor(memo).astype(BF16)], axis=1)
    return cat, (xp, w8, q, k, v, al, be, a_rows, dt_rows, o_f, o_r, states, z, gain, qm)


def _dn_mixer_bwd(dcat, res, km, vm):
    xp, w8, q, k, v, al, be, a_rows, dt_rows, o_f, o_r, states, z, gain, qm = res
    t = dcat.shape[0]
    rt = min(CONV_RT, t)
    (do, dz), (dgain,) = _rowwise_bwd(_fn_outnorm, [o_f, o_r, z], [gain], [dcat[:, :TOK_W]], [F32, None, F32],
                                      ROW_TB, "outnorm_bwd")
    d_f, d_r, dal, dbe, da, ddt = _dn_bwd(q, k, v, al, be, a_rows, dt_rows, states, do, "dn_bwd")
    dxs, dws = [], []
    for kind, nm in enumerate("qkv"):
        dx, dw = _conv_bwd(xp, w8, d_f[kind], d_r[kind], kind, rt, f"conv_bwd_{nm}")
        dxs.append(dx)
        dws.append(dw)
    dconv = jnp.concatenate(dws, axis=1)[:DN_CONV]
    dgate = jnp.stack([dal[..., 0], dbe[..., 0]], axis=1).transpose(3, 0, 1, 2).reshape(t, 4 * DN_HEADS)
    dqm, dkm, dvm = _mem_bwd(qm, km, vm, _heads_major(dcat[:, TOK_W:]), min(MEM_TB, t), "mem_bwd1")
    dp = jnp.concatenate(dxs + [dz, dgate, _heads_minor(dqm), jnp.zeros((t, DN_IN_PAD - DN_IN), F32)],
                         axis=1).astype(BF16)
    return dp, dconv, da[:, :, 0, 0], ddt[:, :, 0, 0], dgain.reshape(DN_HD), dkm, dvm


SWI_TB = 128


def _ffn_fwd(h, w_gu_t, w_d, li):
    gu = _matmul(h, w_gu_t, "nt", F32, f"ffn_gu{li}")
    (a,) = _rowwise(_fn_swiglu, [gu], [], [(D_FF, BF16)], SWI_TB, f"swiglu_fwd{li}")
    return _matmul(a, w_d, "nn", F32, f"ffn_down{li}"), gu, a


def _ffn_bwd(df, h, w_gu_t, w_d, gu, a, li):
    da = _matmul(df, w_d, "nt", F32, f"ffn_down_dx{li}")
    dwd = _matmul(a, df, "tn", BF16, f"ffn_down_dw{li}")
    (dgu,), _ = _rowwise_bwd(_fn_swiglu, [gu], [], [da], [BF16], SWI_TB, f"swiglu_bwd{li}")
    dh = _matmul(dgu, w_gu_t, "nn", F32, f"ffn_gu_dx{li}")
    dwgu_t = _matmul(dgu, h, "tn", BF16, f"ffn_gu_dw{li}")
    return dh, dwgu_t, dwd


def _fn_first(x, g):
    return x, _rms(x, g)


def _me_xyc():
    return lax.axis_index("x"), lax.axis_index("y"), lax.axis_index("c")


def _flip(coords, k):
    x, y, c = coords
    return (1 - x if k & 4 else x, 1 - y if k & 2 else y, 1 - c if k & 1 else c)


def _index(coords):
    x, y, c = coords
    return 4 * x + 2 * y + c


def _window(ref, axis, size, d):
    idx = [slice(None)] * len(ref.shape)
    idx[axis] = pl.ds(pl.multiple_of(d * size, size), size)
    return ref.at[tuple(idx)]


def _comm_call(body, n, ins, out_shapes, name):
    hbm = pl.BlockSpec(memory_space=pl.ANY)
    return pl.pallas_call(
        body, name=name, in_specs=[hbm] * n, out_specs=[hbm] * n, out_shape=out_shapes,
        scratch_shapes=[pltpu.SemaphoreType.DMA((N_DEV - 1, n)), pltpu.SemaphoreType.DMA((N_DEV - 1, n)),
                        pltpu.SemaphoreType.DMA((n,))],
    )(*ins)


def _run_exchange(n, local, remote, send_sems, recv_sems):
    me = _me_xyc()
    locs = [local(p) for p in range(n)]
    for cp in locs:
        cp.start()
    sends = [remote(k, p, me, _flip(me, k)) for k in range(1, N_DEV) for p in range(n)]
    for cp in sends:
        cp.start()
    for k in range(1, N_DEV):
        for p in range(n):
            remote(k, p, _flip(me, k), me).wait_recv()
    for cp in sends:
        cp.wait_send()
    for cp in locs:
        cp.wait()


def _all_gather(shards, axes, name):
    n = len(shards)
    sizes = [s.shape[a] for s, a in zip(shards, axes)]

    def body(*refs):
        ins, outs = refs[:n], refs[n:2 * n]
        send_sems, recv_sems, loc_sems = refs[2 * n:]
        me = _me_xyc()

        def local(p):
            return pltpu.make_async_copy(ins[p], _window(outs[p], axes[p], sizes[p], _index(me)), loc_sems.at[p])

        def remote(k, p, owner, to):
            return pltpu.make_async_remote_copy(
                src_ref=ins[p], dst_ref=_window(outs[p], axes[p], sizes[p], _index(owner)),
                send_sem=send_sems.at[k - 1, p], recv_sem=recv_sems.at[k - 1, p], device_id=to, device_id_type=MESH)

        _run_exchange(n, local, remote, send_sems, recv_sems)

    def full(s, a):
        return s.shape[:a] + (N_DEV * s.shape[a],) + s.shape[a + 1:]

    return _comm_call(body, n, shards, [jax.ShapeDtypeStruct(full(s, a), s.dtype) for s, a in zip(shards, axes)], name)


def _exchange(fulls, axes, name):
    n = len(fulls)
    sizes = [None if a is None else f.shape[a] // N_DEV for f, a in zip(fulls, axes)]

    def part_shape(f, a):
        return f.shape if a is None else f.shape[:a] + (f.shape[a] // N_DEV,) + f.shape[a + 1:]

    def body(*refs):
        ins, outs = refs[:n], refs[n:2 * n]
        send_sems, recv_sems, loc_sems = refs[2 * n:]
        me = _me_xyc()

        def src(p, to):
            return ins[p] if axes[p] is None else _window(ins[p], axes[p], sizes[p], _index(to))

        def local(p):
            return pltpu.make_async_copy(src(p, me), outs[p].at[_index(me)], loc_sems.at[p])

        def remote(k, p, sender, to):
            return pltpu.make_async_remote_copy(
                src_ref=src(p, to), dst_ref=outs[p].at[_index(sender)],
                send_sem=send_sems.at[k - 1, p], recv_sem=recv_sems.at[k - 1, p], device_id=to, device_id_type=MESH)

        _run_exchange(n, local, remote, send_sems, recv_sems)

    return _comm_call(body, n, fulls,
                      [jax.ShapeDtypeStruct((N_DEV,) + part_shape(f, a), f.dtype) for f, a in zip(fulls, axes)], name)


def _adam_math(g, w, m, v):
    m = ADAM_B1 * m + (1.0 - ADAM_B1) * g
    v = ADAM_B2 * v + (1.0 - ADAM_B2) * (g * g)
    m_hat = m / (1.0 - ADAM_B1 ** ADAM_STEP)
    v_hat = v / (1.0 - ADAM_B2 ** ADAM_STEP)
    delta = -ADAM_LR * (m_hat / (jnp.sqrt(v_hat) + ADAM_EPS) + ADAM_WD * w)
    return delta, m, v


def _sum_slabs(r_ref):
    g = r_ref[0].astype(F32)
    for s in range(1, N_DEV):
        g = g + r_ref[s].astype(F32)
    return g


def _adamw_reduce(recv, w, m, v, tb, name):
    r, c = w.shape

    def body(r_ref, w_ref, m_ref, v_ref, g_ref, d_ref, nm_ref, nv_ref):
        g = _sum_slabs(r_ref)
        g_ref[...] = g
        d_ref[...], nm_ref[...], nv_ref[...] = _adam_math(g, w_ref[...], m_ref[...], v_ref[...])

    blk = pl.BlockSpec((tb, c), lambda i: (i, 0))
    return pl.pallas_call(
        body, name=name, grid=(r // tb,),
        in_specs=[pl.BlockSpec((N_DEV, tb, c), lambda i: (0, i, 0)), blk, blk, blk],
        out_specs=[blk] * 4, out_shape=[jax.ShapeDtypeStruct((r, c), F32)] * 4,
        compiler_params=_cparams(("parallel",)),
    )(recv, w, m, v)


def _reduce8(recv, tb, name):
    r, c = recv.shape[1:]

    def body(r_ref, g_ref):
        g_ref[...] = _sum_slabs(r_ref)

    return pl.pallas_call(
        body, name=name, grid=(r // tb,),
        in_specs=[pl.BlockSpec((N_DEV, tb, c), lambda i: (0, i, 0))],
        out_specs=pl.BlockSpec((tb, c), lambda i: (i, 0)), out_shape=jax.ShapeDtypeStruct((r, c), F32),
        compiler_params=_cparams(("parallel",)),
    )(recv)


def _adamw(g, w, m, v, tb, name):
    r, c = w.shape

    def body(g_ref, w_ref, m_ref, v_ref, d_ref, nm_ref, nv_ref):
        d_ref[...], nm_ref[...], nv_ref[...] = _adam_math(g_ref[...], w_ref[...], m_ref[...], v_ref[...])

    blk = pl.BlockSpec((tb, c), lambda i: (i, 0))
    return pl.pallas_call(
        body, name=name, grid=(r // tb,), in_specs=[blk] * 4, out_specs=[blk] * 3,
        out_shape=[jax.ShapeDtypeStruct((r, c), F32)] * 3, compiler_params=_cparams(("parallel",)),
    )(g, w, m, v)


DN_IN_SHARD = DN_IN // N_DEV
DN_IN_SHARD_PAD = 432
CONV_SHARD = (1, DN_CONV, 288)


def _pack_small(arrs, rows):
    flat = jnp.concatenate([a.astype(F32).reshape(-1) for a in arrs])
    return jnp.pad(flat, (0, rows * PACK_C - flat.size)).reshape(rows, PACK_C)


def _unpack_small(packed, shapes):
    flat, out, off = packed.reshape(-1), [], 0
    for shp in shapes:
        n = int(np.prod(shp))
        out.append(flat[off:off + n].reshape(shp))
        off += n
    return out


def kernel(x, mem, rel_bias, att_w_in, att_w_out, dn_w_in, dn_conv, dn_a_log, dn_dt_bias, dn_out_norm, dn_w_out, mem_norm, mem_w_kv, norm_mix_pre, norm_mix_post, norm_ffn_pre, norm_ffn_post, ffn_w_gate_up, ffn_w_down, loss_target, m_rel_bias, m_att_w_in, m_att_w_out, m_dn_w_in, m_dn_conv, m_dn_a_log, m_dn_dt_bias, m_dn_out_norm, m_dn_w_out, m_mem_norm, m_mem_w_kv, m_norm_mix_pre, m_norm_mix_post, m_norm_ffn_pre, m_norm_ffn_post, m_ffn_w_gate_up, m_ffn_w_down, v_rel_bias, v_att_w_in, v_att_w_out, v_dn_w_in, v_dn_conv, v_dn_a_log, v_dn_dt_bias, v_dn_out_norm, v_dn_w_out, v_mem_norm, v_mem_w_kv, v_norm_mix_pre, v_norm_mix_post, v_norm_ffn_pre, v_norm_ffn_post, v_ffn_w_gate_up, v_ffn_w_down):
    x0, mem0, tgt = x[0], mem[0], loss_target[0]
    t = x0.shape[0]
    axes = ("x", "y", "c")

    def t_shard(w):
        return jnp.swapaxes(w, 1, 2).astype(BF16)

    dn_in_pad = ((0, 0), (0, DN_IN_SHARD_PAD - DN_IN_SHARD), (0, 0))
    shards = [t_shard(att_w_in), att_w_out.astype(BF16), jnp.pad(t_shard(dn_w_in), dn_in_pad),
              dn_w_out.astype(BF16), mem_w_kv.astype(BF16), t_shard(ffn_w_gate_up), ffn_w_down.astype(BF16), dn_conv]
    w_att_in_t, w_att_out, w_dn_in_g, w_dn_out, w_kv, w_gu_t, w_down, conv_g = _all_gather(
        shards, [1, 1, 1, 1, 1, 1, 1, 0], "allgather_weights")
    w_att_in_t, w_att_out, w_dn_out = w_att_in_t[0], w_att_out[0], w_dn_out[0]
    w_dn_in_t = jnp.concatenate(
        [w_dn_in_g[0, DN_IN_SHARD_PAD * j:DN_IN_SHARD_PAD * j + DN_IN_SHARD] for j in range(N_DEV)]
        + [jnp.zeros((DN_IN_PAD - DN_IN, D), BF16)], axis=0)
    conv_full = conv_g.transpose(1, 0, 2).reshape(DN_CONV, 3 * TOK_W)

    def gain(a, i):
        return a[i].reshape(1, D)

    (h0,) = _rowwise(_fn_pre, [x0], [gain(norm_mix_pre, 0)], [(D, BF16)], ROW_TB, "pre0")
    km0, vm0, memn0 = _mem_kv_fwd(mem0, gain(mem_norm, 0), w_kv[0], 0)
    p0 = _matmul(h0, w_att_in_t, "nt", F32, "att_in")
    cat0, res0 = _attn_mixer_fwd(p0, rel_bias, km0, vm0)
    y0 = _matmul(cat0, w_att_out, "nn", F32, "att_out")
    g_a = [gain(norm_mix_post, 0), gain(norm_ffn_pre, 0)]
    x1, h1 = _rowwise(_fn_res_pre, [x0, y0], g_a, [(D, F32), (D, BF16)], ROW_TB, "res_pre0")
    f0, gu0, a0 = _ffn_fwd(h1, w_gu_t[0], w_down[0], 0)
    g_b = [gain(norm_ffn_post, 0), gain(norm_mix_pre, 1)]
    x2, h2 = _rowwise(_fn_res_pre, [x1, f0], g_b, [(D, F32), (D, BF16)], ROW_TB, "res_pre1")
    km1, vm1, memn1 = _mem_kv_fwd(mem0, gain(mem_norm, 1), w_kv[1], 1)
    p1 = _matmul(h2, w_dn_in_t, "nt", F32, "dn_in")
    cat1, res1 = _dn_mixer_fwd(p1, conv_full, dn_a_log[0], dn_dt_bias[0], dn_out_norm[0], km1, vm1)
    y1 = _matmul(cat1, w_dn_out, "nn", F32, "dn_out")
    g_c = [gain(norm_mix_post, 1), gain(norm_ffn_pre, 1)]
    x3, h3 = _rowwise(_fn_res_pre, [x2, y1], g_c, [(D, F32), (D, BF16)], ROW_TB, "res_pre2")
    f1, gu1, a1 = _ffn_fwd(h3, w_gu_t[1], w_down[1], 1)
    g_d = [gain(norm_ffn_post, 1)]
    (x4,) = _rowwise(_fn_res, [x3, f1], g_d, [(D, F32)], ROW_TB, "res3")
    dx4, lrow = _loss_kernel(x4, tgt, ROW_TB, "loss")
    loss = lax.psum(lrow[0, 0] * (0.5 / D), axes)

    (df1,), (dg_fpost1,) = _rowwise_bwd(_fn_res, [x3, f1], g_d, [dx4], [None, BF16], ROW_TB, "res3_bwd")
    dh3, dwgu1, dwd1 = _ffn_bwd(df1, h3, w_gu_t[1], w_down[1], gu1, a1, 1)
    (dx2, dy1), (dg_mpost1, dg_fpre1) = _rowwise_bwd(_fn_res_pre, [x2, y1], g_c, [dx4, dh3], [F32, BF16],
                                                     ROW_TB, "res_pre2_bwd")
    dcat1 = _matmul(dy1, w_dn_out, "nt", F32, "dn_out_dx")
    dw_dn_out = _matmul(cat1, dy1, "tn", BF16, "dn_out_dw")
    dp1, dconv, da_log, ddt_bias, dout_norm, dkm1, dvm1 = _dn_mixer_bwd(dcat1, res1, km1, vm1)
    dwkv1, dg_mem1 = _mem_kv_bwd(mem0, gain(mem_norm, 1), w_kv[1], memn1, dkm1, dvm1, 1)
    dh2 = _matmul(dp1, w_dn_in_t, "nn", F32, "dn_in_dx")
    dw_dn_in_t = _matmul(dp1, h2, "tn", BF16, "dn_in_dw")
    (dx1, df0), (dg_fpost0, dg_mpre1) = _rowwise_bwd(_fn_res_pre, [x1, f0], g_b, [dx2, dh2], [F32, BF16],
                                                     ROW_TB, "res_pre1_bwd")
    dh1, dwgu0, dwd0 = _ffn_bwd(df0, h1, w_gu_t[0], w_down[0], gu0, a0, 0)
    (dx0, dy0), (dg_mpost0, dg_fpre0) = _rowwise_bwd(_fn_res_pre, [x0, y0], g_a, [dx1, dh1], [F32, BF16],
                                                     ROW_TB, "res_pre0_bwd")
    dcat0 = _matmul(dy0, w_att_out, "nt", F32, "att_out_dx")
    dw_att_out = _matmul(cat0, dy0, "tn", BF16, "att_out_dw")
    dp0, drel, dkm0, dvm0 = _attn_mixer_bwd(dcat0, res0, km0, vm0)
    dwkv0, dg_mem0 = _mem_kv_bwd(mem0, gain(mem_norm, 0), w_kv[0], memn0, dkm0, dvm0, 0)
    dh0 = _matmul(dp0, w_att_in_t, "nn", F32, "att_in_dx")
    dw_att_in_t = _matmul(dp0, h0, "tn", BF16, "att_in_dw")
    (grad_x,), (dg_mpre0,) = _rowwise_bwd(_fn_first, [x0], [gain(norm_mix_pre, 0)], [dx0, dh0], [F32],
                                          ROW_TB, "pre0_bwd")

    dn_in_parts = [jnp.pad(dw_dn_in_t[DN_IN_SHARD * j:DN_IN_SHARD * (j + 1)],
                           ((0, DN_IN_SHARD_PAD - DN_IN_SHARD), (0, 0))) for j in range(N_DEV)]
    small_grads = [drel, da_log, ddt_bias, dout_norm, jnp.concatenate([dg_mem0, dg_mem1]),
                   jnp.concatenate([dg_mpre0, dg_mpre1]), jnp.concatenate([dg_mpost0, dg_mpost1]),
                   jnp.concatenate([dg_fpre0, dg_fpre1]), jnp.concatenate([dg_fpost0, dg_fpost1]), dconv]
    fulls = [dw_att_in_t[None], dw_att_out[None], jnp.concatenate(dn_in_parts, axis=0)[None], dw_dn_out[None],
             jnp.stack([dwkv0, dwkv1]), jnp.stack([dwgu0, dwgu1]), jnp.stack([dwd0, dwd1]),
             _pack_small(small_grads, SMALL_ROWS)]
    r_att_in, r_att_out, r_dn_in, r_dn_out, r_kv, r_gu, r_down, r_small = _exchange(
        fulls, [1, 1, 1, 1, 1, 1, 1, None], "exchange_grads")

    def rows(a):
        return a.reshape((-1,) + a.shape[-1:])

    def row_sharded(recv, w, m, v, tb, name):
        outs = _adamw_reduce(recv.reshape((N_DEV, -1) + recv.shape[-1:]), rows(w), rows(m), rows(v), tb, name)
        return [o.reshape(w.shape) for o in outs]

    def col_sharded(recv, w, m, v, tb, name):
        g_t = _reduce8(recv.reshape((N_DEV, -1) + recv.shape[-1:]), tb, name + "_sum")
        g = jnp.swapaxes(g_t.reshape(recv.shape[1:])[:, :w.shape[2]], 1, 2)
        outs = _adamw(rows(g), rows(w), rows(m), rows(v), 256, name)
        return [g] + [o.reshape(w.shape) for o in outs]

    big = [col_sharded(r_att_in, att_w_in, m_att_w_in, v_att_w_in, 320, "adamw_att_in"),
           row_sharded(r_att_out, att_w_out, m_att_w_out, v_att_w_out, 128, "adamw_att_out"),
           col_sharded(r_dn_in, dn_w_in, m_dn_w_in, v_dn_w_in, 432, "adamw_dn_in"),
           row_sharded(r_dn_out, dn_w_out, m_dn_w_out, v_dn_w_out, 128, "adamw_dn_out"),
           row_sharded(r_kv, mem_w_kv, m_mem_w_kv, v_mem_w_kv, 256, "adamw_mem_kv"),
           col_sharded(r_gu, ffn_w_gate_up, m_ffn_w_gate_up, v_ffn_w_gate_up, 176, "adamw_ffn_gu"),
           row_sharded(r_down, ffn_w_down, m_ffn_w_down, v_ffn_w_down, 176, "adamw_ffn_down")]
    g_big, d_big, nm_big, nv_big = [[b[i] for b in big] for i in range(4)]

    g_small = _reduce8(r_small, SMALL_ROWS, "reduce_small")
    rep_shapes = [(32, 12), (1, 2, 6), (1, 2, 6), (1, 128), (2, D), (2, D), (2, D), (2, D), (2, D)]
    *g_rep, g_conv_full = _unpack_small(g_small, rep_shapes + [(DN_CONV, 3 * TOK_W)])
    me = _index(_me_xyc())
    g_conv = lax.dynamic_slice(g_conv_full, (0, me * 288), (DN_CONV, 288)).reshape(CONV_SHARD)
    small_shapes = rep_shapes + [CONV_SHARD]
    small_w = [rel_bias, dn_a_log, dn_dt_bias, dn_out_norm, mem_norm, norm_mix_pre, norm_mix_post,
               norm_ffn_pre, norm_ffn_post, dn_conv]
    small_m = [m_rel_bias, m_dn_a_log, m_dn_dt_bias, m_dn_out_norm, m_mem_norm, m_norm_mix_pre, m_norm_mix_post,
               m_norm_ffn_pre, m_norm_ffn_post, m_dn_conv]
    small_v = [v_rel_bias, v_dn_a_log, v_dn_dt_bias, v_dn_out_norm, v_mem_norm, v_norm_mix_pre, v_norm_mix_post,
               v_norm_ffn_pre, v_norm_ffn_post, v_dn_conv]
    g_small_list = g_rep + [g_conv]
    outs_small = _adamw(_pack_small(g_small_list, 24), _pack_small(small_w, 24), _pack_small(small_m, 24),
                        _pack_small(small_v, 24), 24, "adamw_small")
    d_small, nm_small, nv_small = [_unpack_small(o, small_shapes) for o in outs_small]

    def ordered(small, big):
        return [small[0], big[0], big[1], big[2], small[9], small[1], small[2], small[3], big[3], small[4],
                big[4], small[5], small[6], small[7], small[8], big[5], big[6]]

    g_small_out = [g.reshape(s) for g, s in zip(g_small_list, small_shapes)]
    return (loss, grad_x[None], *ordered(g_small_out, g_big), *ordered(d_small, d_big),
            *ordered(nm_small, nm_big), *ordered(nv_small, nv_big))
```

```python
import functools
import math

import numpy as np
import jax
import jax.numpy as jnp
from jax import lax
from jax.experimental import pallas as pl
from jax.experimental.pallas import tpu as pltpu

F32 = jnp.float32
BF16 = jnp.bfloat16
HI = lax.Precision.HIGHEST
MESH = pl.DeviceIdType.MESH

N_DEV = 8
D = 1024
EPS = 1e-6
NEG = -1e30
TOK_W = 768
MEM_W = 256
ATT_HD = 64
DIL_GROUPS = ((128, 1), (512, 4), (2048, 16))
BAND_HALF = 64
REL_BUCKETS = 32
REL_MAX_DIST = 1024
DN_HD = 128
DN_HEADS = 6
DN_CONV = 5
DN_CHUNK = 64
MEM_HEADS = 4
D_FF = 2816
ATT_IN = 2560
DN_IN = 3352
DN_IN_PAD = 3456

ADAM_LR, ADAM_B1, ADAM_B2, ADAM_EPS, ADAM_WD, ADAM_STEP = 0.001, 0.9, 0.999, 1e-08, 0.01, 10

PACK_C = 512
BIG_ROWS = 6480
SMALL_ROWS = 48
VMEM_LIMIT = 48 * 1024 * 1024


def _cparams(sem=None):
    kw = dict(vmem_limit_bytes=VMEM_LIMIT)
    if sem is not None:
        kw["dimension_semantics"] = sem
    return pltpu.CompilerParams(**kw)


def _tile(n, cap):
    if n <= cap:
        return n
    best = None
    for t in range(128, cap + 1, 128):
        if n % t == 0:
            best = t
    assert best is not None, (n, cap)
    return best


def _matmul(a, b, mode, out_dtype, name, tm=512, tn=512, tk=1024):
    if mode == "nn":
        (m, kc), (_, n) = a.shape, b.shape
        dims = (((1,), (0,)), ((), ()))
    elif mode == "nt":
        (m, kc), (n, _) = a.shape, b.shape
        dims = (((1,), (1,)), ((), ()))
    else:
        (kc, m), (_, n) = a.shape, b.shape
        dims = (((0,), (0,)), ((), ()))
    tm = m if m <= tm else _tile(m, tm)
    tn = _tile(n, tn)
    tk = _tile(kc, tk)
    nk = kc // tk

    def body(a_ref, b_ref, o_ref, acc_ref):
        k = pl.program_id(2)
        part = lax.dot_general(a_ref[...], b_ref[...], dims, preferred_element_type=F32)

        @pl.when(k == 0)
        def _():
            acc_ref[...] = part

        @pl.when(k > 0)
        def _():
            acc_ref[...] += part

        @pl.when(k == nk - 1)
        def _():
            o_ref[...] = acc_ref[...].astype(o_ref.dtype)

    if mode == "nn":
        a_spec = pl.BlockSpec((tm, tk), lambda i, j, k: (i, k))
        b_spec = pl.BlockSpec((tk, tn), lambda i, j, k: (k, j))
    elif mode == "nt":
        a_spec = pl.BlockSpec((tm, tk), lambda i, j, k: (i, k))
        b_spec = pl.BlockSpec((tn, tk), lambda i, j, k: (j, k))
    else:
        a_spec = pl.BlockSpec((tk, tm), lambda i, j, k: (k, i))
        b_spec = pl.BlockSpec((tk, tn), lambda i, j, k: (k, j))
    return pl.pallas_call(
        body, name=name, grid=(m // tm, n // tn, nk),
        in_specs=[a_spec, b_spec],
        out_specs=pl.BlockSpec((tm, tn), lambda i, j, k: (i, j)),
        out_shape=jax.ShapeDtypeStruct((m, n), out_dtype),
        scratch_shapes=[pltpu.VMEM((tm, tn), F32)],
        compiler_params=_cparams(("parallel", "parallel", "arbitrary")),
    )(a, b)


def _rowwise(fn, rows, params, outs, tb, name):
    t = rows[0].shape[0]
    nr, npar = len(rows), len(params)

    def body(*refs):
        ins = [r[...].astype(F32) for r in refs[:nr + npar]]
        res = fn(*ins)
        for o_ref, r in zip(refs[nr + npar:], res):
            o_ref[...] = r.astype(o_ref.dtype)

    return pl.pallas_call(
        body, name=name, grid=(t // tb,),
        in_specs=[pl.BlockSpec((tb, r.shape[1]), lambda i: (i, 0)) for r in rows]
        + [pl.BlockSpec(p.shape, lambda i: (0, 0)) for p in params],
        out_specs=[pl.BlockSpec((tb, c), lambda i: (i, 0)) for c, _ in outs],
        out_shape=[jax.ShapeDtypeStruct((t, c), dt) for c, dt in outs],
        compiler_params=_cparams(("parallel",)),
    )(*rows, *params)


def _rowwise_bwd(fn, rows, params, cots, row_grad, tb, name):
    t = rows[0].shape[0]
    nr, npar, nc = len(rows), len(params), len(cots)
    want = [i for i, g in enumerate(row_grad) if g is not None]

    def body(*refs):
        ins = [r[...].astype(F32) for r in refs[:nr + npar]]
        cts = tuple(r[...].astype(F32) for r in refs[nr + npar:nr + npar + nc])
        outs = refs[nr + npar + nc:]
        _, vjp = jax.vjp(fn, *ins)
        grads = vjp(cts)
        for o_ref, i in zip(outs[:len(want)], want):
            o_ref[...] = grads[i].astype(o_ref.dtype)
        first = pl.program_id(0) == 0
        for o_ref, g in zip(outs[len(want):], grads[nr:]):
            @pl.when(first)
            def _(o_ref=o_ref, g=g):
                o_ref[...] = g

            @pl.when(jnp.logical_not(first))
            def _(o_ref=o_ref, g=g):
                o_ref[...] += g

    res = pl.pallas_call(
        body, name=name, grid=(t // tb,),
        in_specs=[pl.BlockSpec((tb, r.shape[1]), lambda i: (i, 0)) for r in rows]
        + [pl.BlockSpec(p.shape, lambda i: (0, 0)) for p in params]
        + [pl.BlockSpec((tb, c.shape[1]), lambda i: (i, 0)) for c in cots],
        out_specs=[pl.BlockSpec((tb, rows[i].shape[1]), lambda i_: (i_, 0)) for i in want]
        + [pl.BlockSpec(p.shape, lambda i: (0, 0)) for p in params],
        out_shape=[jax.ShapeDtypeStruct(rows[i].shape, row_grad[i]) for i in want]
        + [jax.ShapeDtypeStruct(p.shape, F32) for p in params],
        compiler_params=_cparams(("arbitrary",)),
    )(*rows, *params, *cots)
    return list(res[:len(want)]), list(res[len(want):])


def _rms(x, g):
    return x * lax.rsqrt(jnp.mean(x * x, axis=-1, keepdims=True) + EPS) * g


def _fn_pre(x, g):
    return (_rms(x, g),)


def _fn_res_pre(x, y, g_post, g_pre):
    x1 = x + _rms(y, g_post)
    return x1, _rms(x1, g_pre)


def _fn_res(x, y, g_post):
    return (x + _rms(y, g_post),)


def _sigmoid(x):
    return 1.0 / (1.0 + jnp.exp(-x))


def _silu(x):
    return x * _sigmoid(x)


def _fn_swiglu(gu):
    return (_silu(gu[:, :D_FF]) * gu[:, D_FF:],)


def _fn_combine(o, lse):
    ls = [lse[:, 256 * g:256 * (g + 1)] for g in range(3)]
    mx = lax.stop_gradient(jnp.maximum(jnp.maximum(ls[0], ls[1]), ls[2]))
    es = [jnp.exp(l - mx) for l in ls]
    inv = 1.0 / (es[0] + es[1] + es[2])
    return (jnp.concatenate([o[:, 256 * g:256 * (g + 1)] * (es[g] * inv) for g in range(3)], axis=1),)


def _fn_outnorm(o_f, o_r, z, gain):
    res = []
    for h in range(DN_HEADS):
        sl = slice(DN_HD * h, DN_HD * (h + 1))
        o = o_f[:, sl] + o_r[:, sl]
        res.append(o * lax.rsqrt(jnp.mean(o * o, axis=-1, keepdims=True) + EPS) * gain * _silu(z[:, sl]))
    return (jnp.concatenate(res, axis=1),)


def _loss_kernel(x, tgt, tb, name):
    t, d = x.shape

    def body(x_ref, t_ref, dx_ref, l_ref, acc_ref):
        i = pl.program_id(0)
        e = x_ref[...] - t_ref[...]
        dx_ref[...] = e * (1.0 / d)
        part = jnp.sum(e * e, axis=0, keepdims=True)

        @pl.when(i == 0)
        def _():
            acc_ref[...] = part

        @pl.when(i > 0)
        def _():
            acc_ref[...] += part

        @pl.when(i == t // tb - 1)
        def _():
            l_ref[...] = jnp.broadcast_to(jnp.sum(acc_ref[...], axis=-1, keepdims=True), (1, 128))

    return pl.pallas_call(
        body, name=name, grid=(t // tb,),
        in_specs=[pl.BlockSpec((tb, d), lambda i: (i, 0))] * 2,
        out_specs=[pl.BlockSpec((tb, d), lambda i: (i, 0)), pl.BlockSpec((1, 128), lambda i: (0, 0))],
        out_shape=[jax.ShapeDtypeStruct((t, d), F32), jax.ShapeDtypeStruct((1, 128), F32)],
        scratch_shapes=[pltpu.VMEM((1, d), F32)],
        compiler_params=_cparams(("arbitrary",)),
    )(x, tgt)


def _band_fn(l_sub, bq, i, q, kw, vw, bm):
    w = bq + 2 * BAND_HALF
    s = lax.dot_general((q * (ATT_HD ** -0.5)).astype(BF16), kw.astype(BF16), (((1,), (1,)), ((), ())),
                        preferred_element_type=F32) + bm
    kpos = i * bq - BAND_HALF + lax.broadcasted_iota(jnp.int32, (bq, w), 1)
    s = jnp.where((kpos >= 0) & (kpos < l_sub), s, NEG)
    m = lax.stop_gradient(jnp.max(s, axis=-1, keepdims=True))
    p = jnp.exp(s - m)
    den = jnp.sum(p, axis=-1, keepdims=True)
    o = jnp.dot(p.astype(BF16), vw.astype(BF16), preferred_element_type=F32) / den
    return o, jnp.broadcast_to(m + jnp.log(den), o.shape)


def _band_specs(dil, l_sub, bq):
    w = bq + 2 * BAND_HALF
    qs = pl.BlockSpec((None, bq, ATT_HD), lambda h, r, i: (h * dil + r, i, 0))
    ks = pl.BlockSpec((None, l_sub + 2 * BAND_HALF, ATT_HD), lambda h, r, i: (h * dil + r, 0, 0))
    bs = pl.BlockSpec((None, bq, w), lambda h, r, i: (h, 0, 0))
    return qs, ks, bs


def _band_fwd(q, k, v, bm, dil, l_sub, bq, name):
    w = bq + 2 * BAND_HALF
    qs, ks, bs = _band_specs(dil, l_sub, bq)

    def body(q_ref, k_ref, v_ref, bm_ref, o_ref, l_ref):
        i = pl.program_id(2)
        st = pl.multiple_of(i * bq, bq)
        o, lse = _band_fn(l_sub, bq, i, q_ref[...], k_ref[pl.ds(st, w), :], v_ref[pl.ds(st, w), :], bm_ref[...])
        o_ref[...] = o
        l_ref[...] = lse

    return pl.pallas_call(
        body, name=name, grid=(4, dil, l_sub // bq),
        in_specs=[qs, ks, ks, bs], out_specs=[qs, qs],
        out_shape=[jax.ShapeDtypeStruct(q.shape, F32)] * 2,
        compiler_params=_cparams(("parallel", "parallel", "arbitrary")),
    )(q, k, v, bm)


def _band_bwd(q, k, v, bm, do, dlse, dil, l_sub, bq, name):
    w = bq + 2 * BAND_HALF
    qs, ks, bs = _band_specs(dil, l_sub, bq)

    def body(q_ref, k_ref, v_ref, bm_ref, do_ref, dl_ref, dq_ref, dk_ref, dv_ref, dbm_ref):
        r, i = pl.program_id(1), pl.program_id(2)
        st = pl.multiple_of(i * bq, bq)
        _, vjp = jax.vjp(functools.partial(_band_fn, l_sub, bq, i),
                         q_ref[...], k_ref[pl.ds(st, w), :], v_ref[pl.ds(st, w), :], bm_ref[...])
        dq, dkw, dvw, dbm = vjp((do_ref[...], dl_ref[...]))
        dq_ref[...] = dq

        @pl.when(i == 0)
        def _():
            dk_ref[...] = jnp.zeros_like(dk_ref)
            dv_ref[...] = jnp.zeros_like(dv_ref)

        dk_ref[pl.ds(st, w), :] += dkw
        dv_ref[pl.ds(st, w), :] += dvw

        @pl.when((i == 0) & (r == 0))
        def _():
            dbm_ref[...] = dbm

        @pl.when((i > 0) | (r > 0))
        def _():
            dbm_ref[...] += dbm

    return pl.pallas_call(
        body, name=name, grid=(4, dil, l_sub // bq),
        in_specs=[qs, ks, ks, bs, qs, qs], out_specs=[qs, ks, ks, bs],
        out_shape=[jax.ShapeDtypeStruct(q.shape, F32), jax.ShapeDtypeStruct(k.shape, F32),
                   jax.ShapeDtypeStruct(k.shape, F32), jax.ShapeDtypeStruct(bm.shape, F32)],
        compiler_params=_cparams(("parallel", "arbitrary", "arbitrary")),
    )(q, k, v, bm, do, dlse)


def _t5_bucket(rel):
    half = REL_BUCKETS // 2
    max_exact = half // 2
    n = np.abs(rel)
    large = max_exact + (np.log(np.maximum(n, 1) / max_exact) / math.log(REL_MAX_DIST / max_exact)
                         * (half - max_exact)).astype(np.int64)
    large = np.minimum(large, half - 1)
    return ((rel > 0) * half + np.where(n < max_exact, n, large)).astype(np.int32)


def _bucket_onehot(dil):
    idx = _t5_bucket(np.arange(-BAND_HALF, BAND_HALF + 1) * dil)
    oh = np.zeros((2 * BAND_HALF + 1, REL_BUCKETS), np.float32)
    oh[np.arange(2 * BAND_HALF + 1), idx] = 1.0
    return oh


def _band_bias(rel_bias, gi, dil, bq):
    w = bq + 2 * BAND_HALF
    nb = 2 * BAND_HALF + 1
    bias = jnp.dot(jnp.asarray(_bucket_onehot(dil)), rel_bias[:, 4 * gi:4 * gi + 4], precision=HI)
    row = jnp.concatenate([bias.T, jnp.full((4, w + 1 - nb), NEG, F32)], axis=1)
    flat = jnp.tile(row, (1, bq))[:, :bq * w]
    return flat.reshape(4, bq, w)


def _relbias_grad(dbms, name):
    nb = 2 * BAND_HALF + 1
    bq = max(d.shape[1] for d in dbms)
    skew = []
    for dbm in dbms:
        bqg, w = dbm.shape[1], dbm.shape[2]
        flat = jnp.pad(dbm.reshape(4, bqg * w), ((0, 0), (0, bqg)))
        skew.append(jnp.pad(flat.reshape(4, bqg, w + 1)[:, :, :nb], ((0, 0), (0, bq - bqg), (0, 256 - nb))))
    sk = jnp.concatenate(skew, axis=0)
    oh = np.zeros((3, 256, 128), np.float32)
    for gi, (_, dil) in enumerate(DIL_GROUPS):
        oh[gi, :2 * BAND_HALF + 1, :REL_BUCKETS] = _bucket_onehot(dil)

    def body(s_ref, oh_ref, o_ref):
        col = jnp.sum(s_ref[...], axis=0, keepdims=True)
        o_ref[...] = jnp.dot(jnp.broadcast_to(col, (8, 256)), oh_ref[...], precision=HI, preferred_element_type=F32)

    out = pl.pallas_call(
        body, name=name, grid=(12,),
        in_specs=[pl.BlockSpec((None, bq, 256), lambda n: (n, 0, 0)),
                  pl.BlockSpec((None, 256, 128), lambda n: (n // 4, 0, 0))],
        out_specs=pl.BlockSpec((None, 8, 128), lambda n: (n, 0, 0)),
        out_shape=jax.ShapeDtypeStruct((12, 8, 128), F32),
        compiler_params=_cparams(("parallel",)),
    )(sk, jnp.asarray(oh))
    return out[:, 0, :REL_BUCKETS].T


def _mem_fn(q, k, v):
    s = lax.dot_general((q * (ATT_HD ** -0.5)).astype(BF16), k.astype(BF16), (((1,), (1,)), ((), ())),
                        preferred_element_type=F32)
    m = lax.stop_gradient(jnp.max(s, axis=-1, keepdims=True))
    p = jnp.exp(s - m)
    p = p / jnp.sum(p, axis=-1, keepdims=True)
    return jnp.dot(p.astype(BF16), v.astype(BF16), preferred_element_type=F32)


def _mem_specs(tb, ml):
    qs = pl.BlockSpec((None, tb, ATT_HD), lambda h, i: (h, i, 0))
    ks = pl.BlockSpec((None, ml, ATT_HD), lambda h, i: (h, 0, 0))
    return qs, ks


def _mem_fwd(q, k, v, tb, name):
    qs, ks = _mem_specs(tb, k.shape[1])

    def body(q_ref, k_ref, v_ref, o_ref):
        o_ref[...] = _mem_fn(q_ref[...], k_ref[...], v_ref[...])

    return pl.pallas_call(
        body, name=name, grid=(MEM_HEADS, q.shape[1] // tb),
        in_specs=[qs, ks, ks], out_specs=qs, out_shape=jax.ShapeDtypeStruct(q.shape, F32),
        compiler_params=_cparams(("parallel", "parallel")),
    )(q, k, v)


def _mem_bwd(q, k, v, do, tb, name):
    qs, ks = _mem_specs(tb, k.shape[1])

    def body(q_ref, k_ref, v_ref, do_ref, dq_ref, dk_ref, dv_ref):
        i = pl.program_id(1)
        _, vjp = jax.vjp(_mem_fn, q_ref[...], k_ref[...], v_ref[...])
        dq, dk, dv = vjp(do_ref[...])
        dq_ref[...] = dq

        @pl.when(i == 0)
        def _():
            dk_ref[...] = dk
            dv_ref[...] = dv

        @pl.when(i > 0)
        def _():
            dk_ref[...] += dk
            dv_ref[...] += dv

    return pl.pallas_call(
        body, name=name, grid=(MEM_HEADS, q.shape[1] // tb),
        in_specs=[qs, ks, ks, qs], out_specs=[qs, ks, ks],
        out_shape=[jax.ShapeDtypeStruct(q.shape, F32), jax.ShapeDtypeStruct(k.shape, F32),
                   jax.ShapeDtypeStruct(k.shape, F32)],
        compiler_params=_cparams(("parallel", "arbitrary")),
    )(q, k, v, do)


CONV_PAD = 8


def _conv_post(kind, acc):
    s = _silu(acc)
    if kind == 2:
        return s
    scale = DN_HD ** -0.5 if kind == 0 else 1.0
    return s * lax.rsqrt(jnp.sum(s * s, axis=-1, keepdims=True) + EPS) * scale


def _conv_acc(xp_ref, w, r0, rt):
    acc = None
    for i in range(DN_CONV):
        term = w[i:i + 1, :] * xp_ref[pl.ds(CONV_PAD + r0 + i - DN_CONV // 2, rt), :]
        acc = term if acc is None else acc + term
    return acc


def _conv_fwd(xp, w8, kind, rt, name):
    t = xp.shape[0] - 2 * CONV_PAD

    def body(xp_ref, w_ref, o_ref):
        w = w_ref[...]
        for r in range(t // rt):
            o_ref[pl.ds(r * rt, rt), :] = _conv_post(kind, _conv_acc(xp_ref, w, r * rt, rt))

    return pl.pallas_call(
        body, name=name, grid=(DN_HEADS,),
        in_specs=[pl.BlockSpec((t + 2 * CONV_PAD, DN_HD), lambda j: (0, 6 * kind + j)),
                  pl.BlockSpec((8, DN_HD), lambda j: (0, 6 * kind + j))],
        out_specs=pl.BlockSpec((t, DN_HD), lambda j: (0, j)),
        out_shape=jax.ShapeDtypeStruct((t, TOK_W), F32),
        compiler_params=_cparams(("parallel",)),
    )(xp, w8)


def _conv_bwd(xp, w8, d_f, d_r, kind, rt, name):
    t = xp.shape[0] - 2 * CONV_PAD

    def body(xp_ref, w_ref, df_ref, dr_ref, dx_ref, dw_ref, dpad_ref):
        w = w_ref[...]
        zero = jnp.zeros((CONV_PAD, DN_HD), F32)
        dpad_ref[pl.ds(0, CONV_PAD), :] = zero
        dpad_ref[pl.ds(CONV_PAD + t, CONV_PAD), :] = zero
        dw = [jnp.zeros((1, DN_HD), F32) for _ in range(DN_CONV)]
        for r in range(t // rt):
            rows = pl.ds(r * rt, rt)
            acc = _conv_acc(xp_ref, w, r * rt, rt)
            _, vjp = jax.vjp(functools.partial(_conv_post, kind), acc)
            (dacc,) = vjp(df_ref[rows, :] + dr_ref[rows, :])
            dpad_ref[pl.ds(CONV_PAD + r * rt, rt), :] = dacc
            for i in range(DN_CONV):
                xs = xp_ref[pl.ds(CONV_PAD + r * rt + i - DN_CONV // 2, rt), :]
                dw[i] = dw[i] + jnp.sum(dacc * xs, axis=0, keepdims=True)
        dw_ref[...] = jnp.concatenate(dw + [jnp.zeros((8 - DN_CONV, DN_HD), F32)], axis=0)
        for r in range(t // rt):
            acc = None
            for i in range(DN_CONV):
                term = w[i:i + 1, :] * dpad_ref[pl.ds(CONV_PAD + r * rt - i + DN_CONV // 2, rt), :]
                acc = term if acc is None else acc + term
            dx_ref[pl.ds(r * rt, rt), :] = acc

    return pl.pallas_call(
        body, name=name, grid=(DN_HEADS,),
        in_specs=[pl.BlockSpec((t + 2 * CONV_PAD, DN_HD), lambda j: (0, 6 * kind + j)),
                  pl.BlockSpec((8, DN_HD), lambda j: (0, 6 * kind + j)),
                  pl.BlockSpec((t, DN_HD), lambda j: (0, j)),
                  pl.BlockSpec((t, DN_HD), lambda j: (0, j))],
        out_specs=[pl.BlockSpec((t, DN_HD), lambda j: (0, j)), pl.BlockSpec((8, DN_HD), lambda j: (0, j))],
        out_shape=[jax.ShapeDtypeStruct((t, TOK_W), F32), jax.ShapeDtypeStruct((8, TOK_W), F32)],
        scratch_shapes=[pltpu.VMEM((t + 2 * CONV_PAD, DN_HD), F32)],
        compiler_params=_cparams(("parallel",)),
    )(xp, w8, d_f, d_r)


def _softplus(x):
    e = jnp.exp(-jnp.abs(x))
    return jnp.maximum(x, 0.0) + jnp.where(e < 1e-4, e - 0.5 * e * e, jnp.log(1.0 + e))


_NN = (((2,), (1,)), ((0,), (0,)))
_NT = (((2,), (2,)), ((0,), (0,)))
_TN = (((1,), (1,)), ((0,), (0,)))


def _dot(a, b, dims=_NN):
    return lax.dot_general(a.astype(BF16), b.astype(BF16), dims, preferred_element_type=F32)


def _hi_lo(x):
    hi = x.astype(BF16)
    return hi, (x - hi.astype(F32)).astype(BF16)


def _mask_dot(mask_bf16, x, dims):
    x1 = x.astype(BF16)
    r = x - x1.astype(F32)
    x2, x3 = _hi_lo(r)
    d = functools.partial(lax.dot_general, dimension_numbers=dims, preferred_element_type=F32)
    return d(mask_bf16, x1) + d(mask_bf16, x2) + d(mask_bf16, x3)


@jax.custom_vjp
def _dot_mask(mask_bf16, x):
    return _mask_dot(mask_bf16, x, _NN)


def _dot_mask_fwd(mask_bf16, x):
    return _mask_dot(mask_bf16, x, _NN), mask_bf16


def _dot_mask_bwd(mask_bf16, ct):
    return jnp.zeros_like(mask_bf16), _mask_dot(mask_bf16, ct, _TN)


_dot_mask.defvjp(_dot_mask_fwd, _dot_mask_bwd)


def _dot3_raw(a, b, dims):
    a1, a2 = _hi_lo(a)
    b1, b2 = _hi_lo(b)
    d = functools.partial(lax.dot_general, dimension_numbers=dims, preferred_element_type=F32)
    return d(a1, b1) + d(a1, b2) + d(a2, b1)


@jax.custom_vjp
def _dot3(a, b):
    return _dot3_raw(a, b, _NN)


def _dot3_fwd(a, b):
    return _dot3_raw(a, b, _NN), (a, b)


def _dot3_bwd(res, ct):
    a, b = res
    return _dot3_raw(ct, b, _NT), _dot3_raw(a, ct, _TN)


_dot3.defvjp(_dot3_fwd, _dot3_bwd)


def _dn_chunk(q, k, v, al, be, alc, a_row, dt_row, a_rowc, dt_rowc, s):
    n, c = q.shape[0], DN_CHUNK
    rev = lax.broadcasted_iota(jnp.int32, (n, c, c), 0) >= n // 2
    row = lax.broadcasted_iota(jnp.int32, (n, c, c), 1)
    col = lax.broadcasted_iota(jnp.int32, (n, c, c), 2)
    ahead = jnp.where(rev, col - row, row - col)
    incl = ahead >= 0
    strict = ahead > 0
    incl_b = incl.astype(BF16)

    g = -jnp.exp(a_row) * _softplus(al + dt_row)
    beta = _sigmoid(be)
    g_c = -jnp.exp(a_rowc) * _softplus(alc + dt_rowc)
    gc = _dot_mask(incl_b, g)
    gcc = _dot_mask(incl_b, g_c)
    decay = jnp.exp(jnp.where(incl, gcc - jnp.swapaxes(gcc, 1, 2), NEG))
    kb = k * beta
    lmat = jnp.where(strict, _dot(kb, k, _NT) * decay, 0.0)
    rhs = jnp.concatenate([v * beta, kb * jnp.exp(gc)], axis=2)
    xp = -lmat
    sol = rhs + _dot3(xp, rhs)
    for _ in range(5):
        xp = _dot3(xp, xp)
        sol = sol + _dot3(xp, sol)
    u, w = sol[:, :, :DN_HD], sol[:, :, DN_HD:]
    intra = jnp.where(incl, _dot(q, k, _NT) * decay, 0.0)
    v_new = u - _dot(w, s)
    out = _dot(q * jnp.exp(gc), s) + _dot(intra, v_new)
    g_last = jnp.sum(g, axis=1, keepdims=True)
    s_new = s * jnp.exp(g_last) + _dot(k * jnp.exp(g_last - gc), v_new, _TN)
    return out, s_new


DN_HG = 6


def _dn_load(f_refs, r_refs, alf, bef, alr, ber, a_ref, dt_ref):
    c, hg = DN_CHUNK, DN_HG
    sls = [slice(DN_HD * h, DN_HD * (h + 1)) for h in range(hg)]
    toks = [jnp.stack([f[:, sl] for sl in sls] + [r[:, sl] for sl in sls]) for f, r in zip(f_refs, r_refs)]
    al = jnp.concatenate([alf[...], alr[...]], axis=0)
    be = jnp.concatenate([bef[...], ber[...]], axis=0)
    alc = jnp.concatenate([alf[:, :, 0:c], alr[:, :, 0:c]], axis=0)
    a = jnp.concatenate([a_ref[0], a_ref[1]], axis=0)
    dt = jnp.concatenate([dt_ref[0], dt_ref[1]], axis=0)
    ac = jnp.concatenate([a_ref[0, :, :, 0:c], a_ref[1, :, :, 0:c]], axis=0)
    dtc = jnp.concatenate([dt_ref[0, :, :, 0:c], dt_ref[1, :, :, 0:c]], axis=0)
    return toks, (al, be, alc, a, dt, ac, dtc)


def _dn_views(nc, bwd):
    c, hg = DN_CHUNK, DN_HG
    if bwd:
        f_blk = lambda s: nc - 1 - s
        r_blk = lambda s: s
        st_blk = lambda s: nc - 1 - s
    else:
        f_blk = lambda s: s
        r_blk = lambda s: nc - 1 - s
        st_blk = lambda s: s
    tok_f = pl.BlockSpec((c, hg * DN_HD), lambda g, s: (f_blk(s), g))
    tok_r = pl.BlockSpec((c, hg * DN_HD), lambda g, s: (r_blk(s), g))
    gate_f = pl.BlockSpec((None, hg, c, DN_HD), lambda g, s: (0, g, f_blk(s), 0))
    gate_r = pl.BlockSpec((None, hg, c, DN_HD), lambda g, s: (1, g, r_blk(s), 0))
    par = pl.BlockSpec((2, hg, 1, DN_HD), lambda g, s: (0, g, 0, 0))
    state = pl.BlockSpec((2, hg, None, DN_HD, DN_HD), lambda g, s: (0, g, st_blk(s), 0, 0))
    return tok_f, tok_r, gate_f, gate_r, par, state


def _dn_fwd(q, k, v, al, be, a_rows, dt_rows, name):
    t = q.shape[0]
    c, hg = DN_CHUNK, DN_HG
    nc = t // c
    tok_f, tok_r, gate_f, gate_r, par, state = _dn_views(nc, False)

    def body(qf, kf, vf, qr, kr, vr, alf, bef, alr, ber, a_ref, dt_ref, of_ref, or_ref, st_ref, s_ref):
        @pl.when(pl.program_id(1) == 0)
        def _():
            s_ref[...] = jnp.zeros_like(s_ref)

        (q_, k_, v_), gates = _dn_load((qf, kf, vf), (qr, kr, vr), alf, bef, alr, ber, a_ref, dt_ref)
        s = s_ref[...]
        st_ref[0] = s[:hg]
        st_ref[1] = s[hg:]
        out, s_new = _dn_chunk(q_, k_, v_, *gates, s)
        for h in range(hg):
            sl = slice(DN_HD * h, DN_HD * (h + 1))
            of_ref[:, sl] = out[h]
            or_ref[:, sl] = out[hg + h]
        s_ref[...] = s_new

    return pl.pallas_call(
        body, name=name, grid=(DN_HEADS // hg, nc),
        in_specs=[tok_f] * 3 + [tok_r] * 3 + [gate_f, gate_f, gate_r, gate_r, par, par],
        out_specs=[tok_f, tok_r, state],
        out_shape=[jax.ShapeDtypeStruct((t, TOK_W), F32)] * 2
        + [jax.ShapeDtypeStruct((2, DN_HEADS, nc, DN_HD, DN_HD), F32)],
        scratch_shapes=[pltpu.VMEM((2 * hg, DN_HD, DN_HD), F32)],
        compiler_params=_cparams(("parallel", "arbitrary")),
    )(q, k, v, q, k, v, al, be, al, be, a_rows, dt_rows)


def _dn_bwd(q, k, v, al, be, a_rows, dt_rows, states, do, name):
    t = q.shape[0]
    c, hg = DN_CHUNK, DN_HG
    nc = t // c
    tok_f, tok_r, gate_f, gate_r, par, state = _dn_views(nc, True)
    gout_f = pl.BlockSpec((hg, c, DN_HD), lambda g, s: (g, nc - 1 - s, 0))
    gout_r = pl.BlockSpec((hg, c, DN_HD), lambda g, s: (g, s, 0))

    def body(qf, kf, vf, qr, kr, vr, alf, bef, alr, ber, a_ref, dt_ref, st_ref, dof, dor,
             dqf, dkf, dvf, dqr, dkr, dvr, dalf, dbef, dalr, dber, da_ref, ddt_ref, ds_ref):
        first = pl.program_id(1) == 0

        @pl.when(first)
        def _():
            ds_ref[...] = jnp.zeros_like(ds_ref)
            da_ref[...] = jnp.zeros_like(da_ref)
            ddt_ref[...] = jnp.zeros_like(ddt_ref)

        def lanes(x):
            return jnp.sum(x, axis=-1, keepdims=True)

        (q_, k_, v_, do_), gates = _dn_load((qf, kf, vf, dof), (qr, kr, vr, dor), alf, bef, alr, ber, a_ref, dt_ref)
        s = jnp.concatenate([st_ref[0], st_ref[1]], axis=0)
        _, vjp = jax.vjp(_dn_chunk, q_, k_, v_, *gates, s)
        dq, dk, dv, dal, dbe, dalc, da, ddt, dac, ddtc, ds = vjp((do_, ds_ref[...]))
        for h in range(hg):
            sl = slice(DN_HD * h, DN_HD * (h + 1))
            dqf[:, sl], dkf[:, sl], dvf[:, sl] = dq[h], dk[h], dv[h]
            dqr[:, sl], dkr[:, sl], dvr[:, sl] = dq[hg + h], dk[hg + h], dv[hg + h]
        dal = jnp.broadcast_to(lanes(dal) + lanes(dalc), dal.shape)
        dbe = jnp.broadcast_to(lanes(dbe), dbe.shape)
        dalf[...], dalr[...] = dal[:hg], dal[hg:]
        dbef[...], dber[...] = dbe[:hg], dbe[hg:]
        da = jnp.broadcast_to(lanes(da) + lanes(dac), da.shape)
        ddt = jnp.broadcast_to(lanes(ddt) + lanes(ddtc), ddt.shape)
        da_ref[0] += da[:hg]
        da_ref[1] += da[hg:]
        ddt_ref[0] += ddt[:hg]
        ddt_ref[1] += ddt[hg:]
        ds_ref[...] = ds

    tok = jax.ShapeDtypeStruct((t, TOK_W), F32)
    gate = jax.ShapeDtypeStruct((DN_HEADS, t, DN_HD), F32)
    parsh = jax.ShapeDtypeStruct((2, DN_HEADS, 1, DN_HD), F32)
    res = pl.pallas_call(
        body, name=name, grid=(DN_HEADS // hg, nc),
        in_specs=[tok_f] * 3 + [tok_r] * 3 + [gate_f, gate_f, gate_r, gate_r, par, par, state, tok_f, tok_r],
        out_specs=[tok_f] * 3 + [tok_r] * 3 + [gout_f, gout_f, gout_r, gout_r, par, par],
        out_shape=[tok] * 6 + [gate] * 4 + [parsh] * 2,
        scratch_shapes=[pltpu.VMEM((2 * hg, DN_HD, DN_HD), F32)],
        compiler_params=_cparams(("parallel", "arbitrary")),
    )(q, k, v, q, k, v, al, be, al, be, a_rows, dt_rows, states, do, do)
    dqf, dkf, dvf, dqr, dkr, dvr, dalf, dbef, dalr, dber, da, ddt = res
    return (dqf, dkf, dvf), (dqr, dkr, dvr), jnp.stack([dalf, dalr]), jnp.stack([dbef, dber]), da, ddt


BAND_BQ = 256
ROW_TB = 256
MEM_TB = 512
CONV_RT = 512


def _to_sub(x, dil):
    l = x.shape[0] // dil
    return x.reshape(l, dil, 4, ATT_HD).transpose(2, 1, 0, 3).reshape(4 * dil, l, ATT_HD)


def _from_sub(x, dil):
    l = x.shape[1]
    return x.reshape(4, dil, l, ATT_HD).transpose(2, 1, 0, 3).reshape(l * dil, 4 * ATT_HD)


def _heads_major(x):
    return x.reshape(x.shape[0], MEM_HEADS, ATT_HD).transpose(1, 0, 2)


def _heads_minor(x):
    return x.transpose(1, 0, 2).reshape(x.shape[1], MEM_HEADS * ATT_HD)


def _mem_kv_fwd(mem, gain, w_kv, li):
    (memn,) = _rowwise(_fn_pre, [mem], [gain], [(D, BF16)], mem.shape[0], f"memnorm_fwd{li}")
    kv = _matmul(memn, w_kv, "nn", F32, f"memkv_fwd{li}")
    return _heads_major(kv[:, :MEM_W]), _heads_major(kv[:, MEM_W:]), memn


def _mem_kv_bwd(mem, gain, w_kv, memn, dkm, dvm, li):
    dkv = jnp.concatenate([_heads_minor(dkm), _heads_minor(dvm)], axis=1).astype(BF16)
    dw = _matmul(memn, dkv, "tn", BF16, f"memkv_dw{li}")
    dmemn = _matmul(dkv, w_kv, "nt", F32, f"memkv_dx{li}")
    _, (dgain,) = _rowwise_bwd(_fn_pre, [mem], [gain], [dmemn], [None], mem.shape[0], f"memnorm_bwd{li}")
    return dw, dgain


def _attn_mixer_fwd(p, rel_bias, km, vm):
    t = p.shape[0]
    saved, outs, lses = [], [], []
    for gi, (_, dil) in enumerate(DIL_GROUPS):
        l_sub = t // dil
        bq = min(BAND_BQ, l_sub)
        q = _to_sub(p[:, 256 * gi:256 * (gi + 1)], dil)
        pad = ((0, 0), (BAND_HALF, BAND_HALF), (0, 0))
        k = jnp.pad(_to_sub(p[:, TOK_W + 256 * gi:TOK_W + 256 * (gi + 1)], dil), pad)
        v = jnp.pad(_to_sub(p[:, 2 * TOK_W + 256 * gi:2 * TOK_W + 256 * (gi + 1)], dil), pad)
        bm = _band_bias(rel_bias, gi, dil, bq)
        o, lse = _band_fwd(q, k, v, bm, dil, l_sub, bq, f"band_fwd{gi}")
        outs.append(_from_sub(o, dil))
        lses.append(_from_sub(lse, dil))
        saved.append((q, k, v, bm))
    o_all = jnp.concatenate(outs, axis=1)
    lse_all = jnp.concatenate(lses, axis=1)
    (mixed,) = _rowwise(_fn_combine, [o_all, lse_all], [], [(TOK_W, BF16)], ROW_TB, "combine_fwd")
    qm = _heads_major(p[:, 3 * TOK_W:])
    memo = _mem_fwd(qm, km, vm, min(MEM_TB, t), "mem_fwd0")
    cat = jnp.concatenate([mixed, _heads_minor(memo).astype(BF16)], axis=1)
    return cat, (saved, o_all, lse_all, qm)


def _attn_mixer_bwd(dcat, res, km, vm):
    saved, o_all, lse_all, qm = res
    t = dcat.shape[0]
    (do_all, dlse_all), _ = _rowwise_bwd(_fn_combine, [o_all, lse_all], [], [dcat[:, :TOK_W]], [F32, F32],
                                         ROW_TB, "combine_bwd")
    dqs, dks, dvs, dbms = [], [], [], []
    for gi, (_, dil) in enumerate(DIL_GROUPS):
        l_sub = t // dil
        bq = min(BAND_BQ, l_sub)
        q, k, v, bm = saved[gi]
        do = _to_sub(do_all[:, 256 * gi:256 * (gi + 1)], dil)
        dl = _to_sub(dlse_all[:, 256 * gi:256 * (gi + 1)], dil)
        dq, dk, dv, dbm = _band_bwd(q, k, v, bm, do, dl, dil, l_sub, bq, f"band_bwd{gi}")
        dqs.append(_from_sub(dq, dil))
        dks.append(_from_sub(dk[:, BAND_HALF:-BAND_HALF], dil))
        dvs.append(_from_sub(dv[:, BAND_HALF:-BAND_HALF], dil))
        dbms.append(dbm)
    dqm, dkm, dvm = _mem_bwd(qm, km, vm, _heads_major(dcat[:, TOK_W:]), min(MEM_TB, t), "mem_bwd0")
    dp = jnp.concatenate(dqs + dks + dvs + [_heads_minor(dqm)], axis=1).astype(BF16)
    return dp, _relbias_grad(dbms, "relbias_grad"), dkm, dvm


def _dn_mixer_fwd(p, conv_w, a_log, dt_bias, out_norm, km, vm):
    t = p.shape[0]
    rt = min(CONV_RT, t)
    xp = jnp.pad(p[:, :3 * TOK_W], ((CONV_PAD, CONV_PAD), (0, 0)))
    w8 = jnp.pad(conv_w, ((0, 8 - DN_CONV), (0, 0)))
    q = _conv_fwd(xp, w8, 0, rt, "conv_fwd_q")
    k = _conv_fwd(xp, w8, 1, rt, "conv_fwd_k")
    v = _conv_fwd(xp, w8, 2, rt, "conv_fwd_v")
    gate = p[:, 4 * TOK_W:4 * TOK_W + 4 * DN_HEADS].reshape(t, 2, 2, DN_HEADS)
    bshape = (2, DN_HEADS, t, DN_HD)
    al = jnp.broadcast_to(gate[:, :, 0, :].transpose(1, 2, 0)[..., None], bshape)
    be = jnp.broadcast_to(gate[:, :, 1, :].transpose(1, 2, 0)[..., None], bshape)
    a_rows = jnp.broadcast_to(a_log[:, :, None, None], (2, DN_HEADS, 1, DN_HD))
    dt_rows = jnp.broadcast_to(dt_bias[:, :, None, None], (2, DN_HEADS, 1, DN_HD))
    o_f, o_r, states = _dn_fwd(q, k, v, al, be, a_rows, dt_rows, "dn_fwd")
    z = p[:, 3 * TOK_W:4 * TOK_W]
    gain = out_norm.reshape(1, DN_HD)
    (og,) = _rowwise(_fn_outnorm, [o_f, o_r, z], [gain], [(TOK_W, BF16)], ROW_TB, "outnorm_fwd")
    qm = _heads_major(p[:, 4 * TOK_W + 4 * DN_HEADS:DN_IN])
    memo = _mem_fwd(qm, km, vm, min(MEM_TB, t), "mem_fwd1")
    cat = jnp.concatenate([og, _heads_minor(memo).astype(BF16)], axis=1)
    return cat, (xp, w8, q, k, v, al, be, a_rows, dt_rows, o_f, o_r, states, z, gain, qm)


def _dn_mixer_bwd(dcat, res, km, vm):
    xp, w8, q, k, v, al, be, a_rows, dt_rows, o_f, o_r, states, z, gain, qm = res
    t = dcat.shape[0]
    rt = min(CONV_RT, t)
    (do, dz), (dgain,) = _rowwise_bwd(_fn_outnorm, [o_f, o_r, z], [gain], [dcat[:, :TOK_W]], [F32, None, F32],
                                      ROW_TB, "outnorm_bwd")
    d_f, d_r, dal, dbe, da, ddt = _dn_bwd(q, k, v, al, be, a_rows, dt_rows, states, do, "dn_bwd")
    dxs, dws = [], []
    for kind, nm in enumerate("qkv"):
        dx, dw = _conv_bwd(xp, w8, d_f[kind], d_r[kind], kind, rt, f"conv_bwd_{nm}")
        dxs.append(dx)
        dws.append(dw)
    dconv = jnp.concatenate(dws, axis=1)[:DN_CONV]
    dgate = jnp.stack([dal[..., 0], dbe[..., 0]], axis=1).transpose(3, 0, 1, 2).reshape(t, 4 * DN_HEADS)
    dqm, dkm, dvm = _mem_bwd(qm, km, vm, _heads_major(dcat[:, TOK_W:]), min(MEM_TB, t), "mem_bwd1")
    dp = jnp.concatenate(dxs + [dz, dgate, _heads_minor(dqm), jnp.zeros((t, DN_IN_PAD - DN_IN), F32)],
                         axis=1).astype(BF16)
    return dp, dconv, da[:, :, 0, 0], ddt[:, :, 0, 0], dgain.reshape(DN_HD), dkm, dvm


SWI_TB = 128


def _ffn_fwd(h, w_gu_t, w_d, li):
    gu = _matmul(h, w_gu_t, "nt", F32, f"ffn_gu{li}")
    (a,) = _rowwise(_fn_swiglu, [gu], [], [(D_FF, BF16)], SWI_TB, f"swiglu_fwd{li}")
    return _matmul(a, w_d, "nn", F32, f"ffn_down{li}"), gu, a


def _ffn_bwd(df, h, w_gu_t, w_d, gu, a, li):
    da = _matmul(df, w_d, "nt", F32, f"ffn_down_dx{li}")
    dwd = _matmul(a, df, "tn", BF16, f"ffn_down_dw{li}")
    (dgu,), _ = _rowwise_bwd(_fn_swiglu, [gu], [], [da], [BF16], SWI_TB, f"swiglu_bwd{li}")
    dh = _matmul(dgu, w_gu_t, "nn", F32, f"ffn_gu_dx{li}")
    dwgu_t = _matmul(dgu, h, "tn", BF16, f"ffn_gu_dw{li}")
    return dh, dwgu_t, dwd


def _fn_first(x, g):
    return x, _rms(x, g)


def _me_xyc():
    return lax.axis_index("x"), lax.axis_index("y"), lax.axis_index("c")


def _flip(coords, k):
    x, y, c = coords
    return (1 - x if k & 4 else x, 1 - y if k & 2 else y, 1 - c if k & 1 else c)


def _index(coords):
    x, y, c = coords
    return 4 * x + 2 * y + c


def _window(ref, axis, size, d):
    idx = [slice(None)] * len(ref.shape)
    idx[axis] = pl.ds(pl.multiple_of(d * size, size), size)
    return ref.at[tuple(idx)]


def _comm_call(body, n, ins, out_shapes, name):
    hbm = pl.BlockSpec(memory_space=pl.ANY)
    return pl.pallas_call(
        body, name=name, in_specs=[hbm] * n, out_specs=[hbm] * n, out_shape=out_shapes,
        scratch_shapes=[pltpu.SemaphoreType.DMA((N_DEV - 1, n)), pltpu.SemaphoreType.DMA((N_DEV - 1, n)),
                        pltpu.SemaphoreType.DMA((n,))],
    )(*ins)


def _run_exchange(n, local, remote, send_sems, recv_sems):
    me = _me_xyc()
    locs = [local(p) for p in range(n)]
    for cp in locs:
        cp.start()
    sends = [remote(k, p, me, _flip(me, k)) for k in range(1, N_DEV) for p in range(n)]
    for cp in sends:
        cp.start()
    for k in range(1, N_DEV):
        for p in range(n):
            remote(k, p, _flip(me, k), me).wait_recv()
    for cp in sends:
        cp.wait_send()
    for cp in locs:
        cp.wait()


def _all_gather(shards, axes, name):
    n = len(shards)
    sizes = [s.shape[a] for s, a in zip(shards, axes)]

    def body(*refs):
        ins, outs = refs[:n], refs[n:2 * n]
        send_sems, recv_sems, loc_sems = refs[2 * n:]
        me = _me_xyc()

        def local(p):
            return pltpu.make_async_copy(ins[p], _window(outs[p], axes[p], sizes[p], _index(me)), loc_sems.at[p])

        def remote(k, p, owner, to):
            return pltpu.make_async_remote_copy(
                src_ref=ins[p], dst_ref=_window(outs[p], axes[p], sizes[p], _index(owner)),
                send_sem=send_sems.at[k - 1, p], recv_sem=recv_sems.at[k - 1, p], device_id=to, device_id_type=MESH)

        _run_exchange(n, local, remote, send_sems, recv_sems)

    def full(s, a):
        return s.shape[:a] + (N_DEV * s.shape[a],) + s.shape[a + 1:]

    return _comm_call(body, n, shards, [jax.ShapeDtypeStruct(full(s, a), s.dtype) for s, a in zip(shards, axes)], name)


def _exchange(fulls, axes, name):
    n = len(fulls)
    sizes = [None if a is None else f.shape[a] // N_DEV for f, a in zip(fulls, axes)]

    def part_shape(f, a):
        return f.shape if a is None else f.shape[:a] + (f.shape[a] // N_DEV,) + f.shape[a + 1:]

    def body(*refs):
        ins, outs = refs[:n], refs[n:2 * n]
        send_sems, recv_sems, loc_sems = refs[2 * n:]
        me = _me_xyc()

        def src(p, to):
            return ins[p] if axes[p] is None else _window(ins[p], axes[p], sizes[p], _index(to))

        def local(p):
            return pltpu.make_async_copy(src(p, me), outs[p].at[_index(me)], loc_sems.at[p])

        def remote(k, p, sender, to):
            return pltpu.make_async_remote_copy(
                src_ref=src(p, to), dst_ref=outs[p].at[_index(sender)],
                send_sem=send_sems.at[k - 1, p], recv_sem=recv_sems.at[k - 1, p], device_id=to, device_id_type=MESH)

        _run_exchange(n, local, remote, send_sems, recv_sems)

    return _comm_call(body, n, fulls,
                      [jax.ShapeDtypeStruct((N_DEV,) + part_shape(f, a), f.dtype) for f, a in zip(fulls, axes)], name)


def _adam_math(g, w, m, v):
    m = ADAM_B1 * m + (1.0 - ADAM_B1) * g
    v = ADAM_B2 * v + (1.0 - ADAM_B2) * (g * g)
    m_hat = m / (1.0 - ADAM_B1 ** ADAM_STEP)
    v_hat = v / (1.0 - ADAM_B2 ** ADAM_STEP)
    delta = -ADAM_LR * (m_hat / (jnp.sqrt(v_hat) + ADAM_EPS) + ADAM_WD * w)
    return delta, m, v


def _sum_slabs(r_ref):
    g = r_ref[0].astype(F32)
    for s in range(1, N_DEV):
        g = g + r_ref[s].astype(F32)
    return g


def _adamw_reduce(recv, w, m, v, tb, name):
    r, c = w.shape

    def body(r_ref, w_ref, m_ref, v_ref, g_ref, d_ref, nm_ref, nv_ref):
        g = _sum_slabs(r_ref)
        g_ref[...] = g
        d_ref[...], nm_ref[...], nv_ref[...] = _adam_math(g, w_ref[...], m_ref[...], v_ref[...])

    blk = pl.BlockSpec((tb, c), lambda i: (i, 0))
    return pl.pallas_call(
        body, name=name, grid=(r // tb,),
        in_specs=[pl.BlockSpec((N_DEV, tb, c), lambda i: (0, i, 0)), blk, blk, blk],
        out_specs=[blk] * 4, out_shape=[jax.ShapeDtypeStruct((r, c), F32)] * 4,
        compiler_params=_cparams(("parallel",)),
    )(recv, w, m, v)


def _reduce8(recv, tb, name):
    r, c = recv.shape[1:]

    def body(r_ref, g_ref):
        g_ref[...] = _sum_slabs(r_ref)

    return pl.pallas_call(
        body, name=name, grid=(r // tb,),
        in_specs=[pl.BlockSpec((N_DEV, tb, c), lambda i: (0, i, 0))],
        out_specs=pl.BlockSpec((tb, c), lambda i: (i, 0)), out_shape=jax.ShapeDtypeStruct((r, c), F32),
        compiler_params=_cparams(("parallel",)),
    )(recv)


def _adamw(g, w, m, v, tb, name):
    r, c = w.shape

    def body(g_ref, w_ref, m_ref, v_ref, d_ref, nm_ref, nv_ref):
        d_ref[...], nm_ref[...], nv_ref[...] = _adam_math(g_ref[...], w_ref[...], m_ref[...], v_ref[...])

    blk = pl.BlockSpec((tb, c), lambda i: (i, 0))
    return pl.pallas_call(
        body, name=name, grid=(r // tb,), in_specs=[blk] * 4, out_specs=[blk] * 3,
        out_shape=[jax.ShapeDtypeStruct((r, c), F32)] * 3, compiler_params=_cparams(("parallel",)),
    )(g, w, m, v)


DN_IN_SHARD = DN_IN // N_DEV
DN_IN_SHARD_PAD = 432
CONV_SHARD = (1, DN_CONV, 288)


def _pack_small(arrs, rows):
    flat = jnp.concatenate([a.astype(F32).reshape(-1) for a in arrs])
    return jnp.pad(flat, (0, rows * PACK_C - flat.size)).reshape(rows, PACK_C)


def _unpack_small(packed, shapes):
    flat, out, off = packed.reshape(-1), [], 0
    for shp in shapes:
        n = int(np.prod(shp))
        out.append(flat[off:off + n].reshape(shp))
        off += n
    return out


def kernel(x, mem, rel_bias, att_w_in, att_w_out, dn_w_in, dn_conv, dn_a_log, dn_dt_bias, dn_out_norm, dn_w_out, mem_norm, mem_w_kv, norm_mix_pre, norm_mix_post, norm_ffn_pre, norm_ffn_post, ffn_w_gate_up, ffn_w_down, loss_target, m_rel_bias, m_att_w_in, m_att_w_out, m_dn_w_in, m_dn_conv, m_dn_a_log, m_dn_dt_bias, m_dn_out_norm, m_dn_w_out, m_mem_norm, m_mem_w_kv, m_norm_mix_pre, m_norm_mix_post, m_norm_ffn_pre, m_norm_ffn_post, m_ffn_w_gate_up, m_ffn_w_down, v_rel_bias, v_att_w_in, v_att_w_out, v_dn_w_in, v_dn_conv, v_dn_a_log, v_dn_dt_bias, v_dn_out_norm, v_dn_w_out, v_mem_norm, v_mem_w_kv, v_norm_mix_pre, v_norm_mix_post, v_norm_ffn_pre, v_norm_ffn_post, v_ffn_w_gate_up, v_ffn_w_down):
    x0, mem0, tgt = x[0], mem[0], loss_target[0]
    t = x0.shape[0]
    axes = ("x", "y", "c")

    def t_shard(w):
        return jnp.swapaxes(w, 1, 2).astype(BF16)

    dn_in_pad = ((0, 0), (0, DN_IN_SHARD_PAD - DN_IN_SHARD), (0, 0))
    shards = [t_shard(att_w_in), att_w_out.astype(BF16), jnp.pad(t_shard(dn_w_in), dn_in_pad),
              dn_w_out.astype(BF16), mem_w_kv.astype(BF16), t_shard(ffn_w_gate_up), ffn_w_down.astype(BF16), dn_conv]
    w_att_in_t, w_att_out, w_dn_in_g, w_dn_out, w_kv, w_gu_t, w_down, conv_g = _all_gather(
        shards, [1, 1, 1, 1, 1, 1, 1, 0], "allgather_weights")
    w_att_in_t, w_att_out, w_dn_out = w_att_in_t[0], w_att_out[0], w_dn_out[0]
    w_dn_in_t = jnp.concatenate(
        [w_dn_in_g[0, DN_IN_SHARD_PAD * j:DN_IN_SHARD_PAD * j + DN_IN_SHARD] for j in range(N_DEV)]
        + [jnp.zeros((DN_IN_PAD - DN_IN, D), BF16)], axis=0)
    conv_full = conv_g.transpose(1, 0, 2).reshape(DN_CONV, 3 * TOK_W)

    def gain(a, i):
        return a[i].reshape(1, D)

    (h0,) = _rowwise(_fn_pre, [x0], [gain(norm_mix_pre, 0)], [(D, BF16)], ROW_TB, "pre0")
    km0, vm0, memn0 = _mem_kv_fwd(mem0, gain(mem_norm, 0), w_kv[0], 0)
    p0 = _matmul(h0, w_att_in_t, "nt", F32, "att_in")
    cat0, res0 = _attn_mixer_fwd(p0, rel_bias, km0, vm0)
    y0 = _matmul(cat0, w_att_out, "nn", F32, "att_out")
    g_a = [gain(norm_mix_post, 0), gain(norm_ffn_pre, 0)]
    x1, h1 = _rowwise(_fn_res_pre, [x0, y0], g_a, [(D, F32), (D, BF16)], ROW_TB, "res_pre0")
    f0, gu0, a0 = _ffn_fwd(h1, w_gu_t[0], w_down[0], 0)
    g_b = [gain(norm_ffn_post, 0), gain(norm_mix_pre, 1)]
    x2, h2 = _rowwise(_fn_res_pre, [x1, f0], g_b, [(D, F32), (D, BF16)], ROW_TB, "res_pre1")
    km1, vm1, memn1 = _mem_kv_fwd(mem0, gain(mem_norm, 1), w_kv[1], 1)
    p1 = _matmul(h2, w_dn_in_t, "nt", F32, "dn_in")
    cat1, res1 = _dn_mixer_fwd(p1, conv_full, dn_a_log[0], dn_dt_bias[0], dn_out_norm[0], km1, vm1)
    y1 = _matmul(cat1, w_dn_out, "nn", F32, "dn_out")
    g_c = [gain(norm_mix_post, 1), gain(norm_ffn_pre, 1)]
    x3, h3 = _rowwise(_fn_res_pre, [x2, y1], g_c, [(D, F32), (D, BF16)], ROW_TB, "res_pre2")
    f1, gu1, a1 = _ffn_fwd(h3, w_gu_t[1], w_down[1], 1)
    g_d = [gain(norm_ffn_post, 1)]
    (x4,) = _rowwise(_fn_res, [x3, f1], g_d, [(D, F32)], ROW_TB, "res3")
    dx4, lrow = _loss_kernel(x4, tgt, ROW_TB, "loss")
    loss = lax.psum(lrow[0, 0] * (0.5 / D), axes)

    (df1,), (dg_fpost1,) = _rowwise_bwd(_fn_res, [x3, f1], g_d, [dx4], [None, BF16], ROW_TB, "res3_bwd")
    dh3, dwgu1, dwd1 = _ffn_bwd(df1, h3, w_gu_t[1], w_down[1], gu1, a1, 1)
    (dx2, dy1), (dg_mpost1, dg_fpre1) = _rowwise_bwd(_fn_res_pre, [x2, y1], g_c, [dx4, dh3], [F32, BF16],
                                                     ROW_TB, "res_pre2_bwd")
    dcat1 = _matmul(dy1, w_dn_out, "nt", F32, "dn_out_dx")
    dw_dn_out = _matmul(cat1, dy1, "tn", BF16, "dn_out_dw")
    dp1, dconv, da_log, ddt_bias, dout_norm, dkm1, dvm1 = _dn_mixer_bwd(dcat1, res1, km1, vm1)
    dwkv1, dg_mem1 = _mem_kv_bwd(mem0, gain(mem_norm, 1), w_kv[1], memn1, dkm1, dvm1, 1)
    dh2 = _matmul(dp1, w_dn_in_t, "nn", F32, "dn_in_dx")
    dw_dn_in_t = _matmul(dp1, h2, "tn", BF16, "dn_in_dw")
    (dx1, df0), (dg_fpost0, dg_mpre1) = _rowwise_bwd(_fn_res_pre, [x1, f0], g_b, [dx2, dh2], [F32, BF16],
                                                     ROW_TB, "res_pre1_bwd")
    dh1, dwgu0, dwd0 = _ffn_bwd(df0, h1, w_gu_t[0], w_down[0], gu0, a0, 0)
    (dx0, dy0), (dg_mpost0, dg_fpre0) = _rowwise_bwd(_fn_res_pre, [x0, y0], g_a, [dx1, dh1], [F32, BF16],
                                                     ROW_TB, "res_pre0_bwd")
    dcat0 = _matmul(dy0, w_att_out, "nt", F32, "att_out_dx")
    dw_att_out = _matmul(cat0, dy0, "tn", BF16, "att_out_dw")
    dp0, drel, dkm0, dvm0 = _attn_mixer_bwd(dcat0, res0, km0, vm0)
    dwkv0, dg_mem0 = _mem_kv_bwd(mem0, gain(mem_norm, 0), w_kv[0], memn0, dkm0, dvm0, 0)
    dh0 = _matmul(dp0, w_att_in_t, "nn", F32, "att_in_dx")
    dw_att_in_t = _matmul(dp0, h0, "tn", BF16, "att_in_dw")
    (grad_x,), (dg_mpre0,) = _rowwise_bwd(_fn_first, [x0], [gain(norm_mix_pre, 0)], [dx0, dh0], [F32],
                                          ROW_TB, "pre0_bwd")

    dn_in_parts = [jnp.pad(dw_dn_in_t[DN_IN_SHARD * j:DN_IN_SHARD * (j + 1)],
                           ((0, DN_IN_SHARD_PAD - DN_IN_SHARD), (0, 0))) for j in range(N_DEV)]
    small_grads = [drel, da_log, ddt_bias, dout_norm, jnp.concatenate([dg_mem0, dg_mem1]),
                   jnp.concatenate([dg_mpre0, dg_mpre1]), jnp.concatenate([dg_mpost0, dg_mpost1]),
                   jnp.concatenate([dg_fpre0, dg_fpre1]), jnp.concatenate([dg_fpost0, dg_fpost1]), dconv]
    fulls = [dw_att_in_t[None], dw_att_out[None], jnp.concatenate(dn_in_parts, axis=0)[None], dw_dn_out[None],
             jnp.stack([dwkv0, dwkv1]), jnp.stack([dwgu0, dwgu1]), jnp.stack([dwd0, dwd1]),
             _pack_small(small_grads, SMALL_ROWS)]
    r_att_in, r_att_out, r_dn_in, r_dn_out, r_kv, r_gu, r_down, r_small = _exchange(
        fulls, [1, 1, 1, 1, 1, 1, 1, None], "exchange_grads")

    def rows(a):
        return a.reshape((-1,) + a.shape[-1:])

    def row_sharded(recv, w, m, v, tb, name):
        outs = _adamw_reduce(recv.reshape((N_DEV, -1) + recv.shape[-1:]), rows(w), rows(m), rows(v), tb, name)
        return [o.reshape(w.shape) for o in outs]

    def col_sharded(recv, w, m, v, tb, name):
        g_t = _reduce8(recv.reshape((N_DEV, -1) + recv.shape[-1:]), tb, name + "_sum")
        g = jnp.swapaxes(g_t.reshape(recv.shape[1:])[:, :w.shape[2]], 1, 2)
        outs = _adamw(rows(g), rows(w), rows(m), rows(v), 256, name)
        return [g] + [o.reshape(w.shape) for o in outs]

    big = [col_sharded(r_att_in, att_w_in, m_att_w_in, v_att_w_in, 320, "adamw_att_in"),
           row_sharded(r_att_out, att_w_out, m_att_w_out, v_att_w_out, 128, "adamw_att_out"),
           col_sharded(r_dn_in, dn_w_in, m_dn_w_in, v_dn_w_in, 432, "adamw_dn_in"),
           row_sharded(r_dn_out, dn_w_out, m_dn_w_out, v_dn_w_out, 128, "adamw_dn_out"),
           row_sharded(r_kv, mem_w_kv, m_mem_w_kv, v_mem_w_kv, 256, "adamw_mem_kv"),
           col_sharded(r_gu, ffn_w_gate_up, m_ffn_w_gate_up, v_ffn_w_gate_up, 176, "adamw_ffn_gu"),
           row_sharded(r_down, ffn_w_down, m_ffn_w_down, v_ffn_w_down, 176, "adamw_ffn_down")]
    g_big, d_big, nm_big, nv_big = [[b[i] for b in big] for i in range(4)]

    g_small = _reduce8(r_small, SMALL_ROWS, "reduce_small")
    rep_shapes = [(32, 12), (1, 2, 6), (1, 2, 6), (1, 128), (2, D), (2, D), (2, D), (2, D), (2, D)]
    *g_rep, g_conv_full = _unpack_small(g_small, rep_shapes + [(DN_CONV, 3 * TOK_W)])
    me = _index(_me_xyc())
    g_conv = lax.dynamic_slice(g_conv_full, (0, me * 288), (DN_CONV, 288)).reshape(CONV_SHARD)
    small_shapes = rep_shapes + [CONV_SHARD]
    small_w = [rel_bias, dn_a_log, dn_dt_bias, dn_out_norm, mem_norm, norm_mix_pre, norm_mix_post,
               norm_ffn_pre, norm_ffn_post, dn_conv]
    small_m = [m_rel_bias, m_dn_a_log, m_dn_dt_bias, m_dn_out_norm, m_mem_norm, m_norm_mix_pre, m_norm_mix_post,
               m_norm_ffn_pre, m_norm_ffn_post, m_dn_conv]
    small_v = [v_rel_bias, v_dn_a_log, v_dn_dt_bias, v_dn_out_norm, v_mem_norm, v_norm_mix_pre, v_norm_mix_post,
               v_norm_ffn_pre, v_norm_ffn_post, v_dn_conv]
    g_small_list = g_rep + [g_conv]
    outs_small = _adamw(_pack_small(g_small_list, 24), _pack_small(small_w, 24), _pack_small(small_m, 24),
                        _pack_small(small_v, 24), 24, "adamw_small")
    d_small, nm_small, nv_small = [_unpack_small(o, small_shapes) for o in outs_small]

    def ordered(small, big):
        return [small[0], big[0], big[1], big[2], small[9], small[1], small[2], small[3], big[3], small[4],
                big[4], small[5], small[6], small[7], small[8], big[5], big[6]]

    g_small_out = [g.reshape(s) for g, s in zip(g_small_list, small_shapes)]
    return (loss, grad_x[None], *ordered(g_small_out, g_big), *ordered(d_small, d_big),
            *ordered(nm_small, nm_big), *ordered(nv_small, nv_big))
```

```python
import functools
import math

import numpy as np
import jax
import jax.numpy as jnp
from jax import lax
from jax.experimental import pallas as pl
from jax.experimental.pallas import tpu as pltpu

F32 = jnp.float32
BF16 = jnp.bfloat16
HI = lax.Precision.HIGHEST
MESH = pl.DeviceIdType.MESH

N_DEV = 8
D = 1024
EPS = 1e-6
NEG = -1e30
TOK_W = 768
MEM_W = 256
ATT_HD = 64
DIL_GROUPS = ((128, 1), (512, 4), (2048, 16))
BAND_HALF = 64
REL_BUCKETS = 32
REL_MAX_DIST = 1024
DN_HD = 128
DN_HEADS = 6
DN_CONV = 5
DN_CHUNK = 64
MEM_HEADS = 4
D_FF = 2816
ATT_IN = 2560
DN_IN = 3352
DN_IN_PAD = 3456

ADAM_LR, ADAM_B1, ADAM_B2, ADAM_EPS, ADAM_WD, ADAM_STEP = 0.001, 0.9, 0.999, 1e-08, 0.01, 10

PACK_C = 512
BIG_ROWS = 6480
SMALL_ROWS = 48
VMEM_LIMIT = 48 * 1024 * 1024


def _cparams(sem=None):
    kw = dict(vmem_limit_bytes=VMEM_LIMIT)
    if sem is not None:
        kw["dimension_semantics"] = sem
    return pltpu.CompilerParams(**kw)


def _tile(n, cap):
    if n <= cap:
        return n
    best = None
    for t in range(128, cap + 1, 128):
        if n % t == 0:
            best = t
    assert best is not None, (n, cap)
    return best


def _matmul(a, b, mode, out_dtype, name, tm=1024, tn=1408, tk=None):
    if tk is None:
        tk = 2048 if mode == "tn" else 2816
    if mode == "nn":
        (m, kc), (_, n) = a.shape, b.shape
        dims = (((1,), (0,)), ((), ()))
    elif mode == "nt":
        (m, kc), (n, _) = a.shape, b.shape
        dims = (((1,), (1,)), ((), ()))
    else:
        (kc, m), (_, n) = a.shape, b.shape
        dims = (((0,), (0,)), ((), ()))
    tm = m if m <= tm else _tile(m, tm)
    tn = _tile(n, tn)
    tk = _tile(kc, tk)
    nk = kc // tk

    def body(a_ref, b_ref, o_ref, acc_ref):
        k = pl.program_id(2)
        part = lax.dot_general(a_ref[...], b_ref[...], dims, preferred_element_type=F32)

        @pl.when(k == 0)
        def _():
            acc_ref[...] = part

        @pl.when(k > 0)
        def _():
            acc_ref[...] += part

        @pl.when(k == nk - 1)
        def _():
            o_ref[...] = acc_ref[...].astype(o_ref.dtype)

    if mode == "nn":
        a_spec = pl.BlockSpec((tm, tk), lambda i, j, k: (i, k))
        b_spec = pl.BlockSpec((tk, tn), lambda i, j, k: (k, j))
    elif mode == "nt":
        a_spec = pl.BlockSpec((tm, tk), lambda i, j, k: (i, k))
        b_spec = pl.BlockSpec((tn, tk), lambda i, j, k: (j, k))
    else:
        a_spec = pl.BlockSpec((tk, tm), lambda i, j, k: (k, i))
        b_spec = pl.BlockSpec((tk, tn), lambda i, j, k: (k, j))
    return pl.pallas_call(
        body, name=name, grid=(m // tm, n // tn, nk),
        in_specs=[a_spec, b_spec],
        out_specs=pl.BlockSpec((tm, tn), lambda i, j, k: (i, j)),
        out_shape=jax.ShapeDtypeStruct((m, n), out_dtype),
        scratch_shapes=[pltpu.VMEM((tm, tn), F32)],
        compiler_params=_cparams(("parallel", "parallel", "arbitrary")),
    )(a, b)


def _rowwise(fn, rows, params, outs, tb, name):
    t = rows[0].shape[0]
    nr, npar = len(rows), len(params)

    def body(*refs):
        ins = [r[...].astype(F32) for r in refs[:nr + npar]]
        res = fn(*ins)
        for o_ref, r in zip(refs[nr + npar:], res):
            o_ref[...] = r.astype(o_ref.dtype)

    return pl.pallas_call(
        body, name=name, grid=(t // tb,),
        in_specs=[pl.BlockSpec((tb, r.shape[1]), lambda i: (i, 0)) for r in rows]
        + [pl.BlockSpec(p.shape, lambda i: (0, 0)) for p in params],
        out_specs=[pl.BlockSpec((tb, c), lambda i: (i, 0)) for c, _ in outs],
        out_shape=[jax.ShapeDtypeStruct((t, c), dt) for c, dt in outs],
        compiler_params=_cparams(("parallel",)),
    )(*rows, *params)


def _rowwise_bwd(fn, rows, params, cots, row_grad, tb, name):
    t = rows[0].shape[0]
    nr, npar, nc = len(rows), len(params), len(cots)
    want = [i for i, g in enumerate(row_grad) if g is not None]

    def body(*refs):
        ins = [r[...].astype(F32) for r in refs[:nr + npar]]
        cts = tuple(r[...].astype(F32) for r in refs[nr + npar:nr + npar + nc])
        outs = refs[nr + npar + nc:]
        _, vjp = jax.vjp(fn, *ins)
        grads = vjp(cts)
        for o_ref, i in zip(outs[:len(want)], want):
            o_ref[...] = grads[i].astype(o_ref.dtype)
        first = pl.program_id(0) == 0
        for o_ref, g in zip(outs[len(want):], grads[nr:]):
            @pl.when(first)
            def _(o_ref=o_ref, g=g):
                o_ref[...] = g

            @pl.when(jnp.logical_not(first))
            def _(o_ref=o_ref, g=g):
                o_ref[...] += g

    res = pl.pallas_call(
        body, name=name, grid=(t // tb,),
        in_specs=[pl.BlockSpec((tb, r.shape[1]), lambda i: (i, 0)) for r in rows]
        + [pl.BlockSpec(p.shape, lambda i: (0, 0)) for p in params]
        + [pl.BlockSpec((tb, c.shape[1]), lambda i: (i, 0)) for c in cots],
        out_specs=[pl.BlockSpec((tb, rows[i].shape[1]), lambda i_: (i_, 0)) for i in want]
        + [pl.BlockSpec(p.shape, lambda i: (0, 0)) for p in params],
        out_shape=[jax.ShapeDtypeStruct(rows[i].shape, row_grad[i]) for i in want]
        + [jax.ShapeDtypeStruct(p.shape, F32) for p in params],
        compiler_params=_cparams(("arbitrary",)),
    )(*rows, *params, *cots)
    return list(res[:len(want)]), list(res[len(want):])


def _rms(x, g):
    return x * lax.rsqrt(jnp.mean(x * x, axis=-1, keepdims=True) + EPS) * g


def _fn_pre(x, g):
    return (_rms(x, g),)


def _fn_res_pre(x, y, g_post, g_pre):
    x1 = x + _rms(y, g_post)
    return x1, _rms(x1, g_pre)


def _fn_res(x, y, g_post):
    return (x + _rms(y, g_post),)


def _sigmoid(x):
    return 1.0 / (1.0 + jnp.exp(-x))


def _silu(x):
    return x * _sigmoid(x)


def _fn_swiglu(gu):
    return (_silu(gu[:, :D_FF]) * gu[:, D_FF:],)


def _fn_combine(o, lse):
    ls = [lse[:, 256 * g:256 * (g + 1)] for g in range(3)]
    mx = lax.stop_gradient(jnp.maximum(jnp.maximum(ls[0], ls[1]), ls[2]))
    es = [jnp.exp(l - mx) for l in ls]
    inv = 1.0 / (es[0] + es[1] + es[2])
    return (jnp.concatenate([o[:, 256 * g:256 * (g + 1)] * (es[g] * inv) for g in range(3)], axis=1),)


def _fn_outnorm(o_f, o_r, z, gain):
    res = []
    for h in range(DN_HEADS):
        sl = slice(DN_HD * h, DN_HD * (h + 1))
        o = o_f[:, sl] + o_r[:, sl]
        res.append(o * lax.rsqrt(jnp.mean(o * o, axis=-1, keepdims=True) + EPS) * gain * _silu(z[:, sl]))
    return (jnp.concatenate(res, axis=1),)


def _loss_kernel(x, tgt, tb, name):
    t, d = x.shape

    def body(x_ref, t_ref, dx_ref, l_ref, acc_ref):
        i = pl.program_id(0)
        e = x_ref[...] - t_ref[...]
        dx_ref[...] = e * (1.0 / d)
        part = jnp.sum(e * e, axis=0, keepdims=True)

        @pl.when(i == 0)
        def _():
            acc_ref[...] = part

        @pl.when(i > 0)
        def _():
            acc_ref[...] += part

        @pl.when(i == t // tb - 1)
        def _():
            l_ref[...] = jnp.broadcast_to(jnp.sum(acc_ref[...], axis=-1, keepdims=True), (1, 128))

    return pl.pallas_call(
        body, name=name, grid=(t // tb,),
        in_specs=[pl.BlockSpec((tb, d), lambda i: (i, 0))] * 2,
        out_specs=[pl.BlockSpec((tb, d), lambda i: (i, 0)), pl.BlockSpec((1, 128), lambda i: (0, 0))],
        out_shape=[jax.ShapeDtypeStruct((t, d), F32), jax.ShapeDtypeStruct((1, 128), F32)],
        scratch_shapes=[pltpu.VMEM((1, d), F32)],
        compiler_params=_cparams(("arbitrary",)),
    )(x, tgt)


def _band_fn(l_sub, bq, i, q, kw, vw, bm):
    w = bq + 2 * BAND_HALF
    s = lax.dot_general((q * (ATT_HD ** -0.5)).astype(BF16), kw.astype(BF16), (((1,), (1,)), ((), ())),
                        preferred_element_type=F32) + bm
    kpos = i * bq - BAND_HALF + lax.broadcasted_iota(jnp.int32, (bq, w), 1)
    s = jnp.where((kpos >= 0) & (kpos < l_sub), s, NEG)
    m = lax.stop_gradient(jnp.max(s, axis=-1, keepdims=True))
    p = jnp.exp(s - m)
    den = jnp.sum(p, axis=-1, keepdims=True)
    o = jnp.dot(p.astype(BF16), vw.astype(BF16), preferred_element_type=F32) / den
    return o, jnp.broadcast_to(m + jnp.log(den), o.shape)


def _band_specs(dil, l_sub, bq):
    w = bq + 2 * BAND_HALF
    qs = pl.BlockSpec((None, bq, ATT_HD), lambda h, r, i: (h * dil + r, i, 0))
    ks = pl.BlockSpec((None, l_sub + 2 * BAND_HALF, ATT_HD), lambda h, r, i: (h * dil + r, 0, 0))
    bs = pl.BlockSpec((None, bq, w), lambda h, r, i: (h, 0, 0))
    return qs, ks, bs


def _band_fwd(q, k, v, bm, dil, l_sub, bq, name):
    w = bq + 2 * BAND_HALF
    qs, ks, bs = _band_specs(dil, l_sub, bq)

    def body(q_ref, k_ref, v_ref, bm_ref, o_ref, l_ref):
        i = pl.program_id(2)
        st = pl.multiple_of(i * bq, bq)
        o, lse = _band_fn(l_sub, bq, i, q_ref[...].astype(F32), k_ref[pl.ds(st, w), :].astype(F32),
                          v_ref[pl.ds(st, w), :].astype(F32), bm_ref[...])
        o_ref[...] = o
        l_ref[...] = lse

    return pl.pallas_call(
        body, name=name, grid=(4, dil, l_sub // bq),
        in_specs=[qs, ks, ks, bs], out_specs=[qs, qs],
        out_shape=[jax.ShapeDtypeStruct(q.shape, F32)] * 2,
        compiler_params=_cparams(("parallel", "parallel", "arbitrary")),
    )(q, k, v, bm)


def _band_bwd(q, k, v, bm, do, dlse, dil, l_sub, bq, name):
    w = bq + 2 * BAND_HALF
    qs, ks, bs = _band_specs(dil, l_sub, bq)

    def body(q_ref, k_ref, v_ref, bm_ref, do_ref, dl_ref, dq_ref, dk_ref, dv_ref, dbm_ref):
        r, i = pl.program_id(1), pl.program_id(2)
        st = pl.multiple_of(i * bq, bq)
        _, vjp = jax.vjp(functools.partial(_band_fn, l_sub, bq, i),
                         q_ref[...].astype(F32), k_ref[pl.ds(st, w), :].astype(F32),
                         v_ref[pl.ds(st, w), :].astype(F32), bm_ref[...])
        dq, dkw, dvw, dbm = vjp((do_ref[...], dl_ref[...]))
        dq_ref[...] = dq

        @pl.when(i == 0)
        def _():
            dk_ref[...] = jnp.zeros_like(dk_ref)
            dv_ref[...] = jnp.zeros_like(dv_ref)

        dk_ref[pl.ds(st, w), :] += dkw
        dv_ref[pl.ds(st, w), :] += dvw

        @pl.when((i == 0) & (r == 0))
        def _():
            dbm_ref[...] = dbm

        @pl.when((i > 0) | (r > 0))
        def _():
            dbm_ref[...] += dbm

    return pl.pallas_call(
        body, name=name, grid=(4, dil, l_sub // bq),
        in_specs=[qs, ks, ks, bs, qs, qs], out_specs=[qs, ks, ks, bs],
        out_shape=[jax.ShapeDtypeStruct(q.shape, F32), jax.ShapeDtypeStruct(k.shape, F32),
                   jax.ShapeDtypeStruct(k.shape, F32), jax.ShapeDtypeStruct(bm.shape, F32)],
        compiler_params=_cparams(("parallel", "arbitrary", "arbitrary")),
    )(q, k, v, bm, do, dlse)


def _t5_bucket(rel):
    half = REL_BUCKETS // 2
    max_exact = half // 2
    n = np.abs(rel)
    large = max_exact + (np.log(np.maximum(n, 1) / max_exact) / math.log(REL_MAX_DIST / max_exact)
                         * (half - max_exact)).astype(np.int64)
    large = np.minimum(large, half - 1)
    return ((rel > 0) * half + np.where(n < max_exact, n, large)).astype(np.int32)


def _bucket_onehot(dil):
    idx = _t5_bucket(np.arange(-BAND_HALF, BAND_HALF + 1) * dil)
    oh = np.zeros((2 * BAND_HALF + 1, REL_BUCKETS), np.float32)
    oh[np.arange(2 * BAND_HALF + 1), idx] = 1.0
    return oh


def _band_bias(rel_bias, gi, dil, bq):
    w = bq + 2 * BAND_HALF
    nb = 2 * BAND_HALF + 1
    bias = jnp.dot(jnp.asarray(_bucket_onehot(dil)), rel_bias[:, 4 * gi:4 * gi + 4], precision=HI)
    row = jnp.concatenate([bias.T, jnp.full((4, w + 1 - nb), NEG, F32)], axis=1)
    flat = jnp.tile(row, (1, bq))[:, :bq * w]
    return flat.reshape(4, bq, w)


def _relbias_grad(dbms, name):
    nb = 2 * BAND_HALF + 1
    bq = max(d.shape[1] for d in dbms)
    skew = []
    for dbm in dbms:
        bqg, w = dbm.shape[1], dbm.shape[2]
        flat = jnp.pad(dbm.reshape(4, bqg * w), ((0, 0), (0, bqg)))
        skew.append(jnp.pad(flat.reshape(4, bqg, w + 1)[:, :, :nb], ((0, 0), (0, bq - bqg), (0, 256 - nb))))
    sk = jnp.concatenate(skew, axis=0)
    oh = np.zeros((3, 256, 128), np.float32)
    for gi, (_, dil) in enumerate(DIL_GROUPS):
        oh[gi, :2 * BAND_HALF + 1, :REL_BUCKETS] = _bucket_onehot(dil)

    def body(s_ref, oh_ref, o_ref):
        col = jnp.sum(s_ref[...], axis=0, keepdims=True)
        o_ref[...] = jnp.dot(jnp.broadcast_to(col, (8, 256)), oh_ref[...], precision=HI, preferred_element_type=F32)

    out = pl.pallas_call(
        body, name=name, grid=(12,),
        in_specs=[pl.BlockSpec((None, bq, 256), lambda n: (n, 0, 0)),
                  pl.BlockSpec((None, 256, 128), lambda n: (n // 4, 0, 0))],
        out_specs=pl.BlockSpec((None, 8, 128), lambda n: (n, 0, 0)),
        out_shape=jax.ShapeDtypeStruct((12, 8, 128), F32),
        compiler_params=_cparams(("parallel",)),
    )(sk, jnp.asarray(oh))
    return out[:, 0, :REL_BUCKETS].T


def _mem_fn(q, k, v):
    s = lax.dot_general((q * (ATT_HD ** -0.5)).astype(BF16), k.astype(BF16), (((1,), (1,)), ((), ())),
                        preferred_element_type=F32)
    m = lax.stop_gradient(jnp.max(s, axis=-1, keepdims=True))
    p = jnp.exp(s - m)
    p = p / jnp.sum(p, axis=-1, keepdims=True)
    return jnp.dot(p.astype(BF16), v.astype(BF16), preferred_element_type=F32)


def _mem_specs(tb, ml):
    qs = pl.BlockSpec((None, tb, ATT_HD), lambda h, i: (h, i, 0))
    ks = pl.BlockSpec((None, ml, ATT_HD), lambda h, i: (h, 0, 0))
    return qs, ks


def _mem_fwd(q, k, v, tb, name):
    qs, ks = _mem_specs(tb, k.shape[1])

    def body(q_ref, k_ref, v_ref, o_ref):
        o_ref[...] = _mem_fn(q_ref[...].astype(F32), k_ref[...], v_ref[...])

    return pl.pallas_call(
        body, name=name, grid=(MEM_HEADS, q.shape[1] // tb),
        in_specs=[qs, ks, ks], out_specs=qs, out_shape=jax.ShapeDtypeStruct(q.shape, F32),
        compiler_params=_cparams(("parallel", "parallel")),
    )(q, k, v)


def _mem_bwd(q, k, v, do, tb, name):
    qs, ks = _mem_specs(tb, k.shape[1])

    def body(q_ref, k_ref, v_ref, do_ref, dq_ref, dk_ref, dv_ref):
        i = pl.program_id(1)
        _, vjp = jax.vjp(_mem_fn, q_ref[...].astype(F32), k_ref[...], v_ref[...])
        dq, dk, dv = vjp(do_ref[...])
        dq_ref[...] = dq

        @pl.when(i == 0)
        def _():
            dk_ref[...] = dk
            dv_ref[...] = dv

        @pl.when(i > 0)
        def _():
            dk_ref[...] += dk
            dv_ref[...] += dv

    return pl.pallas_call(
        body, name=name, grid=(MEM_HEADS, q.shape[1] // tb),
        in_specs=[qs, ks, ks, qs], out_specs=[qs, ks, ks],
        out_shape=[jax.ShapeDtypeStruct(q.shape, F32), jax.ShapeDtypeStruct(k.shape, F32),
                   jax.ShapeDtypeStruct(k.shape, F32)],
        compiler_params=_cparams(("parallel", "arbitrary")),
    )(q, k, v, do)


CONV_PAD = 8


def _conv_post(kind, acc):
    s = _silu(acc)
    if kind == 2:
        return s
    scale = DN_HD ** -0.5 if kind == 0 else 1.0
    return s * lax.rsqrt(jnp.sum(s * s, axis=-1, keepdims=True) + EPS) * scale


def _conv_acc(xp_ref, w, r0, rt):
    acc = None
    for i in range(DN_CONV):
        term = w[i:i + 1, :] * xp_ref[pl.ds(CONV_PAD + r0 + i - DN_CONV // 2, rt), :]
        acc = term if acc is None else acc + term
    return acc


def _conv_fwd(xp, w8, kind, rt, name):
    t = xp.shape[0] - 2 * CONV_PAD

    def body(xp_ref, w_ref, o_ref):
        w = w_ref[...]
        for r in range(t // rt):
            o_ref[pl.ds(r * rt, rt), :] = _conv_post(kind, _conv_acc(xp_ref, w, r * rt, rt))

    return pl.pallas_call(
        body, name=name, grid=(DN_HEADS,),
        in_specs=[pl.BlockSpec((t + 2 * CONV_PAD, DN_HD), lambda j: (0, 6 * kind + j)),
                  pl.BlockSpec((8, DN_HD), lambda j: (0, 6 * kind + j))],
        out_specs=pl.BlockSpec((t, DN_HD), lambda j: (0, j)),
        out_shape=jax.ShapeDtypeStruct((t, TOK_W), F32),
        compiler_params=_cparams(("parallel",)),
    )(xp, w8)


def _conv_bwd(xp, w8, d_f, d_r, kind, rt, name):
    t = xp.shape[0] - 2 * CONV_PAD

    def body(xp_ref, w_ref, df_ref, dr_ref, dx_ref, dw_ref, dpad_ref):
        w = w_ref[...]
        zero = jnp.zeros((CONV_PAD, DN_HD), F32)
        dpad_ref[pl.ds(0, CONV_PAD), :] = zero
        dpad_ref[pl.ds(CONV_PAD + t, CONV_PAD), :] = zero
        dw = [jnp.zeros((1, DN_HD), F32) for _ in range(DN_CONV)]
        for r in range(t // rt):
            rows = pl.ds(r * rt, rt)
            acc = _conv_acc(xp_ref, w, r * rt, rt)
            _, vjp = jax.vjp(functools.partial(_conv_post, kind), acc)
            (dacc,) = vjp(df_ref[rows, :] + dr_ref[rows, :])
            dpad_ref[pl.ds(CONV_PAD + r * rt, rt), :] = dacc
            for i in range(DN_CONV):
                xs = xp_ref[pl.ds(CONV_PAD + r * rt + i - DN_CONV // 2, rt), :]
                dw[i] = dw[i] + jnp.sum(dacc * xs, axis=0, keepdims=True)
        dw_ref[...] = jnp.concatenate(dw + [jnp.zeros((8 - DN_CONV, DN_HD), F32)], axis=0)
        for r in range(t // rt):
            acc = None
            for i in range(DN_CONV):
                term = w[i:i + 1, :] * dpad_ref[pl.ds(CONV_PAD + r * rt - i + DN_CONV // 2, rt), :]
                acc = term if acc is None else acc + term
            dx_ref[pl.ds(r * rt, rt), :] = acc

    return pl.pallas_call(
        body, name=name, grid=(DN_HEADS,),
        in_specs=[pl.BlockSpec((t + 2 * CONV_PAD, DN_HD), lambda j: (0, 6 * kind + j)),
                  pl.BlockSpec((8, DN_HD), lambda j: (0, 6 * kind + j)),
                  pl.BlockSpec((t, DN_HD), lambda j: (0, j)),
                  pl.BlockSpec((t, DN_HD), lambda j: (0, j))],
        out_specs=[pl.BlockSpec((t, DN_HD), lambda j: (0, j)), pl.BlockSpec((8, DN_HD), lambda j: (0, j))],
        out_shape=[jax.ShapeDtypeStruct((t, TOK_W), F32), jax.ShapeDtypeStruct((8, TOK_W), F32)],
        scratch_shapes=[pltpu.VMEM((t + 2 * CONV_PAD, DN_HD), F32)],
        compiler_params=_cparams(("parallel",)),
    )(xp, w8, d_f, d_r)


def _softplus(x):
    e = jnp.exp(-jnp.abs(x))
    return jnp.maximum(x, 0.0) + jnp.where(e < 1e-4, e - 0.5 * e * e, jnp.log(1.0 + e))


_NN = (((2,), (1,)), ((0,), (0,)))
_NT = (((2,), (2,)), ((0,), (0,)))
_TN = (((1,), (1,)), ((0,), (0,)))


def _dot(a, b, dims=_NN):
    return lax.dot_general(a.astype(BF16), b.astype(BF16), dims, preferred_element_type=F32)


def _hi_lo(x):
    hi = x.astype(BF16)
    return hi, (x - hi.astype(F32)).astype(BF16)


def _mask_dot(mask_bf16, x, dims):
    x1 = x.astype(BF16)
    r = x - x1.astype(F32)
    x2, x3 = _hi_lo(r)
    d = functools.partial(lax.dot_general, dimension_numbers=dims, preferred_element_type=F32)
    return d(mask_bf16, x1) + d(mask_bf16, x2) + d(mask_bf16, x3)


@jax.custom_vjp
def _dot_mask(mask_bf16, x):
    return _mask_dot(mask_bf16, x, _NN)


def _dot_mask_fwd(mask_bf16, x):
    return _mask_dot(mask_bf16, x, _NN), mask_bf16


def _dot_mask_bwd(mask_bf16, ct):
    return jnp.zeros_like(mask_bf16), _mask_dot(mask_bf16, ct, _TN)


_dot_mask.defvjp(_dot_mask_fwd, _dot_mask_bwd)


def _dot3_raw(a, b, dims):
    a1, a2 = _hi_lo(a)
    b1, b2 = _hi_lo(b)
    d = functools.partial(lax.dot_general, dimension_numbers=dims, preferred_element_type=F32)
    return d(a1, b1) + d(a1, b2) + d(a2, b1)


@jax.custom_vjp
def _dot3(a, b):
    return _dot3_raw(a, b, _NN)


def _dot3_fwd(a, b):
    return _dot3_raw(a, b, _NN), (a, b)


def _dot3_bwd(res, ct):
    a, b = res
    return _dot3_raw(ct, b, _NT), _dot3_raw(a, ct, _TN)


_dot3.defvjp(_dot3_fwd, _dot3_bwd)


def _dn_chunk(q, k, v, al, be, alc, a_row, dt_row, a_rowc, dt_rowc, s):
    n, c = q.shape[0], DN_CHUNK
    rev = lax.broadcasted_iota(jnp.int32, (n, c, c), 0) >= n // 2
    row = lax.broadcasted_iota(jnp.int32, (n, c, c), 1)
    col = lax.broadcasted_iota(jnp.int32, (n, c, c), 2)
    ahead = jnp.where(rev, col - row, row - col)
    incl = ahead >= 0
    strict = ahead > 0
    incl_b = incl.astype(BF16)

    g = -jnp.exp(a_row) * _softplus(al + dt_row)
    beta = _sigmoid(be)
    g_c = -jnp.exp(a_rowc) * _softplus(alc + dt_rowc)
    gc = _dot_mask(incl_b, g)
    gcc = _dot_mask(incl_b, g_c)
    decay = jnp.exp(jnp.where(incl, gcc - jnp.swapaxes(gcc, 1, 2), NEG))
    kb = k * beta
    lmat = jnp.where(strict, _dot(kb, k, _NT) * decay, 0.0)
    rhs = jnp.concatenate([v * beta, kb * jnp.exp(gc)], axis=2)
    xp = -lmat
    sol = rhs + _dot3(xp, rhs)
    for _ in range(5):
        xp = _dot3(xp, xp)
        sol = sol + _dot3(xp, sol)
    u, w = sol[:, :, :DN_HD], sol[:, :, DN_HD:]
    intra = jnp.where(incl, _dot(q, k, _NT) * decay, 0.0)
    v_new = u - _dot(w, s)
    out = _dot(q * jnp.exp(gc), s) + _dot(intra, v_new)
    g_last = jnp.sum(g, axis=1, keepdims=True)
    s_new = s * jnp.exp(g_last) + _dot(k * jnp.exp(g_last - gc), v_new, _TN)
    return out, s_new


DN_HG = 6


def _dn_load(f_refs, r_refs, alf, bef, alr, ber, a_ref, dt_ref):
    c, hg = DN_CHUNK, DN_HG
    sls = [slice(DN_HD * h, DN_HD * (h + 1)) for h in range(hg)]
    toks = [jnp.stack([f[:, sl] for sl in sls] + [r[:, sl] for sl in sls]) for f, r in zip(f_refs, r_refs)]
    al = jnp.concatenate([alf[...], alr[...]], axis=0)
    be = jnp.concatenate([bef[...], ber[...]], axis=0)
    alc = jnp.concatenate([alf[:, :, 0:c], alr[:, :, 0:c]], axis=0)
    a = jnp.concatenate([a_ref[0], a_ref[1]], axis=0)
    dt = jnp.concatenate([dt_ref[0], dt_ref[1]], axis=0)
    ac = jnp.concatenate([a_ref[0, :, :, 0:c], a_ref[1, :, :, 0:c]], axis=0)
    dtc = jnp.concatenate([dt_ref[0, :, :, 0:c], dt_ref[1, :, :, 0:c]], axis=0)
    return toks, (al, be, alc, a, dt, ac, dtc)


def _dn_views(nc, bwd):
    c, hg = DN_CHUNK, DN_HG
    if bwd:
        f_blk = lambda s: nc - 1 - s
        r_blk = lambda s: s
        st_blk = lambda s: nc - 1 - s
    else:
        f_blk = lambda s: s
        r_blk = lambda s: nc - 1 - s
        st_blk = lambda s: s
    tok_f = pl.BlockSpec((c, hg * DN_HD), lambda g, s: (f_blk(s), g))
    tok_r = pl.BlockSpec((c, hg * DN_HD), lambda g, s: (r_blk(s), g))
    gate_f = pl.BlockSpec((None, hg, c, DN_HD), lambda g, s: (0, g, f_blk(s), 0))
    gate_r = pl.BlockSpec((None, hg, c, DN_HD), lambda g, s: (1, g, r_blk(s), 0))
    par = pl.BlockSpec((2, hg, 1, DN_HD), lambda g, s: (0, g, 0, 0))
    state = pl.BlockSpec((2, hg, None, DN_HD, DN_HD), lambda g, s: (0, g, st_blk(s), 0, 0))
    return tok_f, tok_r, gate_f, gate_r, par, state


def _dn_fwd(q, k, v, al, be, a_rows, dt_rows, name):
    t = q.shape[0]
    c, hg = DN_CHUNK, DN_HG
    nc = t // c
    tok_f, tok_r, gate_f, gate_r, par, state = _dn_views(nc, False)

    def body(qf, kf, vf, qr, kr, vr, alf, bef, alr, ber, a_ref, dt_ref, of_ref, or_ref, st_ref, s_ref):
        @pl.when(pl.program_id(1) == 0)
        def _():
            s_ref[...] = jnp.zeros_like(s_ref)

        (q_, k_, v_), gates = _dn_load((qf, kf, vf), (qr, kr, vr), alf, bef, alr, ber, a_ref, dt_ref)
        s = s_ref[...]
        st_ref[0] = s[:hg]
        st_ref[1] = s[hg:]
        out, s_new = _dn_chunk(q_, k_, v_, *gates, s)
        for h in range(hg):
            sl = slice(DN_HD * h, DN_HD * (h + 1))
            of_ref[:, sl] = out[h]
            or_ref[:, sl] = out[hg + h]
        s_ref[...] = s_new

    return pl.pallas_call(
        body, name=name, grid=(DN_HEADS // hg, nc),
        in_specs=[tok_f] * 3 + [tok_r] * 3 + [gate_f, gate_f, gate_r, gate_r, par, par],
        out_specs=[tok_f, tok_r, state],
        out_shape=[jax.ShapeDtypeStruct((t, TOK_W), F32)] * 2
        + [jax.ShapeDtypeStruct((2, DN_HEADS, nc, DN_HD, DN_HD), F32)],
        scratch_shapes=[pltpu.VMEM((2 * hg, DN_HD, DN_HD), F32)],
        compiler_params=_cparams(("parallel", "arbitrary")),
    )(q, k, v, q, k, v, al, be, al, be, a_rows, dt_rows)


def _dn_bwd(q, k, v, al, be, a_rows, dt_rows, states, do, name):
    t = q.shape[0]
    c, hg = DN_CHUNK, DN_HG
    nc = t // c
    tok_f, tok_r, gate_f, gate_r, par, state = _dn_views(nc, True)
    gout_f = pl.BlockSpec((hg, c, DN_HD), lambda g, s: (g, nc - 1 - s, 0))
    gout_r = pl.BlockSpec((hg, c, DN_HD), lambda g, s: (g, s, 0))

    def body(qf, kf, vf, qr, kr, vr, alf, bef, alr, ber, a_ref, dt_ref, st_ref, dof, dor,
             dqf, dkf, dvf, dqr, dkr, dvr, dalf, dbef, dalr, dber, da_ref, ddt_ref, ds_ref):
        first = pl.program_id(1) == 0

        @pl.when(first)
        def _():
            ds_ref[...] = jnp.zeros_like(ds_ref)
            da_ref[...] = jnp.zeros_like(da_ref)
            ddt_ref[...] = jnp.zeros_like(ddt_ref)

        def lanes(x):
            return jnp.sum(x, axis=-1, keepdims=True)

        (q_, k_, v_, do_), gates = _dn_load((qf, kf, vf, dof), (qr, kr, vr, dor), alf, bef, alr, ber, a_ref, dt_ref)
        s = jnp.concatenate([st_ref[0], st_ref[1]], axis=0)
        _, vjp = jax.vjp(_dn_chunk, q_, k_, v_, *gates, s)
        dq, dk, dv, dal, dbe, dalc, da, ddt, dac, ddtc, ds = vjp((do_, ds_ref[...]))
        for h in range(hg):
            sl = slice(DN_HD * h, DN_HD * (h + 1))
            dqf[:, sl], dkf[:, sl], dvf[:, sl] = dq[h], dk[h], dv[h]
            dqr[:, sl], dkr[:, sl], dvr[:, sl] = dq[hg + h], dk[hg + h], dv[hg + h]
        dal = jnp.broadcast_to(lanes(dal) + lanes(dalc), dal.shape)
        dbe = jnp.broadcast_to(lanes(dbe), dbe.shape)
        dalf[...], dalr[...] = dal[:hg], dal[hg:]
        dbef[...], dber[...] = dbe[:hg], dbe[hg:]
        da = jnp.broadcast_to(lanes(da) + lanes(dac), da.shape)
        ddt = jnp.broadcast_to(lanes(ddt) + lanes(ddtc), ddt.shape)
        da_ref[0] += da[:hg]
        da_ref[1] += da[hg:]
        ddt_ref[0] += ddt[:hg]
        ddt_ref[1] += ddt[hg:]
        ds_ref[...] = ds

    tok = jax.ShapeDtypeStruct((t, TOK_W), F32)
    gate = jax.ShapeDtypeStruct((DN_HEADS, t, DN_HD), F32)
    parsh = jax.ShapeDtypeStruct((2, DN_HEADS, 1, DN_HD), F32)
    res = pl.pallas_call(
        body, name=name, grid=(DN_HEADS // hg, nc),
        in_specs=[tok_f] * 3 + [tok_r] * 3 + [gate_f, gate_f, gate_r, gate_r, par, par, state, tok_f, tok_r],
        out_specs=[tok_f] * 3 + [tok_r] * 3 + [gout_f, gout_f, gout_r, gout_r, par, par],
        out_shape=[tok] * 6 + [gate] * 4 + [parsh] * 2,
        scratch_shapes=[pltpu.VMEM((2 * hg, DN_HD, DN_HD), F32)],
        compiler_params=_cparams(("parallel", "arbitrary")),
    )(q, k, v, q, k, v, al, be, al, be, a_rows, dt_rows, states, do, do)
    dqf, dkf, dvf, dqr, dkr, dvr, dalf, dbef, dalr, dber, da, ddt = res
    return (dqf, dkf, dvf), (dqr, dkr, dvr), jnp.stack([dalf, dalr]), jnp.stack([dbef, dber]), da, ddt


BAND_BQ = 256
ROW_TB = 256
MEM_TB = 512
CONV_RT = 512


def _to_sub(x, dil):
    l = x.shape[0] // dil
    return x.reshape(l, dil, 4, ATT_HD).transpose(2, 1, 0, 3).reshape(4 * dil, l, ATT_HD)


def _from_sub(x, dil):
    l = x.shape[1]
    return x.reshape(4, dil, l, ATT_HD).transpose(2, 1, 0, 3).reshape(l * dil, 4 * ATT_HD)


def _heads_major(x):
    return x.reshape(x.shape[0], MEM_HEADS, ATT_HD).transpose(1, 0, 2)


def _heads_minor(x):
    return x.transpose(1, 0, 2).reshape(x.shape[1], MEM_HEADS * ATT_HD)


def _mem_kv_fwd(mem, gain, w_kv, li):
    (memn,) = _rowwise(_fn_pre, [mem], [gain], [(D, BF16)], mem.shape[0], f"memnorm_fwd{li}")
    kv = _matmul(memn, w_kv, "nn", F32, f"memkv_fwd{li}")
    return _heads_major(kv[:, :MEM_W]), _heads_major(kv[:, MEM_W:]), memn


def _mem_kv_bwd(mem, gain, w_kv, memn, dkm, dvm, li):
    dkv = jnp.concatenate([_heads_minor(dkm), _heads_minor(dvm)], axis=1).astype(BF16)
    dw = _matmul(memn, dkv, "tn", BF16, f"memkv_dw{li}")
    dmemn = _matmul(dkv, w_kv, "nt", F32, f"memkv_dx{li}")
    _, (dgain,) = _rowwise_bwd(_fn_pre, [mem], [gain], [dmemn], [None], mem.shape[0], f"memnorm_bwd{li}")
    return dw, dgain


def _attn_mixer_fwd(p, rel_bias, km, vm):
    t = p.shape[0]
    saved, outs, lses = [], [], []
    for gi, (_, dil) in enumerate(DIL_GROUPS):
        l_sub = t // dil
        bq = min(BAND_BQ, l_sub)
        q = _to_sub(p[:, 256 * gi:256 * (gi + 1)], dil)
        pad = ((0, 0), (BAND_HALF, BAND_HALF), (0, 0))
        k = jnp.pad(_to_sub(p[:, TOK_W + 256 * gi:TOK_W + 256 * (gi + 1)], dil), pad)
        v = jnp.pad(_to_sub(p[:, 2 * TOK_W + 256 * gi:2 * TOK_W + 256 * (gi + 1)], dil), pad)
        bm = _band_bias(rel_bias, gi, dil, bq)
        o, lse = _band_fwd(q, k, v, bm, dil, l_sub, bq, f"band_fwd{gi}")
        outs.append(_from_sub(o, dil))
        lses.append(_from_sub(lse, dil))
        saved.append((q, k, v, bm))
    o_all = jnp.concatenate(outs, axis=1)
    lse_all = jnp.concatenate(lses, axis=1)
    (mixed,) = _rowwise(_fn_combine, [o_all, lse_all], [], [(TOK_W, BF16)], ROW_TB, "combine_fwd")
    qm = _heads_major(p[:, 3 * TOK_W:])
    memo = _mem_fwd(qm, km, vm, min(MEM_TB, t), "mem_fwd0")
    cat = jnp.concatenate([mixed, _heads_minor(memo).astype(BF16)], axis=1)
    return cat, (saved, o_all, lse_all, qm)


def _attn_mixer_bwd(dcat, res, km, vm):
    saved, o_all, lse_all, qm = res
    t = dcat.shape[0]
    (do_all, dlse_all), _ = _rowwise_bwd(_fn_combine, [o_all, lse_all], [], [dcat[:, :TOK_W]], [F32, F32],
                                         ROW_TB, "combine_bwd")
    dqs, dks, dvs, dbms = [], [], [], []
    for gi, (_, dil) in enumerate(DIL_GROUPS):
        l_sub = t // dil
        bq = min(BAND_BQ, l_sub)
        q, k, v, bm = saved[gi]
        do = _to_sub(do_all[:, 256 * gi:256 * (gi + 1)], dil)
        dl = _to_sub(dlse_all[:, 256 * gi:256 * (gi + 1)], dil)
        dq, dk, dv, dbm = _band_bwd(q, k, v, bm, do, dl, dil, l_sub, bq, f"band_bwd{gi}")
        dqs.append(_from_sub(dq, dil))
        dks.append(_from_sub(dk[:, BAND_HALF:-BAND_HALF], dil))
        dvs.append(_from_sub(dv[:, BAND_HALF:-BAND_HALF], dil))
        dbms.append(dbm)
    dqm, dkm, dvm = _mem_bwd(qm, km, vm, _heads_major(dcat[:, TOK_W:]), min(MEM_TB, t), "mem_bwd0")
    dp = jnp.concatenate(dqs + dks + dvs + [_heads_minor(dqm)], axis=1).astype(BF16)
    return dp, _relbias_grad(dbms, "relbias_grad"), dkm, dvm


def _dn_mixer_fwd(p, conv_w, a_log, dt_bias, out_norm, km, vm):
    t = p.shape[0]
    rt = min(CONV_RT, t)
    xp = jnp.pad(p[:, :3 * TOK_W], ((CONV_PAD, CONV_PAD), (0, 0)))
    w8 = jnp.pad(conv_w, ((0, 8 - DN_CONV), (0, 0)))
    q = _conv_fwd(xp, w8, 0, rt, "conv_fwd_q")
    k = _conv_fwd(xp, w8, 1, rt, "conv_fwd_k")
    v = _conv_fwd(xp, w8, 2, rt, "conv_fwd_v")
    gate = p[:, 4 * TOK_W:4 * TOK_W + 4 * DN_HEADS].reshape(t, 2, 2, DN_HEADS)
    bshape = (2, DN_HEADS, t, DN_HD)
    al = jnp.broadcast_to(gate[:, :, 0, :].transpose(1, 2, 0)[..., None], bshape)
    be = jnp.broadcast_to(gate[:, :, 1, :].transpose(1, 2, 0)[..., None], bshape)
    a_rows = jnp.broadcast_to(a_log[:, :, None, None], (2, DN_HEADS, 1, DN_HD))
    dt_rows = jnp.broadcast_to(dt_bias[:, :, None, None], (2, DN_HEADS, 1, DN_HD))
    o_f, o_r, states = _dn_fwd(q, k, v, al, be, a_rows, dt_rows, "dn_fwd")
    z = p[:, 3 * TOK_W:4 * TOK_W]
    gain = out_norm.reshape(1, DN_HD)
    (og,) = _rowwise(_fn_outnorm, [o_f, o_r, z], [gain], [(TOK_W, BF16)], ROW_TB, "outnorm_fwd")
    qm = _heads_major(p[:, 4 * TOK_W + 4 * DN_HEADS:DN_IN])
    memo = _mem_fwd(qm, km, vm, min(MEM_TB, t), "mem_fwd1")
    cat = jnp.concatenate([og, _heads_minor(memo).astype(BF16)], axis=1)
    return cat, (xp, w8, q, k, v, al, be, a_rows, dt_rows, o_f, o_r, states, z, gain, qm)


def _dn_mixer_bwd(dcat, res, km, vm):
    xp, w8, q, k, v, al, be, a_rows, dt_rows, o_f, o_r, states, z, gain, qm = res
    t = dcat.shape[0]
    rt = min(CONV_RT, t)
    (do, dz), (dgain,) = _rowwise_bwd(_fn_outnorm, [o_f, o_r, z], [gain], [dcat[:, :TOK_W]], [F32, None, F32],
                                      ROW_TB, "outnorm_bwd")
    d_f, d_r, dal, dbe, da, ddt = _dn_bwd(q, k, v, al, be, a_rows, dt_rows, states, do, "dn_bwd")
    dxs, dws = [], []
    for kind, nm in enumerate("qkv"):
        dx, dw = _conv_bwd(xp, w8, d_f[kind], d_r[kind], kind, rt, f"conv_bwd_{nm}")
        dxs.append(dx)
        dws.append(dw)
    dconv = jnp.concatenate(dws, axis=1)[:DN_CONV]
    dgate = jnp.stack([dal[..., 0], dbe[..., 0]], axis=1).transpose(3, 0, 1, 2).reshape(t, 4 * DN_HEADS)
    dqm, dkm, dvm = _mem_bwd(qm, km, vm, _heads_major(dcat[:, TOK_W:]), min(MEM_TB, t), "mem_bwd1")
    dp = jnp.concatenate(dxs + [dz, dgate, _heads_minor(dqm), jnp.zeros((t, DN_IN_PAD - DN_IN), F32)],
                         axis=1).astype(BF16)
    return dp, dconv, da[:, :, 0, 0], ddt[:, :, 0, 0], dgain.reshape(DN_HD), dkm, dvm


SWI_TB = 256


def _ffn_fwd(h, w_gu_t, w_d, li):
    gu = _matmul(h, w_gu_t, "nt", BF16, f"ffn_gu{li}")
    (a,) = _rowwise(_fn_swiglu, [gu], [], [(D_FF, BF16)], SWI_TB, f"swiglu_fwd{li}")
    return _matmul(a, w_d, "nn", F32, f"ffn_down{li}"), gu, a


def _ffn_bwd(df, h, w_gu_t, w_d, gu, a, li):
    da = _matmul(df, w_d, "nt", BF16, f"ffn_down_dx{li}")
    dwd = _matmul(a, df, "tn", BF16, f"ffn_down_dw{li}")
    (dgu,), _ = _rowwise_bwd(_fn_swiglu, [gu], [], [da], [BF16], SWI_TB, f"swiglu_bwd{li}")
    dh = _matmul(dgu, w_gu_t, "nn", F32, f"ffn_gu_dx{li}")
    dwgu_t = _matmul(dgu, h, "tn", BF16, f"ffn_gu_dw{li}")
    return dh, dwgu_t, dwd


def _fn_first(x, g):
    return x, _rms(x, g)


def _me_xyc():
    return lax.axis_index("x"), lax.axis_index("y"), lax.axis_index("c")


def _flip(coords, k):
    x, y, c = coords
    return (1 - x if k & 4 else x, 1 - y if k & 2 else y, 1 - c if k & 1 else c)


def _index(coords):
    x, y, c = coords
    return 4 * x + 2 * y + c


def _window(ref, axis, size, d):
    idx = [slice(None)] * len(ref.shape)
    idx[axis] = pl.ds(pl.multiple_of(d * size, size), size)
    return ref.at[tuple(idx)]


def _comm_call(body, n, ins, out_shapes, name):
    hbm = pl.BlockSpec(memory_space=pl.ANY)
    return pl.pallas_call(
        body, name=name, in_specs=[hbm] * n, out_specs=[hbm] * n, out_shape=out_shapes,
        scratch_shapes=[pltpu.SemaphoreType.DMA((N_DEV - 1, n)), pltpu.SemaphoreType.DMA((N_DEV - 1, n)),
                        pltpu.SemaphoreType.DMA((n,))],
    )(*ins)


def _run_exchange(n, local, remote, send_sems, recv_sems):
    me = _me_xyc()
    locs = [local(p) for p in range(n)]
    for cp in locs:
        cp.start()
    sends = [remote(k, p, me, _flip(me, k)) for k in range(1, N_DEV) for p in range(n)]
    for cp in sends:
        cp.start()
    for k in range(1, N_DEV):
        for p in range(n):
            remote(k, p, _flip(me, k), me).wait_recv()
    for cp in sends:
        cp.wait_send()
    for cp in locs:
        cp.wait()


def _all_gather(shards, axes, name):
    n = len(shards)
    sizes = [s.shape[a] for s, a in zip(shards, axes)]

    def body(*refs):
        ins, outs = refs[:n], refs[n:2 * n]
        send_sems, recv_sems, loc_sems = refs[2 * n:]
        me = _me_xyc()

        def local(p):
            return pltpu.make_async_copy(ins[p], _window(outs[p], axes[p], sizes[p], _index(me)), loc_sems.at[p])

        def remote(k, p, owner, to):
            return pltpu.make_async_remote_copy(
                src_ref=ins[p], dst_ref=_window(outs[p], axes[p], sizes[p], _index(owner)),
                send_sem=send_sems.at[k - 1, p], recv_sem=recv_sems.at[k - 1, p], device_id=to, device_id_type=MESH)

        _run_exchange(n, local, remote, send_sems, recv_sems)

    def full(s, a):
        return s.shape[:a] + (N_DEV * s.shape[a],) + s.shape[a + 1:]

    return _comm_call(body, n, shards, [jax.ShapeDtypeStruct(full(s, a), s.dtype) for s, a in zip(shards, axes)], name)


def _exchange(fulls, axes, name):
    n = len(fulls)
    sizes = [None if a is None else f.shape[a] // N_DEV for f, a in zip(fulls, axes)]

    def part_shape(f, a):
        return f.shape if a is None else f.shape[:a] + (f.shape[a] // N_DEV,) + f.shape[a + 1:]

    def body(*refs):
        ins, outs = refs[:n], refs[n:2 * n]
        send_sems, recv_sems, loc_sems = refs[2 * n:]
        me = _me_xyc()

        def src(p, to):
            return ins[p] if axes[p] is None else _window(ins[p], axes[p], sizes[p], _index(to))

        def local(p):
            return pltpu.make_async_copy(src(p, me), outs[p].at[_index(me)], loc_sems.at[p])

        def remote(k, p, sender, to):
            return pltpu.make_async_remote_copy(
                src_ref=src(p, to), dst_ref=outs[p].at[_index(sender)],
                send_sem=send_sems.at[k - 1, p], recv_sem=recv_sems.at[k - 1, p], device_id=to, device_id_type=MESH)

        _run_exchange(n, local, remote, send_sems, recv_sems)

    return _comm_call(body, n, fulls,
                      [jax.ShapeDtypeStruct((N_DEV,) + part_shape(f, a), f.dtype) for f, a in zip(fulls, axes)], name)


def _adam_math(g, w, m, v):
    m = ADAM_B1 * m + (1.0 - ADAM_B1) * g
    v = ADAM_B2 * v + (1.0 - ADAM_B2) * (g * g)
    m_hat = m / (1.0 - ADAM_B1 ** ADAM_STEP)
    v_hat = v / (1.0 - ADAM_B2 ** ADAM_STEP)
    delta = -ADAM_LR * (m_hat / (jnp.sqrt(v_hat) + ADAM_EPS) + ADAM_WD * w)
    return delta, m, v


def _sum_slabs(r_ref):
    g = r_ref[0].astype(F32)
    for s in range(1, N_DEV):
        g = g + r_ref[s].astype(F32)
    return g


def _adamw_reduce(recv, w, m, v, tb, name):
    r, c = w.shape

    def body(r_ref, w_ref, m_ref, v_ref, g_ref, d_ref, nm_ref, nv_ref):
        g = _sum_slabs(r_ref)
        g_ref[...] = g
        d_ref[...], nm_ref[...], nv_ref[...] = _adam_math(g, w_ref[...], m_ref[...], v_ref[...])

    blk = pl.BlockSpec((tb, c), lambda i: (i, 0))
    return pl.pallas_call(
        body, name=name, grid=(r // tb,),
        in_specs=[pl.BlockSpec((N_DEV, tb, c), lambda i: (0, i, 0)), blk, blk, blk],
        out_specs=[blk] * 4, out_shape=[jax.ShapeDtypeStruct((r, c), F32)] * 4,
        compiler_params=_cparams(("parallel",)),
    )(recv, w, m, v)


def _reduce8(recv, tb, name):
    r, c = recv.shape[1:]

    def body(r_ref, g_ref):
        g_ref[...] = _sum_slabs(r_ref)

    return pl.pallas_call(
        body, name=name, grid=(r // tb,),
        in_specs=[pl.BlockSpec((N_DEV, tb, c), lambda i: (0, i, 0))],
        out_specs=pl.BlockSpec((tb, c), lambda i: (i, 0)), out_shape=jax.ShapeDtypeStruct((r, c), F32),
        compiler_params=_cparams(("parallel",)),
    )(recv)


def _adamw(g, w, m, v, tb, name):
    r, c = w.shape

    def body(g_ref, w_ref, m_ref, v_ref, d_ref, nm_ref, nv_ref):
        d_ref[...], nm_ref[...], nv_ref[...] = _adam_math(g_ref[...], w_ref[...], m_ref[...], v_ref[...])

    blk = pl.BlockSpec((tb, c), lambda i: (i, 0))
    return pl.pallas_call(
        body, name=name, grid=(r // tb,), in_specs=[blk] * 4, out_specs=[blk] * 3,
        out_shape=[jax.ShapeDtypeStruct((r, c), F32)] * 3, compiler_params=_cparams(("parallel",)),
    )(g, w, m, v)


DN_IN_SHARD = DN_IN // N_DEV
DN_IN_SHARD_PAD = 432
CONV_SHARD = (1, DN_CONV, 288)


def _pack_small(arrs, rows):
    flat = jnp.concatenate([a.astype(F32).reshape(-1) for a in arrs])
    return jnp.pad(flat, (0, rows * PACK_C - flat.size)).reshape(rows, PACK_C)


def _unpack_small(packed, shapes):
    flat, out, off = packed.reshape(-1), [], 0
    for shp in shapes:
        n = int(np.prod(shp))
        out.append(flat[off:off + n].reshape(shp))
        off += n
    return out


def kernel(x, mem, rel_bias, att_w_in, att_w_out, dn_w_in, dn_conv, dn_a_log, dn_dt_bias, dn_out_norm, dn_w_out, mem_norm, mem_w_kv, norm_mix_pre, norm_mix_post, norm_ffn_pre, norm_ffn_post, ffn_w_gate_up, ffn_w_down, loss_target, m_rel_bias, m_att_w_in, m_att_w_out, m_dn_w_in, m_dn_conv, m_dn_a_log, m_dn_dt_bias, m_dn_out_norm, m_dn_w_out, m_mem_norm, m_mem_w_kv, m_norm_mix_pre, m_norm_mix_post, m_norm_ffn_pre, m_norm_ffn_post, m_ffn_w_gate_up, m_ffn_w_down, v_rel_bias, v_att_w_in, v_att_w_out, v_dn_w_in, v_dn_conv, v_dn_a_log, v_dn_dt_bias, v_dn_out_norm, v_dn_w_out, v_mem_norm, v_mem_w_kv, v_norm_mix_pre, v_norm_mix_post, v_norm_ffn_pre, v_norm_ffn_post, v_ffn_w_gate_up, v_ffn_w_down):
    x0, mem0, tgt = x[0], mem[0], loss_target[0]
    t = x0.shape[0]
    axes = ("x", "y", "c")

    def t_shard(w):
        return jnp.swapaxes(w, 1, 2).astype(BF16)

    dn_in_pad = ((0, 0), (0, DN_IN_SHARD_PAD - DN_IN_SHARD), (0, 0))
    shards = [t_shard(att_w_in), att_w_out.astype(BF16), jnp.pad(t_shard(dn_w_in), dn_in_pad),
              dn_w_out.astype(BF16), mem_w_kv.astype(BF16), t_shard(ffn_w_gate_up), ffn_w_down.astype(BF16), dn_conv]
    w_att_in_t, w_att_out, w_dn_in_g, w_dn_out, w_kv, w_gu_t, w_down, conv_g = _all_gather(
        shards, [1, 1, 1, 1, 1, 1, 1, 0], "allgather_weights")
    w_att_in_t, w_att_out, w_dn_out = w_att_in_t[0], w_att_out[0], w_dn_out[0]
    w_dn_in_t = jnp.concatenate(
        [w_dn_in_g[0, DN_IN_SHARD_PAD * j:DN_IN_SHARD_PAD * j + DN_IN_SHARD] for j in range(N_DEV)]
        + [jnp.zeros((DN_IN_PAD - DN_IN, D), BF16)], axis=0)
    conv_full = conv_g.transpose(1, 0, 2).reshape(DN_CONV, 3 * TOK_W)

    def gain(a, i):
        return a[i].reshape(1, D)

    (h0,) = _rowwise(_fn_pre, [x0], [gain(norm_mix_pre, 0)], [(D, BF16)], ROW_TB, "pre0")
    km0, vm0, memn0 = _mem_kv_fwd(mem0, gain(mem_norm, 0), w_kv[0], 0)
    p0 = _matmul(h0, w_att_in_t, "nt", BF16, "att_in")
    cat0, res0 = _attn_mixer_fwd(p0, rel_bias, km0, vm0)
    y0 = _matmul(cat0, w_att_out, "nn", F32, "att_out")
    g_a = [gain(norm_mix_post, 0), gain(norm_ffn_pre, 0)]
    x1, h1 = _rowwise(_fn_res_pre, [x0, y0], g_a, [(D, F32), (D, BF16)], ROW_TB, "res_pre0")
    f0, gu0, a0 = _ffn_fwd(h1, w_gu_t[0], w_down[0], 0)
    g_b = [gain(norm_ffn_post, 0), gain(norm_mix_pre, 1)]
    x2, h2 = _rowwise(_fn_res_pre, [x1, f0], g_b, [(D, F32), (D, BF16)], ROW_TB, "res_pre1")
    km1, vm1, memn1 = _mem_kv_fwd(mem0, gain(mem_norm, 1), w_kv[1], 1)
    p1 = _matmul(h2, w_dn_in_t, "nt", F32, "dn_in")
    cat1, res1 = _dn_mixer_fwd(p1, conv_full, dn_a_log[0], dn_dt_bias[0], dn_out_norm[0], km1, vm1)
    y1 = _matmul(cat1, w_dn_out, "nn", F32, "dn_out")
    g_c = [gain(norm_mix_post, 1), gain(norm_ffn_pre, 1)]
    x3, h3 = _rowwise(_fn_res_pre, [x2, y1], g_c, [(D, F32), (D, BF16)], ROW_TB, "res_pre2")
    f1, gu1, a1 = _ffn_fwd(h3, w_gu_t[1], w_down[1], 1)
    g_d = [gain(norm_ffn_post, 1)]
    (x4,) = _rowwise(_fn_res, [x3, f1], g_d, [(D, F32)], ROW_TB, "res3")
    dx4, lrow = _loss_kernel(x4, tgt, ROW_TB, "loss")
    loss = lax.psum(lrow[0, 0] * (0.5 / D), axes)

    (df1,), (dg_fpost1,) = _rowwise_bwd(_fn_res, [x3, f1], g_d, [dx4], [None, BF16], ROW_TB, "res3_bwd")
    dh3, dwgu1, dwd1 = _ffn_bwd(df1, h3, w_gu_t[1], w_down[1], gu1, a1, 1)
    (dx2, dy1), (dg_mpost1, dg_fpre1) = _rowwise_bwd(_fn_res_pre, [x2, y1], g_c, [dx4, dh3], [F32, BF16],
                                                     ROW_TB, "res_pre2_bwd")
    dcat1 = _matmul(dy1, w_dn_out, "nt", F32, "dn_out_dx")
    dw_dn_out = _matmul(cat1, dy1, "tn", BF16, "dn_out_dw")
    dp1, dconv, da_log, ddt_bias, dout_norm, dkm1, dvm1 = _dn_mixer_bwd(dcat1, res1, km1, vm1)
    dwkv1, dg_mem1 = _mem_kv_bwd(mem0, gain(mem_norm, 1), w_kv[1], memn1, dkm1, dvm1, 1)
    dh2 = _matmul(dp1, w_dn_in_t, "nn", F32, "dn_in_dx")
    dw_dn_in_t = _matmul(dp1, h2, "tn", BF16, "dn_in_dw")
    (dx1, df0), (dg_fpost0, dg_mpre1) = _rowwise_bwd(_fn_res_pre, [x1, f0], g_b, [dx2, dh2], [F32, BF16],
                                                     ROW_TB, "res_pre1_bwd")
    dh1, dwgu0, dwd0 = _ffn_bwd(df0, h1, w_gu_t[0], w_down[0], gu0, a0, 0)
    (dx0, dy0), (dg_mpost0, dg_fpre0) = _rowwise_bwd(_fn_res_pre, [x0, y0], g_a, [dx1, dh1], [F32, BF16],
                                                     ROW_TB, "res_pre0_bwd")
    dcat0 = _matmul(dy0, w_att_out, "nt", F32, "att_out_dx")
    dw_att_out = _matmul(cat0, dy0, "tn", BF16, "att_out_dw")
    dp0, drel, dkm0, dvm0 = _attn_mixer_bwd(dcat0, res0, km0, vm0)
    dwkv0, dg_mem0 = _mem_kv_bwd(mem0, gain(mem_norm, 0), w_kv[0], memn0, dkm0, dvm0, 0)
    dh0 = _matmul(dp0, w_att_in_t, "nn", F32, "att_in_dx")
    dw_att_in_t = _matmul(dp0, h0, "tn", BF16, "att_in_dw")
    (grad_x,), (dg_mpre0,) = _rowwise_bwd(_fn_first, [x0], [gain(norm_mix_pre, 0)], [dx0, dh0], [F32],
                                          ROW_TB, "pre0_bwd")

    dn_in_parts = [jnp.pad(dw_dn_in_t[DN_IN_SHARD * j:DN_IN_SHARD * (j + 1)],
                           ((0, DN_IN_SHARD_PAD - DN_IN_SHARD), (0, 0))) for j in range(N_DEV)]
    small_grads = [drel, da_log, ddt_bias, dout_norm, jnp.concatenate([dg_mem0, dg_mem1]),
                   jnp.concatenate([dg_mpre0, dg_mpre1]), jnp.concatenate([dg_mpost0, dg_mpost1]),
                   jnp.concatenate([dg_fpre0, dg_fpre1]), jnp.concatenate([dg_fpost0, dg_fpost1]), dconv]
    fulls = [dw_att_in_t[None], dw_att_out[None], jnp.concatenate(dn_in_parts, axis=0)[None], dw_dn_out[None],
             jnp.stack([dwkv0, dwkv1]), jnp.stack([dwgu0, dwgu1]), jnp.stack([dwd0, dwd1]),
             _pack_small(small_grads, SMALL_ROWS)]
    r_att_in, r_att_out, r_dn_in, r_dn_out, r_kv, r_gu, r_down, r_small = _exchange(
        fulls, [1, 1, 1, 1, 1, 1, 1, None], "exchange_grads")

    def rows(a):
        return a.reshape((-1,) + a.shape[-1:])

    def row_sharded(recv, w, m, v, tb, name):
        outs = _adamw_reduce(recv.reshape((N_DEV, -1) + recv.shape[-1:]), rows(w), rows(m), rows(v), tb, name)
        return [o.reshape(w.shape) for o in outs]

    def col_sharded(recv, w, m, v, tb, name):
        g_t = _reduce8(recv.reshape((N_DEV, -1) + recv.shape[-1:]), tb, name + "_sum")
        g = jnp.swapaxes(g_t.reshape(recv.shape[1:])[:, :w.shape[2]], 1, 2)
        outs = _adamw(rows(g), rows(w), rows(m), rows(v), 256, name)
        return [g] + [o.reshape(w.shape) for o in outs]

    big = [col_sharded(r_att_in, att_w_in, m_att_w_in, v_att_w_in, 320, "adamw_att_in"),
           row_sharded(r_att_out, att_w_out, m_att_w_out, v_att_w_out, 128, "adamw_att_out"),
           col_sharded(r_dn_in, dn_w_in, m_dn_w_in, v_dn_w_in, 432, "adamw_dn_in"),
           row_sharded(r_dn_out, dn_w_out, m_dn_w_out, v_dn_w_out, 128, "adamw_dn_out"),
           row_sharded(r_kv, mem_w_kv, m_mem_w_kv, v_mem_w_kv, 256, "adamw_mem_kv"),
           col_sharded(r_gu, ffn_w_gate_up, m_ffn_w_gate_up, v_ffn_w_gate_up, 176, "adamw_ffn_gu"),
           row_sharded(r_down, ffn_w_down, m_ffn_w_down, v_ffn_w_down, 176, "adamw_ffn_down")]
    g_big, d_big, nm_big, nv_big = [[b[i] for b in big] for i in range(4)]

    g_small = _reduce8(r_small, SMALL_ROWS, "reduce_small")
    rep_shapes = [(32, 12), (1, 2, 6), (1, 2, 6), (1, 128), (2, D), (2, D), (2, D), (2, D), (2, D)]
    *g_rep, g_conv_full = _unpack_small(g_small, rep_shapes + [(DN_CONV, 3 * TOK_W)])
    me = _index(_me_xyc())
    g_conv = lax.dynamic_slice(g_conv_full, (0, me * 288), (DN_CONV, 288)).reshape(CONV_SHARD)
    small_shapes = rep_shapes + [CONV_SHARD]
    small_w = [rel_bias, dn_a_log, dn_dt_bias, dn_out_norm, mem_norm, norm_mix_pre, norm_mix_post,
               norm_ffn_pre, norm_ffn_post, dn_conv]
    small_m = [m_rel_bias, m_dn_a_log, m_dn_dt_bias, m_dn_out_norm, m_mem_norm, m_norm_mix_pre, m_norm_mix_post,
               m_norm_ffn_pre, m_norm_ffn_post, m_dn_conv]
    small_v = [v_rel_bias, v_dn_a_log, v_dn_dt_bias, v_dn_out_norm, v_mem_norm, v_norm_mix_pre, v_norm_mix_post,
               v_norm_ffn_pre, v_norm_ffn_post, v_dn_conv]
    g_small_list = g_rep + [g_conv]
    outs_small = _adamw(_pack_small(g_small_list, 24), _pack_small(small_w, 24), _pack_small(small_m, 24),
                        _pack_small(small_v, 24), 24, "adamw_small")
    d_small, nm_small, nv_small = [_unpack_small(o, small_shapes) for o in outs_small]

    def ordered(small, big):
        return [small[0], big[0], big[1], big[2], small[9], small[1], small[2], small[3], big[3], small[4],
                big[4], small[5], small[6], small[7], small[8], big[5], big[6]]

    g_small_out = [g.reshape(s) for g, s in zip(g_small_list, small_shapes)]
    return (loss, grad_x[None], *ordered(g_small_out, g_big), *ordered(d_small, d_big),
            *ordered(nm_small, nm_big), *ordered(nv_small, nv_big))
```

```python
import functools
import math

import numpy as np
import jax
import jax.numpy as jnp
from jax import lax
from jax.experimental import pallas as pl
from jax.experimental.pallas import tpu as pltpu

F32 = jnp.float32
BF16 = jnp.bfloat16
HI = lax.Precision.HIGHEST
MESH = pl.DeviceIdType.MESH

N_DEV = 8
D = 1024
EPS = 1e-6
NEG = -1e30
TOK_W = 768
MEM_W = 256
ATT_HD = 64
DIL_GROUPS = ((128, 1), (512, 4), (2048, 16))
BAND_HALF = 64
REL_BUCKETS = 32
REL_MAX_DIST = 1024
DN_HD = 128
DN_HEADS = 6
DN_CONV = 5
DN_CHUNK = 64
MEM_HEADS = 4
D_FF = 2816
ATT_IN = 2560
DN_IN = 3352
DN_IN_PAD = 3456

ADAM_LR, ADAM_B1, ADAM_B2, ADAM_EPS, ADAM_WD, ADAM_STEP = 0.001, 0.9, 0.999, 1e-08, 0.01, 10

PACK_C = 512
BIG_ROWS = 6480
SMALL_ROWS = 48
VMEM_LIMIT = 48 * 1024 * 1024


def _cparams(sem=None):
    kw = dict(vmem_limit_bytes=VMEM_LIMIT)
    if sem is not None:
        kw["dimension_semantics"] = sem
    return pltpu.CompilerParams(**kw)


def _tile(n, cap):
    if n <= cap:
        return n
    best = None
    for t in range(128, cap + 1, 128):
        if n % t == 0:
            best = t
    assert best is not None, (n, cap)
    return best


def _matmul(a, b, mode, out_dtype, name, tm=1024, tn=1408, tk=None):
    if tk is None:
        tk = 2048 if mode == "tn" else 2816
    if mode == "nn":
        (m, kc), (_, n) = a.shape, b.shape
        dims = (((1,), (0,)), ((), ()))
    elif mode == "nt":
        (m, kc), (n, _) = a.shape, b.shape
        dims = (((1,), (1,)), ((), ()))
    else:
        (kc, m), (_, n) = a.shape, b.shape
        dims = (((0,), (0,)), ((), ()))
    tm = m if m <= tm else _tile(m, tm)
    tn = _tile(n, tn)
    tk = _tile(kc, tk)
    nk = kc // tk

    def body(a_ref, b_ref, o_ref, acc_ref):
        k = pl.program_id(2)
        part = lax.dot_general(a_ref[...], b_ref[...], dims, preferred_element_type=F32)

        @pl.when(k == 0)
        def _():
            acc_ref[...] = part

        @pl.when(k > 0)
        def _():
            acc_ref[...] += part

        @pl.when(k == nk - 1)
        def _():
            o_ref[...] = acc_ref[...].astype(o_ref.dtype)

    if mode == "nn":
        a_spec = pl.BlockSpec((tm, tk), lambda i, j, k: (i, k))
        b_spec = pl.BlockSpec((tk, tn), lambda i, j, k: (k, j))
    elif mode == "nt":
        a_spec = pl.BlockSpec((tm, tk), lambda i, j, k: (i, k))
        b_spec = pl.BlockSpec((tn, tk), lambda i, j, k: (j, k))
    else:
        a_spec = pl.BlockSpec((tk, tm), lambda i, j, k: (k, i))
        b_spec = pl.BlockSpec((tk, tn), lambda i, j, k: (k, j))
    return pl.pallas_call(
        body, name=name, grid=(m // tm, n // tn, nk),
        in_specs=[a_spec, b_spec],
        out_specs=pl.BlockSpec((tm, tn), lambda i, j, k: (i, j)),
        out_shape=jax.ShapeDtypeStruct((m, n), out_dtype),
        scratch_shapes=[pltpu.VMEM((tm, tn), F32)],
        compiler_params=_cparams(("parallel", "parallel", "arbitrary")),
    )(a, b)


def _rowwise(fn, rows, params, outs, tb, name):
    t = rows[0].shape[0]
    nr, npar = len(rows), len(params)

    def body(*refs):
        ins = [r[...].astype(F32) for r in refs[:nr + npar]]
        res = fn(*ins)
        for o_ref, r in zip(refs[nr + npar:], res):
            o_ref[...] = r.astype(o_ref.dtype)

    return pl.pallas_call(
        body, name=name, grid=(t // tb,),
        in_specs=[pl.BlockSpec((tb, r.shape[1]), lambda i: (i, 0)) for r in rows]
        + [pl.BlockSpec(p.shape, lambda i: (0, 0)) for p in params],
        out_specs=[pl.BlockSpec((tb, c), lambda i: (i, 0)) for c, _ in outs],
        out_shape=[jax.ShapeDtypeStruct((t, c), dt) for c, dt in outs],
        compiler_params=_cparams(("parallel",)),
    )(*rows, *params)


def _rowwise_bwd(fn, rows, params, cots, row_grad, tb, name):
    t = rows[0].shape[0]
    nr, npar, nc = len(rows), len(params), len(cots)
    want = [i for i, g in enumerate(row_grad) if g is not None]

    def body(*refs):
        ins = [r[...].astype(F32) for r in refs[:nr + npar]]
        cts = tuple(r[...].astype(F32) for r in refs[nr + npar:nr + npar + nc])
        outs = refs[nr + npar + nc:]
        _, vjp = jax.vjp(fn, *ins)
        grads = vjp(cts)
        for o_ref, i in zip(outs[:len(want)], want):
            o_ref[...] = grads[i].astype(o_ref.dtype)
        first = pl.program_id(0) == 0
        for o_ref, g in zip(outs[len(want):], grads[nr:]):
            @pl.when(first)
            def _(o_ref=o_ref, g=g):
                o_ref[...] = g

            @pl.when(jnp.logical_not(first))
            def _(o_ref=o_ref, g=g):
                o_ref[...] += g

    res = pl.pallas_call(
        body, name=name, grid=(t // tb,),
        in_specs=[pl.BlockSpec((tb, r.shape[1]), lambda i: (i, 0)) for r in rows]
        + [pl.BlockSpec(p.shape, lambda i: (0, 0)) for p in params]
        + [pl.BlockSpec((tb, c.shape[1]), lambda i: (i, 0)) for c in cots],
        out_specs=[pl.BlockSpec((tb, rows[i].shape[1]), lambda i_: (i_, 0)) for i in want]
        + [pl.BlockSpec(p.shape, lambda i: (0, 0)) for p in params],
        out_shape=[jax.ShapeDtypeStruct(rows[i].shape, row_grad[i]) for i in want]
        + [jax.ShapeDtypeStruct(p.shape, F32) for p in params],
        compiler_params=_cparams(("arbitrary",)),
    )(*rows, *params, *cots)
    return list(res[:len(want)]), list(res[len(want):])


def _rms(x, g):
    return x * lax.rsqrt(jnp.mean(x * x, axis=-1, keepdims=True) + EPS) * g


def _fn_pre(x, g):
    return (_rms(x, g),)


def _fn_res_pre(x, y, g_post, g_pre):
    x1 = x + _rms(y, g_post)
    return x1, _rms(x1, g_pre)


def _fn_res(x, y, g_post):
    return (x + _rms(y, g_post),)


def _sigmoid(x):
    return 1.0 / (1.0 + jnp.exp(-x))


def _silu(x):
    return x * _sigmoid(x)


def _fn_swiglu(gu):
    return (_silu(gu[:, :D_FF]) * gu[:, D_FF:],)


def _fn_combine(o, lse):
    ls = [lse[:, 256 * g:256 * (g + 1)] for g in range(3)]
    mx = lax.stop_gradient(jnp.maximum(jnp.maximum(ls[0], ls[1]), ls[2]))
    es = [jnp.exp(l - mx) for l in ls]
    inv = 1.0 / (es[0] + es[1] + es[2])
    return (jnp.concatenate([o[:, 256 * g:256 * (g + 1)] * (es[g] * inv) for g in range(3)], axis=1),)


def _fn_outnorm(o_f, o_r, z, gain):
    res = []
    for h in range(DN_HEADS):
        sl = slice(DN_HD * h, DN_HD * (h + 1))
        o = o_f[:, sl] + o_r[:, sl]
        res.append(o * lax.rsqrt(jnp.mean(o * o, axis=-1, keepdims=True) + EPS) * gain * _silu(z[:, sl]))
    return (jnp.concatenate(res, axis=1),)


def _loss_kernel(x, tgt, tb, name):
    t, d = x.shape

    def body(x_ref, t_ref, dx_ref, l_ref, acc_ref):
        i = pl.program_id(0)
        e = x_ref[...] - t_ref[...]
        dx_ref[...] = e * (1.0 / d)
        part = jnp.sum(e * e, axis=0, keepdims=True)

        @pl.when(i == 0)
        def _():
            acc_ref[...] = part

        @pl.when(i > 0)
        def _():
            acc_ref[...] += part

        @pl.when(i == t // tb - 1)
        def _():
            l_ref[...] = jnp.broadcast_to(jnp.sum(acc_ref[...], axis=-1, keepdims=True), (1, 128))

    return pl.pallas_call(
        body, name=name, grid=(t // tb,),
        in_specs=[pl.BlockSpec((tb, d), lambda i: (i, 0))] * 2,
        out_specs=[pl.BlockSpec((tb, d), lambda i: (i, 0)), pl.BlockSpec((1, 128), lambda i: (0, 0))],
        out_shape=[jax.ShapeDtypeStruct((t, d), F32), jax.ShapeDtypeStruct((1, 128), F32)],
        scratch_shapes=[pltpu.VMEM((1, d), F32)],
        compiler_params=_cparams(("arbitrary",)),
    )(x, tgt)


def _band_fn(l_sub, bq, i, q, kw, vw, bm):
    w = bq + 2 * BAND_HALF
    s = lax.dot_general((q * (ATT_HD ** -0.5)).astype(BF16), kw.astype(BF16), (((1,), (1,)), ((), ())),
                        preferred_element_type=F32) + bm
    kpos = i * bq - BAND_HALF + lax.broadcasted_iota(jnp.int32, (bq, w), 1)
    s = jnp.where((kpos >= 0) & (kpos < l_sub), s, NEG)
    m = lax.stop_gradient(jnp.max(s, axis=-1, keepdims=True))
    p = jnp.exp(s - m)
    den = jnp.sum(p, axis=-1, keepdims=True)
    o = jnp.dot(p.astype(BF16), vw.astype(BF16), preferred_element_type=F32) / den
    return o, jnp.broadcast_to(m + jnp.log(den), o.shape)


def _band_specs(dil, l_sub, bq):
    w = bq + 2 * BAND_HALF
    qs = pl.BlockSpec((None, bq, ATT_HD), lambda h, r, i: (h * dil + r, i, 0))
    ks = pl.BlockSpec((None, l_sub + 2 * BAND_HALF, ATT_HD), lambda h, r, i: (h * dil + r, 0, 0))
    bs = pl.BlockSpec((None, bq, w), lambda h, r, i: (h, 0, 0))
    return qs, ks, bs


def _band_fwd(q, k, v, bm, dil, l_sub, bq, name):
    w = bq + 2 * BAND_HALF
    qs, ks, bs = _band_specs(dil, l_sub, bq)

    def body(q_ref, k_ref, v_ref, bm_ref, o_ref, l_ref):
        i = pl.program_id(2)
        st = pl.multiple_of(i * bq, bq)
        o, lse = _band_fn(l_sub, bq, i, q_ref[...].astype(F32), k_ref[pl.ds(st, w), :].astype(F32),
                          v_ref[pl.ds(st, w), :].astype(F32), bm_ref[...])
        o_ref[...] = o
        l_ref[...] = lse

    return pl.pallas_call(
        body, name=name, grid=(4, dil, l_sub // bq),
        in_specs=[qs, ks, ks, bs], out_specs=[qs, qs],
        out_shape=[jax.ShapeDtypeStruct(q.shape, F32)] * 2,
        compiler_params=_cparams(("parallel", "parallel", "arbitrary")),
    )(q, k, v, bm)


def _band_bwd(q, k, v, bm, do, dlse, dil, l_sub, bq, name):
    w = bq + 2 * BAND_HALF
    qs, ks, bs = _band_specs(dil, l_sub, bq)

    def body(q_ref, k_ref, v_ref, bm_ref, do_ref, dl_ref, dq_ref, dk_ref, dv_ref, dbm_ref):
        r, i = pl.program_id(1), pl.program_id(2)
        st = pl.multiple_of(i * bq, bq)
        _, vjp = jax.vjp(functools.partial(_band_fn, l_sub, bq, i),
                         q_ref[...].astype(F32), k_ref[pl.ds(st, w), :].astype(F32),
                         v_ref[pl.ds(st, w), :].astype(F32), bm_ref[...])
        dq, dkw, dvw, dbm = vjp((do_ref[...], dl_ref[...]))
        dq_ref[...] = dq

        @pl.when(i == 0)
        def _():
            dk_ref[...] = jnp.zeros_like(dk_ref)
            dv_ref[...] = jnp.zeros_like(dv_ref)

        dk_ref[pl.ds(st, w), :] += dkw
        dv_ref[pl.ds(st, w), :] += dvw

        @pl.when((i == 0) & (r == 0))
        def _():
            dbm_ref[...] = dbm

        @pl.when((i > 0) | (r > 0))
        def _():
            dbm_ref[...] += dbm

    return pl.pallas_call(
        body, name=name, grid=(4, dil, l_sub // bq),
        in_specs=[qs, ks, ks, bs, qs, qs], out_specs=[qs, ks, ks, bs],
        out_shape=[jax.ShapeDtypeStruct(q.shape, F32), jax.ShapeDtypeStruct(k.shape, F32),
                   jax.ShapeDtypeStruct(k.shape, F32), jax.ShapeDtypeStruct(bm.shape, F32)],
        compiler_params=_cparams(("parallel", "arbitrary", "arbitrary")),
    )(q, k, v, bm, do, dlse)


def _t5_bucket(rel):
    half = REL_BUCKETS // 2
    max_exact = half // 2
    n = np.abs(rel)
    large = max_exact + (np.log(np.maximum(n, 1) / max_exact) / math.log(REL_MAX_DIST / max_exact)
                         * (half - max_exact)).astype(np.int64)
    large = np.minimum(large, half - 1)
    return ((rel > 0) * half + np.where(n < max_exact, n, large)).astype(np.int32)


def _bucket_onehot(dil):
    idx = _t5_bucket(np.arange(-BAND_HALF, BAND_HALF + 1) * dil)
    oh = np.zeros((2 * BAND_HALF + 1, REL_BUCKETS), np.float32)
    oh[np.arange(2 * BAND_HALF + 1), idx] = 1.0
    return oh


def _band_bias(rel_bias, gi, dil, bq):
    w = bq + 2 * BAND_HALF
    nb = 2 * BAND_HALF + 1
    bias = jnp.dot(jnp.asarray(_bucket_onehot(dil)), rel_bias[:, 4 * gi:4 * gi + 4], precision=HI)
    row = jnp.concatenate([bias.T, jnp.full((4, w + 1 - nb), NEG, F32)], axis=1)
    flat = jnp.tile(row, (1, bq))[:, :bq * w]
    return flat.reshape(4, bq, w)


def _relbias_grad(dbms, name):
    nb = 2 * BAND_HALF + 1
    bq = max(d.shape[1] for d in dbms)
    skew = []
    for dbm in dbms:
        bqg, w = dbm.shape[1], dbm.shape[2]
        flat = jnp.pad(dbm.reshape(4, bqg * w), ((0, 0), (0, bqg)))
        skew.append(jnp.pad(flat.reshape(4, bqg, w + 1)[:, :, :nb], ((0, 0), (0, bq - bqg), (0, 256 - nb))))
    sk = jnp.concatenate(skew, axis=0)
    oh = np.zeros((3, 256, 128), np.float32)
    for gi, (_, dil) in enumerate(DIL_GROUPS):
        oh[gi, :2 * BAND_HALF + 1, :REL_BUCKETS] = _bucket_onehot(dil)

    def body(s_ref, oh_ref, o_ref):
        col = jnp.sum(s_ref[...], axis=0, keepdims=True)
        o_ref[...] = jnp.dot(jnp.broadcast_to(col, (8, 256)), oh_ref[...], precision=HI, preferred_element_type=F32)

    out = pl.pallas_call(
        body, name=name, grid=(12,),
        in_specs=[pl.BlockSpec((None, bq, 256), lambda n: (n, 0, 0)),
                  pl.BlockSpec((None, 256, 128), lambda n: (n // 4, 0, 0))],
        out_specs=pl.BlockSpec((None, 8, 128), lambda n: (n, 0, 0)),
        out_shape=jax.ShapeDtypeStruct((12, 8, 128), F32),
        compiler_params=_cparams(("parallel",)),
    )(sk, jnp.asarray(oh))
    return out[:, 0, :REL_BUCKETS].T


def _mem_fn(q, k, v):
    s = lax.dot_general((q * (ATT_HD ** -0.5)).astype(BF16), k.astype(BF16), (((1,), (1,)), ((), ())),
                        preferred_element_type=F32)
    m = lax.stop_gradient(jnp.max(s, axis=-1, keepdims=True))
    p = jnp.exp(s - m)
    p = p / jnp.sum(p, axis=-1, keepdims=True)
    return jnp.dot(p.astype(BF16), v.astype(BF16), preferred_element_type=F32)


def _mem_specs(tb, ml):
    qs = pl.BlockSpec((None, tb, ATT_HD), lambda h, i: (h, i, 0))
    ks = pl.BlockSpec((None, ml, ATT_HD), lambda h, i: (h, 0, 0))
    return qs, ks


def _mem_fwd(q, k, v, tb, name):
    qs, ks = _mem_specs(tb, k.shape[1])

    def body(q_ref, k_ref, v_ref, o_ref):
        o_ref[...] = _mem_fn(q_ref[...].astype(F32), k_ref[...], v_ref[...])

    return pl.pallas_call(
        body, name=name, grid=(MEM_HEADS, q.shape[1] // tb),
        in_specs=[qs, ks, ks], out_specs=qs, out_shape=jax.ShapeDtypeStruct(q.shape, F32),
        compiler_params=_cparams(("parallel", "parallel")),
    )(q, k, v)


def _mem_bwd(q, k, v, do, tb, name):
    qs, ks = _mem_specs(tb, k.shape[1])

    def body(q_ref, k_ref, v_ref, do_ref, dq_ref, dk_ref, dv_ref):
        i = pl.program_id(1)
        _, vjp = jax.vjp(_mem_fn, q_ref[...].astype(F32), k_ref[...], v_ref[...])
        dq, dk, dv = vjp(do_ref[...])
        dq_ref[...] = dq

        @pl.when(i == 0)
        def _():
            dk_ref[...] = dk
            dv_ref[...] = dv

        @pl.when(i > 0)
        def _():
            dk_ref[...] += dk
            dv_ref[...] += dv

    return pl.pallas_call(
        body, name=name, grid=(MEM_HEADS, q.shape[1] // tb),
        in_specs=[qs, ks, ks, qs], out_specs=[qs, ks, ks],
        out_shape=[jax.ShapeDtypeStruct(q.shape, F32), jax.ShapeDtypeStruct(k.shape, F32),
                   jax.ShapeDtypeStruct(k.shape, F32)],
        compiler_params=_cparams(("parallel", "arbitrary")),
    )(q, k, v, do)


CONV_PAD = 8


def _conv_post(kind, acc):
    s = _silu(acc)
    if kind == 2:
        return s
    scale = DN_HD ** -0.5 if kind == 0 else 1.0
    return s * lax.rsqrt(jnp.sum(s * s, axis=-1, keepdims=True) + EPS) * scale


def _conv_acc(xp_ref, w, r0, rt):
    acc = None
    for i in range(DN_CONV):
        term = w[i:i + 1, :] * xp_ref[pl.ds(CONV_PAD + r0 + i - DN_CONV // 2, rt), :]
        acc = term if acc is None else acc + term
    return acc


def _conv_fwd(xp, w8, kind, rt, name):
    t = xp.shape[0] - 2 * CONV_PAD

    def body(xp_ref, w_ref, o_ref):
        w = w_ref[...]
        for r in range(t // rt):
            o_ref[pl.ds(r * rt, rt), :] = _conv_post(kind, _conv_acc(xp_ref, w, r * rt, rt))

    return pl.pallas_call(
        body, name=name, grid=(DN_HEADS,),
        in_specs=[pl.BlockSpec((t + 2 * CONV_PAD, DN_HD), lambda j: (0, 6 * kind + j)),
                  pl.BlockSpec((8, DN_HD), lambda j: (0, 6 * kind + j))],
        out_specs=pl.BlockSpec((t, DN_HD), lambda j: (0, j)),
        out_shape=jax.ShapeDtypeStruct((t, TOK_W), F32),
        compiler_params=_cparams(("parallel",)),
    )(xp, w8)


def _conv_bwd(xp, w8, d_f, d_r, kind, rt, name):
    t = xp.shape[0] - 2 * CONV_PAD

    def body(xp_ref, w_ref, df_ref, dr_ref, dx_ref, dw_ref, dpad_ref):
        w = w_ref[...]
        zero = jnp.zeros((CONV_PAD, DN_HD), F32)
        dpad_ref[pl.ds(0, CONV_PAD), :] = zero
        dpad_ref[pl.ds(CONV_PAD + t, CONV_PAD), :] = zero
        dw = [jnp.zeros((1, DN_HD), F32) for _ in range(DN_CONV)]
        for r in range(t // rt):
            rows = pl.ds(r * rt, rt)
            acc = _conv_acc(xp_ref, w, r * rt, rt)
            _, vjp = jax.vjp(functools.partial(_conv_post, kind), acc)
            (dacc,) = vjp(df_ref[rows, :] + dr_ref[rows, :])
            dpad_ref[pl.ds(CONV_PAD + r * rt, rt), :] = dacc
            for i in range(DN_CONV):
                xs = xp_ref[pl.ds(CONV_PAD + r * rt + i - DN_CONV // 2, rt), :]
                dw[i] = dw[i] + jnp.sum(dacc * xs, axis=0, keepdims=True)
        dw_ref[...] = jnp.concatenate(dw + [jnp.zeros((8 - DN_CONV, DN_HD), F32)], axis=0)
        for r in range(t // rt):
            acc = None
            for i in range(DN_CONV):
                term = w[i:i + 1, :] * dpad_ref[pl.ds(CONV_PAD + r * rt - i + DN_CONV // 2, rt), :]
                acc = term if acc is None else acc + term
            dx_ref[pl.ds(r * rt, rt), :] = acc

    return pl.pallas_call(
        body, name=name, grid=(DN_HEADS,),
        in_specs=[pl.BlockSpec((t + 2 * CONV_PAD, DN_HD), lambda j: (0, 6 * kind + j)),
                  pl.BlockSpec((8, DN_HD), lambda j: (0, 6 * kind + j)),
                  pl.BlockSpec((t, DN_HD), lambda j: (0, j)),
                  pl.BlockSpec((t, DN_HD), lambda j: (0, j))],
        out_specs=[pl.BlockSpec((t, DN_HD), lambda j: (0, j)), pl.BlockSpec((8, DN_HD), lambda j: (0, j))],
        out_shape=[jax.ShapeDtypeStruct((t, TOK_W), F32), jax.ShapeDtypeStruct((8, TOK_W), F32)],
        scratch_shapes=[pltpu.VMEM((t + 2 * CONV_PAD, DN_HD), F32)],
        compiler_params=_cparams(("parallel",)),
    )(xp, w8, d_f, d_r)


def _softplus(x):
    e = jnp.exp(-jnp.abs(x))
    return jnp.maximum(x, 0.0) + jnp.where(e < 1e-4, e - 0.5 * e * e, jnp.log(1.0 + e))


_NN = (((2,), (1,)), ((0,), (0,)))
_NT = (((2,), (2,)), ((0,), (0,)))
_TN = (((1,), (1,)), ((0,), (0,)))


def _dot(a, b, dims=_NN):
    return lax.dot_general(a.astype(BF16), b.astype(BF16), dims, preferred_element_type=F32)


def _hi_lo(x):
    hi = x.astype(BF16)
    return hi, (x - hi.astype(F32)).astype(BF16)


def _mask_dot(mask_bf16, x, dims):
    x1 = x.astype(BF16)
    r = x - x1.astype(F32)
    x2, x3 = _hi_lo(r)
    d = functools.partial(lax.dot_general, dimension_numbers=dims, preferred_element_type=F32)
    return d(mask_bf16, x1) + d(mask_bf16, x2) + d(mask_bf16, x3)


@jax.custom_vjp
def _dot_mask(mask_bf16, x):
    return _mask_dot(mask_bf16, x, _NN)


def _dot_mask_fwd(mask_bf16, x):
    return _mask_dot(mask_bf16, x, _NN), mask_bf16


def _dot_mask_bwd(mask_bf16, ct):
    return jnp.zeros_like(mask_bf16), _mask_dot(mask_bf16, ct, _TN)


_dot_mask.defvjp(_dot_mask_fwd, _dot_mask_bwd)


def _dot3_raw(a, b, dims):
    a1, a2 = _hi_lo(a)
    b1, b2 = _hi_lo(b)
    d = functools.partial(lax.dot_general, dimension_numbers=dims, preferred_element_type=F32)
    return d(a1, b1) + d(a1, b2) + d(a2, b1)


@jax.custom_vjp
def _dot3(a, b):
    return _dot3_raw(a, b, _NN)


def _dot3_fwd(a, b):
    return _dot3_raw(a, b, _NN), (a, b)


def _dot3_bwd(res, ct):
    a, b = res
    return _dot3_raw(ct, b, _NT), _dot3_raw(a, ct, _TN)


_dot3.defvjp(_dot3_fwd, _dot3_bwd)


def _dn_chunk(q, k, v, al, be, alc, a_row, dt_row, a_rowc, dt_rowc, s):
    n, c = q.shape[0], DN_CHUNK
    rev = lax.broadcasted_iota(jnp.int32, (n, c, c), 0) >= n // 2
    row = lax.broadcasted_iota(jnp.int32, (n, c, c), 1)
    col = lax.broadcasted_iota(jnp.int32, (n, c, c), 2)
    ahead = jnp.where(rev, col - row, row - col)
    incl = ahead >= 0
    strict = ahead > 0
    incl_b = incl.astype(BF16)

    g = -jnp.exp(a_row) * _softplus(al + dt_row)
    beta = _sigmoid(be)
    g_c = -jnp.exp(a_rowc) * _softplus(alc + dt_rowc)
    gc = _dot_mask(incl_b, g)
    gcc = _dot_mask(incl_b, g_c)
    decay = jnp.exp(jnp.where(incl, gcc - jnp.swapaxes(gcc, 1, 2), NEG))
    kb = k * beta
    lmat = jnp.where(strict, _dot(kb, k, _NT) * decay, 0.0)
    rhs = jnp.concatenate([v * beta, kb * jnp.exp(gc)], axis=2)
    xp = -lmat
    sol = rhs + _dot3(xp, rhs)
    for _ in range(5):
        xp = _dot3(xp, xp)
        sol = sol + _dot3(xp, sol)
    u, w = sol[:, :, :DN_HD], sol[:, :, DN_HD:]
    intra = jnp.where(incl, _dot(q, k, _NT) * decay, 0.0)
    v_new = u - _dot(w, s)
    out = _dot(q * jnp.exp(gc), s) + _dot(intra, v_new)
    g_last = jnp.sum(g, axis=1, keepdims=True)
    s_new = s * jnp.exp(g_last) + _dot(k * jnp.exp(g_last - gc), v_new, _TN)
    return out, s_new


DN_HG = 6


def _dn_load(f_refs, r_refs, alf, bef, alr, ber, a_ref, dt_ref):
    c, hg = DN_CHUNK, DN_HG
    sls = [slice(DN_HD * h, DN_HD * (h + 1)) for h in range(hg)]
    toks = [jnp.stack([f[:, sl] for sl in sls] + [r[:, sl] for sl in sls]) for f, r in zip(f_refs, r_refs)]
    al = jnp.concatenate([alf[...], alr[...]], axis=0)
    be = jnp.concatenate([bef[...], ber[...]], axis=0)
    alc = jnp.concatenate([alf[:, :, 0:c], alr[:, :, 0:c]], axis=0)
    a = jnp.concatenate([a_ref[0], a_ref[1]], axis=0)
    dt = jnp.concatenate([dt_ref[0], dt_ref[1]], axis=0)
    ac = jnp.concatenate([a_ref[0, :, :, 0:c], a_ref[1, :, :, 0:c]], axis=0)
    dtc = jnp.concatenate([dt_ref[0, :, :, 0:c], dt_ref[1, :, :, 0:c]], axis=0)
    return toks, (al, be, alc, a, dt, ac, dtc)


def _dn_views(nc, bwd):
    c, hg = DN_CHUNK, DN_HG
    if bwd:
        f_blk = lambda s: nc - 1 - s
        r_blk = lambda s: s
        st_blk = lambda s: nc - 1 - s
    else:
        f_blk = lambda s: s
        r_blk = lambda s: nc - 1 - s
        st_blk = lambda s: s
    tok_f = pl.BlockSpec((c, hg * DN_HD), lambda g, s: (f_blk(s), g))
    tok_r = pl.BlockSpec((c, hg * DN_HD), lambda g, s: (r_blk(s), g))
    gate_f = pl.BlockSpec((None, hg, c, DN_HD), lambda g, s: (0, g, f_blk(s), 0))
    gate_r = pl.BlockSpec((None, hg, c, DN_HD), lambda g, s: (1, g, r_blk(s), 0))
    par = pl.BlockSpec((2, hg, 1, DN_HD), lambda g, s: (0, g, 0, 0))
    state = pl.BlockSpec((2, hg, None, DN_HD, DN_HD), lambda g, s: (0, g, st_blk(s), 0, 0))
    return tok_f, tok_r, gate_f, gate_r, par, state


def _dn_fwd(q, k, v, al, be, a_rows, dt_rows, name):
    t = q.shape[0]
    c, hg = DN_CHUNK, DN_HG
    nc = t // c
    tok_f, tok_r, gate_f, gate_r, par, state = _dn_views(nc, False)

    def body(qf, kf, vf, qr, kr, vr, alf, bef, alr, ber, a_ref, dt_ref, of_ref, or_ref, st_ref, s_ref):
        @pl.when(pl.program_id(1) == 0)
        def _():
            s_ref[...] = jnp.zeros_like(s_ref)

        (q_, k_, v_), gates = _dn_load((qf, kf, vf), (qr, kr, vr), alf, bef, alr, ber, a_ref, dt_ref)
        s = s_ref[...]
        st_ref[0] = s[:hg]
        st_ref[1] = s[hg:]
        out, s_new = _dn_chunk(q_, k_, v_, *gates, s)
        for h in range(hg):
            sl = slice(DN_HD * h, DN_HD * (h + 1))
            of_ref[:, sl] = out[h]
            or_ref[:, sl] = out[hg + h]
        s_ref[...] = s_new

    return pl.pallas_call(
        body, name=name, grid=(DN_HEADS // hg, nc),
        in_specs=[tok_f] * 3 + [tok_r] * 3 + [gate_f, gate_f, gate_r, gate_r, par, par],
        out_specs=[tok_f, tok_r, state],
        out_shape=[jax.ShapeDtypeStruct((t, TOK_W), F32)] * 2
        + [jax.ShapeDtypeStruct((2, DN_HEADS, nc, DN_HD, DN_HD), F32)],
        scratch_shapes=[pltpu.VMEM((2 * hg, DN_HD, DN_HD), F32)],
        compiler_params=_cparams(("parallel", "arbitrary")),
    )(q, k, v, q, k, v, al, be, al, be, a_rows, dt_rows)


def _dn_bwd(q, k, v, al, be, a_rows, dt_rows, states, do, name):
    t = q.shape[0]
    c, hg = DN_CHUNK, DN_HG
    nc = t // c
    tok_f, tok_r, gate_f, gate_r, par, state = _dn_views(nc, True)
    gout_f = pl.BlockSpec((hg, c, DN_HD), lambda g, s: (g, nc - 1 - s, 0))
    gout_r = pl.BlockSpec((hg, c, DN_HD), lambda g, s: (g, s, 0))

    def body(qf, kf, vf, qr, kr, vr, alf, bef, alr, ber, a_ref, dt_ref, st_ref, dof, dor,
             dqf, dkf, dvf, dqr, dkr, dvr, dalf, dbef, dalr, dber, da_ref, ddt_ref, ds_ref):
        first = pl.program_id(1) == 0

        @pl.when(first)
        def _():
            ds_ref[...] = jnp.zeros_like(ds_ref)
            da_ref[...] = jnp.zeros_like(da_ref)
            ddt_ref[...] = jnp.zeros_like(ddt_ref)

        def lanes(x):
            return jnp.sum(x, axis=-1, keepdims=True)

        (q_, k_, v_, do_), gates = _dn_load((qf, kf, vf, dof), (qr, kr, vr, dor), alf, bef, alr, ber, a_ref, dt_ref)
        s = jnp.concatenate([st_ref[0], st_ref[1]], axis=0)
        _, vjp = jax.vjp(_dn_chunk, q_, k_, v_, *gates, s)
        dq, dk, dv, dal, dbe, dalc, da, ddt, dac, ddtc, ds = vjp((do_, ds_ref[...]))
        for h in range(hg):
            sl = slice(DN_HD * h, DN_HD * (h + 1))
            dqf[:, sl], dkf[:, sl], dvf[:, sl] = dq[h], dk[h], dv[h]
            dqr[:, sl], dkr[:, sl], dvr[:, sl] = dq[hg + h], dk[hg + h], dv[hg + h]
        dal = jnp.broadcast_to(lanes(dal) + lanes(dalc), dal.shape)
        dbe = jnp.broadcast_to(lanes(dbe), dbe.shape)
        dalf[...], dalr[...] = dal[:hg], dal[hg:]
        dbef[...], dber[...] = dbe[:hg], dbe[hg:]
        da = jnp.broadcast_to(lanes(da) + lanes(dac), da.shape)
        ddt = jnp.broadcast_to(lanes(ddt) + lanes(ddtc), ddt.shape)
        da_ref[0] += da[:hg]
        da_ref[1] += da[hg:]
        ddt_ref[0] += ddt[:hg]
        ddt_ref[1] += ddt[hg:]
        ds_ref[...] = ds

    tok = jax.ShapeDtypeStruct((t, TOK_W), F32)
    gate = jax.ShapeDtypeStruct((DN_HEADS, t, DN_HD), F32)
    parsh = jax.ShapeDtypeStruct((2, DN_HEADS, 1, DN_HD), F32)
    res = pl.pallas_call(
        body, name=name, grid=(DN_HEADS // hg, nc),
        in_specs=[tok_f] * 3 + [tok_r] * 3 + [gate_f, gate_f, gate_r, gate_r, par, par, state, tok_f, tok_r],
        out_specs=[tok_f] * 3 + [tok_r] * 3 + [gout_f, gout_f, gout_r, gout_r, par, par],
        out_shape=[tok] * 6 + [gate] * 4 + [parsh] * 2,
        scratch_shapes=[pltpu.VMEM((2 * hg, DN_HD, DN_HD), F32)],
        compiler_params=_cparams(("parallel", "arbitrary")),
    )(q, k, v, q, k, v, al, be, al, be, a_rows, dt_rows, states, do, do)
    dqf, dkf, dvf, dqr, dkr, dvr, dalf, dbef, dalr, dber, da, ddt = res
    return (dqf, dkf, dvf), (dqr, dkr, dvr), jnp.stack([dalf, dalr]), jnp.stack([dbef, dber]), da, ddt


BAND_BQ = 256
ROW_TB = 256
MEM_TB = 512
CONV_RT = 512


def _to_sub(x, dil):
    l = x.shape[0] // dil
    return x.reshape(l, dil, 4, ATT_HD).transpose(2, 1, 0, 3).reshape(4 * dil, l, ATT_HD)


def _from_sub(x, dil):
    l = x.shape[1]
    return x.reshape(4, dil, l, ATT_HD).transpose(2, 1, 0, 3).reshape(l * dil, 4 * ATT_HD)


def _heads_major(x):
    return x.reshape(x.shape[0], MEM_HEADS, ATT_HD).transpose(1, 0, 2)


def _heads_minor(x):
    return x.transpose(1, 0, 2).reshape(x.shape[1], MEM_HEADS * ATT_HD)


def _mem_kv_fwd(mem, gain, w_kv, li):
    (memn,) = _rowwise(_fn_pre, [mem], [gain], [(D, BF16)], mem.shape[0], f"memnorm_fwd{li}")
    kv = _matmul(memn, w_kv, "nn", F32, f"memkv_fwd{li}")
    return _heads_major(kv[:, :MEM_W]), _heads_major(kv[:, MEM_W:]), memn


def _mem_kv_bwd(mem, gain, w_kv, memn, dkm, dvm, li):
    dkv = jnp.concatenate([_heads_minor(dkm), _heads_minor(dvm)], axis=1).astype(BF16)
    dw = _matmul(memn, dkv, "tn", BF16, f"memkv_dw{li}")
    dmemn = _matmul(dkv, w_kv, "nt", F32, f"memkv_dx{li}")
    _, (dgain,) = _rowwise_bwd(_fn_pre, [mem], [gain], [dmemn], [None], mem.shape[0], f"memnorm_bwd{li}")
    return dw, dgain


def _attn_mixer_fwd(p, rel_bias, km, vm):
    t = p.shape[0]
    saved, outs, lses = [], [], []
    for gi, (_, dil) in enumerate(DIL_GROUPS):
        l_sub = t // dil
        bq = min(BAND_BQ, l_sub)
        q = _to_sub(p[:, 256 * gi:256 * (gi + 1)], dil)
        pad = ((0, 0), (BAND_HALF, BAND_HALF), (0, 0))
        k = jnp.pad(_to_sub(p[:, TOK_W + 256 * gi:TOK_W + 256 * (gi + 1)], dil), pad)
        v = jnp.pad(_to_sub(p[:, 2 * TOK_W + 256 * gi:2 * TOK_W + 256 * (gi + 1)], dil), pad)
        bm = _band_bias(rel_bias, gi, dil, bq)
        o, lse = _band_fwd(q, k, v, bm, dil, l_sub, bq, f"band_fwd{gi}")
        outs.append(_from_sub(o, dil))
        lses.append(_from_sub(lse, dil))
        saved.append((q, k, v, bm))
    o_all = jnp.concatenate(outs, axis=1)
    lse_all = jnp.concatenate(lses, axis=1)
    (mixed,) = _rowwise(_fn_combine, [o_all, lse_all], [], [(TOK_W, BF16)], ROW_TB, "combine_fwd")
    qm = _heads_major(p[:, 3 * TOK_W:])
    memo = _mem_fwd(qm, km, vm, min(MEM_TB, t), "mem_fwd0")
    cat = jnp.concatenate([mixed, _heads_minor(memo).astype(BF16)], axis=1)
    return cat, (saved, o_all, lse_all, qm)


def _attn_mixer_bwd(dcat, res, km, vm):
    saved, o_all, lse_all, qm = res
    t = dcat.shape[0]
    (do_all, dlse_all), _ = _rowwise_bwd(_fn_combine, [o_all, lse_all], [], [dcat[:, :TOK_W]], [F32, F32],
                                         ROW_TB, "combine_bwd")
    dqs, dks, dvs, dbms = [], [], [], []
    for gi, (_, dil) in enumerate(DIL_GROUPS):
        l_sub = t // dil
        bq = min(BAND_BQ, l_sub)
        q, k, v, bm = saved[gi]
        do = _to_sub(do_all[:, 256 * gi:256 * (gi + 1)], dil)
        dl = _to_sub(dlse_all[:, 256 * gi:256 * (gi + 1)], dil)
        dq, dk, dv, dbm = _band_bwd(q, k, v, bm, do, dl, dil, l_sub, bq, f"band_bwd{gi}")
        dqs.append(_from_sub(dq, dil))
        dks.append(_from_sub(dk[:, BAND_HALF:-BAND_HALF], dil))
        dvs.append(_from_sub(dv[:, BAND_HALF:-BAND_HALF], dil))
        dbms.append(dbm)
    dqm, dkm, dvm = _mem_bwd(qm, km, vm, _heads_major(dcat[:, TOK_W:]), min(MEM_TB, t), "mem_bwd0")
    dp = jnp.concatenate(dqs + dks + dvs + [_heads_minor(dqm)], axis=1).astype(BF16)
    return dp, _relbias_grad(dbms, "relbias_grad"), dkm, dvm


def _dn_mixer_fwd(p, conv_w, a_log, dt_bias, out_norm, km, vm):
    t = p.shape[0]
    rt = min(CONV_RT, t)
    xp = jnp.pad(p[:, :3 * TOK_W], ((CONV_PAD, CONV_PAD), (0, 0)))
    w8 = jnp.pad(conv_w, ((0, 8 - DN_CONV), (0, 0)))
    q = _conv_fwd(xp, w8, 0, rt, "conv_fwd_q")
    k = _conv_fwd(xp, w8, 1, rt, "conv_fwd_k")
    v = _conv_fwd(xp, w8, 2, rt, "conv_fwd_v")
    gate = p[:, 4 * TOK_W:4 * TOK_W + 4 * DN_HEADS].reshape(t, 2, 2, DN_HEADS)
    bshape = (2, DN_HEADS, t, DN_HD)
    al = jnp.broadcast_to(gate[:, :, 0, :].transpose(1, 2, 0)[..., None], bshape)
    be = jnp.broadcast_to(gate[:, :, 1, :].transpose(1, 2, 0)[..., None], bshape)
    a_rows = jnp.broadcast_to(a_log[:, :, None, None], (2, DN_HEADS, 1, DN_HD))
    dt_rows = jnp.broadcast_to(dt_bias[:, :, None, None], (2, DN_HEADS, 1, DN_HD))
    o_f, o_r, states = _dn_fwd(q, k, v, al, be, a_rows, dt_rows, "dn_fwd")
    z = p[:, 3 * TOK_W:4 * TOK_W]
    gain = out_norm.reshape(1, DN_HD)
    (og,) = _rowwise(_fn_outnorm, [o_f, o_r, z], [gain], [(TOK_W, BF16)], ROW_TB, "outnorm_fwd")
    qm = _heads_major(p[:, 4 * TOK_W + 4 * DN_HEADS:DN_IN])
    memo = _mem_fwd(qm, km, vm, min(MEM_TB, t), "mem_fwd1")
    cat = jnp.concatenate([og, _heads_minor(memo).astype(BF16)], axis=1)
    return cat, (xp, w8, q, k, v, al, be, a_rows, dt_rows, o_f, o_r, states, z, gain, qm)


def _dn_mixer_bwd(dcat, res, km, vm):
    xp, w8, q, k, v, al, be, a_rows, dt_rows, o_f, o_r, states, z, gain, qm = res
    t = dcat.shape[0]
    rt = min(CONV_RT, t)
    (do, dz), (dgain,) = _rowwise_bwd(_fn_outnorm, [o_f, o_r, z], [gain], [dcat[:, :TOK_W]], [F32, None, F32],
                                      ROW_TB, "outnorm_bwd")
    d_f, d_r, dal, dbe, da, ddt = _dn_bwd(q, k, v, al, be, a_rows, dt_rows, states, do, "dn_bwd")
    dxs, dws = [], []
    for kind, nm in enumerate("qkv"):
        dx, dw = _conv_bwd(xp, w8, d_f[kind], d_r[kind], kind, rt, f"conv_bwd_{nm}")
        dxs.append(dx)
        dws.append(dw)
    dconv = jnp.concatenate(dws, axis=1)[:DN_CONV]
    dgate = jnp.stack([dal[..., 0], dbe[..., 0]], axis=1).transpose(3, 0, 1, 2).reshape(t, 4 * DN_HEADS)
    dqm, dkm, dvm = _mem_bwd(qm, km, vm, _heads_major(dcat[:, TOK_W:]), min(MEM_TB, t), "mem_bwd1")
    dp = jnp.concatenate(dxs + [dz, dgate, _heads_minor(dqm), jnp.zeros((t, DN_IN_PAD - DN_IN), F32)],
                         axis=1).astype(BF16)
    return dp, dconv, da[:, :, 0, 0], ddt[:, :, 0, 0], dgain.reshape(DN_HD), dkm, dvm


SWI_TB = 256


def _ffn_fwd(h, w_gu_t, w_d, li):
    gu = _matmul(h, w_gu_t, "nt", BF16, f"ffn_gu{li}")
    (a,) = _rowwise(_fn_swiglu, [gu], [], [(D_FF, BF16)], SWI_TB, f"swiglu_fwd{li}")
    return _matmul(a, w_d, "nn", F32, f"ffn_down{li}"), gu, a


def _ffn_bwd(df, h, w_gu_t, w_d, gu, a, li):
    da = _matmul(df, w_d, "nt", BF16, f"ffn_down_dx{li}")
    dwd = _matmul(a, df, "tn", BF16, f"ffn_down_dw{li}")
    (dgu,), _ = _rowwise_bwd(_fn_swiglu, [gu], [], [da], [BF16], SWI_TB, f"swiglu_bwd{li}")
    dh = _matmul(dgu, w_gu_t, "nn", F32, f"ffn_gu_dx{li}")
    dwgu_t = _matmul(dgu, h, "tn", BF16, f"ffn_gu_dw{li}")
    return dh, dwgu_t, dwd


def _fn_first(x, g):
    return x, _rms(x, g)


def _me_xyc():
    return lax.axis_index("x"), lax.axis_index("y"), lax.axis_index("c")


def _flip(coords, k):
    x, y, c = coords
    return (1 - x if k & 4 else x, 1 - y if k & 2 else y, 1 - c if k & 1 else c)


def _index(coords):
    x, y, c = coords
    return 4 * x + 2 * y + c


def _window(ref, axis, size, d):
    idx = [slice(None)] * len(ref.shape)
    idx[axis] = pl.ds(pl.multiple_of(d * size, size), size)
    return ref.at[tuple(idx)]


def _comm_call(body, n, ins, out_shapes, name):
    hbm = pl.BlockSpec(memory_space=pl.ANY)
    return pl.pallas_call(
        body, name=name, in_specs=[hbm] * n, out_specs=[hbm] * n, out_shape=out_shapes,
        scratch_shapes=[pltpu.SemaphoreType.DMA((N_DEV - 1, n)), pltpu.SemaphoreType.DMA((N_DEV - 1, n)),
                        pltpu.SemaphoreType.DMA((n,))],
    )(*ins)


def _run_exchange(n, local, remote, send_sems, recv_sems):
    me = _me_xyc()
    locs = [local(p) for p in range(n)]
    for cp in locs:
        cp.start()
    sends = [remote(k, p, me, _flip(me, k)) for k in range(1, N_DEV) for p in range(n)]
    for cp in sends:
        cp.start()
    for k in range(1, N_DEV):
        for p in range(n):
            remote(k, p, _flip(me, k), me).wait_recv()
    for cp in sends:
        cp.wait_send()
    for cp in locs:
        cp.wait()


def _all_gather(shards, axes, name):
    n = len(shards)
    sizes = [s.shape[a] for s, a in zip(shards, axes)]

    def body(*refs):
        ins, outs = refs[:n], refs[n:2 * n]
        send_sems, recv_sems, loc_sems = refs[2 * n:]
        me = _me_xyc()

        def local(p):
            return pltpu.make_async_copy(ins[p], _window(outs[p], axes[p], sizes[p], _index(me)), loc_sems.at[p])

        def remote(k, p, owner, to):
            return pltpu.make_async_remote_copy(
                src_ref=ins[p], dst_ref=_window(outs[p], axes[p], sizes[p], _index(owner)),
                send_sem=send_sems.at[k - 1, p], recv_sem=recv_sems.at[k - 1, p], device_id=to, device_id_type=MESH)

        _run_exchange(n, local, remote, send_sems, recv_sems)

    def full(s, a):
        return s.shape[:a] + (N_DEV * s.shape[a],) + s.shape[a + 1:]

    return _comm_call(body, n, shards, [jax.ShapeDtypeStruct(full(s, a), s.dtype) for s, a in zip(shards, axes)], name)


def _exchange(fulls, axes, name):
    n = len(fulls)
    sizes = [None if a is None else f.shape[a] // N_DEV for f, a in zip(fulls, axes)]

    def part_shape(f, a):
        return f.shape if a is None else f.shape[:a] + (f.shape[a] // N_DEV,) + f.shape[a + 1:]

    def body(*refs):
        ins, outs = refs[:n], refs[n:2 * n]
        send_sems, recv_sems, loc_sems = refs[2 * n:]
        me = _me_xyc()

        def src(p, to):
            return ins[p] if axes[p] is None else _window(ins[p], axes[p], sizes[p], _index(to))

        def local(p):
            return pltpu.make_async_copy(src(p, me), outs[p].at[_index(me)], loc_sems.at[p])

        def remote(k, p, sender, to):
            return pltpu.make_async_remote_copy(
                src_ref=src(p, to), dst_ref=outs[p].at[_index(sender)],
                send_sem=send_sems.at[k - 1, p], recv_sem=recv_sems.at[k - 1, p], device_id=to, device_id_type=MESH)

        _run_exchange(n, local, remote, send_sems, recv_sems)

    return _comm_call(body, n, fulls,
                      [jax.ShapeDtypeStruct((N_DEV,) + part_shape(f, a), f.dtype) for f, a in zip(fulls, axes)], name)


_HBM = pl.BlockSpec(memory_space=pltpu.HBM)
_SEM = pl.BlockSpec(memory_space=pltpu.SEMAPHORE)
_EFFECT = pltpu.SideEffectType.DATAFLOW_SIDE_EFFECTING


def _in_hbm(a):
    return pltpu.with_memory_space_constraint(a, pltpu.HBM)


def _split_start(srcs, lands, after, descr, name):
    n = len(srcs)

    def body(*refs):
        ins, lnd = refs[:n], refs[n:2 * n]
        send_sems, recv_sems = refs[2 * n + 1], refs[2 * n + 2]
        token = refs[-1]
        me = _me_xyc()
        for k in range(1, N_DEV):
            for p in range(n):
                descr(k, p, ins, lnd, send_sems, recv_sems, me, _flip(me, k)).start()
        token[...] = jnp.zeros_like(token)

    sems = pltpu.SemaphoreType.DMA(((N_DEV - 1) * n,))
    res = pl.pallas_call(
        body, name=name,
        out_shape=(sems, sems, *[pltpu.HBM(a.shape, a.dtype) for a in (*srcs, *lands)],
                   jax.ShapeDtypeStruct((8, 128), F32)),
        in_specs=[_HBM] * (2 * n) + [pl.BlockSpec(memory_space=pl.ANY)],
        out_specs=(_SEM, _SEM, *[_HBM] * (2 * n), pl.BlockSpec(memory_space=pltpu.VMEM)),
        input_output_aliases={i: 2 + i for i in range(2 * n)},
        compiler_params=pltpu.CompilerParams(has_side_effects=_EFFECT),
    )(*[_in_hbm(a) for a in (*srcs, *lands)], after)
    return res[0], res[1], res[2:2 + n], res[2 + n:2 + 2 * n], res[-1]


def _split_wait(send_sems, recv_sems, srcs, lands, after, descr, name):
    n = len(srcs)

    def body(*refs):
        ins, lnd = refs[:n], refs[n:2 * n]
        s_sems, r_sems = refs[2 * n], refs[2 * n + 1]
        me = _me_xyc()
        for k in range(1, N_DEV):
            for p in range(n):
                peer = _flip(me, k)
                descr(k, p, ins, lnd, s_sems, r_sems, me, peer).wait_send()
                descr(k, p, ins, lnd, s_sems, r_sems, peer, me).wait_recv()

    res = pl.pallas_call(
        body, name=name,
        out_shape=tuple(pltpu.HBM(a.shape, a.dtype) for a in (*srcs, *lands)),
        in_specs=[_HBM] * (2 * n) + [_SEM, _SEM, pl.BlockSpec(memory_space=pl.ANY)],
        out_specs=tuple([_HBM] * (2 * n)),
        input_output_aliases={i: i for i in range(2 * n)},
        compiler_params=pltpu.CompilerParams(has_side_effects=_EFFECT),
    )(*srcs, *lands, send_sems, recv_sems, after)
    return list(res[n:])


def _gather_descr(axes, sizes):
    def descr(k, p, ins, lnd, send_sems, recv_sems, sender, dest):
        return pltpu.make_async_remote_copy(
            src_ref=ins[p], dst_ref=_window(lnd[p], axes[p], sizes[p], _index(sender)),
            send_sem=send_sems.at[(k - 1) * len(axes) + p], recv_sem=recv_sems.at[(k - 1) * len(axes) + p],
            device_id=dest, device_id_type=MESH)
    return descr


def _exchange_descr(axes, sizes):
    def descr(k, p, ins, lnd, send_sems, recv_sems, sender, dest):
        return pltpu.make_async_remote_copy(
            src_ref=_window(ins[p], axes[p], sizes[p], _index(dest)), dst_ref=lnd[p].at[_index(sender)],
            send_sem=send_sems.at[(k - 1) * len(axes) + p], recv_sem=recv_sems.at[(k - 1) * len(axes) + p],
            device_id=dest, device_id_type=MESH)
    return descr


def _gather_begin(shards, axes, after, name):
    sizes = [s.shape[a] for s, a in zip(shards, axes)]
    me = _index(_me_xyc())
    lands = []
    for s, a, sz in zip(shards, axes, sizes):
        full = s.shape[:a] + (N_DEV * sz,) + s.shape[a + 1:]
        lands.append(lax.dynamic_update_slice_in_dim(lax.empty(full, s.dtype), s, me * sz, a))
    descr = _gather_descr(axes, sizes)
    send_sems, recv_sems, srcs, lands, token = _split_start(shards, lands, after, descr, name)
    return (send_sems, recv_sems, srcs, lands, descr), token


def _exchange_begin(fulls, axes, after, name):
    sizes = [f.shape[a] // N_DEV for f, a in zip(fulls, axes)]
    me = _index(_me_xyc())
    lands = []
    for f, a, sz in zip(fulls, axes, sizes):
        own = lax.dynamic_slice_in_dim(f, me * sz, sz, a)
        lands.append(lax.dynamic_update_slice_in_dim(lax.empty((N_DEV,) + own.shape, f.dtype), own[None], me, 0))
    descr = _exchange_descr(axes, sizes)
    send_sems, recv_sems, srcs, lands, token = _split_start(fulls, lands, after, descr, name)
    return (send_sems, recv_sems, srcs, lands, descr), token


def _split_end(handle, after, name):
    send_sems, recv_sems, srcs, lands, descr = handle
    return _split_wait(send_sems, recv_sems, srcs, lands, after, descr, name)


def _adam_math(g, w, m, v):
    m = ADAM_B1 * m + (1.0 - ADAM_B1) * g
    v = ADAM_B2 * v + (1.0 - ADAM_B2) * (g * g)
    m_hat = m / (1.0 - ADAM_B1 ** ADAM_STEP)
    v_hat = v / (1.0 - ADAM_B2 ** ADAM_STEP)
    delta = -ADAM_LR * (m_hat / (jnp.sqrt(v_hat) + ADAM_EPS) + ADAM_WD * w)
    return delta, m, v


def _sum_slabs(r_ref):
    g = r_ref[0].astype(F32)
    for s in range(1, N_DEV):
        g = g + r_ref[s].astype(F32)
    return g


def _adamw_reduce(recv, w, m, v, tb, name):
    r, c = w.shape

    def body(r_ref, w_ref, m_ref, v_ref, g_ref, d_ref, nm_ref, nv_ref):
        g = _sum_slabs(r_ref)
        g_ref[...] = g
        d_ref[...], nm_ref[...], nv_ref[...] = _adam_math(g, w_ref[...], m_ref[...], v_ref[...])

    blk = pl.BlockSpec((tb, c), lambda i: (i, 0))
    return pl.pallas_call(
        body, name=name, grid=(r // tb,),
        in_specs=[pl.BlockSpec((N_DEV, tb, c), lambda i: (0, i, 0)), blk, blk, blk],
        out_specs=[blk] * 4, out_shape=[jax.ShapeDtypeStruct((r, c), F32)] * 4,
        compiler_params=_cparams(("parallel",)),
    )(recv, w, m, v)


def _reduce8(recv, tb, name):
    r, c = recv.shape[1:]

    def body(r_ref, g_ref):
        g_ref[...] = _sum_slabs(r_ref)

    return pl.pallas_call(
        body, name=name, grid=(r // tb,),
        in_specs=[pl.BlockSpec((N_DEV, tb, c), lambda i: (0, i, 0))],
        out_specs=pl.BlockSpec((tb, c), lambda i: (i, 0)), out_shape=jax.ShapeDtypeStruct((r, c), F32),
        compiler_params=_cparams(("parallel",)),
    )(recv)


def _adamw(g, w, m, v, tb, name):
    r, c = w.shape

    def body(g_ref, w_ref, m_ref, v_ref, d_ref, nm_ref, nv_ref):
        d_ref[...], nm_ref[...], nv_ref[...] = _adam_math(g_ref[...], w_ref[...], m_ref[...], v_ref[...])

    blk = pl.BlockSpec((tb, c), lambda i: (i, 0))
    return pl.pallas_call(
        body, name=name, grid=(r // tb,), in_specs=[blk] * 4, out_specs=[blk] * 3,
        out_shape=[jax.ShapeDtypeStruct((r, c), F32)] * 3, compiler_params=_cparams(("parallel",)),
    )(g, w, m, v)


DN_IN_SHARD = DN_IN // N_DEV
DN_IN_SHARD_PAD = 432
CONV_SHARD = (1, DN_CONV, 288)


def _pack_small(arrs, rows):
    flat = jnp.concatenate([a.astype(F32).reshape(-1) for a in arrs])
    return jnp.pad(flat, (0, rows * PACK_C - flat.size)).reshape(rows, PACK_C)


def _unpack_small(packed, shapes):
    flat, out, off = packed.reshape(-1), [], 0
    for shp in shapes:
        n = int(np.prod(shp))
        out.append(flat[off:off + n].reshape(shp))
        off += n
    return out


def kernel(x, mem, rel_bias, att_w_in, att_w_out, dn_w_in, dn_conv, dn_a_log, dn_dt_bias, dn_out_norm, dn_w_out, mem_norm, mem_w_kv, norm_mix_pre, norm_mix_post, norm_ffn_pre, norm_ffn_post, ffn_w_gate_up, ffn_w_down, loss_target, m_rel_bias, m_att_w_in, m_att_w_out, m_dn_w_in, m_dn_conv, m_dn_a_log, m_dn_dt_bias, m_dn_out_norm, m_dn_w_out, m_mem_norm, m_mem_w_kv, m_norm_mix_pre, m_norm_mix_post, m_norm_ffn_pre, m_norm_ffn_post, m_ffn_w_gate_up, m_ffn_w_down, v_rel_bias, v_att_w_in, v_att_w_out, v_dn_w_in, v_dn_conv, v_dn_a_log, v_dn_dt_bias, v_dn_out_norm, v_dn_w_out, v_mem_norm, v_mem_w_kv, v_norm_mix_pre, v_norm_mix_post, v_norm_ffn_pre, v_norm_ffn_post, v_ffn_w_gate_up, v_ffn_w_down):
    x0, mem0, tgt = x[0], mem[0], loss_target[0]
    t = x0.shape[0]
    axes = ("x", "y", "c")

    def t_shard(w):
        return jnp.swapaxes(w, 1, 2).astype(BF16)

    dn_in_pad = ((0, 0), (0, DN_IN_SHARD_PAD - DN_IN_SHARD), (0, 0))
    w_att_in_t, w_att_out, w_kv, conv_g = _all_gather(
        [t_shard(att_w_in), att_w_out.astype(BF16), mem_w_kv.astype(BF16), dn_conv], [1, 1, 1, 0], "allgather_first")
    w_att_in_t, w_att_out = w_att_in_t[0], w_att_out[0]
    conv_full = conv_g.transpose(1, 0, 2).reshape(DN_CONV, 3 * TOK_W)
    gu_t, down = t_shard(ffn_w_gate_up), ffn_w_down.astype(BF16)
    gather_a, tok_a = _gather_begin([gu_t[0:1], down[0:1]], [1, 1], w_att_out, "gather_ffn0_start")
    gather_b, tok_b = _gather_begin(
        [jnp.pad(t_shard(dn_w_in), dn_in_pad), dn_w_out.astype(BF16), gu_t[1:2], down[1:2]], [1, 1, 1, 1],
        tok_a, "gather_layer1_start")

    def gain(a, i):
        return a[i].reshape(1, D)

    (h0,) = _rowwise(_fn_pre, [x0], [gain(norm_mix_pre, 0) + tok_b[0:1, 0:1]], [(D, BF16)], ROW_TB, "pre0")
    km0, vm0, memn0 = _mem_kv_fwd(mem0, gain(mem_norm, 0), w_kv[0], 0)
    p0 = _matmul(h0, w_att_in_t, "nt", BF16, "att_in")
    cat0, res0 = _attn_mixer_fwd(p0, rel_bias, km0, vm0)
    y0 = _matmul(cat0, w_att_out, "nn", F32, "att_out")
    g_a = [gain(norm_mix_post, 0), gain(norm_ffn_pre, 0)]
    x1, h1 = _rowwise(_fn_res_pre, [x0, y0], g_a, [(D, F32), (D, BF16)], ROW_TB, "res_pre0")
    w_gu_t0, w_down0 = [w[0] for w in _split_end(gather_a, h1, "gather_ffn0_wait")]
    f0, gu0, a0 = _ffn_fwd(h1, w_gu_t0, w_down0, 0)
    g_b = [gain(norm_ffn_post, 0), gain(norm_mix_pre, 1)]
    x2, h2 = _rowwise(_fn_res_pre, [x1, f0], g_b, [(D, F32), (D, BF16)], ROW_TB, "res_pre1")
    km1, vm1, memn1 = _mem_kv_fwd(mem0, gain(mem_norm, 1), w_kv[1], 1)
    w_dn_in_g, w_dn_out, w_gu_t1, w_down1 = [w[0] for w in _split_end(gather_b, h2, "gather_layer1_wait")]
    w_dn_in_t = jnp.concatenate(
        [w_dn_in_g[DN_IN_SHARD_PAD * j:DN_IN_SHARD_PAD * j + DN_IN_SHARD] for j in range(N_DEV)]
        + [jnp.zeros((DN_IN_PAD - DN_IN, D), BF16)], axis=0)
    p1 = _matmul(h2, w_dn_in_t, "nt", F32, "dn_in")
    cat1, res1 = _dn_mixer_fwd(p1, conv_full, dn_a_log[0], dn_dt_bias[0], dn_out_norm[0], km1, vm1)
    y1 = _matmul(cat1, w_dn_out, "nn", F32, "dn_out")
    g_c = [gain(norm_mix_post, 1), gain(norm_ffn_pre, 1)]
    x3, h3 = _rowwise(_fn_res_pre, [x2, y1], g_c, [(D, F32), (D, BF16)], ROW_TB, "res_pre2")
    f1, gu1, a1 = _ffn_fwd(h3, w_gu_t1, w_down1, 1)
    g_d = [gain(norm_ffn_post, 1)]
    (x4,) = _rowwise(_fn_res, [x3, f1], g_d, [(D, F32)], ROW_TB, "res3")
    dx4, lrow = _loss_kernel(x4, tgt, ROW_TB, "loss")
    loss = lax.psum(lrow[0, 0] * (0.5 / D), axes)

    (df1,), (dg_fpost1,) = _rowwise_bwd(_fn_res, [x3, f1], g_d, [dx4], [None, BF16], ROW_TB, "res3_bwd")
    dh3, dwgu1, dwd1 = _ffn_bwd(df1, h3, w_gu_t1, w_down1, gu1, a1, 1)
    (dx2, dy1), (dg_mpost1, dg_fpre1) = _rowwise_bwd(_fn_res_pre, [x2, y1], g_c, [dx4, dh3], [F32, BF16],
                                                     ROW_TB, "res_pre2_bwd")
    dcat1 = _matmul(dy1, w_dn_out, "nt", F32, "dn_out_dx")
    dw_dn_out = _matmul(cat1, dy1, "tn", BF16, "dn_out_dw")
    dp1, dconv, da_log, ddt_bias, dout_norm, dkm1, dvm1 = _dn_mixer_bwd(dcat1, res1, km1, vm1)
    dwkv1, dg_mem1 = _mem_kv_bwd(mem0, gain(mem_norm, 1), w_kv[1], memn1, dkm1, dvm1, 1)
    dh2 = _matmul(dp1, w_dn_in_t, "nn", F32, "dn_in_dx")
    dw_dn_in_t = _matmul(dp1, h2, "tn", BF16, "dn_in_dw")
    dn_in_parts = [jnp.pad(dw_dn_in_t[DN_IN_SHARD * j:DN_IN_SHARD * (j + 1)],
                           ((0, DN_IN_SHARD_PAD - DN_IN_SHARD), (0, 0))) for j in range(N_DEV)]
    xch_b, tok = _exchange_begin(
        [jnp.concatenate(dn_in_parts, axis=0)[None], dw_dn_out[None], dwkv1[None], dwgu1[None], dwd1[None]],
        [1, 1, 1, 1, 1], dh2, "exchange_layer1_start")
    (dx1, df0), (dg_fpost0, dg_mpre1) = _rowwise_bwd(_fn_res_pre, [x1, f0], [g + tok[0:1, 0:1] for g in g_b],
                                                     [dx2, dh2], [F32, BF16], ROW_TB, "res_pre1_bwd")
    dh1, dwgu0, dwd0 = _ffn_bwd(df0, h1, w_gu_t0, w_down0, gu0, a0, 0)
    xch_a, tok = _exchange_begin([dwgu0[None], dwd0[None]], [1, 1], dh1, "exchange_ffn0_start")
    (dx0, dy0), (dg_mpost0, dg_fpre0) = _rowwise_bwd(_fn_res_pre, [x0, y0], [g + tok[0:1, 0:1] for g in g_a],
                                                     [dx1, dh1], [F32, BF16], ROW_TB, "res_pre0_bwd")
    dcat0 = _matmul(dy0, w_att_out, "nt", F32, "att_out_dx")
    dw_att_out = _matmul(cat0, dy0, "tn", BF16, "att_out_dw")
    dp0, drel, dkm0, dvm0 = _attn_mixer_bwd(dcat0, res0, km0, vm0)
    dwkv0, dg_mem0 = _mem_kv_bwd(mem0, gain(mem_norm, 0), w_kv[0], memn0, dkm0, dvm0, 0)
    dh0 = _matmul(dp0, w_att_in_t, "nn", F32, "att_in_dx")
    dw_att_in_t = _matmul(dp0, h0, "tn", BF16, "att_in_dw")
    (grad_x,), (dg_mpre0,) = _rowwise_bwd(_fn_first, [x0], [gain(norm_mix_pre, 0)], [dx0, dh0], [F32],
                                          ROW_TB, "pre0_bwd")

    small_grads = [drel, da_log, ddt_bias, dout_norm, jnp.concatenate([dg_mem0, dg_mem1]),
                   jnp.concatenate([dg_mpre0, dg_mpre1]), jnp.concatenate([dg_mpost0, dg_mpost1]),
                   jnp.concatenate([dg_fpre0, dg_fpre1]), jnp.concatenate([dg_fpost0, dg_fpost1]), dconv]
    r_att_in, r_att_out, r_kv0, r_small = _exchange(
        [dw_att_in_t[None], dw_att_out[None], dwkv0[None], _pack_small(small_grads, SMALL_ROWS)],
        [1, 1, 1, None], "exchange_last")
    r_gu0, r_down0 = _split_end(xch_a, r_small, "exchange_ffn0_wait")
    r_dn_in, r_dn_out, r_kv1, r_gu1, r_down1 = _split_end(xch_b, r_small, "exchange_layer1_wait")

    def rows(a):
        return a.reshape((-1,) + a.shape[-1:])

    def row_sharded(recv, w, m, v, tb, name):
        outs = _adamw_reduce(recv.reshape((N_DEV, -1) + recv.shape[-1:]), rows(w), rows(m), rows(v), tb, name)
        return [o.reshape(w.shape) for o in outs]

    def col_sharded(recv, w, m, v, tb, name):
        g_t = _reduce8(recv.reshape((N_DEV, -1) + recv.shape[-1:]), tb, name + "_sum")
        g = jnp.swapaxes(g_t.reshape(recv.shape[1:])[:, :w.shape[2]], 1, 2)
        outs = _adamw(rows(g), rows(w), rows(m), rows(v), 256, name)
        return [g] + [o.reshape(w.shape) for o in outs]

    def per_layer(fn, recvs, w, m, v, tb, name):
        outs = [fn(r, w[l:l + 1], m[l:l + 1], v[l:l + 1], tb, f"{name}{l}") for l, r in enumerate(recvs)]
        return [jnp.concatenate(pair, axis=0) for pair in zip(*outs)]

    big = [col_sharded(r_att_in, att_w_in, m_att_w_in, v_att_w_in, 320, "adamw_att_in"),
           row_sharded(r_att_out, att_w_out, m_att_w_out, v_att_w_out, 128, "adamw_att_out"),
           col_sharded(r_dn_in, dn_w_in, m_dn_w_in, v_dn_w_in, 432, "adamw_dn_in"),
           row_sharded(r_dn_out, dn_w_out, m_dn_w_out, v_dn_w_out, 128, "adamw_dn_out"),
           per_layer(row_sharded, [r_kv0, r_kv1], mem_w_kv, m_mem_w_kv, v_mem_w_kv, 128, "adamw_mem_kv"),
           per_layer(col_sharded, [r_gu0, r_gu1], ffn_w_gate_up, m_ffn_w_gate_up, v_ffn_w_gate_up, 176,
                     "adamw_ffn_gu"),
           per_layer(row_sharded, [r_down0, r_down1], ffn_w_down, m_ffn_w_down, v_ffn_w_down, 176,
                     "adamw_ffn_down")]
    g_big, d_big, nm_big, nv_big = [[b[i] for b in big] for i in range(4)]

    g_small = _reduce8(r_small, SMALL_ROWS, "reduce_small")
    rep_shapes = [(32, 12), (1, 2, 6), (1, 2, 6), (1, 128), (2, D), (2, D), (2, D), (2, D), (2, D)]
    *g_rep, g_conv_full = _unpack_small(g_small, rep_shapes + [(DN_CONV, 3 * TOK_W)])
    me = _index(_me_xyc())
    g_conv = lax.dynamic_slice(g_conv_full, (0, me * 288), (DN_CONV, 288)).reshape(CONV_SHARD)
    small_shapes = rep_shapes + [CONV_SHARD]
    small_w = [rel_bias, dn_a_log, dn_dt_bias, dn_out_norm, mem_norm, norm_mix_pre, norm_mix_post,
               norm_ffn_pre, norm_ffn_post, dn_conv]
    small_m = [m_rel_bias, m_dn_a_log, m_dn_dt_bias, m_dn_out_norm, m_mem_norm, m_norm_mix_pre, m_norm_mix_post,
               m_norm_ffn_pre, m_norm_ffn_post, m_dn_conv]
    small_v = [v_rel_bias, v_dn_a_log, v_dn_dt_bias, v_dn_out_norm, v_mem_norm, v_norm_mix_pre, v_norm_mix_post,
               v_norm_ffn_pre, v_norm_ffn_post, v_dn_conv]
    g_small_list = g_rep + [g_conv]
    outs_small = _adamw(_pack_small(g_small_list, 24), _pack_small(small_w, 24), _pack_small(small_m, 24),
                        _pack_small(small_v, 24), 24, "adamw_small")
    d_small, nm_small, nv_small = [_unpack_small(o, small_shapes) for o in outs_small]

    def ordered(small, big):
        return [small[0], big[0], big[1], big[2], small[9], small[1], small[2], small[3], big[3], small[4],
                big[4], small[5], small[6], small[7], small[8], big[5], big[6]]

    g_small_out = [g.reshape(s) for g, s in zip(g_small_list, small_shapes)]
    return (loss, grad_x[None], *ordered(g_small_out, g_big), *ordered(d_small, d_big),
            *ordered(nm_small, nm_big), *ordered(nv_small, nv_big))
```

```python
import functools
import math

import numpy as np
import jax
import jax.numpy as jnp
from jax import lax
from jax.experimental import pallas as pl
from jax.experimental.pallas import tpu as pltpu

F32 = jnp.float32
BF16 = jnp.bfloat16
HI = lax.Precision.HIGHEST
MESH = pl.DeviceIdType.MESH

N_DEV = 8
D = 1024
EPS = 1e-6
NEG = -1e30
TOK_W = 768
MEM_W = 256
ATT_HD = 64
DIL_GROUPS = ((128, 1), (512, 4), (2048, 16))
BAND_HALF = 64
REL_BUCKETS = 32
REL_MAX_DIST = 1024
DN_HD = 128
DN_HEADS = 6
DN_CONV = 5
DN_CHUNK = 64
MEM_HEADS = 4
D_FF = 2816
ATT_IN = 2560
DN_IN = 3352
DN_IN_PAD = 3456

ADAM_LR, ADAM_B1, ADAM_B2, ADAM_EPS, ADAM_WD, ADAM_STEP = 0.001, 0.9, 0.999, 1e-08, 0.01, 10

PACK_C = 512
BIG_ROWS = 6480
SMALL_ROWS = 48
VMEM_LIMIT = 48 * 1024 * 1024


def _cparams(sem=None):
    kw = dict(vmem_limit_bytes=VMEM_LIMIT)
    if sem is not None:
        kw["dimension_semantics"] = sem
    return pltpu.CompilerParams(**kw)


def _tile(n, cap):
    if n <= cap:
        return n
    best = None
    for t in range(128, cap + 1, 128):
        if n % t == 0:
            best = t
    assert best is not None, (n, cap)
    return best


def _matmul(a, b, mode, out_dtype, name, tm=1024, tn=1408, tk=None):
    if tk is None:
        tk = 2048 if mode == "tn" else 2816
    if mode == "nn":
        (m, kc), (_, n) = a.shape, b.shape
        dims = (((1,), (0,)), ((), ()))
    elif mode == "nt":
        (m, kc), (n, _) = a.shape, b.shape
        dims = (((1,), (1,)), ((), ()))
    else:
        (kc, m), (_, n) = a.shape, b.shape
        dims = (((0,), (0,)), ((), ()))
    tm = m if m <= tm else _tile(m, tm)
    tn = _tile(n, tn)
    tk = _tile(kc, tk)
    nk = kc // tk

    def body(a_ref, b_ref, o_ref, acc_ref):
        k = pl.program_id(2)
        part = lax.dot_general(a_ref[...], b_ref[...], dims, preferred_element_type=F32)

        @pl.when(k == 0)
        def _():
            acc_ref[...] = part

        @pl.when(k > 0)
        def _():
            acc_ref[...] += part

        @pl.when(k == nk - 1)
        def _():
            o_ref[...] = acc_ref[...].astype(o_ref.dtype)

    if mode == "nn":
        a_spec = pl.BlockSpec((tm, tk), lambda i, j, k: (i, k))
        b_spec = pl.BlockSpec((tk, tn), lambda i, j, k: (k, j))
    elif mode == "nt":
        a_spec = pl.BlockSpec((tm, tk), lambda i, j, k: (i, k))
        b_spec = pl.BlockSpec((tn, tk), lambda i, j, k: (j, k))
    else:
        a_spec = pl.BlockSpec((tk, tm), lambda i, j, k: (k, i))
        b_spec = pl.BlockSpec((tk, tn), lambda i, j, k: (k, j))
    return pl.pallas_call(
        body, name=name, grid=(m // tm, n // tn, nk),
        in_specs=[a_spec, b_spec],
        out_specs=pl.BlockSpec((tm, tn), lambda i, j, k: (i, j)),
        out_shape=jax.ShapeDtypeStruct((m, n), out_dtype),
        scratch_shapes=[pltpu.VMEM((tm, tn), F32)],
        compiler_params=_cparams(("parallel", "parallel", "arbitrary")),
    )(a, b)


def _rowwise(fn, rows, params, outs, tb, name):
    t = rows[0].shape[0]
    nr, npar = len(rows), len(params)

    def body(*refs):
        ins = [r[...].astype(F32) for r in refs[:nr + npar]]
        res = fn(*ins)
        for o_ref, r in zip(refs[nr + npar:], res):
            o_ref[...] = r.astype(o_ref.dtype)

    return pl.pallas_call(
        body, name=name, grid=(t // tb,),
        in_specs=[pl.BlockSpec((tb, r.shape[1]), lambda i: (i, 0)) for r in rows]
        + [pl.BlockSpec(p.shape, lambda i: (0, 0)) for p in params],
        out_specs=[pl.BlockSpec((tb, c), lambda i: (i, 0)) for c, _ in outs],
        out_shape=[jax.ShapeDtypeStruct((t, c), dt) for c, dt in outs],
        compiler_params=_cparams(("parallel",)),
    )(*rows, *params)


def _rowwise_bwd(fn, rows, params, cots, row_grad, tb, name):
    t = rows[0].shape[0]
    nr, npar, nc = len(rows), len(params), len(cots)
    want = [i for i, g in enumerate(row_grad) if g is not None]

    def body(*refs):
        ins = [r[...].astype(F32) for r in refs[:nr + npar]]
        cts = tuple(r[...].astype(F32) for r in refs[nr + npar:nr + npar + nc])
        outs = refs[nr + npar + nc:]
        _, vjp = jax.vjp(fn, *ins)
        grads = vjp(cts)
        for o_ref, i in zip(outs[:len(want)], want):
            o_ref[...] = grads[i].astype(o_ref.dtype)
        first = pl.program_id(0) == 0
        for o_ref, g in zip(outs[len(want):], grads[nr:]):
            @pl.when(first)
            def _(o_ref=o_ref, g=g):
                o_ref[...] = g

            @pl.when(jnp.logical_not(first))
            def _(o_ref=o_ref, g=g):
                o_ref[...] += g

    res = pl.pallas_call(
        body, name=name, grid=(t // tb,),
        in_specs=[pl.BlockSpec((tb, r.shape[1]), lambda i: (i, 0)) for r in rows]
        + [pl.BlockSpec(p.shape, lambda i: (0, 0)) for p in params]
        + [pl.BlockSpec((tb, c.shape[1]), lambda i: (i, 0)) for c in cots],
        out_specs=[pl.BlockSpec((tb, rows[i].shape[1]), lambda i_: (i_, 0)) for i in want]
        + [pl.BlockSpec(p.shape, lambda i: (0, 0)) for p in params],
        out_shape=[jax.ShapeDtypeStruct(rows[i].shape, row_grad[i]) for i in want]
        + [jax.ShapeDtypeStruct(p.shape, F32) for p in params],
        compiler_params=_cparams(("arbitrary",)),
    )(*rows, *params, *cots)
    return list(res[:len(want)]), list(res[len(want):])


def _rms(x, g):
    return x * lax.rsqrt(jnp.mean(x * x, axis=-1, keepdims=True) + EPS) * g


def _fn_pre(x, g):
    return (_rms(x, g),)


def _fn_res_pre(x, y, g_post, g_pre):
    x1 = x + _rms(y, g_post)
    return x1, _rms(x1, g_pre)


def _fn_res(x, y, g_post):
    return (x + _rms(y, g_post),)


def _sigmoid(x):
    return 1.0 / (1.0 + jnp.exp(-x))


def _silu(x):
    return x * _sigmoid(x)


def _fn_swiglu(gu):
    return (_silu(gu[:, :D_FF]) * gu[:, D_FF:],)


def _fn_combine(o, lse):
    ls = [lse[:, 256 * g:256 * (g + 1)] for g in range(3)]
    mx = lax.stop_gradient(jnp.maximum(jnp.maximum(ls[0], ls[1]), ls[2]))
    es = [jnp.exp(l - mx) for l in ls]
    inv = 1.0 / (es[0] + es[1] + es[2])
    return (jnp.concatenate([o[:, 256 * g:256 * (g + 1)] * (es[g] * inv) for g in range(3)], axis=1),)


def _fn_outnorm(o_f, o_r, z, gain):
    res = []
    for h in range(DN_HEADS):
        sl = slice(DN_HD * h, DN_HD * (h + 1))
        o = o_f[:, sl] + o_r[:, sl]
        res.append(o * lax.rsqrt(jnp.mean(o * o, axis=-1, keepdims=True) + EPS) * gain * _silu(z[:, sl]))
    return (jnp.concatenate(res, axis=1),)


def _loss_kernel(x, tgt, tb, name):
    t, d = x.shape

    def body(x_ref, t_ref, dx_ref, l_ref, acc_ref):
        i = pl.program_id(0)
        e = x_ref[...] - t_ref[...]
        dx_ref[...] = e * (1.0 / d)
        part = jnp.sum(e * e, axis=0, keepdims=True)

        @pl.when(i == 0)
        def _():
            acc_ref[...] = part

        @pl.when(i > 0)
        def _():
            acc_ref[...] += part

        @pl.when(i == t // tb - 1)
        def _():
            l_ref[...] = jnp.broadcast_to(jnp.sum(acc_ref[...], axis=-1, keepdims=True), (1, 128))

    return pl.pallas_call(
        body, name=name, grid=(t // tb,),
        in_specs=[pl.BlockSpec((tb, d), lambda i: (i, 0))] * 2,
        out_specs=[pl.BlockSpec((tb, d), lambda i: (i, 0)), pl.BlockSpec((1, 128), lambda i: (0, 0))],
        out_shape=[jax.ShapeDtypeStruct((t, d), F32), jax.ShapeDtypeStruct((1, 128), F32)],
        scratch_shapes=[pltpu.VMEM((1, d), F32)],
        compiler_params=_cparams(("arbitrary",)),
    )(x, tgt)


def _band_fn(l_sub, bq, i, q, kw, vw, bm):
    w = bq + 2 * BAND_HALF
    s = lax.dot_general((q * (ATT_HD ** -0.5)).astype(BF16), kw.astype(BF16), (((1,), (1,)), ((), ())),
                        preferred_element_type=F32) + bm
    kpos = i * bq - BAND_HALF + lax.broadcasted_iota(jnp.int32, (bq, w), 1)
    s = jnp.where((kpos >= 0) & (kpos < l_sub), s, NEG)
    m = lax.stop_gradient(jnp.max(s, axis=-1, keepdims=True))
    p = jnp.exp(s - m)
    den = jnp.sum(p, axis=-1, keepdims=True)
    o = jnp.dot(p.astype(BF16), vw.astype(BF16), preferred_element_type=F32) / den
    return o, jnp.broadcast_to(m + jnp.log(den), o.shape)


def _band_specs(dil, l_sub, bq):
    w = bq + 2 * BAND_HALF
    qs = pl.BlockSpec((None, bq, ATT_HD), lambda h, r, i: (h * dil + r, i, 0))
    ks = pl.BlockSpec((None, l_sub + 2 * BAND_HALF, ATT_HD), lambda h, r, i: (h * dil + r, 0, 0))
    bs = pl.BlockSpec((None, bq, w), lambda h, r, i: (h, 0, 0))
    return qs, ks, bs


def _band_fwd(q, k, v, bm, dil, l_sub, bq, name):
    w = bq + 2 * BAND_HALF
    qs, ks, bs = _band_specs(dil, l_sub, bq)

    def body(q_ref, k_ref, v_ref, bm_ref, o_ref, l_ref):
        i = pl.program_id(2)
        st = pl.multiple_of(i * bq, bq)
        o, lse = _band_fn(l_sub, bq, i, q_ref[...].astype(F32), k_ref[pl.ds(st, w), :].astype(F32),
                          v_ref[pl.ds(st, w), :].astype(F32), bm_ref[...])
        o_ref[...] = o
        l_ref[...] = lse

    return pl.pallas_call(
        body, name=name, grid=(4, dil, l_sub // bq),
        in_specs=[qs, ks, ks, bs], out_specs=[qs, qs],
        out_shape=[jax.ShapeDtypeStruct(q.shape, F32)] * 2,
        compiler_params=_cparams(("parallel", "parallel", "arbitrary")),
    )(q, k, v, bm)


def _band_bwd(q, k, v, bm, do, dlse, dil, l_sub, bq, name):
    w = bq + 2 * BAND_HALF
    qs, ks, bs = _band_specs(dil, l_sub, bq)

    def body(q_ref, k_ref, v_ref, bm_ref, do_ref, dl_ref, dq_ref, dk_ref, dv_ref, dbm_ref):
        r, i = pl.program_id(1), pl.program_id(2)
        st = pl.multiple_of(i * bq, bq)
        _, vjp = jax.vjp(functools.partial(_band_fn, l_sub, bq, i),
                         q_ref[...].astype(F32), k_ref[pl.ds(st, w), :].astype(F32),
                         v_ref[pl.ds(st, w), :].astype(F32), bm_ref[...])
        dq, dkw, dvw, dbm = vjp((do_ref[...], dl_ref[...]))
        dq_ref[...] = dq

        @pl.when(i == 0)
        def _():
            dk_ref[...] = jnp.zeros_like(dk_ref)
            dv_ref[...] = jnp.zeros_like(dv_ref)

        dk_ref[pl.ds(st, w), :] += dkw
        dv_ref[pl.ds(st, w), :] += dvw

        @pl.when((i == 0) & (r == 0))
        def _():
            dbm_ref[...] = dbm

        @pl.when((i > 0) | (r > 0))
        def _():
            dbm_ref[...] += dbm

    return pl.pallas_call(
        body, name=name, grid=(4, dil, l_sub // bq),
        in_specs=[qs, ks, ks, bs, qs, qs], out_specs=[qs, ks, ks, bs],
        out_shape=[jax.ShapeDtypeStruct(q.shape, F32), jax.ShapeDtypeStruct(k.shape, F32),
                   jax.ShapeDtypeStruct(k.shape, F32), jax.ShapeDtypeStruct(bm.shape, F32)],
        compiler_params=_cparams(("parallel", "arbitrary", "arbitrary")),
    )(q, k, v, bm, do, dlse)


def _t5_bucket(rel):
    half = REL_BUCKETS // 2
    max_exact = half // 2
    n = np.abs(rel)
    large = max_exact + (np.log(np.maximum(n, 1) / max_exact) / math.log(REL_MAX_DIST / max_exact)
                         * (half - max_exact)).astype(np.int64)
    large = np.minimum(large, half - 1)
    return ((rel > 0) * half + np.where(n < max_exact, n, large)).astype(np.int32)


def _bucket_onehot(dil):
    idx = _t5_bucket(np.arange(-BAND_HALF, BAND_HALF + 1) * dil)
    oh = np.zeros((2 * BAND_HALF + 1, REL_BUCKETS), np.float32)
    oh[np.arange(2 * BAND_HALF + 1), idx] = 1.0
    return oh


def _band_bias(rel_bias, gi, dil, bq):
    w = bq + 2 * BAND_HALF
    nb = 2 * BAND_HALF + 1
    bias = jnp.dot(jnp.asarray(_bucket_onehot(dil)), rel_bias[:, 4 * gi:4 * gi + 4], precision=HI)
    row = jnp.concatenate([bias.T, jnp.full((4, w + 1 - nb), NEG, F32)], axis=1)
    flat = jnp.tile(row, (1, bq))[:, :bq * w]
    return flat.reshape(4, bq, w)


def _relbias_grad(dbms, name):
    nb = 2 * BAND_HALF + 1
    bq = max(d.shape[1] for d in dbms)
    skew = []
    for dbm in dbms:
        bqg, w = dbm.shape[1], dbm.shape[2]
        flat = jnp.pad(dbm.reshape(4, bqg * w), ((0, 0), (0, bqg)))
        skew.append(jnp.pad(flat.reshape(4, bqg, w + 1)[:, :, :nb], ((0, 0), (0, bq - bqg), (0, 256 - nb))))
    sk = jnp.concatenate(skew, axis=0)
    oh = np.zeros((3, 256, 128), np.float32)
    for gi, (_, dil) in enumerate(DIL_GROUPS):
        oh[gi, :2 * BAND_HALF + 1, :REL_BUCKETS] = _bucket_onehot(dil)

    def body(s_ref, oh_ref, o_ref):
        col = jnp.sum(s_ref[...], axis=0, keepdims=True)
        o_ref[...] = jnp.dot(jnp.broadcast_to(col, (8, 256)), oh_ref[...], precision=HI, preferred_element_type=F32)

    out = pl.pallas_call(
        body, name=name, grid=(12,),
        in_specs=[pl.BlockSpec((None, bq, 256), lambda n: (n, 0, 0)),
                  pl.BlockSpec((None, 256, 128), lambda n: (n // 4, 0, 0))],
        out_specs=pl.BlockSpec((None, 8, 128), lambda n: (n, 0, 0)),
        out_shape=jax.ShapeDtypeStruct((12, 8, 128), F32),
        compiler_params=_cparams(("parallel",)),
    )(sk, jnp.asarray(oh))
    return out[:, 0, :REL_BUCKETS].T


def _mem_fn(q, k, v):
    s = lax.dot_general((q * (ATT_HD ** -0.5)).astype(BF16), k.astype(BF16), (((1,), (1,)), ((), ())),
                        preferred_element_type=F32)
    m = lax.stop_gradient(jnp.max(s, axis=-1, keepdims=True))
    p = jnp.exp(s - m)
    p = p / jnp.sum(p, axis=-1, keepdims=True)
    return jnp.dot(p.astype(BF16), v.astype(BF16), preferred_element_type=F32)


def _mem_specs(tb, ml):
    qs = pl.BlockSpec((None, tb, ATT_HD), lambda h, i: (h, i, 0))
    ks = pl.BlockSpec((None, ml, ATT_HD), lambda h, i: (h, 0, 0))
    return qs, ks


def _mem_fwd(q, k, v, tb, name):
    qs, ks = _mem_specs(tb, k.shape[1])

    def body(q_ref, k_ref, v_ref, o_ref):
        o_ref[...] = _mem_fn(q_ref[...].astype(F32), k_ref[...], v_ref[...])

    return pl.pallas_call(
        body, name=name, grid=(MEM_HEADS, q.shape[1] // tb),
        in_specs=[qs, ks, ks], out_specs=qs, out_shape=jax.ShapeDtypeStruct(q.shape, F32),
        compiler_params=_cparams(("parallel", "parallel")),
    )(q, k, v)


def _mem_bwd(q, k, v, do, tb, name):
    qs, ks = _mem_specs(tb, k.shape[1])

    def body(q_ref, k_ref, v_ref, do_ref, dq_ref, dk_ref, dv_ref):
        i = pl.program_id(1)
        _, vjp = jax.vjp(_mem_fn, q_ref[...].astype(F32), k_ref[...], v_ref[...])
        dq, dk, dv = vjp(do_ref[...])
        dq_ref[...] = dq

        @pl.when(i == 0)
        def _():
            dk_ref[...] = dk
            dv_ref[...] = dv

        @pl.when(i > 0)
        def _():
            dk_ref[...] += dk
            dv_ref[...] += dv

    return pl.pallas_call(
        body, name=name, grid=(MEM_HEADS, q.shape[1] // tb),
        in_specs=[qs, ks, ks, qs], out_specs=[qs, ks, ks],
        out_shape=[jax.ShapeDtypeStruct(q.shape, F32), jax.ShapeDtypeStruct(k.shape, F32),
                   jax.ShapeDtypeStruct(k.shape, F32)],
        compiler_params=_cparams(("parallel", "arbitrary")),
    )(q, k, v, do)


CONV_PAD = 8


def _conv_post(kind, acc):
    s = _silu(acc)
    if kind == 2:
        return s
    scale = DN_HD ** -0.5 if kind == 0 else 1.0
    return s * lax.rsqrt(jnp.sum(s * s, axis=-1, keepdims=True) + EPS) * scale


def _conv_acc(xp_ref, w, r0, rt):
    acc = None
    for i in range(DN_CONV):
        term = w[i:i + 1, :] * xp_ref[pl.ds(CONV_PAD + r0 + i - DN_CONV // 2, rt), :]
        acc = term if acc is None else acc + term
    return acc


def _conv_fwd(xp, w8, kind, rt, name):
    t = xp.shape[0] - 2 * CONV_PAD

    def body(xp_ref, w_ref, o_ref):
        w = w_ref[...]
        for r in range(t // rt):
            o_ref[pl.ds(r * rt, rt), :] = _conv_post(kind, _conv_acc(xp_ref, w, r * rt, rt))

    return pl.pallas_call(
        body, name=name, grid=(DN_HEADS,),
        in_specs=[pl.BlockSpec((t + 2 * CONV_PAD, DN_HD), lambda j: (0, 6 * kind + j)),
                  pl.BlockSpec((8, DN_HD), lambda j: (0, 6 * kind + j))],
        out_specs=pl.BlockSpec((t, DN_HD), lambda j: (0, j)),
        out_shape=jax.ShapeDtypeStruct((t, TOK_W), F32),
        compiler_params=_cparams(("parallel",)),
    )(xp, w8)


def _conv_bwd(xp, w8, d_f, d_r, kind, rt, name):
    t = xp.shape[0] - 2 * CONV_PAD

    def body(xp_ref, w_ref, df_ref, dr_ref, dx_ref, dw_ref, dpad_ref):
        w = w_ref[...]
        zero = jnp.zeros((CONV_PAD, DN_HD), F32)
        dpad_ref[pl.ds(0, CONV_PAD), :] = zero
        dpad_ref[pl.ds(CONV_PAD + t, CONV_PAD), :] = zero
        dw = [jnp.zeros((1, DN_HD), F32) for _ in range(DN_CONV)]
        for r in range(t // rt):
            rows = pl.ds(r * rt, rt)
            acc = _conv_acc(xp_ref, w, r * rt, rt)
            _, vjp = jax.vjp(functools.partial(_conv_post, kind), acc)
            (dacc,) = vjp(df_ref[rows, :] + dr_ref[rows, :])
            dpad_ref[pl.ds(CONV_PAD + r * rt, rt), :] = dacc
            for i in range(DN_CONV):
                xs = xp_ref[pl.ds(CONV_PAD + r * rt + i - DN_CONV // 2, rt), :]
                dw[i] = dw[i] + jnp.sum(dacc * xs, axis=0, keepdims=True)
        dw_ref[...] = jnp.concatenate(dw + [jnp.zeros((8 - DN_CONV, DN_HD), F32)], axis=0)
        for r in range(t // rt):
            acc = None
            for i in range(DN_CONV):
                term = w[i:i + 1, :] * dpad_ref[pl.ds(CONV_PAD + r * rt - i + DN_CONV // 2, rt), :]
                acc = term if acc is None else acc + term
            dx_ref[pl.ds(r * rt, rt), :] = acc

    return pl.pallas_call(
        body, name=name, grid=(DN_HEADS,),
        in_specs=[pl.BlockSpec((t + 2 * CONV_PAD, DN_HD), lambda j: (0, 6 * kind + j)),
                  pl.BlockSpec((8, DN_HD), lambda j: (0, 6 * kind + j)),
                  pl.BlockSpec((t, DN_HD), lambda j: (0, j)),
                  pl.BlockSpec((t, DN_HD), lambda j: (0, j))],
        out_specs=[pl.BlockSpec((t, DN_HD), lambda j: (0, j)), pl.BlockSpec((8, DN_HD), lambda j: (0, j))],
        out_shape=[jax.ShapeDtypeStruct((t, TOK_W), F32), jax.ShapeDtypeStruct((8, TOK_W), F32)],
        scratch_shapes=[pltpu.VMEM((t + 2 * CONV_PAD, DN_HD), F32)],
        compiler_params=_cparams(("parallel",)),
    )(xp, w8, d_f, d_r)


def _softplus(x):
    e = jnp.exp(-jnp.abs(x))
    return jnp.maximum(x, 0.0) + jnp.where(e < 1e-4, e - 0.5 * e * e, jnp.log(1.0 + e))


_NN = (((2,), (1,)), ((0,), (0,)))
_NT = (((2,), (2,)), ((0,), (0,)))
_TN = (((1,), (1,)), ((0,), (0,)))


def _dot(a, b, dims=_NN):
    return lax.dot_general(a.astype(BF16), b.astype(BF16), dims, preferred_element_type=F32)


def _hi_lo(x):
    hi = x.astype(BF16)
    return hi, (x - hi.astype(F32)).astype(BF16)


def _mask_dot(mask_bf16, x, dims):
    x1 = x.astype(BF16)
    r = x - x1.astype(F32)
    x2, x3 = _hi_lo(r)
    d = functools.partial(lax.dot_general, dimension_numbers=dims, preferred_element_type=F32)
    return d(mask_bf16, x1) + d(mask_bf16, x2) + d(mask_bf16, x3)


@jax.custom_vjp
def _dot_mask(mask_bf16, x):
    return _mask_dot(mask_bf16, x, _NN)


def _dot_mask_fwd(mask_bf16, x):
    return _mask_dot(mask_bf16, x, _NN), mask_bf16


def _dot_mask_bwd(mask_bf16, ct):
    return jnp.zeros_like(mask_bf16), _mask_dot(mask_bf16, ct, _TN)


_dot_mask.defvjp(_dot_mask_fwd, _dot_mask_bwd)


def _dot3_raw(a, b, dims):
    a1, a2 = _hi_lo(a)
    b1, b2 = _hi_lo(b)
    d = functools.partial(lax.dot_general, dimension_numbers=dims, preferred_element_type=F32)
    return d(a1, b1) + d(a1, b2) + d(a2, b1)


@jax.custom_vjp
def _dot3(a, b):
    return _dot3_raw(a, b, _NN)


def _dot3_fwd(a, b):
    return _dot3_raw(a, b, _NN), (a, b)


def _dot3_bwd(res, ct):
    a, b = res
    return _dot3_raw(ct, b, _NT), _dot3_raw(a, ct, _TN)


_dot3.defvjp(_dot3_fwd, _dot3_bwd)


def _dn_chunk(q, k, v, al, be, alc, a_row, dt_row, a_rowc, dt_rowc, s):
    n, c = q.shape[0], DN_CHUNK
    rev = lax.broadcasted_iota(jnp.int32, (n, c, c), 0) >= n // 2
    row = lax.broadcasted_iota(jnp.int32, (n, c, c), 1)
    col = lax.broadcasted_iota(jnp.int32, (n, c, c), 2)
    ahead = jnp.where(rev, col - row, row - col)
    incl = ahead >= 0
    strict = ahead > 0
    incl_b = incl.astype(BF16)

    g = -jnp.exp(a_row) * _softplus(al + dt_row)
    beta = _sigmoid(be)
    g_c = -jnp.exp(a_rowc) * _softplus(alc + dt_rowc)
    gc = _dot_mask(incl_b, g)
    gcc = _dot_mask(incl_b, g_c)
    decay = jnp.exp(jnp.where(incl, gcc - jnp.swapaxes(gcc, 1, 2), NEG))
    kb = k * beta
    lmat = jnp.where(strict, _dot(kb, k, _NT) * decay, 0.0)
    rhs = jnp.concatenate([v * beta, kb * jnp.exp(gc)], axis=2)
    prow = jnp.where(rev, c - 1 - row, row)
    pcol = jnp.where(rev, c - 1 - col, col)
    del prow, pcol
    xp = -lmat
    sol = rhs + _dot3(xp, rhs)
    for _ in range(5):
        xp = _dot3(xp, xp)
        sol = sol + _dot3(xp, sol)
    u, w = sol[:, :, :DN_HD], sol[:, :, DN_HD:]
    intra = jnp.where(incl, _dot(q, k, _NT) * decay, 0.0)
    v_new = u - _dot(w, s)
    out = _dot(q * jnp.exp(gc), s) + _dot(intra, v_new)
    g_last = jnp.sum(g, axis=1, keepdims=True)
    s_new = s * jnp.exp(g_last) + _dot(k * jnp.exp(g_last - gc), v_new, _TN)
    return out, s_new


DN_HG = 6


def _dn_load(f_refs, r_refs, alf, bef, alr, ber, a_ref, dt_ref):
    c, hg = DN_CHUNK, DN_HG
    sls = [slice(DN_HD * h, DN_HD * (h + 1)) for h in range(hg)]
    toks = [jnp.stack([f[:, sl] for sl in sls] + [r[:, sl] for sl in sls]) for f, r in zip(f_refs, r_refs)]
    al = jnp.concatenate([alf[...], alr[...]], axis=0)
    be = jnp.concatenate([bef[...], ber[...]], axis=0)
    alc = jnp.concatenate([alf[:, :, 0:c], alr[:, :, 0:c]], axis=0)
    a = jnp.concatenate([a_ref[0], a_ref[1]], axis=0)
    dt = jnp.concatenate([dt_ref[0], dt_ref[1]], axis=0)
    ac = jnp.concatenate([a_ref[0, :, :, 0:c], a_ref[1, :, :, 0:c]], axis=0)
    dtc = jnp.concatenate([dt_ref[0, :, :, 0:c], dt_ref[1, :, :, 0:c]], axis=0)
    return toks, (al, be, alc, a, dt, ac, dtc)


def _dn_views(nc, bwd):
    c, hg = DN_CHUNK, DN_HG
    if bwd:
        f_blk = lambda s: nc - 1 - s
        r_blk = lambda s: s
        st_blk = lambda s: nc - 1 - s
    else:
        f_blk = lambda s: s
        r_blk = lambda s: nc - 1 - s
        st_blk = lambda s: s
    tok_f = pl.BlockSpec((c, hg * DN_HD), lambda g, s: (f_blk(s), g))
    tok_r = pl.BlockSpec((c, hg * DN_HD), lambda g, s: (r_blk(s), g))
    gate_f = pl.BlockSpec((None, hg, c, DN_HD), lambda g, s: (0, g, f_blk(s), 0))
    gate_r = pl.BlockSpec((None, hg, c, DN_HD), lambda g, s: (1, g, r_blk(s), 0))
    par = pl.BlockSpec((2, hg, 1, DN_HD), lambda g, s: (0, g, 0, 0))
    state = pl.BlockSpec((2, hg, None, DN_HD, DN_HD), lambda g, s: (0, g, st_blk(s), 0, 0))
    return tok_f, tok_r, gate_f, gate_r, par, state


def _dn_fwd(q, k, v, al, be, a_rows, dt_rows, name):
    t = q.shape[0]
    c, hg = DN_CHUNK, DN_HG
    nc = t // c
    tok_f, tok_r, gate_f, gate_r, par, state = _dn_views(nc, False)

    def body(qf, kf, vf, qr, kr, vr, alf, bef, alr, ber, a_ref, dt_ref, of_ref, or_ref, st_ref, s_ref):
        @pl.when(pl.program_id(1) == 0)
        def _():
            s_ref[...] = jnp.zeros_like(s_ref)

        (q_, k_, v_), gates = _dn_load((qf, kf, vf), (qr, kr, vr), alf, bef, alr, ber, a_ref, dt_ref)
        s = s_ref[...]
        st_ref[0] = s[:hg]
        st_ref[1] = s[hg:]
        out, s_new = _dn_chunk(q_, k_, v_, *gates, s)
        for h in range(hg):
            sl = slice(DN_HD * h, DN_HD * (h + 1))
            of_ref[:, sl] = out[h]
            or_ref[:, sl] = out[hg + h]
        s_ref[...] = s_new

    return pl.pallas_call(
        body, name=name, grid=(DN_HEADS // hg, nc),
        in_specs=[tok_f] * 3 + [tok_r] * 3 + [gate_f, gate_f, gate_r, gate_r, par, par],
        out_specs=[tok_f, tok_r, state],
        out_shape=[jax.ShapeDtypeStruct((t, TOK_W), F32)] * 2
        + [jax.ShapeDtypeStruct((2, DN_HEADS, nc, DN_HD, DN_HD), F32)],
        scratch_shapes=[pltpu.VMEM((2 * hg, DN_HD, DN_HD), F32)],
        compiler_params=_cparams(("parallel", "arbitrary")),
    )(q, k, v, q, k, v, al, be, al, be, a_rows, dt_rows)


def _dn_bwd(q, k, v, al, be, a_rows, dt_rows, states, do, name):
    t = q.shape[0]
    c, hg = DN_CHUNK, DN_HG
    assert hg == DN_HEADS
    nc = t // c
    tok_f, tok_r, gate_f, gate_r, par, state = _dn_views(nc, True)
    gout_f = pl.BlockSpec((c, DN_HD), lambda g, s: (nc - 1 - s, 0))
    gout_r = pl.BlockSpec((c, DN_HD), lambda g, s: (s, 0))

    def body(qf, kf, vf, qr, kr, vr, alf, bef, alr, ber, a_ref, dt_ref, st_ref, dof, dor,
             dqf, dkf, dvf, dqr, dkr, dvr, dgf, dgr, da_ref, ddt_ref, ds_ref):
        first = pl.program_id(1) == 0

        @pl.when(first)
        def _():
            ds_ref[...] = jnp.zeros_like(ds_ref)
            da_ref[...] = jnp.zeros_like(da_ref)
            ddt_ref[...] = jnp.zeros_like(ddt_ref)

        def lanes(x):
            return jnp.sum(x, axis=-1, keepdims=True)

        (q_, k_, v_, do_), gates = _dn_load((qf, kf, vf, dof), (qr, kr, vr, dor), alf, bef, alr, ber, a_ref, dt_ref)
        s = jnp.concatenate([st_ref[0], st_ref[1]], axis=0)
        _, vjp = jax.vjp(_dn_chunk, q_, k_, v_, *gates, s)
        dq, dk, dv, dal, dbe, dalc, da, ddt, dac, ddtc, ds = vjp((do_, ds_ref[...]))
        for h in range(hg):
            sl = slice(DN_HD * h, DN_HD * (h + 1))
            dqf[:, sl], dkf[:, sl], dvf[:, sl] = dq[h], dk[h], dv[h]
            dqr[:, sl], dkr[:, sl], dvr[:, sl] = dq[hg + h], dk[hg + h], dv[hg + h]
        dal, dbe = lanes(dal) + lanes(dalc), lanes(dbe)
        lane = lax.broadcasted_iota(jnp.int32, (c, DN_HD), 1)
        for d, dg_ref in enumerate((dgf, dgr)):
            dg = jnp.zeros((c, DN_HD), F32)
            for h in range(hg):
                dg = jnp.where(lane == h, dal[d * hg + h], jnp.where(lane == hg + h, dbe[d * hg + h], dg))
            dg_ref[...] = dg
        da = jnp.broadcast_to(lanes(da) + lanes(dac), da.shape)
        ddt = jnp.broadcast_to(lanes(ddt) + lanes(ddtc), ddt.shape)
        da_ref[0] += da[:hg]
        da_ref[1] += da[hg:]
        ddt_ref[0] += ddt[:hg]
        ddt_ref[1] += ddt[hg:]
        ds_ref[...] = ds

    tok = jax.ShapeDtypeStruct((t, TOK_W), F32)
    gate = jax.ShapeDtypeStruct((t, DN_HD), F32)
    parsh = jax.ShapeDtypeStruct((2, DN_HEADS, 1, DN_HD), F32)
    res = pl.pallas_call(
        body, name=name, grid=(DN_HEADS // hg, nc),
        in_specs=[tok_f] * 3 + [tok_r] * 3 + [gate_f, gate_f, gate_r, gate_r, par, par, state, tok_f, tok_r],
        out_specs=[tok_f] * 3 + [tok_r] * 3 + [gout_f, gout_r, par, par],
        out_shape=[tok] * 6 + [gate] * 2 + [parsh] * 2,
        scratch_shapes=[pltpu.VMEM((2 * hg, DN_HD, DN_HD), F32)],
        compiler_params=_cparams(("parallel", "arbitrary")),
    )(q, k, v, q, k, v, al, be, al, be, a_rows, dt_rows, states, do, do)
    dqf, dkf, dvf, dqr, dkr, dvr, dgf, dgr, da, ddt = res
    dgate = jnp.concatenate([dgf[:, :2 * DN_HEADS], dgr[:, :2 * DN_HEADS]], axis=1)
    return (dqf, dkf, dvf), (dqr, dkr, dvr), dgate, da, ddt


BAND_BQ = 256
ROW_TB = 256
MEM_TB = 512
CONV_RT = 512


def _to_sub(x, dil):
    l = x.shape[0] // dil
    return x.reshape(l, dil, 4, ATT_HD).transpose(2, 1, 0, 3).reshape(4 * dil, l, ATT_HD)


def _from_sub(x, dil):
    l = x.shape[1]
    return x.reshape(4, dil, l, ATT_HD).transpose(2, 1, 0, 3).reshape(l * dil, 4 * ATT_HD)


def _heads_major(x):
    return x.reshape(x.shape[0], MEM_HEADS, ATT_HD).transpose(1, 0, 2)


def _heads_minor(x):
    return x.transpose(1, 0, 2).reshape(x.shape[1], MEM_HEADS * ATT_HD)


def _mem_kv_fwd(mem, gain, w_kv, li):
    (memn,) = _rowwise(_fn_pre, [mem], [gain], [(D, BF16)], mem.shape[0], f"memnorm_fwd{li}")
    kv = _matmul(memn, w_kv, "nn", F32, f"memkv_fwd{li}")
    return _heads_major(kv[:, :MEM_W]), _heads_major(kv[:, MEM_W:]), memn


def _mem_kv_bwd(mem, gain, w_kv, memn, dkm, dvm, li):
    dkv = jnp.concatenate([_heads_minor(dkm), _heads_minor(dvm)], axis=1).astype(BF16)
    dw = _matmul(memn, dkv, "tn", BF16, f"memkv_dw{li}")
    dmemn = _matmul(dkv, w_kv, "nt", F32, f"memkv_dx{li}")
    _, (dgain,) = _rowwise_bwd(_fn_pre, [mem], [gain], [dmemn], [None], mem.shape[0], f"memnorm_bwd{li}")
    return dw, dgain


def _attn_mixer_fwd(p, rel_bias, kv_fn):
    t = p.shape[0]
    saved, outs, lses = [], [], []
    for gi, (_, dil) in enumerate(DIL_GROUPS):
        l_sub = t // dil
        bq = min(BAND_BQ, l_sub)
        q = _to_sub(p[:, 256 * gi:256 * (gi + 1)], dil)
        pad = ((0, 0), (BAND_HALF, BAND_HALF), (0, 0))
        k = jnp.pad(_to_sub(p[:, TOK_W + 256 * gi:TOK_W + 256 * (gi + 1)], dil), pad)
        v = jnp.pad(_to_sub(p[:, 2 * TOK_W + 256 * gi:2 * TOK_W + 256 * (gi + 1)], dil), pad)
        bm = _band_bias(rel_bias, gi, dil, bq)
        o, lse = _band_fwd(q, k, v, bm, dil, l_sub, bq, f"band_fwd{gi}")
        outs.append(_from_sub(o, dil))
        lses.append(_from_sub(lse, dil))
        saved.append((q, k, v, bm))
    o_all = jnp.concatenate(outs, axis=1)
    lse_all = jnp.concatenate(lses, axis=1)
    (mixed,) = _rowwise(_fn_combine, [o_all, lse_all], [], [(TOK_W, BF16)], ROW_TB, "combine_fwd")
    qm = _heads_major(p[:, 3 * TOK_W:])
    km, vm, memn = kv_fn(mixed)
    memo = _mem_fwd(qm, km, vm, min(MEM_TB, t), "mem_fwd0")
    cat = jnp.concatenate([mixed, _heads_minor(memo).astype(BF16)], axis=1)
    return cat, (saved, o_all, lse_all, qm), (km, vm, memn)


def _attn_mixer_bwd(dcat, res, km, vm):
    saved, o_all, lse_all, qm = res
    t = dcat.shape[0]
    (do_all, dlse_all), _ = _rowwise_bwd(_fn_combine, [o_all, lse_all], [], [dcat[:, :TOK_W]], [F32, F32],
                                         ROW_TB, "combine_bwd")
    dqs, dks, dvs, dbms = [], [], [], []
    for gi, (_, dil) in enumerate(DIL_GROUPS):
        l_sub = t // dil
        bq = min(BAND_BQ, l_sub)
        q, k, v, bm = saved[gi]
        do = _to_sub(do_all[:, 256 * gi:256 * (gi + 1)], dil)
        dl = _to_sub(dlse_all[:, 256 * gi:256 * (gi + 1)], dil)
        dq, dk, dv, dbm = _band_bwd(q, k, v, bm, do, dl, dil, l_sub, bq, f"band_bwd{gi}")
        dqs.append(_from_sub(dq, dil))
        dks.append(_from_sub(dk[:, BAND_HALF:-BAND_HALF], dil))
        dvs.append(_from_sub(dv[:, BAND_HALF:-BAND_HALF], dil))
        dbms.append(dbm)
    dqm, dkm, dvm = _mem_bwd(qm, km, vm, _heads_major(dcat[:, TOK_W:]), min(MEM_TB, t), "mem_bwd0")
    dp = jnp.concatenate(dqs + dks + dvs + [_heads_minor(dqm)], axis=1).astype(BF16)
    return dp, _relbias_grad(dbms, "relbias_grad"), dkm, dvm


def _dn_mixer_fwd(p, conv_w, a_log, dt_bias, out_norm, km, vm):
    t = p.shape[0]
    rt = min(CONV_RT, t)
    xp = jnp.pad(p[:, :3 * TOK_W], ((CONV_PAD, CONV_PAD), (0, 0)))
    w8 = jnp.pad(conv_w, ((0, 8 - DN_CONV), (0, 0)))
    q = _conv_fwd(xp, w8, 0, rt, "conv_fwd_q")
    k = _conv_fwd(xp, w8, 1, rt, "conv_fwd_k")
    v = _conv_fwd(xp, w8, 2, rt, "conv_fwd_v")
    gate = p[:, 4 * TOK_W:4 * TOK_W + 4 * DN_HEADS].reshape(t, 2, 2, DN_HEADS)
    bshape = (2, DN_HEADS, t, DN_HD)
    al = jnp.broadcast_to(gate[:, :, 0, :].transpose(1, 2, 0)[..., None], bshape)
    be = jnp.broadcast_to(gate[:, :, 1, :].transpose(1, 2, 0)[..., None], bshape)
    a_rows = jnp.broadcast_to(a_log[:, :, None, None], (2, DN_HEADS, 1, DN_HD))
    dt_rows = jnp.broadcast_to(dt_bias[:, :, None, None], (2, DN_HEADS, 1, DN_HD))
    o_f, o_r, states = _dn_fwd(q, k, v, al, be, a_rows, dt_rows, "dn_fwd")
    z = p[:, 3 * TOK_W:4 * TOK_W]
    gain = out_norm.reshape(1, DN_HD)
    (og,) = _rowwise(_fn_outnorm, [o_f, o_r, z], [gain], [(TOK_W, BF16)], ROW_TB, "outnorm_fwd")
    qm = _heads_major(p[:, 4 * TOK_W + 4 * DN_HEADS:DN_IN])
    memo = _mem_fwd(qm, km, vm, min(MEM_TB, t), "mem_fwd1")
    cat = jnp.concatenate([og, _heads_minor(memo).astype(BF16)], axis=1)
    return cat, (xp, w8, q, k, v, al, be, a_rows, dt_rows, o_f, o_r, states, z, gain, qm)


def _dn_mixer_bwd(dcat, res, km, vm):
    xp, w8, q, k, v, al, be, a_rows, dt_rows, o_f, o_r, states, z, gain, qm = res
    t = dcat.shape[0]
    rt = min(CONV_RT, t)
    (do, dz), (dgain,) = _rowwise_bwd(_fn_outnorm, [o_f, o_r, z], [gain], [dcat[:, :TOK_W]], [F32, None, F32],
                                      ROW_TB, "outnorm_bwd")
    d_f, d_r, dgate, da, ddt = _dn_bwd(q, k, v, al, be, a_rows, dt_rows, states, do, "dn_bwd")
    dxs, dws = [], []
    for kind, nm in enumerate("qkv"):
        dx, dw = _conv_bwd(xp, w8, d_f[kind], d_r[kind], kind, rt, f"conv_bwd_{nm}")
        dxs.append(dx)
        dws.append(dw)
    dconv = jnp.concatenate(dws, axis=1)[:DN_CONV]
    dqm, dkm, dvm = _mem_bwd(qm, km, vm, _heads_major(dcat[:, TOK_W:]), min(MEM_TB, t), "mem_bwd1")
    dp = jnp.concatenate(dxs + [dz, dgate, _heads_minor(dqm), jnp.zeros((t, DN_IN_PAD - DN_IN), F32)],
                         axis=1).astype(BF16)
    return dp, dconv, da[:, :, 0, 0], ddt[:, :, 0, 0], dgain.reshape(DN_HD), dkm, dvm


SWI_TB = 256


def _ffn_fwd(h, w_gu_t, w_d, li):
    gu = _matmul(h, w_gu_t, "nt", BF16, f"ffn_gu{li}")
    (a,) = _rowwise(_fn_swiglu, [gu], [], [(D_FF, BF16)], SWI_TB, f"swiglu_fwd{li}")
    return _matmul(a, w_d, "nn", F32, f"ffn_down{li}"), gu, a


def _ffn_bwd(df, h, w_gu_t, w_d, gu, a, li):
    da = _matmul(df, w_d, "nt", BF16, f"ffn_down_dx{li}")
    dwd = _matmul(a, df, "tn", BF16, f"ffn_down_dw{li}")
    (dgu,), _ = _rowwise_bwd(_fn_swiglu, [gu], [], [da], [BF16], SWI_TB, f"swiglu_bwd{li}")
    dh = _matmul(dgu, w_gu_t, "nn", F32, f"ffn_gu_dx{li}")
    dwgu_t = _matmul(dgu, h, "tn", BF16, f"ffn_gu_dw{li}")
    return dh, dwgu_t, dwd


def _fn_first(x, g):
    return x, _rms(x, g)


def _me_xyc():
    return lax.axis_index("x"), lax.axis_index("y"), lax.axis_index("c")


def _flip(coords, k):
    x, y, c = coords
    return (1 - x if k & 4 else x, 1 - y if k & 2 else y, 1 - c if k & 1 else c)


def _index(coords):
    x, y, c = coords
    return 4 * x + 2 * y + c


def _window(ref, axis, size, d):
    idx = [slice(None)] * len(ref.shape)
    idx[axis] = pl.ds(pl.multiple_of(d * size, size), size)
    return ref.at[tuple(idx)]


def _comm_call(body, n, ins, out_shapes, name):
    hbm = pl.BlockSpec(memory_space=pl.ANY)
    return pl.pallas_call(
        body, name=name, in_specs=[hbm] * n, out_specs=[hbm] * n, out_shape=out_shapes,
        scratch_shapes=[pltpu.SemaphoreType.DMA((N_DEV - 1, n)), pltpu.SemaphoreType.DMA((N_DEV - 1, n)),
                        pltpu.SemaphoreType.DMA((n,))],
    )(*ins)


def _run_exchange(n, local, remote, send_sems, recv_sems):
    me = _me_xyc()
    locs = [local(p) for p in range(n)]
    for cp in locs:
        cp.start()
    sends = [remote(k, p, me, _flip(me, k)) for k in range(1, N_DEV) for p in range(n)]
    for cp in sends:
        cp.start()
    for k in range(1, N_DEV):
        for p in range(n):
            remote(k, p, _flip(me, k), me).wait_recv()
    for cp in sends:
        cp.wait_send()
    for cp in locs:
        cp.wait()


def _all_gather(shards, axes, name):
    n = len(shards)
    sizes = [s.shape[a] for s, a in zip(shards, axes)]

    def body(*refs):
        ins, outs = refs[:n], refs[n:2 * n]
        send_sems, recv_sems, loc_sems = refs[2 * n:]
        me = _me_xyc()

        def local(p):
            return pltpu.make_async_copy(ins[p], _window(outs[p], axes[p], sizes[p], _index(me)), loc_sems.at[p])

        def remote(k, p, owner, to):
            return pltpu.make_async_remote_copy(
                src_ref=ins[p], dst_ref=_window(outs[p], axes[p], sizes[p], _index(owner)),
                send_sem=send_sems.at[k - 1, p], recv_sem=recv_sems.at[k - 1, p], device_id=to, device_id_type=MESH)

        _run_exchange(n, local, remote, send_sems, recv_sems)

    def full(s, a):
        return s.shape[:a] + (N_DEV * s.shape[a],) + s.shape[a + 1:]

    return _comm_call(body, n, shards, [jax.ShapeDtypeStruct(full(s, a), s.dtype) for s, a in zip(shards, axes)], name)


def _exchange(fulls, axes, name):
    n = len(fulls)
    sizes = [None if a is None else f.shape[a] // N_DEV for f, a in zip(fulls, axes)]

    def part_shape(f, a):
        return f.shape if a is None else f.shape[:a] + (f.shape[a] // N_DEV,) + f.shape[a + 1:]

    def body(*refs):
        ins, outs = refs[:n], refs[n:2 * n]
        send_sems, recv_sems, loc_sems = refs[2 * n:]
        me = _me_xyc()

        def src(p, to):
            return ins[p] if axes[p] is None else _window(ins[p], axes[p], sizes[p], _index(to))

        def local(p):
            return pltpu.make_async_copy(src(p, me), outs[p].at[_index(me)], loc_sems.at[p])

        def remote(k, p, sender, to):
            return pltpu.make_async_remote_copy(
                src_ref=src(p, to), dst_ref=outs[p].at[_index(sender)],
                send_sem=send_sems.at[k - 1, p], recv_sem=recv_sems.at[k - 1, p], device_id=to, device_id_type=MESH)

        _run_exchange(n, local, remote, send_sems, recv_sems)

    return _comm_call(body, n, fulls,
                      [jax.ShapeDtypeStruct((N_DEV,) + part_shape(f, a), f.dtype) for f, a in zip(fulls, axes)], name)


_HBM = pl.BlockSpec(memory_space=pltpu.HBM)
_SEM = pl.BlockSpec(memory_space=pltpu.SEMAPHORE)
_EFFECT = pltpu.SideEffectType.DATAFLOW_SIDE_EFFECTING


def _in_hbm(a):
    return pltpu.with_memory_space_constraint(a, pltpu.HBM)


def _split_start(srcs, lands, after, descr, name):
    n = len(srcs)

    def body(*refs):
        ins, lnd = refs[:n], refs[n:2 * n]
        send_sems, recv_sems = refs[2 * n + 1], refs[2 * n + 2]
        token = refs[-1]
        me = _me_xyc()
        for k in range(1, N_DEV):
            for p in range(n):
                descr(k, p, ins, lnd, send_sems, recv_sems, me, _flip(me, k)).start()
        token[...] = jnp.zeros_like(token)

    sems = pltpu.SemaphoreType.DMA(((N_DEV - 1) * n,))
    res = pl.pallas_call(
        body, name=name,
        out_shape=(sems, sems, *[pltpu.HBM(a.shape, a.dtype) for a in (*srcs, *lands)],
                   jax.ShapeDtypeStruct((8, 128), F32)),
        in_specs=[_HBM] * (2 * n) + [pl.BlockSpec(memory_space=pl.ANY)],
        out_specs=(_SEM, _SEM, *[_HBM] * (2 * n), pl.BlockSpec(memory_space=pltpu.VMEM)),
        input_output_aliases={i: 2 + i for i in range(2 * n)},
        compiler_params=pltpu.CompilerParams(has_side_effects=_EFFECT),
    )(*[_in_hbm(a) for a in (*srcs, *lands)], after)
    return res[0], res[1], res[2:2 + n], res[2 + n:2 + 2 * n], res[-1]


def _split_wait(send_sems, recv_sems, srcs, lands, after, descr, name):
    n = len(srcs)

    def body(*refs):
        ins, lnd = refs[:n], refs[n:2 * n]
        s_sems, r_sems = refs[2 * n], refs[2 * n + 1]
        me = _me_xyc()
        for k in range(1, N_DEV):
            for p in range(n):
                peer = _flip(me, k)
                descr(k, p, ins, lnd, s_sems, r_sems, me, peer).wait_send()
                descr(k, p, ins, lnd, s_sems, r_sems, peer, me).wait_recv()

    res = pl.pallas_call(
        body, name=name,
        out_shape=tuple(pltpu.HBM(a.shape, a.dtype) for a in (*srcs, *lands)),
        in_specs=[_HBM] * (2 * n) + [_SEM, _SEM, pl.BlockSpec(memory_space=pl.ANY)],
        out_specs=tuple([_HBM] * (2 * n)),
        input_output_aliases={i: i for i in range(2 * n)},
        compiler_params=pltpu.CompilerParams(has_side_effects=_EFFECT),
    )(*srcs, *lands, send_sems, recv_sems, after)
    return list(res[n:])


def _gather_descr(axes, sizes):
    def descr(k, p, ins, lnd, send_sems, recv_sems, sender, dest):
        return pltpu.make_async_remote_copy(
            src_ref=ins[p], dst_ref=_window(lnd[p], axes[p], sizes[p], _index(sender)),
            send_sem=send_sems.at[(k - 1) * len(axes) + p], recv_sem=recv_sems.at[(k - 1) * len(axes) + p],
            device_id=dest, device_id_type=MESH)
    return descr


def _exchange_descr(axes, sizes):
    def descr(k, p, ins, lnd, send_sems, recv_sems, sender, dest):
        return pltpu.make_async_remote_copy(
            src_ref=_window(ins[p], axes[p], sizes[p], _index(dest)), dst_ref=lnd[p].at[_index(sender)],
            send_sem=send_sems.at[(k - 1) * len(axes) + p], recv_sem=recv_sems.at[(k - 1) * len(axes) + p],
            device_id=dest, device_id_type=MESH)
    return descr


def _gather_begin(shards, axes, after, name):
    sizes = [s.shape[a] for s, a in zip(shards, axes)]
    me = _index(_me_xyc())
    lands = []
    for s, a, sz in zip(shards, axes, sizes):
        full = s.shape[:a] + (N_DEV * sz,) + s.shape[a + 1:]
        lands.append(lax.dynamic_update_slice_in_dim(lax.empty(full, s.dtype), s, me * sz, a))
    descr = _gather_descr(axes, sizes)
    send_sems, recv_sems, srcs, lands, token = _split_start(shards, lands, after, descr, name)
    return (send_sems, recv_sems, srcs, lands, descr), token


def _exchange_begin(fulls, axes, after, name):
    sizes = [f.shape[a] // N_DEV for f, a in zip(fulls, axes)]
    me = _index(_me_xyc())
    lands = []
    for f, a, sz in zip(fulls, axes, sizes):
        own = lax.dynamic_slice_in_dim(f, me * sz, sz, a)
        lands.append(lax.dynamic_update_slice_in_dim(lax.empty((N_DEV,) + own.shape, f.dtype), own[None], me, 0))
    descr = _exchange_descr(axes, sizes)
    send_sems, recv_sems, srcs, lands, token = _split_start(fulls, lands, after, descr, name)
    return (send_sems, recv_sems, srcs, lands, descr), token


def _split_end(handle, after, name):
    send_sems, recv_sems, srcs, lands, descr = handle
    return _split_wait(send_sems, recv_sems, srcs, lands, after, descr, name)


def _adam_math(g, w, m, v):
    m = ADAM_B1 * m + (1.0 - ADAM_B1) * g
    v = ADAM_B2 * v + (1.0 - ADAM_B2) * (g * g)
    m_hat = m / (1.0 - ADAM_B1 ** ADAM_STEP)
    v_hat = v / (1.0 - ADAM_B2 ** ADAM_STEP)
    delta = -ADAM_LR * (m_hat / (jnp.sqrt(v_hat) + ADAM_EPS) + ADAM_WD * w)
    return delta, m, v


def _sum_slabs(r_ref):
    g = r_ref[0].astype(F32)
    for s in range(1, N_DEV):
        g = g + r_ref[s].astype(F32)
    return g


def _adamw_reduce(recv, w, m, v, tb, name):
    r, c = w.shape

    def body(r_ref, w_ref, m_ref, v_ref, g_ref, d_ref, nm_ref, nv_ref):
        g = _sum_slabs(r_ref)
        g_ref[...] = g
        d_ref[...], nm_ref[...], nv_ref[...] = _adam_math(g, w_ref[...], m_ref[...], v_ref[...])

    blk = pl.BlockSpec((tb, c), lambda i: (i, 0))
    return pl.pallas_call(
        body, name=name, grid=(r // tb,),
        in_specs=[pl.BlockSpec((N_DEV, tb, c), lambda i: (0, i, 0)), blk, blk, blk],
        out_specs=[blk] * 4, out_shape=[jax.ShapeDtypeStruct((r, c), F32)] * 4,
        compiler_params=_cparams(("parallel",)),
    )(recv, w, m, v)


def _reduce8(recv, tb, name):
    r, c = recv.shape[1:]

    def body(r_ref, g_ref):
        g_ref[...] = _sum_slabs(r_ref)

    return pl.pallas_call(
        body, name=name, grid=(r // tb,),
        in_specs=[pl.BlockSpec((N_DEV, tb, c), lambda i: (0, i, 0))],
        out_specs=pl.BlockSpec((tb, c), lambda i: (i, 0)), out_shape=jax.ShapeDtypeStruct((r, c), F32),
        compiler_params=_cparams(("parallel",)),
    )(recv)


def _adamw(g, w, m, v, tb, name):
    r, c = w.shape

    def body(g_ref, w_ref, m_ref, v_ref, d_ref, nm_ref, nv_ref):
        d_ref[...], nm_ref[...], nv_ref[...] = _adam_math(g_ref[...], w_ref[...], m_ref[...], v_ref[...])

    blk = pl.BlockSpec((tb, c), lambda i: (i, 0))
    return pl.pallas_call(
        body, name=name, grid=(r // tb,), in_specs=[blk] * 4, out_specs=[blk] * 3,
        out_shape=[jax.ShapeDtypeStruct((r, c), F32)] * 3, compiler_params=_cparams(("parallel",)),
    )(g, w, m, v)


DN_IN_SHARD = DN_IN // N_DEV
DN_IN_SHARD_PAD = 432
CONV_SHARD = (1, DN_CONV, 288)


def _pack_small(arrs, rows):
    flat = jnp.concatenate([a.astype(F32).reshape(-1) for a in arrs])
    return jnp.pad(flat, (0, rows * PACK_C - flat.size)).reshape(rows, PACK_C)


def _unpack_small(packed, shapes):
    flat, out, off = packed.reshape(-1), [], 0
    for shp in shapes:
        n = int(np.prod(shp))
        out.append(flat[off:off + n].reshape(shp))
        off += n
    return out


def kernel(x, mem, rel_bias, att_w_in, att_w_out, dn_w_in, dn_conv, dn_a_log, dn_dt_bias, dn_out_norm, dn_w_out, mem_norm, mem_w_kv, norm_mix_pre, norm_mix_post, norm_ffn_pre, norm_ffn_post, ffn_w_gate_up, ffn_w_down, loss_target, m_rel_bias, m_att_w_in, m_att_w_out, m_dn_w_in, m_dn_conv, m_dn_a_log, m_dn_dt_bias, m_dn_out_norm, m_dn_w_out, m_mem_norm, m_mem_w_kv, m_norm_mix_pre, m_norm_mix_post, m_norm_ffn_pre, m_norm_ffn_post, m_ffn_w_gate_up, m_ffn_w_down, v_rel_bias, v_att_w_in, v_att_w_out, v_dn_w_in, v_dn_conv, v_dn_a_log, v_dn_dt_bias, v_dn_out_norm, v_dn_w_out, v_mem_norm, v_mem_w_kv, v_norm_mix_pre, v_norm_mix_post, v_norm_ffn_pre, v_norm_ffn_post, v_ffn_w_gate_up, v_ffn_w_down):
    x0, mem0, tgt = x[0], mem[0], loss_target[0]
    t = x0.shape[0]
    axes = ("x", "y", "c")

    def t_shard(w):
        return jnp.swapaxes(w, 1, 2).astype(BF16)

    dn_in_pad = ((0, 0), (0, DN_IN_SHARD_PAD - DN_IN_SHARD), (0, 0))
    (w_att_in_t,) = _all_gather([t_shard(att_w_in)], [1], "allgather_first")
    w_att_in_t = w_att_in_t[0]
    gu_t, down = t_shard(ffn_w_gate_up), ffn_w_down.astype(BF16)
    gather_o, tok_o = _gather_begin([att_w_out.astype(BF16), mem_w_kv.astype(BF16)], [1, 1], w_att_in_t,
                                    "gather_att_out_start")
    gather_a, tok_a = _gather_begin([gu_t[0:1], down[0:1]], [1, 1], tok_o, "gather_ffn0_start")
    gather_b, tok_b = _gather_begin(
        [jnp.pad(t_shard(dn_w_in), dn_in_pad), dn_w_out.astype(BF16), gu_t[1:2], down[1:2], dn_conv],
        [1, 1, 1, 1, 0], tok_a, "gather_layer1_start")

    def gain(a, i):
        return a[i].reshape(1, D)

    (h0,) = _rowwise(_fn_pre, [x0], [gain(norm_mix_pre, 0) + tok_b[0:1, 0:1]], [(D, BF16)], ROW_TB, "pre0")
    p0 = _matmul(h0, w_att_in_t, "nt", BF16, "att_in")
    late = {}

    def kv0(after):
        late["w_att_out"], late["w_kv"] = _split_end(gather_o, after, "gather_att_out_wait")
        return _mem_kv_fwd(mem0, gain(mem_norm, 0), late["w_kv"][0], 0)

    cat0, res0, (km0, vm0, memn0) = _attn_mixer_fwd(p0, rel_bias, kv0)
    w_att_out, w_kv = late["w_att_out"][0], late["w_kv"]
    y0 = _matmul(cat0, w_att_out, "nn", F32, "att_out")
    g_a = [gain(norm_mix_post, 0), gain(norm_ffn_pre, 0)]
    x1, h1 = _rowwise(_fn_res_pre, [x0, y0], g_a, [(D, F32), (D, BF16)], ROW_TB, "res_pre0")
    w_gu_t0, w_down0 = [w[0] for w in _split_end(gather_a, h1, "gather_ffn0_wait")]
    f0, gu0, a0 = _ffn_fwd(h1, w_gu_t0, w_down0, 0)
    g_b = [gain(norm_ffn_post, 0), gain(norm_mix_pre, 1)]
    x2, h2 = _rowwise(_fn_res_pre, [x1, f0], g_b, [(D, F32), (D, BF16)], ROW_TB, "res_pre1")
    km1, vm1, memn1 = _mem_kv_fwd(mem0, gain(mem_norm, 1), w_kv[1], 1)
    w_dn_in_g, w_dn_out, w_gu_t1, w_down1, conv_g = _split_end(gather_b, h2, "gather_layer1_wait")
    w_dn_in_g, w_dn_out, w_gu_t1, w_down1 = w_dn_in_g[0], w_dn_out[0], w_gu_t1[0], w_down1[0]
    conv_full = conv_g.transpose(1, 0, 2).reshape(DN_CONV, 3 * TOK_W)
    w_dn_in_t = jnp.concatenate(
        [w_dn_in_g[DN_IN_SHARD_PAD * j:DN_IN_SHARD_PAD * j + DN_IN_SHARD] for j in range(N_DEV)]
        + [jnp.zeros((DN_IN_PAD - DN_IN, D), BF16)], axis=0)
    p1 = _matmul(h2, w_dn_in_t, "nt", F32, "dn_in")
    cat1, res1 = _dn_mixer_fwd(p1, conv_full, dn_a_log[0], dn_dt_bias[0], dn_out_norm[0], km1, vm1)
    y1 = _matmul(cat1, w_dn_out, "nn", F32, "dn_out")
    g_c = [gain(norm_mix_post, 1), gain(norm_ffn_pre, 1)]
    x3, h3 = _rowwise(_fn_res_pre, [x2, y1], g_c, [(D, F32), (D, BF16)], ROW_TB, "res_pre2")
    f1, gu1, a1 = _ffn_fwd(h3, w_gu_t1, w_down1, 1)
    g_d = [gain(norm_ffn_post, 1)]
    (x4,) = _rowwise(_fn_res, [x3, f1], g_d, [(D, F32)], ROW_TB, "res3")
    dx4, lrow = _loss_kernel(x4, tgt, ROW_TB, "loss")
    loss = lax.psum(lrow[0, 0] * (0.5 / D), axes)

    (df1,), (dg_fpost1,) = _rowwise_bwd(_fn_res, [x3, f1], g_d, [dx4], [None, BF16], ROW_TB, "res3_bwd")
    dh3, dwgu1, dwd1 = _ffn_bwd(df1, h3, w_gu_t1, w_down1, gu1, a1, 1)
    (dx2, dy1), (dg_mpost1, dg_fpre1) = _rowwise_bwd(_fn_res_pre, [x2, y1], g_c, [dx4, dh3], [F32, BF16],
                                                     ROW_TB, "res_pre2_bwd")
    dcat1 = _matmul(dy1, w_dn_out, "nt", F32, "dn_out_dx")
    dw_dn_out = _matmul(cat1, dy1, "tn", BF16, "dn_out_dw")
    dp1, dconv, da_log, ddt_bias, dout_norm, dkm1, dvm1 = _dn_mixer_bwd(dcat1, res1, km1, vm1)
    dwkv1, dg_mem1 = _mem_kv_bwd(mem0, gain(mem_norm, 1), w_kv[1], memn1, dkm1, dvm1, 1)
    dh2 = _matmul(dp1, w_dn_in_t, "nn", F32, "dn_in_dx")
    dw_dn_in_t = _matmul(dp1, h2, "tn", BF16, "dn_in_dw")
    dn_in_parts = [jnp.pad(dw_dn_in_t[DN_IN_SHARD * j:DN_IN_SHARD * (j + 1)],
                           ((0, DN_IN_SHARD_PAD - DN_IN_SHARD), (0, 0))) for j in range(N_DEV)]
    xch_b, tok = _exchange_begin(
        [jnp.concatenate(dn_in_parts, axis=0)[None], dw_dn_out[None], dwkv1[None], dwgu1[None], dwd1[None]],
        [1, 1, 1, 1, 1], dh2, "exchange_layer1_start")
    (dx1, df0), (dg_fpost0, dg_mpre1) = _rowwise_bwd(_fn_res_pre, [x1, f0], [g + tok[0:1, 0:1] for g in g_b],
                                                     [dx2, dh2], [F32, BF16], ROW_TB, "res_pre1_bwd")
    dh1, dwgu0, dwd0 = _ffn_bwd(df0, h1, w_gu_t0, w_down0, gu0, a0, 0)
    xch_a, tok = _exchange_begin([dwgu0[None], dwd0[None]], [1, 1], dh1, "exchange_ffn0_start")
    (dx0, dy0), (dg_mpost0, dg_fpre0) = _rowwise_bwd(_fn_res_pre, [x0, y0], [g + tok[0:1, 0:1] for g in g_a],
                                                     [dx1, dh1], [F32, BF16], ROW_TB, "res_pre0_bwd")
    dcat0 = _matmul(dy0, w_att_out, "nt", F32, "att_out_dx")
    dw_att_out = _matmul(cat0, dy0, "tn", BF16, "att_out_dw")
    dp0, drel, dkm0, dvm0 = _attn_mixer_bwd(dcat0, res0, km0, vm0)
    dwkv0, dg_mem0 = _mem_kv_bwd(mem0, gain(mem_norm, 0), w_kv[0], memn0, dkm0, dvm0, 0)
    xch_o, tok = _exchange_begin([dw_att_out[None], dwkv0[None]], [1, 1], dp0, "exchange_att_out_start")
    dh0 = _matmul(dp0, w_att_in_t, "nn", F32, "att_in_dx")
    dw_att_in_t = _matmul(dp0, h0, "tn", BF16, "att_in_dw")
    (grad_x,), (dg_mpre0,) = _rowwise_bwd(_fn_first, [x0], [gain(norm_mix_pre, 0) + tok[0:1, 0:1]], [dx0, dh0],
                                          [F32], ROW_TB, "pre0_bwd")

    small_grads = [drel, da_log, ddt_bias, dout_norm, jnp.concatenate([dg_mem0, dg_mem1]),
                   jnp.concatenate([dg_mpre0, dg_mpre1]), jnp.concatenate([dg_mpost0, dg_mpost1]),
                   jnp.concatenate([dg_fpre0, dg_fpre1]), jnp.concatenate([dg_fpost0, dg_fpost1]), dconv]
    r_att_in, r_small = _exchange([dw_att_in_t[None], _pack_small(small_grads, SMALL_ROWS)], [1, None],
                                  "exchange_last")
    r_att_out, r_kv0 = _split_end(xch_o, r_small, "exchange_att_out_wait")
    r_gu0, r_down0 = _split_end(xch_a, r_small, "exchange_ffn0_wait")
    r_dn_in, r_dn_out, r_kv1, r_gu1, r_down1 = _split_end(xch_b, r_small, "exchange_layer1_wait")

    def rows(a):
        return a.reshape((-1,) + a.shape[-1:])

    def row_sharded(recv, w, m, v, tb, name):
        outs = _adamw_reduce(recv.reshape((N_DEV, -1) + recv.shape[-1:]), rows(w), rows(m), rows(v), tb, name)
        return [o.reshape(w.shape) for o in outs]

    def col_sharded(recv, w, m, v, tb, name):
        g_t = _reduce8(recv.reshape((N_DEV, -1) + recv.shape[-1:]), tb, name + "_sum")
        g = jnp.swapaxes(g_t.reshape(recv.shape[1:])[:, :w.shape[2]], 1, 2)
        outs = _adamw(rows(g), rows(w), rows(m), rows(v), 256, name)
        return [g] + [o.reshape(w.shape) for o in outs]

    def per_layer(fn, recvs, w, m, v, tb, name):
        outs = [fn(r, w[l:l + 1], m[l:l + 1], v[l:l + 1], tb, f"{name}{l}") for l, r in enumerate(recvs)]
        return [jnp.concatenate(pair, axis=0) for pair in zip(*outs)]

    big = [col_sharded(r_att_in, att_w_in, m_att_w_in, v_att_w_in, 320, "adamw_att_in"),
           row_sharded(r_att_out, att_w_out, m_att_w_out, v_att_w_out, 128, "adamw_att_out"),
           col_sharded(r_dn_in, dn_w_in, m_dn_w_in, v_dn_w_in, 432, "adamw_dn_in"),
           row_sharded(r_dn_out, dn_w_out, m_dn_w_out, v_dn_w_out, 128, "adamw_dn_out"),
           per_layer(row_sharded, [r_kv0, r_kv1], mem_w_kv, m_mem_w_kv, v_mem_w_kv, 128, "adamw_mem_kv"),
           per_layer(col_sharded, [r_gu0, r_gu1], ffn_w_gate_up, m_ffn_w_gate_up, v_ffn_w_gate_up, 176,
                     "adamw_ffn_gu"),
           per_layer(row_sharded, [r_down0, r_down1], ffn_w_down, m_ffn_w_down, v_ffn_w_down, 176,
                     "adamw_ffn_down")]
    g_big, d_big, nm_big, nv_big = [[b[i] for b in big] for i in range(4)]

    g_small = _reduce8(r_small, SMALL_ROWS, "reduce_small")
    rep_shapes = [(32, 12), (1, 2, 6), (1, 2, 6), (1, 128), (2, D), (2, D), (2, D), (2, D), (2, D)]
    *g_rep, g_conv_full = _unpack_small(g_small, rep_shapes + [(DN_CONV, 3 * TOK_W)])
    me = _index(_me_xyc())
    g_conv = lax.dynamic_slice(g_conv_full, (0, me * 288), (DN_CONV, 288)).reshape(CONV_SHARD)
    small_shapes = rep_shapes + [CONV_SHARD]
    small_w = [rel_bias, dn_a_log, dn_dt_bias, dn_out_norm, mem_norm, norm_mix_pre, norm_mix_post,
               norm_ffn_pre, norm_ffn_post, dn_conv]
    small_m = [m_rel_bias, m_dn_a_log, m_dn_dt_bias, m_dn_out_norm, m_mem_norm, m_norm_mix_pre, m_norm_mix_post,
               m_norm_ffn_pre, m_norm_ffn_post, m_dn_conv]
    small_v = [v_rel_bias, v_dn_a_log, v_dn_dt_bias, v_dn_out_norm, v_mem_norm, v_norm_mix_pre, v_norm_mix_post,
               v_norm_ffn_pre, v_norm_ffn_post, v_dn_conv]
    g_small_list = g_rep + [g_conv]
    outs_small = _adamw(_pack_small(g_small_list, 24), _pack_small(small_w, 24), _pack_small(small_m, 24),
                        _pack_small(small_v, 24), 24, "adamw_small")
    d_small, nm_small, nv_small = [_unpack_small(o, small_shapes) for o in outs_small]

    def ordered(small, big):
        return [small[0], big[0], big[1], big[2], small[9], small[1], small[2], small[3], big[3], small[4],
                big[4], small[5], small[6], small[7], small[8], big[5], big[6]]

    g_small_out = [g.reshape(s) for g, s in zip(g_small_list, small_shapes)]
    return (loss, grad_x[None], *ordered(g_small_out, g_big), *ordered(d_small, d_big),
            *ordered(nm_small, nm_big), *ordered(nv_small, nv_big))
```

```python
import functools
import math

import numpy as np
import jax
import jax.numpy as jnp
from jax import lax
from jax.experimental import pallas as pl
from jax.experimental.pallas import tpu as pltpu

F32 = jnp.float32
BF16 = jnp.bfloat16
HI = lax.Precision.HIGHEST
MESH = pl.DeviceIdType.MESH

N_DEV = 8
D = 1024
EPS = 1e-6
NEG = -1e30
TOK_W = 768
MEM_W = 256
ATT_HD = 64
DIL_GROUPS = ((128, 1), (512, 4), (2048, 16))
BAND_HALF = 64
REL_BUCKETS = 32
REL_MAX_DIST = 1024
DN_HD = 128
DN_HEADS = 6
DN_CONV = 5
DN_CHUNK = 64
MEM_HEADS = 4
D_FF = 2816
ATT_IN = 2560
DN_IN = 3352
DN_IN_PAD = 3456

ADAM_LR, ADAM_B1, ADAM_B2, ADAM_EPS, ADAM_WD, ADAM_STEP = 0.001, 0.9, 0.999, 1e-08, 0.01, 10

PACK_C = 512
BIG_ROWS = 6480
SMALL_ROWS = 48
VMEM_LIMIT = 48 * 1024 * 1024


def _cparams(sem=None):
    kw = dict(vmem_limit_bytes=VMEM_LIMIT)
    if sem is not None:
        kw["dimension_semantics"] = sem
    return pltpu.CompilerParams(**kw)


def _tile(n, cap):
    if n <= cap:
        return n
    best = None
    for t in range(128, cap + 1, 128):
        if n % t == 0:
            best = t
    assert best is not None, (n, cap)
    return best


def _matmul(a, b, mode, out_dtype, name, tm=1024, tn=1408, tk=None):
    if tk is None:
        tk = 2048 if mode == "tn" else 2816
    if mode == "nn":
        (m, kc), (_, n) = a.shape, b.shape
        dims = (((1,), (0,)), ((), ()))
    elif mode == "nt":
        (m, kc), (n, _) = a.shape, b.shape
        dims = (((1,), (1,)), ((), ()))
    else:
        (kc, m), (_, n) = a.shape, b.shape
        dims = (((0,), (0,)), ((), ()))
    tm = m if m <= tm else _tile(m, tm)
    tn = _tile(n, tn)
    tk = _tile(kc, tk)
    nk = kc // tk

    def body(a_ref, b_ref, o_ref, acc_ref):
        k = pl.program_id(2)
        part = lax.dot_general(a_ref[...], b_ref[...], dims, preferred_element_type=F32)

        @pl.when(k == 0)
        def _():
            acc_ref[...] = part

        @pl.when(k > 0)
        def _():
            acc_ref[...] += part

        @pl.when(k == nk - 1)
        def _():
            o_ref[...] = acc_ref[...].astype(o_ref.dtype)

    if mode == "nn":
        a_spec = pl.BlockSpec((tm, tk), lambda i, j, k: (i, k))
        b_spec = pl.BlockSpec((tk, tn), lambda i, j, k: (k, j))
    elif mode == "nt":
        a_spec = pl.BlockSpec((tm, tk), lambda i, j, k: (i, k))
        b_spec = pl.BlockSpec((tn, tk), lambda i, j, k: (j, k))
    else:
        a_spec = pl.BlockSpec((tk, tm), lambda i, j, k: (k, i))
        b_spec = pl.BlockSpec((tk, tn), lambda i, j, k: (k, j))
    return pl.pallas_call(
        body, name=name, grid=(m // tm, n // tn, nk),
        in_specs=[a_spec, b_spec],
        out_specs=pl.BlockSpec((tm, tn), lambda i, j, k: (i, j)),
        out_shape=jax.ShapeDtypeStruct((m, n), out_dtype),
        scratch_shapes=[pltpu.VMEM((tm, tn), F32)],
        compiler_params=_cparams(("parallel", "parallel", "arbitrary")),
    )(a, b)


def _rowwise(fn, rows, params, outs, tb, name):
    t = rows[0].shape[0]
    nr, npar = len(rows), len(params)

    def body(*refs):
        ins = [r[...].astype(F32) for r in refs[:nr + npar]]
        res = fn(*ins)
        for o_ref, r in zip(refs[nr + npar:], res):
            o_ref[...] = r.astype(o_ref.dtype)

    return pl.pallas_call(
        body, name=name, grid=(t // tb,),
        in_specs=[pl.BlockSpec((tb, r.shape[1]), lambda i: (i, 0)) for r in rows]
        + [pl.BlockSpec(p.shape, lambda i: (0, 0)) for p in params],
        out_specs=[pl.BlockSpec((tb, c), lambda i: (i, 0)) for c, _ in outs],
        out_shape=[jax.ShapeDtypeStruct((t, c), dt) for c, dt in outs],
        compiler_params=_cparams(("parallel",)),
    )(*rows, *params)


def _rowwise_bwd(fn, rows, params, cots, row_grad, tb, name):
    t = rows[0].shape[0]
    nr, npar, nc = len(rows), len(params), len(cots)
    want = [i for i, g in enumerate(row_grad) if g is not None]

    def body(*refs):
        ins = [r[...].astype(F32) for r in refs[:nr + npar]]
        cts = tuple(r[...].astype(F32) for r in refs[nr + npar:nr + npar + nc])
        outs = refs[nr + npar + nc:]
        _, vjp = jax.vjp(fn, *ins)
        grads = vjp(cts)
        for o_ref, i in zip(outs[:len(want)], want):
            o_ref[...] = grads[i].astype(o_ref.dtype)
        first = pl.program_id(0) == 0
        for o_ref, g in zip(outs[len(want):], grads[nr:]):
            @pl.when(first)
            def _(o_ref=o_ref, g=g):
                o_ref[...] = g

            @pl.when(jnp.logical_not(first))
            def _(o_ref=o_ref, g=g):
                o_ref[...] += g

    res = pl.pallas_call(
        body, name=name, grid=(t // tb,),
        in_specs=[pl.BlockSpec((tb, r.shape[1]), lambda i: (i, 0)) for r in rows]
        + [pl.BlockSpec(p.shape, lambda i: (0, 0)) for p in params]
        + [pl.BlockSpec((tb, c.shape[1]), lambda i: (i, 0)) for c in cots],
        out_specs=[pl.BlockSpec((tb, rows[i].shape[1]), lambda i_: (i_, 0)) for i in want]
        + [pl.BlockSpec(p.shape, lambda i: (0, 0)) for p in params],
        out_shape=[jax.ShapeDtypeStruct(rows[i].shape, row_grad[i]) for i in want]
        + [jax.ShapeDtypeStruct(p.shape, F32) for p in params],
        compiler_params=_cparams(("arbitrary",)),
    )(*rows, *params, *cots)
    return list(res[:len(want)]), list(res[len(want):])


def _rms(x, g):
    return x * lax.rsqrt(jnp.mean(x * x, axis=-1, keepdims=True) + EPS) * g


def _fn_pre(x, g):
    return (_rms(x, g),)


def _fn_res_pre(x, y, g_post, g_pre):
    x1 = x + _rms(y, g_post)
    return x1, _rms(x1, g_pre)


def _fn_res(x, y, g_post):
    return (x + _rms(y, g_post),)


def _sigmoid(x):
    return 1.0 / (1.0 + jnp.exp(-x))


def _silu(x):
    return x * _sigmoid(x)


def _fn_swiglu(gu):
    return (_silu(gu[:, :D_FF]) * gu[:, D_FF:],)


def _fn_combine(o, lse):
    ls = [lse[:, 256 * g:256 * (g + 1)] for g in range(3)]
    mx = lax.stop_gradient(jnp.maximum(jnp.maximum(ls[0], ls[1]), ls[2]))
    es = [jnp.exp(l - mx) for l in ls]
    inv = 1.0 / (es[0] + es[1] + es[2])
    return (jnp.concatenate([o[:, 256 * g:256 * (g + 1)] * (es[g] * inv) for g in range(3)], axis=1),)


def _fn_outnorm(o_f, o_r, z, gain):
    res = []
    for h in range(DN_HEADS):
        sl = slice(DN_HD * h, DN_HD * (h + 1))
        o = o_f[:, sl] + o_r[:, sl]
        res.append(o * lax.rsqrt(jnp.mean(o * o, axis=-1, keepdims=True) + EPS) * gain * _silu(z[:, sl]))
    return (jnp.concatenate(res, axis=1),)


def _loss_kernel(x, tgt, tb, name):
    t, d = x.shape

    def body(x_ref, t_ref, dx_ref, l_ref, acc_ref):
        i = pl.program_id(0)
        e = x_ref[...] - t_ref[...]
        dx_ref[...] = e * (1.0 / d)
        part = jnp.sum(e * e, axis=0, keepdims=True)

        @pl.when(i == 0)
        def _():
            acc_ref[...] = part

        @pl.when(i > 0)
        def _():
            acc_ref[...] += part

        @pl.when(i == t // tb - 1)
        def _():
            l_ref[...] = jnp.broadcast_to(jnp.sum(acc_ref[...], axis=-1, keepdims=True), (1, 128))

    return pl.pallas_call(
        body, name=name, grid=(t // tb,),
        in_specs=[pl.BlockSpec((tb, d), lambda i: (i, 0))] * 2,
        out_specs=[pl.BlockSpec((tb, d), lambda i: (i, 0)), pl.BlockSpec((1, 128), lambda i: (0, 0))],
        out_shape=[jax.ShapeDtypeStruct((t, d), F32), jax.ShapeDtypeStruct((1, 128), F32)],
        scratch_shapes=[pltpu.VMEM((1, d), F32)],
        compiler_params=_cparams(("arbitrary",)),
    )(x, tgt)


def _band_fn(l_sub, bq, i, q, kw, vw, bm):
    w = bq + 2 * BAND_HALF
    s = lax.dot_general((q * (ATT_HD ** -0.5)).astype(BF16), kw.astype(BF16), (((1,), (1,)), ((), ())),
                        preferred_element_type=F32) + bm
    kpos = i * bq - BAND_HALF + lax.broadcasted_iota(jnp.int32, (bq, w), 1)
    s = jnp.where((kpos >= 0) & (kpos < l_sub), s, NEG)
    m = lax.stop_gradient(jnp.max(s, axis=-1, keepdims=True))
    p = jnp.exp(s - m)
    den = jnp.sum(p, axis=-1, keepdims=True)
    o = jnp.dot(p.astype(BF16), vw.astype(BF16), preferred_element_type=F32) / den
    return o, jnp.broadcast_to(m + jnp.log(den), o.shape)


def _band_specs(dil, l_sub, bq):
    w = bq + 2 * BAND_HALF
    qs = pl.BlockSpec((None, bq, ATT_HD), lambda h, r, i: (h * dil + r, i, 0))
    ks = pl.BlockSpec((None, l_sub + 2 * BAND_HALF, ATT_HD), lambda h, r, i: (h * dil + r, 0, 0))
    bs = pl.BlockSpec((None, bq, w), lambda h, r, i: (h, 0, 0))
    return qs, ks, bs


def _band_fwd(q, k, v, bm, dil, l_sub, bq, name):
    w = bq + 2 * BAND_HALF
    qs, ks, bs = _band_specs(dil, l_sub, bq)

    def body(q_ref, k_ref, v_ref, bm_ref, o_ref, l_ref):
        i = pl.program_id(2)
        st = pl.multiple_of(i * bq, bq)
        o, lse = _band_fn(l_sub, bq, i, q_ref[...].astype(F32), k_ref[pl.ds(st, w), :].astype(F32),
                          v_ref[pl.ds(st, w), :].astype(F32), bm_ref[...])
        o_ref[...] = o
        l_ref[...] = lse

    return pl.pallas_call(
        body, name=name, grid=(4, dil, l_sub // bq),
        in_specs=[qs, ks, ks, bs], out_specs=[qs, qs],
        out_shape=[jax.ShapeDtypeStruct(q.shape, F32)] * 2,
        compiler_params=_cparams(("parallel", "parallel", "arbitrary")),
    )(q, k, v, bm)


def _band_bwd(q, k, v, bm, do, dlse, dil, l_sub, bq, name):
    w = bq + 2 * BAND_HALF
    qs, ks, bs = _band_specs(dil, l_sub, bq)

    def body(q_ref, k_ref, v_ref, bm_ref, do_ref, dl_ref, dq_ref, dk_ref, dv_ref, dbm_ref):
        r, i = pl.program_id(1), pl.program_id(2)
        st = pl.multiple_of(i * bq, bq)
        _, vjp = jax.vjp(functools.partial(_band_fn, l_sub, bq, i),
                         q_ref[...].astype(F32), k_ref[pl.ds(st, w), :].astype(F32),
                         v_ref[pl.ds(st, w), :].astype(F32), bm_ref[...])
        dq, dkw, dvw, dbm = vjp((do_ref[...], dl_ref[...]))
        dq_ref[...] = dq

        @pl.when(i == 0)
        def _():
            dk_ref[...] = jnp.zeros_like(dk_ref)
            dv_ref[...] = jnp.zeros_like(dv_ref)

        dk_ref[pl.ds(st, w), :] += dkw
        dv_ref[pl.ds(st, w), :] += dvw

        @pl.when((i == 0) & (r == 0))
        def _():
            dbm_ref[...] = dbm

        @pl.when((i > 0) | (r > 0))
        def _():
            dbm_ref[...] += dbm

    return pl.pallas_call(
        body, name=name, grid=(4, dil, l_sub // bq),
        in_specs=[qs, ks, ks, bs, qs, qs], out_specs=[qs, ks, ks, bs],
        out_shape=[jax.ShapeDtypeStruct(q.shape, F32), jax.ShapeDtypeStruct(k.shape, F32),
                   jax.ShapeDtypeStruct(k.shape, F32), jax.ShapeDtypeStruct(bm.shape, F32)],
        compiler_params=_cparams(("parallel", "arbitrary", "arbitrary")),
    )(q, k, v, bm, do, dlse)


def _t5_bucket(rel):
    half = REL_BUCKETS // 2
    max_exact = half // 2
    n = np.abs(rel)
    large = max_exact + (np.log(np.maximum(n, 1) / max_exact) / math.log(REL_MAX_DIST / max_exact)
                         * (half - max_exact)).astype(np.int64)
    large = np.minimum(large, half - 1)
    return ((rel > 0) * half + np.where(n < max_exact, n, large)).astype(np.int32)


def _bucket_onehot(dil):
    idx = _t5_bucket(np.arange(-BAND_HALF, BAND_HALF + 1) * dil)
    oh = np.zeros((2 * BAND_HALF + 1, REL_BUCKETS), np.float32)
    oh[np.arange(2 * BAND_HALF + 1), idx] = 1.0
    return oh


def _band_bias(rel_bias, gi, dil, bq):
    w = bq + 2 * BAND_HALF
    nb = 2 * BAND_HALF + 1
    bias = jnp.dot(jnp.asarray(_bucket_onehot(dil)), rel_bias[:, 4 * gi:4 * gi + 4], precision=HI)
    row = jnp.concatenate([bias.T, jnp.full((4, w + 1 - nb), NEG, F32)], axis=1)
    flat = jnp.tile(row, (1, bq))[:, :bq * w]
    return flat.reshape(4, bq, w)


def _relbias_grad(dbms, name):
    nb = 2 * BAND_HALF + 1
    bq = max(d.shape[1] for d in dbms)
    skew = []
    for dbm in dbms:
        bqg, w = dbm.shape[1], dbm.shape[2]
        flat = jnp.pad(dbm.reshape(4, bqg * w), ((0, 0), (0, bqg)))
        skew.append(jnp.pad(flat.reshape(4, bqg, w + 1)[:, :, :nb], ((0, 0), (0, bq - bqg), (0, 256 - nb))))
    sk = jnp.concatenate(skew, axis=0)
    oh = np.zeros((3, 256, 128), np.float32)
    for gi, (_, dil) in enumerate(DIL_GROUPS):
        oh[gi, :2 * BAND_HALF + 1, :REL_BUCKETS] = _bucket_onehot(dil)

    def body(s_ref, oh_ref, o_ref):
        col = jnp.sum(s_ref[...], axis=0, keepdims=True)
        o_ref[...] = jnp.dot(jnp.broadcast_to(col, (8, 256)), oh_ref[...], precision=HI, preferred_element_type=F32)

    out = pl.pallas_call(
        body, name=name, grid=(12,),
        in_specs=[pl.BlockSpec((None, bq, 256), lambda n: (n, 0, 0)),
                  pl.BlockSpec((None, 256, 128), lambda n: (n // 4, 0, 0))],
        out_specs=pl.BlockSpec((None, 8, 128), lambda n: (n, 0, 0)),
        out_shape=jax.ShapeDtypeStruct((12, 8, 128), F32),
        compiler_params=_cparams(("parallel",)),
    )(sk, jnp.asarray(oh))
    return out[:, 0, :REL_BUCKETS].T


def _mem_fn(q, k, v):
    s = lax.dot_general((q * (ATT_HD ** -0.5)).astype(BF16), k.astype(BF16), (((1,), (1,)), ((), ())),
                        preferred_element_type=F32)
    m = lax.stop_gradient(jnp.max(s, axis=-1, keepdims=True))
    p = jnp.exp(s - m)
    p = p / jnp.sum(p, axis=-1, keepdims=True)
    return jnp.dot(p.astype(BF16), v.astype(BF16), preferred_element_type=F32)


def _mem_specs(tb, ml):
    qs = pl.BlockSpec((None, tb, ATT_HD), lambda h, i: (h, i, 0))
    ks = pl.BlockSpec((None, ml, ATT_HD), lambda h, i: (h, 0, 0))
    return qs, ks


def _mem_fwd(q, k, v, tb, name):
    qs, ks = _mem_specs(tb, k.shape[1])

    def body(q_ref, k_ref, v_ref, o_ref):
        o_ref[...] = _mem_fn(q_ref[...].astype(F32), k_ref[...], v_ref[...])

    return pl.pallas_call(
        body, name=name, grid=(MEM_HEADS, q.shape[1] // tb),
        in_specs=[qs, ks, ks], out_specs=qs, out_shape=jax.ShapeDtypeStruct(q.shape, F32),
        compiler_params=_cparams(("parallel", "parallel")),
    )(q, k, v)


def _mem_bwd(q, k, v, do, tb, name):
    qs, ks = _mem_specs(tb, k.shape[1])

    def body(q_ref, k_ref, v_ref, do_ref, dq_ref, dk_ref, dv_ref):
        i = pl.program_id(1)
        _, vjp = jax.vjp(_mem_fn, q_ref[...].astype(F32), k_ref[...], v_ref[...])
        dq, dk, dv = vjp(do_ref[...])
        dq_ref[...] = dq

        @pl.when(i == 0)
        def _():
            dk_ref[...] = dk
            dv_ref[...] = dv

        @pl.when(i > 0)
        def _():
            dk_ref[...] += dk
            dv_ref[...] += dv

    return pl.pallas_call(
        body, name=name, grid=(MEM_HEADS, q.shape[1] // tb),
        in_specs=[qs, ks, ks, qs], out_specs=[qs, ks, ks],
        out_shape=[jax.ShapeDtypeStruct(q.shape, F32), jax.ShapeDtypeStruct(k.shape, F32),
                   jax.ShapeDtypeStruct(k.shape, F32)],
        compiler_params=_cparams(("parallel", "arbitrary")),
    )(q, k, v, do)


CONV_PAD = 8


def _conv_post(kind, acc):
    s = _silu(acc)
    if kind == 2:
        return s
    scale = DN_HD ** -0.5 if kind == 0 else 1.0
    return s * lax.rsqrt(jnp.sum(s * s, axis=-1, keepdims=True) + EPS) * scale


def _conv_acc(xp_ref, w, r0, rt):
    acc = None
    for i in range(DN_CONV):
        term = w[i:i + 1, :] * xp_ref[pl.ds(CONV_PAD + r0 + i - DN_CONV // 2, rt), :]
        acc = term if acc is None else acc + term
    return acc


def _conv_fwd(xp, w8, kind, rt, name):
    t = xp.shape[0] - 2 * CONV_PAD

    def body(xp_ref, w_ref, o_ref):
        w = w_ref[...]
        for r in range(t // rt):
            o_ref[pl.ds(r * rt, rt), :] = _conv_post(kind, _conv_acc(xp_ref, w, r * rt, rt))

    return pl.pallas_call(
        body, name=name, grid=(DN_HEADS,),
        in_specs=[pl.BlockSpec((t + 2 * CONV_PAD, DN_HD), lambda j: (0, 6 * kind + j)),
                  pl.BlockSpec((8, DN_HD), lambda j: (0, 6 * kind + j))],
        out_specs=pl.BlockSpec((t, DN_HD), lambda j: (0, j)),
        out_shape=jax.ShapeDtypeStruct((t, TOK_W), F32),
        compiler_params=_cparams(("parallel",)),
    )(xp, w8)


def _conv_bwd(xp, w8, d_f, d_r, kind, rt, name):
    t = xp.shape[0] - 2 * CONV_PAD

    def body(xp_ref, w_ref, df_ref, dr_ref, dx_ref, dw_ref, dpad_ref):
        w = w_ref[...]
        zero = jnp.zeros((CONV_PAD, DN_HD), F32)
        dpad_ref[pl.ds(0, CONV_PAD), :] = zero
        dpad_ref[pl.ds(CONV_PAD + t, CONV_PAD), :] = zero
        dw = [jnp.zeros((1, DN_HD), F32) for _ in range(DN_CONV)]
        for r in range(t // rt):
            rows = pl.ds(r * rt, rt)
            acc = _conv_acc(xp_ref, w, r * rt, rt)
            _, vjp = jax.vjp(functools.partial(_conv_post, kind), acc)
            (dacc,) = vjp(df_ref[rows, :] + dr_ref[rows, :])
            dpad_ref[pl.ds(CONV_PAD + r * rt, rt), :] = dacc
            for i in range(DN_CONV):
                xs = xp_ref[pl.ds(CONV_PAD + r * rt + i - DN_CONV // 2, rt), :]
                dw[i] = dw[i] + jnp.sum(dacc * xs, axis=0, keepdims=True)
        dw_ref[...] = jnp.concatenate(dw + [jnp.zeros((8 - DN_CONV, DN_HD), F32)], axis=0)
        for r in range(t // rt):
            acc = None
            for i in range(DN_CONV):
                term = w[i:i + 1, :] * dpad_ref[pl.ds(CONV_PAD + r * rt - i + DN_CONV // 2, rt), :]
                acc = term if acc is None else acc + term
            dx_ref[pl.ds(r * rt, rt), :] = acc

    return pl.pallas_call(
        body, name=name, grid=(DN_HEADS,),
        in_specs=[pl.BlockSpec((t + 2 * CONV_PAD, DN_HD), lambda j: (0, 6 * kind + j)),
                  pl.BlockSpec((8, DN_HD), lambda j: (0, 6 * kind + j)),
                  pl.BlockSpec((t, DN_HD), lambda j: (0, j)),
                  pl.BlockSpec((t, DN_HD), lambda j: (0, j))],
        out_specs=[pl.BlockSpec((t, DN_HD), lambda j: (0, j)), pl.BlockSpec((8, DN_HD), lambda j: (0, j))],
        out_shape=[jax.ShapeDtypeStruct((t, TOK_W), F32), jax.ShapeDtypeStruct((8, TOK_W), F32)],
        scratch_shapes=[pltpu.VMEM((t + 2 * CONV_PAD, DN_HD), F32)],
        compiler_params=_cparams(("parallel",)),
    )(xp, w8, d_f, d_r)


def _softplus(x):
    e = jnp.exp(-jnp.abs(x))
    return jnp.maximum(x, 0.0) + jnp.where(e < 1e-4, e - 0.5 * e * e, jnp.log(1.0 + e))


_NN = (((2,), (1,)), ((0,), (0,)))
_NT = (((2,), (2,)), ((0,), (0,)))
_TN = (((1,), (1,)), ((0,), (0,)))


def _dot(a, b, dims=_NN):
    return lax.dot_general(a.astype(BF16), b.astype(BF16), dims, preferred_element_type=F32)


def _hi_lo(x):
    hi = x.astype(BF16)
    return hi, (x - hi.astype(F32)).astype(BF16)


def _mask_dot(mask_bf16, x, dims):
    x1 = x.astype(BF16)
    r = x - x1.astype(F32)
    x2, x3 = _hi_lo(r)
    d = functools.partial(lax.dot_general, dimension_numbers=dims, preferred_element_type=F32)
    return d(mask_bf16, x1) + d(mask_bf16, x2) + d(mask_bf16, x3)


@jax.custom_vjp
def _dot_mask(mask_bf16, x):
    return _mask_dot(mask_bf16, x, _NN)


def _dot_mask_fwd(mask_bf16, x):
    return _mask_dot(mask_bf16, x, _NN), mask_bf16


def _dot_mask_bwd(mask_bf16, ct):
    return jnp.zeros_like(mask_bf16), _mask_dot(mask_bf16, ct, _TN)


_dot_mask.defvjp(_dot_mask_fwd, _dot_mask_bwd)


def _dot3_raw(a, b, dims):
    a1, a2 = _hi_lo(a)
    b1, b2 = _hi_lo(b)
    d = functools.partial(lax.dot_general, dimension_numbers=dims, preferred_element_type=F32)
    return d(a1, b1) + d(a1, b2) + d(a2, b1)


def _unit_solve_pass(lmat, rhs):
    xp = -lmat
    powers = [xp]
    sol = rhs + _dot3_raw(xp, rhs, _NN)
    for _ in range(5):
        xp = _dot3_raw(xp, xp, _NN)
        powers.append(xp)
        sol = sol + _dot3_raw(xp, sol, _NN)
    return sol, powers


@jax.custom_vjp
def _unit_solve(lmat, rhs):
    return _unit_solve_pass(lmat, rhs)[0]


def _unit_solve_fwd(lmat, rhs):
    sol, powers = _unit_solve_pass(lmat, rhs)
    return sol, (sol, powers)


def _unit_solve_bwd(res, ct):
    sol, powers = res
    d_rhs = ct
    for xp in powers:
        d_rhs = d_rhs + _dot3_raw(xp, d_rhs, _TN)
    return -_dot3_raw(d_rhs, sol, _NT), d_rhs


_unit_solve.defvjp(_unit_solve_fwd, _unit_solve_bwd)


def _dn_chunk(q, k, v, al, be, alc, a_row, dt_row, a_rowc, dt_rowc, s):
    n, c = q.shape[0], DN_CHUNK
    rev = lax.broadcasted_iota(jnp.int32, (n, c, c), 0) >= n // 2
    row = lax.broadcasted_iota(jnp.int32, (n, c, c), 1)
    col = lax.broadcasted_iota(jnp.int32, (n, c, c), 2)
    ahead = jnp.where(rev, col - row, row - col)
    incl = ahead >= 0
    strict = ahead > 0
    incl_b = incl.astype(BF16)

    g = -jnp.exp(a_row) * _softplus(al + dt_row)
    beta = _sigmoid(be)
    g_c = -jnp.exp(a_rowc) * _softplus(alc + dt_rowc)
    gc = _dot_mask(incl_b, g)
    gcc = _dot_mask(incl_b, g_c)
    decay = jnp.exp(jnp.where(incl, gcc - jnp.swapaxes(gcc, 1, 2), NEG))
    kb = k * beta
    lmat = jnp.where(strict, _dot(kb, k, _NT) * decay, 0.0)
    rhs = jnp.concatenate([v * beta, kb * jnp.exp(gc)], axis=2)
    sol = _unit_solve(lmat, rhs)
    u, w = sol[:, :, :DN_HD], sol[:, :, DN_HD:]
    intra = jnp.where(incl, _dot(q, k, _NT) * decay, 0.0)
    v_new = u - _dot(w, s)
    out = _dot(q * jnp.exp(gc), s) + _dot(intra, v_new)
    g_last = jnp.sum(g, axis=1, keepdims=True)
    s_new = s * jnp.exp(g_last) + _dot(k * jnp.exp(g_last - gc), v_new, _TN)
    return out, s_new


DN_HG = 6


def _dn_load(f_refs, r_refs, alf, bef, alr, ber, a_ref, dt_ref):
    c, hg = DN_CHUNK, DN_HG
    sls = [slice(DN_HD * h, DN_HD * (h + 1)) for h in range(hg)]
    toks = [jnp.stack([f[:, sl] for sl in sls] + [r[:, sl] for sl in sls]) for f, r in zip(f_refs, r_refs)]
    al = jnp.concatenate([alf[...], alr[...]], axis=0)
    be = jnp.concatenate([bef[...], ber[...]], axis=0)
    alc = jnp.concatenate([alf[:, :, 0:c], alr[:, :, 0:c]], axis=0)
    a = jnp.concatenate([a_ref[0], a_ref[1]], axis=0)
    dt = jnp.concatenate([dt_ref[0], dt_ref[1]], axis=0)
    ac = jnp.concatenate([a_ref[0, :, :, 0:c], a_ref[1, :, :, 0:c]], axis=0)
    dtc = jnp.concatenate([dt_ref[0, :, :, 0:c], dt_ref[1, :, :, 0:c]], axis=0)
    return toks, (al, be, alc, a, dt, ac, dtc)


def _dn_views(nc, bwd):
    c, hg = DN_CHUNK, DN_HG
    if bwd:
        f_blk = lambda s: nc - 1 - s
        r_blk = lambda s: s
        st_blk = lambda s: nc - 1 - s
    else:
        f_blk = lambda s: s
        r_blk = lambda s: nc - 1 - s
        st_blk = lambda s: s
    tok_f = pl.BlockSpec((c, hg * DN_HD), lambda g, s: (f_blk(s), g))
    tok_r = pl.BlockSpec((c, hg * DN_HD), lambda g, s: (r_blk(s), g))
    gate_f = pl.BlockSpec((None, hg, c, DN_HD), lambda g, s: (0, g, f_blk(s), 0))
    gate_r = pl.BlockSpec((None, hg, c, DN_HD), lambda g, s: (1, g, r_blk(s), 0))
    par = pl.BlockSpec((2, hg, 1, DN_HD), lambda g, s: (0, g, 0, 0))
    state = pl.BlockSpec((2, hg, None, DN_HD, DN_HD), lambda g, s: (0, g, st_blk(s), 0, 0))
    return tok_f, tok_r, gate_f, gate_r, par, state


def _dn_fwd(q, k, v, al, be, a_rows, dt_rows, name):
    t = q.shape[0]
    c, hg = DN_CHUNK, DN_HG
    nc = t // c
    tok_f, tok_r, gate_f, gate_r, par, state = _dn_views(nc, False)

    def body(qf, kf, vf, qr, kr, vr, alf, bef, alr, ber, a_ref, dt_ref, of_ref, or_ref, st_ref, s_ref):
        @pl.when(pl.program_id(1) == 0)
        def _():
            s_ref[...] = jnp.zeros_like(s_ref)

        (q_, k_, v_), gates = _dn_load((qf, kf, vf), (qr, kr, vr), alf, bef, alr, ber, a_ref, dt_ref)
        s = s_ref[...]
        st_ref[0] = s[:hg]
        st_ref[1] = s[hg:]
        out, s_new = _dn_chunk(q_, k_, v_, *gates, s)
        for h in range(hg):
            sl = slice(DN_HD * h, DN_HD * (h + 1))
            of_ref[:, sl] = out[h]
            or_ref[:, sl] = out[hg + h]
        s_ref[...] = s_new

    return pl.pallas_call(
        body, name=name, grid=(DN_HEADS // hg, nc),
        in_specs=[tok_f] * 3 + [tok_r] * 3 + [gate_f, gate_f, gate_r, gate_r, par, par],
        out_specs=[tok_f, tok_r, state],
        out_shape=[jax.ShapeDtypeStruct((t, TOK_W), F32)] * 2
        + [jax.ShapeDtypeStruct((2, DN_HEADS, nc, DN_HD, DN_HD), F32)],
        scratch_shapes=[pltpu.VMEM((2 * hg, DN_HD, DN_HD), F32)],
        compiler_params=_cparams(("parallel", "arbitrary")),
    )(q, k, v, q, k, v, al, be, al, be, a_rows, dt_rows)


def _dn_bwd(q, k, v, al, be, a_rows, dt_rows, states, do, name):
    t = q.shape[0]
    c, hg = DN_CHUNK, DN_HG
    assert hg == DN_HEADS
    nc = t // c
    tok_f, tok_r, gate_f, gate_r, par, state = _dn_views(nc, True)
    gout_f = pl.BlockSpec((c, DN_HD), lambda g, s: (nc - 1 - s, 0))
    gout_r = pl.BlockSpec((c, DN_HD), lambda g, s: (s, 0))

    def body(qf, kf, vf, qr, kr, vr, alf, bef, alr, ber, a_ref, dt_ref, st_ref, dof, dor,
             dqf, dkf, dvf, dqr, dkr, dvr, dgf, dgr, da_ref, ddt_ref, ds_ref):
        first = pl.program_id(1) == 0

        @pl.when(first)
        def _():
            ds_ref[...] = jnp.zeros_like(ds_ref)
            da_ref[...] = jnp.zeros_like(da_ref)
            ddt_ref[...] = jnp.zeros_like(ddt_ref)

        def lanes(x):
            return jnp.sum(x, axis=-1, keepdims=True)

        (q_, k_, v_, do_), gates = _dn_load((qf, kf, vf, dof), (qr, kr, vr, dor), alf, bef, alr, ber, a_ref, dt_ref)
        s = jnp.concatenate([st_ref[0], st_ref[1]], axis=0)
        _, vjp = jax.vjp(_dn_chunk, q_, k_, v_, *gates, s)
        dq, dk, dv, dal, dbe, dalc, da, ddt, dac, ddtc, ds = vjp((do_, ds_ref[...]))
        for h in range(hg):
            sl = slice(DN_HD * h, DN_HD * (h + 1))
            dqf[:, sl], dkf[:, sl], dvf[:, sl] = dq[h], dk[h], dv[h]
            dqr[:, sl], dkr[:, sl], dvr[:, sl] = dq[hg + h], dk[hg + h], dv[hg + h]
        dal, dbe = lanes(dal) + lanes(dalc), lanes(dbe)
        lane = lax.broadcasted_iota(jnp.int32, (c, DN_HD), 1)
        for d, dg_ref in enumerate((dgf, dgr)):
            dg = jnp.zeros((c, DN_HD), F32)
            for h in range(hg):
                dg = jnp.where(lane == h, dal[d * hg + h], jnp.where(lane == hg + h, dbe[d * hg + h], dg))
            dg_ref[...] = dg
        da = jnp.broadcast_to(lanes(da) + lanes(dac), da.shape)
        ddt = jnp.broadcast_to(lanes(ddt) + lanes(ddtc), ddt.shape)
        da_ref[0] += da[:hg]
        da_ref[1] += da[hg:]
        ddt_ref[0] += ddt[:hg]
        ddt_ref[1] += ddt[hg:]
        ds_ref[...] = ds

    tok = jax.ShapeDtypeStruct((t, TOK_W), F32)
    gate = jax.ShapeDtypeStruct((t, DN_HD), F32)
    parsh = jax.ShapeDtypeStruct((2, DN_HEADS, 1, DN_HD), F32)
    res = pl.pallas_call(
        body, name=name, grid=(DN_HEADS // hg, nc),
        in_specs=[tok_f] * 3 + [tok_r] * 3 + [gate_f, gate_f, gate_r, gate_r, par, par, state, tok_f, tok_r],
        out_specs=[tok_f] * 3 + [tok_r] * 3 + [gout_f, gout_r, par, par],
        out_shape=[tok] * 6 + [gate] * 2 + [parsh] * 2,
        scratch_shapes=[pltpu.VMEM((2 * hg, DN_HD, DN_HD), F32)],
        compiler_params=_cparams(("parallel", "arbitrary")),
    )(q, k, v, q, k, v, al, be, al, be, a_rows, dt_rows, states, do, do)
    dqf, dkf, dvf, dqr, dkr, dvr, dgf, dgr, da, ddt = res
    dgate = jnp.concatenate([dgf[:, :2 * DN_HEADS], dgr[:, :2 * DN_HEADS]], axis=1)
    return (dqf, dkf, dvf), (dqr, dkr, dvr), dgate, da, ddt


BAND_BQ = 256
ROW_TB = 256
MEM_TB = 512
CONV_RT = 512


def _to_sub(x, dil):
    l = x.shape[0] // dil
    return x.reshape(l, dil, 4, ATT_HD).transpose(2, 1, 0, 3).reshape(4 * dil, l, ATT_HD)


def _from_sub(x, dil):
    l = x.shape[1]
    return x.reshape(4, dil, l, ATT_HD).transpose(2, 1, 0, 3).reshape(l * dil, 4 * ATT_HD)


def _heads_major(x):
    return x.reshape(x.shape[0], MEM_HEADS, ATT_HD).transpose(1, 0, 2)


def _heads_minor(x):
    return x.transpose(1, 0, 2).reshape(x.shape[1], MEM_HEADS * ATT_HD)


def _mem_kv_fwd(mem, gain, w_kv, li):
    (memn,) = _rowwise(_fn_pre, [mem], [gain], [(D, BF16)], mem.shape[0], f"memnorm_fwd{li}")
    kv = _matmul(memn, w_kv, "nn", F32, f"memkv_fwd{li}")
    return _heads_major(kv[:, :MEM_W]), _heads_major(kv[:, MEM_W:]), memn


def _mem_kv_bwd(mem, gain, w_kv, memn, dkm, dvm, li):
    dkv = jnp.concatenate([_heads_minor(dkm), _heads_minor(dvm)], axis=1).astype(BF16)
    dw = _matmul(memn, dkv, "tn", BF16, f"memkv_dw{li}")
    dmemn = _matmul(dkv, w_kv, "nt", F32, f"memkv_dx{li}")
    _, (dgain,) = _rowwise_bwd(_fn_pre, [mem], [gain], [dmemn], [None], mem.shape[0], f"memnorm_bwd{li}")
    return dw, dgain


def _attn_mixer_fwd(p, rel_bias, kv_fn):
    t = p.shape[0]
    saved, outs, lses = [], [], []
    for gi, (_, dil) in enumerate(DIL_GROUPS):
        l_sub = t // dil
        bq = min(BAND_BQ, l_sub)
        q = _to_sub(p[:, 256 * gi:256 * (gi + 1)], dil)
        pad = ((0, 0), (BAND_HALF, BAND_HALF), (0, 0))
        k = jnp.pad(_to_sub(p[:, TOK_W + 256 * gi:TOK_W + 256 * (gi + 1)], dil), pad)
        v = jnp.pad(_to_sub(p[:, 2 * TOK_W + 256 * gi:2 * TOK_W + 256 * (gi + 1)], dil), pad)
        bm = _band_bias(rel_bias, gi, dil, bq)
        o, lse = _band_fwd(q, k, v, bm, dil, l_sub, bq, f"band_fwd{gi}")
        outs.append(_from_sub(o, dil))
        lses.append(_from_sub(lse, dil))
        saved.append((q, k, v, bm))
    o_all = jnp.concatenate(outs, axis=1)
    lse_all = jnp.concatenate(lses, axis=1)
    (mixed,) = _rowwise(_fn_combine, [o_all, lse_all], [], [(TOK_W, BF16)], ROW_TB, "combine_fwd")
    qm = _heads_major(p[:, 3 * TOK_W:])
    km, vm, memn = kv_fn(mixed)
    memo = _mem_fwd(qm, km, vm, min(MEM_TB, t), "mem_fwd0")
    cat = jnp.concatenate([mixed, _heads_minor(memo).astype(BF16)], axis=1)
    return cat, (saved, o_all, lse_all, qm), (km, vm, memn)


def _attn_mixer_bwd(dcat, res, km, vm):
    saved, o_all, lse_all, qm = res
    t = dcat.shape[0]
    (do_all, dlse_all), _ = _rowwise_bwd(_fn_combine, [o_all, lse_all], [], [dcat[:, :TOK_W]], [F32, F32],
                                         ROW_TB, "combine_bwd")
    dqs, dks, dvs, dbms = [], [], [], []
    for gi, (_, dil) in enumerate(DIL_GROUPS):
        l_sub = t // dil
        bq = min(BAND_BQ, l_sub)
        q, k, v, bm = saved[gi]
        do = _to_sub(do_all[:, 256 * gi:256 * (gi + 1)], dil)
        dl = _to_sub(dlse_all[:, 256 * gi:256 * (gi + 1)], dil)
        dq, dk, dv, dbm = _band_bwd(q, k, v, bm, do, dl, dil, l_sub, bq, f"band_bwd{gi}")
        dqs.append(_from_sub(dq, dil))
        dks.append(_from_sub(dk[:, BAND_HALF:-BAND_HALF], dil))
        dvs.append(_from_sub(dv[:, BAND_HALF:-BAND_HALF], dil))
        dbms.append(dbm)
    dqm, dkm, dvm = _mem_bwd(qm, km, vm, _heads_major(dcat[:, TOK_W:]), min(MEM_TB, t), "mem_bwd0")
    dp = jnp.concatenate(dqs + dks + dvs + [_heads_minor(dqm)], axis=1).astype(BF16)
    return dp, _relbias_grad(dbms, "relbias_grad"), dkm, dvm


def _dn_mixer_fwd(p, conv_w, a_log, dt_bias, out_norm, km, vm):
    t = p.shape[0]
    rt = min(CONV_RT, t)
    xp = jnp.pad(p[:, :3 * TOK_W], ((CONV_PAD, CONV_PAD), (0, 0)))
    w8 = jnp.pad(conv_w, ((0, 8 - DN_CONV), (0, 0)))
    q = _conv_fwd(xp, w8, 0, rt, "conv_fwd_q")
    k = _conv_fwd(xp, w8, 1, rt, "conv_fwd_k")
    v = _conv_fwd(xp, w8, 2, rt, "conv_fwd_v")
    gate = p[:, 4 * TOK_W:4 * TOK_W + 4 * DN_HEADS].reshape(t, 2, 2, DN_HEADS)
    bshape = (2, DN_HEADS, t, DN_HD)
    al = jnp.broadcast_to(gate[:, :, 0, :].transpose(1, 2, 0)[..., None], bshape)
    be = jnp.broadcast_to(gate[:, :, 1, :].transpose(1, 2, 0)[..., None], bshape)
    a_rows = jnp.broadcast_to(a_log[:, :, None, None], (2, DN_HEADS, 1, DN_HD))
    dt_rows = jnp.broadcast_to(dt_bias[:, :, None, None], (2, DN_HEADS, 1, DN_HD))
    o_f, o_r, states = _dn_fwd(q, k, v, al, be, a_rows, dt_rows, "dn_fwd")
    z = p[:, 3 * TOK_W:4 * TOK_W]
    gain = out_norm.reshape(1, DN_HD)
    (og,) = _rowwise(_fn_outnorm, [o_f, o_r, z], [gain], [(TOK_W, BF16)], ROW_TB, "outnorm_fwd")
    qm = _heads_major(p[:, 4 * TOK_W + 4 * DN_HEADS:DN_IN])
    memo = _mem_fwd(qm, km, vm, min(MEM_TB, t), "mem_fwd1")
    cat = jnp.concatenate([og, _heads_minor(memo).astype(BF16)], axis=1)
    return cat, (xp, w8, q, k, v, al, be, a_rows, dt_rows, o_f, o_r, states, z, gain, qm)


def _dn_mixer_bwd(dcat, res, km, vm):
    xp, w8, q, k, v, al, be, a_rows, dt_rows, o_f, o_r, states, z, gain, qm = res
    t = dcat.shape[0]
    rt = min(CONV_RT, t)
    (do, dz), (dgain,) = _rowwise_bwd(_fn_outnorm, [o_f, o_r, z], [gain], [dcat[:, :TOK_W]], [F32, None, F32],
                                      ROW_TB, "outnorm_bwd")
    d_f, d_r, dgate, da, ddt = _dn_bwd(q, k, v, al, be, a_rows, dt_rows, states, do, "dn_bwd")
    dxs, dws = [], []
    for kind, nm in enumerate("qkv"):
        dx, dw = _conv_bwd(xp, w8, d_f[kind], d_r[kind], kind, rt, f"conv_bwd_{nm}")
        dxs.append(dx)
        dws.append(dw)
    dconv = jnp.concatenate(dws, axis=1)[:DN_CONV]
    dqm, dkm, dvm = _mem_bwd(qm, km, vm, _heads_major(dcat[:, TOK_W:]), min(MEM_TB, t), "mem_bwd1")
    dp = jnp.concatenate(dxs + [dz, dgate, _heads_minor(dqm), jnp.zeros((t, DN_IN_PAD - DN_IN), F32)],
                         axis=1).astype(BF16)
    return dp, dconv, da[:, :, 0, 0], ddt[:, :, 0, 0], dgain.reshape(DN_HD), dkm, dvm


SWI_TB = 256


def _ffn_fwd(h, w_gu_t, w_d, li):
    gu = _matmul(h, w_gu_t, "nt", BF16, f"ffn_gu{li}")
    (a,) = _rowwise(_fn_swiglu, [gu], [], [(D_FF, BF16)], SWI_TB, f"swiglu_fwd{li}")
    return _matmul(a, w_d, "nn", F32, f"ffn_down{li}"), gu, a


def _ffn_bwd(df, h, w_gu_t, w_d, gu, a, li):
    da = _matmul(df, w_d, "nt", BF16, f"ffn_down_dx{li}")
    dwd = _matmul(a, df, "tn", BF16, f"ffn_down_dw{li}")
    (dgu,), _ = _rowwise_bwd(_fn_swiglu, [gu], [], [da], [BF16], SWI_TB, f"swiglu_bwd{li}")
    dh = _matmul(dgu, w_gu_t, "nn", F32, f"ffn_gu_dx{li}")
    dwgu_t = _matmul(dgu, h, "tn", BF16, f"ffn_gu_dw{li}")
    return dh, dwgu_t, dwd


def _fn_first(x, g):
    return x, _rms(x, g)


def _me_xyc():
    return lax.axis_index("x"), lax.axis_index("y"), lax.axis_index("c")


def _flip(coords, k):
    x, y, c = coords
    return (1 - x if k & 4 else x, 1 - y if k & 2 else y, 1 - c if k & 1 else c)


def _index(coords):
    x, y, c = coords
    return 4 * x + 2 * y + c


def _window(ref, axis, size, d):
    idx = [slice(None)] * len(ref.shape)
    idx[axis] = pl.ds(pl.multiple_of(d * size, size), size)
    return ref.at[tuple(idx)]


def _comm_call(body, n, ins, out_shapes, name):
    hbm = pl.BlockSpec(memory_space=pl.ANY)
    return pl.pallas_call(
        body, name=name, in_specs=[hbm] * n, out_specs=[hbm] * n, out_shape=out_shapes,
        scratch_shapes=[pltpu.SemaphoreType.DMA((N_DEV - 1, n)), pltpu.SemaphoreType.DMA((N_DEV - 1, n)),
                        pltpu.SemaphoreType.DMA((n,))],
    )(*ins)


def _run_exchange(n, local, remote, send_sems, recv_sems):
    me = _me_xyc()
    locs = [local(p) for p in range(n)]
    for cp in locs:
        cp.start()
    sends = [remote(k, p, me, _flip(me, k)) for k in range(1, N_DEV) for p in range(n)]
    for cp in sends:
        cp.start()
    for k in range(1, N_DEV):
        for p in range(n):
            remote(k, p, _flip(me, k), me).wait_recv()
    for cp in sends:
        cp.wait_send()
    for cp in locs:
        cp.wait()


def _all_gather(shards, axes, name):
    n = len(shards)
    sizes = [s.shape[a] for s, a in zip(shards, axes)]

    def body(*refs):
        ins, outs = refs[:n], refs[n:2 * n]
        send_sems, recv_sems, loc_sems = refs[2 * n:]
        me = _me_xyc()

        def local(p):
            return pltpu.make_async_copy(ins[p], _window(outs[p], axes[p], sizes[p], _index(me)), loc_sems.at[p])

        def remote(k, p, owner, to):
            return pltpu.make_async_remote_copy(
                src_ref=ins[p], dst_ref=_window(outs[p], axes[p], sizes[p], _index(owner)),
                send_sem=send_sems.at[k - 1, p], recv_sem=recv_sems.at[k - 1, p], device_id=to, device_id_type=MESH)

        _run_exchange(n, local, remote, send_sems, recv_sems)

    def full(s, a):
        return s.shape[:a] + (N_DEV * s.shape[a],) + s.shape[a + 1:]

    return _comm_call(body, n, shards, [jax.ShapeDtypeStruct(full(s, a), s.dtype) for s, a in zip(shards, axes)], name)


def _exchange(fulls, axes, name):
    n = len(fulls)
    sizes = [None if a is None else f.shape[a] // N_DEV for f, a in zip(fulls, axes)]

    def part_shape(f, a):
        return f.shape if a is None else f.shape[:a] + (f.shape[a] // N_DEV,) + f.shape[a + 1:]

    def body(*refs):
        ins, outs = refs[:n], refs[n:2 * n]
        send_sems, recv_sems, loc_sems = refs[2 * n:]
        me = _me_xyc()

        def src(p, to):
            return ins[p] if axes[p] is None else _window(ins[p], axes[p], sizes[p], _index(to))

        def local(p):
            return pltpu.make_async_copy(src(p, me), outs[p].at[_index(me)], loc_sems.at[p])

        def remote(k, p, sender, to):
            return pltpu.make_async_remote_copy(
                src_ref=src(p, to), dst_ref=outs[p].at[_index(sender)],
                send_sem=send_sems.at[k - 1, p], recv_sem=recv_sems.at[k - 1, p], device_id=to, device_id_type=MESH)

        _run_exchange(n, local, remote, send_sems, recv_sems)

    return _comm_call(body, n, fulls,
                      [jax.ShapeDtypeStruct((N_DEV,) + part_shape(f, a), f.dtype) for f, a in zip(fulls, axes)], name)


_HBM = pl.BlockSpec(memory_space=pltpu.HBM)
_SEM = pl.BlockSpec(memory_space=pltpu.SEMAPHORE)
_EFFECT = pltpu.SideEffectType.DATAFLOW_SIDE_EFFECTING


def _in_hbm(a):
    return pltpu.with_memory_space_constraint(a, pltpu.HBM)


def _split_start(srcs, lands, after, descr, name):
    n = len(srcs)

    def body(*refs):
        ins, lnd = refs[:n], refs[n:2 * n]
        send_sems, recv_sems = refs[2 * n + 1], refs[2 * n + 2]
        token = refs[-1]
        me = _me_xyc()
        for k in range(1, N_DEV):
            for p in range(n):
                descr(k, p, ins, lnd, send_sems, recv_sems, me, _flip(me, k)).start()
        token[...] = jnp.zeros_like(token)

    sems = pltpu.SemaphoreType.DMA(((N_DEV - 1) * n,))
    res = pl.pallas_call(
        body, name=name,
        out_shape=(sems, sems, *[pltpu.HBM(a.shape, a.dtype) for a in (*srcs, *lands)],
                   jax.ShapeDtypeStruct((8, 128), F32)),
        in_specs=[_HBM] * (2 * n) + [pl.BlockSpec(memory_space=pl.ANY)],
        out_specs=(_SEM, _SEM, *[_HBM] * (2 * n), pl.BlockSpec(memory_space=pltpu.VMEM)),
        input_output_aliases={i: 2 + i for i in range(2 * n)},
        compiler_params=pltpu.CompilerParams(has_side_effects=_EFFECT),
    )(*[_in_hbm(a) for a in (*srcs, *lands)], after)
    return res[0], res[1], res[2:2 + n], res[2 + n:2 + 2 * n], res[-1]


def _split_wait(send_sems, recv_sems, srcs, lands, after, descr, name):
    n = len(srcs)

    def body(*refs):
        ins, lnd = refs[:n], refs[n:2 * n]
        s_sems, r_sems = refs[2 * n], refs[2 * n + 1]
        me = _me_xyc()
        for k in range(1, N_DEV):
            for p in range(n):
                peer = _flip(me, k)
                descr(k, p, ins, lnd, s_sems, r_sems, me, peer).wait_send()
                descr(k, p, ins, lnd, s_sems, r_sems, peer, me).wait_recv()

    res = pl.pallas_call(
        body, name=name,
        out_shape=tuple(pltpu.HBM(a.shape, a.dtype) for a in (*srcs, *lands)),
        in_specs=[_HBM] * (2 * n) + [_SEM, _SEM, pl.BlockSpec(memory_space=pl.ANY)],
        out_specs=tuple([_HBM] * (2 * n)),
        input_output_aliases={i: i for i in range(2 * n)},
        compiler_params=pltpu.CompilerParams(has_side_effects=_EFFECT),
    )(*srcs, *lands, send_sems, recv_sems, after)
    return list(res[n:])


def _gather_descr(axes, sizes):
    def descr(k, p, ins, lnd, send_sems, recv_sems, sender, dest):
        return pltpu.make_async_remote_copy(
            src_ref=ins[p], dst_ref=_window(lnd[p], axes[p], sizes[p], _index(sender)),
            send_sem=send_sems.at[(k - 1) * len(axes) + p], recv_sem=recv_sems.at[(k - 1) * len(axes) + p],
            device_id=dest, device_id_type=MESH)
    return descr


def _exchange_descr(axes, sizes):
    def descr(k, p, ins, lnd, send_sems, recv_sems, sender, dest):
        return pltpu.make_async_remote_copy(
            src_ref=_window(ins[p], axes[p], sizes[p], _index(dest)), dst_ref=lnd[p].at[_index(sender)],
            send_sem=send_sems.at[(k - 1) * len(axes) + p], recv_sem=recv_sems.at[(k - 1) * len(axes) + p],
            device_id=dest, device_id_type=MESH)
    return descr


def _gather_begin(shards, axes, after, name):
    sizes = [s.shape[a] for s, a in zip(shards, axes)]
    me = _index(_me_xyc())
    lands = []
    for s, a, sz in zip(shards, axes, sizes):
        full = s.shape[:a] + (N_DEV * sz,) + s.shape[a + 1:]
        lands.append(lax.dynamic_update_slice_in_dim(lax.empty(full, s.dtype), s, me * sz, a))
    descr = _gather_descr(axes, sizes)
    send_sems, recv_sems, srcs, lands, token = _split_start(shards, lands, after, descr, name)
    return (send_sems, recv_sems, srcs, lands, descr), token


def _exchange_begin(fulls, axes, after, name):
    sizes = [f.shape[a] // N_DEV for f, a in zip(fulls, axes)]
    me = _index(_me_xyc())
    lands = []
    for f, a, sz in zip(fulls, axes, sizes):
        own = lax.dynamic_slice_in_dim(f, me * sz, sz, a)
        lands.append(lax.dynamic_update_slice_in_dim(lax.empty((N_DEV,) + own.shape, f.dtype), own[None], me, 0))
    descr = _exchange_descr(axes, sizes)
    send_sems, recv_sems, srcs, lands, token = _split_start(fulls, lands, after, descr, name)
    return (send_sems, recv_sems, srcs, lands, descr), token


def _split_end(handle, after, name):
    send_sems, recv_sems, srcs, lands, descr = handle
    return _split_wait(send_sems, recv_sems, srcs, lands, after, descr, name)


def _adam_math(g, w, m, v):
    m = ADAM_B1 * m + (1.0 - ADAM_B1) * g
    v = ADAM_B2 * v + (1.0 - ADAM_B2) * (g * g)
    m_hat = m / (1.0 - ADAM_B1 ** ADAM_STEP)
    v_hat = v / (1.0 - ADAM_B2 ** ADAM_STEP)
    delta = -ADAM_LR * (m_hat / (jnp.sqrt(v_hat) + ADAM_EPS) + ADAM_WD * w)
    return delta, m, v


def _sum_slabs(r_ref):
    g = r_ref[0].astype(F32)
    for s in range(1, N_DEV):
        g = g + r_ref[s].astype(F32)
    return g


def _adamw_reduce(recv, w, m, v, tb, name):
    r, c = w.shape

    def body(r_ref, w_ref, m_ref, v_ref, g_ref, d_ref, nm_ref, nv_ref):
        g = _sum_slabs(r_ref)
        g_ref[...] = g
        d_ref[...], nm_ref[...], nv_ref[...] = _adam_math(g, w_ref[...], m_ref[...], v_ref[...])

    blk = pl.BlockSpec((tb, c), lambda i: (i, 0))
    return pl.pallas_call(
        body, name=name, grid=(r // tb,),
        in_specs=[pl.BlockSpec((N_DEV, tb, c), lambda i: (0, i, 0)), blk, blk, blk],
        out_specs=[blk] * 4, out_shape=[jax.ShapeDtypeStruct((r, c), F32)] * 4,
        compiler_params=_cparams(("parallel",)),
    )(recv, w, m, v)


def _reduce8(recv, tb, name):
    r, c = recv.shape[1:]

    def body(r_ref, g_ref):
        g_ref[...] = _sum_slabs(r_ref)

    return pl.pallas_call(
        body, name=name, grid=(r // tb,),
        in_specs=[pl.BlockSpec((N_DEV, tb, c), lambda i: (0, i, 0))],
        out_specs=pl.BlockSpec((tb, c), lambda i: (i, 0)), out_shape=jax.ShapeDtypeStruct((r, c), F32),
        compiler_params=_cparams(("parallel",)),
    )(recv)


def _adamw(g, w, m, v, tb, name):
    r, c = w.shape

    def body(g_ref, w_ref, m_ref, v_ref, d_ref, nm_ref, nv_ref):
        d_ref[...], nm_ref[...], nv_ref[...] = _adam_math(g_ref[...], w_ref[...], m_ref[...], v_ref[...])

    blk = pl.BlockSpec((tb, c), lambda i: (i, 0))
    return pl.pallas_call(
        body, name=name, grid=(r // tb,), in_specs=[blk] * 4, out_specs=[blk] * 3,
        out_shape=[jax.ShapeDtypeStruct((r, c), F32)] * 3, compiler_params=_cparams(("parallel",)),
    )(g, w, m, v)


DN_IN_SHARD = DN_IN // N_DEV
DN_IN_SHARD_PAD = 432
CONV_SHARD = (1, DN_CONV, 288)


def _pack_small(arrs, rows):
    flat = jnp.concatenate([a.astype(F32).reshape(-1) for a in arrs])
    return jnp.pad(flat, (0, rows * PACK_C - flat.size)).reshape(rows, PACK_C)


def _unpack_small(packed, shapes):
    flat, out, off = packed.reshape(-1), [], 0
    for shp in shapes:
        n = int(np.prod(shp))
        out.append(flat[off:off + n].reshape(shp))
        off += n
    return out


def kernel(x, mem, rel_bias, att_w_in, att_w_out, dn_w_in, dn_conv, dn_a_log, dn_dt_bias, dn_out_norm, dn_w_out, mem_norm, mem_w_kv, norm_mix_pre, norm_mix_post, norm_ffn_pre, norm_ffn_post, ffn_w_gate_up, ffn_w_down, loss_target, m_rel_bias, m_att_w_in, m_att_w_out, m_dn_w_in, m_dn_conv, m_dn_a_log, m_dn_dt_bias, m_dn_out_norm, m_dn_w_out, m_mem_norm, m_mem_w_kv, m_norm_mix_pre, m_norm_mix_post, m_norm_ffn_pre, m_norm_ffn_post, m_ffn_w_gate_up, m_ffn_w_down, v_rel_bias, v_att_w_in, v_att_w_out, v_dn_w_in, v_dn_conv, v_dn_a_log, v_dn_dt_bias, v_dn_out_norm, v_dn_w_out, v_mem_norm, v_mem_w_kv, v_norm_mix_pre, v_norm_mix_post, v_norm_ffn_pre, v_norm_ffn_post, v_ffn_w_gate_up, v_ffn_w_down):
    x0, mem0, tgt = x[0], mem[0], loss_target[0]
    t = x0.shape[0]
    axes = ("x", "y", "c")

    def t_shard(w):
        return jnp.swapaxes(w, 1, 2).astype(BF16)

    dn_in_pad = ((0, 0), (0, DN_IN_SHARD_PAD - DN_IN_SHARD), (0, 0))
    (w_att_in_t,) = _all_gather([t_shard(att_w_in)], [1], "allgather_first")
    w_att_in_t = w_att_in_t[0]
    gu_t, down = t_shard(ffn_w_gate_up), ffn_w_down.astype(BF16)
    gather_o, tok_o = _gather_begin([att_w_out.astype(BF16), mem_w_kv.astype(BF16)], [1, 1], w_att_in_t,
                                    "gather_att_out_start")
    gather_a, tok_a = _gather_begin([gu_t[0:1], down[0:1]], [1, 1], tok_o, "gather_ffn0_start")
    gather_b, tok_b = _gather_begin(
        [jnp.pad(t_shard(dn_w_in), dn_in_pad), dn_w_out.astype(BF16), gu_t[1:2], down[1:2], dn_conv],
        [1, 1, 1, 1, 0], tok_a, "gather_layer1_start")

    def gain(a, i):
        return a[i].reshape(1, D)

    (h0,) = _rowwise(_fn_pre, [x0], [gain(norm_mix_pre, 0) + tok_b[0:1, 0:1]], [(D, BF16)], ROW_TB, "pre0")
    p0 = _matmul(h0, w_att_in_t, "nt", BF16, "att_in")
    late = {}

    def kv0(after):
        late["w_att_out"], late["w_kv"] = _split_end(gather_o, after, "gather_att_out_wait")
        return _mem_kv_fwd(mem0, gain(mem_norm, 0), late["w_kv"][0], 0)

    cat0, res0, (km0, vm0, memn0) = _attn_mixer_fwd(p0, rel_bias, kv0)
    w_att_out, w_kv = late["w_att_out"][0], late["w_kv"]
    y0 = _matmul(cat0, w_att_out, "nn", F32, "att_out")
    g_a = [gain(norm_mix_post, 0), gain(norm_ffn_pre, 0)]
    x1, h1 = _rowwise(_fn_res_pre, [x0, y0], g_a, [(D, F32), (D, BF16)], ROW_TB, "res_pre0")
    w_gu_t0, w_down0 = [w[0] for w in _split_end(gather_a, h1, "gather_ffn0_wait")]
    f0, gu0, a0 = _ffn_fwd(h1, w_gu_t0, w_down0, 0)
    g_b = [gain(norm_ffn_post, 0), gain(norm_mix_pre, 1)]
    x2, h2 = _rowwise(_fn_res_pre, [x1, f0], g_b, [(D, F32), (D, BF16)], ROW_TB, "res_pre1")
    km1, vm1, memn1 = _mem_kv_fwd(mem0, gain(mem_norm, 1), w_kv[1], 1)
    w_dn_in_g, w_dn_out, w_gu_t1, w_down1, conv_g = _split_end(gather_b, h2, "gather_layer1_wait")
    w_dn_in_g, w_dn_out, w_gu_t1, w_down1 = w_dn_in_g[0], w_dn_out[0], w_gu_t1[0], w_down1[0]
    conv_full = conv_g.transpose(1, 0, 2).reshape(DN_CONV, 3 * TOK_W)
    w_dn_in_t = jnp.concatenate(
        [w_dn_in_g[DN_IN_SHARD_PAD * j:DN_IN_SHARD_PAD * j + DN_IN_SHARD] for j in range(N_DEV)]
        + [jnp.zeros((DN_IN_PAD - DN_IN, D), BF16)], axis=0)
    p1 = _matmul(h2, w_dn_in_t, "nt", F32, "dn_in")
    cat1, res1 = _dn_mixer_fwd(p1, conv_full, dn_a_log[0], dn_dt_bias[0], dn_out_norm[0], km1, vm1)
    y1 = _matmul(cat1, w_dn_out, "nn", F32, "dn_out")
    g_c = [gain(norm_mix_post, 1), gain(norm_ffn_pre, 1)]
    x3, h3 = _rowwise(_fn_res_pre, [x2, y1], g_c, [(D, F32), (D, BF16)], ROW_TB, "res_pre2")
    f1, gu1, a1 = _ffn_fwd(h3, w_gu_t1, w_down1, 1)
    g_d = [gain(norm_ffn_post, 1)]
    (x4,) = _rowwise(_fn_res, [x3, f1], g_d, [(D, F32)], ROW_TB, "res3")
    dx4, lrow = _loss_kernel(x4, tgt, ROW_TB, "loss")
    loss = lax.psum(lrow[0, 0] * (0.5 / D), axes)

    (df1,), (dg_fpost1,) = _rowwise_bwd(_fn_res, [x3, f1], g_d, [dx4], [None, BF16], ROW_TB, "res3_bwd")
    dh3, dwgu1, dwd1 = _ffn_bwd(df1, h3, w_gu_t1, w_down1, gu1, a1, 1)
    (dx2, dy1), (dg_mpost1, dg_fpre1) = _rowwise_bwd(_fn_res_pre, [x2, y1], g_c, [dx4, dh3], [F32, BF16],
                                                     ROW_TB, "res_pre2_bwd")
    dcat1 = _matmul(dy1, w_dn_out, "nt", F32, "dn_out_dx")
    dw_dn_out = _matmul(cat1, dy1, "tn", BF16, "dn_out_dw")
    dp1, dconv, da_log, ddt_bias, dout_norm, dkm1, dvm1 = _dn_mixer_bwd(dcat1, res1, km1, vm1)
    dwkv1, dg_mem1 = _mem_kv_bwd(mem0, gain(mem_norm, 1), w_kv[1], memn1, dkm1, dvm1, 1)
    dh2 = _matmul(dp1, w_dn_in_t, "nn", F32, "dn_in_dx")
    dw_dn_in_t = _matmul(dp1, h2, "tn", BF16, "dn_in_dw")
    dn_in_parts = [jnp.pad(dw_dn_in_t[DN_IN_SHARD * j:DN_IN_SHARD * (j + 1)],
                           ((0, DN_IN_SHARD_PAD - DN_IN_SHARD), (0, 0))) for j in range(N_DEV)]
    xch_b, tok = _exchange_begin(
        [jnp.concatenate(dn_in_parts, axis=0)[None], dw_dn_out[None], dwkv1[None], dwgu1[None], dwd1[None]],
        [1, 1, 1, 1, 1], dh2, "exchange_layer1_start")
    (dx1, df0), (dg_fpost0, dg_mpre1) = _rowwise_bwd(_fn_res_pre, [x1, f0], [g + tok[0:1, 0:1] for g in g_b],
                                                     [dx2, dh2], [F32, BF16], ROW_TB, "res_pre1_bwd")
    dh1, dwgu0, dwd0 = _ffn_bwd(df0, h1, w_gu_t0, w_down0, gu0, a0, 0)
    xch_a, tok = _exchange_begin([dwgu0[None], dwd0[None]], [1, 1], dh1, "exchange_ffn0_start")
    (dx0, dy0), (dg_mpost0, dg_fpre0) = _rowwise_bwd(_fn_res_pre, [x0, y0], [g + tok[0:1, 0:1] for g in g_a],
                                                     [dx1, dh1], [F32, BF16], ROW_TB, "res_pre0_bwd")
    dcat0 = _matmul(dy0, w_att_out, "nt", F32, "att_out_dx")
    dw_att_out = _matmul(cat0, dy0, "tn", BF16, "att_out_dw")
    dp0, drel, dkm0, dvm0 = _attn_mixer_bwd(dcat0, res0, km0, vm0)
    dwkv0, dg_mem0 = _mem_kv_bwd(mem0, gain(mem_norm, 0), w_kv[0], memn0, dkm0, dvm0, 0)
    xch_o, tok = _exchange_begin([dw_att_out[None], dwkv0[None]], [1, 1], dp0, "exchange_att_out_start")
    dh0 = _matmul(dp0, w_att_in_t, "nn", F32, "att_in_dx")
    dw_att_in_t = _matmul(dp0, h0, "tn", BF16, "att_in_dw")
    (grad_x,), (dg_mpre0,) = _rowwise_bwd(_fn_first, [x0], [gain(norm_mix_pre, 0) + tok[0:1, 0:1]], [dx0, dh0],
                                          [F32], ROW_TB, "pre0_bwd")

    small_grads = [drel, da_log, ddt_bias, dout_norm, jnp.concatenate([dg_mem0, dg_mem1]),
                   jnp.concatenate([dg_mpre0, dg_mpre1]), jnp.concatenate([dg_mpost0, dg_mpost1]),
                   jnp.concatenate([dg_fpre0, dg_fpre1]), jnp.concatenate([dg_fpost0, dg_fpost1]), dconv]
    r_att_in, r_small = _exchange([dw_att_in_t[None], _pack_small(small_grads, SMALL_ROWS)], [1, None],
                                  "exchange_last")
    r_att_out, r_kv0 = _split_end(xch_o, r_small, "exchange_att_out_wait")
    r_gu0, r_down0 = _split_end(xch_a, r_small, "exchange_ffn0_wait")
    r_dn_in, r_dn_out, r_kv1, r_gu1, r_down1 = _split_end(xch_b, r_small, "exchange_layer1_wait")

    def rows(a):
        return a.reshape((-1,) + a.shape[-1:])

    def row_sharded(recv, w, m, v, tb, name):
        outs = _adamw_reduce(recv.reshape((N_DEV, -1) + recv.shape[-1:]), rows(w), rows(m), rows(v), tb, name)
        return [o.reshape(w.shape) for o in outs]

    def col_sharded(recv, w, m, v, tb, name):
        g_t = _reduce8(recv.reshape((N_DEV, -1) + recv.shape[-1:]), tb, name + "_sum")
        g = jnp.swapaxes(g_t.reshape(recv.shape[1:])[:, :w.shape[2]], 1, 2)
        outs = _adamw(rows(g), rows(w), rows(m), rows(v), 256, name)
        return [g] + [o.reshape(w.shape) for o in outs]

    def per_layer(fn, recvs, w, m, v, tb, name):
        outs = [fn(r, w[l:l + 1], m[l:l + 1], v[l:l + 1], tb, f"{name}{l}") for l, r in enumerate(recvs)]
        return [jnp.concatenate(pair, axis=0) for pair in zip(*outs)]

    big = [col_sharded(r_att_in, att_w_in, m_att_w_in, v_att_w_in, 320, "adamw_att_in"),
           row_sharded(r_att_out, att_w_out, m_att_w_out, v_att_w_out, 128, "adamw_att_out"),
           col_sharded(r_dn_in, dn_w_in, m_dn_w_in, v_dn_w_in, 432, "adamw_dn_in"),
           row_sharded(r_dn_out, dn_w_out, m_dn_w_out, v_dn_w_out, 128, "adamw_dn_out"),
           per_layer(row_sharded, [r_kv0, r_kv1], mem_w_kv, m_mem_w_kv, v_mem_w_kv, 128, "adamw_mem_kv"),
           per_layer(col_sharded, [r_gu0, r_gu1], ffn_w_gate_up, m_ffn_w_gate_up, v_ffn_w_gate_up, 176,
                     "adamw_ffn_gu"),
           per_layer(row_sharded, [r_down0, r_down1], ffn_w_down, m_ffn_w_down, v_ffn_w_down, 176,
                     "adamw_ffn_down")]
    g_big, d_big, nm_big, nv_big = [[b[i] for b in big] for i in range(4)]

    g_small = _reduce8(r_small, SMALL_ROWS, "reduce_small")
    rep_shapes = [(32, 12), (1, 2, 6), (1, 2, 6), (1, 128), (2, D), (2, D), (2, D), (2, D), (2, D)]
    *g_rep, g_conv_full = _unpack_small(g_small, rep_shapes + [(DN_CONV, 3 * TOK_W)])
    me = _index(_me_xyc())
    g_conv = lax.dynamic_slice(g_conv_full, (0, me * 288), (DN_CONV, 288)).reshape(CONV_SHARD)
    small_shapes = rep_shapes + [CONV_SHARD]
    small_w = [rel_bias, dn_a_log, dn_dt_bias, dn_out_norm, mem_norm, norm_mix_pre, norm_mix_post,
               norm_ffn_pre, norm_ffn_post, dn_conv]
    small_m = [m_rel_bias, m_dn_a_log, m_dn_dt_bias, m_dn_out_norm, m_mem_norm, m_norm_mix_pre, m_norm_mix_post,
               m_norm_ffn_pre, m_norm_ffn_post, m_dn_conv]
    small_v = [v_rel_bias, v_dn_a_log, v_dn_dt_bias, v_dn_out_norm, v_mem_norm, v_norm_mix_pre, v_norm_mix_post,
               v_norm_ffn_pre, v_norm_ffn_post, v_dn_conv]
    g_small_list = g_rep + [g_conv]
    outs_small = _adamw(_pack_small(g_small_list, 24), _pack_small(small_w, 24), _pack_small(small_m, 24),
                        _pack_small(small_v, 24), 24, "adamw_small")
    d_small, nm_small, nv_small = [_unpack_small(o, small_shapes) for o in outs_small]

    def ordered(small, big):
        return [small[0], big[0], big[1], big[2], small[9], small[1], small[2], small[3], big[3], small[4],
                big[4], small[5], small[6], small[7], small[8], big[5], big[6]]

    g_small_out = [g.reshape(s) for g, s in zip(g_small_list, small_shapes)]
    return (loss, grad_x[None], *ordered(g_small_out, g_big), *ordered(d_small, d_big),
            *ordered(nm_small, nm_big), *ordered(nv_small, nv_big))
```

```python
import functools
import math

import numpy as np
import jax
import jax.numpy as jnp
from jax import lax
from jax.experimental import pallas as pl
from jax.experimental.pallas import tpu as pltpu

F32 = jnp.float32
BF16 = jnp.bfloat16
HI = lax.Precision.HIGHEST
MESH = pl.DeviceIdType.MESH

N_DEV = 8
D = 1024
EPS = 1e-6
NEG = -1e30
TOK_W = 768
MEM_W = 256
ATT_HD = 64
DIL_GROUPS = ((128, 1), (512, 4), (2048, 16))
BAND_HALF = 64
REL_BUCKETS = 32
REL_MAX_DIST = 1024
DN_HD = 128
DN_HEADS = 6
DN_CONV = 5
DN_CHUNK = 64
MEM_HEADS = 4
D_FF = 2816
ATT_IN = 2560
DN_IN = 3352
DN_IN_PAD = 3456

ADAM_LR, ADAM_B1, ADAM_B2, ADAM_EPS, ADAM_WD, ADAM_STEP = 0.001, 0.9, 0.999, 1e-08, 0.01, 10

PACK_C = 512
BIG_ROWS = 6480
SMALL_ROWS = 48
VMEM_LIMIT = 48 * 1024 * 1024


def _cparams(sem=None):
    kw = dict(vmem_limit_bytes=VMEM_LIMIT)
    if sem is not None:
        kw["dimension_semantics"] = sem
    return pltpu.CompilerParams(**kw)


def _tile(n, cap):
    if n <= cap:
        return n
    best = None
    for t in range(128, cap + 1, 128):
        if n % t == 0:
            best = t
    assert best is not None, (n, cap)
    return best


def _matmul(a, b, mode, out_dtype, name, tm=1024, tn=1408, tk=None):
    if tk is None:
        tk = 2048 if mode == "tn" else 2816
    if mode == "nn":
        (m, kc), (_, n) = a.shape, b.shape
        dims = (((1,), (0,)), ((), ()))
    elif mode == "nt":
        (m, kc), (n, _) = a.shape, b.shape
        dims = (((1,), (1,)), ((), ()))
    else:
        (kc, m), (_, n) = a.shape, b.shape
        dims = (((0,), (0,)), ((), ()))
    tm = m if m <= tm else _tile(m, tm)
    tn = _tile(n, tn)
    tk = _tile(kc, tk)
    nk = kc // tk

    def body(a_ref, b_ref, o_ref, acc_ref):
        k = pl.program_id(2)
        part = lax.dot_general(a_ref[...], b_ref[...], dims, preferred_element_type=F32)

        @pl.when(k == 0)
        def _():
            acc_ref[...] = part

        @pl.when(k > 0)
        def _():
            acc_ref[...] += part

        @pl.when(k == nk - 1)
        def _():
            o_ref[...] = acc_ref[...].astype(o_ref.dtype)

    if mode == "nn":
        a_spec = pl.BlockSpec((tm, tk), lambda i, j, k: (i, k))
        b_spec = pl.BlockSpec((tk, tn), lambda i, j, k: (k, j))
    elif mode == "nt":
        a_spec = pl.BlockSpec((tm, tk), lambda i, j, k: (i, k))
        b_spec = pl.BlockSpec((tn, tk), lambda i, j, k: (j, k))
    else:
        a_spec = pl.BlockSpec((tk, tm), lambda i, j, k: (k, i))
        b_spec = pl.BlockSpec((tk, tn), lambda i, j, k: (k, j))
    return pl.pallas_call(
        body, name=name, grid=(m // tm, n // tn, nk),
        in_specs=[a_spec, b_spec],
        out_specs=pl.BlockSpec((tm, tn), lambda i, j, k: (i, j)),
        out_shape=jax.ShapeDtypeStruct((m, n), out_dtype),
        scratch_shapes=[pltpu.VMEM((tm, tn), F32)],
        compiler_params=_cparams(("parallel", "parallel", "arbitrary")),
    )(a, b)


def _rowwise(fn, rows, params, outs, tb, name):
    t = rows[0].shape[0]
    nr, npar = len(rows), len(params)

    def body(*refs):
        ins = [r[...].astype(F32) for r in refs[:nr + npar]]
        res = fn(*ins)
        for o_ref, r in zip(refs[nr + npar:], res):
            o_ref[...] = r.astype(o_ref.dtype)

    return pl.pallas_call(
        body, name=name, grid=(t // tb,),
        in_specs=[pl.BlockSpec((tb, r.shape[1]), lambda i: (i, 0)) for r in rows]
        + [pl.BlockSpec(p.shape, lambda i: (0, 0)) for p in params],
        out_specs=[pl.BlockSpec((tb, c), lambda i: (i, 0)) for c, _ in outs],
        out_shape=[jax.ShapeDtypeStruct((t, c), dt) for c, dt in outs],
        compiler_params=_cparams(("parallel",)),
    )(*rows, *params)


def _rowwise_bwd(fn, rows, params, cots, row_grad, tb, name):
    t = rows[0].shape[0]
    nr, npar, nc = len(rows), len(params), len(cots)
    want = [i for i, g in enumerate(row_grad) if g is not None]

    def body(*refs):
        ins = [r[...].astype(F32) for r in refs[:nr + npar]]
        cts = tuple(r[...].astype(F32) for r in refs[nr + npar:nr + npar + nc])
        outs = refs[nr + npar + nc:]
        _, vjp = jax.vjp(fn, *ins)
        grads = vjp(cts)
        for o_ref, i in zip(outs[:len(want)], want):
            o_ref[...] = grads[i].astype(o_ref.dtype)
        first = pl.program_id(0) == 0
        for o_ref, g in zip(outs[len(want):], grads[nr:]):
            @pl.when(first)
            def _(o_ref=o_ref, g=g):
                o_ref[...] = g

            @pl.when(jnp.logical_not(first))
            def _(o_ref=o_ref, g=g):
                o_ref[...] += g

    res = pl.pallas_call(
        body, name=name, grid=(t // tb,),
        in_specs=[pl.BlockSpec((tb, r.shape[1]), lambda i: (i, 0)) for r in rows]
        + [pl.BlockSpec(p.shape, lambda i: (0, 0)) for p in params]
        + [pl.BlockSpec((tb, c.shape[1]), lambda i: (i, 0)) for c in cots],
        out_specs=[pl.BlockSpec((tb, rows[i].shape[1]), lambda i_: (i_, 0)) for i in want]
        + [pl.BlockSpec(p.shape, lambda i: (0, 0)) for p in params],
        out_shape=[jax.ShapeDtypeStruct(rows[i].shape, row_grad[i]) for i in want]
        + [jax.ShapeDtypeStruct(p.shape, F32) for p in params],
        compiler_params=_cparams(("arbitrary",)),
    )(*rows, *params, *cots)
    return list(res[:len(want)]), list(res[len(want):])


def _rms(x, g):
    return x * lax.rsqrt(jnp.mean(x * x, axis=-1, keepdims=True) + EPS) * g


def _fn_pre(x, g):
    return (_rms(x, g),)


def _fn_res_pre(x, y, g_post, g_pre):
    x1 = x + _rms(y, g_post)
    return x1, _rms(x1, g_pre)


def _fn_res(x, y, g_post):
    return (x + _rms(y, g_post),)


def _sigmoid(x):
    return 1.0 / (1.0 + jnp.exp(-x))


def _silu(x):
    return x * _sigmoid(x)


def _fn_swiglu(gu):
    return (_silu(gu[:, :D_FF]) * gu[:, D_FF:],)


def _fn_combine(o, lse):
    ls = [lse[:, 256 * g:256 * (g + 1)] for g in range(3)]
    mx = lax.stop_gradient(jnp.maximum(jnp.maximum(ls[0], ls[1]), ls[2]))
    es = [jnp.exp(l - mx) for l in ls]
    inv = 1.0 / (es[0] + es[1] + es[2])
    return (jnp.concatenate([o[:, 256 * g:256 * (g + 1)] * (es[g] * inv) for g in range(3)], axis=1),)


def _fn_outnorm(o_f, o_r, z, gain):
    res = []
    for h in range(DN_HEADS):
        sl = slice(DN_HD * h, DN_HD * (h + 1))
        o = o_f[:, sl] + o_r[:, sl]
        res.append(o * lax.rsqrt(jnp.mean(o * o, axis=-1, keepdims=True) + EPS) * gain * _silu(z[:, sl]))
    return (jnp.concatenate(res, axis=1),)


def _loss_kernel(x, tgt, tb, name):
    t, d = x.shape

    def body(x_ref, t_ref, dx_ref, l_ref, acc_ref):
        i = pl.program_id(0)
        e = x_ref[...] - t_ref[...]
        dx_ref[...] = e * (1.0 / d)
        part = jnp.sum(e * e, axis=0, keepdims=True)

        @pl.when(i == 0)
        def _():
            acc_ref[...] = part

        @pl.when(i > 0)
        def _():
            acc_ref[...] += part

        @pl.when(i == t // tb - 1)
        def _():
            l_ref[...] = jnp.broadcast_to(jnp.sum(acc_ref[...], axis=-1, keepdims=True), (1, 128))

    return pl.pallas_call(
        body, name=name, grid=(t // tb,),
        in_specs=[pl.BlockSpec((tb, d), lambda i: (i, 0))] * 2,
        out_specs=[pl.BlockSpec((tb, d), lambda i: (i, 0)), pl.BlockSpec((1, 128), lambda i: (0, 0))],
        out_shape=[jax.ShapeDtypeStruct((t, d), F32), jax.ShapeDtypeStruct((1, 128), F32)],
        scratch_shapes=[pltpu.VMEM((1, d), F32)],
        compiler_params=_cparams(("arbitrary",)),
    )(x, tgt)


def _band_fn(l_sub, bq, i, q, kw, vw, bm):
    w = bq + 2 * BAND_HALF
    s = lax.dot_general((q * (ATT_HD ** -0.5)).astype(BF16), kw.astype(BF16), (((1,), (1,)), ((), ())),
                        preferred_element_type=F32) + bm
    kpos = i * bq - BAND_HALF + lax.broadcasted_iota(jnp.int32, (bq, w), 1)
    s = jnp.where((kpos >= 0) & (kpos < l_sub), s, NEG)
    m = lax.stop_gradient(jnp.max(s, axis=-1, keepdims=True))
    p = jnp.exp(s - m)
    den = jnp.sum(p, axis=-1, keepdims=True)
    o = jnp.dot(p.astype(BF16), vw.astype(BF16), preferred_element_type=F32) / den
    return o, jnp.broadcast_to(m + jnp.log(den), o.shape)


def _band_specs(dil, l_sub, bq):
    w = bq + 2 * BAND_HALF
    qs = pl.BlockSpec((None, bq, ATT_HD), lambda h, r, i: (h * dil + r, i, 0))
    ks = pl.BlockSpec((None, l_sub + 2 * BAND_HALF, ATT_HD), lambda h, r, i: (h * dil + r, 0, 0))
    bs = pl.BlockSpec((None, bq, w), lambda h, r, i: (h, 0, 0))
    return qs, ks, bs


def _band_fwd(q, k, v, bm, dil, l_sub, bq, name):
    w = bq + 2 * BAND_HALF
    qs, ks, bs = _band_specs(dil, l_sub, bq)

    def body(q_ref, k_ref, v_ref, bm_ref, o_ref, l_ref):
        i = pl.program_id(2)
        st = pl.multiple_of(i * bq, bq)
        o, lse = _band_fn(l_sub, bq, i, q_ref[...].astype(F32), k_ref[pl.ds(st, w), :].astype(F32),
                          v_ref[pl.ds(st, w), :].astype(F32), bm_ref[...])
        o_ref[...] = o
        l_ref[...] = lse

    return pl.pallas_call(
        body, name=name, grid=(4, dil, l_sub // bq),
        in_specs=[qs, ks, ks, bs], out_specs=[qs, qs],
        out_shape=[jax.ShapeDtypeStruct(q.shape, F32)] * 2,
        compiler_params=_cparams(("parallel", "parallel", "arbitrary")),
    )(q, k, v, bm)


def _band_bwd(q, k, v, bm, do, dlse, dil, l_sub, bq, name):
    w = bq + 2 * BAND_HALF
    qs, ks, bs = _band_specs(dil, l_sub, bq)

    def body(q_ref, k_ref, v_ref, bm_ref, do_ref, dl_ref, dq_ref, dk_ref, dv_ref, dbm_ref):
        r, i = pl.program_id(1), pl.program_id(2)
        st = pl.multiple_of(i * bq, bq)
        _, vjp = jax.vjp(functools.partial(_band_fn, l_sub, bq, i),
                         q_ref[...].astype(F32), k_ref[pl.ds(st, w), :].astype(F32),
                         v_ref[pl.ds(st, w), :].astype(F32), bm_ref[...])
        dq, dkw, dvw, dbm = vjp((do_ref[...], dl_ref[...]))
        dq_ref[...] = dq

        @pl.when(i == 0)
        def _():
            dk_ref[...] = jnp.zeros_like(dk_ref)
            dv_ref[...] = jnp.zeros_like(dv_ref)

        dk_ref[pl.ds(st, w), :] += dkw
        dv_ref[pl.ds(st, w), :] += dvw

        @pl.when((i == 0) & (r == 0))
        def _():
            dbm_ref[...] = dbm

        @pl.when((i > 0) | (r > 0))
        def _():
            dbm_ref[...] += dbm

    return pl.pallas_call(
        body, name=name, grid=(4, dil, l_sub // bq),
        in_specs=[qs, ks, ks, bs, qs, qs], out_specs=[qs, ks, ks, bs],
        out_shape=[jax.ShapeDtypeStruct(q.shape, F32), jax.ShapeDtypeStruct(k.shape, F32),
                   jax.ShapeDtypeStruct(k.shape, F32), jax.ShapeDtypeStruct(bm.shape, F32)],
        compiler_params=_cparams(("parallel", "arbitrary", "arbitrary")),
    )(q, k, v, bm, do, dlse)


def _t5_bucket(rel):
    half = REL_BUCKETS // 2
    max_exact = half // 2
    n = np.abs(rel)
    large = max_exact + (np.log(np.maximum(n, 1) / max_exact) / math.log(REL_MAX_DIST / max_exact)
                         * (half - max_exact)).astype(np.int64)
    large = np.minimum(large, half - 1)
    return ((rel > 0) * half + np.where(n < max_exact, n, large)).astype(np.int32)


def _bucket_onehot(dil):
    idx = _t5_bucket(np.arange(-BAND_HALF, BAND_HALF + 1) * dil)
    oh = np.zeros((2 * BAND_HALF + 1, REL_BUCKETS), np.float32)
    oh[np.arange(2 * BAND_HALF + 1), idx] = 1.0
    return oh


def _band_bias(rel_bias, gi, dil, bq):
    w = bq + 2 * BAND_HALF
    nb = 2 * BAND_HALF + 1
    bias = jnp.dot(jnp.asarray(_bucket_onehot(dil)), rel_bias[:, 4 * gi:4 * gi + 4], precision=HI)
    row = jnp.concatenate([bias.T, jnp.full((4, w + 1 - nb), NEG, F32)], axis=1)
    flat = jnp.tile(row, (1, bq))[:, :bq * w]
    return flat.reshape(4, bq, w)


def _relbias_grad(dbms, name):
    nb = 2 * BAND_HALF + 1
    bq = max(d.shape[1] for d in dbms)
    skew = []
    for dbm in dbms:
        bqg, w = dbm.shape[1], dbm.shape[2]
        flat = jnp.pad(dbm.reshape(4, bqg * w), ((0, 0), (0, bqg)))
        skew.append(jnp.pad(flat.reshape(4, bqg, w + 1)[:, :, :nb], ((0, 0), (0, bq - bqg), (0, 256 - nb))))
    sk = jnp.concatenate(skew, axis=0)
    oh = np.zeros((3, 256, 128), np.float32)
    for gi, (_, dil) in enumerate(DIL_GROUPS):
        oh[gi, :2 * BAND_HALF + 1, :REL_BUCKETS] = _bucket_onehot(dil)

    def body(s_ref, oh_ref, o_ref):
        col = jnp.sum(s_ref[...], axis=0, keepdims=True)
        o_ref[...] = jnp.dot(jnp.broadcast_to(col, (8, 256)), oh_ref[...], precision=HI, preferred_element_type=F32)

    out = pl.pallas_call(
        body, name=name, grid=(12,),
        in_specs=[pl.BlockSpec((None, bq, 256), lambda n: (n, 0, 0)),
                  pl.BlockSpec((None, 256, 128), lambda n: (n // 4, 0, 0))],
        out_specs=pl.BlockSpec((None, 8, 128), lambda n: (n, 0, 0)),
        out_shape=jax.ShapeDtypeStruct((12, 8, 128), F32),
        compiler_params=_cparams(("parallel",)),
    )(sk, jnp.asarray(oh))
    return out[:, 0, :REL_BUCKETS].T


def _mem_fn(q, k, v):
    s = lax.dot_general((q * (ATT_HD ** -0.5)).astype(BF16), k.astype(BF16), (((1,), (1,)), ((), ())),
                        preferred_element_type=F32)
    m = lax.stop_gradient(jnp.max(s, axis=-1, keepdims=True))
    p = jnp.exp(s - m)
    p = p / jnp.sum(p, axis=-1, keepdims=True)
    return jnp.dot(p.astype(BF16), v.astype(BF16), preferred_element_type=F32)


def _mem_specs(tb, ml):
    qs = pl.BlockSpec((None, tb, ATT_HD), lambda h, i: (h, i, 0))
    ks = pl.BlockSpec((None, ml, ATT_HD), lambda h, i: (h, 0, 0))
    return qs, ks


def _mem_fwd(q, k, v, tb, name):
    qs, ks = _mem_specs(tb, k.shape[1])

    def body(q_ref, k_ref, v_ref, o_ref):
        o_ref[...] = _mem_fn(q_ref[...].astype(F32), k_ref[...], v_ref[...])

    return pl.pallas_call(
        body, name=name, grid=(MEM_HEADS, q.shape[1] // tb),
        in_specs=[qs, ks, ks], out_specs=qs, out_shape=jax.ShapeDtypeStruct(q.shape, F32),
        compiler_params=_cparams(("parallel", "parallel")),
    )(q, k, v)


def _mem_bwd(q, k, v, do, tb, name):
    qs, ks = _mem_specs(tb, k.shape[1])

    def body(q_ref, k_ref, v_ref, do_ref, dq_ref, dk_ref, dv_ref):
        i = pl.program_id(1)
        _, vjp = jax.vjp(_mem_fn, q_ref[...].astype(F32), k_ref[...], v_ref[...])
        dq, dk, dv = vjp(do_ref[...])
        dq_ref[...] = dq

        @pl.when(i == 0)
        def _():
            dk_ref[...] = dk
            dv_ref[...] = dv

        @pl.when(i > 0)
        def _():
            dk_ref[...] += dk
            dv_ref[...] += dv

    return pl.pallas_call(
        body, name=name, grid=(MEM_HEADS, q.shape[1] // tb),
        in_specs=[qs, ks, ks, qs], out_specs=[qs, ks, ks],
        out_shape=[jax.ShapeDtypeStruct(q.shape, F32), jax.ShapeDtypeStruct(k.shape, F32),
                   jax.ShapeDtypeStruct(k.shape, F32)],
        compiler_params=_cparams(("parallel", "arbitrary")),
    )(q, k, v, do)


CONV_PAD = 8


def _conv_post(kind, acc):
    s = _silu(acc)
    if kind == 2:
        return s
    scale = DN_HD ** -0.5 if kind == 0 else 1.0
    return s * lax.rsqrt(jnp.sum(s * s, axis=-1, keepdims=True) + EPS) * scale


def _conv_acc(xp_ref, w, r0, rt):
    acc = None
    for i in range(DN_CONV):
        term = w[i:i + 1, :] * xp_ref[pl.ds(CONV_PAD + r0 + i - DN_CONV // 2, rt), :]
        acc = term if acc is None else acc + term
    return acc


def _conv_fwd(xp, w8, kind, rt, name):
    t = xp.shape[0] - 2 * CONV_PAD

    def body(xp_ref, w_ref, o_ref):
        w = w_ref[...]
        for r in range(t // rt):
            o_ref[pl.ds(r * rt, rt), :] = _conv_post(kind, _conv_acc(xp_ref, w, r * rt, rt))

    return pl.pallas_call(
        body, name=name, grid=(DN_HEADS,),
        in_specs=[pl.BlockSpec((t + 2 * CONV_PAD, DN_HD), lambda j: (0, 6 * kind + j)),
                  pl.BlockSpec((8, DN_HD), lambda j: (0, 6 * kind + j))],
        out_specs=pl.BlockSpec((t, DN_HD), lambda j: (0, j)),
        out_shape=jax.ShapeDtypeStruct((t, TOK_W), F32),
        compiler_params=_cparams(("parallel",)),
    )(xp, w8)


def _conv_bwd(xp, w8, d_f, d_r, kind, rt, name):
    t = xp.shape[0] - 2 * CONV_PAD

    def body(xp_ref, w_ref, df_ref, dr_ref, dx_ref, dw_ref, dpad_ref):
        w = w_ref[...]
        zero = jnp.zeros((CONV_PAD, DN_HD), F32)
        dpad_ref[pl.ds(0, CONV_PAD), :] = zero
        dpad_ref[pl.ds(CONV_PAD + t, CONV_PAD), :] = zero
        dw = [jnp.zeros((1, DN_HD), F32) for _ in range(DN_CONV)]
        for r in range(t // rt):
            rows = pl.ds(r * rt, rt)
            acc = _conv_acc(xp_ref, w, r * rt, rt)
            _, vjp = jax.vjp(functools.partial(_conv_post, kind), acc)
            (dacc,) = vjp(df_ref[rows, :] + dr_ref[rows, :])
            dpad_ref[pl.ds(CONV_PAD + r * rt, rt), :] = dacc
            for i in range(DN_CONV):
                xs = xp_ref[pl.ds(CONV_PAD + r * rt + i - DN_CONV // 2, rt), :]
                dw[i] = dw[i] + jnp.sum(dacc * xs, axis=0, keepdims=True)
        dw_ref[...] = jnp.concatenate(dw + [jnp.zeros((8 - DN_CONV, DN_HD), F32)], axis=0)
        for r in range(t // rt):
            acc = None
            for i in range(DN_CONV):
                term = w[i:i + 1, :] * dpad_ref[pl.ds(CONV_PAD + r * rt - i + DN_CONV // 2, rt), :]
                acc = term if acc is None else acc + term
            dx_ref[pl.ds(r * rt, rt), :] = acc

    return pl.pallas_call(
        body, name=name, grid=(DN_HEADS,),
        in_specs=[pl.BlockSpec((t + 2 * CONV_PAD, DN_HD), lambda j: (0, 6 * kind + j)),
                  pl.BlockSpec((8, DN_HD), lambda j: (0, 6 * kind + j)),
                  pl.BlockSpec((t, DN_HD), lambda j: (0, j)),
                  pl.BlockSpec((t, DN_HD), lambda j: (0, j))],
        out_specs=[pl.BlockSpec((t, DN_HD), lambda j: (0, j)), pl.BlockSpec((8, DN_HD), lambda j: (0, j))],
        out_shape=[jax.ShapeDtypeStruct((t, TOK_W), F32), jax.ShapeDtypeStruct((8, TOK_W), F32)],
        scratch_shapes=[pltpu.VMEM((t + 2 * CONV_PAD, DN_HD), F32)],
        compiler_params=_cparams(("parallel",)),
    )(xp, w8, d_f, d_r)


def _softplus(x):
    e = jnp.exp(-jnp.abs(x))
    return jnp.maximum(x, 0.0) + jnp.where(e < 1e-4, e - 0.5 * e * e, jnp.log(1.0 + e))


_NN = (((2,), (1,)), ((0,), (0,)))
_NT = (((2,), (2,)), ((0,), (0,)))
_TN = (((1,), (1,)), ((0,), (0,)))


def _dot(a, b, dims=_NN):
    return lax.dot_general(a.astype(BF16), b.astype(BF16), dims, preferred_element_type=F32)


def _hi_lo(x):
    hi = x.astype(BF16)
    return hi, (x - hi.astype(F32)).astype(BF16)


def _mask_dot(mask_bf16, x, dims):
    x1 = x.astype(BF16)
    r = x - x1.astype(F32)
    x2, x3 = _hi_lo(r)
    d = functools.partial(lax.dot_general, dimension_numbers=dims, preferred_element_type=F32)
    return d(mask_bf16, x1) + d(mask_bf16, x2) + d(mask_bf16, x3)


@jax.custom_vjp
def _dot_mask(mask_bf16, x):
    return _mask_dot(mask_bf16, x, _NN)


def _dot_mask_fwd(mask_bf16, x):
    return _mask_dot(mask_bf16, x, _NN), mask_bf16


def _dot_mask_bwd(mask_bf16, ct):
    return jnp.zeros_like(mask_bf16), _mask_dot(mask_bf16, ct, _TN)


_dot_mask.defvjp(_dot_mask_fwd, _dot_mask_bwd)


def _dot3_raw(a, b, dims):
    a1, a2 = _hi_lo(a)
    b1, b2 = _hi_lo(b)
    d = functools.partial(lax.dot_general, dimension_numbers=dims, preferred_element_type=F32)
    return d(a1, b1) + d(a1, b2) + d(a2, b1)


def _unit_solve_pass(lmat, rhs, masks):
    ainv = masks[6] - lmat * masks[0]
    for sh in range(1, 6):
        ainv = ainv - _dot3_raw(_dot3_raw(ainv, lmat * masks[sh], _NN), ainv, _NN)
    return _dot3_raw(ainv, rhs, _NN), ainv


@jax.custom_vjp
def _unit_solve(lmat, rhs, masks):
    return _unit_solve_pass(lmat, rhs, masks)[0]


def _unit_solve_fwd(lmat, rhs, masks):
    sol, ainv = _unit_solve_pass(lmat, rhs, masks)
    return sol, (sol, ainv, masks)


def _unit_solve_bwd(res, ct):
    sol, ainv, masks = res
    d_rhs = _dot3_raw(ainv, ct, _TN)
    return -_dot3_raw(d_rhs, sol, _NT), d_rhs, tuple(jnp.zeros_like(m) for m in masks)


_unit_solve.defvjp(_unit_solve_fwd, _unit_solve_bwd)


def _block_masks(rev, row, col):
    c = DN_CHUNK
    prow = jnp.where(rev, c - 1 - row, row)
    pcol = jnp.where(rev, c - 1 - col, col)
    masks = []
    for sh in range(6):
        differ = (prow ^ pcol) >> sh
        miss = (differ ^ 1) + (1 - ((prow >> sh) & 1))
        masks.append(jnp.where(miss == 0, 1.0, 0.0))
    masks.append(jnp.where(row == col, 1.0, 0.0))
    return tuple(masks)


def _dn_chunk(q, k, v, al, be, alc, a_row, dt_row, a_rowc, dt_rowc, s):
    n, c = q.shape[0], DN_CHUNK
    rev = lax.broadcasted_iota(jnp.int32, (n, c, c), 0) >= n // 2
    row = lax.broadcasted_iota(jnp.int32, (n, c, c), 1)
    col = lax.broadcasted_iota(jnp.int32, (n, c, c), 2)
    ahead = jnp.where(rev, col - row, row - col)
    incl = ahead >= 0
    strict = ahead > 0
    incl_b = incl.astype(BF16)

    g = -jnp.exp(a_row) * _softplus(al + dt_row)
    beta = _sigmoid(be)
    g_c = -jnp.exp(a_rowc) * _softplus(alc + dt_rowc)
    gc = _dot_mask(incl_b, g)
    gcc = _dot_mask(incl_b, g_c)
    decay = jnp.exp(jnp.where(incl, gcc - jnp.swapaxes(gcc, 1, 2), NEG))
    kb = k * beta
    lmat = jnp.where(strict, _dot(kb, k, _NT) * decay, 0.0)
    rhs = jnp.concatenate([v * beta, kb * jnp.exp(gc)], axis=2)
    sol = _unit_solve(lmat, rhs, _block_masks(rev, row, col))
    u, w = sol[:, :, :DN_HD], sol[:, :, DN_HD:]
    intra = jnp.where(incl, _dot(q, k, _NT) * decay, 0.0)
    v_new = u - _dot(w, s)
    out = _dot(q * jnp.exp(gc), s) + _dot(intra, v_new)
    g_last = jnp.sum(g, axis=1, keepdims=True)
    s_new = s * jnp.exp(g_last) + _dot(k * jnp.exp(g_last - gc), v_new, _TN)
    return out, s_new


DN_HG = 6


def _dn_load(f_refs, r_refs, alf, bef, alr, ber, a_ref, dt_ref):
    c, hg = DN_CHUNK, DN_HG
    sls = [slice(DN_HD * h, DN_HD * (h + 1)) for h in range(hg)]
    toks = [jnp.stack([f[:, sl] for sl in sls] + [r[:, sl] for sl in sls]) for f, r in zip(f_refs, r_refs)]
    al = jnp.concatenate([alf[...], alr[...]], axis=0)
    be = jnp.concatenate([bef[...], ber[...]], axis=0)
    alc = jnp.concatenate([alf[:, :, 0:c], alr[:, :, 0:c]], axis=0)
    a = jnp.concatenate([a_ref[0], a_ref[1]], axis=0)
    dt = jnp.concatenate([dt_ref[0], dt_ref[1]], axis=0)
    ac = jnp.concatenate([a_ref[0, :, :, 0:c], a_ref[1, :, :, 0:c]], axis=0)
    dtc = jnp.concatenate([dt_ref[0, :, :, 0:c], dt_ref[1, :, :, 0:c]], axis=0)
    return toks, (al, be, alc, a, dt, ac, dtc)


def _dn_views(nc, bwd):
    c, hg = DN_CHUNK, DN_HG
    if bwd:
        f_blk = lambda s: nc - 1 - s
        r_blk = lambda s: s
        st_blk = lambda s: nc - 1 - s
    else:
        f_blk = lambda s: s
        r_blk = lambda s: nc - 1 - s
        st_blk = lambda s: s
    tok_f = pl.BlockSpec((c, hg * DN_HD), lambda g, s: (f_blk(s), g))
    tok_r = pl.BlockSpec((c, hg * DN_HD), lambda g, s: (r_blk(s), g))
    gate_f = pl.BlockSpec((None, hg, c, DN_HD), lambda g, s: (0, g, f_blk(s), 0))
    gate_r = pl.BlockSpec((None, hg, c, DN_HD), lambda g, s: (1, g, r_blk(s), 0))
    par = pl.BlockSpec((2, hg, 1, DN_HD), lambda g, s: (0, g, 0, 0))
    state = pl.BlockSpec((2, hg, None, DN_HD, DN_HD), lambda g, s: (0, g, st_blk(s), 0, 0))
    return tok_f, tok_r, gate_f, gate_r, par, state


def _dn_fwd(q, k, v, al, be, a_rows, dt_rows, name):
    t = q.shape[0]
    c, hg = DN_CHUNK, DN_HG
    nc = t // c
    tok_f, tok_r, gate_f, gate_r, par, state = _dn_views(nc, False)

    def body(qf, kf, vf, qr, kr, vr, alf, bef, alr, ber, a_ref, dt_ref, of_ref, or_ref, st_ref, s_ref):
        @pl.when(pl.program_id(1) == 0)
        def _():
            s_ref[...] = jnp.zeros_like(s_ref)

        (q_, k_, v_), gates = _dn_load((qf, kf, vf), (qr, kr, vr), alf, bef, alr, ber, a_ref, dt_ref)
        s = s_ref[...]
        st_ref[0] = s[:hg]
        st_ref[1] = s[hg:]
        out, s_new = _dn_chunk(q_, k_, v_, *gates, s)
        for h in range(hg):
            sl = slice(DN_HD * h, DN_HD * (h + 1))
            of_ref[:, sl] = out[h]
            or_ref[:, sl] = out[hg + h]
        s_ref[...] = s_new

    return pl.pallas_call(
        body, name=name, grid=(DN_HEADS // hg, nc),
        in_specs=[tok_f] * 3 + [tok_r] * 3 + [gate_f, gate_f, gate_r, gate_r, par, par],
        out_specs=[tok_f, tok_r, state],
        out_shape=[jax.ShapeDtypeStruct((t, TOK_W), F32)] * 2
        + [jax.ShapeDtypeStruct((2, DN_HEADS, nc, DN_HD, DN_HD), F32)],
        scratch_shapes=[pltpu.VMEM((2 * hg, DN_HD, DN_HD), F32)],
        compiler_params=_cparams(("parallel", "arbitrary")),
    )(q, k, v, q, k, v, al, be, al, be, a_rows, dt_rows)


def _dn_bwd(q, k, v, al, be, a_rows, dt_rows, states, do, name):
    t = q.shape[0]
    c, hg = DN_CHUNK, DN_HG
    assert hg == DN_HEADS
    nc = t // c
    tok_f, tok_r, gate_f, gate_r, par, state = _dn_views(nc, True)
    gout_f = pl.BlockSpec((c, DN_HD), lambda g, s: (nc - 1 - s, 0))
    gout_r = pl.BlockSpec((c, DN_HD), lambda g, s: (s, 0))

    def body(qf, kf, vf, qr, kr, vr, alf, bef, alr, ber, a_ref, dt_ref, st_ref, dof, dor,
             dqf, dkf, dvf, dqr, dkr, dvr, dgf, dgr, da_ref, ddt_ref, ds_ref):
        first = pl.program_id(1) == 0

        @pl.when(first)
        def _():
            ds_ref[...] = jnp.zeros_like(ds_ref)
            da_ref[...] = jnp.zeros_like(da_ref)
            ddt_ref[...] = jnp.zeros_like(ddt_ref)

        def lanes(x):
            return jnp.sum(x, axis=-1, keepdims=True)

        (q_, k_, v_, do_), gates = _dn_load((qf, kf, vf, dof), (qr, kr, vr, dor), alf, bef, alr, ber, a_ref, dt_ref)
        s = jnp.concatenate([st_ref[0], st_ref[1]], axis=0)
        _, vjp = jax.vjp(_dn_chunk, q_, k_, v_, *gates, s)
        dq, dk, dv, dal, dbe, dalc, da, ddt, dac, ddtc, ds = vjp((do_, ds_ref[...]))
        for h in range(hg):
            sl = slice(DN_HD * h, DN_HD * (h + 1))
            dqf[:, sl], dkf[:, sl], dvf[:, sl] = dq[h], dk[h], dv[h]
            dqr[:, sl], dkr[:, sl], dvr[:, sl] = dq[hg + h], dk[hg + h], dv[hg + h]
        dal, dbe = lanes(dal) + lanes(dalc), lanes(dbe)
        lane = lax.broadcasted_iota(jnp.int32, (c, DN_HD), 1)
        for d, dg_ref in enumerate((dgf, dgr)):
            dg = jnp.zeros((c, DN_HD), F32)
            for h in range(hg):
                dg = jnp.where(lane == h, dal[d * hg + h], jnp.where(lane == hg + h, dbe[d * hg + h], dg))
            dg_ref[...] = dg
        da = jnp.broadcast_to(lanes(da) + lanes(dac), da.shape)
        ddt = jnp.broadcast_to(lanes(ddt) + lanes(ddtc), ddt.shape)
        da_ref[0] += da[:hg]
        da_ref[1] += da[hg:]
        ddt_ref[0] += ddt[:hg]
        ddt_ref[1] += ddt[hg:]
        ds_ref[...] = ds

    tok = jax.ShapeDtypeStruct((t, TOK_W), F32)
    gate = jax.ShapeDtypeStruct((t, DN_HD), F32)
    parsh = jax.ShapeDtypeStruct((2, DN_HEADS, 1, DN_HD), F32)
    res = pl.pallas_call(
        body, name=name, grid=(DN_HEADS // hg, nc),
        in_specs=[tok_f] * 3 + [tok_r] * 3 + [gate_f, gate_f, gate_r, gate_r, par, par, state, tok_f, tok_r],
        out_specs=[tok_f] * 3 + [tok_r] * 3 + [gout_f, gout_r, par, par],
        out_shape=[tok] * 6 + [gate] * 2 + [parsh] * 2,
        scratch_shapes=[pltpu.VMEM((2 * hg, DN_HD, DN_HD), F32)],
        compiler_params=_cparams(("parallel", "arbitrary")),
    )(q, k, v, q, k, v, al, be, al, be, a_rows, dt_rows, states, do, do)
    dqf, dkf, dvf, dqr, dkr, dvr, dgf, dgr, da, ddt = res
    dgate = jnp.concatenate([dgf[:, :2 * DN_HEADS], dgr[:, :2 * DN_HEADS]], axis=1)
    return (dqf, dkf, dvf), (dqr, dkr, dvr), dgate, da, ddt


BAND_BQ = 256
ROW_TB = 256
MEM_TB = 512
CONV_RT = 512


def _to_sub(x, dil):
    l = x.shape[0] // dil
    return x.reshape(l, dil, 4, ATT_HD).transpose(2, 1, 0, 3).reshape(4 * dil, l, ATT_HD)


def _from_sub(x, dil):
    l = x.shape[1]
    return x.reshape(4, dil, l, ATT_HD).transpose(2, 1, 0, 3).reshape(l * dil, 4 * ATT_HD)


def _heads_major(x):
    return x.reshape(x.shape[0], MEM_HEADS, ATT_HD).transpose(1, 0, 2)


def _heads_minor(x):
    return x.transpose(1, 0, 2).reshape(x.shape[1], MEM_HEADS * ATT_HD)


def _mem_kv_fwd(mem, gain, w_kv, li):
    (memn,) = _rowwise(_fn_pre, [mem], [gain], [(D, BF16)], mem.shape[0], f"memnorm_fwd{li}")
    kv = _matmul(memn, w_kv, "nn", F32, f"memkv_fwd{li}")
    return _heads_major(kv[:, :MEM_W]), _heads_major(kv[:, MEM_W:]), memn


def _mem_kv_bwd(mem, gain, w_kv, memn, dkm, dvm, li):
    dkv = jnp.concatenate([_heads_minor(dkm), _heads_minor(dvm)], axis=1).astype(BF16)
    dw = _matmul(memn, dkv, "tn", BF16, f"memkv_dw{li}")
    dmemn = _matmul(dkv, w_kv, "nt", F32, f"memkv_dx{li}")
    _, (dgain,) = _rowwise_bwd(_fn_pre, [mem], [gain], [dmemn], [None], mem.shape[0], f"memnorm_bwd{li}")
    return dw, dgain


def _attn_mixer_fwd(p, rel_bias, kv_fn):
    t = p.shape[0]
    saved, outs, lses = [], [], []
    for gi, (_, dil) in enumerate(DIL_GROUPS):
        l_sub = t // dil
        bq = min(BAND_BQ, l_sub)
        q = _to_sub(p[:, 256 * gi:256 * (gi + 1)], dil)
        pad = ((0, 0), (BAND_HALF, BAND_HALF), (0, 0))
        k = jnp.pad(_to_sub(p[:, TOK_W + 256 * gi:TOK_W + 256 * (gi + 1)], dil), pad)
        v = jnp.pad(_to_sub(p[:, 2 * TOK_W + 256 * gi:2 * TOK_W + 256 * (gi + 1)], dil), pad)
        bm = _band_bias(rel_bias, gi, dil, bq)
        o, lse = _band_fwd(q, k, v, bm, dil, l_sub, bq, f"band_fwd{gi}")
        outs.append(_from_sub(o, dil))
        lses.append(_from_sub(lse, dil))
        saved.append((q, k, v, bm))
    o_all = jnp.concatenate(outs, axis=1)
    lse_all = jnp.concatenate(lses, axis=1)
    (mixed,) = _rowwise(_fn_combine, [o_all, lse_all], [], [(TOK_W, BF16)], ROW_TB, "combine_fwd")
    qm = _heads_major(p[:, 3 * TOK_W:])
    km, vm, memn = kv_fn(mixed)
    memo = _mem_fwd(qm, km, vm, min(MEM_TB, t), "mem_fwd0")
    cat = jnp.concatenate([mixed, _heads_minor(memo).astype(BF16)], axis=1)
    return cat, (saved, o_all, lse_all, qm), (km, vm, memn)


def _attn_mixer_bwd(dcat, res, km, vm):
    saved, o_all, lse_all, qm = res
    t = dcat.shape[0]
    (do_all, dlse_all), _ = _rowwise_bwd(_fn_combine, [o_all, lse_all], [], [dcat[:, :TOK_W]], [F32, F32],
                                         ROW_TB, "combine_bwd")
    dqs, dks, dvs, dbms = [], [], [], []
    for gi, (_, dil) in enumerate(DIL_GROUPS):
        l_sub = t // dil
        bq = min(BAND_BQ, l_sub)
        q, k, v, bm = saved[gi]
        do = _to_sub(do_all[:, 256 * gi:256 * (gi + 1)], dil)
        dl = _to_sub(dlse_all[:, 256 * gi:256 * (gi + 1)], dil)
        dq, dk, dv, dbm = _band_bwd(q, k, v, bm, do, dl, dil, l_sub, bq, f"band_bwd{gi}")
        dqs.append(_from_sub(dq, dil))
        dks.append(_from_sub(dk[:, BAND_HALF:-BAND_HALF], dil))
        dvs.append(_from_sub(dv[:, BAND_HALF:-BAND_HALF], dil))
        dbms.append(dbm)
    dqm, dkm, dvm = _mem_bwd(qm, km, vm, _heads_major(dcat[:, TOK_W:]), min(MEM_TB, t), "mem_bwd0")
    dp = jnp.concatenate(dqs + dks + dvs + [_heads_minor(dqm)], axis=1).astype(BF16)
    return dp, _relbias_grad(dbms, "relbias_grad"), dkm, dvm


def _dn_mixer_fwd(p, conv_w, a_log, dt_bias, out_norm, km, vm):
    t = p.shape[0]
    rt = min(CONV_RT, t)
    xp = jnp.pad(p[:, :3 * TOK_W], ((CONV_PAD, CONV_PAD), (0, 0)))
    w8 = jnp.pad(conv_w, ((0, 8 - DN_CONV), (0, 0)))
    q = _conv_fwd(xp, w8, 0, rt, "conv_fwd_q")
    k = _conv_fwd(xp, w8, 1, rt, "conv_fwd_k")
    v = _conv_fwd(xp, w8, 2, rt, "conv_fwd_v")
    gate = p[:, 4 * TOK_W:4 * TOK_W + 4 * DN_HEADS].reshape(t, 2, 2, DN_HEADS)
    bshape = (2, DN_HEADS, t, DN_HD)
    al = jnp.broadcast_to(gate[:, :, 0, :].transpose(1, 2, 0)[..., None], bshape)
    be = jnp.broadcast_to(gate[:, :, 1, :].transpose(1, 2, 0)[..., None], bshape)
    a_rows = jnp.broadcast_to(a_log[:, :, None, None], (2, DN_HEADS, 1, DN_HD))
    dt_rows = jnp.broadcast_to(dt_bias[:, :, None, None], (2, DN_HEADS, 1, DN_HD))
    o_f, o_r, states = _dn_fwd(q, k, v, al, be, a_rows, dt_rows, "dn_fwd")
    z = p[:, 3 * TOK_W:4 * TOK_W]
    gain = out_norm.reshape(1, DN_HD)
    (og,) = _rowwise(_fn_outnorm, [o_f, o_r, z], [gain], [(TOK_W, BF16)], ROW_TB, "outnorm_fwd")
    qm = _heads_major(p[:, 4 * TOK_W + 4 * DN_HEADS:DN_IN])
    memo = _mem_fwd(qm, km, vm, min(MEM_TB, t), "mem_fwd1")
    cat = jnp.concatenate([og, _heads_minor(memo).astype(BF16)], axis=1)
    return cat, (xp, w8, q, k, v, al, be, a_rows, dt_rows, o_f, o_r, states, z, gain, qm)


def _dn_mixer_bwd(dcat, res, km, vm):
    xp, w8, q, k, v, al, be, a_rows, dt_rows, o_f, o_r, states, z, gain, qm = res
    t = dcat.shape[0]
    rt = min(CONV_RT, t)
    (do, dz), (dgain,) = _rowwise_bwd(_fn_outnorm, [o_f, o_r, z], [gain], [dcat[:, :TOK_W]], [F32, None, F32],
                                      ROW_TB, "outnorm_bwd")
    d_f, d_r, dgate, da, ddt = _dn_bwd(q, k, v, al, be, a_rows, dt_rows, states, do, "dn_bwd")
    dxs, dws = [], []
    for kind, nm in enumerate("qkv"):
        dx, dw = _conv_bwd(xp, w8, d_f[kind], d_r[kind], kind, rt, f"conv_bwd_{nm}")
        dxs.append(dx)
        dws.append(dw)
    dconv = jnp.concatenate(dws, axis=1)[:DN_CONV]
    dqm, dkm, dvm = _mem_bwd(qm, km, vm, _heads_major(dcat[:, TOK_W:]), min(MEM_TB, t), "mem_bwd1")
    dp = jnp.concatenate(dxs + [dz, dgate, _heads_minor(dqm), jnp.zeros((t, DN_IN_PAD - DN_IN), F32)],
                         axis=1).astype(BF16)
    return dp, dconv, da[:, :, 0, 0], ddt[:, :, 0, 0], dgain.reshape(DN_HD), dkm, dvm


SWI_TB = 256


def _ffn_fwd(h, w_gu_t, w_d, li):
    gu = _matmul(h, w_gu_t, "nt", BF16, f"ffn_gu{li}")
    (a,) = _rowwise(_fn_swiglu, [gu], [], [(D_FF, BF16)], SWI_TB, f"swiglu_fwd{li}")
    return _matmul(a, w_d, "nn", F32, f"ffn_down{li}"), gu, a


def _ffn_bwd(df, h, w_gu_t, w_d, gu, a, li):
    da = _matmul(df, w_d, "nt", BF16, f"ffn_down_dx{li}")
    dwd = _matmul(a, df, "tn", BF16, f"ffn_down_dw{li}")
    (dgu,), _ = _rowwise_bwd(_fn_swiglu, [gu], [], [da], [BF16], SWI_TB, f"swiglu_bwd{li}")
    dh = _matmul(dgu, w_gu_t, "nn", F32, f"ffn_gu_dx{li}")
    dwgu_t = _matmul(dgu, h, "tn", BF16, f"ffn_gu_dw{li}")
    return dh, dwgu_t, dwd


def _fn_first(x, g):
    return x, _rms(x, g)


def _me_xyc():
    return lax.axis_index("x"), lax.axis_index("y"), lax.axis_index("c")


def _flip(coords, k):
    x, y, c = coords
    return (1 - x if k & 4 else x, 1 - y if k & 2 else y, 1 - c if k & 1 else c)


def _index(coords):
    x, y, c = coords
    return 4 * x + 2 * y + c


def _window(ref, axis, size, d):
    idx = [slice(None)] * len(ref.shape)
    idx[axis] = pl.ds(pl.multiple_of(d * size, size), size)
    return ref.at[tuple(idx)]


def _comm_call(body, n, ins, out_shapes, name):
    hbm = pl.BlockSpec(memory_space=pl.ANY)
    return pl.pallas_call(
        body, name=name, in_specs=[hbm] * n, out_specs=[hbm] * n, out_shape=out_shapes,
        scratch_shapes=[pltpu.SemaphoreType.DMA((N_DEV - 1, n)), pltpu.SemaphoreType.DMA((N_DEV - 1, n)),
                        pltpu.SemaphoreType.DMA((n,))],
    )(*ins)


def _run_exchange(n, local, remote, send_sems, recv_sems):
    me = _me_xyc()
    locs = [local(p) for p in range(n)]
    for cp in locs:
        cp.start()
    sends = [remote(k, p, me, _flip(me, k)) for k in range(1, N_DEV) for p in range(n)]
    for cp in sends:
        cp.start()
    for k in range(1, N_DEV):
        for p in range(n):
            remote(k, p, _flip(me, k), me).wait_recv()
    for cp in sends:
        cp.wait_send()
    for cp in locs:
        cp.wait()


def _all_gather(shards, axes, name):
    n = len(shards)
    sizes = [s.shape[a] for s, a in zip(shards, axes)]

    def body(*refs):
        ins, outs = refs[:n], refs[n:2 * n]
        send_sems, recv_sems, loc_sems = refs[2 * n:]
        me = _me_xyc()

        def local(p):
            return pltpu.make_async_copy(ins[p], _window(outs[p], axes[p], sizes[p], _index(me)), loc_sems.at[p])

        def remote(k, p, owner, to):
            return pltpu.make_async_remote_copy(
                src_ref=ins[p], dst_ref=_window(outs[p], axes[p], sizes[p], _index(owner)),
                send_sem=send_sems.at[k - 1, p], recv_sem=recv_sems.at[k - 1, p], device_id=to, device_id_type=MESH)

        _run_exchange(n, local, remote, send_sems, recv_sems)

    def full(s, a):
        return s.shape[:a] + (N_DEV * s.shape[a],) + s.shape[a + 1:]

    return _comm_call(body, n, shards, [jax.ShapeDtypeStruct(full(s, a), s.dtype) for s, a in zip(shards, axes)], name)


def _exchange(fulls, axes, name):
    n = len(fulls)
    sizes = [None if a is None else f.shape[a] // N_DEV for f, a in zip(fulls, axes)]

    def part_shape(f, a):
        return f.shape if a is None else f.shape[:a] + (f.shape[a] // N_DEV,) + f.shape[a + 1:]

    def body(*refs):
        ins, outs = refs[:n], refs[n:2 * n]
        send_sems, recv_sems, loc_sems = refs[2 * n:]
        me = _me_xyc()

        def src(p, to):
            return ins[p] if axes[p] is None else _window(ins[p], axes[p], sizes[p], _index(to))

        def local(p):
            return pltpu.make_async_copy(src(p, me), outs[p].at[_index(me)], loc_sems.at[p])

        def remote(k, p, sender, to):
            return pltpu.make_async_remote_copy(
                src_ref=src(p, to), dst_ref=outs[p].at[_index(sender)],
                send_sem=send_sems.at[k - 1, p], recv_sem=recv_sems.at[k - 1, p], device_id=to, device_id_type=MESH)

        _run_exchange(n, local, remote, send_sems, recv_sems)

    return _comm_call(body, n, fulls,
                      [jax.ShapeDtypeStruct((N_DEV,) + part_shape(f, a), f.dtype) for f, a in zip(fulls, axes)], name)


_HBM = pl.BlockSpec(memory_space=pltpu.HBM)
_SEM = pl.BlockSpec(memory_space=pltpu.SEMAPHORE)
_EFFECT = pltpu.SideEffectType.DATAFLOW_SIDE_EFFECTING


def _in_hbm(a):
    return pltpu.with_memory_space_constraint(a, pltpu.HBM)


def _split_start(srcs, lands, after, descr, name):
    n = len(srcs)

    def body(*refs):
        ins, lnd = refs[:n], refs[n:2 * n]
        send_sems, recv_sems = refs[2 * n + 1], refs[2 * n + 2]
        token = refs[-1]
        me = _me_xyc()
        for k in range(1, N_DEV):
            for p in range(n):
                descr(k, p, ins, lnd, send_sems, recv_sems, me, _flip(me, k)).start()
        token[...] = jnp.zeros_like(token)

    sems = pltpu.SemaphoreType.DMA(((N_DEV - 1) * n,))
    res = pl.pallas_call(
        body, name=name,
        out_shape=(sems, sems, *[pltpu.HBM(a.shape, a.dtype) for a in (*srcs, *lands)],
                   jax.ShapeDtypeStruct((8, 128), F32)),
        in_specs=[_HBM] * (2 * n) + [pl.BlockSpec(memory_space=pl.ANY)],
        out_specs=(_SEM, _SEM, *[_HBM] * (2 * n), pl.BlockSpec(memory_space=pltpu.VMEM)),
        input_output_aliases={i: 2 + i for i in range(2 * n)},
        compiler_params=pltpu.CompilerParams(has_side_effects=_EFFECT),
    )(*[_in_hbm(a) for a in (*srcs, *lands)], after)
    return res[0], res[1], res[2:2 + n], res[2 + n:2 + 2 * n], res[-1]


def _split_wait(send_sems, recv_sems, srcs, lands, after, descr, name):
    n = len(srcs)

    def body(*refs):
        ins, lnd = refs[:n], refs[n:2 * n]
        s_sems, r_sems = refs[2 * n], refs[2 * n + 1]
        me = _me_xyc()
        for k in range(1, N_DEV):
            for p in range(n):
                peer = _flip(me, k)
                descr(k, p, ins, lnd, s_sems, r_sems, me, peer).wait_send()
                descr(k, p, ins, lnd, s_sems, r_sems, peer, me).wait_recv()

    res = pl.pallas_call(
        body, name=name,
        out_shape=tuple(pltpu.HBM(a.shape, a.dtype) for a in (*srcs, *lands)),
        in_specs=[_HBM] * (2 * n) + [_SEM, _SEM, pl.BlockSpec(memory_space=pl.ANY)],
        out_specs=tuple([_HBM] * (2 * n)),
        input_output_aliases={i: i for i in range(2 * n)},
        compiler_params=pltpu.CompilerParams(has_side_effects=_EFFECT),
    )(*srcs, *lands, send_sems, recv_sems, after)
    return list(res[n:])


def _gather_descr(axes, sizes):
    def descr(k, p, ins, lnd, send_sems, recv_sems, sender, dest):
        return pltpu.make_async_remote_copy(
            src_ref=ins[p], dst_ref=_window(lnd[p], axes[p], sizes[p], _index(sender)),
            send_sem=send_sems.at[(k - 1) * len(axes) + p], recv_sem=recv_sems.at[(k - 1) * len(axes) + p],
            device_id=dest, device_id_type=MESH)
    return descr


def _exchange_descr(axes, sizes):
    def descr(k, p, ins, lnd, send_sems, recv_sems, sender, dest):
        return pltpu.make_async_remote_copy(
            src_ref=_window(ins[p], axes[p], sizes[p], _index(dest)), dst_ref=lnd[p].at[_index(sender)],
            send_sem=send_sems.at[(k - 1) * len(axes) + p], recv_sem=recv_sems.at[(k - 1) * len(axes) + p],
            device_id=dest, device_id_type=MESH)
    return descr


def _gather_begin(shards, axes, after, name):
    sizes = [s.shape[a] for s, a in zip(shards, axes)]
    me = _index(_me_xyc())
    lands = []
    for s, a, sz in zip(shards, axes, sizes):
        full = s.shape[:a] + (N_DEV * sz,) + s.shape[a + 1:]
        lands.append(lax.dynamic_update_slice_in_dim(lax.empty(full, s.dtype), s, me * sz, a))
    descr = _gather_descr(axes, sizes)
    send_sems, recv_sems, srcs, lands, token = _split_start(shards, lands, after, descr, name)
    return (send_sems, recv_sems, srcs, lands, descr), token


def _exchange_begin(fulls, axes, after, name):
    sizes = [f.shape[a] // N_DEV for f, a in zip(fulls, axes)]
    me = _index(_me_xyc())
    lands = []
    for f, a, sz in zip(fulls, axes, sizes):
        own = lax.dynamic_slice_in_dim(f, me * sz, sz, a)
        lands.append(lax.dynamic_update_slice_in_dim(lax.empty((N_DEV,) + own.shape, f.dtype), own[None], me, 0))
    descr = _exchange_descr(axes, sizes)
    send_sems, recv_sems, srcs, lands, token = _split_start(fulls, lands, after, descr, name)
    return (send_sems, recv_sems, srcs, lands, descr), token


def _split_end(handle, after, name):
    send_sems, recv_sems, srcs, lands, descr = handle
    return _split_wait(send_sems, recv_sems, srcs, lands, after, descr, name)


def _adam_math(g, w, m, v):
    m = ADAM_B1 * m + (1.0 - ADAM_B1) * g
    v = ADAM_B2 * v + (1.0 - ADAM_B2) * (g * g)
    m_hat = m / (1.0 - ADAM_B1 ** ADAM_STEP)
    v_hat = v / (1.0 - ADAM_B2 ** ADAM_STEP)
    delta = -ADAM_LR * (m_hat / (jnp.sqrt(v_hat) + ADAM_EPS) + ADAM_WD * w)
    return delta, m, v


def _sum_slabs(r_ref):
    g = r_ref[0].astype(F32)
    for s in range(1, N_DEV):
        g = g + r_ref[s].astype(F32)
    return g


def _adamw_reduce(recv, w, m, v, tb, name):
    r, c = w.shape

    def body(r_ref, w_ref, m_ref, v_ref, g_ref, d_ref, nm_ref, nv_ref):
        g = _sum_slabs(r_ref)
        g_ref[...] = g
        d_ref[...], nm_ref[...], nv_ref[...] = _adam_math(g, w_ref[...], m_ref[...], v_ref[...])

    blk = pl.BlockSpec((tb, c), lambda i: (i, 0))
    return pl.pallas_call(
        body, name=name, grid=(r // tb,),
        in_specs=[pl.BlockSpec((N_DEV, tb, c), lambda i: (0, i, 0)), blk, blk, blk],
        out_specs=[blk] * 4, out_shape=[jax.ShapeDtypeStruct((r, c), F32)] * 4,
        compiler_params=_cparams(("parallel",)),
    )(recv, w, m, v)


def _reduce8(recv, tb, name):
    r, c = recv.shape[1:]

    def body(r_ref, g_ref):
        g_ref[...] = _sum_slabs(r_ref)

    return pl.pallas_call(
        body, name=name, grid=(r // tb,),
        in_specs=[pl.BlockSpec((N_DEV, tb, c), lambda i: (0, i, 0))],
        out_specs=pl.BlockSpec((tb, c), lambda i: (i, 0)), out_shape=jax.ShapeDtypeStruct((r, c), F32),
        compiler_params=_cparams(("parallel",)),
    )(recv)


def _adamw(g, w, m, v, tb, name):
    r, c = w.shape

    def body(g_ref, w_ref, m_ref, v_ref, d_ref, nm_ref, nv_ref):
        d_ref[...], nm_ref[...], nv_ref[...] = _adam_math(g_ref[...], w_ref[...], m_ref[...], v_ref[...])

    blk = pl.BlockSpec((tb, c), lambda i: (i, 0))
    return pl.pallas_call(
        body, name=name, grid=(r // tb,), in_specs=[blk] * 4, out_specs=[blk] * 3,
        out_shape=[jax.ShapeDtypeStruct((r, c), F32)] * 3, compiler_params=_cparams(("parallel",)),
    )(g, w, m, v)


DN_IN_SHARD = DN_IN // N_DEV
DN_IN_SHARD_PAD = 432
CONV_SHARD = (1, DN_CONV, 288)


def _pack_small(arrs, rows):
    flat = jnp.concatenate([a.astype(F32).reshape(-1) for a in arrs])
    return jnp.pad(flat, (0, rows * PACK_C - flat.size)).reshape(rows, PACK_C)


def _unpack_small(packed, shapes):
    flat, out, off = packed.reshape(-1), [], 0
    for shp in shapes:
        n = int(np.prod(shp))
        out.append(flat[off:off + n].reshape(shp))
        off += n
    return out


def kernel(x, mem, rel_bias, att_w_in, att_w_out, dn_w_in, dn_conv, dn_a_log, dn_dt_bias, dn_out_norm, dn_w_out, mem_norm, mem_w_kv, norm_mix_pre, norm_mix_post, norm_ffn_pre, norm_ffn_post, ffn_w_gate_up, ffn_w_down, loss_target, m_rel_bias, m_att_w_in, m_att_w_out, m_dn_w_in, m_dn_conv, m_dn_a_log, m_dn_dt_bias, m_dn_out_norm, m_dn_w_out, m_mem_norm, m_mem_w_kv, m_norm_mix_pre, m_norm_mix_post, m_norm_ffn_pre, m_norm_ffn_post, m_ffn_w_gate_up, m_ffn_w_down, v_rel_bias, v_att_w_in, v_att_w_out, v_dn_w_in, v_dn_conv, v_dn_a_log, v_dn_dt_bias, v_dn_out_norm, v_dn_w_out, v_mem_norm, v_mem_w_kv, v_norm_mix_pre, v_norm_mix_post, v_norm_ffn_pre, v_norm_ffn_post, v_ffn_w_gate_up, v_ffn_w_down):
    x0, mem0, tgt = x[0], mem[0], loss_target[0]
    t = x0.shape[0]
    axes = ("x", "y", "c")

    def t_shard(w):
        return jnp.swapaxes(w, 1, 2).astype(BF16)

    dn_in_pad = ((0, 0), (0, DN_IN_SHARD_PAD - DN_IN_SHARD), (0, 0))
    (w_att_in_t,) = _all_gather([t_shard(att_w_in)], [1], "allgather_first")
    w_att_in_t = w_att_in_t[0]
    gu_t, down = t_shard(ffn_w_gate_up), ffn_w_down.astype(BF16)
    gather_o, tok_o = _gather_begin([att_w_out.astype(BF16), mem_w_kv.astype(BF16)], [1, 1], w_att_in_t,
                                    "gather_att_out_start")
    gather_a, tok_a = _gather_begin([gu_t[0:1], down[0:1]], [1, 1], tok_o, "gather_ffn0_start")
    gather_b, tok_b = _gather_begin(
        [jnp.pad(t_shard(dn_w_in), dn_in_pad), dn_w_out.astype(BF16), gu_t[1:2], down[1:2], dn_conv],
        [1, 1, 1, 1, 0], tok_a, "gather_layer1_start")

    def gain(a, i):
        return a[i].reshape(1, D)

    (h0,) = _rowwise(_fn_pre, [x0], [gain(norm_mix_pre, 0) + tok_b[0:1, 0:1]], [(D, BF16)], ROW_TB, "pre0")
    p0 = _matmul(h0, w_att_in_t, "nt", BF16, "att_in")
    late = {}

    def kv0(after):
        late["w_att_out"], late["w_kv"] = _split_end(gather_o, after, "gather_att_out_wait")
        return _mem_kv_fwd(mem0, gain(mem_norm, 0), late["w_kv"][0], 0)

    cat0, res0, (km0, vm0, memn0) = _attn_mixer_fwd(p0, rel_bias, kv0)
    w_att_out, w_kv = late["w_att_out"][0], late["w_kv"]
    y0 = _matmul(cat0, w_att_out, "nn", F32, "att_out")
    g_a = [gain(norm_mix_post, 0), gain(norm_ffn_pre, 0)]
    x1, h1 = _rowwise(_fn_res_pre, [x0, y0], g_a, [(D, F32), (D, BF16)], ROW_TB, "res_pre0")
    w_gu_t0, w_down0 = [w[0] for w in _split_end(gather_a, h1, "gather_ffn0_wait")]
    f0, gu0, a0 = _ffn_fwd(h1, w_gu_t0, w_down0, 0)
    g_b = [gain(norm_ffn_post, 0), gain(norm_mix_pre, 1)]
    x2, h2 = _rowwise(_fn_res_pre, [x1, f0], g_b, [(D, F32), (D, BF16)], ROW_TB, "res_pre1")
    km1, vm1, memn1 = _mem_kv_fwd(mem0, gain(mem_norm, 1), w_kv[1], 1)
    w_dn_in_g, w_dn_out, w_gu_t1, w_down1, conv_g = _split_end(gather_b, h2, "gather_layer1_wait")
    w_dn_in_g, w_dn_out, w_gu_t1, w_down1 = w_dn_in_g[0], w_dn_out[0], w_gu_t1[0], w_down1[0]
    conv_full = conv_g.transpose(1, 0, 2).reshape(DN_CONV, 3 * TOK_W)
    w_dn_in_t = jnp.concatenate(
        [w_dn_in_g[DN_IN_SHARD_PAD * j:DN_IN_SHARD_PAD * j + DN_IN_SHARD] for j in range(N_DEV)]
        + [jnp.zeros((DN_IN_PAD - DN_IN, D), BF16)], axis=0)
    p1 = _matmul(h2, w_dn_in_t, "nt", F32, "dn_in")
    cat1, res1 = _dn_mixer_fwd(p1, conv_full, dn_a_log[0], dn_dt_bias[0], dn_out_norm[0], km1, vm1)
    y1 = _matmul(cat1, w_dn_out, "nn", F32, "dn_out")
    g_c = [gain(norm_mix_post, 1), gain(norm_ffn_pre, 1)]
    x3, h3 = _rowwise(_fn_res_pre, [x2, y1], g_c, [(D, F32), (D, BF16)], ROW_TB, "res_pre2")
    f1, gu1, a1 = _ffn_fwd(h3, w_gu_t1, w_down1, 1)
    g_d = [gain(norm_ffn_post, 1)]
    (x4,) = _rowwise(_fn_res, [x3, f1], g_d, [(D, F32)], ROW_TB, "res3")
    dx4, lrow = _loss_kernel(x4, tgt, ROW_TB, "loss")
    loss = lax.psum(lrow[0, 0] * (0.5 / D), axes)

    (df1,), (dg_fpost1,) = _rowwise_bwd(_fn_res, [x3, f1], g_d, [dx4], [None, BF16], ROW_TB, "res3_bwd")
    dh3, dwgu1, dwd1 = _ffn_bwd(df1, h3, w_gu_t1, w_down1, gu1, a1, 1)
    (dx2, dy1), (dg_mpost1, dg_fpre1) = _rowwise_bwd(_fn_res_pre, [x2, y1], g_c, [dx4, dh3], [F32, BF16],
                                                     ROW_TB, "res_pre2_bwd")
    dcat1 = _matmul(dy1, w_dn_out, "nt", F32, "dn_out_dx")
    dw_dn_out = _matmul(cat1, dy1, "tn", BF16, "dn_out_dw")
    dp1, dconv, da_log, ddt_bias, dout_norm, dkm1, dvm1 = _dn_mixer_bwd(dcat1, res1, km1, vm1)
    dwkv1, dg_mem1 = _mem_kv_bwd(mem0, gain(mem_norm, 1), w_kv[1], memn1, dkm1, dvm1, 1)
    dh2 = _matmul(dp1, w_dn_in_t, "nn", F32, "dn_in_dx")
    dw_dn_in_t = _matmul(dp1, h2, "tn", BF16, "dn_in_dw")
    dn_in_parts = [jnp.pad(dw_dn_in_t[DN_IN_SHARD * j:DN_IN_SHARD * (j + 1)],
                           ((0, DN_IN_SHARD_PAD - DN_IN_SHARD), (0, 0))) for j in range(N_DEV)]
    xch_b, tok = _exchange_begin(
        [jnp.concatenate(dn_in_parts, axis=0)[None], dw_dn_out[None], dwkv1[None], dwgu1[None], dwd1[None]],
        [1, 1, 1, 1, 1], dh2, "exchange_layer1_start")
    (dx1, df0), (dg_fpost0, dg_mpre1) = _rowwise_bwd(_fn_res_pre, [x1, f0], [g + tok[0:1, 0:1] for g in g_b],
                                                     [dx2, dh2], [F32, BF16], ROW_TB, "res_pre1_bwd")
    dh1, dwgu0, dwd0 = _ffn_bwd(df0, h1, w_gu_t0, w_down0, gu0, a0, 0)
    xch_a, tok = _exchange_begin([dwgu0[None], dwd0[None]], [1, 1], dh1, "exchange_ffn0_start")
    (dx0, dy0), (dg_mpost0, dg_fpre0) = _rowwise_bwd(_fn_res_pre, [x0, y0], [g + tok[0:1, 0:1] for g in g_a],
                                                     [dx1, dh1], [F32, BF16], ROW_TB, "res_pre0_bwd")
    dcat0 = _matmul(dy0, w_att_out, "nt", F32, "att_out_dx")
    dw_att_out = _matmul(cat0, dy0, "tn", BF16, "att_out_dw")
    dp0, drel, dkm0, dvm0 = _attn_mixer_bwd(dcat0, res0, km0, vm0)
    dwkv0, dg_mem0 = _mem_kv_bwd(mem0, gain(mem_norm, 0), w_kv[0], memn0, dkm0, dvm0, 0)
    xch_o, tok = _exchange_begin([dw_att_out[None], dwkv0[None]], [1, 1], dp0, "exchange_att_out_start")
    dh0 = _matmul(dp0, w_att_in_t, "nn", F32, "att_in_dx")
    dw_att_in_t = _matmul(dp0, h0, "tn", BF16, "att_in_dw")
    (grad_x,), (dg_mpre0,) = _rowwise_bwd(_fn_first, [x0], [gain(norm_mix_pre, 0) + tok[0:1, 0:1]], [dx0, dh0],
                                          [F32], ROW_TB, "pre0_bwd")

    small_grads = [drel, da_log, ddt_bias, dout_norm, jnp.concatenate([dg_mem0, dg_mem1]),
                   jnp.concatenate([dg_mpre0, dg_mpre1]), jnp.concatenate([dg_mpost0, dg_mpost1]),
                   jnp.concatenate([dg_fpre0, dg_fpre1]), jnp.concatenate([dg_fpost0, dg_fpost1]), dconv]
    r_att_in, r_small = _exchange([dw_att_in_t[None], _pack_small(small_grads, SMALL_ROWS)], [1, None],
                                  "exchange_last")
    r_att_out, r_kv0 = _split_end(xch_o, r_small, "exchange_att_out_wait")
    r_gu0, r_down0 = _split_end(xch_a, r_small, "exchange_ffn0_wait")
    r_dn_in, r_dn_out, r_kv1, r_gu1, r_down1 = _split_end(xch_b, r_small, "exchange_layer1_wait")

    def rows(a):
        return a.reshape((-1,) + a.shape[-1:])

    def row_sharded(recv, w, m, v, tb, name):
        outs = _adamw_reduce(recv.reshape((N_DEV, -1) + recv.shape[-1:]), rows(w), rows(m), rows(v), tb, name)
        return [o.reshape(w.shape) for o in outs]

    def col_sharded(recv, w, m, v, tb, name):
        g_t = _reduce8(recv.reshape((N_DEV, -1) + recv.shape[-1:]), tb, name + "_sum")
        g = jnp.swapaxes(g_t.reshape(recv.shape[1:])[:, :w.shape[2]], 1, 2)
        outs = _adamw(rows(g), rows(w), rows(m), rows(v), 256, name)
        return [g] + [o.reshape(w.shape) for o in outs]

    def per_layer(fn, recvs, w, m, v, tb, name):
        outs = [fn(r, w[l:l + 1], m[l:l + 1], v[l:l + 1], tb, f"{name}{l}") for l, r in enumerate(recvs)]
        return [jnp.concatenate(pair, axis=0) for pair in zip(*outs)]

    big = [col_sharded(r_att_in, att_w_in, m_att_w_in, v_att_w_in, 320, "adamw_att_in"),
           row_sharded(r_att_out, att_w_out, m_att_w_out, v_att_w_out, 128, "adamw_att_out"),
           col_sharded(r_dn_in, dn_w_in, m_dn_w_in, v_dn_w_in, 432, "adamw_dn_in"),
           row_sharded(r_dn_out, dn_w_out, m_dn_w_out, v_dn_w_out, 128, "adamw_dn_out"),
           per_layer(row_sharded, [r_kv0, r_kv1], mem_w_kv, m_mem_w_kv, v_mem_w_kv, 128, "adamw_mem_kv"),
           per_layer(col_sharded, [r_gu0, r_gu1], ffn_w_gate_up, m_ffn_w_gate_up, v_ffn_w_gate_up, 176,
                     "adamw_ffn_gu"),
           per_layer(row_sharded, [r_down0, r_down1], ffn_w_down, m_ffn_w_down, v_ffn_w_down, 176,
                     "adamw_ffn_down")]
    g_big, d_big, nm_big, nv_big = [[b[i] for b in big] for i in range(4)]

    g_small = _reduce8(r_small, SMALL_ROWS, "reduce_small")
    rep_shapes = [(32, 12), (1, 2, 6), (1, 2, 6), (1, 128), (2, D), (2, D), (2, D), (2, D), (2, D)]
    *g_rep, g_conv_full = _unpack_small(g_small, rep_shapes + [(DN_CONV, 3 * TOK_W)])
    me = _index(_me_xyc())
    g_conv = lax.dynamic_slice(g_conv_full, (0, me * 288), (DN_CONV, 288)).reshape(CONV_SHARD)
    small_shapes = rep_shapes + [CONV_SHARD]
    small_w = [rel_bias, dn_a_log, dn_dt_bias, dn_out_norm, mem_norm, norm_mix_pre, norm_mix_post,
               norm_ffn_pre, norm_ffn_post, dn_conv]
    small_m = [m_rel_bias, m_dn_a_log, m_dn_dt_bias, m_dn_out_norm, m_mem_norm, m_norm_mix_pre, m_norm_mix_post,
               m_norm_ffn_pre, m_norm_ffn_post, m_dn_conv]
    small_v = [v_rel_bias, v_dn_a_log, v_dn_dt_bias, v_dn_out_norm, v_mem_norm, v_norm_mix_pre, v_norm_mix_post,
               v_norm_ffn_pre, v_norm_ffn_post, v_dn_conv]
    g_small_list = g_rep + [g_conv]
    outs_small = _adamw(_pack_small(g_small_list, 24), _pack_small(small_w, 24), _pack_small(small_m, 24),
                        _pack_small(small_v, 24), 24, "adamw_small")
    d_small, nm_small, nv_small = [_unpack_small(o, small_shapes) for o in outs_small]

    def ordered(small, big):
        return [small[0], big[0], big[1], big[2], small[9], small[1], small[2], small[3], big[3], small[4],
                big[4], small[5], small[6], small[7], small[8], big[5], big[6]]

    g_small_out = [g.reshape(s) for g, s in zip(g_small_list, small_shapes)]
    return (loss, grad_x[None], *ordered(g_small_out, g_big), *ordered(d_small, d_big),
            *ordered(nm_small, nm_big), *ordered(nv_small, nv_big))
```

```python
import functools
import math

import numpy as np
import jax
import jax.numpy as jnp
from jax import lax
from jax.experimental import pallas as pl
from jax.experimental.pallas import tpu as pltpu

F32 = jnp.float32
BF16 = jnp.bfloat16
HI = lax.Precision.HIGHEST
MESH = pl.DeviceIdType.MESH

N_DEV = 8
D = 1024
EPS = 1e-6
NEG = -1e30
TOK_W = 768
MEM_W = 256
ATT_HD = 64
DIL_GROUPS = ((128, 1), (512, 4), (2048, 16))
BAND_HALF = 64
REL_BUCKETS = 32
REL_MAX_DIST = 1024
DN_HD = 128
DN_HEADS = 6
DN_CONV = 5
DN_CHUNK = 64
MEM_HEADS = 4
D_FF = 2816
ATT_IN = 2560
DN_IN = 3352
DN_IN_PAD = 3456

ADAM_LR, ADAM_B1, ADAM_B2, ADAM_EPS, ADAM_WD, ADAM_STEP = 0.001, 0.9, 0.999, 1e-08, 0.01, 10

PACK_C = 512
BIG_ROWS = 6480
SMALL_ROWS = 48
VMEM_LIMIT = 48 * 1024 * 1024


def _cparams(sem=None):
    kw = dict(vmem_limit_bytes=VMEM_LIMIT)
    if sem is not None:
        kw["dimension_semantics"] = sem
    return pltpu.CompilerParams(**kw)


def _tile(n, cap):
    if n <= cap:
        return n
    best = None
    for t in range(128, cap + 1, 128):
        if n % t == 0:
            best = t
    assert best is not None, (n, cap)
    return best


def _matmul(a, b, mode, out_dtype, name, tm=1024, tn=1408, tk=None):
    if tk is None:
        tk = 2048 if mode == "tn" else 2816
    if mode == "nn":
        (m, kc), (_, n) = a.shape, b.shape
        dims = (((1,), (0,)), ((), ()))
    elif mode == "nt":
        (m, kc), (n, _) = a.shape, b.shape
        dims = (((1,), (1,)), ((), ()))
    else:
        (kc, m), (_, n) = a.shape, b.shape
        dims = (((0,), (0,)), ((), ()))
    tm = m if m <= tm else _tile(m, tm)
    tn = _tile(n, tn)
    tk = _tile(kc, tk)
    nk = kc // tk

    def body(a_ref, b_ref, o_ref, acc_ref):
        k = pl.program_id(2)
        part = lax.dot_general(a_ref[...], b_ref[...], dims, preferred_element_type=F32)

        @pl.when(k == 0)
        def _():
            acc_ref[...] = part

        @pl.when(k > 0)
        def _():
            acc_ref[...] += part

        @pl.when(k == nk - 1)
        def _():
            o_ref[...] = acc_ref[...].astype(o_ref.dtype)

    if mode == "nn":
        a_spec = pl.BlockSpec((tm, tk), lambda i, j, k: (i, k))
        b_spec = pl.BlockSpec((tk, tn), lambda i, j, k: (k, j))
    elif mode == "nt":
        a_spec = pl.BlockSpec((tm, tk), lambda i, j, k: (i, k))
        b_spec = pl.BlockSpec((tn, tk), lambda i, j, k: (j, k))
    else:
        a_spec = pl.BlockSpec((tk, tm), lambda i, j, k: (k, i))
        b_spec = pl.BlockSpec((tk, tn), lambda i, j, k: (k, j))
    return pl.pallas_call(
        body, name=name, grid=(m // tm, n // tn, nk),
        in_specs=[a_spec, b_spec],
        out_specs=pl.BlockSpec((tm, tn), lambda i, j, k: (i, j)),
        out_shape=jax.ShapeDtypeStruct((m, n), out_dtype),
        scratch_shapes=[pltpu.VMEM((tm, tn), F32)],
        compiler_params=_cparams(("parallel", "parallel", "arbitrary")),
    )(a, b)


def _rowwise(fn, rows, params, outs, tb, name):
    t = rows[0].shape[0]
    nr, npar = len(rows), len(params)

    def body(*refs):
        ins = [r[...].astype(F32) for r in refs[:nr + npar]]
        res = fn(*ins)
        for o_ref, r in zip(refs[nr + npar:], res):
            o_ref[...] = r.astype(o_ref.dtype)

    return pl.pallas_call(
        body, name=name, grid=(t // tb,),
        in_specs=[pl.BlockSpec((tb, r.shape[1]), lambda i: (i, 0)) for r in rows]
        + [pl.BlockSpec(p.shape, lambda i: (0, 0)) for p in params],
        out_specs=[pl.BlockSpec((tb, c), lambda i: (i, 0)) for c, _ in outs],
        out_shape=[jax.ShapeDtypeStruct((t, c), dt) for c, dt in outs],
        compiler_params=_cparams(("parallel",)),
    )(*rows, *params)


def _rowwise_bwd(fn, rows, params, cots, row_grad, tb, name):
    t = rows[0].shape[0]
    nr, npar, nc = len(rows), len(params), len(cots)
    want = [i for i, g in enumerate(row_grad) if g is not None]

    def body(*refs):
        ins = [r[...].astype(F32) for r in refs[:nr + npar]]
        cts = tuple(r[...].astype(F32) for r in refs[nr + npar:nr + npar + nc])
        outs = refs[nr + npar + nc:]
        _, vjp = jax.vjp(fn, *ins)
        grads = vjp(cts)
        for o_ref, i in zip(outs[:len(want)], want):
            o_ref[...] = grads[i].astype(o_ref.dtype)
        first = pl.program_id(0) == 0
        for o_ref, g in zip(outs[len(want):], grads[nr:]):
            @pl.when(first)
            def _(o_ref=o_ref, g=g):
                o_ref[...] = g

            @pl.when(jnp.logical_not(first))
            def _(o_ref=o_ref, g=g):
                o_ref[...] += g

    res = pl.pallas_call(
        body, name=name, grid=(t // tb,),
        in_specs=[pl.BlockSpec((tb, r.shape[1]), lambda i: (i, 0)) for r in rows]
        + [pl.BlockSpec(p.shape, lambda i: (0, 0)) for p in params]
        + [pl.BlockSpec((tb, c.shape[1]), lambda i: (i, 0)) for c in cots],
        out_specs=[pl.BlockSpec((tb, rows[i].shape[1]), lambda i_: (i_, 0)) for i in want]
        + [pl.BlockSpec(p.shape, lambda i: (0, 0)) for p in params],
        out_shape=[jax.ShapeDtypeStruct(rows[i].shape, row_grad[i]) for i in want]
        + [jax.ShapeDtypeStruct(p.shape, F32) for p in params],
        compiler_params=_cparams(("arbitrary",)),
    )(*rows, *params, *cots)
    return list(res[:len(want)]), list(res[len(want):])


def _rms(x, g):
    return x * lax.rsqrt(jnp.mean(x * x, axis=-1, keepdims=True) + EPS) * g


def _fn_pre(x, g):
    return (_rms(x, g),)


def _fn_res_pre(x, y, g_post, g_pre):
    x1 = x + _rms(y, g_post)
    return x1, _rms(x1, g_pre)


def _fn_res(x, y, g_post):
    return (x + _rms(y, g_post),)


def _sigmoid(x):
    return 1.0 / (1.0 + jnp.exp(-x))


def _silu(x):
    return x * _sigmoid(x)


def _fn_swiglu(gu):
    return (_silu(gu[:, :D_FF]) * gu[:, D_FF:],)


def _fn_combine(o, lse):
    ls = [lse[:, 256 * g:256 * (g + 1)] for g in range(3)]
    mx = lax.stop_gradient(jnp.maximum(jnp.maximum(ls[0], ls[1]), ls[2]))
    es = [jnp.exp(l - mx) for l in ls]
    inv = 1.0 / (es[0] + es[1] + es[2])
    return (jnp.concatenate([o[:, 256 * g:256 * (g + 1)] * (es[g] * inv) for g in range(3)], axis=1),)


def _fn_outnorm(o_f, o_r, z, gain):
    res = []
    for h in range(DN_HEADS):
        sl = slice(DN_HD * h, DN_HD * (h + 1))
        o = o_f[:, sl] + o_r[:, sl]
        res.append(o * lax.rsqrt(jnp.mean(o * o, axis=-1, keepdims=True) + EPS) * gain * _silu(z[:, sl]))
    return (jnp.concatenate(res, axis=1),)


def _loss_kernel(x, tgt, tb, name):
    t, d = x.shape

    def body(x_ref, t_ref, dx_ref, l_ref, acc_ref):
        i = pl.program_id(0)
        e = x_ref[...] - t_ref[...]
        dx_ref[...] = e * (1.0 / d)
        part = jnp.sum(e * e, axis=0, keepdims=True)

        @pl.when(i == 0)
        def _():
            acc_ref[...] = part

        @pl.when(i > 0)
        def _():
            acc_ref[...] += part

        @pl.when(i == t // tb - 1)
        def _():
            l_ref[...] = jnp.broadcast_to(jnp.sum(acc_ref[...], axis=-1, keepdims=True), (1, 128))

    return pl.pallas_call(
        body, name=name, grid=(t // tb,),
        in_specs=[pl.BlockSpec((tb, d), lambda i: (i, 0))] * 2,
        out_specs=[pl.BlockSpec((tb, d), lambda i: (i, 0)), pl.BlockSpec((1, 128), lambda i: (0, 0))],
        out_shape=[jax.ShapeDtypeStruct((t, d), F32), jax.ShapeDtypeStruct((1, 128), F32)],
        scratch_shapes=[pltpu.VMEM((1, d), F32)],
        compiler_params=_cparams(("arbitrary",)),
    )(x, tgt)


def _band_fn(l_sub, bq, i, q, kw, vw, bm):
    w = bq + 2 * BAND_HALF
    s = lax.dot_general((q * (ATT_HD ** -0.5)).astype(BF16), kw.astype(BF16), (((1,), (1,)), ((), ())),
                        preferred_element_type=F32) + bm
    kpos = i * bq - BAND_HALF + lax.broadcasted_iota(jnp.int32, (bq, w), 1)
    s = jnp.where((kpos >= 0) & (kpos < l_sub), s, NEG)
    m = lax.stop_gradient(jnp.max(s, axis=-1, keepdims=True))
    p = jnp.exp(s - m)
    den = jnp.sum(p, axis=-1, keepdims=True)
    o = jnp.dot(p.astype(BF16), vw.astype(BF16), preferred_element_type=F32) / den
    return o, jnp.broadcast_to(m + jnp.log(den), o.shape)


def _band_specs(dil, l_sub, bq):
    w = bq + 2 * BAND_HALF
    qs = pl.BlockSpec((None, bq, ATT_HD), lambda h, r, i: (h * dil + r, i, 0))
    ks = pl.BlockSpec((None, l_sub + 2 * BAND_HALF, ATT_HD), lambda h, r, i: (h * dil + r, 0, 0))
    bs = pl.BlockSpec((None, bq, w), lambda h, r, i: (h, 0, 0))
    return qs, ks, bs


def _band_fwd(q, k, v, bm, dil, l_sub, bq, name):
    w = bq + 2 * BAND_HALF
    qs, ks, bs = _band_specs(dil, l_sub, bq)

    def body(q_ref, k_ref, v_ref, bm_ref, o_ref, l_ref):
        i = pl.program_id(2)
        st = pl.multiple_of(i * bq, bq)
        o, lse = _band_fn(l_sub, bq, i, q_ref[...].astype(F32), k_ref[pl.ds(st, w), :].astype(F32),
                          v_ref[pl.ds(st, w), :].astype(F32), bm_ref[...])
        o_ref[...] = o
        l_ref[...] = lse

    return pl.pallas_call(
        body, name=name, grid=(4, dil, l_sub // bq),
        in_specs=[qs, ks, ks, bs], out_specs=[qs, qs],
        out_shape=[jax.ShapeDtypeStruct(q.shape, F32)] * 2,
        compiler_params=_cparams(("parallel", "parallel", "arbitrary")),
    )(q, k, v, bm)


def _band_bwd(q, k, v, bm, do, dlse, dil, l_sub, bq, name):
    w = bq + 2 * BAND_HALF
    qs, ks, bs = _band_specs(dil, l_sub, bq)

    def body(q_ref, k_ref, v_ref, bm_ref, do_ref, dl_ref, dq_ref, dk_ref, dv_ref, dbm_ref):
        r, i = pl.program_id(1), pl.program_id(2)
        st = pl.multiple_of(i * bq, bq)
        _, vjp = jax.vjp(functools.partial(_band_fn, l_sub, bq, i),
                         q_ref[...].astype(F32), k_ref[pl.ds(st, w), :].astype(F32),
                         v_ref[pl.ds(st, w), :].astype(F32), bm_ref[...])
        dq, dkw, dvw, dbm = vjp((do_ref[...].astype(F32), dl_ref[...]))
        dq_ref[...] = dq.astype(dq_ref.dtype)

        @pl.when(i == 0)
        def _():
            dk_ref[...] = jnp.zeros_like(dk_ref)
            dv_ref[...] = jnp.zeros_like(dv_ref)

        dk_ref[pl.ds(st, w), :] += dkw
        dv_ref[pl.ds(st, w), :] += dvw

        @pl.when((i == 0) & (r == 0))
        def _():
            dbm_ref[...] = dbm

        @pl.when((i > 0) | (r > 0))
        def _():
            dbm_ref[...] += dbm

    return pl.pallas_call(
        body, name=name, grid=(4, dil, l_sub // bq),
        in_specs=[qs, ks, ks, bs, qs, qs], out_specs=[qs, ks, ks, bs],
        out_shape=[jax.ShapeDtypeStruct(q.shape, BF16), jax.ShapeDtypeStruct(k.shape, F32),
                   jax.ShapeDtypeStruct(k.shape, F32), jax.ShapeDtypeStruct(bm.shape, F32)],
        compiler_params=_cparams(("parallel", "arbitrary", "arbitrary")),
    )(q, k, v, bm, do, dlse)


def _t5_bucket(rel):
    half = REL_BUCKETS // 2
    max_exact = half // 2
    n = np.abs(rel)
    large = max_exact + (np.log(np.maximum(n, 1) / max_exact) / math.log(REL_MAX_DIST / max_exact)
                         * (half - max_exact)).astype(np.int64)
    large = np.minimum(large, half - 1)
    return ((rel > 0) * half + np.where(n < max_exact, n, large)).astype(np.int32)


def _bucket_onehot(dil):
    idx = _t5_bucket(np.arange(-BAND_HALF, BAND_HALF + 1) * dil)
    oh = np.zeros((2 * BAND_HALF + 1, REL_BUCKETS), np.float32)
    oh[np.arange(2 * BAND_HALF + 1), idx] = 1.0
    return oh


def _band_bias(rel_bias, gi, dil, bq):
    w = bq + 2 * BAND_HALF
    nb = 2 * BAND_HALF + 1
    bias = jnp.dot(jnp.asarray(_bucket_onehot(dil)), rel_bias[:, 4 * gi:4 * gi + 4], precision=HI)
    row = jnp.concatenate([bias.T, jnp.full((4, w + 1 - nb), NEG, F32)], axis=1)
    flat = jnp.tile(row, (1, bq))[:, :bq * w]
    return flat.reshape(4, bq, w)


def _relbias_grad(dbms, name):
    nb = 2 * BAND_HALF + 1
    bq = max(d.shape[1] for d in dbms)
    skew = []
    for dbm in dbms:
        bqg, w = dbm.shape[1], dbm.shape[2]
        flat = jnp.pad(dbm.reshape(4, bqg * w), ((0, 0), (0, bqg)))
        skew.append(jnp.pad(flat.reshape(4, bqg, w + 1)[:, :, :nb], ((0, 0), (0, bq - bqg), (0, 256 - nb))))
    sk = jnp.concatenate(skew, axis=0)
    oh = np.zeros((3, 256, 128), np.float32)
    for gi, (_, dil) in enumerate(DIL_GROUPS):
        oh[gi, :2 * BAND_HALF + 1, :REL_BUCKETS] = _bucket_onehot(dil)

    def body(s_ref, oh_ref, o_ref):
        col = jnp.sum(s_ref[...], axis=0, keepdims=True)
        o_ref[...] = jnp.dot(jnp.broadcast_to(col, (8, 256)), oh_ref[...], precision=HI, preferred_element_type=F32)

    out = pl.pallas_call(
        body, name=name, grid=(12,),
        in_specs=[pl.BlockSpec((None, bq, 256), lambda n: (n, 0, 0)),
                  pl.BlockSpec((None, 256, 128), lambda n: (n // 4, 0, 0))],
        out_specs=pl.BlockSpec((None, 8, 128), lambda n: (n, 0, 0)),
        out_shape=jax.ShapeDtypeStruct((12, 8, 128), F32),
        compiler_params=_cparams(("parallel",)),
    )(sk, jnp.asarray(oh))
    return out[:, 0, :REL_BUCKETS].T


def _mem_fn(q, k, v):
    s = lax.dot_general((q * (ATT_HD ** -0.5)).astype(BF16), k.astype(BF16), (((1,), (1,)), ((), ())),
                        preferred_element_type=F32)
    m = lax.stop_gradient(jnp.max(s, axis=-1, keepdims=True))
    p = jnp.exp(s - m)
    p = p / jnp.sum(p, axis=-1, keepdims=True)
    return jnp.dot(p.astype(BF16), v.astype(BF16), preferred_element_type=F32)


def _mem_specs(tb, ml):
    qs = pl.BlockSpec((None, tb, ATT_HD), lambda h, i: (h, i, 0))
    ks = pl.BlockSpec((None, ml, ATT_HD), lambda h, i: (h, 0, 0))
    return qs, ks


def _mem_fwd(q, k, v, tb, name):
    qs, ks = _mem_specs(tb, k.shape[1])

    def body(q_ref, k_ref, v_ref, o_ref):
        o_ref[...] = _mem_fn(q_ref[...].astype(F32), k_ref[...], v_ref[...])

    return pl.pallas_call(
        body, name=name, grid=(MEM_HEADS, q.shape[1] // tb),
        in_specs=[qs, ks, ks], out_specs=qs, out_shape=jax.ShapeDtypeStruct(q.shape, F32),
        compiler_params=_cparams(("parallel", "parallel")),
    )(q, k, v)


def _mem_bwd(q, k, v, do, tb, name):
    qs, ks = _mem_specs(tb, k.shape[1])

    def body(q_ref, k_ref, v_ref, do_ref, dq_ref, dk_ref, dv_ref):
        i = pl.program_id(1)
        _, vjp = jax.vjp(_mem_fn, q_ref[...].astype(F32), k_ref[...], v_ref[...])
        dq, dk, dv = vjp(do_ref[...])
        dq_ref[...] = dq

        @pl.when(i == 0)
        def _():
            dk_ref[...] = dk
            dv_ref[...] = dv

        @pl.when(i > 0)
        def _():
            dk_ref[...] += dk
            dv_ref[...] += dv

    return pl.pallas_call(
        body, name=name, grid=(MEM_HEADS, q.shape[1] // tb),
        in_specs=[qs, ks, ks, qs], out_specs=[qs, ks, ks],
        out_shape=[jax.ShapeDtypeStruct(q.shape, F32), jax.ShapeDtypeStruct(k.shape, F32),
                   jax.ShapeDtypeStruct(k.shape, F32)],
        compiler_params=_cparams(("parallel", "arbitrary")),
    )(q, k, v, do)


CONV_PAD = 8


def _conv_post(kind, acc):
    s = _silu(acc)
    if kind == 2:
        return s
    scale = DN_HD ** -0.5 if kind == 0 else 1.0
    return s * lax.rsqrt(jnp.sum(s * s, axis=-1, keepdims=True) + EPS) * scale


def _conv_rows(x_ref, t, start, rt):
    lo = min(max(start, 0), t - rt)
    x = x_ref[pl.ds(lo, rt), :]
    shift = lo - start
    if shift == 0:
        return x
    x = pltpu.roll(x, shift % rt, axis=0)
    row = lax.broadcasted_iota(jnp.int32, x.shape, 0)
    return jnp.where((row >= shift) if shift > 0 else (row < rt + shift), x, 0.0)


def _conv_acc(x_ref, t, w, r0, rt):
    acc = None
    for i in range(DN_CONV):
        term = w[i:i + 1, :] * _conv_rows(x_ref, t, r0 + i - DN_CONV // 2, rt)
        acc = term if acc is None else acc + term
    return acc


def _conv_fwd(x, w8, kind, rt, name):
    t = x.shape[0]

    def body(x_ref, w_ref, o_ref):
        w = w_ref[...]
        for r in range(t // rt):
            o_ref[pl.ds(r * rt, rt), :] = _conv_post(kind, _conv_acc(x_ref, t, w, r * rt, rt))

    return pl.pallas_call(
        body, name=name, grid=(DN_HEADS,),
        in_specs=[pl.BlockSpec((t, DN_HD), lambda j: (0, 6 * kind + j)),
                  pl.BlockSpec((8, DN_HD), lambda j: (0, 6 * kind + j))],
        out_specs=pl.BlockSpec((t, DN_HD), lambda j: (0, j)),
        out_shape=jax.ShapeDtypeStruct((t, TOK_W), F32),
        compiler_params=_cparams(("parallel",)),
    )(x, w8)


def _conv_bwd(x, w8, d_f, d_r, kind, rt, name):
    t = x.shape[0]

    def body(xp_ref, w_ref, df_ref, dr_ref, dx_ref, dw_ref, dpad_ref):
        w = w_ref[...]
        zero = jnp.zeros((CONV_PAD, DN_HD), F32)
        dpad_ref[pl.ds(0, CONV_PAD), :] = zero
        dpad_ref[pl.ds(CONV_PAD + t, CONV_PAD), :] = zero
        dw = [jnp.zeros((1, DN_HD), F32) for _ in range(DN_CONV)]
        for r in range(t // rt):
            rows = pl.ds(r * rt, rt)
            acc = _conv_acc(xp_ref, t, w, r * rt, rt)
            _, vjp = jax.vjp(functools.partial(_conv_post, kind), acc)
            (dacc,) = vjp(df_ref[rows, :] + dr_ref[rows, :])
            dpad_ref[pl.ds(CONV_PAD + r * rt, rt), :] = dacc
            for i in range(DN_CONV):
                xs = _conv_rows(xp_ref, t, r * rt + i - DN_CONV // 2, rt)
                dw[i] = dw[i] + jnp.sum(dacc * xs, axis=0, keepdims=True)
        dw_ref[...] = jnp.concatenate(dw + [jnp.zeros((8 - DN_CONV, DN_HD), F32)], axis=0)
        for r in range(t // rt):
            acc = None
            for i in range(DN_CONV):
                term = w[i:i + 1, :] * dpad_ref[pl.ds(CONV_PAD + r * rt - i + DN_CONV // 2, rt), :]
                acc = term if acc is None else acc + term
            dx_ref[pl.ds(r * rt, rt), :] = acc.astype(dx_ref.dtype)

    return pl.pallas_call(
        body, name=name, grid=(DN_HEADS,),
        in_specs=[pl.BlockSpec((t, DN_HD), lambda j: (0, 6 * kind + j)),
                  pl.BlockSpec((8, DN_HD), lambda j: (0, 6 * kind + j)),
                  pl.BlockSpec((t, DN_HD), lambda j: (0, j)),
                  pl.BlockSpec((t, DN_HD), lambda j: (0, j))],
        out_specs=[pl.BlockSpec((t, DN_HD), lambda j: (0, j)), pl.BlockSpec((8, DN_HD), lambda j: (0, j))],
        out_shape=[jax.ShapeDtypeStruct((t, TOK_W), BF16), jax.ShapeDtypeStruct((8, TOK_W), F32)],
        scratch_shapes=[pltpu.VMEM((t + 2 * CONV_PAD, DN_HD), F32)],
        compiler_params=_cparams(("parallel",)),
    )(x, w8, d_f, d_r)


def _softplus(x):
    e = jnp.exp(-jnp.abs(x))
    return jnp.maximum(x, 0.0) + jnp.where(e < 1e-4, e - 0.5 * e * e, jnp.log(1.0 + e))


_NN = (((2,), (1,)), ((0,), (0,)))
_NT = (((2,), (2,)), ((0,), (0,)))
_TN = (((1,), (1,)), ((0,), (0,)))


def _dot(a, b, dims=_NN):
    return lax.dot_general(a.astype(BF16), b.astype(BF16), dims, preferred_element_type=F32)


def _hi_lo(x):
    hi = x.astype(BF16)
    return hi, (x - hi.astype(F32)).astype(BF16)


def _mask_dot(mask_bf16, x, dims):
    x1 = x.astype(BF16)
    r = x - x1.astype(F32)
    x2, x3 = _hi_lo(r)
    d = functools.partial(lax.dot_general, dimension_numbers=dims, preferred_element_type=F32)
    return d(mask_bf16, x1) + d(mask_bf16, x2) + d(mask_bf16, x3)


@jax.custom_vjp
def _dot_mask(mask_bf16, x):
    return _mask_dot(mask_bf16, x, _NN)


def _dot_mask_fwd(mask_bf16, x):
    return _mask_dot(mask_bf16, x, _NN), mask_bf16


def _dot_mask_bwd(mask_bf16, ct):
    return jnp.zeros_like(mask_bf16), _mask_dot(mask_bf16, ct, _TN)


_dot_mask.defvjp(_dot_mask_fwd, _dot_mask_bwd)


def _dot3_raw(a, b, dims):
    a1, a2 = _hi_lo(a)
    b1, b2 = _hi_lo(b)
    d = functools.partial(lax.dot_general, dimension_numbers=dims, preferred_element_type=F32)
    return d(a1, b1) + d(a1, b2) + d(a2, b1)


def _unit_solve_pass(lmat, rhs, masks):
    ainv = masks[6] - lmat * masks[0]
    for sh in range(1, 6):
        ainv = ainv - _dot3_raw(_dot3_raw(ainv, lmat * masks[sh], _NN), ainv, _NN)
    return _dot3_raw(ainv, rhs, _NN), ainv


@jax.custom_vjp
def _unit_solve(lmat, rhs, masks):
    return _unit_solve_pass(lmat, rhs, masks)[0]


def _unit_solve_fwd(lmat, rhs, masks):
    sol, ainv = _unit_solve_pass(lmat, rhs, masks)
    return sol, (sol, ainv, masks)


def _unit_solve_bwd(res, ct):
    sol, ainv, masks = res
    d_rhs = _dot3_raw(ainv, ct, _TN)
    return -_dot3_raw(d_rhs, sol, _NT), d_rhs, tuple(jnp.zeros_like(m) for m in masks)


_unit_solve.defvjp(_unit_solve_fwd, _unit_solve_bwd)


def _block_masks(rev, row, col):
    c = DN_CHUNK
    prow = jnp.where(rev, c - 1 - row, row)
    pcol = jnp.where(rev, c - 1 - col, col)
    masks = []
    for sh in range(6):
        differ = (prow ^ pcol) >> sh
        miss = (differ ^ 1) + (1 - ((prow >> sh) & 1))
        masks.append(jnp.where(miss == 0, 1.0, 0.0))
    masks.append(jnp.where(row == col, 1.0, 0.0))
    return tuple(masks)


def _dn_chunk(q, k, v, al, be, alc, a_row, dt_row, a_rowc, dt_rowc, s):
    n, c = q.shape[0], DN_CHUNK
    rev = lax.broadcasted_iota(jnp.int32, (n, c, c), 0) >= n // 2
    row = lax.broadcasted_iota(jnp.int32, (n, c, c), 1)
    col = lax.broadcasted_iota(jnp.int32, (n, c, c), 2)
    ahead = jnp.where(rev, col - row, row - col)
    incl = ahead >= 0
    strict = ahead > 0
    incl_b = incl.astype(BF16)

    g = -jnp.exp(a_row) * _softplus(al + dt_row)
    beta = _sigmoid(be)
    g_c = -jnp.exp(a_rowc) * _softplus(alc + dt_rowc)
    gc = _dot_mask(incl_b, g)
    gcc = _dot_mask(incl_b, g_c)
    decay = jnp.exp(jnp.where(incl, gcc - jnp.swapaxes(gcc, 1, 2), NEG))
    kb = k * beta
    lmat = jnp.where(strict, _dot(kb, k, _NT) * decay, 0.0)
    rhs = jnp.concatenate([v * beta, kb * jnp.exp(gc)], axis=2)
    sol = _unit_solve(lmat, rhs, _block_masks(rev, row, col))
    u, w = sol[:, :, :DN_HD], sol[:, :, DN_HD:]
    intra = jnp.where(incl, _dot(q, k, _NT) * decay, 0.0)
    v_new = u - _dot(w, s)
    out = _dot(q * jnp.exp(gc), s) + _dot(intra, v_new)
    g_last = jnp.sum(g, axis=1, keepdims=True)
    s_new = s * jnp.exp(g_last) + _dot(k * jnp.exp(g_last - gc), v_new, _TN)
    return out, s_new


DN_HG = 6


def _dn_load(f_refs, r_refs, alf, bef, alr, ber, a_ref, dt_ref):
    c, hg = DN_CHUNK, DN_HG
    sls = [slice(DN_HD * h, DN_HD * (h + 1)) for h in range(hg)]
    toks = [jnp.stack([f[:, sl] for sl in sls] + [r[:, sl] for sl in sls]) for f, r in zip(f_refs, r_refs)]
    al = jnp.concatenate([alf[...], alr[...]], axis=0)
    be = jnp.concatenate([bef[...], ber[...]], axis=0)
    alc = jnp.concatenate([alf[:, :, 0:c], alr[:, :, 0:c]], axis=0)
    a = jnp.concatenate([a_ref[0], a_ref[1]], axis=0)
    dt = jnp.concatenate([dt_ref[0], dt_ref[1]], axis=0)
    ac = jnp.concatenate([a_ref[0, :, :, 0:c], a_ref[1, :, :, 0:c]], axis=0)
    dtc = jnp.concatenate([dt_ref[0, :, :, 0:c], dt_ref[1, :, :, 0:c]], axis=0)
    return toks, (al, be, alc, a, dt, ac, dtc)


def _dn_views(nc, bwd):
    c, hg = DN_CHUNK, DN_HG
    if bwd:
        f_blk = lambda s: nc - 1 - s
        r_blk = lambda s: s
        st_blk = lambda s: nc - 1 - s
    else:
        f_blk = lambda s: s
        r_blk = lambda s: nc - 1 - s
        st_blk = lambda s: s
    tok_f = pl.BlockSpec((c, hg * DN_HD), lambda g, s: (f_blk(s), g))
    tok_r = pl.BlockSpec((c, hg * DN_HD), lambda g, s: (r_blk(s), g))
    gate_f = pl.BlockSpec((None, hg, c, DN_HD), lambda g, s: (0, g, f_blk(s), 0))
    gate_r = pl.BlockSpec((None, hg, c, DN_HD), lambda g, s: (1, g, r_blk(s), 0))
    par = pl.BlockSpec((2, hg, 1, DN_HD), lambda g, s: (0, g, 0, 0))
    state = pl.BlockSpec((2, hg, None, DN_HD, DN_HD), lambda g, s: (0, g, st_blk(s), 0, 0))
    return tok_f, tok_r, gate_f, gate_r, par, state


def _dn_fwd(q, k, v, al, be, a_rows, dt_rows, name):
    t = q.shape[0]
    c, hg = DN_CHUNK, DN_HG
    nc = t // c
    tok_f, tok_r, gate_f, gate_r, par, state = _dn_views(nc, False)

    def body(qf, kf, vf, qr, kr, vr, alf, bef, alr, ber, a_ref, dt_ref, of_ref, or_ref, st_ref, s_ref):
        @pl.when(pl.program_id(1) == 0)
        def _():
            s_ref[...] = jnp.zeros_like(s_ref)

        (q_, k_, v_), gates = _dn_load((qf, kf, vf), (qr, kr, vr), alf, bef, alr, ber, a_ref, dt_ref)
        s = s_ref[...]
        st_ref[0] = s[:hg]
        st_ref[1] = s[hg:]
        out, s_new = _dn_chunk(q_, k_, v_, *gates, s)
        for h in range(hg):
            sl = slice(DN_HD * h, DN_HD * (h + 1))
            of_ref[:, sl] = out[h]
            or_ref[:, sl] = out[hg + h]
        s_ref[...] = s_new

    return pl.pallas_call(
        body, name=name, grid=(DN_HEADS // hg, nc),
        in_specs=[tok_f] * 3 + [tok_r] * 3 + [gate_f, gate_f, gate_r, gate_r, par, par],
        out_specs=[tok_f, tok_r, state],
        out_shape=[jax.ShapeDtypeStruct((t, TOK_W), F32)] * 2
        + [jax.ShapeDtypeStruct((2, DN_HEADS, nc, DN_HD, DN_HD), F32)],
        scratch_shapes=[pltpu.VMEM((2 * hg, DN_HD, DN_HD), F32)],
        compiler_params=_cparams(("parallel", "arbitrary")),
    )(q, k, v, q, k, v, al, be, al, be, a_rows, dt_rows)


def _dn_bwd(q, k, v, al, be, a_rows, dt_rows, states, do, name):
    t = q.shape[0]
    c, hg = DN_CHUNK, DN_HG
    assert hg == DN_HEADS
    nc = t // c
    tok_f, tok_r, gate_f, gate_r, par, state = _dn_views(nc, True)
    gout_f = pl.BlockSpec((c, DN_HD), lambda g, s: (nc - 1 - s, 0))
    gout_r = pl.BlockSpec((c, DN_HD), lambda g, s: (s, 0))

    def body(qf, kf, vf, qr, kr, vr, alf, bef, alr, ber, a_ref, dt_ref, st_ref, dof, dor,
             dqf, dkf, dvf, dqr, dkr, dvr, dgf, dgr, da_ref, ddt_ref, ds_ref):
        first = pl.program_id(1) == 0

        @pl.when(first)
        def _():
            ds_ref[...] = jnp.zeros_like(ds_ref)
            da_ref[...] = jnp.zeros_like(da_ref)
            ddt_ref[...] = jnp.zeros_like(ddt_ref)

        def lanes(x):
            return jnp.sum(x, axis=-1, keepdims=True)

        (q_, k_, v_, do_), gates = _dn_load((qf, kf, vf, dof), (qr, kr, vr, dor), alf, bef, alr, ber, a_ref, dt_ref)
        s = jnp.concatenate([st_ref[0], st_ref[1]], axis=0)
        _, vjp = jax.vjp(_dn_chunk, q_, k_, v_, *gates, s)
        dq, dk, dv, dal, dbe, dalc, da, ddt, dac, ddtc, ds = vjp((do_, ds_ref[...]))
        for h in range(hg):
            sl = slice(DN_HD * h, DN_HD * (h + 1))
            dqf[:, sl], dkf[:, sl], dvf[:, sl] = dq[h], dk[h], dv[h]
            dqr[:, sl], dkr[:, sl], dvr[:, sl] = dq[hg + h], dk[hg + h], dv[hg + h]
        dal, dbe = lanes(dal) + lanes(dalc), lanes(dbe)
        lane = lax.broadcasted_iota(jnp.int32, (c, DN_HD), 1)
        for d, dg_ref in enumerate((dgf, dgr)):
            dg = jnp.zeros((c, DN_HD), F32)
            for h in range(hg):
                dg = jnp.where(lane == h, dal[d * hg + h], jnp.where(lane == hg + h, dbe[d * hg + h], dg))
            dg_ref[...] = dg
        da = jnp.broadcast_to(lanes(da) + lanes(dac), da.shape)
        ddt = jnp.broadcast_to(lanes(ddt) + lanes(ddtc), ddt.shape)
        da_ref[0] += da[:hg]
        da_ref[1] += da[hg:]
        ddt_ref[0] += ddt[:hg]
        ddt_ref[1] += ddt[hg:]
        ds_ref[...] = ds

    tok = jax.ShapeDtypeStruct((t, TOK_W), F32)
    gate = jax.ShapeDtypeStruct((t, DN_HD), F32)
    parsh = jax.ShapeDtypeStruct((2, DN_HEADS, 1, DN_HD), F32)
    res = pl.pallas_call(
        body, name=name, grid=(DN_HEADS // hg, nc),
        in_specs=[tok_f] * 3 + [tok_r] * 3 + [gate_f, gate_f, gate_r, gate_r, par, par, state, tok_f, tok_r],
        out_specs=[tok_f] * 3 + [tok_r] * 3 + [gout_f, gout_r, par, par],
        out_shape=[tok] * 6 + [gate] * 2 + [parsh] * 2,
        scratch_shapes=[pltpu.VMEM((2 * hg, DN_HD, DN_HD), F32)],
        compiler_params=_cparams(("parallel", "arbitrary")),
    )(q, k, v, q, k, v, al, be, al, be, a_rows, dt_rows, states, do, do)
    dqf, dkf, dvf, dqr, dkr, dvr, dgf, dgr, da, ddt = res
    dgate = jnp.concatenate([dgf[:, :2 * DN_HEADS], dgr[:, :2 * DN_HEADS]], axis=1)
    return (dqf, dkf, dvf), (dqr, dkr, dvr), dgate, da, ddt


BAND_BQ = 256
ROW_TB = 256
MEM_TB = 512
CONV_RT = 512


def _to_sub(x, dil):
    l = x.shape[0] // dil
    return x.reshape(l, dil, 4, ATT_HD).transpose(2, 1, 0, 3).reshape(4 * dil, l, ATT_HD)


def _from_sub(x, dil):
    l = x.shape[1]
    return x.reshape(4, dil, l, ATT_HD).transpose(2, 1, 0, 3).reshape(l * dil, 4 * ATT_HD)


def _heads_major(x):
    return x.reshape(x.shape[0], MEM_HEADS, ATT_HD).transpose(1, 0, 2)


def _heads_minor(x):
    return x.transpose(1, 0, 2).reshape(x.shape[1], MEM_HEADS * ATT_HD)


def _mem_kv_fwd(mem, gain, w_kv, li):
    (memn,) = _rowwise(_fn_pre, [mem], [gain], [(D, BF16)], mem.shape[0], f"memnorm_fwd{li}")
    kv = _matmul(memn, w_kv, "nn", F32, f"memkv_fwd{li}")
    return _heads_major(kv[:, :MEM_W]), _heads_major(kv[:, MEM_W:]), memn


def _mem_kv_bwd(mem, gain, w_kv, memn, dkm, dvm, li):
    dkv = jnp.concatenate([_heads_minor(dkm), _heads_minor(dvm)], axis=1).astype(BF16)
    dw = _matmul(memn, dkv, "tn", BF16, f"memkv_dw{li}")
    dmemn = _matmul(dkv, w_kv, "nt", F32, f"memkv_dx{li}")
    _, (dgain,) = _rowwise_bwd(_fn_pre, [mem], [gain], [dmemn], [None], mem.shape[0], f"memnorm_bwd{li}")
    return dw, dgain


def _attn_mixer_fwd(p, rel_bias, kv_fn):
    t = p.shape[0]
    saved, outs, lses = [], [], []
    for gi, (_, dil) in enumerate(DIL_GROUPS):
        l_sub = t // dil
        bq = min(BAND_BQ, l_sub)
        q = _to_sub(p[:, 256 * gi:256 * (gi + 1)], dil)
        pad = ((0, 0), (BAND_HALF, BAND_HALF), (0, 0))
        k = jnp.pad(_to_sub(p[:, TOK_W + 256 * gi:TOK_W + 256 * (gi + 1)], dil), pad)
        v = jnp.pad(_to_sub(p[:, 2 * TOK_W + 256 * gi:2 * TOK_W + 256 * (gi + 1)], dil), pad)
        bm = _band_bias(rel_bias, gi, dil, bq)
        o, lse = _band_fwd(q, k, v, bm, dil, l_sub, bq, f"band_fwd{gi}")
        outs.append(_from_sub(o, dil))
        lses.append(_from_sub(lse, dil))
        saved.append((q, k, v, bm))
    o_all = jnp.concatenate(outs, axis=1)
    lse_all = jnp.concatenate(lses, axis=1)
    (mixed,) = _rowwise(_fn_combine, [o_all, lse_all], [], [(TOK_W, BF16)], ROW_TB, "combine_fwd")
    qm = _heads_major(p[:, 3 * TOK_W:])
    km, vm, memn = kv_fn(mixed)
    memo = _mem_fwd(qm, km, vm, min(MEM_TB, t), "mem_fwd0")
    cat = jnp.concatenate([mixed, _heads_minor(memo).astype(BF16)], axis=1)
    return cat, (saved, o_all, lse_all, qm), (km, vm, memn)


def _attn_mixer_bwd(dcat, res, km, vm):
    saved, o_all, lse_all, qm = res
    t = dcat.shape[0]
    (do_all, dlse_all), _ = _rowwise_bwd(_fn_combine, [o_all, lse_all], [], [dcat[:, :TOK_W]], [BF16, F32],
                                         ROW_TB, "combine_bwd")
    dqs, dks, dvs, dbms = [], [], [], []
    for gi, (_, dil) in enumerate(DIL_GROUPS):
        l_sub = t // dil
        bq = min(BAND_BQ, l_sub)
        q, k, v, bm = saved[gi]
        do = _to_sub(do_all[:, 256 * gi:256 * (gi + 1)], dil)
        dl = _to_sub(dlse_all[:, 256 * gi:256 * (gi + 1)], dil)
        dq, dk, dv, dbm = _band_bwd(q, k, v, bm, do, dl, dil, l_sub, bq, f"band_bwd{gi}")
        dqs.append(_from_sub(dq, dil))
        dks.append(_from_sub(dk[:, BAND_HALF:-BAND_HALF], dil))
        dvs.append(_from_sub(dv[:, BAND_HALF:-BAND_HALF], dil))
        dbms.append(dbm)
    dqm, dkm, dvm = _mem_bwd(qm, km, vm, _heads_major(dcat[:, TOK_W:]), min(MEM_TB, t), "mem_bwd0")
    dp = jnp.concatenate([d.astype(BF16) for d in dqs + dks + dvs + [_heads_minor(dqm)]], axis=1)
    return dp, _relbias_grad(dbms, "relbias_grad"), dkm, dvm


def _dn_mixer_fwd(p, conv_w, a_log, dt_bias, out_norm, km, vm):
    t = p.shape[0]
    rt = min(CONV_RT, t)
    xp = p
    w8 = jnp.pad(conv_w, ((0, 8 - DN_CONV), (0, 0)))
    q = _conv_fwd(xp, w8, 0, rt, "conv_fwd_q")
    k = _conv_fwd(xp, w8, 1, rt, "conv_fwd_k")
    v = _conv_fwd(xp, w8, 2, rt, "conv_fwd_v")
    gate = p[:, 4 * TOK_W:4 * TOK_W + 4 * DN_HEADS].reshape(t, 2, 2, DN_HEADS)
    bshape = (2, DN_HEADS, t, DN_HD)
    al = jnp.broadcast_to(gate[:, :, 0, :].transpose(1, 2, 0)[..., None], bshape)
    be = jnp.broadcast_to(gate[:, :, 1, :].transpose(1, 2, 0)[..., None], bshape)
    a_rows = jnp.broadcast_to(a_log[:, :, None, None], (2, DN_HEADS, 1, DN_HD))
    dt_rows = jnp.broadcast_to(dt_bias[:, :, None, None], (2, DN_HEADS, 1, DN_HD))
    o_f, o_r, states = _dn_fwd(q, k, v, al, be, a_rows, dt_rows, "dn_fwd")
    z = p[:, 3 * TOK_W:4 * TOK_W]
    gain = out_norm.reshape(1, DN_HD)
    (og,) = _rowwise(_fn_outnorm, [o_f, o_r, z], [gain], [(TOK_W, BF16)], ROW_TB, "outnorm_fwd")
    qm = _heads_major(p[:, 4 * TOK_W + 4 * DN_HEADS:DN_IN])
    memo = _mem_fwd(qm, km, vm, min(MEM_TB, t), "mem_fwd1")
    cat = jnp.concatenate([og, _heads_minor(memo).astype(BF16)], axis=1)
    return cat, (xp, w8, q, k, v, al, be, a_rows, dt_rows, o_f, o_r, states, z, gain, qm)


def _dn_mixer_bwd(dcat, res, km, vm):
    xp, w8, q, k, v, al, be, a_rows, dt_rows, o_f, o_r, states, z, gain, qm = res
    t = dcat.shape[0]
    rt = min(CONV_RT, t)
    (do, dz), (dgain,) = _rowwise_bwd(_fn_outnorm, [o_f, o_r, z], [gain], [dcat[:, :TOK_W]], [F32, None, BF16],
                                      ROW_TB, "outnorm_bwd")
    d_f, d_r, dgate, da, ddt = _dn_bwd(q, k, v, al, be, a_rows, dt_rows, states, do, "dn_bwd")
    dxs, dws = [], []
    for kind, nm in enumerate("qkv"):
        dx, dw = _conv_bwd(xp, w8, d_f[kind], d_r[kind], kind, rt, f"conv_bwd_{nm}")
        dxs.append(dx)
        dws.append(dw)
    dconv = jnp.concatenate(dws, axis=1)[:DN_CONV]
    dqm, dkm, dvm = _mem_bwd(qm, km, vm, _heads_major(dcat[:, TOK_W:]), min(MEM_TB, t), "mem_bwd1")
    dp = jnp.concatenate(dxs + [dz, dgate.astype(BF16), _heads_minor(dqm).astype(BF16),
                               jnp.zeros((t, DN_IN_PAD - DN_IN), BF16)], axis=1)
    return dp, dconv, da[:, :, 0, 0], ddt[:, :, 0, 0], dgain.reshape(DN_HD), dkm, dvm


SWI_TB = 256


def _ffn_fwd(h, w_gu_t, w_d, li):
    gu = _matmul(h, w_gu_t, "nt", BF16, f"ffn_gu{li}")
    (a,) = _rowwise(_fn_swiglu, [gu], [], [(D_FF, BF16)], SWI_TB, f"swiglu_fwd{li}")
    return _matmul(a, w_d, "nn", F32, f"ffn_down{li}"), gu, a


def _ffn_bwd(df, h, w_gu_t, w_d, gu, a, li):
    da = _matmul(df, w_d, "nt", BF16, f"ffn_down_dx{li}")
    dwd = _matmul(a, df, "tn", BF16, f"ffn_down_dw{li}")
    (dgu,), _ = _rowwise_bwd(_fn_swiglu, [gu], [], [da], [BF16], SWI_TB, f"swiglu_bwd{li}")
    dh = _matmul(dgu, w_gu_t, "nn", F32, f"ffn_gu_dx{li}")
    dwgu_t = _matmul(dgu, h, "tn", BF16, f"ffn_gu_dw{li}")
    return dh, dwgu_t, dwd


def _fn_first(x, g):
    return x, _rms(x, g)


def _me_xyc():
    return lax.axis_index("x"), lax.axis_index("y"), lax.axis_index("c")


def _flip(coords, k):
    x, y, c = coords
    return (1 - x if k & 4 else x, 1 - y if k & 2 else y, 1 - c if k & 1 else c)


def _index(coords):
    x, y, c = coords
    return 4 * x + 2 * y + c


def _window(ref, axis, size, d):
    idx = [slice(None)] * len(ref.shape)
    idx[axis] = pl.ds(pl.multiple_of(d * size, size), size)
    return ref.at[tuple(idx)]


def _comm_call(body, n, ins, out_shapes, name):
    hbm = pl.BlockSpec(memory_space=pl.ANY)
    return pl.pallas_call(
        body, name=name, in_specs=[hbm] * n, out_specs=[hbm] * n, out_shape=out_shapes,
        scratch_shapes=[pltpu.SemaphoreType.DMA((N_DEV - 1, n)), pltpu.SemaphoreType.DMA((N_DEV - 1, n)),
                        pltpu.SemaphoreType.DMA((n,))],
    )(*ins)


def _run_exchange(n, local, remote, send_sems, recv_sems):
    me = _me_xyc()
    locs = [local(p) for p in range(n)]
    for cp in locs:
        cp.start()
    sends = [remote(k, p, me, _flip(me, k)) for k in range(1, N_DEV) for p in range(n)]
    for cp in sends:
        cp.start()
    for k in range(1, N_DEV):
        for p in range(n):
            remote(k, p, _flip(me, k), me).wait_recv()
    for cp in sends:
        cp.wait_send()
    for cp in locs:
        cp.wait()


def _all_gather(shards, axes, name):
    n = len(shards)
    sizes = [s.shape[a] for s, a in zip(shards, axes)]

    def body(*refs):
        ins, outs = refs[:n], refs[n:2 * n]
        send_sems, recv_sems, loc_sems = refs[2 * n:]
        me = _me_xyc()

        def local(p):
            return pltpu.make_async_copy(ins[p], _window(outs[p], axes[p], sizes[p], _index(me)), loc_sems.at[p])

        def remote(k, p, owner, to):
            return pltpu.make_async_remote_copy(
                src_ref=ins[p], dst_ref=_window(outs[p], axes[p], sizes[p], _index(owner)),
                send_sem=send_sems.at[k - 1, p], recv_sem=recv_sems.at[k - 1, p], device_id=to, device_id_type=MESH)

        _run_exchange(n, local, remote, send_sems, recv_sems)

    def full(s, a):
        return s.shape[:a] + (N_DEV * s.shape[a],) + s.shape[a + 1:]

    return _comm_call(body, n, shards, [jax.ShapeDtypeStruct(full(s, a), s.dtype) for s, a in zip(shards, axes)], name)


def _exchange(fulls, axes, name):
    n = len(fulls)
    sizes = [None if a is None else f.shape[a] // N_DEV for f, a in zip(fulls, axes)]

    def part_shape(f, a):
        return f.shape if a is None else f.shape[:a] + (f.shape[a] // N_DEV,) + f.shape[a + 1:]

    def body(*refs):
        ins, outs = refs[:n], refs[n:2 * n]
        send_sems, recv_sems, loc_sems = refs[2 * n:]
        me = _me_xyc()

        def src(p, to):
            return ins[p] if axes[p] is None else _window(ins[p], axes[p], sizes[p], _index(to))

        def local(p):
            return pltpu.make_async_copy(src(p, me), outs[p].at[_index(me)], loc_sems.at[p])

        def remote(k, p, sender, to):
            return pltpu.make_async_remote_copy(
                src_ref=src(p, to), dst_ref=outs[p].at[_index(sender)],
                send_sem=send_sems.at[k - 1, p], recv_sem=recv_sems.at[k - 1, p], device_id=to, device_id_type=MESH)

        _run_exchange(n, local, remote, send_sems, recv_sems)

    return _comm_call(body, n, fulls,
                      [jax.ShapeDtypeStruct((N_DEV,) + part_shape(f, a), f.dtype) for f, a in zip(fulls, axes)], name)


_HBM = pl.BlockSpec(memory_space=pltpu.HBM)
_SEM = pl.BlockSpec(memory_space=pltpu.SEMAPHORE)
_EFFECT = pltpu.SideEffectType.DATAFLOW_SIDE_EFFECTING


def _in_hbm(a):
    return pltpu.with_memory_space_constraint(a, pltpu.HBM)


def _split_start(srcs, lands, after, descr, name):
    n = len(srcs)

    def body(*refs):
        ins, lnd = refs[:n], refs[n:2 * n]
        send_sems, recv_sems = refs[2 * n + 1], refs[2 * n + 2]
        token = refs[-1]
        me = _me_xyc()
        for k in range(1, N_DEV):
            for p in range(n):
                descr(k, p, ins, lnd, send_sems, recv_sems, me, _flip(me, k)).start()
        token[...] = jnp.zeros_like(token)

    sems = pltpu.SemaphoreType.DMA(((N_DEV - 1) * n,))
    res = pl.pallas_call(
        body, name=name,
        out_shape=(sems, sems, *[pltpu.HBM(a.shape, a.dtype) for a in (*srcs, *lands)],
                   jax.ShapeDtypeStruct((8, 128), F32)),
        in_specs=[_HBM] * (2 * n) + [pl.BlockSpec(memory_space=pl.ANY)],
        out_specs=(_SEM, _SEM, *[_HBM] * (2 * n), pl.BlockSpec(memory_space=pltpu.VMEM)),
        input_output_aliases={i: 2 + i for i in range(2 * n)},
        compiler_params=pltpu.CompilerParams(has_side_effects=_EFFECT),
    )(*[_in_hbm(a) for a in (*srcs, *lands)], after)
    return res[0], res[1], res[2:2 + n], res[2 + n:2 + 2 * n], res[-1]


def _split_wait(send_sems, recv_sems, srcs, lands, after, descr, name):
    n = len(srcs)

    def body(*refs):
        ins, lnd = refs[:n], refs[n:2 * n]
        s_sems, r_sems = refs[2 * n], refs[2 * n + 1]
        me = _me_xyc()
        for k in range(1, N_DEV):
            for p in range(n):
                peer = _flip(me, k)
                descr(k, p, ins, lnd, s_sems, r_sems, me, peer).wait_send()
                descr(k, p, ins, lnd, s_sems, r_sems, peer, me).wait_recv()

    res = pl.pallas_call(
        body, name=name,
        out_shape=tuple(pltpu.HBM(a.shape, a.dtype) for a in (*srcs, *lands)),
        in_specs=[_HBM] * (2 * n) + [_SEM, _SEM, pl.BlockSpec(memory_space=pl.ANY)],
        out_specs=tuple([_HBM] * (2 * n)),
        input_output_aliases={i: i for i in range(2 * n)},
        compiler_params=pltpu.CompilerParams(has_side_effects=_EFFECT),
    )(*srcs, *lands, send_sems, recv_sems, after)
    return list(res[n:])


def _gather_descr(axes, sizes):
    def descr(k, p, ins, lnd, send_sems, recv_sems, sender, dest):
        return pltpu.make_async_remote_copy(
            src_ref=ins[p], dst_ref=_window(lnd[p], axes[p], sizes[p], _index(sender)),
            send_sem=send_sems.at[(k - 1) * len(axes) + p], recv_sem=recv_sems.at[(k - 1) * len(axes) + p],
            device_id=dest, device_id_type=MESH)
    return descr


def _exchange_descr(axes, sizes):
    def descr(k, p, ins, lnd, send_sems, recv_sems, sender, dest):
        return pltpu.make_async_remote_copy(
            src_ref=_window(ins[p], axes[p], sizes[p], _index(dest)), dst_ref=lnd[p].at[_index(sender)],
            send_sem=send_sems.at[(k - 1) * len(axes) + p], recv_sem=recv_sems.at[(k - 1) * len(axes) + p],
            device_id=dest, device_id_type=MESH)
    return descr


def _gather_begin(shards, axes, after, name):
    sizes = [s.shape[a] for s, a in zip(shards, axes)]
    me = _index(_me_xyc())
    lands = []
    for s, a, sz in zip(shards, axes, sizes):
        full = s.shape[:a] + (N_DEV * sz,) + s.shape[a + 1:]
        lands.append(lax.dynamic_update_slice_in_dim(lax.empty(full, s.dtype), s, me * sz, a))
    descr = _gather_descr(axes, sizes)
    send_sems, recv_sems, srcs, lands, token = _split_start(shards, lands, after, descr, name)
    return (send_sems, recv_sems, srcs, lands, descr), token


def _exchange_begin(fulls, axes, after, name):
    sizes = [f.shape[a] // N_DEV for f, a in zip(fulls, axes)]
    me = _index(_me_xyc())
    lands = []
    for f, a, sz in zip(fulls, axes, sizes):
        own = lax.dynamic_slice_in_dim(f, me * sz, sz, a)
        lands.append(lax.dynamic_update_slice_in_dim(lax.empty((N_DEV,) + own.shape, f.dtype), own[None], me, 0))
    descr = _exchange_descr(axes, sizes)
    send_sems, recv_sems, srcs, lands, token = _split_start(fulls, lands, after, descr, name)
    return (send_sems, recv_sems, srcs, lands, descr), token


def _split_end(handle, after, name):
    send_sems, recv_sems, srcs, lands, descr = handle
    return _split_wait(send_sems, recv_sems, srcs, lands, after, descr, name)


def _adam_math(g, w, m, v):
    m = ADAM_B1 * m + (1.0 - ADAM_B1) * g
    v = ADAM_B2 * v + (1.0 - ADAM_B2) * (g * g)
    m_hat = m / (1.0 - ADAM_B1 ** ADAM_STEP)
    v_hat = v / (1.0 - ADAM_B2 ** ADAM_STEP)
    delta = -ADAM_LR * (m_hat / (jnp.sqrt(v_hat) + ADAM_EPS) + ADAM_WD * w)
    return delta, m, v


def _sum_slabs(r_ref):
    g = r_ref[0].astype(F32)
    for s in range(1, N_DEV):
        g = g + r_ref[s].astype(F32)
    return g


def _adamw_reduce(recv, w, m, v, tb, name):
    r, c = w.shape

    def body(r_ref, w_ref, m_ref, v_ref, g_ref, d_ref, nm_ref, nv_ref):
        g = _sum_slabs(r_ref)
        g_ref[...] = g
        d_ref[...], nm_ref[...], nv_ref[...] = _adam_math(g, w_ref[...], m_ref[...], v_ref[...])

    blk = pl.BlockSpec((tb, c), lambda i: (i, 0))
    return pl.pallas_call(
        body, name=name, grid=(r // tb,),
        in_specs=[pl.BlockSpec((N_DEV, tb, c), lambda i: (0, i, 0)), blk, blk, blk],
        out_specs=[blk] * 4, out_shape=[jax.ShapeDtypeStruct((r, c), F32)] * 4,
        compiler_params=_cparams(("parallel",)),
    )(recv, w, m, v)


def _reduce8(recv, tb, name):
    r, c = recv.shape[1:]

    def body(r_ref, g_ref):
        g_ref[...] = _sum_slabs(r_ref)

    return pl.pallas_call(
        body, name=name, grid=(r // tb,),
        in_specs=[pl.BlockSpec((N_DEV, tb, c), lambda i: (0, i, 0))],
        out_specs=pl.BlockSpec((tb, c), lambda i: (i, 0)), out_shape=jax.ShapeDtypeStruct((r, c), F32),
        compiler_params=_cparams(("parallel",)),
    )(recv)


def _adamw(g, w, m, v, tb, name):
    r, c = w.shape

    def body(g_ref, w_ref, m_ref, v_ref, d_ref, nm_ref, nv_ref):
        d_ref[...], nm_ref[...], nv_ref[...] = _adam_math(g_ref[...], w_ref[...], m_ref[...], v_ref[...])

    blk = pl.BlockSpec((tb, c), lambda i: (i, 0))
    return pl.pallas_call(
        body, name=name, grid=(r // tb,), in_specs=[blk] * 4, out_specs=[blk] * 3,
        out_shape=[jax.ShapeDtypeStruct((r, c), F32)] * 3, compiler_params=_cparams(("parallel",)),
    )(g, w, m, v)


DN_IN_SHARD = DN_IN // N_DEV
DN_IN_SHARD_PAD = 432
CONV_SHARD = (1, DN_CONV, 288)


def _pack_small(arrs, rows):
    flat = jnp.concatenate([a.astype(F32).reshape(-1) for a in arrs])
    return jnp.pad(flat, (0, rows * PACK_C - flat.size)).reshape(rows, PACK_C)


def _unpack_small(packed, shapes):
    flat, out, off = packed.reshape(-1), [], 0
    for shp in shapes:
        n = int(np.prod(shp))
        out.append(flat[off:off + n].reshape(shp))
        off += n
    return out


def kernel(x, mem, rel_bias, att_w_in, att_w_out, dn_w_in, dn_conv, dn_a_log, dn_dt_bias, dn_out_norm, dn_w_out, mem_norm, mem_w_kv, norm_mix_pre, norm_mix_post, norm_ffn_pre, norm_ffn_post, ffn_w_gate_up, ffn_w_down, loss_target, m_rel_bias, m_att_w_in, m_att_w_out, m_dn_w_in, m_dn_conv, m_dn_a_log, m_dn_dt_bias, m_dn_out_norm, m_dn_w_out, m_mem_norm, m_mem_w_kv, m_norm_mix_pre, m_norm_mix_post, m_norm_ffn_pre, m_norm_ffn_post, m_ffn_w_gate_up, m_ffn_w_down, v_rel_bias, v_att_w_in, v_att_w_out, v_dn_w_in, v_dn_conv, v_dn_a_log, v_dn_dt_bias, v_dn_out_norm, v_dn_w_out, v_mem_norm, v_mem_w_kv, v_norm_mix_pre, v_norm_mix_post, v_norm_ffn_pre, v_norm_ffn_post, v_ffn_w_gate_up, v_ffn_w_down):
    x0, mem0, tgt = x[0], mem[0], loss_target[0]
    t = x0.shape[0]
    axes = ("x", "y", "c")

    def t_shard(w):
        return jnp.swapaxes(w, 1, 2).astype(BF16)

    dn_in_pad = ((0, 0), (0, DN_IN_SHARD_PAD - DN_IN_SHARD), (0, 0))
    (w_att_in_t,) = _all_gather([t_shard(att_w_in)], [1], "allgather_first")
    w_att_in_t = w_att_in_t[0]
    gu_t, down = t_shard(ffn_w_gate_up), ffn_w_down.astype(BF16)
    gather_o, tok_o = _gather_begin([att_w_out.astype(BF16), mem_w_kv.astype(BF16)], [1, 1], w_att_in_t,
                                    "gather_att_out_start")
    gather_a, tok_a = _gather_begin([gu_t[0:1], down[0:1]], [1, 1], tok_o, "gather_ffn0_start")
    gather_b, tok_b = _gather_begin(
        [jnp.pad(t_shard(dn_w_in), dn_in_pad), dn_w_out.astype(BF16), gu_t[1:2], down[1:2], dn_conv],
        [1, 1, 1, 1, 0], tok_a, "gather_layer1_start")

    def gain(a, i):
        return a[i].reshape(1, D)

    (h0,) = _rowwise(_fn_pre, [x0], [gain(norm_mix_pre, 0) + tok_b[0:1, 0:1]], [(D, BF16)], ROW_TB, "pre0")
    p0 = _matmul(h0, w_att_in_t, "nt", BF16, "att_in")
    late = {}

    def kv0(after):
        late["w_att_out"], late["w_kv"] = _split_end(gather_o, after, "gather_att_out_wait")
        return _mem_kv_fwd(mem0, gain(mem_norm, 0), late["w_kv"][0], 0)

    cat0, res0, (km0, vm0, memn0) = _attn_mixer_fwd(p0, rel_bias, kv0)
    w_att_out, w_kv = late["w_att_out"][0], late["w_kv"]
    y0 = _matmul(cat0, w_att_out, "nn", F32, "att_out")
    g_a = [gain(norm_mix_post, 0), gain(norm_ffn_pre, 0)]
    x1, h1 = _rowwise(_fn_res_pre, [x0, y0], g_a, [(D, F32), (D, BF16)], ROW_TB, "res_pre0")
    w_gu_t0, w_down0 = [w[0] for w in _split_end(gather_a, h1, "gather_ffn0_wait")]
    f0, gu0, a0 = _ffn_fwd(h1, w_gu_t0, w_down0, 0)
    g_b = [gain(norm_ffn_post, 0), gain(norm_mix_pre, 1)]
    x2, h2 = _rowwise(_fn_res_pre, [x1, f0], g_b, [(D, F32), (D, BF16)], ROW_TB, "res_pre1")
    km1, vm1, memn1 = _mem_kv_fwd(mem0, gain(mem_norm, 1), w_kv[1], 1)
    w_dn_in_g, w_dn_out, w_gu_t1, w_down1, conv_g = _split_end(gather_b, h2, "gather_layer1_wait")
    w_dn_in_g, w_dn_out, w_gu_t1, w_down1 = w_dn_in_g[0], w_dn_out[0], w_gu_t1[0], w_down1[0]
    conv_full = conv_g.transpose(1, 0, 2).reshape(DN_CONV, 3 * TOK_W)
    w_dn_in_t = jnp.concatenate(
        [w_dn_in_g[DN_IN_SHARD_PAD * j:DN_IN_SHARD_PAD * j + DN_IN_SHARD] for j in range(N_DEV)]
        + [jnp.zeros((DN_IN_PAD - DN_IN, D), BF16)], axis=0)
    p1 = _matmul(h2, w_dn_in_t, "nt", F32, "dn_in")
    cat1, res1 = _dn_mixer_fwd(p1, conv_full, dn_a_log[0], dn_dt_bias[0], dn_out_norm[0], km1, vm1)
    y1 = _matmul(cat1, w_dn_out, "nn", F32, "dn_out")
    g_c = [gain(norm_mix_post, 1), gain(norm_ffn_pre, 1)]
    x3, h3 = _rowwise(_fn_res_pre, [x2, y1], g_c, [(D, F32), (D, BF16)], ROW_TB, "res_pre2")
    f1, gu1, a1 = _ffn_fwd(h3, w_gu_t1, w_down1, 1)
    g_d = [gain(norm_ffn_post, 1)]
    (x4,) = _rowwise(_fn_res, [x3, f1], g_d, [(D, F32)], ROW_TB, "res3")
    dx4, lrow = _loss_kernel(x4, tgt, ROW_TB, "loss")
    loss = lax.psum(lrow[0, 0] * (0.5 / D), axes)

    (df1,), (dg_fpost1,) = _rowwise_bwd(_fn_res, [x3, f1], g_d, [dx4], [None, BF16], ROW_TB, "res3_bwd")
    dh3, dwgu1, dwd1 = _ffn_bwd(df1, h3, w_gu_t1, w_down1, gu1, a1, 1)
    (dx2, dy1), (dg_mpost1, dg_fpre1) = _rowwise_bwd(_fn_res_pre, [x2, y1], g_c, [dx4, dh3], [F32, BF16],
                                                     ROW_TB, "res_pre2_bwd")
    dcat1 = _matmul(dy1, w_dn_out, "nt", F32, "dn_out_dx")
    dw_dn_out = _matmul(cat1, dy1, "tn", BF16, "dn_out_dw")
    dp1, dconv, da_log, ddt_bias, dout_norm, dkm1, dvm1 = _dn_mixer_bwd(dcat1, res1, km1, vm1)
    dwkv1, dg_mem1 = _mem_kv_bwd(mem0, gain(mem_norm, 1), w_kv[1], memn1, dkm1, dvm1, 1)
    dh2 = _matmul(dp1, w_dn_in_t, "nn", F32, "dn_in_dx")
    dw_dn_in_t = _matmul(dp1, h2, "tn", BF16, "dn_in_dw")
    dn_in_parts = [jnp.pad(dw_dn_in_t[DN_IN_SHARD * j:DN_IN_SHARD * (j + 1)],
                           ((0, DN_IN_SHARD_PAD - DN_IN_SHARD), (0, 0))) for j in range(N_DEV)]
    xch_b, tok = _exchange_begin(
        [jnp.concatenate(dn_in_parts, axis=0)[None], dw_dn_out[None], dwkv1[None], dwgu1[None], dwd1[None]],
        [1, 1, 1, 1, 1], dh2, "exchange_layer1_start")
    (dx1, df0), (dg_fpost0, dg_mpre1) = _rowwise_bwd(_fn_res_pre, [x1, f0], [g + tok[0:1, 0:1] for g in g_b],
                                                     [dx2, dh2], [F32, BF16], ROW_TB, "res_pre1_bwd")
    dh1, dwgu0, dwd0 = _ffn_bwd(df0, h1, w_gu_t0, w_down0, gu0, a0, 0)
    xch_a, tok = _exchange_begin([dwgu0[None], dwd0[None]], [1, 1], dh1, "exchange_ffn0_start")
    (dx0, dy0), (dg_mpost0, dg_fpre0) = _rowwise_bwd(_fn_res_pre, [x0, y0], [g + tok[0:1, 0:1] for g in g_a],
                                                     [dx1, dh1], [F32, BF16], ROW_TB, "res_pre0_bwd")
    dcat0 = _matmul(dy0, w_att_out, "nt", F32, "att_out_dx")
    dw_att_out = _matmul(cat0, dy0, "tn", BF16, "att_out_dw")
    dp0, drel, dkm0, dvm0 = _attn_mixer_bwd(dcat0, res0, km0, vm0)
    dwkv0, dg_mem0 = _mem_kv_bwd(mem0, gain(mem_norm, 0), w_kv[0], memn0, dkm0, dvm0, 0)
    xch_o, tok = _exchange_begin([dw_att_out[None], dwkv0[None]], [1, 1], dp0, "exchange_att_out_start")
    dh0 = _matmul(dp0, w_att_in_t, "nn", F32, "att_in_dx")
    dw_att_in_t = _matmul(dp0, h0, "tn", BF16, "att_in_dw")
    (grad_x,), (dg_mpre0,) = _rowwise_bwd(_fn_first, [x0], [gain(norm_mix_pre, 0) + tok[0:1, 0:1]], [dx0, dh0],
                                          [F32], ROW_TB, "pre0_bwd")

    small_grads = [drel, da_log, ddt_bias, dout_norm, jnp.concatenate([dg_mem0, dg_mem1]),
                   jnp.concatenate([dg_mpre0, dg_mpre1]), jnp.concatenate([dg_mpost0, dg_mpost1]),
                   jnp.concatenate([dg_fpre0, dg_fpre1]), jnp.concatenate([dg_fpost0, dg_fpost1]), dconv]
    r_att_in, r_small = _exchange([dw_att_in_t[None], _pack_small(small_grads, SMALL_ROWS)], [1, None],
                                  "exchange_last")
    r_att_out, r_kv0 = _split_end(xch_o, r_small, "exchange_att_out_wait")
    r_gu0, r_down0 = _split_end(xch_a, r_small, "exchange_ffn0_wait")
    r_dn_in, r_dn_out, r_kv1, r_gu1, r_down1 = _split_end(xch_b, r_small, "exchange_layer1_wait")

    def rows(a):
        return a.reshape((-1,) + a.shape[-1:])

    def row_sharded(recv, w, m, v, tb, name):
        outs = _adamw_reduce(recv.reshape((N_DEV, -1) + recv.shape[-1:]), rows(w), rows(m), rows(v), tb, name)
        return [o.reshape(w.shape) for o in outs]

    def col_sharded(recv, w, m, v, tb, name):
        g_t = _reduce8(recv.reshape((N_DEV, -1) + recv.shape[-1:]), tb, name + "_sum")
        g = jnp.swapaxes(g_t.reshape(recv.shape[1:])[:, :w.shape[2]], 1, 2)
        outs = _adamw(rows(g), rows(w), rows(m), rows(v), 256, name)
        return [g] + [o.reshape(w.shape) for o in outs]

    def per_layer(fn, recvs, w, m, v, tb, name):
        outs = [fn(r, w[l:l + 1], m[l:l + 1], v[l:l + 1], tb, f"{name}{l}") for l, r in enumerate(recvs)]
        return [jnp.concatenate(pair, axis=0) for pair in zip(*outs)]

    big = [col_sharded(r_att_in, att_w_in, m_att_w_in, v_att_w_in, 320, "adamw_att_in"),
           row_sharded(r_att_out, att_w_out, m_att_w_out, v_att_w_out, 128, "adamw_att_out"),
           col_sharded(r_dn_in, dn_w_in, m_dn_w_in, v_dn_w_in, 432, "adamw_dn_in"),
           row_sharded(r_dn_out, dn_w_out, m_dn_w_out, v_dn_w_out, 128, "adamw_dn_out"),
           per_layer(row_sharded, [r_kv0, r_kv1], mem_w_kv, m_mem_w_kv, v_mem_w_kv, 128, "adamw_mem_kv"),
           per_layer(col_sharded, [r_gu0, r_gu1], ffn_w_gate_up, m_ffn_w_gate_up, v_ffn_w_gate_up, 176,
                     "adamw_ffn_gu"),
           per_layer(row_sharded, [r_down0, r_down1], ffn_w_down, m_ffn_w_down, v_ffn_w_down, 176,
                     "adamw_ffn_down")]
    g_big, d_big, nm_big, nv_big = [[b[i] for b in big] for i in range(4)]

    g_small = _reduce8(r_small, SMALL_ROWS, "reduce_small")
    rep_shapes = [(32, 12), (1, 2, 6), (1, 2, 6), (1, 128), (2, D), (2, D), (2, D), (2, D), (2, D)]
    *g_rep, g_conv_full = _unpack_small(g_small, rep_shapes + [(DN_CONV, 3 * TOK_W)])
    me = _index(_me_xyc())
    g_conv = lax.dynamic_slice(g_conv_full, (0, me * 288), (DN_CONV, 288)).reshape(CONV_SHARD)
    small_shapes = rep_shapes + [CONV_SHARD]
    small_w = [rel_bias, dn_a_log, dn_dt_bias, dn_out_norm, mem_norm, norm_mix_pre, norm_mix_post,
               norm_ffn_pre, norm_ffn_post, dn_conv]
    small_m = [m_rel_bias, m_dn_a_log, m_dn_dt_bias, m_dn_out_norm, m_mem_norm, m_norm_mix_pre, m_norm_mix_post,
               m_norm_ffn_pre, m_norm_ffn_post, m_dn_conv]
    small_v = [v_rel_bias, v_dn_a_log, v_dn_dt_bias, v_dn_out_norm, v_mem_norm, v_norm_mix_pre, v_norm_mix_post,
               v_norm_ffn_pre, v_norm_ffn_post, v_dn_conv]
    g_small_list = g_rep + [g_conv]
    outs_small = _adamw(_pack_small(g_small_list, 24), _pack_small(small_w, 24), _pack_small(small_m, 24),
                        _pack_small(small_v, 24), 24, "adamw_small")
    d_small, nm_small, nv_small = [_unpack_small(o, small_shapes) for o in outs_small]

    def ordered(small, big):
        return [small[0], big[0], big[1], big[2], small[9], small[1], small[2], small[3], big[3], small[4],
                big[4], small[5], small[6], small[7], small[8], big[5], big[6]]

    g_small_out = [g.reshape(s) for g, s in zip(g_small_list, small_shapes)]
    return (loss, grad_x[None], *ordered(g_small_out, g_big), *ordered(d_small, d_big),
            *ordered(nm_small, nm_big), *ordered(nv_small, nv_big))
```

```python
import functools
import math

import numpy as np
import jax
import jax.numpy as jnp
from jax import lax
from jax.experimental import pallas as pl
from jax.experimental.pallas import tpu as pltpu

F32 = jnp.float32
BF16 = jnp.bfloat16
HI = lax.Precision.HIGHEST
MESH = pl.DeviceIdType.MESH

N_DEV = 8
D = 1024
EPS = 1e-6
NEG = -1e30
TOK_W = 768
MEM_W = 256
ATT_HD = 64
DIL_GROUPS = ((128, 1), (512, 4), (2048, 16))
BAND_HALF = 64
REL_BUCKETS = 32
REL_MAX_DIST = 1024
DN_HD = 128
DN_HEADS = 6
DN_CONV = 5
DN_CHUNK = 64
MEM_HEADS = 4
D_FF = 2816
ATT_IN = 2560
DN_IN = 3352
DN_IN_PAD = 3456

ADAM_LR, ADAM_B1, ADAM_B2, ADAM_EPS, ADAM_WD, ADAM_STEP = 0.001, 0.9, 0.999, 1e-08, 0.01, 10

PACK_C = 512
BIG_ROWS = 6480
SMALL_ROWS = 48
VMEM_LIMIT = 48 * 1024 * 1024


def _cparams(sem=None):
    kw = dict(vmem_limit_bytes=VMEM_LIMIT)
    if sem is not None:
        kw["dimension_semantics"] = sem
    return pltpu.CompilerParams(**kw)


def _tile(n, cap):
    if n <= cap:
        return n
    best = None
    for t in range(128, cap + 1, 128):
        if n % t == 0:
            best = t
    assert best is not None, (n, cap)
    return best


def _matmul(a, b, mode, out_dtype, name, tm=1024, tn=1408, tk=None):
    if tk is None:
        tk = 4096 if mode == "tn" else 2816
    if mode == "tn":
        tm = min(tm, 512)
    if mode == "nn":
        (m, kc), (_, n) = a.shape, b.shape
        dims = (((1,), (0,)), ((), ()))
    elif mode == "nt":
        (m, kc), (n, _) = a.shape, b.shape
        dims = (((1,), (1,)), ((), ()))
    else:
        (kc, m), (_, n) = a.shape, b.shape
        dims = (((0,), (0,)), ((), ()))
    tm = m if m <= tm else _tile(m, tm)
    tn = _tile(n, tn)
    tk = _tile(kc, tk)
    nk = kc // tk

    def body(a_ref, b_ref, o_ref, acc_ref):
        k = pl.program_id(2)
        part = lax.dot_general(a_ref[...], b_ref[...], dims, preferred_element_type=F32)

        @pl.when(k == 0)
        def _():
            acc_ref[...] = part

        @pl.when(k > 0)
        def _():
            acc_ref[...] += part

        @pl.when(k == nk - 1)
        def _():
            o_ref[...] = acc_ref[...].astype(o_ref.dtype)

    if mode == "nn":
        a_spec = pl.BlockSpec((tm, tk), lambda i, j, k: (i, k))
        b_spec = pl.BlockSpec((tk, tn), lambda i, j, k: (k, j))
    elif mode == "nt":
        a_spec = pl.BlockSpec((tm, tk), lambda i, j, k: (i, k))
        b_spec = pl.BlockSpec((tn, tk), lambda i, j, k: (j, k))
    else:
        a_spec = pl.BlockSpec((tk, tm), lambda i, j, k: (k, i))
        b_spec = pl.BlockSpec((tk, tn), lambda i, j, k: (k, j))
    return pl.pallas_call(
        body, name=name, grid=(m // tm, n // tn, nk),
        in_specs=[a_spec, b_spec],
        out_specs=pl.BlockSpec((tm, tn), lambda i, j, k: (i, j)),
        out_shape=jax.ShapeDtypeStruct((m, n), out_dtype),
        scratch_shapes=[pltpu.VMEM((tm, tn), F32)],
        compiler_params=_cparams(("parallel", "parallel", "arbitrary")),
    )(a, b)


def _rowwise(fn, rows, params, outs, tb, name):
    t = rows[0].shape[0]
    nr, npar = len(rows), len(params)

    def body(*refs):
        ins = [r[...].astype(F32) for r in refs[:nr + npar]]
        res = fn(*ins)
        for o_ref, r in zip(refs[nr + npar:], res):
            o_ref[...] = r.astype(o_ref.dtype)

    return pl.pallas_call(
        body, name=name, grid=(t // tb,),
        in_specs=[pl.BlockSpec((tb, r.shape[1]), lambda i: (i, 0)) for r in rows]
        + [pl.BlockSpec(p.shape, lambda i: (0, 0)) for p in params],
        out_specs=[pl.BlockSpec((tb, c), lambda i: (i, 0)) for c, _ in outs],
        out_shape=[jax.ShapeDtypeStruct((t, c), dt) for c, dt in outs],
        compiler_params=_cparams(("parallel",)),
    )(*rows, *params)


def _rowwise_bwd(fn, rows, params, cots, row_grad, tb, name):
    t = rows[0].shape[0]
    nr, npar, nc = len(rows), len(params), len(cots)
    want = [i for i, g in enumerate(row_grad) if g is not None]

    def body(*refs):
        ins = [r[...].astype(F32) for r in refs[:nr + npar]]
        cts = tuple(r[...].astype(F32) for r in refs[nr + npar:nr + npar + nc])
        outs = refs[nr + npar + nc:]
        _, vjp = jax.vjp(fn, *ins)
        grads = vjp(cts)
        for o_ref, i in zip(outs[:len(want)], want):
            o_ref[...] = grads[i].astype(o_ref.dtype)
        first = pl.program_id(0) == 0
        for o_ref, g in zip(outs[len(want):], grads[nr:]):
            @pl.when(first)
            def _(o_ref=o_ref, g=g):
                o_ref[...] = g

            @pl.when(jnp.logical_not(first))
            def _(o_ref=o_ref, g=g):
                o_ref[...] += g

    res = pl.pallas_call(
        body, name=name, grid=(t // tb,),
        in_specs=[pl.BlockSpec((tb, r.shape[1]), lambda i: (i, 0)) for r in rows]
        + [pl.BlockSpec(p.shape, lambda i: (0, 0)) for p in params]
        + [pl.BlockSpec((tb, c.shape[1]), lambda i: (i, 0)) for c in cots],
        out_specs=[pl.BlockSpec((tb, rows[i].shape[1]), lambda i_: (i_, 0)) for i in want]
        + [pl.BlockSpec(p.shape, lambda i: (0, 0)) for p in params],
        out_shape=[jax.ShapeDtypeStruct(rows[i].shape, row_grad[i]) for i in want]
        + [jax.ShapeDtypeStruct(p.shape, F32) for p in params],
        compiler_params=_cparams(("arbitrary",)),
    )(*rows, *params, *cots)
    return list(res[:len(want)]), list(res[len(want):])


def _rms(x, g):
    return x * lax.rsqrt(jnp.mean(x * x, axis=-1, keepdims=True) + EPS) * g


def _fn_pre(x, g):
    return (_rms(x, g),)


def _fn_res_pre(x, y, g_post, g_pre):
    x1 = x + _rms(y, g_post)
    return x1, _rms(x1, g_pre)


def _fn_res(x, y, g_post):
    return (x + _rms(y, g_post),)


def _sigmoid(x):
    return 1.0 / (1.0 + jnp.exp(-x))


def _silu(x):
    return x * _sigmoid(x)


def _fn_swiglu(gu):
    return (_silu(gu[:, :D_FF]) * gu[:, D_FF:],)


def _fn_combine(o, lse):
    ls = [lse[:, 256 * g:256 * (g + 1)] for g in range(3)]
    mx = lax.stop_gradient(jnp.maximum(jnp.maximum(ls[0], ls[1]), ls[2]))
    es = [jnp.exp(l - mx) for l in ls]
    inv = 1.0 / (es[0] + es[1] + es[2])
    return (jnp.concatenate([o[:, 256 * g:256 * (g + 1)] * (es[g] * inv) for g in range(3)], axis=1),)


def _fn_outnorm(o_f, o_r, z, gain):
    res = []
    for h in range(DN_HEADS):
        sl = slice(DN_HD * h, DN_HD * (h + 1))
        o = o_f[:, sl] + o_r[:, sl]
        res.append(o * lax.rsqrt(jnp.mean(o * o, axis=-1, keepdims=True) + EPS) * gain * _silu(z[:, sl]))
    return (jnp.concatenate(res, axis=1),)


def _loss_kernel(x, tgt, tb, name):
    t, d = x.shape

    def body(x_ref, t_ref, dx_ref, l_ref, acc_ref):
        i = pl.program_id(0)
        e = x_ref[...] - t_ref[...]
        dx_ref[...] = e * (1.0 / d)
        part = jnp.sum(e * e, axis=0, keepdims=True)

        @pl.when(i == 0)
        def _():
            acc_ref[...] = part

        @pl.when(i > 0)
        def _():
            acc_ref[...] += part

        @pl.when(i == t // tb - 1)
        def _():
            l_ref[...] = jnp.broadcast_to(jnp.sum(acc_ref[...], axis=-1, keepdims=True), (1, 128))

    return pl.pallas_call(
        body, name=name, grid=(t // tb,),
        in_specs=[pl.BlockSpec((tb, d), lambda i: (i, 0))] * 2,
        out_specs=[pl.BlockSpec((tb, d), lambda i: (i, 0)), pl.BlockSpec((1, 128), lambda i: (0, 0))],
        out_shape=[jax.ShapeDtypeStruct((t, d), F32), jax.ShapeDtypeStruct((1, 128), F32)],
        scratch_shapes=[pltpu.VMEM((1, d), F32)],
        compiler_params=_cparams(("arbitrary",)),
    )(x, tgt)


def _band_fn(l_sub, bq, i, q, kw, vw, bm):
    w = bq + 2 * BAND_HALF
    s = lax.dot_general((q * (ATT_HD ** -0.5)).astype(BF16), kw.astype(BF16), (((1,), (1,)), ((), ())),
                        preferred_element_type=F32) + bm
    kpos = i * bq - BAND_HALF + lax.broadcasted_iota(jnp.int32, (bq, w), 1)
    s = jnp.where((kpos >= 0) & (kpos < l_sub), s, NEG)
    m = lax.stop_gradient(jnp.max(s, axis=-1, keepdims=True))
    p = jnp.exp(s - m)
    den = jnp.sum(p, axis=-1, keepdims=True)
    o = jnp.dot(p.astype(BF16), vw.astype(BF16), preferred_element_type=F32) / den
    return o, jnp.broadcast_to(m + jnp.log(den), o.shape)


def _band_specs(dil, l_sub, bq):
    w = bq + 2 * BAND_HALF
    qs = pl.BlockSpec((None, bq, ATT_HD), lambda h, r, i: (h * dil + r, i, 0))
    ks = pl.BlockSpec((None, l_sub + 2 * BAND_HALF, ATT_HD), lambda h, r, i: (h * dil + r, 0, 0))
    bs = pl.BlockSpec((None, bq, w), lambda h, r, i: (h, 0, 0))
    return qs, ks, bs


def _band_fwd(q, k, v, bm, dil, l_sub, bq, name):
    w = bq + 2 * BAND_HALF
    qs, ks, bs = _band_specs(dil, l_sub, bq)

    def body(q_ref, k_ref, v_ref, bm_ref, o_ref, l_ref):
        i = pl.program_id(2)
        st = pl.multiple_of(i * bq, bq)
        o, lse = _band_fn(l_sub, bq, i, q_ref[...].astype(F32), k_ref[pl.ds(st, w), :].astype(F32),
                          v_ref[pl.ds(st, w), :].astype(F32), bm_ref[...])
        o_ref[...] = o
        l_ref[...] = lse

    return pl.pallas_call(
        body, name=name, grid=(4, dil, l_sub // bq),
        in_specs=[qs, ks, ks, bs], out_specs=[qs, qs],
        out_shape=[jax.ShapeDtypeStruct(q.shape, F32)] * 2,
        compiler_params=_cparams(("parallel", "parallel", "arbitrary")),
    )(q, k, v, bm)


def _band_bwd(q, k, v, bm, do, dlse, dil, l_sub, bq, name):
    w = bq + 2 * BAND_HALF
    qs, ks, bs = _band_specs(dil, l_sub, bq)

    def body(q_ref, k_ref, v_ref, bm_ref, do_ref, dl_ref, dq_ref, dk_ref, dv_ref, dbm_ref):
        r, i = pl.program_id(1), pl.program_id(2)
        st = pl.multiple_of(i * bq, bq)
        _, vjp = jax.vjp(functools.partial(_band_fn, l_sub, bq, i),
                         q_ref[...].astype(F32), k_ref[pl.ds(st, w), :].astype(F32),
                         v_ref[pl.ds(st, w), :].astype(F32), bm_ref[...])
        dq, dkw, dvw, dbm = vjp((do_ref[...].astype(F32), dl_ref[...]))
        dq_ref[...] = dq.astype(dq_ref.dtype)

        @pl.when(i == 0)
        def _():
            dk_ref[...] = jnp.zeros_like(dk_ref)
            dv_ref[...] = jnp.zeros_like(dv_ref)

        dk_ref[pl.ds(st, w), :] += dkw
        dv_ref[pl.ds(st, w), :] += dvw

        @pl.when((i == 0) & (r == 0))
        def _():
            dbm_ref[...] = dbm

        @pl.when((i > 0) | (r > 0))
        def _():
            dbm_ref[...] += dbm

    return pl.pallas_call(
        body, name=name, grid=(4, dil, l_sub // bq),
        in_specs=[qs, ks, ks, bs, qs, qs], out_specs=[qs, ks, ks, bs],
        out_shape=[jax.ShapeDtypeStruct(q.shape, BF16), jax.ShapeDtypeStruct(k.shape, F32),
                   jax.ShapeDtypeStruct(k.shape, F32), jax.ShapeDtypeStruct(bm.shape, F32)],
        compiler_params=_cparams(("parallel", "arbitrary", "arbitrary")),
    )(q, k, v, bm, do, dlse)


def _t5_bucket(rel):
    half = REL_BUCKETS // 2
    max_exact = half // 2
    n = np.abs(rel)
    large = max_exact + (np.log(np.maximum(n, 1) / max_exact) / math.log(REL_MAX_DIST / max_exact)
                         * (half - max_exact)).astype(np.int64)
    large = np.minimum(large, half - 1)
    return ((rel > 0) * half + np.where(n < max_exact, n, large)).astype(np.int32)


def _bucket_onehot(dil):
    idx = _t5_bucket(np.arange(-BAND_HALF, BAND_HALF + 1) * dil)
    oh = np.zeros((2 * BAND_HALF + 1, REL_BUCKETS), np.float32)
    oh[np.arange(2 * BAND_HALF + 1), idx] = 1.0
    return oh


def _band_bias(rel_bias, gi, dil, bq):
    w = bq + 2 * BAND_HALF
    nb = 2 * BAND_HALF + 1
    bias = jnp.dot(jnp.asarray(_bucket_onehot(dil)), rel_bias[:, 4 * gi:4 * gi + 4], precision=HI)
    row = jnp.concatenate([bias.T, jnp.full((4, w + 1 - nb), NEG, F32)], axis=1)
    flat = jnp.tile(row, (1, bq))[:, :bq * w]
    return flat.reshape(4, bq, w)


def _relbias_grad(dbms, name):
    nb = 2 * BAND_HALF + 1
    bq = max(d.shape[1] for d in dbms)
    skew = []
    for dbm in dbms:
        bqg, w = dbm.shape[1], dbm.shape[2]
        flat = jnp.pad(dbm.reshape(4, bqg * w), ((0, 0), (0, bqg)))
        skew.append(jnp.pad(flat.reshape(4, bqg, w + 1)[:, :, :nb], ((0, 0), (0, bq - bqg), (0, 256 - nb))))
    sk = jnp.concatenate(skew, axis=0)
    oh = np.zeros((3, 256, 128), np.float32)
    for gi, (_, dil) in enumerate(DIL_GROUPS):
        oh[gi, :2 * BAND_HALF + 1, :REL_BUCKETS] = _bucket_onehot(dil)

    def body(s_ref, oh_ref, o_ref):
        col = jnp.sum(s_ref[...], axis=0, keepdims=True)
        o_ref[...] = jnp.dot(jnp.broadcast_to(col, (8, 256)), oh_ref[...], precision=HI, preferred_element_type=F32)

    out = pl.pallas_call(
        body, name=name, grid=(12,),
        in_specs=[pl.BlockSpec((None, bq, 256), lambda n: (n, 0, 0)),
                  pl.BlockSpec((None, 256, 128), lambda n: (n // 4, 0, 0))],
        out_specs=pl.BlockSpec((None, 8, 128), lambda n: (n, 0, 0)),
        out_shape=jax.ShapeDtypeStruct((12, 8, 128), F32),
        compiler_params=_cparams(("parallel",)),
    )(sk, jnp.asarray(oh))
    return out[:, 0, :REL_BUCKETS].T


def _mem_fn(q, k, v):
    s = lax.dot_general((q * (ATT_HD ** -0.5)).astype(BF16), k.astype(BF16), (((1,), (1,)), ((), ())),
                        preferred_element_type=F32)
    m = lax.stop_gradient(jnp.max(s, axis=-1, keepdims=True))
    p = jnp.exp(s - m)
    p = p / jnp.sum(p, axis=-1, keepdims=True)
    return jnp.dot(p.astype(BF16), v.astype(BF16), preferred_element_type=F32)


def _mem_specs(tb, ml):
    qs = pl.BlockSpec((None, tb, ATT_HD), lambda h, i: (h, i, 0))
    ks = pl.BlockSpec((None, ml, ATT_HD), lambda h, i: (h, 0, 0))
    return qs, ks


def _mem_fwd(q, k, v, tb, name):
    qs, ks = _mem_specs(tb, k.shape[1])

    def body(q_ref, k_ref, v_ref, o_ref):
        o_ref[...] = _mem_fn(q_ref[...].astype(F32), k_ref[...], v_ref[...])

    return pl.pallas_call(
        body, name=name, grid=(MEM_HEADS, q.shape[1] // tb),
        in_specs=[qs, ks, ks], out_specs=qs, out_shape=jax.ShapeDtypeStruct(q.shape, F32),
        compiler_params=_cparams(("parallel", "parallel")),
    )(q, k, v)


def _mem_bwd(q, k, v, do, tb, name):
    qs, ks = _mem_specs(tb, k.shape[1])

    def body(q_ref, k_ref, v_ref, do_ref, dq_ref, dk_ref, dv_ref):
        i = pl.program_id(1)
        _, vjp = jax.vjp(_mem_fn, q_ref[...].astype(F32), k_ref[...], v_ref[...])
        dq, dk, dv = vjp(do_ref[...])
        dq_ref[...] = dq

        @pl.when(i == 0)
        def _():
            dk_ref[...] = dk
            dv_ref[...] = dv

        @pl.when(i > 0)
        def _():
            dk_ref[...] += dk
            dv_ref[...] += dv

    return pl.pallas_call(
        body, name=name, grid=(MEM_HEADS, q.shape[1] // tb),
        in_specs=[qs, ks, ks, qs], out_specs=[qs, ks, ks],
        out_shape=[jax.ShapeDtypeStruct(q.shape, F32), jax.ShapeDtypeStruct(k.shape, F32),
                   jax.ShapeDtypeStruct(k.shape, F32)],
        compiler_params=_cparams(("parallel", "arbitrary")),
    )(q, k, v, do)


CONV_PAD = 8


def _conv_post(kind, acc):
    s = _silu(acc)
    if kind == 2:
        return s
    scale = DN_HD ** -0.5 if kind == 0 else 1.0
    return s * lax.rsqrt(jnp.sum(s * s, axis=-1, keepdims=True) + EPS) * scale


def _conv_rows(x_ref, t, start, rt):
    lo = min(max(start, 0), t - rt)
    x = x_ref[pl.ds(lo, rt), :]
    shift = lo - start
    if shift == 0:
        return x
    x = pltpu.roll(x, shift % rt, axis=0)
    row = lax.broadcasted_iota(jnp.int32, x.shape, 0)
    return jnp.where((row >= shift) if shift > 0 else (row < rt + shift), x, 0.0)


def _conv_acc(x_ref, t, w, r0, rt):
    acc = None
    for i in range(DN_CONV):
        term = w[i:i + 1, :] * _conv_rows(x_ref, t, r0 + i - DN_CONV // 2, rt)
        acc = term if acc is None else acc + term
    return acc


def _conv_fwd(x, w8, kind, rt, name):
    t = x.shape[0]

    def body(x_ref, w_ref, o_ref):
        w = w_ref[...]
        for r in range(t // rt):
            o_ref[pl.ds(r * rt, rt), :] = _conv_post(kind, _conv_acc(x_ref, t, w, r * rt, rt))

    return pl.pallas_call(
        body, name=name, grid=(DN_HEADS,),
        in_specs=[pl.BlockSpec((t, DN_HD), lambda j: (0, 6 * kind + j)),
                  pl.BlockSpec((8, DN_HD), lambda j: (0, 6 * kind + j))],
        out_specs=pl.BlockSpec((t, DN_HD), lambda j: (0, j)),
        out_shape=jax.ShapeDtypeStruct((t, TOK_W), F32),
        compiler_params=_cparams(("parallel",)),
    )(x, w8)


def _conv_bwd(x, w8, d_f, d_r, kind, rt, name):
    t = x.shape[0]

    def body(xp_ref, w_ref, df_ref, dr_ref, dx_ref, dw_ref, dpad_ref):
        w = w_ref[...]
        zero = jnp.zeros((CONV_PAD, DN_HD), F32)
        dpad_ref[pl.ds(0, CONV_PAD), :] = zero
        dpad_ref[pl.ds(CONV_PAD + t, CONV_PAD), :] = zero
        dw = [jnp.zeros((1, DN_HD), F32) for _ in range(DN_CONV)]
        for r in range(t // rt):
            rows = pl.ds(r * rt, rt)
            acc = _conv_acc(xp_ref, t, w, r * rt, rt)
            _, vjp = jax.vjp(functools.partial(_conv_post, kind), acc)
            (dacc,) = vjp(df_ref[rows, :] + dr_ref[rows, :])
            dpad_ref[pl.ds(CONV_PAD + r * rt, rt), :] = dacc
            for i in range(DN_CONV):
                xs = _conv_rows(xp_ref, t, r * rt + i - DN_CONV // 2, rt)
                dw[i] = dw[i] + jnp.sum(dacc * xs, axis=0, keepdims=True)
        dw_ref[...] = jnp.concatenate(dw + [jnp.zeros((8 - DN_CONV, DN_HD), F32)], axis=0)
        for r in range(t // rt):
            acc = None
            for i in range(DN_CONV):
                term = w[i:i + 1, :] * dpad_ref[pl.ds(CONV_PAD + r * rt - i + DN_CONV // 2, rt), :]
                acc = term if acc is None else acc + term
            dx_ref[pl.ds(r * rt, rt), :] = acc.astype(dx_ref.dtype)

    return pl.pallas_call(
        body, name=name, grid=(DN_HEADS,),
        in_specs=[pl.BlockSpec((t, DN_HD), lambda j: (0, 6 * kind + j)),
                  pl.BlockSpec((8, DN_HD), lambda j: (0, 6 * kind + j)),
                  pl.BlockSpec((t, DN_HD), lambda j: (0, j)),
                  pl.BlockSpec((t, DN_HD), lambda j: (0, j))],
        out_specs=[pl.BlockSpec((t, DN_HD), lambda j: (0, j)), pl.BlockSpec((8, DN_HD), lambda j: (0, j))],
        out_shape=[jax.ShapeDtypeStruct((t, TOK_W), BF16), jax.ShapeDtypeStruct((8, TOK_W), F32)],
        scratch_shapes=[pltpu.VMEM((t + 2 * CONV_PAD, DN_HD), F32)],
        compiler_params=_cparams(("parallel",)),
    )(x, w8, d_f, d_r)


def _softplus(x):
    e = jnp.exp(-jnp.abs(x))
    return jnp.maximum(x, 0.0) + jnp.where(e < 1e-4, e - 0.5 * e * e, jnp.log(1.0 + e))


_NN = (((2,), (1,)), ((0,), (0,)))
_NT = (((2,), (2,)), ((0,), (0,)))
_TN = (((1,), (1,)), ((0,), (0,)))


def _dot(a, b, dims=_NN):
    return lax.dot_general(a.astype(BF16), b.astype(BF16), dims, preferred_element_type=F32)


def _hi_lo(x):
    hi = x.astype(BF16)
    return hi, (x - hi.astype(F32)).astype(BF16)


def _mask_dot(mask_bf16, x, dims):
    x1 = x.astype(BF16)
    r = x - x1.astype(F32)
    x2, x3 = _hi_lo(r)
    d = functools.partial(lax.dot_general, dimension_numbers=dims, preferred_element_type=F32)
    return d(mask_bf16, x1) + d(mask_bf16, x2) + d(mask_bf16, x3)


@jax.custom_vjp
def _dot_mask(mask_bf16, x):
    return _mask_dot(mask_bf16, x, _NN)


def _dot_mask_fwd(mask_bf16, x):
    return _mask_dot(mask_bf16, x, _NN), mask_bf16


def _dot_mask_bwd(mask_bf16, ct):
    return jnp.zeros_like(mask_bf16), _mask_dot(mask_bf16, ct, _TN)


_dot_mask.defvjp(_dot_mask_fwd, _dot_mask_bwd)


def _dot3_raw(a, b, dims):
    a1, a2 = _hi_lo(a)
    b1, b2 = _hi_lo(b)
    d = functools.partial(lax.dot_general, dimension_numbers=dims, preferred_element_type=F32)
    return d(a1, b1) + d(a1, b2) + d(a2, b1)


def _unit_solve_pass(lmat, rhs, masks):
    ainv = masks[6] - lmat * masks[0]
    for sh in range(1, 6):
        ainv = ainv - _dot3_raw(_dot3_raw(ainv, lmat * masks[sh], _NN), ainv, _NN)
    return _dot3_raw(ainv, rhs, _NN), ainv


@jax.custom_vjp
def _unit_solve(lmat, rhs, masks):
    return _unit_solve_pass(lmat, rhs, masks)[0]


def _unit_solve_fwd(lmat, rhs, masks):
    sol, ainv = _unit_solve_pass(lmat, rhs, masks)
    return sol, (sol, ainv, masks)


def _unit_solve_bwd(res, ct):
    sol, ainv, masks = res
    d_rhs = _dot3_raw(ainv, ct, _TN)
    return -_dot3_raw(d_rhs, sol, _NT), d_rhs, tuple(jnp.zeros_like(m) for m in masks)


_unit_solve.defvjp(_unit_solve_fwd, _unit_solve_bwd)


def _block_masks(rev, row, col):
    c = DN_CHUNK
    prow = jnp.where(rev, c - 1 - row, row)
    pcol = jnp.where(rev, c - 1 - col, col)
    masks = []
    for sh in range(6):
        differ = (prow ^ pcol) >> sh
        miss = (differ ^ 1) + (1 - ((prow >> sh) & 1))
        masks.append(jnp.where(miss == 0, 1.0, 0.0))
    masks.append(jnp.where(row == col, 1.0, 0.0))
    return tuple(masks)


def _dn_chunk(q, k, v, al, be, alc, a_row, dt_row, a_rowc, dt_rowc, s):
    n, c = q.shape[0], DN_CHUNK
    rev = lax.broadcasted_iota(jnp.int32, (n, c, c), 0) >= n // 2
    row = lax.broadcasted_iota(jnp.int32, (n, c, c), 1)
    col = lax.broadcasted_iota(jnp.int32, (n, c, c), 2)
    ahead = jnp.where(rev, col - row, row - col)
    incl = ahead >= 0
    strict = ahead > 0
    incl_b = incl.astype(BF16)

    g = -jnp.exp(a_row) * _softplus(al + dt_row)
    beta = _sigmoid(be)
    g_c = -jnp.exp(a_rowc) * _softplus(alc + dt_rowc)
    gc = _dot_mask(incl_b, g)
    gcc = _dot_mask(incl_b, g_c)
    decay = jnp.exp(jnp.where(incl, gcc - jnp.swapaxes(gcc, 1, 2), NEG))
    kb = k * beta
    lmat = jnp.where(strict, _dot(kb, k, _NT) * decay, 0.0)
    rhs = jnp.concatenate([v * beta, kb * jnp.exp(gc)], axis=2)
    sol = _unit_solve(lmat, rhs, _block_masks(rev, row, col))
    u, w = sol[:, :, :DN_HD], sol[:, :, DN_HD:]
    intra = jnp.where(incl, _dot(q, k, _NT) * decay, 0.0)
    v_new = u - _dot(w, s)
    out = _dot(q * jnp.exp(gc), s) + _dot(intra, v_new)
    g_last = jnp.sum(g, axis=1, keepdims=True)
    s_new = s * jnp.exp(g_last) + _dot(k * jnp.exp(g_last - gc), v_new, _TN)
    return out, s_new


DN_HG = 6


def _dn_load(f_refs, r_refs, alf, bef, alr, ber, a_ref, dt_ref):
    c, hg = DN_CHUNK, DN_HG
    sls = [slice(DN_HD * h, DN_HD * (h + 1)) for h in range(hg)]
    toks = [jnp.stack([f[:, sl] for sl in sls] + [r[:, sl] for sl in sls]) for f, r in zip(f_refs, r_refs)]
    al = jnp.concatenate([alf[...], alr[...]], axis=0)
    be = jnp.concatenate([bef[...], ber[...]], axis=0)
    alc = jnp.concatenate([alf[:, :, 0:c], alr[:, :, 0:c]], axis=0)
    a = jnp.concatenate([a_ref[0], a_ref[1]], axis=0)
    dt = jnp.concatenate([dt_ref[0], dt_ref[1]], axis=0)
    ac = jnp.concatenate([a_ref[0, :, :, 0:c], a_ref[1, :, :, 0:c]], axis=0)
    dtc = jnp.concatenate([dt_ref[0, :, :, 0:c], dt_ref[1, :, :, 0:c]], axis=0)
    return toks, (al, be, alc, a, dt, ac, dtc)


def _dn_views(nc, bwd):
    c, hg = DN_CHUNK, DN_HG
    if bwd:
        f_blk = lambda s: nc - 1 - s
        r_blk = lambda s: s
        st_blk = lambda s: nc - 1 - s
    else:
        f_blk = lambda s: s
        r_blk = lambda s: nc - 1 - s
        st_blk = lambda s: s
    tok_f = pl.BlockSpec((c, hg * DN_HD), lambda g, s: (f_blk(s), g))
    tok_r = pl.BlockSpec((c, hg * DN_HD), lambda g, s: (r_blk(s), g))
    gate_f = pl.BlockSpec((None, hg, c, DN_HD), lambda g, s: (0, g, f_blk(s), 0))
    gate_r = pl.BlockSpec((None, hg, c, DN_HD), lambda g, s: (1, g, r_blk(s), 0))
    par = pl.BlockSpec((2, hg, 1, DN_HD), lambda g, s: (0, g, 0, 0))
    state = pl.BlockSpec((2, hg, None, DN_HD, DN_HD), lambda g, s: (0, g, st_blk(s), 0, 0))
    return tok_f, tok_r, gate_f, gate_r, par, state


def _dn_fwd(q, k, v, al, be, a_rows, dt_rows, name):
    t = q.shape[0]
    c, hg = DN_CHUNK, DN_HG
    nc = t // c
    tok_f, tok_r, gate_f, gate_r, par, state = _dn_views(nc, False)

    def body(qf, kf, vf, qr, kr, vr, alf, bef, alr, ber, a_ref, dt_ref, of_ref, or_ref, st_ref, s_ref):
        @pl.when(pl.program_id(1) == 0)
        def _():
            s_ref[...] = jnp.zeros_like(s_ref)

        (q_, k_, v_), gates = _dn_load((qf, kf, vf), (qr, kr, vr), alf, bef, alr, ber, a_ref, dt_ref)
        s = s_ref[...]
        st_ref[0] = s[:hg]
        st_ref[1] = s[hg:]
        out, s_new = _dn_chunk(q_, k_, v_, *gates, s)
        for h in range(hg):
            sl = slice(DN_HD * h, DN_HD * (h + 1))
            of_ref[:, sl] = out[h]
            or_ref[:, sl] = out[hg + h]
        s_ref[...] = s_new

    return pl.pallas_call(
        body, name=name, grid=(DN_HEADS // hg, nc),
        in_specs=[tok_f] * 3 + [tok_r] * 3 + [gate_f, gate_f, gate_r, gate_r, par, par],
        out_specs=[tok_f, tok_r, state],
        out_shape=[jax.ShapeDtypeStruct((t, TOK_W), F32)] * 2
        + [jax.ShapeDtypeStruct((2, DN_HEADS, nc, DN_HD, DN_HD), F32)],
        scratch_shapes=[pltpu.VMEM((2 * hg, DN_HD, DN_HD), F32)],
        compiler_params=_cparams(("parallel", "arbitrary")),
    )(q, k, v, q, k, v, al, be, al, be, a_rows, dt_rows)


def _dn_bwd(q, k, v, al, be, a_rows, dt_rows, states, do, name):
    t = q.shape[0]
    c, hg = DN_CHUNK, DN_HG
    assert hg == DN_HEADS
    nc = t // c
    tok_f, tok_r, gate_f, gate_r, par, state = _dn_views(nc, True)
    gout_f = pl.BlockSpec((c, DN_HD), lambda g, s: (nc - 1 - s, 0))
    gout_r = pl.BlockSpec((c, DN_HD), lambda g, s: (s, 0))

    def body(qf, kf, vf, qr, kr, vr, alf, bef, alr, ber, a_ref, dt_ref, st_ref, dof, dor,
             dqf, dkf, dvf, dqr, dkr, dvr, dgf, dgr, da_ref, ddt_ref, ds_ref):
        first = pl.program_id(1) == 0

        @pl.when(first)
        def _():
            ds_ref[...] = jnp.zeros_like(ds_ref)
            da_ref[...] = jnp.zeros_like(da_ref)
            ddt_ref[...] = jnp.zeros_like(ddt_ref)

        def lanes(x):
            return jnp.sum(x, axis=-1, keepdims=True)

        (q_, k_, v_, do_), gates = _dn_load((qf, kf, vf, dof), (qr, kr, vr, dor), alf, bef, alr, ber, a_ref, dt_ref)
        s = jnp.concatenate([st_ref[0], st_ref[1]], axis=0)
        _, vjp = jax.vjp(_dn_chunk, q_, k_, v_, *gates, s)
        dq, dk, dv, dal, dbe, dalc, da, ddt, dac, ddtc, ds = vjp((do_, ds_ref[...]))
        for h in range(hg):
            sl = slice(DN_HD * h, DN_HD * (h + 1))
            dqf[:, sl], dkf[:, sl], dvf[:, sl] = dq[h], dk[h], dv[h]
            dqr[:, sl], dkr[:, sl], dvr[:, sl] = dq[hg + h], dk[hg + h], dv[hg + h]
        dal, dbe = lanes(dal) + lanes(dalc), lanes(dbe)
        lane = lax.broadcasted_iota(jnp.int32, (c, DN_HD), 1)
        for d, dg_ref in enumerate((dgf, dgr)):
            dg = jnp.zeros((c, DN_HD), F32)
            for h in range(hg):
                dg = jnp.where(lane == h, dal[d * hg + h], jnp.where(lane == hg + h, dbe[d * hg + h], dg))
            dg_ref[...] = dg
        da = jnp.broadcast_to(lanes(da) + lanes(dac), da.shape)
        ddt = jnp.broadcast_to(lanes(ddt) + lanes(ddtc), ddt.shape)
        da_ref[0] += da[:hg]
        da_ref[1] += da[hg:]
        ddt_ref[0] += ddt[:hg]
        ddt_ref[1] += ddt[hg:]
        ds_ref[...] = ds

    tok = jax.ShapeDtypeStruct((t, TOK_W), F32)
    gate = jax.ShapeDtypeStruct((t, DN_HD), F32)
    parsh = jax.ShapeDtypeStruct((2, DN_HEADS, 1, DN_HD), F32)
    res = pl.pallas_call(
        body, name=name, grid=(DN_HEADS // hg, nc),
        in_specs=[tok_f] * 3 + [tok_r] * 3 + [gate_f, gate_f, gate_r, gate_r, par, par, state, tok_f, tok_r],
        out_specs=[tok_f] * 3 + [tok_r] * 3 + [gout_f, gout_r, par, par],
        out_shape=[tok] * 6 + [gate] * 2 + [parsh] * 2,
        scratch_shapes=[pltpu.VMEM((2 * hg, DN_HD, DN_HD), F32)],
        compiler_params=_cparams(("parallel", "arbitrary")),
    )(q, k, v, q, k, v, al, be, al, be, a_rows, dt_rows, states, do, do)
    dqf, dkf, dvf, dqr, dkr, dvr, dgf, dgr, da, ddt = res
    dgate = jnp.concatenate([dgf[:, :2 * DN_HEADS], dgr[:, :2 * DN_HEADS]], axis=1)
    return (dqf, dkf, dvf), (dqr, dkr, dvr), dgate, da, ddt


BAND_BQ = 256
ROW_TB = 256
MEM_TB = 512
CONV_RT = 512


def _to_sub(x, dil):
    l = x.shape[0] // dil
    return x.reshape(l, dil, 4, ATT_HD).transpose(2, 1, 0, 3).reshape(4 * dil, l, ATT_HD)


def _from_sub(x, dil):
    l = x.shape[1]
    return x.reshape(4, dil, l, ATT_HD).transpose(2, 1, 0, 3).reshape(l * dil, 4 * ATT_HD)


def _heads_major(x):
    return x.reshape(x.shape[0], MEM_HEADS, ATT_HD).transpose(1, 0, 2)


def _heads_minor(x):
    return x.transpose(1, 0, 2).reshape(x.shape[1], MEM_HEADS * ATT_HD)


def _mem_kv_fwd(mem, gain, w_kv, li):
    (memn,) = _rowwise(_fn_pre, [mem], [gain], [(D, BF16)], mem.shape[0], f"memnorm_fwd{li}")
    kv = _matmul(memn, w_kv, "nn", F32, f"memkv_fwd{li}")
    return _heads_major(kv[:, :MEM_W]), _heads_major(kv[:, MEM_W:]), memn


def _mem_kv_bwd(mem, gain, w_kv, memn, dkm, dvm, li):
    dkv = jnp.concatenate([_heads_minor(dkm), _heads_minor(dvm)], axis=1).astype(BF16)
    dw = _matmul(memn, dkv, "tn", BF16, f"memkv_dw{li}")
    dmemn = _matmul(dkv, w_kv, "nt", F32, f"memkv_dx{li}")
    _, (dgain,) = _rowwise_bwd(_fn_pre, [mem], [gain], [dmemn], [None], mem.shape[0], f"memnorm_bwd{li}")
    return dw, dgain


def _attn_mixer_fwd(p, rel_bias, kv_fn):
    t = p.shape[0]
    saved, outs, lses = [], [], []
    for gi, (_, dil) in enumerate(DIL_GROUPS):
        l_sub = t // dil
        bq = min(BAND_BQ, l_sub)
        q = _to_sub(p[:, 256 * gi:256 * (gi + 1)], dil)
        pad = ((0, 0), (BAND_HALF, BAND_HALF), (0, 0))
        k = jnp.pad(_to_sub(p[:, TOK_W + 256 * gi:TOK_W + 256 * (gi + 1)], dil), pad)
        v = jnp.pad(_to_sub(p[:, 2 * TOK_W + 256 * gi:2 * TOK_W + 256 * (gi + 1)], dil), pad)
        bm = _band_bias(rel_bias, gi, dil, bq)
        o, lse = _band_fwd(q, k, v, bm, dil, l_sub, bq, f"band_fwd{gi}")
        outs.append(_from_sub(o, dil))
        lses.append(_from_sub(lse, dil))
        saved.append((q, k, v, bm))
    o_all = jnp.concatenate(outs, axis=1)
    lse_all = jnp.concatenate(lses, axis=1)
    (mixed,) = _rowwise(_fn_combine, [o_all, lse_all], [], [(TOK_W, BF16)], ROW_TB, "combine_fwd")
    qm = _heads_major(p[:, 3 * TOK_W:])
    km, vm, memn = kv_fn(mixed)
    memo = _mem_fwd(qm, km, vm, min(MEM_TB, t), "mem_fwd0")
    cat = jnp.concatenate([mixed, _heads_minor(memo).astype(BF16)], axis=1)
    return cat, (saved, o_all, lse_all, qm), (km, vm, memn)


def _attn_mixer_bwd(dcat, res, km, vm):
    saved, o_all, lse_all, qm = res
    t = dcat.shape[0]
    (do_all, dlse_all), _ = _rowwise_bwd(_fn_combine, [o_all, lse_all], [], [dcat[:, :TOK_W]], [BF16, F32],
                                         ROW_TB, "combine_bwd")
    dqs, dks, dvs, dbms = [], [], [], []
    for gi, (_, dil) in enumerate(DIL_GROUPS):
        l_sub = t // dil
        bq = min(BAND_BQ, l_sub)
        q, k, v, bm = saved[gi]
        do = _to_sub(do_all[:, 256 * gi:256 * (gi + 1)], dil)
        dl = _to_sub(dlse_all[:, 256 * gi:256 * (gi + 1)], dil)
        dq, dk, dv, dbm = _band_bwd(q, k, v, bm, do, dl, dil, l_sub, bq, f"band_bwd{gi}")
        dqs.append(_from_sub(dq, dil))
        dks.append(_from_sub(dk[:, BAND_HALF:-BAND_HALF], dil))
        dvs.append(_from_sub(dv[:, BAND_HALF:-BAND_HALF], dil))
        dbms.append(dbm)
    dqm, dkm, dvm = _mem_bwd(qm, km, vm, _heads_major(dcat[:, TOK_W:]), min(MEM_TB, t), "mem_bwd0")
    dp = jnp.concatenate([d.astype(BF16) for d in dqs + dks + dvs + [_heads_minor(dqm)]], axis=1)
    return dp, _relbias_grad(dbms, "relbias_grad"), dkm, dvm


def _dn_mixer_fwd(p, conv_w, a_log, dt_bias, out_norm, km, vm):
    t = p.shape[0]
    rt = min(CONV_RT, t)
    xp = p
    w8 = jnp.pad(conv_w, ((0, 8 - DN_CONV), (0, 0)))
    q = _conv_fwd(xp, w8, 0, rt, "conv_fwd_q")
    k = _conv_fwd(xp, w8, 1, rt, "conv_fwd_k")
    v = _conv_fwd(xp, w8, 2, rt, "conv_fwd_v")
    gate = p[:, 4 * TOK_W:4 * TOK_W + 4 * DN_HEADS].reshape(t, 2, 2, DN_HEADS)
    bshape = (2, DN_HEADS, t, DN_HD)
    al = jnp.broadcast_to(gate[:, :, 0, :].transpose(1, 2, 0)[..., None], bshape)
    be = jnp.broadcast_to(gate[:, :, 1, :].transpose(1, 2, 0)[..., None], bshape)
    a_rows = jnp.broadcast_to(a_log[:, :, None, None], (2, DN_HEADS, 1, DN_HD))
    dt_rows = jnp.broadcast_to(dt_bias[:, :, None, None], (2, DN_HEADS, 1, DN_HD))
    o_f, o_r, states = _dn_fwd(q, k, v, al, be, a_rows, dt_rows, "dn_fwd")
    z = p[:, 3 * TOK_W:4 * TOK_W]
    gain = out_norm.reshape(1, DN_HD)
    (og,) = _rowwise(_fn_outnorm, [o_f, o_r, z], [gain], [(TOK_W, BF16)], ROW_TB, "outnorm_fwd")
    qm = _heads_major(p[:, 4 * TOK_W + 4 * DN_HEADS:DN_IN])
    memo = _mem_fwd(qm, km, vm, min(MEM_TB, t), "mem_fwd1")
    cat = jnp.concatenate([og, _heads_minor(memo).astype(BF16)], axis=1)
    return cat, (xp, w8, q, k, v, al, be, a_rows, dt_rows, o_f, o_r, states, z, gain, qm)


def _dn_mixer_bwd(dcat, res, km, vm):
    xp, w8, q, k, v, al, be, a_rows, dt_rows, o_f, o_r, states, z, gain, qm = res
    t = dcat.shape[0]
    rt = min(CONV_RT, t)
    (do, dz), (dgain,) = _rowwise_bwd(_fn_outnorm, [o_f, o_r, z], [gain], [dcat[:, :TOK_W]], [F32, None, BF16],
                                      ROW_TB, "outnorm_bwd")
    d_f, d_r, dgate, da, ddt = _dn_bwd(q, k, v, al, be, a_rows, dt_rows, states, do, "dn_bwd")
    dxs, dws = [], []
    for kind, nm in enumerate("qkv"):
        dx, dw = _conv_bwd(xp, w8, d_f[kind], d_r[kind], kind, rt, f"conv_bwd_{nm}")
        dxs.append(dx)
        dws.append(dw)
    dconv = jnp.concatenate(dws, axis=1)[:DN_CONV]
    dqm, dkm, dvm = _mem_bwd(qm, km, vm, _heads_major(dcat[:, TOK_W:]), min(MEM_TB, t), "mem_bwd1")
    dp = jnp.concatenate(dxs + [dz, dgate.astype(BF16), _heads_minor(dqm).astype(BF16),
                               jnp.zeros((t, DN_IN_PAD - DN_IN), BF16)], axis=1)
    return dp, dconv, da[:, :, 0, 0], ddt[:, :, 0, 0], dgain.reshape(DN_HD), dkm, dvm


SWI_TB = 256


def _ffn_fwd(h, w_gu_t, w_d, li):
    gu = _matmul(h, w_gu_t, "nt", BF16, f"ffn_gu{li}")
    (a,) = _rowwise(_fn_swiglu, [gu], [], [(D_FF, BF16)], SWI_TB, f"swiglu_fwd{li}")
    return _matmul(a, w_d, "nn", F32, f"ffn_down{li}"), gu, a


def _ffn_bwd(df, h, w_gu_t, w_d, gu, a, li):
    da = _matmul(df, w_d, "nt", BF16, f"ffn_down_dx{li}")
    dwd = _matmul(a, df, "tn", BF16, f"ffn_down_dw{li}")
    (dgu,), _ = _rowwise_bwd(_fn_swiglu, [gu], [], [da], [BF16], SWI_TB, f"swiglu_bwd{li}")
    dh = _matmul(dgu, w_gu_t, "nn", F32, f"ffn_gu_dx{li}")
    dwgu_t = _matmul(dgu, h, "tn", BF16, f"ffn_gu_dw{li}")
    return dh, dwgu_t, dwd


def _fn_first(x, g):
    return x, _rms(x, g)


def _me_xyc():
    return lax.axis_index("x"), lax.axis_index("y"), lax.axis_index("c")


def _flip(coords, k):
    x, y, c = coords
    return (1 - x if k & 4 else x, 1 - y if k & 2 else y, 1 - c if k & 1 else c)


def _index(coords):
    x, y, c = coords
    return 4 * x + 2 * y + c


def _window(ref, axis, size, d):
    idx = [slice(None)] * len(ref.shape)
    idx[axis] = pl.ds(pl.multiple_of(d * size, size), size)
    return ref.at[tuple(idx)]


def _comm_call(body, n, ins, out_shapes, name):
    hbm = pl.BlockSpec(memory_space=pl.ANY)
    return pl.pallas_call(
        body, name=name, in_specs=[hbm] * n, out_specs=[hbm] * n, out_shape=out_shapes,
        scratch_shapes=[pltpu.SemaphoreType.DMA((N_DEV - 1, n)), pltpu.SemaphoreType.DMA((N_DEV - 1, n)),
                        pltpu.SemaphoreType.DMA((n,))],
    )(*ins)


def _run_exchange(n, local, remote, send_sems, recv_sems):
    me = _me_xyc()
    locs = [local(p) for p in range(n)]
    for cp in locs:
        cp.start()
    sends = [remote(k, p, me, _flip(me, k)) for k in range(1, N_DEV) for p in range(n)]
    for cp in sends:
        cp.start()
    for k in range(1, N_DEV):
        for p in range(n):
            remote(k, p, _flip(me, k), me).wait_recv()
    for cp in sends:
        cp.wait_send()
    for cp in locs:
        cp.wait()


def _all_gather(shards, axes, name):
    n = len(shards)
    sizes = [s.shape[a] for s, a in zip(shards, axes)]

    def body(*refs):
        ins, outs = refs[:n], refs[n:2 * n]
        send_sems, recv_sems, loc_sems = refs[2 * n:]
        me = _me_xyc()

        def local(p):
            return pltpu.make_async_copy(ins[p], _window(outs[p], axes[p], sizes[p], _index(me)), loc_sems.at[p])

        def remote(k, p, owner, to):
            return pltpu.make_async_remote_copy(
                src_ref=ins[p], dst_ref=_window(outs[p], axes[p], sizes[p], _index(owner)),
                send_sem=send_sems.at[k - 1, p], recv_sem=recv_sems.at[k - 1, p], device_id=to, device_id_type=MESH)

        _run_exchange(n, local, remote, send_sems, recv_sems)

    def full(s, a):
        return s.shape[:a] + (N_DEV * s.shape[a],) + s.shape[a + 1:]

    return _comm_call(body, n, shards, [jax.ShapeDtypeStruct(full(s, a), s.dtype) for s, a in zip(shards, axes)], name)


def _exchange(fulls, axes, name):
    n = len(fulls)
    sizes = [None if a is None else f.shape[a] // N_DEV for f, a in zip(fulls, axes)]

    def part_shape(f, a):
        return f.shape if a is None else f.shape[:a] + (f.shape[a] // N_DEV,) + f.shape[a + 1:]

    def body(*refs):
        ins, outs = refs[:n], refs[n:2 * n]
        send_sems, recv_sems, loc_sems = refs[2 * n:]
        me = _me_xyc()

        def src(p, to):
            return ins[p] if axes[p] is None else _window(ins[p], axes[p], sizes[p], _index(to))

        def local(p):
            return pltpu.make_async_copy(src(p, me), outs[p].at[_index(me)], loc_sems.at[p])

        def remote(k, p, sender, to):
            return pltpu.make_async_remote_copy(
                src_ref=src(p, to), dst_ref=outs[p].at[_index(sender)],
                send_sem=send_sems.at[k - 1, p], recv_sem=recv_sems.at[k - 1, p], device_id=to, device_id_type=MESH)

        _run_exchange(n, local, remote, send_sems, recv_sems)

    return _comm_call(body, n, fulls,
                      [jax.ShapeDtypeStruct((N_DEV,) + part_shape(f, a), f.dtype) for f, a in zip(fulls, axes)], name)


_HBM = pl.BlockSpec(memory_space=pltpu.HBM)
_SEM = pl.BlockSpec(memory_space=pltpu.SEMAPHORE)
_EFFECT = pltpu.SideEffectType.DATAFLOW_SIDE_EFFECTING


def _in_hbm(a):
    return pltpu.with_memory_space_constraint(a, pltpu.HBM)


def _split_start(srcs, lands, after, descr, name):
    n = len(srcs)

    def body(*refs):
        ins, lnd = refs[:n], refs[n:2 * n]
        send_sems, recv_sems = refs[2 * n + 1], refs[2 * n + 2]
        token = refs[-1]
        me = _me_xyc()
        for k in range(1, N_DEV):
            for p in range(n):
                descr(k, p, ins, lnd, send_sems, recv_sems, me, _flip(me, k)).start()
        token[...] = jnp.zeros_like(token)

    sems = pltpu.SemaphoreType.DMA(((N_DEV - 1) * n,))
    res = pl.pallas_call(
        body, name=name,
        out_shape=(sems, sems, *[pltpu.HBM(a.shape, a.dtype) for a in (*srcs, *lands)],
                   jax.ShapeDtypeStruct((8, 128), F32)),
        in_specs=[_HBM] * (2 * n) + [pl.BlockSpec(memory_space=pl.ANY)],
        out_specs=(_SEM, _SEM, *[_HBM] * (2 * n), pl.BlockSpec(memory_space=pltpu.VMEM)),
        input_output_aliases={i: 2 + i for i in range(2 * n)},
        compiler_params=pltpu.CompilerParams(has_side_effects=_EFFECT),
    )(*[_in_hbm(a) for a in (*srcs, *lands)], after)
    return res[0], res[1], res[2:2 + n], res[2 + n:2 + 2 * n], res[-1]


def _split_wait(send_sems, recv_sems, srcs, lands, after, descr, name):
    n = len(srcs)

    def body(*refs):
        ins, lnd = refs[:n], refs[n:2 * n]
        s_sems, r_sems = refs[2 * n], refs[2 * n + 1]
        me = _me_xyc()
        for k in range(1, N_DEV):
            for p in range(n):
                peer = _flip(me, k)
                descr(k, p, ins, lnd, s_sems, r_sems, me, peer).wait_send()
                descr(k, p, ins, lnd, s_sems, r_sems, peer, me).wait_recv()

    res = pl.pallas_call(
        body, name=name,
        out_shape=tuple(pltpu.HBM(a.shape, a.dtype) for a in (*srcs, *lands)),
        in_specs=[_HBM] * (2 * n) + [_SEM, _SEM, pl.BlockSpec(memory_space=pl.ANY)],
        out_specs=tuple([_HBM] * (2 * n)),
        input_output_aliases={i: i for i in range(2 * n)},
        compiler_params=pltpu.CompilerParams(has_side_effects=_EFFECT),
    )(*srcs, *lands, send_sems, recv_sems, after)
    return list(res[n:])


def _gather_descr(axes, sizes):
    def descr(k, p, ins, lnd, send_sems, recv_sems, sender, dest):
        return pltpu.make_async_remote_copy(
            src_ref=ins[p], dst_ref=_window(lnd[p], axes[p], sizes[p], _index(sender)),
            send_sem=send_sems.at[(k - 1) * len(axes) + p], recv_sem=recv_sems.at[(k - 1) * len(axes) + p],
            device_id=dest, device_id_type=MESH)
    return descr


def _exchange_descr(axes, sizes):
    def descr(k, p, ins, lnd, send_sems, recv_sems, sender, dest):
        return pltpu.make_async_remote_copy(
            src_ref=_window(ins[p], axes[p], sizes[p], _index(dest)), dst_ref=lnd[p].at[_index(sender)],
            send_sem=send_sems.at[(k - 1) * len(axes) + p], recv_sem=recv_sems.at[(k - 1) * len(axes) + p],
            device_id=dest, device_id_type=MESH)
    return descr


def _gather_begin(shards, axes, after, name):
    sizes = [s.shape[a] for s, a in zip(shards, axes)]
    me = _index(_me_xyc())
    lands = []
    for s, a, sz in zip(shards, axes, sizes):
        full = s.shape[:a] + (N_DEV * sz,) + s.shape[a + 1:]
        lands.append(lax.dynamic_update_slice_in_dim(lax.empty(full, s.dtype), s, me * sz, a))
    descr = _gather_descr(axes, sizes)
    send_sems, recv_sems, srcs, lands, token = _split_start(shards, lands, after, descr, name)
    return (send_sems, recv_sems, srcs, lands, descr), token


def _exchange_begin(fulls, axes, after, name):
    sizes = [f.shape[a] // N_DEV for f, a in zip(fulls, axes)]
    me = _index(_me_xyc())
    lands = []
    for f, a, sz in zip(fulls, axes, sizes):
        own = lax.dynamic_slice_in_dim(f, me * sz, sz, a)
        lands.append(lax.dynamic_update_slice_in_dim(lax.empty((N_DEV,) + own.shape, f.dtype), own[None], me, 0))
    descr = _exchange_descr(axes, sizes)
    send_sems, recv_sems, srcs, lands, token = _split_start(fulls, lands, after, descr, name)
    return (send_sems, recv_sems, srcs, lands, descr), token


def _split_end(handle, after, name):
    send_sems, recv_sems, srcs, lands, descr = handle
    return _split_wait(send_sems, recv_sems, srcs, lands, after, descr, name)


def _adam_math(g, w, m, v):
    m = ADAM_B1 * m + (1.0 - ADAM_B1) * g
    v = ADAM_B2 * v + (1.0 - ADAM_B2) * (g * g)
    m_hat = m / (1.0 - ADAM_B1 ** ADAM_STEP)
    v_hat = v / (1.0 - ADAM_B2 ** ADAM_STEP)
    delta = -ADAM_LR * (m_hat / (jnp.sqrt(v_hat) + ADAM_EPS) + ADAM_WD * w)
    return delta, m, v


def _sum_slabs(r_ref):
    g = r_ref[0].astype(F32)
    for s in range(1, N_DEV):
        g = g + r_ref[s].astype(F32)
    return g


def _adamw_reduce(recv, w, m, v, tb, name):
    r, c = w.shape

    def body(r_ref, w_ref, m_ref, v_ref, g_ref, d_ref, nm_ref, nv_ref):
        g = _sum_slabs(r_ref)
        g_ref[...] = g
        d_ref[...], nm_ref[...], nv_ref[...] = _adam_math(g, w_ref[...], m_ref[...], v_ref[...])

    blk = pl.BlockSpec((tb, c), lambda i: (i, 0))
    return pl.pallas_call(
        body, name=name, grid=(r // tb,),
        in_specs=[pl.BlockSpec((N_DEV, tb, c), lambda i: (0, i, 0)), blk, blk, blk],
        out_specs=[blk] * 4, out_shape=[jax.ShapeDtypeStruct((r, c), F32)] * 4,
        compiler_params=_cparams(("parallel",)),
    )(recv, w, m, v)


def _reduce8(recv, tb, name):
    r, c = recv.shape[1:]

    def body(r_ref, g_ref):
        g_ref[...] = _sum_slabs(r_ref)

    return pl.pallas_call(
        body, name=name, grid=(r // tb,),
        in_specs=[pl.BlockSpec((N_DEV, tb, c), lambda i: (0, i, 0))],
        out_specs=pl.BlockSpec((tb, c), lambda i: (i, 0)), out_shape=jax.ShapeDtypeStruct((r, c), F32),
        compiler_params=_cparams(("parallel",)),
    )(recv)


def _adamw(g, w, m, v, tb, name):
    r, c = w.shape

    def body(g_ref, w_ref, m_ref, v_ref, d_ref, nm_ref, nv_ref):
        d_ref[...], nm_ref[...], nv_ref[...] = _adam_math(g_ref[...], w_ref[...], m_ref[...], v_ref[...])

    blk = pl.BlockSpec((tb, c), lambda i: (i, 0))
    return pl.pallas_call(
        body, name=name, grid=(r // tb,), in_specs=[blk] * 4, out_specs=[blk] * 3,
        out_shape=[jax.ShapeDtypeStruct((r, c), F32)] * 3, compiler_params=_cparams(("parallel",)),
    )(g, w, m, v)


DN_IN_SHARD = DN_IN // N_DEV
DN_IN_SHARD_PAD = 432
CONV_SHARD = (1, DN_CONV, 288)


def _pack_small(arrs, rows):
    flat = jnp.concatenate([a.astype(F32).reshape(-1) for a in arrs])
    return jnp.pad(flat, (0, rows * PACK_C - flat.size)).reshape(rows, PACK_C)


def _unpack_small(packed, shapes):
    flat, out, off = packed.reshape(-1), [], 0
    for shp in shapes:
        n = int(np.prod(shp))
        out.append(flat[off:off + n].reshape(shp))
        off += n
    return out


def kernel(x, mem, rel_bias, att_w_in, att_w_out, dn_w_in, dn_conv, dn_a_log, dn_dt_bias, dn_out_norm, dn_w_out, mem_norm, mem_w_kv, norm_mix_pre, norm_mix_post, norm_ffn_pre, norm_ffn_post, ffn_w_gate_up, ffn_w_down, loss_target, m_rel_bias, m_att_w_in, m_att_w_out, m_dn_w_in, m_dn_conv, m_dn_a_log, m_dn_dt_bias, m_dn_out_norm, m_dn_w_out, m_mem_norm, m_mem_w_kv, m_norm_mix_pre, m_norm_mix_post, m_norm_ffn_pre, m_norm_ffn_post, m_ffn_w_gate_up, m_ffn_w_down, v_rel_bias, v_att_w_in, v_att_w_out, v_dn_w_in, v_dn_conv, v_dn_a_log, v_dn_dt_bias, v_dn_out_norm, v_dn_w_out, v_mem_norm, v_mem_w_kv, v_norm_mix_pre, v_norm_mix_post, v_norm_ffn_pre, v_norm_ffn_post, v_ffn_w_gate_up, v_ffn_w_down):
    x0, mem0, tgt = x[0], mem[0], loss_target[0]
    t = x0.shape[0]
    axes = ("x", "y", "c")

    def t_shard(w):
        return jnp.swapaxes(w, 1, 2).astype(BF16)

    dn_in_pad = ((0, 0), (0, DN_IN_SHARD_PAD - DN_IN_SHARD), (0, 0))
    (w_att_in_t,) = _all_gather([t_shard(att_w_in)], [1], "allgather_first")
    w_att_in_t = w_att_in_t[0]
    gu_t, down = t_shard(ffn_w_gate_up), ffn_w_down.astype(BF16)
    gather_o, tok_o = _gather_begin([att_w_out.astype(BF16), mem_w_kv.astype(BF16)], [1, 1], w_att_in_t,
                                    "gather_att_out_start")
    gather_a, tok_a = _gather_begin([gu_t[0:1], down[0:1]], [1, 1], tok_o, "gather_ffn0_start")
    gather_b, tok_b = _gather_begin(
        [jnp.pad(t_shard(dn_w_in), dn_in_pad), dn_w_out.astype(BF16), gu_t[1:2], down[1:2], dn_conv],
        [1, 1, 1, 1, 0], tok_a, "gather_layer1_start")

    def gain(a, i):
        return a[i].reshape(1, D)

    (h0,) = _rowwise(_fn_pre, [x0], [gain(norm_mix_pre, 0) + tok_b[0:1, 0:1]], [(D, BF16)], ROW_TB, "pre0")
    p0 = _matmul(h0, w_att_in_t, "nt", BF16, "att_in")
    late = {}

    def kv0(after):
        late["w_att_out"], late["w_kv"] = _split_end(gather_o, after, "gather_att_out_wait")
        return _mem_kv_fwd(mem0, gain(mem_norm, 0), late["w_kv"][0], 0)

    cat0, res0, (km0, vm0, memn0) = _attn_mixer_fwd(p0, rel_bias, kv0)
    w_att_out, w_kv = late["w_att_out"][0], late["w_kv"]
    y0 = _matmul(cat0, w_att_out, "nn", F32, "att_out")
    g_a = [gain(norm_mix_post, 0), gain(norm_ffn_pre, 0)]
    x1, h1 = _rowwise(_fn_res_pre, [x0, y0], g_a, [(D, F32), (D, BF16)], ROW_TB, "res_pre0")
    w_gu_t0, w_down0 = [w[0] for w in _split_end(gather_a, h1, "gather_ffn0_wait")]
    f0, gu0, a0 = _ffn_fwd(h1, w_gu_t0, w_down0, 0)
    g_b = [gain(norm_ffn_post, 0), gain(norm_mix_pre, 1)]
    x2, h2 = _rowwise(_fn_res_pre, [x1, f0], g_b, [(D, F32), (D, BF16)], ROW_TB, "res_pre1")
    km1, vm1, memn1 = _mem_kv_fwd(mem0, gain(mem_norm, 1), w_kv[1], 1)
    w_dn_in_g, w_dn_out, w_gu_t1, w_down1, conv_g = _split_end(gather_b, h2, "gather_layer1_wait")
    w_dn_in_g, w_dn_out, w_gu_t1, w_down1 = w_dn_in_g[0], w_dn_out[0], w_gu_t1[0], w_down1[0]
    conv_full = conv_g.transpose(1, 0, 2).reshape(DN_CONV, 3 * TOK_W)
    w_dn_in_t = jnp.concatenate(
        [w_dn_in_g[DN_IN_SHARD_PAD * j:DN_IN_SHARD_PAD * j + DN_IN_SHARD] for j in range(N_DEV)]
        + [jnp.zeros((DN_IN_PAD - DN_IN, D), BF16)], axis=0)
    p1 = _matmul(h2, w_dn_in_t, "nt", F32, "dn_in")
    cat1, res1 = _dn_mixer_fwd(p1, conv_full, dn_a_log[0], dn_dt_bias[0], dn_out_norm[0], km1, vm1)
    y1 = _matmul(cat1, w_dn_out, "nn", F32, "dn_out")
    g_c = [gain(norm_mix_post, 1), gain(norm_ffn_pre, 1)]
    x3, h3 = _rowwise(_fn_res_pre, [x2, y1], g_c, [(D, F32), (D, BF16)], ROW_TB, "res_pre2")
    f1, gu1, a1 = _ffn_fwd(h3, w_gu_t1, w_down1, 1)
    g_d = [gain(norm_ffn_post, 1)]
    (x4,) = _rowwise(_fn_res, [x3, f1], g_d, [(D, F32)], ROW_TB, "res3")
    dx4, lrow = _loss_kernel(x4, tgt, ROW_TB, "loss")
    loss = lax.psum(lrow[0, 0] * (0.5 / D), axes)

    (df1,), (dg_fpost1,) = _rowwise_bwd(_fn_res, [x3, f1], g_d, [dx4], [None, BF16], ROW_TB, "res3_bwd")
    dh3, dwgu1, dwd1 = _ffn_bwd(df1, h3, w_gu_t1, w_down1, gu1, a1, 1)
    (dx2, dy1), (dg_mpost1, dg_fpre1) = _rowwise_bwd(_fn_res_pre, [x2, y1], g_c, [dx4, dh3], [F32, BF16],
                                                     ROW_TB, "res_pre2_bwd")
    dcat1 = _matmul(dy1, w_dn_out, "nt", F32, "dn_out_dx")
    dw_dn_out = _matmul(cat1, dy1, "tn", BF16, "dn_out_dw")
    dp1, dconv, da_log, ddt_bias, dout_norm, dkm1, dvm1 = _dn_mixer_bwd(dcat1, res1, km1, vm1)
    dwkv1, dg_mem1 = _mem_kv_bwd(mem0, gain(mem_norm, 1), w_kv[1], memn1, dkm1, dvm1, 1)
    dh2 = _matmul(dp1, w_dn_in_t, "nn", F32, "dn_in_dx")
    dw_dn_in_t = _matmul(dp1, h2, "tn", BF16, "dn_in_dw")
    dn_in_parts = [jnp.pad(dw_dn_in_t[DN_IN_SHARD * j:DN_IN_SHARD * (j + 1)],
                           ((0, DN_IN_SHARD_PAD - DN_IN_SHARD), (0, 0))) for j in range(N_DEV)]
    xch_b, tok = _exchange_begin(
        [jnp.concatenate(dn_in_parts, axis=0)[None], dw_dn_out[None], dwkv1[None], dwgu1[None], dwd1[None]],
        [1, 1, 1, 1, 1], dh2, "exchange_layer1_start")
    (dx1, df0), (dg_fpost0, dg_mpre1) = _rowwise_bwd(_fn_res_pre, [x1, f0], [g + tok[0:1, 0:1] for g in g_b],
                                                     [dx2, dh2], [F32, BF16], ROW_TB, "res_pre1_bwd")
    dh1, dwgu0, dwd0 = _ffn_bwd(df0, h1, w_gu_t0, w_down0, gu0, a0, 0)
    xch_a, tok = _exchange_begin([dwgu0[None], dwd0[None]], [1, 1], dh1, "exchange_ffn0_start")
    (dx0, dy0), (dg_mpost0, dg_fpre0) = _rowwise_bwd(_fn_res_pre, [x0, y0], [g + tok[0:1, 0:1] for g in g_a],
                                                     [dx1, dh1], [F32, BF16], ROW_TB, "res_pre0_bwd")
    dcat0 = _matmul(dy0, w_att_out, "nt", F32, "att_out_dx")
    dw_att_out = _matmul(cat0, dy0, "tn", BF16, "att_out_dw")
    dp0, drel, dkm0, dvm0 = _attn_mixer_bwd(dcat0, res0, km0, vm0)
    dwkv0, dg_mem0 = _mem_kv_bwd(mem0, gain(mem_norm, 0), w_kv[0], memn0, dkm0, dvm0, 0)
    xch_o, tok = _exchange_begin([dw_att_out[None], dwkv0[None]], [1, 1], dp0, "exchange_att_out_start")
    dw_att_in_t = _matmul(dp0, h0, "tn", BF16, "att_in_dw")
    xch_i, tok_i = _exchange_begin([dw_att_in_t[None]], [1], tok, "exchange_att_in_start")
    dh0 = _matmul(dp0, w_att_in_t, "nn", F32, "att_in_dx")
    (grad_x,), (dg_mpre0,) = _rowwise_bwd(_fn_first, [x0], [gain(norm_mix_pre, 0) + tok_i[0:1, 0:1]], [dx0, dh0],
                                          [F32], ROW_TB, "pre0_bwd")

    small_grads = [drel, da_log, ddt_bias, dout_norm, jnp.concatenate([dg_mem0, dg_mem1]),
                   jnp.concatenate([dg_mpre0, dg_mpre1]), jnp.concatenate([dg_mpost0, dg_mpost1]),
                   jnp.concatenate([dg_fpre0, dg_fpre1]), jnp.concatenate([dg_fpost0, dg_fpost1]), dconv]
    (r_small,) = _exchange([_pack_small(small_grads, SMALL_ROWS)], [None], "exchange_last")
    (r_att_in,) = _split_end(xch_i, r_small, "exchange_att_in_wait")
    r_att_out, r_kv0 = _split_end(xch_o, r_small, "exchange_att_out_wait")
    r_gu0, r_down0 = _split_end(xch_a, r_small, "exchange_ffn0_wait")
    r_dn_in, r_dn_out, r_kv1, r_gu1, r_down1 = _split_end(xch_b, r_small, "exchange_layer1_wait")

    def rows(a):
        return a.reshape((-1,) + a.shape[-1:])

    def row_sharded(recv, w, m, v, tb, name):
        outs = _adamw_reduce(recv.reshape((N_DEV, -1) + recv.shape[-1:]), rows(w), rows(m), rows(v), tb, name)
        return [o.reshape(w.shape) for o in outs]

    def col_sharded(recv, w, m, v, tb, name):
        g_t = _reduce8(recv.reshape((N_DEV, -1) + recv.shape[-1:]), tb, name + "_sum")
        g = jnp.swapaxes(g_t.reshape(recv.shape[1:])[:, :w.shape[2]], 1, 2)
        outs = _adamw(rows(g), rows(w), rows(m), rows(v), 256, name)
        return [g] + [o.reshape(w.shape) for o in outs]

    def per_layer(fn, recvs, w, m, v, tb, name):
        outs = [fn(r, w[l:l + 1], m[l:l + 1], v[l:l + 1], tb, f"{name}{l}") for l, r in enumerate(recvs)]
        return [jnp.concatenate(pair, axis=0) for pair in zip(*outs)]

    big = [col_sharded(r_att_in, att_w_in, m_att_w_in, v_att_w_in, 320, "adamw_att_in"),
           row_sharded(r_att_out, att_w_out, m_att_w_out, v_att_w_out, 128, "adamw_att_out"),
           col_sharded(r_dn_in, dn_w_in, m_dn_w_in, v_dn_w_in, 432, "adamw_dn_in"),
           row_sharded(r_dn_out, dn_w_out, m_dn_w_out, v_dn_w_out, 128, "adamw_dn_out"),
           per_layer(row_sharded, [r_kv0, r_kv1], mem_w_kv, m_mem_w_kv, v_mem_w_kv, 128, "adamw_mem_kv"),
           per_layer(col_sharded, [r_gu0, r_gu1], ffn_w_gate_up, m_ffn_w_gate_up, v_ffn_w_gate_up, 176,
                     "adamw_ffn_gu"),
           per_layer(row_sharded, [r_down0, r_down1], ffn_w_down, m_ffn_w_down, v_ffn_w_down, 176,
                     "adamw_ffn_down")]
    g_big, d_big, nm_big, nv_big = [[b[i] for b in big] for i in range(4)]

    g_small = _reduce8(r_small, SMALL_ROWS, "reduce_small")
    rep_shapes = [(32, 12), (1, 2, 6), (1, 2, 6), (1, 128), (2, D), (2, D), (2, D), (2, D), (2, D)]
    *g_rep, g_conv_full = _unpack_small(g_small, rep_shapes + [(DN_CONV, 3 * TOK_W)])
    me = _index(_me_xyc())
    g_conv = lax.dynamic_slice(g_conv_full, (0, me * 288), (DN_CONV, 288)).reshape(CONV_SHARD)
    small_shapes = rep_shapes + [CONV_SHARD]
    small_w = [rel_bias, dn_a_log, dn_dt_bias, dn_out_norm, mem_norm, norm_mix_pre, norm_mix_post,
               norm_ffn_pre, norm_ffn_post, dn_conv]
    small_m = [m_rel_bias, m_dn_a_log, m_dn_dt_bias, m_dn_out_norm, m_mem_norm, m_norm_mix_pre, m_norm_mix_post,
               m_norm_ffn_pre, m_norm_ffn_post, m_dn_conv]
    small_v = [v_rel_bias, v_dn_a_log, v_dn_dt_bias, v_dn_out_norm, v_mem_norm, v_norm_mix_pre, v_norm_mix_post,
               v_norm_ffn_pre, v_norm_ffn_post, v_dn_conv]
    g_small_list = g_rep + [g_conv]
    outs_small = _adamw(_pack_small(g_small_list, 24), _pack_small(small_w, 24), _pack_small(small_m, 24),
                        _pack_small(small_v, 24), 24, "adamw_small")
    d_small, nm_small, nv_small = [_unpack_small(o, small_shapes) for o in outs_small]

    def ordered(small, big):
        return [small[0], big[0], big[1], big[2], small[9], small[1], small[2], small[3], big[3], small[4],
                big[4], small[5], small[6], small[7], small[8], big[5], big[6]]

    g_small_out = [g.reshape(s) for g, s in zip(g_small_list, small_shapes)]
    return (loss, grad_x[None], *ordered(g_small_out, g_big), *ordered(d_small, d_big),
            *ordered(nm_small, nm_big), *ordered(nv_small, nv_big))
```

```python
import functools
import math

import numpy as np
import jax
import jax.numpy as jnp
from jax import lax
from jax.experimental import pallas as pl
from jax.experimental.pallas import tpu as pltpu

F32 = jnp.float32
BF16 = jnp.bfloat16
HI = lax.Precision.HIGHEST
MESH = pl.DeviceIdType.MESH

N_DEV = 8
D = 1024
EPS = 1e-6
NEG = -1e30
TOK_W = 768
MEM_W = 256
ATT_HD = 64
DIL_GROUPS = ((128, 1), (512, 4), (2048, 16))
BAND_HALF = 64
REL_BUCKETS = 32
REL_MAX_DIST = 1024
DN_HD = 128
DN_HEADS = 6
DN_CONV = 5
DN_CHUNK = 64
MEM_HEADS = 4
D_FF = 2816
ATT_IN = 2560
DN_IN = 3352
DN_IN_PAD = 3456

ADAM_LR, ADAM_B1, ADAM_B2, ADAM_EPS, ADAM_WD, ADAM_STEP = 0.001, 0.9, 0.999, 1e-08, 0.01, 10

PACK_C = 512
BIG_ROWS = 6480
SMALL_ROWS = 48
VMEM_LIMIT = 48 * 1024 * 1024


def _cparams(sem=None):
    kw = dict(vmem_limit_bytes=VMEM_LIMIT)
    if sem is not None:
        kw["dimension_semantics"] = sem
    return pltpu.CompilerParams(**kw)


def _tile(n, cap):
    if n <= cap:
        return n
    best = None
    for t in range(128, cap + 1, 128):
        if n % t == 0:
            best = t
    assert best is not None, (n, cap)
    return best


def _matmul(a, b, mode, out_dtype, name, tm=1024, tn=1408, tk=None):
    if tk is None:
        tk = 4096 if mode == "tn" else 2816
    if mode == "tn":
        tm = min(tm, 512)
    if mode == "nn":
        (m, kc), (_, n) = a.shape, b.shape
        dims = (((1,), (0,)), ((), ()))
    elif mode == "nt":
        (m, kc), (n, _) = a.shape, b.shape
        dims = (((1,), (1,)), ((), ()))
    else:
        (kc, m), (_, n) = a.shape, b.shape
        dims = (((0,), (0,)), ((), ()))
    tm = m if m <= tm else _tile(m, tm)
    tn = _tile(n, tn)
    tk = _tile(kc, tk)
    nk = kc // tk

    def body(a_ref, b_ref, o_ref, acc_ref):
        k = pl.program_id(2)
        part = lax.dot_general(a_ref[...], b_ref[...], dims, preferred_element_type=F32)

        @pl.when(k == 0)
        def _():
            acc_ref[...] = part

        @pl.when(k > 0)
        def _():
            acc_ref[...] += part

        @pl.when(k == nk - 1)
        def _():
            o_ref[...] = acc_ref[...].astype(o_ref.dtype)

    if mode == "nn":
        a_spec = pl.BlockSpec((tm, tk), lambda i, j, k: (i, k))
        b_spec = pl.BlockSpec((tk, tn), lambda i, j, k: (k, j))
    elif mode == "nt":
        a_spec = pl.BlockSpec((tm, tk), lambda i, j, k: (i, k))
        b_spec = pl.BlockSpec((tn, tk), lambda i, j, k: (j, k))
    else:
        a_spec = pl.BlockSpec((tk, tm), lambda i, j, k: (k, i))
        b_spec = pl.BlockSpec((tk, tn), lambda i, j, k: (k, j))
    return pl.pallas_call(
        body, name=name, grid=(m // tm, n // tn, nk),
        in_specs=[a_spec, b_spec],
        out_specs=pl.BlockSpec((tm, tn), lambda i, j, k: (i, j)),
        out_shape=jax.ShapeDtypeStruct((m, n), out_dtype),
        scratch_shapes=[pltpu.VMEM((tm, tn), F32)],
        compiler_params=_cparams(("parallel", "parallel", "arbitrary")),
    )(a, b)


def _rowwise(fn, rows, params, outs, tb, name):
    t = rows[0].shape[0]
    nr, npar = len(rows), len(params)

    def body(*refs):
        ins = [r[...].astype(F32) for r in refs[:nr + npar]]
        res = fn(*ins)
        for o_ref, r in zip(refs[nr + npar:], res):
            o_ref[...] = r.astype(o_ref.dtype)

    return pl.pallas_call(
        body, name=name, grid=(t // tb,),
        in_specs=[pl.BlockSpec((tb, r.shape[1]), lambda i: (i, 0)) for r in rows]
        + [pl.BlockSpec(p.shape, lambda i: (0, 0)) for p in params],
        out_specs=[pl.BlockSpec((tb, c), lambda i: (i, 0)) for c, _ in outs],
        out_shape=[jax.ShapeDtypeStruct((t, c), dt) for c, dt in outs],
        compiler_params=_cparams(("parallel",)),
    )(*rows, *params)


def _rowwise_bwd(fn, rows, params, cots, row_grad, tb, name):
    t = rows[0].shape[0]
    nr, npar, nc = len(rows), len(params), len(cots)
    want = [i for i, g in enumerate(row_grad) if g is not None]

    def body(*refs):
        ins = [r[...].astype(F32) for r in refs[:nr + npar]]
        cts = tuple(r[...].astype(F32) for r in refs[nr + npar:nr + npar + nc])
        outs = refs[nr + npar + nc:]
        _, vjp = jax.vjp(fn, *ins)
        grads = vjp(cts)
        for o_ref, i in zip(outs[:len(want)], want):
            o_ref[...] = grads[i].astype(o_ref.dtype)
        first = pl.program_id(0) == 0
        for o_ref, g in zip(outs[len(want):], grads[nr:]):
            @pl.when(first)
            def _(o_ref=o_ref, g=g):
                o_ref[...] = g

            @pl.when(jnp.logical_not(first))
            def _(o_ref=o_ref, g=g):
                o_ref[...] += g

    res = pl.pallas_call(
        body, name=name, grid=(t // tb,),
        in_specs=[pl.BlockSpec((tb, r.shape[1]), lambda i: (i, 0)) for r in rows]
        + [pl.BlockSpec(p.shape, lambda i: (0, 0)) for p in params]
        + [pl.BlockSpec((tb, c.shape[1]), lambda i: (i, 0)) for c in cots],
        out_specs=[pl.BlockSpec((tb, rows[i].shape[1]), lambda i_: (i_, 0)) for i in want]
        + [pl.BlockSpec(p.shape, lambda i: (0, 0)) for p in params],
        out_shape=[jax.ShapeDtypeStruct(rows[i].shape, row_grad[i]) for i in want]
        + [jax.ShapeDtypeStruct(p.shape, F32) for p in params],
        compiler_params=_cparams(("arbitrary",)),
    )(*rows, *params, *cots)
    return list(res[:len(want)]), list(res[len(want):])


def _rms(x, g):
    return x * lax.rsqrt(jnp.mean(x * x, axis=-1, keepdims=True) + EPS) * g


def _fn_pre(x, g):
    return (_rms(x, g),)


def _fn_res_pre(x, y, g_post, g_pre):
    x1 = x + _rms(y, g_post)
    return x1, _rms(x1, g_pre)


def _fn_res(x, y, g_post):
    return (x + _rms(y, g_post),)


def _sigmoid(x):
    return 1.0 / (1.0 + jnp.exp(-x))


def _silu(x):
    return x * _sigmoid(x)


def _fn_swiglu(gu):
    return (_silu(gu[:, :D_FF]) * gu[:, D_FF:],)


def _fn_combine(o, lse):
    ls = [lse[:, 256 * g:256 * (g + 1)] for g in range(3)]
    mx = lax.stop_gradient(jnp.maximum(jnp.maximum(ls[0], ls[1]), ls[2]))
    es = [jnp.exp(l - mx) for l in ls]
    inv = 1.0 / (es[0] + es[1] + es[2])
    return (jnp.concatenate([o[:, 256 * g:256 * (g + 1)] * (es[g] * inv) for g in range(3)], axis=1),)


def _fn_outnorm(o_f, o_r, z, gain):
    res = []
    for h in range(DN_HEADS):
        sl = slice(DN_HD * h, DN_HD * (h + 1))
        o = o_f[:, sl] + o_r[:, sl]
        res.append(o * lax.rsqrt(jnp.mean(o * o, axis=-1, keepdims=True) + EPS) * gain * _silu(z[:, sl]))
    return (jnp.concatenate(res, axis=1),)


def _loss_kernel(x, tgt, tb, name):
    t, d = x.shape

    def body(x_ref, t_ref, dx_ref, l_ref, acc_ref):
        i = pl.program_id(0)
        e = x_ref[...] - t_ref[...]
        dx_ref[...] = e * (1.0 / d)
        part = jnp.sum(e * e, axis=0, keepdims=True)

        @pl.when(i == 0)
        def _():
            acc_ref[...] = part

        @pl.when(i > 0)
        def _():
            acc_ref[...] += part

        @pl.when(i == t // tb - 1)
        def _():
            l_ref[...] = jnp.broadcast_to(jnp.sum(acc_ref[...], axis=-1, keepdims=True), (1, 128))

    return pl.pallas_call(
        body, name=name, grid=(t // tb,),
        in_specs=[pl.BlockSpec((tb, d), lambda i: (i, 0))] * 2,
        out_specs=[pl.BlockSpec((tb, d), lambda i: (i, 0)), pl.BlockSpec((1, 128), lambda i: (0, 0))],
        out_shape=[jax.ShapeDtypeStruct((t, d), F32), jax.ShapeDtypeStruct((1, 128), F32)],
        scratch_shapes=[pltpu.VMEM((1, d), F32)],
        compiler_params=_cparams(("arbitrary",)),
    )(x, tgt)


def _band_fn(l_sub, bq, i, q, kw, vw, bm):
    w = bq + 2 * BAND_HALF
    s = lax.dot_general((q * (ATT_HD ** -0.5)).astype(BF16), kw.astype(BF16), (((1,), (1,)), ((), ())),
                        preferred_element_type=F32) + bm
    kpos = i * bq - BAND_HALF + lax.broadcasted_iota(jnp.int32, (bq, w), 1)
    s = jnp.where((kpos >= 0) & (kpos < l_sub), s, NEG)
    m = lax.stop_gradient(jnp.max(s, axis=-1, keepdims=True))
    p = jnp.exp(s - m)
    den = jnp.sum(p, axis=-1, keepdims=True)
    o = jnp.dot(p.astype(BF16), vw.astype(BF16), preferred_element_type=F32) / den
    return o, jnp.broadcast_to(m + jnp.log(den), o.shape)


def _band_specs(dil, l_sub, bq):
    w = bq + 2 * BAND_HALF
    qs = pl.BlockSpec((None, bq, ATT_HD), lambda h, r, i: (h * dil + r, i, 0))
    ks = pl.BlockSpec((None, l_sub + 2 * BAND_HALF, ATT_HD), lambda h, r, i: (h * dil + r, 0, 0))
    bs = pl.BlockSpec((None, bq, w), lambda h, r, i: (h, 0, 0))
    return qs, ks, bs


def _band_fwd(q, k, v, bm, dil, l_sub, bq, name):
    w = bq + 2 * BAND_HALF
    qs, ks, bs = _band_specs(dil, l_sub, bq)

    def body(q_ref, k_ref, v_ref, bm_ref, o_ref, l_ref):
        i = pl.program_id(2)
        st = pl.multiple_of(i * bq, bq)
        o, lse = _band_fn(l_sub, bq, i, q_ref[...].astype(F32), k_ref[pl.ds(st, w), :].astype(F32),
                          v_ref[pl.ds(st, w), :].astype(F32), bm_ref[...])
        o_ref[...] = o
        l_ref[...] = lse

    return pl.pallas_call(
        body, name=name, grid=(4, dil, l_sub // bq),
        in_specs=[qs, ks, ks, bs], out_specs=[qs, qs],
        out_shape=[jax.ShapeDtypeStruct(q.shape, F32)] * 2,
        compiler_params=_cparams(("parallel", "parallel", "arbitrary")),
    )(q, k, v, bm)


def _band_bwd(q, k, v, bm, do, dlse, dil, l_sub, bq, name):
    w = bq + 2 * BAND_HALF
    qs, ks, bs = _band_specs(dil, l_sub, bq)

    def body(q_ref, k_ref, v_ref, bm_ref, do_ref, dl_ref, dq_ref, dk_ref, dv_ref, dbm_ref):
        r, i = pl.program_id(1), pl.program_id(2)
        st = pl.multiple_of(i * bq, bq)
        _, vjp = jax.vjp(functools.partial(_band_fn, l_sub, bq, i),
                         q_ref[...].astype(F32), k_ref[pl.ds(st, w), :].astype(F32),
                         v_ref[pl.ds(st, w), :].astype(F32), bm_ref[...])
        dq, dkw, dvw, dbm = vjp((do_ref[...].astype(F32), dl_ref[...]))
        dq_ref[...] = dq.astype(dq_ref.dtype)

        @pl.when(i == 0)
        def _():
            dk_ref[...] = jnp.zeros_like(dk_ref)
            dv_ref[...] = jnp.zeros_like(dv_ref)

        dk_ref[pl.ds(st, w), :] += dkw
        dv_ref[pl.ds(st, w), :] += dvw

        @pl.when((i == 0) & (r == 0))
        def _():
            dbm_ref[...] = dbm

        @pl.when((i > 0) | (r > 0))
        def _():
            dbm_ref[...] += dbm

    return pl.pallas_call(
        body, name=name, grid=(4, dil, l_sub // bq),
        in_specs=[qs, ks, ks, bs, qs, qs], out_specs=[qs, ks, ks, bs],
        out_shape=[jax.ShapeDtypeStruct(q.shape, BF16), jax.ShapeDtypeStruct(k.shape, F32),
                   jax.ShapeDtypeStruct(k.shape, F32), jax.ShapeDtypeStruct(bm.shape, F32)],
        compiler_params=_cparams(("parallel", "arbitrary", "arbitrary")),
    )(q, k, v, bm, do, dlse)


def _t5_bucket(rel):
    half = REL_BUCKETS // 2
    max_exact = half // 2
    n = np.abs(rel)
    large = max_exact + (np.log(np.maximum(n, 1) / max_exact) / math.log(REL_MAX_DIST / max_exact)
                         * (half - max_exact)).astype(np.int64)
    large = np.minimum(large, half - 1)
    return ((rel > 0) * half + np.where(n < max_exact, n, large)).astype(np.int32)


def _bucket_onehot(dil):
    idx = _t5_bucket(np.arange(-BAND_HALF, BAND_HALF + 1) * dil)
    oh = np.zeros((2 * BAND_HALF + 1, REL_BUCKETS), np.float32)
    oh[np.arange(2 * BAND_HALF + 1), idx] = 1.0
    return oh


def _band_bias(rel_bias, gi, dil, bq):
    w = bq + 2 * BAND_HALF
    nb = 2 * BAND_HALF + 1
    bias = jnp.dot(jnp.asarray(_bucket_onehot(dil)), rel_bias[:, 4 * gi:4 * gi + 4], precision=HI)
    row = jnp.concatenate([bias.T, jnp.full((4, w + 1 - nb), NEG, F32)], axis=1)
    flat = jnp.tile(row, (1, bq))[:, :bq * w]
    return flat.reshape(4, bq, w)


def _relbias_grad(dbms, name):
    nb = 2 * BAND_HALF + 1
    bq = max(d.shape[1] for d in dbms)
    skew = []
    for dbm in dbms:
        bqg, w = dbm.shape[1], dbm.shape[2]
        flat = jnp.pad(dbm.reshape(4, bqg * w), ((0, 0), (0, bqg)))
        skew.append(jnp.pad(flat.reshape(4, bqg, w + 1)[:, :, :nb], ((0, 0), (0, bq - bqg), (0, 256 - nb))))
    sk = jnp.concatenate(skew, axis=0)
    oh = np.zeros((3, 256, 128), np.float32)
    for gi, (_, dil) in enumerate(DIL_GROUPS):
        oh[gi, :2 * BAND_HALF + 1, :REL_BUCKETS] = _bucket_onehot(dil)

    def body(s_ref, oh_ref, o_ref):
        col = jnp.sum(s_ref[...], axis=0, keepdims=True)
        o_ref[...] = jnp.dot(jnp.broadcast_to(col, (8, 256)), oh_ref[...], precision=HI, preferred_element_type=F32)

    out = pl.pallas_call(
        body, name=name, grid=(12,),
        in_specs=[pl.BlockSpec((None, bq, 256), lambda n: (n, 0, 0)),
                  pl.BlockSpec((None, 256, 128), lambda n: (n // 4, 0, 0))],
        out_specs=pl.BlockSpec((None, 8, 128), lambda n: (n, 0, 0)),
        out_shape=jax.ShapeDtypeStruct((12, 8, 128), F32),
        compiler_params=_cparams(("parallel",)),
    )(sk, jnp.asarray(oh))
    return out[:, 0, :REL_BUCKETS].T


def _mem_fn(q, k, v):
    s = lax.dot_general((q * (ATT_HD ** -0.5)).astype(BF16), k.astype(BF16), (((1,), (1,)), ((), ())),
                        preferred_element_type=F32)
    m = lax.stop_gradient(jnp.max(s, axis=-1, keepdims=True))
    p = jnp.exp(s - m)
    p = p / jnp.sum(p, axis=-1, keepdims=True)
    return jnp.dot(p.astype(BF16), v.astype(BF16), preferred_element_type=F32)


def _mem_specs(tb, ml):
    qs = pl.BlockSpec((None, tb, ATT_HD), lambda h, i: (h, i, 0))
    ks = pl.BlockSpec((None, ml, ATT_HD), lambda h, i: (h, 0, 0))
    return qs, ks


def _mem_fwd(q, k, v, tb, name):
    qs, ks = _mem_specs(tb, k.shape[1])

    def body(q_ref, k_ref, v_ref, o_ref):
        o_ref[...] = _mem_fn(q_ref[...].astype(F32), k_ref[...], v_ref[...])

    return pl.pallas_call(
        body, name=name, grid=(MEM_HEADS, q.shape[1] // tb),
        in_specs=[qs, ks, ks], out_specs=qs, out_shape=jax.ShapeDtypeStruct(q.shape, F32),
        compiler_params=_cparams(("parallel", "parallel")),
    )(q, k, v)


def _mem_bwd(q, k, v, do, tb, name):
    qs, ks = _mem_specs(tb, k.shape[1])

    def body(q_ref, k_ref, v_ref, do_ref, dq_ref, dk_ref, dv_ref):
        i = pl.program_id(1)
        _, vjp = jax.vjp(_mem_fn, q_ref[...].astype(F32), k_ref[...], v_ref[...])
        dq, dk, dv = vjp(do_ref[...])
        dq_ref[...] = dq

        @pl.when(i == 0)
        def _():
            dk_ref[...] = dk
            dv_ref[...] = dv

        @pl.when(i > 0)
        def _():
            dk_ref[...] += dk
            dv_ref[...] += dv

    return pl.pallas_call(
        body, name=name, grid=(MEM_HEADS, q.shape[1] // tb),
        in_specs=[qs, ks, ks, qs], out_specs=[qs, ks, ks],
        out_shape=[jax.ShapeDtypeStruct(q.shape, F32), jax.ShapeDtypeStruct(k.shape, F32),
                   jax.ShapeDtypeStruct(k.shape, F32)],
        compiler_params=_cparams(("parallel", "arbitrary")),
    )(q, k, v, do)


CONV_PAD = 8


def _conv_post(kind, acc):
    s = _silu(acc)
    if kind == 2:
        return s
    scale = DN_HD ** -0.5 if kind == 0 else 1.0
    return s * lax.rsqrt(jnp.sum(s * s, axis=-1, keepdims=True) + EPS) * scale


def _conv_rows(x_ref, t, start, rt):
    lo = min(max(start, 0), t - rt)
    x = x_ref[pl.ds(lo, rt), :]
    shift = lo - start
    if shift == 0:
        return x
    x = pltpu.roll(x, shift % rt, axis=0)
    row = lax.broadcasted_iota(jnp.int32, x.shape, 0)
    return jnp.where((row >= shift) if shift > 0 else (row < rt + shift), x, 0.0)


def _conv_acc(x_ref, t, w, r0, rt):
    acc = None
    for i in range(DN_CONV):
        term = w[i:i + 1, :] * _conv_rows(x_ref, t, r0 + i - DN_CONV // 2, rt)
        acc = term if acc is None else acc + term
    return acc


def _conv_fwd(x, w8, kind, rt, name):
    t = x.shape[0]

    def body(x_ref, w_ref, o_ref):
        w = w_ref[...]
        for r in range(t // rt):
            o_ref[pl.ds(r * rt, rt), :] = _conv_post(kind, _conv_acc(x_ref, t, w, r * rt, rt))

    return pl.pallas_call(
        body, name=name, grid=(DN_HEADS,),
        in_specs=[pl.BlockSpec((t, DN_HD), lambda j: (0, 6 * kind + j)),
                  pl.BlockSpec((8, DN_HD), lambda j: (0, 6 * kind + j))],
        out_specs=pl.BlockSpec((t, DN_HD), lambda j: (0, j)),
        out_shape=jax.ShapeDtypeStruct((t, TOK_W), F32),
        compiler_params=_cparams(("parallel",)),
    )(x, w8)


def _conv_bwd(x, w8, d_f, d_r, kind, rt, name):
    t = x.shape[0]

    def body(xp_ref, w_ref, df_ref, dr_ref, dx_ref, dw_ref, dpad_ref):
        w = w_ref[...]
        zero = jnp.zeros((CONV_PAD, DN_HD), F32)
        dpad_ref[pl.ds(0, CONV_PAD), :] = zero
        dpad_ref[pl.ds(CONV_PAD + t, CONV_PAD), :] = zero
        dw = [jnp.zeros((1, DN_HD), F32) for _ in range(DN_CONV)]
        for r in range(t // rt):
            rows = pl.ds(r * rt, rt)
            acc = _conv_acc(xp_ref, t, w, r * rt, rt)
            _, vjp = jax.vjp(functools.partial(_conv_post, kind), acc)
            (dacc,) = vjp(df_ref[rows, :] + dr_ref[rows, :])
            dpad_ref[pl.ds(CONV_PAD + r * rt, rt), :] = dacc
            for i in range(DN_CONV):
                xs = _conv_rows(xp_ref, t, r * rt + i - DN_CONV // 2, rt)
                dw[i] = dw[i] + jnp.sum(dacc * xs, axis=0, keepdims=True)
        dw_ref[...] = jnp.concatenate(dw + [jnp.zeros((8 - DN_CONV, DN_HD), F32)], axis=0)
        for r in range(t // rt):
            acc = None
            for i in range(DN_CONV):
                term = w[i:i + 1, :] * dpad_ref[pl.ds(CONV_PAD + r * rt - i + DN_CONV // 2, rt), :]
                acc = term if acc is None else acc + term
            dx_ref[pl.ds(r * rt, rt), :] = acc.astype(dx_ref.dtype)

    return pl.pallas_call(
        body, name=name, grid=(DN_HEADS,),
        in_specs=[pl.BlockSpec((t, DN_HD), lambda j: (0, 6 * kind + j)),
                  pl.BlockSpec((8, DN_HD), lambda j: (0, 6 * kind + j)),
                  pl.BlockSpec((t, DN_HD), lambda j: (0, j)),
                  pl.BlockSpec((t, DN_HD), lambda j: (0, j))],
        out_specs=[pl.BlockSpec((t, DN_HD), lambda j: (0, j)), pl.BlockSpec((8, DN_HD), lambda j: (0, j))],
        out_shape=[jax.ShapeDtypeStruct((t, TOK_W), BF16), jax.ShapeDtypeStruct((8, TOK_W), F32)],
        scratch_shapes=[pltpu.VMEM((t + 2 * CONV_PAD, DN_HD), F32)],
        compiler_params=_cparams(("parallel",)),
    )(x, w8, d_f, d_r)


def _softplus(x):
    e = jnp.exp(-jnp.abs(x))
    return jnp.maximum(x, 0.0) + jnp.where(e < 1e-4, e - 0.5 * e * e, jnp.log(1.0 + e))


_NN = (((2,), (1,)), ((0,), (0,)))
_NT = (((2,), (2,)), ((0,), (0,)))
_TN = (((1,), (1,)), ((0,), (0,)))


def _dot(a, b, dims=_NN):
    return lax.dot_general(a.astype(BF16), b.astype(BF16), dims, preferred_element_type=F32)


def _hi_lo(x):
    hi = x.astype(BF16)
    return hi, (x - hi.astype(F32)).astype(BF16)


def _mask_dot(mask_bf16, x, dims):
    x1 = x.astype(BF16)
    r = x - x1.astype(F32)
    x2, x3 = _hi_lo(r)
    d = functools.partial(lax.dot_general, dimension_numbers=dims, preferred_element_type=F32)
    return d(mask_bf16, x1) + d(mask_bf16, x2) + d(mask_bf16, x3)


@jax.custom_vjp
def _dot_mask(mask_bf16, x):
    return _mask_dot(mask_bf16, x, _NN)


def _dot_mask_fwd(mask_bf16, x):
    return _mask_dot(mask_bf16, x, _NN), mask_bf16


def _dot_mask_bwd(mask_bf16, ct):
    return jnp.zeros_like(mask_bf16), _mask_dot(mask_bf16, ct, _TN)


_dot_mask.defvjp(_dot_mask_fwd, _dot_mask_bwd)


def _dot3_raw(a, b, dims):
    a1, a2 = _hi_lo(a)
    b1, b2 = _hi_lo(b)
    d = functools.partial(lax.dot_general, dimension_numbers=dims, preferred_element_type=F32)
    return d(a1, b1) + d(a1, b2) + d(a2, b1)


def _unit_solve_pass(lmat, rhs, masks):
    ainv = masks[6] - lmat * masks[0]
    for sh in range(1, 6):
        ainv = ainv - _dot(_dot(ainv, lmat * masks[sh]), ainv)
    return _dot3_raw(ainv, rhs, _NN), ainv


@jax.custom_vjp
def _unit_solve(lmat, rhs, masks):
    return _unit_solve_pass(lmat, rhs, masks)[0]


def _unit_solve_fwd(lmat, rhs, masks):
    sol, ainv = _unit_solve_pass(lmat, rhs, masks)
    return sol, (sol, ainv, masks)


def _unit_solve_bwd(res, ct):
    sol, ainv, masks = res
    d_rhs = _dot3_raw(ainv, ct, _TN)
    return -_dot3_raw(d_rhs, sol, _NT), d_rhs, tuple(jnp.zeros_like(m) for m in masks)


_unit_solve.defvjp(_unit_solve_fwd, _unit_solve_bwd)


def _block_masks(rev, row, col):
    c = DN_CHUNK
    prow = jnp.where(rev, c - 1 - row, row)
    pcol = jnp.where(rev, c - 1 - col, col)
    masks = []
    for sh in range(6):
        differ = (prow ^ pcol) >> sh
        miss = (differ ^ 1) + (1 - ((prow >> sh) & 1))
        masks.append(jnp.where(miss == 0, 1.0, 0.0))
    masks.append(jnp.where(row == col, 1.0, 0.0))
    return tuple(masks)


def _dn_chunk(q, k, v, al, be, alc, a_row, dt_row, a_rowc, dt_rowc, s):
    n, c = q.shape[0], DN_CHUNK
    rev = lax.broadcasted_iota(jnp.int32, (n, c, c), 0) >= n // 2
    row = lax.broadcasted_iota(jnp.int32, (n, c, c), 1)
    col = lax.broadcasted_iota(jnp.int32, (n, c, c), 2)
    ahead = jnp.where(rev, col - row, row - col)
    incl = ahead >= 0
    strict = ahead > 0
    incl_b = incl.astype(BF16)

    g = -jnp.exp(a_row) * _softplus(al + dt_row)
    beta = _sigmoid(be)
    g_c = -jnp.exp(a_rowc) * _softplus(alc + dt_rowc)
    gc = _dot_mask(incl_b, g)
    gcc = _dot_mask(incl_b, g_c)
    decay = jnp.exp(jnp.where(incl, gcc - jnp.swapaxes(gcc, 1, 2), NEG))
    kb = k * beta
    lmat = jnp.where(strict, _dot(kb, k, _NT) * decay, 0.0)
    rhs = jnp.concatenate([v * beta, kb * jnp.exp(gc)], axis=2)
    sol = _unit_solve(lmat, rhs, _block_masks(rev, row, col))
    u, w = sol[:, :, :DN_HD], sol[:, :, DN_HD:]
    intra = jnp.where(incl, _dot(q, k, _NT) * decay, 0.0)
    v_new = u - _dot(w, s)
    out = _dot(q * jnp.exp(gc), s) + _dot(intra, v_new)
    g_last = jnp.sum(g, axis=1, keepdims=True)
    s_new = s * jnp.exp(g_last) + _dot(k * jnp.exp(g_last - gc), v_new, _TN)
    return out, s_new


DN_HG = 6


def _dn_load(f_refs, r_refs, alf, bef, alr, ber, a_ref, dt_ref):
    c, hg = DN_CHUNK, DN_HG
    sls = [slice(DN_HD * h, DN_HD * (h + 1)) for h in range(hg)]
    toks = [jnp.stack([f[:, sl] for sl in sls] + [r[:, sl] for sl in sls]) for f, r in zip(f_refs, r_refs)]
    al = jnp.concatenate([alf[...], alr[...]], axis=0)
    be = jnp.concatenate([bef[...], ber[...]], axis=0)
    alc = jnp.concatenate([alf[:, :, 0:c], alr[:, :, 0:c]], axis=0)
    a = jnp.concatenate([a_ref[0], a_ref[1]], axis=0)
    dt = jnp.concatenate([dt_ref[0], dt_ref[1]], axis=0)
    ac = jnp.concatenate([a_ref[0, :, :, 0:c], a_ref[1, :, :, 0:c]], axis=0)
    dtc = jnp.concatenate([dt_ref[0, :, :, 0:c], dt_ref[1, :, :, 0:c]], axis=0)
    return toks, (al, be, alc, a, dt, ac, dtc)


def _dn_views(nc, bwd):
    c, hg = DN_CHUNK, DN_HG
    if bwd:
        f_blk = lambda s: nc - 1 - s
        r_blk = lambda s: s
        st_blk = lambda s: nc - 1 - s
    else:
        f_blk = lambda s: s
        r_blk = lambda s: nc - 1 - s
        st_blk = lambda s: s
    tok_f = pl.BlockSpec((c, hg * DN_HD), lambda g, s: (f_blk(s), g))
    tok_r = pl.BlockSpec((c, hg * DN_HD), lambda g, s: (r_blk(s), g))
    gate_f = pl.BlockSpec((None, hg, c, DN_HD), lambda g, s: (0, g, f_blk(s), 0))
    gate_r = pl.BlockSpec((None, hg, c, DN_HD), lambda g, s: (1, g, r_blk(s), 0))
    par = pl.BlockSpec((2, hg, 1, DN_HD), lambda g, s: (0, g, 0, 0))
    state = pl.BlockSpec((2, hg, None, DN_HD, DN_HD), lambda g, s: (0, g, st_blk(s), 0, 0))
    return tok_f, tok_r, gate_f, gate_r, par, state


def _dn_fwd(q, k, v, al, be, a_rows, dt_rows, name):
    t = q.shape[0]
    c, hg = DN_CHUNK, DN_HG
    nc = t // c
    tok_f, tok_r, gate_f, gate_r, par, state = _dn_views(nc, False)

    def body(qf, kf, vf, qr, kr, vr, alf, bef, alr, ber, a_ref, dt_ref, of_ref, or_ref, st_ref, s_ref):
        @pl.when(pl.program_id(1) == 0)
        def _():
            s_ref[...] = jnp.zeros_like(s_ref)

        (q_, k_, v_), gates = _dn_load((qf, kf, vf), (qr, kr, vr), alf, bef, alr, ber, a_ref, dt_ref)
        s = s_ref[...]
        st_ref[0] = s[:hg]
        st_ref[1] = s[hg:]
        out, s_new = _dn_chunk(q_, k_, v_, *gates, s)
        for h in range(hg):
            sl = slice(DN_HD * h, DN_HD * (h + 1))
            of_ref[:, sl] = out[h]
            or_ref[:, sl] = out[hg + h]
        s_ref[...] = s_new

    return pl.pallas_call(
        body, name=name, grid=(DN_HEADS // hg, nc),
        in_specs=[tok_f] * 3 + [tok_r] * 3 + [gate_f, gate_f, gate_r, gate_r, par, par],
        out_specs=[tok_f, tok_r, state],
        out_shape=[jax.ShapeDtypeStruct((t, TOK_W), F32)] * 2
        + [jax.ShapeDtypeStruct((2, DN_HEADS, nc, DN_HD, DN_HD), F32)],
        scratch_shapes=[pltpu.VMEM((2 * hg, DN_HD, DN_HD), F32)],
        compiler_params=_cparams(("parallel", "arbitrary")),
    )(q, k, v, q, k, v, al, be, al, be, a_rows, dt_rows)


def _dn_bwd(q, k, v, al, be, a_rows, dt_rows, states, do, name):
    t = q.shape[0]
    c, hg = DN_CHUNK, DN_HG
    assert hg == DN_HEADS
    nc = t // c
    tok_f, tok_r, gate_f, gate_r, par, state = _dn_views(nc, True)
    gout_f = pl.BlockSpec((c, DN_HD), lambda g, s: (nc - 1 - s, 0))
    gout_r = pl.BlockSpec((c, DN_HD), lambda g, s: (s, 0))

    def body(qf, kf, vf, qr, kr, vr, alf, bef, alr, ber, a_ref, dt_ref, st_ref, dof, dor,
             dqf, dkf, dvf, dqr, dkr, dvr, dgf, dgr, da_ref, ddt_ref, ds_ref):
        first = pl.program_id(1) == 0

        @pl.when(first)
        def _():
            ds_ref[...] = jnp.zeros_like(ds_ref)
            da_ref[...] = jnp.zeros_like(da_ref)
            ddt_ref[...] = jnp.zeros_like(ddt_ref)

        def lanes(x):
            return jnp.sum(x, axis=-1, keepdims=True)

        (q_, k_, v_, do_), gates = _dn_load((qf, kf, vf, dof), (qr, kr, vr, dor), alf, bef, alr, ber, a_ref, dt_ref)
        s = jnp.concatenate([st_ref[0], st_ref[1]], axis=0)
        _, vjp = jax.vjp(_dn_chunk, q_, k_, v_, *gates, s)
        dq, dk, dv, dal, dbe, dalc, da, ddt, dac, ddtc, ds = vjp((do_, ds_ref[...]))
        for h in range(hg):
            sl = slice(DN_HD * h, DN_HD * (h + 1))
            dqf[:, sl], dkf[:, sl], dvf[:, sl] = dq[h], dk[h], dv[h]
            dqr[:, sl], dkr[:, sl], dvr[:, sl] = dq[hg + h], dk[hg + h], dv[hg + h]
        dal, dbe = lanes(dal) + lanes(dalc), lanes(dbe)
        lane = lax.broadcasted_iota(jnp.int32, (c, DN_HD), 1)
        for d, dg_ref in enumerate((dgf, dgr)):
            dg = jnp.zeros((c, DN_HD), F32)
            for h in range(hg):
                dg = jnp.where(lane == h, dal[d * hg + h], jnp.where(lane == hg + h, dbe[d * hg + h], dg))
            dg_ref[...] = dg
        da = jnp.broadcast_to(lanes(da) + lanes(dac), da.shape)
        ddt = jnp.broadcast_to(lanes(ddt) + lanes(ddtc), ddt.shape)
        da_ref[0] += da[:hg]
        da_ref[1] += da[hg:]
        ddt_ref[0] += ddt[:hg]
        ddt_ref[1] += ddt[hg:]
        ds_ref[...] = ds

    tok = jax.ShapeDtypeStruct((t, TOK_W), F32)
    gate = jax.ShapeDtypeStruct((t, DN_HD), F32)
    parsh = jax.ShapeDtypeStruct((2, DN_HEADS, 1, DN_HD), F32)
    res = pl.pallas_call(
        body, name=name, grid=(DN_HEADS // hg, nc),
        in_specs=[tok_f] * 3 + [tok_r] * 3 + [gate_f, gate_f, gate_r, gate_r, par, par, state, tok_f, tok_r],
        out_specs=[tok_f] * 3 + [tok_r] * 3 + [gout_f, gout_r, par, par],
        out_shape=[tok] * 6 + [gate] * 2 + [parsh] * 2,
        scratch_shapes=[pltpu.VMEM((2 * hg, DN_HD, DN_HD), F32)],
        compiler_params=_cparams(("parallel", "arbitrary")),
    )(q, k, v, q, k, v, al, be, al, be, a_rows, dt_rows, states, do, do)
    dqf, dkf, dvf, dqr, dkr, dvr, dgf, dgr, da, ddt = res
    dgate = jnp.concatenate([dgf[:, :2 * DN_HEADS], dgr[:, :2 * DN_HEADS]], axis=1)
    return (dqf, dkf, dvf), (dqr, dkr, dvr), dgate, da, ddt


BAND_BQ = 256
ROW_TB = 256
MEM_TB = 512
CONV_RT = 512


def _to_sub(x, dil):
    l = x.shape[0] // dil
    return x.reshape(l, dil, 4, ATT_HD).transpose(2, 1, 0, 3).reshape(4 * dil, l, ATT_HD)


def _from_sub(x, dil):
    l = x.shape[1]
    return x.reshape(4, dil, l, ATT_HD).transpose(2, 1, 0, 3).reshape(l * dil, 4 * ATT_HD)


def _heads_major(x):
    return x.reshape(x.shape[0], MEM_HEADS, ATT_HD).transpose(1, 0, 2)


def _heads_minor(x):
    return x.transpose(1, 0, 2).reshape(x.shape[1], MEM_HEADS * ATT_HD)


def _mem_kv_fwd(mem, gain, w_kv, li):
    (memn,) = _rowwise(_fn_pre, [mem], [gain], [(D, BF16)], mem.shape[0], f"memnorm_fwd{li}")
    kv = _matmul(memn, w_kv, "nn", F32, f"memkv_fwd{li}")
    return _heads_major(kv[:, :MEM_W]), _heads_major(kv[:, MEM_W:]), memn


def _mem_kv_bwd(mem, gain, w_kv, memn, dkm, dvm, li):
    dkv = jnp.concatenate([_heads_minor(dkm), _heads_minor(dvm)], axis=1).astype(BF16)
    dw = _matmul(memn, dkv, "tn", BF16, f"memkv_dw{li}")
    dmemn = _matmul(dkv, w_kv, "nt", F32, f"memkv_dx{li}")
    _, (dgain,) = _rowwise_bwd(_fn_pre, [mem], [gain], [dmemn], [None], mem.shape[0], f"memnorm_bwd{li}")
    return dw, dgain


def _attn_mixer_fwd(p, rel_bias, kv_fn):
    t = p.shape[0]
    saved, outs, lses = [], [], []
    for gi, (_, dil) in enumerate(DIL_GROUPS):
        l_sub = t // dil
        bq = min(BAND_BQ, l_sub)
        q = _to_sub(p[:, 256 * gi:256 * (gi + 1)], dil)
        pad = ((0, 0), (BAND_HALF, BAND_HALF), (0, 0))
        k = jnp.pad(_to_sub(p[:, TOK_W + 256 * gi:TOK_W + 256 * (gi + 1)], dil), pad)
        v = jnp.pad(_to_sub(p[:, 2 * TOK_W + 256 * gi:2 * TOK_W + 256 * (gi + 1)], dil), pad)
        bm = _band_bias(rel_bias, gi, dil, bq)
        o, lse = _band_fwd(q, k, v, bm, dil, l_sub, bq, f"band_fwd{gi}")
        outs.append(_from_sub(o, dil))
        lses.append(_from_sub(lse, dil))
        saved.append((q, k, v, bm))
    o_all = jnp.concatenate(outs, axis=1)
    lse_all = jnp.concatenate(lses, axis=1)
    (mixed,) = _rowwise(_fn_combine, [o_all, lse_all], [], [(TOK_W, BF16)], ROW_TB, "combine_fwd")
    qm = _heads_major(p[:, 3 * TOK_W:])
    km, vm, memn = kv_fn(mixed)
    memo = _mem_fwd(qm, km, vm, min(MEM_TB, t), "mem_fwd0")
    cat = jnp.concatenate([mixed, _heads_minor(memo).astype(BF16)], axis=1)
    return cat, (saved, o_all, lse_all, qm), (km, vm, memn)


def _attn_mixer_bwd(dcat, res, km, vm):
    saved, o_all, lse_all, qm = res
    t = dcat.shape[0]
    (do_all, dlse_all), _ = _rowwise_bwd(_fn_combine, [o_all, lse_all], [], [dcat[:, :TOK_W]], [BF16, F32],
                                         ROW_TB, "combine_bwd")
    dqs, dks, dvs, dbms = [], [], [], []
    for gi, (_, dil) in enumerate(DIL_GROUPS):
        l_sub = t // dil
        bq = min(BAND_BQ, l_sub)
        q, k, v, bm = saved[gi]
        do = _to_sub(do_all[:, 256 * gi:256 * (gi + 1)], dil)
        dl = _to_sub(dlse_all[:, 256 * gi:256 * (gi + 1)], dil)
        dq, dk, dv, dbm = _band_bwd(q, k, v, bm, do, dl, dil, l_sub, bq, f"band_bwd{gi}")
        dqs.append(_from_sub(dq, dil))
        dks.append(_from_sub(dk[:, BAND_HALF:-BAND_HALF], dil))
        dvs.append(_from_sub(dv[:, BAND_HALF:-BAND_HALF], dil))
        dbms.append(dbm)
    dqm, dkm, dvm = _mem_bwd(qm, km, vm, _heads_major(dcat[:, TOK_W:]), min(MEM_TB, t), "mem_bwd0")
    dp = jnp.concatenate([d.astype(BF16) for d in dqs + dks + dvs + [_heads_minor(dqm)]], axis=1)
    return dp, _relbias_grad(dbms, "relbias_grad"), dkm, dvm


def _dn_mixer_fwd(p, conv_w, a_log, dt_bias, out_norm, km, vm):
    t = p.shape[0]
    rt = min(CONV_RT, t)
    xp = p
    w8 = jnp.pad(conv_w, ((0, 8 - DN_CONV), (0, 0)))
    q = _conv_fwd(xp, w8, 0, rt, "conv_fwd_q")
    k = _conv_fwd(xp, w8, 1, rt, "conv_fwd_k")
    v = _conv_fwd(xp, w8, 2, rt, "conv_fwd_v")
    gate = p[:, 4 * TOK_W:4 * TOK_W + 4 * DN_HEADS].reshape(t, 2, 2, DN_HEADS)
    bshape = (2, DN_HEADS, t, DN_HD)
    al = jnp.broadcast_to(gate[:, :, 0, :].transpose(1, 2, 0)[..., None], bshape)
    be = jnp.broadcast_to(gate[:, :, 1, :].transpose(1, 2, 0)[..., None], bshape)
    a_rows = jnp.broadcast_to(a_log[:, :, None, None], (2, DN_HEADS, 1, DN_HD))
    dt_rows = jnp.broadcast_to(dt_bias[:, :, None, None], (2, DN_HEADS, 1, DN_HD))
    o_f, o_r, states = _dn_fwd(q, k, v, al, be, a_rows, dt_rows, "dn_fwd")
    z = p[:, 3 * TOK_W:4 * TOK_W]
    gain = out_norm.reshape(1, DN_HD)
    (og,) = _rowwise(_fn_outnorm, [o_f, o_r, z], [gain], [(TOK_W, BF16)], ROW_TB, "outnorm_fwd")
    qm = _heads_major(p[:, 4 * TOK_W + 4 * DN_HEADS:DN_IN])
    memo = _mem_fwd(qm, km, vm, min(MEM_TB, t), "mem_fwd1")
    cat = jnp.concatenate([og, _heads_minor(memo).astype(BF16)], axis=1)
    return cat, (xp, w8, q, k, v, al, be, a_rows, dt_rows, o_f, o_r, states, z, gain, qm)


def _dn_mixer_bwd(dcat, res, km, vm):
    xp, w8, q, k, v, al, be, a_rows, dt_rows, o_f, o_r, states, z, gain, qm = res
    t = dcat.shape[0]
    rt = min(CONV_RT, t)
    (do, dz), (dgain,) = _rowwise_bwd(_fn_outnorm, [o_f, o_r, z], [gain], [dcat[:, :TOK_W]], [F32, None, BF16],
                                      ROW_TB, "outnorm_bwd")
    d_f, d_r, dgate, da, ddt = _dn_bwd(q, k, v, al, be, a_rows, dt_rows, states, do, "dn_bwd")
    dxs, dws = [], []
    for kind, nm in enumerate("qkv"):
        dx, dw = _conv_bwd(xp, w8, d_f[kind], d_r[kind], kind, rt, f"conv_bwd_{nm}")
        dxs.append(dx)
        dws.append(dw)
    dconv = jnp.concatenate(dws, axis=1)[:DN_CONV]
    dqm, dkm, dvm = _mem_bwd(qm, km, vm, _heads_major(dcat[:, TOK_W:]), min(MEM_TB, t), "mem_bwd1")
    dp = jnp.concatenate(dxs + [dz, dgate.astype(BF16), _heads_minor(dqm).astype(BF16),
                               jnp.zeros((t, DN_IN_PAD - DN_IN), BF16)], axis=1)
    return dp, dconv, da[:, :, 0, 0], ddt[:, :, 0, 0], dgain.reshape(DN_HD), dkm, dvm


SWI_TB = 256


def _ffn_fwd(h, w_gu_t, w_d, li):
    gu = _matmul(h, w_gu_t, "nt", BF16, f"ffn_gu{li}")
    (a,) = _rowwise(_fn_swiglu, [gu], [], [(D_FF, BF16)], SWI_TB, f"swiglu_fwd{li}")
    return _matmul(a, w_d, "nn", F32, f"ffn_down{li}"), gu, a


def _ffn_bwd(df, h, w_gu_t, w_d, gu, a, li):
    da = _matmul(df, w_d, "nt", BF16, f"ffn_down_dx{li}")
    dwd = _matmul(a, df, "tn", BF16, f"ffn_down_dw{li}")
    (dgu,), _ = _rowwise_bwd(_fn_swiglu, [gu], [], [da], [BF16], SWI_TB, f"swiglu_bwd{li}")
    dh = _matmul(dgu, w_gu_t, "nn", F32, f"ffn_gu_dx{li}")
    dwgu_t = _matmul(dgu, h, "tn", BF16, f"ffn_gu_dw{li}")
    return dh, dwgu_t, dwd


def _fn_first(x, g):
    return x, _rms(x, g)


def _me_xyc():
    return lax.axis_index("x"), lax.axis_index("y"), lax.axis_index("c")


def _flip(coords, k):
    x, y, c = coords
    return (1 - x if k & 4 else x, 1 - y if k & 2 else y, 1 - c if k & 1 else c)


def _index(coords):
    x, y, c = coords
    return 4 * x + 2 * y + c


def _window(ref, axis, size, d):
    idx = [slice(None)] * len(ref.shape)
    idx[axis] = pl.ds(pl.multiple_of(d * size, size), size)
    return ref.at[tuple(idx)]


def _comm_call(body, n, ins, out_shapes, name):
    hbm = pl.BlockSpec(memory_space=pl.ANY)
    return pl.pallas_call(
        body, name=name, in_specs=[hbm] * n, out_specs=[hbm] * n, out_shape=out_shapes,
        scratch_shapes=[pltpu.SemaphoreType.DMA((N_DEV - 1, n)), pltpu.SemaphoreType.DMA((N_DEV - 1, n)),
                        pltpu.SemaphoreType.DMA((n,))],
    )(*ins)


def _run_exchange(n, local, remote, send_sems, recv_sems):
    me = _me_xyc()
    locs = [local(p) for p in range(n)]
    for cp in locs:
        cp.start()
    sends = [remote(k, p, me, _flip(me, k)) for k in range(1, N_DEV) for p in range(n)]
    for cp in sends:
        cp.start()
    for k in range(1, N_DEV):
        for p in range(n):
            remote(k, p, _flip(me, k), me).wait_recv()
    for cp in sends:
        cp.wait_send()
    for cp in locs:
        cp.wait()


def _all_gather(shards, axes, name):
    n = len(shards)
    sizes = [s.shape[a] for s, a in zip(shards, axes)]

    def body(*refs):
        ins, outs = refs[:n], refs[n:2 * n]
        send_sems, recv_sems, loc_sems = refs[2 * n:]
        me = _me_xyc()

        def local(p):
            return pltpu.make_async_copy(ins[p], _window(outs[p], axes[p], sizes[p], _index(me)), loc_sems.at[p])

        def remote(k, p, owner, to):
            return pltpu.make_async_remote_copy(
                src_ref=ins[p], dst_ref=_window(outs[p], axes[p], sizes[p], _index(owner)),
                send_sem=send_sems.at[k - 1, p], recv_sem=recv_sems.at[k - 1, p], device_id=to, device_id_type=MESH)

        _run_exchange(n, local, remote, send_sems, recv_sems)

    def full(s, a):
        return s.shape[:a] + (N_DEV * s.shape[a],) + s.shape[a + 1:]

    return _comm_call(body, n, shards, [jax.ShapeDtypeStruct(full(s, a), s.dtype) for s, a in zip(shards, axes)], name)


def _exchange(fulls, axes, name):
    n = len(fulls)
    sizes = [None if a is None else f.shape[a] // N_DEV for f, a in zip(fulls, axes)]

    def part_shape(f, a):
        return f.shape if a is None else f.shape[:a] + (f.shape[a] // N_DEV,) + f.shape[a + 1:]

    def body(*refs):
        ins, outs = refs[:n], refs[n:2 * n]
        send_sems, recv_sems, loc_sems = refs[2 * n:]
        me = _me_xyc()

        def src(p, to):
            return ins[p] if axes[p] is None else _window(ins[p], axes[p], sizes[p], _index(to))

        def local(p):
            return pltpu.make_async_copy(src(p, me), outs[p].at[_index(me)], loc_sems.at[p])

        def remote(k, p, sender, to):
            return pltpu.make_async_remote_copy(
                src_ref=src(p, to), dst_ref=outs[p].at[_index(sender)],
                send_sem=send_sems.at[k - 1, p], recv_sem=recv_sems.at[k - 1, p], device_id=to, device_id_type=MESH)

        _run_exchange(n, local, remote, send_sems, recv_sems)

    return _comm_call(body, n, fulls,
                      [jax.ShapeDtypeStruct((N_DEV,) + part_shape(f, a), f.dtype) for f, a in zip(fulls, axes)], name)


_HBM = pl.BlockSpec(memory_space=pltpu.HBM)
_SEM = pl.BlockSpec(memory_space=pltpu.SEMAPHORE)
_EFFECT = pltpu.SideEffectType.DATAFLOW_SIDE_EFFECTING


def _in_hbm(a):
    return pltpu.with_memory_space_constraint(a, pltpu.HBM)


def _split_start(srcs, lands, after, descr, name):
    n = len(srcs)

    def body(*refs):
        ins, lnd = refs[:n], refs[n:2 * n]
        send_sems, recv_sems = refs[2 * n + 1], refs[2 * n + 2]
        token = refs[-1]
        me = _me_xyc()
        for k in range(1, N_DEV):
            for p in range(n):
                descr(k, p, ins, lnd, send_sems, recv_sems, me, _flip(me, k)).start()
        token[...] = jnp.zeros_like(token)

    sems = pltpu.SemaphoreType.DMA(((N_DEV - 1) * n,))
    res = pl.pallas_call(
        body, name=name,
        out_shape=(sems, sems, *[pltpu.HBM(a.shape, a.dtype) for a in (*srcs, *lands)],
                   jax.ShapeDtypeStruct((8, 128), F32)),
        in_specs=[_HBM] * (2 * n) + [pl.BlockSpec(memory_space=pl.ANY)],
        out_specs=(_SEM, _SEM, *[_HBM] * (2 * n), pl.BlockSpec(memory_space=pltpu.VMEM)),
        input_output_aliases={i: 2 + i for i in range(2 * n)},
        compiler_params=pltpu.CompilerParams(has_side_effects=_EFFECT),
    )(*[_in_hbm(a) for a in (*srcs, *lands)], after)
    return res[0], res[1], res[2:2 + n], res[2 + n:2 + 2 * n], res[-1]


def _split_wait(send_sems, recv_sems, srcs, lands, after, descr, name):
    n = len(srcs)

    def body(*refs):
        ins, lnd = refs[:n], refs[n:2 * n]
        s_sems, r_sems = refs[2 * n], refs[2 * n + 1]
        me = _me_xyc()
        for k in range(1, N_DEV):
            for p in range(n):
                peer = _flip(me, k)
                descr(k, p, ins, lnd, s_sems, r_sems, me, peer).wait_send()
                descr(k, p, ins, lnd, s_sems, r_sems, peer, me).wait_recv()

    res = pl.pallas_call(
        body, name=name,
        out_shape=tuple(pltpu.HBM(a.shape, a.dtype) for a in (*srcs, *lands)),
        in_specs=[_HBM] * (2 * n) + [_SEM, _SEM, pl.BlockSpec(memory_space=pl.ANY)],
        out_specs=tuple([_HBM] * (2 * n)),
        input_output_aliases={i: i for i in range(2 * n)},
        compiler_params=pltpu.CompilerParams(has_side_effects=_EFFECT),
    )(*srcs, *lands, send_sems, recv_sems, after)
    return list(res[n:])


def _gather_descr(axes, sizes):
    def descr(k, p, ins, lnd, send_sems, recv_sems, sender, dest):
        return pltpu.make_async_remote_copy(
            src_ref=ins[p], dst_ref=_window(lnd[p], axes[p], sizes[p], _index(sender)),
            send_sem=send_sems.at[(k - 1) * len(axes) + p], recv_sem=recv_sems.at[(k - 1) * len(axes) + p],
            device_id=dest, device_id_type=MESH)
    return descr


def _exchange_descr(axes, sizes):
    def descr(k, p, ins, lnd, send_sems, recv_sems, sender, dest):
        return pltpu.make_async_remote_copy(
            src_ref=_window(ins[p], axes[p], sizes[p], _index(dest)), dst_ref=lnd[p].at[_index(sender)],
            send_sem=send_sems.at[(k - 1) * len(axes) + p], recv_sem=recv_sems.at[(k - 1) * len(axes) + p],
            device_id=dest, device_id_type=MESH)
    return descr


def _gather_begin(shards, axes, after, name):
    sizes = [s.shape[a] for s, a in zip(shards, axes)]
    me = _index(_me_xyc())
    lands = []
    for s, a, sz in zip(shards, axes, sizes):
        full = s.shape[:a] + (N_DEV * sz,) + s.shape[a + 1:]
        lands.append(lax.dynamic_update_slice_in_dim(lax.empty(full, s.dtype), s, me * sz, a))
    descr = _gather_descr(axes, sizes)
    send_sems, recv_sems, srcs, lands, token = _split_start(shards, lands, after, descr, name)
    return (send_sems, recv_sems, srcs, lands, descr), token


def _exchange_begin(fulls, axes, after, name):
    sizes = [f.shape[a] // N_DEV for f, a in zip(fulls, axes)]
    me = _index(_me_xyc())
    lands = []
    for f, a, sz in zip(fulls, axes, sizes):
        own = lax.dynamic_slice_in_dim(f, me * sz, sz, a)
        lands.append(lax.dynamic_update_slice_in_dim(lax.empty((N_DEV,) + own.shape, f.dtype), own[None], me, 0))
    descr = _exchange_descr(axes, sizes)
    send_sems, recv_sems, srcs, lands, token = _split_start(fulls, lands, after, descr, name)
    return (send_sems, recv_sems, srcs, lands, descr), token


def _split_end(handle, after, name):
    send_sems, recv_sems, srcs, lands, descr = handle
    return _split_wait(send_sems, recv_sems, srcs, lands, after, descr, name)


def _adam_math(g, w, m, v):
    m = ADAM_B1 * m + (1.0 - ADAM_B1) * g
    v = ADAM_B2 * v + (1.0 - ADAM_B2) * (g * g)
    m_hat = m / (1.0 - ADAM_B1 ** ADAM_STEP)
    v_hat = v / (1.0 - ADAM_B2 ** ADAM_STEP)
    delta = -ADAM_LR * (m_hat / (jnp.sqrt(v_hat) + ADAM_EPS) + ADAM_WD * w)
    return delta, m, v


def _sum_slabs(r_ref):
    g = r_ref[0].astype(F32)
    for s in range(1, N_DEV):
        g = g + r_ref[s].astype(F32)
    return g


def _adamw_reduce(recv, w, m, v, tb, name):
    r, c = w.shape

    def body(r_ref, w_ref, m_ref, v_ref, g_ref, d_ref, nm_ref, nv_ref):
        g = _sum_slabs(r_ref)
        g_ref[...] = g
        d_ref[...], nm_ref[...], nv_ref[...] = _adam_math(g, w_ref[...], m_ref[...], v_ref[...])

    blk = pl.BlockSpec((tb, c), lambda i: (i, 0))
    return pl.pallas_call(
        body, name=name, grid=(r // tb,),
        in_specs=[pl.BlockSpec((N_DEV, tb, c), lambda i: (0, i, 0)), blk, blk, blk],
        out_specs=[blk] * 4, out_shape=[jax.ShapeDtypeStruct((r, c), F32)] * 4,
        compiler_params=_cparams(("parallel",)),
    )(recv, w, m, v)


def _reduce8(recv, tb, name):
    r, c = recv.shape[1:]

    def body(r_ref, g_ref):
        g_ref[...] = _sum_slabs(r_ref)

    return pl.pallas_call(
        body, name=name, grid=(r // tb,),
        in_specs=[pl.BlockSpec((N_DEV, tb, c), lambda i: (0, i, 0))],
        out_specs=pl.BlockSpec((tb, c), lambda i: (i, 0)), out_shape=jax.ShapeDtypeStruct((r, c), F32),
        compiler_params=_cparams(("parallel",)),
    )(recv)


def _adamw(g, w, m, v, tb, name):
    r, c = w.shape

    def body(g_ref, w_ref, m_ref, v_ref, d_ref, nm_ref, nv_ref):
        d_ref[...], nm_ref[...], nv_ref[...] = _adam_math(g_ref[...], w_ref[...], m_ref[...], v_ref[...])

    blk = pl.BlockSpec((tb, c), lambda i: (i, 0))
    return pl.pallas_call(
        body, name=name, grid=(r // tb,), in_specs=[blk] * 4, out_specs=[blk] * 3,
        out_shape=[jax.ShapeDtypeStruct((r, c), F32)] * 3, compiler_params=_cparams(("parallel",)),
    )(g, w, m, v)


DN_IN_SHARD = DN_IN // N_DEV
DN_IN_SHARD_PAD = 432
CONV_SHARD = (1, DN_CONV, 288)


def _pack_small(arrs, rows):
    flat = jnp.concatenate([a.astype(F32).reshape(-1) for a in arrs])
    return jnp.pad(flat, (0, rows * PACK_C - flat.size)).reshape(rows, PACK_C)


def _unpack_small(packed, shapes):
    flat, out, off = packed.reshape(-1), [], 0
    for shp in shapes:
        n = int(np.prod(shp))
        out.append(flat[off:off + n].reshape(shp))
        off += n
    return out


def kernel(x, mem, rel_bias, att_w_in, att_w_out, dn_w_in, dn_conv, dn_a_log, dn_dt_bias, dn_out_norm, dn_w_out, mem_norm, mem_w_kv, norm_mix_pre, norm_mix_post, norm_ffn_pre, norm_ffn_post, ffn_w_gate_up, ffn_w_down, loss_target, m_rel_bias, m_att_w_in, m_att_w_out, m_dn_w_in, m_dn_conv, m_dn_a_log, m_dn_dt_bias, m_dn_out_norm, m_dn_w_out, m_mem_norm, m_mem_w_kv, m_norm_mix_pre, m_norm_mix_post, m_norm_ffn_pre, m_norm_ffn_post, m_ffn_w_gate_up, m_ffn_w_down, v_rel_bias, v_att_w_in, v_att_w_out, v_dn_w_in, v_dn_conv, v_dn_a_log, v_dn_dt_bias, v_dn_out_norm, v_dn_w_out, v_mem_norm, v_mem_w_kv, v_norm_mix_pre, v_norm_mix_post, v_norm_ffn_pre, v_norm_ffn_post, v_ffn_w_gate_up, v_ffn_w_down):
    x0, mem0, tgt = x[0], mem[0], loss_target[0]
    t = x0.shape[0]
    axes = ("x", "y", "c")

    def t_shard(w):
        return jnp.swapaxes(w, 1, 2).astype(BF16)

    dn_in_pad = ((0, 0), (0, DN_IN_SHARD_PAD - DN_IN_SHARD), (0, 0))
    (w_att_in_t,) = _all_gather([t_shard(att_w_in)], [1], "allgather_first")
    w_att_in_t = w_att_in_t[0]
    gu_t, down = t_shard(ffn_w_gate_up), ffn_w_down.astype(BF16)
    gather_o, tok_o = _gather_begin([att_w_out.astype(BF16), mem_w_kv.astype(BF16)], [1, 1], w_att_in_t,
                                    "gather_att_out_start")
    gather_a, tok_a = _gather_begin([gu_t[0:1], down[0:1]], [1, 1], tok_o, "gather_ffn0_start")
    gather_b, tok_b = _gather_begin(
        [jnp.pad(t_shard(dn_w_in), dn_in_pad), dn_w_out.astype(BF16), gu_t[1:2], down[1:2], dn_conv],
        [1, 1, 1, 1, 0], tok_a, "gather_layer1_start")

    def gain(a, i):
        return a[i].reshape(1, D)

    (h0,) = _rowwise(_fn_pre, [x0], [gain(norm_mix_pre, 0) + tok_b[0:1, 0:1]], [(D, BF16)], ROW_TB, "pre0")
    p0 = _matmul(h0, w_att_in_t, "nt", BF16, "att_in")
    late = {}

    def kv0(after):
        late["w_att_out"], late["w_kv"] = _split_end(gather_o, after, "gather_att_out_wait")
        return _mem_kv_fwd(mem0, gain(mem_norm, 0), late["w_kv"][0], 0)

    cat0, res0, (km0, vm0, memn0) = _attn_mixer_fwd(p0, rel_bias, kv0)
    w_att_out, w_kv = late["w_att_out"][0], late["w_kv"]
    y0 = _matmul(cat0, w_att_out, "nn", F32, "att_out")
    g_a = [gain(norm_mix_post, 0), gain(norm_ffn_pre, 0)]
    x1, h1 = _rowwise(_fn_res_pre, [x0, y0], g_a, [(D, F32), (D, BF16)], ROW_TB, "res_pre0")
    w_gu_t0, w_down0 = [w[0] for w in _split_end(gather_a, h1, "gather_ffn0_wait")]
    f0, gu0, a0 = _ffn_fwd(h1, w_gu_t0, w_down0, 0)
    g_b = [gain(norm_ffn_post, 0), gain(norm_mix_pre, 1)]
    x2, h2 = _rowwise(_fn_res_pre, [x1, f0], g_b, [(D, F32), (D, BF16)], ROW_TB, "res_pre1")
    km1, vm1, memn1 = _mem_kv_fwd(mem0, gain(mem_norm, 1), w_kv[1], 1)
    w_dn_in_g, w_dn_out, w_gu_t1, w_down1, conv_g = _split_end(gather_b, h2, "gather_layer1_wait")
    w_dn_in_g, w_dn_out, w_gu_t1, w_down1 = w_dn_in_g[0], w_dn_out[0], w_gu_t1[0], w_down1[0]
    conv_full = conv_g.transpose(1, 0, 2).reshape(DN_CONV, 3 * TOK_W)
    w_dn_in_t = jnp.concatenate(
        [w_dn_in_g[DN_IN_SHARD_PAD * j:DN_IN_SHARD_PAD * j + DN_IN_SHARD] for j in range(N_DEV)]
        + [jnp.zeros((DN_IN_PAD - DN_IN, D), BF16)], axis=0)
    p1 = _matmul(h2, w_dn_in_t, "nt", F32, "dn_in")
    cat1, res1 = _dn_mixer_fwd(p1, conv_full, dn_a_log[0], dn_dt_bias[0], dn_out_norm[0], km1, vm1)
    y1 = _matmul(cat1, w_dn_out, "nn", F32, "dn_out")
    g_c = [gain(norm_mix_post, 1), gain(norm_ffn_pre, 1)]
    x3, h3 = _rowwise(_fn_res_pre, [x2, y1], g_c, [(D, F32), (D, BF16)], ROW_TB, "res_pre2")
    f1, gu1, a1 = _ffn_fwd(h3, w_gu_t1, w_down1, 1)
    g_d = [gain(norm_ffn_post, 1)]
    (x4,) = _rowwise(_fn_res, [x3, f1], g_d, [(D, F32)], ROW_TB, "res3")
    dx4, lrow = _loss_kernel(x4, tgt, ROW_TB, "loss")
    loss = lax.psum(lrow[0, 0] * (0.5 / D), axes)

    (df1,), (dg_fpost1,) = _rowwise_bwd(_fn_res, [x3, f1], g_d, [dx4], [None, BF16], ROW_TB, "res3_bwd")
    dh3, dwgu1, dwd1 = _ffn_bwd(df1, h3, w_gu_t1, w_down1, gu1, a1, 1)
    (dx2, dy1), (dg_mpost1, dg_fpre1) = _rowwise_bwd(_fn_res_pre, [x2, y1], g_c, [dx4, dh3], [F32, BF16],
                                                     ROW_TB, "res_pre2_bwd")
    dcat1 = _matmul(dy1, w_dn_out, "nt", F32, "dn_out_dx")
    dw_dn_out = _matmul(cat1, dy1, "tn", BF16, "dn_out_dw")
    dp1, dconv, da_log, ddt_bias, dout_norm, dkm1, dvm1 = _dn_mixer_bwd(dcat1, res1, km1, vm1)
    dwkv1, dg_mem1 = _mem_kv_bwd(mem0, gain(mem_norm, 1), w_kv[1], memn1, dkm1, dvm1, 1)
    dh2 = _matmul(dp1, w_dn_in_t, "nn", F32, "dn_in_dx")
    dw_dn_in_t = _matmul(dp1, h2, "tn", BF16, "dn_in_dw")
    dn_in_parts = [jnp.pad(dw_dn_in_t[DN_IN_SHARD * j:DN_IN_SHARD * (j + 1)],
                           ((0, DN_IN_SHARD_PAD - DN_IN_SHARD), (0, 0))) for j in range(N_DEV)]
    xch_b, tok = _exchange_begin(
        [jnp.concatenate(dn_in_parts, axis=0)[None], dw_dn_out[None], dwkv1[None], dwgu1[None], dwd1[None]],
        [1, 1, 1, 1, 1], dh2, "exchange_layer1_start")
    (dx1, df0), (dg_fpost0, dg_mpre1) = _rowwise_bwd(_fn_res_pre, [x1, f0], [g + tok[0:1, 0:1] for g in g_b],
                                                     [dx2, dh2], [F32, BF16], ROW_TB, "res_pre1_bwd")
    dh1, dwgu0, dwd0 = _ffn_bwd(df0, h1, w_gu_t0, w_down0, gu0, a0, 0)
    xch_a, tok = _exchange_begin([dwgu0[None], dwd0[None]], [1, 1], dh1, "exchange_ffn0_start")
    (dx0, dy0), (dg_mpost0, dg_fpre0) = _rowwise_bwd(_fn_res_pre, [x0, y0], [g + tok[0:1, 0:1] for g in g_a],
                                                     [dx1, dh1], [F32, BF16], ROW_TB, "res_pre0_bwd")
    dcat0 = _matmul(dy0, w_att_out, "nt", F32, "att_out_dx")
    dw_att_out = _matmul(cat0, dy0, "tn", BF16, "att_out_dw")
    dp0, drel, dkm0, dvm0 = _attn_mixer_bwd(dcat0, res0, km0, vm0)
    dwkv0, dg_mem0 = _mem_kv_bwd(mem0, gain(mem_norm, 0), w_kv[0], memn0, dkm0, dvm0, 0)
    xch_o, tok = _exchange_begin([dw_att_out[None], dwkv0[None]], [1, 1], dp0, "exchange_att_out_start")
    dw_att_in_t = _matmul(dp0, h0, "tn", BF16, "att_in_dw")
    xch_i, tok_i = _exchange_begin([dw_att_in_t[None]], [1], tok, "exchange_att_in_start")
    dh0 = _matmul(dp0, w_att_in_t, "nn", F32, "att_in_dx")
    (grad_x,), (dg_mpre0,) = _rowwise_bwd(_fn_first, [x0], [gain(norm_mix_pre, 0) + tok_i[0:1, 0:1]], [dx0, dh0],
                                          [F32], ROW_TB, "pre0_bwd")

    small_grads = [drel, da_log, ddt_bias, dout_norm, jnp.concatenate([dg_mem0, dg_mem1]),
                   jnp.concatenate([dg_mpre0, dg_mpre1]), jnp.concatenate([dg_mpost0, dg_mpost1]),
                   jnp.concatenate([dg_fpre0, dg_fpre1]), jnp.concatenate([dg_fpost0, dg_fpost1]), dconv]
    (r_small,) = _exchange([_pack_small(small_grads, SMALL_ROWS)], [None], "exchange_last")
    (r_att_in,) = _split_end(xch_i, r_small, "exchange_att_in_wait")
    r_att_out, r_kv0 = _split_end(xch_o, r_small, "exchange_att_out_wait")
    r_gu0, r_down0 = _split_end(xch_a, r_small, "exchange_ffn0_wait")
    r_dn_in, r_dn_out, r_kv1, r_gu1, r_down1 = _split_end(xch_b, r_small, "exchange_layer1_wait")

    def rows(a):
        return a.reshape((-1,) + a.shape[-1:])

    def row_sharded(recv, w, m, v, tb, name):
        outs = _adamw_reduce(recv.reshape((N_DEV, -1) + recv.shape[-1:]), rows(w), rows(m), rows(v), tb, name)
        return [o.reshape(w.shape) for o in outs]

    def col_sharded(recv, w, m, v, tb, name):
        g_t = _reduce8(recv.reshape((N_DEV, -1) + recv.shape[-1:]), tb, name + "_sum")
        g = jnp.swapaxes(g_t.reshape(recv.shape[1:])[:, :w.shape[2]], 1, 2)
        outs = _adamw(rows(g), rows(w), rows(m), rows(v), 256, name)
        return [g] + [o.reshape(w.shape) for o in outs]

    def per_layer(fn, recvs, w, m, v, tb, name):
        outs = [fn(r, w[l:l + 1], m[l:l + 1], v[l:l + 1], tb, f"{name}{l}") for l, r in enumerate(recvs)]
        return [jnp.concatenate(pair, axis=0) for pair in zip(*outs)]

    big = [col_sharded(r_att_in, att_w_in, m_att_w_in, v_att_w_in, 320, "adamw_att_in"),
           row_sharded(r_att_out, att_w_out, m_att_w_out, v_att_w_out, 128, "adamw_att_out"),
           col_sharded(r_dn_in, dn_w_in, m_dn_w_in, v_dn_w_in, 432, "adamw_dn_in"),
           row_sharded(r_dn_out, dn_w_out, m_dn_w_out, v_dn_w_out, 128, "adamw_dn_out"),
           per_layer(row_sharded, [r_kv0, r_kv1], mem_w_kv, m_mem_w_kv, v_mem_w_kv, 128, "adamw_mem_kv"),
           per_layer(col_sharded, [r_gu0, r_gu1], ffn_w_gate_up, m_ffn_w_gate_up, v_ffn_w_gate_up, 176,
                     "adamw_ffn_gu"),
           per_layer(row_sharded, [r_down0, r_down1], ffn_w_down, m_ffn_w_down, v_ffn_w_down, 176,
                     "adamw_ffn_down")]
    g_big, d_big, nm_big, nv_big = [[b[i] for b in big] for i in range(4)]

    g_small = _reduce8(r_small, SMALL_ROWS, "reduce_small")
    rep_shapes = [(32, 12), (1, 2, 6), (1, 2, 6), (1, 128), (2, D), (2, D), (2, D), (2, D), (2, D)]
    *g_rep, g_conv_full = _unpack_small(g_small, rep_shapes + [(DN_CONV, 3 * TOK_W)])
    me = _index(_me_xyc())
    g_conv = lax.dynamic_slice(g_conv_full, (0, me * 288), (DN_CONV, 288)).reshape(CONV_SHARD)
    small_shapes = rep_shapes + [CONV_SHARD]
    small_w = [rel_bias, dn_a_log, dn_dt_bias, dn_out_norm, mem_norm, norm_mix_pre, norm_mix_post,
               norm_ffn_pre, norm_ffn_post, dn_conv]
    small_m = [m_rel_bias, m_dn_a_log, m_dn_dt_bias, m_dn_out_norm, m_mem_norm, m_norm_mix_pre, m_norm_mix_post,
               m_norm_ffn_pre, m_norm_ffn_post, m_dn_conv]
    small_v = [v_rel_bias, v_dn_a_log, v_dn_dt_bias, v_dn_out_norm, v_mem_norm, v_norm_mix_pre, v_norm_mix_post,
               v_norm_ffn_pre, v_norm_ffn_post, v_dn_conv]
    g_small_list = g_rep + [g_conv]
    outs_small = _adamw(_pack_small(g_small_list, 24), _pack_small(small_w, 24), _pack_small(small_m, 24),
                        _pack_small(small_v, 24), 24, "adamw_small")
    d_small, nm_small, nv_small = [_unpack_small(o, small_shapes) for o in outs_small]

    def ordered(small, big):
        return [small[0], big[0], big[1], big[2], small[9], small[1], small[2], small[3], big[3], small[4],
                big[4], small[5], small[6], small[7], small[8], big[5], big[6]]

    g_small_out = [g.reshape(s) for g, s in zip(g_small_list, small_shapes)]
    return (loss, grad_x[None], *ordered(g_small_out, g_big), *ordered(d_small, d_big),
            *ordered(nm_small, nm_big), *ordered(nv_small, nv_big))
```

```python
import functools
import math

import numpy as np
import jax
import jax.numpy as jnp
from jax import lax
from jax.experimental import pallas as pl
from jax.experimental.pallas import tpu as pltpu

F32 = jnp.float32
BF16 = jnp.bfloat16
HI = lax.Precision.HIGHEST
MESH = pl.DeviceIdType.MESH

N_DEV = 8
D = 1024
EPS = 1e-6
NEG = -1e30
TOK_W = 768
MEM_W = 256
ATT_HD = 64
DIL_GROUPS = ((128, 1), (512, 4), (2048, 16))
BAND_HALF = 64
REL_BUCKETS = 32
REL_MAX_DIST = 1024
DN_HD = 128
DN_HEADS = 6
DN_CONV = 5
DN_CHUNK = 64
MEM_HEADS = 4
D_FF = 2816
ATT_IN = 2560
DN_IN = 3352
DN_IN_PAD = 3456

ADAM_LR, ADAM_B1, ADAM_B2, ADAM_EPS, ADAM_WD, ADAM_STEP = 0.001, 0.9, 0.999, 1e-08, 0.01, 10

PACK_C = 512
BIG_ROWS = 6480
SMALL_ROWS = 48
VMEM_LIMIT = 48 * 1024 * 1024


def _cparams(sem=None):
    kw = dict(vmem_limit_bytes=VMEM_LIMIT)
    if sem is not None:
        kw["dimension_semantics"] = sem
    return pltpu.CompilerParams(**kw)


def _tile(n, cap):
    if n <= cap:
        return n
    best = None
    for t in range(128, cap + 1, 128):
        if n % t == 0:
            best = t
    assert best is not None, (n, cap)
    return best


def _matmul(a, b, mode, out_dtype, name, tm=1024, tn=1408, tk=None):
    if tk is None:
        tk = 4096 if mode == "tn" else 2816
    if mode == "tn":
        tm = min(tm, 512)
    if mode == "nn":
        (m, kc), (_, n) = a.shape, b.shape
        dims = (((1,), (0,)), ((), ()))
    elif mode == "nt":
        (m, kc), (n, _) = a.shape, b.shape
        dims = (((1,), (1,)), ((), ()))
    else:
        (kc, m), (_, n) = a.shape, b.shape
        dims = (((0,), (0,)), ((), ()))
    tm = m if m <= tm else _tile(m, tm)
    tn = _tile(n, tn)
    tk = _tile(kc, tk)
    nk = kc // tk

    def body(a_ref, b_ref, o_ref, acc_ref):
        k = pl.program_id(2)
        part = lax.dot_general(a_ref[...], b_ref[...], dims, preferred_element_type=F32)

        @pl.when(k == 0)
        def _():
            acc_ref[...] = part

        @pl.when(k > 0)
        def _():
            acc_ref[...] += part

        @pl.when(k == nk - 1)
        def _():
            o_ref[...] = acc_ref[...].astype(o_ref.dtype)

    if mode == "nn":
        a_spec = pl.BlockSpec((tm, tk), lambda i, j, k: (i, k))
        b_spec = pl.BlockSpec((tk, tn), lambda i, j, k: (k, j))
    elif mode == "nt":
        a_spec = pl.BlockSpec((tm, tk), lambda i, j, k: (i, k))
        b_spec = pl.BlockSpec((tn, tk), lambda i, j, k: (j, k))
    else:
        a_spec = pl.BlockSpec((tk, tm), lambda i, j, k: (k, i))
        b_spec = pl.BlockSpec((tk, tn), lambda i, j, k: (k, j))
    return pl.pallas_call(
        body, name=name, grid=(m // tm, n // tn, nk),
        in_specs=[a_spec, b_spec],
        out_specs=pl.BlockSpec((tm, tn), lambda i, j, k: (i, j)),
        out_shape=jax.ShapeDtypeStruct((m, n), out_dtype),
        scratch_shapes=[pltpu.VMEM((tm, tn), F32)],
        compiler_params=_cparams(("parallel", "parallel", "arbitrary")),
    )(a, b)


def _rowwise(fn, rows, params, outs, tb, name):
    t = rows[0].shape[0]
    nr, npar = len(rows), len(params)

    def body(*refs):
        ins = [r[...].astype(F32) for r in refs[:nr + npar]]
        res = fn(*ins)
        for o_ref, r in zip(refs[nr + npar:], res):
            o_ref[...] = r.astype(o_ref.dtype)

    return pl.pallas_call(
        body, name=name, grid=(t // tb,),
        in_specs=[pl.BlockSpec((tb, r.shape[1]), lambda i: (i, 0)) for r in rows]
        + [pl.BlockSpec(p.shape, lambda i: (0, 0)) for p in params],
        out_specs=[pl.BlockSpec((tb, c), lambda i: (i, 0)) for c, _ in outs],
        out_shape=[jax.ShapeDtypeStruct((t, c), dt) for c, dt in outs],
        compiler_params=_cparams(("parallel",)),
    )(*rows, *params)


def _rowwise_bwd(fn, rows, params, cots, row_grad, tb, name):
    t = rows[0].shape[0]
    nr, npar, nc = len(rows), len(params), len(cots)
    want = [i for i, g in enumerate(row_grad) if g is not None]

    def body(*refs):
        ins = [r[...].astype(F32) for r in refs[:nr + npar]]
        cts = tuple(r[...].astype(F32) for r in refs[nr + npar:nr + npar + nc])
        outs = refs[nr + npar + nc:]
        _, vjp = jax.vjp(fn, *ins)
        grads = vjp(cts)
        for o_ref, i in zip(outs[:len(want)], want):
            o_ref[...] = grads[i].astype(o_ref.dtype)
        first = pl.program_id(0) == 0
        for o_ref, g in zip(outs[len(want):], grads[nr:]):
            @pl.when(first)
            def _(o_ref=o_ref, g=g):
                o_ref[...] = g

            @pl.when(jnp.logical_not(first))
            def _(o_ref=o_ref, g=g):
                o_ref[...] += g

    res = pl.pallas_call(
        body, name=name, grid=(t // tb,),
        in_specs=[pl.BlockSpec((tb, r.shape[1]), lambda i: (i, 0)) for r in rows]
        + [pl.BlockSpec(p.shape, lambda i: (0, 0)) for p in params]
        + [pl.BlockSpec((tb, c.shape[1]), lambda i: (i, 0)) for c in cots],
        out_specs=[pl.BlockSpec((tb, rows[i].shape[1]), lambda i_: (i_, 0)) for i in want]
        + [pl.BlockSpec(p.shape, lambda i: (0, 0)) for p in params],
        out_shape=[jax.ShapeDtypeStruct(rows[i].shape, row_grad[i]) for i in want]
        + [jax.ShapeDtypeStruct(p.shape, F32) for p in params],
        compiler_params=_cparams(("arbitrary",)),
    )(*rows, *params, *cots)
    return list(res[:len(want)]), list(res[len(want):])


def _rms(x, g):
    return x * lax.rsqrt(jnp.mean(x * x, axis=-1, keepdims=True) + EPS) * g


def _fn_pre(x, g):
    return (_rms(x, g),)


def _fn_res_pre(x, y, g_post, g_pre):
    x1 = x + _rms(y, g_post)
    return x1, _rms(x1, g_pre)


def _fn_res(x, y, g_post):
    return (x + _rms(y, g_post),)


def _sigmoid(x):
    return 1.0 / (1.0 + jnp.exp(-x))


def _silu(x):
    return x * _sigmoid(x)


def _fn_swiglu(gu):
    return (_silu(gu[:, :D_FF]) * gu[:, D_FF:],)


def _fn_combine(o, lse):
    ls = [lse[:, 256 * g:256 * (g + 1)] for g in range(3)]
    mx = lax.stop_gradient(jnp.maximum(jnp.maximum(ls[0], ls[1]), ls[2]))
    es = [jnp.exp(l - mx) for l in ls]
    inv = 1.0 / (es[0] + es[1] + es[2])
    return (jnp.concatenate([o[:, 256 * g:256 * (g + 1)] * (es[g] * inv) for g in range(3)], axis=1),)


def _fn_outnorm(o_f, o_r, z, gain):
    res = []
    for h in range(DN_HEADS):
        sl = slice(DN_HD * h, DN_HD * (h + 1))
        o = o_f[:, sl] + o_r[:, sl]
        res.append(o * lax.rsqrt(jnp.mean(o * o, axis=-1, keepdims=True) + EPS) * gain * _silu(z[:, sl]))
    return (jnp.concatenate(res, axis=1),)


def _loss_kernel(x, tgt, tb, name):
    t, d = x.shape

    def body(x_ref, t_ref, dx_ref, l_ref, acc_ref):
        i = pl.program_id(0)
        e = x_ref[...] - t_ref[...]
        dx_ref[...] = e * (1.0 / d)
        part = jnp.sum(e * e, axis=0, keepdims=True)

        @pl.when(i == 0)
        def _():
            acc_ref[...] = part

        @pl.when(i > 0)
        def _():
            acc_ref[...] += part

        @pl.when(i == t // tb - 1)
        def _():
            l_ref[...] = jnp.broadcast_to(jnp.sum(acc_ref[...], axis=-1, keepdims=True), (1, 128))

    return pl.pallas_call(
        body, name=name, grid=(t // tb,),
        in_specs=[pl.BlockSpec((tb, d), lambda i: (i, 0))] * 2,
        out_specs=[pl.BlockSpec((tb, d), lambda i: (i, 0)), pl.BlockSpec((1, 128), lambda i: (0, 0))],
        out_shape=[jax.ShapeDtypeStruct((t, d), F32), jax.ShapeDtypeStruct((1, 128), F32)],
        scratch_shapes=[pltpu.VMEM((1, d), F32)],
        compiler_params=_cparams(("arbitrary",)),
    )(x, tgt)


def _band_fn(l_sub, bq, i, q, kw, vw, bm):
    w = bq + 2 * BAND_HALF
    s = lax.dot_general((q * (ATT_HD ** -0.5)).astype(BF16), kw.astype(BF16), (((2,), (2,)), ((0,), (0,))),
                        preferred_element_type=F32) + bm
    kpos = i * bq - BAND_HALF + lax.broadcasted_iota(jnp.int32, (4, bq, w), 2)
    s = jnp.where((kpos >= 0) & (kpos < l_sub), s, NEG)
    m = lax.stop_gradient(jnp.max(s, axis=-1, keepdims=True))
    p = jnp.exp(s - m)
    den = jnp.sum(p, axis=-1, keepdims=True)
    o = lax.dot_general(p.astype(BF16), vw.astype(BF16), (((2,), (1,)), ((0,), (0,))),
                        preferred_element_type=F32) / den
    return o, jnp.broadcast_to(m + jnp.log(den), o.shape)


def _band_specs(l_sub, bq):
    w = bq + 2 * BAND_HALF
    qs = pl.BlockSpec((None, 4, bq, ATT_HD), lambda r, i: (r, 0, i, 0))
    ks = pl.BlockSpec((None, 4, l_sub + 2 * BAND_HALF, ATT_HD), lambda r, i: (r, 0, 0, 0))
    bs = pl.BlockSpec((4, bq, w), lambda r, i: (0, 0, 0))
    return qs, ks, bs


def _band_fwd(q, k, v, bm, dil, l_sub, bq, name):
    w = bq + 2 * BAND_HALF
    qs, ks, bs = _band_specs(l_sub, bq)

    def body(q_ref, k_ref, v_ref, bm_ref, o_ref, l_ref):
        i = pl.program_id(1)
        st = pl.multiple_of(i * bq, bq)
        o, lse = _band_fn(l_sub, bq, i, q_ref[...].astype(F32), k_ref[:, pl.ds(st, w), :].astype(F32),
                          v_ref[:, pl.ds(st, w), :].astype(F32), bm_ref[...])
        o_ref[...] = o
        l_ref[...] = lse

    return pl.pallas_call(
        body, name=name, grid=(dil, l_sub // bq),
        in_specs=[qs, ks, ks, bs], out_specs=[qs, qs],
        out_shape=[jax.ShapeDtypeStruct(q.shape, F32)] * 2,
        compiler_params=_cparams(("parallel", "arbitrary")),
    )(q, k, v, bm)


def _band_bwd(q, k, v, bm, do, dlse, dil, l_sub, bq, name):
    w = bq + 2 * BAND_HALF
    qs, ks, bs = _band_specs(l_sub, bq)

    def body(q_ref, k_ref, v_ref, bm_ref, do_ref, dl_ref, dq_ref, dk_ref, dv_ref, dbm_ref):
        r, i = pl.program_id(0), pl.program_id(1)
        st = pl.multiple_of(i * bq, bq)
        _, vjp = jax.vjp(functools.partial(_band_fn, l_sub, bq, i),
                         q_ref[...].astype(F32), k_ref[:, pl.ds(st, w), :].astype(F32),
                         v_ref[:, pl.ds(st, w), :].astype(F32), bm_ref[...])
        dq, dkw, dvw, dbm = vjp((do_ref[...].astype(F32), dl_ref[...]))
        dq_ref[...] = dq.astype(dq_ref.dtype)

        @pl.when(i == 0)
        def _():
            dk_ref[...] = jnp.zeros_like(dk_ref)
            dv_ref[...] = jnp.zeros_like(dv_ref)

        dk_ref[:, pl.ds(st, w), :] += dkw
        dv_ref[:, pl.ds(st, w), :] += dvw

        @pl.when((i == 0) & (r == 0))
        def _():
            dbm_ref[...] = dbm

        @pl.when((i > 0) | (r > 0))
        def _():
            dbm_ref[...] += dbm

    return pl.pallas_call(
        body, name=name, grid=(dil, l_sub // bq),
        in_specs=[qs, ks, ks, bs, qs, qs], out_specs=[qs, ks, ks, bs],
        out_shape=[jax.ShapeDtypeStruct(q.shape, BF16), jax.ShapeDtypeStruct(k.shape, F32),
                   jax.ShapeDtypeStruct(k.shape, F32), jax.ShapeDtypeStruct(bm.shape, F32)],
        compiler_params=_cparams(("arbitrary", "arbitrary")),
    )(q, k, v, bm, do, dlse)


def _t5_bucket(rel):
    half = REL_BUCKETS // 2
    max_exact = half // 2
    n = np.abs(rel)
    large = max_exact + (np.log(np.maximum(n, 1) / max_exact) / math.log(REL_MAX_DIST / max_exact)
                         * (half - max_exact)).astype(np.int64)
    large = np.minimum(large, half - 1)
    return ((rel > 0) * half + np.where(n < max_exact, n, large)).astype(np.int32)


def _bucket_onehot(dil):
    idx = _t5_bucket(np.arange(-BAND_HALF, BAND_HALF + 1) * dil)
    oh = np.zeros((2 * BAND_HALF + 1, REL_BUCKETS), np.float32)
    oh[np.arange(2 * BAND_HALF + 1), idx] = 1.0
    return oh


def _band_bias(rel_bias, gi, dil, bq):
    w = bq + 2 * BAND_HALF
    nb = 2 * BAND_HALF + 1
    bias = jnp.dot(jnp.asarray(_bucket_onehot(dil)), rel_bias[:, 4 * gi:4 * gi + 4], precision=HI)
    row = jnp.concatenate([bias.T, jnp.full((4, w + 1 - nb), NEG, F32)], axis=1)
    flat = jnp.tile(row, (1, bq))[:, :bq * w]
    return flat.reshape(4, bq, w)


def _relbias_grad(dbms, name):
    nb = 2 * BAND_HALF + 1
    bq = max(d.shape[1] for d in dbms)
    skew = []
    for dbm in dbms:
        bqg, w = dbm.shape[1], dbm.shape[2]
        flat = jnp.pad(dbm.reshape(4, bqg * w), ((0, 0), (0, bqg)))
        skew.append(jnp.pad(flat.reshape(4, bqg, w + 1)[:, :, :nb], ((0, 0), (0, bq - bqg), (0, 256 - nb))))
    sk = jnp.concatenate(skew, axis=0)
    oh = np.zeros((3, 256, 128), np.float32)
    for gi, (_, dil) in enumerate(DIL_GROUPS):
        oh[gi, :2 * BAND_HALF + 1, :REL_BUCKETS] = _bucket_onehot(dil)

    def body(s_ref, oh_ref, o_ref):
        col = jnp.sum(s_ref[...], axis=0, keepdims=True)
        o_ref[...] = jnp.dot(jnp.broadcast_to(col, (8, 256)), oh_ref[...], precision=HI, preferred_element_type=F32)

    out = pl.pallas_call(
        body, name=name, grid=(12,),
        in_specs=[pl.BlockSpec((None, bq, 256), lambda n: (n, 0, 0)),
                  pl.BlockSpec((None, 256, 128), lambda n: (n // 4, 0, 0))],
        out_specs=pl.BlockSpec((None, 8, 128), lambda n: (n, 0, 0)),
        out_shape=jax.ShapeDtypeStruct((12, 8, 128), F32),
        compiler_params=_cparams(("parallel",)),
    )(sk, jnp.asarray(oh))
    return out[:, 0, :REL_BUCKETS].T


def _mem_fn(q, k, v):
    s = lax.dot_general((q * (ATT_HD ** -0.5)).astype(BF16), k.astype(BF16), (((1,), (1,)), ((), ())),
                        preferred_element_type=F32)
    m = lax.stop_gradient(jnp.max(s, axis=-1, keepdims=True))
    p = jnp.exp(s - m)
    p = p / jnp.sum(p, axis=-1, keepdims=True)
    return jnp.dot(p.astype(BF16), v.astype(BF16), preferred_element_type=F32)


def _mem_specs(tb, ml):
    qs = pl.BlockSpec((None, tb, ATT_HD), lambda h, i: (h, i, 0))
    ks = pl.BlockSpec((None, ml, ATT_HD), lambda h, i: (h, 0, 0))
    return qs, ks


def _mem_fwd(q, k, v, tb, name):
    qs, ks = _mem_specs(tb, k.shape[1])

    def body(q_ref, k_ref, v_ref, o_ref):
        o_ref[...] = _mem_fn(q_ref[...].astype(F32), k_ref[...], v_ref[...])

    return pl.pallas_call(
        body, name=name, grid=(MEM_HEADS, q.shape[1] // tb),
        in_specs=[qs, ks, ks], out_specs=qs, out_shape=jax.ShapeDtypeStruct(q.shape, F32),
        compiler_params=_cparams(("parallel", "parallel")),
    )(q, k, v)


def _mem_bwd(q, k, v, do, tb, name):
    qs, ks = _mem_specs(tb, k.shape[1])

    def body(q_ref, k_ref, v_ref, do_ref, dq_ref, dk_ref, dv_ref):
        i = pl.program_id(1)
        _, vjp = jax.vjp(_mem_fn, q_ref[...].astype(F32), k_ref[...], v_ref[...])
        dq, dk, dv = vjp(do_ref[...])
        dq_ref[...] = dq

        @pl.when(i == 0)
        def _():
            dk_ref[...] = dk
            dv_ref[...] = dv

        @pl.when(i > 0)
        def _():
            dk_ref[...] += dk
            dv_ref[...] += dv

    return pl.pallas_call(
        body, name=name, grid=(MEM_HEADS, q.shape[1] // tb),
        in_specs=[qs, ks, ks, qs], out_specs=[qs, ks, ks],
        out_shape=[jax.ShapeDtypeStruct(q.shape, F32), jax.ShapeDtypeStruct(k.shape, F32),
                   jax.ShapeDtypeStruct(k.shape, F32)],
        compiler_params=_cparams(("parallel", "arbitrary")),
    )(q, k, v, do)


CONV_PAD = 8


def _conv_post(kind, acc):
    s = _silu(acc)
    if kind == 2:
        return s
    scale = DN_HD ** -0.5 if kind == 0 else 1.0
    return s * lax.rsqrt(jnp.sum(s * s, axis=-1, keepdims=True) + EPS) * scale


def _conv_rows(x_ref, t, start, rt):
    lo = min(max(start, 0), t - rt)
    x = x_ref[pl.ds(lo, rt), :]
    shift = lo - start
    if shift == 0:
        return x
    x = pltpu.roll(x, shift % rt, axis=0)
    row = lax.broadcasted_iota(jnp.int32, x.shape, 0)
    return jnp.where((row >= shift) if shift > 0 else (row < rt + shift), x, 0.0)


def _conv_acc(x_ref, t, w, r0, rt):
    acc = None
    for i in range(DN_CONV):
        term = w[i:i + 1, :] * _conv_rows(x_ref, t, r0 + i - DN_CONV // 2, rt)
        acc = term if acc is None else acc + term
    return acc


def _conv_fwd(x, w8, kind, rt, name):
    t = x.shape[0]

    def body(x_ref, w_ref, o_ref):
        w = w_ref[...]
        for r in range(t // rt):
            o_ref[pl.ds(r * rt, rt), :] = _conv_post(kind, _conv_acc(x_ref, t, w, r * rt, rt))

    return pl.pallas_call(
        body, name=name, grid=(DN_HEADS,),
        in_specs=[pl.BlockSpec((t, DN_HD), lambda j: (0, 6 * kind + j)),
                  pl.BlockSpec((8, DN_HD), lambda j: (0, 6 * kind + j))],
        out_specs=pl.BlockSpec((t, DN_HD), lambda j: (0, j)),
        out_shape=jax.ShapeDtypeStruct((t, TOK_W), F32),
        compiler_params=_cparams(("parallel",)),
    )(x, w8)


def _conv_bwd(x, w8, d_f, d_r, kind, rt, name):
    t = x.shape[0]

    def body(xp_ref, w_ref, df_ref, dr_ref, dx_ref, dw_ref, dpad_ref):
        w = w_ref[...]
        zero = jnp.zeros((CONV_PAD, DN_HD), F32)
        dpad_ref[pl.ds(0, CONV_PAD), :] = zero
        dpad_ref[pl.ds(CONV_PAD + t, CONV_PAD), :] = zero
        dw = [jnp.zeros((1, DN_HD), F32) for _ in range(DN_CONV)]
        for r in range(t // rt):
            rows = pl.ds(r * rt, rt)
            acc = _conv_acc(xp_ref, t, w, r * rt, rt)
            _, vjp = jax.vjp(functools.partial(_conv_post, kind), acc)
            (dacc,) = vjp(df_ref[rows, :] + dr_ref[rows, :])
            dpad_ref[pl.ds(CONV_PAD + r * rt, rt), :] = dacc
            for i in range(DN_CONV):
                xs = _conv_rows(xp_ref, t, r * rt + i - DN_CONV // 2, rt)
                dw[i] = dw[i] + jnp.sum(dacc * xs, axis=0, keepdims=True)
        dw_ref[...] = jnp.concatenate(dw + [jnp.zeros((8 - DN_CONV, DN_HD), F32)], axis=0)
        for r in range(t // rt):
            acc = None
            for i in range(DN_CONV):
                term = w[i:i + 1, :] * dpad_ref[pl.ds(CONV_PAD + r * rt - i + DN_CONV // 2, rt), :]
                acc = term if acc is None else acc + term
            dx_ref[pl.ds(r * rt, rt), :] = acc.astype(dx_ref.dtype)

    return pl.pallas_call(
        body, name=name, grid=(DN_HEADS,),
        in_specs=[pl.BlockSpec((t, DN_HD), lambda j: (0, 6 * kind + j)),
                  pl.BlockSpec((8, DN_HD), lambda j: (0, 6 * kind + j)),
                  pl.BlockSpec((t, DN_HD), lambda j: (0, j)),
                  pl.BlockSpec((t, DN_HD), lambda j: (0, j))],
        out_specs=[pl.BlockSpec((t, DN_HD), lambda j: (0, j)), pl.BlockSpec((8, DN_HD), lambda j: (0, j))],
        out_shape=[jax.ShapeDtypeStruct((t, TOK_W), BF16), jax.ShapeDtypeStruct((8, TOK_W), F32)],
        scratch_shapes=[pltpu.VMEM((t + 2 * CONV_PAD, DN_HD), F32)],
        compiler_params=_cparams(("parallel",)),
    )(x, w8, d_f, d_r)


def _softplus(x):
    e = jnp.exp(-jnp.abs(x))
    return jnp.maximum(x, 0.0) + jnp.where(e < 1e-4, e - 0.5 * e * e, jnp.log(1.0 + e))


_NN = (((2,), (1,)), ((0,), (0,)))
_NT = (((2,), (2,)), ((0,), (0,)))
_TN = (((1,), (1,)), ((0,), (0,)))


def _dot(a, b, dims=_NN):
    return lax.dot_general(a.astype(BF16), b.astype(BF16), dims, preferred_element_type=F32)


def _hi_lo(x):
    hi = x.astype(BF16)
    return hi, (x - hi.astype(F32)).astype(BF16)


def _mask_dot(mask_bf16, x, dims):
    x1 = x.astype(BF16)
    r = x - x1.astype(F32)
    x2, x3 = _hi_lo(r)
    d = functools.partial(lax.dot_general, dimension_numbers=dims, preferred_element_type=F32)
    return d(mask_bf16, x1) + d(mask_bf16, x2) + d(mask_bf16, x3)


@jax.custom_vjp
def _dot_mask(mask_bf16, x):
    return _mask_dot(mask_bf16, x, _NN)


def _dot_mask_fwd(mask_bf16, x):
    return _mask_dot(mask_bf16, x, _NN), mask_bf16


def _dot_mask_bwd(mask_bf16, ct):
    return jnp.zeros_like(mask_bf16), _mask_dot(mask_bf16, ct, _TN)


_dot_mask.defvjp(_dot_mask_fwd, _dot_mask_bwd)


def _dot3_raw(a, b, dims):
    a1, a2 = _hi_lo(a)
    b1, b2 = _hi_lo(b)
    d = functools.partial(lax.dot_general, dimension_numbers=dims, preferred_element_type=F32)
    return d(a1, b1) + d(a1, b2) + d(a2, b1)


def _unit_solve_pass(lmat, rhs, masks):
    ainv = masks[6] - lmat * masks[0]
    for sh in range(1, 6):
        ainv = ainv - _dot(_dot(ainv, lmat * masks[sh]), ainv)
    return _dot3_raw(ainv, rhs, _NN), ainv


@jax.custom_vjp
def _unit_solve(lmat, rhs, masks):
    return _unit_solve_pass(lmat, rhs, masks)[0]


def _unit_solve_fwd(lmat, rhs, masks):
    sol, ainv = _unit_solve_pass(lmat, rhs, masks)
    return sol, (sol, ainv, masks)


def _unit_solve_bwd(res, ct):
    sol, ainv, masks = res
    d_rhs = _dot3_raw(ainv, ct, _TN)
    return -_dot3_raw(d_rhs, sol, _NT), d_rhs, tuple(jnp.zeros_like(m) for m in masks)


_unit_solve.defvjp(_unit_solve_fwd, _unit_solve_bwd)


def _block_masks(rev, row, col):
    c = DN_CHUNK
    prow = jnp.where(rev, c - 1 - row, row)
    pcol = jnp.where(rev, c - 1 - col, col)
    masks = []
    for sh in range(6):
        differ = (prow ^ pcol) >> sh
        miss = (differ ^ 1) + (1 - ((prow >> sh) & 1))
        masks.append(jnp.where(miss == 0, 1.0, 0.0))
    masks.append(jnp.where(row == col, 1.0, 0.0))
    return tuple(masks)


def _dn_chunk(q, k, v, al, be, alc, a_row, dt_row, a_rowc, dt_rowc, s):
    n, c = q.shape[0], DN_CHUNK
    rev = lax.broadcasted_iota(jnp.int32, (n, c, c), 0) >= n // 2
    row = lax.broadcasted_iota(jnp.int32, (n, c, c), 1)
    col = lax.broadcasted_iota(jnp.int32, (n, c, c), 2)
    ahead = jnp.where(rev, col - row, row - col)
    incl = ahead >= 0
    strict = ahead > 0
    incl_b = incl.astype(BF16)

    g = -jnp.exp(a_row) * _softplus(al + dt_row)
    beta = _sigmoid(be)
    g_c = -jnp.exp(a_rowc) * _softplus(alc + dt_rowc)
    gc = _dot_mask(incl_b, g)
    gcc = _dot_mask(incl_b, g_c)
    decay = jnp.exp(jnp.where(incl, gcc - jnp.swapaxes(gcc, 1, 2), NEG))
    kb = k * beta
    lmat = jnp.where(strict, _dot(kb, k, _NT) * decay, 0.0)
    rhs = jnp.concatenate([v * beta, kb * jnp.exp(gc)], axis=2)
    sol = _unit_solve(lmat, rhs, _block_masks(rev, row, col))
    u, w = sol[:, :, :DN_HD], sol[:, :, DN_HD:]
    intra = jnp.where(incl, _dot(q, k, _NT) * decay, 0.0)
    v_new = u - _dot(w, s)
    out = _dot(q * jnp.exp(gc), s) + _dot(intra, v_new)
    g_last = jnp.sum(g, axis=1, keepdims=True)
    s_new = s * jnp.exp(g_last) + _dot(k * jnp.exp(g_last - gc), v_new, _TN)
    return out, s_new


DN_HG = 6


def _dn_load(f_refs, r_refs, alf, bef, alr, ber, a_ref, dt_ref):
    c, hg = DN_CHUNK, DN_HG
    sls = [slice(DN_HD * h, DN_HD * (h + 1)) for h in range(hg)]
    toks = [jnp.stack([f[:, sl] for sl in sls] + [r[:, sl] for sl in sls]) for f, r in zip(f_refs, r_refs)]
    al = jnp.concatenate([alf[...], alr[...]], axis=0)
    be = jnp.concatenate([bef[...], ber[...]], axis=0)
    alc = jnp.concatenate([alf[:, :, 0:c], alr[:, :, 0:c]], axis=0)
    a = jnp.concatenate([a_ref[0], a_ref[1]], axis=0)
    dt = jnp.concatenate([dt_ref[0], dt_ref[1]], axis=0)
    ac = jnp.concatenate([a_ref[0, :, :, 0:c], a_ref[1, :, :, 0:c]], axis=0)
    dtc = jnp.concatenate([dt_ref[0, :, :, 0:c], dt_ref[1, :, :, 0:c]], axis=0)
    return toks, (al, be, alc, a, dt, ac, dtc)


def _dn_views(nc, bwd):
    c, hg = DN_CHUNK, DN_HG
    if bwd:
        f_blk = lambda s: nc - 1 - s
        r_blk = lambda s: s
        st_blk = lambda s: nc - 1 - s
    else:
        f_blk = lambda s: s
        r_blk = lambda s: nc - 1 - s
        st_blk = lambda s: s
    tok_f = pl.BlockSpec((c, hg * DN_HD), lambda g, s: (f_blk(s), g))
    tok_r = pl.BlockSpec((c, hg * DN_HD), lambda g, s: (r_blk(s), g))
    gate_f = pl.BlockSpec((None, hg, c, DN_HD), lambda g, s: (0, g, f_blk(s), 0))
    gate_r = pl.BlockSpec((None, hg, c, DN_HD), lambda g, s: (1, g, r_blk(s), 0))
    par = pl.BlockSpec((2, hg, 1, DN_HD), lambda g, s: (0, g, 0, 0))
    state = pl.BlockSpec((2, hg, None, DN_HD, DN_HD), lambda g, s: (0, g, st_blk(s), 0, 0))
    return tok_f, tok_r, gate_f, gate_r, par, state


def _dn_fwd(q, k, v, al, be, a_rows, dt_rows, name):
    t = q.shape[0]
    c, hg = DN_CHUNK, DN_HG
    nc = t // c
    tok_f, tok_r, gate_f, gate_r, par, state = _dn_views(nc, False)

    def body(qf, kf, vf, qr, kr, vr, alf, bef, alr, ber, a_ref, dt_ref, of_ref, or_ref, st_ref, s_ref):
        @pl.when(pl.program_id(1) == 0)
        def _():
            s_ref[...] = jnp.zeros_like(s_ref)

        (q_, k_, v_), gates = _dn_load((qf, kf, vf), (qr, kr, vr), alf, bef, alr, ber, a_ref, dt_ref)
        s = s_ref[...]
        st_ref[0] = s[:hg]
        st_ref[1] = s[hg:]
        out, s_new = _dn_chunk(q_, k_, v_, *gates, s)
        for h in range(hg):
            sl = slice(DN_HD * h, DN_HD * (h + 1))
            of_ref[:, sl] = out[h]
            or_ref[:, sl] = out[hg + h]
        s_ref[...] = s_new

    return pl.pallas_call(
        body, name=name, grid=(DN_HEADS // hg, nc),
        in_specs=[tok_f] * 3 + [tok_r] * 3 + [gate_f, gate_f, gate_r, gate_r, par, par],
        out_specs=[tok_f, tok_r, state],
        out_shape=[jax.ShapeDtypeStruct((t, TOK_W), F32)] * 2
        + [jax.ShapeDtypeStruct((2, DN_HEADS, nc, DN_HD, DN_HD), F32)],
        scratch_shapes=[pltpu.VMEM((2 * hg, DN_HD, DN_HD), F32)],
        compiler_params=_cparams(("parallel", "arbitrary")),
    )(q, k, v, q, k, v, al, be, al, be, a_rows, dt_rows)


def _dn_bwd(q, k, v, al, be, a_rows, dt_rows, states, do, name):
    t = q.shape[0]
    c, hg = DN_CHUNK, DN_HG
    assert hg == DN_HEADS
    nc = t // c
    tok_f, tok_r, gate_f, gate_r, par, state = _dn_views(nc, True)
    gout_f = pl.BlockSpec((c, DN_HD), lambda g, s: (nc - 1 - s, 0))
    gout_r = pl.BlockSpec((c, DN_HD), lambda g, s: (s, 0))

    def body(qf, kf, vf, qr, kr, vr, alf, bef, alr, ber, a_ref, dt_ref, st_ref, dof, dor,
             dqf, dkf, dvf, dqr, dkr, dvr, dgf, dgr, da_ref, ddt_ref, ds_ref):
        first = pl.program_id(1) == 0

        @pl.when(first)
        def _():
            ds_ref[...] = jnp.zeros_like(ds_ref)
            da_ref[...] = jnp.zeros_like(da_ref)
            ddt_ref[...] = jnp.zeros_like(ddt_ref)

        def lanes(x):
            return jnp.sum(x, axis=-1, keepdims=True)

        (q_, k_, v_, do_), gates = _dn_load((qf, kf, vf, dof), (qr, kr, vr, dor), alf, bef, alr, ber, a_ref, dt_ref)
        s = jnp.concatenate([st_ref[0], st_ref[1]], axis=0)
        _, vjp = jax.vjp(_dn_chunk, q_, k_, v_, *gates, s)
        dq, dk, dv, dal, dbe, dalc, da, ddt, dac, ddtc, ds = vjp((do_, ds_ref[...]))
        for h in range(hg):
            sl = slice(DN_HD * h, DN_HD * (h + 1))
            dqf[:, sl], dkf[:, sl], dvf[:, sl] = dq[h], dk[h], dv[h]
            dqr[:, sl], dkr[:, sl], dvr[:, sl] = dq[hg + h], dk[hg + h], dv[hg + h]
        dal, dbe = lanes(dal) + lanes(dalc), lanes(dbe)
        lane = lax.broadcasted_iota(jnp.int32, (c, DN_HD), 1)
        for d, dg_ref in enumerate((dgf, dgr)):
            dg = jnp.zeros((c, DN_HD), F32)
            for h in range(hg):
                dg = jnp.where(lane == h, dal[d * hg + h], jnp.where(lane == hg + h, dbe[d * hg + h], dg))
            dg_ref[...] = dg
        da = jnp.broadcast_to(lanes(da) + lanes(dac), da.shape)
        ddt = jnp.broadcast_to(lanes(ddt) + lanes(ddtc), ddt.shape)
        da_ref[0] += da[:hg]
        da_ref[1] += da[hg:]
        ddt_ref[0] += ddt[:hg]
        ddt_ref[1] += ddt[hg:]
        ds_ref[...] = ds

    tok = jax.ShapeDtypeStruct((t, TOK_W), F32)
    gate = jax.ShapeDtypeStruct((t, DN_HD), F32)
    parsh = jax.ShapeDtypeStruct((2, DN_HEADS, 1, DN_HD), F32)
    res = pl.pallas_call(
        body, name=name, grid=(DN_HEADS // hg, nc),
        in_specs=[tok_f] * 3 + [tok_r] * 3 + [gate_f, gate_f, gate_r, gate_r, par, par, state, tok_f, tok_r],
        out_specs=[tok_f] * 3 + [tok_r] * 3 + [gout_f, gout_r, par, par],
        out_shape=[tok] * 6 + [gate] * 2 + [parsh] * 2,
        scratch_shapes=[pltpu.VMEM((2 * hg, DN_HD, DN_HD), F32)],
        compiler_params=_cparams(("parallel", "arbitrary")),
    )(q, k, v, q, k, v, al, be, al, be, a_rows, dt_rows, states, do, do)
    dqf, dkf, dvf, dqr, dkr, dvr, dgf, dgr, da, ddt = res
    dgate = jnp.concatenate([dgf[:, :2 * DN_HEADS], dgr[:, :2 * DN_HEADS]], axis=1)
    return (dqf, dkf, dvf), (dqr, dkr, dvr), dgate, da, ddt


BAND_BQ = 256
ROW_TB = 256
MEM_TB = 512
CONV_RT = 512


def _to_sub(x, dil):
    l = x.shape[0] // dil
    return x.reshape(l, dil, 4, ATT_HD).transpose(1, 2, 0, 3)


def _from_sub(x, dil):
    l = x.shape[2]
    return x.transpose(2, 0, 1, 3).reshape(l * dil, 4 * ATT_HD)


def _heads_major(x):
    return x.reshape(x.shape[0], MEM_HEADS, ATT_HD).transpose(1, 0, 2)


def _heads_minor(x):
    return x.transpose(1, 0, 2).reshape(x.shape[1], MEM_HEADS * ATT_HD)


def _mem_kv_fwd(mem, gain, w_kv, li):
    (memn,) = _rowwise(_fn_pre, [mem], [gain], [(D, BF16)], mem.shape[0], f"memnorm_fwd{li}")
    kv = _matmul(memn, w_kv, "nn", F32, f"memkv_fwd{li}")
    return _heads_major(kv[:, :MEM_W]), _heads_major(kv[:, MEM_W:]), memn


def _mem_kv_bwd(mem, gain, w_kv, memn, dkm, dvm, li):
    dkv = jnp.concatenate([_heads_minor(dkm), _heads_minor(dvm)], axis=1).astype(BF16)
    dw = _matmul(memn, dkv, "tn", BF16, f"memkv_dw{li}")
    dmemn = _matmul(dkv, w_kv, "nt", F32, f"memkv_dx{li}")
    _, (dgain,) = _rowwise_bwd(_fn_pre, [mem], [gain], [dmemn], [None], mem.shape[0], f"memnorm_bwd{li}")
    return dw, dgain


def _attn_mixer_fwd(p, rel_bias, kv_fn):
    t = p.shape[0]
    saved, outs, lses = [], [], []
    for gi, (_, dil) in enumerate(DIL_GROUPS):
        l_sub = t // dil
        bq = min(BAND_BQ, l_sub)
        q = _to_sub(p[:, 256 * gi:256 * (gi + 1)], dil)
        pad = ((0, 0), (0, 0), (BAND_HALF, BAND_HALF), (0, 0))
        k = jnp.pad(_to_sub(p[:, TOK_W + 256 * gi:TOK_W + 256 * (gi + 1)], dil), pad)
        v = jnp.pad(_to_sub(p[:, 2 * TOK_W + 256 * gi:2 * TOK_W + 256 * (gi + 1)], dil), pad)
        bm = _band_bias(rel_bias, gi, dil, bq)
        o, lse = _band_fwd(q, k, v, bm, dil, l_sub, bq, f"band_fwd{gi}")
        outs.append(_from_sub(o, dil))
        lses.append(_from_sub(lse, dil))
        saved.append((q, k, v, bm))
    o_all = jnp.concatenate(outs, axis=1)
    lse_all = jnp.concatenate(lses, axis=1)
    (mixed,) = _rowwise(_fn_combine, [o_all, lse_all], [], [(TOK_W, BF16)], ROW_TB, "combine_fwd")
    qm = _heads_major(p[:, 3 * TOK_W:])
    km, vm, memn = kv_fn(mixed)
    memo = _mem_fwd(qm, km, vm, min(MEM_TB, t), "mem_fwd0")
    cat = jnp.concatenate([mixed, _heads_minor(memo).astype(BF16)], axis=1)
    return cat, (saved, o_all, lse_all, qm), (km, vm, memn)


def _attn_mixer_bwd(dcat, res, km, vm):
    saved, o_all, lse_all, qm = res
    t = dcat.shape[0]
    (do_all, dlse_all), _ = _rowwise_bwd(_fn_combine, [o_all, lse_all], [], [dcat[:, :TOK_W]], [BF16, F32],
                                         ROW_TB, "combine_bwd")
    dqs, dks, dvs, dbms = [], [], [], []
    for gi, (_, dil) in enumerate(DIL_GROUPS):
        l_sub = t // dil
        bq = min(BAND_BQ, l_sub)
        q, k, v, bm = saved[gi]
        do = _to_sub(do_all[:, 256 * gi:256 * (gi + 1)], dil)
        dl = _to_sub(dlse_all[:, 256 * gi:256 * (gi + 1)], dil)
        dq, dk, dv, dbm = _band_bwd(q, k, v, bm, do, dl, dil, l_sub, bq, f"band_bwd{gi}")
        dqs.append(_from_sub(dq, dil))
        dks.append(_from_sub(dk[:, :, BAND_HALF:-BAND_HALF], dil))
        dvs.append(_from_sub(dv[:, :, BAND_HALF:-BAND_HALF], dil))
        dbms.append(dbm)
    dqm, dkm, dvm = _mem_bwd(qm, km, vm, _heads_major(dcat[:, TOK_W:]), min(MEM_TB, t), "mem_bwd0")
    dp = jnp.concatenate([d.astype(BF16) for d in dqs + dks + dvs + [_heads_minor(dqm)]], axis=1)
    return dp, _relbias_grad(dbms, "relbias_grad"), dkm, dvm


def _dn_mixer_fwd(p, conv_w, a_log, dt_bias, out_norm, km, vm):
    t = p.shape[0]
    rt = min(CONV_RT, t)
    xp = p
    w8 = jnp.pad(conv_w, ((0, 8 - DN_CONV), (0, 0)))
    q = _conv_fwd(xp, w8, 0, rt, "conv_fwd_q")
    k = _conv_fwd(xp, w8, 1, rt, "conv_fwd_k")
    v = _conv_fwd(xp, w8, 2, rt, "conv_fwd_v")
    gate = p[:, 4 * TOK_W:4 * TOK_W + 4 * DN_HEADS].reshape(t, 2, 2, DN_HEADS)
    bshape = (2, DN_HEADS, t, DN_HD)
    al = jnp.broadcast_to(gate[:, :, 0, :].transpose(1, 2, 0)[..., None], bshape)
    be = jnp.broadcast_to(gate[:, :, 1, :].transpose(1, 2, 0)[..., None], bshape)
    a_rows = jnp.broadcast_to(a_log[:, :, None, None], (2, DN_HEADS, 1, DN_HD))
    dt_rows = jnp.broadcast_to(dt_bias[:, :, None, None], (2, DN_HEADS, 1, DN_HD))
    o_f, o_r, states = _dn_fwd(q, k, v, al, be, a_rows, dt_rows, "dn_fwd")
    z = p[:, 3 * TOK_W:4 * TOK_W]
    gain = out_norm.reshape(1, DN_HD)
    (og,) = _rowwise(_fn_outnorm, [o_f, o_r, z], [gain], [(TOK_W, BF16)], ROW_TB, "outnorm_fwd")
    qm = _heads_major(p[:, 4 * TOK_W + 4 * DN_HEADS:DN_IN])
    memo = _mem_fwd(qm, km, vm, min(MEM_TB, t), "mem_fwd1")
    cat = jnp.concatenate([og, _heads_minor(memo).astype(BF16)], axis=1)
    return cat, (xp, w8, q, k, v, al, be, a_rows, dt_rows, o_f, o_r, states, z, gain, qm)


def _dn_mixer_bwd(dcat, res, km, vm):
    xp, w8, q, k, v, al, be, a_rows, dt_rows, o_f, o_r, states, z, gain, qm = res
    t = dcat.shape[0]
    rt = min(CONV_RT, t)
    (do, dz), (dgain,) = _rowwise_bwd(_fn_outnorm, [o_f, o_r, z], [gain], [dcat[:, :TOK_W]], [F32, None, BF16],
                                      ROW_TB, "outnorm_bwd")
    d_f, d_r, dgate, da, ddt = _dn_bwd(q, k, v, al, be, a_rows, dt_rows, states, do, "dn_bwd")
    dxs, dws = [], []
    for kind, nm in enumerate("qkv"):
        dx, dw = _conv_bwd(xp, w8, d_f[kind], d_r[kind], kind, rt, f"conv_bwd_{nm}")
        dxs.append(dx)
        dws.append(dw)
    dconv = jnp.concatenate(dws, axis=1)[:DN_CONV]
    dqm, dkm, dvm = _mem_bwd(qm, km, vm, _heads_major(dcat[:, TOK_W:]), min(MEM_TB, t), "mem_bwd1")
    dp = jnp.concatenate(dxs + [dz, dgate.astype(BF16), _heads_minor(dqm).astype(BF16),
                               jnp.zeros((t, DN_IN_PAD - DN_IN), BF16)], axis=1)
    return dp, dconv, da[:, :, 0, 0], ddt[:, :, 0, 0], dgain.reshape(DN_HD), dkm, dvm


SWI_TB = 256


def _ffn_fwd(h, w_gu_t, w_d, li):
    gu = _matmul(h, w_gu_t, "nt", BF16, f"ffn_gu{li}")
    (a,) = _rowwise(_fn_swiglu, [gu], [], [(D_FF, BF16)], SWI_TB, f"swiglu_fwd{li}")
    return _matmul(a, w_d, "nn", F32, f"ffn_down{li}"), gu, a


def _ffn_bwd(df, h, w_gu_t, w_d, gu, a, li):
    da = _matmul(df, w_d, "nt", BF16, f"ffn_down_dx{li}")
    dwd = _matmul(a, df, "tn", BF16, f"ffn_down_dw{li}")
    (dgu,), _ = _rowwise_bwd(_fn_swiglu, [gu], [], [da], [BF16], SWI_TB, f"swiglu_bwd{li}")
    dh = _matmul(dgu, w_gu_t, "nn", F32, f"ffn_gu_dx{li}")
    dwgu_t = _matmul(dgu, h, "tn", BF16, f"ffn_gu_dw{li}")
    return dh, dwgu_t, dwd


def _fn_first(x, g):
    return x, _rms(x, g)


def _me_xyc():
    return lax.axis_index("x"), lax.axis_index("y"), lax.axis_index("c")


def _flip(coords, k):
    x, y, c = coords
    return (1 - x if k & 4 else x, 1 - y if k & 2 else y, 1 - c if k & 1 else c)


def _index(coords):
    x, y, c = coords
    return 4 * x + 2 * y + c


def _window(ref, axis, size, d):
    idx = [slice(None)] * len(ref.shape)
    idx[axis] = pl.ds(pl.multiple_of(d * size, size), size)
    return ref.at[tuple(idx)]


def _comm_call(body, n, ins, out_shapes, name):
    hbm = pl.BlockSpec(memory_space=pl.ANY)
    return pl.pallas_call(
        body, name=name, in_specs=[hbm] * n, out_specs=[hbm] * n, out_shape=out_shapes,
        scratch_shapes=[pltpu.SemaphoreType.DMA((N_DEV - 1, n)), pltpu.SemaphoreType.DMA((N_DEV - 1, n)),
                        pltpu.SemaphoreType.DMA((n,))],
    )(*ins)


def _run_exchange(n, local, remote, send_sems, recv_sems):
    me = _me_xyc()
    locs = [local(p) for p in range(n)]
    for cp in locs:
        cp.start()
    sends = [remote(k, p, me, _flip(me, k)) for k in range(1, N_DEV) for p in range(n)]
    for cp in sends:
        cp.start()
    for k in range(1, N_DEV):
        for p in range(n):
            remote(k, p, _flip(me, k), me).wait_recv()
    for cp in sends:
        cp.wait_send()
    for cp in locs:
        cp.wait()


def _all_gather(shards, axes, name):
    n = len(shards)
    sizes = [s.shape[a] for s, a in zip(shards, axes)]

    def body(*refs):
        ins, outs = refs[:n], refs[n:2 * n]
        send_sems, recv_sems, loc_sems = refs[2 * n:]
        me = _me_xyc()

        def local(p):
            return pltpu.make_async_copy(ins[p], _window(outs[p], axes[p], sizes[p], _index(me)), loc_sems.at[p])

        def remote(k, p, owner, to):
            return pltpu.make_async_remote_copy(
                src_ref=ins[p], dst_ref=_window(outs[p], axes[p], sizes[p], _index(owner)),
                send_sem=send_sems.at[k - 1, p], recv_sem=recv_sems.at[k - 1, p], device_id=to, device_id_type=MESH)

        _run_exchange(n, local, remote, send_sems, recv_sems)

    def full(s, a):
        return s.shape[:a] + (N_DEV * s.shape[a],) + s.shape[a + 1:]

    return _comm_call(body, n, shards, [jax.ShapeDtypeStruct(full(s, a), s.dtype) for s, a in zip(shards, axes)], name)


def _exchange(fulls, axes, name):
    n = len(fulls)
    sizes = [None if a is None else f.shape[a] // N_DEV for f, a in zip(fulls, axes)]

    def part_shape(f, a):
        return f.shape if a is None else f.shape[:a] + (f.shape[a] // N_DEV,) + f.shape[a + 1:]

    def body(*refs):
        ins, outs = refs[:n], refs[n:2 * n]
        send_sems, recv_sems, loc_sems = refs[2 * n:]
        me = _me_xyc()

        def src(p, to):
            return ins[p] if axes[p] is None else _window(ins[p], axes[p], sizes[p], _index(to))

        def local(p):
            return pltpu.make_async_copy(src(p, me), outs[p].at[_index(me)], loc_sems.at[p])

        def remote(k, p, sender, to):
            return pltpu.make_async_remote_copy(
                src_ref=src(p, to), dst_ref=outs[p].at[_index(sender)],
                send_sem=send_sems.at[k - 1, p], recv_sem=recv_sems.at[k - 1, p], device_id=to, device_id_type=MESH)

        _run_exchange(n, local, remote, send_sems, recv_sems)

    return _comm_call(body, n, fulls,
                      [jax.ShapeDtypeStruct((N_DEV,) + part_shape(f, a), f.dtype) for f, a in zip(fulls, axes)], name)


_HBM = pl.BlockSpec(memory_space=pltpu.HBM)
_SEM = pl.BlockSpec(memory_space=pltpu.SEMAPHORE)
_EFFECT = pltpu.SideEffectType.DATAFLOW_SIDE_EFFECTING


def _in_hbm(a):
    return pltpu.with_memory_space_constraint(a, pltpu.HBM)


def _split_start(srcs, lands, after, descr, name):
    n = len(srcs)

    def body(*refs):
        ins, lnd = refs[:n], refs[n:2 * n]
        send_sems, recv_sems = refs[2 * n + 1], refs[2 * n + 2]
        token = refs[-1]
        me = _me_xyc()
        for k in range(1, N_DEV):
            for p in range(n):
                descr(k, p, ins, lnd, send_sems, recv_sems, me, _flip(me, k)).start()
        token[...] = jnp.zeros_like(token)

    sems = pltpu.SemaphoreType.DMA(((N_DEV - 1) * n,))
    res = pl.pallas_call(
        body, name=name,
        out_shape=(sems, sems, *[pltpu.HBM(a.shape, a.dtype) for a in (*srcs, *lands)],
                   jax.ShapeDtypeStruct((8, 128), F32)),
        in_specs=[_HBM] * (2 * n) + [pl.BlockSpec(memory_space=pl.ANY)],
        out_specs=(_SEM, _SEM, *[_HBM] * (2 * n), pl.BlockSpec(memory_space=pltpu.VMEM)),
        input_output_aliases={i: 2 + i for i in range(2 * n)},
        compiler_params=pltpu.CompilerParams(has_side_effects=_EFFECT),
    )(*[_in_hbm(a) for a in (*srcs, *lands)], after)
    return res[0], res[1], res[2:2 + n], res[2 + n:2 + 2 * n], res[-1]


def _split_wait(send_sems, recv_sems, srcs, lands, after, descr, name):
    n = len(srcs)

    def body(*refs):
        ins, lnd = refs[:n], refs[n:2 * n]
        s_sems, r_sems = refs[2 * n], refs[2 * n + 1]
        me = _me_xyc()
        for k in range(1, N_DEV):
            for p in range(n):
                peer = _flip(me, k)
                descr(k, p, ins, lnd, s_sems, r_sems, me, peer).wait_send()
                descr(k, p, ins, lnd, s_sems, r_sems, peer, me).wait_recv()

    res = pl.pallas_call(
        body, name=name,
        out_shape=tuple(pltpu.HBM(a.shape, a.dtype) for a in (*srcs, *lands)),
        in_specs=[_HBM] * (2 * n) + [_SEM, _SEM, pl.BlockSpec(memory_space=pl.ANY)],
        out_specs=tuple([_HBM] * (2 * n)),
        input_output_aliases={i: i for i in range(2 * n)},
        compiler_params=pltpu.CompilerParams(has_side_effects=_EFFECT),
    )(*srcs, *lands, send_sems, recv_sems, after)
    return list(res[n:])


def _gather_descr(axes, sizes):
    def descr(k, p, ins, lnd, send_sems, recv_sems, sender, dest):
        return pltpu.make_async_remote_copy(
            src_ref=ins[p], dst_ref=_window(lnd[p], axes[p], sizes[p], _index(sender)),
            send_sem=send_sems.at[(k - 1) * len(axes) + p], recv_sem=recv_sems.at[(k - 1) * len(axes) + p],
            device_id=dest, device_id_type=MESH)
    return descr


def _exchange_descr(axes, sizes):
    def descr(k, p, ins, lnd, send_sems, recv_sems, sender, dest):
        return pltpu.make_async_remote_copy(
            src_ref=_window(ins[p], axes[p], sizes[p], _index(dest)), dst_ref=lnd[p].at[_index(sender)],
            send_sem=send_sems.at[(k - 1) * len(axes) + p], recv_sem=recv_sems.at[(k - 1) * len(axes) + p],
            device_id=dest, device_id_type=MESH)
    return descr


def _gather_begin(shards, axes, after, name):
    sizes = [s.shape[a] for s, a in zip(shards, axes)]
    me = _index(_me_xyc())
    lands = []
    for s, a, sz in zip(shards, axes, sizes):
        full = s.shape[:a] + (N_DEV * sz,) + s.shape[a + 1:]
        lands.append(lax.dynamic_update_slice_in_dim(lax.empty(full, s.dtype), s, me * sz, a))
    descr = _gather_descr(axes, sizes)
    send_sems, recv_sems, srcs, lands, token = _split_start(shards, lands, after, descr, name)
    return (send_sems, recv_sems, srcs, lands, descr), token


def _exchange_begin(fulls, axes, after, name):
    sizes = [f.shape[a] // N_DEV for f, a in zip(fulls, axes)]
    me = _index(_me_xyc())
    lands = []
    for f, a, sz in zip(fulls, axes, sizes):
        own = lax.dynamic_slice_in_dim(f, me * sz, sz, a)
        lands.append(lax.dynamic_update_slice_in_dim(lax.empty((N_DEV,) + own.shape, f.dtype), own[None], me, 0))
    descr = _exchange_descr(axes, sizes)
    send_sems, recv_sems, srcs, lands, token = _split_start(fulls, lands, after, descr, name)
    return (send_sems, recv_sems, srcs, lands, descr), token


def _split_end(handle, after, name):
    send_sems, recv_sems, srcs, lands, descr = handle
    return _split_wait(send_sems, recv_sems, srcs, lands, after, descr, name)


def _adam_math(g, w, m, v):
    m = ADAM_B1 * m + (1.0 - ADAM_B1) * g
    v = ADAM_B2 * v + (1.0 - ADAM_B2) * (g * g)
    m_hat = m / (1.0 - ADAM_B1 ** ADAM_STEP)
    v_hat = v / (1.0 - ADAM_B2 ** ADAM_STEP)
    delta = -ADAM_LR * (m_hat / (jnp.sqrt(v_hat) + ADAM_EPS) + ADAM_WD * w)
    return delta, m, v


def _sum_slabs(r_ref):
    g = r_ref[0].astype(F32)
    for s in range(1, N_DEV):
        g = g + r_ref[s].astype(F32)
    return g


def _adamw_reduce(recv, w, m, v, tb, name):
    r, c = w.shape

    def body(r_ref, w_ref, m_ref, v_ref, g_ref, d_ref, nm_ref, nv_ref):
        g = _sum_slabs(r_ref)
        g_ref[...] = g
        d_ref[...], nm_ref[...], nv_ref[...] = _adam_math(g, w_ref[...], m_ref[...], v_ref[...])

    blk = pl.BlockSpec((tb, c), lambda i: (i, 0))
    return pl.pallas_call(
        body, name=name, grid=(r // tb,),
        in_specs=[pl.BlockSpec((N_DEV, tb, c), lambda i: (0, i, 0)), blk, blk, blk],
        out_specs=[blk] * 4, out_shape=[jax.ShapeDtypeStruct((r, c), F32)] * 4,
        compiler_params=_cparams(("parallel",)),
    )(recv, w, m, v)


def _reduce8(recv, tb, name):
    r, c = recv.shape[1:]

    def body(r_ref, g_ref):
        g_ref[...] = _sum_slabs(r_ref)

    return pl.pallas_call(
        body, name=name, grid=(r // tb,),
        in_specs=[pl.BlockSpec((N_DEV, tb, c), lambda i: (0, i, 0))],
        out_specs=pl.BlockSpec((tb, c), lambda i: (i, 0)), out_shape=jax.ShapeDtypeStruct((r, c), F32),
        compiler_params=_cparams(("parallel",)),
    )(recv)


def _adamw(g, w, m, v, tb, name):
    r, c = w.shape

    def body(g_ref, w_ref, m_ref, v_ref, d_ref, nm_ref, nv_ref):
        d_ref[...], nm_ref[...], nv_ref[...] = _adam_math(g_ref[...], w_ref[...], m_ref[...], v_ref[...])

    blk = pl.BlockSpec((tb, c), lambda i: (i, 0))
    return pl.pallas_call(
        body, name=name, grid=(r // tb,), in_specs=[blk] * 4, out_specs=[blk] * 3,
        out_shape=[jax.ShapeDtypeStruct((r, c), F32)] * 3, compiler_params=_cparams(("parallel",)),
    )(g, w, m, v)


DN_IN_SHARD = DN_IN // N_DEV
DN_IN_SHARD_PAD = 432
CONV_SHARD = (1, DN_CONV, 288)


def _pack_small(arrs, rows):
    flat = jnp.concatenate([a.astype(F32).reshape(-1) for a in arrs])
    return jnp.pad(flat, (0, rows * PACK_C - flat.size)).reshape(rows, PACK_C)


def _unpack_small(packed, shapes):
    flat, out, off = packed.reshape(-1), [], 0
    for shp in shapes:
        n = int(np.prod(shp))
        out.append(flat[off:off + n].reshape(shp))
        off += n
    return out


def kernel(x, mem, rel_bias, att_w_in, att_w_out, dn_w_in, dn_conv, dn_a_log, dn_dt_bias, dn_out_norm, dn_w_out, mem_norm, mem_w_kv, norm_mix_pre, norm_mix_post, norm_ffn_pre, norm_ffn_post, ffn_w_gate_up, ffn_w_down, loss_target, m_rel_bias, m_att_w_in, m_att_w_out, m_dn_w_in, m_dn_conv, m_dn_a_log, m_dn_dt_bias, m_dn_out_norm, m_dn_w_out, m_mem_norm, m_mem_w_kv, m_norm_mix_pre, m_norm_mix_post, m_norm_ffn_pre, m_norm_ffn_post, m_ffn_w_gate_up, m_ffn_w_down, v_rel_bias, v_att_w_in, v_att_w_out, v_dn_w_in, v_dn_conv, v_dn_a_log, v_dn_dt_bias, v_dn_out_norm, v_dn_w_out, v_mem_norm, v_mem_w_kv, v_norm_mix_pre, v_norm_mix_post, v_norm_ffn_pre, v_norm_ffn_post, v_ffn_w_gate_up, v_ffn_w_down):
    x0, mem0, tgt = x[0], mem[0], loss_target[0]
    t = x0.shape[0]
    axes = ("x", "y", "c")

    def t_shard(w):
        return jnp.swapaxes(w, 1, 2).astype(BF16)

    dn_in_pad = ((0, 0), (0, DN_IN_SHARD_PAD - DN_IN_SHARD), (0, 0))
    (w_att_in_t,) = _all_gather([t_shard(att_w_in)], [1], "allgather_first")
    w_att_in_t = w_att_in_t[0]
    gu_t, down = t_shard(ffn_w_gate_up), ffn_w_down.astype(BF16)
    gather_o, tok_o = _gather_begin([att_w_out.astype(BF16), mem_w_kv.astype(BF16)], [1, 1], w_att_in_t,
                                    "gather_att_out_start")
    gather_a, tok_a = _gather_begin([gu_t[0:1], down[0:1]], [1, 1], tok_o, "gather_ffn0_start")
    gather_b, tok_b = _gather_begin(
        [jnp.pad(t_shard(dn_w_in), dn_in_pad), dn_w_out.astype(BF16), gu_t[1:2], down[1:2], dn_conv],
        [1, 1, 1, 1, 0], tok_a, "gather_layer1_start")

    def gain(a, i):
        return a[i].reshape(1, D)

    (h0,) = _rowwise(_fn_pre, [x0], [gain(norm_mix_pre, 0) + tok_b[0:1, 0:1]], [(D, BF16)], ROW_TB, "pre0")
    p0 = _matmul(h0, w_att_in_t, "nt", BF16, "att_in")
    late = {}

    def kv0(after):
        late["w_att_out"], late["w_kv"] = _split_end(gather_o, after, "gather_att_out_wait")
        return _mem_kv_fwd(mem0, gain(mem_norm, 0), late["w_kv"][0], 0)

    cat0, res0, (km0, vm0, memn0) = _attn_mixer_fwd(p0, rel_bias, kv0)
    w_att_out, w_kv = late["w_att_out"][0], late["w_kv"]
    y0 = _matmul(cat0, w_att_out, "nn", F32, "att_out")
    g_a = [gain(norm_mix_post, 0), gain(norm_ffn_pre, 0)]
    x1, h1 = _rowwise(_fn_res_pre, [x0, y0], g_a, [(D, F32), (D, BF16)], ROW_TB, "res_pre0")
    w_gu_t0, w_down0 = [w[0] for w in _split_end(gather_a, h1, "gather_ffn0_wait")]
    f0, gu0, a0 = _ffn_fwd(h1, w_gu_t0, w_down0, 0)
    g_b = [gain(norm_ffn_post, 0), gain(norm_mix_pre, 1)]
    x2, h2 = _rowwise(_fn_res_pre, [x1, f0], g_b, [(D, F32), (D, BF16)], ROW_TB, "res_pre1")
    km1, vm1, memn1 = _mem_kv_fwd(mem0, gain(mem_norm, 1), w_kv[1], 1)
    w_dn_in_g, w_dn_out, w_gu_t1, w_down1, conv_g = _split_end(gather_b, h2, "gather_layer1_wait")
    w_dn_in_g, w_dn_out, w_gu_t1, w_down1 = w_dn_in_g[0], w_dn_out[0], w_gu_t1[0], w_down1[0]
    conv_full = conv_g.transpose(1, 0, 2).reshape(DN_CONV, 3 * TOK_W)
    w_dn_in_t = jnp.concatenate(
        [w_dn_in_g[DN_IN_SHARD_PAD * j:DN_IN_SHARD_PAD * j + DN_IN_SHARD] for j in range(N_DEV)]
        + [jnp.zeros((DN_IN_PAD - DN_IN, D), BF16)], axis=0)
    p1 = _matmul(h2, w_dn_in_t, "nt", F32, "dn_in")
    cat1, res1 = _dn_mixer_fwd(p1, conv_full, dn_a_log[0], dn_dt_bias[0], dn_out_norm[0], km1, vm1)
    y1 = _matmul(cat1, w_dn_out, "nn", F32, "dn_out")
    g_c = [gain(norm_mix_post, 1), gain(norm_ffn_pre, 1)]
    x3, h3 = _rowwise(_fn_res_pre, [x2, y1], g_c, [(D, F32), (D, BF16)], ROW_TB, "res_pre2")
    f1, gu1, a1 = _ffn_fwd(h3, w_gu_t1, w_down1, 1)
    g_d = [gain(norm_ffn_post, 1)]
    (x4,) = _rowwise(_fn_res, [x3, f1], g_d, [(D, F32)], ROW_TB, "res3")
    dx4, lrow = _loss_kernel(x4, tgt, ROW_TB, "loss")
    loss = lax.psum(lrow[0, 0] * (0.5 / D), axes)

    (df1,), (dg_fpost1,) = _rowwise_bwd(_fn_res, [x3, f1], g_d, [dx4], [None, BF16], ROW_TB, "res3_bwd")
    dh3, dwgu1, dwd1 = _ffn_bwd(df1, h3, w_gu_t1, w_down1, gu1, a1, 1)
    (dx2, dy1), (dg_mpost1, dg_fpre1) = _rowwise_bwd(_fn_res_pre, [x2, y1], g_c, [dx4, dh3], [F32, BF16],
                                                     ROW_TB, "res_pre2_bwd")
    dcat1 = _matmul(dy1, w_dn_out, "nt", F32, "dn_out_dx")
    dw_dn_out = _matmul(cat1, dy1, "tn", BF16, "dn_out_dw")
    dp1, dconv, da_log, ddt_bias, dout_norm, dkm1, dvm1 = _dn_mixer_bwd(dcat1, res1, km1, vm1)
    dwkv1, dg_mem1 = _mem_kv_bwd(mem0, gain(mem_norm, 1), w_kv[1], memn1, dkm1, dvm1, 1)
    dh2 = _matmul(dp1, w_dn_in_t, "nn", F32, "dn_in_dx")
    dw_dn_in_t = _matmul(dp1, h2, "tn", BF16, "dn_in_dw")
    dn_in_parts = [jnp.pad(dw_dn_in_t[DN_IN_SHARD * j:DN_IN_SHARD * (j + 1)],
                           ((0, DN_IN_SHARD_PAD - DN_IN_SHARD), (0, 0))) for j in range(N_DEV)]
    xch_b, tok = _exchange_begin(
        [jnp.concatenate(dn_in_parts, axis=0)[None], dw_dn_out[None], dwkv1[None], dwgu1[None], dwd1[None]],
        [1, 1, 1, 1, 1], dh2, "exchange_layer1_start")
    (dx1, df0), (dg_fpost0, dg_mpre1) = _rowwise_bwd(_fn_res_pre, [x1, f0], [g + tok[0:1, 0:1] for g in g_b],
                                                     [dx2, dh2], [F32, BF16], ROW_TB, "res_pre1_bwd")
    dh1, dwgu0, dwd0 = _ffn_bwd(df0, h1, w_gu_t0, w_down0, gu0, a0, 0)
    xch_a, tok = _exchange_begin([dwgu0[None], dwd0[None]], [1, 1], dh1, "exchange_ffn0_start")
    (dx0, dy0), (dg_mpost0, dg_fpre0) = _rowwise_bwd(_fn_res_pre, [x0, y0], [g + tok[0:1, 0:1] for g in g_a],
                                                     [dx1, dh1], [F32, BF16], ROW_TB, "res_pre0_bwd")
    dcat0 = _matmul(dy0, w_att_out, "nt", F32, "att_out_dx")
    dw_att_out = _matmul(cat0, dy0, "tn", BF16, "att_out_dw")
    dp0, drel, dkm0, dvm0 = _attn_mixer_bwd(dcat0, res0, km0, vm0)
    dwkv0, dg_mem0 = _mem_kv_bwd(mem0, gain(mem_norm, 0), w_kv[0], memn0, dkm0, dvm0, 0)
    xch_o, tok = _exchange_begin([dw_att_out[None], dwkv0[None]], [1, 1], dp0, "exchange_att_out_start")
    dw_att_in_t = _matmul(dp0, h0, "tn", BF16, "att_in_dw")
    xch_i, tok_i = _exchange_begin([dw_att_in_t[None]], [1], tok, "exchange_att_in_start")
    dh0 = _matmul(dp0, w_att_in_t, "nn", F32, "att_in_dx")
    (grad_x,), (dg_mpre0,) = _rowwise_bwd(_fn_first, [x0], [gain(norm_mix_pre, 0) + tok_i[0:1, 0:1]], [dx0, dh0],
                                          [F32], ROW_TB, "pre0_bwd")

    small_grads = [drel, da_log, ddt_bias, dout_norm, jnp.concatenate([dg_mem0, dg_mem1]),
                   jnp.concatenate([dg_mpre0, dg_mpre1]), jnp.concatenate([dg_mpost0, dg_mpost1]),
                   jnp.concatenate([dg_fpre0, dg_fpre1]), jnp.concatenate([dg_fpost0, dg_fpost1]), dconv]
    (r_small,) = _exchange([_pack_small(small_grads, SMALL_ROWS)], [None], "exchange_last")
    (r_att_in,) = _split_end(xch_i, r_small, "exchange_att_in_wait")
    r_att_out, r_kv0 = _split_end(xch_o, r_small, "exchange_att_out_wait")
    r_gu0, r_down0 = _split_end(xch_a, r_small, "exchange_ffn0_wait")
    r_dn_in, r_dn_out, r_kv1, r_gu1, r_down1 = _split_end(xch_b, r_small, "exchange_layer1_wait")

    def rows(a):
        return a.reshape((-1,) + a.shape[-1:])

    def row_sharded(recv, w, m, v, tb, name):
        outs = _adamw_reduce(recv.reshape((N_DEV, -1) + recv.shape[-1:]), rows(w), rows(m), rows(v), tb, name)
        return [o.reshape(w.shape) for o in outs]

    def col_sharded(recv, w, m, v, tb, name):
        g_t = _reduce8(recv.reshape((N_DEV, -1) + recv.shape[-1:]), tb, name + "_sum")
        g = jnp.swapaxes(g_t.reshape(recv.shape[1:])[:, :w.shape[2]], 1, 2)
        outs = _adamw(rows(g), rows(w), rows(m), rows(v), 256, name)
        return [g] + [o.reshape(w.shape) for o in outs]

    def per_layer(fn, recvs, w, m, v, tb, name):
        outs = [fn(r, w[l:l + 1], m[l:l + 1], v[l:l + 1], tb, f"{name}{l}") for l, r in enumerate(recvs)]
        return [jnp.concatenate(pair, axis=0) for pair in zip(*outs)]

    big = [col_sharded(r_att_in, att_w_in, m_att_w_in, v_att_w_in, 320, "adamw_att_in"),
           row_sharded(r_att_out, att_w_out, m_att_w_out, v_att_w_out, 128, "adamw_att_out"),
           col_sharded(r_dn_in, dn_w_in, m_dn_w_in, v_dn_w_in, 432, "adamw_dn_in"),
           row_sharded(r_dn_out, dn_w_out, m_dn_w_out, v_dn_w_out, 128, "adamw_dn_out"),
           per_layer(row_sharded, [r_kv0, r_kv1], mem_w_kv, m_mem_w_kv, v_mem_w_kv, 128, "adamw_mem_kv"),
           per_layer(col_sharded, [r_gu0, r_gu1], ffn_w_gate_up, m_ffn_w_gate_up, v_ffn_w_gate_up, 176,
                     "adamw_ffn_gu"),
           per_layer(row_sharded, [r_down0, r_down1], ffn_w_down, m_ffn_w_down, v_ffn_w_down, 176,
                     "adamw_ffn_down")]
    g_big, d_big, nm_big, nv_big = [[b[i] for b in big] for i in range(4)]

    g_small = _reduce8(r_small, SMALL_ROWS, "reduce_small")
    rep_shapes = [(32, 12), (1, 2, 6), (1, 2, 6), (1, 128), (2, D), (2, D), (2, D), (2, D), (2, D)]
    *g_rep, g_conv_full = _unpack_small(g_small, rep_shapes + [(DN_CONV, 3 * TOK_W)])
    me = _index(_me_xyc())
    g_conv = lax.dynamic_slice(g_conv_full, (0, me * 288), (DN_CONV, 288)).reshape(CONV_SHARD)
    small_shapes = rep_shapes + [CONV_SHARD]
    small_w = [rel_bias, dn_a_log, dn_dt_bias, dn_out_norm, mem_norm, norm_mix_pre, norm_mix_post,
               norm_ffn_pre, norm_ffn_post, dn_conv]
    small_m = [m_rel_bias, m_dn_a_log, m_dn_dt_bias, m_dn_out_norm, m_mem_norm, m_norm_mix_pre, m_norm_mix_post,
               m_norm_ffn_pre, m_norm_ffn_post, m_dn_conv]
    small_v = [v_rel_bias, v_dn_a_log, v_dn_dt_bias, v_dn_out_norm, v_mem_norm, v_norm_mix_pre, v_norm_mix_post,
               v_norm_ffn_pre, v_norm_ffn_post, v_dn_conv]
    g_small_list = g_rep + [g_conv]
    outs_small = _adamw(_pack_small(g_small_list, 24), _pack_small(small_w, 24), _pack_small(small_m, 24),
                        _pack_small(small_v, 24), 24, "adamw_small")
    d_small, nm_small, nv_small = [_unpack_small(o, small_shapes) for o in outs_small]

    def ordered(small, big):
        return [small[0], big[0], big[1], big[2], small[9], small[1], small[2], small[3], big[3], small[4],
                big[4], small[5], small[6], small[7], small[8], big[5], big[6]]

    g_small_out = [g.reshape(s) for g, s in zip(g_small_list, small_shapes)]
    return (loss, grad_x[None], *ordered(g_small_out, g_big), *ordered(d_small, d_big),
            *ordered(nm_small, nm_big), *ordered(nv_small, nv_big))
```

```python
import functools
import math

import numpy as np
import jax
import jax.numpy as jnp
from jax import lax
from jax.experimental import pallas as pl
from jax.experimental.pallas import tpu as pltpu

F32 = jnp.float32
BF16 = jnp.bfloat16
HI = lax.Precision.HIGHEST
MESH = pl.DeviceIdType.MESH

N_DEV = 8
D = 1024
EPS = 1e-6
NEG = -1e30
TOK_W = 768
MEM_W = 256
ATT_HD = 64
DIL_GROUPS = ((128, 1), (512, 4), (2048, 16))
BAND_HALF = 64
REL_BUCKETS = 32
REL_MAX_DIST = 1024
DN_HD = 128
DN_HEADS = 6
DN_CONV = 5
DN_CHUNK = 64
MEM_HEADS = 4
D_FF = 2816
ATT_IN = 2560
DN_IN = 3352
DN_IN_PAD = 3456

ADAM_LR, ADAM_B1, ADAM_B2, ADAM_EPS, ADAM_WD, ADAM_STEP = 0.001, 0.9, 0.999, 1e-08, 0.01, 10

PACK_C = 512
BIG_ROWS = 6480
SMALL_ROWS = 48
VMEM_LIMIT = 48 * 1024 * 1024


def _cparams(sem=None):
    kw = dict(vmem_limit_bytes=VMEM_LIMIT)
    if sem is not None:
        kw["dimension_semantics"] = sem
    return pltpu.CompilerParams(**kw)


def _tile(n, cap):
    if n <= cap:
        return n
    best = None
    for t in range(128, cap + 1, 128):
        if n % t == 0:
            best = t
    assert best is not None, (n, cap)
    return best


def _matmul(a, b, mode, out_dtype, name, tm=1024, tn=1408, tk=None):
    if tk is None:
        tk = 4096 if mode == "tn" else 2816
    if mode == "tn":
        tm = min(tm, 512)
    if mode == "nn":
        (m, kc), (_, n) = a.shape, b.shape
        dims = (((1,), (0,)), ((), ()))
    elif mode == "nt":
        (m, kc), (n, _) = a.shape, b.shape
        dims = (((1,), (1,)), ((), ()))
    else:
        (kc, m), (_, n) = a.shape, b.shape
        dims = (((0,), (0,)), ((), ()))
    tm = m if m <= tm else _tile(m, tm)
    tn = _tile(n, tn)
    tk = _tile(kc, tk)
    nk = kc // tk

    def body(a_ref, b_ref, o_ref, acc_ref):
        k = pl.program_id(2)
        part = lax.dot_general(a_ref[...], b_ref[...], dims, preferred_element_type=F32)

        @pl.when(k == 0)
        def _():
            acc_ref[...] = part

        @pl.when(k > 0)
        def _():
            acc_ref[...] += part

        @pl.when(k == nk - 1)
        def _():
            o_ref[...] = acc_ref[...].astype(o_ref.dtype)

    if mode == "nn":
        a_spec = pl.BlockSpec((tm, tk), lambda i, j, k: (i, k))
        b_spec = pl.BlockSpec((tk, tn), lambda i, j, k: (k, j))
    elif mode == "nt":
        a_spec = pl.BlockSpec((tm, tk), lambda i, j, k: (i, k))
        b_spec = pl.BlockSpec((tn, tk), lambda i, j, k: (j, k))
    else:
        a_spec = pl.BlockSpec((tk, tm), lambda i, j, k: (k, i))
        b_spec = pl.BlockSpec((tk, tn), lambda i, j, k: (k, j))
    return pl.pallas_call(
        body, name=name, grid=(m // tm, n // tn, nk),
        in_specs=[a_spec, b_spec],
        out_specs=pl.BlockSpec((tm, tn), lambda i, j, k: (i, j)),
        out_shape=jax.ShapeDtypeStruct((m, n), out_dtype),
        scratch_shapes=[pltpu.VMEM((tm, tn), F32)],
        compiler_params=_cparams(("parallel", "parallel", "arbitrary")),
    )(a, b)


def _rowwise(fn, rows, params, outs, tb, name):
    t = rows[0].shape[0]
    nr, npar = len(rows), len(params)

    def body(*refs):
        ins = [r[...].astype(F32) for r in refs[:nr + npar]]
        res = fn(*ins)
        for o_ref, r in zip(refs[nr + npar:], res):
            o_ref[...] = r.astype(o_ref.dtype)

    return pl.pallas_call(
        body, name=name, grid=(t // tb,),
        in_specs=[pl.BlockSpec((tb, r.shape[1]), lambda i: (i, 0)) for r in rows]
        + [pl.BlockSpec(p.shape, lambda i: (0, 0)) for p in params],
        out_specs=[pl.BlockSpec((tb, c), lambda i: (i, 0)) for c, _ in outs],
        out_shape=[jax.ShapeDtypeStruct((t, c), dt) for c, dt in outs],
        compiler_params=_cparams(("parallel",)),
    )(*rows, *params)


def _rowwise_bwd(fn, rows, params, cots, row_grad, tb, name):
    t = rows[0].shape[0]
    nr, npar, nc = len(rows), len(params), len(cots)
    want = [i for i, g in enumerate(row_grad) if g is not None]

    def body(*refs):
        ins = [r[...].astype(F32) for r in refs[:nr + npar]]
        cts = tuple(r[...].astype(F32) for r in refs[nr + npar:nr + npar + nc])
        outs = refs[nr + npar + nc:]
        _, vjp = jax.vjp(fn, *ins)
        grads = vjp(cts)
        for o_ref, i in zip(outs[:len(want)], want):
            o_ref[...] = grads[i].astype(o_ref.dtype)
        first = pl.program_id(0) == 0
        for o_ref, g in zip(outs[len(want):], grads[nr:]):
            @pl.when(first)
            def _(o_ref=o_ref, g=g):
                o_ref[...] = g

            @pl.when(jnp.logical_not(first))
            def _(o_ref=o_ref, g=g):
                o_ref[...] += g

    res = pl.pallas_call(
        body, name=name, grid=(t // tb,),
        in_specs=[pl.BlockSpec((tb, r.shape[1]), lambda i: (i, 0)) for r in rows]
        + [pl.BlockSpec(p.shape, lambda i: (0, 0)) for p in params]
        + [pl.BlockSpec((tb, c.shape[1]), lambda i: (i, 0)) for c in cots],
        out_specs=[pl.BlockSpec((tb, rows[i].shape[1]), lambda i_: (i_, 0)) for i in want]
        + [pl.BlockSpec(p.shape, lambda i: (0, 0)) for p in params],
        out_shape=[jax.ShapeDtypeStruct(rows[i].shape, row_grad[i]) for i in want]
        + [jax.ShapeDtypeStruct(p.shape, F32) for p in params],
        compiler_params=_cparams(("arbitrary",)),
    )(*rows, *params, *cots)
    return list(res[:len(want)]), list(res[len(want):])


def _rms(x, g):
    return x * lax.rsqrt(jnp.mean(x * x, axis=-1, keepdims=True) + EPS) * g


def _fn_pre(x, g):
    return (_rms(x, g),)


def _fn_res_pre(x, y, g_post, g_pre):
    x1 = x + _rms(y, g_post)
    return x1, _rms(x1, g_pre)


def _fn_res(x, y, g_post):
    return (x + _rms(y, g_post),)


def _sigmoid(x):
    return 1.0 / (1.0 + jnp.exp(-x))


def _silu(x):
    return x * _sigmoid(x)


def _fn_swiglu(gu):
    return (_silu(gu[:, :D_FF]) * gu[:, D_FF:],)


def _fn_combine(o, lse):
    ls = [lse[:, 256 * g:256 * (g + 1)] for g in range(3)]
    mx = lax.stop_gradient(jnp.maximum(jnp.maximum(ls[0], ls[1]), ls[2]))
    es = [jnp.exp(l - mx) for l in ls]
    inv = 1.0 / (es[0] + es[1] + es[2])
    return (jnp.concatenate([o[:, 256 * g:256 * (g + 1)] * (es[g] * inv) for g in range(3)], axis=1),)


def _fn_outnorm(o_f, o_r, z, gain):
    res = []
    for h in range(DN_HEADS):
        sl = slice(DN_HD * h, DN_HD * (h + 1))
        o = o_f[:, sl] + o_r[:, sl]
        res.append(o * lax.rsqrt(jnp.mean(o * o, axis=-1, keepdims=True) + EPS) * gain * _silu(z[:, sl]))
    return (jnp.concatenate(res, axis=1),)


def _loss_kernel(x, tgt, tb, name):
    t, d = x.shape

    def body(x_ref, t_ref, dx_ref, l_ref, acc_ref):
        i = pl.program_id(0)
        e = x_ref[...] - t_ref[...]
        dx_ref[...] = e * (1.0 / d)
        part = jnp.sum(e * e, axis=0, keepdims=True)

        @pl.when(i == 0)
        def _():
            acc_ref[...] = part

        @pl.when(i > 0)
        def _():
            acc_ref[...] += part

        @pl.when(i == t // tb - 1)
        def _():
            l_ref[...] = jnp.broadcast_to(jnp.sum(acc_ref[...], axis=-1, keepdims=True), (1, 128))

    return pl.pallas_call(
        body, name=name, grid=(t // tb,),
        in_specs=[pl.BlockSpec((tb, d), lambda i: (i, 0))] * 2,
        out_specs=[pl.BlockSpec((tb, d), lambda i: (i, 0)), pl.BlockSpec((1, 128), lambda i: (0, 0))],
        out_shape=[jax.ShapeDtypeStruct((t, d), F32), jax.ShapeDtypeStruct((1, 128), F32)],
        scratch_shapes=[pltpu.VMEM((1, d), F32)],
        compiler_params=_cparams(("arbitrary",)),
    )(x, tgt)


def _band_fn(l_sub, bq, i, q, kw, vw, bm):
    w = bq + 2 * BAND_HALF
    s = lax.dot_general((q * (ATT_HD ** -0.5)).astype(BF16), kw.astype(BF16), (((2,), (2,)), ((0,), (0,))),
                        preferred_element_type=F32) + bm
    kpos = i * bq - BAND_HALF + lax.broadcasted_iota(jnp.int32, (4, bq, w), 2)
    s = jnp.where((kpos >= 0) & (kpos < l_sub), s, NEG)
    m = lax.stop_gradient(jnp.max(s, axis=-1, keepdims=True))
    p = jnp.exp(s - m)
    den = jnp.sum(p, axis=-1, keepdims=True)
    o = lax.dot_general(p.astype(BF16), vw.astype(BF16), (((2,), (1,)), ((0,), (0,))),
                        preferred_element_type=F32) / den
    return o, jnp.broadcast_to(m + jnp.log(den), o.shape)


def _band_specs(l_sub, bq):
    w = bq + 2 * BAND_HALF
    qs = pl.BlockSpec((None, 4, bq, ATT_HD), lambda r, i: (r, 0, i, 0))
    ks = pl.BlockSpec((None, 4, l_sub + 2 * BAND_HALF, ATT_HD), lambda r, i: (r, 0, 0, 0))
    bs = pl.BlockSpec((4, bq, w), lambda r, i: (0, 0, 0))
    return qs, ks, bs


def _band_fwd(q, k, v, bm, dil, l_sub, bq, name):
    w = bq + 2 * BAND_HALF
    qs, ks, bs = _band_specs(l_sub, bq)

    def body(q_ref, k_ref, v_ref, bm_ref, o_ref, l_ref):
        i = pl.program_id(1)
        st = pl.multiple_of(i * bq, bq)
        o, lse = _band_fn(l_sub, bq, i, q_ref[...].astype(F32), k_ref[:, pl.ds(st, w), :].astype(F32),
                          v_ref[:, pl.ds(st, w), :].astype(F32), bm_ref[...])
        o_ref[...] = o
        l_ref[...] = lse

    return pl.pallas_call(
        body, name=name, grid=(dil, l_sub // bq),
        in_specs=[qs, ks, ks, bs], out_specs=[qs, qs],
        out_shape=[jax.ShapeDtypeStruct(q.shape, F32)] * 2,
        compiler_params=_cparams(("parallel", "arbitrary")),
    )(q, k, v, bm)


def _band_bwd(q, k, v, bm, do, dlse, dil, l_sub, bq, name):
    w = bq + 2 * BAND_HALF
    qs, ks, bs = _band_specs(l_sub, bq)

    def body(q_ref, k_ref, v_ref, bm_ref, do_ref, dl_ref, dq_ref, dk_ref, dv_ref, dbm_ref):
        r, i = pl.program_id(0), pl.program_id(1)
        st = pl.multiple_of(i * bq, bq)
        _, vjp = jax.vjp(functools.partial(_band_fn, l_sub, bq, i),
                         q_ref[...].astype(F32), k_ref[:, pl.ds(st, w), :].astype(F32),
                         v_ref[:, pl.ds(st, w), :].astype(F32), bm_ref[...])
        dq, dkw, dvw, dbm = vjp((do_ref[...].astype(F32), dl_ref[...]))
        dq_ref[...] = dq.astype(dq_ref.dtype)

        @pl.when(i == 0)
        def _():
            dk_ref[...] = jnp.zeros_like(dk_ref)
            dv_ref[...] = jnp.zeros_like(dv_ref)

        dk_ref[:, pl.ds(st, w), :] += dkw
        dv_ref[:, pl.ds(st, w), :] += dvw

        @pl.when((i == 0) & (r == 0))
        def _():
            dbm_ref[...] = dbm

        @pl.when((i > 0) | (r > 0))
        def _():
            dbm_ref[...] += dbm

    return pl.pallas_call(
        body, name=name, grid=(dil, l_sub // bq),
        in_specs=[qs, ks, ks, bs, qs, qs], out_specs=[qs, ks, ks, bs],
        out_shape=[jax.ShapeDtypeStruct(q.shape, BF16), jax.ShapeDtypeStruct(k.shape, F32),
                   jax.ShapeDtypeStruct(k.shape, F32), jax.ShapeDtypeStruct(bm.shape, F32)],
        compiler_params=_cparams(("arbitrary", "arbitrary")),
    )(q, k, v, bm, do, dlse)


def _t5_bucket(rel):
    half = REL_BUCKETS // 2
    max_exact = half // 2
    n = np.abs(rel)
    large = max_exact + (np.log(np.maximum(n, 1) / max_exact) / math.log(REL_MAX_DIST / max_exact)
                         * (half - max_exact)).astype(np.int64)
    large = np.minimum(large, half - 1)
    return ((rel > 0) * half + np.where(n < max_exact, n, large)).astype(np.int32)


def _bucket_onehot(dil):
    idx = _t5_bucket(np.arange(-BAND_HALF, BAND_HALF + 1) * dil)
    oh = np.zeros((2 * BAND_HALF + 1, REL_BUCKETS), np.float32)
    oh[np.arange(2 * BAND_HALF + 1), idx] = 1.0
    return oh


def _band_bias(rel_bias, gi, dil, bq):
    w = bq + 2 * BAND_HALF
    nb = 2 * BAND_HALF + 1
    bias = jnp.dot(jnp.asarray(_bucket_onehot(dil)), rel_bias[:, 4 * gi:4 * gi + 4], precision=HI)
    row = jnp.concatenate([bias.T, jnp.full((4, w + 1 - nb), NEG, F32)], axis=1)
    flat = jnp.tile(row, (1, bq))[:, :bq * w]
    return flat.reshape(4, bq, w)


def _relbias_grad(dbms, name):
    nb = 2 * BAND_HALF + 1
    bq = max(d.shape[1] for d in dbms)
    skew = []
    for dbm in dbms:
        bqg, w = dbm.shape[1], dbm.shape[2]
        flat = jnp.pad(dbm.reshape(4, bqg * w), ((0, 0), (0, bqg)))
        skew.append(jnp.pad(flat.reshape(4, bqg, w + 1)[:, :, :nb], ((0, 0), (0, bq - bqg), (0, 256 - nb))))
    sk = jnp.concatenate(skew, axis=0)
    oh = np.zeros((3, 256, 128), np.float32)
    for gi, (_, dil) in enumerate(DIL_GROUPS):
        oh[gi, :2 * BAND_HALF + 1, :REL_BUCKETS] = _bucket_onehot(dil)

    def body(s_ref, oh_ref, o_ref):
        col = jnp.sum(s_ref[...], axis=0, keepdims=True)
        o_ref[...] = jnp.dot(jnp.broadcast_to(col, (8, 256)), oh_ref[...], precision=HI, preferred_element_type=F32)

    out = pl.pallas_call(
        body, name=name, grid=(12,),
        in_specs=[pl.BlockSpec((None, bq, 256), lambda n: (n, 0, 0)),
                  pl.BlockSpec((None, 256, 128), lambda n: (n // 4, 0, 0))],
        out_specs=pl.BlockSpec((None, 8, 128), lambda n: (n, 0, 0)),
        out_shape=jax.ShapeDtypeStruct((12, 8, 128), F32),
        compiler_params=_cparams(("parallel",)),
    )(sk, jnp.asarray(oh))
    return out[:, 0, :REL_BUCKETS].T


def _mem_fn(q, k, v):
    s = lax.dot_general((q * (ATT_HD ** -0.5)).astype(BF16), k.astype(BF16), (((2,), (2,)), ((0,), (0,))),
                        preferred_element_type=F32)
    m = lax.stop_gradient(jnp.max(s, axis=-1, keepdims=True))
    p = jnp.exp(s - m)
    p = p / jnp.sum(p, axis=-1, keepdims=True)
    return lax.dot_general(p.astype(BF16), v.astype(BF16), (((2,), (1,)), ((0,), (0,))), preferred_element_type=F32)


def _mem_specs(tb, ml):
    qs = pl.BlockSpec((MEM_HEADS, tb, ATT_HD), lambda i: (0, i, 0))
    ks = pl.BlockSpec((MEM_HEADS, ml, ATT_HD), lambda i: (0, 0, 0))
    return qs, ks


def _mem_fwd(q, k, v, tb, name):
    qs, ks = _mem_specs(tb, k.shape[1])

    def body(q_ref, k_ref, v_ref, o_ref):
        o_ref[...] = _mem_fn(q_ref[...].astype(F32), k_ref[...], v_ref[...])

    return pl.pallas_call(
        body, name=name, grid=(q.shape[1] // tb,),
        in_specs=[qs, ks, ks], out_specs=qs, out_shape=jax.ShapeDtypeStruct(q.shape, F32),
        compiler_params=_cparams(("parallel",)),
    )(q, k, v)


def _mem_bwd(q, k, v, do, tb, name):
    qs, ks = _mem_specs(tb, k.shape[1])

    def body(q_ref, k_ref, v_ref, do_ref, dq_ref, dk_ref, dv_ref):
        i = pl.program_id(0)
        _, vjp = jax.vjp(_mem_fn, q_ref[...].astype(F32), k_ref[...], v_ref[...])
        dq, dk, dv = vjp(do_ref[...])
        dq_ref[...] = dq

        @pl.when(i == 0)
        def _():
            dk_ref[...] = dk
            dv_ref[...] = dv

        @pl.when(i > 0)
        def _():
            dk_ref[...] += dk
            dv_ref[...] += dv

    return pl.pallas_call(
        body, name=name, grid=(q.shape[1] // tb,),
        in_specs=[qs, ks, ks, qs], out_specs=[qs, ks, ks],
        out_shape=[jax.ShapeDtypeStruct(q.shape, F32), jax.ShapeDtypeStruct(k.shape, F32),
                   jax.ShapeDtypeStruct(k.shape, F32)],
        compiler_params=_cparams(("arbitrary",)),
    )(q, k, v, do)


CONV_PAD = 8


def _conv_post(kind, acc):
    s = _silu(acc)
    if kind == 2:
        return s
    scale = DN_HD ** -0.5 if kind == 0 else 1.0
    return s * lax.rsqrt(jnp.sum(s * s, axis=-1, keepdims=True) + EPS) * scale


def _conv_rows(x_ref, t, start, rt):
    lo = min(max(start, 0), t - rt)
    x = x_ref[pl.ds(lo, rt), :]
    shift = lo - start
    if shift == 0:
        return x
    x = pltpu.roll(x, shift % rt, axis=0)
    row = lax.broadcasted_iota(jnp.int32, x.shape, 0)
    return jnp.where((row >= shift) if shift > 0 else (row < rt + shift), x, 0.0)


def _conv_acc(x_ref, t, w, r0, rt):
    acc = None
    for i in range(DN_CONV):
        term = w[i:i + 1, :] * _conv_rows(x_ref, t, r0 + i - DN_CONV // 2, rt)
        acc = term if acc is None else acc + term
    return acc


def _conv_fwd(x, w8, kind, rt, name):
    t = x.shape[0]

    def body(x_ref, w_ref, o_ref):
        w = w_ref[...]
        for r in range(t // rt):
            o_ref[pl.ds(r * rt, rt), :] = _conv_post(kind, _conv_acc(x_ref, t, w, r * rt, rt))

    return pl.pallas_call(
        body, name=name, grid=(DN_HEADS,),
        in_specs=[pl.BlockSpec((t, DN_HD), lambda j: (0, 6 * kind + j)),
                  pl.BlockSpec((8, DN_HD), lambda j: (0, 6 * kind + j))],
        out_specs=pl.BlockSpec((t, DN_HD), lambda j: (0, j)),
        out_shape=jax.ShapeDtypeStruct((t, TOK_W), F32),
        compiler_params=_cparams(("parallel",)),
    )(x, w8)


def _conv_bwd(x, w8, d_f, d_r, kind, rt, name):
    t = x.shape[0]

    def body(xp_ref, w_ref, df_ref, dr_ref, dx_ref, dw_ref, dpad_ref):
        w = w_ref[...]
        zero = jnp.zeros((CONV_PAD, DN_HD), F32)
        dpad_ref[pl.ds(0, CONV_PAD), :] = zero
        dpad_ref[pl.ds(CONV_PAD + t, CONV_PAD), :] = zero
        dw = [jnp.zeros((1, DN_HD), F32) for _ in range(DN_CONV)]
        for r in range(t // rt):
            rows = pl.ds(r * rt, rt)
            acc = _conv_acc(xp_ref, t, w, r * rt, rt)
            _, vjp = jax.vjp(functools.partial(_conv_post, kind), acc)
            (dacc,) = vjp(df_ref[rows, :] + dr_ref[rows, :])
            dpad_ref[pl.ds(CONV_PAD + r * rt, rt), :] = dacc
            for i in range(DN_CONV):
                xs = _conv_rows(xp_ref, t, r * rt + i - DN_CONV // 2, rt)
                dw[i] = dw[i] + jnp.sum(dacc * xs, axis=0, keepdims=True)
        dw_ref[...] = jnp.concatenate(dw + [jnp.zeros((8 - DN_CONV, DN_HD), F32)], axis=0)
        for r in range(t // rt):
            acc = None
            for i in range(DN_CONV):
                term = w[i:i + 1, :] * dpad_ref[pl.ds(CONV_PAD + r * rt - i + DN_CONV // 2, rt), :]
                acc = term if acc is None else acc + term
            dx_ref[pl.ds(r * rt, rt), :] = acc.astype(dx_ref.dtype)

    return pl.pallas_call(
        body, name=name, grid=(DN_HEADS,),
        in_specs=[pl.BlockSpec((t, DN_HD), lambda j: (0, 6 * kind + j)),
                  pl.BlockSpec((8, DN_HD), lambda j: (0, 6 * kind + j)),
                  pl.BlockSpec((t, DN_HD), lambda j: (0, j)),
                  pl.BlockSpec((t, DN_HD), lambda j: (0, j))],
        out_specs=[pl.BlockSpec((t, DN_HD), lambda j: (0, j)), pl.BlockSpec((8, DN_HD), lambda j: (0, j))],
        out_shape=[jax.ShapeDtypeStruct((t, TOK_W), BF16), jax.ShapeDtypeStruct((8, TOK_W), F32)],
        scratch_shapes=[pltpu.VMEM((t + 2 * CONV_PAD, DN_HD), F32)],
        compiler_params=_cparams(("parallel",)),
    )(x, w8, d_f, d_r)


def _softplus(x):
    e = jnp.exp(-jnp.abs(x))
    return jnp.maximum(x, 0.0) + jnp.where(e < 1e-4, e - 0.5 * e * e, jnp.log(1.0 + e))


_NN = (((2,), (1,)), ((0,), (0,)))
_NT = (((2,), (2,)), ((0,), (0,)))
_TN = (((1,), (1,)), ((0,), (0,)))


def _dot(a, b, dims=_NN):
    return lax.dot_general(a.astype(BF16), b.astype(BF16), dims, preferred_element_type=F32)


def _hi_lo(x):
    hi = x.astype(BF16)
    return hi, (x - hi.astype(F32)).astype(BF16)


def _mask_dot(mask_bf16, x, dims):
    x1 = x.astype(BF16)
    r = x - x1.astype(F32)
    x2, x3 = _hi_lo(r)
    d = functools.partial(lax.dot_general, dimension_numbers=dims, preferred_element_type=F32)
    return d(mask_bf16, x1) + d(mask_bf16, x2) + d(mask_bf16, x3)


@jax.custom_vjp
def _dot_mask(mask_bf16, x):
    return _mask_dot(mask_bf16, x, _NN)


def _dot_mask_fwd(mask_bf16, x):
    return _mask_dot(mask_bf16, x, _NN), mask_bf16


def _dot_mask_bwd(mask_bf16, ct):
    return jnp.zeros_like(mask_bf16), _mask_dot(mask_bf16, ct, _TN)


_dot_mask.defvjp(_dot_mask_fwd, _dot_mask_bwd)


def _dot3_raw(a, b, dims):
    a1, a2 = _hi_lo(a)
    b1, b2 = _hi_lo(b)
    d = functools.partial(lax.dot_general, dimension_numbers=dims, preferred_element_type=F32)
    return d(a1, b1) + d(a1, b2) + d(a2, b1)


def _unit_solve_pass(lmat, rhs, masks):
    ainv = masks[6] - lmat * masks[0]
    for sh in range(1, 6):
        ainv = ainv - _dot(_dot(ainv, lmat * masks[sh]), ainv)
    return _dot3_raw(ainv, rhs, _NN), ainv


@jax.custom_vjp
def _unit_solve(lmat, rhs, masks):
    return _unit_solve_pass(lmat, rhs, masks)[0]


def _unit_solve_fwd(lmat, rhs, masks):
    sol, ainv = _unit_solve_pass(lmat, rhs, masks)
    return sol, (sol, ainv, masks)


def _unit_solve_bwd(res, ct):
    sol, ainv, masks = res
    d_rhs = _dot3_raw(ainv, ct, _TN)
    return -_dot3_raw(d_rhs, sol, _NT), d_rhs, tuple(jnp.zeros_like(m) for m in masks)


_unit_solve.defvjp(_unit_solve_fwd, _unit_solve_bwd)


def _block_masks(rev, row, col):
    c = DN_CHUNK
    prow = jnp.where(rev, c - 1 - row, row)
    pcol = jnp.where(rev, c - 1 - col, col)
    masks = []
    for sh in range(6):
        differ = (prow ^ pcol) >> sh
        miss = (differ ^ 1) + (1 - ((prow >> sh) & 1))
        masks.append(jnp.where(miss == 0, 1.0, 0.0))
    masks.append(jnp.where(row == col, 1.0, 0.0))
    return tuple(masks)


def _dn_chunk(q, k, v, al, be, alc, a_row, dt_row, a_rowc, dt_rowc, s):
    n, c = q.shape[0], DN_CHUNK
    rev = lax.broadcasted_iota(jnp.int32, (n, c, c), 0) >= n // 2
    row = lax.broadcasted_iota(jnp.int32, (n, c, c), 1)
    col = lax.broadcasted_iota(jnp.int32, (n, c, c), 2)
    ahead = jnp.where(rev, col - row, row - col)
    incl = ahead >= 0
    strict = ahead > 0
    incl_b = incl.astype(BF16)

    g = -jnp.exp(a_row) * _softplus(al + dt_row)
    beta = _sigmoid(be)
    g_c = -jnp.exp(a_rowc) * _softplus(alc + dt_rowc)
    gc = _dot_mask(incl_b, g)
    gcc = _dot_mask(incl_b, g_c)
    decay = jnp.exp(jnp.where(incl, gcc - jnp.swapaxes(gcc, 1, 2), NEG))
    kb = k * beta
    lmat = jnp.where(strict, _dot(kb, k, _NT) * decay, 0.0)
    rhs = jnp.concatenate([v * beta, kb * jnp.exp(gc)], axis=2)
    sol = _unit_solve(lmat, rhs, _block_masks(rev, row, col))
    u, w = sol[:, :, :DN_HD], sol[:, :, DN_HD:]
    intra = jnp.where(incl, _dot(q, k, _NT) * decay, 0.0)
    v_new = u - _dot(w, s)
    out = _dot(q * jnp.exp(gc), s) + _dot(intra, v_new)
    g_last = jnp.sum(g, axis=1, keepdims=True)
    s_new = s * jnp.exp(g_last) + _dot(k * jnp.exp(g_last - gc), v_new, _TN)
    return out, s_new


DN_HG = 6


def _dn_load(f_refs, r_refs, alf, bef, alr, ber, a_ref, dt_ref):
    c, hg = DN_CHUNK, DN_HG
    sls = [slice(DN_HD * h, DN_HD * (h + 1)) for h in range(hg)]
    toks = [jnp.stack([f[:, sl] for sl in sls] + [r[:, sl] for sl in sls]) for f, r in zip(f_refs, r_refs)]
    al = jnp.concatenate([alf[...], alr[...]], axis=0)
    be = jnp.concatenate([bef[...], ber[...]], axis=0)
    alc = jnp.concatenate([alf[:, :, 0:c], alr[:, :, 0:c]], axis=0)
    a = jnp.concatenate([a_ref[0], a_ref[1]], axis=0)
    dt = jnp.concatenate([dt_ref[0], dt_ref[1]], axis=0)
    ac = jnp.concatenate([a_ref[0, :, :, 0:c], a_ref[1, :, :, 0:c]], axis=0)
    dtc = jnp.concatenate([dt_ref[0, :, :, 0:c], dt_ref[1, :, :, 0:c]], axis=0)
    return toks, (al, be, alc, a, dt, ac, dtc)


def _dn_views(nc, bwd):
    c, hg = DN_CHUNK, DN_HG
    if bwd:
        f_blk = lambda s: nc - 1 - s
        r_blk = lambda s: s
        st_blk = lambda s: nc - 1 - s
    else:
        f_blk = lambda s: s
        r_blk = lambda s: nc - 1 - s
        st_blk = lambda s: s
    tok_f = pl.BlockSpec((c, hg * DN_HD), lambda g, s: (f_blk(s), g))
    tok_r = pl.BlockSpec((c, hg * DN_HD), lambda g, s: (r_blk(s), g))
    gate_f = pl.BlockSpec((None, hg, c, DN_HD), lambda g, s: (0, g, f_blk(s), 0))
    gate_r = pl.BlockSpec((None, hg, c, DN_HD), lambda g, s: (1, g, r_blk(s), 0))
    par = pl.BlockSpec((2, hg, 1, DN_HD), lambda g, s: (0, g, 0, 0))
    state = pl.BlockSpec((2, hg, None, DN_HD, DN_HD), lambda g, s: (0, g, st_blk(s), 0, 0))
    return tok_f, tok_r, gate_f, gate_r, par, state


def _dn_fwd(q, k, v, al, be, a_rows, dt_rows, name):
    t = q.shape[0]
    c, hg = DN_CHUNK, DN_HG
    nc = t // c
    tok_f, tok_r, gate_f, gate_r, par, state = _dn_views(nc, False)

    def body(qf, kf, vf, qr, kr, vr, alf, bef, alr, ber, a_ref, dt_ref, of_ref, or_ref, st_ref, s_ref):
        @pl.when(pl.program_id(1) == 0)
        def _():
            s_ref[...] = jnp.zeros_like(s_ref)

        (q_, k_, v_), gates = _dn_load((qf, kf, vf), (qr, kr, vr), alf, bef, alr, ber, a_ref, dt_ref)
        s = s_ref[...]
        st_ref[0] = s[:hg]
        st_ref[1] = s[hg:]
        out, s_new = _dn_chunk(q_, k_, v_, *gates, s)
        for h in range(hg):
            sl = slice(DN_HD * h, DN_HD * (h + 1))
            of_ref[:, sl] = out[h]
            or_ref[:, sl] = out[hg + h]
        s_ref[...] = s_new

    return pl.pallas_call(
        body, name=name, grid=(DN_HEADS // hg, nc),
        in_specs=[tok_f] * 3 + [tok_r] * 3 + [gate_f, gate_f, gate_r, gate_r, par, par],
        out_specs=[tok_f, tok_r, state],
        out_shape=[jax.ShapeDtypeStruct((t, TOK_W), F32)] * 2
        + [jax.ShapeDtypeStruct((2, DN_HEADS, nc, DN_HD, DN_HD), F32)],
        scratch_shapes=[pltpu.VMEM((2 * hg, DN_HD, DN_HD), F32)],
        compiler_params=_cparams(("parallel", "arbitrary")),
    )(q, k, v, q, k, v, al, be, al, be, a_rows, dt_rows)


def _dn_bwd(q, k, v, al, be, a_rows, dt_rows, states, do, name):
    t = q.shape[0]
    c, hg = DN_CHUNK, DN_HG
    assert hg == DN_HEADS
    nc = t // c
    tok_f, tok_r, gate_f, gate_r, par, state = _dn_views(nc, True)
    gout_f = pl.BlockSpec((c, DN_HD), lambda g, s: (nc - 1 - s, 0))
    gout_r = pl.BlockSpec((c, DN_HD), lambda g, s: (s, 0))

    def body(qf, kf, vf, qr, kr, vr, alf, bef, alr, ber, a_ref, dt_ref, st_ref, dof, dor,
             dqf, dkf, dvf, dqr, dkr, dvr, dgf, dgr, da_ref, ddt_ref, ds_ref):
        first = pl.program_id(1) == 0

        @pl.when(first)
        def _():
            ds_ref[...] = jnp.zeros_like(ds_ref)
            da_ref[...] = jnp.zeros_like(da_ref)
            ddt_ref[...] = jnp.zeros_like(ddt_ref)

        def lanes(x):
            return jnp.sum(x, axis=-1, keepdims=True)

        (q_, k_, v_, do_), gates = _dn_load((qf, kf, vf, dof), (qr, kr, vr, dor), alf, bef, alr, ber, a_ref, dt_ref)
        s = jnp.concatenate([st_ref[0], st_ref[1]], axis=0)
        _, vjp = jax.vjp(_dn_chunk, q_, k_, v_, *gates, s)
        dq, dk, dv, dal, dbe, dalc, da, ddt, dac, ddtc, ds = vjp((do_, ds_ref[...]))
        for h in range(hg):
            sl = slice(DN_HD * h, DN_HD * (h + 1))
            dqf[:, sl], dkf[:, sl], dvf[:, sl] = dq[h], dk[h], dv[h]
            dqr[:, sl], dkr[:, sl], dvr[:, sl] = dq[hg + h], dk[hg + h], dv[hg + h]
        dal, dbe = lanes(dal) + lanes(dalc), lanes(dbe)
        lane = lax.broadcasted_iota(jnp.int32, (c, DN_HD), 1)
        for d, dg_ref in enumerate((dgf, dgr)):
            dg = jnp.zeros((c, DN_HD), F32)
            for h in range(hg):
                dg = jnp.where(lane == h, dal[d * hg + h], jnp.where(lane == hg + h, dbe[d * hg + h], dg))
            dg_ref[...] = dg
        da = jnp.broadcast_to(lanes(da) + lanes(dac), da.shape)
        ddt = jnp.broadcast_to(lanes(ddt) + lanes(ddtc), ddt.shape)
        da_ref[0] += da[:hg]
        da_ref[1] += da[hg:]
        ddt_ref[0] += ddt[:hg]
        ddt_ref[1] += ddt[hg:]
        ds_ref[...] = ds

    tok = jax.ShapeDtypeStruct((t, TOK_W), F32)
    gate = jax.ShapeDtypeStruct((t, DN_HD), F32)
    parsh = jax.ShapeDtypeStruct((2, DN_HEADS, 1, DN_HD), F32)
    res = pl.pallas_call(
        body, name=name, grid=(DN_HEADS // hg, nc),
        in_specs=[tok_f] * 3 + [tok_r] * 3 + [gate_f, gate_f, gate_r, gate_r, par, par, state, tok_f, tok_r],
        out_specs=[tok_f] * 3 + [tok_r] * 3 + [gout_f, gout_r, par, par],
        out_shape=[tok] * 6 + [gate] * 2 + [parsh] * 2,
        scratch_shapes=[pltpu.VMEM((2 * hg, DN_HD, DN_HD), F32)],
        compiler_params=_cparams(("parallel", "arbitrary")),
    )(q, k, v, q, k, v, al, be, al, be, a_rows, dt_rows, states, do, do)
    dqf, dkf, dvf, dqr, dkr, dvr, dgf, dgr, da, ddt = res
    dgate = jnp.concatenate([dgf[:, :2 * DN_HEADS], dgr[:, :2 * DN_HEADS]], axis=1)
    return (dqf, dkf, dvf), (dqr, dkr, dvr), dgate, da, ddt


BAND_BQ = 256
ROW_TB = 256
MEM_TB = 512
CONV_RT = 512


def _to_sub(x, dil):
    l = x.shape[0] // dil
    return x.reshape(l, dil, 4, ATT_HD).transpose(1, 2, 0, 3)


def _from_sub(x, dil):
    l = x.shape[2]
    return x.transpose(2, 0, 1, 3).reshape(l * dil, 4 * ATT_HD)


def _heads_major(x):
    return x.reshape(x.shape[0], MEM_HEADS, ATT_HD).transpose(1, 0, 2)


def _heads_minor(x):
    return x.transpose(1, 0, 2).reshape(x.shape[1], MEM_HEADS * ATT_HD)


def _mem_kv_fwd(mem, gain, w_kv, li):
    (memn,) = _rowwise(_fn_pre, [mem], [gain], [(D, BF16)], mem.shape[0], f"memnorm_fwd{li}")
    kv = _matmul(memn, w_kv, "nn", F32, f"memkv_fwd{li}")
    return _heads_major(kv[:, :MEM_W]), _heads_major(kv[:, MEM_W:]), memn


def _mem_kv_bwd(mem, gain, w_kv, memn, dkm, dvm, li):
    dkv = jnp.concatenate([_heads_minor(dkm), _heads_minor(dvm)], axis=1).astype(BF16)
    dw = _matmul(memn, dkv, "tn", BF16, f"memkv_dw{li}")
    dmemn = _matmul(dkv, w_kv, "nt", F32, f"memkv_dx{li}")
    _, (dgain,) = _rowwise_bwd(_fn_pre, [mem], [gain], [dmemn], [None], mem.shape[0], f"memnorm_bwd{li}")
    return dw, dgain


def _attn_mixer_fwd(p, rel_bias, kv_fn):
    t = p.shape[0]
    saved, outs, lses = [], [], []
    for gi, (_, dil) in enumerate(DIL_GROUPS):
        l_sub = t // dil
        bq = min(BAND_BQ, l_sub)
        q = _to_sub(p[:, 256 * gi:256 * (gi + 1)], dil)
        pad = ((0, 0), (0, 0), (BAND_HALF, BAND_HALF), (0, 0))
        k = jnp.pad(_to_sub(p[:, TOK_W + 256 * gi:TOK_W + 256 * (gi + 1)], dil), pad)
        v = jnp.pad(_to_sub(p[:, 2 * TOK_W + 256 * gi:2 * TOK_W + 256 * (gi + 1)], dil), pad)
        bm = _band_bias(rel_bias, gi, dil, bq)
        o, lse = _band_fwd(q, k, v, bm, dil, l_sub, bq, f"band_fwd{gi}")
        outs.append(_from_sub(o, dil))
        lses.append(_from_sub(lse, dil))
        saved.append((q, k, v, bm))
    o_all = jnp.concatenate(outs, axis=1)
    lse_all = jnp.concatenate(lses, axis=1)
    (mixed,) = _rowwise(_fn_combine, [o_all, lse_all], [], [(TOK_W, BF16)], ROW_TB, "combine_fwd")
    qm = _heads_major(p[:, 3 * TOK_W:])
    km, vm, memn = kv_fn(mixed)
    memo = _mem_fwd(qm, km, vm, min(MEM_TB, t), "mem_fwd0")
    cat = jnp.concatenate([mixed, _heads_minor(memo).astype(BF16)], axis=1)
    return cat, (saved, o_all, lse_all, qm), (km, vm, memn)


def _attn_mixer_bwd(dcat, res, km, vm):
    saved, o_all, lse_all, qm = res
    t = dcat.shape[0]
    (do_all, dlse_all), _ = _rowwise_bwd(_fn_combine, [o_all, lse_all], [], [dcat[:, :TOK_W]], [BF16, F32],
                                         ROW_TB, "combine_bwd")
    dqs, dks, dvs, dbms = [], [], [], []
    for gi, (_, dil) in enumerate(DIL_GROUPS):
        l_sub = t // dil
        bq = min(BAND_BQ, l_sub)
        q, k, v, bm = saved[gi]
        do = _to_sub(do_all[:, 256 * gi:256 * (gi + 1)], dil)
        dl = _to_sub(dlse_all[:, 256 * gi:256 * (gi + 1)], dil)
        dq, dk, dv, dbm = _band_bwd(q, k, v, bm, do, dl, dil, l_sub, bq, f"band_bwd{gi}")
        dqs.append(_from_sub(dq, dil))
        dks.append(_from_sub(dk[:, :, BAND_HALF:-BAND_HALF], dil))
        dvs.append(_from_sub(dv[:, :, BAND_HALF:-BAND_HALF], dil))
        dbms.append(dbm)
    dqm, dkm, dvm = _mem_bwd(qm, km, vm, _heads_major(dcat[:, TOK_W:]), min(MEM_TB, t), "mem_bwd0")
    dp = jnp.concatenate([d.astype(BF16) for d in dqs + dks + dvs + [_heads_minor(dqm)]], axis=1)
    return dp, _relbias_grad(dbms, "relbias_grad"), dkm, dvm


def _dn_mixer_fwd(p, conv_w, a_log, dt_bias, out_norm, km, vm):
    t = p.shape[0]
    rt = min(CONV_RT, t)
    xp = p
    w8 = jnp.pad(conv_w, ((0, 8 - DN_CONV), (0, 0)))
    q = _conv_fwd(xp, w8, 0, rt, "conv_fwd_q")
    k = _conv_fwd(xp, w8, 1, rt, "conv_fwd_k")
    v = _conv_fwd(xp, w8, 2, rt, "conv_fwd_v")
    gate = p[:, 4 * TOK_W:4 * TOK_W + 4 * DN_HEADS].reshape(t, 2, 2, DN_HEADS)
    bshape = (2, DN_HEADS, t, DN_HD)
    al = jnp.broadcast_to(gate[:, :, 0, :].transpose(1, 2, 0)[..., None], bshape)
    be = jnp.broadcast_to(gate[:, :, 1, :].transpose(1, 2, 0)[..., None], bshape)
    a_rows = jnp.broadcast_to(a_log[:, :, None, None], (2, DN_HEADS, 1, DN_HD))
    dt_rows = jnp.broadcast_to(dt_bias[:, :, None, None], (2, DN_HEADS, 1, DN_HD))
    o_f, o_r, states = _dn_fwd(q, k, v, al, be, a_rows, dt_rows, "dn_fwd")
    z = p[:, 3 * TOK_W:4 * TOK_W]
    gain = out_norm.reshape(1, DN_HD)
    (og,) = _rowwise(_fn_outnorm, [o_f, o_r, z], [gain], [(TOK_W, BF16)], ROW_TB, "outnorm_fwd")
    qm = _heads_major(p[:, 4 * TOK_W + 4 * DN_HEADS:DN_IN])
    memo = _mem_fwd(qm, km, vm, min(MEM_TB, t), "mem_fwd1")
    cat = jnp.concatenate([og, _heads_minor(memo).astype(BF16)], axis=1)
    return cat, (xp, w8, q, k, v, al, be, a_rows, dt_rows, o_f, o_r, states, z, gain, qm)


def _dn_mixer_bwd(dcat, res, km, vm):
    xp, w8, q, k, v, al, be, a_rows, dt_rows, o_f, o_r, states, z, gain, qm = res
    t = dcat.shape[0]
    rt = min(CONV_RT, t)
    (do, dz), (dgain,) = _rowwise_bwd(_fn_outnorm, [o_f, o_r, z], [gain], [dcat[:, :TOK_W]], [F32, None, BF16],
                                      ROW_TB, "outnorm_bwd")
    d_f, d_r, dgate, da, ddt = _dn_bwd(q, k, v, al, be, a_rows, dt_rows, states, do, "dn_bwd")
    dxs, dws = [], []
    for kind, nm in enumerate("qkv"):
        dx, dw = _conv_bwd(xp, w8, d_f[kind], d_r[kind], kind, rt, f"conv_bwd_{nm}")
        dxs.append(dx)
        dws.append(dw)
    dconv = jnp.concatenate(dws, axis=1)[:DN_CONV]
    dqm, dkm, dvm = _mem_bwd(qm, km, vm, _heads_major(dcat[:, TOK_W:]), min(MEM_TB, t), "mem_bwd1")
    dp = jnp.concatenate(dxs + [dz, dgate.astype(BF16), _heads_minor(dqm).astype(BF16),
                               jnp.zeros((t, DN_IN_PAD - DN_IN), BF16)], axis=1)
    return dp, dconv, da[:, :, 0, 0], ddt[:, :, 0, 0], dgain.reshape(DN_HD), dkm, dvm


SWI_TB = 256


def _ffn_fwd(h, w_gu_t, w_d, li):
    gu = _matmul(h, w_gu_t, "nt", BF16, f"ffn_gu{li}")
    (a,) = _rowwise(_fn_swiglu, [gu], [], [(D_FF, BF16)], SWI_TB, f"swiglu_fwd{li}")
    return _matmul(a, w_d, "nn", F32, f"ffn_down{li}"), gu, a


def _ffn_bwd(df, h, w_gu_t, w_d, gu, a, li):
    da = _matmul(df, w_d, "nt", BF16, f"ffn_down_dx{li}")
    dwd = _matmul(a, df, "tn", BF16, f"ffn_down_dw{li}")
    (dgu,), _ = _rowwise_bwd(_fn_swiglu, [gu], [], [da], [BF16], SWI_TB, f"swiglu_bwd{li}")
    dh = _matmul(dgu, w_gu_t, "nn", F32, f"ffn_gu_dx{li}")
    dwgu_t = _matmul(dgu, h, "tn", BF16, f"ffn_gu_dw{li}")
    return dh, dwgu_t, dwd


def _fn_first(x, g):
    return x, _rms(x, g)


def _me_xyc():
    return lax.axis_index("x"), lax.axis_index("y"), lax.axis_index("c")


def _flip(coords, k):
    x, y, c = coords
    return (1 - x if k & 4 else x, 1 - y if k & 2 else y, 1 - c if k & 1 else c)


def _index(coords):
    x, y, c = coords
    return 4 * x + 2 * y + c


def _window(ref, axis, size, d):
    idx = [slice(None)] * len(ref.shape)
    idx[axis] = pl.ds(pl.multiple_of(d * size, size), size)
    return ref.at[tuple(idx)]


def _comm_call(body, n, ins, out_shapes, name):
    hbm = pl.BlockSpec(memory_space=pl.ANY)
    return pl.pallas_call(
        body, name=name, in_specs=[hbm] * n, out_specs=[hbm] * n, out_shape=out_shapes,
        scratch_shapes=[pltpu.SemaphoreType.DMA((N_DEV - 1, n)), pltpu.SemaphoreType.DMA((N_DEV - 1, n)),
                        pltpu.SemaphoreType.DMA((n,))],
    )(*ins)


def _run_exchange(n, local, remote, send_sems, recv_sems):
    me = _me_xyc()
    locs = [local(p) for p in range(n)]
    for cp in locs:
        cp.start()
    sends = [remote(k, p, me, _flip(me, k)) for k in range(1, N_DEV) for p in range(n)]
    for cp in sends:
        cp.start()
    for k in range(1, N_DEV):
        for p in range(n):
            remote(k, p, _flip(me, k), me).wait_recv()
    for cp in sends:
        cp.wait_send()
    for cp in locs:
        cp.wait()


def _all_gather(shards, axes, name):
    n = len(shards)
    sizes = [s.shape[a] for s, a in zip(shards, axes)]

    def body(*refs):
        ins, outs = refs[:n], refs[n:2 * n]
        send_sems, recv_sems, loc_sems = refs[2 * n:]
        me = _me_xyc()

        def local(p):
            return pltpu.make_async_copy(ins[p], _window(outs[p], axes[p], sizes[p], _index(me)), loc_sems.at[p])

        def remote(k, p, owner, to):
            return pltpu.make_async_remote_copy(
                src_ref=ins[p], dst_ref=_window(outs[p], axes[p], sizes[p], _index(owner)),
                send_sem=send_sems.at[k - 1, p], recv_sem=recv_sems.at[k - 1, p], device_id=to, device_id_type=MESH)

        _run_exchange(n, local, remote, send_sems, recv_sems)

    def full(s, a):
        return s.shape[:a] + (N_DEV * s.shape[a],) + s.shape[a + 1:]

    return _comm_call(body, n, shards, [jax.ShapeDtypeStruct(full(s, a), s.dtype) for s, a in zip(shards, axes)], name)


def _exchange(fulls, axes, name):
    n = len(fulls)
    sizes = [None if a is None else f.shape[a] // N_DEV for f, a in zip(fulls, axes)]

    def part_shape(f, a):
        return f.shape if a is None else f.shape[:a] + (f.shape[a] // N_DEV,) + f.shape[a + 1:]

    def body(*refs):
        ins, outs = refs[:n], refs[n:2 * n]
        send_sems, recv_sems, loc_sems = refs[2 * n:]
        me = _me_xyc()

        def src(p, to):
            return ins[p] if axes[p] is None else _window(ins[p], axes[p], sizes[p], _index(to))

        def local(p):
            return pltpu.make_async_copy(src(p, me), outs[p].at[_index(me)], loc_sems.at[p])

        def remote(k, p, sender, to):
            return pltpu.make_async_remote_copy(
                src_ref=src(p, to), dst_ref=outs[p].at[_index(sender)],
                send_sem=send_sems.at[k - 1, p], recv_sem=recv_sems.at[k - 1, p], device_id=to, device_id_type=MESH)

        _run_exchange(n, local, remote, send_sems, recv_sems)

    return _comm_call(body, n, fulls,
                      [jax.ShapeDtypeStruct((N_DEV,) + part_shape(f, a), f.dtype) for f, a in zip(fulls, axes)], name)


_HBM = pl.BlockSpec(memory_space=pltpu.HBM)
_SEM = pl.BlockSpec(memory_space=pltpu.SEMAPHORE)
_EFFECT = pltpu.SideEffectType.DATAFLOW_SIDE_EFFECTING


def _in_hbm(a):
    return pltpu.with_memory_space_constraint(a, pltpu.HBM)


def _split_start(srcs, lands, after, descr, name):
    n = len(srcs)

    def body(*refs):
        ins, lnd = refs[:n], refs[n:2 * n]
        send_sems, recv_sems = refs[2 * n + 1], refs[2 * n + 2]
        token = refs[-1]
        me = _me_xyc()
        for k in range(1, N_DEV):
            for p in range(n):
                descr(k, p, ins, lnd, send_sems, recv_sems, me, _flip(me, k)).start()
        token[...] = jnp.zeros_like(token)

    sems = pltpu.SemaphoreType.DMA(((N_DEV - 1) * n,))
    res = pl.pallas_call(
        body, name=name,
        out_shape=(sems, sems, *[pltpu.HBM(a.shape, a.dtype) for a in (*srcs, *lands)],
                   jax.ShapeDtypeStruct((8, 128), F32)),
        in_specs=[_HBM] * (2 * n) + [pl.BlockSpec(memory_space=pl.ANY)],
        out_specs=(_SEM, _SEM, *[_HBM] * (2 * n), pl.BlockSpec(memory_space=pltpu.VMEM)),
        input_output_aliases={i: 2 + i for i in range(2 * n)},
        compiler_params=pltpu.CompilerParams(has_side_effects=_EFFECT),
    )(*[_in_hbm(a) for a in (*srcs, *lands)], after)
    return res[0], res[1], res[2:2 + n], res[2 + n:2 + 2 * n], res[-1]


def _split_wait(send_sems, recv_sems, srcs, lands, after, descr, name):
    n = len(srcs)

    def body(*refs):
        ins, lnd = refs[:n], refs[n:2 * n]
        s_sems, r_sems = refs[2 * n], refs[2 * n + 1]
        me = _me_xyc()
        for k in range(1, N_DEV):
            for p in range(n):
                peer = _flip(me, k)
                descr(k, p, ins, lnd, s_sems, r_sems, me, peer).wait_send()
                descr(k, p, ins, lnd, s_sems, r_sems, peer, me).wait_recv()

    res = pl.pallas_call(
        body, name=name,
        out_shape=tuple(pltpu.HBM(a.shape, a.dtype) for a in (*srcs, *lands)),
        in_specs=[_HBM] * (2 * n) + [_SEM, _SEM, pl.BlockSpec(memory_space=pl.ANY)],
        out_specs=tuple([_HBM] * (2 * n)),
        input_output_aliases={i: i for i in range(2 * n)},
        compiler_params=pltpu.CompilerParams(has_side_effects=_EFFECT),
    )(*srcs, *lands, send_sems, recv_sems, after)
    return list(res[n:])


def _gather_descr(axes, sizes):
    def descr(k, p, ins, lnd, send_sems, recv_sems, sender, dest):
        return pltpu.make_async_remote_copy(
            src_ref=ins[p], dst_ref=_window(lnd[p], axes[p], sizes[p], _index(sender)),
            send_sem=send_sems.at[(k - 1) * len(axes) + p], recv_sem=recv_sems.at[(k - 1) * len(axes) + p],
            device_id=dest, device_id_type=MESH)
    return descr


def _exchange_descr(axes, sizes):
    def descr(k, p, ins, lnd, send_sems, recv_sems, sender, dest):
        return pltpu.make_async_remote_copy(
            src_ref=_window(ins[p], axes[p], sizes[p], _index(dest)), dst_ref=lnd[p].at[_index(sender)],
            send_sem=send_sems.at[(k - 1) * len(axes) + p], recv_sem=recv_sems.at[(k - 1) * len(axes) + p],
            device_id=dest, device_id_type=MESH)
    return descr


def _gather_begin(shards, axes, after, name):
    sizes = [s.shape[a] for s, a in zip(shards, axes)]
    me = _index(_me_xyc())
    lands = []
    for s, a, sz in zip(shards, axes, sizes):
        full = s.shape[:a] + (N_DEV * sz,) + s.shape[a + 1:]
        lands.append(lax.dynamic_update_slice_in_dim(lax.empty(full, s.dtype), s, me * sz, a))
    descr = _gather_descr(axes, sizes)
    send_sems, recv_sems, srcs, lands, token = _split_start(shards, lands, after, descr, name)
    return (send_sems, recv_sems, srcs, lands, descr), token


def _exchange_begin(fulls, axes, after, name):
    sizes = [f.shape[a] // N_DEV for f, a in zip(fulls, axes)]
    me = _index(_me_xyc())
    lands = []
    for f, a, sz in zip(fulls, axes, sizes):
        own = lax.dynamic_slice_in_dim(f, me * sz, sz, a)
        lands.append(lax.dynamic_update_slice_in_dim(lax.empty((N_DEV,) + own.shape, f.dtype), own[None], me, 0))
    descr = _exchange_descr(axes, sizes)
    send_sems, recv_sems, srcs, lands, token = _split_start(fulls, lands, after, descr, name)
    return (send_sems, recv_sems, srcs, lands, descr), token


def _split_end(handle, after, name):
    send_sems, recv_sems, srcs, lands, descr = handle
    return _split_wait(send_sems, recv_sems, srcs, lands, after, descr, name)


def _adam_math(g, w, m, v):
    m = ADAM_B1 * m + (1.0 - ADAM_B1) * g
    v = ADAM_B2 * v + (1.0 - ADAM_B2) * (g * g)
    m_hat = m / (1.0 - ADAM_B1 ** ADAM_STEP)
    v_hat = v / (1.0 - ADAM_B2 ** ADAM_STEP)
    delta = -ADAM_LR * (m_hat / (jnp.sqrt(v_hat) + ADAM_EPS) + ADAM_WD * w)
    return delta, m, v


def _sum_slabs(r_ref):
    g = r_ref[0].astype(F32)
    for s in range(1, N_DEV):
        g = g + r_ref[s].astype(F32)
    return g


def _adamw_reduce(recv, w, m, v, tb, name):
    r, c = w.shape

    def body(r_ref, w_ref, m_ref, v_ref, g_ref, d_ref, nm_ref, nv_ref):
        g = _sum_slabs(r_ref)
        g_ref[...] = g
        d_ref[...], nm_ref[...], nv_ref[...] = _adam_math(g, w_ref[...], m_ref[...], v_ref[...])

    blk = pl.BlockSpec((tb, c), lambda i: (i, 0))
    return pl.pallas_call(
        body, name=name, grid=(r // tb,),
        in_specs=[pl.BlockSpec((N_DEV, tb, c), lambda i: (0, i, 0)), blk, blk, blk],
        out_specs=[blk] * 4, out_shape=[jax.ShapeDtypeStruct((r, c), F32)] * 4,
        compiler_params=_cparams(("parallel",)),
    )(recv, w, m, v)


def _reduce8(recv, tb, name):
    r, c = recv.shape[1:]

    def body(r_ref, g_ref):
        g_ref[...] = _sum_slabs(r_ref)

    return pl.pallas_call(
        body, name=name, grid=(r // tb,),
        in_specs=[pl.BlockSpec((N_DEV, tb, c), lambda i: (0, i, 0))],
        out_specs=pl.BlockSpec((tb, c), lambda i: (i, 0)), out_shape=jax.ShapeDtypeStruct((r, c), F32),
        compiler_params=_cparams(("parallel",)),
    )(recv)


def _adamw(g, w, m, v, tb, name):
    r, c = w.shape

    def body(g_ref, w_ref, m_ref, v_ref, d_ref, nm_ref, nv_ref):
        d_ref[...], nm_ref[...], nv_ref[...] = _adam_math(g_ref[...], w_ref[...], m_ref[...], v_ref[...])

    blk = pl.BlockSpec((tb, c), lambda i: (i, 0))
    return pl.pallas_call(
        body, name=name, grid=(r // tb,), in_specs=[blk] * 4, out_specs=[blk] * 3,
        out_shape=[jax.ShapeDtypeStruct((r, c), F32)] * 3, compiler_params=_cparams(("parallel",)),
    )(g, w, m, v)


DN_IN_SHARD = DN_IN // N_DEV
DN_IN_SHARD_PAD = 432
CONV_SHARD = (1, DN_CONV, 288)


def _pack_small(arrs, rows):
    flat = jnp.concatenate([a.astype(F32).reshape(-1) for a in arrs])
    return jnp.pad(flat, (0, rows * PACK_C - flat.size)).reshape(rows, PACK_C)


def _unpack_small(packed, shapes):
    flat, out, off = packed.reshape(-1), [], 0
    for shp in shapes:
        n = int(np.prod(shp))
        out.append(flat[off:off + n].reshape(shp))
        off += n
    return out


def kernel(x, mem, rel_bias, att_w_in, att_w_out, dn_w_in, dn_conv, dn_a_log, dn_dt_bias, dn_out_norm, dn_w_out, mem_norm, mem_w_kv, norm_mix_pre, norm_mix_post, norm_ffn_pre, norm_ffn_post, ffn_w_gate_up, ffn_w_down, loss_target, m_rel_bias, m_att_w_in, m_att_w_out, m_dn_w_in, m_dn_conv, m_dn_a_log, m_dn_dt_bias, m_dn_out_norm, m_dn_w_out, m_mem_norm, m_mem_w_kv, m_norm_mix_pre, m_norm_mix_post, m_norm_ffn_pre, m_norm_ffn_post, m_ffn_w_gate_up, m_ffn_w_down, v_rel_bias, v_att_w_in, v_att_w_out, v_dn_w_in, v_dn_conv, v_dn_a_log, v_dn_dt_bias, v_dn_out_norm, v_dn_w_out, v_mem_norm, v_mem_w_kv, v_norm_mix_pre, v_norm_mix_post, v_norm_ffn_pre, v_norm_ffn_post, v_ffn_w_gate_up, v_ffn_w_down):
    x0, mem0, tgt = x[0], mem[0], loss_target[0]
    t = x0.shape[0]
    axes = ("x", "y", "c")

    def t_shard(w):
        return jnp.swapaxes(w, 1, 2).astype(BF16)

    dn_in_pad = ((0, 0), (0, DN_IN_SHARD_PAD - DN_IN_SHARD), (0, 0))
    (w_att_in_t,) = _all_gather([t_shard(att_w_in)], [1], "allgather_first")
    w_att_in_t = w_att_in_t[0]
    gu_t, down = t_shard(ffn_w_gate_up), ffn_w_down.astype(BF16)
    gather_o, tok_o = _gather_begin([att_w_out.astype(BF16), mem_w_kv.astype(BF16)], [1, 1], w_att_in_t,
                                    "gather_att_out_start")
    gather_a, tok_a = _gather_begin([gu_t[0:1], down[0:1]], [1, 1], tok_o, "gather_ffn0_start")
    gather_b, tok_b = _gather_begin(
        [jnp.pad(t_shard(dn_w_in), dn_in_pad), dn_w_out.astype(BF16), gu_t[1:2], down[1:2], dn_conv],
        [1, 1, 1, 1, 0], tok_a, "gather_layer1_start")

    def gain(a, i):
        return a[i].reshape(1, D)

    (h0,) = _rowwise(_fn_pre, [x0], [gain(norm_mix_pre, 0) + tok_b[0:1, 0:1]], [(D, BF16)], ROW_TB, "pre0")
    p0 = _matmul(h0, w_att_in_t, "nt", BF16, "att_in")
    late = {}

    def kv0(after):
        late["w_att_out"], late["w_kv"] = _split_end(gather_o, after, "gather_att_out_wait")
        return _mem_kv_fwd(mem0, gain(mem_norm, 0), late["w_kv"][0], 0)

    cat0, res0, (km0, vm0, memn0) = _attn_mixer_fwd(p0, rel_bias, kv0)
    w_att_out, w_kv = late["w_att_out"][0], late["w_kv"]
    y0 = _matmul(cat0, w_att_out, "nn", F32, "att_out")
    g_a = [gain(norm_mix_post, 0), gain(norm_ffn_pre, 0)]
    x1, h1 = _rowwise(_fn_res_pre, [x0, y0], g_a, [(D, F32), (D, BF16)], ROW_TB, "res_pre0")
    w_gu_t0, w_down0 = [w[0] for w in _split_end(gather_a, h1, "gather_ffn0_wait")]
    f0, gu0, a0 = _ffn_fwd(h1, w_gu_t0, w_down0, 0)
    g_b = [gain(norm_ffn_post, 0), gain(norm_mix_pre, 1)]
    x2, h2 = _rowwise(_fn_res_pre, [x1, f0], g_b, [(D, F32), (D, BF16)], ROW_TB, "res_pre1")
    km1, vm1, memn1 = _mem_kv_fwd(mem0, gain(mem_norm, 1), w_kv[1], 1)
    w_dn_in_g, w_dn_out, w_gu_t1, w_down1, conv_g = _split_end(gather_b, h2, "gather_layer1_wait")
    w_dn_in_g, w_dn_out, w_gu_t1, w_down1 = w_dn_in_g[0], w_dn_out[0], w_gu_t1[0], w_down1[0]
    conv_full = conv_g.transpose(1, 0, 2).reshape(DN_CONV, 3 * TOK_W)
    w_dn_in_t = jnp.concatenate(
        [w_dn_in_g[DN_IN_SHARD_PAD * j:DN_IN_SHARD_PAD * j + DN_IN_SHARD] for j in range(N_DEV)]
        + [jnp.zeros((DN_IN_PAD - DN_IN, D), BF16)], axis=0)
    p1 = _matmul(h2, w_dn_in_t, "nt", F32, "dn_in")
    cat1, res1 = _dn_mixer_fwd(p1, conv_full, dn_a_log[0], dn_dt_bias[0], dn_out_norm[0], km1, vm1)
    y1 = _matmul(cat1, w_dn_out, "nn", F32, "dn_out")
    g_c = [gain(norm_mix_post, 1), gain(norm_ffn_pre, 1)]
    x3, h3 = _rowwise(_fn_res_pre, [x2, y1], g_c, [(D, F32), (D, BF16)], ROW_TB, "res_pre2")
    f1, gu1, a1 = _ffn_fwd(h3, w_gu_t1, w_down1, 1)
    g_d = [gain(norm_ffn_post, 1)]
    (x4,) = _rowwise(_fn_res, [x3, f1], g_d, [(D, F32)], ROW_TB, "res3")
    dx4, lrow = _loss_kernel(x4, tgt, ROW_TB, "loss")
    loss = lax.psum(lrow[0, 0] * (0.5 / D), axes)

    (df1,), (dg_fpost1,) = _rowwise_bwd(_fn_res, [x3, f1], g_d, [dx4], [None, BF16], ROW_TB, "res3_bwd")
    dh3, dwgu1, dwd1 = _ffn_bwd(df1, h3, w_gu_t1, w_down1, gu1, a1, 1)
    (dx2, dy1), (dg_mpost1, dg_fpre1) = _rowwise_bwd(_fn_res_pre, [x2, y1], g_c, [dx4, dh3], [F32, BF16],
                                                     ROW_TB, "res_pre2_bwd")
    dcat1 = _matmul(dy1, w_dn_out, "nt", F32, "dn_out_dx")
    dw_dn_out = _matmul(cat1, dy1, "tn", BF16, "dn_out_dw")
    dp1, dconv, da_log, ddt_bias, dout_norm, dkm1, dvm1 = _dn_mixer_bwd(dcat1, res1, km1, vm1)
    dwkv1, dg_mem1 = _mem_kv_bwd(mem0, gain(mem_norm, 1), w_kv[1], memn1, dkm1, dvm1, 1)
    dh2 = _matmul(dp1, w_dn_in_t, "nn", F32, "dn_in_dx")
    dw_dn_in_t = _matmul(dp1, h2, "tn", BF16, "dn_in_dw")
    dn_in_parts = [jnp.pad(dw_dn_in_t[DN_IN_SHARD * j:DN_IN_SHARD * (j + 1)],
                           ((0, DN_IN_SHARD_PAD - DN_IN_SHARD), (0, 0))) for j in range(N_DEV)]
    xch_b, tok = _exchange_begin(
        [jnp.concatenate(dn_in_parts, axis=0)[None], dw_dn_out[None], dwkv1[None], dwgu1[None], dwd1[None]],
        [1, 1, 1, 1, 1], dh2, "exchange_layer1_start")
    (dx1, df0), (dg_fpost0, dg_mpre1) = _rowwise_bwd(_fn_res_pre, [x1, f0], [g + tok[0:1, 0:1] for g in g_b],
                                                     [dx2, dh2], [F32, BF16], ROW_TB, "res_pre1_bwd")
    dh1, dwgu0, dwd0 = _ffn_bwd(df0, h1, w_gu_t0, w_down0, gu0, a0, 0)
    xch_a, tok = _exchange_begin([dwgu0[None], dwd0[None]], [1, 1], dh1, "exchange_ffn0_start")
    (dx0, dy0), (dg_mpost0, dg_fpre0) = _rowwise_bwd(_fn_res_pre, [x0, y0], [g + tok[0:1, 0:1] for g in g_a],
                                                     [dx1, dh1], [F32, BF16], ROW_TB, "res_pre0_bwd")
    dcat0 = _matmul(dy0, w_att_out, "nt", F32, "att_out_dx")
    dw_att_out = _matmul(cat0, dy0, "tn", BF16, "att_out_dw")
    dp0, drel, dkm0, dvm0 = _attn_mixer_bwd(dcat0, res0, km0, vm0)
    dwkv0, dg_mem0 = _mem_kv_bwd(mem0, gain(mem_norm, 0), w_kv[0], memn0, dkm0, dvm0, 0)
    xch_o, tok = _exchange_begin([dw_att_out[None], dwkv0[None]], [1, 1], dp0, "exchange_att_out_start")
    dw_att_in_t = _matmul(dp0, h0, "tn", BF16, "att_in_dw")
    xch_i, tok_i = _exchange_begin([dw_att_in_t[None]], [1], tok, "exchange_att_in_start")
    dh0 = _matmul(dp0, w_att_in_t, "nn", F32, "att_in_dx")
    (grad_x,), (dg_mpre0,) = _rowwise_bwd(_fn_first, [x0], [gain(norm_mix_pre, 0) + tok_i[0:1, 0:1]], [dx0, dh0],
                                          [F32], ROW_TB, "pre0_bwd")

    small_grads = [drel, da_log, ddt_bias, dout_norm, jnp.concatenate([dg_mem0, dg_mem1]),
                   jnp.concatenate([dg_mpre0, dg_mpre1]), jnp.concatenate([dg_mpost0, dg_mpost1]),
                   jnp.concatenate([dg_fpre0, dg_fpre1]), jnp.concatenate([dg_fpost0, dg_fpost1]), dconv]
    (r_small,) = _exchange([_pack_small(small_grads, SMALL_ROWS)], [None], "exchange_last")
    (r_att_in,) = _split_end(xch_i, r_small, "exchange_att_in_wait")
    r_att_out, r_kv0 = _split_end(xch_o, r_small, "exchange_att_out_wait")
    r_gu0, r_down0 = _split_end(xch_a, r_small, "exchange_ffn0_wait")
    r_dn_in, r_dn_out, r_kv1, r_gu1, r_down1 = _split_end(xch_b, r_small, "exchange_layer1_wait")

    def rows(a):
        return a.reshape((-1,) + a.shape[-1:])

    def row_sharded(recv, w, m, v, tb, name):
        outs = _adamw_reduce(recv.reshape((N_DEV, -1) + recv.shape[-1:]), rows(w), rows(m), rows(v), tb, name)
        return [o.reshape(w.shape) for o in outs]

    def col_sharded(recv, w, m, v, tb, name):
        g_t = _reduce8(recv.reshape((N_DEV, -1) + recv.shape[-1:]), tb, name + "_sum")
        g = jnp.swapaxes(g_t.reshape(recv.shape[1:])[:, :w.shape[2]], 1, 2)
        outs = _adamw(rows(g), rows(w), rows(m), rows(v), 256, name)
        return [g] + [o.reshape(w.shape) for o in outs]

    def per_layer(fn, recvs, w, m, v, tb, name):
        outs = [fn(r, w[l:l + 1], m[l:l + 1], v[l:l + 1], tb, f"{name}{l}") for l, r in enumerate(recvs)]
        return [jnp.concatenate(pair, axis=0) for pair in zip(*outs)]

    big = [col_sharded(r_att_in, att_w_in, m_att_w_in, v_att_w_in, 320, "adamw_att_in"),
           row_sharded(r_att_out, att_w_out, m_att_w_out, v_att_w_out, 128, "adamw_att_out"),
           col_sharded(r_dn_in, dn_w_in, m_dn_w_in, v_dn_w_in, 432, "adamw_dn_in"),
           row_sharded(r_dn_out, dn_w_out, m_dn_w_out, v_dn_w_out, 128, "adamw_dn_out"),
           per_layer(row_sharded, [r_kv0, r_kv1], mem_w_kv, m_mem_w_kv, v_mem_w_kv, 128, "adamw_mem_kv"),
           per_layer(col_sharded, [r_gu0, r_gu1], ffn_w_gate_up, m_ffn_w_gate_up, v_ffn_w_gate_up, 176,
                     "adamw_ffn_gu"),
           per_layer(row_sharded, [r_down0, r_down1], ffn_w_down, m_ffn_w_down, v_ffn_w_down, 176,
                     "adamw_ffn_down")]
    g_big, d_big, nm_big, nv_big = [[b[i] for b in big] for i in range(4)]

    g_small = _reduce8(r_small, SMALL_ROWS, "reduce_small")
    rep_shapes = [(32, 12), (1, 2, 6), (1, 2, 6), (1, 128), (2, D), (2, D), (2, D), (2, D), (2, D)]
    *g_rep, g_conv_full = _unpack_small(g_small, rep_shapes + [(DN_CONV, 3 * TOK_W)])
    me = _index(_me_xyc())
    g_conv = lax.dynamic_slice(g_conv_full, (0, me * 288), (DN_CONV, 288)).reshape(CONV_SHARD)
    small_shapes = rep_shapes + [CONV_SHARD]
    small_w = [rel_bias, dn_a_log, dn_dt_bias, dn_out_norm, mem_norm, norm_mix_pre, norm_mix_post,
               norm_ffn_pre, norm_ffn_post, dn_conv]
    small_m = [m_rel_bias, m_dn_a_log, m_dn_dt_bias, m_dn_out_norm, m_mem_norm, m_norm_mix_pre, m_norm_mix_post,
               m_norm_ffn_pre, m_norm_ffn_post, m_dn_conv]
    small_v = [v_rel_bias, v_dn_a_log, v_dn_dt_bias, v_dn_out_norm, v_mem_norm, v_norm_mix_pre, v_norm_mix_post,
               v_norm_ffn_pre, v_norm_ffn_post, v_dn_conv]
    g_small_list = g_rep + [g_conv]
    outs_small = _adamw(_pack_small(g_small_list, 24), _pack_small(small_w, 24), _pack_small(small_m, 24),
                        _pack_small(small_v, 24), 24, "adamw_small")
    d_small, nm_small, nv_small = [_unpack_small(o, small_shapes) for o in outs_small]

    def ordered(small, big):
        return [small[0], big[0], big[1], big[2], small[9], small[1], small[2], small[3], big[3], small[4],
                big[4], small[5], small[6], small[7], small[8], big[5], big[6]]

    g_small_out = [g.reshape(s) for g, s in zip(g_small_list, small_shapes)]
    return (loss, grad_x[None], *ordered(g_small_out, g_big), *ordered(d_small, d_big),
            *ordered(nm_small, nm_big), *ordered(nv_small, nv_big))
```

```python
import functools
import math

import numpy as np
import jax
import jax.numpy as jnp
from jax import lax
from jax.experimental import pallas as pl
from jax.experimental.pallas import tpu as pltpu

F32 = jnp.float32
BF16 = jnp.bfloat16
HI = lax.Precision.HIGHEST
MESH = pl.DeviceIdType.MESH

N_DEV = 8
D = 1024
EPS = 1e-6
NEG = -1e30
TOK_W = 768
MEM_W = 256
ATT_HD = 64
DIL_GROUPS = ((128, 1), (512, 4), (2048, 16))
BAND_HALF = 64
REL_BUCKETS = 32
REL_MAX_DIST = 1024
DN_HD = 128
DN_HEADS = 6
DN_CONV = 5
DN_CHUNK = 64
MEM_HEADS = 4
D_FF = 2816
ATT_IN = 2560
DN_IN = 3352
DN_IN_PAD = 3456

ADAM_LR, ADAM_B1, ADAM_B2, ADAM_EPS, ADAM_WD, ADAM_STEP = 0.001, 0.9, 0.999, 1e-08, 0.01, 10

PACK_C = 512
BIG_ROWS = 6480
SMALL_ROWS = 48
VMEM_LIMIT = 48 * 1024 * 1024


def _cparams(sem=None):
    kw = dict(vmem_limit_bytes=VMEM_LIMIT)
    if sem is not None:
        kw["dimension_semantics"] = sem
    return pltpu.CompilerParams(**kw)


def _tile(n, cap):
    if n <= cap:
        return n
    best = None
    for t in range(128, cap + 1, 128):
        if n % t == 0:
            best = t
    assert best is not None, (n, cap)
    return best


def _matmul(a, b, mode, out_dtype, name, tm=1024, tn=1408, tk=None):
    if tk is None:
        tk = 4096 if mode == "tn" else 2816
    if mode == "tn":
        tm = min(tm, 512)
    if mode == "nn":
        (m, kc), (_, n) = a.shape, b.shape
        dims = (((1,), (0,)), ((), ()))
    elif mode == "nt":
        (m, kc), (n, _) = a.shape, b.shape
        dims = (((1,), (1,)), ((), ()))
    else:
        (kc, m), (_, n) = a.shape, b.shape
        dims = (((0,), (0,)), ((), ()))
    tm = m if m <= tm else _tile(m, tm)
    tn = _tile(n, tn)
    tk = _tile(kc, tk)
    nk = kc // tk

    def body(a_ref, b_ref, o_ref, acc_ref):
        k = pl.program_id(2)
        part = lax.dot_general(a_ref[...], b_ref[...], dims, preferred_element_type=F32)

        @pl.when(k == 0)
        def _():
            acc_ref[...] = part

        @pl.when(k > 0)
        def _():
            acc_ref[...] += part

        @pl.when(k == nk - 1)
        def _():
            o_ref[...] = acc_ref[...].astype(o_ref.dtype)

    if mode == "nn":
        a_spec = pl.BlockSpec((tm, tk), lambda i, j, k: (i, k))
        b_spec = pl.BlockSpec((tk, tn), lambda i, j, k: (k, j))
    elif mode == "nt":
        a_spec = pl.BlockSpec((tm, tk), lambda i, j, k: (i, k))
        b_spec = pl.BlockSpec((tn, tk), lambda i, j, k: (j, k))
    else:
        a_spec = pl.BlockSpec((tk, tm), lambda i, j, k: (k, i))
        b_spec = pl.BlockSpec((tk, tn), lambda i, j, k: (k, j))
    return pl.pallas_call(
        body, name=name, grid=(m // tm, n // tn, nk),
        in_specs=[a_spec, b_spec],
        out_specs=pl.BlockSpec((tm, tn), lambda i, j, k: (i, j)),
        out_shape=jax.ShapeDtypeStruct((m, n), out_dtype),
        scratch_shapes=[pltpu.VMEM((tm, tn), F32)],
        compiler_params=_cparams(("parallel", "parallel", "arbitrary")),
    )(a, b)


def _rowwise(fn, rows, params, outs, tb, name):
    t = rows[0].shape[0]
    nr, npar = len(rows), len(params)

    def body(*refs):
        ins = [r[...].astype(F32) for r in refs[:nr + npar]]
        res = fn(*ins)
        for o_ref, r in zip(refs[nr + npar:], res):
            o_ref[...] = r.astype(o_ref.dtype)

    return pl.pallas_call(
        body, name=name, grid=(t // tb,),
        in_specs=[pl.BlockSpec((tb, r.shape[1]), lambda i: (i, 0)) for r in rows]
        + [pl.BlockSpec(p.shape, lambda i: (0, 0)) for p in params],
        out_specs=[pl.BlockSpec((tb, c), lambda i: (i, 0)) for c, _ in outs],
        out_shape=[jax.ShapeDtypeStruct((t, c), dt) for c, dt in outs],
        compiler_params=_cparams(("parallel",)),
    )(*rows, *params)


def _rowwise_bwd(fn, rows, params, cots, row_grad, tb, name):
    t = rows[0].shape[0]
    nr, npar, nc = len(rows), len(params), len(cots)
    want = [i for i, g in enumerate(row_grad) if g is not None]

    def body(*refs):
        ins = [r[...].astype(F32) for r in refs[:nr + npar]]
        cts = tuple(r[...].astype(F32) for r in refs[nr + npar:nr + npar + nc])
        outs = refs[nr + npar + nc:]
        _, vjp = jax.vjp(fn, *ins)
        grads = vjp(cts)
        for o_ref, i in zip(outs[:len(want)], want):
            o_ref[...] = grads[i].astype(o_ref.dtype)
        first = pl.program_id(0) == 0
        for o_ref, g in zip(outs[len(want):], grads[nr:]):
            @pl.when(first)
            def _(o_ref=o_ref, g=g):
                o_ref[...] = g

            @pl.when(jnp.logical_not(first))
            def _(o_ref=o_ref, g=g):
                o_ref[...] += g

    res = pl.pallas_call(
        body, name=name, grid=(t // tb,),
        in_specs=[pl.BlockSpec((tb, r.shape[1]), lambda i: (i, 0)) for r in rows]
        + [pl.BlockSpec(p.shape, lambda i: (0, 0)) for p in params]
        + [pl.BlockSpec((tb, c.shape[1]), lambda i: (i, 0)) for c in cots],
        out_specs=[pl.BlockSpec((tb, rows[i].shape[1]), lambda i_: (i_, 0)) for i in want]
        + [pl.BlockSpec(p.shape, lambda i: (0, 0)) for p in params],
        out_shape=[jax.ShapeDtypeStruct(rows[i].shape, row_grad[i]) for i in want]
        + [jax.ShapeDtypeStruct(p.shape, F32) for p in params],
        compiler_params=_cparams(("arbitrary",)),
    )(*rows, *params, *cots)
    return list(res[:len(want)]), list(res[len(want):])


def _rms(x, g):
    return x * lax.rsqrt(jnp.mean(x * x, axis=-1, keepdims=True) + EPS) * g


def _fn_pre(x, g):
    return (_rms(x, g),)


def _fn_res_pre(x, y, g_post, g_pre):
    x1 = x + _rms(y, g_post)
    return x1, _rms(x1, g_pre)


def _fn_res(x, y, g_post):
    return (x + _rms(y, g_post),)


def _sigmoid(x):
    return 1.0 / (1.0 + jnp.exp(-x))


def _silu(x):
    return x * _sigmoid(x)


def _fn_swiglu(gu):
    return (_silu(gu[:, :D_FF]) * gu[:, D_FF:],)


def _fn_combine(o, lse):
    ls = [lse[:, 256 * g:256 * (g + 1)] for g in range(3)]
    mx = lax.stop_gradient(jnp.maximum(jnp.maximum(ls[0], ls[1]), ls[2]))
    es = [jnp.exp(l - mx) for l in ls]
    inv = 1.0 / (es[0] + es[1] + es[2])
    return (jnp.concatenate([o[:, 256 * g:256 * (g + 1)] * (es[g] * inv) for g in range(3)], axis=1),)


def _fn_outnorm(o_f, o_r, z, gain):
    res = []
    for h in range(DN_HEADS):
        sl = slice(DN_HD * h, DN_HD * (h + 1))
        o = o_f[:, sl] + o_r[:, sl]
        res.append(o * lax.rsqrt(jnp.mean(o * o, axis=-1, keepdims=True) + EPS) * gain * _silu(z[:, sl]))
    return (jnp.concatenate(res, axis=1),)


def _loss_kernel(x, tgt, tb, name):
    t, d = x.shape

    def body(x_ref, t_ref, dx_ref, l_ref, acc_ref):
        i = pl.program_id(0)
        e = x_ref[...] - t_ref[...]
        dx_ref[...] = e * (1.0 / d)
        part = jnp.sum(e * e, axis=0, keepdims=True)

        @pl.when(i == 0)
        def _():
            acc_ref[...] = part

        @pl.when(i > 0)
        def _():
            acc_ref[...] += part

        @pl.when(i == t // tb - 1)
        def _():
            l_ref[...] = jnp.broadcast_to(jnp.sum(acc_ref[...], axis=-1, keepdims=True), (1, 128))

    return pl.pallas_call(
        body, name=name, grid=(t // tb,),
        in_specs=[pl.BlockSpec((tb, d), lambda i: (i, 0))] * 2,
        out_specs=[pl.BlockSpec((tb, d), lambda i: (i, 0)), pl.BlockSpec((1, 128), lambda i: (0, 0))],
        out_shape=[jax.ShapeDtypeStruct((t, d), F32), jax.ShapeDtypeStruct((1, 128), F32)],
        scratch_shapes=[pltpu.VMEM((1, d), F32)],
        compiler_params=_cparams(("arbitrary",)),
    )(x, tgt)


def _band_fn(l_sub, bq, i, q, kw, vw, bm):
    w = bq + 2 * BAND_HALF
    s = lax.dot_general((q * (ATT_HD ** -0.5)).astype(BF16), kw.astype(BF16), (((2,), (2,)), ((0,), (0,))),
                        preferred_element_type=F32) + bm
    kpos = i * bq - BAND_HALF + lax.broadcasted_iota(jnp.int32, (4, bq, w), 2)
    s = jnp.where((kpos >= 0) & (kpos < l_sub), s, NEG)
    m = lax.stop_gradient(jnp.max(s, axis=-1, keepdims=True))
    p = jnp.exp(s - m)
    den = jnp.sum(p, axis=-1, keepdims=True)
    o = lax.dot_general(p.astype(BF16), vw.astype(BF16), (((2,), (1,)), ((0,), (0,))),
                        preferred_element_type=F32) / den
    return o, jnp.broadcast_to(m + jnp.log(den), o.shape)


def _band_specs(l_sub, bq):
    w = bq + 2 * BAND_HALF
    qs = pl.BlockSpec((None, 4, bq, ATT_HD), lambda r, i: (r, 0, i, 0))
    ks = pl.BlockSpec((None, 4, l_sub + 2 * BAND_HALF, ATT_HD), lambda r, i: (r, 0, 0, 0))
    bs = pl.BlockSpec((4, bq, w), lambda r, i: (0, 0, 0))
    return qs, ks, bs


def _band_fwd(q, k, v, bm, dil, l_sub, bq, name):
    w = bq + 2 * BAND_HALF
    qs, ks, bs = _band_specs(l_sub, bq)

    def body(q_ref, k_ref, v_ref, bm_ref, o_ref, l_ref):
        i = pl.program_id(1)
        st = pl.multiple_of(i * bq, bq)
        o, lse = _band_fn(l_sub, bq, i, q_ref[...].astype(F32), k_ref[:, pl.ds(st, w), :].astype(F32),
                          v_ref[:, pl.ds(st, w), :].astype(F32), bm_ref[...])
        o_ref[...] = o
        l_ref[...] = lse

    return pl.pallas_call(
        body, name=name, grid=(dil, l_sub // bq),
        in_specs=[qs, ks, ks, bs], out_specs=[qs, qs],
        out_shape=[jax.ShapeDtypeStruct(q.shape, F32)] * 2,
        compiler_params=_cparams(("parallel", "arbitrary")),
    )(q, k, v, bm)


def _band_bwd(q, k, v, bm, do, dlse, dil, l_sub, bq, name):
    w = bq + 2 * BAND_HALF
    qs, ks, bs = _band_specs(l_sub, bq)

    def body(q_ref, k_ref, v_ref, bm_ref, do_ref, dl_ref, dq_ref, dk_ref, dv_ref, dbm_ref):
        r, i = pl.program_id(0), pl.program_id(1)
        st = pl.multiple_of(i * bq, bq)
        _, vjp = jax.vjp(functools.partial(_band_fn, l_sub, bq, i),
                         q_ref[...].astype(F32), k_ref[:, pl.ds(st, w), :].astype(F32),
                         v_ref[:, pl.ds(st, w), :].astype(F32), bm_ref[...])
        dq, dkw, dvw, dbm = vjp((do_ref[...].astype(F32), dl_ref[...]))
        dq_ref[...] = dq.astype(dq_ref.dtype)

        @pl.when(i == 0)
        def _():
            dk_ref[...] = jnp.zeros_like(dk_ref)
            dv_ref[...] = jnp.zeros_like(dv_ref)

        dk_ref[:, pl.ds(st, w), :] += dkw
        dv_ref[:, pl.ds(st, w), :] += dvw

        @pl.when((i == 0) & (r == 0))
        def _():
            dbm_ref[...] = dbm

        @pl.when((i > 0) | (r > 0))
        def _():
            dbm_ref[...] += dbm

    return pl.pallas_call(
        body, name=name, grid=(dil, l_sub // bq),
        in_specs=[qs, ks, ks, bs, qs, qs], out_specs=[qs, ks, ks, bs],
        out_shape=[jax.ShapeDtypeStruct(q.shape, BF16), jax.ShapeDtypeStruct(k.shape, F32),
                   jax.ShapeDtypeStruct(k.shape, F32), jax.ShapeDtypeStruct(bm.shape, F32)],
        compiler_params=_cparams(("arbitrary", "arbitrary")),
    )(q, k, v, bm, do, dlse)


def _t5_bucket(rel):
    half = REL_BUCKETS // 2
    max_exact = half // 2
    n = np.abs(rel)
    large = max_exact + (np.log(np.maximum(n, 1) / max_exact) / math.log(REL_MAX_DIST / max_exact)
                         * (half - max_exact)).astype(np.int64)
    large = np.minimum(large, half - 1)
    return ((rel > 0) * half + np.where(n < max_exact, n, large)).astype(np.int32)


def _bucket_onehot(dil):
    idx = _t5_bucket(np.arange(-BAND_HALF, BAND_HALF + 1) * dil)
    oh = np.zeros((2 * BAND_HALF + 1, REL_BUCKETS), np.float32)
    oh[np.arange(2 * BAND_HALF + 1), idx] = 1.0
    return oh


def _band_bias(rel_bias, gi, dil, bq):
    w = bq + 2 * BAND_HALF
    nb = 2 * BAND_HALF + 1
    bias = jnp.dot(jnp.asarray(_bucket_onehot(dil)), rel_bias[:, 4 * gi:4 * gi + 4], precision=HI)
    row = jnp.concatenate([bias.T, jnp.full((4, w + 1 - nb), NEG, F32)], axis=1)
    flat = jnp.tile(row, (1, bq))[:, :bq * w]
    return flat.reshape(4, bq, w)


def _relbias_grad(dbms, name):
    nb = 2 * BAND_HALF + 1
    bq = max(d.shape[1] for d in dbms)
    skew = []
    for dbm in dbms:
        bqg, w = dbm.shape[1], dbm.shape[2]
        flat = jnp.pad(dbm.reshape(4, bqg * w), ((0, 0), (0, bqg)))
        skew.append(jnp.pad(flat.reshape(4, bqg, w + 1)[:, :, :nb], ((0, 0), (0, bq - bqg), (0, 256 - nb))))
    sk = jnp.concatenate(skew, axis=0)
    oh = np.zeros((3, 256, 128), np.float32)
    for gi, (_, dil) in enumerate(DIL_GROUPS):
        oh[gi, :2 * BAND_HALF + 1, :REL_BUCKETS] = _bucket_onehot(dil)

    def body(s_ref, oh_ref, o_ref):
        col = jnp.sum(s_ref[...], axis=0, keepdims=True)
        o_ref[...] = jnp.dot(jnp.broadcast_to(col, (8, 256)), oh_ref[...], precision=HI, preferred_element_type=F32)

    out = pl.pallas_call(
        body, name=name, grid=(12,),
        in_specs=[pl.BlockSpec((None, bq, 256), lambda n: (n, 0, 0)),
                  pl.BlockSpec((None, 256, 128), lambda n: (n // 4, 0, 0))],
        out_specs=pl.BlockSpec((None, 8, 128), lambda n: (n, 0, 0)),
        out_shape=jax.ShapeDtypeStruct((12, 8, 128), F32),
        compiler_params=_cparams(("parallel",)),
    )(sk, jnp.asarray(oh))
    return out[:, 0, :REL_BUCKETS].T


def _mem_fn(q, k, v):
    s = lax.dot_general((q * (ATT_HD ** -0.5)).astype(BF16), k.astype(BF16), (((2,), (2,)), ((0,), (0,))),
                        preferred_element_type=F32)
    m = lax.stop_gradient(jnp.max(s, axis=-1, keepdims=True))
    p = jnp.exp(s - m)
    p = p / jnp.sum(p, axis=-1, keepdims=True)
    return lax.dot_general(p.astype(BF16), v.astype(BF16), (((2,), (1,)), ((0,), (0,))), preferred_element_type=F32)


def _mem_specs(tb, ml):
    qs = pl.BlockSpec((MEM_HEADS, tb, ATT_HD), lambda i: (0, i, 0))
    ks = pl.BlockSpec((MEM_HEADS, ml, ATT_HD), lambda i: (0, 0, 0))
    return qs, ks


def _mem_fwd(q, k, v, tb, name):
    qs, ks = _mem_specs(tb, k.shape[1])

    def body(q_ref, k_ref, v_ref, o_ref):
        o_ref[...] = _mem_fn(q_ref[...].astype(F32), k_ref[...], v_ref[...])

    return pl.pallas_call(
        body, name=name, grid=(q.shape[1] // tb,),
        in_specs=[qs, ks, ks], out_specs=qs, out_shape=jax.ShapeDtypeStruct(q.shape, F32),
        compiler_params=_cparams(("parallel",)),
    )(q, k, v)


def _mem_bwd(q, k, v, do, tb, name):
    qs, ks = _mem_specs(tb, k.shape[1])

    def body(q_ref, k_ref, v_ref, do_ref, dq_ref, dk_ref, dv_ref):
        i = pl.program_id(0)
        _, vjp = jax.vjp(_mem_fn, q_ref[...].astype(F32), k_ref[...], v_ref[...])
        dq, dk, dv = vjp(do_ref[...])
        dq_ref[...] = dq

        @pl.when(i == 0)
        def _():
            dk_ref[...] = dk
            dv_ref[...] = dv

        @pl.when(i > 0)
        def _():
            dk_ref[...] += dk
            dv_ref[...] += dv

    return pl.pallas_call(
        body, name=name, grid=(q.shape[1] // tb,),
        in_specs=[qs, ks, ks, qs], out_specs=[qs, ks, ks],
        out_shape=[jax.ShapeDtypeStruct(q.shape, F32), jax.ShapeDtypeStruct(k.shape, F32),
                   jax.ShapeDtypeStruct(k.shape, F32)],
        compiler_params=_cparams(("arbitrary",)),
    )(q, k, v, do)


CONV_PAD = 8


def _conv_post(kind, acc):
    s = _silu(acc)
    if kind == 2:
        return s
    scale = DN_HD ** -0.5 if kind == 0 else 1.0
    return s * lax.rsqrt(jnp.sum(s * s, axis=-1, keepdims=True) + EPS) * scale


def _conv_rows(x_ref, t, start, rt):
    lo = min(max(start, 0), t - rt)
    x = x_ref[pl.ds(lo, rt), :]
    shift = lo - start
    if shift == 0:
        return x
    x = pltpu.roll(x, shift % rt, axis=0)
    row = lax.broadcasted_iota(jnp.int32, x.shape, 0)
    return jnp.where((row >= shift) if shift > 0 else (row < rt + shift), x, 0.0)


def _conv_acc(x_ref, t, w, r0, rt):
    acc = None
    for i in range(DN_CONV):
        term = w[i:i + 1, :] * _conv_rows(x_ref, t, r0 + i - DN_CONV // 2, rt)
        acc = term if acc is None else acc + term
    return acc


def _conv_fwd(x, w8, kind, rt, name):
    t = x.shape[0]

    def body(x_ref, w_ref, o_ref):
        w = w_ref[...]
        for r in range(t // rt):
            o_ref[pl.ds(r * rt, rt), :] = _conv_post(kind, _conv_acc(x_ref, t, w, r * rt, rt))

    return pl.pallas_call(
        body, name=name, grid=(DN_HEADS,),
        in_specs=[pl.BlockSpec((t, DN_HD), lambda j: (0, 6 * kind + j)),
                  pl.BlockSpec((8, DN_HD), lambda j: (0, 6 * kind + j))],
        out_specs=pl.BlockSpec((t, DN_HD), lambda j: (0, j)),
        out_shape=jax.ShapeDtypeStruct((t, TOK_W), F32),
        compiler_params=_cparams(("parallel",)),
    )(x, w8)


def _conv_bwd(x, w8, d_f, d_r, dp, kind, rt, name):
    t = x.shape[0]

    def body(xp_ref, w_ref, df_ref, dr_ref, dp_in, dx_ref, dw_ref, dpad_ref):
        del dp_in
        w = w_ref[...]
        zero = jnp.zeros((CONV_PAD, DN_HD), F32)
        dpad_ref[pl.ds(0, CONV_PAD), :] = zero
        dpad_ref[pl.ds(CONV_PAD + t, CONV_PAD), :] = zero
        dw = [jnp.zeros((1, DN_HD), F32) for _ in range(DN_CONV)]
        for r in range(t // rt):
            rows = pl.ds(r * rt, rt)
            acc = _conv_acc(xp_ref, t, w, r * rt, rt)
            _, vjp = jax.vjp(functools.partial(_conv_post, kind), acc)
            (dacc,) = vjp(df_ref[rows, :] + dr_ref[rows, :])
            dpad_ref[pl.ds(CONV_PAD + r * rt, rt), :] = dacc
            for i in range(DN_CONV):
                xs = _conv_rows(xp_ref, t, r * rt + i - DN_CONV // 2, rt)
                dw[i] = dw[i] + jnp.sum(dacc * xs, axis=0, keepdims=True)
        dw_ref[...] = jnp.concatenate(dw + [jnp.zeros((8 - DN_CONV, DN_HD), F32)], axis=0)
        for r in range(t // rt):
            acc = None
            for i in range(DN_CONV):
                term = w[i:i + 1, :] * dpad_ref[pl.ds(CONV_PAD + r * rt - i + DN_CONV // 2, rt), :]
                acc = term if acc is None else acc + term
            dx_ref[pl.ds(r * rt, rt), :] = acc.astype(dx_ref.dtype)

    return pl.pallas_call(
        body, name=name, grid=(DN_HEADS,),
        in_specs=[pl.BlockSpec((t, DN_HD), lambda j: (0, 6 * kind + j)),
                  pl.BlockSpec((8, DN_HD), lambda j: (0, 6 * kind + j)),
                  pl.BlockSpec((t, DN_HD), lambda j: (0, j)),
                  pl.BlockSpec((t, DN_HD), lambda j: (0, j)),
                  pl.BlockSpec(memory_space=pl.ANY)],
        out_specs=[pl.BlockSpec((t, DN_HD), lambda j: (0, 6 * kind + j)),
                   pl.BlockSpec((8, DN_HD), lambda j: (0, j))],
        out_shape=[jax.ShapeDtypeStruct(dp.shape, dp.dtype), jax.ShapeDtypeStruct((8, TOK_W), F32)],
        input_output_aliases={4: 0},
        scratch_shapes=[pltpu.VMEM((t + 2 * CONV_PAD, DN_HD), F32)],
        compiler_params=_cparams(("parallel",)),
    )(x, w8, d_f, d_r, dp)


def _softplus(x):
    e = jnp.exp(-jnp.abs(x))
    return jnp.maximum(x, 0.0) + jnp.where(e < 1e-4, e - 0.5 * e * e, jnp.log(1.0 + e))


_NN = (((2,), (1,)), ((0,), (0,)))
_NT = (((2,), (2,)), ((0,), (0,)))
_TN = (((1,), (1,)), ((0,), (0,)))


def _dot(a, b, dims=_NN):
    return lax.dot_general(a.astype(BF16), b.astype(BF16), dims, preferred_element_type=F32)


def _hi_lo(x):
    hi = x.astype(BF16)
    return hi, (x - hi.astype(F32)).astype(BF16)


def _mask_dot(mask_bf16, x, dims):
    x1 = x.astype(BF16)
    r = x - x1.astype(F32)
    x2, x3 = _hi_lo(r)
    d = functools.partial(lax.dot_general, dimension_numbers=dims, preferred_element_type=F32)
    return d(mask_bf16, x1) + d(mask_bf16, x2) + d(mask_bf16, x3)


@jax.custom_vjp
def _dot_mask(mask_bf16, x):
    return _mask_dot(mask_bf16, x, _NN)


def _dot_mask_fwd(mask_bf16, x):
    return _mask_dot(mask_bf16, x, _NN), mask_bf16


def _dot_mask_bwd(mask_bf16, ct):
    return jnp.zeros_like(mask_bf16), _mask_dot(mask_bf16, ct, _TN)


_dot_mask.defvjp(_dot_mask_fwd, _dot_mask_bwd)


def _dot3_raw(a, b, dims):
    a1, a2 = _hi_lo(a)
    b1, b2 = _hi_lo(b)
    d = functools.partial(lax.dot_general, dimension_numbers=dims, preferred_element_type=F32)
    return d(a1, b1) + d(a1, b2) + d(a2, b1)


def _unit_solve_pass(lmat, rhs, masks):
    ainv = masks[6] - lmat * masks[0]
    for sh in range(1, 6):
        ainv = ainv - _dot(_dot(ainv, lmat * masks[sh]), ainv)
    return _dot3_raw(ainv, rhs, _NN), ainv


@jax.custom_vjp
def _unit_solve(lmat, rhs, masks):
    return _unit_solve_pass(lmat, rhs, masks)[0]


def _unit_solve_fwd(lmat, rhs, masks):
    sol, ainv = _unit_solve_pass(lmat, rhs, masks)
    return sol, (sol, ainv, masks)


def _unit_solve_bwd(res, ct):
    sol, ainv, masks = res
    d_rhs = _dot3_raw(ainv, ct, _TN)
    return -_dot3_raw(d_rhs, sol, _NT), d_rhs, tuple(jnp.zeros_like(m) for m in masks)


_unit_solve.defvjp(_unit_solve_fwd, _unit_solve_bwd)


def _block_masks(rev, row, col):
    c = DN_CHUNK
    prow = jnp.where(rev, c - 1 - row, row)
    pcol = jnp.where(rev, c - 1 - col, col)
    masks = []
    for sh in range(6):
        differ = (prow ^ pcol) >> sh
        miss = (differ ^ 1) + (1 - ((prow >> sh) & 1))
        masks.append(jnp.where(miss == 0, 1.0, 0.0))
    masks.append(jnp.where(row == col, 1.0, 0.0))
    return tuple(masks)


def _dn_chunk(q, k, v, al, be, alc, a_row, dt_row, a_rowc, dt_rowc, s):
    n, c = q.shape[0], DN_CHUNK
    rev = lax.broadcasted_iota(jnp.int32, (n, c, c), 0) >= n // 2
    row = lax.broadcasted_iota(jnp.int32, (n, c, c), 1)
    col = lax.broadcasted_iota(jnp.int32, (n, c, c), 2)
    ahead = jnp.where(rev, col - row, row - col)
    incl = ahead >= 0
    strict = ahead > 0
    incl_b = incl.astype(BF16)

    g = -jnp.exp(a_row) * _softplus(al + dt_row)
    beta = _sigmoid(be)
    g_c = -jnp.exp(a_rowc) * _softplus(alc + dt_rowc)
    gc = _dot_mask(incl_b, g)
    gcc = _dot_mask(incl_b, g_c)
    decay = jnp.exp(jnp.where(incl, gcc - jnp.swapaxes(gcc, 1, 2), NEG))
    kb = k * beta
    lmat = jnp.where(strict, _dot(kb, k, _NT) * decay, 0.0)
    rhs = jnp.concatenate([v * beta, kb * jnp.exp(gc)], axis=2)
    sol = _unit_solve(lmat, rhs, _block_masks(rev, row, col))
    u, w = sol[:, :, :DN_HD], sol[:, :, DN_HD:]
    intra = jnp.where(incl, _dot(q, k, _NT) * decay, 0.0)
    v_new = u - _dot(w, s)
    out = _dot(q * jnp.exp(gc), s) + _dot(intra, v_new)
    g_last = jnp.sum(g, axis=1, keepdims=True)
    s_new = s * jnp.exp(g_last) + _dot(k * jnp.exp(g_last - gc), v_new, _TN)
    return out, s_new


DN_HG = 6


def _dn_load(f_refs, r_refs, alf, bef, alr, ber, a_ref, dt_ref):
    c, hg = DN_CHUNK, DN_HG
    sls = [slice(DN_HD * h, DN_HD * (h + 1)) for h in range(hg)]
    toks = [jnp.stack([f[:, sl] for sl in sls] + [r[:, sl] for sl in sls]) for f, r in zip(f_refs, r_refs)]
    al = jnp.concatenate([alf[...], alr[...]], axis=0)
    be = jnp.concatenate([bef[...], ber[...]], axis=0)
    alc = jnp.concatenate([alf[:, :, 0:c], alr[:, :, 0:c]], axis=0)
    a = jnp.concatenate([a_ref[0], a_ref[1]], axis=0)
    dt = jnp.concatenate([dt_ref[0], dt_ref[1]], axis=0)
    ac = jnp.concatenate([a_ref[0, :, :, 0:c], a_ref[1, :, :, 0:c]], axis=0)
    dtc = jnp.concatenate([dt_ref[0, :, :, 0:c], dt_ref[1, :, :, 0:c]], axis=0)
    return toks, (al, be, alc, a, dt, ac, dtc)


def _dn_views(nc, bwd):
    c, hg = DN_CHUNK, DN_HG
    if bwd:
        f_blk = lambda s: nc - 1 - s
        r_blk = lambda s: s
        st_blk = lambda s: nc - 1 - s
    else:
        f_blk = lambda s: s
        r_blk = lambda s: nc - 1 - s
        st_blk = lambda s: s
    tok_f = pl.BlockSpec((c, hg * DN_HD), lambda g, s: (f_blk(s), g))
    tok_r = pl.BlockSpec((c, hg * DN_HD), lambda g, s: (r_blk(s), g))
    gate_f = pl.BlockSpec((None, hg, c, DN_HD), lambda g, s: (0, g, f_blk(s), 0))
    gate_r = pl.BlockSpec((None, hg, c, DN_HD), lambda g, s: (1, g, r_blk(s), 0))
    par = pl.BlockSpec((2, hg, 1, DN_HD), lambda g, s: (0, g, 0, 0))
    state = pl.BlockSpec((2, hg, None, DN_HD, DN_HD), lambda g, s: (0, g, st_blk(s), 0, 0))
    return tok_f, tok_r, gate_f, gate_r, par, state


def _dn_fwd(q, k, v, al, be, a_rows, dt_rows, name):
    t = q.shape[0]
    c, hg = DN_CHUNK, DN_HG
    nc = t // c
    tok_f, tok_r, gate_f, gate_r, par, state = _dn_views(nc, False)

    def body(qf, kf, vf, qr, kr, vr, alf, bef, alr, ber, a_ref, dt_ref, of_ref, or_ref, st_ref, s_ref):
        @pl.when(pl.program_id(1) == 0)
        def _():
            s_ref[...] = jnp.zeros_like(s_ref)

        (q_, k_, v_), gates = _dn_load((qf, kf, vf), (qr, kr, vr), alf, bef, alr, ber, a_ref, dt_ref)
        s = s_ref[...]
        st_ref[0] = s[:hg]
        st_ref[1] = s[hg:]
        out, s_new = _dn_chunk(q_, k_, v_, *gates, s)
        for h in range(hg):
            sl = slice(DN_HD * h, DN_HD * (h + 1))
            of_ref[:, sl] = out[h]
            or_ref[:, sl] = out[hg + h]
        s_ref[...] = s_new

    return pl.pallas_call(
        body, name=name, grid=(DN_HEADS // hg, nc),
        in_specs=[tok_f] * 3 + [tok_r] * 3 + [gate_f, gate_f, gate_r, gate_r, par, par],
        out_specs=[tok_f, tok_r, state],
        out_shape=[jax.ShapeDtypeStruct((t, TOK_W), F32)] * 2
        + [jax.ShapeDtypeStruct((2, DN_HEADS, nc, DN_HD, DN_HD), F32)],
        scratch_shapes=[pltpu.VMEM((2 * hg, DN_HD, DN_HD), F32)],
        compiler_params=_cparams(("parallel", "arbitrary")),
    )(q, k, v, q, k, v, al, be, al, be, a_rows, dt_rows)


def _dn_bwd(q, k, v, al, be, a_rows, dt_rows, states, do, name):
    t = q.shape[0]
    c, hg = DN_CHUNK, DN_HG
    assert hg == DN_HEADS
    nc = t // c
    tok_f, tok_r, gate_f, gate_r, par, state = _dn_views(nc, True)
    gout_f = pl.BlockSpec((c, DN_HD), lambda g, s: (nc - 1 - s, 0))
    gout_r = pl.BlockSpec((c, DN_HD), lambda g, s: (s, 0))

    def body(qf, kf, vf, qr, kr, vr, alf, bef, alr, ber, a_ref, dt_ref, st_ref, dof, dor,
             dqf, dkf, dvf, dqr, dkr, dvr, dgf, dgr, da_ref, ddt_ref, ds_ref):
        first = pl.program_id(1) == 0

        @pl.when(first)
        def _():
            ds_ref[...] = jnp.zeros_like(ds_ref)
            da_ref[...] = jnp.zeros_like(da_ref)
            ddt_ref[...] = jnp.zeros_like(ddt_ref)

        def lanes(x):
            return jnp.sum(x, axis=-1, keepdims=True)

        (q_, k_, v_, do_), gates = _dn_load((qf, kf, vf, dof), (qr, kr, vr, dor), alf, bef, alr, ber, a_ref, dt_ref)
        s = jnp.concatenate([st_ref[0], st_ref[1]], axis=0)
        _, vjp = jax.vjp(_dn_chunk, q_, k_, v_, *gates, s)
        dq, dk, dv, dal, dbe, dalc, da, ddt, dac, ddtc, ds = vjp((do_, ds_ref[...]))
        for h in range(hg):
            sl = slice(DN_HD * h, DN_HD * (h + 1))
            dqf[:, sl], dkf[:, sl], dvf[:, sl] = dq[h], dk[h], dv[h]
            dqr[:, sl], dkr[:, sl], dvr[:, sl] = dq[hg + h], dk[hg + h], dv[hg + h]
        dal, dbe = lanes(dal) + lanes(dalc), lanes(dbe)
        lane = lax.broadcasted_iota(jnp.int32, (c, DN_HD), 1)
        for d, dg_ref in enumerate((dgf, dgr)):
            dg = jnp.zeros((c, DN_HD), F32)
            for h in range(hg):
                dg = jnp.where(lane == h, dal[d * hg + h], jnp.where(lane == hg + h, dbe[d * hg + h], dg))
            dg_ref[...] = dg
        da = jnp.broadcast_to(lanes(da) + lanes(dac), da.shape)
        ddt = jnp.broadcast_to(lanes(ddt) + lanes(ddtc), ddt.shape)
        da_ref[0] += da[:hg]
        da_ref[1] += da[hg:]
        ddt_ref[0] += ddt[:hg]
        ddt_ref[1] += ddt[hg:]
        ds_ref[...] = ds

    tok = jax.ShapeDtypeStruct((t, TOK_W), F32)
    gate = jax.ShapeDtypeStruct((t, DN_HD), F32)
    parsh = jax.ShapeDtypeStruct((2, DN_HEADS, 1, DN_HD), F32)
    res = pl.pallas_call(
        body, name=name, grid=(DN_HEADS // hg, nc),
        in_specs=[tok_f] * 3 + [tok_r] * 3 + [gate_f, gate_f, gate_r, gate_r, par, par, state, tok_f, tok_r],
        out_specs=[tok_f] * 3 + [tok_r] * 3 + [gout_f, gout_r, par, par],
        out_shape=[tok] * 6 + [gate] * 2 + [parsh] * 2,
        scratch_shapes=[pltpu.VMEM((2 * hg, DN_HD, DN_HD), F32)],
        compiler_params=_cparams(("parallel", "arbitrary")),
    )(q, k, v, q, k, v, al, be, al, be, a_rows, dt_rows, states, do, do)
    dqf, dkf, dvf, dqr, dkr, dvr, dgf, dgr, da, ddt = res
    dgate = jnp.concatenate([dgf[:, :2 * DN_HEADS], dgr[:, :2 * DN_HEADS]], axis=1)
    return (dqf, dkf, dvf), (dqr, dkr, dvr), dgate, da, ddt


BAND_BQ = 256
ROW_TB = 256
MEM_TB = 512
CONV_RT = 512


def _to_sub(x, dil):
    l = x.shape[0] // dil
    return x.reshape(l, dil, 4, ATT_HD).transpose(1, 2, 0, 3)


def _from_sub(x, dil):
    l = x.shape[2]
    return x.transpose(2, 0, 1, 3).reshape(l * dil, 4 * ATT_HD)


def _heads_major(x):
    return x.reshape(x.shape[0], MEM_HEADS, ATT_HD).transpose(1, 0, 2)


def _heads_minor(x):
    return x.transpose(1, 0, 2).reshape(x.shape[1], MEM_HEADS * ATT_HD)


def _mem_kv_fwd(mem, gain, w_kv, li):
    (memn,) = _rowwise(_fn_pre, [mem], [gain], [(D, BF16)], mem.shape[0], f"memnorm_fwd{li}")
    kv = _matmul(memn, w_kv, "nn", F32, f"memkv_fwd{li}")
    return _heads_major(kv[:, :MEM_W]), _heads_major(kv[:, MEM_W:]), memn


def _mem_kv_bwd(mem, gain, w_kv, memn, dkm, dvm, li):
    dkv = jnp.concatenate([_heads_minor(dkm), _heads_minor(dvm)], axis=1).astype(BF16)
    dw = _matmul(memn, dkv, "tn", BF16, f"memkv_dw{li}")
    dmemn = _matmul(dkv, w_kv, "nt", F32, f"memkv_dx{li}")
    _, (dgain,) = _rowwise_bwd(_fn_pre, [mem], [gain], [dmemn], [None], mem.shape[0], f"memnorm_bwd{li}")
    return dw, dgain


def _attn_mixer_fwd(p, rel_bias, kv_fn):
    t = p.shape[0]
    saved, outs, lses = [], [], []
    for gi, (_, dil) in enumerate(DIL_GROUPS):
        l_sub = t // dil
        bq = min(BAND_BQ, l_sub)
        q = _to_sub(p[:, 256 * gi:256 * (gi + 1)], dil)
        pad = ((0, 0), (0, 0), (BAND_HALF, BAND_HALF), (0, 0))
        k = jnp.pad(_to_sub(p[:, TOK_W + 256 * gi:TOK_W + 256 * (gi + 1)], dil), pad)
        v = jnp.pad(_to_sub(p[:, 2 * TOK_W + 256 * gi:2 * TOK_W + 256 * (gi + 1)], dil), pad)
        bm = _band_bias(rel_bias, gi, dil, bq)
        o, lse = _band_fwd(q, k, v, bm, dil, l_sub, bq, f"band_fwd{gi}")
        outs.append(_from_sub(o, dil))
        lses.append(_from_sub(lse, dil))
        saved.append((q, k, v, bm))
    o_all = jnp.concatenate(outs, axis=1)
    lse_all = jnp.concatenate(lses, axis=1)
    (mixed,) = _rowwise(_fn_combine, [o_all, lse_all], [], [(TOK_W, BF16)], ROW_TB, "combine_fwd")
    qm = _heads_major(p[:, 3 * TOK_W:])
    km, vm, memn = kv_fn(mixed)
    memo = _mem_fwd(qm, km, vm, min(MEM_TB, t), "mem_fwd0")
    cat = jnp.concatenate([mixed, _heads_minor(memo).astype(BF16)], axis=1)
    return cat, (saved, o_all, lse_all, qm), (km, vm, memn)


def _attn_mixer_bwd(dcat, res, km, vm):
    saved, o_all, lse_all, qm = res
    t = dcat.shape[0]
    (do_all, dlse_all), _ = _rowwise_bwd(_fn_combine, [o_all, lse_all], [], [dcat[:, :TOK_W]], [BF16, F32],
                                         ROW_TB, "combine_bwd")
    dqs, dks, dvs, dbms = [], [], [], []
    for gi, (_, dil) in enumerate(DIL_GROUPS):
        l_sub = t // dil
        bq = min(BAND_BQ, l_sub)
        q, k, v, bm = saved[gi]
        do = _to_sub(do_all[:, 256 * gi:256 * (gi + 1)], dil)
        dl = _to_sub(dlse_all[:, 256 * gi:256 * (gi + 1)], dil)
        dq, dk, dv, dbm = _band_bwd(q, k, v, bm, do, dl, dil, l_sub, bq, f"band_bwd{gi}")
        dqs.append(_from_sub(dq, dil))
        dks.append(_from_sub(dk[:, :, BAND_HALF:-BAND_HALF], dil))
        dvs.append(_from_sub(dv[:, :, BAND_HALF:-BAND_HALF], dil))
        dbms.append(dbm)
    dqm, dkm, dvm = _mem_bwd(qm, km, vm, _heads_major(dcat[:, TOK_W:]), min(MEM_TB, t), "mem_bwd0")
    dp = jnp.concatenate([d.astype(BF16) for d in dqs + dks + dvs + [_heads_minor(dqm)]], axis=1)
    return dp, _relbias_grad(dbms, "relbias_grad"), dkm, dvm


def _dn_mixer_fwd(p, conv_w, a_log, dt_bias, out_norm, km, vm):
    t = p.shape[0]
    rt = min(CONV_RT, t)
    xp = p
    w8 = jnp.pad(conv_w, ((0, 8 - DN_CONV), (0, 0)))
    q = _conv_fwd(xp, w8, 0, rt, "conv_fwd_q")
    k = _conv_fwd(xp, w8, 1, rt, "conv_fwd_k")
    v = _conv_fwd(xp, w8, 2, rt, "conv_fwd_v")
    gate = p[:, 4 * TOK_W:4 * TOK_W + 4 * DN_HEADS].reshape(t, 2, 2, DN_HEADS)
    bshape = (2, DN_HEADS, t, DN_HD)
    al = jnp.broadcast_to(gate[:, :, 0, :].transpose(1, 2, 0)[..., None], bshape)
    be = jnp.broadcast_to(gate[:, :, 1, :].transpose(1, 2, 0)[..., None], bshape)
    a_rows = jnp.broadcast_to(a_log[:, :, None, None], (2, DN_HEADS, 1, DN_HD))
    dt_rows = jnp.broadcast_to(dt_bias[:, :, None, None], (2, DN_HEADS, 1, DN_HD))
    o_f, o_r, states = _dn_fwd(q, k, v, al, be, a_rows, dt_rows, "dn_fwd")
    z = p[:, 3 * TOK_W:4 * TOK_W]
    gain = out_norm.reshape(1, DN_HD)
    (og,) = _rowwise(_fn_outnorm, [o_f, o_r, z], [gain], [(TOK_W, BF16)], ROW_TB, "outnorm_fwd")
    qm = _heads_major(p[:, 4 * TOK_W + 4 * DN_HEADS:DN_IN])
    memo = _mem_fwd(qm, km, vm, min(MEM_TB, t), "mem_fwd1")
    cat = jnp.concatenate([og, _heads_minor(memo).astype(BF16)], axis=1)
    return cat, (xp, w8, q, k, v, al, be, a_rows, dt_rows, o_f, o_r, states, z, gain, qm)


def _dn_mixer_bwd(dcat, res, km, vm):
    xp, w8, q, k, v, al, be, a_rows, dt_rows, o_f, o_r, states, z, gain, qm = res
    t = dcat.shape[0]
    rt = min(CONV_RT, t)
    (do, dz), (dgain,) = _rowwise_bwd(_fn_outnorm, [o_f, o_r, z], [gain], [dcat[:, :TOK_W]], [F32, None, BF16],
                                      ROW_TB, "outnorm_bwd")
    d_f, d_r, dgate, da, ddt = _dn_bwd(q, k, v, al, be, a_rows, dt_rows, states, do, "dn_bwd")
    dqm, dkm, dvm = _mem_bwd(qm, km, vm, _heads_major(dcat[:, TOK_W:]), min(MEM_TB, t), "mem_bwd1")
    rest = jnp.concatenate([dgate.astype(BF16), _heads_minor(dqm).astype(BF16),
                            jnp.zeros((t, DN_IN_PAD - DN_IN), BF16)], axis=1)
    dp = lax.dynamic_update_slice(lax.empty((t, DN_IN_PAD), BF16), dz, (0, 3 * TOK_W))
    dp = lax.dynamic_update_slice(dp, rest, (0, 4 * TOK_W))
    dws = []
    for kind, nm in enumerate("qkv"):
        dp, dw = _conv_bwd(xp, w8, d_f[kind], d_r[kind], dp, kind, rt, f"conv_bwd_{nm}")
        dws.append(dw)
    dconv = jnp.concatenate(dws, axis=1)[:DN_CONV]
    return dp, dconv, da[:, :, 0, 0], ddt[:, :, 0, 0], dgain.reshape(DN_HD), dkm, dvm


SWI_TB = 256


def _ffn_fwd(h, w_gu_t, w_d, li):
    gu = _matmul(h, w_gu_t, "nt", BF16, f"ffn_gu{li}")
    (a,) = _rowwise(_fn_swiglu, [gu], [], [(D_FF, BF16)], SWI_TB, f"swiglu_fwd{li}")
    return _matmul(a, w_d, "nn", F32, f"ffn_down{li}"), gu, a


def _ffn_bwd(df, h, w_gu_t, w_d, gu, a, li):
    da = _matmul(df, w_d, "nt", BF16, f"ffn_down_dx{li}")
    dwd = _matmul(a, df, "tn", BF16, f"ffn_down_dw{li}")
    (dgu,), _ = _rowwise_bwd(_fn_swiglu, [gu], [], [da], [BF16], SWI_TB, f"swiglu_bwd{li}")
    dh = _matmul(dgu, w_gu_t, "nn", F32, f"ffn_gu_dx{li}")
    dwgu_t = _matmul(dgu, h, "tn", BF16, f"ffn_gu_dw{li}")
    return dh, dwgu_t, dwd


def _fn_first(x, g):
    return x, _rms(x, g)


def _me_xyc():
    return lax.axis_index("x"), lax.axis_index("y"), lax.axis_index("c")


def _flip(coords, k):
    x, y, c = coords
    return (1 - x if k & 4 else x, 1 - y if k & 2 else y, 1 - c if k & 1 else c)


def _index(coords):
    x, y, c = coords
    return 4 * x + 2 * y + c


def _window(ref, axis, size, d):
    idx = [slice(None)] * len(ref.shape)
    idx[axis] = pl.ds(pl.multiple_of(d * size, size), size)
    return ref.at[tuple(idx)]


def _comm_call(body, n, ins, out_shapes, name):
    hbm = pl.BlockSpec(memory_space=pl.ANY)
    return pl.pallas_call(
        body, name=name, in_specs=[hbm] * n, out_specs=[hbm] * n, out_shape=out_shapes,
        scratch_shapes=[pltpu.SemaphoreType.DMA((N_DEV - 1, n)), pltpu.SemaphoreType.DMA((N_DEV - 1, n)),
                        pltpu.SemaphoreType.DMA((n,))],
    )(*ins)


def _run_exchange(n, local, remote, send_sems, recv_sems):
    me = _me_xyc()
    locs = [local(p) for p in range(n)]
    for cp in locs:
        cp.start()
    sends = [remote(k, p, me, _flip(me, k)) for k in range(1, N_DEV) for p in range(n)]
    for cp in sends:
        cp.start()
    for k in range(1, N_DEV):
        for p in range(n):
            remote(k, p, _flip(me, k), me).wait_recv()
    for cp in sends:
        cp.wait_send()
    for cp in locs:
        cp.wait()


def _all_gather(shards, axes, name):
    n = len(shards)
    sizes = [s.shape[a] for s, a in zip(shards, axes)]

    def body(*refs):
        ins, outs = refs[:n], refs[n:2 * n]
        send_sems, recv_sems, loc_sems = refs[2 * n:]
        me = _me_xyc()

        def local(p):
            return pltpu.make_async_copy(ins[p], _window(outs[p], axes[p], sizes[p], _index(me)), loc_sems.at[p])

        def remote(k, p, owner, to):
            return pltpu.make_async_remote_copy(
                src_ref=ins[p], dst_ref=_window(outs[p], axes[p], sizes[p], _index(owner)),
                send_sem=send_sems.at[k - 1, p], recv_sem=recv_sems.at[k - 1, p], device_id=to, device_id_type=MESH)

        _run_exchange(n, local, remote, send_sems, recv_sems)

    def full(s, a):
        return s.shape[:a] + (N_DEV * s.shape[a],) + s.shape[a + 1:]

    return _comm_call(body, n, shards, [jax.ShapeDtypeStruct(full(s, a), s.dtype) for s, a in zip(shards, axes)], name)


def _exchange(fulls, axes, name):
    n = len(fulls)
    sizes = [None if a is None else f.shape[a] // N_DEV for f, a in zip(fulls, axes)]

    def part_shape(f, a):
        return f.shape if a is None else f.shape[:a] + (f.shape[a] // N_DEV,) + f.shape[a + 1:]

    def body(*refs):
        ins, outs = refs[:n], refs[n:2 * n]
        send_sems, recv_sems, loc_sems = refs[2 * n:]
        me = _me_xyc()

        def src(p, to):
            return ins[p] if axes[p] is None else _window(ins[p], axes[p], sizes[p], _index(to))

        def local(p):
            return pltpu.make_async_copy(src(p, me), outs[p].at[_index(me)], loc_sems.at[p])

        def remote(k, p, sender, to):
            return pltpu.make_async_remote_copy(
                src_ref=src(p, to), dst_ref=outs[p].at[_index(sender)],
                send_sem=send_sems.at[k - 1, p], recv_sem=recv_sems.at[k - 1, p], device_id=to, device_id_type=MESH)

        _run_exchange(n, local, remote, send_sems, recv_sems)

    return _comm_call(body, n, fulls,
                      [jax.ShapeDtypeStruct((N_DEV,) + part_shape(f, a), f.dtype) for f, a in zip(fulls, axes)], name)


_HBM = pl.BlockSpec(memory_space=pltpu.HBM)
_SEM = pl.BlockSpec(memory_space=pltpu.SEMAPHORE)
_EFFECT = pltpu.SideEffectType.DATAFLOW_SIDE_EFFECTING


def _in_hbm(a):
    return pltpu.with_memory_space_constraint(a, pltpu.HBM)


def _split_start(srcs, lands, after, descr, name):
    n = len(srcs)

    def body(*refs):
        ins, lnd = refs[:n], refs[n:2 * n]
        send_sems, recv_sems = refs[2 * n + 1], refs[2 * n + 2]
        token = refs[-1]
        me = _me_xyc()
        for k in range(1, N_DEV):
            for p in range(n):
                descr(k, p, ins, lnd, send_sems, recv_sems, me, _flip(me, k)).start()
        token[...] = jnp.zeros_like(token)

    sems = pltpu.SemaphoreType.DMA(((N_DEV - 1) * n,))
    res = pl.pallas_call(
        body, name=name,
        out_shape=(sems, sems, *[pltpu.HBM(a.shape, a.dtype) for a in (*srcs, *lands)],
                   jax.ShapeDtypeStruct((8, 128), F32)),
        in_specs=[_HBM] * (2 * n) + [pl.BlockSpec(memory_space=pl.ANY)],
        out_specs=(_SEM, _SEM, *[_HBM] * (2 * n), pl.BlockSpec(memory_space=pltpu.VMEM)),
        input_output_aliases={i: 2 + i for i in range(2 * n)},
        compiler_params=pltpu.CompilerParams(has_side_effects=_EFFECT),
    )(*[_in_hbm(a) for a in (*srcs, *lands)], after)
    return res[0], res[1], res[2:2 + n], res[2 + n:2 + 2 * n], res[-1]


def _split_wait(send_sems, recv_sems, srcs, lands, after, descr, name):
    n = len(srcs)

    def body(*refs):
        ins, lnd = refs[:n], refs[n:2 * n]
        s_sems, r_sems = refs[2 * n], refs[2 * n + 1]
        me = _me_xyc()
        for k in range(1, N_DEV):
            for p in range(n):
                peer = _flip(me, k)
                descr(k, p, ins, lnd, s_sems, r_sems, me, peer).wait_send()
                descr(k, p, ins, lnd, s_sems, r_sems, peer, me).wait_recv()

    res = pl.pallas_call(
        body, name=name,
        out_shape=tuple(pltpu.HBM(a.shape, a.dtype) for a in (*srcs, *lands)),
        in_specs=[_HBM] * (2 * n) + [_SEM, _SEM, pl.BlockSpec(memory_space=pl.ANY)],
        out_specs=tuple([_HBM] * (2 * n)),
        input_output_aliases={i: i for i in range(2 * n)},
        compiler_params=pltpu.CompilerParams(has_side_effects=_EFFECT),
    )(*srcs, *lands, send_sems, recv_sems, after)
    return list(res[n:])


def _gather_descr(axes, sizes):
    def descr(k, p, ins, lnd, send_sems, recv_sems, sender, dest):
        return pltpu.make_async_remote_copy(
            src_ref=ins[p], dst_ref=_window(lnd[p], axes[p], sizes[p], _index(sender)),
            send_sem=send_sems.at[(k - 1) * len(axes) + p], recv_sem=recv_sems.at[(k - 1) * len(axes) + p],
            device_id=dest, device_id_type=MESH)
    return descr


def _exchange_descr(axes, sizes):
    def descr(k, p, ins, lnd, send_sems, recv_sems, sender, dest):
        return pltpu.make_async_remote_copy(
            src_ref=_window(ins[p], axes[p], sizes[p], _index(dest)), dst_ref=lnd[p].at[_index(sender)],
            send_sem=send_sems.at[(k - 1) * len(axes) + p], recv_sem=recv_sems.at[(k - 1) * len(axes) + p],
            device_id=dest, device_id_type=MESH)
    return descr


def _gather_begin(shards, axes, after, name):
    sizes = [s.shape[a] for s, a in zip(shards, axes)]
    me = _index(_me_xyc())
    lands = []
    for s, a, sz in zip(shards, axes, sizes):
        full = s.shape[:a] + (N_DEV * sz,) + s.shape[a + 1:]
        lands.append(lax.dynamic_update_slice_in_dim(lax.empty(full, s.dtype), s, me * sz, a))
    descr = _gather_descr(axes, sizes)
    send_sems, recv_sems, srcs, lands, token = _split_start(shards, lands, after, descr, name)
    return (send_sems, recv_sems, srcs, lands, descr), token


def _exchange_begin(fulls, axes, after, name):
    sizes = [f.shape[a] // N_DEV for f, a in zip(fulls, axes)]
    me = _index(_me_xyc())
    lands = []
    for f, a, sz in zip(fulls, axes, sizes):
        own = lax.dynamic_slice_in_dim(f, me * sz, sz, a)
        lands.append(lax.dynamic_update_slice_in_dim(lax.empty((N_DEV,) + own.shape, f.dtype), own[None], me, 0))
    descr = _exchange_descr(axes, sizes)
    send_sems, recv_sems, srcs, lands, token = _split_start(fulls, lands, after, descr, name)
    return (send_sems, recv_sems, srcs, lands, descr), token


def _split_end(handle, after, name):
    send_sems, recv_sems, srcs, lands, descr = handle
    return _split_wait(send_sems, recv_sems, srcs, lands, after, descr, name)


def _adam_math(g, w, m, v):
    m = ADAM_B1 * m + (1.0 - ADAM_B1) * g
    v = ADAM_B2 * v + (1.0 - ADAM_B2) * (g * g)
    m_hat = m / (1.0 - ADAM_B1 ** ADAM_STEP)
    v_hat = v / (1.0 - ADAM_B2 ** ADAM_STEP)
    delta = -ADAM_LR * (m_hat / (jnp.sqrt(v_hat) + ADAM_EPS) + ADAM_WD * w)
    return delta, m, v


def _sum_slabs(r_ref):
    g = r_ref[0].astype(F32)
    for s in range(1, N_DEV):
        g = g + r_ref[s].astype(F32)
    return g


def _adamw_reduce(recv, w, m, v, tb, name):
    r, c = w.shape

    def body(r_ref, w_ref, m_ref, v_ref, g_ref, d_ref, nm_ref, nv_ref):
        g = _sum_slabs(r_ref)
        g_ref[...] = g
        d_ref[...], nm_ref[...], nv_ref[...] = _adam_math(g, w_ref[...], m_ref[...], v_ref[...])

    blk = pl.BlockSpec((tb, c), lambda i: (i, 0))
    return pl.pallas_call(
        body, name=name, grid=(r // tb,),
        in_specs=[pl.BlockSpec((N_DEV, tb, c), lambda i: (0, i, 0)), blk, blk, blk],
        out_specs=[blk] * 4, out_shape=[jax.ShapeDtypeStruct((r, c), F32)] * 4,
        compiler_params=_cparams(("parallel",)),
    )(recv, w, m, v)


def _reduce8(recv, tb, name):
    r, c = recv.shape[1:]

    def body(r_ref, g_ref):
        g_ref[...] = _sum_slabs(r_ref)

    return pl.pallas_call(
        body, name=name, grid=(r // tb,),
        in_specs=[pl.BlockSpec((N_DEV, tb, c), lambda i: (0, i, 0))],
        out_specs=pl.BlockSpec((tb, c), lambda i: (i, 0)), out_shape=jax.ShapeDtypeStruct((r, c), F32),
        compiler_params=_cparams(("parallel",)),
    )(recv)


def _adamw(g, w, m, v, tb, name):
    r, c = w.shape

    def body(g_ref, w_ref, m_ref, v_ref, d_ref, nm_ref, nv_ref):
        d_ref[...], nm_ref[...], nv_ref[...] = _adam_math(g_ref[...], w_ref[...], m_ref[...], v_ref[...])

    blk = pl.BlockSpec((tb, c), lambda i: (i, 0))
    return pl.pallas_call(
        body, name=name, grid=(r // tb,), in_specs=[blk] * 4, out_specs=[blk] * 3,
        out_shape=[jax.ShapeDtypeStruct((r, c), F32)] * 3, compiler_params=_cparams(("parallel",)),
    )(g, w, m, v)


DN_IN_SHARD = DN_IN // N_DEV
DN_IN_SHARD_PAD = 432
CONV_SHARD = (1, DN_CONV, 288)


def _pack_small(arrs, rows):
    flat = jnp.concatenate([a.astype(F32).reshape(-1) for a in arrs])
    return jnp.pad(flat, (0, rows * PACK_C - flat.size)).reshape(rows, PACK_C)


def _unpack_small(packed, shapes):
    flat, out, off = packed.reshape(-1), [], 0
    for shp in shapes:
        n = int(np.prod(shp))
        out.append(flat[off:off + n].reshape(shp))
        off += n
    return out


def kernel(x, mem, rel_bias, att_w_in, att_w_out, dn_w_in, dn_conv, dn_a_log, dn_dt_bias, dn_out_norm, dn_w_out, mem_norm, mem_w_kv, norm_mix_pre, norm_mix_post, norm_ffn_pre, norm_ffn_post, ffn_w_gate_up, ffn_w_down, loss_target, m_rel_bias, m_att_w_in, m_att_w_out, m_dn_w_in, m_dn_conv, m_dn_a_log, m_dn_dt_bias, m_dn_out_norm, m_dn_w_out, m_mem_norm, m_mem_w_kv, m_norm_mix_pre, m_norm_mix_post, m_norm_ffn_pre, m_norm_ffn_post, m_ffn_w_gate_up, m_ffn_w_down, v_rel_bias, v_att_w_in, v_att_w_out, v_dn_w_in, v_dn_conv, v_dn_a_log, v_dn_dt_bias, v_dn_out_norm, v_dn_w_out, v_mem_norm, v_mem_w_kv, v_norm_mix_pre, v_norm_mix_post, v_norm_ffn_pre, v_norm_ffn_post, v_ffn_w_gate_up, v_ffn_w_down):
    x0, mem0, tgt = x[0], mem[0], loss_target[0]
    t = x0.shape[0]
    axes = ("x", "y", "c")

    def t_shard(w):
        return jnp.swapaxes(w, 1, 2).astype(BF16)

    dn_in_pad = ((0, 0), (0, DN_IN_SHARD_PAD - DN_IN_SHARD), (0, 0))
    (w_att_in_t,) = _all_gather([t_shard(att_w_in)], [1], "allgather_first")
    w_att_in_t = w_att_in_t[0]
    gu_t, down = t_shard(ffn_w_gate_up), ffn_w_down.astype(BF16)
    gather_o, tok_o = _gather_begin([att_w_out.astype(BF16), mem_w_kv.astype(BF16)], [1, 1], w_att_in_t,
                                    "gather_att_out_start")
    gather_a, tok_a = _gather_begin([gu_t[0:1], down[0:1]], [1, 1], tok_o, "gather_ffn0_start")
    gather_b, tok_b = _gather_begin(
        [jnp.pad(t_shard(dn_w_in), dn_in_pad), dn_w_out.astype(BF16), gu_t[1:2], down[1:2], dn_conv],
        [1, 1, 1, 1, 0], tok_a, "gather_layer1_start")

    def gain(a, i):
        return a[i].reshape(1, D)

    (h0,) = _rowwise(_fn_pre, [x0], [gain(norm_mix_pre, 0) + tok_b[0:1, 0:1]], [(D, BF16)], ROW_TB, "pre0")
    p0 = _matmul(h0, w_att_in_t, "nt", BF16, "att_in")
    late = {}

    def kv0(after):
        late["w_att_out"], late["w_kv"] = _split_end(gather_o, after, "gather_att_out_wait")
        return _mem_kv_fwd(mem0, gain(mem_norm, 0), late["w_kv"][0], 0)

    cat0, res0, (km0, vm0, memn0) = _attn_mixer_fwd(p0, rel_bias, kv0)
    w_att_out, w_kv = late["w_att_out"][0], late["w_kv"]
    y0 = _matmul(cat0, w_att_out, "nn", F32, "att_out")
    g_a = [gain(norm_mix_post, 0), gain(norm_ffn_pre, 0)]
    x1, h1 = _rowwise(_fn_res_pre, [x0, y0], g_a, [(D, F32), (D, BF16)], ROW_TB, "res_pre0")
    w_gu_t0, w_down0 = [w[0] for w in _split_end(gather_a, h1, "gather_ffn0_wait")]
    f0, gu0, a0 = _ffn_fwd(h1, w_gu_t0, w_down0, 0)
    g_b = [gain(norm_ffn_post, 0), gain(norm_mix_pre, 1)]
    x2, h2 = _rowwise(_fn_res_pre, [x1, f0], g_b, [(D, F32), (D, BF16)], ROW_TB, "res_pre1")
    km1, vm1, memn1 = _mem_kv_fwd(mem0, gain(mem_norm, 1), w_kv[1], 1)
    w_dn_in_g, w_dn_out, w_gu_t1, w_down1, conv_g = _split_end(gather_b, h2, "gather_layer1_wait")
    w_dn_in_g, w_dn_out, w_gu_t1, w_down1 = w_dn_in_g[0], w_dn_out[0], w_gu_t1[0], w_down1[0]
    conv_full = conv_g.transpose(1, 0, 2).reshape(DN_CONV, 3 * TOK_W)
    w_dn_in_t = jnp.concatenate(
        [w_dn_in_g[DN_IN_SHARD_PAD * j:DN_IN_SHARD_PAD * j + DN_IN_SHARD] for j in range(N_DEV)]
        + [jnp.zeros((DN_IN_PAD - DN_IN, D), BF16)], axis=0)
    p1 = _matmul(h2, w_dn_in_t, "nt", F32, "dn_in")
    cat1, res1 = _dn_mixer_fwd(p1, conv_full, dn_a_log[0], dn_dt_bias[0], dn_out_norm[0], km1, vm1)
    y1 = _matmul(cat1, w_dn_out, "nn", F32, "dn_out")
    g_c = [gain(norm_mix_post, 1), gain(norm_ffn_pre, 1)]
    x3, h3 = _rowwise(_fn_res_pre, [x2, y1], g_c, [(D, F32), (D, BF16)], ROW_TB, "res_pre2")
    f1, gu1, a1 = _ffn_fwd(h3, w_gu_t1, w_down1, 1)
    g_d = [gain(norm_ffn_post, 1)]
    (x4,) = _rowwise(_fn_res, [x3, f1], g_d, [(D, F32)], ROW_TB, "res3")
    dx4, lrow = _loss_kernel(x4, tgt, ROW_TB, "loss")
    loss = lax.psum(lrow[0, 0] * (0.5 / D), axes)

    (df1,), (dg_fpost1,) = _rowwise_bwd(_fn_res, [x3, f1], g_d, [dx4], [None, BF16], ROW_TB, "res3_bwd")
    dh3, dwgu1, dwd1 = _ffn_bwd(df1, h3, w_gu_t1, w_down1, gu1, a1, 1)
    (dx2, dy1), (dg_mpost1, dg_fpre1) = _rowwise_bwd(_fn_res_pre, [x2, y1], g_c, [dx4, dh3], [F32, BF16],
                                                     ROW_TB, "res_pre2_bwd")
    dcat1 = _matmul(dy1, w_dn_out, "nt", F32, "dn_out_dx")
    dw_dn_out = _matmul(cat1, dy1, "tn", BF16, "dn_out_dw")
    dp1, dconv, da_log, ddt_bias, dout_norm, dkm1, dvm1 = _dn_mixer_bwd(dcat1, res1, km1, vm1)
    dwkv1, dg_mem1 = _mem_kv_bwd(mem0, gain(mem_norm, 1), w_kv[1], memn1, dkm1, dvm1, 1)
    dh2 = _matmul(dp1, w_dn_in_t, "nn", F32, "dn_in_dx")
    dw_dn_in_t = _matmul(dp1, h2, "tn", BF16, "dn_in_dw")
    dn_in_parts = [jnp.pad(dw_dn_in_t[DN_IN_SHARD * j:DN_IN_SHARD * (j + 1)],
                           ((0, DN_IN_SHARD_PAD - DN_IN_SHARD), (0, 0))) for j in range(N_DEV)]
    xch_b, tok = _exchange_begin(
        [jnp.concatenate(dn_in_parts, axis=0)[None], dw_dn_out[None], dwkv1[None], dwgu1[None], dwd1[None]],
        [1, 1, 1, 1, 1], dh2, "exchange_layer1_start")
    (dx1, df0), (dg_fpost0, dg_mpre1) = _rowwise_bwd(_fn_res_pre, [x1, f0], [g + tok[0:1, 0:1] for g in g_b],
                                                     [dx2, dh2], [F32, BF16], ROW_TB, "res_pre1_bwd")
    dh1, dwgu0, dwd0 = _ffn_bwd(df0, h1, w_gu_t0, w_down0, gu0, a0, 0)
    xch_a, tok = _exchange_begin([dwgu0[None], dwd0[None]], [1, 1], dh1, "exchange_ffn0_start")
    (dx0, dy0), (dg_mpost0, dg_fpre0) = _rowwise_bwd(_fn_res_pre, [x0, y0], [g + tok[0:1, 0:1] for g in g_a],
                                                     [dx1, dh1], [F32, BF16], ROW_TB, "res_pre0_bwd")
    dcat0 = _matmul(dy0, w_att_out, "nt", F32, "att_out_dx")
    dw_att_out = _matmul(cat0, dy0, "tn", BF16, "att_out_dw")
    dp0, drel, dkm0, dvm0 = _attn_mixer_bwd(dcat0, res0, km0, vm0)
    dwkv0, dg_mem0 = _mem_kv_bwd(mem0, gain(mem_norm, 0), w_kv[0], memn0, dkm0, dvm0, 0)
    xch_o, tok = _exchange_begin([dw_att_out[None], dwkv0[None]], [1, 1], dp0, "exchange_att_out_start")
    dw_att_in_t = _matmul(dp0, h0, "tn", BF16, "att_in_dw")
    xch_i, tok_i = _exchange_begin([dw_att_in_t[None]], [1], tok, "exchange_att_in_start")
    dh0 = _matmul(dp0, w_att_in_t, "nn", F32, "att_in_dx")
    (grad_x,), (dg_mpre0,) = _rowwise_bwd(_fn_first, [x0], [gain(norm_mix_pre, 0) + tok_i[0:1, 0:1]], [dx0, dh0],
                                          [F32], ROW_TB, "pre0_bwd")

    small_grads = [drel, da_log, ddt_bias, dout_norm, jnp.concatenate([dg_mem0, dg_mem1]),
                   jnp.concatenate([dg_mpre0, dg_mpre1]), jnp.concatenate([dg_mpost0, dg_mpost1]),
                   jnp.concatenate([dg_fpre0, dg_fpre1]), jnp.concatenate([dg_fpost0, dg_fpost1]), dconv]
    (r_small,) = _exchange([_pack_small(small_grads, SMALL_ROWS)], [None], "exchange_last")
    (r_att_in,) = _split_end(xch_i, r_small, "exchange_att_in_wait")
    r_att_out, r_kv0 = _split_end(xch_o, r_small, "exchange_att_out_wait")
    r_gu0, r_down0 = _split_end(xch_a, r_small, "exchange_ffn0_wait")
    r_dn_in, r_dn_out, r_kv1, r_gu1, r_down1 = _split_end(xch_b, r_small, "exchange_layer1_wait")

    def rows(a):
        return a.reshape((-1,) + a.shape[-1:])

    def row_sharded(recv, w, m, v, tb, name):
        outs = _adamw_reduce(recv.reshape((N_DEV, -1) + recv.shape[-1:]), rows(w), rows(m), rows(v), tb, name)
        return [o.reshape(w.shape) for o in outs]

    def col_sharded(recv, w, m, v, tb, name):
        g_t = _reduce8(recv.reshape((N_DEV, -1) + recv.shape[-1:]), tb, name + "_sum")
        g = jnp.swapaxes(g_t.reshape(recv.shape[1:])[:, :w.shape[2]], 1, 2)
        outs = _adamw(rows(g), rows(w), rows(m), rows(v), 256, name)
        return [g] + [o.reshape(w.shape) for o in outs]

    def per_layer(fn, recvs, w, m, v, tb, name):
        outs = [fn(r, w[l:l + 1], m[l:l + 1], v[l:l + 1], tb, f"{name}{l}") for l, r in enumerate(recvs)]
        return [jnp.concatenate(pair, axis=0) for pair in zip(*outs)]

    big = [col_sharded(r_att_in, att_w_in, m_att_w_in, v_att_w_in, 320, "adamw_att_in"),
           row_sharded(r_att_out, att_w_out, m_att_w_out, v_att_w_out, 128, "adamw_att_out"),
           col_sharded(r_dn_in, dn_w_in, m_dn_w_in, v_dn_w_in, 432, "adamw_dn_in"),
           row_sharded(r_dn_out, dn_w_out, m_dn_w_out, v_dn_w_out, 128, "adamw_dn_out"),
           per_layer(row_sharded, [r_kv0, r_kv1], mem_w_kv, m_mem_w_kv, v_mem_w_kv, 128, "adamw_mem_kv"),
           per_layer(col_sharded, [r_gu0, r_gu1], ffn_w_gate_up, m_ffn_w_gate_up, v_ffn_w_gate_up, 176,
                     "adamw_ffn_gu"),
           per_layer(row_sharded, [r_down0, r_down1], ffn_w_down, m_ffn_w_down, v_ffn_w_down, 176,
                     "adamw_ffn_down")]
    g_big, d_big, nm_big, nv_big = [[b[i] for b in big] for i in range(4)]

    g_small = _reduce8(r_small, SMALL_ROWS, "reduce_small")
    rep_shapes = [(32, 12), (1, 2, 6), (1, 2, 6), (1, 128), (2, D), (2, D), (2, D), (2, D), (2, D)]
    *g_rep, g_conv_full = _unpack_small(g_small, rep_shapes + [(DN_CONV, 3 * TOK_W)])
    me = _index(_me_xyc())
    g_conv = lax.dynamic_slice(g_conv_full, (0, me * 288), (DN_CONV, 288)).reshape(CONV_SHARD)
    small_shapes = rep_shapes + [CONV_SHARD]
    small_w = [rel_bias, dn_a_log, dn_dt_bias, dn_out_norm, mem_norm, norm_mix_pre, norm_mix_post,
               norm_ffn_pre, norm_ffn_post, dn_conv]
    small_m = [m_rel_bias, m_dn_a_log, m_dn_dt_bias, m_dn_out_norm, m_mem_norm, m_norm_mix_pre, m_norm_mix_post,
               m_norm_ffn_pre, m_norm_ffn_post, m_dn_conv]
    small_v = [v_rel_bias, v_dn_a_log, v_dn_dt_bias, v_dn_out_norm, v_mem_norm, v_norm_mix_pre, v_norm_mix_post,
               v_norm_ffn_pre, v_norm_ffn_post, v_dn_conv]
    g_small_list = g_rep + [g_conv]
    outs_small = _adamw(_pack_small(g_small_list, 24), _pack_small(small_w, 24), _pack_small(small_m, 24),
                        _pack_small(small_v, 24), 24, "adamw_small")
    d_small, nm_small, nv_small = [_unpack_small(o, small_shapes) for o in outs_small]

    def ordered(small, big):
        return [small[0], big[0], big[1], big[2], small[9], small[1], small[2], small[3], big[3], small[4],
                big[4], small[5], small[6], small[7], small[8], big[5], big[6]]

    g_small_out = [g.reshape(s) for g, s in zip(g_small_list, small_shapes)]
    return (loss, grad_x[None], *ordered(g_small_out, g_big), *ordered(d_small, d_big),
            *ordered(nm_small, nm_big), *ordered(nv_small, nv_big))
```

```python
import functools
import math
from typing import NamedTuple

import numpy as np
import jax
import jax.numpy as jnp
from jax import lax
from jax.experimental import pallas as pl
from jax.experimental.pallas import tpu as pltpu

F32 = jnp.float32
BF16 = jnp.bfloat16
HI = lax.Precision.HIGHEST
MESH = pl.DeviceIdType.MESH

N_DEV = 8
D = 1024
EPS = 1e-6
NEG = -1e30
TOK_W = 768
MEM_W = 256
ATT_HD = 64
DIL_GROUPS = ((128, 1), (512, 4), (2048, 16))
BAND_HALF = 64
REL_BUCKETS = 32
REL_MAX_DIST = 1024
DN_HD = 128
DN_HEADS = 6
DN_CONV = 5
DN_CHUNK = 64
MEM_HEADS = 4
D_FF = 2816
DN_IN = 3352
DN_IN_PAD = 3456

ADAM_LR, ADAM_B1, ADAM_B2, ADAM_EPS, ADAM_WD, ADAM_STEP = 0.001, 0.9, 0.999, 1e-08, 0.01, 10

PACK_C = 512
SMALL_ROWS = 48
VMEM_LIMIT = 48 * 1024 * 1024


def _cparams(sem=None):
    kw = dict(vmem_limit_bytes=VMEM_LIMIT)
    if sem is not None:
        kw["dimension_semantics"] = sem
    return pltpu.CompilerParams(**kw)


def _tile(n, cap):
    if n <= cap:
        return n
    best = None
    for t in range(128, cap + 1, 128):
        if n % t == 0:
            best = t
    assert best is not None, (n, cap)
    return best


def _matmul(a, b, mode, out_dtype, name, tm=1024, tn=1408, tk=None):
    if tk is None:
        tk = 4096 if mode == "tn" else 2816
    if mode == "tn":
        tm = min(tm, 512)
    if mode == "nn":
        (m, kc), (_, n) = a.shape, b.shape
        dims = (((1,), (0,)), ((), ()))
    elif mode == "nt":
        (m, kc), (n, _) = a.shape, b.shape
        dims = (((1,), (1,)), ((), ()))
    else:
        (kc, m), (_, n) = a.shape, b.shape
        dims = (((0,), (0,)), ((), ()))
    tm = m if m <= tm else _tile(m, tm)
    tn = _tile(n, tn)
    tk = _tile(kc, tk)
    nk = kc // tk

    def body(a_ref, b_ref, o_ref, acc_ref):
        k = pl.program_id(2)
        part = lax.dot_general(a_ref[...], b_ref[...], dims, preferred_element_type=F32)

        @pl.when(k == 0)
        def _():
            acc_ref[...] = part

        @pl.when(k > 0)
        def _():
            acc_ref[...] += part

        @pl.when(k == nk - 1)
        def _():
            o_ref[...] = acc_ref[...].astype(o_ref.dtype)

    if mode == "nn":
        a_spec = pl.BlockSpec((tm, tk), lambda i, j, k: (i, k))
        b_spec = pl.BlockSpec((tk, tn), lambda i, j, k: (k, j))
    elif mode == "nt":
        a_spec = pl.BlockSpec((tm, tk), lambda i, j, k: (i, k))
        b_spec = pl.BlockSpec((tn, tk), lambda i, j, k: (j, k))
    else:
        a_spec = pl.BlockSpec((tk, tm), lambda i, j, k: (k, i))
        b_spec = pl.BlockSpec((tk, tn), lambda i, j, k: (k, j))
    return pl.pallas_call(
        body, name=name, grid=(m // tm, n // tn, nk),
        in_specs=[a_spec, b_spec],
        out_specs=pl.BlockSpec((tm, tn), lambda i, j, k: (i, j)),
        out_shape=jax.ShapeDtypeStruct((m, n), out_dtype),
        scratch_shapes=[pltpu.VMEM((tm, tn), F32)],
        compiler_params=_cparams(("parallel", "parallel", "arbitrary")),
    )(a, b)


class _Cols(NamedTuple):
    arr: jax.Array
    width: int
    block: int

    @property
    def shape(self):
        return (self.arr.shape[0], self.width)


def _row_spec(r, tb):
    if isinstance(r, _Cols):
        return pl.BlockSpec((tb, r.width), lambda i, b=r.block: (i, b))
    return pl.BlockSpec((tb, r.shape[1]), lambda i: (i, 0))


def _row_arr(r):
    return r.arr if isinstance(r, _Cols) else r


def _rowwise(fn, rows, params, outs, tb, name):
    t = rows[0].shape[0]
    nr, npar = len(rows), len(params)

    def body(*refs):
        ins = [r[...].astype(F32) for r in refs[:nr + npar]]
        res = fn(*ins)
        for o_ref, r in zip(refs[nr + npar:], res):
            o_ref[...] = r.astype(o_ref.dtype)

    return pl.pallas_call(
        body, name=name, grid=(t // tb,),
        in_specs=[_row_spec(r, tb) for r in rows] + [pl.BlockSpec(p.shape, lambda i: (0, 0)) for p in params],
        out_specs=[pl.BlockSpec((tb, c), lambda i: (i, 0)) for c, _ in outs],
        out_shape=[jax.ShapeDtypeStruct((t, c), dt) for c, dt in outs],
        compiler_params=_cparams(("parallel",)),
    )(*[_row_arr(r) for r in rows], *params)


def _rowwise_bwd(fn, rows, params, cots, row_grad, tb, name):
    t = rows[0].shape[0]
    nr, npar, nc = len(rows), len(params), len(cots)
    want = [i for i, g in enumerate(row_grad) if g is not None]

    def body(*refs):
        ins = [r[...].astype(F32) for r in refs[:nr + npar]]
        cts = tuple(r[...].astype(F32) for r in refs[nr + npar:nr + npar + nc])
        outs = refs[nr + npar + nc:]
        _, vjp = jax.vjp(fn, *ins)
        grads = vjp(cts)
        for o_ref, i in zip(outs[:len(want)], want):
            o_ref[...] = grads[i].astype(o_ref.dtype)
        first = pl.program_id(0) == 0
        for o_ref, g in zip(outs[len(want):], grads[nr:]):
            @pl.when(first)
            def _(o_ref=o_ref, g=g):
                o_ref[...] = g

            @pl.when(jnp.logical_not(first))
            def _(o_ref=o_ref, g=g):
                o_ref[...] += g

    res = pl.pallas_call(
        body, name=name, grid=(t // tb,),
        in_specs=[_row_spec(r, tb) for r in rows] + [pl.BlockSpec(p.shape, lambda i: (0, 0)) for p in params]
        + [_row_spec(c, tb) for c in cots],
        out_specs=[pl.BlockSpec((tb, rows[i].shape[1]), lambda i_: (i_, 0)) for i in want]
        + [pl.BlockSpec(p.shape, lambda i: (0, 0)) for p in params],
        out_shape=[jax.ShapeDtypeStruct(tuple(rows[i].shape), row_grad[i]) for i in want]
        + [jax.ShapeDtypeStruct(p.shape, F32) for p in params],
        compiler_params=_cparams(("arbitrary",)),
    )(*[_row_arr(r) for r in rows], *params, *[_row_arr(c) for c in cots])
    return list(res[:len(want)]), list(res[len(want):])


def _rms(x, g):
    return x * lax.rsqrt(jnp.mean(x * x, axis=-1, keepdims=True) + EPS) * g


def _fn_pre(x, g):
    return (_rms(x, g),)


def _fn_res_pre(x, y, g_post, g_pre):
    x1 = x + _rms(y, g_post)
    return x1, _rms(x1, g_pre)


def _fn_res(x, y, g_post):
    return (x + _rms(y, g_post),)


def _sigmoid(x):
    return 1.0 / (1.0 + jnp.exp(-x))


def _silu(x):
    return x * _sigmoid(x)


def _fn_swiglu(gu):
    return (_silu(gu[:, :D_FF]) * gu[:, D_FF:],)


def _fn_combine(o, lse):
    ls = [lse[:, 256 * g:256 * (g + 1)] for g in range(3)]
    mx = lax.stop_gradient(jnp.maximum(jnp.maximum(ls[0], ls[1]), ls[2]))
    es = [jnp.exp(l - mx) for l in ls]
    inv = 1.0 / (es[0] + es[1] + es[2])
    return (jnp.concatenate([o[:, 256 * g:256 * (g + 1)] * (es[g] * inv) for g in range(3)], axis=1),)


def _fn_outnorm(o_f, o_r, z, gain):
    res = []
    for h in range(DN_HEADS):
        sl = slice(DN_HD * h, DN_HD * (h + 1))
        o = o_f[:, sl] + o_r[:, sl]
        res.append(o * lax.rsqrt(jnp.mean(o * o, axis=-1, keepdims=True) + EPS) * gain * _silu(z[:, sl]))
    return (jnp.concatenate(res, axis=1),)


def _loss_kernel(x, tgt, tb, name):
    t, d = x.shape

    def body(x_ref, t_ref, dx_ref, l_ref, acc_ref):
        i = pl.program_id(0)
        e = x_ref[...] - t_ref[...]
        dx_ref[...] = e * (1.0 / d)
        part = jnp.sum(e * e, axis=0, keepdims=True)

        @pl.when(i == 0)
        def _():
            acc_ref[...] = part

        @pl.when(i > 0)
        def _():
            acc_ref[...] += part

        @pl.when(i == t // tb - 1)
        def _():
            l_ref[...] = jnp.broadcast_to(jnp.sum(acc_ref[...], axis=-1, keepdims=True), (1, 128))

    return pl.pallas_call(
        body, name=name, grid=(t // tb,),
        in_specs=[pl.BlockSpec((tb, d), lambda i: (i, 0))] * 2,
        out_specs=[pl.BlockSpec((tb, d), lambda i: (i, 0)), pl.BlockSpec((1, 128), lambda i: (0, 0))],
        out_shape=[jax.ShapeDtypeStruct((t, d), F32), jax.ShapeDtypeStruct((1, 128), F32)],
        scratch_shapes=[pltpu.VMEM((1, d), F32)],
        compiler_params=_cparams(("arbitrary",)),
    )(x, tgt)


def _band_fn(l_sub, bq, i, q, kw, vw, bm):
    w = bq + 2 * BAND_HALF
    s = lax.dot_general((q * (ATT_HD ** -0.5)).astype(BF16), kw.astype(BF16), (((2,), (2,)), ((0,), (0,))),
                        preferred_element_type=F32) + bm
    kpos = i * bq - BAND_HALF + lax.broadcasted_iota(jnp.int32, (4, bq, w), 2)
    s = jnp.where((kpos >= 0) & (kpos < l_sub), s, NEG)
    m = lax.stop_gradient(jnp.max(s, axis=-1, keepdims=True))
    p = jnp.exp(s - m)
    den = jnp.sum(p, axis=-1, keepdims=True)
    o = lax.dot_general(p.astype(BF16), vw.astype(BF16), (((2,), (1,)), ((0,), (0,))),
                        preferred_element_type=F32) / den
    return o, jnp.broadcast_to(m + jnp.log(den), o.shape)


def _band_specs(l_sub, bq):
    w = bq + 2 * BAND_HALF
    qs = pl.BlockSpec((None, 4, bq, ATT_HD), lambda r, i: (r, 0, i, 0))
    ks = pl.BlockSpec((None, 4, l_sub + 2 * BAND_HALF, ATT_HD), lambda r, i: (r, 0, 0, 0))
    bs = pl.BlockSpec((4, bq, w), lambda r, i: (0, 0, 0))
    return qs, ks, bs


def _band_fwd(q, k, v, bm, dil, l_sub, bq, name):
    w = bq + 2 * BAND_HALF
    qs, ks, bs = _band_specs(l_sub, bq)

    def body(q_ref, k_ref, v_ref, bm_ref, o_ref, l_ref):
        i = pl.program_id(1)
        st = pl.multiple_of(i * bq, bq)
        o, lse = _band_fn(l_sub, bq, i, q_ref[...].astype(F32), k_ref[:, pl.ds(st, w), :].astype(F32),
                          v_ref[:, pl.ds(st, w), :].astype(F32), bm_ref[...])
        o_ref[...] = o
        l_ref[...] = lse

    return pl.pallas_call(
        body, name=name, grid=(dil, l_sub // bq),
        in_specs=[qs, ks, ks, bs], out_specs=[qs, qs],
        out_shape=[jax.ShapeDtypeStruct(q.shape, F32)] * 2,
        compiler_params=_cparams(("parallel", "arbitrary")),
    )(q, k, v, bm)


def _band_bwd(q, k, v, bm, do, dlse, dil, l_sub, bq, name):
    w = bq + 2 * BAND_HALF
    qs, ks, bs = _band_specs(l_sub, bq)

    def body(q_ref, k_ref, v_ref, bm_ref, do_ref, dl_ref, dq_ref, dk_ref, dv_ref, dbm_ref):
        r, i = pl.program_id(0), pl.program_id(1)
        st = pl.multiple_of(i * bq, bq)
        _, vjp = jax.vjp(functools.partial(_band_fn, l_sub, bq, i),
                         q_ref[...].astype(F32), k_ref[:, pl.ds(st, w), :].astype(F32),
                         v_ref[:, pl.ds(st, w), :].astype(F32), bm_ref[...])
        dq, dkw, dvw, dbm = vjp((do_ref[...].astype(F32), dl_ref[...]))
        dq_ref[...] = dq.astype(dq_ref.dtype)

        @pl.when(i == 0)
        def _():
            dk_ref[...] = jnp.zeros_like(dk_ref)
            dv_ref[...] = jnp.zeros_like(dv_ref)

        dk_ref[:, pl.ds(st, w), :] += dkw
        dv_ref[:, pl.ds(st, w), :] += dvw

        @pl.when((i == 0) & (r == 0))
        def _():
            dbm_ref[...] = dbm

        @pl.when((i > 0) | (r > 0))
        def _():
            dbm_ref[...] += dbm

    return pl.pallas_call(
        body, name=name, grid=(dil, l_sub // bq),
        in_specs=[qs, ks, ks, bs, qs, qs], out_specs=[qs, ks, ks, bs],
        out_shape=[jax.ShapeDtypeStruct(q.shape, BF16), jax.ShapeDtypeStruct(k.shape, F32),
                   jax.ShapeDtypeStruct(k.shape, F32), jax.ShapeDtypeStruct(bm.shape, F32)],
        compiler_params=_cparams(("arbitrary", "arbitrary")),
    )(q, k, v, bm, do, dlse)


def _t5_bucket(rel):
    half = REL_BUCKETS // 2
    max_exact = half // 2
    n = np.abs(rel)
    large = max_exact + (np.log(np.maximum(n, 1) / max_exact) / math.log(REL_MAX_DIST / max_exact)
                         * (half - max_exact)).astype(np.int64)
    large = np.minimum(large, half - 1)
    return ((rel > 0) * half + np.where(n < max_exact, n, large)).astype(np.int32)


def _bucket_onehot(dil):
    idx = _t5_bucket(np.arange(-BAND_HALF, BAND_HALF + 1) * dil)
    oh = np.zeros((2 * BAND_HALF + 1, REL_BUCKETS), np.float32)
    oh[np.arange(2 * BAND_HALF + 1), idx] = 1.0
    return oh


def _band_bias(rel_bias, gi, dil, bq):
    w = bq + 2 * BAND_HALF
    nb = 2 * BAND_HALF + 1
    bias = jnp.dot(jnp.asarray(_bucket_onehot(dil)), rel_bias[:, 4 * gi:4 * gi + 4], precision=HI)
    row = jnp.concatenate([bias.T, jnp.full((4, w + 1 - nb), NEG, F32)], axis=1)
    flat = jnp.tile(row, (1, bq))[:, :bq * w]
    return flat.reshape(4, bq, w)


def _relbias_grad(dbms, name):
    nb = 2 * BAND_HALF + 1
    bq = max(d.shape[1] for d in dbms)
    skew = []
    for dbm in dbms:
        bqg, w = dbm.shape[1], dbm.shape[2]
        flat = jnp.pad(dbm.reshape(4, bqg * w), ((0, 0), (0, bqg)))
        skew.append(jnp.pad(flat.reshape(4, bqg, w + 1)[:, :, :nb], ((0, 0), (0, bq - bqg), (0, 256 - nb))))
    sk = jnp.concatenate(skew, axis=0)
    oh = np.zeros((3, 256, 128), np.float32)
    for gi, (_, dil) in enumerate(DIL_GROUPS):
        oh[gi, :2 * BAND_HALF + 1, :REL_BUCKETS] = _bucket_onehot(dil)

    def body(s_ref, oh_ref, o_ref):
        col = jnp.sum(s_ref[...], axis=0, keepdims=True)
        o_ref[...] = jnp.dot(jnp.broadcast_to(col, (8, 256)), oh_ref[...], precision=HI, preferred_element_type=F32)

    out = pl.pallas_call(
        body, name=name, grid=(12,),
        in_specs=[pl.BlockSpec((None, bq, 256), lambda n: (n, 0, 0)),
                  pl.BlockSpec((None, 256, 128), lambda n: (n // 4, 0, 0))],
        out_specs=pl.BlockSpec((None, 8, 128), lambda n: (n, 0, 0)),
        out_shape=jax.ShapeDtypeStruct((12, 8, 128), F32),
        compiler_params=_cparams(("parallel",)),
    )(sk, jnp.asarray(oh))
    return out[:, 0, :REL_BUCKETS].T


def _mem_fn(q, k, v):
    s = lax.dot_general((q * (ATT_HD ** -0.5)).astype(BF16), k.astype(BF16), (((2,), (2,)), ((0,), (0,))),
                        preferred_element_type=F32)
    m = lax.stop_gradient(jnp.max(s, axis=-1, keepdims=True))
    p = jnp.exp(s - m)
    p = p / jnp.sum(p, axis=-1, keepdims=True)
    return lax.dot_general(p.astype(BF16), v.astype(BF16), (((2,), (1,)), ((0,), (0,))), preferred_element_type=F32)


def _mem_specs(tb, ml):
    qs = pl.BlockSpec((MEM_HEADS, tb, ATT_HD), lambda i: (0, i, 0))
    ks = pl.BlockSpec((MEM_HEADS, ml, ATT_HD), lambda i: (0, 0, 0))
    return qs, ks


def _mem_fwd(q, k, v, tb, name):
    qs, ks = _mem_specs(tb, k.shape[1])

    def body(q_ref, k_ref, v_ref, o_ref):
        o_ref[...] = _mem_fn(q_ref[...].astype(F32), k_ref[...], v_ref[...])

    return pl.pallas_call(
        body, name=name, grid=(q.shape[1] // tb,),
        in_specs=[qs, ks, ks], out_specs=qs, out_shape=jax.ShapeDtypeStruct(q.shape, F32),
        compiler_params=_cparams(("parallel",)),
    )(q, k, v)


def _mem_bwd(q, k, v, do, tb, name):
    qs, ks = _mem_specs(tb, k.shape[1])

    def body(q_ref, k_ref, v_ref, do_ref, dq_ref, dk_ref, dv_ref):
        i = pl.program_id(0)
        _, vjp = jax.vjp(_mem_fn, q_ref[...].astype(F32), k_ref[...], v_ref[...])
        dq, dk, dv = vjp(do_ref[...])
        dq_ref[...] = dq

        @pl.when(i == 0)
        def _():
            dk_ref[...] = dk
            dv_ref[...] = dv

        @pl.when(i > 0)
        def _():
            dk_ref[...] += dk
            dv_ref[...] += dv

    return pl.pallas_call(
        body, name=name, grid=(q.shape[1] // tb,),
        in_specs=[qs, ks, ks, qs], out_specs=[qs, ks, ks],
        out_shape=[jax.ShapeDtypeStruct(q.shape, F32), jax.ShapeDtypeStruct(k.shape, F32),
                   jax.ShapeDtypeStruct(k.shape, F32)],
        compiler_params=_cparams(("arbitrary",)),
    )(q, k, v, do)


CONV_PAD = 8


def _conv_post(kind, acc):
    s = _silu(acc)
    if kind == 2:
        return s
    scale = DN_HD ** -0.5 if kind == 0 else 1.0
    return s * lax.rsqrt(jnp.sum(s * s, axis=-1, keepdims=True) + EPS) * scale


def _conv_rows(x_ref, t, start, rt):
    lo = min(max(start, 0), t - rt)
    x = x_ref[pl.ds(lo, rt), :]
    shift = lo - start
    if shift == 0:
        return x
    x = pltpu.roll(x, shift % rt, axis=0)
    row = lax.broadcasted_iota(jnp.int32, x.shape, 0)
    return jnp.where((row >= shift) if shift > 0 else (row < rt + shift), x, 0.0)


def _conv_acc(x_ref, t, w, r0, rt):
    acc = None
    for i in range(DN_CONV):
        term = w[i:i + 1, :] * _conv_rows(x_ref, t, r0 + i - DN_CONV // 2, rt)
        acc = term if acc is None else acc + term
    return acc


def _conv_fwd(x, w8, kind, rt, name):
    t = x.shape[0]

    def body(x_ref, w_ref, o_ref):
        w = w_ref[...]
        for r in range(t // rt):
            o_ref[pl.ds(r * rt, rt), :] = _conv_post(kind, _conv_acc(x_ref, t, w, r * rt, rt))

    return pl.pallas_call(
        body, name=name, grid=(DN_HEADS,),
        in_specs=[pl.BlockSpec((t, DN_HD), lambda j: (0, 6 * kind + j)),
                  pl.BlockSpec((8, DN_HD), lambda j: (0, 6 * kind + j))],
        out_specs=pl.BlockSpec((t, DN_HD), lambda j: (0, j)),
        out_shape=jax.ShapeDtypeStruct((t, TOK_W), F32),
        compiler_params=_cparams(("parallel",)),
    )(x, w8)


def _conv_bwd(x, w8, d_f, d_r, dp, kind, rt, name):
    t = x.shape[0]

    def body(xp_ref, w_ref, df_ref, dr_ref, dp_in, dx_ref, dw_ref, dpad_ref):
        del dp_in
        w = w_ref[...]
        zero = jnp.zeros((CONV_PAD, DN_HD), F32)
        dpad_ref[pl.ds(0, CONV_PAD), :] = zero
        dpad_ref[pl.ds(CONV_PAD + t, CONV_PAD), :] = zero
        dw = [jnp.zeros((1, DN_HD), F32) for _ in range(DN_CONV)]
        for r in range(t // rt):
            rows = pl.ds(r * rt, rt)
            acc = _conv_acc(xp_ref, t, w, r * rt, rt)
            _, vjp = jax.vjp(functools.partial(_conv_post, kind), acc)
            (dacc,) = vjp(df_ref[rows, :] + dr_ref[rows, :])
            dpad_ref[pl.ds(CONV_PAD + r * rt, rt), :] = dacc
            for i in range(DN_CONV):
                xs = _conv_rows(xp_ref, t, r * rt + i - DN_CONV // 2, rt)
                dw[i] = dw[i] + jnp.sum(dacc * xs, axis=0, keepdims=True)
        dw_ref[...] = jnp.concatenate(dw + [jnp.zeros((8 - DN_CONV, DN_HD), F32)], axis=0)
        for r in range(t // rt):
            acc = None
            for i in range(DN_CONV):
                term = w[i:i + 1, :] * dpad_ref[pl.ds(CONV_PAD + r * rt - i + DN_CONV // 2, rt), :]
                acc = term if acc is None else acc + term
            dx_ref[pl.ds(r * rt, rt), :] = acc.astype(dx_ref.dtype)

    return pl.pallas_call(
        body, name=name, grid=(DN_HEADS,),
        in_specs=[pl.BlockSpec((t, DN_HD), lambda j: (0, 6 * kind + j)),
                  pl.BlockSpec((8, DN_HD), lambda j: (0, 6 * kind + j)),
                  pl.BlockSpec((t, DN_HD), lambda j: (0, j)),
                  pl.BlockSpec((t, DN_HD), lambda j: (0, j)),
                  pl.BlockSpec(memory_space=pl.ANY)],
        out_specs=[pl.BlockSpec((t, DN_HD), lambda j: (0, 6 * kind + j)),
                   pl.BlockSpec((8, DN_HD), lambda j: (0, j))],
        out_shape=[jax.ShapeDtypeStruct(dp.shape, dp.dtype), jax.ShapeDtypeStruct((8, TOK_W), F32)],
        input_output_aliases={4: 0},
        scratch_shapes=[pltpu.VMEM((t + 2 * CONV_PAD, DN_HD), F32)],
        compiler_params=_cparams(("parallel",)),
    )(x, w8, d_f, d_r, dp)


def _softplus(x):
    e = jnp.exp(-jnp.abs(x))
    return jnp.maximum(x, 0.0) + jnp.where(e < 1e-4, e - 0.5 * e * e, jnp.log(1.0 + e))


_NN = (((2,), (1,)), ((0,), (0,)))
_NT = (((2,), (2,)), ((0,), (0,)))
_TN = (((1,), (1,)), ((0,), (0,)))


def _dot(a, b, dims=_NN):
    return lax.dot_general(a.astype(BF16), b.astype(BF16), dims, preferred_element_type=F32)


def _hi_lo(x):
    hi = x.astype(BF16)
    return hi, (x - hi.astype(F32)).astype(BF16)


def _mask_dot(mask_bf16, x, dims):
    x1 = x.astype(BF16)
    r = x - x1.astype(F32)
    x2, x3 = _hi_lo(r)
    d = functools.partial(lax.dot_general, dimension_numbers=dims, preferred_element_type=F32)
    return d(mask_bf16, x1) + d(mask_bf16, x2) + d(mask_bf16, x3)


@jax.custom_vjp
def _dot_mask(mask_bf16, x):
    return _mask_dot(mask_bf16, x, _NN)


def _dot_mask_fwd(mask_bf16, x):
    return _mask_dot(mask_bf16, x, _NN), mask_bf16


def _dot_mask_bwd(mask_bf16, ct):
    return jnp.zeros_like(mask_bf16), _mask_dot(mask_bf16, ct, _TN)


_dot_mask.defvjp(_dot_mask_fwd, _dot_mask_bwd)


def _dot3_raw(a, b, dims):
    a1, a2 = _hi_lo(a)
    b1, b2 = _hi_lo(b)
    d = functools.partial(lax.dot_general, dimension_numbers=dims, preferred_element_type=F32)
    return d(a1, b1) + d(a1, b2) + d(a2, b1)


def _unit_solve_pass(lmat, rhs, masks):
    ainv = masks[6] - lmat * masks[0]
    for sh in range(1, 6):
        ainv = ainv - _dot(_dot(ainv, lmat * masks[sh]), ainv)
    return _dot3_raw(ainv, rhs, _NN), ainv


@jax.custom_vjp
def _unit_solve(lmat, rhs, masks):
    return _unit_solve_pass(lmat, rhs, masks)[0]


def _unit_solve_fwd(lmat, rhs, masks):
    sol, ainv = _unit_solve_pass(lmat, rhs, masks)
    return sol, (sol, ainv, masks)


def _unit_solve_bwd(res, ct):
    sol, ainv, masks = res
    d_rhs = _dot3_raw(ainv, ct, _TN)
    return -_dot3_raw(d_rhs, sol, _NT), d_rhs, tuple(jnp.zeros_like(m) for m in masks)


_unit_solve.defvjp(_unit_solve_fwd, _unit_solve_bwd)


def _block_masks(rev, row, col):
    c = DN_CHUNK
    prow = jnp.where(rev, c - 1 - row, row)
    pcol = jnp.where(rev, c - 1 - col, col)
    masks = []
    for sh in range(6):
        differ = (prow ^ pcol) >> sh
        miss = (differ ^ 1) + (1 - ((prow >> sh) & 1))
        masks.append(jnp.where(miss == 0, 1.0, 0.0))
    masks.append(jnp.where(row == col, 1.0, 0.0))
    return tuple(masks)


def _dn_chunk(q, k, v, al, be, alc, a_row, dt_row, a_rowc, dt_rowc, s):
    n, c = q.shape[0], DN_CHUNK
    rev = lax.broadcasted_iota(jnp.int32, (n, c, c), 0) >= n // 2
    row = lax.broadcasted_iota(jnp.int32, (n, c, c), 1)
    col = lax.broadcasted_iota(jnp.int32, (n, c, c), 2)
    ahead = jnp.where(rev, col - row, row - col)
    incl = ahead >= 0
    strict = ahead > 0
    incl_b = incl.astype(BF16)

    g = -jnp.exp(a_row) * _softplus(al + dt_row)
    beta = _sigmoid(be)
    g_c = -jnp.exp(a_rowc) * _softplus(alc + dt_rowc)
    gc = _dot_mask(incl_b, g)
    gcc = _dot_mask(incl_b, g_c)
    decay = jnp.exp(jnp.where(incl, gcc - jnp.swapaxes(gcc, 1, 2), NEG))
    kb = k * beta
    lmat = jnp.where(strict, _dot(kb, k, _NT) * decay, 0.0)
    rhs = jnp.concatenate([v * beta, kb * jnp.exp(gc)], axis=2)
    sol = _unit_solve(lmat, rhs, _block_masks(rev, row, col))
    u, w = sol[:, :, :DN_HD], sol[:, :, DN_HD:]
    intra = jnp.where(incl, _dot(q, k, _NT) * decay, 0.0)
    v_new = u - _dot(w, s)
    out = _dot(q * jnp.exp(gc), s) + _dot(intra, v_new)
    g_last = jnp.sum(g, axis=1, keepdims=True)
    s_new = s * jnp.exp(g_last) + _dot(k * jnp.exp(g_last - gc), v_new, _TN)
    return out, s_new


DN_HG = 6


def _dn_load(f_refs, r_refs, alf, bef, alr, ber, a_ref, dt_ref):
    c, hg = DN_CHUNK, DN_HG
    sls = [slice(DN_HD * h, DN_HD * (h + 1)) for h in range(hg)]
    toks = [jnp.stack([f[:, sl] for sl in sls] + [r[:, sl] for sl in sls]) for f, r in zip(f_refs, r_refs)]
    al = jnp.concatenate([alf[...], alr[...]], axis=0)
    be = jnp.concatenate([bef[...], ber[...]], axis=0)
    alc = jnp.concatenate([alf[:, :, 0:c], alr[:, :, 0:c]], axis=0)
    a = jnp.concatenate([a_ref[0], a_ref[1]], axis=0)
    dt = jnp.concatenate([dt_ref[0], dt_ref[1]], axis=0)
    ac = jnp.concatenate([a_ref[0, :, :, 0:c], a_ref[1, :, :, 0:c]], axis=0)
    dtc = jnp.concatenate([dt_ref[0, :, :, 0:c], dt_ref[1, :, :, 0:c]], axis=0)
    return toks, (al, be, alc, a, dt, ac, dtc)


def _dn_views(nc, bwd):
    c, hg = DN_CHUNK, DN_HG
    if bwd:
        f_blk = lambda s: nc - 1 - s
        r_blk = lambda s: s
        st_blk = lambda s: nc - 1 - s
    else:
        f_blk = lambda s: s
        r_blk = lambda s: nc - 1 - s
        st_blk = lambda s: s
    tok_f = pl.BlockSpec((c, hg * DN_HD), lambda g, s: (f_blk(s), g))
    tok_r = pl.BlockSpec((c, hg * DN_HD), lambda g, s: (r_blk(s), g))
    gate_f = pl.BlockSpec((None, hg, c, DN_HD), lambda g, s: (0, g, f_blk(s), 0))
    gate_r = pl.BlockSpec((None, hg, c, DN_HD), lambda g, s: (1, g, r_blk(s), 0))
    par = pl.BlockSpec((2, hg, 1, DN_HD), lambda g, s: (0, g, 0, 0))
    state = pl.BlockSpec((2, hg, None, DN_HD, DN_HD), lambda g, s: (0, g, st_blk(s), 0, 0))
    return tok_f, tok_r, gate_f, gate_r, par, state


def _dn_fwd(q, k, v, al, be, a_rows, dt_rows, name):
    t = q.shape[0]
    c, hg = DN_CHUNK, DN_HG
    nc = t // c
    tok_f, tok_r, gate_f, gate_r, par, state = _dn_views(nc, False)

    def body(qf, kf, vf, qr, kr, vr, alf, bef, alr, ber, a_ref, dt_ref, of_ref, or_ref, st_ref, s_ref):
        @pl.when(pl.program_id(1) == 0)
        def _():
            s_ref[...] = jnp.zeros_like(s_ref)

        (q_, k_, v_), gates = _dn_load((qf, kf, vf), (qr, kr, vr), alf, bef, alr, ber, a_ref, dt_ref)
        s = s_ref[...]
        st_ref[0] = s[:hg]
        st_ref[1] = s[hg:]
        out, s_new = _dn_chunk(q_, k_, v_, *gates, s)
        for h in range(hg):
            sl = slice(DN_HD * h, DN_HD * (h + 1))
            of_ref[:, sl] = out[h]
            or_ref[:, sl] = out[hg + h]
        s_ref[...] = s_new

    return pl.pallas_call(
        body, name=name, grid=(DN_HEADS // hg, nc),
        in_specs=[tok_f] * 3 + [tok_r] * 3 + [gate_f, gate_f, gate_r, gate_r, par, par],
        out_specs=[tok_f, tok_r, state],
        out_shape=[jax.ShapeDtypeStruct((t, TOK_W), F32)] * 2
        + [jax.ShapeDtypeStruct((2, DN_HEADS, nc, DN_HD, DN_HD), F32)],
        scratch_shapes=[pltpu.VMEM((2 * hg, DN_HD, DN_HD), F32)],
        compiler_params=_cparams(("parallel", "arbitrary")),
    )(q, k, v, q, k, v, al, be, al, be, a_rows, dt_rows)


def _dn_bwd(q, k, v, al, be, a_rows, dt_rows, states, do, name):
    t = q.shape[0]
    c, hg = DN_CHUNK, DN_HG
    assert hg == DN_HEADS
    nc = t // c
    tok_f, tok_r, gate_f, gate_r, par, state = _dn_views(nc, True)
    gout_f = pl.BlockSpec((c, DN_HD), lambda g, s: (nc - 1 - s, 0))
    gout_r = pl.BlockSpec((c, DN_HD), lambda g, s: (s, 0))

    def body(qf, kf, vf, qr, kr, vr, alf, bef, alr, ber, a_ref, dt_ref, st_ref, dof, dor,
             dqf, dkf, dvf, dqr, dkr, dvr, dgf, dgr, da_ref, ddt_ref, ds_ref):
        first = pl.program_id(1) == 0

        @pl.when(first)
        def _():
            ds_ref[...] = jnp.zeros_like(ds_ref)
            da_ref[...] = jnp.zeros_like(da_ref)
            ddt_ref[...] = jnp.zeros_like(ddt_ref)

        def lanes(x):
            return jnp.sum(x, axis=-1, keepdims=True)

        (q_, k_, v_, do_), gates = _dn_load((qf, kf, vf, dof), (qr, kr, vr, dor), alf, bef, alr, ber, a_ref, dt_ref)
        s = jnp.concatenate([st_ref[0], st_ref[1]], axis=0)
        _, vjp = jax.vjp(_dn_chunk, q_, k_, v_, *gates, s)
        dq, dk, dv, dal, dbe, dalc, da, ddt, dac, ddtc, ds = vjp((do_, ds_ref[...]))
        for h in range(hg):
            sl = slice(DN_HD * h, DN_HD * (h + 1))
            dqf[:, sl], dkf[:, sl], dvf[:, sl] = dq[h], dk[h], dv[h]
            dqr[:, sl], dkr[:, sl], dvr[:, sl] = dq[hg + h], dk[hg + h], dv[hg + h]
        dal, dbe = lanes(dal) + lanes(dalc), lanes(dbe)
        lane = lax.broadcasted_iota(jnp.int32, (c, DN_HD), 1)
        for d, dg_ref in enumerate((dgf, dgr)):
            dg = jnp.zeros((c, DN_HD), F32)
            for h in range(hg):
                dg = jnp.where(lane == h, dal[d * hg + h], jnp.where(lane == hg + h, dbe[d * hg + h], dg))
            dg_ref[...] = dg
        da = jnp.broadcast_to(lanes(da) + lanes(dac), da.shape)
        ddt = jnp.broadcast_to(lanes(ddt) + lanes(ddtc), ddt.shape)
        da_ref[0] += da[:hg]
        da_ref[1] += da[hg:]
        ddt_ref[0] += ddt[:hg]
        ddt_ref[1] += ddt[hg:]
        ds_ref[...] = ds

    tok = jax.ShapeDtypeStruct((t, TOK_W), F32)
    gate = jax.ShapeDtypeStruct((t, DN_HD), F32)
    parsh = jax.ShapeDtypeStruct((2, DN_HEADS, 1, DN_HD), F32)
    res = pl.pallas_call(
        body, name=name, grid=(DN_HEADS // hg, nc),
        in_specs=[tok_f] * 3 + [tok_r] * 3 + [gate_f, gate_f, gate_r, gate_r, par, par, state, tok_f, tok_r],
        out_specs=[tok_f] * 3 + [tok_r] * 3 + [gout_f, gout_r, par, par],
        out_shape=[tok] * 6 + [gate] * 2 + [parsh] * 2,
        scratch_shapes=[pltpu.VMEM((2 * hg, DN_HD, DN_HD), F32)],
        compiler_params=_cparams(("parallel", "arbitrary")),
    )(q, k, v, q, k, v, al, be, al, be, a_rows, dt_rows, states, do, do)
    dqf, dkf, dvf, dqr, dkr, dvr, dgf, dgr, da, ddt = res
    dgate = jnp.concatenate([dgf[:, :2 * DN_HEADS], dgr[:, :2 * DN_HEADS]], axis=1)
    return (dqf, dkf, dvf), (dqr, dkr, dvr), dgate, da, ddt


BAND_BQ = 256
ROW_TB = 256
MEM_TB = 512
CONV_RT = 512


def _to_sub(x, dil):
    l = x.shape[0] // dil
    return x.reshape(l, dil, 4, ATT_HD).transpose(1, 2, 0, 3)


def _from_sub(x, dil):
    l = x.shape[2]
    return x.transpose(2, 0, 1, 3).reshape(l * dil, 4 * ATT_HD)


def _heads_major(x):
    return x.reshape(x.shape[0], MEM_HEADS, ATT_HD).transpose(1, 0, 2)


def _heads_minor(x):
    return x.transpose(1, 0, 2).reshape(x.shape[1], MEM_HEADS * ATT_HD)


def _mem_kv_fwd(mem, gain, w_kv, li):
    (memn,) = _rowwise(_fn_pre, [mem], [gain], [(D, BF16)], mem.shape[0], f"memnorm_fwd{li}")
    kv = _matmul(memn, w_kv, "nn", F32, f"memkv_fwd{li}")
    return _heads_major(kv[:, :MEM_W]), _heads_major(kv[:, MEM_W:]), memn


def _mem_kv_bwd(mem, gain, w_kv, memn, dkm, dvm, li):
    dkv = jnp.concatenate([_heads_minor(dkm), _heads_minor(dvm)], axis=1).astype(BF16)
    dw = _matmul(memn, dkv, "tn", BF16, f"memkv_dw{li}")
    dmemn = _matmul(dkv, w_kv, "nt", F32, f"memkv_dx{li}")
    _, (dgain,) = _rowwise_bwd(_fn_pre, [mem], [gain], [dmemn], [None], mem.shape[0], f"memnorm_bwd{li}")
    return dw, dgain


def _attn_mixer_fwd(p, rel_bias, kv_fn):
    t = p.shape[0]
    saved, outs, lses = [], [], []
    for gi, (_, dil) in enumerate(DIL_GROUPS):
        l_sub = t // dil
        bq = min(BAND_BQ, l_sub)
        q = _to_sub(p[:, 256 * gi:256 * (gi + 1)], dil)
        pad = ((0, 0), (0, 0), (BAND_HALF, BAND_HALF), (0, 0))
        k = jnp.pad(_to_sub(p[:, TOK_W + 256 * gi:TOK_W + 256 * (gi + 1)], dil), pad)
        v = jnp.pad(_to_sub(p[:, 2 * TOK_W + 256 * gi:2 * TOK_W + 256 * (gi + 1)], dil), pad)
        bm = _band_bias(rel_bias, gi, dil, bq)
        o, lse = _band_fwd(q, k, v, bm, dil, l_sub, bq, f"band_fwd{gi}")
        outs.append(_from_sub(o, dil))
        lses.append(_from_sub(lse, dil))
        saved.append((q, k, v, bm))
    o_all = jnp.concatenate(outs, axis=1)
    lse_all = jnp.concatenate(lses, axis=1)
    (mixed,) = _rowwise(_fn_combine, [o_all, lse_all], [], [(TOK_W, BF16)], ROW_TB, "combine_fwd")
    qm = _heads_major(p[:, 3 * TOK_W:])
    km, vm, memn = kv_fn(mixed)
    memo = _mem_fwd(qm, km, vm, min(MEM_TB, t), "mem_fwd0")
    cat = jnp.concatenate([mixed, _heads_minor(memo).astype(BF16)], axis=1)
    return cat, (saved, o_all, lse_all, qm), (km, vm, memn)


def _attn_mixer_bwd(dcat, res, km, vm):
    saved, o_all, lse_all, qm = res
    t = dcat.shape[0]
    (do_all, dlse_all), _ = _rowwise_bwd(_fn_combine, [o_all, lse_all], [], [_Cols(dcat, TOK_W, 0)], [BF16, F32],
                                         ROW_TB, "combine_bwd")
    dqs, dks, dvs, dbms = [], [], [], []
    for gi, (_, dil) in enumerate(DIL_GROUPS):
        l_sub = t // dil
        bq = min(BAND_BQ, l_sub)
        q, k, v, bm = saved[gi]
        do = _to_sub(do_all[:, 256 * gi:256 * (gi + 1)], dil)
        dl = _to_sub(dlse_all[:, 256 * gi:256 * (gi + 1)], dil)
        dq, dk, dv, dbm = _band_bwd(q, k, v, bm, do, dl, dil, l_sub, bq, f"band_bwd{gi}")
        dqs.append(_from_sub(dq, dil))
        dks.append(_from_sub(dk[:, :, BAND_HALF:-BAND_HALF], dil))
        dvs.append(_from_sub(dv[:, :, BAND_HALF:-BAND_HALF], dil))
        dbms.append(dbm)
    dqm, dkm, dvm = _mem_bwd(qm, km, vm, _heads_major(dcat[:, TOK_W:]), min(MEM_TB, t), "mem_bwd0")
    dp = jnp.concatenate([d.astype(BF16) for d in dqs + dks + dvs + [_heads_minor(dqm)]], axis=1)
    return dp, _relbias_grad(dbms, "relbias_grad"), dkm, dvm


def _dn_mixer_fwd(p, conv_w, a_log, dt_bias, out_norm, km, vm):
    t = p.shape[0]
    rt = min(CONV_RT, t)
    xp = p
    w8 = jnp.pad(conv_w, ((0, 8 - DN_CONV), (0, 0)))
    q = _conv_fwd(xp, w8, 0, rt, "conv_fwd_q")
    k = _conv_fwd(xp, w8, 1, rt, "conv_fwd_k")
    v = _conv_fwd(xp, w8, 2, rt, "conv_fwd_v")
    gate = p[:, 4 * TOK_W:4 * TOK_W + 4 * DN_HEADS].reshape(t, 2, 2, DN_HEADS)
    bshape = (2, DN_HEADS, t, DN_HD)
    al = jnp.broadcast_to(gate[:, :, 0, :].transpose(1, 2, 0)[..., None], bshape)
    be = jnp.broadcast_to(gate[:, :, 1, :].transpose(1, 2, 0)[..., None], bshape)
    a_rows = jnp.broadcast_to(a_log[:, :, None, None], (2, DN_HEADS, 1, DN_HD))
    dt_rows = jnp.broadcast_to(dt_bias[:, :, None, None], (2, DN_HEADS, 1, DN_HD))
    o_f, o_r, states = _dn_fwd(q, k, v, al, be, a_rows, dt_rows, "dn_fwd")
    gain = out_norm.reshape(1, DN_HD)
    (og,) = _rowwise(_fn_outnorm, [o_f, o_r, _Cols(p, TOK_W, 3)], [gain], [(TOK_W, BF16)], ROW_TB, "outnorm_fwd")
    qm = _heads_major(p[:, 4 * TOK_W + 4 * DN_HEADS:DN_IN])
    memo = _mem_fwd(qm, km, vm, min(MEM_TB, t), "mem_fwd1")
    cat = jnp.concatenate([og, _heads_minor(memo).astype(BF16)], axis=1)
    return cat, (xp, w8, q, k, v, al, be, a_rows, dt_rows, o_f, o_r, states, gain, qm)


def _dn_mixer_bwd(dcat, res, km, vm):
    xp, w8, q, k, v, al, be, a_rows, dt_rows, o_f, o_r, states, gain, qm = res
    t = dcat.shape[0]
    rt = min(CONV_RT, t)
    (do, dz), (dgain,) = _rowwise_bwd(_fn_outnorm, [o_f, o_r, _Cols(xp, TOK_W, 3)], [gain],
                                      [_Cols(dcat, TOK_W, 0)], [F32, None, BF16],
                                      ROW_TB, "outnorm_bwd")
    d_f, d_r, dgate, da, ddt = _dn_bwd(q, k, v, al, be, a_rows, dt_rows, states, do, "dn_bwd")
    dqm, dkm, dvm = _mem_bwd(qm, km, vm, _heads_major(dcat[:, TOK_W:]), min(MEM_TB, t), "mem_bwd1")
    rest = jnp.concatenate([dgate.astype(BF16), _heads_minor(dqm).astype(BF16),
                            jnp.zeros((t, DN_IN_PAD - DN_IN), BF16)], axis=1)
    dp = lax.dynamic_update_slice(lax.empty((t, DN_IN_PAD), BF16), dz, (0, 3 * TOK_W))
    dp = lax.dynamic_update_slice(dp, rest, (0, 4 * TOK_W))
    dws = []
    for kind, nm in enumerate("qkv"):
        dp, dw = _conv_bwd(xp, w8, d_f[kind], d_r[kind], dp, kind, rt, f"conv_bwd_{nm}")
        dws.append(dw)
    dconv = jnp.concatenate(dws, axis=1)[:DN_CONV]
    return dp, dconv, da[:, :, 0, 0], ddt[:, :, 0, 0], dgain.reshape(DN_HD), dkm, dvm


SWI_TB = 256


def _ffn_fwd(h, w_gu_t, w_d, li):
    gu = _matmul(h, w_gu_t, "nt", BF16, f"ffn_gu{li}")
    (a,) = _rowwise(_fn_swiglu, [gu], [], [(D_FF, BF16)], SWI_TB, f"swiglu_fwd{li}")
    return _matmul(a, w_d, "nn", F32, f"ffn_down{li}"), gu, a


def _ffn_bwd(df, h, w_gu_t, w_d, gu, a, li):
    da = _matmul(df, w_d, "nt", BF16, f"ffn_down_dx{li}")
    dwd = _matmul(a, df, "tn", BF16, f"ffn_down_dw{li}")
    (dgu,), _ = _rowwise_bwd(_fn_swiglu, [gu], [], [da], [BF16], SWI_TB, f"swiglu_bwd{li}")
    dh = _matmul(dgu, w_gu_t, "nn", F32, f"ffn_gu_dx{li}")
    dwgu_t = _matmul(dgu, h, "tn", BF16, f"ffn_gu_dw{li}")
    return dh, dwgu_t, dwd


def _fn_first(x, g):
    return x, _rms(x, g)


def _me_xyc():
    return lax.axis_index("x"), lax.axis_index("y"), lax.axis_index("c")


def _flip(coords, k):
    x, y, c = coords
    return (1 - x if k & 4 else x, 1 - y if k & 2 else y, 1 - c if k & 1 else c)


def _index(coords):
    x, y, c = coords
    return 4 * x + 2 * y + c


def _window(ref, axis, size, d):
    idx = [slice(None)] * len(ref.shape)
    idx[axis] = pl.ds(pl.multiple_of(d * size, size), size)
    return ref.at[tuple(idx)]


def _comm_call(body, n, ins, out_shapes, name):
    hbm = pl.BlockSpec(memory_space=pl.ANY)
    return pl.pallas_call(
        body, name=name, in_specs=[hbm] * n, out_specs=[hbm] * n, out_shape=out_shapes,
        scratch_shapes=[pltpu.SemaphoreType.DMA((N_DEV - 1, n)), pltpu.SemaphoreType.DMA((N_DEV - 1, n)),
                        pltpu.SemaphoreType.DMA((n,))],
    )(*ins)


def _run_exchange(n, local, remote, send_sems, recv_sems):
    me = _me_xyc()
    locs = [local(p) for p in range(n)]
    for cp in locs:
        cp.start()
    sends = [remote(k, p, me, _flip(me, k)) for k in range(1, N_DEV) for p in range(n)]
    for cp in sends:
        cp.start()
    for k in range(1, N_DEV):
        for p in range(n):
            remote(k, p, _flip(me, k), me).wait_recv()
    for cp in sends:
        cp.wait_send()
    for cp in locs:
        cp.wait()


def _all_gather(shards, axes, name):
    n = len(shards)
    sizes = [s.shape[a] for s, a in zip(shards, axes)]

    def body(*refs):
        ins, outs = refs[:n], refs[n:2 * n]
        send_sems, recv_sems, loc_sems = refs[2 * n:]
        me = _me_xyc()

        def local(p):
            return pltpu.make_async_copy(ins[p], _window(outs[p], axes[p], sizes[p], _index(me)), loc_sems.at[p])

        def remote(k, p, owner, to):
            return pltpu.make_async_remote_copy(
                src_ref=ins[p], dst_ref=_window(outs[p], axes[p], sizes[p], _index(owner)),
                send_sem=send_sems.at[k - 1, p], recv_sem=recv_sems.at[k - 1, p], device_id=to, device_id_type=MESH)

        _run_exchange(n, local, remote, send_sems, recv_sems)

    def full(s, a):
        return s.shape[:a] + (N_DEV * s.shape[a],) + s.shape[a + 1:]

    return _comm_call(body, n, shards, [jax.ShapeDtypeStruct(full(s, a), s.dtype) for s, a in zip(shards, axes)], name)


def _exchange(fulls, axes, name):
    n = len(fulls)
    sizes = [None if a is None else f.shape[a] // N_DEV for f, a in zip(fulls, axes)]

    def part_shape(f, a):
        return f.shape if a is None else f.shape[:a] + (f.shape[a] // N_DEV,) + f.shape[a + 1:]

    def body(*refs):
        ins, outs = refs[:n], refs[n:2 * n]
        send_sems, recv_sems, loc_sems = refs[2 * n:]
        me = _me_xyc()

        def src(p, to):
            return ins[p] if axes[p] is None else _window(ins[p], axes[p], sizes[p], _index(to))

        def local(p):
            return pltpu.make_async_copy(src(p, me), outs[p].at[_index(me)], loc_sems.at[p])

        def remote(k, p, sender, to):
            return pltpu.make_async_remote_copy(
                src_ref=src(p, to), dst_ref=outs[p].at[_index(sender)],
                send_sem=send_sems.at[k - 1, p], recv_sem=recv_sems.at[k - 1, p], device_id=to, device_id_type=MESH)

        _run_exchange(n, local, remote, send_sems, recv_sems)

    return _comm_call(body, n, fulls,
                      [jax.ShapeDtypeStruct((N_DEV,) + part_shape(f, a), f.dtype) for f, a in zip(fulls, axes)], name)


_HBM = pl.BlockSpec(memory_space=pltpu.HBM)
_SEM = pl.BlockSpec(memory_space=pltpu.SEMAPHORE)
_EFFECT = pltpu.SideEffectType.DATAFLOW_SIDE_EFFECTING


def _in_hbm(a):
    return pltpu.with_memory_space_constraint(a, pltpu.HBM)


def _split_start(srcs, lands, after, descr, name):
    n = len(srcs)

    def body(*refs):
        ins, lnd = refs[:n], refs[n:2 * n]
        send_sems, recv_sems = refs[2 * n + 1], refs[2 * n + 2]
        token = refs[-1]
        me = _me_xyc()
        for k in range(1, N_DEV):
            for p in range(n):
                descr(k, p, ins, lnd, send_sems, recv_sems, me, _flip(me, k)).start()
        token[...] = jnp.zeros_like(token)

    sems = pltpu.SemaphoreType.DMA(((N_DEV - 1) * n,))
    res = pl.pallas_call(
        body, name=name,
        out_shape=(sems, sems, *[pltpu.HBM(a.shape, a.dtype) for a in (*srcs, *lands)],
                   jax.ShapeDtypeStruct((8, 128), F32)),
        in_specs=[_HBM] * (2 * n) + [pl.BlockSpec(memory_space=pl.ANY)],
        out_specs=(_SEM, _SEM, *[_HBM] * (2 * n), pl.BlockSpec(memory_space=pltpu.VMEM)),
        input_output_aliases={i: 2 + i for i in range(2 * n)},
        compiler_params=pltpu.CompilerParams(has_side_effects=_EFFECT),
    )(*[_in_hbm(a) for a in (*srcs, *lands)], after)
    return res[0], res[1], res[2:2 + n], res[2 + n:2 + 2 * n], res[-1]


def _split_wait(send_sems, recv_sems, srcs, lands, after, descr, name):
    n = len(srcs)

    def body(*refs):
        ins, lnd = refs[:n], refs[n:2 * n]
        s_sems, r_sems = refs[2 * n], refs[2 * n + 1]
        me = _me_xyc()
        for k in range(1, N_DEV):
            for p in range(n):
                peer = _flip(me, k)
                descr(k, p, ins, lnd, s_sems, r_sems, me, peer).wait_send()
                descr(k, p, ins, lnd, s_sems, r_sems, peer, me).wait_recv()

    res = pl.pallas_call(
        body, name=name,
        out_shape=tuple(pltpu.HBM(a.shape, a.dtype) for a in (*srcs, *lands)),
        in_specs=[_HBM] * (2 * n) + [_SEM, _SEM, pl.BlockSpec(memory_space=pl.ANY)],
        out_specs=tuple([_HBM] * (2 * n)),
        input_output_aliases={i: i for i in range(2 * n)},
        compiler_params=pltpu.CompilerParams(has_side_effects=_EFFECT),
    )(*srcs, *lands, send_sems, recv_sems, after)
    return list(res[n:])


def _gather_descr(axes, sizes):
    def descr(k, p, ins, lnd, send_sems, recv_sems, sender, dest):
        return pltpu.make_async_remote_copy(
            src_ref=ins[p], dst_ref=_window(lnd[p], axes[p], sizes[p], _index(sender)),
            send_sem=send_sems.at[(k - 1) * len(axes) + p], recv_sem=recv_sems.at[(k - 1) * len(axes) + p],
            device_id=dest, device_id_type=MESH)
    return descr


def _exchange_descr(axes, sizes):
    def descr(k, p, ins, lnd, send_sems, recv_sems, sender, dest):
        return pltpu.make_async_remote_copy(
            src_ref=_window(ins[p], axes[p], sizes[p], _index(dest)), dst_ref=lnd[p].at[_index(sender)],
            send_sem=send_sems.at[(k - 1) * len(axes) + p], recv_sem=recv_sems.at[(k - 1) * len(axes) + p],
            device_id=dest, device_id_type=MESH)
    return descr


def _gather_begin(shards, axes, after, name):
    sizes = [s.shape[a] for s, a in zip(shards, axes)]
    me = _index(_me_xyc())
    lands = []
    for s, a, sz in zip(shards, axes, sizes):
        full = s.shape[:a] + (N_DEV * sz,) + s.shape[a + 1:]
        lands.append(lax.dynamic_update_slice_in_dim(lax.empty(full, s.dtype), s, me * sz, a))
    descr = _gather_descr(axes, sizes)
    send_sems, recv_sems, srcs, lands, token = _split_start(shards, lands, after, descr, name)
    return (send_sems, recv_sems, srcs, lands, descr), token


def _exchange_begin(fulls, axes, after, name):
    sizes = [f.shape[a] // N_DEV for f, a in zip(fulls, axes)]
    me = _index(_me_xyc())
    lands = []
    for f, a, sz in zip(fulls, axes, sizes):
        own = lax.dynamic_slice_in_dim(f, me * sz, sz, a)
        lands.append(lax.dynamic_update_slice_in_dim(lax.empty((N_DEV,) + own.shape, f.dtype), own[None], me, 0))
    descr = _exchange_descr(axes, sizes)
    send_sems, recv_sems, srcs, lands, token = _split_start(fulls, lands, after, descr, name)
    return (send_sems, recv_sems, srcs, lands, descr), token


def _split_end(handle, after, name):
    send_sems, recv_sems, srcs, lands, descr = handle
    return _split_wait(send_sems, recv_sems, srcs, lands, after, descr, name)


def _adam_math(g, w, m, v):
    m = ADAM_B1 * m + (1.0 - ADAM_B1) * g
    v = ADAM_B2 * v + (1.0 - ADAM_B2) * (g * g)
    m_hat = m / (1.0 - ADAM_B1 ** ADAM_STEP)
    v_hat = v / (1.0 - ADAM_B2 ** ADAM_STEP)
    delta = -ADAM_LR * (m_hat / (jnp.sqrt(v_hat) + ADAM_EPS) + ADAM_WD * w)
    return delta, m, v


def _sum_slabs(r_ref):
    g = r_ref[0].astype(F32)
    for s in range(1, N_DEV):
        g = g + r_ref[s].astype(F32)
    return g


def _adamw_reduce(recv, w, m, v, tb, name):
    r, c = w.shape

    def body(r_ref, w_ref, m_ref, v_ref, g_ref, d_ref, nm_ref, nv_ref):
        g = _sum_slabs(r_ref)
        g_ref[...] = g
        d_ref[...], nm_ref[...], nv_ref[...] = _adam_math(g, w_ref[...], m_ref[...], v_ref[...])

    blk = pl.BlockSpec((tb, c), lambda i: (i, 0))
    return pl.pallas_call(
        body, name=name, grid=(r // tb,),
        in_specs=[pl.BlockSpec((N_DEV, tb, c), lambda i: (0, i, 0)), blk, blk, blk],
        out_specs=[blk] * 4, out_shape=[jax.ShapeDtypeStruct((r, c), F32)] * 4,
        compiler_params=_cparams(("parallel",)),
    )(recv, w, m, v)


def _reduce8(recv, tb, name):
    r, c = recv.shape[1:]

    def body(r_ref, g_ref):
        g_ref[...] = _sum_slabs(r_ref)

    return pl.pallas_call(
        body, name=name, grid=(r // tb,),
        in_specs=[pl.BlockSpec((N_DEV, tb, c), lambda i: (0, i, 0))],
        out_specs=pl.BlockSpec((tb, c), lambda i: (i, 0)), out_shape=jax.ShapeDtypeStruct((r, c), F32),
        compiler_params=_cparams(("parallel",)),
    )(recv)


def _adamw(g, w, m, v, tb, name):
    r, c = w.shape

    def body(g_ref, w_ref, m_ref, v_ref, d_ref, nm_ref, nv_ref):
        d_ref[...], nm_ref[...], nv_ref[...] = _adam_math(g_ref[...], w_ref[...], m_ref[...], v_ref[...])

    blk = pl.BlockSpec((tb, c), lambda i: (i, 0))
    return pl.pallas_call(
        body, name=name, grid=(r // tb,), in_specs=[blk] * 4, out_specs=[blk] * 3,
        out_shape=[jax.ShapeDtypeStruct((r, c), F32)] * 3, compiler_params=_cparams(("parallel",)),
    )(g, w, m, v)


DN_IN_SHARD = DN_IN // N_DEV
DN_IN_SHARD_PAD = 432
CONV_SHARD = (1, DN_CONV, 288)


def _pack_small(arrs, rows):
    flat = jnp.concatenate([a.astype(F32).reshape(-1) for a in arrs])
    return jnp.pad(flat, (0, rows * PACK_C - flat.size)).reshape(rows, PACK_C)


def _unpack_small(packed, shapes):
    flat, out, off = packed.reshape(-1), [], 0
    for shp in shapes:
        n = int(np.prod(shp))
        out.append(flat[off:off + n].reshape(shp))
        off += n
    return out


def kernel(x, mem, rel_bias, att_w_in, att_w_out, dn_w_in, dn_conv, dn_a_log, dn_dt_bias, dn_out_norm, dn_w_out, mem_norm, mem_w_kv, norm_mix_pre, norm_mix_post, norm_ffn_pre, norm_ffn_post, ffn_w_gate_up, ffn_w_down, loss_target, m_rel_bias, m_att_w_in, m_att_w_out, m_dn_w_in, m_dn_conv, m_dn_a_log, m_dn_dt_bias, m_dn_out_norm, m_dn_w_out, m_mem_norm, m_mem_w_kv, m_norm_mix_pre, m_norm_mix_post, m_norm_ffn_pre, m_norm_ffn_post, m_ffn_w_gate_up, m_ffn_w_down, v_rel_bias, v_att_w_in, v_att_w_out, v_dn_w_in, v_dn_conv, v_dn_a_log, v_dn_dt_bias, v_dn_out_norm, v_dn_w_out, v_mem_norm, v_mem_w_kv, v_norm_mix_pre, v_norm_mix_post, v_norm_ffn_pre, v_norm_ffn_post, v_ffn_w_gate_up, v_ffn_w_down):
    x0, mem0, tgt = x[0], mem[0], loss_target[0]
    t = x0.shape[0]
    axes = ("x", "y", "c")

    def t_shard(w):
        return jnp.swapaxes(w, 1, 2).astype(BF16)

    dn_in_pad = ((0, 0), (0, DN_IN_SHARD_PAD - DN_IN_SHARD), (0, 0))
    (w_att_in_t,) = _all_gather([t_shard(att_w_in)], [1], "allgather_first")
    w_att_in_t = w_att_in_t[0]
    gu_t, down = t_shard(ffn_w_gate_up), ffn_w_down.astype(BF16)
    gather_o, tok_o = _gather_begin([att_w_out.astype(BF16), mem_w_kv.astype(BF16)], [1, 1], w_att_in_t,
                                    "gather_att_out_start")
    gather_a, tok_a = _gather_begin([gu_t[0:1], down[0:1]], [1, 1], tok_o, "gather_ffn0_start")
    gather_b, tok_b = _gather_begin(
        [jnp.pad(t_shard(dn_w_in), dn_in_pad), dn_w_out.astype(BF16), gu_t[1:2], down[1:2], dn_conv],
        [1, 1, 1, 1, 0], tok_a, "gather_layer1_start")

    def gain(a, i):
        return a[i].reshape(1, D)

    (h0,) = _rowwise(_fn_pre, [x0], [gain(norm_mix_pre, 0) + tok_b[0:1, 0:1]], [(D, BF16)], ROW_TB, "pre0")
    p0 = _matmul(h0, w_att_in_t, "nt", BF16, "att_in")
    late = {}

    def kv0(after):
        late["w_att_out"], late["w_kv"] = _split_end(gather_o, after, "gather_att_out_wait")
        return _mem_kv_fwd(mem0, gain(mem_norm, 0), late["w_kv"][0], 0)

    cat0, res0, (km0, vm0, memn0) = _attn_mixer_fwd(p0, rel_bias, kv0)
    w_att_out, w_kv = late["w_att_out"][0], late["w_kv"]
    y0 = _matmul(cat0, w_att_out, "nn", F32, "att_out")
    g_a = [gain(norm_mix_post, 0), gain(norm_ffn_pre, 0)]
    x1, h1 = _rowwise(_fn_res_pre, [x0, y0], g_a, [(D, F32), (D, BF16)], ROW_TB, "res_pre0")
    w_gu_t0, w_down0 = [w[0] for w in _split_end(gather_a, h1, "gather_ffn0_wait")]
    f0, gu0, a0 = _ffn_fwd(h1, w_gu_t0, w_down0, 0)
    g_b = [gain(norm_ffn_post, 0), gain(norm_mix_pre, 1)]
    x2, h2 = _rowwise(_fn_res_pre, [x1, f0], g_b, [(D, F32), (D, BF16)], ROW_TB, "res_pre1")
    km1, vm1, memn1 = _mem_kv_fwd(mem0, gain(mem_norm, 1), w_kv[1], 1)
    w_dn_in_g, w_dn_out, w_gu_t1, w_down1, conv_g = _split_end(gather_b, h2, "gather_layer1_wait")
    w_dn_in_g, w_dn_out, w_gu_t1, w_down1 = w_dn_in_g[0], w_dn_out[0], w_gu_t1[0], w_down1[0]
    conv_full = conv_g.transpose(1, 0, 2).reshape(DN_CONV, 3 * TOK_W)
    w_dn_in_t = jnp.concatenate(
        [w_dn_in_g[DN_IN_SHARD_PAD * j:DN_IN_SHARD_PAD * j + DN_IN_SHARD] for j in range(N_DEV)]
        + [jnp.zeros((DN_IN_PAD - DN_IN, D), BF16)], axis=0)
    p1 = _matmul(h2, w_dn_in_t, "nt", F32, "dn_in")
    cat1, res1 = _dn_mixer_fwd(p1, conv_full, dn_a_log[0], dn_dt_bias[0], dn_out_norm[0], km1, vm1)
    y1 = _matmul(cat1, w_dn_out, "nn", F32, "dn_out")
    g_c = [gain(norm_mix_post, 1), gain(norm_ffn_pre, 1)]
    x3, h3 = _rowwise(_fn_res_pre, [x2, y1], g_c, [(D, F32), (D, BF16)], ROW_TB, "res_pre2")
    f1, gu1, a1 = _ffn_fwd(h3, w_gu_t1, w_down1, 1)
    g_d = [gain(norm_ffn_post, 1)]
    (x4,) = _rowwise(_fn_res, [x3, f1], g_d, [(D, F32)], ROW_TB, "res3")
    dx4, lrow = _loss_kernel(x4, tgt, ROW_TB, "loss")
    loss = lax.psum(lrow[0, 0] * (0.5 / D), axes)

    (df1,), (dg_fpost1,) = _rowwise_bwd(_fn_res, [x3, f1], g_d, [dx4], [None, BF16], ROW_TB, "res3_bwd")
    dh3, dwgu1, dwd1 = _ffn_bwd(df1, h3, w_gu_t1, w_down1, gu1, a1, 1)
    (dx2, dy1), (dg_mpost1, dg_fpre1) = _rowwise_bwd(_fn_res_pre, [x2, y1], g_c, [dx4, dh3], [F32, BF16],
                                                     ROW_TB, "res_pre2_bwd")
    dcat1 = _matmul(dy1, w_dn_out, "nt", F32, "dn_out_dx")
    dw_dn_out = _matmul(cat1, dy1, "tn", BF16, "dn_out_dw")
    dp1, dconv, da_log, ddt_bias, dout_norm, dkm1, dvm1 = _dn_mixer_bwd(dcat1, res1, km1, vm1)
    dwkv1, dg_mem1 = _mem_kv_bwd(mem0, gain(mem_norm, 1), w_kv[1], memn1, dkm1, dvm1, 1)
    dh2 = _matmul(dp1, w_dn_in_t, "nn", F32, "dn_in_dx")
    dw_dn_in_t = _matmul(dp1, h2, "tn", BF16, "dn_in_dw")
    dn_in_parts = [jnp.pad(dw_dn_in_t[DN_IN_SHARD * j:DN_IN_SHARD * (j + 1)],
                           ((0, DN_IN_SHARD_PAD - DN_IN_SHARD), (0, 0))) for j in range(N_DEV)]
    xch_b, tok = _exchange_begin(
        [jnp.concatenate(dn_in_parts, axis=0)[None], dw_dn_out[None], dwkv1[None], dwgu1[None], dwd1[None]],
        [1, 1, 1, 1, 1], dh2, "exchange_layer1_start")
    (dx1, df0), (dg_fpost0, dg_mpre1) = _rowwise_bwd(_fn_res_pre, [x1, f0], [g + tok[0:1, 0:1] for g in g_b],
                                                     [dx2, dh2], [F32, BF16], ROW_TB, "res_pre1_bwd")
    dh1, dwgu0, dwd0 = _ffn_bwd(df0, h1, w_gu_t0, w_down0, gu0, a0, 0)
    xch_a, tok = _exchange_begin([dwgu0[None], dwd0[None]], [1, 1], dh1, "exchange_ffn0_start")
    (dx0, dy0), (dg_mpost0, dg_fpre0) = _rowwise_bwd(_fn_res_pre, [x0, y0], [g + tok[0:1, 0:1] for g in g_a],
                                                     [dx1, dh1], [F32, BF16], ROW_TB, "res_pre0_bwd")
    dcat0 = _matmul(dy0, w_att_out, "nt", F32, "att_out_dx")
    dw_att_out = _matmul(cat0, dy0, "tn", BF16, "att_out_dw")
    dp0, drel, dkm0, dvm0 = _attn_mixer_bwd(dcat0, res0, km0, vm0)
    dwkv0, dg_mem0 = _mem_kv_bwd(mem0, gain(mem_norm, 0), w_kv[0], memn0, dkm0, dvm0, 0)
    xch_o, tok = _exchange_begin([dw_att_out[None], dwkv0[None]], [1, 1], dp0, "exchange_att_out_start")
    dw_att_in_t = _matmul(dp0, h0, "tn", BF16, "att_in_dw")
    xch_i, tok_i = _exchange_begin([dw_att_in_t[None]], [1], tok, "exchange_att_in_start")
    dh0 = _matmul(dp0, w_att_in_t, "nn", F32, "att_in_dx")
    (grad_x,), (dg_mpre0,) = _rowwise_bwd(_fn_first, [x0], [gain(norm_mix_pre, 0) + tok_i[0:1, 0:1]], [dx0, dh0],
                                          [F32], ROW_TB, "pre0_bwd")

    small_grads = [drel, da_log, ddt_bias, dout_norm, jnp.concatenate([dg_mem0, dg_mem1]),
                   jnp.concatenate([dg_mpre0, dg_mpre1]), jnp.concatenate([dg_mpost0, dg_mpost1]),
                   jnp.concatenate([dg_fpre0, dg_fpre1]), jnp.concatenate([dg_fpost0, dg_fpost1]), dconv]
    (r_small,) = _exchange([_pack_small(small_grads, SMALL_ROWS)], [None], "exchange_last")
    (r_att_in,) = _split_end(xch_i, r_small, "exchange_att_in_wait")
    r_att_out, r_kv0 = _split_end(xch_o, r_small, "exchange_att_out_wait")
    r_gu0, r_down0 = _split_end(xch_a, r_small, "exchange_ffn0_wait")
    r_dn_in, r_dn_out, r_kv1, r_gu1, r_down1 = _split_end(xch_b, r_small, "exchange_layer1_wait")

    def rows(a):
        return a.reshape((-1,) + a.shape[-1:])

    def row_sharded(recv, w, m, v, tb, name):
        outs = _adamw_reduce(recv.reshape((N_DEV, -1) + recv.shape[-1:]), rows(w), rows(m), rows(v), tb, name)
        return [o.reshape(w.shape) for o in outs]

    def col_sharded(recv, w, m, v, tb, name):
        g_t = _reduce8(recv.reshape((N_DEV, -1) + recv.shape[-1:]), tb, name + "_sum")
        g = jnp.swapaxes(g_t.reshape(recv.shape[1:])[:, :w.shape[2]], 1, 2)
        outs = _adamw(rows(g), rows(w), rows(m), rows(v), 256, name)
        return [g] + [o.reshape(w.shape) for o in outs]

    def per_layer(fn, recvs, w, m, v, tb, name):
        outs = [fn(r, w[l:l + 1], m[l:l + 1], v[l:l + 1], tb, f"{name}{l}") for l, r in enumerate(recvs)]
        return [jnp.concatenate(pair, axis=0) for pair in zip(*outs)]

    big = [col_sharded(r_att_in, att_w_in, m_att_w_in, v_att_w_in, 320, "adamw_att_in"),
           row_sharded(r_att_out, att_w_out, m_att_w_out, v_att_w_out, 128, "adamw_att_out"),
           col_sharded(r_dn_in, dn_w_in, m_dn_w_in, v_dn_w_in, 432, "adamw_dn_in"),
           row_sharded(r_dn_out, dn_w_out, m_dn_w_out, v_dn_w_out, 128, "adamw_dn_out"),
           per_layer(row_sharded, [r_kv0, r_kv1], mem_w_kv, m_mem_w_kv, v_mem_w_kv, 128, "adamw_mem_kv"),
           per_layer(col_sharded, [r_gu0, r_gu1], ffn_w_gate_up, m_ffn_w_gate_up, v_ffn_w_gate_up, 176,
                     "adamw_ffn_gu"),
           per_layer(row_sharded, [r_down0, r_down1], ffn_w_down, m_ffn_w_down, v_ffn_w_down, 176,
                     "adamw_ffn_down")]
    g_big, d_big, nm_big, nv_big = [[b[i] for b in big] for i in range(4)]

    g_small = _reduce8(r_small, SMALL_ROWS, "reduce_small")
    rep_shapes = [(32, 12), (1, 2, 6), (1, 2, 6), (1, 128), (2, D), (2, D), (2, D), (2, D), (2, D)]
    *g_rep, g_conv_full = _unpack_small(g_small, rep_shapes + [(DN_CONV, 3 * TOK_W)])
    me = _index(_me_xyc())
    g_conv = lax.dynamic_slice(g_conv_full, (0, me * 288), (DN_CONV, 288)).reshape(CONV_SHARD)
    small_shapes = rep_shapes + [CONV_SHARD]
    small_w = [rel_bias, dn_a_log, dn_dt_bias, dn_out_norm, mem_norm, norm_mix_pre, norm_mix_post,
               norm_ffn_pre, norm_ffn_post, dn_conv]
    small_m = [m_rel_bias, m_dn_a_log, m_dn_dt_bias, m_dn_out_norm, m_mem_norm, m_norm_mix_pre, m_norm_mix_post,
               m_norm_ffn_pre, m_norm_ffn_post, m_dn_conv]
    small_v = [v_rel_bias, v_dn_a_log, v_dn_dt_bias, v_dn_out_norm, v_mem_norm, v_norm_mix_pre, v_norm_mix_post,
               v_norm_ffn_pre, v_norm_ffn_post, v_dn_conv]
    g_small_list = g_rep + [g_conv]
    outs_small = _adamw(_pack_small(g_small_list, 24), _pack_small(small_w, 24), _pack_small(small_m, 24),
                        _pack_small(small_v, 24), 24, "adamw_small")
    d_small, nm_small, nv_small = [_unpack_small(o, small_shapes) for o in outs_small]

    def ordered(small, big):
        return [small[0], big[0], big[1], big[2], small[9], small[1], small[2], small[3], big[3], small[4],
                big[4], small[5], small[6], small[7], small[8], big[5], big[6]]

    g_small_out = [g.reshape(s) for g, s in zip(g_small_list, small_shapes)]
    return (loss, grad_x[None], *ordered(g_small_out, g_big), *ordered(d_small, d_big),
            *ordered(nm_small, nm_big), *ordered(nv_small, nv_big))
```

```python
import functools
import math
from typing import NamedTuple

import numpy as np
import jax
import jax.numpy as jnp
from jax import lax
from jax.experimental import pallas as pl
from jax.experimental.pallas import tpu as pltpu

F32 = jnp.float32
BF16 = jnp.bfloat16
HI = lax.Precision.HIGHEST
MESH = pl.DeviceIdType.MESH

N_DEV = 8
D = 1024
EPS = 1e-6
NEG = -1e30
TOK_W = 768
MEM_W = 256
ATT_HD = 64
DIL_GROUPS = ((128, 1), (512, 4), (2048, 16))
BAND_HALF = 64
REL_BUCKETS = 32
REL_MAX_DIST = 1024
DN_HD = 128
DN_HEADS = 6
DN_CONV = 5
DN_CHUNK = 64
MEM_HEADS = 4
D_FF = 2816
DN_IN = 3352
DN_IN_PAD = 3456

ADAM_LR, ADAM_B1, ADAM_B2, ADAM_EPS, ADAM_WD, ADAM_STEP = 0.001, 0.9, 0.999, 1e-08, 0.01, 10

PACK_C = 512
SMALL_ROWS = 48
VMEM_LIMIT = 48 * 1024 * 1024


def _cparams(sem=None):
    kw = dict(vmem_limit_bytes=VMEM_LIMIT)
    if sem is not None:
        kw["dimension_semantics"] = sem
    return pltpu.CompilerParams(**kw)


def _tile(n, cap):
    if n <= cap:
        return n
    best = None
    for t in range(128, cap + 1, 128):
        if n % t == 0:
            best = t
    assert best is not None, (n, cap)
    return best


def _matmul(a, b, mode, out_dtype, name, tm=1024, tn=1408, tk=None):
    if tk is None:
        tk = 4096 if mode == "tn" else 2816
    if mode == "tn":
        tm = min(tm, 512)
    if mode == "nn":
        (m, kc), (_, n) = a.shape, b.shape
        dims = (((1,), (0,)), ((), ()))
    elif mode == "nt":
        (m, kc), (n, _) = a.shape, b.shape
        dims = (((1,), (1,)), ((), ()))
    else:
        (kc, m), (_, n) = a.shape, b.shape
        dims = (((0,), (0,)), ((), ()))
    tm = m if m <= tm else _tile(m, tm)
    tn = _tile(n, tn)
    tk = _tile(kc, tk)
    nk = kc // tk

    def body(a_ref, b_ref, o_ref, acc_ref):
        k = pl.program_id(2)
        part = lax.dot_general(a_ref[...], b_ref[...], dims, preferred_element_type=F32)

        @pl.when(k == 0)
        def _():
            acc_ref[...] = part

        @pl.when(k > 0)
        def _():
            acc_ref[...] += part

        @pl.when(k == nk - 1)
        def _():
            o_ref[...] = acc_ref[...].astype(o_ref.dtype)

    if mode == "nn":
        a_spec = pl.BlockSpec((tm, tk), lambda i, j, k: (i, k))
        b_spec = pl.BlockSpec((tk, tn), lambda i, j, k: (k, j))
    elif mode == "nt":
        a_spec = pl.BlockSpec((tm, tk), lambda i, j, k: (i, k))
        b_spec = pl.BlockSpec((tn, tk), lambda i, j, k: (j, k))
    else:
        a_spec = pl.BlockSpec((tk, tm), lambda i, j, k: (k, i))
        b_spec = pl.BlockSpec((tk, tn), lambda i, j, k: (k, j))
    return pl.pallas_call(
        body, name=name, grid=(m // tm, n // tn, nk),
        in_specs=[a_spec, b_spec],
        out_specs=pl.BlockSpec((tm, tn), lambda i, j, k: (i, j)),
        out_shape=jax.ShapeDtypeStruct((m, n), out_dtype),
        scratch_shapes=[pltpu.VMEM((tm, tn), F32)],
        compiler_params=_cparams(("parallel", "parallel", "arbitrary")),
    )(a, b)


class _Cols(NamedTuple):
    arr: jax.Array
    width: int
    block: int

    @property
    def shape(self):
        return (self.arr.shape[0], self.width)


def _row_spec(r, tb):
    if isinstance(r, _Cols):
        return pl.BlockSpec((tb, r.width), lambda i, b=r.block: (i, b))
    return pl.BlockSpec((tb, r.shape[1]), lambda i: (i, 0))


def _row_arr(r):
    return r.arr if isinstance(r, _Cols) else r


def _rowwise(fn, rows, params, outs, tb, name):
    t = rows[0].shape[0]
    nr, npar = len(rows), len(params)

    def body(*refs):
        ins = [r[...].astype(F32) for r in refs[:nr + npar]]
        res = fn(*ins)
        for o_ref, r in zip(refs[nr + npar:], res):
            o_ref[...] = r.astype(o_ref.dtype)

    return pl.pallas_call(
        body, name=name, grid=(t // tb,),
        in_specs=[_row_spec(r, tb) for r in rows] + [pl.BlockSpec(p.shape, lambda i: (0, 0)) for p in params],
        out_specs=[pl.BlockSpec((tb, c), lambda i: (i, 0)) for c, _ in outs],
        out_shape=[jax.ShapeDtypeStruct((t, c), dt) for c, dt in outs],
        compiler_params=_cparams(("parallel",)),
    )(*[_row_arr(r) for r in rows], *params)


def _rowwise_bwd(fn, rows, params, cots, row_grad, tb, name):
    t = rows[0].shape[0]
    nr, npar, nc = len(rows), len(params), len(cots)
    want = [i for i, g in enumerate(row_grad) if g is not None]

    def body(*refs):
        ins = [r[...].astype(F32) for r in refs[:nr + npar]]
        cts = tuple(r[...].astype(F32) for r in refs[nr + npar:nr + npar + nc])
        outs = refs[nr + npar + nc:]
        _, vjp = jax.vjp(fn, *ins)
        grads = vjp(cts)
        for o_ref, i in zip(outs[:len(want)], want):
            o_ref[...] = grads[i].astype(o_ref.dtype)
        first = pl.program_id(0) == 0
        for o_ref, g in zip(outs[len(want):], grads[nr:]):
            @pl.when(first)
            def _(o_ref=o_ref, g=g):
                o_ref[...] = g

            @pl.when(jnp.logical_not(first))
            def _(o_ref=o_ref, g=g):
                o_ref[...] += g

    res = pl.pallas_call(
        body, name=name, grid=(t // tb,),
        in_specs=[_row_spec(r, tb) for r in rows] + [pl.BlockSpec(p.shape, lambda i: (0, 0)) for p in params]
        + [_row_spec(c, tb) for c in cots],
        out_specs=[pl.BlockSpec((tb, rows[i].shape[1]), lambda i_: (i_, 0)) for i in want]
        + [pl.BlockSpec(p.shape, lambda i: (0, 0)) for p in params],
        out_shape=[jax.ShapeDtypeStruct(tuple(rows[i].shape), row_grad[i]) for i in want]
        + [jax.ShapeDtypeStruct(p.shape, F32) for p in params],
        compiler_params=_cparams(("arbitrary",)),
    )(*[_row_arr(r) for r in rows], *params, *[_row_arr(c) for c in cots])
    return list(res[:len(want)]), list(res[len(want):])


def _rms(x, g):
    return x * lax.rsqrt(jnp.mean(x * x, axis=-1, keepdims=True) + EPS) * g


def _fn_pre(x, g):
    return (_rms(x, g),)


def _fn_res_pre(x, y, g_post, g_pre):
    x1 = x + _rms(y, g_post)
    return x1, _rms(x1, g_pre)


def _fn_res(x, y, g_post):
    return (x + _rms(y, g_post),)


def _sigmoid(x):
    return 1.0 / (1.0 + jnp.exp(-x))


def _silu(x):
    return x * _sigmoid(x)


def _fn_swiglu(gu):
    return (_silu(gu[:, :D_FF]) * gu[:, D_FF:],)


def _fn_combine(o, lse):
    ls = [lse[:, 256 * g:256 * (g + 1)] for g in range(3)]
    mx = lax.stop_gradient(jnp.maximum(jnp.maximum(ls[0], ls[1]), ls[2]))
    es = [jnp.exp(l - mx) for l in ls]
    inv = 1.0 / (es[0] + es[1] + es[2])
    return (jnp.concatenate([o[:, 256 * g:256 * (g + 1)] * (es[g] * inv) for g in range(3)], axis=1),)


def _fn_outnorm(o_f, o_r, z, gain):
    res = []
    for h in range(DN_HEADS):
        sl = slice(DN_HD * h, DN_HD * (h + 1))
        o = o_f[:, sl] + o_r[:, sl]
        res.append(o * lax.rsqrt(jnp.mean(o * o, axis=-1, keepdims=True) + EPS) * gain * _silu(z[:, sl]))
    return (jnp.concatenate(res, axis=1),)


def _loss_kernel(x, tgt, tb, name):
    t, d = x.shape

    def body(x_ref, t_ref, dx_ref, l_ref, acc_ref):
        i = pl.program_id(0)
        e = x_ref[...] - t_ref[...]
        dx_ref[...] = e * (1.0 / d)
        part = jnp.sum(e * e, axis=0, keepdims=True)

        @pl.when(i == 0)
        def _():
            acc_ref[...] = part

        @pl.when(i > 0)
        def _():
            acc_ref[...] += part

        @pl.when(i == t // tb - 1)
        def _():
            l_ref[...] = jnp.broadcast_to(jnp.sum(acc_ref[...], axis=-1, keepdims=True), (1, 128))

    return pl.pallas_call(
        body, name=name, grid=(t // tb,),
        in_specs=[pl.BlockSpec((tb, d), lambda i: (i, 0))] * 2,
        out_specs=[pl.BlockSpec((tb, d), lambda i: (i, 0)), pl.BlockSpec((1, 128), lambda i: (0, 0))],
        out_shape=[jax.ShapeDtypeStruct((t, d), F32), jax.ShapeDtypeStruct((1, 128), F32)],
        scratch_shapes=[pltpu.VMEM((1, d), F32)],
        compiler_params=_cparams(("arbitrary",)),
    )(x, tgt)


def _band_fn(l_sub, bq, i, q, kw, vw, bm):
    w = bq + 2 * BAND_HALF
    s = lax.dot_general((q * (ATT_HD ** -0.5)).astype(BF16), kw.astype(BF16), (((2,), (2,)), ((0,), (0,))),
                        preferred_element_type=F32) + bm
    kpos = i * bq - BAND_HALF + lax.broadcasted_iota(jnp.int32, (4, bq, w), 2)
    s = jnp.where((kpos >= 0) & (kpos < l_sub), s, NEG)
    m = lax.stop_gradient(jnp.max(s, axis=-1, keepdims=True))
    p = jnp.exp(s - m)
    den = jnp.sum(p, axis=-1, keepdims=True)
    o = lax.dot_general(p.astype(BF16), vw.astype(BF16), (((2,), (1,)), ((0,), (0,))),
                        preferred_element_type=F32) / den
    return o, jnp.broadcast_to(m + jnp.log(den), o.shape)


def _band_specs(l_sub, bq):
    w = bq + 2 * BAND_HALF
    qs = pl.BlockSpec((None, 4, bq, ATT_HD), lambda r, i: (r, 0, i, 0))
    ks = pl.BlockSpec((None, 4, l_sub + 2 * BAND_HALF, ATT_HD), lambda r, i: (r, 0, 0, 0))
    bs = pl.BlockSpec((4, bq, w), lambda r, i: (0, 0, 0))
    return qs, ks, bs


def _band_fwd(q, k, v, bm, dil, l_sub, bq, name):
    w = bq + 2 * BAND_HALF
    qs, ks, bs = _band_specs(l_sub, bq)

    def body(q_ref, k_ref, v_ref, bm_ref, o_ref, l_ref):
        i = pl.program_id(1)
        st = pl.multiple_of(i * bq, bq)
        o, lse = _band_fn(l_sub, bq, i, q_ref[...].astype(F32), k_ref[:, pl.ds(st, w), :].astype(F32),
                          v_ref[:, pl.ds(st, w), :].astype(F32), bm_ref[...])
        o_ref[...] = o
        l_ref[...] = lse

    return pl.pallas_call(
        body, name=name, grid=(dil, l_sub // bq),
        in_specs=[qs, ks, ks, bs], out_specs=[qs, qs],
        out_shape=[jax.ShapeDtypeStruct(q.shape, F32)] * 2,
        compiler_params=_cparams(("parallel", "arbitrary")),
    )(q, k, v, bm)


def _band_bwd(q, k, v, bm, do, dlse, dil, l_sub, bq, name):
    w = bq + 2 * BAND_HALF
    qs, ks, bs = _band_specs(l_sub, bq)

    def body(q_ref, k_ref, v_ref, bm_ref, do_ref, dl_ref, dq_ref, dk_ref, dv_ref, dbm_ref):
        r, i = pl.program_id(0), pl.program_id(1)
        st = pl.multiple_of(i * bq, bq)
        _, vjp = jax.vjp(functools.partial(_band_fn, l_sub, bq, i),
                         q_ref[...].astype(F32), k_ref[:, pl.ds(st, w), :].astype(F32),
                         v_ref[:, pl.ds(st, w), :].astype(F32), bm_ref[...])
        dq, dkw, dvw, dbm = vjp((do_ref[...].astype(F32), dl_ref[...]))
        dq_ref[...] = dq.astype(dq_ref.dtype)

        @pl.when(i == 0)
        def _():
            dk_ref[...] = jnp.zeros_like(dk_ref)
            dv_ref[...] = jnp.zeros_like(dv_ref)

        dk_ref[:, pl.ds(st, w), :] += dkw
        dv_ref[:, pl.ds(st, w), :] += dvw

        @pl.when((i == 0) & (r == 0))
        def _():
            dbm_ref[...] = dbm

        @pl.when((i > 0) | (r > 0))
        def _():
            dbm_ref[...] += dbm

    return pl.pallas_call(
        body, name=name, grid=(dil, l_sub // bq),
        in_specs=[qs, ks, ks, bs, qs, qs], out_specs=[qs, ks, ks, bs],
        out_shape=[jax.ShapeDtypeStruct(q.shape, BF16), jax.ShapeDtypeStruct(k.shape, F32),
                   jax.ShapeDtypeStruct(k.shape, F32), jax.ShapeDtypeStruct(bm.shape, F32)],
        compiler_params=_cparams(("arbitrary", "arbitrary")),
    )(q, k, v, bm, do, dlse)


def _t5_bucket(rel):
    half = REL_BUCKETS // 2
    max_exact = half // 2
    n = np.abs(rel)
    large = max_exact + (np.log(np.maximum(n, 1) / max_exact) / math.log(REL_MAX_DIST / max_exact)
                         * (half - max_exact)).astype(np.int64)
    large = np.minimum(large, half - 1)
    return ((rel > 0) * half + np.where(n < max_exact, n, large)).astype(np.int32)


def _bucket_onehot(dil):
    idx = _t5_bucket(np.arange(-BAND_HALF, BAND_HALF + 1) * dil)
    oh = np.zeros((2 * BAND_HALF + 1, REL_BUCKETS), np.float32)
    oh[np.arange(2 * BAND_HALF + 1), idx] = 1.0
    return oh


def _band_bias(rel_bias, gi, dil, bq):
    w = bq + 2 * BAND_HALF
    nb = 2 * BAND_HALF + 1
    bias = jnp.dot(jnp.asarray(_bucket_onehot(dil)), rel_bias[:, 4 * gi:4 * gi + 4], precision=HI)
    row = jnp.concatenate([bias.T, jnp.full((4, w + 1 - nb), NEG, F32)], axis=1)
    flat = jnp.tile(row, (1, bq))[:, :bq * w]
    return flat.reshape(4, bq, w)


def _relbias_grad(dbms, name):
    nb = 2 * BAND_HALF + 1
    bq = max(d.shape[1] for d in dbms)
    skew = []
    for dbm in dbms:
        bqg, w = dbm.shape[1], dbm.shape[2]
        flat = jnp.pad(dbm.reshape(4, bqg * w), ((0, 0), (0, bqg)))
        skew.append(jnp.pad(flat.reshape(4, bqg, w + 1)[:, :, :nb], ((0, 0), (0, bq - bqg), (0, 256 - nb))))
    sk = jnp.concatenate(skew, axis=0)
    oh = np.zeros((3, 256, 128), np.float32)
    for gi, (_, dil) in enumerate(DIL_GROUPS):
        oh[gi, :2 * BAND_HALF + 1, :REL_BUCKETS] = _bucket_onehot(dil)

    def body(s_ref, oh_ref, o_ref):
        col = jnp.sum(s_ref[...], axis=0, keepdims=True)
        o_ref[...] = jnp.dot(jnp.broadcast_to(col, (8, 256)), oh_ref[...], precision=HI, preferred_element_type=F32)

    out = pl.pallas_call(
        body, name=name, grid=(12,),
        in_specs=[pl.BlockSpec((None, bq, 256), lambda n: (n, 0, 0)),
                  pl.BlockSpec((None, 256, 128), lambda n: (n // 4, 0, 0))],
        out_specs=pl.BlockSpec((None, 8, 128), lambda n: (n, 0, 0)),
        out_shape=jax.ShapeDtypeStruct((12, 8, 128), F32),
        compiler_params=_cparams(("parallel",)),
    )(sk, jnp.asarray(oh))
    return out[:, 0, :REL_BUCKETS].T


def _mem_fn(q, k, v):
    s = lax.dot_general((q * (ATT_HD ** -0.5)).astype(BF16), k.astype(BF16), (((2,), (2,)), ((0,), (0,))),
                        preferred_element_type=F32)
    m = lax.stop_gradient(jnp.max(s, axis=-1, keepdims=True))
    p = jnp.exp(s - m)
    p = p / jnp.sum(p, axis=-1, keepdims=True)
    return lax.dot_general(p.astype(BF16), v.astype(BF16), (((2,), (1,)), ((0,), (0,))), preferred_element_type=F32)


def _mem_specs(tb, ml):
    qs = pl.BlockSpec((MEM_HEADS, tb, ATT_HD), lambda i: (0, i, 0))
    ks = pl.BlockSpec((MEM_HEADS, ml, ATT_HD), lambda i: (0, 0, 0))
    return qs, ks


def _mem_fwd(q, k, v, tb, name):
    qs, ks = _mem_specs(tb, k.shape[1])

    def body(q_ref, k_ref, v_ref, o_ref):
        o_ref[...] = _mem_fn(q_ref[...].astype(F32), k_ref[...], v_ref[...])

    return pl.pallas_call(
        body, name=name, grid=(q.shape[1] // tb,),
        in_specs=[qs, ks, ks], out_specs=qs, out_shape=jax.ShapeDtypeStruct(q.shape, F32),
        compiler_params=_cparams(("parallel",)),
    )(q, k, v)


def _mem_bwd(q, k, v, do, tb, name):
    qs, ks = _mem_specs(tb, k.shape[1])

    def body(q_ref, k_ref, v_ref, do_ref, dq_ref, dk_ref, dv_ref):
        i = pl.program_id(0)
        _, vjp = jax.vjp(_mem_fn, q_ref[...].astype(F32), k_ref[...], v_ref[...])
        dq, dk, dv = vjp(do_ref[...])
        dq_ref[...] = dq

        @pl.when(i == 0)
        def _():
            dk_ref[...] = dk
            dv_ref[...] = dv

        @pl.when(i > 0)
        def _():
            dk_ref[...] += dk
            dv_ref[...] += dv

    return pl.pallas_call(
        body, name=name, grid=(q.shape[1] // tb,),
        in_specs=[qs, ks, ks, qs], out_specs=[qs, ks, ks],
        out_shape=[jax.ShapeDtypeStruct(q.shape, F32), jax.ShapeDtypeStruct(k.shape, F32),
                   jax.ShapeDtypeStruct(k.shape, F32)],
        compiler_params=_cparams(("arbitrary",)),
    )(q, k, v, do)


CONV_PAD = 8


def _conv_post(kind, acc):
    s = _silu(acc)
    if kind == 2:
        return s
    scale = DN_HD ** -0.5 if kind == 0 else 1.0
    return s * lax.rsqrt(jnp.sum(s * s, axis=-1, keepdims=True) + EPS) * scale


def _conv_rows(x_ref, t, start, rt):
    lo = min(max(start, 0), t - rt)
    x = x_ref[pl.ds(lo, rt), :]
    shift = lo - start
    if shift == 0:
        return x
    x = pltpu.roll(x, shift % rt, axis=0)
    row = lax.broadcasted_iota(jnp.int32, x.shape, 0)
    return jnp.where((row >= shift) if shift > 0 else (row < rt + shift), x, 0.0)


def _conv_acc(x_ref, t, w, r0, rt):
    acc = None
    for i in range(DN_CONV):
        term = w[i:i + 1, :] * _conv_rows(x_ref, t, r0 + i - DN_CONV // 2, rt)
        acc = term if acc is None else acc + term
    return acc


def _conv_fwd(x, w8, kind, rt, name):
    t = x.shape[0]

    def body(x_ref, w_ref, o_ref):
        w = w_ref[...]
        for r in range(t // rt):
            o_ref[pl.ds(r * rt, rt), :] = _conv_post(kind, _conv_acc(x_ref, t, w, r * rt, rt))

    return pl.pallas_call(
        body, name=name, grid=(DN_HEADS,),
        in_specs=[pl.BlockSpec((t, DN_HD), lambda j: (0, 6 * kind + j)),
                  pl.BlockSpec((8, DN_HD), lambda j: (0, 6 * kind + j))],
        out_specs=pl.BlockSpec((t, DN_HD), lambda j: (0, j)),
        out_shape=jax.ShapeDtypeStruct((t, TOK_W), F32),
        compiler_params=_cparams(("parallel",)),
    )(x, w8)


def _conv_bwd(x, w8, d_f, d_r, dp, kind, rt, name):
    t = x.shape[0]

    def body(xp_ref, w_ref, df_ref, dr_ref, dp_in, dx_ref, dw_ref, dpad_ref):
        del dp_in
        w = w_ref[...]
        zero = jnp.zeros((CONV_PAD, DN_HD), F32)
        dpad_ref[pl.ds(0, CONV_PAD), :] = zero
        dpad_ref[pl.ds(CONV_PAD + t, CONV_PAD), :] = zero
        dw = [jnp.zeros((1, DN_HD), F32) for _ in range(DN_CONV)]
        for r in range(t // rt):
            rows = pl.ds(r * rt, rt)
            acc = _conv_acc(xp_ref, t, w, r * rt, rt)
            _, vjp = jax.vjp(functools.partial(_conv_post, kind), acc)
            (dacc,) = vjp(df_ref[rows, :] + dr_ref[rows, :])
            dpad_ref[pl.ds(CONV_PAD + r * rt, rt), :] = dacc
            for i in range(DN_CONV):
                xs = _conv_rows(xp_ref, t, r * rt + i - DN_CONV // 2, rt)
                dw[i] = dw[i] + jnp.sum(dacc * xs, axis=0, keepdims=True)
        dw_ref[...] = jnp.concatenate(dw + [jnp.zeros((8 - DN_CONV, DN_HD), F32)], axis=0)
        for r in range(t // rt):
            acc = None
            for i in range(DN_CONV):
                term = w[i:i + 1, :] * dpad_ref[pl.ds(CONV_PAD + r * rt - i + DN_CONV // 2, rt), :]
                acc = term if acc is None else acc + term
            dx_ref[pl.ds(r * rt, rt), :] = acc.astype(dx_ref.dtype)

    return pl.pallas_call(
        body, name=name, grid=(DN_HEADS,),
        in_specs=[pl.BlockSpec((t, DN_HD), lambda j: (0, 6 * kind + j)),
                  pl.BlockSpec((8, DN_HD), lambda j: (0, 6 * kind + j)),
                  pl.BlockSpec((t, DN_HD), lambda j: (0, j)),
                  pl.BlockSpec((t, DN_HD), lambda j: (0, j)),
                  pl.BlockSpec(memory_space=pl.ANY)],
        out_specs=[pl.BlockSpec((t, DN_HD), lambda j: (0, 6 * kind + j)),
                   pl.BlockSpec((8, DN_HD), lambda j: (0, j))],
        out_shape=[jax.ShapeDtypeStruct(dp.shape, dp.dtype), jax.ShapeDtypeStruct((8, TOK_W), F32)],
        input_output_aliases={4: 0},
        scratch_shapes=[pltpu.VMEM((t + 2 * CONV_PAD, DN_HD), F32)],
        compiler_params=_cparams(("parallel",)),
    )(x, w8, d_f, d_r, dp)


def _softplus(x):
    e = jnp.exp(-jnp.abs(x))
    return jnp.maximum(x, 0.0) + jnp.where(e < 1e-4, e - 0.5 * e * e, jnp.log(1.0 + e))


_NN = (((2,), (1,)), ((0,), (0,)))
_NT = (((2,), (2,)), ((0,), (0,)))
_TN = (((1,), (1,)), ((0,), (0,)))


def _dot(a, b, dims=_NN):
    return lax.dot_general(a.astype(BF16), b.astype(BF16), dims, preferred_element_type=F32)


def _hi_lo(x):
    hi = x.astype(BF16)
    return hi, (x - hi.astype(F32)).astype(BF16)


def _mask_dot(mask_bf16, x, dims):
    x1 = x.astype(BF16)
    r = x - x1.astype(F32)
    x2, x3 = _hi_lo(r)
    d = functools.partial(lax.dot_general, dimension_numbers=dims, preferred_element_type=F32)
    return d(mask_bf16, x1) + d(mask_bf16, x2) + d(mask_bf16, x3)


@jax.custom_vjp
def _dot_mask(mask_bf16, x):
    return _mask_dot(mask_bf16, x, _NN)


def _dot_mask_fwd(mask_bf16, x):
    return _mask_dot(mask_bf16, x, _NN), mask_bf16


def _dot_mask_bwd(mask_bf16, ct):
    return jnp.zeros_like(mask_bf16), _mask_dot(mask_bf16, ct, _TN)


_dot_mask.defvjp(_dot_mask_fwd, _dot_mask_bwd)


def _dot3_raw(a, b, dims):
    a1, a2 = _hi_lo(a)
    b1, b2 = _hi_lo(b)
    d = functools.partial(lax.dot_general, dimension_numbers=dims, preferred_element_type=F32)
    return d(a1, b1) + d(a1, b2) + d(a2, b1)


def _unit_solve_pass(lmat, rhs, masks):
    ainv = masks[6] - lmat * masks[0]
    for sh in range(1, 6):
        ainv = ainv - _dot(_dot(ainv, lmat * masks[sh]), ainv)
    return _dot3_raw(ainv, rhs, _NN), ainv


@jax.custom_vjp
def _unit_solve(lmat, rhs, masks):
    return _unit_solve_pass(lmat, rhs, masks)[0]


def _unit_solve_fwd(lmat, rhs, masks):
    sol, ainv = _unit_solve_pass(lmat, rhs, masks)
    return sol, (sol, ainv, masks)


def _unit_solve_bwd(res, ct):
    sol, ainv, masks = res
    d_rhs = _dot3_raw(ainv, ct, _TN)
    return -_dot3_raw(d_rhs, sol, _NT), d_rhs, tuple(jnp.zeros_like(m) for m in masks)


_unit_solve.defvjp(_unit_solve_fwd, _unit_solve_bwd)


def _block_masks(rev, row, col):
    c = DN_CHUNK
    prow = jnp.where(rev, c - 1 - row, row)
    pcol = jnp.where(rev, c - 1 - col, col)
    masks = []
    for sh in range(6):
        differ = (prow ^ pcol) >> sh
        miss = (differ ^ 1) + (1 - ((prow >> sh) & 1))
        masks.append(jnp.where(miss == 0, 1.0, 0.0))
    masks.append(jnp.where(row == col, 1.0, 0.0))
    return tuple(masks)


def _dn_chunk(q, k, v, al, be, alc, a_row, dt_row, a_rowc, dt_rowc, s):
    n, c = q.shape[0], DN_CHUNK
    rev = lax.broadcasted_iota(jnp.int32, (n, c, c), 0) >= n // 2
    row = lax.broadcasted_iota(jnp.int32, (n, c, c), 1)
    col = lax.broadcasted_iota(jnp.int32, (n, c, c), 2)
    ahead = jnp.where(rev, col - row, row - col)
    incl = ahead >= 0
    strict = ahead > 0
    incl_b = incl.astype(BF16)

    g = -jnp.exp(a_row) * _softplus(al + dt_row)
    beta = _sigmoid(be)
    g_c = -jnp.exp(a_rowc) * _softplus(alc + dt_rowc)
    gc = _dot_mask(incl_b, g)
    gcc = _dot_mask(incl_b, g_c)
    decay = jnp.exp(jnp.where(incl, gcc - jnp.swapaxes(gcc, 1, 2), NEG))
    kb = k * beta
    lmat = jnp.where(strict, _dot(kb, k, _NT) * decay, 0.0)
    rhs = jnp.concatenate([v * beta, kb * jnp.exp(gc)], axis=2)
    sol = _unit_solve(lmat, rhs, _block_masks(rev, row, col))
    u, w = sol[:, :, :DN_HD], sol[:, :, DN_HD:]
    intra = jnp.where(incl, _dot(q, k, _NT) * decay, 0.0)
    v_new = u - _dot(w, s)
    out = _dot(q * jnp.exp(gc), s) + _dot(intra, v_new)
    g_last = jnp.sum(g, axis=1, keepdims=True)
    s_new = s * jnp.exp(g_last) + _dot(k * jnp.exp(g_last - gc), v_new, _TN)
    return out, s_new


DN_HG = 6


def _dn_load(f_refs, r_refs, alf, bef, alr, ber, a_ref, dt_ref):
    c, hg = DN_CHUNK, DN_HG
    sls = [slice(DN_HD * h, DN_HD * (h + 1)) for h in range(hg)]
    toks = [jnp.stack([f[:, sl] for sl in sls] + [r[:, sl] for sl in sls]) for f, r in zip(f_refs, r_refs)]
    al = jnp.concatenate([alf[...], alr[...]], axis=0)
    be = jnp.concatenate([bef[...], ber[...]], axis=0)
    alc = jnp.concatenate([alf[:, :, 0:c], alr[:, :, 0:c]], axis=0)
    a = jnp.concatenate([a_ref[0], a_ref[1]], axis=0)
    dt = jnp.concatenate([dt_ref[0], dt_ref[1]], axis=0)
    ac = jnp.concatenate([a_ref[0, :, :, 0:c], a_ref[1, :, :, 0:c]], axis=0)
    dtc = jnp.concatenate([dt_ref[0, :, :, 0:c], dt_ref[1, :, :, 0:c]], axis=0)
    return toks, (al, be, alc, a, dt, ac, dtc)


def _dn_views(nc, bwd):
    c, hg = DN_CHUNK, DN_HG
    if bwd:
        f_blk = lambda s: nc - 1 - s
        r_blk = lambda s: s
        st_blk = lambda s: nc - 1 - s
    else:
        f_blk = lambda s: s
        r_blk = lambda s: nc - 1 - s
        st_blk = lambda s: s
    tok_f = pl.BlockSpec((c, hg * DN_HD), lambda g, s: (f_blk(s), g))
    tok_r = pl.BlockSpec((c, hg * DN_HD), lambda g, s: (r_blk(s), g))
    gate_f = pl.BlockSpec((None, hg, c, DN_HD), lambda g, s: (0, g, f_blk(s), 0))
    gate_r = pl.BlockSpec((None, hg, c, DN_HD), lambda g, s: (1, g, r_blk(s), 0))
    par = pl.BlockSpec((2, hg, 1, DN_HD), lambda g, s: (0, g, 0, 0))
    state = pl.BlockSpec((2, hg, None, DN_HD, DN_HD), lambda g, s: (0, g, st_blk(s), 0, 0))
    return tok_f, tok_r, gate_f, gate_r, par, state


def _dn_fwd(q, k, v, al, be, a_rows, dt_rows, name):
    t = q.shape[0]
    c, hg = DN_CHUNK, DN_HG
    nc = t // c
    tok_f, tok_r, gate_f, gate_r, par, state = _dn_views(nc, False)

    def body(qf, kf, vf, qr, kr, vr, alf, bef, alr, ber, a_ref, dt_ref, of_ref, or_ref, st_ref, s_ref):
        @pl.when(pl.program_id(1) == 0)
        def _():
            s_ref[...] = jnp.zeros_like(s_ref)

        (q_, k_, v_), gates = _dn_load((qf, kf, vf), (qr, kr, vr), alf, bef, alr, ber, a_ref, dt_ref)
        s = s_ref[...]
        st_ref[0] = s[:hg]
        st_ref[1] = s[hg:]
        out, s_new = _dn_chunk(q_, k_, v_, *gates, s)
        for h in range(hg):
            sl = slice(DN_HD * h, DN_HD * (h + 1))
            of_ref[:, sl] = out[h]
            or_ref[:, sl] = out[hg + h]
        s_ref[...] = s_new

    return pl.pallas_call(
        body, name=name, grid=(DN_HEADS // hg, nc),
        in_specs=[tok_f] * 3 + [tok_r] * 3 + [gate_f, gate_f, gate_r, gate_r, par, par],
        out_specs=[tok_f, tok_r, state],
        out_shape=[jax.ShapeDtypeStruct((t, TOK_W), F32)] * 2
        + [jax.ShapeDtypeStruct((2, DN_HEADS, nc, DN_HD, DN_HD), F32)],
        scratch_shapes=[pltpu.VMEM((2 * hg, DN_HD, DN_HD), F32)],
        compiler_params=_cparams(("parallel", "arbitrary")),
    )(q, k, v, q, k, v, al, be, al, be, a_rows, dt_rows)


def _dn_bwd(q, k, v, al, be, a_rows, dt_rows, states, do, name):
    t = q.shape[0]
    c, hg = DN_CHUNK, DN_HG
    assert hg == DN_HEADS
    nc = t // c
    tok_f, tok_r, gate_f, gate_r, par, state = _dn_views(nc, True)
    gout_f = pl.BlockSpec((c, DN_HD), lambda g, s: (nc - 1 - s, 0))
    gout_r = pl.BlockSpec((c, DN_HD), lambda g, s: (s, 0))

    def body(qf, kf, vf, qr, kr, vr, alf, bef, alr, ber, a_ref, dt_ref, st_ref, dof, dor,
             dqf, dkf, dvf, dqr, dkr, dvr, dgf, dgr, da_ref, ddt_ref, ds_ref):
        first = pl.program_id(1) == 0

        @pl.when(first)
        def _():
            ds_ref[...] = jnp.zeros_like(ds_ref)
            da_ref[...] = jnp.zeros_like(da_ref)
            ddt_ref[...] = jnp.zeros_like(ddt_ref)

        def lanes(x):
            return jnp.sum(x, axis=-1, keepdims=True)

        (q_, k_, v_, do_), gates = _dn_load((qf, kf, vf, dof), (qr, kr, vr, dor), alf, bef, alr, ber, a_ref, dt_ref)
        s = jnp.concatenate([st_ref[0], st_ref[1]], axis=0)
        _, vjp = jax.vjp(_dn_chunk, q_, k_, v_, *gates, s)
        dq, dk, dv, dal, dbe, dalc, da, ddt, dac, ddtc, ds = vjp((do_, ds_ref[...]))
        for h in range(hg):
            sl = slice(DN_HD * h, DN_HD * (h + 1))
            dqf[:, sl], dkf[:, sl], dvf[:, sl] = dq[h], dk[h], dv[h]
            dqr[:, sl], dkr[:, sl], dvr[:, sl] = dq[hg + h], dk[hg + h], dv[hg + h]
        dal, dbe = lanes(dal) + lanes(dalc), lanes(dbe)
        lane = lax.broadcasted_iota(jnp.int32, (c, DN_HD), 1)
        for d, dg_ref in enumerate((dgf, dgr)):
            dg = jnp.zeros((c, DN_HD), F32)
            for h in range(hg):
                dg = jnp.where(lane == h, dal[d * hg + h], jnp.where(lane == hg + h, dbe[d * hg + h], dg))
            dg_ref[...] = dg
        da = jnp.broadcast_to(lanes(da) + lanes(dac), da.shape)
        ddt = jnp.broadcast_to(lanes(ddt) + lanes(ddtc), ddt.shape)
        da_ref[0] += da[:hg]
        da_ref[1] += da[hg:]
        ddt_ref[0] += ddt[:hg]
        ddt_ref[1] += ddt[hg:]
        ds_ref[...] = ds

    tok = jax.ShapeDtypeStruct((t, TOK_W), F32)
    gate = jax.ShapeDtypeStruct((t, DN_HD), F32)
    parsh = jax.ShapeDtypeStruct((2, DN_HEADS, 1, DN_HD), F32)
    res = pl.pallas_call(
        body, name=name, grid=(DN_HEADS // hg, nc),
        in_specs=[tok_f] * 3 + [tok_r] * 3 + [gate_f, gate_f, gate_r, gate_r, par, par, state, tok_f, tok_r],
        out_specs=[tok_f] * 3 + [tok_r] * 3 + [gout_f, gout_r, par, par],
        out_shape=[tok] * 6 + [gate] * 2 + [parsh] * 2,
        scratch_shapes=[pltpu.VMEM((2 * hg, DN_HD, DN_HD), F32)],
        compiler_params=_cparams(("parallel", "arbitrary")),
    )(q, k, v, q, k, v, al, be, al, be, a_rows, dt_rows, states, do, do)
    dqf, dkf, dvf, dqr, dkr, dvr, dgf, dgr, da, ddt = res
    dgate = jnp.concatenate([dgf[:, :2 * DN_HEADS], dgr[:, :2 * DN_HEADS]], axis=1)
    return (dqf, dkf, dvf), (dqr, dkr, dvr), dgate, da, ddt


BAND_BQ = 256
ROW_TB = 256
MEM_TB = 512
CONV_RT = 512


def _to_sub(x, dil):
    l = x.shape[0] // dil
    return x.reshape(l, dil, 4, ATT_HD).transpose(1, 2, 0, 3)


def _from_sub(x, dil):
    l = x.shape[2]
    return x.transpose(2, 0, 1, 3).reshape(l * dil, 4 * ATT_HD)


def _sub_in(x, col_block, dil, pad, dtype, name):
    if dil > 1:
        y = _to_sub(x[:, 256 * col_block:256 * (col_block + 1)], dil).astype(dtype)
        return jnp.pad(y, ((0, 0), (0, 0), (pad, pad), (0, 0))) if pad else y
    t = x.shape[0]

    def body(x_ref, o_ref):
        if pad:
            zero = jnp.zeros((4, pad, ATT_HD), dtype)
            o_ref[0, :, 0:pad, :] = zero
            o_ref[0, :, pad + t:pad + t + pad, :] = zero
        for h in range(4):
            o_ref[0, h, pad:pad + t, :] = x_ref[:, ATT_HD * h:ATT_HD * (h + 1)].astype(dtype)

    return pl.pallas_call(
        body, name=name, grid=(1,), in_specs=[pl.BlockSpec((t, 256), lambda i: (0, col_block))],
        out_specs=pl.BlockSpec((1, 4, t + 2 * pad, ATT_HD), lambda i: (0, 0, 0, 0)),
        out_shape=jax.ShapeDtypeStruct((1, 4, t + 2 * pad, ATT_HD), dtype), compiler_params=_cparams(("arbitrary",)),
    )(x)


def _sub_out(x, dil, pad, dtype, name):
    if dil > 1:
        return _from_sub(x[:, :, pad:x.shape[2] - pad] if pad else x, dil).astype(dtype)
    t = x.shape[2] - 2 * pad

    def body(x_ref, o_ref):
        for h in range(4):
            o_ref[:, ATT_HD * h:ATT_HD * (h + 1)] = x_ref[0, h, pad:pad + t, :].astype(dtype)

    return pl.pallas_call(
        body, name=name, grid=(1,), in_specs=[pl.BlockSpec(x.shape, lambda i: (0, 0, 0, 0))],
        out_specs=pl.BlockSpec((t, 256), lambda i: (0, 0)), out_shape=jax.ShapeDtypeStruct((t, 256), dtype),
        compiler_params=_cparams(("arbitrary",)),
    )(x)


def _heads_major(x):
    return x.reshape(x.shape[0], MEM_HEADS, ATT_HD).transpose(1, 0, 2)


def _heads_minor(x):
    return x.transpose(1, 0, 2).reshape(x.shape[1], MEM_HEADS * ATT_HD)


def _mem_kv_fwd(mem, gain, w_kv, li):
    (memn,) = _rowwise(_fn_pre, [mem], [gain], [(D, BF16)], mem.shape[0], f"memnorm_fwd{li}")
    kv = _matmul(memn, w_kv, "nn", F32, f"memkv_fwd{li}")
    return _heads_major(kv[:, :MEM_W]), _heads_major(kv[:, MEM_W:]), memn


def _mem_kv_bwd(mem, gain, w_kv, memn, dkm, dvm, li):
    dkv = jnp.concatenate([_heads_minor(dkm), _heads_minor(dvm)], axis=1).astype(BF16)
    dw = _matmul(memn, dkv, "tn", BF16, f"memkv_dw{li}")
    dmemn = _matmul(dkv, w_kv, "nt", F32, f"memkv_dx{li}")
    _, (dgain,) = _rowwise_bwd(_fn_pre, [mem], [gain], [dmemn], [None], mem.shape[0], f"memnorm_bwd{li}")
    return dw, dgain


def _attn_mixer_fwd(p, rel_bias, kv_fn):
    t = p.shape[0]
    saved, outs, lses = [], [], []
    for gi, (_, dil) in enumerate(DIL_GROUPS):
        l_sub = t // dil
        bq = min(BAND_BQ, l_sub)
        q = _sub_in(p, gi, dil, 0, p.dtype, f"sub_q{gi}")
        k = _sub_in(p, 3 + gi, dil, BAND_HALF, p.dtype, f"sub_k{gi}")
        v = _sub_in(p, 6 + gi, dil, BAND_HALF, p.dtype, f"sub_v{gi}")
        bm = _band_bias(rel_bias, gi, dil, bq)
        o, lse = _band_fwd(q, k, v, bm, dil, l_sub, bq, f"band_fwd{gi}")
        outs.append(_sub_out(o, dil, 0, F32, f"sub_o{gi}"))
        lses.append(_sub_out(lse, dil, 0, F32, f"sub_lse{gi}"))
        saved.append((q, k, v, bm))
    o_all = jnp.concatenate(outs, axis=1)
    lse_all = jnp.concatenate(lses, axis=1)
    (mixed,) = _rowwise(_fn_combine, [o_all, lse_all], [], [(TOK_W, BF16)], ROW_TB, "combine_fwd")
    qm = _heads_major(p[:, 3 * TOK_W:])
    km, vm, memn = kv_fn(mixed)
    memo = _mem_fwd(qm, km, vm, min(MEM_TB, t), "mem_fwd0")
    cat = jnp.concatenate([mixed, _heads_minor(memo).astype(BF16)], axis=1)
    return cat, (saved, o_all, lse_all, qm), (km, vm, memn)


def _attn_mixer_bwd(dcat, res, km, vm):
    saved, o_all, lse_all, qm = res
    t = dcat.shape[0]
    (do_all, dlse_all), _ = _rowwise_bwd(_fn_combine, [o_all, lse_all], [], [_Cols(dcat, TOK_W, 0)], [BF16, F32],
                                         ROW_TB, "combine_bwd")
    dqs, dks, dvs, dbms = [], [], [], []
    for gi, (_, dil) in enumerate(DIL_GROUPS):
        l_sub = t // dil
        bq = min(BAND_BQ, l_sub)
        q, k, v, bm = saved[gi]
        do = _sub_in(do_all, gi, dil, 0, do_all.dtype, f"sub_do{gi}")
        dl = _sub_in(dlse_all, gi, dil, 0, F32, f"sub_dlse{gi}")
        dq, dk, dv, dbm = _band_bwd(q, k, v, bm, do, dl, dil, l_sub, bq, f"band_bwd{gi}")
        dqs.append(_sub_out(dq, dil, 0, BF16, f"sub_dq{gi}"))
        dks.append(_sub_out(dk, dil, BAND_HALF, BF16, f"sub_dk{gi}"))
        dvs.append(_sub_out(dv, dil, BAND_HALF, BF16, f"sub_dv{gi}"))
        dbms.append(dbm)
    dqm, dkm, dvm = _mem_bwd(qm, km, vm, _heads_major(dcat[:, TOK_W:]), min(MEM_TB, t), "mem_bwd0")
    dp = jnp.concatenate(dqs + dks + dvs + [_heads_minor(dqm).astype(BF16)], axis=1)
    return dp, _relbias_grad(dbms, "relbias_grad"), dkm, dvm


def _dn_mixer_fwd(p, conv_w, a_log, dt_bias, out_norm, km, vm):
    t = p.shape[0]
    rt = min(CONV_RT, t)
    xp = p
    w8 = jnp.pad(conv_w, ((0, 8 - DN_CONV), (0, 0)))
    q = _conv_fwd(xp, w8, 0, rt, "conv_fwd_q")
    k = _conv_fwd(xp, w8, 1, rt, "conv_fwd_k")
    v = _conv_fwd(xp, w8, 2, rt, "conv_fwd_v")
    gate = p[:, 4 * TOK_W:4 * TOK_W + 4 * DN_HEADS].reshape(t, 2, 2, DN_HEADS)
    bshape = (2, DN_HEADS, t, DN_HD)
    al = jnp.broadcast_to(gate[:, :, 0, :].transpose(1, 2, 0)[..., None], bshape)
    be = jnp.broadcast_to(gate[:, :, 1, :].transpose(1, 2, 0)[..., None], bshape)
    a_rows = jnp.broadcast_to(a_log[:, :, None, None], (2, DN_HEADS, 1, DN_HD))
    dt_rows = jnp.broadcast_to(dt_bias[:, :, None, None], (2, DN_HEADS, 1, DN_HD))
    o_f, o_r, states = _dn_fwd(q, k, v, al, be, a_rows, dt_rows, "dn_fwd")
    gain = out_norm.reshape(1, DN_HD)
    (og,) = _rowwise(_fn_outnorm, [o_f, o_r, _Cols(p, TOK_W, 3)], [gain], [(TOK_W, BF16)], ROW_TB, "outnorm_fwd")
    qm = _heads_major(p[:, 4 * TOK_W + 4 * DN_HEADS:DN_IN])
    memo = _mem_fwd(qm, km, vm, min(MEM_TB, t), "mem_fwd1")
    cat = jnp.concatenate([og, _heads_minor(memo).astype(BF16)], axis=1)
    return cat, (xp, w8, q, k, v, al, be, a_rows, dt_rows, o_f, o_r, states, gain, qm)


def _dn_mixer_bwd(dcat, res, km, vm):
    xp, w8, q, k, v, al, be, a_rows, dt_rows, o_f, o_r, states, gain, qm = res
    t = dcat.shape[0]
    rt = min(CONV_RT, t)
    (do, dz), (dgain,) = _rowwise_bwd(_fn_outnorm, [o_f, o_r, _Cols(xp, TOK_W, 3)], [gain],
                                      [_Cols(dcat, TOK_W, 0)], [F32, None, BF16],
                                      ROW_TB, "outnorm_bwd")
    d_f, d_r, dgate, da, ddt = _dn_bwd(q, k, v, al, be, a_rows, dt_rows, states, do, "dn_bwd")
    dqm, dkm, dvm = _mem_bwd(qm, km, vm, _heads_major(dcat[:, TOK_W:]), min(MEM_TB, t), "mem_bwd1")
    rest = jnp.concatenate([dgate.astype(BF16), _heads_minor(dqm).astype(BF16),
                            jnp.zeros((t, DN_IN_PAD - DN_IN), BF16)], axis=1)
    dp = lax.dynamic_update_slice(lax.empty((t, DN_IN_PAD), BF16), dz, (0, 3 * TOK_W))
    dp = lax.dynamic_update_slice(dp, rest, (0, 4 * TOK_W))
    dws = []
    for kind, nm in enumerate("qkv"):
        dp, dw = _conv_bwd(xp, w8, d_f[kind], d_r[kind], dp, kind, rt, f"conv_bwd_{nm}")
        dws.append(dw)
    dconv = jnp.concatenate(dws, axis=1)[:DN_CONV]
    return dp, dconv, da[:, :, 0, 0], ddt[:, :, 0, 0], dgain.reshape(DN_HD), dkm, dvm


SWI_TB = 256


def _ffn_fwd(h, w_gu_t, w_d, li):
    gu = _matmul(h, w_gu_t, "nt", BF16, f"ffn_gu{li}")
    (a,) = _rowwise(_fn_swiglu, [gu], [], [(D_FF, BF16)], SWI_TB, f"swiglu_fwd{li}")
    return _matmul(a, w_d, "nn", F32, f"ffn_down{li}"), gu, a


def _ffn_bwd(df, h, w_gu_t, w_d, gu, a, li):
    da = _matmul(df, w_d, "nt", BF16, f"ffn_down_dx{li}")
    dwd = _matmul(a, df, "tn", BF16, f"ffn_down_dw{li}")
    (dgu,), _ = _rowwise_bwd(_fn_swiglu, [gu], [], [da], [BF16], SWI_TB, f"swiglu_bwd{li}")
    dh = _matmul(dgu, w_gu_t, "nn", F32, f"ffn_gu_dx{li}")
    dwgu_t = _matmul(dgu, h, "tn", BF16, f"ffn_gu_dw{li}")
    return dh, dwgu_t, dwd


def _fn_first(x, g):
    return x, _rms(x, g)


def _me_xyc():
    return lax.axis_index("x"), lax.axis_index("y"), lax.axis_index("c")


def _flip(coords, k):
    x, y, c = coords
    return (1 - x if k & 4 else x, 1 - y if k & 2 else y, 1 - c if k & 1 else c)


def _index(coords):
    x, y, c = coords
    return 4 * x + 2 * y + c


def _window(ref, axis, size, d):
    idx = [slice(None)] * len(ref.shape)
    idx[axis] = pl.ds(pl.multiple_of(d * size, size), size)
    return ref.at[tuple(idx)]


def _comm_call(body, n, ins, out_shapes, name):
    hbm = pl.BlockSpec(memory_space=pl.ANY)
    return pl.pallas_call(
        body, name=name, in_specs=[hbm] * n, out_specs=[hbm] * n, out_shape=out_shapes,
        scratch_shapes=[pltpu.SemaphoreType.DMA((N_DEV - 1, n)), pltpu.SemaphoreType.DMA((N_DEV - 1, n)),
                        pltpu.SemaphoreType.DMA((n,))],
    )(*ins)


def _run_exchange(n, local, remote, send_sems, recv_sems):
    me = _me_xyc()
    locs = [local(p) for p in range(n)]
    for cp in locs:
        cp.start()
    sends = [remote(k, p, me, _flip(me, k)) for k in range(1, N_DEV) for p in range(n)]
    for cp in sends:
        cp.start()
    for k in range(1, N_DEV):
        for p in range(n):
            remote(k, p, _flip(me, k), me).wait_recv()
    for cp in sends:
        cp.wait_send()
    for cp in locs:
        cp.wait()


def _all_gather(shards, axes, name):
    n = len(shards)
    sizes = [s.shape[a] for s, a in zip(shards, axes)]

    def body(*refs):
        ins, outs = refs[:n], refs[n:2 * n]
        send_sems, recv_sems, loc_sems = refs[2 * n:]
        me = _me_xyc()

        def local(p):
            return pltpu.make_async_copy(ins[p], _window(outs[p], axes[p], sizes[p], _index(me)), loc_sems.at[p])

        def remote(k, p, owner, to):
            return pltpu.make_async_remote_copy(
                src_ref=ins[p], dst_ref=_window(outs[p], axes[p], sizes[p], _index(owner)),
                send_sem=send_sems.at[k - 1, p], recv_sem=recv_sems.at[k - 1, p], device_id=to, device_id_type=MESH)

        _run_exchange(n, local, remote, send_sems, recv_sems)

    def full(s, a):
        return s.shape[:a] + (N_DEV * s.shape[a],) + s.shape[a + 1:]

    return _comm_call(body, n, shards, [jax.ShapeDtypeStruct(full(s, a), s.dtype) for s, a in zip(shards, axes)], name)


def _exchange(fulls, axes, name):
    n = len(fulls)
    sizes = [None if a is None else f.shape[a] // N_DEV for f, a in zip(fulls, axes)]

    def part_shape(f, a):
        return f.shape if a is None else f.shape[:a] + (f.shape[a] // N_DEV,) + f.shape[a + 1:]

    def body(*refs):
        ins, outs = refs[:n], refs[n:2 * n]
        send_sems, recv_sems, loc_sems = refs[2 * n:]
        me = _me_xyc()

        def src(p, to):
            return ins[p] if axes[p] is None else _window(ins[p], axes[p], sizes[p], _index(to))

        def local(p):
            return pltpu.make_async_copy(src(p, me), outs[p].at[_index(me)], loc_sems.at[p])

        def remote(k, p, sender, to):
            return pltpu.make_async_remote_copy(
                src_ref=src(p, to), dst_ref=outs[p].at[_index(sender)],
                send_sem=send_sems.at[k - 1, p], recv_sem=recv_sems.at[k - 1, p], device_id=to, device_id_type=MESH)

        _run_exchange(n, local, remote, send_sems, recv_sems)

    return _comm_call(body, n, fulls,
                      [jax.ShapeDtypeStruct((N_DEV,) + part_shape(f, a), f.dtype) for f, a in zip(fulls, axes)], name)


_HBM = pl.BlockSpec(memory_space=pltpu.HBM)
_SEM = pl.BlockSpec(memory_space=pltpu.SEMAPHORE)
_EFFECT = pltpu.SideEffectType.DATAFLOW_SIDE_EFFECTING


def _in_hbm(a):
    return pltpu.with_memory_space_constraint(a, pltpu.HBM)


def _split_start(srcs, lands, after, descr, name):
    n = len(srcs)

    def body(*refs):
        ins, lnd = refs[:n], refs[n:2 * n]
        send_sems, recv_sems = refs[2 * n + 1], refs[2 * n + 2]
        token = refs[-1]
        me = _me_xyc()
        for k in range(1, N_DEV):
            for p in range(n):
                descr(k, p, ins, lnd, send_sems, recv_sems, me, _flip(me, k)).start()
        token[...] = jnp.zeros_like(token)

    sems = pltpu.SemaphoreType.DMA(((N_DEV - 1) * n,))
    res = pl.pallas_call(
        body, name=name,
        out_shape=(sems, sems, *[pltpu.HBM(a.shape, a.dtype) for a in (*srcs, *lands)],
                   jax.ShapeDtypeStruct((8, 128), F32)),
        in_specs=[_HBM] * (2 * n) + [pl.BlockSpec(memory_space=pl.ANY)],
        out_specs=(_SEM, _SEM, *[_HBM] * (2 * n), pl.BlockSpec(memory_space=pltpu.VMEM)),
        input_output_aliases={i: 2 + i for i in range(2 * n)},
        compiler_params=pltpu.CompilerParams(has_side_effects=_EFFECT),
    )(*[_in_hbm(a) for a in (*srcs, *lands)], after)
    return res[0], res[1], res[2:2 + n], res[2 + n:2 + 2 * n], res[-1]


def _split_wait(send_sems, recv_sems, srcs, lands, after, descr, name):
    n = len(srcs)

    def body(*refs):
        ins, lnd = refs[:n], refs[n:2 * n]
        s_sems, r_sems = refs[2 * n], refs[2 * n + 1]
        me = _me_xyc()
        for k in range(1, N_DEV):
            for p in range(n):
                peer = _flip(me, k)
                descr(k, p, ins, lnd, s_sems, r_sems, me, peer).wait_send()
                descr(k, p, ins, lnd, s_sems, r_sems, peer, me).wait_recv()

    res = pl.pallas_call(
        body, name=name,
        out_shape=tuple(pltpu.HBM(a.shape, a.dtype) for a in (*srcs, *lands)),
        in_specs=[_HBM] * (2 * n) + [_SEM, _SEM, pl.BlockSpec(memory_space=pl.ANY)],
        out_specs=tuple([_HBM] * (2 * n)),
        input_output_aliases={i: i for i in range(2 * n)},
        compiler_params=pltpu.CompilerParams(has_side_effects=_EFFECT),
    )(*srcs, *lands, send_sems, recv_sems, after)
    return list(res[n:])


def _gather_descr(axes, sizes):
    def descr(k, p, ins, lnd, send_sems, recv_sems, sender, dest):
        return pltpu.make_async_remote_copy(
            src_ref=ins[p], dst_ref=_window(lnd[p], axes[p], sizes[p], _index(sender)),
            send_sem=send_sems.at[(k - 1) * len(axes) + p], recv_sem=recv_sems.at[(k - 1) * len(axes) + p],
            device_id=dest, device_id_type=MESH)
    return descr


def _exchange_descr(axes, sizes):
    def descr(k, p, ins, lnd, send_sems, recv_sems, sender, dest):
        return pltpu.make_async_remote_copy(
            src_ref=_window(ins[p], axes[p], sizes[p], _index(dest)), dst_ref=lnd[p].at[_index(sender)],
            send_sem=send_sems.at[(k - 1) * len(axes) + p], recv_sem=recv_sems.at[(k - 1) * len(axes) + p],
            device_id=dest, device_id_type=MESH)
    return descr


def _gather_begin(shards, axes, after, name):
    sizes = [s.shape[a] for s, a in zip(shards, axes)]
    me = _index(_me_xyc())
    lands = []
    for s, a, sz in zip(shards, axes, sizes):
        full = s.shape[:a] + (N_DEV * sz,) + s.shape[a + 1:]
        lands.append(lax.dynamic_update_slice_in_dim(lax.empty(full, s.dtype), s, me * sz, a))
    descr = _gather_descr(axes, sizes)
    send_sems, recv_sems, srcs, lands, token = _split_start(shards, lands, after, descr, name)
    return (send_sems, recv_sems, srcs, lands, descr), token


def _exchange_begin(fulls, axes, after, name):
    sizes = [f.shape[a] // N_DEV for f, a in zip(fulls, axes)]
    me = _index(_me_xyc())
    lands = []
    for f, a, sz in zip(fulls, axes, sizes):
        own = lax.dynamic_slice_in_dim(f, me * sz, sz, a)
        lands.append(lax.dynamic_update_slice_in_dim(lax.empty((N_DEV,) + own.shape, f.dtype), own[None], me, 0))
    descr = _exchange_descr(axes, sizes)
    send_sems, recv_sems, srcs, lands, token = _split_start(fulls, lands, after, descr, name)
    return (send_sems, recv_sems, srcs, lands, descr), token


def _split_end(handle, after, name):
    send_sems, recv_sems, srcs, lands, descr = handle
    return _split_wait(send_sems, recv_sems, srcs, lands, after, descr, name)


def _adam_math(g, w, m, v):
    m = ADAM_B1 * m + (1.0 - ADAM_B1) * g
    v = ADAM_B2 * v + (1.0 - ADAM_B2) * (g * g)
    m_hat = m / (1.0 - ADAM_B1 ** ADAM_STEP)
    v_hat = v / (1.0 - ADAM_B2 ** ADAM_STEP)
    delta = -ADAM_LR * (m_hat / (jnp.sqrt(v_hat) + ADAM_EPS) + ADAM_WD * w)
    return delta, m, v


def _sum_slabs(r_ref):
    g = r_ref[0].astype(F32)
    for s in range(1, N_DEV):
        g = g + r_ref[s].astype(F32)
    return g


def _adamw_reduce(recv, w, m, v, tb, name):
    r, c = w.shape

    def body(r_ref, w_ref, m_ref, v_ref, g_ref, d_ref, nm_ref, nv_ref):
        g = _sum_slabs(r_ref)
        g_ref[...] = g
        d_ref[...], nm_ref[...], nv_ref[...] = _adam_math(g, w_ref[...], m_ref[...], v_ref[...])

    blk = pl.BlockSpec((tb, c), lambda i: (i, 0))
    return pl.pallas_call(
        body, name=name, grid=(r // tb,),
        in_specs=[pl.BlockSpec((N_DEV, tb, c), lambda i: (0, i, 0)), blk, blk, blk],
        out_specs=[blk] * 4, out_shape=[jax.ShapeDtypeStruct((r, c), F32)] * 4,
        compiler_params=_cparams(("parallel",)),
    )(recv, w, m, v)


def _reduce8(recv, tb, name):
    r, c = recv.shape[1:]

    def body(r_ref, g_ref):
        g_ref[...] = _sum_slabs(r_ref)

    return pl.pallas_call(
        body, name=name, grid=(r // tb,),
        in_specs=[pl.BlockSpec((N_DEV, tb, c), lambda i: (0, i, 0))],
        out_specs=pl.BlockSpec((tb, c), lambda i: (i, 0)), out_shape=jax.ShapeDtypeStruct((r, c), F32),
        compiler_params=_cparams(("parallel",)),
    )(recv)


def _adamw(g, w, m, v, tb, name):
    r, c = w.shape

    def body(g_ref, w_ref, m_ref, v_ref, d_ref, nm_ref, nv_ref):
        d_ref[...], nm_ref[...], nv_ref[...] = _adam_math(g_ref[...], w_ref[...], m_ref[...], v_ref[...])

    blk = pl.BlockSpec((tb, c), lambda i: (i, 0))
    return pl.pallas_call(
        body, name=name, grid=(r // tb,), in_specs=[blk] * 4, out_specs=[blk] * 3,
        out_shape=[jax.ShapeDtypeStruct((r, c), F32)] * 3, compiler_params=_cparams(("parallel",)),
    )(g, w, m, v)


DN_IN_SHARD = DN_IN // N_DEV
DN_IN_SHARD_PAD = 432
CONV_SHARD = (1, DN_CONV, 288)


def _pack_small(arrs, rows):
    flat = jnp.concatenate([a.astype(F32).reshape(-1) for a in arrs])
    return jnp.pad(flat, (0, rows * PACK_C - flat.size)).reshape(rows, PACK_C)


def _unpack_small(packed, shapes):
    flat, out, off = packed.reshape(-1), [], 0
    for shp in shapes:
        n = int(np.prod(shp))
        out.append(flat[off:off + n].reshape(shp))
        off += n
    return out


def kernel(x, mem, rel_bias, att_w_in, att_w_out, dn_w_in, dn_conv, dn_a_log, dn_dt_bias, dn_out_norm, dn_w_out, mem_norm, mem_w_kv, norm_mix_pre, norm_mix_post, norm_ffn_pre, norm_ffn_post, ffn_w_gate_up, ffn_w_down, loss_target, m_rel_bias, m_att_w_in, m_att_w_out, m_dn_w_in, m_dn_conv, m_dn_a_log, m_dn_dt_bias, m_dn_out_norm, m_dn_w_out, m_mem_norm, m_mem_w_kv, m_norm_mix_pre, m_norm_mix_post, m_norm_ffn_pre, m_norm_ffn_post, m_ffn_w_gate_up, m_ffn_w_down, v_rel_bias, v_att_w_in, v_att_w_out, v_dn_w_in, v_dn_conv, v_dn_a_log, v_dn_dt_bias, v_dn_out_norm, v_dn_w_out, v_mem_norm, v_mem_w_kv, v_norm_mix_pre, v_norm_mix_post, v_norm_ffn_pre, v_norm_ffn_post, v_ffn_w_gate_up, v_ffn_w_down):
    x0, mem0, tgt = x[0], mem[0], loss_target[0]
    t = x0.shape[0]
    axes = ("x", "y", "c")

    def t_shard(w):
        return jnp.swapaxes(w, 1, 2).astype(BF16)

    dn_in_pad = ((0, 0), (0, DN_IN_SHARD_PAD - DN_IN_SHARD), (0, 0))
    (w_att_in_t,) = _all_gather([t_shard(att_w_in)], [1], "allgather_first")
    w_att_in_t = w_att_in_t[0]
    gu_t, down = t_shard(ffn_w_gate_up), ffn_w_down.astype(BF16)
    gather_o, tok_o = _gather_begin([att_w_out.astype(BF16), mem_w_kv.astype(BF16)], [1, 1], w_att_in_t,
                                    "gather_att_out_start")
    gather_a, tok_a = _gather_begin([gu_t[0:1], down[0:1]], [1, 1], tok_o, "gather_ffn0_start")
    gather_b, tok_b = _gather_begin(
        [jnp.pad(t_shard(dn_w_in), dn_in_pad), dn_w_out.astype(BF16), gu_t[1:2], down[1:2], dn_conv],
        [1, 1, 1, 1, 0], tok_a, "gather_layer1_start")

    def gain(a, i):
        return a[i].reshape(1, D)

    (h0,) = _rowwise(_fn_pre, [x0], [gain(norm_mix_pre, 0) + tok_b[0:1, 0:1]], [(D, BF16)], ROW_TB, "pre0")
    p0 = _matmul(h0, w_att_in_t, "nt", BF16, "att_in")
    late = {}

    def kv0(after):
        late["w_att_out"], late["w_kv"] = _split_end(gather_o, after, "gather_att_out_wait")
        return _mem_kv_fwd(mem0, gain(mem_norm, 0), late["w_kv"][0], 0)

    cat0, res0, (km0, vm0, memn0) = _attn_mixer_fwd(p0, rel_bias, kv0)
    w_att_out, w_kv = late["w_att_out"][0], late["w_kv"]
    y0 = _matmul(cat0, w_att_out, "nn", F32, "att_out")
    g_a = [gain(norm_mix_post, 0), gain(norm_ffn_pre, 0)]
    x1, h1 = _rowwise(_fn_res_pre, [x0, y0], g_a, [(D, F32), (D, BF16)], ROW_TB, "res_pre0")
    w_gu_t0, w_down0 = [w[0] for w in _split_end(gather_a, h1, "gather_ffn0_wait")]
    f0, gu0, a0 = _ffn_fwd(h1, w_gu_t0, w_down0, 0)
    g_b = [gain(norm_ffn_post, 0), gain(norm_mix_pre, 1)]
    x2, h2 = _rowwise(_fn_res_pre, [x1, f0], g_b, [(D, F32), (D, BF16)], ROW_TB, "res_pre1")
    km1, vm1, memn1 = _mem_kv_fwd(mem0, gain(mem_norm, 1), w_kv[1], 1)
    w_dn_in_g, w_dn_out, w_gu_t1, w_down1, conv_g = _split_end(gather_b, h2, "gather_layer1_wait")
    w_dn_in_g, w_dn_out, w_gu_t1, w_down1 = w_dn_in_g[0], w_dn_out[0], w_gu_t1[0], w_down1[0]
    conv_full = conv_g.transpose(1, 0, 2).reshape(DN_CONV, 3 * TOK_W)
    w_dn_in_t = jnp.concatenate(
        [w_dn_in_g[DN_IN_SHARD_PAD * j:DN_IN_SHARD_PAD * j + DN_IN_SHARD] for j in range(N_DEV)]
        + [jnp.zeros((DN_IN_PAD - DN_IN, D), BF16)], axis=0)
    p1 = _matmul(h2, w_dn_in_t, "nt", F32, "dn_in")
    cat1, res1 = _dn_mixer_fwd(p1, conv_full, dn_a_log[0], dn_dt_bias[0], dn_out_norm[0], km1, vm1)
    y1 = _matmul(cat1, w_dn_out, "nn", F32, "dn_out")
    g_c = [gain(norm_mix_post, 1), gain(norm_ffn_pre, 1)]
    x3, h3 = _rowwise(_fn_res_pre, [x2, y1], g_c, [(D, F32), (D, BF16)], ROW_TB, "res_pre2")
    f1, gu1, a1 = _ffn_fwd(h3, w_gu_t1, w_down1, 1)
    g_d = [gain(norm_ffn_post, 1)]
    (x4,) = _rowwise(_fn_res, [x3, f1], g_d, [(D, F32)], ROW_TB, "res3")
    dx4, lrow = _loss_kernel(x4, tgt, ROW_TB, "loss")
    loss = lax.psum(lrow[0, 0] * (0.5 / D), axes)

    (df1,), (dg_fpost1,) = _rowwise_bwd(_fn_res, [x3, f1], g_d, [dx4], [None, BF16], ROW_TB, "res3_bwd")
    dh3, dwgu1, dwd1 = _ffn_bwd(df1, h3, w_gu_t1, w_down1, gu1, a1, 1)
    (dx2, dy1), (dg_mpost1, dg_fpre1) = _rowwise_bwd(_fn_res_pre, [x2, y1], g_c, [dx4, dh3], [F32, BF16],
                                                     ROW_TB, "res_pre2_bwd")
    dcat1 = _matmul(dy1, w_dn_out, "nt", F32, "dn_out_dx")
    dw_dn_out = _matmul(cat1, dy1, "tn", BF16, "dn_out_dw")
    dp1, dconv, da_log, ddt_bias, dout_norm, dkm1, dvm1 = _dn_mixer_bwd(dcat1, res1, km1, vm1)
    dwkv1, dg_mem1 = _mem_kv_bwd(mem0, gain(mem_norm, 1), w_kv[1], memn1, dkm1, dvm1, 1)
    dh2 = _matmul(dp1, w_dn_in_t, "nn", F32, "dn_in_dx")
    dw_dn_in_t = _matmul(dp1, h2, "tn", BF16, "dn_in_dw")
    dn_in_parts = [jnp.pad(dw_dn_in_t[DN_IN_SHARD * j:DN_IN_SHARD * (j + 1)],
                           ((0, DN_IN_SHARD_PAD - DN_IN_SHARD), (0, 0))) for j in range(N_DEV)]
    xch_b, tok = _exchange_begin(
        [jnp.concatenate(dn_in_parts, axis=0)[None], dw_dn_out[None], dwkv1[None], dwgu1[None], dwd1[None]],
        [1, 1, 1, 1, 1], dh2, "exchange_layer1_start")
    (dx1, df0), (dg_fpost0, dg_mpre1) = _rowwise_bwd(_fn_res_pre, [x1, f0], [g + tok[0:1, 0:1] for g in g_b],
                                                     [dx2, dh2], [F32, BF16], ROW_TB, "res_pre1_bwd")
    dh1, dwgu0, dwd0 = _ffn_bwd(df0, h1, w_gu_t0, w_down0, gu0, a0, 0)
    xch_a, tok = _exchange_begin([dwgu0[None], dwd0[None]], [1, 1], dh1, "exchange_ffn0_start")
    (dx0, dy0), (dg_mpost0, dg_fpre0) = _rowwise_bwd(_fn_res_pre, [x0, y0], [g + tok[0:1, 0:1] for g in g_a],
                                                     [dx1, dh1], [F32, BF16], ROW_TB, "res_pre0_bwd")
    dcat0 = _matmul(dy0, w_att_out, "nt", F32, "att_out_dx")
    dw_att_out = _matmul(cat0, dy0, "tn", BF16, "att_out_dw")
    dp0, drel, dkm0, dvm0 = _attn_mixer_bwd(dcat0, res0, km0, vm0)
    dwkv0, dg_mem0 = _mem_kv_bwd(mem0, gain(mem_norm, 0), w_kv[0], memn0, dkm0, dvm0, 0)
    xch_o, tok = _exchange_begin([dw_att_out[None], dwkv0[None]], [1, 1], dp0, "exchange_att_out_start")
    dw_att_in_t = _matmul(dp0, h0, "tn", BF16, "att_in_dw")
    xch_i, tok_i = _exchange_begin([dw_att_in_t[None]], [1], tok, "exchange_att_in_start")
    dh0 = _matmul(dp0, w_att_in_t, "nn", F32, "att_in_dx")
    (grad_x,), (dg_mpre0,) = _rowwise_bwd(_fn_first, [x0], [gain(norm_mix_pre, 0) + tok_i[0:1, 0:1]], [dx0, dh0],
                                          [F32], ROW_TB, "pre0_bwd")

    small_grads = [drel, da_log, ddt_bias, dout_norm, jnp.concatenate([dg_mem0, dg_mem1]),
                   jnp.concatenate([dg_mpre0, dg_mpre1]), jnp.concatenate([dg_mpost0, dg_mpost1]),
                   jnp.concatenate([dg_fpre0, dg_fpre1]), jnp.concatenate([dg_fpost0, dg_fpost1]), dconv]
    (r_small,) = _exchange([_pack_small(small_grads, SMALL_ROWS)], [None], "exchange_last")
    (r_att_in,) = _split_end(xch_i, r_small, "exchange_att_in_wait")
    r_att_out, r_kv0 = _split_end(xch_o, r_small, "exchange_att_out_wait")
    r_gu0, r_down0 = _split_end(xch_a, r_small, "exchange_ffn0_wait")
    r_dn_in, r_dn_out, r_kv1, r_gu1, r_down1 = _split_end(xch_b, r_small, "exchange_layer1_wait")

    def rows(a):
        return a.reshape((-1,) + a.shape[-1:])

    def row_sharded(recv, w, m, v, tb, name):
        outs = _adamw_reduce(recv.reshape((N_DEV, -1) + recv.shape[-1:]), rows(w), rows(m), rows(v), tb, name)
        return [o.reshape(w.shape) for o in outs]

    def col_sharded(recv, w, m, v, tb, name):
        g_t = _reduce8(recv.reshape((N_DEV, -1) + recv.shape[-1:]), tb, name + "_sum")
        g = jnp.swapaxes(g_t.reshape(recv.shape[1:])[:, :w.shape[2]], 1, 2)
        outs = _adamw(rows(g), rows(w), rows(m), rows(v), 256, name)
        return [g] + [o.reshape(w.shape) for o in outs]

    def per_layer(fn, recvs, w, m, v, tb, name):
        outs = [fn(r, w[l:l + 1], m[l:l + 1], v[l:l + 1], tb, f"{name}{l}") for l, r in enumerate(recvs)]
        return [jnp.concatenate(pair, axis=0) for pair in zip(*outs)]

    big = [col_sharded(r_att_in, att_w_in, m_att_w_in, v_att_w_in, 320, "adamw_att_in"),
           row_sharded(r_att_out, att_w_out, m_att_w_out, v_att_w_out, 128, "adamw_att_out"),
           col_sharded(r_dn_in, dn_w_in, m_dn_w_in, v_dn_w_in, 432, "adamw_dn_in"),
           row_sharded(r_dn_out, dn_w_out, m_dn_w_out, v_dn_w_out, 128, "adamw_dn_out"),
           per_layer(row_sharded, [r_kv0, r_kv1], mem_w_kv, m_mem_w_kv, v_mem_w_kv, 128, "adamw_mem_kv"),
           per_layer(col_sharded, [r_gu0, r_gu1], ffn_w_gate_up, m_ffn_w_gate_up, v_ffn_w_gate_up, 176,
                     "adamw_ffn_gu"),
           per_layer(row_sharded, [r_down0, r_down1], ffn_w_down, m_ffn_w_down, v_ffn_w_down, 176,
                     "adamw_ffn_down")]
    g_big, d_big, nm_big, nv_big = [[b[i] for b in big] for i in range(4)]

    g_small = _reduce8(r_small, SMALL_ROWS, "reduce_small")
    rep_shapes = [(32, 12), (1, 2, 6), (1, 2, 6), (1, 128), (2, D), (2, D), (2, D), (2, D), (2, D)]
    *g_rep, g_conv_full = _unpack_small(g_small, rep_shapes + [(DN_CONV, 3 * TOK_W)])
    me = _index(_me_xyc())
    g_conv = lax.dynamic_slice(g_conv_full, (0, me * 288), (DN_CONV, 288)).reshape(CONV_SHARD)
    small_shapes = rep_shapes + [CONV_SHARD]
    small_w = [rel_bias, dn_a_log, dn_dt_bias, dn_out_norm, mem_norm, norm_mix_pre, norm_mix_post,
               norm_ffn_pre, norm_ffn_post, dn_conv]
    small_m = [m_rel_bias, m_dn_a_log, m_dn_dt_bias, m_dn_out_norm, m_mem_norm, m_norm_mix_pre, m_norm_mix_post,
               m_norm_ffn_pre, m_norm_ffn_post, m_dn_conv]
    small_v = [v_rel_bias, v_dn_a_log, v_dn_dt_bias, v_dn_out_norm, v_mem_norm, v_norm_mix_pre, v_norm_mix_post,
               v_norm_ffn_pre, v_norm_ffn_post, v_dn_conv]
    g_small_list = g_rep + [g_conv]
    outs_small = _adamw(_pack_small(g_small_list, 24), _pack_small(small_w, 24), _pack_small(small_m, 24),
                        _pack_small(small_v, 24), 24, "adamw_small")
    d_small, nm_small, nv_small = [_unpack_small(o, small_shapes) for o in outs_small]

    def ordered(small, big):
        return [small[0], big[0], big[1], big[2], small[9], small[1], small[2], small[3], big[3], small[4],
                big[4], small[5], small[6], small[7], small[8], big[5], big[6]]

    g_small_out = [g.reshape(s) for g, s in zip(g_small_list, small_shapes)]
    return (loss, grad_x[None], *ordered(g_small_out, g_big), *ordered(d_small, d_big),
            *ordered(nm_small, nm_big), *ordered(nv_small, nv_big))
```

```python
import functools
import math
from typing import NamedTuple

import numpy as np
import jax
import jax.numpy as jnp
from jax import lax
from jax.experimental import pallas as pl
from jax.experimental.pallas import tpu as pltpu

F32 = jnp.float32
BF16 = jnp.bfloat16
HI = lax.Precision.HIGHEST
MESH = pl.DeviceIdType.MESH

N_DEV = 8
D = 1024
EPS = 1e-6
NEG = -1e30
TOK_W = 768
MEM_W = 256
ATT_HD = 64
DIL_GROUPS = ((128, 1), (512, 4), (2048, 16))
BAND_HALF = 64
REL_BUCKETS = 32
REL_MAX_DIST = 1024
DN_HD = 128
DN_HEADS = 6
DN_CONV = 5
DN_CHUNK = 64
MEM_HEADS = 4
D_FF = 2816
DN_IN = 3352
DN_IN_PAD = 3456

ADAM_LR, ADAM_B1, ADAM_B2, ADAM_EPS, ADAM_WD, ADAM_STEP = 0.001, 0.9, 0.999, 1e-08, 0.01, 10

PACK_C = 512
SMALL_ROWS = 48
VMEM_LIMIT = 48 * 1024 * 1024


def _cparams(sem=None):
    kw = dict(vmem_limit_bytes=VMEM_LIMIT)
    if sem is not None:
        kw["dimension_semantics"] = sem
    return pltpu.CompilerParams(**kw)


def _tile(n, cap):
    if n <= cap:
        return n
    best = None
    for t in range(128, cap + 1, 128):
        if n % t == 0:
            best = t
    assert best is not None, (n, cap)
    return best


def _matmul(a, b, mode, out_dtype, name, tm=1024, tn=1408, tk=None):
    if tk is None:
        tk = 4096 if mode == "tn" else 2816
    if mode == "tn":
        tm = min(tm, 512)
    if mode == "nn":
        (m, kc), (_, n) = a.shape, b.shape
        dims = (((1,), (0,)), ((), ()))
    elif mode == "nt":
        (m, kc), (n, _) = a.shape, b.shape
        dims = (((1,), (1,)), ((), ()))
    else:
        (kc, m), (_, n) = a.shape, b.shape
        dims = (((0,), (0,)), ((), ()))
    tm = m if m <= tm else _tile(m, tm)
    tn = _tile(n, tn)
    tk = _tile(kc, tk)
    nk = kc // tk

    def body(a_ref, b_ref, o_ref, acc_ref):
        k = pl.program_id(2)
        part = lax.dot_general(a_ref[...], b_ref[...], dims, preferred_element_type=F32)

        @pl.when(k == 0)
        def _():
            acc_ref[...] = part

        @pl.when(k > 0)
        def _():
            acc_ref[...] += part

        @pl.when(k == nk - 1)
        def _():
            o_ref[...] = acc_ref[...].astype(o_ref.dtype)

    if mode == "nn":
        a_spec = pl.BlockSpec((tm, tk), lambda i, j, k: (i, k))
        b_spec = pl.BlockSpec((tk, tn), lambda i, j, k: (k, j))
    elif mode == "nt":
        a_spec = pl.BlockSpec((tm, tk), lambda i, j, k: (i, k))
        b_spec = pl.BlockSpec((tn, tk), lambda i, j, k: (j, k))
    else:
        a_spec = pl.BlockSpec((tk, tm), lambda i, j, k: (k, i))
        b_spec = pl.BlockSpec((tk, tn), lambda i, j, k: (k, j))
    return pl.pallas_call(
        body, name=name, grid=(m // tm, n // tn, nk),
        in_specs=[a_spec, b_spec],
        out_specs=pl.BlockSpec((tm, tn), lambda i, j, k: (i, j)),
        out_shape=jax.ShapeDtypeStruct((m, n), out_dtype),
        scratch_shapes=[pltpu.VMEM((tm, tn), F32)],
        compiler_params=_cparams(("parallel", "parallel", "arbitrary")),
    )(a, b)


class _Cols(NamedTuple):
    arr: jax.Array
    width: int
    block: int

    @property
    def shape(self):
        return (self.arr.shape[0], self.width)


def _row_spec(r, tb):
    if isinstance(r, _Cols):
        return pl.BlockSpec((tb, r.width), lambda i, b=r.block: (i, b))
    return pl.BlockSpec((tb, r.shape[1]), lambda i: (i, 0))


def _row_arr(r):
    return r.arr if isinstance(r, _Cols) else r


def _rowwise(fn, rows, params, outs, tb, name):
    t = rows[0].shape[0]
    nr, npar = len(rows), len(params)

    def body(*refs):
        ins = [r[...].astype(F32) for r in refs[:nr + npar]]
        res = fn(*ins)
        for o_ref, r in zip(refs[nr + npar:], res):
            o_ref[...] = r.astype(o_ref.dtype)

    return pl.pallas_call(
        body, name=name, grid=(t // tb,),
        in_specs=[_row_spec(r, tb) for r in rows] + [pl.BlockSpec(p.shape, lambda i: (0, 0)) for p in params],
        out_specs=[pl.BlockSpec((tb, c), lambda i: (i, 0)) for c, _ in outs],
        out_shape=[jax.ShapeDtypeStruct((t, c), dt) for c, dt in outs],
        compiler_params=_cparams(("parallel",)),
    )(*[_row_arr(r) for r in rows], *params)


def _rowwise_bwd(fn, rows, params, cots, row_grad, tb, name):
    t = rows[0].shape[0]
    nr, npar, nc = len(rows), len(params), len(cots)
    want = [i for i, g in enumerate(row_grad) if g is not None]

    def body(*refs):
        ins = [r[...].astype(F32) for r in refs[:nr + npar]]
        cts = tuple(r[...].astype(F32) for r in refs[nr + npar:nr + npar + nc])
        outs = refs[nr + npar + nc:]
        _, vjp = jax.vjp(fn, *ins)
        grads = vjp(cts)
        for o_ref, i in zip(outs[:len(want)], want):
            o_ref[...] = grads[i].astype(o_ref.dtype)
        first = pl.program_id(0) == 0
        for o_ref, g in zip(outs[len(want):], grads[nr:]):
            @pl.when(first)
            def _(o_ref=o_ref, g=g):
                o_ref[...] = g

            @pl.when(jnp.logical_not(first))
            def _(o_ref=o_ref, g=g):
                o_ref[...] += g

    res = pl.pallas_call(
        body, name=name, grid=(t // tb,),
        in_specs=[_row_spec(r, tb) for r in rows] + [pl.BlockSpec(p.shape, lambda i: (0, 0)) for p in params]
        + [_row_spec(c, tb) for c in cots],
        out_specs=[pl.BlockSpec((tb, rows[i].shape[1]), lambda i_: (i_, 0)) for i in want]
        + [pl.BlockSpec(p.shape, lambda i: (0, 0)) for p in params],
        out_shape=[jax.ShapeDtypeStruct(tuple(rows[i].shape), row_grad[i]) for i in want]
        + [jax.ShapeDtypeStruct(p.shape, F32) for p in params],
        compiler_params=_cparams(("arbitrary",)),
    )(*[_row_arr(r) for r in rows], *params, *[_row_arr(c) for c in cots])
    return list(res[:len(want)]), list(res[len(want):])


def _rms(x, g):
    return x * lax.rsqrt(jnp.mean(x * x, axis=-1, keepdims=True) + EPS) * g


def _fn_pre(x, g):
    return (_rms(x, g),)


def _fn_res_pre(x, y, g_post, g_pre):
    x1 = x + _rms(y, g_post)
    return x1, _rms(x1, g_pre)


def _fn_res(x, y, g_post):
    return (x + _rms(y, g_post),)


def _sigmoid(x):
    return 1.0 / (1.0 + jnp.exp(-x))


def _silu(x):
    return x * _sigmoid(x)


def _fn_swiglu(gu):
    return (_silu(gu[:, :D_FF]) * gu[:, D_FF:],)


def _fn_combine(o, lse):
    ls = [lse[:, 256 * g:256 * (g + 1)] for g in range(3)]
    mx = lax.stop_gradient(jnp.maximum(jnp.maximum(ls[0], ls[1]), ls[2]))
    es = [jnp.exp(l - mx) for l in ls]
    inv = 1.0 / (es[0] + es[1] + es[2])
    return (jnp.concatenate([o[:, 256 * g:256 * (g + 1)] * (es[g] * inv) for g in range(3)], axis=1),)


def _fn_outnorm(o_f, o_r, z, gain):
    res = []
    for h in range(DN_HEADS):
        sl = slice(DN_HD * h, DN_HD * (h + 1))
        o = o_f[:, sl] + o_r[:, sl]
        res.append(o * lax.rsqrt(jnp.mean(o * o, axis=-1, keepdims=True) + EPS) * gain * _silu(z[:, sl]))
    return (jnp.concatenate(res, axis=1),)


def _loss_kernel(x, tgt, tb, name):
    t, d = x.shape

    def body(x_ref, t_ref, dx_ref, l_ref, acc_ref):
        i = pl.program_id(0)
        e = x_ref[...] - t_ref[...]
        dx_ref[...] = e * (1.0 / d)
        part = jnp.sum(e * e, axis=0, keepdims=True)

        @pl.when(i == 0)
        def _():
            acc_ref[...] = part

        @pl.when(i > 0)
        def _():
            acc_ref[...] += part

        @pl.when(i == t // tb - 1)
        def _():
            l_ref[...] = jnp.broadcast_to(jnp.sum(acc_ref[...], axis=-1, keepdims=True), (1, 128))

    return pl.pallas_call(
        body, name=name, grid=(t // tb,),
        in_specs=[pl.BlockSpec((tb, d), lambda i: (i, 0))] * 2,
        out_specs=[pl.BlockSpec((tb, d), lambda i: (i, 0)), pl.BlockSpec((1, 128), lambda i: (0, 0))],
        out_shape=[jax.ShapeDtypeStruct((t, d), F32), jax.ShapeDtypeStruct((1, 128), F32)],
        scratch_shapes=[pltpu.VMEM((1, d), F32)],
        compiler_params=_cparams(("arbitrary",)),
    )(x, tgt)


def _band_fn(l_sub, bq, i, q, kw, vw, bm):
    w = bq + 2 * BAND_HALF
    s = lax.dot_general((q * (ATT_HD ** -0.5)).astype(BF16), kw.astype(BF16), (((2,), (2,)), ((0,), (0,))),
                        preferred_element_type=F32) + bm
    kpos = i * bq - BAND_HALF + lax.broadcasted_iota(jnp.int32, (4, bq, w), 2)
    s = jnp.where((kpos >= 0) & (kpos < l_sub), s, NEG)
    m = lax.stop_gradient(jnp.max(s, axis=-1, keepdims=True))
    p = jnp.exp(s - m)
    den = jnp.sum(p, axis=-1, keepdims=True)
    o = lax.dot_general(p.astype(BF16), vw.astype(BF16), (((2,), (1,)), ((0,), (0,))),
                        preferred_element_type=F32) / den
    return o, jnp.broadcast_to(m + jnp.log(den), o.shape)


def _band_specs(l_sub, bq):
    w = bq + 2 * BAND_HALF
    qs = pl.BlockSpec((None, 4, bq, ATT_HD), lambda r, i: (r, 0, i, 0))
    ks = pl.BlockSpec((None, 4, l_sub + 2 * BAND_HALF, ATT_HD), lambda r, i: (r, 0, 0, 0))
    bs = pl.BlockSpec((4, bq, w), lambda r, i: (0, 0, 0))
    return qs, ks, bs


def _band_fwd(q, k, v, bm, dil, l_sub, bq, name):
    w = bq + 2 * BAND_HALF
    qs, ks, bs = _band_specs(l_sub, bq)

    def body(q_ref, k_ref, v_ref, bm_ref, o_ref, l_ref):
        i = pl.program_id(1)
        st = pl.multiple_of(i * bq, bq)
        o, lse = _band_fn(l_sub, bq, i, q_ref[...].astype(F32), k_ref[:, pl.ds(st, w), :].astype(F32),
                          v_ref[:, pl.ds(st, w), :].astype(F32), bm_ref[...])
        o_ref[...] = o
        l_ref[...] = lse

    return pl.pallas_call(
        body, name=name, grid=(dil, l_sub // bq),
        in_specs=[qs, ks, ks, bs], out_specs=[qs, qs],
        out_shape=[jax.ShapeDtypeStruct(q.shape, F32)] * 2,
        compiler_params=_cparams(("parallel", "arbitrary")),
    )(q, k, v, bm)


def _band_bwd(q, k, v, bm, do, dlse, dil, l_sub, bq, name):
    w = bq + 2 * BAND_HALF
    qs, ks, bs = _band_specs(l_sub, bq)

    def body(q_ref, k_ref, v_ref, bm_ref, do_ref, dl_ref, dq_ref, dk_ref, dv_ref, dbm_ref):
        r, i = pl.program_id(0), pl.program_id(1)
        st = pl.multiple_of(i * bq, bq)
        _, vjp = jax.vjp(functools.partial(_band_fn, l_sub, bq, i),
                         q_ref[...].astype(F32), k_ref[:, pl.ds(st, w), :].astype(F32),
                         v_ref[:, pl.ds(st, w), :].astype(F32), bm_ref[...])
        dq, dkw, dvw, dbm = vjp((do_ref[...].astype(F32), dl_ref[...]))
        dq_ref[...] = dq.astype(dq_ref.dtype)

        @pl.when(i == 0)
        def _():
            dk_ref[...] = jnp.zeros_like(dk_ref)
            dv_ref[...] = jnp.zeros_like(dv_ref)

        dk_ref[:, pl.ds(st, w), :] += dkw
        dv_ref[:, pl.ds(st, w), :] += dvw

        @pl.when((i == 0) & (r == 0))
        def _():
            dbm_ref[...] = dbm

        @pl.when((i > 0) | (r > 0))
        def _():
            dbm_ref[...] += dbm

    return pl.pallas_call(
        body, name=name, grid=(dil, l_sub // bq),
        in_specs=[qs, ks, ks, bs, qs, qs], out_specs=[qs, ks, ks, bs],
        out_shape=[jax.ShapeDtypeStruct(q.shape, BF16), jax.ShapeDtypeStruct(k.shape, F32),
                   jax.ShapeDtypeStruct(k.shape, F32), jax.ShapeDtypeStruct(bm.shape, F32)],
        compiler_params=_cparams(("arbitrary", "arbitrary")),
    )(q, k, v, bm, do, dlse)


def _t5_bucket(rel):
    half = REL_BUCKETS // 2
    max_exact = half // 2
    n = np.abs(rel)
    large = max_exact + (np.log(np.maximum(n, 1) / max_exact) / math.log(REL_MAX_DIST / max_exact)
                         * (half - max_exact)).astype(np.int64)
    large = np.minimum(large, half - 1)
    return ((rel > 0) * half + np.where(n < max_exact, n, large)).astype(np.int32)


def _bucket_onehot(dil):
    idx = _t5_bucket(np.arange(-BAND_HALF, BAND_HALF + 1) * dil)
    oh = np.zeros((2 * BAND_HALF + 1, REL_BUCKETS), np.float32)
    oh[np.arange(2 * BAND_HALF + 1), idx] = 1.0
    return oh


def _band_bias(rel_bias, gi, dil, bq):
    w = bq + 2 * BAND_HALF
    nb = 2 * BAND_HALF + 1
    bias = jnp.dot(jnp.asarray(_bucket_onehot(dil)), rel_bias[:, 4 * gi:4 * gi + 4], precision=HI)
    row = jnp.concatenate([bias.T, jnp.full((4, w + 1 - nb), NEG, F32)], axis=1)
    flat = jnp.tile(row, (1, bq))[:, :bq * w]
    return flat.reshape(4, bq, w)


def _relbias_grad(dbms, name):
    nb = 2 * BAND_HALF + 1
    bq = max(d.shape[1] for d in dbms)
    skew = []
    for dbm in dbms:
        bqg, w = dbm.shape[1], dbm.shape[2]
        flat = jnp.pad(dbm.reshape(4, bqg * w), ((0, 0), (0, bqg)))
        skew.append(jnp.pad(flat.reshape(4, bqg, w + 1)[:, :, :nb], ((0, 0), (0, bq - bqg), (0, 256 - nb))))
    sk = jnp.concatenate(skew, axis=0)
    oh = np.zeros((3, 256, 128), np.float32)
    for gi, (_, dil) in enumerate(DIL_GROUPS):
        oh[gi, :2 * BAND_HALF + 1, :REL_BUCKETS] = _bucket_onehot(dil)

    def body(s_ref, oh_ref, o_ref):
        col = jnp.sum(s_ref[...], axis=0, keepdims=True)
        o_ref[...] = jnp.dot(jnp.broadcast_to(col, (8, 256)), oh_ref[...], precision=HI, preferred_element_type=F32)

    out = pl.pallas_call(
        body, name=name, grid=(12,),
        in_specs=[pl.BlockSpec((None, bq, 256), lambda n: (n, 0, 0)),
                  pl.BlockSpec((None, 256, 128), lambda n: (n // 4, 0, 0))],
        out_specs=pl.BlockSpec((None, 8, 128), lambda n: (n, 0, 0)),
        out_shape=jax.ShapeDtypeStruct((12, 8, 128), F32),
        compiler_params=_cparams(("parallel",)),
    )(sk, jnp.asarray(oh))
    return out[:, 0, :REL_BUCKETS].T


def _mem_fn(q, k, v):
    s = lax.dot_general((q * (ATT_HD ** -0.5)).astype(BF16), k.astype(BF16), (((2,), (2,)), ((0,), (0,))),
                        preferred_element_type=F32)
    m = lax.stop_gradient(jnp.max(s, axis=-1, keepdims=True))
    p = jnp.exp(s - m)
    p = p / jnp.sum(p, axis=-1, keepdims=True)
    return lax.dot_general(p.astype(BF16), v.astype(BF16), (((2,), (1,)), ((0,), (0,))), preferred_element_type=F32)


def _mem_specs(tb, ml):
    qs = pl.BlockSpec((MEM_HEADS, tb, ATT_HD), lambda i: (0, i, 0))
    ks = pl.BlockSpec((MEM_HEADS, ml, ATT_HD), lambda i: (0, 0, 0))
    return qs, ks


def _mem_fwd(q, k, v, tb, name):
    qs, ks = _mem_specs(tb, k.shape[1])

    def body(q_ref, k_ref, v_ref, o_ref):
        o_ref[...] = _mem_fn(q_ref[...].astype(F32), k_ref[...], v_ref[...])

    return pl.pallas_call(
        body, name=name, grid=(q.shape[1] // tb,),
        in_specs=[qs, ks, ks], out_specs=qs, out_shape=jax.ShapeDtypeStruct(q.shape, F32),
        compiler_params=_cparams(("parallel",)),
    )(q, k, v)


def _mem_bwd(q, k, v, do, tb, name):
    qs, ks = _mem_specs(tb, k.shape[1])

    def body(q_ref, k_ref, v_ref, do_ref, dq_ref, dk_ref, dv_ref):
        i = pl.program_id(0)
        _, vjp = jax.vjp(_mem_fn, q_ref[...].astype(F32), k_ref[...], v_ref[...])
        dq, dk, dv = vjp(do_ref[...])
        dq_ref[...] = dq

        @pl.when(i == 0)
        def _():
            dk_ref[...] = dk
            dv_ref[...] = dv

        @pl.when(i > 0)
        def _():
            dk_ref[...] += dk
            dv_ref[...] += dv

    return pl.pallas_call(
        body, name=name, grid=(q.shape[1] // tb,),
        in_specs=[qs, ks, ks, qs], out_specs=[qs, ks, ks],
        out_shape=[jax.ShapeDtypeStruct(q.shape, F32), jax.ShapeDtypeStruct(k.shape, F32),
                   jax.ShapeDtypeStruct(k.shape, F32)],
        compiler_params=_cparams(("arbitrary",)),
    )(q, k, v, do)


CONV_PAD = 8


def _conv_post(kind, acc):
    s = _silu(acc)
    if kind == 2:
        return s
    scale = DN_HD ** -0.5 if kind == 0 else 1.0
    return s * lax.rsqrt(jnp.sum(s * s, axis=-1, keepdims=True) + EPS) * scale


def _conv_rows(x_ref, t, start, rt):
    lo = min(max(start, 0), t - rt)
    x = x_ref[pl.ds(lo, rt), :]
    shift = lo - start
    if shift == 0:
        return x
    x = pltpu.roll(x, shift % rt, axis=0)
    row = lax.broadcasted_iota(jnp.int32, x.shape, 0)
    return jnp.where((row >= shift) if shift > 0 else (row < rt + shift), x, 0.0)


def _conv_acc(x_ref, t, w, r0, rt):
    acc = None
    for i in range(DN_CONV):
        term = w[i:i + 1, :] * _conv_rows(x_ref, t, r0 + i - DN_CONV // 2, rt)
        acc = term if acc is None else acc + term
    return acc


def _conv_fwd(x, w8, kind, rt, name):
    t = x.shape[0]

    def body(x_ref, w_ref, o_ref):
        w = w_ref[...]
        for r in range(t // rt):
            o_ref[pl.ds(r * rt, rt), :] = _conv_post(kind, _conv_acc(x_ref, t, w, r * rt, rt))

    return pl.pallas_call(
        body, name=name, grid=(DN_HEADS,),
        in_specs=[pl.BlockSpec((t, DN_HD), lambda j: (0, 6 * kind + j)),
                  pl.BlockSpec((8, DN_HD), lambda j: (0, 6 * kind + j))],
        out_specs=pl.BlockSpec((t, DN_HD), lambda j: (0, j)),
        out_shape=jax.ShapeDtypeStruct((t, TOK_W), F32),
        compiler_params=_cparams(("parallel",)),
    )(x, w8)


def _conv_bwd(x, w8, d_f, d_r, dp, kind, rt, name):
    t = x.shape[0]

    def body(xp_ref, w_ref, df_ref, dr_ref, dp_in, dx_ref, dw_ref, dpad_ref):
        del dp_in
        w = w_ref[...]
        zero = jnp.zeros((CONV_PAD, DN_HD), F32)
        dpad_ref[pl.ds(0, CONV_PAD), :] = zero
        dpad_ref[pl.ds(CONV_PAD + t, CONV_PAD), :] = zero
        dw = [jnp.zeros((1, DN_HD), F32) for _ in range(DN_CONV)]
        for r in range(t // rt):
            rows = pl.ds(r * rt, rt)
            acc = _conv_acc(xp_ref, t, w, r * rt, rt)
            _, vjp = jax.vjp(functools.partial(_conv_post, kind), acc)
            (dacc,) = vjp(df_ref[rows, :] + dr_ref[rows, :])
            dpad_ref[pl.ds(CONV_PAD + r * rt, rt), :] = dacc
            for i in range(DN_CONV):
                xs = _conv_rows(xp_ref, t, r * rt + i - DN_CONV // 2, rt)
                dw[i] = dw[i] + jnp.sum(dacc * xs, axis=0, keepdims=True)
        dw_ref[...] = jnp.concatenate(dw + [jnp.zeros((8 - DN_CONV, DN_HD), F32)], axis=0)
        for r in range(t // rt):
            acc = None
            for i in range(DN_CONV):
                term = w[i:i + 1, :] * dpad_ref[pl.ds(CONV_PAD + r * rt - i + DN_CONV // 2, rt), :]
                acc = term if acc is None else acc + term
            dx_ref[pl.ds(r * rt, rt), :] = acc.astype(dx_ref.dtype)

    return pl.pallas_call(
        body, name=name, grid=(DN_HEADS,),
        in_specs=[pl.BlockSpec((t, DN_HD), lambda j: (0, 6 * kind + j)),
                  pl.BlockSpec((8, DN_HD), lambda j: (0, 6 * kind + j)),
                  pl.BlockSpec((t, DN_HD), lambda j: (0, j)),
                  pl.BlockSpec((t, DN_HD), lambda j: (0, j)),
                  pl.BlockSpec(memory_space=pl.ANY)],
        out_specs=[pl.BlockSpec((t, DN_HD), lambda j: (0, 6 * kind + j)),
                   pl.BlockSpec((8, DN_HD), lambda j: (0, j))],
        out_shape=[jax.ShapeDtypeStruct(dp.shape, dp.dtype), jax.ShapeDtypeStruct((8, TOK_W), F32)],
        input_output_aliases={4: 0},
        scratch_shapes=[pltpu.VMEM((t + 2 * CONV_PAD, DN_HD), F32)],
        compiler_params=_cparams(("parallel",)),
    )(x, w8, d_f, d_r, dp)


def _softplus(x):
    e = jnp.exp(-jnp.abs(x))
    return jnp.maximum(x, 0.0) + jnp.where(e < 1e-4, e - 0.5 * e * e, jnp.log(1.0 + e))


_NN = (((2,), (1,)), ((0,), (0,)))
_NT = (((2,), (2,)), ((0,), (0,)))
_TN = (((1,), (1,)), ((0,), (0,)))


def _dot(a, b, dims=_NN):
    return lax.dot_general(a.astype(BF16), b.astype(BF16), dims, preferred_element_type=F32)


def _hi_lo(x):
    hi = x.astype(BF16)
    return hi, (x - hi.astype(F32)).astype(BF16)


def _mask_dot(mask_bf16, x, dims):
    x1 = x.astype(BF16)
    r = x - x1.astype(F32)
    x2, x3 = _hi_lo(r)
    d = functools.partial(lax.dot_general, dimension_numbers=dims, preferred_element_type=F32)
    return d(mask_bf16, x1) + d(mask_bf16, x2) + d(mask_bf16, x3)


@jax.custom_vjp
def _dot_mask(mask_bf16, x):
    return _mask_dot(mask_bf16, x, _NN)


def _dot_mask_fwd(mask_bf16, x):
    return _mask_dot(mask_bf16, x, _NN), mask_bf16


def _dot_mask_bwd(mask_bf16, ct):
    return jnp.zeros_like(mask_bf16), _mask_dot(mask_bf16, ct, _TN)


_dot_mask.defvjp(_dot_mask_fwd, _dot_mask_bwd)


def _dot3_raw(a, b, dims):
    a1, a2 = _hi_lo(a)
    b1, b2 = _hi_lo(b)
    d = functools.partial(lax.dot_general, dimension_numbers=dims, preferred_element_type=F32)
    return d(a1, b1) + d(a1, b2) + d(a2, b1)


def _unit_solve_pass(lmat, rhs, masks):
    ainv = masks[6] - lmat * masks[0]
    for sh in range(1, 6):
        ainv = ainv - _dot(_dot(ainv, lmat * masks[sh]), ainv)
    return _dot3_raw(ainv, rhs, _NN), ainv


@jax.custom_vjp
def _unit_solve(lmat, rhs, masks):
    return _unit_solve_pass(lmat, rhs, masks)[0]


def _unit_solve_fwd(lmat, rhs, masks):
    sol, ainv = _unit_solve_pass(lmat, rhs, masks)
    return sol, (sol, ainv, masks)


def _unit_solve_bwd(res, ct):
    sol, ainv, masks = res
    d_rhs = _dot3_raw(ainv, ct, _TN)
    return -_dot3_raw(d_rhs, sol, _NT), d_rhs, tuple(jnp.zeros_like(m) for m in masks)


_unit_solve.defvjp(_unit_solve_fwd, _unit_solve_bwd)


def _block_masks(rev, row, col):
    c = DN_CHUNK
    prow = jnp.where(rev, c - 1 - row, row)
    pcol = jnp.where(rev, c - 1 - col, col)
    masks = []
    for sh in range(6):
        differ = (prow ^ pcol) >> sh
        miss = (differ ^ 1) + (1 - ((prow >> sh) & 1))
        masks.append(jnp.where(miss == 0, 1.0, 0.0))
    masks.append(jnp.where(row == col, 1.0, 0.0))
    return tuple(masks)


def _dn_chunk(q, k, v, al, be, alc, a_row, dt_row, a_rowc, dt_rowc, s):
    n, c = q.shape[0], DN_CHUNK
    rev = lax.broadcasted_iota(jnp.int32, (n, c, c), 0) >= n // 2
    row = lax.broadcasted_iota(jnp.int32, (n, c, c), 1)
    col = lax.broadcasted_iota(jnp.int32, (n, c, c), 2)
    ahead = jnp.where(rev, col - row, row - col)
    incl = ahead >= 0
    strict = ahead > 0
    incl_b = incl.astype(BF16)

    g = -jnp.exp(a_row) * _softplus(al + dt_row)
    beta = _sigmoid(be)
    g_c = -jnp.exp(a_rowc) * _softplus(alc + dt_rowc)
    gc = _dot_mask(incl_b, g)
    gcc = _dot_mask(incl_b, g_c)
    decay = jnp.exp(jnp.where(incl, gcc - jnp.swapaxes(gcc, 1, 2), NEG))
    kb = k * beta
    lmat = jnp.where(strict, _dot(kb, k, _NT) * decay, 0.0)
    rhs = jnp.concatenate([v * beta, kb * jnp.exp(gc)], axis=2)
    sol = _unit_solve(lmat, rhs, _block_masks(rev, row, col))
    u, w = sol[:, :, :DN_HD], sol[:, :, DN_HD:]
    intra = jnp.where(incl, _dot(q, k, _NT) * decay, 0.0)
    v_new = u - _dot(w, s)
    out = _dot(q * jnp.exp(gc), s) + _dot(intra, v_new)
    g_last = jnp.sum(g, axis=1, keepdims=True)
    s_new = s * jnp.exp(g_last) + _dot(k * jnp.exp(g_last - gc), v_new, _TN)
    return out, s_new


DN_HG = 6


def _dn_load(f_refs, r_refs, alf, bef, alr, ber, a_ref, dt_ref):
    c, hg = DN_CHUNK, DN_HG
    sls = [slice(DN_HD * h, DN_HD * (h + 1)) for h in range(hg)]
    toks = [jnp.stack([f[:, sl] for sl in sls] + [r[:, sl] for sl in sls]) for f, r in zip(f_refs, r_refs)]
    al = jnp.concatenate([alf[...], alr[...]], axis=0)
    be = jnp.concatenate([bef[...], ber[...]], axis=0)
    alc = jnp.concatenate([alf[:, :, 0:c], alr[:, :, 0:c]], axis=0)
    a = jnp.concatenate([a_ref[0], a_ref[1]], axis=0)
    dt = jnp.concatenate([dt_ref[0], dt_ref[1]], axis=0)
    ac = jnp.concatenate([a_ref[0, :, :, 0:c], a_ref[1, :, :, 0:c]], axis=0)
    dtc = jnp.concatenate([dt_ref[0, :, :, 0:c], dt_ref[1, :, :, 0:c]], axis=0)
    return toks, (al, be, alc, a, dt, ac, dtc)


def _dn_views(nc, bwd):
    c, hg = DN_CHUNK, DN_HG
    if bwd:
        f_blk = lambda s: nc - 1 - s
        r_blk = lambda s: s
        st_blk = lambda s: nc - 1 - s
    else:
        f_blk = lambda s: s
        r_blk = lambda s: nc - 1 - s
        st_blk = lambda s: s
    tok_f = pl.BlockSpec((c, hg * DN_HD), lambda g, s: (f_blk(s), g))
    tok_r = pl.BlockSpec((c, hg * DN_HD), lambda g, s: (r_blk(s), g))
    gate_f = pl.BlockSpec((None, hg, c, DN_HD), lambda g, s: (0, g, f_blk(s), 0))
    gate_r = pl.BlockSpec((None, hg, c, DN_HD), lambda g, s: (1, g, r_blk(s), 0))
    par = pl.BlockSpec((2, hg, 1, DN_HD), lambda g, s: (0, g, 0, 0))
    state = pl.BlockSpec((2, hg, None, DN_HD, DN_HD), lambda g, s: (0, g, st_blk(s), 0, 0))
    return tok_f, tok_r, gate_f, gate_r, par, state


def _dn_fwd(q, k, v, al, be, a_rows, dt_rows, name):
    t = q.shape[0]
    c, hg = DN_CHUNK, DN_HG
    nc = t // c
    tok_f, tok_r, gate_f, gate_r, par, state = _dn_views(nc, False)

    def body(qf, kf, vf, qr, kr, vr, alf, bef, alr, ber, a_ref, dt_ref, of_ref, or_ref, st_ref, s_ref):
        @pl.when(pl.program_id(1) == 0)
        def _():
            s_ref[...] = jnp.zeros_like(s_ref)

        (q_, k_, v_), gates = _dn_load((qf, kf, vf), (qr, kr, vr), alf, bef, alr, ber, a_ref, dt_ref)
        s = s_ref[...]
        st_ref[0] = s[:hg]
        st_ref[1] = s[hg:]
        out, s_new = _dn_chunk(q_, k_, v_, *gates, s)
        for h in range(hg):
            sl = slice(DN_HD * h, DN_HD * (h + 1))
            of_ref[:, sl] = out[h]
            or_ref[:, sl] = out[hg + h]
        s_ref[...] = s_new

    return pl.pallas_call(
        body, name=name, grid=(DN_HEADS // hg, nc),
        in_specs=[tok_f] * 3 + [tok_r] * 3 + [gate_f, gate_f, gate_r, gate_r, par, par],
        out_specs=[tok_f, tok_r, state],
        out_shape=[jax.ShapeDtypeStruct((t, TOK_W), F32)] * 2
        + [jax.ShapeDtypeStruct((2, DN_HEADS, nc, DN_HD, DN_HD), F32)],
        scratch_shapes=[pltpu.VMEM((2 * hg, DN_HD, DN_HD), F32)],
        compiler_params=_cparams(("parallel", "arbitrary")),
    )(q, k, v, q, k, v, al, be, al, be, a_rows, dt_rows)


def _dn_bwd(q, k, v, al, be, a_rows, dt_rows, states, do, name):
    t = q.shape[0]
    c, hg = DN_CHUNK, DN_HG
    assert hg == DN_HEADS
    nc = t // c
    tok_f, tok_r, gate_f, gate_r, par, state = _dn_views(nc, True)
    gout_f = pl.BlockSpec((c, DN_HD), lambda g, s: (nc - 1 - s, 0))
    gout_r = pl.BlockSpec((c, DN_HD), lambda g, s: (s, 0))

    def body(qf, kf, vf, qr, kr, vr, alf, bef, alr, ber, a_ref, dt_ref, st_ref, dof, dor,
             dqf, dkf, dvf, dqr, dkr, dvr, dgf, dgr, da_ref, ddt_ref, ds_ref):
        first = pl.program_id(1) == 0

        @pl.when(first)
        def _():
            ds_ref[...] = jnp.zeros_like(ds_ref)
            da_ref[...] = jnp.zeros_like(da_ref)
            ddt_ref[...] = jnp.zeros_like(ddt_ref)

        def lanes(x):
            return jnp.sum(x, axis=-1, keepdims=True)

        (q_, k_, v_, do_), gates = _dn_load((qf, kf, vf, dof), (qr, kr, vr, dor), alf, bef, alr, ber, a_ref, dt_ref)
        s = jnp.concatenate([st_ref[0], st_ref[1]], axis=0)
        _, vjp = jax.vjp(_dn_chunk, q_, k_, v_, *gates, s)
        dq, dk, dv, dal, dbe, dalc, da, ddt, dac, ddtc, ds = vjp((do_, ds_ref[...]))
        for h in range(hg):
            sl = slice(DN_HD * h, DN_HD * (h + 1))
            dqf[:, sl], dkf[:, sl], dvf[:, sl] = dq[h], dk[h], dv[h]
            dqr[:, sl], dkr[:, sl], dvr[:, sl] = dq[hg + h], dk[hg + h], dv[hg + h]
        dal, dbe = lanes(dal) + lanes(dalc), lanes(dbe)
        lane = lax.broadcasted_iota(jnp.int32, (c, DN_HD), 1)
        for d, dg_ref in enumerate((dgf, dgr)):
            dg = jnp.zeros((c, DN_HD), F32)
            for h in range(hg):
                dg = jnp.where(lane == h, dal[d * hg + h], jnp.where(lane == hg + h, dbe[d * hg + h], dg))
            dg_ref[...] = dg
        da = jnp.broadcast_to(lanes(da) + lanes(dac), da.shape)
        ddt = jnp.broadcast_to(lanes(ddt) + lanes(ddtc), ddt.shape)
        da_ref[0] += da[:hg]
        da_ref[1] += da[hg:]
        ddt_ref[0] += ddt[:hg]
        ddt_ref[1] += ddt[hg:]
        ds_ref[...] = ds

    tok = jax.ShapeDtypeStruct((t, TOK_W), F32)
    gate = jax.ShapeDtypeStruct((t, DN_HD), F32)
    parsh = jax.ShapeDtypeStruct((2, DN_HEADS, 1, DN_HD), F32)
    res = pl.pallas_call(
        body, name=name, grid=(DN_HEADS // hg, nc),
        in_specs=[tok_f] * 3 + [tok_r] * 3 + [gate_f, gate_f, gate_r, gate_r, par, par, state, tok_f, tok_r],
        out_specs=[tok_f] * 3 + [tok_r] * 3 + [gout_f, gout_r, par, par],
        out_shape=[tok] * 6 + [gate] * 2 + [parsh] * 2,
        scratch_shapes=[pltpu.VMEM((2 * hg, DN_HD, DN_HD), F32)],
        compiler_params=_cparams(("parallel", "arbitrary")),
    )(q, k, v, q, k, v, al, be, al, be, a_rows, dt_rows, states, do, do)
    dqf, dkf, dvf, dqr, dkr, dvr, dgf, dgr, da, ddt = res
    dgate = jnp.concatenate([dgf[:, :2 * DN_HEADS], dgr[:, :2 * DN_HEADS]], axis=1)
    return (dqf, dkf, dvf), (dqr, dkr, dvr), dgate, da, ddt


BAND_BQ = 256
ROW_TB = 512
MEM_TB = 512
CONV_RT = 512


def _to_sub(x, dil):
    l = x.shape[0] // dil
    return x.reshape(l, dil, 4, ATT_HD).transpose(1, 2, 0, 3)


def _from_sub(x, dil):
    l = x.shape[2]
    return x.transpose(2, 0, 1, 3).reshape(l * dil, 4 * ATT_HD)


def _sub_in(x, col_block, dil, pad, dtype, name):
    if dil > 1:
        y = _to_sub(x[:, 256 * col_block:256 * (col_block + 1)], dil).astype(dtype)
        return jnp.pad(y, ((0, 0), (0, 0), (pad, pad), (0, 0))) if pad else y
    t = x.shape[0]

    def body(x_ref, o_ref):
        if pad:
            zero = jnp.zeros((4, pad, ATT_HD), dtype)
            o_ref[0, :, 0:pad, :] = zero
            o_ref[0, :, pad + t:pad + t + pad, :] = zero
        for h in range(4):
            o_ref[0, h, pad:pad + t, :] = x_ref[:, ATT_HD * h:ATT_HD * (h + 1)].astype(dtype)

    return pl.pallas_call(
        body, name=name, grid=(1,), in_specs=[pl.BlockSpec((t, 256), lambda i: (0, col_block))],
        out_specs=pl.BlockSpec((1, 4, t + 2 * pad, ATT_HD), lambda i: (0, 0, 0, 0)),
        out_shape=jax.ShapeDtypeStruct((1, 4, t + 2 * pad, ATT_HD), dtype), compiler_params=_cparams(("arbitrary",)),
    )(x)


def _sub_out(x, dil, pad, dtype, name):
    if dil > 1:
        return _from_sub(x[:, :, pad:x.shape[2] - pad] if pad else x, dil).astype(dtype)
    t = x.shape[2] - 2 * pad

    def body(x_ref, o_ref):
        for h in range(4):
            o_ref[:, ATT_HD * h:ATT_HD * (h + 1)] = x_ref[0, h, pad:pad + t, :].astype(dtype)

    return pl.pallas_call(
        body, name=name, grid=(1,), in_specs=[pl.BlockSpec(x.shape, lambda i: (0, 0, 0, 0))],
        out_specs=pl.BlockSpec((t, 256), lambda i: (0, 0)), out_shape=jax.ShapeDtypeStruct((t, 256), dtype),
        compiler_params=_cparams(("arbitrary",)),
    )(x)


def _heads_major(x):
    return x.reshape(x.shape[0], MEM_HEADS, ATT_HD).transpose(1, 0, 2)


def _heads_minor(x):
    return x.transpose(1, 0, 2).reshape(x.shape[1], MEM_HEADS * ATT_HD)


def _mem_kv_fwd(mem, gain, w_kv, li):
    (memn,) = _rowwise(_fn_pre, [mem], [gain], [(D, BF16)], mem.shape[0], f"memnorm_fwd{li}")
    kv = _matmul(memn, w_kv, "nn", F32, f"memkv_fwd{li}")
    return _heads_major(kv[:, :MEM_W]), _heads_major(kv[:, MEM_W:]), memn


def _mem_kv_bwd(mem, gain, w_kv, memn, dkm, dvm, li):
    dkv = jnp.concatenate([_heads_minor(dkm), _heads_minor(dvm)], axis=1).astype(BF16)
    dw = _matmul(memn, dkv, "tn", BF16, f"memkv_dw{li}")
    dmemn = _matmul(dkv, w_kv, "nt", F32, f"memkv_dx{li}")
    _, (dgain,) = _rowwise_bwd(_fn_pre, [mem], [gain], [dmemn], [None], mem.shape[0], f"memnorm_bwd{li}")
    return dw, dgain


def _attn_mixer_fwd(p, rel_bias, kv_fn):
    t = p.shape[0]
    saved, outs, lses = [], [], []
    for gi, (_, dil) in enumerate(DIL_GROUPS):
        l_sub = t // dil
        bq = min(BAND_BQ, l_sub)
        q = _sub_in(p, gi, dil, 0, p.dtype, f"sub_q{gi}")
        k = _sub_in(p, 3 + gi, dil, BAND_HALF, p.dtype, f"sub_k{gi}")
        v = _sub_in(p, 6 + gi, dil, BAND_HALF, p.dtype, f"sub_v{gi}")
        bm = _band_bias(rel_bias, gi, dil, bq)
        o, lse = _band_fwd(q, k, v, bm, dil, l_sub, bq, f"band_fwd{gi}")
        outs.append(_sub_out(o, dil, 0, F32, f"sub_o{gi}"))
        lses.append(_sub_out(lse, dil, 0, F32, f"sub_lse{gi}"))
        saved.append((q, k, v, bm))
    o_all = jnp.concatenate(outs, axis=1)
    lse_all = jnp.concatenate(lses, axis=1)
    (mixed,) = _rowwise(_fn_combine, [o_all, lse_all], [], [(TOK_W, BF16)], ROW_TB, "combine_fwd")
    qm = _heads_major(p[:, 3 * TOK_W:])
    km, vm, memn = kv_fn(mixed)
    memo = _mem_fwd(qm, km, vm, min(MEM_TB, t), "mem_fwd0")
    cat = jnp.concatenate([mixed, _heads_minor(memo).astype(BF16)], axis=1)
    return cat, (saved, o_all, lse_all, qm), (km, vm, memn)


def _attn_mixer_bwd(dcat, res, km, vm):
    saved, o_all, lse_all, qm = res
    t = dcat.shape[0]
    (do_all, dlse_all), _ = _rowwise_bwd(_fn_combine, [o_all, lse_all], [], [_Cols(dcat, TOK_W, 0)], [BF16, F32],
                                         ROW_TB, "combine_bwd")
    dqs, dks, dvs, dbms = [], [], [], []
    for gi, (_, dil) in enumerate(DIL_GROUPS):
        l_sub = t // dil
        bq = min(BAND_BQ, l_sub)
        q, k, v, bm = saved[gi]
        do = _sub_in(do_all, gi, dil, 0, do_all.dtype, f"sub_do{gi}")
        dl = _sub_in(dlse_all, gi, dil, 0, F32, f"sub_dlse{gi}")
        dq, dk, dv, dbm = _band_bwd(q, k, v, bm, do, dl, dil, l_sub, bq, f"band_bwd{gi}")
        dqs.append(_sub_out(dq, dil, 0, BF16, f"sub_dq{gi}"))
        dks.append(_sub_out(dk, dil, BAND_HALF, BF16, f"sub_dk{gi}"))
        dvs.append(_sub_out(dv, dil, BAND_HALF, BF16, f"sub_dv{gi}"))
        dbms.append(dbm)
    dqm, dkm, dvm = _mem_bwd(qm, km, vm, _heads_major(dcat[:, TOK_W:]), min(MEM_TB, t), "mem_bwd0")
    dp = jnp.concatenate(dqs + dks + dvs + [_heads_minor(dqm).astype(BF16)], axis=1)
    return dp, _relbias_grad(dbms, "relbias_grad"), dkm, dvm


def _dn_mixer_fwd(p, conv_w, a_log, dt_bias, out_norm, km, vm):
    t = p.shape[0]
    rt = min(CONV_RT, t)
    xp = p
    w8 = jnp.pad(conv_w, ((0, 8 - DN_CONV), (0, 0)))
    q = _conv_fwd(xp, w8, 0, rt, "conv_fwd_q")
    k = _conv_fwd(xp, w8, 1, rt, "conv_fwd_k")
    v = _conv_fwd(xp, w8, 2, rt, "conv_fwd_v")
    gate = p[:, 4 * TOK_W:4 * TOK_W + 4 * DN_HEADS].reshape(t, 2, 2, DN_HEADS)
    bshape = (2, DN_HEADS, t, DN_HD)
    al = jnp.broadcast_to(gate[:, :, 0, :].transpose(1, 2, 0)[..., None], bshape)
    be = jnp.broadcast_to(gate[:, :, 1, :].transpose(1, 2, 0)[..., None], bshape)
    a_rows = jnp.broadcast_to(a_log[:, :, None, None], (2, DN_HEADS, 1, DN_HD))
    dt_rows = jnp.broadcast_to(dt_bias[:, :, None, None], (2, DN_HEADS, 1, DN_HD))
    o_f, o_r, states = _dn_fwd(q, k, v, al, be, a_rows, dt_rows, "dn_fwd")
    gain = out_norm.reshape(1, DN_HD)
    (og,) = _rowwise(_fn_outnorm, [o_f, o_r, _Cols(p, TOK_W, 3)], [gain], [(TOK_W, BF16)], ROW_TB, "outnorm_fwd")
    qm = _heads_major(p[:, 4 * TOK_W + 4 * DN_HEADS:DN_IN])
    memo = _mem_fwd(qm, km, vm, min(MEM_TB, t), "mem_fwd1")
    cat = jnp.concatenate([og, _heads_minor(memo).astype(BF16)], axis=1)
    return cat, (xp, w8, q, k, v, al, be, a_rows, dt_rows, o_f, o_r, states, gain, qm)


def _dn_mixer_bwd(dcat, res, km, vm):
    xp, w8, q, k, v, al, be, a_rows, dt_rows, o_f, o_r, states, gain, qm = res
    t = dcat.shape[0]
    rt = min(CONV_RT, t)
    (do, dz), (dgain,) = _rowwise_bwd(_fn_outnorm, [o_f, o_r, _Cols(xp, TOK_W, 3)], [gain],
                                      [_Cols(dcat, TOK_W, 0)], [F32, None, BF16],
                                      ROW_TB, "outnorm_bwd")
    d_f, d_r, dgate, da, ddt = _dn_bwd(q, k, v, al, be, a_rows, dt_rows, states, do, "dn_bwd")
    dqm, dkm, dvm = _mem_bwd(qm, km, vm, _heads_major(dcat[:, TOK_W:]), min(MEM_TB, t), "mem_bwd1")
    rest = jnp.concatenate([dgate.astype(BF16), _heads_minor(dqm).astype(BF16),
                            jnp.zeros((t, DN_IN_PAD - DN_IN), BF16)], axis=1)
    dp = lax.dynamic_update_slice(lax.empty((t, DN_IN_PAD), BF16), dz, (0, 3 * TOK_W))
    dp = lax.dynamic_update_slice(dp, rest, (0, 4 * TOK_W))
    dws = []
    for kind, nm in enumerate("qkv"):
        dp, dw = _conv_bwd(xp, w8, d_f[kind], d_r[kind], dp, kind, rt, f"conv_bwd_{nm}")
        dws.append(dw)
    dconv = jnp.concatenate(dws, axis=1)[:DN_CONV]
    return dp, dconv, da[:, :, 0, 0], ddt[:, :, 0, 0], dgain.reshape(DN_HD), dkm, dvm


SWI_TB = 256


def _ffn_fwd(h, w_gu_t, w_d, li):
    gu = _matmul(h, w_gu_t, "nt", BF16, f"ffn_gu{li}")
    (a,) = _rowwise(_fn_swiglu, [gu], [], [(D_FF, BF16)], SWI_TB, f"swiglu_fwd{li}")
    return _matmul(a, w_d, "nn", F32, f"ffn_down{li}"), gu, a


def _ffn_bwd(df, h, w_gu_t, w_d, gu, a, li):
    da = _matmul(df, w_d, "nt", BF16, f"ffn_down_dx{li}")
    dwd = _matmul(a, df, "tn", BF16, f"ffn_down_dw{li}")
    (dgu,), _ = _rowwise_bwd(_fn_swiglu, [gu], [], [da], [BF16], SWI_TB, f"swiglu_bwd{li}")
    dh = _matmul(dgu, w_gu_t, "nn", F32, f"ffn_gu_dx{li}")
    dwgu_t = _matmul(dgu, h, "tn", BF16, f"ffn_gu_dw{li}")
    return dh, dwgu_t, dwd


def _fn_first(x, g):
    return x, _rms(x, g)


def _me_xyc():
    return lax.axis_index("x"), lax.axis_index("y"), lax.axis_index("c")


def _flip(coords, k):
    x, y, c = coords
    return (1 - x if k & 4 else x, 1 - y if k & 2 else y, 1 - c if k & 1 else c)


def _index(coords):
    x, y, c = coords
    return 4 * x + 2 * y + c


def _window(ref, axis, size, d):
    idx = [slice(None)] * len(ref.shape)
    idx[axis] = pl.ds(pl.multiple_of(d * size, size), size)
    return ref.at[tuple(idx)]


def _comm_call(body, n, ins, out_shapes, name):
    hbm = pl.BlockSpec(memory_space=pl.ANY)
    return pl.pallas_call(
        body, name=name, in_specs=[hbm] * n, out_specs=[hbm] * n, out_shape=out_shapes,
        scratch_shapes=[pltpu.SemaphoreType.DMA((N_DEV - 1, n)), pltpu.SemaphoreType.DMA((N_DEV - 1, n)),
                        pltpu.SemaphoreType.DMA((n,))],
    )(*ins)


def _run_exchange(n, local, remote, send_sems, recv_sems):
    me = _me_xyc()
    locs = [local(p) for p in range(n)]
    for cp in locs:
        cp.start()
    sends = [remote(k, p, me, _flip(me, k)) for k in range(1, N_DEV) for p in range(n)]
    for cp in sends:
        cp.start()
    for k in range(1, N_DEV):
        for p in range(n):
            remote(k, p, _flip(me, k), me).wait_recv()
    for cp in sends:
        cp.wait_send()
    for cp in locs:
        cp.wait()


def _all_gather(shards, axes, name):
    n = len(shards)
    sizes = [s.shape[a] for s, a in zip(shards, axes)]

    def body(*refs):
        ins, outs = refs[:n], refs[n:2 * n]
        send_sems, recv_sems, loc_sems = refs[2 * n:]
        me = _me_xyc()

        def local(p):
            return pltpu.make_async_copy(ins[p], _window(outs[p], axes[p], sizes[p], _index(me)), loc_sems.at[p])

        def remote(k, p, owner, to):
            return pltpu.make_async_remote_copy(
                src_ref=ins[p], dst_ref=_window(outs[p], axes[p], sizes[p], _index(owner)),
                send_sem=send_sems.at[k - 1, p], recv_sem=recv_sems.at[k - 1, p], device_id=to, device_id_type=MESH)

        _run_exchange(n, local, remote, send_sems, recv_sems)

    def full(s, a):
        return s.shape[:a] + (N_DEV * s.shape[a],) + s.shape[a + 1:]

    return _comm_call(body, n, shards, [jax.ShapeDtypeStruct(full(s, a), s.dtype) for s, a in zip(shards, axes)], name)


def _exchange(fulls, axes, name):
    n = len(fulls)
    sizes = [None if a is None else f.shape[a] // N_DEV for f, a in zip(fulls, axes)]

    def part_shape(f, a):
        return f.shape if a is None else f.shape[:a] + (f.shape[a] // N_DEV,) + f.shape[a + 1:]

    def body(*refs):
        ins, outs = refs[:n], refs[n:2 * n]
        send_sems, recv_sems, loc_sems = refs[2 * n:]
        me = _me_xyc()

        def src(p, to):
            return ins[p] if axes[p] is None else _window(ins[p], axes[p], sizes[p], _index(to))

        def local(p):
            return pltpu.make_async_copy(src(p, me), outs[p].at[_index(me)], loc_sems.at[p])

        def remote(k, p, sender, to):
            return pltpu.make_async_remote_copy(
                src_ref=src(p, to), dst_ref=outs[p].at[_index(sender)],
                send_sem=send_sems.at[k - 1, p], recv_sem=recv_sems.at[k - 1, p], device_id=to, device_id_type=MESH)

        _run_exchange(n, local, remote, send_sems, recv_sems)

    return _comm_call(body, n, fulls,
                      [jax.ShapeDtypeStruct((N_DEV,) + part_shape(f, a), f.dtype) for f, a in zip(fulls, axes)], name)


_HBM = pl.BlockSpec(memory_space=pltpu.HBM)
_SEM = pl.BlockSpec(memory_space=pltpu.SEMAPHORE)
_EFFECT = pltpu.SideEffectType.DATAFLOW_SIDE_EFFECTING


def _in_hbm(a):
    return pltpu.with_memory_space_constraint(a, pltpu.HBM)


def _split_start(srcs, lands, after, descr, name):
    n = len(srcs)

    def body(*refs):
        ins, lnd = refs[:n], refs[n:2 * n]
        send_sems, recv_sems = refs[2 * n + 1], refs[2 * n + 2]
        token = refs[-1]
        me = _me_xyc()
        for k in range(1, N_DEV):
            for p in range(n):
                descr(k, p, ins, lnd, send_sems, recv_sems, me, _flip(me, k)).start()
        token[...] = jnp.zeros_like(token)

    sems = pltpu.SemaphoreType.DMA(((N_DEV - 1) * n,))
    res = pl.pallas_call(
        body, name=name,
        out_shape=(sems, sems, *[pltpu.HBM(a.shape, a.dtype) for a in (*srcs, *lands)],
                   jax.ShapeDtypeStruct((8, 128), F32)),
        in_specs=[_HBM] * (2 * n) + [pl.BlockSpec(memory_space=pl.ANY)],
        out_specs=(_SEM, _SEM, *[_HBM] * (2 * n), pl.BlockSpec(memory_space=pltpu.VMEM)),
        input_output_aliases={i: 2 + i for i in range(2 * n)},
        compiler_params=pltpu.CompilerParams(has_side_effects=_EFFECT),
    )(*[_in_hbm(a) for a in (*srcs, *lands)], after)
    return res[0], res[1], res[2:2 + n], res[2 + n:2 + 2 * n], res[-1]


def _split_wait(send_sems, recv_sems, srcs, lands, after, descr, name):
    n = len(srcs)

    def body(*refs):
        ins, lnd = refs[:n], refs[n:2 * n]
        s_sems, r_sems = refs[2 * n], refs[2 * n + 1]
        me = _me_xyc()
        for k in range(1, N_DEV):
            for p in range(n):
                peer = _flip(me, k)
                descr(k, p, ins, lnd, s_sems, r_sems, me, peer).wait_send()
                descr(k, p, ins, lnd, s_sems, r_sems, peer, me).wait_recv()

    res = pl.pallas_call(
        body, name=name,
        out_shape=tuple(pltpu.HBM(a.shape, a.dtype) for a in (*srcs, *lands)),
        in_specs=[_HBM] * (2 * n) + [_SEM, _SEM, pl.BlockSpec(memory_space=pl.ANY)],
        out_specs=tuple([_HBM] * (2 * n)),
        input_output_aliases={i: i for i in range(2 * n)},
        compiler_params=pltpu.CompilerParams(has_side_effects=_EFFECT),
    )(*srcs, *lands, send_sems, recv_sems, after)
    return list(res[n:])


def _gather_descr(axes, sizes):
    def descr(k, p, ins, lnd, send_sems, recv_sems, sender, dest):
        return pltpu.make_async_remote_copy(
            src_ref=ins[p], dst_ref=_window(lnd[p], axes[p], sizes[p], _index(sender)),
            send_sem=send_sems.at[(k - 1) * len(axes) + p], recv_sem=recv_sems.at[(k - 1) * len(axes) + p],
            device_id=dest, device_id_type=MESH)
    return descr


def _exchange_descr(axes, sizes):
    def descr(k, p, ins, lnd, send_sems, recv_sems, sender, dest):
        return pltpu.make_async_remote_copy(
            src_ref=_window(ins[p], axes[p], sizes[p], _index(dest)), dst_ref=lnd[p].at[_index(sender)],
            send_sem=send_sems.at[(k - 1) * len(axes) + p], recv_sem=recv_sems.at[(k - 1) * len(axes) + p],
            device_id=dest, device_id_type=MESH)
    return descr


def _gather_begin(shards, axes, after, name):
    sizes = [s.shape[a] for s, a in zip(shards, axes)]
    me = _index(_me_xyc())
    lands = []
    for s, a, sz in zip(shards, axes, sizes):
        full = s.shape[:a] + (N_DEV * sz,) + s.shape[a + 1:]
        lands.append(lax.dynamic_update_slice_in_dim(lax.empty(full, s.dtype), s, me * sz, a))
    descr = _gather_descr(axes, sizes)
    send_sems, recv_sems, srcs, lands, token = _split_start(shards, lands, after, descr, name)
    return (send_sems, recv_sems, srcs, lands, descr), token


def _exchange_begin(fulls, axes, after, name):
    sizes = [f.shape[a] // N_DEV for f, a in zip(fulls, axes)]
    me = _index(_me_xyc())
    lands = []
    for f, a, sz in zip(fulls, axes, sizes):
        own = lax.dynamic_slice_in_dim(f, me * sz, sz, a)
        lands.append(lax.dynamic_update_slice_in_dim(lax.empty((N_DEV,) + own.shape, f.dtype), own[None], me, 0))
    descr = _exchange_descr(axes, sizes)
    send_sems, recv_sems, srcs, lands, token = _split_start(fulls, lands, after, descr, name)
    return (send_sems, recv_sems, srcs, lands, descr), token


def _split_end(handle, after, name):
    send_sems, recv_sems, srcs, lands, descr = handle
    return _split_wait(send_sems, recv_sems, srcs, lands, after, descr, name)


def _adam_math(g, w, m, v):
    m = ADAM_B1 * m + (1.0 - ADAM_B1) * g
    v = ADAM_B2 * v + (1.0 - ADAM_B2) * (g * g)
    m_hat = m / (1.0 - ADAM_B1 ** ADAM_STEP)
    v_hat = v / (1.0 - ADAM_B2 ** ADAM_STEP)
    delta = -ADAM_LR * (m_hat / (jnp.sqrt(v_hat) + ADAM_EPS) + ADAM_WD * w)
    return delta, m, v


def _sum_slabs(r_ref):
    g = r_ref[0].astype(F32)
    for s in range(1, N_DEV):
        g = g + r_ref[s].astype(F32)
    return g


def _adamw_reduce(recv, w, m, v, tb, name):
    r, c = w.shape

    def body(r_ref, w_ref, m_ref, v_ref, g_ref, d_ref, nm_ref, nv_ref):
        g = _sum_slabs(r_ref)
        g_ref[...] = g
        d_ref[...], nm_ref[...], nv_ref[...] = _adam_math(g, w_ref[...], m_ref[...], v_ref[...])

    blk = pl.BlockSpec((tb, c), lambda i: (i, 0))
    return pl.pallas_call(
        body, name=name, grid=(r // tb,),
        in_specs=[pl.BlockSpec((N_DEV, tb, c), lambda i: (0, i, 0)), blk, blk, blk],
        out_specs=[blk] * 4, out_shape=[jax.ShapeDtypeStruct((r, c), F32)] * 4,
        compiler_params=_cparams(("parallel",)),
    )(recv, w, m, v)


def _reduce8(recv, tb, name):
    r, c = recv.shape[1:]

    def body(r_ref, g_ref):
        g_ref[...] = _sum_slabs(r_ref)

    return pl.pallas_call(
        body, name=name, grid=(r // tb,),
        in_specs=[pl.BlockSpec((N_DEV, tb, c), lambda i: (0, i, 0))],
        out_specs=pl.BlockSpec((tb, c), lambda i: (i, 0)), out_shape=jax.ShapeDtypeStruct((r, c), F32),
        compiler_params=_cparams(("parallel",)),
    )(recv)


def _adamw(g, w, m, v, tb, name):
    r, c = w.shape

    def body(g_ref, w_ref, m_ref, v_ref, d_ref, nm_ref, nv_ref):
        d_ref[...], nm_ref[...], nv_ref[...] = _adam_math(g_ref[...], w_ref[...], m_ref[...], v_ref[...])

    blk = pl.BlockSpec((tb, c), lambda i: (i, 0))
    return pl.pallas_call(
        body, name=name, grid=(r // tb,), in_specs=[blk] * 4, out_specs=[blk] * 3,
        out_shape=[jax.ShapeDtypeStruct((r, c), F32)] * 3, compiler_params=_cparams(("parallel",)),
    )(g, w, m, v)


DN_IN_SHARD = DN_IN // N_DEV
DN_IN_SHARD_PAD = 432
CONV_SHARD = (1, DN_CONV, 288)


def _pack_small(arrs, rows):
    flat = jnp.concatenate([a.astype(F32).reshape(-1) for a in arrs])
    return jnp.pad(flat, (0, rows * PACK_C - flat.size)).reshape(rows, PACK_C)


def _unpack_small(packed, shapes):
    flat, out, off = packed.reshape(-1), [], 0
    for shp in shapes:
        n = int(np.prod(shp))
        out.append(flat[off:off + n].reshape(shp))
        off += n
    return out


def kernel(x, mem, rel_bias, att_w_in, att_w_out, dn_w_in, dn_conv, dn_a_log, dn_dt_bias, dn_out_norm, dn_w_out, mem_norm, mem_w_kv, norm_mix_pre, norm_mix_post, norm_ffn_pre, norm_ffn_post, ffn_w_gate_up, ffn_w_down, loss_target, m_rel_bias, m_att_w_in, m_att_w_out, m_dn_w_in, m_dn_conv, m_dn_a_log, m_dn_dt_bias, m_dn_out_norm, m_dn_w_out, m_mem_norm, m_mem_w_kv, m_norm_mix_pre, m_norm_mix_post, m_norm_ffn_pre, m_norm_ffn_post, m_ffn_w_gate_up, m_ffn_w_down, v_rel_bias, v_att_w_in, v_att_w_out, v_dn_w_in, v_dn_conv, v_dn_a_log, v_dn_dt_bias, v_dn_out_norm, v_dn_w_out, v_mem_norm, v_mem_w_kv, v_norm_mix_pre, v_norm_mix_post, v_norm_ffn_pre, v_norm_ffn_post, v_ffn_w_gate_up, v_ffn_w_down):
    x0, mem0, tgt = x[0], mem[0], loss_target[0]
    t = x0.shape[0]
    axes = ("x", "y", "c")

    def t_shard(w):
        return jnp.swapaxes(w, 1, 2).astype(BF16)

    dn_in_pad = ((0, 0), (0, DN_IN_SHARD_PAD - DN_IN_SHARD), (0, 0))
    (w_att_in_t,) = _all_gather([t_shard(att_w_in)], [1], "allgather_first")
    w_att_in_t = w_att_in_t[0]
    gu_t, down = t_shard(ffn_w_gate_up), ffn_w_down.astype(BF16)
    gather_o, tok_o = _gather_begin([att_w_out.astype(BF16), mem_w_kv.astype(BF16)], [1, 1], w_att_in_t,
                                    "gather_att_out_start")
    gather_a, tok_a = _gather_begin([gu_t[0:1], down[0:1]], [1, 1], tok_o, "gather_ffn0_start")
    gather_b, tok_b = _gather_begin(
        [jnp.pad(t_shard(dn_w_in), dn_in_pad), dn_w_out.astype(BF16), gu_t[1:2], down[1:2], dn_conv],
        [1, 1, 1, 1, 0], tok_a, "gather_layer1_start")

    def gain(a, i):
        return a[i].reshape(1, D)

    (h0,) = _rowwise(_fn_pre, [x0], [gain(norm_mix_pre, 0) + tok_b[0:1, 0:1]], [(D, BF16)], ROW_TB, "pre0")
    p0 = _matmul(h0, w_att_in_t, "nt", BF16, "att_in")
    late = {}

    def kv0(after):
        late["w_att_out"], late["w_kv"] = _split_end(gather_o, after, "gather_att_out_wait")
        return _mem_kv_fwd(mem0, gain(mem_norm, 0), late["w_kv"][0], 0)

    cat0, res0, (km0, vm0, memn0) = _attn_mixer_fwd(p0, rel_bias, kv0)
    w_att_out, w_kv = late["w_att_out"][0], late["w_kv"]
    y0 = _matmul(cat0, w_att_out, "nn", F32, "att_out")
    g_a = [gain(norm_mix_post, 0), gain(norm_ffn_pre, 0)]
    x1, h1 = _rowwise(_fn_res_pre, [x0, y0], g_a, [(D, F32), (D, BF16)], ROW_TB, "res_pre0")
    w_gu_t0, w_down0 = [w[0] for w in _split_end(gather_a, h1, "gather_ffn0_wait")]
    f0, gu0, a0 = _ffn_fwd(h1, w_gu_t0, w_down0, 0)
    g_b = [gain(norm_ffn_post, 0), gain(norm_mix_pre, 1)]
    x2, h2 = _rowwise(_fn_res_pre, [x1, f0], g_b, [(D, F32), (D, BF16)], ROW_TB, "res_pre1")
    km1, vm1, memn1 = _mem_kv_fwd(mem0, gain(mem_norm, 1), w_kv[1], 1)
    w_dn_in_g, w_dn_out, w_gu_t1, w_down1, conv_g = _split_end(gather_b, h2, "gather_layer1_wait")
    w_dn_in_g, w_dn_out, w_gu_t1, w_down1 = w_dn_in_g[0], w_dn_out[0], w_gu_t1[0], w_down1[0]
    conv_full = conv_g.transpose(1, 0, 2).reshape(DN_CONV, 3 * TOK_W)
    w_dn_in_t = jnp.concatenate(
        [w_dn_in_g[DN_IN_SHARD_PAD * j:DN_IN_SHARD_PAD * j + DN_IN_SHARD] for j in range(N_DEV)]
        + [jnp.zeros((DN_IN_PAD - DN_IN, D), BF16)], axis=0)
    p1 = _matmul(h2, w_dn_in_t, "nt", F32, "dn_in")
    cat1, res1 = _dn_mixer_fwd(p1, conv_full, dn_a_log[0], dn_dt_bias[0], dn_out_norm[0], km1, vm1)
    y1 = _matmul(cat1, w_dn_out, "nn", F32, "dn_out")
    g_c = [gain(norm_mix_post, 1), gain(norm_ffn_pre, 1)]
    x3, h3 = _rowwise(_fn_res_pre, [x2, y1], g_c, [(D, F32), (D, BF16)], ROW_TB, "res_pre2")
    f1, gu1, a1 = _ffn_fwd(h3, w_gu_t1, w_down1, 1)
    g_d = [gain(norm_ffn_post, 1)]
    (x4,) = _rowwise(_fn_res, [x3, f1], g_d, [(D, F32)], ROW_TB, "res3")
    dx4, lrow = _loss_kernel(x4, tgt, ROW_TB, "loss")
    loss = lax.psum(lrow[0, 0] * (0.5 / D), axes)

    (df1,), (dg_fpost1,) = _rowwise_bwd(_fn_res, [x3, f1], g_d, [dx4], [None, BF16], ROW_TB, "res3_bwd")
    dh3, dwgu1, dwd1 = _ffn_bwd(df1, h3, w_gu_t1, w_down1, gu1, a1, 1)
    (dx2, dy1), (dg_mpost1, dg_fpre1) = _rowwise_bwd(_fn_res_pre, [x2, y1], g_c, [dx4, dh3], [F32, BF16],
                                                     ROW_TB, "res_pre2_bwd")
    dcat1 = _matmul(dy1, w_dn_out, "nt", F32, "dn_out_dx")
    dw_dn_out = _matmul(cat1, dy1, "tn", BF16, "dn_out_dw")
    dp1, dconv, da_log, ddt_bias, dout_norm, dkm1, dvm1 = _dn_mixer_bwd(dcat1, res1, km1, vm1)
    dwkv1, dg_mem1 = _mem_kv_bwd(mem0, gain(mem_norm, 1), w_kv[1], memn1, dkm1, dvm1, 1)
    dh2 = _matmul(dp1, w_dn_in_t, "nn", F32, "dn_in_dx")
    dw_dn_in_t = _matmul(dp1, h2, "tn", BF16, "dn_in_dw")
    dn_in_parts = [jnp.pad(dw_dn_in_t[DN_IN_SHARD * j:DN_IN_SHARD * (j + 1)],
                           ((0, DN_IN_SHARD_PAD - DN_IN_SHARD), (0, 0))) for j in range(N_DEV)]
    xch_b, tok = _exchange_begin(
        [jnp.concatenate(dn_in_parts, axis=0)[None], dw_dn_out[None], dwkv1[None], dwgu1[None], dwd1[None]],
        [1, 1, 1, 1, 1], dh2, "exchange_layer1_start")
    (dx1, df0), (dg_fpost0, dg_mpre1) = _rowwise_bwd(_fn_res_pre, [x1, f0], [g + tok[0:1, 0:1] for g in g_b],
                                                     [dx2, dh2], [F32, BF16], ROW_TB, "res_pre1_bwd")
    dh1, dwgu0, dwd0 = _ffn_bwd(df0, h1, w_gu_t0, w_down0, gu0, a0, 0)
    xch_a, tok = _exchange_begin([dwgu0[None], dwd0[None]], [1, 1], dh1, "exchange_ffn0_start")
    (dx0, dy0), (dg_mpost0, dg_fpre0) = _rowwise_bwd(_fn_res_pre, [x0, y0], [g + tok[0:1, 0:1] for g in g_a],
                                                     [dx1, dh1], [F32, BF16], ROW_TB, "res_pre0_bwd")
    dcat0 = _matmul(dy0, w_att_out, "nt", F32, "att_out_dx")
    dw_att_out = _matmul(cat0, dy0, "tn", BF16, "att_out_dw")
    dp0, drel, dkm0, dvm0 = _attn_mixer_bwd(dcat0, res0, km0, vm0)
    dwkv0, dg_mem0 = _mem_kv_bwd(mem0, gain(mem_norm, 0), w_kv[0], memn0, dkm0, dvm0, 0)
    xch_o, tok = _exchange_begin([dw_att_out[None], dwkv0[None]], [1, 1], dp0, "exchange_att_out_start")
    dw_att_in_t = _matmul(dp0, h0, "tn", BF16, "att_in_dw")
    xch_i, tok_i = _exchange_begin([dw_att_in_t[None]], [1], tok, "exchange_att_in_start")
    dh0 = _matmul(dp0, w_att_in_t, "nn", F32, "att_in_dx")
    (grad_x,), (dg_mpre0,) = _rowwise_bwd(_fn_first, [x0], [gain(norm_mix_pre, 0) + tok_i[0:1, 0:1]], [dx0, dh0],
                                          [F32], ROW_TB, "pre0_bwd")

    small_grads = [drel, da_log, ddt_bias, dout_norm, jnp.concatenate([dg_mem0, dg_mem1]),
                   jnp.concatenate([dg_mpre0, dg_mpre1]), jnp.concatenate([dg_mpost0, dg_mpost1]),
                   jnp.concatenate([dg_fpre0, dg_fpre1]), jnp.concatenate([dg_fpost0, dg_fpost1]), dconv]
    (r_small,) = _exchange([_pack_small(small_grads, SMALL_ROWS)], [None], "exchange_last")
    (r_att_in,) = _split_end(xch_i, r_small, "exchange_att_in_wait")
    r_att_out, r_kv0 = _split_end(xch_o, r_small, "exchange_att_out_wait")
    r_gu0, r_down0 = _split_end(xch_a, r_small, "exchange_ffn0_wait")
    r_dn_in, r_dn_out, r_kv1, r_gu1, r_down1 = _split_end(xch_b, r_small, "exchange_layer1_wait")

    def rows(a):
        return a.reshape((-1,) + a.shape[-1:])

    def row_sharded(recv, w, m, v, tb, name):
        outs = _adamw_reduce(recv.reshape((N_DEV, -1) + recv.shape[-1:]), rows(w), rows(m), rows(v), tb, name)
        return [o.reshape(w.shape) for o in outs]

    def col_sharded(recv, w, m, v, tb, name):
        g_t = _reduce8(recv.reshape((N_DEV, -1) + recv.shape[-1:]), tb, name + "_sum")
        g = jnp.swapaxes(g_t.reshape(recv.shape[1:])[:, :w.shape[2]], 1, 2)
        outs = _adamw(rows(g), rows(w), rows(m), rows(v), 256, name)
        return [g] + [o.reshape(w.shape) for o in outs]

    def per_layer(fn, recvs, w, m, v, tb, name):
        outs = [fn(r, w[l:l + 1], m[l:l + 1], v[l:l + 1], tb, f"{name}{l}") for l, r in enumerate(recvs)]
        return [jnp.concatenate(pair, axis=0) for pair in zip(*outs)]

    big = [col_sharded(r_att_in, att_w_in, m_att_w_in, v_att_w_in, 320, "adamw_att_in"),
           row_sharded(r_att_out, att_w_out, m_att_w_out, v_att_w_out, 128, "adamw_att_out"),
           col_sharded(r_dn_in, dn_w_in, m_dn_w_in, v_dn_w_in, 432, "adamw_dn_in"),
           row_sharded(r_dn_out, dn_w_out, m_dn_w_out, v_dn_w_out, 128, "adamw_dn_out"),
           per_layer(row_sharded, [r_kv0, r_kv1], mem_w_kv, m_mem_w_kv, v_mem_w_kv, 128, "adamw_mem_kv"),
           per_layer(col_sharded, [r_gu0, r_gu1], ffn_w_gate_up, m_ffn_w_gate_up, v_ffn_w_gate_up, 176,
                     "adamw_ffn_gu"),
           per_layer(row_sharded, [r_down0, r_down1], ffn_w_down, m_ffn_w_down, v_ffn_w_down, 176,
                     "adamw_ffn_down")]
    g_big, d_big, nm_big, nv_big = [[b[i] for b in big] for i in range(4)]

    g_small = _reduce8(r_small, SMALL_ROWS, "reduce_small")
    rep_shapes = [(32, 12), (1, 2, 6), (1, 2, 6), (1, 128), (2, D), (2, D), (2, D), (2, D), (2, D)]
    *g_rep, g_conv_full = _unpack_small(g_small, rep_shapes + [(DN_CONV, 3 * TOK_W)])
    me = _index(_me_xyc())
    g_conv = lax.dynamic_slice(g_conv_full, (0, me * 288), (DN_CONV, 288)).reshape(CONV_SHARD)
    small_shapes = rep_shapes + [CONV_SHARD]
    small_w = [rel_bias, dn_a_log, dn_dt_bias, dn_out_norm, mem_norm, norm_mix_pre, norm_mix_post,
               norm_ffn_pre, norm_ffn_post, dn_conv]
    small_m = [m_rel_bias, m_dn_a_log, m_dn_dt_bias, m_dn_out_norm, m_mem_norm, m_norm_mix_pre, m_norm_mix_post,
               m_norm_ffn_pre, m_norm_ffn_post, m_dn_conv]
    small_v = [v_rel_bias, v_dn_a_log, v_dn_dt_bias, v_dn_out_norm, v_mem_norm, v_norm_mix_pre, v_norm_mix_post,
               v_norm_ffn_pre, v_norm_ffn_post, v_dn_conv]
    g_small_list = g_rep + [g_conv]
    outs_small = _adamw(_pack_small(g_small_list, 24), _pack_small(small_w, 24), _pack_small(small_m, 24),
                        _pack_small(small_v, 24), 24, "adamw_small")
    d_small, nm_small, nv_small = [_unpack_small(o, small_shapes) for o in outs_small]

    def ordered(small, big):
        return [small[0], big[0], big[1], big[2], small[9], small[1], small[2], small[3], big[3], small[4],
                big[4], small[5], small[6], small[7], small[8], big[5], big[6]]

    g_small_out = [g.reshape(s) for g, s in zip(g_small_list, small_shapes)]
    return (loss, grad_x[None], *ordered(g_small_out, g_big), *ordered(d_small, d_big),
            *ordered(nm_small, nm_big), *ordered(nv_small, nv_big))
```

```python
import functools
import math
from typing import NamedTuple

import numpy as np
import jax
import jax.numpy as jnp
from jax import lax
from jax.experimental import pallas as pl
from jax.experimental.pallas import tpu as pltpu

F32 = jnp.float32
BF16 = jnp.bfloat16
HI = lax.Precision.HIGHEST
MESH = pl.DeviceIdType.MESH

N_DEV = 8
D = 1024
EPS = 1e-6
NEG = -1e30
TOK_W = 768
MEM_W = 256
ATT_HD = 64
DIL_GROUPS = ((128, 1), (512, 4), (2048, 16))
BAND_HALF = 64
REL_BUCKETS = 32
REL_MAX_DIST = 1024
DN_HD = 128
DN_HEADS = 6
DN_CONV = 5
DN_CHUNK = 64
MEM_HEADS = 4
D_FF = 2816
DN_IN = 3352
DN_IN_PAD = 3456

ADAM_LR, ADAM_B1, ADAM_B2, ADAM_EPS, ADAM_WD, ADAM_STEP = 0.001, 0.9, 0.999, 1e-08, 0.01, 10

PACK_C = 512
SMALL_ROWS = 48
VMEM_LIMIT = 48 * 1024 * 1024


def _cparams(sem=None):
    kw = dict(vmem_limit_bytes=VMEM_LIMIT)
    if sem is not None:
        kw["dimension_semantics"] = sem
    return pltpu.CompilerParams(**kw)


def _tile(n, cap):
    if n <= cap:
        return n
    best = None
    for t in range(128, cap + 1, 128):
        if n % t == 0:
            best = t
    assert best is not None, (n, cap)
    return best


def _matmul(a, b, mode, out_dtype, name, tm=1024, tn=1408, tk=None):
    if tk is None:
        tk = 4096 if mode == "tn" else 2816
    if mode == "tn":
        tm = min(tm, 512)
    if mode == "nn":
        (m, kc), (_, n) = a.shape, b.shape
        dims = (((1,), (0,)), ((), ()))
    elif mode == "nt":
        (m, kc), (n, _) = a.shape, b.shape
        dims = (((1,), (1,)), ((), ()))
    else:
        (kc, m), (_, n) = a.shape, b.shape
        dims = (((0,), (0,)), ((), ()))
    tm = m if m <= tm else _tile(m, tm)
    tn = _tile(n, tn)
    tk = _tile(kc, tk)
    nk = kc // tk

    def body(a_ref, b_ref, o_ref, acc_ref):
        k = pl.program_id(2)
        part = lax.dot_general(a_ref[...], b_ref[...], dims, preferred_element_type=F32)

        @pl.when(k == 0)
        def _():
            acc_ref[...] = part

        @pl.when(k > 0)
        def _():
            acc_ref[...] += part

        @pl.when(k == nk - 1)
        def _():
            o_ref[...] = acc_ref[...].astype(o_ref.dtype)

    if mode == "nn":
        a_spec = pl.BlockSpec((tm, tk), lambda i, j, k: (i, k))
        b_spec = pl.BlockSpec((tk, tn), lambda i, j, k: (k, j))
    elif mode == "nt":
        a_spec = pl.BlockSpec((tm, tk), lambda i, j, k: (i, k))
        b_spec = pl.BlockSpec((tn, tk), lambda i, j, k: (j, k))
    else:
        a_spec = pl.BlockSpec((tk, tm), lambda i, j, k: (k, i))
        b_spec = pl.BlockSpec((tk, tn), lambda i, j, k: (k, j))
    return pl.pallas_call(
        body, name=name, grid=(m // tm, n // tn, nk),
        in_specs=[a_spec, b_spec],
        out_specs=pl.BlockSpec((tm, tn), lambda i, j, k: (i, j)),
        out_shape=jax.ShapeDtypeStruct((m, n), out_dtype),
        scratch_shapes=[pltpu.VMEM((tm, tn), F32)],
        compiler_params=_cparams(("parallel", "parallel", "arbitrary")),
    )(a, b)


class _Cols(NamedTuple):
    arr: jax.Array
    width: int
    block: int

    @property
    def shape(self):
        return (self.arr.shape[0], self.width)


def _row_spec(r, tb):
    if isinstance(r, _Cols):
        return pl.BlockSpec((tb, r.width), lambda i, b=r.block: (i, b))
    return pl.BlockSpec((tb, r.shape[1]), lambda i: (i, 0))


def _row_arr(r):
    return r.arr if isinstance(r, _Cols) else r


def _rowwise(fn, rows, params, outs, tb, name):
    t = rows[0].shape[0]
    nr, npar = len(rows), len(params)

    def body(*refs):
        ins = [r[...].astype(F32) for r in refs[:nr + npar]]
        res = fn(*ins)
        for o_ref, r in zip(refs[nr + npar:], res):
            o_ref[...] = r.astype(o_ref.dtype)

    return pl.pallas_call(
        body, name=name, grid=(t // tb,),
        in_specs=[_row_spec(r, tb) for r in rows] + [pl.BlockSpec(p.shape, lambda i: (0, 0)) for p in params],
        out_specs=[pl.BlockSpec((tb, c), lambda i: (i, 0)) for c, _ in outs],
        out_shape=[jax.ShapeDtypeStruct((t, c), dt) for c, dt in outs],
        compiler_params=_cparams(("parallel",)),
    )(*[_row_arr(r) for r in rows], *params)


def _rowwise_bwd(fn, rows, params, cots, row_grad, tb, name):
    t = rows[0].shape[0]
    nr, npar, nc = len(rows), len(params), len(cots)
    want = [i for i, g in enumerate(row_grad) if g is not None]

    def body(*refs):
        ins = [r[...].astype(F32) for r in refs[:nr + npar]]
        cts = tuple(r[...].astype(F32) for r in refs[nr + npar:nr + npar + nc])
        outs = refs[nr + npar + nc:]
        _, vjp = jax.vjp(fn, *ins)
        grads = vjp(cts)
        for o_ref, i in zip(outs[:len(want)], want):
            o_ref[...] = grads[i].astype(o_ref.dtype)
        first = pl.program_id(0) == 0
        for o_ref, g in zip(outs[len(want):], grads[nr:]):
            @pl.when(first)
            def _(o_ref=o_ref, g=g):
                o_ref[...] = g

            @pl.when(jnp.logical_not(first))
            def _(o_ref=o_ref, g=g):
                o_ref[...] += g

    res = pl.pallas_call(
        body, name=name, grid=(t // tb,),
        in_specs=[_row_spec(r, tb) for r in rows] + [pl.BlockSpec(p.shape, lambda i: (0, 0)) for p in params]
        + [_row_spec(c, tb) for c in cots],
        out_specs=[pl.BlockSpec((tb, rows[i].shape[1]), lambda i_: (i_, 0)) for i in want]
        + [pl.BlockSpec(p.shape, lambda i: (0, 0)) for p in params],
        out_shape=[jax.ShapeDtypeStruct(tuple(rows[i].shape), row_grad[i]) for i in want]
        + [jax.ShapeDtypeStruct(p.shape, F32) for p in params],
        compiler_params=_cparams(("arbitrary",)),
    )(*[_row_arr(r) for r in rows], *params, *[_row_arr(c) for c in cots])
    return list(res[:len(want)]), list(res[len(want):])


def _rms(x, g):
    return x * lax.rsqrt(jnp.mean(x * x, axis=-1, keepdims=True) + EPS) * g


def _fn_pre(x, g):
    return (_rms(x, g),)


def _fn_res_pre(x, y, g_post, g_pre):
    x1 = x + _rms(y, g_post)
    return x1, _rms(x1, g_pre)


def _fn_res(x, y, g_post):
    return (x + _rms(y, g_post),)


def _sigmoid(x):
    return 1.0 / (1.0 + jnp.exp(-x))


def _silu(x):
    return x * _sigmoid(x)


def _fn_swiglu(gu):
    return (_silu(gu[:, :D_FF]) * gu[:, D_FF:],)


def _fn_combine(o, lse):
    ls = [lse[:, 256 * g:256 * (g + 1)] for g in range(3)]
    mx = lax.stop_gradient(jnp.maximum(jnp.maximum(ls[0], ls[1]), ls[2]))
    es = [jnp.exp(l - mx) for l in ls]
    inv = 1.0 / (es[0] + es[1] + es[2])
    return (jnp.concatenate([o[:, 256 * g:256 * (g + 1)] * (es[g] * inv) for g in range(3)], axis=1),)


def _fn_outnorm(o_f, o_r, z, gain):
    res = []
    for h in range(DN_HEADS):
        sl = slice(DN_HD * h, DN_HD * (h + 1))
        o = o_f[:, sl] + o_r[:, sl]
        res.append(o * lax.rsqrt(jnp.mean(o * o, axis=-1, keepdims=True) + EPS) * gain * _silu(z[:, sl]))
    return (jnp.concatenate(res, axis=1),)


def _loss_kernel(x, tgt, tb, name):
    t, d = x.shape

    def body(x_ref, t_ref, dx_ref, l_ref, acc_ref):
        i = pl.program_id(0)
        e = x_ref[...] - t_ref[...]
        dx_ref[...] = e * (1.0 / d)
        part = jnp.sum(e * e, axis=0, keepdims=True)

        @pl.when(i == 0)
        def _():
            acc_ref[...] = part

        @pl.when(i > 0)
        def _():
            acc_ref[...] += part

        @pl.when(i == t // tb - 1)
        def _():
            l_ref[...] = jnp.broadcast_to(jnp.sum(acc_ref[...], axis=-1, keepdims=True), (1, 128))

    return pl.pallas_call(
        body, name=name, grid=(t // tb,),
        in_specs=[pl.BlockSpec((tb, d), lambda i: (i, 0))] * 2,
        out_specs=[pl.BlockSpec((tb, d), lambda i: (i, 0)), pl.BlockSpec((1, 128), lambda i: (0, 0))],
        out_shape=[jax.ShapeDtypeStruct((t, d), F32), jax.ShapeDtypeStruct((1, 128), F32)],
        scratch_shapes=[pltpu.VMEM((1, d), F32)],
        compiler_params=_cparams(("arbitrary",)),
    )(x, tgt)


def _band_fn(l_sub, bq, i, q, kw, vw, bm):
    w = bq + 2 * BAND_HALF
    s = lax.dot_general((q * (ATT_HD ** -0.5)).astype(BF16), kw.astype(BF16), (((2,), (2,)), ((0,), (0,))),
                        preferred_element_type=F32) + bm
    kpos = i * bq - BAND_HALF + lax.broadcasted_iota(jnp.int32, (4, bq, w), 2)
    s = jnp.where((kpos >= 0) & (kpos < l_sub), s, NEG)
    m = lax.stop_gradient(jnp.max(s, axis=-1, keepdims=True))
    p = jnp.exp(s - m)
    den = jnp.sum(p, axis=-1, keepdims=True)
    o = lax.dot_general(p.astype(BF16), vw.astype(BF16), (((2,), (1,)), ((0,), (0,))),
                        preferred_element_type=F32) / den
    return o, jnp.broadcast_to(m + jnp.log(den), o.shape)


def _band_specs(l_sub, bq):
    w = bq + 2 * BAND_HALF
    qs = pl.BlockSpec((None, 4, bq, ATT_HD), lambda r, i: (r, 0, i, 0))
    ks = pl.BlockSpec((None, 4, l_sub + 2 * BAND_HALF, ATT_HD), lambda r, i: (r, 0, 0, 0))
    bs = pl.BlockSpec((4, bq, w), lambda r, i: (0, 0, 0))
    return qs, ks, bs


def _band_fwd(q, k, v, bm, dil, l_sub, bq, name):
    w = bq + 2 * BAND_HALF
    qs, ks, bs = _band_specs(l_sub, bq)

    def body(q_ref, k_ref, v_ref, bm_ref, o_ref, l_ref):
        i = pl.program_id(1)
        st = pl.multiple_of(i * bq, bq)
        o, lse = _band_fn(l_sub, bq, i, q_ref[...].astype(F32), k_ref[:, pl.ds(st, w), :].astype(F32),
                          v_ref[:, pl.ds(st, w), :].astype(F32), bm_ref[...])
        o_ref[...] = o
        l_ref[...] = lse

    return pl.pallas_call(
        body, name=name, grid=(dil, l_sub // bq),
        in_specs=[qs, ks, ks, bs], out_specs=[qs, qs],
        out_shape=[jax.ShapeDtypeStruct(q.shape, F32)] * 2,
        compiler_params=_cparams(("parallel", "arbitrary")),
    )(q, k, v, bm)


def _band_bwd(q, k, v, bm, do, dlse, dil, l_sub, bq, name):
    w = bq + 2 * BAND_HALF
    qs, ks, bs = _band_specs(l_sub, bq)

    def body(q_ref, k_ref, v_ref, bm_ref, do_ref, dl_ref, dq_ref, dk_ref, dv_ref, dbm_ref):
        r, i = pl.program_id(0), pl.program_id(1)
        st = pl.multiple_of(i * bq, bq)
        _, vjp = jax.vjp(functools.partial(_band_fn, l_sub, bq, i),
                         q_ref[...].astype(F32), k_ref[:, pl.ds(st, w), :].astype(F32),
                         v_ref[:, pl.ds(st, w), :].astype(F32), bm_ref[...])
        dq, dkw, dvw, dbm = vjp((do_ref[...].astype(F32), dl_ref[...]))
        dq_ref[...] = dq.astype(dq_ref.dtype)

        @pl.when(i == 0)
        def _():
            dk_ref[...] = jnp.zeros_like(dk_ref)
            dv_ref[...] = jnp.zeros_like(dv_ref)

        dk_ref[:, pl.ds(st, w), :] += dkw
        dv_ref[:, pl.ds(st, w), :] += dvw

        @pl.when((i == 0) & (r == 0))
        def _():
            dbm_ref[...] = dbm

        @pl.when((i > 0) | (r > 0))
        def _():
            dbm_ref[...] += dbm

    return pl.pallas_call(
        body, name=name, grid=(dil, l_sub // bq),
        in_specs=[qs, ks, ks, bs, qs, qs], out_specs=[qs, ks, ks, bs],
        out_shape=[jax.ShapeDtypeStruct(q.shape, BF16), jax.ShapeDtypeStruct(k.shape, F32),
                   jax.ShapeDtypeStruct(k.shape, F32), jax.ShapeDtypeStruct(bm.shape, F32)],
        compiler_params=_cparams(("arbitrary", "arbitrary")),
    )(q, k, v, bm, do, dlse)


def _t5_bucket(rel):
    half = REL_BUCKETS // 2
    max_exact = half // 2
    n = np.abs(rel)
    large = max_exact + (np.log(np.maximum(n, 1) / max_exact) / math.log(REL_MAX_DIST / max_exact)
                         * (half - max_exact)).astype(np.int64)
    large = np.minimum(large, half - 1)
    return ((rel > 0) * half + np.where(n < max_exact, n, large)).astype(np.int32)


def _bucket_onehot(dil):
    idx = _t5_bucket(np.arange(-BAND_HALF, BAND_HALF + 1) * dil)
    oh = np.zeros((2 * BAND_HALF + 1, REL_BUCKETS), np.float32)
    oh[np.arange(2 * BAND_HALF + 1), idx] = 1.0
    return oh


def _band_bias(rel_bias, gi, dil, bq):
    w = bq + 2 * BAND_HALF
    nb = 2 * BAND_HALF + 1
    bias = jnp.dot(jnp.asarray(_bucket_onehot(dil)), rel_bias[:, 4 * gi:4 * gi + 4], precision=HI)
    row = jnp.concatenate([bias.T, jnp.full((4, w + 1 - nb), NEG, F32)], axis=1)
    flat = jnp.tile(row, (1, bq))[:, :bq * w]
    return flat.reshape(4, bq, w)


def _relbias_grad(dbms, name):
    nb = 2 * BAND_HALF + 1
    bq = max(d.shape[1] for d in dbms)
    skew = []
    for dbm in dbms:
        bqg, w = dbm.shape[1], dbm.shape[2]
        flat = jnp.pad(dbm.reshape(4, bqg * w), ((0, 0), (0, bqg)))
        skew.append(jnp.pad(flat.reshape(4, bqg, w + 1)[:, :, :nb], ((0, 0), (0, bq - bqg), (0, 256 - nb))))
    sk = jnp.concatenate(skew, axis=0)
    oh = np.zeros((3, 256, 128), np.float32)
    for gi, (_, dil) in enumerate(DIL_GROUPS):
        oh[gi, :2 * BAND_HALF + 1, :REL_BUCKETS] = _bucket_onehot(dil)

    def body(s_ref, oh_ref, o_ref):
        col = jnp.sum(s_ref[...], axis=0, keepdims=True)
        o_ref[...] = jnp.dot(jnp.broadcast_to(col, (8, 256)), oh_ref[...], precision=HI, preferred_element_type=F32)

    out = pl.pallas_call(
        body, name=name, grid=(12,),
        in_specs=[pl.BlockSpec((None, bq, 256), lambda n: (n, 0, 0)),
                  pl.BlockSpec((None, 256, 128), lambda n: (n // 4, 0, 0))],
        out_specs=pl.BlockSpec((None, 8, 128), lambda n: (n, 0, 0)),
        out_shape=jax.ShapeDtypeStruct((12, 8, 128), F32),
        compiler_params=_cparams(("parallel",)),
    )(sk, jnp.asarray(oh))
    return out[:, 0, :REL_BUCKETS].T


def _mem_fn(q, k, v):
    s = lax.dot_general((q * (ATT_HD ** -0.5)).astype(BF16), k.astype(BF16), (((2,), (2,)), ((0,), (0,))),
                        preferred_element_type=F32)
    m = lax.stop_gradient(jnp.max(s, axis=-1, keepdims=True))
    p = jnp.exp(s - m)
    p = p / jnp.sum(p, axis=-1, keepdims=True)
    return lax.dot_general(p.astype(BF16), v.astype(BF16), (((2,), (1,)), ((0,), (0,))), preferred_element_type=F32)


def _mem_specs(tb, ml):
    qs = pl.BlockSpec((MEM_HEADS, tb, ATT_HD), lambda i: (0, i, 0))
    ks = pl.BlockSpec((MEM_HEADS, ml, ATT_HD), lambda i: (0, 0, 0))
    return qs, ks


def _mem_fwd(q, k, v, tb, name):
    qs, ks = _mem_specs(tb, k.shape[1])

    def body(q_ref, k_ref, v_ref, o_ref):
        o_ref[...] = _mem_fn(q_ref[...].astype(F32), k_ref[...], v_ref[...])

    return pl.pallas_call(
        body, name=name, grid=(q.shape[1] // tb,),
        in_specs=[qs, ks, ks], out_specs=qs, out_shape=jax.ShapeDtypeStruct(q.shape, F32),
        compiler_params=_cparams(("parallel",)),
    )(q, k, v)


def _mem_bwd(q, k, v, do, tb, name):
    qs, ks = _mem_specs(tb, k.shape[1])

    def body(q_ref, k_ref, v_ref, do_ref, dq_ref, dk_ref, dv_ref):
        i = pl.program_id(0)
        _, vjp = jax.vjp(_mem_fn, q_ref[...].astype(F32), k_ref[...], v_ref[...])
        dq, dk, dv = vjp(do_ref[...])
        dq_ref[...] = dq

        @pl.when(i == 0)
        def _():
            dk_ref[...] = dk
            dv_ref[...] = dv

        @pl.when(i > 0)
        def _():
            dk_ref[...] += dk
            dv_ref[...] += dv

    return pl.pallas_call(
        body, name=name, grid=(q.shape[1] // tb,),
        in_specs=[qs, ks, ks, qs], out_specs=[qs, ks, ks],
        out_shape=[jax.ShapeDtypeStruct(q.shape, F32), jax.ShapeDtypeStruct(k.shape, F32),
                   jax.ShapeDtypeStruct(k.shape, F32)],
        compiler_params=_cparams(("arbitrary",)),
    )(q, k, v, do)


CONV_PAD = 8


def _conv_post(kind, acc):
    s = _silu(acc)
    if kind == 2:
        return s
    scale = DN_HD ** -0.5 if kind == 0 else 1.0
    return s * lax.rsqrt(jnp.sum(s * s, axis=-1, keepdims=True) + EPS) * scale


def _conv_rows(x_ref, t, start, rt):
    lo = min(max(start, 0), t - rt)
    x = x_ref[pl.ds(lo, rt), :]
    shift = lo - start
    if shift == 0:
        return x
    x = pltpu.roll(x, shift % rt, axis=0)
    row = lax.broadcasted_iota(jnp.int32, x.shape, 0)
    return jnp.where((row >= shift) if shift > 0 else (row < rt + shift), x, 0.0)


def _conv_acc(x_ref, t, w, r0, rt):
    acc = None
    for i in range(DN_CONV):
        term = w[i:i + 1, :] * _conv_rows(x_ref, t, r0 + i - DN_CONV // 2, rt)
        acc = term if acc is None else acc + term
    return acc


def _conv_fwd(x, w8, kind, rt, name):
    t = x.shape[0]

    def body(x_ref, w_ref, o_ref):
        w = w_ref[...]
        for r in range(t // rt):
            o_ref[pl.ds(r * rt, rt), :] = _conv_post(kind, _conv_acc(x_ref, t, w, r * rt, rt))

    return pl.pallas_call(
        body, name=name, grid=(DN_HEADS,),
        in_specs=[pl.BlockSpec((t, DN_HD), lambda j: (0, 6 * kind + j)),
                  pl.BlockSpec((8, DN_HD), lambda j: (0, 6 * kind + j))],
        out_specs=pl.BlockSpec((t, DN_HD), lambda j: (0, j)),
        out_shape=jax.ShapeDtypeStruct((t, TOK_W), F32),
        compiler_params=_cparams(("parallel",)),
    )(x, w8)


def _conv_bwd(x, w8, d_f, d_r, dp, kind, rt, name):
    t = x.shape[0]

    def body(xp_ref, w_ref, df_ref, dr_ref, dp_in, dx_ref, dw_ref, dpad_ref):
        del dp_in
        w = w_ref[...]
        zero = jnp.zeros((CONV_PAD, DN_HD), F32)
        dpad_ref[pl.ds(0, CONV_PAD), :] = zero
        dpad_ref[pl.ds(CONV_PAD + t, CONV_PAD), :] = zero
        dw = [jnp.zeros((1, DN_HD), F32) for _ in range(DN_CONV)]
        for r in range(t // rt):
            rows = pl.ds(r * rt, rt)
            acc = _conv_acc(xp_ref, t, w, r * rt, rt)
            _, vjp = jax.vjp(functools.partial(_conv_post, kind), acc)
            (dacc,) = vjp(df_ref[rows, :] + dr_ref[rows, :])
            dpad_ref[pl.ds(CONV_PAD + r * rt, rt), :] = dacc
            for i in range(DN_CONV):
                xs = _conv_rows(xp_ref, t, r * rt + i - DN_CONV // 2, rt)
                dw[i] = dw[i] + jnp.sum(dacc * xs, axis=0, keepdims=True)
        dw_ref[...] = jnp.concatenate(dw + [jnp.zeros((8 - DN_CONV, DN_HD), F32)], axis=0)
        for r in range(t // rt):
            acc = None
            for i in range(DN_CONV):
                term = w[i:i + 1, :] * dpad_ref[pl.ds(CONV_PAD + r * rt - i + DN_CONV // 2, rt), :]
                acc = term if acc is None else acc + term
            dx_ref[pl.ds(r * rt, rt), :] = acc.astype(dx_ref.dtype)

    return pl.pallas_call(
        body, name=name, grid=(DN_HEADS,),
        in_specs=[pl.BlockSpec((t, DN_HD), lambda j: (0, 6 * kind + j)),
                  pl.BlockSpec((8, DN_HD), lambda j: (0, 6 * kind + j)),
                  pl.BlockSpec((t, DN_HD), lambda j: (0, j)),
                  pl.BlockSpec((t, DN_HD), lambda j: (0, j)),
                  pl.BlockSpec(memory_space=pl.ANY)],
        out_specs=[pl.BlockSpec((t, DN_HD), lambda j: (0, 6 * kind + j)),
                   pl.BlockSpec((8, DN_HD), lambda j: (0, j))],
        out_shape=[jax.ShapeDtypeStruct(dp.shape, dp.dtype), jax.ShapeDtypeStruct((8, TOK_W), F32)],
        input_output_aliases={4: 0},
        scratch_shapes=[pltpu.VMEM((t + 2 * CONV_PAD, DN_HD), F32)],
        compiler_params=_cparams(("parallel",)),
    )(x, w8, d_f, d_r, dp)


def _softplus(x):
    e = jnp.exp(-jnp.abs(x))
    return jnp.maximum(x, 0.0) + jnp.where(e < 1e-4, e - 0.5 * e * e, jnp.log(1.0 + e))


_NN = (((2,), (1,)), ((0,), (0,)))
_NT = (((2,), (2,)), ((0,), (0,)))
_TN = (((1,), (1,)), ((0,), (0,)))


def _dot(a, b, dims=_NN):
    return lax.dot_general(a.astype(BF16), b.astype(BF16), dims, preferred_element_type=F32)


def _hi_lo(x):
    hi = x.astype(BF16)
    return hi, (x - hi.astype(F32)).astype(BF16)


def _mask_dot(mask_bf16, x, dims):
    x1 = x.astype(BF16)
    r = x - x1.astype(F32)
    x2, x3 = _hi_lo(r)
    d = functools.partial(lax.dot_general, dimension_numbers=dims, preferred_element_type=F32)
    return d(mask_bf16, x1) + d(mask_bf16, x2) + d(mask_bf16, x3)


@jax.custom_vjp
def _dot_mask(mask_bf16, x):
    return _mask_dot(mask_bf16, x, _NN)


def _dot_mask_fwd(mask_bf16, x):
    return _mask_dot(mask_bf16, x, _NN), mask_bf16


def _dot_mask_bwd(mask_bf16, ct):
    return jnp.zeros_like(mask_bf16), _mask_dot(mask_bf16, ct, _TN)


_dot_mask.defvjp(_dot_mask_fwd, _dot_mask_bwd)


def _unit_solve_pass(lmat, rhs, masks):
    ainv = masks[6] - lmat * masks[0]
    for sh in range(1, 6):
        ainv = ainv - _dot(_dot(ainv, lmat * masks[sh]), ainv)
    return _dot(ainv, rhs), ainv


@jax.custom_vjp
def _unit_solve(lmat, rhs, masks):
    return _unit_solve_pass(lmat, rhs, masks)[0]


def _unit_solve_fwd(lmat, rhs, masks):
    sol, ainv = _unit_solve_pass(lmat, rhs, masks)
    return sol, (sol, ainv, masks)


def _unit_solve_bwd(res, ct):
    sol, ainv, masks = res
    d_rhs = _dot(ainv, ct, _TN)
    return -_dot(d_rhs, sol, _NT), d_rhs, tuple(jnp.zeros_like(m) for m in masks)


_unit_solve.defvjp(_unit_solve_fwd, _unit_solve_bwd)


def _block_masks(rev, row, col):
    c = DN_CHUNK
    prow = jnp.where(rev, c - 1 - row, row)
    pcol = jnp.where(rev, c - 1 - col, col)
    masks = []
    for sh in range(6):
        differ = (prow ^ pcol) >> sh
        miss = (differ ^ 1) + (1 - ((prow >> sh) & 1))
        masks.append(jnp.where(miss == 0, 1.0, 0.0))
    masks.append(jnp.where(row == col, 1.0, 0.0))
    return tuple(masks)


def _dn_chunk(q, k, v, al, be, alc, a_row, dt_row, a_rowc, dt_rowc, s):
    n, c = q.shape[0], DN_CHUNK
    rev = lax.broadcasted_iota(jnp.int32, (n, c, c), 0) >= n // 2
    row = lax.broadcasted_iota(jnp.int32, (n, c, c), 1)
    col = lax.broadcasted_iota(jnp.int32, (n, c, c), 2)
    ahead = jnp.where(rev, col - row, row - col)
    incl = ahead >= 0
    strict = ahead > 0
    incl_b = incl.astype(BF16)

    g = -jnp.exp(a_row) * _softplus(al + dt_row)
    beta = _sigmoid(be)
    g_c = -jnp.exp(a_rowc) * _softplus(alc + dt_rowc)
    gc = _dot_mask(incl_b, g)
    gcc = _dot_mask(incl_b, g_c)
    decay = jnp.exp(jnp.where(incl, gcc - jnp.swapaxes(gcc, 1, 2), NEG))
    kb = k * beta
    lmat = jnp.where(strict, _dot(kb, k, _NT) * decay, 0.0)
    rhs = jnp.concatenate([v * beta, kb * jnp.exp(gc)], axis=2)
    sol = _unit_solve(lmat, rhs, _block_masks(rev, row, col))
    u, w = sol[:, :, :DN_HD], sol[:, :, DN_HD:]
    intra = jnp.where(incl, _dot(q, k, _NT) * decay, 0.0)
    v_new = u - _dot(w, s)
    out = _dot(q * jnp.exp(gc), s) + _dot(intra, v_new)
    g_last = jnp.sum(g, axis=1, keepdims=True)
    s_new = s * jnp.exp(g_last) + _dot(k * jnp.exp(g_last - gc), v_new, _TN)
    return out, s_new


DN_HG = 6


def _dn_load(f_refs, r_refs, alf, bef, alr, ber, a_ref, dt_ref):
    c, hg = DN_CHUNK, DN_HG
    sls = [slice(DN_HD * h, DN_HD * (h + 1)) for h in range(hg)]
    toks = [jnp.stack([f[:, sl] for sl in sls] + [r[:, sl] for sl in sls]) for f, r in zip(f_refs, r_refs)]
    al = jnp.concatenate([alf[...], alr[...]], axis=0)
    be = jnp.concatenate([bef[...], ber[...]], axis=0)
    alc = jnp.concatenate([alf[:, :, 0:c], alr[:, :, 0:c]], axis=0)
    a = jnp.concatenate([a_ref[0], a_ref[1]], axis=0)
    dt = jnp.concatenate([dt_ref[0], dt_ref[1]], axis=0)
    ac = jnp.concatenate([a_ref[0, :, :, 0:c], a_ref[1, :, :, 0:c]], axis=0)
    dtc = jnp.concatenate([dt_ref[0, :, :, 0:c], dt_ref[1, :, :, 0:c]], axis=0)
    return toks, (al, be, alc, a, dt, ac, dtc)


def _dn_views(nc, bwd):
    c, hg = DN_CHUNK, DN_HG
    if bwd:
        f_blk = lambda s: nc - 1 - s
        r_blk = lambda s: s
        st_blk = lambda s: nc - 1 - s
    else:
        f_blk = lambda s: s
        r_blk = lambda s: nc - 1 - s
        st_blk = lambda s: s
    tok_f = pl.BlockSpec((c, hg * DN_HD), lambda g, s: (f_blk(s), g))
    tok_r = pl.BlockSpec((c, hg * DN_HD), lambda g, s: (r_blk(s), g))
    gate_f = pl.BlockSpec((None, hg, c, DN_HD), lambda g, s: (0, g, f_blk(s), 0))
    gate_r = pl.BlockSpec((None, hg, c, DN_HD), lambda g, s: (1, g, r_blk(s), 0))
    par = pl.BlockSpec((2, hg, 1, DN_HD), lambda g, s: (0, g, 0, 0))
    state = pl.BlockSpec((2, hg, None, DN_HD, DN_HD), lambda g, s: (0, g, st_blk(s), 0, 0))
    return tok_f, tok_r, gate_f, gate_r, par, state


def _dn_fwd(q, k, v, al, be, a_rows, dt_rows, name):
    t = q.shape[0]
    c, hg = DN_CHUNK, DN_HG
    nc = t // c
    tok_f, tok_r, gate_f, gate_r, par, state = _dn_views(nc, False)

    def body(qf, kf, vf, qr, kr, vr, alf, bef, alr, ber, a_ref, dt_ref, of_ref, or_ref, st_ref, s_ref):
        @pl.when(pl.program_id(1) == 0)
        def _():
            s_ref[...] = jnp.zeros_like(s_ref)

        (q_, k_, v_), gates = _dn_load((qf, kf, vf), (qr, kr, vr), alf, bef, alr, ber, a_ref, dt_ref)
        s = s_ref[...]
        st_ref[0] = s[:hg]
        st_ref[1] = s[hg:]
        out, s_new = _dn_chunk(q_, k_, v_, *gates, s)
        for h in range(hg):
            sl = slice(DN_HD * h, DN_HD * (h + 1))
            of_ref[:, sl] = out[h]
            or_ref[:, sl] = out[hg + h]
        s_ref[...] = s_new

    return pl.pallas_call(
        body, name=name, grid=(DN_HEADS // hg, nc),
        in_specs=[tok_f] * 3 + [tok_r] * 3 + [gate_f, gate_f, gate_r, gate_r, par, par],
        out_specs=[tok_f, tok_r, state],
        out_shape=[jax.ShapeDtypeStruct((t, TOK_W), F32)] * 2
        + [jax.ShapeDtypeStruct((2, DN_HEADS, nc, DN_HD, DN_HD), F32)],
        scratch_shapes=[pltpu.VMEM((2 * hg, DN_HD, DN_HD), F32)],
        compiler_params=_cparams(("parallel", "arbitrary")),
    )(q, k, v, q, k, v, al, be, al, be, a_rows, dt_rows)


def _dn_bwd(q, k, v, al, be, a_rows, dt_rows, states, do, name):
    t = q.shape[0]
    c, hg = DN_CHUNK, DN_HG
    assert hg == DN_HEADS
    nc = t // c
    tok_f, tok_r, gate_f, gate_r, par, state = _dn_views(nc, True)
    gout_f = pl.BlockSpec((c, DN_HD), lambda g, s: (nc - 1 - s, 0))
    gout_r = pl.BlockSpec((c, DN_HD), lambda g, s: (s, 0))

    def body(qf, kf, vf, qr, kr, vr, alf, bef, alr, ber, a_ref, dt_ref, st_ref, dof, dor,
             dqf, dkf, dvf, dqr, dkr, dvr, dgf, dgr, da_ref, ddt_ref, ds_ref):
        first = pl.program_id(1) == 0

        @pl.when(first)
        def _():
            ds_ref[...] = jnp.zeros_like(ds_ref)
            da_ref[...] = jnp.zeros_like(da_ref)
            ddt_ref[...] = jnp.zeros_like(ddt_ref)

        def lanes(x):
            return jnp.sum(x, axis=-1, keepdims=True)

        (q_, k_, v_, do_), gates = _dn_load((qf, kf, vf, dof), (qr, kr, vr, dor), alf, bef, alr, ber, a_ref, dt_ref)
        s = jnp.concatenate([st_ref[0], st_ref[1]], axis=0)
        _, vjp = jax.vjp(_dn_chunk, q_, k_, v_, *gates, s)
        dq, dk, dv, dal, dbe, dalc, da, ddt, dac, ddtc, ds = vjp((do_, ds_ref[...]))
        for h in range(hg):
            sl = slice(DN_HD * h, DN_HD * (h + 1))
            dqf[:, sl], dkf[:, sl], dvf[:, sl] = dq[h], dk[h], dv[h]
            dqr[:, sl], dkr[:, sl], dvr[:, sl] = dq[hg + h], dk[hg + h], dv[hg + h]
        dal, dbe = lanes(dal) + lanes(dalc), lanes(dbe)
        lane = lax.broadcasted_iota(jnp.int32, (c, DN_HD), 1)
        for d, dg_ref in enumerate((dgf, dgr)):
            dg = jnp.zeros((c, DN_HD), F32)
            for h in range(hg):
                dg = jnp.where(lane == h, dal[d * hg + h], jnp.where(lane == hg + h, dbe[d * hg + h], dg))
            dg_ref[...] = dg
        da = jnp.broadcast_to(lanes(da) + lanes(dac), da.shape)
        ddt = jnp.broadcast_to(lanes(ddt) + lanes(ddtc), ddt.shape)
        da_ref[0] += da[:hg]
        da_ref[1] += da[hg:]
        ddt_ref[0] += ddt[:hg]
        ddt_ref[1] += ddt[hg:]
        ds_ref[...] = ds

    tok = jax.ShapeDtypeStruct((t, TOK_W), F32)
    gate = jax.ShapeDtypeStruct((t, DN_HD), F32)
    parsh = jax.ShapeDtypeStruct((2, DN_HEADS, 1, DN_HD), F32)
    res = pl.pallas_call(
        body, name=name, grid=(DN_HEADS // hg, nc),
        in_specs=[tok_f] * 3 + [tok_r] * 3 + [gate_f, gate_f, gate_r, gate_r, par, par, state, tok_f, tok_r],
        out_specs=[tok_f] * 3 + [tok_r] * 3 + [gout_f, gout_r, par, par],
        out_shape=[tok] * 6 + [gate] * 2 + [parsh] * 2,
        scratch_shapes=[pltpu.VMEM((2 * hg, DN_HD, DN_HD), F32)],
        compiler_params=_cparams(("parallel", "arbitrary")),
    )(q, k, v, q, k, v, al, be, al, be, a_rows, dt_rows, states, do, do)
    dqf, dkf, dvf, dqr, dkr, dvr, dgf, dgr, da, ddt = res
    dgate = jnp.concatenate([dgf[:, :2 * DN_HEADS], dgr[:, :2 * DN_HEADS]], axis=1)
    return (dqf, dkf, dvf), (dqr, dkr, dvr), dgate, da, ddt


BAND_BQ = 256
ROW_TB = 512
MEM_TB = 512
CONV_RT = 512


def _to_sub(x, dil):
    l = x.shape[0] // dil
    return x.reshape(l, dil, 4, ATT_HD).transpose(1, 2, 0, 3)


def _from_sub(x, dil):
    l = x.shape[2]
    return x.transpose(2, 0, 1, 3).reshape(l * dil, 4 * ATT_HD)


def _sub_in(x, col_block, dil, pad, dtype, name):
    if dil > 1:
        y = _to_sub(x[:, 256 * col_block:256 * (col_block + 1)], dil).astype(dtype)
        return jnp.pad(y, ((0, 0), (0, 0), (pad, pad), (0, 0))) if pad else y
    t = x.shape[0]

    def body(x_ref, o_ref):
        if pad:
            zero = jnp.zeros((4, pad, ATT_HD), dtype)
            o_ref[0, :, 0:pad, :] = zero
            o_ref[0, :, pad + t:pad + t + pad, :] = zero
        for h in range(4):
            o_ref[0, h, pad:pad + t, :] = x_ref[:, ATT_HD * h:ATT_HD * (h + 1)].astype(dtype)

    return pl.pallas_call(
        body, name=name, grid=(1,), in_specs=[pl.BlockSpec((t, 256), lambda i: (0, col_block))],
        out_specs=pl.BlockSpec((1, 4, t + 2 * pad, ATT_HD), lambda i: (0, 0, 0, 0)),
        out_shape=jax.ShapeDtypeStruct((1, 4, t + 2 * pad, ATT_HD), dtype), compiler_params=_cparams(("arbitrary",)),
    )(x)


def _sub_out(x, dil, pad, dtype, name):
    if dil > 1:
        return _from_sub(x[:, :, pad:x.shape[2] - pad] if pad else x, dil).astype(dtype)
    t = x.shape[2] - 2 * pad

    def body(x_ref, o_ref):
        for h in range(4):
            o_ref[:, ATT_HD * h:ATT_HD * (h + 1)] = x_ref[0, h, pad:pad + t, :].astype(dtype)

    return pl.pallas_call(
        body, name=name, grid=(1,), in_specs=[pl.BlockSpec(x.shape, lambda i: (0, 0, 0, 0))],
        out_specs=pl.BlockSpec((t, 256), lambda i: (0, 0)), out_shape=jax.ShapeDtypeStruct((t, 256), dtype),
        compiler_params=_cparams(("arbitrary",)),
    )(x)


def _heads_major(x):
    return x.reshape(x.shape[0], MEM_HEADS, ATT_HD).transpose(1, 0, 2)


def _heads_minor(x):
    return x.transpose(1, 0, 2).reshape(x.shape[1], MEM_HEADS * ATT_HD)


def _mem_kv_fwd(mem, gain, w_kv, li):
    (memn,) = _rowwise(_fn_pre, [mem], [gain], [(D, BF16)], mem.shape[0], f"memnorm_fwd{li}")
    kv = _matmul(memn, w_kv, "nn", F32, f"memkv_fwd{li}")
    return _heads_major(kv[:, :MEM_W]), _heads_major(kv[:, MEM_W:]), memn


def _mem_kv_bwd(mem, gain, w_kv, memn, dkm, dvm, li):
    dkv = jnp.concatenate([_heads_minor(dkm), _heads_minor(dvm)], axis=1).astype(BF16)
    dw = _matmul(memn, dkv, "tn", BF16, f"memkv_dw{li}")
    dmemn = _matmul(dkv, w_kv, "nt", F32, f"memkv_dx{li}")
    _, (dgain,) = _rowwise_bwd(_fn_pre, [mem], [gain], [dmemn], [None], mem.shape[0], f"memnorm_bwd{li}")
    return dw, dgain


def _attn_mixer_fwd(p, rel_bias, kv_fn):
    t = p.shape[0]
    saved, outs, lses = [], [], []
    for gi, (_, dil) in enumerate(DIL_GROUPS):
        l_sub = t // dil
        bq = min(BAND_BQ, l_sub)
        q = _sub_in(p, gi, dil, 0, p.dtype, f"sub_q{gi}")
        k = _sub_in(p, 3 + gi, dil, BAND_HALF, p.dtype, f"sub_k{gi}")
        v = _sub_in(p, 6 + gi, dil, BAND_HALF, p.dtype, f"sub_v{gi}")
        bm = _band_bias(rel_bias, gi, dil, bq)
        o, lse = _band_fwd(q, k, v, bm, dil, l_sub, bq, f"band_fwd{gi}")
        outs.append(_sub_out(o, dil, 0, F32, f"sub_o{gi}"))
        lses.append(_sub_out(lse, dil, 0, F32, f"sub_lse{gi}"))
        saved.append((q, k, v, bm))
    o_all = jnp.concatenate(outs, axis=1)
    lse_all = jnp.concatenate(lses, axis=1)
    (mixed,) = _rowwise(_fn_combine, [o_all, lse_all], [], [(TOK_W, BF16)], ROW_TB, "combine_fwd")
    qm = _heads_major(p[:, 3 * TOK_W:])
    km, vm, memn = kv_fn(mixed)
    memo = _mem_fwd(qm, km, vm, min(MEM_TB, t), "mem_fwd0")
    cat = jnp.concatenate([mixed, _heads_minor(memo).astype(BF16)], axis=1)
    return cat, (saved, o_all, lse_all, qm), (km, vm, memn)


def _attn_mixer_bwd(dcat, res, km, vm):
    saved, o_all, lse_all, qm = res
    t = dcat.shape[0]
    (do_all, dlse_all), _ = _rowwise_bwd(_fn_combine, [o_all, lse_all], [], [_Cols(dcat, TOK_W, 0)], [BF16, F32],
                                         ROW_TB, "combine_bwd")
    dqs, dks, dvs, dbms = [], [], [], []
    for gi, (_, dil) in enumerate(DIL_GROUPS):
        l_sub = t // dil
        bq = min(BAND_BQ, l_sub)
        q, k, v, bm = saved[gi]
        do = _sub_in(do_all, gi, dil, 0, do_all.dtype, f"sub_do{gi}")
        dl = _sub_in(dlse_all, gi, dil, 0, F32, f"sub_dlse{gi}")
        dq, dk, dv, dbm = _band_bwd(q, k, v, bm, do, dl, dil, l_sub, bq, f"band_bwd{gi}")
        dqs.append(_sub_out(dq, dil, 0, BF16, f"sub_dq{gi}"))
        dks.append(_sub_out(dk, dil, BAND_HALF, BF16, f"sub_dk{gi}"))
        dvs.append(_sub_out(dv, dil, BAND_HALF, BF16, f"sub_dv{gi}"))
        dbms.append(dbm)
    dqm, dkm, dvm = _mem_bwd(qm, km, vm, _heads_major(dcat[:, TOK_W:]), min(MEM_TB, t), "mem_bwd0")
    dp = jnp.concatenate(dqs + dks + dvs + [_heads_minor(dqm).astype(BF16)], axis=1)
    return dp, _relbias_grad(dbms, "relbias_grad"), dkm, dvm


def _dn_mixer_fwd(p, conv_w, a_log, dt_bias, out_norm, km, vm):
    t = p.shape[0]
    rt = min(CONV_RT, t)
    xp = p
    w8 = jnp.pad(conv_w, ((0, 8 - DN_CONV), (0, 0)))
    q = _conv_fwd(xp, w8, 0, rt, "conv_fwd_q")
    k = _conv_fwd(xp, w8, 1, rt, "conv_fwd_k")
    v = _conv_fwd(xp, w8, 2, rt, "conv_fwd_v")
    gate = p[:, 4 * TOK_W:4 * TOK_W + 4 * DN_HEADS].reshape(t, 2, 2, DN_HEADS)
    bshape = (2, DN_HEADS, t, DN_HD)
    al = jnp.broadcast_to(gate[:, :, 0, :].transpose(1, 2, 0)[..., None], bshape)
    be = jnp.broadcast_to(gate[:, :, 1, :].transpose(1, 2, 0)[..., None], bshape)
    a_rows = jnp.broadcast_to(a_log[:, :, None, None], (2, DN_HEADS, 1, DN_HD))
    dt_rows = jnp.broadcast_to(dt_bias[:, :, None, None], (2, DN_HEADS, 1, DN_HD))
    o_f, o_r, states = _dn_fwd(q, k, v, al, be, a_rows, dt_rows, "dn_fwd")
    gain = out_norm.reshape(1, DN_HD)
    (og,) = _rowwise(_fn_outnorm, [o_f, o_r, _Cols(p, TOK_W, 3)], [gain], [(TOK_W, BF16)], ROW_TB, "outnorm_fwd")
    qm = _heads_major(p[:, 4 * TOK_W + 4 * DN_HEADS:DN_IN])
    memo = _mem_fwd(qm, km, vm, min(MEM_TB, t), "mem_fwd1")
    cat = jnp.concatenate([og, _heads_minor(memo).astype(BF16)], axis=1)
    return cat, (xp, w8, q, k, v, al, be, a_rows, dt_rows, o_f, o_r, states, gain, qm)


def _dn_mixer_bwd(dcat, res, km, vm):
    xp, w8, q, k, v, al, be, a_rows, dt_rows, o_f, o_r, states, gain, qm = res
    t = dcat.shape[0]
    rt = min(CONV_RT, t)
    (do, dz), (dgain,) = _rowwise_bwd(_fn_outnorm, [o_f, o_r, _Cols(xp, TOK_W, 3)], [gain],
                                      [_Cols(dcat, TOK_W, 0)], [F32, None, BF16],
                                      ROW_TB, "outnorm_bwd")
    d_f, d_r, dgate, da, ddt = _dn_bwd(q, k, v, al, be, a_rows, dt_rows, states, do, "dn_bwd")
    dqm, dkm, dvm = _mem_bwd(qm, km, vm, _heads_major(dcat[:, TOK_W:]), min(MEM_TB, t), "mem_bwd1")
    rest = jnp.concatenate([dgate.astype(BF16), _heads_minor(dqm).astype(BF16),
                            jnp.zeros((t, DN_IN_PAD - DN_IN), BF16)], axis=1)
    dp = lax.dynamic_update_slice(lax.empty((t, DN_IN_PAD), BF16), dz, (0, 3 * TOK_W))
    dp = lax.dynamic_update_slice(dp, rest, (0, 4 * TOK_W))
    dws = []
    for kind, nm in enumerate("qkv"):
        dp, dw = _conv_bwd(xp, w8, d_f[kind], d_r[kind], dp, kind, rt, f"conv_bwd_{nm}")
        dws.append(dw)
    dconv = jnp.concatenate(dws, axis=1)[:DN_CONV]
    return dp, dconv, da[:, :, 0, 0], ddt[:, :, 0, 0], dgain.reshape(DN_HD), dkm, dvm


SWI_TB = 256


def _ffn_fwd(h, w_gu_t, w_d, li):
    gu = _matmul(h, w_gu_t, "nt", BF16, f"ffn_gu{li}")
    (a,) = _rowwise(_fn_swiglu, [gu], [], [(D_FF, BF16)], SWI_TB, f"swiglu_fwd{li}")
    return _matmul(a, w_d, "nn", F32, f"ffn_down{li}"), gu, a


def _ffn_bwd(df, h, w_gu_t, w_d, gu, a, li):
    da = _matmul(df, w_d, "nt", BF16, f"ffn_down_dx{li}")
    dwd = _matmul(a, df, "tn", BF16, f"ffn_down_dw{li}")
    (dgu,), _ = _rowwise_bwd(_fn_swiglu, [gu], [], [da], [BF16], SWI_TB, f"swiglu_bwd{li}")
    dh = _matmul(dgu, w_gu_t, "nn", F32, f"ffn_gu_dx{li}")
    dwgu_t = _matmul(dgu, h, "tn", BF16, f"ffn_gu_dw{li}")
    return dh, dwgu_t, dwd


def _fn_first(x, g):
    return x, _rms(x, g)


def _me_xyc():
    return lax.axis_index("x"), lax.axis_index("y"), lax.axis_index("c")


def _flip(coords, k):
    x, y, c = coords
    return (1 - x if k & 4 else x, 1 - y if k & 2 else y, 1 - c if k & 1 else c)


def _index(coords):
    x, y, c = coords
    return 4 * x + 2 * y + c


def _window(ref, axis, size, d):
    idx = [slice(None)] * len(ref.shape)
    idx[axis] = pl.ds(pl.multiple_of(d * size, size), size)
    return ref.at[tuple(idx)]


def _comm_call(body, n, ins, out_shapes, name):
    hbm = pl.BlockSpec(memory_space=pl.ANY)
    return pl.pallas_call(
        body, name=name, in_specs=[hbm] * n, out_specs=[hbm] * n, out_shape=out_shapes,
        scratch_shapes=[pltpu.SemaphoreType.DMA((N_DEV - 1, n)), pltpu.SemaphoreType.DMA((N_DEV - 1, n)),
                        pltpu.SemaphoreType.DMA((n,))],
    )(*ins)


def _run_exchange(n, local, remote, send_sems, recv_sems):
    me = _me_xyc()
    locs = [local(p) for p in range(n)]
    for cp in locs:
        cp.start()
    sends = [remote(k, p, me, _flip(me, k)) for k in range(1, N_DEV) for p in range(n)]
    for cp in sends:
        cp.start()
    for k in range(1, N_DEV):
        for p in range(n):
            remote(k, p, _flip(me, k), me).wait_recv()
    for cp in sends:
        cp.wait_send()
    for cp in locs:
        cp.wait()


def _all_gather(shards, axes, name):
    n = len(shards)
    sizes = [s.shape[a] for s, a in zip(shards, axes)]

    def body(*refs):
        ins, outs = refs[:n], refs[n:2 * n]
        send_sems, recv_sems, loc_sems = refs[2 * n:]
        me = _me_xyc()

        def local(p):
            return pltpu.make_async_copy(ins[p], _window(outs[p], axes[p], sizes[p], _index(me)), loc_sems.at[p])

        def remote(k, p, owner, to):
            return pltpu.make_async_remote_copy(
                src_ref=ins[p], dst_ref=_window(outs[p], axes[p], sizes[p], _index(owner)),
                send_sem=send_sems.at[k - 1, p], recv_sem=recv_sems.at[k - 1, p], device_id=to, device_id_type=MESH)

        _run_exchange(n, local, remote, send_sems, recv_sems)

    def full(s, a):
        return s.shape[:a] + (N_DEV * s.shape[a],) + s.shape[a + 1:]

    return _comm_call(body, n, shards, [jax.ShapeDtypeStruct(full(s, a), s.dtype) for s, a in zip(shards, axes)], name)


def _exchange(fulls, axes, name):
    n = len(fulls)
    sizes = [None if a is None else f.shape[a] // N_DEV for f, a in zip(fulls, axes)]

    def part_shape(f, a):
        return f.shape if a is None else f.shape[:a] + (f.shape[a] // N_DEV,) + f.shape[a + 1:]

    def body(*refs):
        ins, outs = refs[:n], refs[n:2 * n]
        send_sems, recv_sems, loc_sems = refs[2 * n:]
        me = _me_xyc()

        def src(p, to):
            return ins[p] if axes[p] is None else _window(ins[p], axes[p], sizes[p], _index(to))

        def local(p):
            return pltpu.make_async_copy(src(p, me), outs[p].at[_index(me)], loc_sems.at[p])

        def remote(k, p, sender, to):
            return pltpu.make_async_remote_copy(
                src_ref=src(p, to), dst_ref=outs[p].at[_index(sender)],
                send_sem=send_sems.at[k - 1, p], recv_sem=recv_sems.at[k - 1, p], device_id=to, device_id_type=MESH)

        _run_exchange(n, local, remote, send_sems, recv_sems)

    return _comm_call(body, n, fulls,
                      [jax.ShapeDtypeStruct((N_DEV,) + part_shape(f, a), f.dtype) for f, a in zip(fulls, axes)], name)


_HBM = pl.BlockSpec(memory_space=pltpu.HBM)
_SEM = pl.BlockSpec(memory_space=pltpu.SEMAPHORE)
_EFFECT = pltpu.SideEffectType.DATAFLOW_SIDE_EFFECTING


def _in_hbm(a):
    return pltpu.with_memory_space_constraint(a, pltpu.HBM)


def _split_start(srcs, lands, after, descr, name):
    n = len(srcs)

    def body(*refs):
        ins, lnd = refs[:n], refs[n:2 * n]
        send_sems, recv_sems = refs[2 * n + 1], refs[2 * n + 2]
        token = refs[-1]
        me = _me_xyc()
        for k in range(1, N_DEV):
            for p in range(n):
                descr(k, p, ins, lnd, send_sems, recv_sems, me, _flip(me, k)).start()
        token[...] = jnp.zeros_like(token)

    sems = pltpu.SemaphoreType.DMA(((N_DEV - 1) * n,))
    res = pl.pallas_call(
        body, name=name,
        out_shape=(sems, sems, *[pltpu.HBM(a.shape, a.dtype) for a in (*srcs, *lands)],
                   jax.ShapeDtypeStruct((8, 128), F32)),
        in_specs=[_HBM] * (2 * n) + [pl.BlockSpec(memory_space=pl.ANY)],
        out_specs=(_SEM, _SEM, *[_HBM] * (2 * n), pl.BlockSpec(memory_space=pltpu.VMEM)),
        input_output_aliases={i: 2 + i for i in range(2 * n)},
        compiler_params=pltpu.CompilerParams(has_side_effects=_EFFECT),
    )(*[_in_hbm(a) for a in (*srcs, *lands)], after)
    return res[0], res[1], res[2:2 + n], res[2 + n:2 + 2 * n], res[-1]


def _split_wait(send_sems, recv_sems, srcs, lands, after, descr, name):
    n = len(srcs)

    def body(*refs):
        ins, lnd = refs[:n], refs[n:2 * n]
        s_sems, r_sems = refs[2 * n], refs[2 * n + 1]
        me = _me_xyc()
        for k in range(1, N_DEV):
            for p in range(n):
                peer = _flip(me, k)
                descr(k, p, ins, lnd, s_sems, r_sems, me, peer).wait_send()
                descr(k, p, ins, lnd, s_sems, r_sems, peer, me).wait_recv()

    res = pl.pallas_call(
        body, name=name,
        out_shape=tuple(pltpu.HBM(a.shape, a.dtype) for a in (*srcs, *lands)),
        in_specs=[_HBM] * (2 * n) + [_SEM, _SEM, pl.BlockSpec(memory_space=pl.ANY)],
        out_specs=tuple([_HBM] * (2 * n)),
        input_output_aliases={i: i for i in range(2 * n)},
        compiler_params=pltpu.CompilerParams(has_side_effects=_EFFECT),
    )(*srcs, *lands, send_sems, recv_sems, after)
    return list(res[n:])


def _gather_descr(axes, sizes):
    def descr(k, p, ins, lnd, send_sems, recv_sems, sender, dest):
        return pltpu.make_async_remote_copy(
            src_ref=ins[p], dst_ref=_window(lnd[p], axes[p], sizes[p], _index(sender)),
            send_sem=send_sems.at[(k - 1) * len(axes) + p], recv_sem=recv_sems.at[(k - 1) * len(axes) + p],
            device_id=dest, device_id_type=MESH)
    return descr


def _exchange_descr(axes, sizes):
    def descr(k, p, ins, lnd, send_sems, recv_sems, sender, dest):
        return pltpu.make_async_remote_copy(
            src_ref=_window(ins[p], axes[p], sizes[p], _index(dest)), dst_ref=lnd[p].at[_index(sender)],
            send_sem=send_sems.at[(k - 1) * len(axes) + p], recv_sem=recv_sems.at[(k - 1) * len(axes) + p],
            device_id=dest, device_id_type=MESH)
    return descr


def _gather_begin(shards, axes, after, name):
    sizes = [s.shape[a] for s, a in zip(shards, axes)]
    me = _index(_me_xyc())
    lands = []
    for s, a, sz in zip(shards, axes, sizes):
        full = s.shape[:a] + (N_DEV * sz,) + s.shape[a + 1:]
        lands.append(lax.dynamic_update_slice_in_dim(lax.empty(full, s.dtype), s, me * sz, a))
    descr = _gather_descr(axes, sizes)
    send_sems, recv_sems, srcs, lands, token = _split_start(shards, lands, after, descr, name)
    return (send_sems, recv_sems, srcs, lands, descr), token


def _exchange_begin(fulls, axes, after, name):
    sizes = [f.shape[a] // N_DEV for f, a in zip(fulls, axes)]
    me = _index(_me_xyc())
    lands = []
    for f, a, sz in zip(fulls, axes, sizes):
        own = lax.dynamic_slice_in_dim(f, me * sz, sz, a)
        lands.append(lax.dynamic_update_slice_in_dim(lax.empty((N_DEV,) + own.shape, f.dtype), own[None], me, 0))
    descr = _exchange_descr(axes, sizes)
    send_sems, recv_sems, srcs, lands, token = _split_start(fulls, lands, after, descr, name)
    return (send_sems, recv_sems, srcs, lands, descr), token


def _split_end(handle, after, name):
    send_sems, recv_sems, srcs, lands, descr = handle
    return _split_wait(send_sems, recv_sems, srcs, lands, after, descr, name)


def _adam_math(g, w, m, v):
    m = ADAM_B1 * m + (1.0 - ADAM_B1) * g
    v = ADAM_B2 * v + (1.0 - ADAM_B2) * (g * g)
    m_hat = m / (1.0 - ADAM_B1 ** ADAM_STEP)
    v_hat = v / (1.0 - ADAM_B2 ** ADAM_STEP)
    delta = -ADAM_LR * (m_hat / (jnp.sqrt(v_hat) + ADAM_EPS) + ADAM_WD * w)
    return delta, m, v


def _sum_slabs(r_ref):
    g = r_ref[0].astype(F32)
    for s in range(1, N_DEV):
        g = g + r_ref[s].astype(F32)
    return g


def _adamw_reduce(recv, w, m, v, tb, name):
    r, c = w.shape

    def body(r_ref, w_ref, m_ref, v_ref, g_ref, d_ref, nm_ref, nv_ref):
        g = _sum_slabs(r_ref)
        g_ref[...] = g
        d_ref[...], nm_ref[...], nv_ref[...] = _adam_math(g, w_ref[...], m_ref[...], v_ref[...])

    blk = pl.BlockSpec((tb, c), lambda i: (i, 0))
    return pl.pallas_call(
        body, name=name, grid=(r // tb,),
        in_specs=[pl.BlockSpec((N_DEV, tb, c), lambda i: (0, i, 0)), blk, blk, blk],
        out_specs=[blk] * 4, out_shape=[jax.ShapeDtypeStruct((r, c), F32)] * 4,
        compiler_params=_cparams(("parallel",)),
    )(recv, w, m, v)


def _reduce8(recv, tb, name):
    r, c = recv.shape[1:]

    def body(r_ref, g_ref):
        g_ref[...] = _sum_slabs(r_ref)

    return pl.pallas_call(
        body, name=name, grid=(r // tb,),
        in_specs=[pl.BlockSpec((N_DEV, tb, c), lambda i: (0, i, 0))],
        out_specs=pl.BlockSpec((tb, c), lambda i: (i, 0)), out_shape=jax.ShapeDtypeStruct((r, c), F32),
        compiler_params=_cparams(("parallel",)),
    )(recv)


def _adamw(g, w, m, v, tb, name):
    r, c = w.shape

    def body(g_ref, w_ref, m_ref, v_ref, d_ref, nm_ref, nv_ref):
        d_ref[...], nm_ref[...], nv_ref[...] = _adam_math(g_ref[...], w_ref[...], m_ref[...], v_ref[...])

    blk = pl.BlockSpec((tb, c), lambda i: (i, 0))
    return pl.pallas_call(
        body, name=name, grid=(r // tb,), in_specs=[blk] * 4, out_specs=[blk] * 3,
        out_shape=[jax.ShapeDtypeStruct((r, c), F32)] * 3, compiler_params=_cparams(("parallel",)),
    )(g, w, m, v)


DN_IN_SHARD = DN_IN // N_DEV
DN_IN_SHARD_PAD = 432
CONV_SHARD = (1, DN_CONV, 288)


def _pack_small(arrs, rows):
    flat = jnp.concatenate([a.astype(F32).reshape(-1) for a in arrs])
    return jnp.pad(flat, (0, rows * PACK_C - flat.size)).reshape(rows, PACK_C)


def _unpack_small(packed, shapes):
    flat, out, off = packed.reshape(-1), [], 0
    for shp in shapes:
        n = int(np.prod(shp))
        out.append(flat[off:off + n].reshape(shp))
        off += n
    return out


def kernel(x, mem, rel_bias, att_w_in, att_w_out, dn_w_in, dn_conv, dn_a_log, dn_dt_bias, dn_out_norm, dn_w_out, mem_norm, mem_w_kv, norm_mix_pre, norm_mix_post, norm_ffn_pre, norm_ffn_post, ffn_w_gate_up, ffn_w_down, loss_target, m_rel_bias, m_att_w_in, m_att_w_out, m_dn_w_in, m_dn_conv, m_dn_a_log, m_dn_dt_bias, m_dn_out_norm, m_dn_w_out, m_mem_norm, m_mem_w_kv, m_norm_mix_pre, m_norm_mix_post, m_norm_ffn_pre, m_norm_ffn_post, m_ffn_w_gate_up, m_ffn_w_down, v_rel_bias, v_att_w_in, v_att_w_out, v_dn_w_in, v_dn_conv, v_dn_a_log, v_dn_dt_bias, v_dn_out_norm, v_dn_w_out, v_mem_norm, v_mem_w_kv, v_norm_mix_pre, v_norm_mix_post, v_norm_ffn_pre, v_norm_ffn_post, v_ffn_w_gate_up, v_ffn_w_down):
    x0, mem0, tgt = x[0], mem[0], loss_target[0]
    t = x0.shape[0]
    axes = ("x", "y", "c")

    def t_shard(w):
        return jnp.swapaxes(w, 1, 2).astype(BF16)

    dn_in_pad = ((0, 0), (0, DN_IN_SHARD_PAD - DN_IN_SHARD), (0, 0))
    (w_att_in_t,) = _all_gather([t_shard(att_w_in)], [1], "allgather_first")
    w_att_in_t = w_att_in_t[0]
    gu_t, down = t_shard(ffn_w_gate_up), ffn_w_down.astype(BF16)
    gather_o, tok_o = _gather_begin([att_w_out.astype(BF16), mem_w_kv.astype(BF16)], [1, 1], w_att_in_t,
                                    "gather_att_out_start")
    gather_a, tok_a = _gather_begin([gu_t[0:1], down[0:1]], [1, 1], tok_o, "gather_ffn0_start")
    gather_b, tok_b = _gather_begin(
        [jnp.pad(t_shard(dn_w_in), dn_in_pad), dn_w_out.astype(BF16), gu_t[1:2], down[1:2], dn_conv],
        [1, 1, 1, 1, 0], tok_a, "gather_layer1_start")

    def gain(a, i):
        return a[i].reshape(1, D)

    (h0,) = _rowwise(_fn_pre, [x0], [gain(norm_mix_pre, 0) + tok_b[0:1, 0:1]], [(D, BF16)], ROW_TB, "pre0")
    p0 = _matmul(h0, w_att_in_t, "nt", BF16, "att_in")
    late = {}

    def kv0(after):
        late["w_att_out"], late["w_kv"] = _split_end(gather_o, after, "gather_att_out_wait")
        return _mem_kv_fwd(mem0, gain(mem_norm, 0), late["w_kv"][0], 0)

    cat0, res0, (km0, vm0, memn0) = _attn_mixer_fwd(p0, rel_bias, kv0)
    w_att_out, w_kv = late["w_att_out"][0], late["w_kv"]
    y0 = _matmul(cat0, w_att_out, "nn", F32, "att_out")
    g_a = [gain(norm_mix_post, 0), gain(norm_ffn_pre, 0)]
    x1, h1 = _rowwise(_fn_res_pre, [x0, y0], g_a, [(D, F32), (D, BF16)], ROW_TB, "res_pre0")
    w_gu_t0, w_down0 = [w[0] for w in _split_end(gather_a, h1, "gather_ffn0_wait")]
    f0, gu0, a0 = _ffn_fwd(h1, w_gu_t0, w_down0, 0)
    g_b = [gain(norm_ffn_post, 0), gain(norm_mix_pre, 1)]
    x2, h2 = _rowwise(_fn_res_pre, [x1, f0], g_b, [(D, F32), (D, BF16)], ROW_TB, "res_pre1")
    km1, vm1, memn1 = _mem_kv_fwd(mem0, gain(mem_norm, 1), w_kv[1], 1)
    w_dn_in_g, w_dn_out, w_gu_t1, w_down1, conv_g = _split_end(gather_b, h2, "gather_layer1_wait")
    w_dn_in_g, w_dn_out, w_gu_t1, w_down1 = w_dn_in_g[0], w_dn_out[0], w_gu_t1[0], w_down1[0]
    conv_full = conv_g.transpose(1, 0, 2).reshape(DN_CONV, 3 * TOK_W)
    w_dn_in_t = jnp.concatenate(
        [w_dn_in_g[DN_IN_SHARD_PAD * j:DN_IN_SHARD_PAD * j + DN_IN_SHARD] for j in range(N_DEV)]
        + [jnp.zeros((DN_IN_PAD - DN_IN, D), BF16)], axis=0)
    p1 = _matmul(h2, w_dn_in_t, "nt", F32, "dn_in")
    cat1, res1 = _dn_mixer_fwd(p1, conv_full, dn_a_log[0], dn_dt_bias[0], dn_out_norm[0], km1, vm1)
    y1 = _matmul(cat1, w_dn_out, "nn", F32, "dn_out")
    g_c = [gain(norm_mix_post, 1), gain(norm_ffn_pre, 1)]
    x3, h3 = _rowwise(_fn_res_pre, [x2, y1], g_c, [(D, F32), (D, BF16)], ROW_TB, "res_pre2")
    f1, gu1, a1 = _ffn_fwd(h3, w_gu_t1, w_down1, 1)
    g_d = [gain(norm_ffn_post, 1)]
    (x4,) = _rowwise(_fn_res, [x3, f1], g_d, [(D, F32)], ROW_TB, "res3")
    dx4, lrow = _loss_kernel(x4, tgt, ROW_TB, "loss")
    loss = lax.psum(lrow[0, 0] * (0.5 / D), axes)

    (df1,), (dg_fpost1,) = _rowwise_bwd(_fn_res, [x3, f1], g_d, [dx4], [None, BF16], ROW_TB, "res3_bwd")
    dh3, dwgu1, dwd1 = _ffn_bwd(df1, h3, w_gu_t1, w_down1, gu1, a1, 1)
    (dx2, dy1), (dg_mpost1, dg_fpre1) = _rowwise_bwd(_fn_res_pre, [x2, y1], g_c, [dx4, dh3], [F32, BF16],
                                                     ROW_TB, "res_pre2_bwd")
    dcat1 = _matmul(dy1, w_dn_out, "nt", F32, "dn_out_dx")
    dw_dn_out = _matmul(cat1, dy1, "tn", BF16, "dn_out_dw")
    dp1, dconv, da_log, ddt_bias, dout_norm, dkm1, dvm1 = _dn_mixer_bwd(dcat1, res1, km1, vm1)
    dwkv1, dg_mem1 = _mem_kv_bwd(mem0, gain(mem_norm, 1), w_kv[1], memn1, dkm1, dvm1, 1)
    dh2 = _matmul(dp1, w_dn_in_t, "nn", F32, "dn_in_dx")
    dw_dn_in_t = _matmul(dp1, h2, "tn", BF16, "dn_in_dw")
    dn_in_parts = [jnp.pad(dw_dn_in_t[DN_IN_SHARD * j:DN_IN_SHARD * (j + 1)],
                           ((0, DN_IN_SHARD_PAD - DN_IN_SHARD), (0, 0))) for j in range(N_DEV)]
    xch_b, tok = _exchange_begin(
        [jnp.concatenate(dn_in_parts, axis=0)[None], dw_dn_out[None], dwkv1[None], dwgu1[None], dwd1[None]],
        [1, 1, 1, 1, 1], dh2, "exchange_layer1_start")
    (dx1, df0), (dg_fpost0, dg_mpre1) = _rowwise_bwd(_fn_res_pre, [x1, f0], [g + tok[0:1, 0:1] for g in g_b],
                                                     [dx2, dh2], [F32, BF16], ROW_TB, "res_pre1_bwd")
    dh1, dwgu0, dwd0 = _ffn_bwd(df0, h1, w_gu_t0, w_down0, gu0, a0, 0)
    xch_a, tok = _exchange_begin([dwgu0[None], dwd0[None]], [1, 1], dh1, "exchange_ffn0_start")
    (dx0, dy0), (dg_mpost0, dg_fpre0) = _rowwise_bwd(_fn_res_pre, [x0, y0], [g + tok[0:1, 0:1] for g in g_a],
                                                     [dx1, dh1], [F32, BF16], ROW_TB, "res_pre0_bwd")
    dcat0 = _matmul(dy0, w_att_out, "nt", F32, "att_out_dx")
    dw_att_out = _matmul(cat0, dy0, "tn", BF16, "att_out_dw")
    dp0, drel, dkm0, dvm0 = _attn_mixer_bwd(dcat0, res0, km0, vm0)
    dwkv0, dg_mem0 = _mem_kv_bwd(mem0, gain(mem_norm, 0), w_kv[0], memn0, dkm0, dvm0, 0)
    xch_o, tok = _exchange_begin([dw_att_out[None], dwkv0[None]], [1, 1], dp0, "exchange_att_out_start")
    dw_att_in_t = _matmul(dp0, h0, "tn", BF16, "att_in_dw")
    xch_i, tok_i = _exchange_begin([dw_att_in_t[None]], [1], tok, "exchange_att_in_start")
    dh0 = _matmul(dp0, w_att_in_t, "nn", F32, "att_in_dx")
    (grad_x,), (dg_mpre0,) = _rowwise_bwd(_fn_first, [x0], [gain(norm_mix_pre, 0) + tok_i[0:1, 0:1]], [dx0, dh0],
                                          [F32], ROW_TB, "pre0_bwd")

    small_grads = [drel, da_log, ddt_bias, dout_norm, jnp.concatenate([dg_mem0, dg_mem1]),
                   jnp.concatenate([dg_mpre0, dg_mpre1]), jnp.concatenate([dg_mpost0, dg_mpost1]),
                   jnp.concatenate([dg_fpre0, dg_fpre1]), jnp.concatenate([dg_fpost0, dg_fpost1]), dconv]
    (r_small,) = _exchange([_pack_small(small_grads, SMALL_ROWS)], [None], "exchange_last")
    (r_att_in,) = _split_end(xch_i, r_small, "exchange_att_in_wait")
    r_att_out, r_kv0 = _split_end(xch_o, r_small, "exchange_att_out_wait")
    r_gu0, r_down0 = _split_end(xch_a, r_small, "exchange_ffn0_wait")
    r_dn_in, r_dn_out, r_kv1, r_gu1, r_down1 = _split_end(xch_b, r_small, "exchange_layer1_wait")

    def rows(a):
        return a.reshape((-1,) + a.shape[-1:])

    def row_sharded(recv, w, m, v, tb, name):
        outs = _adamw_reduce(recv.reshape((N_DEV, -1) + recv.shape[-1:]), rows(w), rows(m), rows(v), tb, name)
        return [o.reshape(w.shape) for o in outs]

    def col_sharded(recv, w, m, v, tb, name):
        g_t = _reduce8(recv.reshape((N_DEV, -1) + recv.shape[-1:]), tb, name + "_sum")
        g = jnp.swapaxes(g_t.reshape(recv.shape[1:])[:, :w.shape[2]], 1, 2)
        outs = _adamw(rows(g), rows(w), rows(m), rows(v), 256, name)
        return [g] + [o.reshape(w.shape) for o in outs]

    def per_layer(fn, recvs, w, m, v, tb, name):
        outs = [fn(r, w[l:l + 1], m[l:l + 1], v[l:l + 1], tb, f"{name}{l}") for l, r in enumerate(recvs)]
        return [jnp.concatenate(pair, axis=0) for pair in zip(*outs)]

    big = [col_sharded(r_att_in, att_w_in, m_att_w_in, v_att_w_in, 320, "adamw_att_in"),
           row_sharded(r_att_out, att_w_out, m_att_w_out, v_att_w_out, 128, "adamw_att_out"),
           col_sharded(r_dn_in, dn_w_in, m_dn_w_in, v_dn_w_in, 432, "adamw_dn_in"),
           row_sharded(r_dn_out, dn_w_out, m_dn_w_out, v_dn_w_out, 128, "adamw_dn_out"),
           per_layer(row_sharded, [r_kv0, r_kv1], mem_w_kv, m_mem_w_kv, v_mem_w_kv, 128, "adamw_mem_kv"),
           per_layer(col_sharded, [r_gu0, r_gu1], ffn_w_gate_up, m_ffn_w_gate_up, v_ffn_w_gate_up, 176,
                     "adamw_ffn_gu"),
           per_layer(row_sharded, [r_down0, r_down1], ffn_w_down, m_ffn_w_down, v_ffn_w_down, 176,
                     "adamw_ffn_down")]
    g_big, d_big, nm_big, nv_big = [[b[i] for b in big] for i in range(4)]

    g_small = _reduce8(r_small, SMALL_ROWS, "reduce_small")
    rep_shapes = [(32, 12), (1, 2, 6), (1, 2, 6), (1, 128), (2, D), (2, D), (2, D), (2, D), (2, D)]
    *g_rep, g_conv_full = _unpack_small(g_small, rep_shapes + [(DN_CONV, 3 * TOK_W)])
    me = _index(_me_xyc())
    g_conv = lax.dynamic_slice(g_conv_full, (0, me * 288), (DN_CONV, 288)).reshape(CONV_SHARD)
    small_shapes = rep_shapes + [CONV_SHARD]
    small_w = [rel_bias, dn_a_log, dn_dt_bias, dn_out_norm, mem_norm, norm_mix_pre, norm_mix_post,
               norm_ffn_pre, norm_ffn_post, dn_conv]
    small_m = [m_rel_bias, m_dn_a_log, m_dn_dt_bias, m_dn_out_norm, m_mem_norm, m_norm_mix_pre, m_norm_mix_post,
               m_norm_ffn_pre, m_norm_ffn_post, m_dn_conv]
    small_v = [v_rel_bias, v_dn_a_log, v_dn_dt_bias, v_dn_out_norm, v_mem_norm, v_norm_mix_pre, v_norm_mix_post,
               v_norm_ffn_pre, v_norm_ffn_post, v_dn_conv]
    g_small_list = g_rep + [g_conv]
    outs_small = _adamw(_pack_small(g_small_list, 24), _pack_small(small_w, 24), _pack_small(small_m, 24),
                        _pack_small(small_v, 24), 24, "adamw_small")
    d_small, nm_small, nv_small = [_unpack_small(o, small_shapes) for o in outs_small]

    def ordered(small, big):
        return [small[0], big[0], big[1], big[2], small[9], small[1], small[2], small[3], big[3], small[4],
                big[4], small[5], small[6], small[7], small[8], big[5], big[6]]

    g_small_out = [g.reshape(s) for g, s in zip(g_small_list, small_shapes)]
    return (loss, grad_x[None], *ordered(g_small_out, g_big), *ordered(d_small, d_big),
            *ordered(nm_small, nm_big), *ordered(nv_small, nv_big))
```

```python
import functools
import math
from typing import NamedTuple

import numpy as np
import jax
import jax.numpy as jnp
from jax import lax
from jax.experimental import pallas as pl
from jax.experimental.pallas import tpu as pltpu

F32 = jnp.float32
BF16 = jnp.bfloat16
HI = lax.Precision.HIGHEST
MESH = pl.DeviceIdType.MESH

N_DEV = 8
D = 1024
EPS = 1e-6
NEG = -1e30
TOK_W = 768
MEM_W = 256
ATT_HD = 64
DIL_GROUPS = ((128, 1), (512, 4), (2048, 16))
BAND_HALF = 64
REL_BUCKETS = 32
REL_MAX_DIST = 1024
DN_HD = 128
DN_HEADS = 6
DN_CONV = 5
DN_CHUNK = 64
MEM_HEADS = 4
D_FF = 2816
DN_IN = 3352
DN_IN_PAD = 3456

ADAM_LR, ADAM_B1, ADAM_B2, ADAM_EPS, ADAM_WD, ADAM_STEP = 0.001, 0.9, 0.999, 1e-08, 0.01, 10

PACK_C = 512
SMALL_ROWS = 48
VMEM_LIMIT = 48 * 1024 * 1024


def _cparams(sem=None):
    kw = dict(vmem_limit_bytes=VMEM_LIMIT)
    if sem is not None:
        kw["dimension_semantics"] = sem
    return pltpu.CompilerParams(**kw)


def _tile(n, cap):
    if n <= cap:
        return n
    best = None
    for t in range(128, cap + 1, 128):
        if n % t == 0:
            best = t
    assert best is not None, (n, cap)
    return best


def _matmul(a, b, mode, out_dtype, name, tm=1024, tn=1408, tk=None):
    if tk is None:
        tk = 4096 if mode == "tn" else 2816
    if mode == "tn":
        tm = min(tm, 512)
    if mode == "nn":
        (m, kc), (_, n) = a.shape, b.shape
        dims = (((1,), (0,)), ((), ()))
    elif mode == "nt":
        (m, kc), (n, _) = a.shape, b.shape
        dims = (((1,), (1,)), ((), ()))
    else:
        (kc, m), (_, n) = a.shape, b.shape
        dims = (((0,), (0,)), ((), ()))
    tm = m if m <= tm else _tile(m, tm)
    tn = _tile(n, tn)
    tk = _tile(kc, tk)
    nk = kc // tk

    def body(a_ref, b_ref, o_ref, acc_ref):
        k = pl.program_id(2)
        part = lax.dot_general(a_ref[...], b_ref[...], dims, preferred_element_type=F32)

        @pl.when(k == 0)
        def _():
            acc_ref[...] = part

        @pl.when(k > 0)
        def _():
            acc_ref[...] += part

        @pl.when(k == nk - 1)
        def _():
            o_ref[...] = acc_ref[...].astype(o_ref.dtype)

    if mode == "nn":
        a_spec = pl.BlockSpec((tm, tk), lambda i, j, k: (i, k))
        b_spec = pl.BlockSpec((tk, tn), lambda i, j, k: (k, j))
    elif mode == "nt":
        a_spec = pl.BlockSpec((tm, tk), lambda i, j, k: (i, k))
        b_spec = pl.BlockSpec((tn, tk), lambda i, j, k: (j, k))
    else:
        a_spec = pl.BlockSpec((tk, tm), lambda i, j, k: (k, i))
        b_spec = pl.BlockSpec((tk, tn), lambda i, j, k: (k, j))
    return pl.pallas_call(
        body, name=name, grid=(m // tm, n // tn, nk),
        in_specs=[a_spec, b_spec],
        out_specs=pl.BlockSpec((tm, tn), lambda i, j, k: (i, j)),
        out_shape=jax.ShapeDtypeStruct((m, n), out_dtype),
        scratch_shapes=[pltpu.VMEM((tm, tn), F32)],
        compiler_params=_cparams(("parallel", "parallel", "arbitrary")),
    )(a, b)


class _Cols(NamedTuple):
    arr: jax.Array
    width: int
    block: int

    @property
    def shape(self):
        return (self.arr.shape[0], self.width)


def _row_spec(r, tb):
    if isinstance(r, _Cols):
        return pl.BlockSpec((tb, r.width), lambda i, b=r.block: (i, b))
    return pl.BlockSpec((tb, r.shape[1]), lambda i: (i, 0))


def _row_arr(r):
    return r.arr if isinstance(r, _Cols) else r


def _rowwise(fn, rows, params, outs, tb, name):
    t = rows[0].shape[0]
    nr, npar = len(rows), len(params)

    def body(*refs):
        ins = [r[...].astype(F32) for r in refs[:nr + npar]]
        res = fn(*ins)
        for o_ref, r in zip(refs[nr + npar:], res):
            o_ref[...] = r.astype(o_ref.dtype)

    return pl.pallas_call(
        body, name=name, grid=(t // tb,),
        in_specs=[_row_spec(r, tb) for r in rows] + [pl.BlockSpec(p.shape, lambda i: (0, 0)) for p in params],
        out_specs=[pl.BlockSpec((tb, c), lambda i: (i, 0)) for c, _ in outs],
        out_shape=[jax.ShapeDtypeStruct((t, c), dt) for c, dt in outs],
        compiler_params=_cparams(("parallel",)),
    )(*[_row_arr(r) for r in rows], *params)


def _rowwise_bwd(fn, rows, params, cots, row_grad, tb, name):
    t = rows[0].shape[0]
    nr, npar, nc = len(rows), len(params), len(cots)
    want = [i for i, g in enumerate(row_grad) if g is not None]

    def body(*refs):
        ins = [r[...].astype(F32) for r in refs[:nr + npar]]
        cts = tuple(r[...].astype(F32) for r in refs[nr + npar:nr + npar + nc])
        outs = refs[nr + npar + nc:]
        _, vjp = jax.vjp(fn, *ins)
        grads = vjp(cts)
        for o_ref, i in zip(outs[:len(want)], want):
            o_ref[...] = grads[i].astype(o_ref.dtype)
        first = pl.program_id(0) == 0
        for o_ref, g in zip(outs[len(want):], grads[nr:]):
            @pl.when(first)
            def _(o_ref=o_ref, g=g):
                o_ref[...] = g

            @pl.when(jnp.logical_not(first))
            def _(o_ref=o_ref, g=g):
                o_ref[...] += g

    res = pl.pallas_call(
        body, name=name, grid=(t // tb,),
        in_specs=[_row_spec(r, tb) for r in rows] + [pl.BlockSpec(p.shape, lambda i: (0, 0)) for p in params]
        + [_row_spec(c, tb) for c in cots],
        out_specs=[pl.BlockSpec((tb, rows[i].shape[1]), lambda i_: (i_, 0)) for i in want]
        + [pl.BlockSpec(p.shape, lambda i: (0, 0)) for p in params],
        out_shape=[jax.ShapeDtypeStruct(tuple(rows[i].shape), row_grad[i]) for i in want]
        + [jax.ShapeDtypeStruct(p.shape, F32) for p in params],
        compiler_params=_cparams(("arbitrary",)),
    )(*[_row_arr(r) for r in rows], *params, *[_row_arr(c) for c in cots])
    return list(res[:len(want)]), list(res[len(want):])


def _rms(x, g):
    return x * lax.rsqrt(jnp.mean(x * x, axis=-1, keepdims=True) + EPS) * g


def _fn_pre(x, g):
    return (_rms(x, g),)


def _fn_res_pre(x, y, g_post, g_pre):
    x1 = x + _rms(y, g_post)
    return x1, _rms(x1, g_pre)


def _fn_res(x, y, g_post):
    return (x + _rms(y, g_post),)


def _sigmoid(x):
    return 1.0 / (1.0 + jnp.exp(-x))


def _silu(x):
    return x * _sigmoid(x)


def _fn_swiglu(gu):
    return (_silu(gu[:, :D_FF]) * gu[:, D_FF:],)


def _fn_combine(o, lse):
    ls = [lse[:, 256 * g:256 * (g + 1)] for g in range(3)]
    mx = lax.stop_gradient(jnp.maximum(jnp.maximum(ls[0], ls[1]), ls[2]))
    es = [jnp.exp(l - mx) for l in ls]
    inv = 1.0 / (es[0] + es[1] + es[2])
    return (jnp.concatenate([o[:, 256 * g:256 * (g + 1)] * (es[g] * inv) for g in range(3)], axis=1),)


def _fn_outnorm(o_f, o_r, z, gain):
    res = []
    for h in range(DN_HEADS):
        sl = slice(DN_HD * h, DN_HD * (h + 1))
        o = o_f[:, sl] + o_r[:, sl]
        res.append(o * lax.rsqrt(jnp.mean(o * o, axis=-1, keepdims=True) + EPS) * gain * _silu(z[:, sl]))
    return (jnp.concatenate(res, axis=1),)


def _loss_kernel(x, tgt, tb, name):
    t, d = x.shape

    def body(x_ref, t_ref, dx_ref, l_ref, acc_ref):
        i = pl.program_id(0)
        e = x_ref[...] - t_ref[...]
        dx_ref[...] = e * (1.0 / d)
        part = jnp.sum(e * e, axis=0, keepdims=True)

        @pl.when(i == 0)
        def _():
            acc_ref[...] = part

        @pl.when(i > 0)
        def _():
            acc_ref[...] += part

        @pl.when(i == t // tb - 1)
        def _():
            l_ref[...] = jnp.broadcast_to(jnp.sum(acc_ref[...], axis=-1, keepdims=True), (1, 128))

    return pl.pallas_call(
        body, name=name, grid=(t // tb,),
        in_specs=[pl.BlockSpec((tb, d), lambda i: (i, 0))] * 2,
        out_specs=[pl.BlockSpec((tb, d), lambda i: (i, 0)), pl.BlockSpec((1, 128), lambda i: (0, 0))],
        out_shape=[jax.ShapeDtypeStruct((t, d), F32), jax.ShapeDtypeStruct((1, 128), F32)],
        scratch_shapes=[pltpu.VMEM((1, d), F32)],
        compiler_params=_cparams(("arbitrary",)),
    )(x, tgt)


def _band_fn(l_sub, bq, i, q, kw, vw, bm):
    w = bq + 2 * BAND_HALF
    s = lax.dot_general((q * (ATT_HD ** -0.5)).astype(BF16), kw.astype(BF16), (((2,), (2,)), ((0,), (0,))),
                        preferred_element_type=F32) + bm
    kpos = i * bq - BAND_HALF + lax.broadcasted_iota(jnp.int32, (4, bq, w), 2)
    s = jnp.where((kpos >= 0) & (kpos < l_sub), s, NEG)
    m = lax.stop_gradient(jnp.max(s, axis=-1, keepdims=True))
    p = jnp.exp(s - m)
    den = jnp.sum(p, axis=-1, keepdims=True)
    o = lax.dot_general(p.astype(BF16), vw.astype(BF16), (((2,), (1,)), ((0,), (0,))),
                        preferred_element_type=F32) / den
    return o, jnp.broadcast_to(m + jnp.log(den), o.shape)


def _band_specs(l_sub, bq):
    w = bq + 2 * BAND_HALF
    qs = pl.BlockSpec((None, 4, bq, ATT_HD), lambda r, i: (r, 0, i, 0))
    ks = pl.BlockSpec((None, 4, l_sub + 2 * BAND_HALF, ATT_HD), lambda r, i: (r, 0, 0, 0))
    bs = pl.BlockSpec((4, bq, w), lambda r, i: (0, 0, 0))
    return qs, ks, bs


def _band_fwd(q, k, v, bm, dil, l_sub, bq, name):
    w = bq + 2 * BAND_HALF
    qs, ks, bs = _band_specs(l_sub, bq)

    def body(q_ref, k_ref, v_ref, bm_ref, o_ref, l_ref):
        i = pl.program_id(1)
        st = pl.multiple_of(i * bq, bq)
        o, lse = _band_fn(l_sub, bq, i, q_ref[...].astype(F32), k_ref[:, pl.ds(st, w), :].astype(F32),
                          v_ref[:, pl.ds(st, w), :].astype(F32), bm_ref[...])
        o_ref[...] = o
        l_ref[...] = lse

    return pl.pallas_call(
        body, name=name, grid=(dil, l_sub // bq),
        in_specs=[qs, ks, ks, bs], out_specs=[qs, qs],
        out_shape=[jax.ShapeDtypeStruct(q.shape, F32)] * 2,
        compiler_params=_cparams(("parallel", "arbitrary")),
    )(q, k, v, bm)


def _band_bwd(q, k, v, bm, do, dlse, dil, l_sub, bq, name):
    w = bq + 2 * BAND_HALF
    qs, ks, bs = _band_specs(l_sub, bq)

    def body(q_ref, k_ref, v_ref, bm_ref, do_ref, dl_ref, dq_ref, dk_ref, dv_ref, dbm_ref):
        r, i = pl.program_id(0), pl.program_id(1)
        st = pl.multiple_of(i * bq, bq)
        _, vjp = jax.vjp(functools.partial(_band_fn, l_sub, bq, i),
                         q_ref[...].astype(F32), k_ref[:, pl.ds(st, w), :].astype(F32),
                         v_ref[:, pl.ds(st, w), :].astype(F32), bm_ref[...])
        dq, dkw, dvw, dbm = vjp((do_ref[...].astype(F32), dl_ref[...]))
        dq_ref[...] = dq.astype(dq_ref.dtype)

        @pl.when(i == 0)
        def _():
            dk_ref[...] = jnp.zeros_like(dk_ref)
            dv_ref[...] = jnp.zeros_like(dv_ref)

        dk_ref[:, pl.ds(st, w), :] += dkw
        dv_ref[:, pl.ds(st, w), :] += dvw

        @pl.when((i == 0) & (r == 0))
        def _():
            dbm_ref[...] = dbm

        @pl.when((i > 0) | (r > 0))
        def _():
            dbm_ref[...] += dbm

    return pl.pallas_call(
        body, name=name, grid=(dil, l_sub // bq),
        in_specs=[qs, ks, ks, bs, qs, qs], out_specs=[qs, ks, ks, bs],
        out_shape=[jax.ShapeDtypeStruct(q.shape, BF16), jax.ShapeDtypeStruct(k.shape, F32),
                   jax.ShapeDtypeStruct(k.shape, F32), jax.ShapeDtypeStruct(bm.shape, F32)],
        compiler_params=_cparams(("arbitrary", "arbitrary")),
    )(q, k, v, bm, do, dlse)


def _t5_bucket(rel):
    half = REL_BUCKETS // 2
    max_exact = half // 2
    n = np.abs(rel)
    large = max_exact + (np.log(np.maximum(n, 1) / max_exact) / math.log(REL_MAX_DIST / max_exact)
                         * (half - max_exact)).astype(np.int64)
    large = np.minimum(large, half - 1)
    return ((rel > 0) * half + np.where(n < max_exact, n, large)).astype(np.int32)


def _bucket_onehot(dil):
    idx = _t5_bucket(np.arange(-BAND_HALF, BAND_HALF + 1) * dil)
    oh = np.zeros((2 * BAND_HALF + 1, REL_BUCKETS), np.float32)
    oh[np.arange(2 * BAND_HALF + 1), idx] = 1.0
    return oh


def _band_bias(rel_bias, gi, dil, bq):
    w = bq + 2 * BAND_HALF
    nb = 2 * BAND_HALF + 1
    bias = jnp.dot(jnp.asarray(_bucket_onehot(dil)), rel_bias[:, 4 * gi:4 * gi + 4], precision=HI)
    row = jnp.concatenate([bias.T, jnp.full((4, w + 1 - nb), NEG, F32)], axis=1)
    flat = jnp.tile(row, (1, bq))[:, :bq * w]
    return flat.reshape(4, bq, w)


def _relbias_grad(dbms, name):
    nb = 2 * BAND_HALF + 1
    bq = max(d.shape[1] for d in dbms)
    skew = []
    for dbm in dbms:
        bqg, w = dbm.shape[1], dbm.shape[2]
        flat = jnp.pad(dbm.reshape(4, bqg * w), ((0, 0), (0, bqg)))
        skew.append(jnp.pad(flat.reshape(4, bqg, w + 1)[:, :, :nb], ((0, 0), (0, bq - bqg), (0, 256 - nb))))
    sk = jnp.concatenate(skew, axis=0)
    oh = np.zeros((3, 256, 128), np.float32)
    for gi, (_, dil) in enumerate(DIL_GROUPS):
        oh[gi, :2 * BAND_HALF + 1, :REL_BUCKETS] = _bucket_onehot(dil)

    def body(s_ref, oh_ref, o_ref):
        col = jnp.sum(s_ref[...], axis=0, keepdims=True)
        o_ref[...] = jnp.dot(jnp.broadcast_to(col, (8, 256)), oh_ref[...], precision=HI, preferred_element_type=F32)

    out = pl.pallas_call(
        body, name=name, grid=(12,),
        in_specs=[pl.BlockSpec((None, bq, 256), lambda n: (n, 0, 0)),
                  pl.BlockSpec((None, 256, 128), lambda n: (n // 4, 0, 0))],
        out_specs=pl.BlockSpec((None, 8, 128), lambda n: (n, 0, 0)),
        out_shape=jax.ShapeDtypeStruct((12, 8, 128), F32),
        compiler_params=_cparams(("parallel",)),
    )(sk, jnp.asarray(oh))
    return out[:, 0, :REL_BUCKETS].T


def _mem_fn(q, k, v):
    s = lax.dot_general((q * (ATT_HD ** -0.5)).astype(BF16), k.astype(BF16), (((2,), (2,)), ((0,), (0,))),
                        preferred_element_type=F32)
    m = lax.stop_gradient(jnp.max(s, axis=-1, keepdims=True))
    p = jnp.exp(s - m)
    p = p / jnp.sum(p, axis=-1, keepdims=True)
    return lax.dot_general(p.astype(BF16), v.astype(BF16), (((2,), (1,)), ((0,), (0,))), preferred_element_type=F32)


def _mem_specs(tb, ml):
    qs = pl.BlockSpec((MEM_HEADS, tb, ATT_HD), lambda i: (0, i, 0))
    ks = pl.BlockSpec((MEM_HEADS, ml, ATT_HD), lambda i: (0, 0, 0))
    return qs, ks


def _mem_fwd(q, k, v, tb, name):
    qs, ks = _mem_specs(tb, k.shape[1])

    def body(q_ref, k_ref, v_ref, o_ref):
        o_ref[...] = _mem_fn(q_ref[...].astype(F32), k_ref[...], v_ref[...])

    return pl.pallas_call(
        body, name=name, grid=(q.shape[1] // tb,),
        in_specs=[qs, ks, ks], out_specs=qs, out_shape=jax.ShapeDtypeStruct(q.shape, F32),
        compiler_params=_cparams(("parallel",)),
    )(q, k, v)


def _mem_bwd(q, k, v, do, tb, name):
    qs, ks = _mem_specs(tb, k.shape[1])

    def body(q_ref, k_ref, v_ref, do_ref, dq_ref, dk_ref, dv_ref):
        i = pl.program_id(0)
        _, vjp = jax.vjp(_mem_fn, q_ref[...].astype(F32), k_ref[...], v_ref[...])
        dq, dk, dv = vjp(do_ref[...])
        dq_ref[...] = dq

        @pl.when(i == 0)
        def _():
            dk_ref[...] = dk
            dv_ref[...] = dv

        @pl.when(i > 0)
        def _():
            dk_ref[...] += dk
            dv_ref[...] += dv

    return pl.pallas_call(
        body, name=name, grid=(q.shape[1] // tb,),
        in_specs=[qs, ks, ks, qs], out_specs=[qs, ks, ks],
        out_shape=[jax.ShapeDtypeStruct(q.shape, F32), jax.ShapeDtypeStruct(k.shape, F32),
                   jax.ShapeDtypeStruct(k.shape, F32)],
        compiler_params=_cparams(("arbitrary",)),
    )(q, k, v, do)


CONV_PAD = 8


def _conv_post(kind, acc):
    s = _silu(acc)
    if kind == 2:
        return s
    scale = DN_HD ** -0.5 if kind == 0 else 1.0
    return s * lax.rsqrt(jnp.sum(s * s, axis=-1, keepdims=True) + EPS) * scale


def _conv_rows(x_ref, t, start, rt):
    lo = min(max(start, 0), t - rt)
    x = x_ref[pl.ds(lo, rt), :]
    shift = lo - start
    if shift == 0:
        return x
    x = pltpu.roll(x, shift % rt, axis=0)
    row = lax.broadcasted_iota(jnp.int32, x.shape, 0)
    return jnp.where((row >= shift) if shift > 0 else (row < rt + shift), x, 0.0)


def _conv_acc(x_ref, t, w, r0, rt):
    acc = None
    for i in range(DN_CONV):
        term = w[i:i + 1, :] * _conv_rows(x_ref, t, r0 + i - DN_CONV // 2, rt)
        acc = term if acc is None else acc + term
    return acc


def _conv_fwd(x, w8, kind, rt, name):
    t = x.shape[0]

    def body(x_ref, w_ref, o_ref):
        w = w_ref[...]
        for r in range(t // rt):
            o_ref[pl.ds(r * rt, rt), :] = _conv_post(kind, _conv_acc(x_ref, t, w, r * rt, rt))

    return pl.pallas_call(
        body, name=name, grid=(DN_HEADS,),
        in_specs=[pl.BlockSpec((t, DN_HD), lambda j: (0, 6 * kind + j)),
                  pl.BlockSpec((8, DN_HD), lambda j: (0, 6 * kind + j))],
        out_specs=pl.BlockSpec((t, DN_HD), lambda j: (0, j)),
        out_shape=jax.ShapeDtypeStruct((t, TOK_W), F32),
        compiler_params=_cparams(("parallel",)),
    )(x, w8)


def _conv_bwd(x, w8, d_f, d_r, dp, kind, rt, name):
    t = x.shape[0]

    def body(xp_ref, w_ref, df_ref, dr_ref, dp_in, dx_ref, dw_ref, dpad_ref):
        del dp_in
        w = w_ref[...]
        zero = jnp.zeros((CONV_PAD, DN_HD), F32)
        dpad_ref[pl.ds(0, CONV_PAD), :] = zero
        dpad_ref[pl.ds(CONV_PAD + t, CONV_PAD), :] = zero
        dw = [jnp.zeros((1, DN_HD), F32) for _ in range(DN_CONV)]
        for r in range(t // rt):
            rows = pl.ds(r * rt, rt)
            acc = _conv_acc(xp_ref, t, w, r * rt, rt)
            _, vjp = jax.vjp(functools.partial(_conv_post, kind), acc)
            (dacc,) = vjp(df_ref[rows, :] + dr_ref[rows, :])
            dpad_ref[pl.ds(CONV_PAD + r * rt, rt), :] = dacc
            for i in range(DN_CONV):
                xs = _conv_rows(xp_ref, t, r * rt + i - DN_CONV // 2, rt)
                dw[i] = dw[i] + jnp.sum(dacc * xs, axis=0, keepdims=True)
        dw_ref[...] = jnp.concatenate(dw + [jnp.zeros((8 - DN_CONV, DN_HD), F32)], axis=0)
        for r in range(t // rt):
            acc = None
            for i in range(DN_CONV):
                term = w[i:i + 1, :] * dpad_ref[pl.ds(CONV_PAD + r * rt - i + DN_CONV // 2, rt), :]
                acc = term if acc is None else acc + term
            dx_ref[pl.ds(r * rt, rt), :] = acc.astype(dx_ref.dtype)

    return pl.pallas_call(
        body, name=name, grid=(DN_HEADS,),
        in_specs=[pl.BlockSpec((t, DN_HD), lambda j: (0, 6 * kind + j)),
                  pl.BlockSpec((8, DN_HD), lambda j: (0, 6 * kind + j)),
                  pl.BlockSpec((t, DN_HD), lambda j: (0, j)),
                  pl.BlockSpec((t, DN_HD), lambda j: (0, j)),
                  pl.BlockSpec(memory_space=pl.ANY)],
        out_specs=[pl.BlockSpec((t, DN_HD), lambda j: (0, 6 * kind + j)),
                   pl.BlockSpec((8, DN_HD), lambda j: (0, j))],
        out_shape=[jax.ShapeDtypeStruct(dp.shape, dp.dtype), jax.ShapeDtypeStruct((8, TOK_W), F32)],
        input_output_aliases={4: 0},
        scratch_shapes=[pltpu.VMEM((t + 2 * CONV_PAD, DN_HD), F32)],
        compiler_params=_cparams(("parallel",)),
    )(x, w8, d_f, d_r, dp)


def _softplus(x):
    e = jnp.exp(-jnp.abs(x))
    return jnp.maximum(x, 0.0) + jnp.where(e < 1e-4, e - 0.5 * e * e, jnp.log(1.0 + e))


_NN = (((2,), (1,)), ((0,), (0,)))
_NT = (((2,), (2,)), ((0,), (0,)))
_TN = (((1,), (1,)), ((0,), (0,)))


def _dot(a, b, dims=_NN):
    return lax.dot_general(a.astype(BF16), b.astype(BF16), dims, preferred_element_type=F32)


def _hi_lo(x):
    hi = x.astype(BF16)
    return hi, (x - hi.astype(F32)).astype(BF16)


def _mask_dot(mask_bf16, x, dims):
    x1 = x.astype(BF16)
    r = x - x1.astype(F32)
    x2, x3 = _hi_lo(r)
    d = functools.partial(lax.dot_general, dimension_numbers=dims, preferred_element_type=F32)
    return d(mask_bf16, x1) + d(mask_bf16, x2) + d(mask_bf16, x3)


@jax.custom_vjp
def _dot_mask(mask_bf16, x):
    return _mask_dot(mask_bf16, x, _NN)


def _dot_mask_fwd(mask_bf16, x):
    return _mask_dot(mask_bf16, x, _NN), mask_bf16


def _dot_mask_bwd(mask_bf16, ct):
    return jnp.zeros_like(mask_bf16), _mask_dot(mask_bf16, ct, _TN)


_dot_mask.defvjp(_dot_mask_fwd, _dot_mask_bwd)


def _unit_solve_pass(lmat, rhs, masks):
    ainv = masks[6] - lmat * masks[0]
    for sh in range(1, 6):
        ainv = ainv - _dot(_dot(ainv, lmat * masks[sh]), ainv)
    return _dot(ainv, rhs), ainv


@jax.custom_vjp
def _unit_solve(lmat, rhs, masks):
    return _unit_solve_pass(lmat, rhs, masks)[0]


def _unit_solve_fwd(lmat, rhs, masks):
    sol, ainv = _unit_solve_pass(lmat, rhs, masks)
    return sol, (sol, ainv, masks)


def _unit_solve_bwd(res, ct):
    sol, ainv, masks = res
    d_rhs = _dot(ainv, ct, _TN)
    return -_dot(d_rhs, sol, _NT), d_rhs, tuple(jnp.zeros_like(m) for m in masks)


_unit_solve.defvjp(_unit_solve_fwd, _unit_solve_bwd)


def _block_masks(rev, row, col):
    c = DN_CHUNK
    prow = jnp.where(rev, c - 1 - row, row)
    pcol = jnp.where(rev, c - 1 - col, col)
    masks = []
    for sh in range(6):
        differ = (prow ^ pcol) >> sh
        miss = (differ ^ 1) + (1 - ((prow >> sh) & 1))
        masks.append(jnp.where(miss == 0, 1.0, 0.0))
    masks.append(jnp.where(row == col, 1.0, 0.0))
    return tuple(masks)


def _dn_chunk(q, k, v, al, be, alc, a_row, dt_row, a_rowc, dt_rowc, s):
    n, c = q.shape[0], DN_CHUNK
    rev = lax.broadcasted_iota(jnp.int32, (n, c, c), 0) >= n // 2
    row = lax.broadcasted_iota(jnp.int32, (n, c, c), 1)
    col = lax.broadcasted_iota(jnp.int32, (n, c, c), 2)
    ahead = jnp.where(rev, col - row, row - col)
    incl = ahead >= 0
    strict = ahead > 0
    incl_b = incl.astype(BF16)

    g = -jnp.exp(a_row) * _softplus(al + dt_row)
    beta = _sigmoid(be)
    del alc, a_rowc, dt_rowc
    gc = _dot_mask(incl_b, g)
    gcc = gc[:, :, :c]
    decay = jnp.exp(jnp.where(incl, gcc - jnp.swapaxes(gcc, 1, 2), NEG))
    kb = k * beta
    lmat = jnp.where(strict, _dot(kb, k, _NT) * decay, 0.0)
    rhs = jnp.concatenate([v * beta, kb * jnp.exp(gc)], axis=2)
    sol = _unit_solve(lmat, rhs, _block_masks(rev, row, col))
    u, w = sol[:, :, :DN_HD], sol[:, :, DN_HD:]
    intra = jnp.where(incl, _dot(q, k, _NT) * decay, 0.0)
    v_new = u - _dot(w, s)
    out = _dot(q * jnp.exp(gc), s) + _dot(intra, v_new)
    g_last = jnp.sum(g, axis=1, keepdims=True)
    s_new = s * jnp.exp(g_last) + _dot(k * jnp.exp(g_last - gc), v_new, _TN)
    return out, s_new


DN_HG = 6


def _dn_load(f_refs, r_refs, alf, bef, alr, ber, a_ref, dt_ref):
    c, hg = DN_CHUNK, DN_HG
    sls = [slice(DN_HD * h, DN_HD * (h + 1)) for h in range(hg)]
    toks = [jnp.stack([f[:, sl] for sl in sls] + [r[:, sl] for sl in sls]) for f, r in zip(f_refs, r_refs)]
    al = jnp.concatenate([alf[...], alr[...]], axis=0)
    be = jnp.concatenate([bef[...], ber[...]], axis=0)
    alc = jnp.concatenate([alf[:, :, 0:c], alr[:, :, 0:c]], axis=0)
    a = jnp.concatenate([a_ref[0], a_ref[1]], axis=0)
    dt = jnp.concatenate([dt_ref[0], dt_ref[1]], axis=0)
    ac = jnp.concatenate([a_ref[0, :, :, 0:c], a_ref[1, :, :, 0:c]], axis=0)
    dtc = jnp.concatenate([dt_ref[0, :, :, 0:c], dt_ref[1, :, :, 0:c]], axis=0)
    return toks, (al, be, alc, a, dt, ac, dtc)


def _dn_views(nc, bwd):
    c, hg = DN_CHUNK, DN_HG
    if bwd:
        f_blk = lambda s: nc - 1 - s
        r_blk = lambda s: s
        st_blk = lambda s: nc - 1 - s
    else:
        f_blk = lambda s: s
        r_blk = lambda s: nc - 1 - s
        st_blk = lambda s: s
    tok_f = pl.BlockSpec((c, hg * DN_HD), lambda g, s: (f_blk(s), g))
    tok_r = pl.BlockSpec((c, hg * DN_HD), lambda g, s: (r_blk(s), g))
    gate_f = pl.BlockSpec((None, hg, c, DN_HD), lambda g, s: (0, g, f_blk(s), 0))
    gate_r = pl.BlockSpec((None, hg, c, DN_HD), lambda g, s: (1, g, r_blk(s), 0))
    par = pl.BlockSpec((2, hg, 1, DN_HD), lambda g, s: (0, g, 0, 0))
    state = pl.BlockSpec((2, hg, None, DN_HD, DN_HD), lambda g, s: (0, g, st_blk(s), 0, 0))
    return tok_f, tok_r, gate_f, gate_r, par, state


def _dn_fwd(q, k, v, al, be, a_rows, dt_rows, name):
    t = q.shape[0]
    c, hg = DN_CHUNK, DN_HG
    nc = t // c
    tok_f, tok_r, gate_f, gate_r, par, state = _dn_views(nc, False)

    def body(qf, kf, vf, qr, kr, vr, alf, bef, alr, ber, a_ref, dt_ref, of_ref, or_ref, st_ref, s_ref):
        @pl.when(pl.program_id(1) == 0)
        def _():
            s_ref[...] = jnp.zeros_like(s_ref)

        (q_, k_, v_), gates = _dn_load((qf, kf, vf), (qr, kr, vr), alf, bef, alr, ber, a_ref, dt_ref)
        s = s_ref[...]
        st_ref[0] = s[:hg]
        st_ref[1] = s[hg:]
        out, s_new = _dn_chunk(q_, k_, v_, *gates, s)
        for h in range(hg):
            sl = slice(DN_HD * h, DN_HD * (h + 1))
            of_ref[:, sl] = out[h]
            or_ref[:, sl] = out[hg + h]
        s_ref[...] = s_new

    return pl.pallas_call(
        body, name=name, grid=(DN_HEADS // hg, nc),
        in_specs=[tok_f] * 3 + [tok_r] * 3 + [gate_f, gate_f, gate_r, gate_r, par, par],
        out_specs=[tok_f, tok_r, state],
        out_shape=[jax.ShapeDtypeStruct((t, TOK_W), F32)] * 2
        + [jax.ShapeDtypeStruct((2, DN_HEADS, nc, DN_HD, DN_HD), F32)],
        scratch_shapes=[pltpu.VMEM((2 * hg, DN_HD, DN_HD), F32)],
        compiler_params=_cparams(("parallel", "arbitrary")),
    )(q, k, v, q, k, v, al, be, al, be, a_rows, dt_rows)


def _dn_bwd(q, k, v, al, be, a_rows, dt_rows, states, do, name):
    t = q.shape[0]
    c, hg = DN_CHUNK, DN_HG
    assert hg == DN_HEADS
    nc = t // c
    tok_f, tok_r, gate_f, gate_r, par, state = _dn_views(nc, True)
    gout_f = pl.BlockSpec((c, DN_HD), lambda g, s: (nc - 1 - s, 0))
    gout_r = pl.BlockSpec((c, DN_HD), lambda g, s: (s, 0))

    def body(qf, kf, vf, qr, kr, vr, alf, bef, alr, ber, a_ref, dt_ref, st_ref, dof, dor,
             dqf, dkf, dvf, dqr, dkr, dvr, dgf, dgr, da_ref, ddt_ref, ds_ref):
        first = pl.program_id(1) == 0

        @pl.when(first)
        def _():
            ds_ref[...] = jnp.zeros_like(ds_ref)
            da_ref[...] = jnp.zeros_like(da_ref)
            ddt_ref[...] = jnp.zeros_like(ddt_ref)

        def lanes(x):
            return jnp.sum(x, axis=-1, keepdims=True)

        (q_, k_, v_, do_), gates = _dn_load((qf, kf, vf, dof), (qr, kr, vr, dor), alf, bef, alr, ber, a_ref, dt_ref)
        s = jnp.concatenate([st_ref[0], st_ref[1]], axis=0)
        _, vjp = jax.vjp(_dn_chunk, q_, k_, v_, *gates, s)
        dq, dk, dv, dal, dbe, dalc, da, ddt, dac, ddtc, ds = vjp((do_, ds_ref[...]))
        for h in range(hg):
            sl = slice(DN_HD * h, DN_HD * (h + 1))
            dqf[:, sl], dkf[:, sl], dvf[:, sl] = dq[h], dk[h], dv[h]
            dqr[:, sl], dkr[:, sl], dvr[:, sl] = dq[hg + h], dk[hg + h], dv[hg + h]
        dal, dbe = lanes(dal) + lanes(dalc), lanes(dbe)
        lane = lax.broadcasted_iota(jnp.int32, (c, DN_HD), 1)
        for d, dg_ref in enumerate((dgf, dgr)):
            dg = jnp.zeros((c, DN_HD), F32)
            for h in range(hg):
                dg = jnp.where(lane == h, dal[d * hg + h], jnp.where(lane == hg + h, dbe[d * hg + h], dg))
            dg_ref[...] = dg
        da = jnp.broadcast_to(lanes(da) + lanes(dac), da.shape)
        ddt = jnp.broadcast_to(lanes(ddt) + lanes(ddtc), ddt.shape)
        da_ref[0] += da[:hg]
        da_ref[1] += da[hg:]
        ddt_ref[0] += ddt[:hg]
        ddt_ref[1] += ddt[hg:]
        ds_ref[...] = ds

    tok = jax.ShapeDtypeStruct((t, TOK_W), F32)
    gate = jax.ShapeDtypeStruct((t, DN_HD), F32)
    parsh = jax.ShapeDtypeStruct((2, DN_HEADS, 1, DN_HD), F32)
    res = pl.pallas_call(
        body, name=name, grid=(DN_HEADS // hg, nc),
        in_specs=[tok_f] * 3 + [tok_r] * 3 + [gate_f, gate_f, gate_r, gate_r, par, par, state, tok_f, tok_r],
        out_specs=[tok_f] * 3 + [tok_r] * 3 + [gout_f, gout_r, par, par],
        out_shape=[tok] * 6 + [gate] * 2 + [parsh] * 2,
        scratch_shapes=[pltpu.VMEM((2 * hg, DN_HD, DN_HD), F32)],
        compiler_params=_cparams(("parallel", "arbitrary")),
    )(q, k, v, q, k, v, al, be, al, be, a_rows, dt_rows, states, do, do)
    dqf, dkf, dvf, dqr, dkr, dvr, dgf, dgr, da, ddt = res
    dgate = jnp.concatenate([dgf[:, :2 * DN_HEADS], dgr[:, :2 * DN_HEADS]], axis=1)
    return (dqf, dkf, dvf), (dqr, dkr, dvr), dgate, da, ddt


BAND_BQ = 256
ROW_TB = 512
MEM_TB = 512
CONV_RT = 512


def _to_sub(x, dil):
    l = x.shape[0] // dil
    return x.reshape(l, dil, 4, ATT_HD).transpose(1, 2, 0, 3)


def _from_sub(x, dil):
    l = x.shape[2]
    return x.transpose(2, 0, 1, 3).reshape(l * dil, 4 * ATT_HD)


def _sub_in(x, col_block, dil, pad, dtype, name):
    if dil > 1:
        y = _to_sub(x[:, 256 * col_block:256 * (col_block + 1)], dil).astype(dtype)
        return jnp.pad(y, ((0, 0), (0, 0), (pad, pad), (0, 0))) if pad else y
    t = x.shape[0]

    def body(x_ref, o_ref):
        if pad:
            zero = jnp.zeros((4, pad, ATT_HD), dtype)
            o_ref[0, :, 0:pad, :] = zero
            o_ref[0, :, pad + t:pad + t + pad, :] = zero
        for h in range(4):
            o_ref[0, h, pad:pad + t, :] = x_ref[:, ATT_HD * h:ATT_HD * (h + 1)].astype(dtype)

    return pl.pallas_call(
        body, name=name, grid=(1,), in_specs=[pl.BlockSpec((t, 256), lambda i: (0, col_block))],
        out_specs=pl.BlockSpec((1, 4, t + 2 * pad, ATT_HD), lambda i: (0, 0, 0, 0)),
        out_shape=jax.ShapeDtypeStruct((1, 4, t + 2 * pad, ATT_HD), dtype), compiler_params=_cparams(("arbitrary",)),
    )(x)


def _sub_out(x, dil, pad, dtype, name):
    if dil > 1:
        return _from_sub(x[:, :, pad:x.shape[2] - pad] if pad else x, dil).astype(dtype)
    t = x.shape[2] - 2 * pad

    def body(x_ref, o_ref):
        for h in range(4):
            o_ref[:, ATT_HD * h:ATT_HD * (h + 1)] = x_ref[0, h, pad:pad + t, :].astype(dtype)

    return pl.pallas_call(
        body, name=name, grid=(1,), in_specs=[pl.BlockSpec(x.shape, lambda i: (0, 0, 0, 0))],
        out_specs=pl.BlockSpec((t, 256), lambda i: (0, 0)), out_shape=jax.ShapeDtypeStruct((t, 256), dtype),
        compiler_params=_cparams(("arbitrary",)),
    )(x)


def _heads_major(x):
    return x.reshape(x.shape[0], MEM_HEADS, ATT_HD).transpose(1, 0, 2)


def _heads_minor(x):
    return x.transpose(1, 0, 2).reshape(x.shape[1], MEM_HEADS * ATT_HD)


def _mem_kv_fwd(mem, gain, w_kv, li):
    (memn,) = _rowwise(_fn_pre, [mem], [gain], [(D, BF16)], mem.shape[0], f"memnorm_fwd{li}")
    kv = _matmul(memn, w_kv, "nn", F32, f"memkv_fwd{li}")
    return _heads_major(kv[:, :MEM_W]), _heads_major(kv[:, MEM_W:]), memn


def _mem_kv_bwd(mem, gain, w_kv, memn, dkm, dvm, li):
    dkv = jnp.concatenate([_heads_minor(dkm), _heads_minor(dvm)], axis=1).astype(BF16)
    dw = _matmul(memn, dkv, "tn", BF16, f"memkv_dw{li}")
    dmemn = _matmul(dkv, w_kv, "nt", F32, f"memkv_dx{li}")
    _, (dgain,) = _rowwise_bwd(_fn_pre, [mem], [gain], [dmemn], [None], mem.shape[0], f"memnorm_bwd{li}")
    return dw, dgain


def _attn_mixer_fwd(p, rel_bias, kv_fn):
    t = p.shape[0]
    saved, outs, lses = [], [], []
    for gi, (_, dil) in enumerate(DIL_GROUPS):
        l_sub = t // dil
        bq = min(BAND_BQ, l_sub)
        q = _sub_in(p, gi, dil, 0, p.dtype, f"sub_q{gi}")
        k = _sub_in(p, 3 + gi, dil, BAND_HALF, p.dtype, f"sub_k{gi}")
        v = _sub_in(p, 6 + gi, dil, BAND_HALF, p.dtype, f"sub_v{gi}")
        bm = _band_bias(rel_bias, gi, dil, bq)
        o, lse = _band_fwd(q, k, v, bm, dil, l_sub, bq, f"band_fwd{gi}")
        outs.append(_sub_out(o, dil, 0, F32, f"sub_o{gi}"))
        lses.append(_sub_out(lse, dil, 0, F32, f"sub_lse{gi}"))
        saved.append((q, k, v, bm))
    o_all = jnp.concatenate(outs, axis=1)
    lse_all = jnp.concatenate(lses, axis=1)
    (mixed,) = _rowwise(_fn_combine, [o_all, lse_all], [], [(TOK_W, BF16)], ROW_TB, "combine_fwd")
    qm = _heads_major(p[:, 3 * TOK_W:])
    km, vm, memn = kv_fn(mixed)
    memo = _mem_fwd(qm, km, vm, min(MEM_TB, t), "mem_fwd0")
    cat = jnp.concatenate([mixed, _heads_minor(memo).astype(BF16)], axis=1)
    return cat, (saved, o_all, lse_all, qm), (km, vm, memn)


def _attn_mixer_bwd(dcat, res, km, vm):
    saved, o_all, lse_all, qm = res
    t = dcat.shape[0]
    (do_all, dlse_all), _ = _rowwise_bwd(_fn_combine, [o_all, lse_all], [], [_Cols(dcat, TOK_W, 0)], [BF16, F32],
                                         ROW_TB, "combine_bwd")
    dqs, dks, dvs, dbms = [], [], [], []
    for gi, (_, dil) in enumerate(DIL_GROUPS):
        l_sub = t // dil
        bq = min(BAND_BQ, l_sub)
        q, k, v, bm = saved[gi]
        do = _sub_in(do_all, gi, dil, 0, do_all.dtype, f"sub_do{gi}")
        dl = _sub_in(dlse_all, gi, dil, 0, F32, f"sub_dlse{gi}")
        dq, dk, dv, dbm = _band_bwd(q, k, v, bm, do, dl, dil, l_sub, bq, f"band_bwd{gi}")
        dqs.append(_sub_out(dq, dil, 0, BF16, f"sub_dq{gi}"))
        dks.append(_sub_out(dk, dil, BAND_HALF, BF16, f"sub_dk{gi}"))
        dvs.append(_sub_out(dv, dil, BAND_HALF, BF16, f"sub_dv{gi}"))
        dbms.append(dbm)
    dqm, dkm, dvm = _mem_bwd(qm, km, vm, _heads_major(dcat[:, TOK_W:]), min(MEM_TB, t), "mem_bwd0")
    dp = jnp.concatenate(dqs + dks + dvs + [_heads_minor(dqm).astype(BF16)], axis=1)
    return dp, _relbias_grad(dbms, "relbias_grad"), dkm, dvm


def _dn_mixer_fwd(p, conv_w, a_log, dt_bias, out_norm, km, vm):
    t = p.shape[0]
    rt = min(CONV_RT, t)
    xp = p
    w8 = jnp.pad(conv_w, ((0, 8 - DN_CONV), (0, 0)))
    q = _conv_fwd(xp, w8, 0, rt, "conv_fwd_q")
    k = _conv_fwd(xp, w8, 1, rt, "conv_fwd_k")
    v = _conv_fwd(xp, w8, 2, rt, "conv_fwd_v")
    gate = p[:, 4 * TOK_W:4 * TOK_W + 4 * DN_HEADS].reshape(t, 2, 2, DN_HEADS)
    bshape = (2, DN_HEADS, t, DN_HD)
    al = jnp.broadcast_to(gate[:, :, 0, :].transpose(1, 2, 0)[..., None], bshape)
    be = jnp.broadcast_to(gate[:, :, 1, :].transpose(1, 2, 0)[..., None], bshape)
    a_rows = jnp.broadcast_to(a_log[:, :, None, None], (2, DN_HEADS, 1, DN_HD))
    dt_rows = jnp.broadcast_to(dt_bias[:, :, None, None], (2, DN_HEADS, 1, DN_HD))
    o_f, o_r, states = _dn_fwd(q, k, v, al, be, a_rows, dt_rows, "dn_fwd")
    gain = out_norm.reshape(1, DN_HD)
    (og,) = _rowwise(_fn_outnorm, [o_f, o_r, _Cols(p, TOK_W, 3)], [gain], [(TOK_W, BF16)], ROW_TB, "outnorm_fwd")
    qm = _heads_major(p[:, 4 * TOK_W + 4 * DN_HEADS:DN_IN])
    memo = _mem_fwd(qm, km, vm, min(MEM_TB, t), "mem_fwd1")
    cat = jnp.concatenate([og, _heads_minor(memo).astype(BF16)], axis=1)
    return cat, (xp, w8, q, k, v, al, be, a_rows, dt_rows, o_f, o_r, states, gain, qm)


def _dn_mixer_bwd(dcat, res, km, vm):
    xp, w8, q, k, v, al, be, a_rows, dt_rows, o_f, o_r, states, gain, qm = res
    t = dcat.shape[0]
    rt = min(CONV_RT, t)
    (do, dz), (dgain,) = _rowwise_bwd(_fn_outnorm, [o_f, o_r, _Cols(xp, TOK_W, 3)], [gain],
                                      [_Cols(dcat, TOK_W, 0)], [F32, None, BF16],
                                      ROW_TB, "outnorm_bwd")
    d_f, d_r, dgate, da, ddt = _dn_bwd(q, k, v, al, be, a_rows, dt_rows, states, do, "dn_bwd")
    dqm, dkm, dvm = _mem_bwd(qm, km, vm, _heads_major(dcat[:, TOK_W:]), min(MEM_TB, t), "mem_bwd1")
    rest = jnp.concatenate([dgate.astype(BF16), _heads_minor(dqm).astype(BF16),
                            jnp.zeros((t, DN_IN_PAD - DN_IN), BF16)], axis=1)
    dp = lax.dynamic_update_slice(lax.empty((t, DN_IN_PAD), BF16), dz, (0, 3 * TOK_W))
    dp = lax.dynamic_update_slice(dp, rest, (0, 4 * TOK_W))
    dws = []
    for kind, nm in enumerate("qkv"):
        dp, dw = _conv_bwd(xp, w8, d_f[kind], d_r[kind], dp, kind, rt, f"conv_bwd_{nm}")
        dws.append(dw)
    dconv = jnp.concatenate(dws, axis=1)[:DN_CONV]
    return dp, dconv, da[:, :, 0, 0], ddt[:, :, 0, 0], dgain.reshape(DN_HD), dkm, dvm


SWI_TB = 256


def _ffn_fwd(h, w_gu_t, w_d, li):
    gu = _matmul(h, w_gu_t, "nt", BF16, f"ffn_gu{li}")
    (a,) = _rowwise(_fn_swiglu, [gu], [], [(D_FF, BF16)], SWI_TB, f"swiglu_fwd{li}")
    return _matmul(a, w_d, "nn", F32, f"ffn_down{li}"), gu, a


def _ffn_bwd(df, h, w_gu_t, w_d, gu, a, li):
    da = _matmul(df, w_d, "nt", BF16, f"ffn_down_dx{li}")
    dwd = _matmul(a, df, "tn", BF16, f"ffn_down_dw{li}")
    (dgu,), _ = _rowwise_bwd(_fn_swiglu, [gu], [], [da], [BF16], SWI_TB, f"swiglu_bwd{li}")
    dh = _matmul(dgu, w_gu_t, "nn", F32, f"ffn_gu_dx{li}")
    dwgu_t = _matmul(dgu, h, "tn", BF16, f"ffn_gu_dw{li}")
    return dh, dwgu_t, dwd


def _fn_first(x, g):
    return x, _rms(x, g)


def _me_xyc():
    return lax.axis_index("x"), lax.axis_index("y"), lax.axis_index("c")


def _flip(coords, k):
    x, y, c = coords
    return (1 - x if k & 4 else x, 1 - y if k & 2 else y, 1 - c if k & 1 else c)


def _index(coords):
    x, y, c = coords
    return 4 * x + 2 * y + c


def _window(ref, axis, size, d):
    idx = [slice(None)] * len(ref.shape)
    idx[axis] = pl.ds(pl.multiple_of(d * size, size), size)
    return ref.at[tuple(idx)]


def _comm_call(body, n, ins, out_shapes, name):
    hbm = pl.BlockSpec(memory_space=pl.ANY)
    return pl.pallas_call(
        body, name=name, in_specs=[hbm] * n, out_specs=[hbm] * n, out_shape=out_shapes,
        scratch_shapes=[pltpu.SemaphoreType.DMA((N_DEV - 1, n)), pltpu.SemaphoreType.DMA((N_DEV - 1, n)),
                        pltpu.SemaphoreType.DMA((n,))],
    )(*ins)


def _run_exchange(n, local, remote, send_sems, recv_sems):
    me = _me_xyc()
    locs = [local(p) for p in range(n)]
    for cp in locs:
        cp.start()
    sends = [remote(k, p, me, _flip(me, k)) for k in range(1, N_DEV) for p in range(n)]
    for cp in sends:
        cp.start()
    for k in range(1, N_DEV):
        for p in range(n):
            remote(k, p, _flip(me, k), me).wait_recv()
    for cp in sends:
        cp.wait_send()
    for cp in locs:
        cp.wait()


def _all_gather(shards, axes, name):
    n = len(shards)
    sizes = [s.shape[a] for s, a in zip(shards, axes)]

    def body(*refs):
        ins, outs = refs[:n], refs[n:2 * n]
        send_sems, recv_sems, loc_sems = refs[2 * n:]
        me = _me_xyc()

        def local(p):
            return pltpu.make_async_copy(ins[p], _window(outs[p], axes[p], sizes[p], _index(me)), loc_sems.at[p])

        def remote(k, p, owner, to):
            return pltpu.make_async_remote_copy(
                src_ref=ins[p], dst_ref=_window(outs[p], axes[p], sizes[p], _index(owner)),
                send_sem=send_sems.at[k - 1, p], recv_sem=recv_sems.at[k - 1, p], device_id=to, device_id_type=MESH)

        _run_exchange(n, local, remote, send_sems, recv_sems)

    def full(s, a):
        return s.shape[:a] + (N_DEV * s.shape[a],) + s.shape[a + 1:]

    return _comm_call(body, n, shards, [jax.ShapeDtypeStruct(full(s, a), s.dtype) for s, a in zip(shards, axes)], name)


def _exchange(fulls, axes, name):
    n = len(fulls)
    sizes = [None if a is None else f.shape[a] // N_DEV for f, a in zip(fulls, axes)]

    def part_shape(f, a):
        return f.shape if a is None else f.shape[:a] + (f.shape[a] // N_DEV,) + f.shape[a + 1:]

    def body(*refs):
        ins, outs = refs[:n], refs[n:2 * n]
        send_sems, recv_sems, loc_sems = refs[2 * n:]
        me = _me_xyc()

        def src(p, to):
            return ins[p] if axes[p] is None else _window(ins[p], axes[p], sizes[p], _index(to))

        def local(p):
            return pltpu.make_async_copy(src(p, me), outs[p].at[_index(me)], loc_sems.at[p])

        def remote(k, p, sender, to):
            return pltpu.make_async_remote_copy(
                src_ref=src(p, to), dst_ref=outs[p].at[_index(sender)],
                send_sem=send_sems.at[k - 1, p], recv_sem=recv_sems.at[k - 1, p], device_id=to, device_id_type=MESH)

        _run_exchange(n, local, remote, send_sems, recv_sems)

    return _comm_call(body, n, fulls,
                      [jax.ShapeDtypeStruct((N_DEV,) + part_shape(f, a), f.dtype) for f, a in zip(fulls, axes)], name)


_HBM = pl.BlockSpec(memory_space=pltpu.HBM)
_SEM = pl.BlockSpec(memory_space=pltpu.SEMAPHORE)
_EFFECT = pltpu.SideEffectType.DATAFLOW_SIDE_EFFECTING


def _in_hbm(a):
    return pltpu.with_memory_space_constraint(a, pltpu.HBM)


def _split_start(srcs, lands, after, descr, name):
    n = len(srcs)

    def body(*refs):
        ins, lnd = refs[:n], refs[n:2 * n]
        send_sems, recv_sems = refs[2 * n + 1], refs[2 * n + 2]
        token = refs[-1]
        me = _me_xyc()
        for k in range(1, N_DEV):
            for p in range(n):
                descr(k, p, ins, lnd, send_sems, recv_sems, me, _flip(me, k)).start()
        token[...] = jnp.zeros_like(token)

    sems = pltpu.SemaphoreType.DMA(((N_DEV - 1) * n,))
    res = pl.pallas_call(
        body, name=name,
        out_shape=(sems, sems, *[pltpu.HBM(a.shape, a.dtype) for a in (*srcs, *lands)],
                   jax.ShapeDtypeStruct((8, 128), F32)),
        in_specs=[_HBM] * (2 * n) + [pl.BlockSpec(memory_space=pl.ANY)],
        out_specs=(_SEM, _SEM, *[_HBM] * (2 * n), pl.BlockSpec(memory_space=pltpu.VMEM)),
        input_output_aliases={i: 2 + i for i in range(2 * n)},
        compiler_params=pltpu.CompilerParams(has_side_effects=_EFFECT),
    )(*[_in_hbm(a) for a in (*srcs, *lands)], after)
    return res[0], res[1], res[2:2 + n], res[2 + n:2 + 2 * n], res[-1]


def _split_wait(send_sems, recv_sems, srcs, lands, after, descr, name):
    n = len(srcs)

    def body(*refs):
        ins, lnd = refs[:n], refs[n:2 * n]
        s_sems, r_sems = refs[2 * n], refs[2 * n + 1]
        me = _me_xyc()
        for k in range(1, N_DEV):
            for p in range(n):
                peer = _flip(me, k)
                descr(k, p, ins, lnd, s_sems, r_sems, me, peer).wait_send()
                descr(k, p, ins, lnd, s_sems, r_sems, peer, me).wait_recv()

    res = pl.pallas_call(
        body, name=name,
        out_shape=tuple(pltpu.HBM(a.shape, a.dtype) for a in (*srcs, *lands)),
        in_specs=[_HBM] * (2 * n) + [_SEM, _SEM, pl.BlockSpec(memory_space=pl.ANY)],
        out_specs=tuple([_HBM] * (2 * n)),
        input_output_aliases={i: i for i in range(2 * n)},
        compiler_params=pltpu.CompilerParams(has_side_effects=_EFFECT),
    )(*srcs, *lands, send_sems, recv_sems, after)
    return list(res[n:])


def _gather_descr(axes, sizes):
    def descr(k, p, ins, lnd, send_sems, recv_sems, sender, dest):
        return pltpu.make_async_remote_copy(
            src_ref=ins[p], dst_ref=_window(lnd[p], axes[p], sizes[p], _index(sender)),
            send_sem=send_sems.at[(k - 1) * len(axes) + p], recv_sem=recv_sems.at[(k - 1) * len(axes) + p],
            device_id=dest, device_id_type=MESH)
    return descr


def _exchange_descr(axes, sizes):
    def descr(k, p, ins, lnd, send_sems, recv_sems, sender, dest):
        return pltpu.make_async_remote_copy(
            src_ref=_window(ins[p], axes[p], sizes[p], _index(dest)), dst_ref=lnd[p].at[_index(sender)],
            send_sem=send_sems.at[(k - 1) * len(axes) + p], recv_sem=recv_sems.at[(k - 1) * len(axes) + p],
            device_id=dest, device_id_type=MESH)
    return descr


def _gather_begin(shards, axes, after, name):
    sizes = [s.shape[a] for s, a in zip(shards, axes)]
    me = _index(_me_xyc())
    lands = []
    for s, a, sz in zip(shards, axes, sizes):
        full = s.shape[:a] + (N_DEV * sz,) + s.shape[a + 1:]
        lands.append(lax.dynamic_update_slice_in_dim(lax.empty(full, s.dtype), s, me * sz, a))
    descr = _gather_descr(axes, sizes)
    send_sems, recv_sems, srcs, lands, token = _split_start(shards, lands, after, descr, name)
    return (send_sems, recv_sems, srcs, lands, descr), token


def _exchange_begin(fulls, axes, after, name):
    sizes = [f.shape[a] // N_DEV for f, a in zip(fulls, axes)]
    me = _index(_me_xyc())
    lands = []
    for f, a, sz in zip(fulls, axes, sizes):
        own = lax.dynamic_slice_in_dim(f, me * sz, sz, a)
        lands.append(lax.dynamic_update_slice_in_dim(lax.empty((N_DEV,) + own.shape, f.dtype), own[None], me, 0))
    descr = _exchange_descr(axes, sizes)
    send_sems, recv_sems, srcs, lands, token = _split_start(fulls, lands, after, descr, name)
    return (send_sems, recv_sems, srcs, lands, descr), token


def _split_end(handle, after, name):
    send_sems, recv_sems, srcs, lands, descr = handle
    return _split_wait(send_sems, recv_sems, srcs, lands, after, descr, name)


def _adam_math(g, w, m, v):
    m = ADAM_B1 * m + (1.0 - ADAM_B1) * g
    v = ADAM_B2 * v + (1.0 - ADAM_B2) * (g * g)
    m_hat = m / (1.0 - ADAM_B1 ** ADAM_STEP)
    v_hat = v / (1.0 - ADAM_B2 ** ADAM_STEP)
    delta = -ADAM_LR * (m_hat / (jnp.sqrt(v_hat) + ADAM_EPS) + ADAM_WD * w)
    return delta, m, v


def _sum_slabs(r_ref):
    g = r_ref[0].astype(F32)
    for s in range(1, N_DEV):
        g = g + r_ref[s].astype(F32)
    return g


def _adamw_reduce(recv, w, m, v, tb, name):
    r, c = w.shape

    def body(r_ref, w_ref, m_ref, v_ref, g_ref, d_ref, nm_ref, nv_ref):
        g = _sum_slabs(r_ref)
        g_ref[...] = g
        d_ref[...], nm_ref[...], nv_ref[...] = _adam_math(g, w_ref[...], m_ref[...], v_ref[...])

    blk = pl.BlockSpec((tb, c), lambda i: (i, 0))
    return pl.pallas_call(
        body, name=name, grid=(r // tb,),
        in_specs=[pl.BlockSpec((N_DEV, tb, c), lambda i: (0, i, 0)), blk, blk, blk],
        out_specs=[blk] * 4, out_shape=[jax.ShapeDtypeStruct((r, c), F32)] * 4,
        compiler_params=_cparams(("parallel",)),
    )(recv, w, m, v)


def _reduce8(recv, tb, name):
    r, c = recv.shape[1:]

    def body(r_ref, g_ref):
        g_ref[...] = _sum_slabs(r_ref)

    return pl.pallas_call(
        body, name=name, grid=(r // tb,),
        in_specs=[pl.BlockSpec((N_DEV, tb, c), lambda i: (0, i, 0))],
        out_specs=pl.BlockSpec((tb, c), lambda i: (i, 0)), out_shape=jax.ShapeDtypeStruct((r, c), F32),
        compiler_params=_cparams(("parallel",)),
    )(recv)


def _adamw(g, w, m, v, tb, name):
    r, c = w.shape

    def body(g_ref, w_ref, m_ref, v_ref, d_ref, nm_ref, nv_ref):
        d_ref[...], nm_ref[...], nv_ref[...] = _adam_math(g_ref[...], w_ref[...], m_ref[...], v_ref[...])

    blk = pl.BlockSpec((tb, c), lambda i: (i, 0))
    return pl.pallas_call(
        body, name=name, grid=(r // tb,), in_specs=[blk] * 4, out_specs=[blk] * 3,
        out_shape=[jax.ShapeDtypeStruct((r, c), F32)] * 3, compiler_params=_cparams(("parallel",)),
    )(g, w, m, v)


DN_IN_SHARD = DN_IN // N_DEV
DN_IN_SHARD_PAD = 432
CONV_SHARD = (1, DN_CONV, 288)


def _pack_small(arrs, rows):
    flat = jnp.concatenate([a.astype(F32).reshape(-1) for a in arrs])
    return jnp.pad(flat, (0, rows * PACK_C - flat.size)).reshape(rows, PACK_C)


def _unpack_small(packed, shapes):
    flat, out, off = packed.reshape(-1), [], 0
    for shp in shapes:
        n = int(np.prod(shp))
        out.append(flat[off:off + n].reshape(shp))
        off += n
    return out


def kernel(x, mem, rel_bias, att_w_in, att_w_out, dn_w_in, dn_conv, dn_a_log, dn_dt_bias, dn_out_norm, dn_w_out, mem_norm, mem_w_kv, norm_mix_pre, norm_mix_post, norm_ffn_pre, norm_ffn_post, ffn_w_gate_up, ffn_w_down, loss_target, m_rel_bias, m_att_w_in, m_att_w_out, m_dn_w_in, m_dn_conv, m_dn_a_log, m_dn_dt_bias, m_dn_out_norm, m_dn_w_out, m_mem_norm, m_mem_w_kv, m_norm_mix_pre, m_norm_mix_post, m_norm_ffn_pre, m_norm_ffn_post, m_ffn_w_gate_up, m_ffn_w_down, v_rel_bias, v_att_w_in, v_att_w_out, v_dn_w_in, v_dn_conv, v_dn_a_log, v_dn_dt_bias, v_dn_out_norm, v_dn_w_out, v_mem_norm, v_mem_w_kv, v_norm_mix_pre, v_norm_mix_post, v_norm_ffn_pre, v_norm_ffn_post, v_ffn_w_gate_up, v_ffn_w_down):
    x0, mem0, tgt = x[0], mem[0], loss_target[0]
    t = x0.shape[0]
    axes = ("x", "y", "c")

    def t_shard(w):
        return jnp.swapaxes(w, 1, 2).astype(BF16)

    dn_in_pad = ((0, 0), (0, DN_IN_SHARD_PAD - DN_IN_SHARD), (0, 0))
    (w_att_in_t,) = _all_gather([t_shard(att_w_in)], [1], "allgather_first")
    w_att_in_t = w_att_in_t[0]
    gu_t, down = t_shard(ffn_w_gate_up), ffn_w_down.astype(BF16)
    gather_o, tok_o = _gather_begin([att_w_out.astype(BF16), mem_w_kv.astype(BF16)], [1, 1], w_att_in_t,
                                    "gather_att_out_start")
    gather_a, tok_a = _gather_begin([gu_t[0:1], down[0:1]], [1, 1], tok_o, "gather_ffn0_start")
    gather_b, tok_b = _gather_begin(
        [jnp.pad(t_shard(dn_w_in), dn_in_pad), dn_w_out.astype(BF16), gu_t[1:2], down[1:2], dn_conv],
        [1, 1, 1, 1, 0], tok_a, "gather_layer1_start")

    def gain(a, i):
        return a[i].reshape(1, D)

    (h0,) = _rowwise(_fn_pre, [x0], [gain(norm_mix_pre, 0) + tok_b[0:1, 0:1]], [(D, BF16)], ROW_TB, "pre0")
    p0 = _matmul(h0, w_att_in_t, "nt", BF16, "att_in")
    late = {}

    def kv0(after):
        late["w_att_out"], late["w_kv"] = _split_end(gather_o, after, "gather_att_out_wait")
        return _mem_kv_fwd(mem0, gain(mem_norm, 0), late["w_kv"][0], 0)

    cat0, res0, (km0, vm0, memn0) = _attn_mixer_fwd(p0, rel_bias, kv0)
    w_att_out, w_kv = late["w_att_out"][0], late["w_kv"]
    y0 = _matmul(cat0, w_att_out, "nn", F32, "att_out")
    g_a = [gain(norm_mix_post, 0), gain(norm_ffn_pre, 0)]
    x1, h1 = _rowwise(_fn_res_pre, [x0, y0], g_a, [(D, F32), (D, BF16)], ROW_TB, "res_pre0")
    w_gu_t0, w_down0 = [w[0] for w in _split_end(gather_a, h1, "gather_ffn0_wait")]
    f0, gu0, a0 = _ffn_fwd(h1, w_gu_t0, w_down0, 0)
    g_b = [gain(norm_ffn_post, 0), gain(norm_mix_pre, 1)]
    x2, h2 = _rowwise(_fn_res_pre, [x1, f0], g_b, [(D, F32), (D, BF16)], ROW_TB, "res_pre1")
    km1, vm1, memn1 = _mem_kv_fwd(mem0, gain(mem_norm, 1), w_kv[1], 1)
    w_dn_in_g, w_dn_out, w_gu_t1, w_down1, conv_g = _split_end(gather_b, h2, "gather_layer1_wait")
    w_dn_in_g, w_dn_out, w_gu_t1, w_down1 = w_dn_in_g[0], w_dn_out[0], w_gu_t1[0], w_down1[0]
    conv_full = conv_g.transpose(1, 0, 2).reshape(DN_CONV, 3 * TOK_W)
    w_dn_in_t = jnp.concatenate(
        [w_dn_in_g[DN_IN_SHARD_PAD * j:DN_IN_SHARD_PAD * j + DN_IN_SHARD] for j in range(N_DEV)]
        + [jnp.zeros((DN_IN_PAD - DN_IN, D), BF16)], axis=0)
    p1 = _matmul(h2, w_dn_in_t, "nt", F32, "dn_in")
    cat1, res1 = _dn_mixer_fwd(p1, conv_full, dn_a_log[0], dn_dt_bias[0], dn_out_norm[0], km1, vm1)
    y1 = _matmul(cat1, w_dn_out, "nn", F32, "dn_out")
    g_c = [gain(norm_mix_post, 1), gain(norm_ffn_pre, 1)]
    x3, h3 = _rowwise(_fn_res_pre, [x2, y1], g_c, [(D, F32), (D, BF16)], ROW_TB, "res_pre2")
    f1, gu1, a1 = _ffn_fwd(h3, w_gu_t1, w_down1, 1)
    g_d = [gain(norm_ffn_post, 1)]
    (x4,) = _rowwise(_fn_res, [x3, f1], g_d, [(D, F32)], ROW_TB, "res3")
    dx4, lrow = _loss_kernel(x4, tgt, ROW_TB, "loss")
    loss = lax.psum(lrow[0, 0] * (0.5 / D), axes)

    (df1,), (dg_fpost1,) = _rowwise_bwd(_fn_res, [x3, f1], g_d, [dx4], [None, BF16], ROW_TB, "res3_bwd")
    dh3, dwgu1, dwd1 = _ffn_bwd(df1, h3, w_gu_t1, w_down1, gu1, a1, 1)
    (dx2, dy1), (dg_mpost1, dg_fpre1) = _rowwise_bwd(_fn_res_pre, [x2, y1], g_c, [dx4, dh3], [F32, BF16],
                                                     ROW_TB, "res_pre2_bwd")
    dcat1 = _matmul(dy1, w_dn_out, "nt", F32, "dn_out_dx")
    dw_dn_out = _matmul(cat1, dy1, "tn", BF16, "dn_out_dw")
    dp1, dconv, da_log, ddt_bias, dout_norm, dkm1, dvm1 = _dn_mixer_bwd(dcat1, res1, km1, vm1)
    dwkv1, dg_mem1 = _mem_kv_bwd(mem0, gain(mem_norm, 1), w_kv[1], memn1, dkm1, dvm1, 1)
    dh2 = _matmul(dp1, w_dn_in_t, "nn", F32, "dn_in_dx")
    dw_dn_in_t = _matmul(dp1, h2, "tn", BF16, "dn_in_dw")
    dn_in_parts = [jnp.pad(dw_dn_in_t[DN_IN_SHARD * j:DN_IN_SHARD * (j + 1)],
                           ((0, DN_IN_SHARD_PAD - DN_IN_SHARD), (0, 0))) for j in range(N_DEV)]
    xch_b, tok = _exchange_begin(
        [jnp.concatenate(dn_in_parts, axis=0)[None], dw_dn_out[None], dwkv1[None], dwgu1[None], dwd1[None]],
        [1, 1, 1, 1, 1], dh2, "exchange_layer1_start")
    (dx1, df0), (dg_fpost0, dg_mpre1) = _rowwise_bwd(_fn_res_pre, [x1, f0], [g + tok[0:1, 0:1] for g in g_b],
                                                     [dx2, dh2], [F32, BF16], ROW_TB, "res_pre1_bwd")
    dh1, dwgu0, dwd0 = _ffn_bwd(df0, h1, w_gu_t0, w_down0, gu0, a0, 0)
    xch_a, tok = _exchange_begin([dwgu0[None], dwd0[None]], [1, 1], dh1, "exchange_ffn0_start")
    (dx0, dy0), (dg_mpost0, dg_fpre0) = _rowwise_bwd(_fn_res_pre, [x0, y0], [g + tok[0:1, 0:1] for g in g_a],
                                                     [dx1, dh1], [F32, BF16], ROW_TB, "res_pre0_bwd")
    dcat0 = _matmul(dy0, w_att_out, "nt", F32, "att_out_dx")
    dw_att_out = _matmul(cat0, dy0, "tn", BF16, "att_out_dw")
    dp0, drel, dkm0, dvm0 = _attn_mixer_bwd(dcat0, res0, km0, vm0)
    dwkv0, dg_mem0 = _mem_kv_bwd(mem0, gain(mem_norm, 0), w_kv[0], memn0, dkm0, dvm0, 0)
    xch_o, tok = _exchange_begin([dw_att_out[None], dwkv0[None]], [1, 1], dp0, "exchange_att_out_start")
    dw_att_in_t = _matmul(dp0, h0, "tn", BF16, "att_in_dw")
    xch_i, tok_i = _exchange_begin([dw_att_in_t[None]], [1], tok, "exchange_att_in_start")
    dh0 = _matmul(dp0, w_att_in_t, "nn", F32, "att_in_dx")
    (grad_x,), (dg_mpre0,) = _rowwise_bwd(_fn_first, [x0], [gain(norm_mix_pre, 0) + tok_i[0:1, 0:1]], [dx0, dh0],
                                          [F32], ROW_TB, "pre0_bwd")

    small_grads = [drel, da_log, ddt_bias, dout_norm, jnp.concatenate([dg_mem0, dg_mem1]),
                   jnp.concatenate([dg_mpre0, dg_mpre1]), jnp.concatenate([dg_mpost0, dg_mpost1]),
                   jnp.concatenate([dg_fpre0, dg_fpre1]), jnp.concatenate([dg_fpost0, dg_fpost1]), dconv]
    (r_small,) = _exchange([_pack_small(small_grads, SMALL_ROWS)], [None], "exchange_last")
    (r_att_in,) = _split_end(xch_i, r_small, "exchange_att_in_wait")
    r_att_out, r_kv0 = _split_end(xch_o, r_small, "exchange_att_out_wait")
    r_gu0, r_down0 = _split_end(xch_a, r_small, "exchange_ffn0_wait")
    r_dn_in, r_dn_out, r_kv1, r_gu1, r_down1 = _split_end(xch_b, r_small, "exchange_layer1_wait")

    def rows(a):
        return a.reshape((-1,) + a.shape[-1:])

    def row_sharded(recv, w, m, v, tb, name):
        outs = _adamw_reduce(recv.reshape((N_DEV, -1) + recv.shape[-1:]), rows(w), rows(m), rows(v), tb, name)
        return [o.reshape(w.shape) for o in outs]

    def col_sharded(recv, w, m, v, tb, name):
        g_t = _reduce8(recv.reshape((N_DEV, -1) + recv.shape[-1:]), tb, name + "_sum")
        g = jnp.swapaxes(g_t.reshape(recv.shape[1:])[:, :w.shape[2]], 1, 2)
        outs = _adamw(rows(g), rows(w), rows(m), rows(v), 256, name)
        return [g] + [o.reshape(w.shape) for o in outs]

    def per_layer(fn, recvs, w, m, v, tb, name):
        outs = [fn(r, w[l:l + 1], m[l:l + 1], v[l:l + 1], tb, f"{name}{l}") for l, r in enumerate(recvs)]
        return [jnp.concatenate(pair, axis=0) for pair in zip(*outs)]

    big = [col_sharded(r_att_in, att_w_in, m_att_w_in, v_att_w_in, 320, "adamw_att_in"),
           row_sharded(r_att_out, att_w_out, m_att_w_out, v_att_w_out, 128, "adamw_att_out"),
           col_sharded(r_dn_in, dn_w_in, m_dn_w_in, v_dn_w_in, 432, "adamw_dn_in"),
           row_sharded(r_dn_out, dn_w_out, m_dn_w_out, v_dn_w_out, 128, "adamw_dn_out"),
           per_layer(row_sharded, [r_kv0, r_kv1], mem_w_kv, m_mem_w_kv, v_mem_w_kv, 128, "adamw_mem_kv"),
           per_layer(col_sharded, [r_gu0, r_gu1], ffn_w_gate_up, m_ffn_w_gate_up, v_ffn_w_gate_up, 176,
                     "adamw_ffn_gu"),
           per_layer(row_sharded, [r_down0, r_down1], ffn_w_down, m_ffn_w_down, v_ffn_w_down, 176,
                     "adamw_ffn_down")]
    g_big, d_big, nm_big, nv_big = [[b[i] for b in big] for i in range(4)]

    g_small = _reduce8(r_small, SMALL_ROWS, "reduce_small")
    rep_shapes = [(32, 12), (1, 2, 6), (1, 2, 6), (1, 128), (2, D), (2, D), (2, D), (2, D), (2, D)]
    *g_rep, g_conv_full = _unpack_small(g_small, rep_shapes + [(DN_CONV, 3 * TOK_W)])
    me = _index(_me_xyc())
    g_conv = lax.dynamic_slice(g_conv_full, (0, me * 288), (DN_CONV, 288)).reshape(CONV_SHARD)
    small_shapes = rep_shapes + [CONV_SHARD]
    small_w = [rel_bias, dn_a_log, dn_dt_bias, dn_out_norm, mem_norm, norm_mix_pre, norm_mix_post,
               norm_ffn_pre, norm_ffn_post, dn_conv]
    small_m = [m_rel_bias, m_dn_a_log, m_dn_dt_bias, m_dn_out_norm, m_mem_norm, m_norm_mix_pre, m_norm_mix_post,
               m_norm_ffn_pre, m_norm_ffn_post, m_dn_conv]
    small_v = [v_rel_bias, v_dn_a_log, v_dn_dt_bias, v_dn_out_norm, v_mem_norm, v_norm_mix_pre, v_norm_mix_post,
               v_norm_ffn_pre, v_norm_ffn_post, v_dn_conv]
    g_small_list = g_rep + [g_conv]
    outs_small = _adamw(_pack_small(g_small_list, 24), _pack_small(small_w, 24), _pack_small(small_m, 24),
                        _pack_small(small_v, 24), 24, "adamw_small")
    d_small, nm_small, nv_small = [_unpack_small(o, small_shapes) for o in outs_small]

    def ordered(small, big):
        return [small[0], big[0], big[1], big[2], small[9], small[1], small[2], small[3], big[3], small[4],
                big[4], small[5], small[6], small[7], small[8], big[5], big[6]]

    g_small_out = [g.reshape(s) for g, s in zip(g_small_list, small_shapes)]
    return (loss, grad_x[None], *ordered(g_small_out, g_big), *ordered(d_small, d_big),
            *ordered(nm_small, nm_big), *ordered(nv_small, nv_big))
```

```python
import functools
import math
from typing import NamedTuple

import numpy as np
import jax
import jax.numpy as jnp
from jax import lax
from jax.experimental import pallas as pl
from jax.experimental.pallas import tpu as pltpu

F32 = jnp.float32
BF16 = jnp.bfloat16
HI = lax.Precision.HIGHEST
MESH = pl.DeviceIdType.MESH

N_DEV = 8
D = 1024
EPS = 1e-6
NEG = -1e30
TOK_W = 768
MEM_W = 256
ATT_HD = 64
DIL_GROUPS = ((128, 1), (512, 4), (2048, 16))
BAND_HALF = 64
REL_BUCKETS = 32
REL_MAX_DIST = 1024
DN_HD = 128
DN_HEADS = 6
DN_CONV = 5
DN_CHUNK = 64
MEM_HEADS = 4
D_FF = 2816
DN_IN = 3352
DN_IN_PAD = 3456

ADAM_LR, ADAM_B1, ADAM_B2, ADAM_EPS, ADAM_WD, ADAM_STEP = 0.001, 0.9, 0.999, 1e-08, 0.01, 10

PACK_C = 512
SMALL_ROWS = 48
VMEM_LIMIT = 48 * 1024 * 1024


def _cparams(sem=None):
    kw = dict(vmem_limit_bytes=VMEM_LIMIT)
    if sem is not None:
        kw["dimension_semantics"] = sem
    return pltpu.CompilerParams(**kw)


def _tile(n, cap):
    if n <= cap:
        return n
    best = None
    for t in range(128, cap + 1, 128):
        if n % t == 0:
            best = t
    assert best is not None, (n, cap)
    return best


def _matmul(a, b, mode, out_dtype, name, tm=1024, tn=1408, tk=None):
    if tk is None:
        tk = 4096 if mode == "tn" else 2816
    if mode == "tn":
        tm = min(tm, 512)
    if mode == "nn":
        (m, kc), (_, n) = a.shape, b.shape
        dims = (((1,), (0,)), ((), ()))
    elif mode == "nt":
        (m, kc), (n, _) = a.shape, b.shape
        dims = (((1,), (1,)), ((), ()))
    else:
        (kc, m), (_, n) = a.shape, b.shape
        dims = (((0,), (0,)), ((), ()))
    tm = m if m <= tm else _tile(m, tm)
    tn = _tile(n, tn)
    tk = _tile(kc, tk)
    nk = kc // tk

    def body(a_ref, b_ref, o_ref, acc_ref):
        k = pl.program_id(2)
        part = lax.dot_general(a_ref[...], b_ref[...], dims, preferred_element_type=F32)

        @pl.when(k == 0)
        def _():
            acc_ref[...] = part

        @pl.when(k > 0)
        def _():
            acc_ref[...] += part

        @pl.when(k == nk - 1)
        def _():
            o_ref[...] = acc_ref[...].astype(o_ref.dtype)

    if mode == "nn":
        a_spec = pl.BlockSpec((tm, tk), lambda i, j, k: (i, k))
        b_spec = pl.BlockSpec((tk, tn), lambda i, j, k: (k, j))
    elif mode == "nt":
        a_spec = pl.BlockSpec((tm, tk), lambda i, j, k: (i, k))
        b_spec = pl.BlockSpec((tn, tk), lambda i, j, k: (j, k))
    else:
        a_spec = pl.BlockSpec((tk, tm), lambda i, j, k: (k, i))
        b_spec = pl.BlockSpec((tk, tn), lambda i, j, k: (k, j))
    return pl.pallas_call(
        body, name=name, grid=(m // tm, n // tn, nk),
        in_specs=[a_spec, b_spec],
        out_specs=pl.BlockSpec((tm, tn), lambda i, j, k: (i, j)),
        out_shape=jax.ShapeDtypeStruct((m, n), out_dtype),
        scratch_shapes=[pltpu.VMEM((tm, tn), F32)],
        compiler_params=_cparams(("parallel", "parallel", "arbitrary")),
    )(a, b)


class _Cols(NamedTuple):
    arr: jax.Array
    width: int
    block: int

    @property
    def shape(self):
        return (self.arr.shape[0], self.width)


def _row_spec(r, tb):
    if isinstance(r, _Cols):
        return pl.BlockSpec((tb, r.width), lambda i, b=r.block: (i, b))
    return pl.BlockSpec((tb, r.shape[1]), lambda i: (i, 0))


def _row_arr(r):
    return r.arr if isinstance(r, _Cols) else r


def _rowwise(fn, rows, params, outs, tb, name):
    t = rows[0].shape[0]
    nr, npar = len(rows), len(params)

    def body(*refs):
        ins = [r[...].astype(F32) for r in refs[:nr + npar]]
        res = fn(*ins)
        for o_ref, r in zip(refs[nr + npar:], res):
            o_ref[...] = r.astype(o_ref.dtype)

    return pl.pallas_call(
        body, name=name, grid=(t // tb,),
        in_specs=[_row_spec(r, tb) for r in rows] + [pl.BlockSpec(p.shape, lambda i: (0, 0)) for p in params],
        out_specs=[pl.BlockSpec((tb, c), lambda i: (i, 0)) for c, _ in outs],
        out_shape=[jax.ShapeDtypeStruct((t, c), dt) for c, dt in outs],
        compiler_params=_cparams(("parallel",)),
    )(*[_row_arr(r) for r in rows], *params)


def _rowwise_bwd(fn, rows, params, cots, row_grad, tb, name):
    t = rows[0].shape[0]
    nr, npar, nc = len(rows), len(params), len(cots)
    want = [i for i, g in enumerate(row_grad) if g is not None]

    def body(*refs):
        ins = [r[...].astype(F32) for r in refs[:nr + npar]]
        cts = tuple(r[...].astype(F32) for r in refs[nr + npar:nr + npar + nc])
        outs = refs[nr + npar + nc:]
        _, vjp = jax.vjp(fn, *ins)
        grads = vjp(cts)
        for o_ref, i in zip(outs[:len(want)], want):
            o_ref[...] = grads[i].astype(o_ref.dtype)
        first = pl.program_id(0) == 0
        for o_ref, g in zip(outs[len(want):], grads[nr:]):
            @pl.when(first)
            def _(o_ref=o_ref, g=g):
                o_ref[...] = g

            @pl.when(jnp.logical_not(first))
            def _(o_ref=o_ref, g=g):
                o_ref[...] += g

    res = pl.pallas_call(
        body, name=name, grid=(t // tb,),
        in_specs=[_row_spec(r, tb) for r in rows] + [pl.BlockSpec(p.shape, lambda i: (0, 0)) for p in params]
        + [_row_spec(c, tb) for c in cots],
        out_specs=[pl.BlockSpec((tb, rows[i].shape[1]), lambda i_: (i_, 0)) for i in want]
        + [pl.BlockSpec(p.shape, lambda i: (0, 0)) for p in params],
        out_shape=[jax.ShapeDtypeStruct(tuple(rows[i].shape), row_grad[i]) for i in want]
        + [jax.ShapeDtypeStruct(p.shape, F32) for p in params],
        compiler_params=_cparams(("arbitrary",)),
    )(*[_row_arr(r) for r in rows], *params, *[_row_arr(c) for c in cots])
    return list(res[:len(want)]), list(res[len(want):])


def _rms(x, g):
    return x * lax.rsqrt(jnp.mean(x * x, axis=-1, keepdims=True) + EPS) * g


def _fn_pre(x, g):
    return (_rms(x, g),)


def _fn_res_pre(x, y, g_post, g_pre):
    x1 = x + _rms(y, g_post)
    return x1, _rms(x1, g_pre)


def _fn_res(x, y, g_post):
    return (x + _rms(y, g_post),)


def _sigmoid(x):
    return 1.0 / (1.0 + jnp.exp(-x))


def _silu(x):
    return x * _sigmoid(x)


def _fn_swiglu(gu):
    return (_silu(gu[:, :D_FF]) * gu[:, D_FF:],)


def _fn_combine(o, lse):
    ls = [lse[:, 256 * g:256 * (g + 1)] for g in range(3)]
    mx = lax.stop_gradient(jnp.maximum(jnp.maximum(ls[0], ls[1]), ls[2]))
    es = [jnp.exp(l - mx) for l in ls]
    inv = 1.0 / (es[0] + es[1] + es[2])
    return (jnp.concatenate([o[:, 256 * g:256 * (g + 1)] * (es[g] * inv) for g in range(3)], axis=1),)


def _fn_outnorm(o_f, o_r, z, gain):
    res = []
    for h in range(DN_HEADS):
        sl = slice(DN_HD * h, DN_HD * (h + 1))
        o = o_f[:, sl] + o_r[:, sl]
        res.append(o * lax.rsqrt(jnp.mean(o * o, axis=-1, keepdims=True) + EPS) * gain * _silu(z[:, sl]))
    return (jnp.concatenate(res, axis=1),)


def _loss_kernel(x, tgt, tb, name):
    t, d = x.shape

    def body(x_ref, t_ref, dx_ref, l_ref, acc_ref):
        i = pl.program_id(0)
        e = x_ref[...] - t_ref[...]
        dx_ref[...] = e * (1.0 / d)
        part = jnp.sum(e * e, axis=0, keepdims=True)

        @pl.when(i == 0)
        def _():
            acc_ref[...] = part

        @pl.when(i > 0)
        def _():
            acc_ref[...] += part

        @pl.when(i == t // tb - 1)
        def _():
            l_ref[...] = jnp.broadcast_to(jnp.sum(acc_ref[...], axis=-1, keepdims=True), (1, 128))

    return pl.pallas_call(
        body, name=name, grid=(t // tb,),
        in_specs=[pl.BlockSpec((tb, d), lambda i: (i, 0))] * 2,
        out_specs=[pl.BlockSpec((tb, d), lambda i: (i, 0)), pl.BlockSpec((1, 128), lambda i: (0, 0))],
        out_shape=[jax.ShapeDtypeStruct((t, d), F32), jax.ShapeDtypeStruct((1, 128), F32)],
        scratch_shapes=[pltpu.VMEM((1, d), F32)],
        compiler_params=_cparams(("arbitrary",)),
    )(x, tgt)


def _band_fn(l_sub, bq, i, q, kw, vw, bm):
    w = bq + 2 * BAND_HALF
    s = lax.dot_general((q * (ATT_HD ** -0.5)).astype(BF16), kw.astype(BF16), (((2,), (2,)), ((0,), (0,))),
                        preferred_element_type=F32) + bm
    kpos = i * bq - BAND_HALF + lax.broadcasted_iota(jnp.int32, (4, bq, w), 2)
    s = jnp.where((kpos >= 0) & (kpos < l_sub), s, NEG)
    m = lax.stop_gradient(jnp.max(s, axis=-1, keepdims=True))
    p = jnp.exp(s - m)
    den = jnp.sum(p, axis=-1, keepdims=True)
    o = lax.dot_general(p.astype(BF16), vw.astype(BF16), (((2,), (1,)), ((0,), (0,))),
                        preferred_element_type=F32) / den
    return o, jnp.broadcast_to(m + jnp.log(den), o.shape)


def _band_specs(l_sub, bq):
    w = bq + 2 * BAND_HALF
    qs = pl.BlockSpec((None, 4, bq, ATT_HD), lambda r, i: (r, 0, i, 0))
    ks = pl.BlockSpec((None, 4, l_sub + 2 * BAND_HALF, ATT_HD), lambda r, i: (r, 0, 0, 0))
    bs = pl.BlockSpec((4, bq, w), lambda r, i: (0, 0, 0))
    return qs, ks, bs


def _band_fwd(q, k, v, bm, dil, l_sub, bq, name):
    w = bq + 2 * BAND_HALF
    qs, ks, bs = _band_specs(l_sub, bq)

    def body(q_ref, k_ref, v_ref, bm_ref, o_ref, l_ref):
        i = pl.program_id(1)
        st = pl.multiple_of(i * bq, bq)
        o, lse = _band_fn(l_sub, bq, i, q_ref[...].astype(F32), k_ref[:, pl.ds(st, w), :].astype(F32),
                          v_ref[:, pl.ds(st, w), :].astype(F32), bm_ref[...])
        o_ref[...] = o
        l_ref[...] = lse

    return pl.pallas_call(
        body, name=name, grid=(dil, l_sub // bq),
        in_specs=[qs, ks, ks, bs], out_specs=[qs, qs],
        out_shape=[jax.ShapeDtypeStruct(q.shape, F32)] * 2,
        compiler_params=_cparams(("parallel", "arbitrary")),
    )(q, k, v, bm)


def _band_bwd(q, k, v, bm, do, dlse, dil, l_sub, bq, name):
    w = bq + 2 * BAND_HALF
    qs, ks, bs = _band_specs(l_sub, bq)

    def body(q_ref, k_ref, v_ref, bm_ref, do_ref, dl_ref, dq_ref, dk_ref, dv_ref, dbm_ref):
        r, i = pl.program_id(0), pl.program_id(1)
        st = pl.multiple_of(i * bq, bq)
        _, vjp = jax.vjp(functools.partial(_band_fn, l_sub, bq, i),
                         q_ref[...].astype(F32), k_ref[:, pl.ds(st, w), :].astype(F32),
                         v_ref[:, pl.ds(st, w), :].astype(F32), bm_ref[...])
        dq, dkw, dvw, dbm = vjp((do_ref[...].astype(F32), dl_ref[...]))
        dq_ref[...] = dq.astype(dq_ref.dtype)

        @pl.when(i == 0)
        def _():
            dk_ref[...] = jnp.zeros_like(dk_ref)
            dv_ref[...] = jnp.zeros_like(dv_ref)

        dk_ref[:, pl.ds(st, w), :] += dkw
        dv_ref[:, pl.ds(st, w), :] += dvw

        @pl.when((i == 0) & (r == 0))
        def _():
            dbm_ref[...] = dbm

        @pl.when((i > 0) | (r > 0))
        def _():
            dbm_ref[...] += dbm

    return pl.pallas_call(
        body, name=name, grid=(dil, l_sub // bq),
        in_specs=[qs, ks, ks, bs, qs, qs], out_specs=[qs, ks, ks, bs],
        out_shape=[jax.ShapeDtypeStruct(q.shape, BF16), jax.ShapeDtypeStruct(k.shape, F32),
                   jax.ShapeDtypeStruct(k.shape, F32), jax.ShapeDtypeStruct(bm.shape, F32)],
        compiler_params=_cparams(("arbitrary", "arbitrary")),
    )(q, k, v, bm, do, dlse)


def _t5_bucket(rel):
    half = REL_BUCKETS // 2
    max_exact = half // 2
    n = np.abs(rel)
    large = max_exact + (np.log(np.maximum(n, 1) / max_exact) / math.log(REL_MAX_DIST / max_exact)
                         * (half - max_exact)).astype(np.int64)
    large = np.minimum(large, half - 1)
    return ((rel > 0) * half + np.where(n < max_exact, n, large)).astype(np.int32)


def _bucket_onehot(dil):
    idx = _t5_bucket(np.arange(-BAND_HALF, BAND_HALF + 1) * dil)
    oh = np.zeros((2 * BAND_HALF + 1, REL_BUCKETS), np.float32)
    oh[np.arange(2 * BAND_HALF + 1), idx] = 1.0
    return oh


def _band_bias(rel_bias, gi, dil, bq):
    w = bq + 2 * BAND_HALF
    nb = 2 * BAND_HALF + 1
    bias = jnp.dot(jnp.asarray(_bucket_onehot(dil)), rel_bias[:, 4 * gi:4 * gi + 4], precision=HI)
    row = jnp.concatenate([bias.T, jnp.full((4, w + 1 - nb), NEG, F32)], axis=1)
    flat = jnp.tile(row, (1, bq))[:, :bq * w]
    return flat.reshape(4, bq, w)


def _relbias_grad(dbms, name):
    nb = 2 * BAND_HALF + 1
    bq = max(d.shape[1] for d in dbms)
    skew = []
    for dbm in dbms:
        bqg, w = dbm.shape[1], dbm.shape[2]
        flat = jnp.pad(dbm.reshape(4, bqg * w), ((0, 0), (0, bqg)))
        skew.append(jnp.pad(flat.reshape(4, bqg, w + 1)[:, :, :nb], ((0, 0), (0, bq - bqg), (0, 256 - nb))))
    sk = jnp.concatenate(skew, axis=0)
    oh = np.zeros((3, 256, 128), np.float32)
    for gi, (_, dil) in enumerate(DIL_GROUPS):
        oh[gi, :2 * BAND_HALF + 1, :REL_BUCKETS] = _bucket_onehot(dil)

    def body(s_ref, oh_ref, o_ref):
        col = jnp.sum(s_ref[...], axis=0, keepdims=True)
        o_ref[...] = jnp.dot(jnp.broadcast_to(col, (8, 256)), oh_ref[...], precision=HI, preferred_element_type=F32)

    out = pl.pallas_call(
        body, name=name, grid=(12,),
        in_specs=[pl.BlockSpec((None, bq, 256), lambda n: (n, 0, 0)),
                  pl.BlockSpec((None, 256, 128), lambda n: (n // 4, 0, 0))],
        out_specs=pl.BlockSpec((None, 8, 128), lambda n: (n, 0, 0)),
        out_shape=jax.ShapeDtypeStruct((12, 8, 128), F32),
        compiler_params=_cparams(("parallel",)),
    )(sk, jnp.asarray(oh))
    return out[:, 0, :REL_BUCKETS].T


def _mem_fn(q, k, v):
    s = lax.dot_general((q * (ATT_HD ** -0.5)).astype(BF16), k.astype(BF16), (((2,), (2,)), ((0,), (0,))),
                        preferred_element_type=F32)
    m = lax.stop_gradient(jnp.max(s, axis=-1, keepdims=True))
    p = jnp.exp(s - m)
    p = p / jnp.sum(p, axis=-1, keepdims=True)
    return lax.dot_general(p.astype(BF16), v.astype(BF16), (((2,), (1,)), ((0,), (0,))), preferred_element_type=F32)


def _mem_specs(tb, ml):
    qs = pl.BlockSpec((MEM_HEADS, tb, ATT_HD), lambda i: (0, i, 0))
    ks = pl.BlockSpec((MEM_HEADS, ml, ATT_HD), lambda i: (0, 0, 0))
    return qs, ks


def _mem_fwd(q, k, v, tb, name):
    qs, ks = _mem_specs(tb, k.shape[1])

    def body(q_ref, k_ref, v_ref, o_ref):
        o_ref[...] = _mem_fn(q_ref[...].astype(F32), k_ref[...], v_ref[...])

    return pl.pallas_call(
        body, name=name, grid=(q.shape[1] // tb,),
        in_specs=[qs, ks, ks], out_specs=qs, out_shape=jax.ShapeDtypeStruct(q.shape, F32),
        compiler_params=_cparams(("parallel",)),
    )(q, k, v)


def _mem_bwd(q, k, v, do, tb, name):
    qs, ks = _mem_specs(tb, k.shape[1])

    def body(q_ref, k_ref, v_ref, do_ref, dq_ref, dk_ref, dv_ref):
        i = pl.program_id(0)
        _, vjp = jax.vjp(_mem_fn, q_ref[...].astype(F32), k_ref[...], v_ref[...])
        dq, dk, dv = vjp(do_ref[...])
        dq_ref[...] = dq

        @pl.when(i == 0)
        def _():
            dk_ref[...] = dk
            dv_ref[...] = dv

        @pl.when(i > 0)
        def _():
            dk_ref[...] += dk
            dv_ref[...] += dv

    return pl.pallas_call(
        body, name=name, grid=(q.shape[1] // tb,),
        in_specs=[qs, ks, ks, qs], out_specs=[qs, ks, ks],
        out_shape=[jax.ShapeDtypeStruct(q.shape, F32), jax.ShapeDtypeStruct(k.shape, F32),
                   jax.ShapeDtypeStruct(k.shape, F32)],
        compiler_params=_cparams(("arbitrary",)),
    )(q, k, v, do)


CONV_PAD = 8


def _conv_post(kind, acc):
    s = _silu(acc)
    if kind == 2:
        return s
    scale = DN_HD ** -0.5 if kind == 0 else 1.0
    return s * lax.rsqrt(jnp.sum(s * s, axis=-1, keepdims=True) + EPS) * scale


def _conv_rows(x_ref, t, start, rt):
    lo = min(max(start, 0), t - rt)
    x = x_ref[pl.ds(lo, rt), :]
    shift = lo - start
    if shift == 0:
        return x
    x = pltpu.roll(x, shift % rt, axis=0)
    row = lax.broadcasted_iota(jnp.int32, x.shape, 0)
    return jnp.where((row >= shift) if shift > 0 else (row < rt + shift), x, 0.0)


def _conv_acc(x_ref, t, w, r0, rt):
    acc = None
    for i in range(DN_CONV):
        term = w[i:i + 1, :] * _conv_rows(x_ref, t, r0 + i - DN_CONV // 2, rt)
        acc = term if acc is None else acc + term
    return acc


def _conv_fwd(x, w8, kind, rt, name):
    t = x.shape[0]

    def body(x_ref, w_ref, o_ref):
        w = w_ref[...]
        for r in range(t // rt):
            o_ref[pl.ds(r * rt, rt), :] = _conv_post(kind, _conv_acc(x_ref, t, w, r * rt, rt))

    return pl.pallas_call(
        body, name=name, grid=(DN_HEADS,),
        in_specs=[pl.BlockSpec((t, DN_HD), lambda j: (0, 6 * kind + j)),
                  pl.BlockSpec((8, DN_HD), lambda j: (0, 6 * kind + j))],
        out_specs=pl.BlockSpec((t, DN_HD), lambda j: (0, j)),
        out_shape=jax.ShapeDtypeStruct((t, TOK_W), F32),
        compiler_params=_cparams(("parallel",)),
    )(x, w8)


def _conv_bwd(x, w8, d_f, d_r, dp, kind, rt, name):
    t = x.shape[0]

    def body(xp_ref, w_ref, df_ref, dr_ref, dp_in, dx_ref, dw_ref, dpad_ref):
        del dp_in
        w = w_ref[...]
        zero = jnp.zeros((CONV_PAD, DN_HD), F32)
        dpad_ref[pl.ds(0, CONV_PAD), :] = zero
        dpad_ref[pl.ds(CONV_PAD + t, CONV_PAD), :] = zero
        dw = [jnp.zeros((1, DN_HD), F32) for _ in range(DN_CONV)]
        for r in range(t // rt):
            rows = pl.ds(r * rt, rt)
            acc = _conv_acc(xp_ref, t, w, r * rt, rt)
            _, vjp = jax.vjp(functools.partial(_conv_post, kind), acc)
            (dacc,) = vjp(df_ref[rows, :] + dr_ref[rows, :])
            dpad_ref[pl.ds(CONV_PAD + r * rt, rt), :] = dacc
            for i in range(DN_CONV):
                xs = _conv_rows(xp_ref, t, r * rt + i - DN_CONV // 2, rt)
                dw[i] = dw[i] + jnp.sum(dacc * xs, axis=0, keepdims=True)
        dw_ref[...] = jnp.concatenate(dw + [jnp.zeros((8 - DN_CONV, DN_HD), F32)], axis=0)
        for r in range(t // rt):
            acc = None
            for i in range(DN_CONV):
                term = w[i:i + 1, :] * dpad_ref[pl.ds(CONV_PAD + r * rt - i + DN_CONV // 2, rt), :]
                acc = term if acc is None else acc + term
            dx_ref[pl.ds(r * rt, rt), :] = acc.astype(dx_ref.dtype)

    return pl.pallas_call(
        body, name=name, grid=(DN_HEADS,),
        in_specs=[pl.BlockSpec((t, DN_HD), lambda j: (0, 6 * kind + j)),
                  pl.BlockSpec((8, DN_HD), lambda j: (0, 6 * kind + j)),
                  pl.BlockSpec((t, DN_HD), lambda j: (0, j)),
                  pl.BlockSpec((t, DN_HD), lambda j: (0, j)),
                  pl.BlockSpec(memory_space=pl.ANY)],
        out_specs=[pl.BlockSpec((t, DN_HD), lambda j: (0, 6 * kind + j)),
                   pl.BlockSpec((8, DN_HD), lambda j: (0, j))],
        out_shape=[jax.ShapeDtypeStruct(dp.shape, dp.dtype), jax.ShapeDtypeStruct((8, TOK_W), F32)],
        input_output_aliases={4: 0},
        scratch_shapes=[pltpu.VMEM((t + 2 * CONV_PAD, DN_HD), F32)],
        compiler_params=_cparams(("parallel",)),
    )(x, w8, d_f, d_r, dp)


def _softplus(x):
    e = jnp.exp(-jnp.abs(x))
    return jnp.maximum(x, 0.0) + jnp.where(e < 1e-4, e - 0.5 * e * e, jnp.log(1.0 + e))


_NN = (((2,), (1,)), ((0,), (0,)))
_NT = (((2,), (2,)), ((0,), (0,)))
_TN = (((1,), (1,)), ((0,), (0,)))


def _dot(a, b, dims=_NN):
    return lax.dot_general(a.astype(BF16), b.astype(BF16), dims, preferred_element_type=F32)


def _hi_lo(x):
    hi = x.astype(BF16)
    return hi, (x - hi.astype(F32)).astype(BF16)


def _mask_dot(mask_bf16, x, dims):
    x1 = x.astype(BF16)
    r = x - x1.astype(F32)
    x2, x3 = _hi_lo(r)
    d = functools.partial(lax.dot_general, dimension_numbers=dims, preferred_element_type=F32)
    return d(mask_bf16, x1) + d(mask_bf16, x2) + d(mask_bf16, x3)


@jax.custom_vjp
def _dot_mask(mask_bf16, x):
    return _mask_dot(mask_bf16, x, _NN)


def _dot_mask_fwd(mask_bf16, x):
    return _mask_dot(mask_bf16, x, _NN), mask_bf16


def _dot_mask_bwd(mask_bf16, ct):
    return jnp.zeros_like(mask_bf16), _mask_dot(mask_bf16, ct, _TN)


_dot_mask.defvjp(_dot_mask_fwd, _dot_mask_bwd)


def _unit_solve_pass(lmat, rhs, masks):
    ainv = masks[6] - lmat * masks[0]
    for sh in range(1, 6):
        ainv = ainv - _dot(_dot(ainv, lmat * masks[sh]), ainv)
    return _dot(ainv, rhs), ainv


@jax.custom_vjp
def _unit_solve(lmat, rhs, masks):
    return _unit_solve_pass(lmat, rhs, masks)[0]


def _unit_solve_fwd(lmat, rhs, masks):
    sol, ainv = _unit_solve_pass(lmat, rhs, masks)
    return sol, (sol, ainv, masks)


def _unit_solve_bwd(res, ct):
    sol, ainv, masks = res
    d_rhs = _dot(ainv, ct, _TN)
    return -_dot(d_rhs, sol, _NT), d_rhs, tuple(jnp.zeros_like(m) for m in masks)


_unit_solve.defvjp(_unit_solve_fwd, _unit_solve_bwd)


def _block_masks(rev, row, col):
    c = DN_CHUNK
    prow = jnp.where(rev, c - 1 - row, row)
    pcol = jnp.where(rev, c - 1 - col, col)
    masks = []
    for sh in range(6):
        differ = (prow ^ pcol) >> sh
        miss = (differ ^ 1) + (1 - ((prow >> sh) & 1))
        masks.append(jnp.where(miss == 0, 1.0, 0.0))
    masks.append(jnp.where(row == col, 1.0, 0.0))
    return tuple(masks)


def _dn_chunk(q, k, v, al, be, a_row, dt_row, s):
    n, c = q.shape[0], DN_CHUNK
    rev = lax.broadcasted_iota(jnp.int32, (n, c, c), 0) >= n // 2
    row = lax.broadcasted_iota(jnp.int32, (n, c, c), 1)
    col = lax.broadcasted_iota(jnp.int32, (n, c, c), 2)
    ahead = jnp.where(rev, col - row, row - col)
    incl = ahead >= 0
    strict = ahead > 0
    incl_b = incl.astype(BF16)

    g = -jnp.exp(a_row) * _softplus(al + dt_row)
    beta = _sigmoid(be)
    gc = _dot_mask(incl_b, g)
    gcc = gc[:, :, :c]
    decay = jnp.exp(jnp.where(incl, gcc - jnp.swapaxes(gcc, 1, 2), NEG))
    kb = k * beta
    lmat = jnp.where(strict, _dot(kb, k, _NT) * decay, 0.0)
    rhs = jnp.concatenate([v * beta, kb * jnp.exp(gc)], axis=2)
    sol = _unit_solve(lmat, rhs, _block_masks(rev, row, col))
    u, w = sol[:, :, :DN_HD], sol[:, :, DN_HD:]
    intra = jnp.where(incl, _dot(q, k, _NT) * decay, 0.0)
    v_new = u - _dot(w, s)
    out = _dot(q * jnp.exp(gc), s) + _dot(intra, v_new)
    g_last = jnp.sum(g, axis=1, keepdims=True)
    s_new = s * jnp.exp(g_last) + _dot(k * jnp.exp(g_last - gc), v_new, _TN)
    return out, s_new


DN_HG = 6


def _dn_load(f_refs, r_refs, alf, bef, alr, ber, a_ref, dt_ref):
    sls = [slice(DN_HD * h, DN_HD * (h + 1)) for h in range(DN_HG)]
    toks = [jnp.stack([f[:, sl] for sl in sls] + [r[:, sl] for sl in sls]) for f, r in zip(f_refs, r_refs)]
    al = jnp.concatenate([alf[...], alr[...]], axis=0)
    be = jnp.concatenate([bef[...], ber[...]], axis=0)
    a = jnp.concatenate([a_ref[0], a_ref[1]], axis=0)
    dt = jnp.concatenate([dt_ref[0], dt_ref[1]], axis=0)
    return toks, (al, be, a, dt)


def _dn_views(nc, bwd):
    c, hg = DN_CHUNK, DN_HG
    if bwd:
        f_blk = lambda s: nc - 1 - s
        r_blk = lambda s: s
        st_blk = lambda s: nc - 1 - s
    else:
        f_blk = lambda s: s
        r_blk = lambda s: nc - 1 - s
        st_blk = lambda s: s
    tok_f = pl.BlockSpec((c, hg * DN_HD), lambda g, s: (f_blk(s), g))
    tok_r = pl.BlockSpec((c, hg * DN_HD), lambda g, s: (r_blk(s), g))
    gate_f = pl.BlockSpec((None, hg, c, DN_HD), lambda g, s: (0, g, f_blk(s), 0))
    gate_r = pl.BlockSpec((None, hg, c, DN_HD), lambda g, s: (1, g, r_blk(s), 0))
    par = pl.BlockSpec((2, hg, 1, DN_HD), lambda g, s: (0, g, 0, 0))
    state = pl.BlockSpec((2, hg, None, DN_HD, DN_HD), lambda g, s: (0, g, st_blk(s), 0, 0))
    return tok_f, tok_r, gate_f, gate_r, par, state


def _dn_fwd(q, k, v, al, be, a_rows, dt_rows, name):
    t = q.shape[0]
    c, hg = DN_CHUNK, DN_HG
    nc = t // c
    tok_f, tok_r, gate_f, gate_r, par, state = _dn_views(nc, False)

    def body(qf, kf, vf, qr, kr, vr, alf, bef, alr, ber, a_ref, dt_ref, of_ref, or_ref, st_ref, s_ref):
        @pl.when(pl.program_id(1) == 0)
        def _():
            s_ref[...] = jnp.zeros_like(s_ref)

        (q_, k_, v_), gates = _dn_load((qf, kf, vf), (qr, kr, vr), alf, bef, alr, ber, a_ref, dt_ref)
        s = s_ref[...]
        st_ref[0] = s[:hg]
        st_ref[1] = s[hg:]
        out, s_new = _dn_chunk(q_, k_, v_, *gates, s)
        for h in range(hg):
            sl = slice(DN_HD * h, DN_HD * (h + 1))
            of_ref[:, sl] = out[h]
            or_ref[:, sl] = out[hg + h]
        s_ref[...] = s_new

    return pl.pallas_call(
        body, name=name, grid=(DN_HEADS // hg, nc),
        in_specs=[tok_f] * 3 + [tok_r] * 3 + [gate_f, gate_f, gate_r, gate_r, par, par],
        out_specs=[tok_f, tok_r, state],
        out_shape=[jax.ShapeDtypeStruct((t, TOK_W), F32)] * 2
        + [jax.ShapeDtypeStruct((2, DN_HEADS, nc, DN_HD, DN_HD), F32)],
        scratch_shapes=[pltpu.VMEM((2 * hg, DN_HD, DN_HD), F32)],
        compiler_params=_cparams(("parallel", "arbitrary")),
    )(q, k, v, q, k, v, al, be, al, be, a_rows, dt_rows)


def _dn_bwd(q, k, v, al, be, a_rows, dt_rows, states, do, name):
    t = q.shape[0]
    c, hg = DN_CHUNK, DN_HG
    assert hg == DN_HEADS
    nc = t // c
    tok_f, tok_r, gate_f, gate_r, par, state = _dn_views(nc, True)
    gout_f = pl.BlockSpec((c, DN_HD), lambda g, s: (nc - 1 - s, 0))
    gout_r = pl.BlockSpec((c, DN_HD), lambda g, s: (s, 0))

    def body(qf, kf, vf, qr, kr, vr, alf, bef, alr, ber, a_ref, dt_ref, st_ref, dof, dor,
             dqf, dkf, dvf, dqr, dkr, dvr, dgf, dgr, da_ref, ddt_ref, ds_ref):
        first = pl.program_id(1) == 0

        @pl.when(first)
        def _():
            ds_ref[...] = jnp.zeros_like(ds_ref)
            da_ref[...] = jnp.zeros_like(da_ref)
            ddt_ref[...] = jnp.zeros_like(ddt_ref)

        def lanes(x):
            return jnp.sum(x, axis=-1, keepdims=True)

        (q_, k_, v_, do_), gates = _dn_load((qf, kf, vf, dof), (qr, kr, vr, dor), alf, bef, alr, ber, a_ref, dt_ref)
        s = jnp.concatenate([st_ref[0], st_ref[1]], axis=0)
        _, vjp = jax.vjp(_dn_chunk, q_, k_, v_, *gates, s)
        dq, dk, dv, dal, dbe, da, ddt, ds = vjp((do_, ds_ref[...]))
        for h in range(hg):
            sl = slice(DN_HD * h, DN_HD * (h + 1))
            dqf[:, sl], dkf[:, sl], dvf[:, sl] = dq[h], dk[h], dv[h]
            dqr[:, sl], dkr[:, sl], dvr[:, sl] = dq[hg + h], dk[hg + h], dv[hg + h]
        dal, dbe = lanes(dal), lanes(dbe)
        lane = lax.broadcasted_iota(jnp.int32, (c, DN_HD), 1)
        for d, dg_ref in enumerate((dgf, dgr)):
            dg = jnp.zeros((c, DN_HD), F32)
            for h in range(hg):
                dg = jnp.where(lane == h, dal[d * hg + h], jnp.where(lane == hg + h, dbe[d * hg + h], dg))
            dg_ref[...] = dg
        da = jnp.broadcast_to(lanes(da), da.shape)
        ddt = jnp.broadcast_to(lanes(ddt), ddt.shape)
        da_ref[0] += da[:hg]
        da_ref[1] += da[hg:]
        ddt_ref[0] += ddt[:hg]
        ddt_ref[1] += ddt[hg:]
        ds_ref[...] = ds

    tok = jax.ShapeDtypeStruct((t, TOK_W), F32)
    gate = jax.ShapeDtypeStruct((t, DN_HD), F32)
    parsh = jax.ShapeDtypeStruct((2, DN_HEADS, 1, DN_HD), F32)
    res = pl.pallas_call(
        body, name=name, grid=(DN_HEADS // hg, nc),
        in_specs=[tok_f] * 3 + [tok_r] * 3 + [gate_f, gate_f, gate_r, gate_r, par, par, state, tok_f, tok_r],
        out_specs=[tok_f] * 3 + [tok_r] * 3 + [gout_f, gout_r, par, par],
        out_shape=[tok] * 6 + [gate] * 2 + [parsh] * 2,
        scratch_shapes=[pltpu.VMEM((2 * hg, DN_HD, DN_HD), F32)],
        compiler_params=_cparams(("parallel", "arbitrary")),
    )(q, k, v, q, k, v, al, be, al, be, a_rows, dt_rows, states, do, do)
    dqf, dkf, dvf, dqr, dkr, dvr, dgf, dgr, da, ddt = res
    dgate = jnp.concatenate([dgf[:, :2 * DN_HEADS], dgr[:, :2 * DN_HEADS]], axis=1)
    return (dqf, dkf, dvf), (dqr, dkr, dvr), dgate, da, ddt


BAND_BQ = 256
ROW_TB = 512
MEM_TB = 512
CONV_RT = 512


def _to_sub(x, dil):
    l = x.shape[0] // dil
    return x.reshape(l, dil, 4, ATT_HD).transpose(1, 2, 0, 3)


def _from_sub(x, dil):
    l = x.shape[2]
    return x.transpose(2, 0, 1, 3).reshape(l * dil, 4 * ATT_HD)


def _sub_in(x, col_block, dil, pad, dtype, name):
    if dil > 1:
        y = _to_sub(x[:, 256 * col_block:256 * (col_block + 1)], dil).astype(dtype)
        return jnp.pad(y, ((0, 0), (0, 0), (pad, pad), (0, 0))) if pad else y
    t = x.shape[0]

    def body(x_ref, o_ref):
        if pad:
            zero = jnp.zeros((4, pad, ATT_HD), dtype)
            o_ref[0, :, 0:pad, :] = zero
            o_ref[0, :, pad + t:pad + t + pad, :] = zero
        for h in range(4):
            o_ref[0, h, pad:pad + t, :] = x_ref[:, ATT_HD * h:ATT_HD * (h + 1)].astype(dtype)

    return pl.pallas_call(
        body, name=name, grid=(1,), in_specs=[pl.BlockSpec((t, 256), lambda i: (0, col_block))],
        out_specs=pl.BlockSpec((1, 4, t + 2 * pad, ATT_HD), lambda i: (0, 0, 0, 0)),
        out_shape=jax.ShapeDtypeStruct((1, 4, t + 2 * pad, ATT_HD), dtype), compiler_params=_cparams(("arbitrary",)),
    )(x)


def _sub_out(x, dil, pad, dtype, name):
    if dil > 1:
        return _from_sub(x[:, :, pad:x.shape[2] - pad] if pad else x, dil).astype(dtype)
    t = x.shape[2] - 2 * pad

    def body(x_ref, o_ref):
        for h in range(4):
            o_ref[:, ATT_HD * h:ATT_HD * (h + 1)] = x_ref[0, h, pad:pad + t, :].astype(dtype)

    return pl.pallas_call(
        body, name=name, grid=(1,), in_specs=[pl.BlockSpec(x.shape, lambda i: (0, 0, 0, 0))],
        out_specs=pl.BlockSpec((t, 256), lambda i: (0, 0)), out_shape=jax.ShapeDtypeStruct((t, 256), dtype),
        compiler_params=_cparams(("arbitrary",)),
    )(x)


def _heads_major(x):
    return x.reshape(x.shape[0], MEM_HEADS, ATT_HD).transpose(1, 0, 2)


def _heads_minor(x):
    return x.transpose(1, 0, 2).reshape(x.shape[1], MEM_HEADS * ATT_HD)


def _mem_kv_fwd(mem, gain, w_kv, li):
    (memn,) = _rowwise(_fn_pre, [mem], [gain], [(D, BF16)], mem.shape[0], f"memnorm_fwd{li}")
    kv = _matmul(memn, w_kv, "nn", F32, f"memkv_fwd{li}")
    return _heads_major(kv[:, :MEM_W]), _heads_major(kv[:, MEM_W:]), memn


def _mem_kv_bwd(mem, gain, w_kv, memn, dkm, dvm, li):
    dkv = jnp.concatenate([_heads_minor(dkm), _heads_minor(dvm)], axis=1).astype(BF16)
    dw = _matmul(memn, dkv, "tn", BF16, f"memkv_dw{li}")
    dmemn = _matmul(dkv, w_kv, "nt", F32, f"memkv_dx{li}")
    _, (dgain,) = _rowwise_bwd(_fn_pre, [mem], [gain], [dmemn], [None], mem.shape[0], f"memnorm_bwd{li}")
    return dw, dgain


def _attn_mixer_fwd(p, rel_bias, kv_fn):
    t = p.shape[0]
    saved, outs, lses = [], [], []
    for gi, (_, dil) in enumerate(DIL_GROUPS):
        l_sub = t // dil
        bq = min(BAND_BQ, l_sub)
        q = _sub_in(p, gi, dil, 0, p.dtype, f"sub_q{gi}")
        k = _sub_in(p, 3 + gi, dil, BAND_HALF, p.dtype, f"sub_k{gi}")
        v = _sub_in(p, 6 + gi, dil, BAND_HALF, p.dtype, f"sub_v{gi}")
        bm = _band_bias(rel_bias, gi, dil, bq)
        o, lse = _band_fwd(q, k, v, bm, dil, l_sub, bq, f"band_fwd{gi}")
        outs.append(_sub_out(o, dil, 0, F32, f"sub_o{gi}"))
        lses.append(_sub_out(lse, dil, 0, F32, f"sub_lse{gi}"))
        saved.append((q, k, v, bm))
    o_all = jnp.concatenate(outs, axis=1)
    lse_all = jnp.concatenate(lses, axis=1)
    (mixed,) = _rowwise(_fn_combine, [o_all, lse_all], [], [(TOK_W, BF16)], ROW_TB, "combine_fwd")
    qm = _heads_major(p[:, 3 * TOK_W:])
    km, vm, memn = kv_fn(mixed)
    memo = _mem_fwd(qm, km, vm, min(MEM_TB, t), "mem_fwd0")
    cat = jnp.concatenate([mixed, _heads_minor(memo).astype(BF16)], axis=1)
    return cat, (saved, o_all, lse_all, qm), (km, vm, memn)


def _attn_mixer_bwd(dcat, res, km, vm):
    saved, o_all, lse_all, qm = res
    t = dcat.shape[0]
    (do_all, dlse_all), _ = _rowwise_bwd(_fn_combine, [o_all, lse_all], [], [_Cols(dcat, TOK_W, 0)], [BF16, F32],
                                         ROW_TB, "combine_bwd")
    dqs, dks, dvs, dbms = [], [], [], []
    for gi, (_, dil) in enumerate(DIL_GROUPS):
        l_sub = t // dil
        bq = min(BAND_BQ, l_sub)
        q, k, v, bm = saved[gi]
        do = _sub_in(do_all, gi, dil, 0, do_all.dtype, f"sub_do{gi}")
        dl = _sub_in(dlse_all, gi, dil, 0, F32, f"sub_dlse{gi}")
        dq, dk, dv, dbm = _band_bwd(q, k, v, bm, do, dl, dil, l_sub, bq, f"band_bwd{gi}")
        dqs.append(_sub_out(dq, dil, 0, BF16, f"sub_dq{gi}"))
        dks.append(_sub_out(dk, dil, BAND_HALF, BF16, f"sub_dk{gi}"))
        dvs.append(_sub_out(dv, dil, BAND_HALF, BF16, f"sub_dv{gi}"))
        dbms.append(dbm)
    dqm, dkm, dvm = _mem_bwd(qm, km, vm, _heads_major(dcat[:, TOK_W:]), min(MEM_TB, t), "mem_bwd0")
    dp = jnp.concatenate(dqs + dks + dvs + [_heads_minor(dqm).astype(BF16)], axis=1)
    return dp, _relbias_grad(dbms, "relbias_grad"), dkm, dvm


def _dn_mixer_fwd(p, conv_w, a_log, dt_bias, out_norm, km, vm):
    t = p.shape[0]
    rt = min(CONV_RT, t)
    xp = p
    w8 = jnp.pad(conv_w, ((0, 8 - DN_CONV), (0, 0)))
    q = _conv_fwd(xp, w8, 0, rt, "conv_fwd_q")
    k = _conv_fwd(xp, w8, 1, rt, "conv_fwd_k")
    v = _conv_fwd(xp, w8, 2, rt, "conv_fwd_v")
    gate = p[:, 4 * TOK_W:4 * TOK_W + 4 * DN_HEADS].reshape(t, 2, 2, DN_HEADS)
    bshape = (2, DN_HEADS, t, DN_HD)
    al = jnp.broadcast_to(gate[:, :, 0, :].transpose(1, 2, 0)[..., None], bshape)
    be = jnp.broadcast_to(gate[:, :, 1, :].transpose(1, 2, 0)[..., None], bshape)
    a_rows = jnp.broadcast_to(a_log[:, :, None, None], (2, DN_HEADS, 1, DN_HD))
    dt_rows = jnp.broadcast_to(dt_bias[:, :, None, None], (2, DN_HEADS, 1, DN_HD))
    o_f, o_r, states = _dn_fwd(q, k, v, al, be, a_rows, dt_rows, "dn_fwd")
    gain = out_norm.reshape(1, DN_HD)
    (og,) = _rowwise(_fn_outnorm, [o_f, o_r, _Cols(p, TOK_W, 3)], [gain], [(TOK_W, BF16)], ROW_TB, "outnorm_fwd")
    qm = _heads_major(p[:, 4 * TOK_W + 4 * DN_HEADS:DN_IN])
    memo = _mem_fwd(qm, km, vm, min(MEM_TB, t), "mem_fwd1")
    cat = jnp.concatenate([og, _heads_minor(memo).astype(BF16)], axis=1)
    return cat, (xp, w8, q, k, v, al, be, a_rows, dt_rows, o_f, o_r, states, gain, qm)


def _dn_mixer_bwd(dcat, res, km, vm):
    xp, w8, q, k, v, al, be, a_rows, dt_rows, o_f, o_r, states, gain, qm = res
    t = dcat.shape[0]
    rt = min(CONV_RT, t)
    (do, dz), (dgain,) = _rowwise_bwd(_fn_outnorm, [o_f, o_r, _Cols(xp, TOK_W, 3)], [gain],
                                      [_Cols(dcat, TOK_W, 0)], [F32, None, BF16],
                                      ROW_TB, "outnorm_bwd")
    d_f, d_r, dgate, da, ddt = _dn_bwd(q, k, v, al, be, a_rows, dt_rows, states, do, "dn_bwd")
    dqm, dkm, dvm = _mem_bwd(qm, km, vm, _heads_major(dcat[:, TOK_W:]), min(MEM_TB, t), "mem_bwd1")
    rest = jnp.concatenate([dgate.astype(BF16), _heads_minor(dqm).astype(BF16),
                            jnp.zeros((t, DN_IN_PAD - DN_IN), BF16)], axis=1)
    dp = lax.dynamic_update_slice(lax.empty((t, DN_IN_PAD), BF16), dz, (0, 3 * TOK_W))
    dp = lax.dynamic_update_slice(dp, rest, (0, 4 * TOK_W))
    dws = []
    for kind, nm in enumerate("qkv"):
        dp, dw = _conv_bwd(xp, w8, d_f[kind], d_r[kind], dp, kind, rt, f"conv_bwd_{nm}")
        dws.append(dw)
    dconv = jnp.concatenate(dws, axis=1)[:DN_CONV]
    return dp, dconv, da[:, :, 0, 0], ddt[:, :, 0, 0], dgain.reshape(DN_HD), dkm, dvm


SWI_TB = 256


def _ffn_fwd(h, w_gu_t, w_d, li):
    gu = _matmul(h, w_gu_t, "nt", BF16, f"ffn_gu{li}")
    (a,) = _rowwise(_fn_swiglu, [gu], [], [(D_FF, BF16)], SWI_TB, f"swiglu_fwd{li}")
    return _matmul(a, w_d, "nn", F32, f"ffn_down{li}"), gu, a


def _ffn_bwd(df, h, w_gu_t, w_d, gu, a, li):
    da = _matmul(df, w_d, "nt", BF16, f"ffn_down_dx{li}")
    dwd = _matmul(a, df, "tn", BF16, f"ffn_down_dw{li}")
    (dgu,), _ = _rowwise_bwd(_fn_swiglu, [gu], [], [da], [BF16], SWI_TB, f"swiglu_bwd{li}")
    dh = _matmul(dgu, w_gu_t, "nn", F32, f"ffn_gu_dx{li}")
    dwgu_t = _matmul(dgu, h, "tn", BF16, f"ffn_gu_dw{li}")
    return dh, dwgu_t, dwd


def _fn_first(x, g):
    return x, _rms(x, g)


def _me_xyc():
    return lax.axis_index("x"), lax.axis_index("y"), lax.axis_index("c")


def _flip(coords, k):
    x, y, c = coords
    return (1 - x if k & 4 else x, 1 - y if k & 2 else y, 1 - c if k & 1 else c)


def _index(coords):
    x, y, c = coords
    return 4 * x + 2 * y + c


def _window(ref, axis, size, d):
    idx = [slice(None)] * len(ref.shape)
    idx[axis] = pl.ds(pl.multiple_of(d * size, size), size)
    return ref.at[tuple(idx)]


def _comm_call(body, n, ins, out_shapes, name):
    hbm = pl.BlockSpec(memory_space=pl.ANY)
    return pl.pallas_call(
        body, name=name, in_specs=[hbm] * n, out_specs=[hbm] * n, out_shape=out_shapes,
        scratch_shapes=[pltpu.SemaphoreType.DMA((N_DEV - 1, n)), pltpu.SemaphoreType.DMA((N_DEV - 1, n)),
                        pltpu.SemaphoreType.DMA((n,))],
    )(*ins)


def _run_exchange(n, local, remote, send_sems, recv_sems):
    me = _me_xyc()
    locs = [local(p) for p in range(n)]
    for cp in locs:
        cp.start()
    sends = [remote(k, p, me, _flip(me, k)) for k in range(1, N_DEV) for p in range(n)]
    for cp in sends:
        cp.start()
    for k in range(1, N_DEV):
        for p in range(n):
            remote(k, p, _flip(me, k), me).wait_recv()
    for cp in sends:
        cp.wait_send()
    for cp in locs:
        cp.wait()


def _all_gather(shards, axes, name):
    n = len(shards)
    sizes = [s.shape[a] for s, a in zip(shards, axes)]

    def body(*refs):
        ins, outs = refs[:n], refs[n:2 * n]
        send_sems, recv_sems, loc_sems = refs[2 * n:]
        me = _me_xyc()

        def local(p):
            return pltpu.make_async_copy(ins[p], _window(outs[p], axes[p], sizes[p], _index(me)), loc_sems.at[p])

        def remote(k, p, owner, to):
            return pltpu.make_async_remote_copy(
                src_ref=ins[p], dst_ref=_window(outs[p], axes[p], sizes[p], _index(owner)),
                send_sem=send_sems.at[k - 1, p], recv_sem=recv_sems.at[k - 1, p], device_id=to, device_id_type=MESH)

        _run_exchange(n, local, remote, send_sems, recv_sems)

    def full(s, a):
        return s.shape[:a] + (N_DEV * s.shape[a],) + s.shape[a + 1:]

    return _comm_call(body, n, shards, [jax.ShapeDtypeStruct(full(s, a), s.dtype) for s, a in zip(shards, axes)], name)


def _exchange(fulls, axes, name):
    n = len(fulls)
    sizes = [None if a is None else f.shape[a] // N_DEV for f, a in zip(fulls, axes)]

    def part_shape(f, a):
        return f.shape if a is None else f.shape[:a] + (f.shape[a] // N_DEV,) + f.shape[a + 1:]

    def body(*refs):
        ins, outs = refs[:n], refs[n:2 * n]
        send_sems, recv_sems, loc_sems = refs[2 * n:]
        me = _me_xyc()

        def src(p, to):
            return ins[p] if axes[p] is None else _window(ins[p], axes[p], sizes[p], _index(to))

        def local(p):
            return pltpu.make_async_copy(src(p, me), outs[p].at[_index(me)], loc_sems.at[p])

        def remote(k, p, sender, to):
            return pltpu.make_async_remote_copy(
                src_ref=src(p, to), dst_ref=outs[p].at[_index(sender)],
                send_sem=send_sems.at[k - 1, p], recv_sem=recv_sems.at[k - 1, p], device_id=to, device_id_type=MESH)

        _run_exchange(n, local, remote, send_sems, recv_sems)

    return _comm_call(body, n, fulls,
                      [jax.ShapeDtypeStruct((N_DEV,) + part_shape(f, a), f.dtype) for f, a in zip(fulls, axes)], name)


_HBM = pl.BlockSpec(memory_space=pltpu.HBM)
_SEM = pl.BlockSpec(memory_space=pltpu.SEMAPHORE)
_EFFECT = pltpu.SideEffectType.DATAFLOW_SIDE_EFFECTING


def _in_hbm(a):
    return pltpu.with_memory_space_constraint(a, pltpu.HBM)


def _split_start(srcs, lands, after, descr, name):
    n = len(srcs)

    def body(*refs):
        ins, lnd = refs[:n], refs[n:2 * n]
        send_sems, recv_sems = refs[2 * n + 1], refs[2 * n + 2]
        token = refs[-1]
        me = _me_xyc()
        for k in range(1, N_DEV):
            for p in range(n):
                descr(k, p, ins, lnd, send_sems, recv_sems, me, _flip(me, k)).start()
        token[...] = jnp.zeros_like(token)

    sems = pltpu.SemaphoreType.DMA(((N_DEV - 1) * n,))
    res = pl.pallas_call(
        body, name=name,
        out_shape=(sems, sems, *[pltpu.HBM(a.shape, a.dtype) for a in (*srcs, *lands)],
                   jax.ShapeDtypeStruct((8, 128), F32)),
        in_specs=[_HBM] * (2 * n) + [pl.BlockSpec(memory_space=pl.ANY)],
        out_specs=(_SEM, _SEM, *[_HBM] * (2 * n), pl.BlockSpec(memory_space=pltpu.VMEM)),
        input_output_aliases={i: 2 + i for i in range(2 * n)},
        compiler_params=pltpu.CompilerParams(has_side_effects=_EFFECT),
    )(*[_in_hbm(a) for a in (*srcs, *lands)], after)
    return res[0], res[1], res[2:2 + n], res[2 + n:2 + 2 * n], res[-1]


def _split_wait(send_sems, recv_sems, srcs, lands, after, descr, name):
    n = len(srcs)

    def body(*refs):
        ins, lnd = refs[:n], refs[n:2 * n]
        s_sems, r_sems = refs[2 * n], refs[2 * n + 1]
        me = _me_xyc()
        for k in range(1, N_DEV):
            for p in range(n):
                peer = _flip(me, k)
                descr(k, p, ins, lnd, s_sems, r_sems, me, peer).wait_send()
                descr(k, p, ins, lnd, s_sems, r_sems, peer, me).wait_recv()

    res = pl.pallas_call(
        body, name=name,
        out_shape=tuple(pltpu.HBM(a.shape, a.dtype) for a in (*srcs, *lands)),
        in_specs=[_HBM] * (2 * n) + [_SEM, _SEM, pl.BlockSpec(memory_space=pl.ANY)],
        out_specs=tuple([_HBM] * (2 * n)),
        input_output_aliases={i: i for i in range(2 * n)},
        compiler_params=pltpu.CompilerParams(has_side_effects=_EFFECT),
    )(*srcs, *lands, send_sems, recv_sems, after)
    return list(res[n:])


def _gather_descr(axes, sizes):
    def descr(k, p, ins, lnd, send_sems, recv_sems, sender, dest):
        return pltpu.make_async_remote_copy(
            src_ref=ins[p], dst_ref=_window(lnd[p], axes[p], sizes[p], _index(sender)),
            send_sem=send_sems.at[(k - 1) * len(axes) + p], recv_sem=recv_sems.at[(k - 1) * len(axes) + p],
            device_id=dest, device_id_type=MESH)
    return descr


def _exchange_descr(axes, sizes):
    def descr(k, p, ins, lnd, send_sems, recv_sems, sender, dest):
        return pltpu.make_async_remote_copy(
            src_ref=_window(ins[p], axes[p], sizes[p], _index(dest)), dst_ref=lnd[p].at[_index(sender)],
            send_sem=send_sems.at[(k - 1) * len(axes) + p], recv_sem=recv_sems.at[(k - 1) * len(axes) + p],
            device_id=dest, device_id_type=MESH)
    return descr


def _gather_begin(shards, axes, after, name):
    sizes = [s.shape[a] for s, a in zip(shards, axes)]
    me = _index(_me_xyc())
    lands = []
    for s, a, sz in zip(shards, axes, sizes):
        full = s.shape[:a] + (N_DEV * sz,) + s.shape[a + 1:]
        lands.append(lax.dynamic_update_slice_in_dim(lax.empty(full, s.dtype), s, me * sz, a))
    descr = _gather_descr(axes, sizes)
    send_sems, recv_sems, srcs, lands, token = _split_start(shards, lands, after, descr, name)
    return (send_sems, recv_sems, srcs, lands, descr), token


def _exchange_begin(fulls, axes, after, name):
    sizes = [f.shape[a] // N_DEV for f, a in zip(fulls, axes)]
    me = _index(_me_xyc())
    lands = []
    for f, a, sz in zip(fulls, axes, sizes):
        own = lax.dynamic_slice_in_dim(f, me * sz, sz, a)
        lands.append(lax.dynamic_update_slice_in_dim(lax.empty((N_DEV,) + own.shape, f.dtype), own[None], me, 0))
    descr = _exchange_descr(axes, sizes)
    send_sems, recv_sems, srcs, lands, token = _split_start(fulls, lands, after, descr, name)
    return (send_sems, recv_sems, srcs, lands, descr), token


def _split_end(handle, after, name):
    send_sems, recv_sems, srcs, lands, descr = handle
    return _split_wait(send_sems, recv_sems, srcs, lands, after, descr, name)


def _adam_math(g, w, m, v):
    m = ADAM_B1 * m + (1.0 - ADAM_B1) * g
    v = ADAM_B2 * v + (1.0 - ADAM_B2) * (g * g)
    m_hat = m / (1.0 - ADAM_B1 ** ADAM_STEP)
    v_hat = v / (1.0 - ADAM_B2 ** ADAM_STEP)
    delta = -ADAM_LR * (m_hat / (jnp.sqrt(v_hat) + ADAM_EPS) + ADAM_WD * w)
    return delta, m, v


def _sum_slabs(r_ref):
    g = r_ref[0].astype(F32)
    for s in range(1, N_DEV):
        g = g + r_ref[s].astype(F32)
    return g


def _adamw_reduce(recv, w, m, v, tb, name):
    r, c = w.shape

    def body(r_ref, w_ref, m_ref, v_ref, g_ref, d_ref, nm_ref, nv_ref):
        g = _sum_slabs(r_ref)
        g_ref[...] = g
        d_ref[...], nm_ref[...], nv_ref[...] = _adam_math(g, w_ref[...], m_ref[...], v_ref[...])

    blk = pl.BlockSpec((tb, c), lambda i: (i, 0))
    return pl.pallas_call(
        body, name=name, grid=(r // tb,),
        in_specs=[pl.BlockSpec((N_DEV, tb, c), lambda i: (0, i, 0)), blk, blk, blk],
        out_specs=[blk] * 4, out_shape=[jax.ShapeDtypeStruct((r, c), F32)] * 4,
        compiler_params=_cparams(("parallel",)),
    )(recv, w, m, v)


def _reduce8(recv, tb, name):
    r, c = recv.shape[1:]

    def body(r_ref, g_ref):
        g_ref[...] = _sum_slabs(r_ref)

    return pl.pallas_call(
        body, name=name, grid=(r // tb,),
        in_specs=[pl.BlockSpec((N_DEV, tb, c), lambda i: (0, i, 0))],
        out_specs=pl.BlockSpec((tb, c), lambda i: (i, 0)), out_shape=jax.ShapeDtypeStruct((r, c), F32),
        compiler_params=_cparams(("parallel",)),
    )(recv)


def _adamw(g, w, m, v, tb, name):
    r, c = w.shape

    def body(g_ref, w_ref, m_ref, v_ref, d_ref, nm_ref, nv_ref):
        d_ref[...], nm_ref[...], nv_ref[...] = _adam_math(g_ref[...], w_ref[...], m_ref[...], v_ref[...])

    blk = pl.BlockSpec((tb, c), lambda i: (i, 0))
    return pl.pallas_call(
        body, name=name, grid=(r // tb,), in_specs=[blk] * 4, out_specs=[blk] * 3,
        out_shape=[jax.ShapeDtypeStruct((r, c), F32)] * 3, compiler_params=_cparams(("parallel",)),
    )(g, w, m, v)


DN_IN_SHARD = DN_IN // N_DEV
DN_IN_SHARD_PAD = 432
CONV_SHARD = (1, DN_CONV, 288)


def _pack_small(arrs, rows):
    flat = jnp.concatenate([a.astype(F32).reshape(-1) for a in arrs])
    return jnp.pad(flat, (0, rows * PACK_C - flat.size)).reshape(rows, PACK_C)


def _unpack_small(packed, shapes):
    flat, out, off = packed.reshape(-1), [], 0
    for shp in shapes:
        n = int(np.prod(shp))
        out.append(flat[off:off + n].reshape(shp))
        off += n
    return out


def kernel(x, mem, rel_bias, att_w_in, att_w_out, dn_w_in, dn_conv, dn_a_log, dn_dt_bias, dn_out_norm, dn_w_out, mem_norm, mem_w_kv, norm_mix_pre, norm_mix_post, norm_ffn_pre, norm_ffn_post, ffn_w_gate_up, ffn_w_down, loss_target, m_rel_bias, m_att_w_in, m_att_w_out, m_dn_w_in, m_dn_conv, m_dn_a_log, m_dn_dt_bias, m_dn_out_norm, m_dn_w_out, m_mem_norm, m_mem_w_kv, m_norm_mix_pre, m_norm_mix_post, m_norm_ffn_pre, m_norm_ffn_post, m_ffn_w_gate_up, m_ffn_w_down, v_rel_bias, v_att_w_in, v_att_w_out, v_dn_w_in, v_dn_conv, v_dn_a_log, v_dn_dt_bias, v_dn_out_norm, v_dn_w_out, v_mem_norm, v_mem_w_kv, v_norm_mix_pre, v_norm_mix_post, v_norm_ffn_pre, v_norm_ffn_post, v_ffn_w_gate_up, v_ffn_w_down):
    x0, mem0, tgt = x[0], mem[0], loss_target[0]
    t = x0.shape[0]
    axes = ("x", "y", "c")

    def t_shard(w):
        return jnp.swapaxes(w, 1, 2).astype(BF16)

    dn_in_pad = ((0, 0), (0, DN_IN_SHARD_PAD - DN_IN_SHARD), (0, 0))
    (w_att_in_t,) = _all_gather([t_shard(att_w_in)], [1], "allgather_first")
    w_att_in_t = w_att_in_t[0]
    gu_t, down = t_shard(ffn_w_gate_up), ffn_w_down.astype(BF16)
    gather_o, tok_o = _gather_begin([att_w_out.astype(BF16), mem_w_kv.astype(BF16)], [1, 1], w_att_in_t,
                                    "gather_att_out_start")
    gather_a, tok_a = _gather_begin([gu_t[0:1], down[0:1]], [1, 1], tok_o, "gather_ffn0_start")
    gather_b, tok_b = _gather_begin(
        [jnp.pad(t_shard(dn_w_in), dn_in_pad), dn_w_out.astype(BF16), gu_t[1:2], down[1:2], dn_conv],
        [1, 1, 1, 1, 0], tok_a, "gather_layer1_start")

    def gain(a, i):
        return a[i].reshape(1, D)

    (h0,) = _rowwise(_fn_pre, [x0], [gain(norm_mix_pre, 0) + tok_b[0:1, 0:1]], [(D, BF16)], ROW_TB, "pre0")
    p0 = _matmul(h0, w_att_in_t, "nt", BF16, "att_in")
    late = {}

    def kv0(after):
        late["w_att_out"], late["w_kv"] = _split_end(gather_o, after, "gather_att_out_wait")
        return _mem_kv_fwd(mem0, gain(mem_norm, 0), late["w_kv"][0], 0)

    cat0, res0, (km0, vm0, memn0) = _attn_mixer_fwd(p0, rel_bias, kv0)
    w_att_out, w_kv = late["w_att_out"][0], late["w_kv"]
    y0 = _matmul(cat0, w_att_out, "nn", F32, "att_out")
    g_a = [gain(norm_mix_post, 0), gain(norm_ffn_pre, 0)]
    x1, h1 = _rowwise(_fn_res_pre, [x0, y0], g_a, [(D, F32), (D, BF16)], ROW_TB, "res_pre0")
    w_gu_t0, w_down0 = [w[0] for w in _split_end(gather_a, h1, "gather_ffn0_wait")]
    f0, gu0, a0 = _ffn_fwd(h1, w_gu_t0, w_down0, 0)
    g_b = [gain(norm_ffn_post, 0), gain(norm_mix_pre, 1)]
    x2, h2 = _rowwise(_fn_res_pre, [x1, f0], g_b, [(D, F32), (D, BF16)], ROW_TB, "res_pre1")
    km1, vm1, memn1 = _mem_kv_fwd(mem0, gain(mem_norm, 1), w_kv[1], 1)
    w_dn_in_g, w_dn_out, w_gu_t1, w_down1, conv_g = _split_end(gather_b, h2, "gather_layer1_wait")
    w_dn_in_g, w_dn_out, w_gu_t1, w_down1 = w_dn_in_g[0], w_dn_out[0], w_gu_t1[0], w_down1[0]
    conv_full = conv_g.transpose(1, 0, 2).reshape(DN_CONV, 3 * TOK_W)
    w_dn_in_t = jnp.concatenate(
        [w_dn_in_g[DN_IN_SHARD_PAD * j:DN_IN_SHARD_PAD * j + DN_IN_SHARD] for j in range(N_DEV)]
        + [jnp.zeros((DN_IN_PAD - DN_IN, D), BF16)], axis=0)
    p1 = _matmul(h2, w_dn_in_t, "nt", F32, "dn_in")
    cat1, res1 = _dn_mixer_fwd(p1, conv_full, dn_a_log[0], dn_dt_bias[0], dn_out_norm[0], km1, vm1)
    y1 = _matmul(cat1, w_dn_out, "nn", F32, "dn_out")
    g_c = [gain(norm_mix_post, 1), gain(norm_ffn_pre, 1)]
    x3, h3 = _rowwise(_fn_res_pre, [x2, y1], g_c, [(D, F32), (D, BF16)], ROW_TB, "res_pre2")
    f1, gu1, a1 = _ffn_fwd(h3, w_gu_t1, w_down1, 1)
    g_d = [gain(norm_ffn_post, 1)]
    (x4,) = _rowwise(_fn_res, [x3, f1], g_d, [(D, F32)], ROW_TB, "res3")
    dx4, lrow = _loss_kernel(x4, tgt, ROW_TB, "loss")
    loss = lax.psum(lrow[0, 0] * (0.5 / D), axes)

    (df1,), (dg_fpost1,) = _rowwise_bwd(_fn_res, [x3, f1], g_d, [dx4], [None, BF16], ROW_TB, "res3_bwd")
    dh3, dwgu1, dwd1 = _ffn_bwd(df1, h3, w_gu_t1, w_down1, gu1, a1, 1)
    (dx2, dy1), (dg_mpost1, dg_fpre1) = _rowwise_bwd(_fn_res_pre, [x2, y1], g_c, [dx4, dh3], [F32, BF16],
                                                     ROW_TB, "res_pre2_bwd")
    dcat1 = _matmul(dy1, w_dn_out, "nt", F32, "dn_out_dx")
    dw_dn_out = _matmul(cat1, dy1, "tn", BF16, "dn_out_dw")
    dp1, dconv, da_log, ddt_bias, dout_norm, dkm1, dvm1 = _dn_mixer_bwd(dcat1, res1, km1, vm1)
    dwkv1, dg_mem1 = _mem_kv_bwd(mem0, gain(mem_norm, 1), w_kv[1], memn1, dkm1, dvm1, 1)
    dh2 = _matmul(dp1, w_dn_in_t, "nn", F32, "dn_in_dx")
    dw_dn_in_t = _matmul(dp1, h2, "tn", BF16, "dn_in_dw")
    dn_in_parts = [jnp.pad(dw_dn_in_t[DN_IN_SHARD * j:DN_IN_SHARD * (j + 1)],
                           ((0, DN_IN_SHARD_PAD - DN_IN_SHARD), (0, 0))) for j in range(N_DEV)]
    xch_b, tok = _exchange_begin(
        [jnp.concatenate(dn_in_parts, axis=0)[None], dw_dn_out[None], dwkv1[None], dwgu1[None], dwd1[None]],
        [1, 1, 1, 1, 1], dh2, "exchange_layer1_start")
    (dx1, df0), (dg_fpost0, dg_mpre1) = _rowwise_bwd(_fn_res_pre, [x1, f0], [g + tok[0:1, 0:1] for g in g_b],
                                                     [dx2, dh2], [F32, BF16], ROW_TB, "res_pre1_bwd")
    dh1, dwgu0, dwd0 = _ffn_bwd(df0, h1, w_gu_t0, w_down0, gu0, a0, 0)
    xch_a, tok = _exchange_begin([dwgu0[None], dwd0[None]], [1, 1], dh1, "exchange_ffn0_start")
    (dx0, dy0), (dg_mpost0, dg_fpre0) = _rowwise_bwd(_fn_res_pre, [x0, y0], [g + tok[0:1, 0:1] for g in g_a],
                                                     [dx1, dh1], [F32, BF16], ROW_TB, "res_pre0_bwd")
    dcat0 = _matmul(dy0, w_att_out, "nt", F32, "att_out_dx")
    dw_att_out = _matmul(cat0, dy0, "tn", BF16, "att_out_dw")
    dp0, drel, dkm0, dvm0 = _attn_mixer_bwd(dcat0, res0, km0, vm0)
    dwkv0, dg_mem0 = _mem_kv_bwd(mem0, gain(mem_norm, 0), w_kv[0], memn0, dkm0, dvm0, 0)
    xch_o, tok = _exchange_begin([dw_att_out[None], dwkv0[None]], [1, 1], dp0, "exchange_att_out_start")
    dw_att_in_t = _matmul(dp0, h0, "tn", BF16, "att_in_dw")
    xch_i, tok_i = _exchange_begin([dw_att_in_t[None]], [1], tok, "exchange_att_in_start")
    dh0 = _matmul(dp0, w_att_in_t, "nn", F32, "att_in_dx")
    (grad_x,), (dg_mpre0,) = _rowwise_bwd(_fn_first, [x0], [gain(norm_mix_pre, 0) + tok_i[0:1, 0:1]], [dx0, dh0],
                                          [F32], ROW_TB, "pre0_bwd")

    small_grads = [drel, da_log, ddt_bias, dout_norm, jnp.concatenate([dg_mem0, dg_mem1]),
                   jnp.concatenate([dg_mpre0, dg_mpre1]), jnp.concatenate([dg_mpost0, dg_mpost1]),
                   jnp.concatenate([dg_fpre0, dg_fpre1]), jnp.concatenate([dg_fpost0, dg_fpost1]), dconv]
    (r_small,) = _exchange([_pack_small(small_grads, SMALL_ROWS)], [None], "exchange_last")
    (r_att_in,) = _split_end(xch_i, r_small, "exchange_att_in_wait")
    r_att_out, r_kv0 = _split_end(xch_o, r_small, "exchange_att_out_wait")
    r_gu0, r_down0 = _split_end(xch_a, r_small, "exchange_ffn0_wait")
    r_dn_in, r_dn_out, r_kv1, r_gu1, r_down1 = _split_end(xch_b, r_small, "exchange_layer1_wait")

    def rows(a):
        return a.reshape((-1,) + a.shape[-1:])

    def row_sharded(recv, w, m, v, tb, name):
        outs = _adamw_reduce(recv.reshape((N_DEV, -1) + recv.shape[-1:]), rows(w), rows(m), rows(v), tb, name)
        return [o.reshape(w.shape) for o in outs]

    def col_sharded(recv, w, m, v, tb, name):
        g_t = _reduce8(recv.reshape((N_DEV, -1) + recv.shape[-1:]), tb, name + "_sum")
        g = jnp.swapaxes(g_t.reshape(recv.shape[1:])[:, :w.shape[2]], 1, 2)
        outs = _adamw(rows(g), rows(w), rows(m), rows(v), 256, name)
        return [g] + [o.reshape(w.shape) for o in outs]

    def per_layer(fn, recvs, w, m, v, tb, name):
        outs = [fn(r, w[l:l + 1], m[l:l + 1], v[l:l + 1], tb, f"{name}{l}") for l, r in enumerate(recvs)]
        return [jnp.concatenate(pair, axis=0) for pair in zip(*outs)]

    big = [col_sharded(r_att_in, att_w_in, m_att_w_in, v_att_w_in, 320, "adamw_att_in"),
           row_sharded(r_att_out, att_w_out, m_att_w_out, v_att_w_out, 128, "adamw_att_out"),
           col_sharded(r_dn_in, dn_w_in, m_dn_w_in, v_dn_w_in, 432, "adamw_dn_in"),
           row_sharded(r_dn_out, dn_w_out, m_dn_w_out, v_dn_w_out, 128, "adamw_dn_out"),
           per_layer(row_sharded, [r_kv0, r_kv1], mem_w_kv, m_mem_w_kv, v_mem_w_kv, 128, "adamw_mem_kv"),
           per_layer(col_sharded, [r_gu0, r_gu1], ffn_w_gate_up, m_ffn_w_gate_up, v_ffn_w_gate_up, 176,
                     "adamw_ffn_gu"),
           per_layer(row_sharded, [r_down0, r_down1], ffn_w_down, m_ffn_w_down, v_ffn_w_down, 176,
                     "adamw_ffn_down")]
    g_big, d_big, nm_big, nv_big = [[b[i] for b in big] for i in range(4)]

    g_small = _reduce8(r_small, SMALL_ROWS, "reduce_small")
    rep_shapes = [(32, 12), (1, 2, 6), (1, 2, 6), (1, 128), (2, D), (2, D), (2, D), (2, D), (2, D)]
    *g_rep, g_conv_full = _unpack_small(g_small, rep_shapes + [(DN_CONV, 3 * TOK_W)])
    me = _index(_me_xyc())
    g_conv = lax.dynamic_slice(g_conv_full, (0, me * 288), (DN_CONV, 288)).reshape(CONV_SHARD)
    small_shapes = rep_shapes + [CONV_SHARD]
    small_w = [rel_bias, dn_a_log, dn_dt_bias, dn_out_norm, mem_norm, norm_mix_pre, norm_mix_post,
               norm_ffn_pre, norm_ffn_post, dn_conv]
    small_m = [m_rel_bias, m_dn_a_log, m_dn_dt_bias, m_dn_out_norm, m_mem_norm, m_norm_mix_pre, m_norm_mix_post,
               m_norm_ffn_pre, m_norm_ffn_post, m_dn_conv]
    small_v = [v_rel_bias, v_dn_a_log, v_dn_dt_bias, v_dn_out_norm, v_mem_norm, v_norm_mix_pre, v_norm_mix_post,
               v_norm_ffn_pre, v_norm_ffn_post, v_dn_conv]
    g_small_list = g_rep + [g_conv]
    outs_small = _adamw(_pack_small(g_small_list, 24), _pack_small(small_w, 24), _pack_small(small_m, 24),
                        _pack_small(small_v, 24), 24, "adamw_small")
    d_small, nm_small, nv_small = [_unpack_small(o, small_shapes) for o in outs_small]

    def ordered(small, big):
        return [small[0], big[0], big[1], big[2], small[9], small[1], small[2], small[3], big[3], small[4],
                big[4], small[5], small[6], small[7], small[8], big[5], big[6]]

    g_small_out = [g.reshape(s) for g, s in zip(g_small_list, small_shapes)]
    return (loss, grad_x[None], *ordered(g_small_out, g_big), *ordered(d_small, d_big),
            *ordered(nm_small, nm_big), *ordered(nv_small, nv_big))
```

```python
import functools
import math
from typing import NamedTuple

import numpy as np
import jax
import jax.numpy as jnp
from jax import lax
from jax.experimental import pallas as pl
from jax.experimental.pallas import tpu as pltpu

F32 = jnp.float32
BF16 = jnp.bfloat16
HI = lax.Precision.HIGHEST
MESH = pl.DeviceIdType.MESH

N_DEV = 8
D = 1024
EPS = 1e-6
NEG = -1e30
TOK_W = 768
MEM_W = 256
ATT_HD = 64
DIL_GROUPS = ((128, 1), (512, 4), (2048, 16))
BAND_HALF = 64
REL_BUCKETS = 32
REL_MAX_DIST = 1024
DN_HD = 128
DN_HEADS = 6
DN_CONV = 5
DN_CHUNK = 64
MEM_HEADS = 4
D_FF = 2816
DN_IN = 3352
DN_IN_PAD = 3456

ADAM_LR, ADAM_B1, ADAM_B2, ADAM_EPS, ADAM_WD, ADAM_STEP = 0.001, 0.9, 0.999, 1e-08, 0.01, 10

PACK_C = 512
SMALL_ROWS = 48
VMEM_LIMIT = 48 * 1024 * 1024


def _cparams(sem=None):
    kw = dict(vmem_limit_bytes=VMEM_LIMIT)
    if sem is not None:
        kw["dimension_semantics"] = sem
    return pltpu.CompilerParams(**kw)


def _tile(n, cap):
    if n <= cap:
        return n
    best = None
    for t in range(128, cap + 1, 128):
        if n % t == 0:
            best = t
    assert best is not None, (n, cap)
    return best


def _matmul(a, b, mode, out_dtype, name, tm=1024, tn=1408, tk=None):
    if tk is None:
        tk = 4096 if mode == "tn" else 2816
    if mode == "tn":
        tm = min(tm, 512)
    if mode == "nn":
        (m, kc), (_, n) = a.shape, b.shape
        dims = (((1,), (0,)), ((), ()))
    elif mode == "nt":
        (m, kc), (n, _) = a.shape, b.shape
        dims = (((1,), (1,)), ((), ()))
    else:
        (kc, m), (_, n) = a.shape, b.shape
        dims = (((0,), (0,)), ((), ()))
    tm = m if m <= tm else _tile(m, tm)
    tn = _tile(n, tn)
    tk = _tile(kc, tk)
    nk = kc // tk

    def body(a_ref, b_ref, o_ref, acc_ref):
        k = pl.program_id(2)
        part = lax.dot_general(a_ref[...], b_ref[...], dims, preferred_element_type=F32)

        @pl.when(k == 0)
        def _():
            acc_ref[...] = part

        @pl.when(k > 0)
        def _():
            acc_ref[...] += part

        @pl.when(k == nk - 1)
        def _():
            o_ref[...] = acc_ref[...].astype(o_ref.dtype)

    if mode == "nn":
        a_spec = pl.BlockSpec((tm, tk), lambda i, j, k: (i, k))
        b_spec = pl.BlockSpec((tk, tn), lambda i, j, k: (k, j))
    elif mode == "nt":
        a_spec = pl.BlockSpec((tm, tk), lambda i, j, k: (i, k))
        b_spec = pl.BlockSpec((tn, tk), lambda i, j, k: (j, k))
    else:
        a_spec = pl.BlockSpec((tk, tm), lambda i, j, k: (k, i))
        b_spec = pl.BlockSpec((tk, tn), lambda i, j, k: (k, j))
    return pl.pallas_call(
        body, name=name, grid=(m // tm, n // tn, nk),
        in_specs=[a_spec, b_spec],
        out_specs=pl.BlockSpec((tm, tn), lambda i, j, k: (i, j)),
        out_shape=jax.ShapeDtypeStruct((m, n), out_dtype),
        scratch_shapes=[pltpu.VMEM((tm, tn), F32)],
        compiler_params=_cparams(("parallel", "parallel", "arbitrary")),
    )(a, b)


class _Cols(NamedTuple):
    arr: jax.Array
    width: int
    block: int

    @property
    def shape(self):
        return (self.arr.shape[0], self.width)


def _row_spec(r, tb):
    if isinstance(r, _Cols):
        return pl.BlockSpec((tb, r.width), lambda i, b=r.block: (i, b))
    return pl.BlockSpec((tb, r.shape[1]), lambda i: (i, 0))


def _row_arr(r):
    return r.arr if isinstance(r, _Cols) else r


def _rowwise(fn, rows, params, outs, tb, name):
    t = rows[0].shape[0]
    nr, npar = len(rows), len(params)

    def body(*refs):
        ins = [r[...].astype(F32) for r in refs[:nr + npar]]
        res = fn(*ins)
        for o_ref, r in zip(refs[nr + npar:], res):
            o_ref[...] = r.astype(o_ref.dtype)

    return pl.pallas_call(
        body, name=name, grid=(t // tb,),
        in_specs=[_row_spec(r, tb) for r in rows] + [pl.BlockSpec(p.shape, lambda i: (0, 0)) for p in params],
        out_specs=[pl.BlockSpec((tb, c), lambda i: (i, 0)) for c, _ in outs],
        out_shape=[jax.ShapeDtypeStruct((t, c), dt) for c, dt in outs],
        compiler_params=_cparams(("parallel",)),
    )(*[_row_arr(r) for r in rows], *params)


def _rowwise_bwd(fn, rows, params, cots, row_grad, tb, name):
    t = rows[0].shape[0]
    nr, npar, nc = len(rows), len(params), len(cots)
    want = [i for i, g in enumerate(row_grad) if g is not None]

    def body(*refs):
        ins = [r[...].astype(F32) for r in refs[:nr + npar]]
        cts = tuple(r[...].astype(F32) for r in refs[nr + npar:nr + npar + nc])
        outs = refs[nr + npar + nc:]
        _, vjp = jax.vjp(fn, *ins)
        grads = vjp(cts)
        for o_ref, i in zip(outs[:len(want)], want):
            o_ref[...] = grads[i].astype(o_ref.dtype)
        first = pl.program_id(0) == 0
        for o_ref, g in zip(outs[len(want):], grads[nr:]):
            @pl.when(first)
            def _(o_ref=o_ref, g=g):
                o_ref[...] = g

            @pl.when(jnp.logical_not(first))
            def _(o_ref=o_ref, g=g):
                o_ref[...] += g

    res = pl.pallas_call(
        body, name=name, grid=(t // tb,),
        in_specs=[_row_spec(r, tb) for r in rows] + [pl.BlockSpec(p.shape, lambda i: (0, 0)) for p in params]
        + [_row_spec(c, tb) for c in cots],
        out_specs=[pl.BlockSpec((tb, rows[i].shape[1]), lambda i_: (i_, 0)) for i in want]
        + [pl.BlockSpec(p.shape, lambda i: (0, 0)) for p in params],
        out_shape=[jax.ShapeDtypeStruct(tuple(rows[i].shape), row_grad[i]) for i in want]
        + [jax.ShapeDtypeStruct(p.shape, F32) for p in params],
        compiler_params=_cparams(("arbitrary",)),
    )(*[_row_arr(r) for r in rows], *params, *[_row_arr(c) for c in cots])
    return list(res[:len(want)]), list(res[len(want):])


def _rms(x, g):
    return x * lax.rsqrt(jnp.mean(x * x, axis=-1, keepdims=True) + EPS) * g


def _fn_pre(x, g):
    return (_rms(x, g),)


def _fn_res_pre(x, y, g_post, g_pre):
    x1 = x + _rms(y, g_post)
    return x1, _rms(x1, g_pre)


def _fn_res(x, y, g_post):
    return (x + _rms(y, g_post),)


def _sigmoid(x):
    return 1.0 / (1.0 + jnp.exp(-x))


def _silu(x):
    return x * _sigmoid(x)


def _fn_swiglu(gu):
    return (_silu(gu[:, :D_FF]) * gu[:, D_FF:],)


def _fn_combine(o, lse):
    ls = [lse[:, 256 * g:256 * (g + 1)] for g in range(3)]
    mx = lax.stop_gradient(jnp.maximum(jnp.maximum(ls[0], ls[1]), ls[2]))
    es = [jnp.exp(l - mx) for l in ls]
    inv = 1.0 / (es[0] + es[1] + es[2])
    return (jnp.concatenate([o[:, 256 * g:256 * (g + 1)] * (es[g] * inv) for g in range(3)], axis=1),)


def _fn_outnorm(o_f, o_r, z, gain):
    res = []
    for h in range(DN_HEADS):
        sl = slice(DN_HD * h, DN_HD * (h + 1))
        o = o_f[:, sl] + o_r[:, sl]
        res.append(o * lax.rsqrt(jnp.mean(o * o, axis=-1, keepdims=True) + EPS) * gain * _silu(z[:, sl]))
    return (jnp.concatenate(res, axis=1),)


def _loss_kernel(x, tgt, tb, name):
    t, d = x.shape

    def body(x_ref, t_ref, dx_ref, l_ref, acc_ref):
        i = pl.program_id(0)
        e = x_ref[...] - t_ref[...]
        dx_ref[...] = e * (1.0 / d)
        part = jnp.sum(e * e, axis=0, keepdims=True)

        @pl.when(i == 0)
        def _():
            acc_ref[...] = part

        @pl.when(i > 0)
        def _():
            acc_ref[...] += part

        @pl.when(i == t // tb - 1)
        def _():
            l_ref[...] = jnp.broadcast_to(jnp.sum(acc_ref[...], axis=-1, keepdims=True), (1, 128))

    return pl.pallas_call(
        body, name=name, grid=(t // tb,),
        in_specs=[pl.BlockSpec((tb, d), lambda i: (i, 0))] * 2,
        out_specs=[pl.BlockSpec((tb, d), lambda i: (i, 0)), pl.BlockSpec((1, 128), lambda i: (0, 0))],
        out_shape=[jax.ShapeDtypeStruct((t, d), F32), jax.ShapeDtypeStruct((1, 128), F32)],
        scratch_shapes=[pltpu.VMEM((1, d), F32)],
        compiler_params=_cparams(("arbitrary",)),
    )(x, tgt)


def _band_fn(l_sub, bq, i, q, kw, vw, bm):
    w = bq + 2 * BAND_HALF
    s = lax.dot_general((q * (ATT_HD ** -0.5)).astype(BF16), kw.astype(BF16), (((2,), (2,)), ((0,), (0,))),
                        preferred_element_type=F32) + bm
    kpos = i * bq - BAND_HALF + lax.broadcasted_iota(jnp.int32, (4, bq, w), 2)
    s = jnp.where((kpos >= 0) & (kpos < l_sub), s, NEG)
    m = lax.stop_gradient(jnp.max(s, axis=-1, keepdims=True))
    p = jnp.exp(s - m)
    den = jnp.sum(p, axis=-1, keepdims=True)
    o = lax.dot_general(p.astype(BF16), vw.astype(BF16), (((2,), (1,)), ((0,), (0,))),
                        preferred_element_type=F32) / den
    return o, jnp.broadcast_to(m + jnp.log(den), o.shape)


def _band_specs(l_sub, bq):
    w = bq + 2 * BAND_HALF
    qs = pl.BlockSpec((None, 4, bq, ATT_HD), lambda r, i: (r, 0, i, 0))
    ks = pl.BlockSpec((None, 4, l_sub + 2 * BAND_HALF, ATT_HD), lambda r, i: (r, 0, 0, 0))
    bs = pl.BlockSpec((4, bq, w), lambda r, i: (0, 0, 0))
    return qs, ks, bs


def _band_fwd(q, k, v, bm, dil, l_sub, bq, name):
    w = bq + 2 * BAND_HALF
    qs, ks, bs = _band_specs(l_sub, bq)

    def body(q_ref, k_ref, v_ref, bm_ref, o_ref, l_ref):
        i = pl.program_id(1)
        st = pl.multiple_of(i * bq, bq)
        o, lse = _band_fn(l_sub, bq, i, q_ref[...].astype(F32), k_ref[:, pl.ds(st, w), :].astype(F32),
                          v_ref[:, pl.ds(st, w), :].astype(F32), bm_ref[...])
        o_ref[...] = o
        l_ref[...] = lse

    return pl.pallas_call(
        body, name=name, grid=(dil, l_sub // bq),
        in_specs=[qs, ks, ks, bs], out_specs=[qs, qs],
        out_shape=[jax.ShapeDtypeStruct(q.shape, F32)] * 2,
        compiler_params=_cparams(("parallel", "arbitrary")),
    )(q, k, v, bm)


def _band_bwd(q, k, v, bm, do, dlse, dil, l_sub, bq, name):
    w = bq + 2 * BAND_HALF
    qs, ks, bs = _band_specs(l_sub, bq)

    def body(q_ref, k_ref, v_ref, bm_ref, do_ref, dl_ref, dq_ref, dk_ref, dv_ref, dbm_ref):
        r, i = pl.program_id(0), pl.program_id(1)
        st = pl.multiple_of(i * bq, bq)
        _, vjp = jax.vjp(functools.partial(_band_fn, l_sub, bq, i),
                         q_ref[...].astype(F32), k_ref[:, pl.ds(st, w), :].astype(F32),
                         v_ref[:, pl.ds(st, w), :].astype(F32), bm_ref[...])
        dq, dkw, dvw, dbm = vjp((do_ref[...].astype(F32), dl_ref[...]))
        dq_ref[...] = dq.astype(dq_ref.dtype)

        @pl.when(i == 0)
        def _():
            dk_ref[...] = jnp.zeros_like(dk_ref)
            dv_ref[...] = jnp.zeros_like(dv_ref)

        dk_ref[:, pl.ds(st, w), :] += dkw
        dv_ref[:, pl.ds(st, w), :] += dvw

        @pl.when((i == 0) & (r == 0))
        def _():
            dbm_ref[...] = dbm

        @pl.when((i > 0) | (r > 0))
        def _():
            dbm_ref[...] += dbm

    return pl.pallas_call(
        body, name=name, grid=(dil, l_sub // bq),
        in_specs=[qs, ks, ks, bs, qs, qs], out_specs=[qs, ks, ks, bs],
        out_shape=[jax.ShapeDtypeStruct(q.shape, BF16), jax.ShapeDtypeStruct(k.shape, F32),
                   jax.ShapeDtypeStruct(k.shape, F32), jax.ShapeDtypeStruct(bm.shape, F32)],
        compiler_params=_cparams(("arbitrary", "arbitrary")),
    )(q, k, v, bm, do, dlse)


def _t5_bucket(rel):
    half = REL_BUCKETS // 2
    max_exact = half // 2
    n = np.abs(rel)
    large = max_exact + (np.log(np.maximum(n, 1) / max_exact) / math.log(REL_MAX_DIST / max_exact)
                         * (half - max_exact)).astype(np.int64)
    large = np.minimum(large, half - 1)
    return ((rel > 0) * half + np.where(n < max_exact, n, large)).astype(np.int32)


def _bucket_onehot(dil):
    idx = _t5_bucket(np.arange(-BAND_HALF, BAND_HALF + 1) * dil)
    oh = np.zeros((2 * BAND_HALF + 1, REL_BUCKETS), np.float32)
    oh[np.arange(2 * BAND_HALF + 1), idx] = 1.0
    return oh


def _band_bias(rel_bias, gi, dil, bq):
    w = bq + 2 * BAND_HALF
    nb = 2 * BAND_HALF + 1
    bias = jnp.dot(jnp.asarray(_bucket_onehot(dil)), rel_bias[:, 4 * gi:4 * gi + 4], precision=HI)
    row = jnp.concatenate([bias.T, jnp.full((4, w + 1 - nb), NEG, F32)], axis=1)
    flat = jnp.tile(row, (1, bq))[:, :bq * w]
    return flat.reshape(4, bq, w)


def _relbias_grad(dbms, name):
    nb = 2 * BAND_HALF + 1
    bq = max(d.shape[1] for d in dbms)
    skew = []
    for dbm in dbms:
        bqg, w = dbm.shape[1], dbm.shape[2]
        flat = jnp.pad(dbm.reshape(4, bqg * w), ((0, 0), (0, bqg)))
        skew.append(jnp.pad(flat.reshape(4, bqg, w + 1)[:, :, :nb], ((0, 0), (0, bq - bqg), (0, 256 - nb))))
    sk = jnp.concatenate(skew, axis=0)
    oh = np.zeros((3, 256, 128), np.float32)
    for gi, (_, dil) in enumerate(DIL_GROUPS):
        oh[gi, :2 * BAND_HALF + 1, :REL_BUCKETS] = _bucket_onehot(dil)

    def body(s_ref, oh_ref, o_ref):
        col = jnp.sum(s_ref[...], axis=0, keepdims=True)
        o_ref[...] = jnp.dot(jnp.broadcast_to(col, (8, 256)), oh_ref[...], precision=HI, preferred_element_type=F32)

    out = pl.pallas_call(
        body, name=name, grid=(12,),
        in_specs=[pl.BlockSpec((None, bq, 256), lambda n: (n, 0, 0)),
                  pl.BlockSpec((None, 256, 128), lambda n: (n // 4, 0, 0))],
        out_specs=pl.BlockSpec((None, 8, 128), lambda n: (n, 0, 0)),
        out_shape=jax.ShapeDtypeStruct((12, 8, 128), F32),
        compiler_params=_cparams(("parallel",)),
    )(sk, jnp.asarray(oh))
    return out[:, 0, :REL_BUCKETS].T


def _mem_fn(q, k, v):
    s = lax.dot_general((q * (ATT_HD ** -0.5)).astype(BF16), k.astype(BF16), (((2,), (2,)), ((0,), (0,))),
                        preferred_element_type=F32)
    m = lax.stop_gradient(jnp.max(s, axis=-1, keepdims=True))
    p = jnp.exp(s - m)
    p = p / jnp.sum(p, axis=-1, keepdims=True)
    return lax.dot_general(p.astype(BF16), v.astype(BF16), (((2,), (1,)), ((0,), (0,))), preferred_element_type=F32)


def _mem_specs(tb, ml):
    qs = pl.BlockSpec((MEM_HEADS, tb, ATT_HD), lambda i: (0, i, 0))
    ks = pl.BlockSpec((MEM_HEADS, ml, ATT_HD), lambda i: (0, 0, 0))
    return qs, ks


def _mem_fwd(q, k, v, tb, name):
    qs, ks = _mem_specs(tb, k.shape[1])

    def body(q_ref, k_ref, v_ref, o_ref):
        o_ref[...] = _mem_fn(q_ref[...].astype(F32), k_ref[...], v_ref[...])

    return pl.pallas_call(
        body, name=name, grid=(q.shape[1] // tb,),
        in_specs=[qs, ks, ks], out_specs=qs, out_shape=jax.ShapeDtypeStruct(q.shape, F32),
        compiler_params=_cparams(("parallel",)),
    )(q, k, v)


def _mem_bwd(q, k, v, do, tb, name):
    qs, ks = _mem_specs(tb, k.shape[1])

    def body(q_ref, k_ref, v_ref, do_ref, dq_ref, dk_ref, dv_ref):
        i = pl.program_id(0)
        _, vjp = jax.vjp(_mem_fn, q_ref[...].astype(F32), k_ref[...], v_ref[...])
        dq, dk, dv = vjp(do_ref[...])
        dq_ref[...] = dq

        @pl.when(i == 0)
        def _():
            dk_ref[...] = dk
            dv_ref[...] = dv

        @pl.when(i > 0)
        def _():
            dk_ref[...] += dk
            dv_ref[...] += dv

    return pl.pallas_call(
        body, name=name, grid=(q.shape[1] // tb,),
        in_specs=[qs, ks, ks, qs], out_specs=[qs, ks, ks],
        out_shape=[jax.ShapeDtypeStruct(q.shape, F32), jax.ShapeDtypeStruct(k.shape, F32),
                   jax.ShapeDtypeStruct(k.shape, F32)],
        compiler_params=_cparams(("arbitrary",)),
    )(q, k, v, do)


CONV_PAD = 8


def _conv_post(kind, acc):
    s = _silu(acc)
    if kind == 2:
        return s
    scale = DN_HD ** -0.5 if kind == 0 else 1.0
    return s * lax.rsqrt(jnp.sum(s * s, axis=-1, keepdims=True) + EPS) * scale


def _conv_rows(x_ref, t, start, rt):
    lo = min(max(start, 0), t - rt)
    x = x_ref[pl.ds(lo, rt), :]
    shift = lo - start
    if shift == 0:
        return x
    x = pltpu.roll(x, shift % rt, axis=0)
    row = lax.broadcasted_iota(jnp.int32, x.shape, 0)
    return jnp.where((row >= shift) if shift > 0 else (row < rt + shift), x, 0.0)


def _conv_acc(x_ref, t, w, r0, rt):
    acc = None
    for i in range(DN_CONV):
        term = w[i:i + 1, :] * _conv_rows(x_ref, t, r0 + i - DN_CONV // 2, rt)
        acc = term if acc is None else acc + term
    return acc


def _conv_fwd(x, w8, kind, rt, name):
    t = x.shape[0]

    def body(x_ref, w_ref, o_ref):
        w = w_ref[...]
        for r in range(t // rt):
            o_ref[pl.ds(r * rt, rt), :] = _conv_post(kind, _conv_acc(x_ref, t, w, r * rt, rt))

    return pl.pallas_call(
        body, name=name, grid=(DN_HEADS,),
        in_specs=[pl.BlockSpec((t, DN_HD), lambda j: (0, 6 * kind + j)),
                  pl.BlockSpec((8, DN_HD), lambda j: (0, 6 * kind + j))],
        out_specs=pl.BlockSpec((t, DN_HD), lambda j: (0, j)),
        out_shape=jax.ShapeDtypeStruct((t, TOK_W), F32),
        compiler_params=_cparams(("parallel",)),
    )(x, w8)


def _conv_bwd(x, w8, d_f, d_r, dp, kind, rt, name):
    t = x.shape[0]

    def body(xp_ref, w_ref, df_ref, dr_ref, dp_in, dx_ref, dw_ref, dpad_ref):
        del dp_in
        w = w_ref[...]
        zero = jnp.zeros((CONV_PAD, DN_HD), F32)
        dpad_ref[pl.ds(0, CONV_PAD), :] = zero
        dpad_ref[pl.ds(CONV_PAD + t, CONV_PAD), :] = zero
        dw = [jnp.zeros((1, DN_HD), F32) for _ in range(DN_CONV)]
        for r in range(t // rt):
            rows = pl.ds(r * rt, rt)
            acc = _conv_acc(xp_ref, t, w, r * rt, rt)
            _, vjp = jax.vjp(functools.partial(_conv_post, kind), acc)
            (dacc,) = vjp(df_ref[rows, :] + dr_ref[rows, :])
            dpad_ref[pl.ds(CONV_PAD + r * rt, rt), :] = dacc
            for i in range(DN_CONV):
                xs = _conv_rows(xp_ref, t, r * rt + i - DN_CONV // 2, rt)
                dw[i] = dw[i] + jnp.sum(dacc * xs, axis=0, keepdims=True)
        dw_ref[...] = jnp.concatenate(dw + [jnp.zeros((8 - DN_CONV, DN_HD), F32)], axis=0)
        for r in range(t // rt):
            acc = None
            for i in range(DN_CONV):
                term = w[i:i + 1, :] * dpad_ref[pl.ds(CONV_PAD + r * rt - i + DN_CONV // 2, rt), :]
                acc = term if acc is None else acc + term
            dx_ref[pl.ds(r * rt, rt), :] = acc.astype(dx_ref.dtype)

    return pl.pallas_call(
        body, name=name, grid=(DN_HEADS,),
        in_specs=[pl.BlockSpec((t, DN_HD), lambda j: (0, 6 * kind + j)),
                  pl.BlockSpec((8, DN_HD), lambda j: (0, 6 * kind + j)),
                  pl.BlockSpec((t, DN_HD), lambda j: (0, j)),
                  pl.BlockSpec((t, DN_HD), lambda j: (0, j)),
                  pl.BlockSpec(memory_space=pl.ANY)],
        out_specs=[pl.BlockSpec((t, DN_HD), lambda j: (0, 6 * kind + j)),
                   pl.BlockSpec((8, DN_HD), lambda j: (0, j))],
        out_shape=[jax.ShapeDtypeStruct(dp.shape, dp.dtype), jax.ShapeDtypeStruct((8, TOK_W), F32)],
        input_output_aliases={4: 0},
        scratch_shapes=[pltpu.VMEM((t + 2 * CONV_PAD, DN_HD), F32)],
        compiler_params=_cparams(("parallel",)),
    )(x, w8, d_f, d_r, dp)


def _softplus(x):
    e = jnp.exp(-jnp.abs(x))
    return jnp.maximum(x, 0.0) + jnp.where(e < 1e-4, e - 0.5 * e * e, jnp.log(1.0 + e))


_NN = (((2,), (1,)), ((0,), (0,)))
_NT = (((2,), (2,)), ((0,), (0,)))
_TN = (((1,), (1,)), ((0,), (0,)))


def _dot(a, b, dims=_NN):
    return lax.dot_general(a.astype(BF16), b.astype(BF16), dims, preferred_element_type=F32)


def _hi_lo(x):
    hi = x.astype(BF16)
    return hi, (x - hi.astype(F32)).astype(BF16)


def _mask_dot(mask_bf16, x, dims):
    x1 = x.astype(BF16)
    r = x - x1.astype(F32)
    x2, x3 = _hi_lo(r)
    d = functools.partial(lax.dot_general, dimension_numbers=dims, preferred_element_type=F32)
    return d(mask_bf16, x1) + d(mask_bf16, x2) + d(mask_bf16, x3)


@jax.custom_vjp
def _dot_mask(mask_bf16, x):
    return _mask_dot(mask_bf16, x, _NN)


def _dot_mask_fwd(mask_bf16, x):
    return _mask_dot(mask_bf16, x, _NN), mask_bf16


def _dot_mask_bwd(mask_bf16, ct):
    return jnp.zeros_like(mask_bf16), _mask_dot(mask_bf16, ct, _TN)


_dot_mask.defvjp(_dot_mask_fwd, _dot_mask_bwd)


def _unit_solve_pass(lmat, rhs, masks):
    ainv = masks[6] - lmat * masks[0]
    for sh in range(1, 6):
        ainv = ainv - _dot(_dot(ainv, lmat * masks[sh]), ainv)
    return _dot(ainv, rhs), ainv


@jax.custom_vjp
def _unit_solve(lmat, rhs, masks):
    return _unit_solve_pass(lmat, rhs, masks)[0]


def _unit_solve_fwd(lmat, rhs, masks):
    sol, ainv = _unit_solve_pass(lmat, rhs, masks)
    return sol, (sol, ainv, masks)


def _unit_solve_bwd(res, ct):
    sol, ainv, masks = res
    d_rhs = _dot(ainv, ct, _TN)
    return -_dot(d_rhs, sol, _NT), d_rhs, tuple(jnp.zeros_like(m) for m in masks)


_unit_solve.defvjp(_unit_solve_fwd, _unit_solve_bwd)


def _block_masks(rev, row, col):
    c = DN_CHUNK
    prow = jnp.where(rev, c - 1 - row, row)
    pcol = jnp.where(rev, c - 1 - col, col)
    masks = []
    for sh in range(6):
        differ = (prow ^ pcol) >> sh
        miss = (differ ^ 1) + (1 - ((prow >> sh) & 1))
        masks.append(jnp.where(miss == 0, 1.0, 0.0))
    masks.append(jnp.where(row == col, 1.0, 0.0))
    return tuple(masks)


def _dn_chunk(q, k, v, al, be, a_row, dt_row, s):
    n, c = q.shape[0], DN_CHUNK
    rev = lax.broadcasted_iota(jnp.int32, (n, c, c), 0) >= n // 2
    row = lax.broadcasted_iota(jnp.int32, (n, c, c), 1)
    col = lax.broadcasted_iota(jnp.int32, (n, c, c), 2)
    ahead = jnp.where(rev, col - row, row - col)
    incl = ahead >= 0
    strict = ahead > 0
    incl_b = incl.astype(BF16)

    g = -jnp.exp(a_row) * _softplus(al + dt_row)
    beta = _sigmoid(be)
    gc = _dot_mask(incl_b, g)
    gcc = gc[:, :, :c]
    decay = jnp.exp(jnp.where(incl, gcc - jnp.swapaxes(gcc, 1, 2), NEG))
    kb = k * beta
    lmat = jnp.where(strict, _dot(kb, k, _NT) * decay, 0.0)
    rhs = jnp.concatenate([v * beta, kb * jnp.exp(gc)], axis=2)
    sol = _unit_solve(lmat, rhs, _block_masks(rev, row, col))
    u, w = sol[:, :, :DN_HD], sol[:, :, DN_HD:]
    intra = jnp.where(incl, _dot(q, k, _NT) * decay, 0.0)
    v_new = u - _dot(w, s)
    out = _dot(q * jnp.exp(gc), s) + _dot(intra, v_new)
    g_last = jnp.sum(g, axis=1, keepdims=True)
    s_new = s * jnp.exp(g_last) + _dot(k * jnp.exp(g_last - gc), v_new, _TN)
    return out, s_new


DN_HG = 6


def _dn_load(f_refs, r_refs, alf, bef, alr, ber, a_ref, dt_ref):
    sls = [slice(DN_HD * h, DN_HD * (h + 1)) for h in range(DN_HG)]
    toks = [jnp.stack([f[:, sl] for sl in sls] + [r[:, sl] for sl in sls]) for f, r in zip(f_refs, r_refs)]
    al = jnp.concatenate([alf[...], alr[...]], axis=0)
    be = jnp.concatenate([bef[...], ber[...]], axis=0)
    a = jnp.concatenate([a_ref[0], a_ref[1]], axis=0)
    dt = jnp.concatenate([dt_ref[0], dt_ref[1]], axis=0)
    return toks, (al, be, a, dt)


def _dn_views(nc, bwd):
    c, hg = DN_CHUNK, DN_HG
    if bwd:
        f_blk = lambda s: nc - 1 - s
        r_blk = lambda s: s
        st_blk = lambda s: nc - 1 - s
    else:
        f_blk = lambda s: s
        r_blk = lambda s: nc - 1 - s
        st_blk = lambda s: s
    tok_f = pl.BlockSpec((c, hg * DN_HD), lambda g, s: (f_blk(s), g))
    tok_r = pl.BlockSpec((c, hg * DN_HD), lambda g, s: (r_blk(s), g))
    gate_f = pl.BlockSpec((None, hg, c, DN_HD), lambda g, s: (0, g, f_blk(s), 0))
    gate_r = pl.BlockSpec((None, hg, c, DN_HD), lambda g, s: (1, g, r_blk(s), 0))
    par = pl.BlockSpec((2, hg, 1, DN_HD), lambda g, s: (0, g, 0, 0))
    state = pl.BlockSpec((2, hg, None, DN_HD, DN_HD), lambda g, s: (0, g, st_blk(s), 0, 0))
    return tok_f, tok_r, gate_f, gate_r, par, state


def _dn_fwd(q, k, v, al, be, a_rows, dt_rows, name):
    t = q.shape[0]
    c, hg = DN_CHUNK, DN_HG
    nc = t // c
    tok_f, tok_r, gate_f, gate_r, par, state = _dn_views(nc, False)

    def body(qf, kf, vf, qr, kr, vr, alf, bef, alr, ber, a_ref, dt_ref, of_ref, or_ref, st_ref, s_ref):
        @pl.when(pl.program_id(1) == 0)
        def _():
            s_ref[...] = jnp.zeros_like(s_ref)

        (q_, k_, v_), gates = _dn_load((qf, kf, vf), (qr, kr, vr), alf, bef, alr, ber, a_ref, dt_ref)
        s = s_ref[...]
        st_ref[0] = s[:hg]
        st_ref[1] = s[hg:]
        out, s_new = _dn_chunk(q_, k_, v_, *gates, s)
        for h in range(hg):
            sl = slice(DN_HD * h, DN_HD * (h + 1))
            of_ref[:, sl] = out[h]
            or_ref[:, sl] = out[hg + h]
        s_ref[...] = s_new

    return pl.pallas_call(
        body, name=name, grid=(DN_HEADS // hg, nc),
        in_specs=[tok_f] * 3 + [tok_r] * 3 + [gate_f, gate_f, gate_r, gate_r, par, par],
        out_specs=[tok_f, tok_r, state],
        out_shape=[jax.ShapeDtypeStruct((t, TOK_W), F32)] * 2
        + [jax.ShapeDtypeStruct((2, DN_HEADS, nc, DN_HD, DN_HD), F32)],
        scratch_shapes=[pltpu.VMEM((2 * hg, DN_HD, DN_HD), F32)],
        compiler_params=_cparams(("parallel", "arbitrary")),
    )(q, k, v, q, k, v, al, be, al, be, a_rows, dt_rows)


def _dn_bwd(q, k, v, al, be, a_rows, dt_rows, states, do, name):
    t = q.shape[0]
    c, hg = DN_CHUNK, DN_HG
    assert hg == DN_HEADS
    nc = t // c
    tok_f, tok_r, gate_f, gate_r, par, state = _dn_views(nc, True)
    gout_f = pl.BlockSpec((c, DN_HD), lambda g, s: (nc - 1 - s, 0))
    gout_r = pl.BlockSpec((c, DN_HD), lambda g, s: (s, 0))

    def body(qf, kf, vf, qr, kr, vr, alf, bef, alr, ber, a_ref, dt_ref, st_ref, dof, dor,
             dqf, dkf, dvf, dqr, dkr, dvr, dgf, dgr, da_ref, ddt_ref, ds_ref):
        first = pl.program_id(1) == 0

        @pl.when(first)
        def _():
            ds_ref[...] = jnp.zeros_like(ds_ref)
            da_ref[...] = jnp.zeros_like(da_ref)
            ddt_ref[...] = jnp.zeros_like(ddt_ref)

        def lanes(x):
            return jnp.sum(x, axis=-1, keepdims=True)

        (q_, k_, v_, do_), gates = _dn_load((qf, kf, vf, dof), (qr, kr, vr, dor), alf, bef, alr, ber, a_ref, dt_ref)
        s = jnp.concatenate([st_ref[0], st_ref[1]], axis=0)
        _, vjp = jax.vjp(_dn_chunk, q_, k_, v_, *gates, s)
        dq, dk, dv, dal, dbe, da, ddt, ds = vjp((do_, ds_ref[...]))
        for h in range(hg):
            sl = slice(DN_HD * h, DN_HD * (h + 1))
            dqf[:, sl], dkf[:, sl], dvf[:, sl] = dq[h], dk[h], dv[h]
            dqr[:, sl], dkr[:, sl], dvr[:, sl] = dq[hg + h], dk[hg + h], dv[hg + h]
        dal, dbe = lanes(dal), lanes(dbe)
        lane = lax.broadcasted_iota(jnp.int32, (c, DN_HD), 1)
        for d, dg_ref in enumerate((dgf, dgr)):
            dg = jnp.zeros((c, DN_HD), F32)
            for h in range(hg):
                dg = jnp.where(lane == h, dal[d * hg + h], jnp.where(lane == hg + h, dbe[d * hg + h], dg))
            dg_ref[...] = dg
        da = jnp.broadcast_to(lanes(da), da.shape)
        ddt = jnp.broadcast_to(lanes(ddt), ddt.shape)
        da_ref[0] += da[:hg]
        da_ref[1] += da[hg:]
        ddt_ref[0] += ddt[:hg]
        ddt_ref[1] += ddt[hg:]
        ds_ref[...] = ds

    tok = jax.ShapeDtypeStruct((t, TOK_W), F32)
    gate = jax.ShapeDtypeStruct((t, DN_HD), F32)
    parsh = jax.ShapeDtypeStruct((2, DN_HEADS, 1, DN_HD), F32)
    res = pl.pallas_call(
        body, name=name, grid=(DN_HEADS // hg, nc),
        in_specs=[tok_f] * 3 + [tok_r] * 3 + [gate_f, gate_f, gate_r, gate_r, par, par, state, tok_f, tok_r],
        out_specs=[tok_f] * 3 + [tok_r] * 3 + [gout_f, gout_r, par, par],
        out_shape=[tok] * 6 + [gate] * 2 + [parsh] * 2,
        scratch_shapes=[pltpu.VMEM((2 * hg, DN_HD, DN_HD), F32)],
        compiler_params=_cparams(("parallel", "arbitrary")),
    )(q, k, v, q, k, v, al, be, al, be, a_rows, dt_rows, states, do, do)
    dqf, dkf, dvf, dqr, dkr, dvr, dgf, dgr, da, ddt = res
    dgate = jnp.concatenate([dgf[:, :2 * DN_HEADS], dgr[:, :2 * DN_HEADS]], axis=1)
    return (dqf, dkf, dvf), (dqr, dkr, dvr), dgate, da, ddt


BAND_BQ = 256
ROW_TB = 512
MEM_TB = 512
CONV_RT = 512


def _to_sub(x, dil):
    l = x.shape[0] // dil
    return x.reshape(l, dil, 4, ATT_HD).transpose(1, 2, 0, 3)


def _from_sub(x, dil):
    l = x.shape[2]
    return x.transpose(2, 0, 1, 3).reshape(l * dil, 4 * ATT_HD)


def _sub_in(x, col_block, dil, pad, dtype, name):
    if dil > 1:
        y = _to_sub(x[:, 256 * col_block:256 * (col_block + 1)], dil).astype(dtype)
        return jnp.pad(y, ((0, 0), (0, 0), (pad, pad), (0, 0))) if pad else y
    t = x.shape[0]

    def body(x_ref, o_ref):
        if pad:
            zero = jnp.zeros((4, pad, ATT_HD), dtype)
            o_ref[0, :, 0:pad, :] = zero
            o_ref[0, :, pad + t:pad + t + pad, :] = zero
        for h in range(4):
            o_ref[0, h, pad:pad + t, :] = x_ref[:, ATT_HD * h:ATT_HD * (h + 1)].astype(dtype)

    return pl.pallas_call(
        body, name=name, grid=(1,), in_specs=[pl.BlockSpec((t, 256), lambda i: (0, col_block))],
        out_specs=pl.BlockSpec((1, 4, t + 2 * pad, ATT_HD), lambda i: (0, 0, 0, 0)),
        out_shape=jax.ShapeDtypeStruct((1, 4, t + 2 * pad, ATT_HD), dtype), compiler_params=_cparams(("arbitrary",)),
    )(x)


def _sub_out(x, dil, pad, dtype, name):
    if dil > 1:
        return _from_sub(x[:, :, pad:x.shape[2] - pad] if pad else x, dil).astype(dtype)
    t = x.shape[2] - 2 * pad

    def body(x_ref, o_ref):
        for h in range(4):
            o_ref[:, ATT_HD * h:ATT_HD * (h + 1)] = x_ref[0, h, pad:pad + t, :].astype(dtype)

    return pl.pallas_call(
        body, name=name, grid=(1,), in_specs=[pl.BlockSpec(x.shape, lambda i: (0, 0, 0, 0))],
        out_specs=pl.BlockSpec((t, 256), lambda i: (0, 0)), out_shape=jax.ShapeDtypeStruct((t, 256), dtype),
        compiler_params=_cparams(("arbitrary",)),
    )(x)


def _heads_major(x):
    return x.reshape(x.shape[0], MEM_HEADS, ATT_HD).transpose(1, 0, 2)


def _heads_minor(x):
    return x.transpose(1, 0, 2).reshape(x.shape[1], MEM_HEADS * ATT_HD)


def _mem_kv_fwd(mem, gain, w_kv, li):
    (memn,) = _rowwise(_fn_pre, [mem], [gain], [(D, BF16)], mem.shape[0], f"memnorm_fwd{li}")
    kv = _matmul(memn, w_kv, "nn", F32, f"memkv_fwd{li}")
    return _heads_major(kv[:, :MEM_W]), _heads_major(kv[:, MEM_W:]), memn


def _mem_kv_bwd(mem, gain, w_kv, memn, dkm, dvm, li):
    dkv = jnp.concatenate([_heads_minor(dkm), _heads_minor(dvm)], axis=1).astype(BF16)
    dw = _matmul(memn, dkv, "tn", BF16, f"memkv_dw{li}")
    dmemn = _matmul(dkv, w_kv, "nt", F32, f"memkv_dx{li}")
    _, (dgain,) = _rowwise_bwd(_fn_pre, [mem], [gain], [dmemn], [None], mem.shape[0], f"memnorm_bwd{li}")
    return dw, dgain


def _attn_mixer_fwd(p, rel_bias, kv_fn):
    t = p.shape[0]
    saved, outs, lses = [], [], []
    for gi, (_, dil) in enumerate(DIL_GROUPS):
        l_sub = t // dil
        bq = min(BAND_BQ, l_sub)
        q = _sub_in(p, gi, dil, 0, p.dtype, f"sub_q{gi}")
        k = _sub_in(p, 3 + gi, dil, BAND_HALF, p.dtype, f"sub_k{gi}")
        v = _sub_in(p, 6 + gi, dil, BAND_HALF, p.dtype, f"sub_v{gi}")
        bm = _band_bias(rel_bias, gi, dil, bq)
        o, lse = _band_fwd(q, k, v, bm, dil, l_sub, bq, f"band_fwd{gi}")
        outs.append(_sub_out(o, dil, 0, F32, f"sub_o{gi}"))
        lses.append(_sub_out(lse, dil, 0, F32, f"sub_lse{gi}"))
        saved.append((q, k, v, bm))
    o_all = jnp.concatenate(outs, axis=1)
    lse_all = jnp.concatenate(lses, axis=1)
    (mixed,) = _rowwise(_fn_combine, [o_all, lse_all], [], [(TOK_W, BF16)], ROW_TB, "combine_fwd")
    qm = _heads_major(p[:, 3 * TOK_W:])
    km, vm, memn = kv_fn(mixed)
    memo = _mem_fwd(qm, km, vm, min(MEM_TB, t), "mem_fwd0")
    cat = jnp.concatenate([mixed, _heads_minor(memo).astype(BF16)], axis=1)
    return cat, (saved, o_all, lse_all, qm), (km, vm, memn)


def _attn_mixer_bwd(dcat, res, km, vm):
    saved, o_all, lse_all, qm = res
    t = dcat.shape[0]
    (do_all, dlse_all), _ = _rowwise_bwd(_fn_combine, [o_all, lse_all], [], [_Cols(dcat, TOK_W, 0)], [BF16, F32],
                                         ROW_TB, "combine_bwd")
    dqs, dks, dvs, dbms = [], [], [], []
    for gi, (_, dil) in enumerate(DIL_GROUPS):
        l_sub = t // dil
        bq = min(BAND_BQ, l_sub)
        q, k, v, bm = saved[gi]
        do = _sub_in(do_all, gi, dil, 0, do_all.dtype, f"sub_do{gi}")
        dl = _sub_in(dlse_all, gi, dil, 0, F32, f"sub_dlse{gi}")
        dq, dk, dv, dbm = _band_bwd(q, k, v, bm, do, dl, dil, l_sub, bq, f"band_bwd{gi}")
        dqs.append(_sub_out(dq, dil, 0, BF16, f"sub_dq{gi}"))
        dks.append(_sub_out(dk, dil, BAND_HALF, BF16, f"sub_dk{gi}"))
        dvs.append(_sub_out(dv, dil, BAND_HALF, BF16, f"sub_dv{gi}"))
        dbms.append(dbm)
    dqm, dkm, dvm = _mem_bwd(qm, km, vm, _heads_major(dcat[:, TOK_W:]), min(MEM_TB, t), "mem_bwd0")
    dp = jnp.concatenate(dqs + dks + dvs + [_heads_minor(dqm).astype(BF16)], axis=1)
    return dp, _relbias_grad(dbms, "relbias_grad"), dkm, dvm


def _dn_mixer_fwd(p, conv_w, a_log, dt_bias, out_norm, km, vm):
    t = p.shape[0]
    rt = min(CONV_RT, t)
    xp = p
    w8 = jnp.pad(conv_w, ((0, 8 - DN_CONV), (0, 0)))
    q = _conv_fwd(xp, w8, 0, rt, "conv_fwd_q")
    k = _conv_fwd(xp, w8, 1, rt, "conv_fwd_k")
    v = _conv_fwd(xp, w8, 2, rt, "conv_fwd_v")
    gate = p[:, 4 * TOK_W:4 * TOK_W + 4 * DN_HEADS].reshape(t, 2, 2, DN_HEADS)
    bshape = (2, DN_HEADS, t, DN_HD)
    al = jnp.broadcast_to(gate[:, :, 0, :].transpose(1, 2, 0)[..., None], bshape)
    be = jnp.broadcast_to(gate[:, :, 1, :].transpose(1, 2, 0)[..., None], bshape)
    a_rows = jnp.broadcast_to(a_log[:, :, None, None], (2, DN_HEADS, 1, DN_HD))
    dt_rows = jnp.broadcast_to(dt_bias[:, :, None, None], (2, DN_HEADS, 1, DN_HD))
    o_f, o_r, states = _dn_fwd(q, k, v, al, be, a_rows, dt_rows, "dn_fwd")
    gain = out_norm.reshape(1, DN_HD)
    (og,) = _rowwise(_fn_outnorm, [o_f, o_r, _Cols(p, TOK_W, 3)], [gain], [(TOK_W, BF16)], ROW_TB, "outnorm_fwd")
    qm = _heads_major(p[:, 4 * TOK_W + 4 * DN_HEADS:DN_IN])
    memo = _mem_fwd(qm, km, vm, min(MEM_TB, t), "mem_fwd1")
    cat = jnp.concatenate([og, _heads_minor(memo).astype(BF16)], axis=1)
    return cat, (xp, w8, q, k, v, al, be, a_rows, dt_rows, o_f, o_r, states, gain, qm)


def _dn_mixer_bwd(dcat, res, km, vm):
    xp, w8, q, k, v, al, be, a_rows, dt_rows, o_f, o_r, states, gain, qm = res
    t = dcat.shape[0]
    rt = min(CONV_RT, t)
    (do, dz), (dgain,) = _rowwise_bwd(_fn_outnorm, [o_f, o_r, _Cols(xp, TOK_W, 3)], [gain],
                                      [_Cols(dcat, TOK_W, 0)], [F32, None, BF16],
                                      ROW_TB, "outnorm_bwd")
    d_f, d_r, dgate, da, ddt = _dn_bwd(q, k, v, al, be, a_rows, dt_rows, states, do, "dn_bwd")
    dqm, dkm, dvm = _mem_bwd(qm, km, vm, _heads_major(dcat[:, TOK_W:]), min(MEM_TB, t), "mem_bwd1")
    rest = jnp.concatenate([dgate.astype(BF16), _heads_minor(dqm).astype(BF16),
                            jnp.zeros((t, DN_IN_PAD - DN_IN), BF16)], axis=1)
    dp = lax.dynamic_update_slice(lax.empty((t, DN_IN_PAD), BF16), dz, (0, 3 * TOK_W))
    dp = lax.dynamic_update_slice(dp, rest, (0, 4 * TOK_W))
    dws = []
    for kind, nm in enumerate("qkv"):
        dp, dw = _conv_bwd(xp, w8, d_f[kind], d_r[kind], dp, kind, rt, f"conv_bwd_{nm}")
        dws.append(dw)
    dconv = jnp.concatenate(dws, axis=1)[:DN_CONV]
    return dp, dconv, da[:, :, 0, 0], ddt[:, :, 0, 0], dgain.reshape(DN_HD), dkm, dvm


SWI_TB = 256


def _ffn_fwd(h, w_gu_t, w_d, li):
    gu = _matmul(h, w_gu_t, "nt", BF16, f"ffn_gu{li}")
    (a,) = _rowwise(_fn_swiglu, [gu], [], [(D_FF, BF16)], SWI_TB, f"swiglu_fwd{li}")
    return _matmul(a, w_d, "nn", F32, f"ffn_down{li}"), gu, a


def _ffn_bwd(df, h, w_gu_t, w_d, gu, a, li):
    da = _matmul(df, w_d, "nt", BF16, f"ffn_down_dx{li}")
    dwd = _matmul(a, df, "tn", BF16, f"ffn_down_dw{li}")
    (dgu,), _ = _rowwise_bwd(_fn_swiglu, [gu], [], [da], [BF16], SWI_TB, f"swiglu_bwd{li}")
    dh = _matmul(dgu, w_gu_t, "nn", BF16, f"ffn_gu_dx{li}")
    dwgu_t = _matmul(dgu, h, "tn", BF16, f"ffn_gu_dw{li}")
    return dh, dwgu_t, dwd


def _fn_first(x, g):
    return x, _rms(x, g)


def _me_xyc():
    return lax.axis_index("x"), lax.axis_index("y"), lax.axis_index("c")


def _flip(coords, k):
    x, y, c = coords
    return (1 - x if k & 4 else x, 1 - y if k & 2 else y, 1 - c if k & 1 else c)


def _index(coords):
    x, y, c = coords
    return 4 * x + 2 * y + c


def _window(ref, axis, size, d):
    idx = [slice(None)] * len(ref.shape)
    idx[axis] = pl.ds(pl.multiple_of(d * size, size), size)
    return ref.at[tuple(idx)]


def _comm_call(body, n, ins, out_shapes, name):
    hbm = pl.BlockSpec(memory_space=pl.ANY)
    return pl.pallas_call(
        body, name=name, in_specs=[hbm] * n, out_specs=[hbm] * n, out_shape=out_shapes,
        scratch_shapes=[pltpu.SemaphoreType.DMA((N_DEV - 1, n)), pltpu.SemaphoreType.DMA((N_DEV - 1, n)),
                        pltpu.SemaphoreType.DMA((n,))],
    )(*ins)


def _run_exchange(n, local, remote, send_sems, recv_sems):
    me = _me_xyc()
    locs = [local(p) for p in range(n)]
    for cp in locs:
        cp.start()
    sends = [remote(k, p, me, _flip(me, k)) for k in range(1, N_DEV) for p in range(n)]
    for cp in sends:
        cp.start()
    for k in range(1, N_DEV):
        for p in range(n):
            remote(k, p, _flip(me, k), me).wait_recv()
    for cp in sends:
        cp.wait_send()
    for cp in locs:
        cp.wait()


def _all_gather(shards, axes, name):
    n = len(shards)
    sizes = [s.shape[a] for s, a in zip(shards, axes)]

    def body(*refs):
        ins, outs = refs[:n], refs[n:2 * n]
        send_sems, recv_sems, loc_sems = refs[2 * n:]
        me = _me_xyc()

        def local(p):
            return pltpu.make_async_copy(ins[p], _window(outs[p], axes[p], sizes[p], _index(me)), loc_sems.at[p])

        def remote(k, p, owner, to):
            return pltpu.make_async_remote_copy(
                src_ref=ins[p], dst_ref=_window(outs[p], axes[p], sizes[p], _index(owner)),
                send_sem=send_sems.at[k - 1, p], recv_sem=recv_sems.at[k - 1, p], device_id=to, device_id_type=MESH)

        _run_exchange(n, local, remote, send_sems, recv_sems)

    def full(s, a):
        return s.shape[:a] + (N_DEV * s.shape[a],) + s.shape[a + 1:]

    return _comm_call(body, n, shards, [jax.ShapeDtypeStruct(full(s, a), s.dtype) for s, a in zip(shards, axes)], name)


def _exchange(fulls, axes, name):
    n = len(fulls)
    sizes = [None if a is None else f.shape[a] // N_DEV for f, a in zip(fulls, axes)]

    def part_shape(f, a):
        return f.shape if a is None else f.shape[:a] + (f.shape[a] // N_DEV,) + f.shape[a + 1:]

    def body(*refs):
        ins, outs = refs[:n], refs[n:2 * n]
        send_sems, recv_sems, loc_sems = refs[2 * n:]
        me = _me_xyc()

        def src(p, to):
            return ins[p] if axes[p] is None else _window(ins[p], axes[p], sizes[p], _index(to))

        def local(p):
            return pltpu.make_async_copy(src(p, me), outs[p].at[_index(me)], loc_sems.at[p])

        def remote(k, p, sender, to):
            return pltpu.make_async_remote_copy(
                src_ref=src(p, to), dst_ref=outs[p].at[_index(sender)],
                send_sem=send_sems.at[k - 1, p], recv_sem=recv_sems.at[k - 1, p], device_id=to, device_id_type=MESH)

        _run_exchange(n, local, remote, send_sems, recv_sems)

    return _comm_call(body, n, fulls,
                      [jax.ShapeDtypeStruct((N_DEV,) + part_shape(f, a), f.dtype) for f, a in zip(fulls, axes)], name)


_HBM = pl.BlockSpec(memory_space=pltpu.HBM)
_SEM = pl.BlockSpec(memory_space=pltpu.SEMAPHORE)
_EFFECT = pltpu.SideEffectType.DATAFLOW_SIDE_EFFECTING


def _in_hbm(a):
    return pltpu.with_memory_space_constraint(a, pltpu.HBM)


def _split_start(srcs, lands, after, descr, name):
    n = len(srcs)

    def body(*refs):
        ins, lnd = refs[:n], refs[n:2 * n]
        send_sems, recv_sems = refs[2 * n + 1], refs[2 * n + 2]
        token = refs[-1]
        me = _me_xyc()
        for k in range(1, N_DEV):
            for p in range(n):
                descr(k, p, ins, lnd, send_sems, recv_sems, me, _flip(me, k)).start()
        token[...] = jnp.zeros_like(token)

    sems = pltpu.SemaphoreType.DMA(((N_DEV - 1) * n,))
    res = pl.pallas_call(
        body, name=name,
        out_shape=(sems, sems, *[pltpu.HBM(a.shape, a.dtype) for a in (*srcs, *lands)],
                   jax.ShapeDtypeStruct((8, 128), F32)),
        in_specs=[_HBM] * (2 * n) + [pl.BlockSpec(memory_space=pl.ANY)],
        out_specs=(_SEM, _SEM, *[_HBM] * (2 * n), pl.BlockSpec(memory_space=pltpu.VMEM)),
        input_output_aliases={i: 2 + i for i in range(2 * n)},
        compiler_params=pltpu.CompilerParams(has_side_effects=_EFFECT),
    )(*[_in_hbm(a) for a in (*srcs, *lands)], after)
    return res[0], res[1], res[2:2 + n], res[2 + n:2 + 2 * n], res[-1]


def _split_wait(send_sems, recv_sems, srcs, lands, after, descr, name):
    n = len(srcs)

    def body(*refs):
        ins, lnd = refs[:n], refs[n:2 * n]
        s_sems, r_sems = refs[2 * n], refs[2 * n + 1]
        me = _me_xyc()
        for k in range(1, N_DEV):
            for p in range(n):
                peer = _flip(me, k)
                descr(k, p, ins, lnd, s_sems, r_sems, me, peer).wait_send()
                descr(k, p, ins, lnd, s_sems, r_sems, peer, me).wait_recv()

    res = pl.pallas_call(
        body, name=name,
        out_shape=tuple(pltpu.HBM(a.shape, a.dtype) for a in (*srcs, *lands)),
        in_specs=[_HBM] * (2 * n) + [_SEM, _SEM, pl.BlockSpec(memory_space=pl.ANY)],
        out_specs=tuple([_HBM] * (2 * n)),
        input_output_aliases={i: i for i in range(2 * n)},
        compiler_params=pltpu.CompilerParams(has_side_effects=_EFFECT),
    )(*srcs, *lands, send_sems, recv_sems, after)
    return list(res[n:])


def _gather_descr(axes, sizes):
    def descr(k, p, ins, lnd, send_sems, recv_sems, sender, dest):
        return pltpu.make_async_remote_copy(
            src_ref=ins[p], dst_ref=_window(lnd[p], axes[p], sizes[p], _index(sender)),
            send_sem=send_sems.at[(k - 1) * len(axes) + p], recv_sem=recv_sems.at[(k - 1) * len(axes) + p],
            device_id=dest, device_id_type=MESH)
    return descr


def _exchange_descr(axes, sizes):
    def descr(k, p, ins, lnd, send_sems, recv_sems, sender, dest):
        return pltpu.make_async_remote_copy(
            src_ref=_window(ins[p], axes[p], sizes[p], _index(dest)), dst_ref=lnd[p].at[_index(sender)],
            send_sem=send_sems.at[(k - 1) * len(axes) + p], recv_sem=recv_sems.at[(k - 1) * len(axes) + p],
            device_id=dest, device_id_type=MESH)
    return descr


def _gather_begin(shards, axes, after, name):
    sizes = [s.shape[a] for s, a in zip(shards, axes)]
    me = _index(_me_xyc())
    lands = []
    for s, a, sz in zip(shards, axes, sizes):
        full = s.shape[:a] + (N_DEV * sz,) + s.shape[a + 1:]
        lands.append(lax.dynamic_update_slice_in_dim(lax.empty(full, s.dtype), s, me * sz, a))
    descr = _gather_descr(axes, sizes)
    send_sems, recv_sems, srcs, lands, token = _split_start(shards, lands, after, descr, name)
    return (send_sems, recv_sems, srcs, lands, descr), token


def _exchange_begin(fulls, axes, after, name):
    sizes = [f.shape[a] // N_DEV for f, a in zip(fulls, axes)]
    me = _index(_me_xyc())
    lands = []
    for f, a, sz in zip(fulls, axes, sizes):
        own = lax.dynamic_slice_in_dim(f, me * sz, sz, a)
        lands.append(lax.dynamic_update_slice_in_dim(lax.empty((N_DEV,) + own.shape, f.dtype), own[None], me, 0))
    descr = _exchange_descr(axes, sizes)
    send_sems, recv_sems, srcs, lands, token = _split_start(fulls, lands, after, descr, name)
    return (send_sems, recv_sems, srcs, lands, descr), token


def _split_end(handle, after, name):
    send_sems, recv_sems, srcs, lands, descr = handle
    return _split_wait(send_sems, recv_sems, srcs, lands, after, descr, name)


def _adam_math(g, w, m, v):
    m = ADAM_B1 * m + (1.0 - ADAM_B1) * g
    v = ADAM_B2 * v + (1.0 - ADAM_B2) * (g * g)
    m_hat = m / (1.0 - ADAM_B1 ** ADAM_STEP)
    v_hat = v / (1.0 - ADAM_B2 ** ADAM_STEP)
    delta = -ADAM_LR * (m_hat / (jnp.sqrt(v_hat) + ADAM_EPS) + ADAM_WD * w)
    return delta, m, v


def _sum_slabs(r_ref):
    g = r_ref[0].astype(F32)
    for s in range(1, N_DEV):
        g = g + r_ref[s].astype(F32)
    return g


def _adamw_reduce(recv, w, m, v, tb, name):
    r, c = w.shape

    def body(r_ref, w_ref, m_ref, v_ref, g_ref, d_ref, nm_ref, nv_ref):
        g = _sum_slabs(r_ref)
        g_ref[...] = g
        d_ref[...], nm_ref[...], nv_ref[...] = _adam_math(g, w_ref[...], m_ref[...], v_ref[...])

    blk = pl.BlockSpec((tb, c), lambda i: (i, 0))
    return pl.pallas_call(
        body, name=name, grid=(r // tb,),
        in_specs=[pl.BlockSpec((N_DEV, tb, c), lambda i: (0, i, 0)), blk, blk, blk],
        out_specs=[blk] * 4, out_shape=[jax.ShapeDtypeStruct((r, c), F32)] * 4,
        compiler_params=_cparams(("parallel",)),
    )(recv, w, m, v)


def _reduce8(recv, tb, name):
    r, c = recv.shape[1:]

    def body(r_ref, g_ref):
        g_ref[...] = _sum_slabs(r_ref)

    return pl.pallas_call(
        body, name=name, grid=(r // tb,),
        in_specs=[pl.BlockSpec((N_DEV, tb, c), lambda i: (0, i, 0))],
        out_specs=pl.BlockSpec((tb, c), lambda i: (i, 0)), out_shape=jax.ShapeDtypeStruct((r, c), F32),
        compiler_params=_cparams(("parallel",)),
    )(recv)


def _adamw(g, w, m, v, tb, name):
    r, c = w.shape

    def body(g_ref, w_ref, m_ref, v_ref, d_ref, nm_ref, nv_ref):
        d_ref[...], nm_ref[...], nv_ref[...] = _adam_math(g_ref[...], w_ref[...], m_ref[...], v_ref[...])

    blk = pl.BlockSpec((tb, c), lambda i: (i, 0))
    return pl.pallas_call(
        body, name=name, grid=(r // tb,), in_specs=[blk] * 4, out_specs=[blk] * 3,
        out_shape=[jax.ShapeDtypeStruct((r, c), F32)] * 3, compiler_params=_cparams(("parallel",)),
    )(g, w, m, v)


DN_IN_SHARD = DN_IN // N_DEV
DN_IN_SHARD_PAD = 432
CONV_SHARD = (1, DN_CONV, 288)


def _pack_small(arrs, rows):
    flat = jnp.concatenate([a.astype(F32).reshape(-1) for a in arrs])
    return jnp.pad(flat, (0, rows * PACK_C - flat.size)).reshape(rows, PACK_C)


def _unpack_small(packed, shapes):
    flat, out, off = packed.reshape(-1), [], 0
    for shp in shapes:
        n = int(np.prod(shp))
        out.append(flat[off:off + n].reshape(shp))
        off += n
    return out


def kernel(x, mem, rel_bias, att_w_in, att_w_out, dn_w_in, dn_conv, dn_a_log, dn_dt_bias, dn_out_norm, dn_w_out, mem_norm, mem_w_kv, norm_mix_pre, norm_mix_post, norm_ffn_pre, norm_ffn_post, ffn_w_gate_up, ffn_w_down, loss_target, m_rel_bias, m_att_w_in, m_att_w_out, m_dn_w_in, m_dn_conv, m_dn_a_log, m_dn_dt_bias, m_dn_out_norm, m_dn_w_out, m_mem_norm, m_mem_w_kv, m_norm_mix_pre, m_norm_mix_post, m_norm_ffn_pre, m_norm_ffn_post, m_ffn_w_gate_up, m_ffn_w_down, v_rel_bias, v_att_w_in, v_att_w_out, v_dn_w_in, v_dn_conv, v_dn_a_log, v_dn_dt_bias, v_dn_out_norm, v_dn_w_out, v_mem_norm, v_mem_w_kv, v_norm_mix_pre, v_norm_mix_post, v_norm_ffn_pre, v_norm_ffn_post, v_ffn_w_gate_up, v_ffn_w_down):
    x0, mem0, tgt = x[0], mem[0], loss_target[0]
    t = x0.shape[0]
    axes = ("x", "y", "c")

    def t_shard(w):
        return jnp.swapaxes(w, 1, 2).astype(BF16)

    dn_in_pad = ((0, 0), (0, DN_IN_SHARD_PAD - DN_IN_SHARD), (0, 0))
    (w_att_in_t,) = _all_gather([t_shard(att_w_in)], [1], "allgather_first")
    w_att_in_t = w_att_in_t[0]
    gu_t, down = t_shard(ffn_w_gate_up), ffn_w_down.astype(BF16)
    gather_o, tok_o = _gather_begin([att_w_out.astype(BF16), mem_w_kv.astype(BF16)], [1, 1], w_att_in_t,
                                    "gather_att_out_start")
    gather_a, tok_a = _gather_begin([gu_t[0:1], down[0:1]], [1, 1], tok_o, "gather_ffn0_start")
    gather_b, tok_b = _gather_begin(
        [jnp.pad(t_shard(dn_w_in), dn_in_pad), dn_w_out.astype(BF16), gu_t[1:2], down[1:2], dn_conv],
        [1, 1, 1, 1, 0], tok_a, "gather_layer1_start")

    def gain(a, i):
        return a[i].reshape(1, D)

    (h0,) = _rowwise(_fn_pre, [x0], [gain(norm_mix_pre, 0) + tok_b[0:1, 0:1]], [(D, BF16)], ROW_TB, "pre0")
    p0 = _matmul(h0, w_att_in_t, "nt", BF16, "att_in")
    late = {}

    def kv0(after):
        late["w_att_out"], late["w_kv"] = _split_end(gather_o, after, "gather_att_out_wait")
        return _mem_kv_fwd(mem0, gain(mem_norm, 0), late["w_kv"][0], 0)

    cat0, res0, (km0, vm0, memn0) = _attn_mixer_fwd(p0, rel_bias, kv0)
    w_att_out, w_kv = late["w_att_out"][0], late["w_kv"]
    y0 = _matmul(cat0, w_att_out, "nn", F32, "att_out")
    g_a = [gain(norm_mix_post, 0), gain(norm_ffn_pre, 0)]
    x1, h1 = _rowwise(_fn_res_pre, [x0, y0], g_a, [(D, F32), (D, BF16)], ROW_TB, "res_pre0")
    w_gu_t0, w_down0 = [w[0] for w in _split_end(gather_a, h1, "gather_ffn0_wait")]
    f0, gu0, a0 = _ffn_fwd(h1, w_gu_t0, w_down0, 0)
    g_b = [gain(norm_ffn_post, 0), gain(norm_mix_pre, 1)]
    x2, h2 = _rowwise(_fn_res_pre, [x1, f0], g_b, [(D, F32), (D, BF16)], ROW_TB, "res_pre1")
    km1, vm1, memn1 = _mem_kv_fwd(mem0, gain(mem_norm, 1), w_kv[1], 1)
    w_dn_in_g, w_dn_out, w_gu_t1, w_down1, conv_g = _split_end(gather_b, h2, "gather_layer1_wait")
    w_dn_in_g, w_dn_out, w_gu_t1, w_down1 = w_dn_in_g[0], w_dn_out[0], w_gu_t1[0], w_down1[0]
    conv_full = conv_g.transpose(1, 0, 2).reshape(DN_CONV, 3 * TOK_W)
    w_dn_in_t = jnp.concatenate(
        [w_dn_in_g[DN_IN_SHARD_PAD * j:DN_IN_SHARD_PAD * j + DN_IN_SHARD] for j in range(N_DEV)]
        + [jnp.zeros((DN_IN_PAD - DN_IN, D), BF16)], axis=0)
    p1 = _matmul(h2, w_dn_in_t, "nt", F32, "dn_in")
    cat1, res1 = _dn_mixer_fwd(p1, conv_full, dn_a_log[0], dn_dt_bias[0], dn_out_norm[0], km1, vm1)
    y1 = _matmul(cat1, w_dn_out, "nn", F32, "dn_out")
    g_c = [gain(norm_mix_post, 1), gain(norm_ffn_pre, 1)]
    x3, h3 = _rowwise(_fn_res_pre, [x2, y1], g_c, [(D, F32), (D, BF16)], ROW_TB, "res_pre2")
    f1, gu1, a1 = _ffn_fwd(h3, w_gu_t1, w_down1, 1)
    g_d = [gain(norm_ffn_post, 1)]
    (x4,) = _rowwise(_fn_res, [x3, f1], g_d, [(D, F32)], ROW_TB, "res3")
    dx4, lrow = _loss_kernel(x4, tgt, ROW_TB, "loss")
    loss = lax.psum(lrow[0, 0] * (0.5 / D), axes)

    (df1,), (dg_fpost1,) = _rowwise_bwd(_fn_res, [x3, f1], g_d, [dx4], [None, BF16], ROW_TB, "res3_bwd")
    dh3, dwgu1, dwd1 = _ffn_bwd(df1, h3, w_gu_t1, w_down1, gu1, a1, 1)
    (dx2, dy1), (dg_mpost1, dg_fpre1) = _rowwise_bwd(_fn_res_pre, [x2, y1], g_c, [dx4, dh3], [F32, BF16],
                                                     ROW_TB, "res_pre2_bwd")
    dcat1 = _matmul(dy1, w_dn_out, "nt", F32, "dn_out_dx")
    dw_dn_out = _matmul(cat1, dy1, "tn", BF16, "dn_out_dw")
    dp1, dconv, da_log, ddt_bias, dout_norm, dkm1, dvm1 = _dn_mixer_bwd(dcat1, res1, km1, vm1)
    dwkv1, dg_mem1 = _mem_kv_bwd(mem0, gain(mem_norm, 1), w_kv[1], memn1, dkm1, dvm1, 1)
    dh2 = _matmul(dp1, w_dn_in_t, "nn", BF16, "dn_in_dx")
    dw_dn_in_t = _matmul(dp1, h2, "tn", BF16, "dn_in_dw")
    dn_in_parts = [jnp.pad(dw_dn_in_t[DN_IN_SHARD * j:DN_IN_SHARD * (j + 1)],
                           ((0, DN_IN_SHARD_PAD - DN_IN_SHARD), (0, 0))) for j in range(N_DEV)]
    xch_b, tok = _exchange_begin(
        [jnp.concatenate(dn_in_parts, axis=0)[None], dw_dn_out[None], dwkv1[None], dwgu1[None], dwd1[None]],
        [1, 1, 1, 1, 1], dh2, "exchange_layer1_start")
    (dx1, df0), (dg_fpost0, dg_mpre1) = _rowwise_bwd(_fn_res_pre, [x1, f0], [g + tok[0:1, 0:1] for g in g_b],
                                                     [dx2, dh2], [F32, BF16], ROW_TB, "res_pre1_bwd")
    dh1, dwgu0, dwd0 = _ffn_bwd(df0, h1, w_gu_t0, w_down0, gu0, a0, 0)
    xch_a, tok = _exchange_begin([dwgu0[None], dwd0[None]], [1, 1], dh1, "exchange_ffn0_start")
    (dx0, dy0), (dg_mpost0, dg_fpre0) = _rowwise_bwd(_fn_res_pre, [x0, y0], [g + tok[0:1, 0:1] for g in g_a],
                                                     [dx1, dh1], [F32, BF16], ROW_TB, "res_pre0_bwd")
    dcat0 = _matmul(dy0, w_att_out, "nt", F32, "att_out_dx")
    dw_att_out = _matmul(cat0, dy0, "tn", BF16, "att_out_dw")
    dp0, drel, dkm0, dvm0 = _attn_mixer_bwd(dcat0, res0, km0, vm0)
    dwkv0, dg_mem0 = _mem_kv_bwd(mem0, gain(mem_norm, 0), w_kv[0], memn0, dkm0, dvm0, 0)
    xch_o, tok = _exchange_begin([dw_att_out[None], dwkv0[None]], [1, 1], dp0, "exchange_att_out_start")
    dw_att_in_t = _matmul(dp0, h0, "tn", BF16, "att_in_dw")
    xch_i, tok_i = _exchange_begin([dw_att_in_t[None]], [1], tok, "exchange_att_in_start")
    dh0 = _matmul(dp0, w_att_in_t, "nn", BF16, "att_in_dx")
    (grad_x,), (dg_mpre0,) = _rowwise_bwd(_fn_first, [x0], [gain(norm_mix_pre, 0) + tok_i[0:1, 0:1]], [dx0, dh0],
                                          [F32], ROW_TB, "pre0_bwd")

    small_grads = [drel, da_log, ddt_bias, dout_norm, jnp.concatenate([dg_mem0, dg_mem1]),
                   jnp.concatenate([dg_mpre0, dg_mpre1]), jnp.concatenate([dg_mpost0, dg_mpost1]),
                   jnp.concatenate([dg_fpre0, dg_fpre1]), jnp.concatenate([dg_fpost0, dg_fpost1]), dconv]
    (r_small,) = _exchange([_pack_small(small_grads, SMALL_ROWS)], [None], "exchange_last")
    (r_att_in,) = _split_end(xch_i, r_small, "exchange_att_in_wait")
    r_att_out, r_kv0 = _split_end(xch_o, r_small, "exchange_att_out_wait")
    r_gu0, r_down0 = _split_end(xch_a, r_small, "exchange_ffn0_wait")
    r_dn_in, r_dn_out, r_kv1, r_gu1, r_down1 = _split_end(xch_b, r_small, "exchange_layer1_wait")

    def rows(a):
        return a.reshape((-1,) + a.shape[-1:])

    def row_sharded(recv, w, m, v, tb, name):
        outs = _adamw_reduce(recv.reshape((N_DEV, -1) + recv.shape[-1:]), rows(w), rows(m), rows(v), tb, name)
        return [o.reshape(w.shape) for o in outs]

    def col_sharded(recv, w, m, v, tb, name):
        g_t = _reduce8(recv.reshape((N_DEV, -1) + recv.shape[-1:]), tb, name + "_sum")
        g = jnp.swapaxes(g_t.reshape(recv.shape[1:])[:, :w.shape[2]], 1, 2)
        outs = _adamw(rows(g), rows(w), rows(m), rows(v), 256, name)
        return [g] + [o.reshape(w.shape) for o in outs]

    def per_layer(fn, recvs, w, m, v, tb, name):
        outs = [fn(r, w[l:l + 1], m[l:l + 1], v[l:l + 1], tb, f"{name}{l}") for l, r in enumerate(recvs)]
        return [jnp.concatenate(pair, axis=0) for pair in zip(*outs)]

    big = [col_sharded(r_att_in, att_w_in, m_att_w_in, v_att_w_in, 320, "adamw_att_in"),
           row_sharded(r_att_out, att_w_out, m_att_w_out, v_att_w_out, 128, "adamw_att_out"),
           col_sharded(r_dn_in, dn_w_in, m_dn_w_in, v_dn_w_in, 432, "adamw_dn_in"),
           row_sharded(r_dn_out, dn_w_out, m_dn_w_out, v_dn_w_out, 128, "adamw_dn_out"),
           per_layer(row_sharded, [r_kv0, r_kv1], mem_w_kv, m_mem_w_kv, v_mem_w_kv, 128, "adamw_mem_kv"),
           per_layer(col_sharded, [r_gu0, r_gu1], ffn_w_gate_up, m_ffn_w_gate_up, v_ffn_w_gate_up, 176,
                     "adamw_ffn_gu"),
           per_layer(row_sharded, [r_down0, r_down1], ffn_w_down, m_ffn_w_down, v_ffn_w_down, 176,
                     "adamw_ffn_down")]
    g_big, d_big, nm_big, nv_big = [[b[i] for b in big] for i in range(4)]

    g_small = _reduce8(r_small, SMALL_ROWS, "reduce_small")
    rep_shapes = [(32, 12), (1, 2, 6), (1, 2, 6), (1, 128), (2, D), (2, D), (2, D), (2, D), (2, D)]
    *g_rep, g_conv_full = _unpack_small(g_small, rep_shapes + [(DN_CONV, 3 * TOK_W)])
    me = _index(_me_xyc())
    g_conv = lax.dynamic_slice(g_conv_full, (0, me * 288), (DN_CONV, 288)).reshape(CONV_SHARD)
    small_shapes = rep_shapes + [CONV_SHARD]
    small_w = [rel_bias, dn_a_log, dn_dt_bias, dn_out_norm, mem_norm, norm_mix_pre, norm_mix_post,
               norm_ffn_pre, norm_ffn_post, dn_conv]
    small_m = [m_rel_bias, m_dn_a_log, m_dn_dt_bias, m_dn_out_norm, m_mem_norm, m_norm_mix_pre, m_norm_mix_post,
               m_norm_ffn_pre, m_norm_ffn_post, m_dn_conv]
    small_v = [v_rel_bias, v_dn_a_log, v_dn_dt_bias, v_dn_out_norm, v_mem_norm, v_norm_mix_pre, v_norm_mix_post,
               v_norm_ffn_pre, v_norm_ffn_post, v_dn_conv]
    g_small_list = g_rep + [g_conv]
    outs_small = _adamw(_pack_small(g_small_list, 24), _pack_small(small_w, 24), _pack_small(small_m, 24),
                        _pack_small(small_v, 24), 24, "adamw_small")
    d_small, nm_small, nv_small = [_unpack_small(o, small_shapes) for o in outs_small]

    def ordered(small, big):
        return [small[0], big[0], big[1], big[2], small[9], small[1], small[2], small[3], big[3], small[4],
                big[4], small[5], small[6], small[7], small[8], big[5], big[6]]

    g_small_out = [g.reshape(s) for g, s in zip(g_small_list, small_shapes)]
    return (loss, grad_x[None], *ordered(g_small_out, g_big), *ordered(d_small, d_big),
            *ordered(nm_small, nm_big), *ordered(nv_small, nv_big))
```

```python
import functools
import math
from typing import NamedTuple

import numpy as np
import jax
import jax.numpy as jnp
from jax import lax
from jax.experimental import pallas as pl
from jax.experimental.pallas import tpu as pltpu

F32 = jnp.float32
BF16 = jnp.bfloat16
HI = lax.Precision.HIGHEST
MESH = pl.DeviceIdType.MESH

N_DEV = 8
D = 1024
EPS = 1e-6
NEG = -1e30
TOK_W = 768
MEM_W = 256
ATT_HD = 64
DIL_GROUPS = ((128, 1), (512, 4), (2048, 16))
BAND_HALF = 64
REL_BUCKETS = 32
REL_MAX_DIST = 1024
DN_HD = 128
DN_HEADS = 6
DN_CONV = 5
DN_CHUNK = 64
MEM_HEADS = 4
D_FF = 2816
DN_IN = 3352
DN_IN_PAD = 3456

ADAM_LR, ADAM_B1, ADAM_B2, ADAM_EPS, ADAM_WD, ADAM_STEP = 0.001, 0.9, 0.999, 1e-08, 0.01, 10

PACK_C = 512
SMALL_ROWS = 48
VMEM_LIMIT = 48 * 1024 * 1024


def _cparams(sem=None):
    kw = dict(vmem_limit_bytes=VMEM_LIMIT)
    if sem is not None:
        kw["dimension_semantics"] = sem
    return pltpu.CompilerParams(**kw)


def _tile(n, cap):
    if n <= cap:
        return n
    best = None
    for t in range(128, cap + 1, 128):
        if n % t == 0:
            best = t
    assert best is not None, (n, cap)
    return best


def _matmul(a, b, mode, out_dtype, name, tm=1024, tn=1408, tk=None):
    if tk is None:
        tk = 4096 if mode == "tn" else 2816
    if mode == "tn":
        tm = min(tm, 512)
    if mode == "nn":
        (m, kc), (_, n) = a.shape, b.shape
        dims = (((1,), (0,)), ((), ()))
    elif mode == "nt":
        (m, kc), (n, _) = a.shape, b.shape
        dims = (((1,), (1,)), ((), ()))
    else:
        (kc, m), (_, n) = a.shape, b.shape
        dims = (((0,), (0,)), ((), ()))
    tm = m if m <= tm else _tile(m, tm)
    tn = _tile(n, tn)
    tk = _tile(kc, tk)
    nk = kc // tk

    def body(a_ref, b_ref, o_ref, acc_ref):
        k = pl.program_id(2)
        part = lax.dot_general(a_ref[...], b_ref[...], dims, preferred_element_type=F32)

        @pl.when(k == 0)
        def _():
            acc_ref[...] = part

        @pl.when(k > 0)
        def _():
            acc_ref[...] += part

        @pl.when(k == nk - 1)
        def _():
            o_ref[...] = acc_ref[...].astype(o_ref.dtype)

    if mode == "nn":
        a_spec = pl.BlockSpec((tm, tk), lambda i, j, k: (i, k))
        b_spec = pl.BlockSpec((tk, tn), lambda i, j, k: (k, j))
    elif mode == "nt":
        a_spec = pl.BlockSpec((tm, tk), lambda i, j, k: (i, k))
        b_spec = pl.BlockSpec((tn, tk), lambda i, j, k: (j, k))
    else:
        a_spec = pl.BlockSpec((tk, tm), lambda i, j, k: (k, i))
        b_spec = pl.BlockSpec((tk, tn), lambda i, j, k: (k, j))
    return pl.pallas_call(
        body, name=name, grid=(m // tm, n // tn, nk),
        in_specs=[a_spec, b_spec],
        out_specs=pl.BlockSpec((tm, tn), lambda i, j, k: (i, j)),
        out_shape=jax.ShapeDtypeStruct((m, n), out_dtype),
        scratch_shapes=[pltpu.VMEM((tm, tn), F32)],
        compiler_params=_cparams(("parallel", "parallel", "arbitrary")),
    )(a, b)


class _Cols(NamedTuple):
    arr: jax.Array
    width: int
    block: int

    @property
    def shape(self):
        return (self.arr.shape[0], self.width)


def _row_spec(r, tb):
    if isinstance(r, _Cols):
        return pl.BlockSpec((tb, r.width), lambda i, b=r.block: (i, b))
    return pl.BlockSpec((tb, r.shape[1]), lambda i: (i, 0))


def _row_arr(r):
    return r.arr if isinstance(r, _Cols) else r


def _rowwise(fn, rows, params, outs, tb, name):
    t = rows[0].shape[0]
    nr, npar = len(rows), len(params)

    def body(*refs):
        ins = [r[...].astype(F32) for r in refs[:nr + npar]]
        res = fn(*ins)
        for o_ref, r in zip(refs[nr + npar:], res):
            o_ref[...] = r.astype(o_ref.dtype)

    return pl.pallas_call(
        body, name=name, grid=(t // tb,),
        in_specs=[_row_spec(r, tb) for r in rows] + [pl.BlockSpec(p.shape, lambda i: (0, 0)) for p in params],
        out_specs=[pl.BlockSpec((tb, c), lambda i: (i, 0)) for c, _ in outs],
        out_shape=[jax.ShapeDtypeStruct((t, c), dt) for c, dt in outs],
        compiler_params=_cparams(("parallel",)),
    )(*[_row_arr(r) for r in rows], *params)


def _rowwise_bwd(fn, rows, params, cots, row_grad, tb, name):
    t = rows[0].shape[0]
    nr, npar, nc = len(rows), len(params), len(cots)
    want = [i for i, g in enumerate(row_grad) if g is not None]

    def body(*refs):
        ins = [r[...].astype(F32) for r in refs[:nr + npar]]
        cts = tuple(r[...].astype(F32) for r in refs[nr + npar:nr + npar + nc])
        outs = refs[nr + npar + nc:]
        _, vjp = jax.vjp(fn, *ins)
        grads = vjp(cts)
        for o_ref, i in zip(outs[:len(want)], want):
            o_ref[...] = grads[i].astype(o_ref.dtype)
        first = pl.program_id(0) == 0
        for o_ref, g in zip(outs[len(want):], grads[nr:]):
            @pl.when(first)
            def _(o_ref=o_ref, g=g):
                o_ref[...] = g

            @pl.when(jnp.logical_not(first))
            def _(o_ref=o_ref, g=g):
                o_ref[...] += g

    res = pl.pallas_call(
        body, name=name, grid=(t // tb,),
        in_specs=[_row_spec(r, tb) for r in rows] + [pl.BlockSpec(p.shape, lambda i: (0, 0)) for p in params]
        + [_row_spec(c, tb) for c in cots],
        out_specs=[pl.BlockSpec((tb, rows[i].shape[1]), lambda i_: (i_, 0)) for i in want]
        + [pl.BlockSpec(p.shape, lambda i: (0, 0)) for p in params],
        out_shape=[jax.ShapeDtypeStruct(tuple(rows[i].shape), row_grad[i]) for i in want]
        + [jax.ShapeDtypeStruct(p.shape, F32) for p in params],
        compiler_params=_cparams(("arbitrary",)),
    )(*[_row_arr(r) for r in rows], *params, *[_row_arr(c) for c in cots])
    return list(res[:len(want)]), list(res[len(want):])


def _rms(x, g):
    return x * lax.rsqrt(jnp.mean(x * x, axis=-1, keepdims=True) + EPS) * g


def _fn_pre(x, g):
    return (_rms(x, g),)


def _fn_res_pre(x, y, g_post, g_pre):
    x1 = x + _rms(y, g_post)
    return x1, _rms(x1, g_pre)


def _fn_res(x, y, g_post):
    return (x + _rms(y, g_post),)


def _sigmoid(x):
    return 1.0 / (1.0 + jnp.exp(-x))


def _silu(x):
    return x * _sigmoid(x)


def _fn_swiglu(gu):
    return (_silu(gu[:, :D_FF]) * gu[:, D_FF:],)


def _fn_combine(o, lse):
    ls = [lse[:, 256 * g:256 * (g + 1)] for g in range(3)]
    mx = lax.stop_gradient(jnp.maximum(jnp.maximum(ls[0], ls[1]), ls[2]))
    es = [jnp.exp(l - mx) for l in ls]
    inv = 1.0 / (es[0] + es[1] + es[2])
    return (jnp.concatenate([o[:, 256 * g:256 * (g + 1)] * (es[g] * inv) for g in range(3)], axis=1),)


def _fn_outnorm(o_f, o_r, z, gain):
    res = []
    for h in range(DN_HEADS):
        sl = slice(DN_HD * h, DN_HD * (h + 1))
        o = o_f[:, sl] + o_r[:, sl]
        res.append(o * lax.rsqrt(jnp.mean(o * o, axis=-1, keepdims=True) + EPS) * gain * _silu(z[:, sl]))
    return (jnp.concatenate(res, axis=1),)


def _loss_kernel(x, tgt, tb, name):
    t, d = x.shape

    def body(x_ref, t_ref, dx_ref, l_ref, acc_ref):
        i = pl.program_id(0)
        e = x_ref[...] - t_ref[...]
        dx_ref[...] = e * (1.0 / d)
        part = jnp.sum(e * e, axis=0, keepdims=True)

        @pl.when(i == 0)
        def _():
            acc_ref[...] = part

        @pl.when(i > 0)
        def _():
            acc_ref[...] += part

        @pl.when(i == t // tb - 1)
        def _():
            l_ref[...] = jnp.broadcast_to(jnp.sum(acc_ref[...], axis=-1, keepdims=True), (1, 128))

    return pl.pallas_call(
        body, name=name, grid=(t // tb,),
        in_specs=[pl.BlockSpec((tb, d), lambda i: (i, 0))] * 2,
        out_specs=[pl.BlockSpec((tb, d), lambda i: (i, 0)), pl.BlockSpec((1, 128), lambda i: (0, 0))],
        out_shape=[jax.ShapeDtypeStruct((t, d), F32), jax.ShapeDtypeStruct((1, 128), F32)],
        scratch_shapes=[pltpu.VMEM((1, d), F32)],
        compiler_params=_cparams(("arbitrary",)),
    )(x, tgt)


def _band_fn(l_sub, bq, i, q, kw, vw, bm):
    w = bq + 2 * BAND_HALF
    s = lax.dot_general((q * (ATT_HD ** -0.5)).astype(BF16), kw.astype(BF16), (((2,), (2,)), ((0,), (0,))),
                        preferred_element_type=F32) + bm
    kpos = i * bq - BAND_HALF + lax.broadcasted_iota(jnp.int32, (4, bq, w), 2)
    s = jnp.where((kpos >= 0) & (kpos < l_sub), s, NEG)
    m = lax.stop_gradient(jnp.max(s, axis=-1, keepdims=True))
    p = jnp.exp(s - m)
    den = jnp.sum(p, axis=-1, keepdims=True)
    o = lax.dot_general(p.astype(BF16), vw.astype(BF16), (((2,), (1,)), ((0,), (0,))),
                        preferred_element_type=F32) / den
    return o, jnp.broadcast_to(m + jnp.log(den), o.shape)


def _band_specs(l_sub, bq):
    w = bq + 2 * BAND_HALF
    qs = pl.BlockSpec((None, 4, bq, ATT_HD), lambda r, i: (r, 0, i, 0))
    ks = pl.BlockSpec((None, 4, l_sub + 2 * BAND_HALF, ATT_HD), lambda r, i: (r, 0, 0, 0))
    bs = pl.BlockSpec((4, bq, w), lambda r, i: (0, 0, 0))
    return qs, ks, bs


def _band_fwd(q, k, v, bm, dil, l_sub, bq, name):
    w = bq + 2 * BAND_HALF
    qs, ks, bs = _band_specs(l_sub, bq)

    def body(q_ref, k_ref, v_ref, bm_ref, o_ref, l_ref):
        i = pl.program_id(1)
        st = pl.multiple_of(i * bq, bq)
        o, lse = _band_fn(l_sub, bq, i, q_ref[...].astype(F32), k_ref[:, pl.ds(st, w), :].astype(F32),
                          v_ref[:, pl.ds(st, w), :].astype(F32), bm_ref[...])
        o_ref[...] = o
        l_ref[...] = lse

    return pl.pallas_call(
        body, name=name, grid=(dil, l_sub // bq),
        in_specs=[qs, ks, ks, bs], out_specs=[qs, qs],
        out_shape=[jax.ShapeDtypeStruct(q.shape, F32)] * 2,
        compiler_params=_cparams(("parallel", "arbitrary")),
    )(q, k, v, bm)


def _band_bwd(q, k, v, bm, do, dlse, dil, l_sub, bq, name):
    w = bq + 2 * BAND_HALF
    qs, ks, bs = _band_specs(l_sub, bq)

    def body(q_ref, k_ref, v_ref, bm_ref, do_ref, dl_ref, dq_ref, dk_ref, dv_ref, dbm_ref):
        r, i = pl.program_id(0), pl.program_id(1)
        st = pl.multiple_of(i * bq, bq)
        _, vjp = jax.vjp(functools.partial(_band_fn, l_sub, bq, i),
                         q_ref[...].astype(F32), k_ref[:, pl.ds(st, w), :].astype(F32),
                         v_ref[:, pl.ds(st, w), :].astype(F32), bm_ref[...])
        dq, dkw, dvw, dbm = vjp((do_ref[...].astype(F32), dl_ref[...]))
        dq_ref[...] = dq.astype(dq_ref.dtype)

        @pl.when(i == 0)
        def _():
            dk_ref[...] = jnp.zeros_like(dk_ref)
            dv_ref[...] = jnp.zeros_like(dv_ref)

        dk_ref[:, pl.ds(st, w), :] += dkw
        dv_ref[:, pl.ds(st, w), :] += dvw

        @pl.when((i == 0) & (r == 0))
        def _():
            dbm_ref[...] = dbm

        @pl.when((i > 0) | (r > 0))
        def _():
            dbm_ref[...] += dbm

    return pl.pallas_call(
        body, name=name, grid=(dil, l_sub // bq),
        in_specs=[qs, ks, ks, bs, qs, qs], out_specs=[qs, ks, ks, bs],
        out_shape=[jax.ShapeDtypeStruct(q.shape, BF16), jax.ShapeDtypeStruct(k.shape, F32),
                   jax.ShapeDtypeStruct(k.shape, F32), jax.ShapeDtypeStruct(bm.shape, F32)],
        compiler_params=_cparams(("arbitrary", "arbitrary")),
    )(q, k, v, bm, do, dlse)


def _t5_bucket(rel):
    half = REL_BUCKETS // 2
    max_exact = half // 2
    n = np.abs(rel)
    large = max_exact + (np.log(np.maximum(n, 1) / max_exact) / math.log(REL_MAX_DIST / max_exact)
                         * (half - max_exact)).astype(np.int64)
    large = np.minimum(large, half - 1)
    return ((rel > 0) * half + np.where(n < max_exact, n, large)).astype(np.int32)


def _bucket_onehot(dil):
    idx = _t5_bucket(np.arange(-BAND_HALF, BAND_HALF + 1) * dil)
    oh = np.zeros((2 * BAND_HALF + 1, REL_BUCKETS), np.float32)
    oh[np.arange(2 * BAND_HALF + 1), idx] = 1.0
    return oh


def _band_bias(rel_bias, gi, dil, bq):
    w = bq + 2 * BAND_HALF
    nb = 2 * BAND_HALF + 1
    bias = jnp.dot(jnp.asarray(_bucket_onehot(dil)), rel_bias[:, 4 * gi:4 * gi + 4], precision=HI)
    row = jnp.concatenate([bias.T, jnp.full((4, w + 1 - nb), NEG, F32)], axis=1)
    flat = jnp.tile(row, (1, bq))[:, :bq * w]
    return flat.reshape(4, bq, w)


def _relbias_grad(dbms, name):
    nb = 2 * BAND_HALF + 1
    bq = max(d.shape[1] for d in dbms)
    skew = []
    for dbm in dbms:
        bqg, w = dbm.shape[1], dbm.shape[2]
        flat = jnp.pad(dbm.reshape(4, bqg * w), ((0, 0), (0, bqg)))
        skew.append(jnp.pad(flat.reshape(4, bqg, w + 1)[:, :, :nb], ((0, 0), (0, bq - bqg), (0, 256 - nb))))
    sk = jnp.concatenate(skew, axis=0)
    oh = np.zeros((3, 256, 128), np.float32)
    for gi, (_, dil) in enumerate(DIL_GROUPS):
        oh[gi, :2 * BAND_HALF + 1, :REL_BUCKETS] = _bucket_onehot(dil)

    def body(s_ref, oh_ref, o_ref):
        col = jnp.sum(s_ref[...], axis=0, keepdims=True)
        o_ref[...] = jnp.dot(jnp.broadcast_to(col, (8, 256)), oh_ref[...], precision=HI, preferred_element_type=F32)

    out = pl.pallas_call(
        body, name=name, grid=(12,),
        in_specs=[pl.BlockSpec((None, bq, 256), lambda n: (n, 0, 0)),
                  pl.BlockSpec((None, 256, 128), lambda n: (n // 4, 0, 0))],
        out_specs=pl.BlockSpec((None, 8, 128), lambda n: (n, 0, 0)),
        out_shape=jax.ShapeDtypeStruct((12, 8, 128), F32),
        compiler_params=_cparams(("parallel",)),
    )(sk, jnp.asarray(oh))
    return out[:, 0, :REL_BUCKETS].T


def _mem_fn(q, k, v):
    s = lax.dot_general((q * (ATT_HD ** -0.5)).astype(BF16), k.astype(BF16), (((2,), (2,)), ((0,), (0,))),
                        preferred_element_type=F32)
    m = lax.stop_gradient(jnp.max(s, axis=-1, keepdims=True))
    p = jnp.exp(s - m)
    p = p / jnp.sum(p, axis=-1, keepdims=True)
    return lax.dot_general(p.astype(BF16), v.astype(BF16), (((2,), (1,)), ((0,), (0,))), preferred_element_type=F32)


def _mem_specs(tb, ml):
    qs = pl.BlockSpec((MEM_HEADS, tb, ATT_HD), lambda i: (0, i, 0))
    ks = pl.BlockSpec((MEM_HEADS, ml, ATT_HD), lambda i: (0, 0, 0))
    return qs, ks


def _mem_fwd(q, k, v, tb, name):
    qs, ks = _mem_specs(tb, k.shape[1])

    def body(q_ref, k_ref, v_ref, o_ref):
        o_ref[...] = _mem_fn(q_ref[...].astype(F32), k_ref[...], v_ref[...])

    return pl.pallas_call(
        body, name=name, grid=(q.shape[1] // tb,),
        in_specs=[qs, ks, ks], out_specs=qs, out_shape=jax.ShapeDtypeStruct(q.shape, F32),
        compiler_params=_cparams(("parallel",)),
    )(q, k, v)


def _mem_bwd(q, k, v, do, tb, name):
    qs, ks = _mem_specs(tb, k.shape[1])

    def body(q_ref, k_ref, v_ref, do_ref, dq_ref, dk_ref, dv_ref):
        i = pl.program_id(0)
        _, vjp = jax.vjp(_mem_fn, q_ref[...].astype(F32), k_ref[...], v_ref[...])
        dq, dk, dv = vjp(do_ref[...])
        dq_ref[...] = dq

        @pl.when(i == 0)
        def _():
            dk_ref[...] = dk
            dv_ref[...] = dv

        @pl.when(i > 0)
        def _():
            dk_ref[...] += dk
            dv_ref[...] += dv

    return pl.pallas_call(
        body, name=name, grid=(q.shape[1] // tb,),
        in_specs=[qs, ks, ks, qs], out_specs=[qs, ks, ks],
        out_shape=[jax.ShapeDtypeStruct(q.shape, F32), jax.ShapeDtypeStruct(k.shape, F32),
                   jax.ShapeDtypeStruct(k.shape, F32)],
        compiler_params=_cparams(("arbitrary",)),
    )(q, k, v, do)


CONV_PAD = 8


def _conv_post(kind, acc):
    s = _silu(acc)
    if kind == 2:
        return s
    scale = DN_HD ** -0.5 if kind == 0 else 1.0
    return s * lax.rsqrt(jnp.sum(s * s, axis=-1, keepdims=True) + EPS) * scale


def _conv_rows(x_ref, t, start, rt):
    lo = min(max(start, 0), t - rt)
    x = x_ref[pl.ds(lo, rt), :]
    shift = lo - start
    if shift == 0:
        return x
    x = pltpu.roll(x, shift % rt, axis=0)
    row = lax.broadcasted_iota(jnp.int32, x.shape, 0)
    return jnp.where((row >= shift) if shift > 0 else (row < rt + shift), x, 0.0)


def _conv_acc(x_ref, t, w, r0, rt):
    acc = None
    for i in range(DN_CONV):
        term = w[i:i + 1, :] * _conv_rows(x_ref, t, r0 + i - DN_CONV // 2, rt)
        acc = term if acc is None else acc + term
    return acc


def _conv_fwd(x, w8, kind, rt, name):
    t = x.shape[0]

    def body(x_ref, w_ref, o_ref):
        w = w_ref[...]
        for r in range(t // rt):
            o_ref[pl.ds(r * rt, rt), :] = _conv_post(kind, _conv_acc(x_ref, t, w, r * rt, rt))

    return pl.pallas_call(
        body, name=name, grid=(DN_HEADS,),
        in_specs=[pl.BlockSpec((t, DN_HD), lambda j: (0, 6 * kind + j)),
                  pl.BlockSpec((8, DN_HD), lambda j: (0, 6 * kind + j))],
        out_specs=pl.BlockSpec((t, DN_HD), lambda j: (0, j)),
        out_shape=jax.ShapeDtypeStruct((t, TOK_W), F32),
        compiler_params=_cparams(("parallel",)),
    )(x, w8)


def _conv_bwd(x, w8, d_f, d_r, dp, kind, rt, name):
    t = x.shape[0]

    def body(xp_ref, w_ref, df_ref, dr_ref, dp_in, dx_ref, dw_ref, dpad_ref):
        del dp_in
        w = w_ref[...]
        zero = jnp.zeros((CONV_PAD, DN_HD), F32)
        dpad_ref[pl.ds(0, CONV_PAD), :] = zero
        dpad_ref[pl.ds(CONV_PAD + t, CONV_PAD), :] = zero
        dw = [jnp.zeros((1, DN_HD), F32) for _ in range(DN_CONV)]
        for r in range(t // rt):
            rows = pl.ds(r * rt, rt)
            acc = _conv_acc(xp_ref, t, w, r * rt, rt)
            _, vjp = jax.vjp(functools.partial(_conv_post, kind), acc)
            (dacc,) = vjp(df_ref[rows, :] + dr_ref[rows, :])
            dpad_ref[pl.ds(CONV_PAD + r * rt, rt), :] = dacc
            for i in range(DN_CONV):
                xs = _conv_rows(xp_ref, t, r * rt + i - DN_CONV // 2, rt)
                dw[i] = dw[i] + jnp.sum(dacc * xs, axis=0, keepdims=True)
        dw_ref[...] = jnp.concatenate(dw + [jnp.zeros((8 - DN_CONV, DN_HD), F32)], axis=0)
        for r in range(t // rt):
            acc = None
            for i in range(DN_CONV):
                term = w[i:i + 1, :] * dpad_ref[pl.ds(CONV_PAD + r * rt - i + DN_CONV // 2, rt), :]
                acc = term if acc is None else acc + term
            dx_ref[pl.ds(r * rt, rt), :] = acc.astype(dx_ref.dtype)

    return pl.pallas_call(
        body, name=name, grid=(DN_HEADS,),
        in_specs=[pl.BlockSpec((t, DN_HD), lambda j: (0, 6 * kind + j)),
                  pl.BlockSpec((8, DN_HD), lambda j: (0, 6 * kind + j)),
                  pl.BlockSpec((t, DN_HD), lambda j: (0, j)),
                  pl.BlockSpec((t, DN_HD), lambda j: (0, j)),
                  pl.BlockSpec(memory_space=pl.ANY)],
        out_specs=[pl.BlockSpec((t, DN_HD), lambda j: (0, 6 * kind + j)),
                   pl.BlockSpec((8, DN_HD), lambda j: (0, j))],
        out_shape=[jax.ShapeDtypeStruct(dp.shape, dp.dtype), jax.ShapeDtypeStruct((8, TOK_W), F32)],
        input_output_aliases={4: 0},
        scratch_shapes=[pltpu.VMEM((t + 2 * CONV_PAD, DN_HD), F32)],
        compiler_params=_cparams(("parallel",)),
    )(x, w8, d_f, d_r, dp)


def _softplus(x):
    e = jnp.exp(-jnp.abs(x))
    return jnp.maximum(x, 0.0) + jnp.where(e < 1e-4, e - 0.5 * e * e, jnp.log(1.0 + e))


_NN = (((2,), (1,)), ((0,), (0,)))
_NT = (((2,), (2,)), ((0,), (0,)))
_TN = (((1,), (1,)), ((0,), (0,)))


def _dot(a, b, dims=_NN):
    return lax.dot_general(a.astype(BF16), b.astype(BF16), dims, preferred_element_type=F32)


def _hi_lo(x):
    hi = x.astype(BF16)
    return hi, (x - hi.astype(F32)).astype(BF16)


def _mask_dot(mask_bf16, x, dims):
    x1 = x.astype(BF16)
    r = x - x1.astype(F32)
    x2, x3 = _hi_lo(r)
    d = functools.partial(lax.dot_general, dimension_numbers=dims, preferred_element_type=F32)
    return d(mask_bf16, x1) + d(mask_bf16, x2) + d(mask_bf16, x3)


@jax.custom_vjp
def _dot_mask(mask_bf16, x):
    return _mask_dot(mask_bf16, x, _NN)


def _dot_mask_fwd(mask_bf16, x):
    return _mask_dot(mask_bf16, x, _NN), mask_bf16


def _dot_mask_bwd(mask_bf16, ct):
    return jnp.zeros_like(mask_bf16), _mask_dot(mask_bf16, ct, _TN)


_dot_mask.defvjp(_dot_mask_fwd, _dot_mask_bwd)


def _unit_solve_pass(lmat, rhs, masks):
    ainv = masks[6] - lmat * masks[0]
    for sh in range(1, 6):
        ainv = ainv - _dot(_dot(ainv, lmat * masks[sh]), ainv)
    return _dot(ainv, rhs), ainv


@jax.custom_vjp
def _unit_solve(lmat, rhs, masks):
    return _unit_solve_pass(lmat, rhs, masks)[0]


def _unit_solve_fwd(lmat, rhs, masks):
    sol, ainv = _unit_solve_pass(lmat, rhs, masks)
    return sol, (sol, ainv, masks)


def _unit_solve_bwd(res, ct):
    sol, ainv, masks = res
    d_rhs = _dot(ainv, ct, _TN)
    return -_dot(d_rhs, sol, _NT), d_rhs, tuple(jnp.zeros_like(m) for m in masks)


_unit_solve.defvjp(_unit_solve_fwd, _unit_solve_bwd)


def _block_masks(rev, row, col):
    c = DN_CHUNK
    prow = jnp.where(rev, c - 1 - row, row)
    pcol = jnp.where(rev, c - 1 - col, col)
    masks = []
    for sh in range(6):
        differ = (prow ^ pcol) >> sh
        miss = (differ ^ 1) + (1 - ((prow >> sh) & 1))
        masks.append(jnp.where(miss == 0, 1.0, 0.0))
    masks.append(jnp.where(row == col, 1.0, 0.0))
    return tuple(masks)


def _dn_chunk(q, k, v, al, be, a_row, dt_row, s):
    n, c = q.shape[0], DN_CHUNK
    rev = lax.broadcasted_iota(jnp.int32, (n, c, c), 0) >= n // 2
    row = lax.broadcasted_iota(jnp.int32, (n, c, c), 1)
    col = lax.broadcasted_iota(jnp.int32, (n, c, c), 2)
    ahead = jnp.where(rev, col - row, row - col)
    incl = ahead >= 0
    strict = ahead > 0
    incl_b = incl.astype(BF16)

    g = -jnp.exp(a_row) * _softplus(al + dt_row)
    beta = _sigmoid(be)
    gc = _dot_mask(incl_b, g)
    gcc = gc[:, :, :c]
    decay = jnp.exp(jnp.where(incl, gcc - jnp.swapaxes(gcc, 1, 2), NEG))
    kb = k * beta
    lmat = jnp.where(strict, _dot(kb, k, _NT) * decay, 0.0)
    rhs = jnp.concatenate([v * beta, kb * jnp.exp(gc)], axis=2)
    sol = _unit_solve(lmat, rhs, _block_masks(rev, row, col))
    u, w = sol[:, :, :DN_HD], sol[:, :, DN_HD:]
    intra = jnp.where(incl, _dot(q, k, _NT) * decay, 0.0)
    v_new = u - _dot(w, s)
    out = _dot(q * jnp.exp(gc), s) + _dot(intra, v_new)
    g_last = jnp.sum(g, axis=1, keepdims=True)
    s_new = s * jnp.exp(g_last) + _dot(k * jnp.exp(g_last - gc), v_new, _TN)
    return out, s_new


DN_HG = 6


def _dn_load(f_refs, r_refs, alf, bef, alr, ber, a_ref, dt_ref):
    sls = [slice(DN_HD * h, DN_HD * (h + 1)) for h in range(DN_HG)]
    toks = [jnp.stack([f[:, sl] for sl in sls] + [r[:, sl] for sl in sls]) for f, r in zip(f_refs, r_refs)]
    al = jnp.concatenate([alf[...], alr[...]], axis=0)
    be = jnp.concatenate([bef[...], ber[...]], axis=0)
    a = jnp.concatenate([a_ref[0], a_ref[1]], axis=0)
    dt = jnp.concatenate([dt_ref[0], dt_ref[1]], axis=0)
    return toks, (al, be, a, dt)


def _dn_views(nc, bwd):
    c, hg = DN_CHUNK, DN_HG
    if bwd:
        f_blk = lambda s: nc - 1 - s
        r_blk = lambda s: s
        st_blk = lambda s: nc - 1 - s
    else:
        f_blk = lambda s: s
        r_blk = lambda s: nc - 1 - s
        st_blk = lambda s: s
    tok_f = pl.BlockSpec((c, hg * DN_HD), lambda g, s: (f_blk(s), g))
    tok_r = pl.BlockSpec((c, hg * DN_HD), lambda g, s: (r_blk(s), g))
    gate_f = pl.BlockSpec((None, hg, c, DN_HD), lambda g, s: (0, g, f_blk(s), 0))
    gate_r = pl.BlockSpec((None, hg, c, DN_HD), lambda g, s: (1, g, r_blk(s), 0))
    par = pl.BlockSpec((2, hg, 1, DN_HD), lambda g, s: (0, g, 0, 0))
    state = pl.BlockSpec((2, hg, None, DN_HD, DN_HD), lambda g, s: (0, g, st_blk(s), 0, 0))
    return tok_f, tok_r, gate_f, gate_r, par, state


def _dn_fwd(q, k, v, al, be, a_rows, dt_rows, name):
    t = q.shape[0]
    c, hg = DN_CHUNK, DN_HG
    nc = t // c
    tok_f, tok_r, gate_f, gate_r, par, state = _dn_views(nc, False)

    def body(qf, kf, vf, qr, kr, vr, alf, bef, alr, ber, a_ref, dt_ref, of_ref, or_ref, st_ref, s_ref):
        @pl.when(pl.program_id(1) == 0)
        def _():
            s_ref[...] = jnp.zeros_like(s_ref)

        (q_, k_, v_), gates = _dn_load((qf, kf, vf), (qr, kr, vr), alf, bef, alr, ber, a_ref, dt_ref)
        s = s_ref[...]
        st_ref[0] = s[:hg]
        st_ref[1] = s[hg:]
        out, s_new = _dn_chunk(q_, k_, v_, *gates, s)
        for h in range(hg):
            sl = slice(DN_HD * h, DN_HD * (h + 1))
            of_ref[:, sl] = out[h]
            or_ref[:, sl] = out[hg + h]
        s_ref[...] = s_new

    return pl.pallas_call(
        body, name=name, grid=(DN_HEADS // hg, nc),
        in_specs=[tok_f] * 3 + [tok_r] * 3 + [gate_f, gate_f, gate_r, gate_r, par, par],
        out_specs=[tok_f, tok_r, state],
        out_shape=[jax.ShapeDtypeStruct((t, TOK_W), F32)] * 2
        + [jax.ShapeDtypeStruct((2, DN_HEADS, nc, DN_HD, DN_HD), F32)],
        scratch_shapes=[pltpu.VMEM((2 * hg, DN_HD, DN_HD), F32)],
        compiler_params=_cparams(("parallel", "arbitrary")),
    )(q, k, v, q, k, v, al, be, al, be, a_rows, dt_rows)


def _dn_bwd(q, k, v, al, be, a_rows, dt_rows, states, do, name):
    t = q.shape[0]
    c, hg = DN_CHUNK, DN_HG
    assert hg == DN_HEADS
    nc = t // c
    tok_f, tok_r, gate_f, gate_r, par, state = _dn_views(nc, True)
    gout_f = pl.BlockSpec((c, DN_HD), lambda g, s: (nc - 1 - s, 0))
    gout_r = pl.BlockSpec((c, DN_HD), lambda g, s: (s, 0))

    def body(qf, kf, vf, qr, kr, vr, alf, bef, alr, ber, a_ref, dt_ref, st_ref, dof, dor,
             dqf, dkf, dvf, dqr, dkr, dvr, dgf, dgr, da_ref, ddt_ref, ds_ref):
        first = pl.program_id(1) == 0

        @pl.when(first)
        def _():
            ds_ref[...] = jnp.zeros_like(ds_ref)
            da_ref[...] = jnp.zeros_like(da_ref)
            ddt_ref[...] = jnp.zeros_like(ddt_ref)

        def lanes(x):
            return jnp.sum(x, axis=-1, keepdims=True)

        (q_, k_, v_, do_), gates = _dn_load((qf, kf, vf, dof), (qr, kr, vr, dor), alf, bef, alr, ber, a_ref, dt_ref)
        s = jnp.concatenate([st_ref[0], st_ref[1]], axis=0)
        _, vjp = jax.vjp(_dn_chunk, q_, k_, v_, *gates, s)
        dq, dk, dv, dal, dbe, da, ddt, ds = vjp((do_, ds_ref[...]))
        for h in range(hg):
            sl = slice(DN_HD * h, DN_HD * (h + 1))
            dqf[:, sl], dkf[:, sl], dvf[:, sl] = dq[h], dk[h], dv[h]
            dqr[:, sl], dkr[:, sl], dvr[:, sl] = dq[hg + h], dk[hg + h], dv[hg + h]
        dal, dbe = lanes(dal), lanes(dbe)
        lane = lax.broadcasted_iota(jnp.int32, (c, DN_HD), 1)
        for d, dg_ref in enumerate((dgf, dgr)):
            dg = jnp.zeros((c, DN_HD), F32)
            for h in range(hg):
                dg = jnp.where(lane == h, dal[d * hg + h], jnp.where(lane == hg + h, dbe[d * hg + h], dg))
            dg_ref[...] = dg
        da = jnp.broadcast_to(lanes(da), da.shape)
        ddt = jnp.broadcast_to(lanes(ddt), ddt.shape)
        da_ref[0] += da[:hg]
        da_ref[1] += da[hg:]
        ddt_ref[0] += ddt[:hg]
        ddt_ref[1] += ddt[hg:]
        ds_ref[...] = ds

    tok = jax.ShapeDtypeStruct((t, TOK_W), F32)
    gate = jax.ShapeDtypeStruct((t, DN_HD), F32)
    parsh = jax.ShapeDtypeStruct((2, DN_HEADS, 1, DN_HD), F32)
    res = pl.pallas_call(
        body, name=name, grid=(DN_HEADS // hg, nc),
        in_specs=[tok_f] * 3 + [tok_r] * 3 + [gate_f, gate_f, gate_r, gate_r, par, par, state, tok_f, tok_r],
        out_specs=[tok_f] * 3 + [tok_r] * 3 + [gout_f, gout_r, par, par],
        out_shape=[tok] * 6 + [gate] * 2 + [parsh] * 2,
        scratch_shapes=[pltpu.VMEM((2 * hg, DN_HD, DN_HD), F32)],
        compiler_params=_cparams(("parallel", "arbitrary")),
    )(q, k, v, q, k, v, al, be, al, be, a_rows, dt_rows, states, do, do)
    dqf, dkf, dvf, dqr, dkr, dvr, dgf, dgr, da, ddt = res
    dgate = jnp.concatenate([dgf[:, :2 * DN_HEADS], dgr[:, :2 * DN_HEADS]], axis=1)
    return (dqf, dkf, dvf), (dqr, dkr, dvr), dgate, da, ddt


BAND_BQ = 256
ROW_TB = 512
MEM_TB = 512
CONV_RT = 512


def _to_sub(x, dil):
    l = x.shape[0] // dil
    return x.reshape(l, dil, 4, ATT_HD).transpose(1, 2, 0, 3)


def _from_sub(x, dil):
    l = x.shape[2]
    return x.transpose(2, 0, 1, 3).reshape(l * dil, 4 * ATT_HD)


def _sub_in(x, col_block, dil, pad, dtype, name):
    if dil > 1:
        y = _to_sub(x[:, 256 * col_block:256 * (col_block + 1)], dil).astype(dtype)
        return jnp.pad(y, ((0, 0), (0, 0), (pad, pad), (0, 0))) if pad else y
    t = x.shape[0]

    def body(x_ref, o_ref):
        if pad:
            zero = jnp.zeros((4, pad, ATT_HD), dtype)
            o_ref[0, :, 0:pad, :] = zero
            o_ref[0, :, pad + t:pad + t + pad, :] = zero
        for h in range(4):
            o_ref[0, h, pad:pad + t, :] = x_ref[:, ATT_HD * h:ATT_HD * (h + 1)].astype(dtype)

    return pl.pallas_call(
        body, name=name, grid=(1,), in_specs=[pl.BlockSpec((t, 256), lambda i: (0, col_block))],
        out_specs=pl.BlockSpec((1, 4, t + 2 * pad, ATT_HD), lambda i: (0, 0, 0, 0)),
        out_shape=jax.ShapeDtypeStruct((1, 4, t + 2 * pad, ATT_HD), dtype), compiler_params=_cparams(("arbitrary",)),
    )(x)


def _sub_out(x, dil, pad, dtype, name):
    if dil > 1:
        return _from_sub(x[:, :, pad:x.shape[2] - pad] if pad else x, dil).astype(dtype)
    t = x.shape[2] - 2 * pad

    def body(x_ref, o_ref):
        for h in range(4):
            o_ref[:, ATT_HD * h:ATT_HD * (h + 1)] = x_ref[0, h, pad:pad + t, :].astype(dtype)

    return pl.pallas_call(
        body, name=name, grid=(1,), in_specs=[pl.BlockSpec(x.shape, lambda i: (0, 0, 0, 0))],
        out_specs=pl.BlockSpec((t, 256), lambda i: (0, 0)), out_shape=jax.ShapeDtypeStruct((t, 256), dtype),
        compiler_params=_cparams(("arbitrary",)),
    )(x)


def _heads_major(x):
    return x.reshape(x.shape[0], MEM_HEADS, ATT_HD).transpose(1, 0, 2)


def _heads_minor(x):
    return x.transpose(1, 0, 2).reshape(x.shape[1], MEM_HEADS * ATT_HD)


def _mem_kv_fwd(mem, gain, w_kv, li):
    (memn,) = _rowwise(_fn_pre, [mem], [gain], [(D, BF16)], mem.shape[0], f"memnorm_fwd{li}")
    kv = _matmul(memn, w_kv, "nn", F32, f"memkv_fwd{li}")
    return _heads_major(kv[:, :MEM_W]), _heads_major(kv[:, MEM_W:]), memn


def _mem_kv_bwd(mem, gain, w_kv, memn, dkm, dvm, li):
    dkv = jnp.concatenate([_heads_minor(dkm), _heads_minor(dvm)], axis=1).astype(BF16)
    dw = _matmul(memn, dkv, "tn", BF16, f"memkv_dw{li}")
    dmemn = _matmul(dkv, w_kv, "nt", F32, f"memkv_dx{li}")
    _, (dgain,) = _rowwise_bwd(_fn_pre, [mem], [gain], [dmemn], [None], mem.shape[0], f"memnorm_bwd{li}")
    return dw, dgain


def _attn_mixer_fwd(p, rel_bias, kv_fn):
    t = p.shape[0]
    saved, outs, lses = [], [], []
    for gi, (_, dil) in enumerate(DIL_GROUPS):
        l_sub = t // dil
        bq = min(BAND_BQ, l_sub)
        q = _sub_in(p, gi, dil, 0, p.dtype, f"sub_q{gi}")
        k = _sub_in(p, 3 + gi, dil, BAND_HALF, p.dtype, f"sub_k{gi}")
        v = _sub_in(p, 6 + gi, dil, BAND_HALF, p.dtype, f"sub_v{gi}")
        bm = _band_bias(rel_bias, gi, dil, bq)
        o, lse = _band_fwd(q, k, v, bm, dil, l_sub, bq, f"band_fwd{gi}")
        outs.append(_sub_out(o, dil, 0, F32, f"sub_o{gi}"))
        lses.append(_sub_out(lse, dil, 0, F32, f"sub_lse{gi}"))
        saved.append((q, k, v, bm))
    o_all = jnp.concatenate(outs, axis=1)
    lse_all = jnp.concatenate(lses, axis=1)
    (mixed,) = _rowwise(_fn_combine, [o_all, lse_all], [], [(TOK_W, BF16)], ROW_TB, "combine_fwd")
    qm = _heads_major(p[:, 3 * TOK_W:])
    km, vm, memn = kv_fn(mixed)
    memo = _mem_fwd(qm, km, vm, min(MEM_TB, t), "mem_fwd0")
    cat = jnp.concatenate([mixed, _heads_minor(memo).astype(BF16)], axis=1)
    return cat, (saved, o_all, lse_all, qm), (km, vm, memn)


def _attn_mixer_bwd(dcat, res, km, vm):
    saved, o_all, lse_all, qm = res
    t = dcat.shape[0]
    (do_all, dlse_all), _ = _rowwise_bwd(_fn_combine, [o_all, lse_all], [], [_Cols(dcat, TOK_W, 0)], [BF16, F32],
                                         ROW_TB, "combine_bwd")
    dqs, dks, dvs, dbms = [], [], [], []
    for gi, (_, dil) in enumerate(DIL_GROUPS):
        l_sub = t // dil
        bq = min(BAND_BQ, l_sub)
        q, k, v, bm = saved[gi]
        do = _sub_in(do_all, gi, dil, 0, do_all.dtype, f"sub_do{gi}")
        dl = _sub_in(dlse_all, gi, dil, 0, F32, f"sub_dlse{gi}")
        dq, dk, dv, dbm = _band_bwd(q, k, v, bm, do, dl, dil, l_sub, bq, f"band_bwd{gi}")
        dqs.append(_sub_out(dq, dil, 0, BF16, f"sub_dq{gi}"))
        dks.append(_sub_out(dk, dil, BAND_HALF, BF16, f"sub_dk{gi}"))
        dvs.append(_sub_out(dv, dil, BAND_HALF, BF16, f"sub_dv{gi}"))
        dbms.append(dbm)
    dqm, dkm, dvm = _mem_bwd(qm, km, vm, _heads_major(dcat[:, TOK_W:]), min(MEM_TB, t), "mem_bwd0")
    dp = jnp.concatenate(dqs + dks + dvs + [_heads_minor(dqm).astype(BF16)], axis=1)
    return dp, _relbias_grad(dbms, "relbias_grad"), dkm, dvm


def _dn_mixer_fwd(p, conv_w, a_log, dt_bias, out_norm, km, vm):
    t = p.shape[0]
    rt = min(CONV_RT, t)
    xp = p
    w8 = jnp.pad(conv_w, ((0, 8 - DN_CONV), (0, 0)))
    q = _conv_fwd(xp, w8, 0, rt, "conv_fwd_q")
    k = _conv_fwd(xp, w8, 1, rt, "conv_fwd_k")
    v = _conv_fwd(xp, w8, 2, rt, "conv_fwd_v")
    gate = p[:, 4 * TOK_W:4 * TOK_W + 4 * DN_HEADS].reshape(t, 2, 2, DN_HEADS)
    bshape = (2, DN_HEADS, t, DN_HD)
    al = jnp.broadcast_to(gate[:, :, 0, :].transpose(1, 2, 0)[..., None], bshape)
    be = jnp.broadcast_to(gate[:, :, 1, :].transpose(1, 2, 0)[..., None], bshape)
    a_rows = jnp.broadcast_to(a_log[:, :, None, None], (2, DN_HEADS, 1, DN_HD))
    dt_rows = jnp.broadcast_to(dt_bias[:, :, None, None], (2, DN_HEADS, 1, DN_HD))
    o_f, o_r, states = _dn_fwd(q, k, v, al, be, a_rows, dt_rows, "dn_fwd")
    gain = out_norm.reshape(1, DN_HD)
    (og,) = _rowwise(_fn_outnorm, [o_f, o_r, _Cols(p, TOK_W, 3)], [gain], [(TOK_W, BF16)], ROW_TB, "outnorm_fwd")
    qm = _heads_major(p[:, 4 * TOK_W + 4 * DN_HEADS:DN_IN])
    memo = _mem_fwd(qm, km, vm, min(MEM_TB, t), "mem_fwd1")
    cat = jnp.concatenate([og, _heads_minor(memo).astype(BF16)], axis=1)
    return cat, (xp, w8, q, k, v, al, be, a_rows, dt_rows, o_f, o_r, states, gain, qm)


def _dn_mixer_bwd(dcat, res, km, vm):
    xp, w8, q, k, v, al, be, a_rows, dt_rows, o_f, o_r, states, gain, qm = res
    t = dcat.shape[0]
    rt = min(CONV_RT, t)
    (do, dz), (dgain,) = _rowwise_bwd(_fn_outnorm, [o_f, o_r, _Cols(xp, TOK_W, 3)], [gain],
                                      [_Cols(dcat, TOK_W, 0)], [F32, None, BF16],
                                      ROW_TB, "outnorm_bwd")
    d_f, d_r, dgate, da, ddt = _dn_bwd(q, k, v, al, be, a_rows, dt_rows, states, do, "dn_bwd")
    dqm, dkm, dvm = _mem_bwd(qm, km, vm, _heads_major(dcat[:, TOK_W:]), min(MEM_TB, t), "mem_bwd1")
    rest = jnp.concatenate([dgate.astype(BF16), _heads_minor(dqm).astype(BF16),
                            jnp.zeros((t, DN_IN_PAD - DN_IN), BF16)], axis=1)
    dp = lax.dynamic_update_slice(lax.empty((t, DN_IN_PAD), BF16), dz, (0, 3 * TOK_W))
    dp = lax.dynamic_update_slice(dp, rest, (0, 4 * TOK_W))
    dws = []
    for kind, nm in enumerate("qkv"):
        dp, dw = _conv_bwd(xp, w8, d_f[kind], d_r[kind], dp, kind, rt, f"conv_bwd_{nm}")
        dws.append(dw)
    dconv = jnp.concatenate(dws, axis=1)[:DN_CONV]
    return dp, dconv, da[:, :, 0, 0], ddt[:, :, 0, 0], dgain.reshape(DN_HD), dkm, dvm


SWI_TB = 256


def _ffn_fwd(h, w_gu_t, w_d, li):
    gu = _matmul(h, w_gu_t, "nt", BF16, f"ffn_gu{li}")
    (a,) = _rowwise(_fn_swiglu, [gu], [], [(D_FF, BF16)], SWI_TB, f"swiglu_fwd{li}")
    return _matmul(a, w_d, "nn", F32, f"ffn_down{li}"), gu, a


def _ffn_bwd(df, h, w_gu_t, w_d, gu, a, li):
    da = _matmul(df, w_d, "nt", BF16, f"ffn_down_dx{li}")
    dwd = _matmul(a, df, "tn", BF16, f"ffn_down_dw{li}")
    (dgu,), _ = _rowwise_bwd(_fn_swiglu, [gu], [], [da], [BF16], SWI_TB, f"swiglu_bwd{li}")
    dh = _matmul(dgu, w_gu_t, "nn", BF16, f"ffn_gu_dx{li}")
    dwgu_t = _matmul(dgu, h, "tn", BF16, f"ffn_gu_dw{li}")
    return dh, dwgu_t, dwd


def _fn_first(x, g):
    return x, _rms(x, g)


def _me_xyc():
    return lax.axis_index("x"), lax.axis_index("y"), lax.axis_index("c")


def _flip(coords, k):
    x, y, c = coords
    return (1 - x if k & 4 else x, 1 - y if k & 2 else y, 1 - c if k & 1 else c)


def _index(coords):
    x, y, c = coords
    return 4 * x + 2 * y + c


def _window(ref, axis, size, d):
    idx = [slice(None)] * len(ref.shape)
    idx[axis] = pl.ds(pl.multiple_of(d * size, size), size)
    return ref.at[tuple(idx)]


def _comm_call(body, n, ins, out_shapes, name):
    hbm = pl.BlockSpec(memory_space=pl.ANY)
    return pl.pallas_call(
        body, name=name, in_specs=[hbm] * n, out_specs=[hbm] * n, out_shape=out_shapes,
        scratch_shapes=[pltpu.SemaphoreType.DMA((N_DEV - 1, n)), pltpu.SemaphoreType.DMA((N_DEV - 1, n)),
                        pltpu.SemaphoreType.DMA((n,))],
    )(*ins)


def _run_exchange(n, local, remote, send_sems, recv_sems):
    me = _me_xyc()
    locs = [local(p) for p in range(n)]
    for cp in locs:
        cp.start()
    sends = [remote(k, p, me, _flip(me, k)) for k in range(1, N_DEV) for p in range(n)]
    for cp in sends:
        cp.start()
    for k in range(1, N_DEV):
        for p in range(n):
            remote(k, p, _flip(me, k), me).wait_recv()
    for cp in sends:
        cp.wait_send()
    for cp in locs:
        cp.wait()


def _all_gather_two_level(shard, axis, name):
    size = shard.shape[axis]

    def body(x_ref, o_ref, send_sems, recv_sems, loc_sem):
        x, y, c = _me_xyc()
        me, sibling = (x, y, c), (x, y, 1 - c)
        chips = [(1 - x, y), (x, 1 - y), (1 - x, 1 - y)]

        def part(owner):
            return _window(o_ref, axis, size, _index(owner))

        def copy(k, owner, to, src=None):
            return pltpu.make_async_remote_copy(
                src_ref=part(owner) if src is None else src, dst_ref=part(owner),
                send_sem=send_sems.at[k], recv_sem=recv_sems.at[k], device_id=to, device_id_type=MESH)

        mine = pltpu.make_async_copy(x_ref, part(me), loc_sem)
        mine.start()
        first = [copy(0, me, sibling, src=x_ref)]
        first += [copy(1 + j, me, (*chip, c), src=x_ref) for j, chip in enumerate(chips)]
        for cp in first:
            cp.start()
        passed = [copy(4 + j, (*chip, c), sibling) for j, chip in enumerate(chips)]
        for j, chip in enumerate(chips):
            copy(1 + j, (*chip, c), me).wait_recv()
            passed[j].start()
        copy(0, sibling, me).wait_recv()
        for j, chip in enumerate(chips):
            copy(4 + j, (*chip, 1 - c), me).wait_recv()
        for cp in first + passed:
            cp.wait_send()
        mine.wait()

    full = shard.shape[:axis] + (N_DEV * size,) + shard.shape[axis + 1:]
    hbm = pl.BlockSpec(memory_space=pl.ANY)
    return pl.pallas_call(
        body, name=name, in_specs=[hbm], out_specs=hbm, out_shape=jax.ShapeDtypeStruct(full, shard.dtype),
        scratch_shapes=[pltpu.SemaphoreType.DMA((N_DEV - 1,)), pltpu.SemaphoreType.DMA((N_DEV - 1,)),
                        pltpu.SemaphoreType.DMA],
    )(shard)


def _all_gather(shards, axes, name):
    n = len(shards)
    sizes = [s.shape[a] for s, a in zip(shards, axes)]

    def body(*refs):
        ins, outs = refs[:n], refs[n:2 * n]
        send_sems, recv_sems, loc_sems = refs[2 * n:]
        me = _me_xyc()

        def local(p):
            return pltpu.make_async_copy(ins[p], _window(outs[p], axes[p], sizes[p], _index(me)), loc_sems.at[p])

        def remote(k, p, owner, to):
            return pltpu.make_async_remote_copy(
                src_ref=ins[p], dst_ref=_window(outs[p], axes[p], sizes[p], _index(owner)),
                send_sem=send_sems.at[k - 1, p], recv_sem=recv_sems.at[k - 1, p], device_id=to, device_id_type=MESH)

        _run_exchange(n, local, remote, send_sems, recv_sems)

    def full(s, a):
        return s.shape[:a] + (N_DEV * s.shape[a],) + s.shape[a + 1:]

    return _comm_call(body, n, shards, [jax.ShapeDtypeStruct(full(s, a), s.dtype) for s, a in zip(shards, axes)], name)


def _exchange(fulls, axes, name):
    n = len(fulls)
    sizes = [None if a is None else f.shape[a] // N_DEV for f, a in zip(fulls, axes)]

    def part_shape(f, a):
        return f.shape if a is None else f.shape[:a] + (f.shape[a] // N_DEV,) + f.shape[a + 1:]

    def body(*refs):
        ins, outs = refs[:n], refs[n:2 * n]
        send_sems, recv_sems, loc_sems = refs[2 * n:]
        me = _me_xyc()

        def src(p, to):
            return ins[p] if axes[p] is None else _window(ins[p], axes[p], sizes[p], _index(to))

        def local(p):
            return pltpu.make_async_copy(src(p, me), outs[p].at[_index(me)], loc_sems.at[p])

        def remote(k, p, sender, to):
            return pltpu.make_async_remote_copy(
                src_ref=src(p, to), dst_ref=outs[p].at[_index(sender)],
                send_sem=send_sems.at[k - 1, p], recv_sem=recv_sems.at[k - 1, p], device_id=to, device_id_type=MESH)

        _run_exchange(n, local, remote, send_sems, recv_sems)

    return _comm_call(body, n, fulls,
                      [jax.ShapeDtypeStruct((N_DEV,) + part_shape(f, a), f.dtype) for f, a in zip(fulls, axes)], name)


_HBM = pl.BlockSpec(memory_space=pltpu.HBM)
_SEM = pl.BlockSpec(memory_space=pltpu.SEMAPHORE)
_EFFECT = pltpu.SideEffectType.DATAFLOW_SIDE_EFFECTING


def _in_hbm(a):
    return pltpu.with_memory_space_constraint(a, pltpu.HBM)


def _split_start(srcs, lands, after, descr, name):
    n = len(srcs)

    def body(*refs):
        ins, lnd = refs[:n], refs[n:2 * n]
        send_sems, recv_sems = refs[2 * n + 1], refs[2 * n + 2]
        token = refs[-1]
        me = _me_xyc()
        for k in range(1, N_DEV):
            for p in range(n):
                descr(k, p, ins, lnd, send_sems, recv_sems, me, _flip(me, k)).start()
        token[...] = jnp.zeros_like(token)

    sems = pltpu.SemaphoreType.DMA(((N_DEV - 1) * n,))
    res = pl.pallas_call(
        body, name=name,
        out_shape=(sems, sems, *[pltpu.HBM(a.shape, a.dtype) for a in (*srcs, *lands)],
                   jax.ShapeDtypeStruct((8, 128), F32)),
        in_specs=[_HBM] * (2 * n) + [pl.BlockSpec(memory_space=pl.ANY)],
        out_specs=(_SEM, _SEM, *[_HBM] * (2 * n), pl.BlockSpec(memory_space=pltpu.VMEM)),
        input_output_aliases={i: 2 + i for i in range(2 * n)},
        compiler_params=pltpu.CompilerParams(has_side_effects=_EFFECT),
    )(*[_in_hbm(a) for a in (*srcs, *lands)], after)
    return res[0], res[1], res[2:2 + n], res[2 + n:2 + 2 * n], res[-1]


def _split_wait(send_sems, recv_sems, srcs, lands, after, descr, name):
    n = len(srcs)

    def body(*refs):
        ins, lnd = refs[:n], refs[n:2 * n]
        s_sems, r_sems = refs[2 * n], refs[2 * n + 1]
        me = _me_xyc()
        for k in range(1, N_DEV):
            for p in range(n):
                peer = _flip(me, k)
                descr(k, p, ins, lnd, s_sems, r_sems, me, peer).wait_send()
                descr(k, p, ins, lnd, s_sems, r_sems, peer, me).wait_recv()

    res = pl.pallas_call(
        body, name=name,
        out_shape=tuple(pltpu.HBM(a.shape, a.dtype) for a in (*srcs, *lands)),
        in_specs=[_HBM] * (2 * n) + [_SEM, _SEM, pl.BlockSpec(memory_space=pl.ANY)],
        out_specs=tuple([_HBM] * (2 * n)),
        input_output_aliases={i: i for i in range(2 * n)},
        compiler_params=pltpu.CompilerParams(has_side_effects=_EFFECT),
    )(*srcs, *lands, send_sems, recv_sems, after)
    return list(res[n:])


def _gather_descr(axes, sizes):
    def descr(k, p, ins, lnd, send_sems, recv_sems, sender, dest):
        return pltpu.make_async_remote_copy(
            src_ref=ins[p], dst_ref=_window(lnd[p], axes[p], sizes[p], _index(sender)),
            send_sem=send_sems.at[(k - 1) * len(axes) + p], recv_sem=recv_sems.at[(k - 1) * len(axes) + p],
            device_id=dest, device_id_type=MESH)
    return descr


def _exchange_descr(axes, sizes):
    def descr(k, p, ins, lnd, send_sems, recv_sems, sender, dest):
        return pltpu.make_async_remote_copy(
            src_ref=_window(ins[p], axes[p], sizes[p], _index(dest)), dst_ref=lnd[p].at[_index(sender)],
            send_sem=send_sems.at[(k - 1) * len(axes) + p], recv_sem=recv_sems.at[(k - 1) * len(axes) + p],
            device_id=dest, device_id_type=MESH)
    return descr


def _gather_begin(shards, axes, after, name):
    sizes = [s.shape[a] for s, a in zip(shards, axes)]
    me = _index(_me_xyc())
    lands = []
    for s, a, sz in zip(shards, axes, sizes):
        full = s.shape[:a] + (N_DEV * sz,) + s.shape[a + 1:]
        lands.append(lax.dynamic_update_slice_in_dim(lax.empty(full, s.dtype), s, me * sz, a))
    descr = _gather_descr(axes, sizes)
    send_sems, recv_sems, srcs, lands, token = _split_start(shards, lands, after, descr, name)
    return (send_sems, recv_sems, srcs, lands, descr), token


def _exchange_begin(fulls, axes, after, name):
    sizes = [f.shape[a] // N_DEV for f, a in zip(fulls, axes)]
    me = _index(_me_xyc())
    lands = []
    for f, a, sz in zip(fulls, axes, sizes):
        own = lax.dynamic_slice_in_dim(f, me * sz, sz, a)
        lands.append(lax.dynamic_update_slice_in_dim(lax.empty((N_DEV,) + own.shape, f.dtype), own[None], me, 0))
    descr = _exchange_descr(axes, sizes)
    send_sems, recv_sems, srcs, lands, token = _split_start(fulls, lands, after, descr, name)
    return (send_sems, recv_sems, srcs, lands, descr), token


def _split_end(handle, after, name):
    send_sems, recv_sems, srcs, lands, descr = handle
    return _split_wait(send_sems, recv_sems, srcs, lands, after, descr, name)


def _adam_math(g, w, m, v):
    m = ADAM_B1 * m + (1.0 - ADAM_B1) * g
    v = ADAM_B2 * v + (1.0 - ADAM_B2) * (g * g)
    m_hat = m / (1.0 - ADAM_B1 ** ADAM_STEP)
    v_hat = v / (1.0 - ADAM_B2 ** ADAM_STEP)
    delta = -ADAM_LR * (m_hat / (jnp.sqrt(v_hat) + ADAM_EPS) + ADAM_WD * w)
    return delta, m, v


def _sum_slabs(r_ref):
    g = r_ref[0].astype(F32)
    for s in range(1, N_DEV):
        g = g + r_ref[s].astype(F32)
    return g


def _adamw_reduce(recv, w, m, v, tb, name):
    r, c = w.shape

    def body(r_ref, w_ref, m_ref, v_ref, g_ref, d_ref, nm_ref, nv_ref):
        g = _sum_slabs(r_ref)
        g_ref[...] = g
        d_ref[...], nm_ref[...], nv_ref[...] = _adam_math(g, w_ref[...], m_ref[...], v_ref[...])

    blk = pl.BlockSpec((tb, c), lambda i: (i, 0))
    return pl.pallas_call(
        body, name=name, grid=(r // tb,),
        in_specs=[pl.BlockSpec((N_DEV, tb, c), lambda i: (0, i, 0)), blk, blk, blk],
        out_specs=[blk] * 4, out_shape=[jax.ShapeDtypeStruct((r, c), F32)] * 4,
        compiler_params=_cparams(("parallel",)),
    )(recv, w, m, v)


def _reduce8(recv, tb, name):
    r, c = recv.shape[1:]

    def body(r_ref, g_ref):
        g_ref[...] = _sum_slabs(r_ref)

    return pl.pallas_call(
        body, name=name, grid=(r // tb,),
        in_specs=[pl.BlockSpec((N_DEV, tb, c), lambda i: (0, i, 0))],
        out_specs=pl.BlockSpec((tb, c), lambda i: (i, 0)), out_shape=jax.ShapeDtypeStruct((r, c), F32),
        compiler_params=_cparams(("parallel",)),
    )(recv)


def _adamw(g, w, m, v, tb, name):
    r, c = w.shape

    def body(g_ref, w_ref, m_ref, v_ref, d_ref, nm_ref, nv_ref):
        d_ref[...], nm_ref[...], nv_ref[...] = _adam_math(g_ref[...], w_ref[...], m_ref[...], v_ref[...])

    blk = pl.BlockSpec((tb, c), lambda i: (i, 0))
    return pl.pallas_call(
        body, name=name, grid=(r // tb,), in_specs=[blk] * 4, out_specs=[blk] * 3,
        out_shape=[jax.ShapeDtypeStruct((r, c), F32)] * 3, compiler_params=_cparams(("parallel",)),
    )(g, w, m, v)


DN_IN_SHARD = DN_IN // N_DEV
DN_IN_SHARD_PAD = 432
CONV_SHARD = (1, DN_CONV, 288)


def _pack_small(arrs, rows):
    flat = jnp.concatenate([a.astype(F32).reshape(-1) for a in arrs])
    return jnp.pad(flat, (0, rows * PACK_C - flat.size)).reshape(rows, PACK_C)


def _unpack_small(packed, shapes):
    flat, out, off = packed.reshape(-1), [], 0
    for shp in shapes:
        n = int(np.prod(shp))
        out.append(flat[off:off + n].reshape(shp))
        off += n
    return out


def kernel(x, mem, rel_bias, att_w_in, att_w_out, dn_w_in, dn_conv, dn_a_log, dn_dt_bias, dn_out_norm, dn_w_out, mem_norm, mem_w_kv, norm_mix_pre, norm_mix_post, norm_ffn_pre, norm_ffn_post, ffn_w_gate_up, ffn_w_down, loss_target, m_rel_bias, m_att_w_in, m_att_w_out, m_dn_w_in, m_dn_conv, m_dn_a_log, m_dn_dt_bias, m_dn_out_norm, m_dn_w_out, m_mem_norm, m_mem_w_kv, m_norm_mix_pre, m_norm_mix_post, m_norm_ffn_pre, m_norm_ffn_post, m_ffn_w_gate_up, m_ffn_w_down, v_rel_bias, v_att_w_in, v_att_w_out, v_dn_w_in, v_dn_conv, v_dn_a_log, v_dn_dt_bias, v_dn_out_norm, v_dn_w_out, v_mem_norm, v_mem_w_kv, v_norm_mix_pre, v_norm_mix_post, v_norm_ffn_pre, v_norm_ffn_post, v_ffn_w_gate_up, v_ffn_w_down):
    x0, mem0, tgt = x[0], mem[0], loss_target[0]
    t = x0.shape[0]
    axes = ("x", "y", "c")

    def t_shard(w):
        return jnp.swapaxes(w, 1, 2).astype(BF16)

    dn_in_pad = ((0, 0), (0, DN_IN_SHARD_PAD - DN_IN_SHARD), (0, 0))
    w_att_in_t = _all_gather_two_level(t_shard(att_w_in), 1, "allgather_first")
    w_att_in_t = w_att_in_t[0]
    gu_t, down = t_shard(ffn_w_gate_up), ffn_w_down.astype(BF16)
    gather_o, tok_o = _gather_begin([att_w_out.astype(BF16), mem_w_kv.astype(BF16)], [1, 1], w_att_in_t,
                                    "gather_att_out_start")
    gather_a, tok_a = _gather_begin([gu_t[0:1], down[0:1]], [1, 1], tok_o, "gather_ffn0_start")
    gather_b, tok_b = _gather_begin(
        [jnp.pad(t_shard(dn_w_in), dn_in_pad), dn_w_out.astype(BF16), gu_t[1:2], down[1:2], dn_conv],
        [1, 1, 1, 1, 0], tok_a, "gather_layer1_start")

    def gain(a, i):
        return a[i].reshape(1, D)

    (h0,) = _rowwise(_fn_pre, [x0], [gain(norm_mix_pre, 0) + tok_b[0:1, 0:1]], [(D, BF16)], ROW_TB, "pre0")
    p0 = _matmul(h0, w_att_in_t, "nt", BF16, "att_in")
    late = {}

    def kv0(after):
        late["w_att_out"], late["w_kv"] = _split_end(gather_o, after, "gather_att_out_wait")
        return _mem_kv_fwd(mem0, gain(mem_norm, 0), late["w_kv"][0], 0)

    cat0, res0, (km0, vm0, memn0) = _attn_mixer_fwd(p0, rel_bias, kv0)
    w_att_out, w_kv = late["w_att_out"][0], late["w_kv"]
    y0 = _matmul(cat0, w_att_out, "nn", F32, "att_out")
    g_a = [gain(norm_mix_post, 0), gain(norm_ffn_pre, 0)]
    x1, h1 = _rowwise(_fn_res_pre, [x0, y0], g_a, [(D, F32), (D, BF16)], ROW_TB, "res_pre0")
    w_gu_t0, w_down0 = [w[0] for w in _split_end(gather_a, h1, "gather_ffn0_wait")]
    f0, gu0, a0 = _ffn_fwd(h1, w_gu_t0, w_down0, 0)
    g_b = [gain(norm_ffn_post, 0), gain(norm_mix_pre, 1)]
    x2, h2 = _rowwise(_fn_res_pre, [x1, f0], g_b, [(D, F32), (D, BF16)], ROW_TB, "res_pre1")
    km1, vm1, memn1 = _mem_kv_fwd(mem0, gain(mem_norm, 1), w_kv[1], 1)
    w_dn_in_g, w_dn_out, w_gu_t1, w_down1, conv_g = _split_end(gather_b, h2, "gather_layer1_wait")
    w_dn_in_g, w_dn_out, w_gu_t1, w_down1 = w_dn_in_g[0], w_dn_out[0], w_gu_t1[0], w_down1[0]
    conv_full = conv_g.transpose(1, 0, 2).reshape(DN_CONV, 3 * TOK_W)
    w_dn_in_t = jnp.concatenate(
        [w_dn_in_g[DN_IN_SHARD_PAD * j:DN_IN_SHARD_PAD * j + DN_IN_SHARD] for j in range(N_DEV)]
        + [jnp.zeros((DN_IN_PAD - DN_IN, D), BF16)], axis=0)
    p1 = _matmul(h2, w_dn_in_t, "nt", F32, "dn_in")
    cat1, res1 = _dn_mixer_fwd(p1, conv_full, dn_a_log[0], dn_dt_bias[0], dn_out_norm[0], km1, vm1)
    y1 = _matmul(cat1, w_dn_out, "nn", F32, "dn_out")
    g_c = [gain(norm_mix_post, 1), gain(norm_ffn_pre, 1)]
    x3, h3 = _rowwise(_fn_res_pre, [x2, y1], g_c, [(D, F32), (D, BF16)], ROW_TB, "res_pre2")
    f1, gu1, a1 = _ffn_fwd(h3, w_gu_t1, w_down1, 1)
    g_d = [gain(norm_ffn_post, 1)]
    (x4,) = _rowwise(_fn_res, [x3, f1], g_d, [(D, F32)], ROW_TB, "res3")
    dx4, lrow = _loss_kernel(x4, tgt, ROW_TB, "loss")
    loss = lax.psum(lrow[0, 0] * (0.5 / D), axes)

    (df1,), (dg_fpost1,) = _rowwise_bwd(_fn_res, [x3, f1], g_d, [dx4], [None, BF16], ROW_TB, "res3_bwd")
    dh3, dwgu1, dwd1 = _ffn_bwd(df1, h3, w_gu_t1, w_down1, gu1, a1, 1)
    (dx2, dy1), (dg_mpost1, dg_fpre1) = _rowwise_bwd(_fn_res_pre, [x2, y1], g_c, [dx4, dh3], [F32, BF16],
                                                     ROW_TB, "res_pre2_bwd")
    dcat1 = _matmul(dy1, w_dn_out, "nt", F32, "dn_out_dx")
    dw_dn_out = _matmul(cat1, dy1, "tn", BF16, "dn_out_dw")
    dp1, dconv, da_log, ddt_bias, dout_norm, dkm1, dvm1 = _dn_mixer_bwd(dcat1, res1, km1, vm1)
    dwkv1, dg_mem1 = _mem_kv_bwd(mem0, gain(mem_norm, 1), w_kv[1], memn1, dkm1, dvm1, 1)
    dh2 = _matmul(dp1, w_dn_in_t, "nn", BF16, "dn_in_dx")
    dw_dn_in_t = _matmul(dp1, h2, "tn", BF16, "dn_in_dw")
    dn_in_parts = [jnp.pad(dw_dn_in_t[DN_IN_SHARD * j:DN_IN_SHARD * (j + 1)],
                           ((0, DN_IN_SHARD_PAD - DN_IN_SHARD), (0, 0))) for j in range(N_DEV)]
    xch_b, tok = _exchange_begin(
        [jnp.concatenate(dn_in_parts, axis=0)[None], dw_dn_out[None], dwkv1[None], dwgu1[None], dwd1[None]],
        [1, 1, 1, 1, 1], dh2, "exchange_layer1_start")
    (dx1, df0), (dg_fpost0, dg_mpre1) = _rowwise_bwd(_fn_res_pre, [x1, f0], [g + tok[0:1, 0:1] for g in g_b],
                                                     [dx2, dh2], [F32, BF16], ROW_TB, "res_pre1_bwd")
    dh1, dwgu0, dwd0 = _ffn_bwd(df0, h1, w_gu_t0, w_down0, gu0, a0, 0)
    xch_a, tok = _exchange_begin([dwgu0[None], dwd0[None]], [1, 1], dh1, "exchange_ffn0_start")
    (dx0, dy0), (dg_mpost0, dg_fpre0) = _rowwise_bwd(_fn_res_pre, [x0, y0], [g + tok[0:1, 0:1] for g in g_a],
                                                     [dx1, dh1], [F32, BF16], ROW_TB, "res_pre0_bwd")
    dcat0 = _matmul(dy0, w_att_out, "nt", F32, "att_out_dx")
    dw_att_out = _matmul(cat0, dy0, "tn", BF16, "att_out_dw")
    dp0, drel, dkm0, dvm0 = _attn_mixer_bwd(dcat0, res0, km0, vm0)
    dwkv0, dg_mem0 = _mem_kv_bwd(mem0, gain(mem_norm, 0), w_kv[0], memn0, dkm0, dvm0, 0)
    xch_o, tok = _exchange_begin([dw_att_out[None], dwkv0[None]], [1, 1], dp0, "exchange_att_out_start")
    dw_att_in_t = _matmul(dp0, h0, "tn", BF16, "att_in_dw")
    xch_i, tok_i = _exchange_begin([dw_att_in_t[None]], [1], tok, "exchange_att_in_start")
    dh0 = _matmul(dp0, w_att_in_t, "nn", BF16, "att_in_dx")
    (grad_x,), (dg_mpre0,) = _rowwise_bwd(_fn_first, [x0], [gain(norm_mix_pre, 0) + tok_i[0:1, 0:1]], [dx0, dh0],
                                          [F32], ROW_TB, "pre0_bwd")

    small_grads = [drel, da_log, ddt_bias, dout_norm, jnp.concatenate([dg_mem0, dg_mem1]),
                   jnp.concatenate([dg_mpre0, dg_mpre1]), jnp.concatenate([dg_mpost0, dg_mpost1]),
                   jnp.concatenate([dg_fpre0, dg_fpre1]), jnp.concatenate([dg_fpost0, dg_fpost1]), dconv]
    (r_small,) = _exchange([_pack_small(small_grads, SMALL_ROWS)], [None], "exchange_last")
    (r_att_in,) = _split_end(xch_i, r_small, "exchange_att_in_wait")
    r_att_out, r_kv0 = _split_end(xch_o, r_small, "exchange_att_out_wait")
    r_gu0, r_down0 = _split_end(xch_a, r_small, "exchange_ffn0_wait")
    r_dn_in, r_dn_out, r_kv1, r_gu1, r_down1 = _split_end(xch_b, r_small, "exchange_layer1_wait")

    def rows(a):
        return a.reshape((-1,) + a.shape[-1:])

    def row_sharded(recv, w, m, v, tb, name):
        outs = _adamw_reduce(recv.reshape((N_DEV, -1) + recv.shape[-1:]), rows(w), rows(m), rows(v), tb, name)
        return [o.reshape(w.shape) for o in outs]

    def col_sharded(recv, w, m, v, tb, name):
        g_t = _reduce8(recv.reshape((N_DEV, -1) + recv.shape[-1:]), tb, name + "_sum")
        g = jnp.swapaxes(g_t.reshape(recv.shape[1:])[:, :w.shape[2]], 1, 2)
        outs = _adamw(rows(g), rows(w), rows(m), rows(v), 256, name)
        return [g] + [o.reshape(w.shape) for o in outs]

    def per_layer(fn, recvs, w, m, v, tb, name):
        outs = [fn(r, w[l:l + 1], m[l:l + 1], v[l:l + 1], tb, f"{name}{l}") for l, r in enumerate(recvs)]
        return [jnp.concatenate(pair, axis=0) for pair in zip(*outs)]

    big = [col_sharded(r_att_in, att_w_in, m_att_w_in, v_att_w_in, 320, "adamw_att_in"),
           row_sharded(r_att_out, att_w_out, m_att_w_out, v_att_w_out, 128, "adamw_att_out"),
           col_sharded(r_dn_in, dn_w_in, m_dn_w_in, v_dn_w_in, 432, "adamw_dn_in"),
           row_sharded(r_dn_out, dn_w_out, m_dn_w_out, v_dn_w_out, 128, "adamw_dn_out"),
           per_layer(row_sharded, [r_kv0, r_kv1], mem_w_kv, m_mem_w_kv, v_mem_w_kv, 128, "adamw_mem_kv"),
           per_layer(col_sharded, [r_gu0, r_gu1], ffn_w_gate_up, m_ffn_w_gate_up, v_ffn_w_gate_up, 176,
                     "adamw_ffn_gu"),
           per_layer(row_sharded, [r_down0, r_down1], ffn_w_down, m_ffn_w_down, v_ffn_w_down, 176,
                     "adamw_ffn_down")]
    g_big, d_big, nm_big, nv_big = [[b[i] for b in big] for i in range(4)]

    g_small = _reduce8(r_small, SMALL_ROWS, "reduce_small")
    rep_shapes = [(32, 12), (1, 2, 6), (1, 2, 6), (1, 128), (2, D), (2, D), (2, D), (2, D), (2, D)]
    *g_rep, g_conv_full = _unpack_small(g_small, rep_shapes + [(DN_CONV, 3 * TOK_W)])
    me = _index(_me_xyc())
    g_conv = lax.dynamic_slice(g_conv_full, (0, me * 288), (DN_CONV, 288)).reshape(CONV_SHARD)
    small_shapes = rep_shapes + [CONV_SHARD]
    small_w = [rel_bias, dn_a_log, dn_dt_bias, dn_out_norm, mem_norm, norm_mix_pre, norm_mix_post,
               norm_ffn_pre, norm_ffn_post, dn_conv]
    small_m = [m_rel_bias, m_dn_a_log, m_dn_dt_bias, m_dn_out_norm, m_mem_norm, m_norm_mix_pre, m_norm_mix_post,
               m_norm_ffn_pre, m_norm_ffn_post, m_dn_conv]
    small_v = [v_rel_bias, v_dn_a_log, v_dn_dt_bias, v_dn_out_norm, v_mem_norm, v_norm_mix_pre, v_norm_mix_post,
               v_norm_ffn_pre, v_norm_ffn_post, v_dn_conv]
    g_small_list = g_rep + [g_conv]
    outs_small = _adamw(_pack_small(g_small_list, 24), _pack_small(small_w, 24), _pack_small(small_m, 24),
                        _pack_small(small_v, 24), 24, "adamw_small")
    d_small, nm_small, nv_small = [_unpack_small(o, small_shapes) for o in outs_small]

    def ordered(small, big):
        return [small[0], big[0], big[1], big[2], small[9], small[1], small[2], small[3], big[3], small[4],
                big[4], small[5], small[6], small[7], small[8], big[5], big[6]]

    g_small_out = [g.reshape(s) for g, s in zip(g_small_list, small_shapes)]
    return (loss, grad_x[None], *ordered(g_small_out, g_big), *ordered(d_small, d_big),
            *ordered(nm_small, nm_big), *ordered(nv_small, nv_big))
```
